```python
import jax, jax.numpy as jnp
from jax import lax
import numpy as np

D_MODEL = 2048
BATCH = 8
SEQ = 4096
DEPTH = 1

PLE_DIM = 256
ATTN_WIDTH = D_MODEL // 2
CONV_WIDTH = D_MODEL - ATTN_WIDTH
HEAD_DIM = 64
N_Q_HEADS = ATTN_WIDTH // HEAD_DIM
N_KV_HEADS = 4
GQA_GROUP = N_Q_HEADS // N_KV_HEADS
KV_WIDTH = N_KV_HEADS * HEAD_DIM
WINDOW = 128
BLOCK = 128
ROT_DIM = HEAD_DIM // 4
ROPE_THETA = 500000.0
CONV_K = 3
CONV_GROUPS = 16
EPS = 1e-6
NEG_INF = -1e30

SEG_WIDTHS = (ATTN_WIDTH, KV_WIDTH, KV_WIDTH, ATTN_WIDTH,
              CONV_WIDTH, CONV_WIDTH, CONV_WIDTH, CONV_WIDTH)
IN_WIDTH = sum(SEG_WIDTHS)
SPLITS = [int(v) for v in np.cumsum(SEG_WIDTHS)[:-1]]

kernel_name = "hymba_swa_sink_shortconv_ple"


def rms_norm(x, gain):
    xf = x.astype(jnp.float32)
    y = xf * lax.rsqrt(jnp.mean(xf * xf, axis=-1, keepdims=True) + EPS)
    return (y * gain.astype(jnp.float32)).astype(x.dtype)


def partial_rope(x, pos):
    half = ROT_DIM // 2
    inv_freq = jnp.power(jnp.float32(ROPE_THETA), -jnp.arange(half, dtype=jnp.float32) * 2.0 / ROT_DIM)
    ang = pos.astype(jnp.float32)[:, None] * inv_freq[None, :]
    cos = jnp.cos(ang)[None, :, None, :]
    sin = jnp.sin(ang)[None, :, None, :]
    xr = x[..., :ROT_DIM].astype(jnp.float32)
    x1, x2 = xr[..., :half], xr[..., half:]
    rot = jnp.concatenate([x1 * cos - x2 * sin, x2 * cos + x1 * sin], axis=-1).astype(x.dtype)
    return jnp.concatenate([rot, x[..., ROT_DIM:]], axis=-1)


def sliding_window_attention(q, k, v, sinks):
    b, s = q.shape[0], q.shape[1]
    nb = s // BLOCK
    qb = q.reshape(b, nb, BLOCK, N_KV_HEADS, GQA_GROUP, HEAD_DIM)

    def band(t):
        tb = t.reshape(b, nb, BLOCK, N_KV_HEADS, HEAD_DIM)
        prev = jnp.pad(tb, ((0, 0), (1, 0), (0, 0), (0, 0), (0, 0)))[:, :-1]
        return jnp.concatenate([prev, tb], axis=2)

    kb, vb = band(k), band(v)
    scores = jnp.einsum('bnqhgd,bnkhd->bnhgqk', qb, kb,
                        preferred_element_type=jnp.float32) * (HEAD_DIM ** -0.5)
    q_pos = jnp.arange(nb)[:, None] * BLOCK + jnp.arange(BLOCK)[None, :]
    k_pos = jnp.arange(nb)[:, None] * BLOCK - BLOCK + jnp.arange(2 * BLOCK)[None, :]
    diff = q_pos[:, :, None] - k_pos[:, None, :]
    mask = (diff >= 0) & (diff < WINDOW) & (k_pos[:, None, :] >= 0)
    scores = jnp.where(mask[None, :, None, None], scores, NEG_INF)
    sink = sinks.astype(jnp.float32).reshape(N_KV_HEADS, GQA_GROUP)[None, None, :, :, None, None]
    m = jnp.maximum(jnp.max(scores, axis=-1, keepdims=True), sink)
    e = jnp.exp(scores - m)
    probs = e / (jnp.sum(e, axis=-1, keepdims=True) + jnp.exp(sink - m))
    out = jnp.einsum('bnhgqk,bnkhd->bnqhgd', probs.astype(v.dtype), vb)
    return out.reshape(b, s, N_Q_HEADS * HEAD_DIM)


def short_conv(u, w):
    return lax.conv_general_dilated(
        u, w[:, None, :].astype(u.dtype), window_strides=(1,),
        padding=[(CONV_K - 1, 0)], dimension_numbers=('NWC', 'WIO', 'NWC'),
        feature_group_count=u.shape[-1])


def _fwd_setup_inputs(seed: int = 0) -> dict:
    key = jax.random.key(seed)
    ks = jax.random.split(key, 16)
    f32 = jnp.float32
    nrm = lambda k, shape, scale: jax.random.normal(k, shape, f32) * scale
    return {
        "x": nrm(ks[0], (BATCH, SEQ, D_MODEL), 1.0),
        "p": nrm(ks[1], (DEPTH, BATCH, SEQ, PLE_DIM), 1.0),
        "norm_gain": 1.0 + nrm(ks[2], (DEPTH, D_MODEL), 0.02),
        "w_in": nrm(ks[3], (DEPTH, D_MODEL, IN_WIDTH), D_MODEL ** -0.5),
        "q_norm_gain": 1.0 + nrm(ks[4], (DEPTH, HEAD_DIM), 0.02),
        "k_norm_gain": 1.0 + nrm(ks[5], (DEPTH, HEAD_DIM), 0.02),
        "attn_sinks": nrm(ks[6], (DEPTH, N_Q_HEADS), 0.5),
        "conv_w": nrm(ks[7], (DEPTH, CONV_K, CONV_WIDTH), CONV_K ** -0.5),
        "w_out": nrm(ks[8], (DEPTH, D_MODEL, D_MODEL), D_MODEL ** -0.5),
        "ple_gate_norm_gain": 1.0 + nrm(ks[9], (DEPTH, D_MODEL), 0.02),
        "w_ple_gate": nrm(ks[10], (DEPTH, D_MODEL, D_MODEL), D_MODEL ** -0.5),
        "b_ple_gate": nrm(ks[11], (DEPTH, D_MODEL), 0.01),
        "w_ple_proj": nrm(ks[12], (DEPTH, PLE_DIM, D_MODEL), PLE_DIM ** -0.5),
        "ple_norm_gain": 1.0 + nrm(ks[13], (DEPTH, D_MODEL), 0.02),
    }


def _fwd_reference(x, p, norm_gain, w_in, q_norm_gain, k_norm_gain, attn_sinks, conv_w, w_out,
              ple_gate_norm_gain, w_ple_gate, b_ple_gate, w_ple_proj, ple_norm_gain):
    b, s = x.shape[0], x.shape[1]
    pos = jnp.arange(s)
    for i in range(DEPTH):
        h = rms_norm(x, norm_gain[i])
        z = h @ w_in[i]
        q, k, v, g_attn, c_b, c_c, c_h, g_conv = jnp.split(z, SPLITS, axis=-1)

        q = q.reshape(b, s, N_Q_HEADS, HEAD_DIM)
        k = k.reshape(b, s, N_KV_HEADS, HEAD_DIM)
        v = v.reshape(b, s, N_KV_HEADS, HEAD_DIM)
        q = partial_rope(rms_norm(q, q_norm_gain[i]), pos)
        k = partial_rope(rms_norm(k, k_norm_gain[i]), pos)
        y_attn = sliding_window_attention(q, k, v, attn_sinks[i]) * jax.nn.silu(g_attn)

        y_conv = c_b * short_conv(c_c * c_h, conv_w[i]) * jax.nn.silu(g_conv)

        mix = jnp.concatenate([y_attn, y_conv], axis=-1)
        x = x + mix @ w_out[i]

        gate = jax.nn.sigmoid(rms_norm(x, ple_gate_norm_gain[i]) @ w_ple_gate[i] + b_ple_gate[i])
        e = rms_norm(p[i] @ w_ple_proj[i], ple_norm_gain[i])
        x = x + gate * e
    return x


import jax as _jax
import jax.numpy as _jnp

TWIN_FORMAT = 'train_step'
FWD_PARAMS = ['x', 'p', 'norm_gain', 'w_in', 'q_norm_gain', 'k_norm_gain', 'attn_sinks', 'conv_w', 'w_out', 'ple_gate_norm_gain', 'w_ple_gate', 'b_ple_gate', 'w_ple_proj', 'ple_norm_gain']
TWIN_WEIGHTS = ['norm_gain', 'w_in', 'q_norm_gain', 'k_norm_gain', 'attn_sinks', 'conv_w', 'w_out', 'ple_gate_norm_gain', 'w_ple_gate', 'b_ple_gate', 'w_ple_proj', 'ple_norm_gain']
TWIN_DIFF_INPUT = 'x'
TWIN_INPUTS = ['x', 'p', 'norm_gain', 'w_in', 'q_norm_gain', 'k_norm_gain', 'attn_sinks', 'conv_w', 'w_out', 'ple_gate_norm_gain', 'w_ple_gate', 'b_ple_gate', 'w_ple_proj', 'ple_norm_gain', 'loss_target', 'm_norm_gain', 'm_w_in', 'm_q_norm_gain', 'm_k_norm_gain', 'm_attn_sinks', 'm_conv_w', 'm_w_out', 'm_ple_gate_norm_gain', 'm_w_ple_gate', 'm_b_ple_gate', 'm_w_ple_proj', 'm_ple_norm_gain', 'v_norm_gain', 'v_w_in', 'v_q_norm_gain', 'v_k_norm_gain', 'v_attn_sinks', 'v_conv_w', 'v_w_out', 'v_ple_gate_norm_gain', 'v_w_ple_gate', 'v_b_ple_gate', 'v_w_ple_proj', 'v_ple_norm_gain']
TWIN_OUTPUTS = ['loss', 'grad_x', 'grad_norm_gain', 'grad_w_in', 'grad_q_norm_gain', 'grad_k_norm_gain', 'grad_attn_sinks', 'grad_conv_w', 'grad_w_out', 'grad_ple_gate_norm_gain', 'grad_w_ple_gate', 'grad_b_ple_gate', 'grad_w_ple_proj', 'grad_ple_norm_gain', 'delta_norm_gain', 'delta_w_in', 'delta_q_norm_gain', 'delta_k_norm_gain', 'delta_attn_sinks', 'delta_conv_w', 'delta_w_out', 'delta_ple_gate_norm_gain', 'delta_w_ple_gate', 'delta_b_ple_gate', 'delta_w_ple_proj', 'delta_ple_norm_gain', 'new_m_norm_gain', 'new_m_w_in', 'new_m_q_norm_gain', 'new_m_k_norm_gain', 'new_m_attn_sinks', 'new_m_conv_w', 'new_m_w_out', 'new_m_ple_gate_norm_gain', 'new_m_w_ple_gate', 'new_m_b_ple_gate', 'new_m_w_ple_proj', 'new_m_ple_norm_gain', 'new_v_norm_gain', 'new_v_w_in', 'new_v_q_norm_gain', 'new_v_k_norm_gain', 'new_v_attn_sinks', 'new_v_conv_w', 'new_v_w_out', 'new_v_ple_gate_norm_gain', 'new_v_w_ple_gate', 'new_v_b_ple_gate', 'new_v_w_ple_proj', 'new_v_ple_norm_gain']
TWIN_LEAF_KINDS = {'loss': 'loss', 'grad_x': 'grad_x', 'grad_norm_gain': 'grad_w', 'grad_w_in': 'grad_w', 'grad_q_norm_gain': 'grad_w', 'grad_k_norm_gain': 'grad_w', 'grad_attn_sinks': 'grad_w', 'grad_conv_w': 'grad_w', 'grad_w_out': 'grad_w', 'grad_ple_gate_norm_gain': 'grad_w', 'grad_w_ple_gate': 'grad_w', 'grad_b_ple_gate': 'grad_w', 'grad_w_ple_proj': 'grad_w', 'grad_ple_norm_gain': 'grad_w', 'delta_norm_gain': 'delta_w', 'delta_w_in': 'delta_w', 'delta_q_norm_gain': 'delta_w', 'delta_k_norm_gain': 'delta_w', 'delta_attn_sinks': 'delta_w', 'delta_conv_w': 'delta_w', 'delta_w_out': 'delta_w', 'delta_ple_gate_norm_gain': 'delta_w', 'delta_w_ple_gate': 'delta_w', 'delta_b_ple_gate': 'delta_w', 'delta_w_ple_proj': 'delta_w', 'delta_ple_norm_gain': 'delta_w', 'new_m_norm_gain': 'new_m', 'new_m_w_in': 'new_m', 'new_m_q_norm_gain': 'new_m', 'new_m_k_norm_gain': 'new_m', 'new_m_attn_sinks': 'new_m', 'new_m_conv_w': 'new_m', 'new_m_w_out': 'new_m', 'new_m_ple_gate_norm_gain': 'new_m', 'new_m_w_ple_gate': 'new_m', 'new_m_b_ple_gate': 'new_m', 'new_m_w_ple_proj': 'new_m', 'new_m_ple_norm_gain': 'new_m', 'new_v_norm_gain': 'new_v', 'new_v_w_in': 'new_v', 'new_v_q_norm_gain': 'new_v', 'new_v_k_norm_gain': 'new_v', 'new_v_attn_sinks': 'new_v', 'new_v_conv_w': 'new_v', 'new_v_w_out': 'new_v', 'new_v_ple_gate_norm_gain': 'new_v', 'new_v_w_ple_gate': 'new_v', 'new_v_b_ple_gate': 'new_v', 'new_v_w_ple_proj': 'new_v', 'new_v_ple_norm_gain': 'new_v'}


def _forward(args):
    return _fwd_reference(*[args[k] for k in FWD_PARAMS])


def _output_shape():
    out = _jax.eval_shape(lambda: _forward(_fwd_setup_inputs(0)))
    return out.shape, out.dtype

N_MICROBATCH = 1
ADAM_LR = 0.001
ADAM_B1 = 0.9
ADAM_B2 = 0.999
ADAM_EPS = 1e-08
ADAM_WD = 0.01
ADAM_STEP = 10
PER_EXAMPLE_BATCH_AXIS = {'x': 0, 'p': 1, 'loss_target': 0}
SHARED_INPUTS = []
_WEIGHT_DTYPES = {'norm_gain': _jnp.float32, 'w_in': _jnp.float32, 'q_norm_gain': _jnp.float32, 'k_norm_gain': _jnp.float32, 'attn_sinks': _jnp.float32, 'conv_w': _jnp.float32, 'w_out': _jnp.float32, 'ple_gate_norm_gain': _jnp.float32, 'w_ple_gate': _jnp.float32, 'b_ple_gate': _jnp.float32, 'w_ple_proj': _jnp.float32, 'ple_norm_gain': _jnp.float32}
MOMENT_SCALE = {'norm_gain': 1.116165e+01, 'w_in': 2.723996e-01, 'q_norm_gain': 1.009766e+00, 'k_norm_gain': 1.000861e+00, 'attn_sinks': 1.378021e-01, 'conv_w': 2.821063e+00, 'w_out': 8.664800e-02, 'ple_gate_norm_gain': 4.716764e-01, 'w_ple_gate': 3.301166e-02, 'b_ple_gate': 1.658786e+00, 'w_ple_proj': 7.347662e-02, 'ple_norm_gain': 4.697232e+00}


def _to_microbatches(a, axis):
    t = _jnp.moveaxis(a, axis, 0)
    t = t.reshape((N_MICROBATCH, t.shape[0] // N_MICROBATCH) + t.shape[1:])
    return _jnp.moveaxis(t, 1, axis + 1)


def setup_inputs(seed: int = 0) -> dict:
    inp = _fwd_setup_inputs(seed)
    key = _jax.random.fold_in(_jax.random.key(seed), 7919)
    shape, _ = _output_shape()
    out = dict(inp)
    out["loss_target"] = _jax.random.normal(_jax.random.fold_in(key, 0), shape, _jnp.float32)
    for i, name in enumerate(TWIN_WEIGHTS):
        w = inp[name].astype(_jnp.float32)
        if MOMENT_SCALE is None:
            s = _jnp.sqrt(_jnp.mean(_jnp.square(w)) + 1e-30)
        else:
            s = MOMENT_SCALE[name]
        km, kv = _jax.random.split(_jax.random.fold_in(key, i + 1))
        out[name] = w
        out["m_" + name] = s * _jax.random.normal(km, w.shape, _jnp.float32)
        out["v_" + name] = (s * s) * _jax.random.uniform(kv, w.shape, _jnp.float32, 0.5, 1.5)
    if N_MICROBATCH > 1:
        for name, axis in PER_EXAMPLE_BATCH_AXIS.items():
            out[name] = _to_microbatches(out[name], axis)
    return {'x': out['x'], 'p': out['p'], 'norm_gain': out['norm_gain'], 'w_in': out['w_in'], 'q_norm_gain': out['q_norm_gain'], 'k_norm_gain': out['k_norm_gain'], 'attn_sinks': out['attn_sinks'], 'conv_w': out['conv_w'], 'w_out': out['w_out'], 'ple_gate_norm_gain': out['ple_gate_norm_gain'], 'w_ple_gate': out['w_ple_gate'], 'b_ple_gate': out['b_ple_gate'], 'w_ple_proj': out['w_ple_proj'], 'ple_norm_gain': out['ple_norm_gain'], 'loss_target': out['loss_target'], 'm_norm_gain': out['m_norm_gain'], 'm_w_in': out['m_w_in'], 'm_q_norm_gain': out['m_q_norm_gain'], 'm_k_norm_gain': out['m_k_norm_gain'], 'm_attn_sinks': out['m_attn_sinks'], 'm_conv_w': out['m_conv_w'], 'm_w_out': out['m_w_out'], 'm_ple_gate_norm_gain': out['m_ple_gate_norm_gain'], 'm_w_ple_gate': out['m_w_ple_gate'], 'm_b_ple_gate': out['m_b_ple_gate'], 'm_w_ple_proj': out['m_w_ple_proj'], 'm_ple_norm_gain': out['m_ple_norm_gain'], 'v_norm_gain': out['v_norm_gain'], 'v_w_in': out['v_w_in'], 'v_q_norm_gain': out['v_q_norm_gain'], 'v_k_norm_gain': out['v_k_norm_gain'], 'v_attn_sinks': out['v_attn_sinks'], 'v_conv_w': out['v_conv_w'], 'v_w_out': out['v_w_out'], 'v_ple_gate_norm_gain': out['v_ple_gate_norm_gain'], 'v_w_ple_gate': out['v_w_ple_gate'], 'v_b_ple_gate': out['v_b_ple_gate'], 'v_w_ple_proj': out['v_w_ple_proj'], 'v_ple_norm_gain': out['v_ple_norm_gain']}


def _loss(weights, diff, rest, loss_target):
    with _jax.named_scope("forward"):
        args = {**rest, TWIN_DIFF_INPUT: diff, **{k: w.astype(_WEIGHT_DTYPES[k]) for k, w in weights.items()}}
        y = _forward(args)
    with _jax.named_scope("loss_head"):
        err = _jnp.square(y.astype(_jnp.float32) - loss_target)
        return 0.5 * _jnp.sum(_jnp.mean(err, axis=-1)) if err.ndim else 0.5 * err


def _adamw(w, g, m, v):
    m = ADAM_B1 * m + (1.0 - ADAM_B1) * g
    v = ADAM_B2 * v + (1.0 - ADAM_B2) * _jnp.square(g)
    m_hat = m / (1.0 - ADAM_B1 ** ADAM_STEP)
    v_hat = v / (1.0 - ADAM_B2 ** ADAM_STEP)
    delta = -ADAM_LR * (m_hat / (_jnp.sqrt(v_hat) + ADAM_EPS) + ADAM_WD * w)
    return delta, m, v


def reference(x, p, norm_gain, w_in, q_norm_gain, k_norm_gain, attn_sinks, conv_w, w_out, ple_gate_norm_gain, w_ple_gate, b_ple_gate, w_ple_proj, ple_norm_gain, loss_target, m_norm_gain, m_w_in, m_q_norm_gain, m_k_norm_gain, m_attn_sinks, m_conv_w, m_w_out, m_ple_gate_norm_gain, m_w_ple_gate, m_b_ple_gate, m_w_ple_proj, m_ple_norm_gain, v_norm_gain, v_w_in, v_q_norm_gain, v_k_norm_gain, v_attn_sinks, v_conv_w, v_w_out, v_ple_gate_norm_gain, v_w_ple_gate, v_b_ple_gate, v_w_ple_proj, v_ple_norm_gain):
    given = dict(x=x, p=p, norm_gain=norm_gain, w_in=w_in, q_norm_gain=q_norm_gain, k_norm_gain=k_norm_gain, attn_sinks=attn_sinks, conv_w=conv_w, w_out=w_out, ple_gate_norm_gain=ple_gate_norm_gain, w_ple_gate=w_ple_gate, b_ple_gate=b_ple_gate, w_ple_proj=w_ple_proj, ple_norm_gain=ple_norm_gain, loss_target=loss_target, m_norm_gain=m_norm_gain, m_w_in=m_w_in, m_q_norm_gain=m_q_norm_gain, m_k_norm_gain=m_k_norm_gain, m_attn_sinks=m_attn_sinks, m_conv_w=m_conv_w, m_w_out=m_w_out, m_ple_gate_norm_gain=m_ple_gate_norm_gain, m_w_ple_gate=m_w_ple_gate, m_b_ple_gate=m_b_ple_gate, m_w_ple_proj=m_w_ple_proj, m_ple_norm_gain=m_ple_norm_gain, v_norm_gain=v_norm_gain, v_w_in=v_w_in, v_q_norm_gain=v_q_norm_gain, v_k_norm_gain=v_k_norm_gain, v_attn_sinks=v_attn_sinks, v_conv_w=v_conv_w, v_w_out=v_w_out, v_ple_gate_norm_gain=v_ple_gate_norm_gain, v_w_ple_gate=v_w_ple_gate, v_b_ple_gate=v_b_ple_gate, v_w_ple_proj=v_w_ple_proj, v_ple_norm_gain=v_ple_norm_gain)
    weights = {n: given[n] for n in TWIN_WEIGHTS}
    shared = {n: given[n] for n in SHARED_INPUTS}
    per_example = {n: given[n] for n in ['x', 'p']}
    grad_fn = _jax.value_and_grad(_loss, argnums=(0, 1))

    def one_microbatch(ex, loss_target):
        ex = dict(ex)
        diff = ex.pop(TWIN_DIFF_INPUT)
        return grad_fn(weights, diff, {**shared, **ex}, loss_target)

    if N_MICROBATCH == 1:
        loss, (grad_w, grad_x) = one_microbatch(per_example, given["loss_target"])
    else:
        def body(carry, xs):
            loss_sum, grad_sum = carry
            l_k, (gw_k, gx_k) = one_microbatch(xs[0], xs[1])
            with _jax.named_scope("update"):
                return (loss_sum + l_k, _jax.tree.map(_jnp.add, grad_sum, gw_k)), gx_k

        init = (_jnp.zeros((), _jnp.float32), _jax.tree.map(_jnp.zeros_like, weights))
        (loss, grad_w), grad_x = _jax.lax.scan(body, init, (per_example, given["loss_target"]))
    with _jax.named_scope("update"):
        delta_w, new_m, new_v = {}, {}, {}
        for n in TWIN_WEIGHTS:
            delta_w[n], new_m[n], new_v[n] = _adamw(weights[n], grad_w[n], given["m_" + n], given["v_" + n])
    return (loss, grad_x, *[grad_w[n] for n in TWIN_WEIGHTS], *[delta_w[n] for n in TWIN_WEIGHTS],
            *[new_m[n] for n in TWIN_WEIGHTS], *[new_v[n] for n in TWIN_WEIGHTS])
```

```python
import functools

import jax
import jax.numpy as jnp
from jax import lax
from jax.experimental import pallas as pl
from jax.experimental.pallas import tpu as pltpu

F32, BF16 = jnp.float32, jnp.bfloat16

D_MODEL = 2048
PLE_DIM = 256
ATTN_W = 1024
HEAD = 64
N_Q_HEADS = 16
KV_W = 256
QKV_W = ATTN_W + 2 * KV_W
REST_W = 5 * 1024
IN_W = QKV_W + REST_W
K2_W = 4 * 128
ROT = 16
ROPE_THETA = 500000.0
EPS = 1e-6
NEG_INF = -1e30
BLK = 128
LANES = 128
SUBLANES = 8
N_DEV = 8
SHARD_IN = IN_W // N_DEV
SLAB_ROWS = 16
V7X_VMEM_LIMIT = 52 * 1024 * 1024

ADAM_LR, ADAM_B1, ADAM_B2, ADAM_EPS, ADAM_WD, ADAM_STEP = 0.001, 0.9, 0.999, 1e-08, 0.01, 10
MESH = pl.DeviceIdType.MESH


def _params(*semantics):
    return pltpu.CompilerParams(dimension_semantics=semantics, vmem_limit_bytes=V7X_VMEM_LIMIT)


def _dot(a, b):
    return jnp.dot(a, b, preferred_element_type=F32)


def _dot_nt(a, b):
    return lax.dot_general(a, b, (((1,), (1,)), ((), ())), preferred_element_type=F32)


def _dot_tn(a, b):
    return lax.dot_general(a, b, (((0,), (0,)), ((), ())), preferred_element_type=F32)


def _rms(xf):
    r = lax.rsqrt(jnp.mean(xf * xf, axis=-1, keepdims=True) + EPS)
    return xf * r, r


def _rms_bwd(dxn, xn, r):
    return r * (dxn - xn * jnp.mean(dxn * xn, axis=-1, keepdims=True))


def _sig(g):
    return jax.nn.sigmoid(g)


def _dsilu(g, sg):
    return sg * (1.0 + g * (1.0 - sg))


def _low_half(shape):
    return lax.broadcasted_iota(jnp.int32, shape, len(shape) - 1) < HEAD


def _half_sums(v):
    lo = _low_half(v.shape)
    s_lo = jnp.sum(jnp.where(lo, v, 0.0), axis=-1, keepdims=True)
    s_hi = jnp.sum(jnp.where(lo, 0.0, v), axis=-1, keepdims=True)
    return jnp.where(lo, s_lo, s_hi)


def _rope(v, a, bm, bp):
    return v * a + pltpu.roll(v, LANES - ROT // 2, 1) * bm + pltpu.roll(v, ROT // 2, 1) * bp


def _rope_t(dy, a, bm, bp):
    return dy * a + pltpu.roll(dy * bm, ROT // 2, 1) + pltpu.roll(dy * bp, LANES - ROT // 2, 1)


def _dup_halves(v):
    lo = _low_half(v.shape)
    a = jnp.where(lo, v, 0.0)
    b = jnp.where(lo, 0.0, v)
    return a + pltpu.roll(a, HEAD, 1), b + pltpu.roll(b, HEAD, 1)


def _rope_tables(s):
    half = ROT // 2
    inv_freq = jnp.power(jnp.float32(ROPE_THETA), -jnp.arange(half, dtype=F32) * 2.0 / ROT)
    ang = jnp.arange(s).astype(F32)[:, None] * inv_freq[None, :]
    cos, sin = jnp.cos(ang), jnp.sin(ang)
    one = jnp.ones((s, HEAD - ROT), F32)
    zero = jnp.zeros((s, HEAD - ROT), F32)
    z8 = jnp.zeros((s, half), F32)
    a = jnp.concatenate([cos, cos, one], axis=1)
    bm = jnp.concatenate([-sin, z8, zero], axis=1)
    bp = jnp.concatenate([z8, sin, zero], axis=1)
    return tuple(jnp.concatenate([t, t], axis=1) for t in (a, bm, bp))


def _fwd_in_a(x, g1, w_qkv, tm):
    s = x.shape[0]

    def body(x_ref, g_ref, w_ref, h_ref, z_ref):
        xn, _ = _rms(x_ref[...])
        h = (xn * g_ref[...]).astype(BF16)
        h_ref[...] = h
        z_ref[...] = _dot(h, w_ref[...])

    return pl.pallas_call(
        body, name="fwd_in_a",
        out_shape=(jax.ShapeDtypeStruct((s, D_MODEL), BF16), jax.ShapeDtypeStruct((s, QKV_W), F32)),
        grid=(s // tm,),
        in_specs=[pl.BlockSpec((tm, D_MODEL), lambda i: (i, 0)),
                  pl.BlockSpec((1, D_MODEL), lambda i: (0, 0)),
                  pl.BlockSpec((D_MODEL, QKV_W), lambda i: (0, 0))],
        out_specs=(pl.BlockSpec((tm, D_MODEL), lambda i: (i, 0)),
                   pl.BlockSpec((tm, QKV_W), lambda i: (i, 0))),
        compiler_params=_params("parallel"))(x, g1, w_qkv)


def _mm_nn(a, b, tm, tn, name):
    m, k = a.shape
    n = b.shape[1]

    def body(a_ref, b_ref, o_ref):
        o_ref[...] = _dot(a_ref[...], b_ref[...])

    return pl.pallas_call(
        body, name=name,
        out_shape=jax.ShapeDtypeStruct((m, n), F32),
        grid=(n // tn, m // tm),
        in_specs=[pl.BlockSpec((tm, k), lambda j, i: (i, 0)),
                  pl.BlockSpec((k, tn), lambda j, i: (0, j))],
        out_specs=pl.BlockSpec((tm, tn), lambda j, i: (i, j)),
        compiler_params=_params("parallel", "parallel"))(a, b)


def _qk_prep(z_a, ra, rbm, rbp, gq2, gk2, tm):
    s = z_a.shape[0]

    def body(z_ref, a_ref, bm_ref, bp_ref, gq_ref, gk_ref, q_ref, k2_ref, v2_ref):
        a, bm, bp = a_ref[...], bm_ref[...], bp_ref[...]
        for r in range(ATTN_W // LANES):
            x = z_ref[:, LANES * r:LANES * (r + 1)]
            rr = lax.rsqrt(_half_sums(x * x) * (1.0 / HEAD) + EPS)
            q_ref[:, LANES * r:LANES * (r + 1)] = _rope(x * rr * gq_ref[...], a, bm, bp).astype(BF16)
        for m in range(KV_W // LANES):
            x = z_ref[:, ATTN_W + LANES * m:ATTN_W + LANES * (m + 1)]
            rr = lax.rsqrt(_half_sums(x * x) * (1.0 / HEAD) + EPS)
            k_lo, k_hi = _dup_halves(_rope(x * rr * gk_ref[...], a, bm, bp))
            k2_ref[:, 2 * LANES * m:2 * LANES * m + LANES] = k_lo.astype(BF16)
            k2_ref[:, 2 * LANES * m + LANES:2 * LANES * (m + 1)] = k_hi.astype(BF16)
            v_lo, v_hi = _dup_halves(z_ref[:, ATTN_W + KV_W + LANES * m:ATTN_W + KV_W + LANES * (m + 1)])
            v2_ref[:, 2 * LANES * m:2 * LANES * m + LANES] = v_lo.astype(BF16)
            v2_ref[:, 2 * LANES * m + LANES:2 * LANES * (m + 1)] = v_hi.astype(BF16)

    row = lambda w: pl.BlockSpec((tm, w), lambda i: (i, 0))
    one = pl.BlockSpec((1, LANES), lambda i: (0, 0))
    return pl.pallas_call(
        body, name="qk_prep",
        out_shape=(jax.ShapeDtypeStruct((s, ATTN_W), BF16), jax.ShapeDtypeStruct((s, K2_W), BF16),
                   jax.ShapeDtypeStruct((s, K2_W), BF16)),
        grid=(s // tm,),
        in_specs=[row(QKV_W), row(LANES), row(LANES), row(LANES), one, one],
        out_specs=(row(ATTN_W), row(K2_W), row(K2_W)),
        compiler_params=_params("parallel"))(z_a, ra, rbm, rbp, gq2, gk2)


def _window_mask(n):
    row = lax.broadcasted_iota(jnp.int32, (BLK, 2 * BLK), 0)
    col = lax.broadcasted_iota(jnp.int32, (BLK, 2 * BLK), 1)
    return (col > row) & (col <= row + BLK) & ((col >= BLK) | (n > 0))


def _head_probs(qm, kw, valid, sink):
    sc = jnp.where(valid, _dot_nt(qm, kw) * (HEAD ** -0.5), NEG_INF)
    mx = jnp.maximum(jnp.max(sc, axis=-1, keepdims=True), sink)
    ex = jnp.exp(sc - mx)
    den = jnp.sum(ex, axis=-1, keepdims=True) + jnp.exp(sink - mx)
    return ex / den, mx, den


def _conv_fwd(zb_ref, zbp_ref, cw_ref, ext_ref, n):
    u = zb_ref[:, 2048:3072] * zb_ref[:, 3072:4096]
    pu = zbp_ref[:, 2048:3072] * zbp_ref[:, 3072:4096]
    ext_ref[0:SUBLANES, :] = jnp.where(n > 0, pu, 0.0)
    ext_ref[SUBLANES:SUBLANES + BLK, :] = u
    um1 = ext_ref[SUBLANES - 1:SUBLANES - 1 + BLK, :]
    um2 = ext_ref[SUBLANES - 2:SUBLANES - 2 + BLK, :]
    cv = cw_ref[0:1, :] * um2 + cw_ref[1:2, :] * um1 + cw_ref[2:3, :] * u
    return u, um1, um2, cv


def _prev_rows(n):
    return (jnp.maximum(n * (BLK // SUBLANES) - 1, 0), 0)


def _attn_fwd(qn, k2, v2, z_b, conv_wp, sinks):
    s = qn.shape[0]
    nb = s // BLK

    def body(sink_ref, q_ref, kc_ref, kp_ref, vc_ref, vp_ref, zb_ref, zbp_ref, cw_ref, a_ref, mix_ref, ext_ref):
        n = pl.program_id(0)
        valid = _window_mask(n)
        lo = _low_half((BLK, LANES))
        for r in range(ATTN_W // LANES):
            kvh = r // 2
            cols = slice(LANES * kvh, LANES * (kvh + 1))
            qp = q_ref[:, LANES * r:LANES * (r + 1)]
            kw = jnp.concatenate([kp_ref[:, cols], kc_ref[:, cols]], axis=0)
            vw = jnp.concatenate([vp_ref[:, cols], vc_ref[:, cols]], axis=0)
            outs = []
            for e in range(2):
                qm = jnp.where(lo if e == 0 else jnp.logical_not(lo), qp, jnp.zeros_like(qp))
                p, _, _ = _head_probs(qm, kw, valid, sink_ref[0, 2 * r + e])
                outs.append(_dot(p.astype(BF16), vw))
            a = jnp.where(lo, outs[0], outs[1])
            a_ref[:, LANES * r:LANES * (r + 1)] = a
            g = zb_ref[:, LANES * r:LANES * (r + 1)]
            mix_ref[:, LANES * r:LANES * (r + 1)] = (a * (g * _sig(g))).astype(BF16)
        _, _, _, cv = _conv_fwd(zb_ref, zbp_ref, cw_ref, ext_ref, n)
        gc = zb_ref[:, 4096:5120]
        mix_ref[:, ATTN_W:D_MODEL] = (zb_ref[:, 1024:2048] * cv * (gc * _sig(gc))).astype(BF16)

    cur = lambda w: pl.BlockSpec((BLK, w), lambda n: (n, 0))
    prev = lambda w: pl.BlockSpec((BLK, w), lambda n: (jnp.maximum(n - 1, 0), 0))
    return pl.pallas_call(
        body, name="attn_fwd",
        out_shape=(jax.ShapeDtypeStruct((s, ATTN_W), F32), jax.ShapeDtypeStruct((s, D_MODEL), BF16)),
        grid=(nb,),
        in_specs=[pl.BlockSpec(memory_space=pltpu.SMEM),
                  cur(ATTN_W), cur(K2_W), prev(K2_W), cur(K2_W), prev(K2_W), cur(REST_W),
                  pl.BlockSpec((SUBLANES, REST_W), _prev_rows),
                  pl.BlockSpec((SUBLANES, ATTN_W), lambda n: (0, 0))],
        out_specs=(cur(ATTN_W), cur(D_MODEL)),
        scratch_shapes=[pltpu.VMEM((BLK + 2 * SUBLANES, ATTN_W), F32)],
        compiler_params=_params("parallel"))(sinks, qn, k2, k2, v2, v2, z_b, z_b, conv_wp)


def _fwd_out(mix, w_out, x, g2, tm):
    s = x.shape[0]

    def body(m_ref, w_ref, x_ref, g_ref, x1_ref, h_ref):
        x1 = x_ref[...] + _dot(m_ref[...], w_ref[...])
        x1_ref[...] = x1
        xn, _ = _rms(x1)
        h_ref[...] = (xn * g_ref[...]).astype(BF16)

    row = pl.BlockSpec((tm, D_MODEL), lambda i: (i, 0))
    return pl.pallas_call(
        body, name="fwd_out",
        out_shape=(jax.ShapeDtypeStruct((s, D_MODEL), F32), jax.ShapeDtypeStruct((s, D_MODEL), BF16)),
        grid=(s // tm,),
        in_specs=[row, pl.BlockSpec((D_MODEL, D_MODEL), lambda i: (0, 0)), row,
                  pl.BlockSpec((1, D_MODEL), lambda i: (0, 0))],
        out_specs=(row, row),
        compiler_params=_params("parallel"))(mix, w_out, x, g2)


def _ple(hn2, w_pg, b_pg, p, w_pp, g3, x1, target, tm):
    s = x1.shape[0]

    def body(h_ref, wg_ref, b_ref, p_ref, wp_ref, g3_ref, x1_ref, t_ref, dy_ref, dgp_ref, dt_ref, acc_ref):
        gate = _sig(_dot(h_ref[...], wg_ref[...]) + b_ref[...])
        t = _dot(p_ref[...].astype(BF16), wp_ref[...])
        tn, r3 = _rms(t)
        e = tn * g3_ref[...]
        diff = x1_ref[...] + gate * e - t_ref[...]
        dy = diff * (1.0 / D_MODEL)
        dy_ref[...] = dy
        dgp = dy * e * (gate * (1.0 - gate))
        dgp_ref[...] = dgp.astype(BF16)
        de = dy * gate
        dt_ref[...] = _rms_bwd(de * g3_ref[...], tn, r3).astype(BF16)

        @pl.when(pl.program_id(0) == 0)
        def _():
            acc_ref[...] = jnp.zeros_like(acc_ref)

        acc_ref[0:1, :] += jnp.sum(dgp, axis=0, keepdims=True)
        acc_ref[1:2, :] += jnp.sum(de * tn, axis=0, keepdims=True)
        acc_ref[2:3, :] += jnp.sum(diff * diff, axis=0, keepdims=True) * (0.5 / D_MODEL)

    row = pl.BlockSpec((tm, D_MODEL), lambda i: (i, 0))
    vec = pl.BlockSpec((1, D_MODEL), lambda i: (0, 0))
    return pl.pallas_call(
        body, name="ple",
        out_shape=(jax.ShapeDtypeStruct((s, D_MODEL), F32), jax.ShapeDtypeStruct((s, D_MODEL), BF16),
                   jax.ShapeDtypeStruct((s, D_MODEL), BF16), jax.ShapeDtypeStruct((SUBLANES, D_MODEL), F32)),
        grid=(s // tm,),
        in_specs=[row, pl.BlockSpec((D_MODEL, D_MODEL), lambda i: (0, 0)), vec,
                  pl.BlockSpec((tm, PLE_DIM), lambda i: (i, 0)),
                  pl.BlockSpec((PLE_DIM, D_MODEL), lambda i: (0, 0)), vec, row, row],
        out_specs=(row, row, row, pl.BlockSpec((SUBLANES, D_MODEL), lambda i: (0, 0))),
        compiler_params=_params("arbitrary"))(hn2, w_pg, b_pg, p, w_pp, g3, x1, target)


def _gate_bwd(dgp, w_pg, x1, dy, g2, tm):
    s = x1.shape[0]

    def body(d_ref, w_ref, x1_ref, dy_ref, g_ref, dx_ref, dxb_ref, acc_ref):
        dh = _dot_nt(d_ref[...], w_ref[...])
        xn, r = _rms(x1_ref[...])
        dx1 = dy_ref[...] + _rms_bwd(dh * g_ref[...], xn, r)
        dx_ref[...] = dx1
        dxb_ref[...] = dx1.astype(BF16)

        @pl.when(pl.program_id(0) == 0)
        def _():
            acc_ref[...] = jnp.zeros_like(acc_ref)

        acc_ref[0:1, :] += jnp.sum(dh * xn, axis=0, keepdims=True)

    row = pl.BlockSpec((tm, D_MODEL), lambda i: (i, 0))
    return pl.pallas_call(
        body, name="gate_bwd",
        out_shape=(jax.ShapeDtypeStruct((s, D_MODEL), F32), jax.ShapeDtypeStruct((s, D_MODEL), BF16),
                   jax.ShapeDtypeStruct((SUBLANES, D_MODEL), F32)),
        grid=(s // tm,),
        in_specs=[row, pl.BlockSpec((D_MODEL, D_MODEL), lambda i: (0, 0)), row, row,
                  pl.BlockSpec((1, D_MODEL), lambda i: (0, 0))],
        out_specs=(row, row, pl.BlockSpec((SUBLANES, D_MODEL), lambda i: (0, 0))),
        compiler_params=_params("arbitrary"))(dgp, w_pg, x1, dy, g2)


def _mm_nt(a, b, tm, name):
    m, k = a.shape
    n = b.shape[0]

    def body(a_ref, b_ref, o_ref):
        o_ref[...] = _dot_nt(a_ref[...], b_ref[...])

    return pl.pallas_call(
        body, name=name,
        out_shape=jax.ShapeDtypeStruct((m, n), F32),
        grid=(m // tm,),
        in_specs=[pl.BlockSpec((tm, k), lambda i: (i, 0)), pl.BlockSpec((n, k), lambda i: (0, 0))],
        out_specs=pl.BlockSpec((tm, n), lambda i: (i, 0)),
        compiler_params=_params("parallel"))(a, b)


def _attn_bwd(qn, k2, v2, a, z_b, dmix, conv_wp, sinks):
    s = qn.shape[0]
    nb = s // BLK

    def body(sink_ref, q_ref, kc_ref, kp_ref, vc_ref, vp_ref, a_ref, zb_ref, zbp_ref, zbn_ref, dm_ref, dmn_ref,
             cw_ref, dq_ref, dkc_ref, dkp_ref, dvc_ref, dvp_ref, dzb_ref, acc_ref, ext_ref):
        n = pl.program_id(0)
        valid = _window_mask(n)
        lo = _low_half((BLK, LANES))
        lane = lax.broadcasted_iota(jnp.int32, (1, ATTN_W), 1)

        @pl.when(n == 0)
        def _():
            acc_ref[...] = jnp.zeros_like(acc_ref)

        dsink = jnp.zeros((1, ATTN_W), F32)
        for kvh in range(K2_W // LANES):
            cols = slice(LANES * kvh, LANES * (kvh + 1))
            kw = jnp.concatenate([kp_ref[:, cols], kc_ref[:, cols]], axis=0)
            vw = jnp.concatenate([vp_ref[:, cols], vc_ref[:, cols]], axis=0)
            dk2 = jnp.zeros((2 * BLK, LANES), F32)
            dv2 = jnp.zeros((2 * BLK, LANES), F32)
            for r in (2 * kvh, 2 * kvh + 1):
                rc = slice(LANES * r, LANES * (r + 1))
                g = zb_ref[:, rc]
                sg = _sig(g)
                dm = dm_ref[:, rc]
                av = a_ref[:, rc]
                da = dm * (g * sg)
                dzb_ref[:, rc] = (dm * av * _dsilu(g, sg)).astype(BF16)
                qp = q_ref[:, rc]
                dqs = []
                for e in range(2):
                    half = lo if e == 0 else jnp.logical_not(lo)
                    sink = sink_ref[0, 2 * r + e]
                    qm = jnp.where(half, qp, jnp.zeros_like(qp))
                    p, mx, den = _head_probs(qm, kw, valid, sink)
                    do = jnp.where(half, da, 0.0)
                    delta = jnp.sum(do * av, axis=-1, keepdims=True)
                    dob = do.astype(BF16)
                    ds = p * (_dot_nt(dob, vw) - delta) * (HEAD ** -0.5)
                    dsb = ds.astype(BF16)
                    dqs.append(_dot(dsb, kw))
                    dk2 = dk2 + _dot(ds.T.astype(BF16), qm)
                    dv2 = dv2 + _dot(p.T.astype(BF16), dob)
                    dsk = -jnp.sum(jnp.exp(sink - mx) / den * delta, axis=0, keepdims=True)
                    dsink = dsink + jnp.where(lane == 2 * r + e, dsk, 0.0)
                dq_ref[:, rc] = jnp.where(lo, dqs[0], dqs[1])
            dkp_ref[:, cols] = dk2[0:BLK]
            dkc_ref[:, cols] = dk2[BLK:2 * BLK]
            dvp_ref[:, cols] = dv2[0:BLK]
            dvc_ref[:, cols] = dv2[BLK:2 * BLK]
        acc_ref[0:1, :] += dsink

        u, um1, um2, cv = _conv_fwd(zb_ref, zbp_ref, cw_ref, ext_ref, n)
        cb = zb_ref[:, 1024:2048]
        gc = zb_ref[:, 4096:5120]
        sgc = _sig(gc)
        dmc = dm_ref[:, ATTN_W:D_MODEL]
        t = dmc * (gc * sgc)
        dcv = t * cb
        dzb_ref[:, 1024:2048] = (t * cv).astype(BF16)
        dzb_ref[:, 4096:5120] = (dmc * cb * cv * _dsilu(gc, sgc)).astype(BF16)
        gcn = zbn_ref[:, 4096:5120]
        dcvn = dmn_ref[:, ATTN_W:D_MODEL] * (gcn * _sig(gcn)) * zbn_ref[:, 1024:2048]
        ext_ref[0:BLK, :] = dcv
        ext_ref[BLK:BLK + SUBLANES, :] = jnp.where(n < nb - 1, dcvn, 0.0)
        du = (cw_ref[2:3, :] * dcv + cw_ref[1:2, :] * ext_ref[1:1 + BLK, :]
              + cw_ref[0:1, :] * ext_ref[2:2 + BLK, :])
        dzb_ref[:, 2048:3072] = (du * zb_ref[:, 3072:4096]).astype(BF16)
        dzb_ref[:, 3072:4096] = (du * zb_ref[:, 2048:3072]).astype(BF16)
        acc_ref[1:2, :] += jnp.sum(dcv * um2, axis=0, keepdims=True)
        acc_ref[2:3, :] += jnp.sum(dcv * um1, axis=0, keepdims=True)
        acc_ref[3:4, :] += jnp.sum(dcv * u, axis=0, keepdims=True)

    cur = lambda w: pl.BlockSpec((BLK, w), lambda n: (n, 0))
    prev = lambda w: pl.BlockSpec((BLK, w), lambda n: (jnp.maximum(n - 1, 0), 0))
    nxt = lambda w: pl.BlockSpec(
        (SUBLANES, w), lambda n: (jnp.minimum((n + 1) * (BLK // SUBLANES), nb * (BLK // SUBLANES) - 1), 0))
    f32 = lambda w: jax.ShapeDtypeStruct((s, w), F32)
    return pl.pallas_call(
        body, name="attn_bwd",
        out_shape=(f32(ATTN_W), f32(K2_W), f32(K2_W), f32(K2_W), f32(K2_W),
                   jax.ShapeDtypeStruct((s, REST_W), BF16), jax.ShapeDtypeStruct((SUBLANES, ATTN_W), F32)),
        grid=(nb,),
        in_specs=[pl.BlockSpec(memory_space=pltpu.SMEM),
                  cur(ATTN_W), cur(K2_W), prev(K2_W), cur(K2_W), prev(K2_W), cur(ATTN_W), cur(REST_W),
                  pl.BlockSpec((SUBLANES, REST_W), _prev_rows), nxt(REST_W), cur(D_MODEL), nxt(D_MODEL),
                  pl.BlockSpec((SUBLANES, ATTN_W), lambda n: (0, 0))],
        out_specs=(cur(ATTN_W), cur(K2_W), cur(K2_W), cur(K2_W), cur(K2_W), cur(REST_W),
                   pl.BlockSpec((SUBLANES, ATTN_W), lambda n: (0, 0))),
        scratch_shapes=[pltpu.VMEM((BLK + 2 * SUBLANES, ATTN_W), F32)],
        compiler_params=_params("arbitrary"))(sinks, qn, k2, k2, v2, v2, a, z_b, z_b, z_b, dmix, dmix, conv_wp)


def _qkv_bwd(z_a, dq, dkc, dkp, dvc, dvp, ra, rbm, rbp, gq2, gk2):
    s = z_a.shape[0]
    nb = s // BLK

    def body(z_ref, dq_ref, dkc_ref, dkp_ref, dvc_ref, dvp_ref, a_ref, bm_ref, bp_ref, gq_ref, gk_ref,
             dz_ref, acc_ref):
        n = pl.program_id(0)
        a, bm, bp = a_ref[...], bm_ref[...], bp_ref[...]
        lo = _low_half((BLK, LANES))
        last = n == nb - 1

        @pl.when(n == 0)
        def _():
            acc_ref[...] = jnp.zeros_like(acc_ref)

        def norm_bwd(x, dy, gain):
            rr = lax.rsqrt(_half_sums(x * x) * (1.0 / HEAD) + EPS)
            xh = x * rr
            dxg = _rope_t(dy, a, bm, bp)
            dxh = dxg * gain
            dx = rr * (dxh - xh * (_half_sums(dxh * xh) * (1.0 / HEAD)))
            return dx, jnp.sum(dxg * xh, axis=0, keepdims=True)

        def folded(cur_ref, prev_ref, m):
            parts = []
            for h in (2 * m, 2 * m + 1):
                v = cur_ref[:, LANES * h:LANES * (h + 1)] + jnp.where(
                    last, 0.0, prev_ref[:, LANES * h:LANES * (h + 1)])
                parts.append(v + pltpu.roll(v, HEAD, 1))
            return jnp.where(lo, parts[0], parts[1])

        gq_acc = jnp.zeros((1, LANES), F32)
        for r in range(ATTN_W // LANES):
            rc = slice(LANES * r, LANES * (r + 1))
            dx, gg = norm_bwd(z_ref[:, rc], dq_ref[:, rc], gq_ref[...])
            dz_ref[:, rc] = dx.astype(BF16)
            gq_acc = gq_acc + gg
        acc_ref[0:1, :] += gq_acc
        gk_acc = jnp.zeros((1, LANES), F32)
        for m in range(KV_W // LANES):
            kc = slice(ATTN_W + LANES * m, ATTN_W + LANES * (m + 1))
            dx, gg = norm_bwd(z_ref[:, kc], folded(dkc_ref, dkp_ref, m), gk_ref[...])
            dz_ref[:, kc] = dx.astype(BF16)
            gk_acc = gk_acc + gg
            vc = slice(ATTN_W + KV_W + LANES * m, ATTN_W + KV_W + LANES * (m + 1))
            dz_ref[:, vc] = folded(dvc_ref, dvp_ref, m).astype(BF16)
        acc_ref[1:2, :] += gk_acc

    cur = lambda w: pl.BlockSpec((BLK, w), lambda n: (n, 0))
    nxt = lambda w: pl.BlockSpec((BLK, w), lambda n: (jnp.minimum(n + 1, nb - 1), 0))
    one = pl.BlockSpec((1, LANES), lambda n: (0, 0))
    return pl.pallas_call(
        body, name="qkv_bwd",
        out_shape=(jax.ShapeDtypeStruct((s, QKV_W), BF16), jax.ShapeDtypeStruct((SUBLANES, LANES), F32)),
        grid=(nb,),
        in_specs=[cur(QKV_W), cur(ATTN_W), cur(K2_W), nxt(K2_W), cur(K2_W), nxt(K2_W),
                  cur(LANES), cur(LANES), cur(LANES), one, one],
        out_specs=(cur(QKV_W), pl.BlockSpec((SUBLANES, LANES), lambda n: (0, 0))),
        compiler_params=_params("arbitrary"))(z_a, dq, dkc, dkp, dvc, dvp, ra, rbm, rbp, gq2, gk2)


IN_CHUNK = 512
N_CHUNK_A = QKV_W // IN_CHUNK
N_CHUNK = IN_W // IN_CHUNK


def _in_bwd(dz_a, dz_b, w_qkv, w_rest, x, dx1, g1, tm):
    s = x.shape[0]

    def body(da_ref, db_ref, wa_ref, wb_ref, x_ref, dx1_ref, g_ref, gx_ref, acc_ref, dh_ref):
        i, k = pl.program_id(0), pl.program_id(1)

        @pl.when(k == 0)
        def _():
            dh_ref[...] = _dot_nt(da_ref[...], wa_ref[...])

        @pl.when((k > 0) & (k < N_CHUNK_A))
        def _():
            dh_ref[...] += _dot_nt(da_ref[...], wa_ref[...])

        @pl.when(k >= N_CHUNK_A)
        def _():
            dh_ref[...] += _dot_nt(db_ref[...], wb_ref[...])

        @pl.when((i == 0) & (k == 0))
        def _():
            acc_ref[...] = jnp.zeros_like(acc_ref)

        @pl.when(k == N_CHUNK - 1)
        def _():
            dh = dh_ref[...]
            xn, r = _rms(x_ref[...])
            gx_ref[...] = dx1_ref[...] + _rms_bwd(dh * g_ref[...], xn, r)
            acc_ref[0:1, :] += jnp.sum(dh * xn, axis=0, keepdims=True)

    row = pl.BlockSpec((tm, D_MODEL), lambda i, k: (i, 0))
    ka = lambda i, k: jnp.minimum(k, N_CHUNK_A - 1)
    kb = lambda i, k: jnp.maximum(k - N_CHUNK_A, 0)
    return pl.pallas_call(
        body, name="in_bwd",
        out_shape=(jax.ShapeDtypeStruct((s, D_MODEL), F32), jax.ShapeDtypeStruct((SUBLANES, D_MODEL), F32)),
        grid=(s // tm, N_CHUNK),
        in_specs=[pl.BlockSpec((tm, IN_CHUNK), lambda i, k: (i, ka(i, k))),
                  pl.BlockSpec((tm, IN_CHUNK), lambda i, k: (i, kb(i, k))),
                  pl.BlockSpec((D_MODEL, IN_CHUNK), lambda i, k: (0, ka(i, k))),
                  pl.BlockSpec((D_MODEL, IN_CHUNK), lambda i, k: (0, kb(i, k))),
                  row, row, pl.BlockSpec((1, D_MODEL), lambda i, k: (0, 0))],
        out_specs=(row, pl.BlockSpec((SUBLANES, D_MODEL), lambda i, k: (0, 0))),
        scratch_shapes=[pltpu.VMEM((tm, D_MODEL), F32)],
        compiler_params=_params("arbitrary", "arbitrary"))(dz_a, dz_b, w_qkv, w_rest, x, dx1, g1)


def _mm_tn(a, bs, tn, tk, name):
    kdim, m = a.shape
    nblk = [b.shape[1] // tn for b in bs]
    starts = [sum(nblk[:t]) for t in range(len(bs))]
    nk = kdim // tk

    def body(a_ref, *refs):
        b_refs, o_ref, acc_ref = refs[:len(bs)], refs[len(bs)], refs[len(bs) + 1]
        j, k = pl.program_id(0), pl.program_id(1)
        at = a_ref[...].astype(BF16)
        for t, b_ref in enumerate(b_refs):
            inside = (j >= starts[t]) & (j < starts[t] + nblk[t])

            @pl.when(inside & (k == 0))
            def _():
                acc_ref[...] = _dot_tn(at, b_ref[...])

            @pl.when(inside & (k > 0))
            def _():
                acc_ref[...] += _dot_tn(at, b_ref[...])

        @pl.when(k == nk - 1)
        def _():
            o_ref[...] = acc_ref[...].astype(BF16)

    def b_spec(t):
        return pl.BlockSpec((tk, tn), lambda j, k: (k, jnp.clip(j - starts[t], 0, nblk[t] - 1)))

    return pl.pallas_call(
        body, name=name,
        out_shape=jax.ShapeDtypeStruct((m, sum(nblk) * tn), BF16),
        grid=(sum(nblk), nk),
        in_specs=[pl.BlockSpec((tk, m), lambda j, k: (k, 0))] + [b_spec(t) for t in range(len(bs))],
        out_specs=pl.BlockSpec((m, tn), lambda j, k: (0, j)),
        scratch_shapes=[pltpu.VMEM((m, tn), F32)],
        compiler_params=_params("parallel", "arbitrary"))(a, *bs)


def _place():
    return lax.axis_index("x"), lax.axis_index("y"), lax.axis_index("c")


ANY = pl.BlockSpec(memory_space=pl.ANY)


def _all_gather(shards):
    na = len(shards)

    def body(*refs):
        ins, outs = refs[:na], refs[na:2 * na]
        send_sems, recv_sems, local_sems = refs[2 * na:]
        x, y, c = _place()
        me, sibling = (x, y, c), (x, y, 1 - c)
        chips = [(1 - x, y), (x, 1 - y), (1 - x, 1 - y)]

        def copy(t, k, block, to, src=None):
            dst = outs[t].at[4 * block[0] + 2 * block[1] + block[2]]
            return pltpu.make_async_remote_copy(
                src_ref=dst if src is None else src, dst_ref=dst, send_sem=send_sems.at[t, k],
                recv_sem=recv_sems.at[t, k], device_id=to, device_id_type=MESH)

        mine = [pltpu.make_async_copy(ins[t], outs[t].at[4 * x + 2 * y + c], local_sems.at[t]) for t in range(na)]
        for cp in mine:
            cp.start()
        first = []
        for j, chip in enumerate(chips):
            first += [copy(t, 1 + j, me, (*chip, c), src=ins[t]) for t in range(na)]
        first += [copy(t, 0, me, sibling, src=ins[t]) for t in range(na)]
        for cp in first:
            cp.start()
        passed = []
        for j, chip in enumerate(chips):
            for t in range(na):
                copy(t, 1 + j, (*chip, c), me).wait_recv()
                passed.append(copy(t, 4 + j, (*chip, c), sibling))
                passed[-1].start()
        for t in range(na):
            copy(t, 0, sibling, me).wait_recv()
        for j, chip in enumerate(chips):
            for t in range(na):
                copy(t, 4 + j, (*chip, 1 - c), me).wait_recv()
        for cp in first + passed:
            cp.wait_send()
        for cp in mine:
            cp.wait()

    return pl.pallas_call(
        body, name="all_gather_weights",
        out_shape=tuple(jax.ShapeDtypeStruct((N_DEV,) + a.shape, a.dtype) for a in shards),
        in_specs=[ANY] * na, out_specs=tuple([ANY] * na),
        scratch_shapes=[pltpu.SemaphoreType.DMA((na, 7)), pltpu.SemaphoreType.DMA((na, 7)),
                        pltpu.SemaphoreType.DMA((na,))])(*shards)


def _all_reduce_slab(slab, name):
    def body(in_ref, out_ref, gath_ref, send_sems, recv_sems):
        x, y, c = _place()
        me = 4 * x + 2 * y + c
        gath_ref[me] = in_ref[...]
        copies = []
        for k in range(1, N_DEV):
            peer = (x ^ (k >> 2), y ^ ((k >> 1) & 1), c ^ (k & 1))
            copies.append(pltpu.make_async_remote_copy(
                src_ref=in_ref, dst_ref=gath_ref.at[me], send_sem=send_sems.at[k - 1],
                recv_sem=recv_sems.at[k - 1], device_id=peer, device_id_type=MESH))
        for cp in copies:
            cp.start()
        for cp in copies:
            cp.wait_recv()
        for cp in copies:
            cp.wait_send()
        total = gath_ref[0]
        for d in range(1, N_DEV):
            total = total + gath_ref[d]
        out_ref[...] = total

    vmem = pl.BlockSpec(memory_space=pltpu.VMEM)
    return pl.pallas_call(
        body, name=name,
        out_shape=jax.ShapeDtypeStruct(slab.shape, F32),
        in_specs=[vmem], out_specs=vmem,
        scratch_shapes=[pltpu.VMEM((N_DEV,) + slab.shape, F32),
                        pltpu.SemaphoreType.DMA((N_DEV - 1,)), pltpu.SemaphoreType.DMA((N_DEV - 1,))])(slab)


def _pair_exchange(grads):
    na = len(grads)

    def body(*refs):
        ins, outs = refs[:na], refs[na:2 * na]
        send_sems, recv_sems = refs[2 * na:]
        x, y, c = _place()
        copies = [pltpu.make_async_remote_copy(
            src_ref=ins[t].at[:, 1 - c], dst_ref=outs[t], send_sem=send_sems.at[t], recv_sem=recv_sems.at[t],
            device_id=(x, y, 1 - c), device_id_type=MESH) for t in range(na)]
        for cp in copies:
            cp.start()
        for cp in copies:
            cp.wait()

    return pl.pallas_call(
        body, name="pair_exchange",
        out_shape=tuple(jax.ShapeDtypeStruct((4,) + g.shape[2:], g.dtype) for g in grads),
        in_specs=[ANY] * na, out_specs=tuple([ANY] * na),
        scratch_shapes=[pltpu.SemaphoreType.DMA((na,)), pltpu.SemaphoreType.DMA((na,))])(*grads)


def _pair_sum(g, r, place, tr, name):
    _, _, rows, cols = g.shape

    def body(place_ref, g_ref, r_ref, pb_ref, own_ref):
        tot = g_ref[0, 0].astype(F32) + r_ref[0].astype(F32)
        pb_ref[0] = tot.astype(BF16)

        @pl.when(pl.program_id(1) == place_ref[1])
        def _():
            own_ref[...] = tot

    grid_spec = pltpu.PrefetchScalarGridSpec(
        num_scalar_prefetch=1, grid=(rows // tr, 4),
        in_specs=[pl.BlockSpec((1, 1, tr, cols), lambda i, q, place_ref: (q, place_ref[0], i, 0)),
                  pl.BlockSpec((1, tr, cols), lambda i, q, place_ref: (q, i, 0))],
        out_specs=(pl.BlockSpec((1, tr, cols), lambda i, q, place_ref: (q, i, 0)),
                   pl.BlockSpec((tr, cols), lambda i, q, place_ref: (i, 0))))
    return pl.pallas_call(
        body, name=name, grid_spec=grid_spec,
        out_shape=(jax.ShapeDtypeStruct((4, rows, cols), BF16), jax.ShapeDtypeStruct((rows, cols), F32)),
        compiler_params=_params("arbitrary", "arbitrary"))(place, g, r)


def _chip_exchange(sums):
    na = len(sums)

    def body(*refs):
        ins, outs = refs[:na], refs[na:2 * na]
        send_sems, recv_sems = refs[2 * na:]
        x, y, c = _place()
        copies = []
        for k in (1, 2, 3):
            px, py = x ^ (k >> 1), y ^ (k & 1)
            copies += [pltpu.make_async_remote_copy(
                src_ref=ins[t].at[2 * px + py], dst_ref=outs[t].at[k - 1], send_sem=send_sems.at[t, k - 1],
                recv_sem=recv_sems.at[t, k - 1], device_id=(px, py, c), device_id_type=MESH) for t in range(na)]
        for cp in copies:
            cp.start()
        for cp in copies:
            cp.wait()

    return pl.pallas_call(
        body, name="chip_exchange",
        out_shape=tuple(jax.ShapeDtypeStruct((3,) + g.shape[1:], g.dtype) for g in sums),
        in_specs=[ANY] * na, out_specs=tuple([ANY] * na),
        scratch_shapes=[pltpu.SemaphoreType.DMA((na, 3)), pltpu.SemaphoreType.DMA((na, 3))])(*sums)


def _adamw_math(w, g, m, v):
    m = ADAM_B1 * m + (1.0 - ADAM_B1) * g
    v = ADAM_B2 * v + (1.0 - ADAM_B2) * (g * g)
    m_hat = m / (1.0 - ADAM_B1 ** ADAM_STEP)
    v_hat = v / (1.0 - ADAM_B2 ** ADAM_STEP)
    return -ADAM_LR * (m_hat / (jnp.sqrt(v_hat) + ADAM_EPS) + ADAM_WD * w), m, v


def _adamw(own, others, w, m, v, tr, name):
    rows, cols = w.shape
    blk = pl.BlockSpec((tr, cols), lambda i: (i, 0))

    def body(own_ref, oth_ref, w_ref, m_ref, v_ref, g_ref, d_ref, nm_ref, nv_ref):
        g = own_ref[...]
        for k in range(3):
            g = g + oth_ref[k].astype(F32)
        g_ref[...] = g
        d_ref[...], nm_ref[...], nv_ref[...] = _adamw_math(w_ref[...], g, m_ref[...], v_ref[...])

    out = jax.ShapeDtypeStruct((rows, cols), F32)
    return pl.pallas_call(
        body, name=name, out_shape=(out, out, out, out), grid=(rows // tr,),
        in_specs=[blk, pl.BlockSpec((3, tr, cols), lambda i: (0, i, 0)), blk, blk, blk],
        out_specs=(blk, blk, blk, blk),
        compiler_params=_params("parallel"))(own, others, w, m, v)


def _adamw_slab(w, g, m, v):
    def body(w_ref, g_ref, m_ref, v_ref, d_ref, nm_ref, nv_ref):
        d_ref[...], nm_ref[...], nv_ref[...] = _adamw_math(w_ref[...], g_ref[...], m_ref[...], v_ref[...])

    out = jax.ShapeDtypeStruct(w.shape, F32)
    vmem = pl.BlockSpec(memory_space=pltpu.VMEM)
    return pl.pallas_call(body, name="adamw_small", out_shape=(out, out, out),
                          in_specs=[vmem] * 4, out_specs=(vmem, vmem, vmem))(w, g, m, v)


def _row(v, width=D_MODEL):
    v = v.reshape(1, -1)
    return jnp.pad(v, ((0, 0), (0, width - v.shape[1])))


def _local_step(x, p, target, g1, w_qkv, w_rest, gq, gk, sinks, conv_w, w_out, g2, w_pg, b_pg, w_pp, g3):
    s = x.shape[0]
    tm = min(256, s)
    ra, rbm, rbp = _rope_tables(s)
    gq2 = jnp.tile(gq.reshape(1, HEAD), (1, 2))
    gk2 = jnp.tile(gk.reshape(1, HEAD), (1, 2))
    conv_wp = jnp.pad(conv_w, ((0, SUBLANES - conv_w.shape[0]), (0, 0)))

    h, z_a = _fwd_in_a(x, g1, w_qkv, min(512, s))
    z_b = _mm_nn(h, w_rest, min(512, s), 1024, "fwd_in_b")
    qn, k2, v2 = _qk_prep(z_a, ra, rbm, rbp, gq2, gk2, tm)
    a, mix = _attn_fwd(qn, k2, v2, z_b, conv_wp, sinks)
    x1, hn2 = _fwd_out(mix, w_out, x, g2, tm)
    dy, dgp, dt, acc_ple = _ple(hn2, w_pg, b_pg, p, w_pp, g3, x1, target, tm)

    dx1, dx1b, acc_g2 = _gate_bwd(dgp, w_pg, x1, dy, g2, tm)
    dmix = _mm_nt(dx1b, w_out, tm, "out_bwd")
    dq, dkc, dkp, dvc, dvp, dz_b, acc_attn = _attn_bwd(qn, k2, v2, a, z_b, dmix, conv_wp, sinks)
    dz_a, acc_qk = _qkv_bwd(z_a, dq, dkc, dkp, dvc, dvp, ra, rbm, rbp, gq2, gk2)
    grad_x, acc_g1 = _in_bwd(dz_a, dz_b, w_qkv, w_rest, x, dx1, g1, min(512, s))

    tk = min(512, s)
    gw_in = _mm_tn(h, [dz_a, dz_b], 512, tk, "grad_w_in")
    gw_out = _mm_tn(mix, [dx1b], 512, tk, "grad_w_out")
    gw_pg = _mm_tn(hn2, [dgp], 512, tk, "grad_w_ple_gate")
    gw_pp = _mm_tn(p, [dt], 512, tk, "grad_w_ple_proj")

    fold = lambda v: _row((v[:HEAD] + v[HEAD:]))
    rows = [acc_g1[0:1], acc_g2[0:1], acc_ple[0:1], acc_ple[1:2], fold(acc_qk[0]), fold(acc_qk[1]),
            _row(acc_attn[0, :N_Q_HEADS]), _row(acc_attn[1]), _row(acc_attn[2]), _row(acc_attn[3]), acc_ple[2:3]]
    return grad_x, (gw_in, gw_out, gw_pg, gw_pp), rows


ROW_CONV, ROW_LOSS = 7, 10


def _slab(rows):
    rows = list(rows)
    return jnp.concatenate(rows + [jnp.zeros((SLAB_ROWS - len(rows), D_MODEL), F32)], axis=0)


def kernel(x, p, norm_gain, w_in, q_norm_gain, k_norm_gain, attn_sinks, conv_w, w_out, ple_gate_norm_gain, w_ple_gate, b_ple_gate, w_ple_proj, ple_norm_gain, loss_target, m_norm_gain, m_w_in, m_q_norm_gain, m_k_norm_gain, m_attn_sinks, m_conv_w, m_w_out, m_ple_gate_norm_gain, m_w_ple_gate, m_b_ple_gate, m_w_ple_proj, m_ple_norm_gain, v_norm_gain, v_w_in, v_q_norm_gain, v_k_norm_gain, v_attn_sinks, v_conv_w, v_w_out, v_ple_gate_norm_gain, v_w_ple_gate, v_b_ple_gate, v_w_ple_proj, v_ple_norm_gain):
    me = 4 * lax.axis_index("x") + 2 * lax.axis_index("y") + lax.axis_index("c")
    conv_cols = conv_w.shape[2]

    g_in, g_out, g_pg, g_pp = _all_gather(
        [w_in[0].astype(BF16), w_out[0].astype(BF16), w_ple_gate[0].astype(BF16), w_ple_proj[0].astype(BF16)])
    w_full = jnp.transpose(g_in, (1, 0, 2)).reshape(D_MODEL, IN_W)
    w_qkv, w_rest = w_full[:, :QKV_W], w_full[:, QKV_W:]
    w_out_f = g_out.reshape(D_MODEL, D_MODEL)
    w_pg_f = g_pg.reshape(D_MODEL, D_MODEL)
    w_pp_f = jnp.transpose(g_pp, (1, 0, 2)).reshape(PLE_DIM, D_MODEL)
    conv_rows = [lax.dynamic_update_slice(jnp.zeros((1, D_MODEL), F32), conv_w[0, t:t + 1], (0, conv_cols * me))
                 for t in range(3)]
    conv_full = _all_reduce_slab(_slab(conv_rows), "gather_conv_w")[0:3, :ATTN_W]

    grad_x, (gw_in, gw_out, gw_pg, gw_pp), rows = _local_step(
        x[0], p[0, 0], loss_target[0], norm_gain, w_qkv, w_rest, q_norm_gain[0], k_norm_gain[0], attn_sinks,
        conv_full, w_out_f, ple_gate_norm_gain, w_pg_f, b_ple_gate, w_pp_f, ple_norm_gain)

    red = _all_reduce_slab(_slab(rows), "reduce_small")
    loss = jnp.sum(red[ROW_LOSS])
    g_conv = [lax.dynamic_slice(red[ROW_CONV + t:ROW_CONV + t + 1], (0, conv_cols * me), (1, conv_cols))
              for t in range(3)]
    small = [norm_gain, ple_gate_norm_gain, b_ple_gate, ple_norm_gain, q_norm_gain, k_norm_gain, attn_sinks]
    small_m = [m_norm_gain, m_ple_gate_norm_gain, m_b_ple_gate, m_ple_norm_gain, m_q_norm_gain, m_k_norm_gain,
               m_attn_sinks]
    small_v = [v_norm_gain, v_ple_gate_norm_gain, v_b_ple_gate, v_ple_norm_gain, v_q_norm_gain, v_k_norm_gain,
               v_attn_sinks]
    pack = lambda vs, cw: _slab([_row(t) for t in vs] + [_row(cw[0, t]) for t in range(3)])
    g_slab = _slab([red[t:t + 1] for t in range(ROW_CONV)] + [_row(t) for t in g_conv])
    d_slab, m_slab, v_slab = _adamw_slab(pack(small, conv_w), g_slab, pack(small_m, m_conv_w), pack(small_v, v_conv_w))

    def unpack(slab_):
        outs = [slab_[t:t + 1, :w.shape[1]] for t, w in enumerate(small)]
        return outs, slab_[ROW_CONV:ROW_CONV + 3, :conv_cols][None]

    gw_in_t = jnp.transpose(gw_in.reshape(D_MODEL, N_DEV, SHARD_IN), (1, 0, 2))
    gw_pp_t = jnp.transpose(gw_pp.reshape(PLE_DIM, N_DEV, PLE_DIM), (1, 0, 2))
    grads = [gw_in_t, gw_out.reshape(N_DEV, D_MODEL // N_DEV, D_MODEL),
             gw_pg.reshape(N_DEV, D_MODEL // N_DEV, D_MODEL), gw_pp_t]
    grads = [g.reshape((4, 2) + g.shape[1:]) for g in grads]
    from_sibling = _pair_exchange(grads)
    names = ("w_in", "w_out", "w_ple_gate", "w_ple_proj")
    place = jnp.stack([lax.axis_index("c"), 2 * lax.axis_index("x") + lax.axis_index("y")]).astype(jnp.int32)
    sums = [_pair_sum(g, r, place, 256, "pair_sum_" + nm) for g, r, nm in zip(grads, from_sibling, names)]
    from_chips = _chip_exchange([pb for pb, _ in sums])
    big = []
    for (_, own), oth, w, m, v, name in zip(
            sums, from_chips, (w_in, w_out, w_ple_gate, w_ple_proj), (m_w_in, m_w_out, m_w_ple_gate, m_w_ple_proj),
            (v_w_in, v_w_out, v_w_ple_gate, v_w_ple_proj), names):
        big.append([t[None] for t in _adamw(own, oth, w[0], m[0], v[0], 256, "adamw_" + name)])

    (g_s, g_cv), (d_s, d_cv), (m_s, m_cv), (v_s, v_cv) = (unpack(t) for t in (g_slab, d_slab, m_slab, v_slab))

    def order(sm, cv, k):
        return [sm[0], big[0][k], sm[4], sm[5], sm[6], cv, big[1][k], sm[1], big[2][k], sm[2], big[3][k], sm[3]]

    return (loss, grad_x[None], *order(g_s, g_cv, 0), *order(d_s, d_cv, 1), *order(m_s, m_cv, 2),
            *order(v_s, v_cv, 3))
```

```python
import functools

import jax
import jax.numpy as jnp
from jax import lax
from jax.experimental import pallas as pl
from jax.experimental.pallas import tpu as pltpu

F32, BF16 = jnp.float32, jnp.bfloat16

D_MODEL = 2048
PLE_DIM = 256
ATTN_W = 1024
HEAD = 64
N_Q_HEADS = 16
KV_W = 256
QKV_W = ATTN_W + 2 * KV_W
REST_W = 5 * 1024
IN_W = QKV_W + REST_W
K2_W = 4 * 128
ROT = 16
ROPE_THETA = 500000.0
EPS = 1e-6
NEG_INF = -1e30
BLK = 128
LANES = 128
SUBLANES = 8
N_DEV = 8
SHARD_IN = IN_W // N_DEV
SLAB_ROWS = 16
SUB_ROWS = 128
V7X_VMEM_LIMIT = 52 * 1024 * 1024

ADAM_LR, ADAM_B1, ADAM_B2, ADAM_EPS, ADAM_WD, ADAM_STEP = 0.001, 0.9, 0.999, 1e-08, 0.01, 10
MESH = pl.DeviceIdType.MESH


def _params(*semantics):
    return pltpu.CompilerParams(dimension_semantics=semantics, vmem_limit_bytes=V7X_VMEM_LIMIT)


ANY = pl.BlockSpec(memory_space=pl.ANY)


def _resident(shape):
    return pl.BlockSpec(shape, lambda *_: (0,) * len(shape), pipeline_mode=pl.Buffered(1))


def _dot(a, b):
    return jnp.dot(a, b, preferred_element_type=F32)


def _dot_nt(a, b):
    return lax.dot_general(a, b, (((1,), (1,)), ((), ())), preferred_element_type=F32)


def _rms(xf):
    r = lax.rsqrt(jnp.mean(xf * xf, axis=-1, keepdims=True) + EPS)
    return xf * r, r


def _rms_bwd(dxn, xn, r):
    return r * (dxn - xn * jnp.mean(dxn * xn, axis=-1, keepdims=True))


def _sig(g):
    return jax.nn.sigmoid(g)


def _dsilu(g, sg):
    return sg * (1.0 + g * (1.0 - sg))


def _low_half(shape):
    return lax.broadcasted_iota(jnp.int32, shape, len(shape) - 1) < HEAD


def _half_sums(v):
    lo = _low_half(v.shape)
    s_lo = jnp.sum(jnp.where(lo, v, 0.0), axis=-1, keepdims=True)
    s_hi = jnp.sum(jnp.where(lo, 0.0, v), axis=-1, keepdims=True)
    return jnp.where(lo, s_lo, s_hi)


def _rope(v, a, bm, bp):
    return v * a + pltpu.roll(v, LANES - ROT // 2, 1) * bm + pltpu.roll(v, ROT // 2, 1) * bp


def _rope_t(dy, a, bm, bp):
    return dy * a + pltpu.roll(dy * bm, ROT // 2, 1) + pltpu.roll(dy * bp, LANES - ROT // 2, 1)


def _dup_halves(v):
    lo = _low_half(v.shape)
    a = jnp.where(lo, v, 0.0)
    b = jnp.where(lo, 0.0, v)
    return a + pltpu.roll(a, HEAD, 1), b + pltpu.roll(b, HEAD, 1)


def _rope_tables(s):
    half = ROT // 2
    lane = lax.broadcasted_iota(jnp.int32, (s, LANES), 1) % HEAD
    pos = lax.broadcasted_iota(jnp.int32, (s, LANES), 0).astype(F32)
    inv_freq = jnp.power(jnp.float32(ROPE_THETA), -(lane % half).astype(F32) * 2.0 / ROT)
    ang = pos * inv_freq
    cos, sin = jnp.cos(ang), jnp.sin(ang)
    a = jnp.where(lane < ROT, cos, 1.0)
    bm = jnp.where(lane < half, -sin, 0.0)
    bp = jnp.where((lane >= half) & (lane < ROT), sin, 0.0)
    return a, bm, bp


def _fwd_in_a(x, g1, w_qkv, tm):
    s = x.shape[0]

    def body(x_ref, g_ref, w_ref, h_ref, ht_ref, z_ref):
        xn, _ = _rms(x_ref[...])
        h = (xn * g_ref[...]).astype(BF16)
        h_ref[...] = h
        ht_ref[...] = h.T
        z_ref[...] = _dot(h, w_ref[...])

    return pl.pallas_call(
        body, name="fwd_in_a",
        out_shape=(jax.ShapeDtypeStruct((s, D_MODEL), BF16), jax.ShapeDtypeStruct((D_MODEL, s), BF16),
                   jax.ShapeDtypeStruct((s, QKV_W), F32)),
        grid=(s // tm,),
        in_specs=[pl.BlockSpec((tm, D_MODEL), lambda i: (i, 0)),
                  pl.BlockSpec((1, D_MODEL), lambda i: (0, 0)),
                  _resident((D_MODEL, QKV_W))],
        out_specs=(pl.BlockSpec((tm, D_MODEL), lambda i: (i, 0)),
                   pl.BlockSpec((D_MODEL, tm), lambda i: (0, i)),
                   pl.BlockSpec((tm, QKV_W), lambda i: (i, 0))),
        compiler_params=_params("parallel"))(x, g1, w_qkv)


def _mm_nn(a, b, tm, tn, name):
    m, k = a.shape
    n = b.shape[1]

    def body(a_ref, b_ref, o_ref):
        o_ref[...] = _dot(a_ref[...], b_ref[...])

    return pl.pallas_call(
        body, name=name,
        out_shape=jax.ShapeDtypeStruct((m, n), F32),
        grid=(n // tn, m // tm),
        in_specs=[pl.BlockSpec((tm, k), lambda j, i: (i, 0)),
                  pl.BlockSpec((k, tn), lambda j, i: (0, j))],
        out_specs=pl.BlockSpec((tm, tn), lambda j, i: (i, j)),
        compiler_params=_params("parallel", "parallel"))(a, b)


def _qk_prep(z_a, ra, rbm, rbp, gq2, gk2, tm):
    s = z_a.shape[0]

    def body(z_ref, a_ref, bm_ref, bp_ref, gq_ref, gk_ref, q_ref, k2_ref, v2_ref):
        a, bm, bp = a_ref[...], bm_ref[...], bp_ref[...]
        for r in range(ATTN_W // LANES):
            x = z_ref[:, LANES * r:LANES * (r + 1)]
            rr = lax.rsqrt(_half_sums(x * x) * (1.0 / HEAD) + EPS)
            q_ref[:, LANES * r:LANES * (r + 1)] = _rope(x * rr * gq_ref[...], a, bm, bp).astype(BF16)
        for m in range(KV_W // LANES):
            x = z_ref[:, ATTN_W + LANES * m:ATTN_W + LANES * (m + 1)]
            rr = lax.rsqrt(_half_sums(x * x) * (1.0 / HEAD) + EPS)
            k_lo, k_hi = _dup_halves(_rope(x * rr * gk_ref[...], a, bm, bp))
            k2_ref[:, 2 * LANES * m:2 * LANES * m + LANES] = k_lo.astype(BF16)
            k2_ref[:, 2 * LANES * m + LANES:2 * LANES * (m + 1)] = k_hi.astype(BF16)
            v_lo, v_hi = _dup_halves(z_ref[:, ATTN_W + KV_W + LANES * m:ATTN_W + KV_W + LANES * (m + 1)])
            v2_ref[:, 2 * LANES * m:2 * LANES * m + LANES] = v_lo.astype(BF16)
            v2_ref[:, 2 * LANES * m + LANES:2 * LANES * (m + 1)] = v_hi.astype(BF16)

    row = lambda w: pl.BlockSpec((tm, w), lambda i: (i, 0))
    one = pl.BlockSpec((1, LANES), lambda i: (0, 0))
    return pl.pallas_call(
        body, name="qk_prep",
        out_shape=(jax.ShapeDtypeStruct((s, ATTN_W), BF16), jax.ShapeDtypeStruct((s, K2_W), BF16),
                   jax.ShapeDtypeStruct((s, K2_W), BF16)),
        grid=(s // tm,),
        in_specs=[row(QKV_W), row(LANES), row(LANES), row(LANES), one, one],
        out_specs=(row(ATTN_W), row(K2_W), row(K2_W)),
        compiler_params=_params("parallel"))(z_a, ra, rbm, rbp, gq2, gk2)


def _window_mask(n):
    row = lax.broadcasted_iota(jnp.int32, (BLK, 2 * BLK), 0)
    col = lax.broadcasted_iota(jnp.int32, (BLK, 2 * BLK), 1)
    return (col > row) & (col <= row + BLK) & ((col >= BLK) | (n > 0))


def _head_probs(qm, kw, valid, sink):
    sc = jnp.where(valid, _dot_nt(qm, kw) * (HEAD ** -0.5), NEG_INF)
    mx = jnp.maximum(jnp.max(sc, axis=-1, keepdims=True), sink)
    ex = jnp.exp(sc - mx)
    den = jnp.sum(ex, axis=-1, keepdims=True) + jnp.exp(sink - mx)
    return ex / den, mx, den


def _conv_fwd(zb_ref, zbp_ref, cw_ref, ext_ref, n):
    u = zb_ref[:, 2048:3072] * zb_ref[:, 3072:4096]
    pu = zbp_ref[:, 2048:3072] * zbp_ref[:, 3072:4096]
    ext_ref[0:SUBLANES, :] = jnp.where(n > 0, pu, 0.0)
    ext_ref[SUBLANES:SUBLANES + BLK, :] = u
    um1 = ext_ref[SUBLANES - 1:SUBLANES - 1 + BLK, :]
    um2 = ext_ref[SUBLANES - 2:SUBLANES - 2 + BLK, :]
    cv = cw_ref[0:1, :] * um2 + cw_ref[1:2, :] * um1 + cw_ref[2:3, :] * u
    return u, um1, um2, cv


def _prev_rows(n):
    return (jnp.maximum(n * (BLK // SUBLANES) - 1, 0), 0)


def _attn_fwd(qn, k2, v2, z_b, conv_wp, sinks):
    s = qn.shape[0]
    nb = s // BLK

    def body(sink_ref, q_ref, kc_ref, kp_ref, vc_ref, vp_ref, zb_ref, zbp_ref, cw_ref, a_ref, mix_ref, mixt_ref,
             ext_ref):
        n = pl.program_id(0)
        valid = _window_mask(n)
        lo = _low_half((BLK, LANES))
        for r in range(ATTN_W // LANES):
            kvh = r // 2
            cols = slice(LANES * kvh, LANES * (kvh + 1))
            qp = q_ref[:, LANES * r:LANES * (r + 1)]
            kw = jnp.concatenate([kp_ref[:, cols], kc_ref[:, cols]], axis=0)
            vw = jnp.concatenate([vp_ref[:, cols], vc_ref[:, cols]], axis=0)
            outs = []
            for e in range(2):
                qm = jnp.where(lo if e == 0 else jnp.logical_not(lo), qp, jnp.zeros_like(qp))
                p, _, _ = _head_probs(qm, kw, valid, sink_ref[0, 2 * r + e])
                outs.append(_dot(p.astype(BF16), vw))
            a = jnp.where(lo, outs[0], outs[1])
            a_ref[:, LANES * r:LANES * (r + 1)] = a
            g = zb_ref[:, LANES * r:LANES * (r + 1)]
            mix_ref[:, LANES * r:LANES * (r + 1)] = (a * (g * _sig(g))).astype(BF16)
        _, _, _, cv = _conv_fwd(zb_ref, zbp_ref, cw_ref, ext_ref, n)
        gc = zb_ref[:, 4096:5120]
        mix_ref[:, ATTN_W:D_MODEL] = (zb_ref[:, 1024:2048] * cv * (gc * _sig(gc))).astype(BF16)
        mixt_ref[...] = mix_ref[...].T

    cur = lambda w: pl.BlockSpec((BLK, w), lambda n: (n, 0))
    prev = lambda w: pl.BlockSpec((BLK, w), lambda n: (jnp.maximum(n - 1, 0), 0))
    return pl.pallas_call(
        body, name="attn_fwd",
        out_shape=(jax.ShapeDtypeStruct((s, ATTN_W), F32), jax.ShapeDtypeStruct((s, D_MODEL), BF16),
                   jax.ShapeDtypeStruct((D_MODEL, s), BF16)),
        grid=(nb,),
        in_specs=[pl.BlockSpec(memory_space=pltpu.SMEM),
                  cur(ATTN_W), cur(K2_W), prev(K2_W), cur(K2_W), prev(K2_W), cur(REST_W),
                  pl.BlockSpec((SUBLANES, REST_W), _prev_rows),
                  pl.BlockSpec((SUBLANES, ATTN_W), lambda n: (0, 0))],
        out_specs=(cur(ATTN_W), cur(D_MODEL), pl.BlockSpec((D_MODEL, BLK), lambda n: (0, n))),
        scratch_shapes=[pltpu.VMEM((BLK + 2 * SUBLANES, ATTN_W), F32)],
        compiler_params=_params("parallel"))(sinks, qn, k2, k2, v2, v2, z_b, z_b, conv_wp)


def _fwd_out(mix, w_out, x, g2, tm):
    s = x.shape[0]

    def body(m_ref, w_ref, x_ref, g_ref, x1_ref, h_ref, ht_ref):
        x1 = x_ref[...] + _dot(m_ref[...], w_ref[...])
        x1_ref[...] = x1
        xn, _ = _rms(x1)
        h = (xn * g_ref[...]).astype(BF16)
        h_ref[...] = h
        ht_ref[...] = h.T

    row = pl.BlockSpec((tm, D_MODEL), lambda i: (i, 0))
    return pl.pallas_call(
        body, name="fwd_out",
        out_shape=(jax.ShapeDtypeStruct((s, D_MODEL), F32), jax.ShapeDtypeStruct((s, D_MODEL), BF16),
                   jax.ShapeDtypeStruct((D_MODEL, s), BF16)),
        grid=(s // tm,),
        in_specs=[row, _resident((D_MODEL, D_MODEL)), row, pl.BlockSpec((1, D_MODEL), lambda i: (0, 0))],
        out_specs=(row, row, pl.BlockSpec((D_MODEL, tm), lambda i: (0, i))),
        compiler_params=_params("parallel"))(mix, w_out, x, g2)


def _ple(hn2, w_pg, b_pg, p, w_pp, g3, x1, target, tm):
    s = x1.shape[0]

    def body(h_ref, wg_ref, b_ref, p_ref, wp_ref, g3_ref, x1_ref, t_ref, dy_ref, dgp_ref, dt_ref, pt_ref, acc_ref):
        gate = _sig(_dot(h_ref[...], wg_ref[...]) + b_ref[...])
        pb = p_ref[...].astype(BF16)
        pt_ref[...] = pb.T
        t = _dot(pb, wp_ref[...])
        tn, r3 = _rms(t)
        e = tn * g3_ref[...]
        diff = x1_ref[...] + gate * e - t_ref[...]
        dy = diff * (1.0 / D_MODEL)
        dy_ref[...] = dy
        dgp = dy * e * (gate * (1.0 - gate))
        dgp_ref[...] = dgp.astype(BF16)
        de = dy * gate
        dt_ref[...] = _rms_bwd(de * g3_ref[...], tn, r3).astype(BF16)

        @pl.when(pl.program_id(0) == 0)
        def _():
            acc_ref[...] = jnp.zeros_like(acc_ref)

        acc_ref[0:1, :] += jnp.sum(dgp, axis=0, keepdims=True)
        acc_ref[1:2, :] += jnp.sum(de * tn, axis=0, keepdims=True)
        acc_ref[2:3, :] += jnp.sum(diff * diff, axis=0, keepdims=True) * (0.5 / D_MODEL)

    row = pl.BlockSpec((tm, D_MODEL), lambda i: (i, 0))
    vec = pl.BlockSpec((1, D_MODEL), lambda i: (0, 0))
    return pl.pallas_call(
        body, name="ple",
        out_shape=(jax.ShapeDtypeStruct((s, D_MODEL), F32), jax.ShapeDtypeStruct((s, D_MODEL), BF16),
                   jax.ShapeDtypeStruct((s, D_MODEL), BF16), jax.ShapeDtypeStruct((PLE_DIM, s), BF16),
                   jax.ShapeDtypeStruct((SUBLANES, D_MODEL), F32)),
        grid=(s // tm,),
        in_specs=[row, _resident((D_MODEL, D_MODEL)), vec, pl.BlockSpec((tm, PLE_DIM), lambda i: (i, 0)),
                  _resident((PLE_DIM, D_MODEL)), vec, row, row],
        out_specs=(row, row, row, pl.BlockSpec((PLE_DIM, tm), lambda i: (0, i)),
                   pl.BlockSpec((SUBLANES, D_MODEL), lambda i: (0, 0))),
        compiler_params=_params("arbitrary"))(hn2, w_pg, b_pg, p, w_pp, g3, x1, target)


def _gate_bwd(dgp, w_pg, x1, dy, g2, tm):
    s = x1.shape[0]

    def body(d_ref, w_ref, x1_ref, dy_ref, g_ref, dx_ref, dxb_ref, acc_ref):
        dh = _dot_nt(d_ref[...], w_ref[...])
        xn, r = _rms(x1_ref[...])
        dx1 = dy_ref[...] + _rms_bwd(dh * g_ref[...], xn, r)
        dx_ref[...] = dx1
        dxb_ref[...] = dx1.astype(BF16)

        @pl.when(pl.program_id(0) == 0)
        def _():
            acc_ref[...] = jnp.zeros_like(acc_ref)

        acc_ref[0:1, :] += jnp.sum(dh * xn, axis=0, keepdims=True)

    row = pl.BlockSpec((tm, D_MODEL), lambda i: (i, 0))
    return pl.pallas_call(
        body, name="gate_bwd",
        out_shape=(jax.ShapeDtypeStruct((s, D_MODEL), F32), jax.ShapeDtypeStruct((s, D_MODEL), BF16),
                   jax.ShapeDtypeStruct((SUBLANES, D_MODEL), F32)),
        grid=(s // tm,),
        in_specs=[row, _resident((D_MODEL, D_MODEL)), row, row, pl.BlockSpec((1, D_MODEL), lambda i: (0, 0))],
        out_specs=(row, row, pl.BlockSpec((SUBLANES, D_MODEL), lambda i: (0, 0))),
        compiler_params=_params("arbitrary"))(dgp, w_pg, x1, dy, g2)


def _mm_nt(a, b, tm, name):
    m, k = a.shape
    n = b.shape[0]

    def body(a_ref, b_ref, o_ref):
        o_ref[...] = _dot_nt(a_ref[...], b_ref[...])

    return pl.pallas_call(
        body, name=name,
        out_shape=jax.ShapeDtypeStruct((m, n), F32),
        grid=(m // tm,),
        in_specs=[pl.BlockSpec((tm, k), lambda i: (i, 0)), _resident((n, k))],
        out_specs=pl.BlockSpec((tm, n), lambda i: (i, 0)),
        compiler_params=_params("parallel"))(a, b)


def _attn_bwd(qn, k2, v2, a, z_b, dmix, conv_wp, sinks):
    s = qn.shape[0]
    nb = s // BLK

    def body(sink_ref, q_ref, kc_ref, kp_ref, vc_ref, vp_ref, a_ref, zb_ref, zbp_ref, zbn_ref, dm_ref, dmn_ref,
             cw_ref, dq_ref, dkc_ref, dkp_ref, dvc_ref, dvp_ref, dzb_ref, acc_ref, ext_ref):
        n = pl.program_id(0)
        valid = _window_mask(n)
        lo = _low_half((BLK, LANES))
        lane = lax.broadcasted_iota(jnp.int32, (1, ATTN_W), 1)

        @pl.when(n == 0)
        def _():
            acc_ref[...] = jnp.zeros_like(acc_ref)

        dsink = jnp.zeros((1, ATTN_W), F32)
        for kvh in range(K2_W // LANES):
            cols = slice(LANES * kvh, LANES * (kvh + 1))
            kw = jnp.concatenate([kp_ref[:, cols], kc_ref[:, cols]], axis=0)
            vw = jnp.concatenate([vp_ref[:, cols], vc_ref[:, cols]], axis=0)
            dk2 = jnp.zeros((2 * BLK, LANES), F32)
            dv2 = jnp.zeros((2 * BLK, LANES), F32)
            for r in (2 * kvh, 2 * kvh + 1):
                rc = slice(LANES * r, LANES * (r + 1))
                g = zb_ref[:, rc]
                sg = _sig(g)
                dm = dm_ref[:, rc]
                av = a_ref[:, rc]
                da = dm * (g * sg)
                dzb_ref[:, rc] = (dm * av * _dsilu(g, sg)).astype(BF16)
                qp = q_ref[:, rc]
                dqs = []
                for e in range(2):
                    half = lo if e == 0 else jnp.logical_not(lo)
                    sink = sink_ref[0, 2 * r + e]
                    qm = jnp.where(half, qp, jnp.zeros_like(qp))
                    p, mx, den = _head_probs(qm, kw, valid, sink)
                    do = jnp.where(half, da, 0.0)
                    delta = jnp.sum(do * av, axis=-1, keepdims=True)
                    dob = do.astype(BF16)
                    ds = p * (_dot_nt(dob, vw) - delta) * (HEAD ** -0.5)
                    dsb = ds.astype(BF16)
                    dqs.append(_dot(dsb, kw))
                    dk2 = dk2 + _dot(ds.T.astype(BF16), qm)
                    dv2 = dv2 + _dot(p.T.astype(BF16), dob)
                    dsk = -jnp.sum(jnp.exp(sink - mx) / den * delta, axis=0, keepdims=True)
                    dsink = dsink + jnp.where(lane == 2 * r + e, dsk, 0.0)
                dq_ref[:, rc] = jnp.where(lo, dqs[0], dqs[1])
            dkp_ref[:, cols] = dk2[0:BLK]
            dkc_ref[:, cols] = dk2[BLK:2 * BLK]
            dvp_ref[:, cols] = dv2[0:BLK]
            dvc_ref[:, cols] = dv2[BLK:2 * BLK]
        acc_ref[0:1, :] += dsink

        u, um1, um2, cv = _conv_fwd(zb_ref, zbp_ref, cw_ref, ext_ref, n)
        cb = zb_ref[:, 1024:2048]
        gc = zb_ref[:, 4096:5120]
        sgc = _sig(gc)
        dmc = dm_ref[:, ATTN_W:D_MODEL]
        t = dmc * (gc * sgc)
        dcv = t * cb
        dzb_ref[:, 1024:2048] = (t * cv).astype(BF16)
        dzb_ref[:, 4096:5120] = (dmc * cb * cv * _dsilu(gc, sgc)).astype(BF16)
        gcn = zbn_ref[:, 4096:5120]
        dcvn = dmn_ref[:, ATTN_W:D_MODEL] * (gcn * _sig(gcn)) * zbn_ref[:, 1024:2048]
        ext_ref[0:BLK, :] = dcv
        ext_ref[BLK:BLK + SUBLANES, :] = jnp.where(n < nb - 1, dcvn, 0.0)
        du = (cw_ref[2:3, :] * dcv + cw_ref[1:2, :] * ext_ref[1:1 + BLK, :]
              + cw_ref[0:1, :] * ext_ref[2:2 + BLK, :])
        dzb_ref[:, 2048:3072] = (du * zb_ref[:, 3072:4096]).astype(BF16)
        dzb_ref[:, 3072:4096] = (du * zb_ref[:, 2048:3072]).astype(BF16)
        acc_ref[1:2, :] += jnp.sum(dcv * um2, axis=0, keepdims=True)
        acc_ref[2:3, :] += jnp.sum(dcv * um1, axis=0, keepdims=True)
        acc_ref[3:4, :] += jnp.sum(dcv * u, axis=0, keepdims=True)

    cur = lambda w: pl.BlockSpec((BLK, w), lambda n: (n, 0))
    prev = lambda w: pl.BlockSpec((BLK, w), lambda n: (jnp.maximum(n - 1, 0), 0))
    nxt = lambda w: pl.BlockSpec(
        (SUBLANES, w), lambda n: (jnp.minimum((n + 1) * (BLK // SUBLANES), nb * (BLK // SUBLANES) - 1), 0))
    f32 = lambda w: jax.ShapeDtypeStruct((s, w), F32)
    return pl.pallas_call(
        body, name="attn_bwd",
        out_shape=(f32(ATTN_W), f32(K2_W), f32(K2_W), f32(K2_W), f32(K2_W),
                   jax.ShapeDtypeStruct((s, REST_W), BF16), jax.ShapeDtypeStruct((SUBLANES, ATTN_W), F32)),
        grid=(nb,),
        in_specs=[pl.BlockSpec(memory_space=pltpu.SMEM),
                  cur(ATTN_W), cur(K2_W), prev(K2_W), cur(K2_W), prev(K2_W), cur(ATTN_W), cur(REST_W),
                  pl.BlockSpec((SUBLANES, REST_W), _prev_rows), nxt(REST_W), cur(D_MODEL), nxt(D_MODEL),
                  pl.BlockSpec((SUBLANES, ATTN_W), lambda n: (0, 0))],
        out_specs=(cur(ATTN_W), cur(K2_W), cur(K2_W), cur(K2_W), cur(K2_W), cur(REST_W),
                   pl.BlockSpec((SUBLANES, ATTN_W), lambda n: (0, 0))),
        scratch_shapes=[pltpu.VMEM((BLK + 2 * SUBLANES, ATTN_W), F32)],
        compiler_params=_params("arbitrary"))(sinks, qn, k2, k2, v2, v2, a, z_b, z_b, z_b, dmix, dmix, conv_wp)


def _qkv_bwd(z_a, dq, dkc, dkp, dvc, dvp, ra, rbm, rbp, gq2, gk2):
    s = z_a.shape[0]
    nb = s // BLK

    def body(z_ref, dq_ref, dkc_ref, dkp_ref, dvc_ref, dvp_ref, a_ref, bm_ref, bp_ref, gq_ref, gk_ref,
             dz_ref, acc_ref):
        n = pl.program_id(0)
        a, bm, bp = a_ref[...], bm_ref[...], bp_ref[...]
        lo = _low_half((BLK, LANES))
        last = n == nb - 1

        @pl.when(n == 0)
        def _():
            acc_ref[...] = jnp.zeros_like(acc_ref)

        def norm_bwd(x, dy, gain):
            rr = lax.rsqrt(_half_sums(x * x) * (1.0 / HEAD) + EPS)
            xh = x * rr
            dxg = _rope_t(dy, a, bm, bp)
            dxh = dxg * gain
            dx = rr * (dxh - xh * (_half_sums(dxh * xh) * (1.0 / HEAD)))
            return dx, jnp.sum(dxg * xh, axis=0, keepdims=True)

        def folded(cur_ref, prev_ref, m):
            parts = []
            for h in (2 * m, 2 * m + 1):
                v = cur_ref[:, LANES * h:LANES * (h + 1)] + jnp.where(
                    last, 0.0, prev_ref[:, LANES * h:LANES * (h + 1)])
                parts.append(v + pltpu.roll(v, HEAD, 1))
            return jnp.where(lo, parts[0], parts[1])

        gq_acc = jnp.zeros((1, LANES), F32)
        for r in range(ATTN_W // LANES):
            rc = slice(LANES * r, LANES * (r + 1))
            dx, gg = norm_bwd(z_ref[:, rc], dq_ref[:, rc], gq_ref[...])
            dz_ref[:, rc] = dx.astype(BF16)
            gq_acc = gq_acc + gg
        acc_ref[0:1, :] += gq_acc
        gk_acc = jnp.zeros((1, LANES), F32)
        for m in range(KV_W // LANES):
            kc = slice(ATTN_W + LANES * m, ATTN_W + LANES * (m + 1))
            dx, gg = norm_bwd(z_ref[:, kc], folded(dkc_ref, dkp_ref, m), gk_ref[...])
            dz_ref[:, kc] = dx.astype(BF16)
            gk_acc = gk_acc + gg
            vc = slice(ATTN_W + KV_W + LANES * m, ATTN_W + KV_W + LANES * (m + 1))
            dz_ref[:, vc] = folded(dvc_ref, dvp_ref, m).astype(BF16)
        acc_ref[1:2, :] += gk_acc

    cur = lambda w: pl.BlockSpec((BLK, w), lambda n: (n, 0))
    nxt = lambda w: pl.BlockSpec((BLK, w), lambda n: (jnp.minimum(n + 1, nb - 1), 0))
    one = pl.BlockSpec((1, LANES), lambda n: (0, 0))
    return pl.pallas_call(
        body, name="qkv_bwd",
        out_shape=(jax.ShapeDtypeStruct((s, QKV_W), BF16), jax.ShapeDtypeStruct((SUBLANES, LANES), F32)),
        grid=(nb,),
        in_specs=[cur(QKV_W), cur(ATTN_W), cur(K2_W), nxt(K2_W), cur(K2_W), nxt(K2_W),
                  cur(LANES), cur(LANES), cur(LANES), one, one],
        out_specs=(cur(QKV_W), pl.BlockSpec((SUBLANES, LANES), lambda n: (0, 0))),
        compiler_params=_params("arbitrary"))(z_a, dq, dkc, dkp, dvc, dvp, ra, rbm, rbp, gq2, gk2)


REST_CHUNK = 1280
N_REST_CHUNKS = REST_W // REST_CHUNK


def _in_bwd(dz_a, dz_b, w_qkv, w_rest, x, dx1, g1, tm):
    s = x.shape[0]
    nk = 1 + N_REST_CHUNKS

    def body(da_ref, db_ref, wa_ref, wb_ref, x_hbm, dx1_hbm, g_ref, gx_ref, acc_ref, x_buf, dx1_buf, sems):
        i, k = pl.program_id(0), pl.program_id(1)
        rows = pl.ds(pl.multiple_of(i * tm, tm), tm)
        fetch = [pltpu.make_async_copy(x_hbm.at[rows], x_buf, sems.at[0]),
                 pltpu.make_async_copy(dx1_hbm.at[rows], dx1_buf, sems.at[1])]

        sub = min(SUB_ROWS, tm)
        blocks = [slice(r, r + sub) for r in range(0, tm, sub)]

        @pl.when(k == 0)
        def _():
            for cp in fetch:
                cp.start()
            gx_ref[...] = _dot_nt(da_ref[...], wa_ref[...])

        @pl.when(k > 0)
        def _():
            gx_ref[...] += _dot_nt(db_ref[...], wb_ref[...])

        @pl.when((i == 0) & (k == 0))
        def _():
            acc_ref[...] = jnp.zeros_like(acc_ref)

        @pl.when(k == nk - 1)
        def _():
            for cp in fetch:
                cp.wait()
            for rb in blocks:
                dh = gx_ref[rb, :]
                xn, r = _rms(x_buf[rb, :])
                gx_ref[rb, :] = dx1_buf[rb, :] + _rms_bwd(dh * g_ref[...], xn, r)
                acc_ref[0:1, :] += jnp.sum(dh * xn, axis=0, keepdims=True)

    kb = lambda i, k: jnp.maximum(k - 1, 0)
    return pl.pallas_call(
        body, name="in_bwd",
        out_shape=(jax.ShapeDtypeStruct((s, D_MODEL), F32), jax.ShapeDtypeStruct((SUBLANES, D_MODEL), F32)),
        grid=(s // tm, nk),
        in_specs=[pl.BlockSpec((tm, QKV_W), lambda i, k: (i, 0)),
                  pl.BlockSpec((tm, REST_CHUNK), lambda i, k: (i, kb(i, k))),
                  _resident((D_MODEL, QKV_W)),
                  pl.BlockSpec((D_MODEL, REST_CHUNK), lambda i, k: (0, kb(i, k))),
                  ANY, ANY, pl.BlockSpec((1, D_MODEL), lambda i, k: (0, 0))],
        out_specs=(pl.BlockSpec((tm, D_MODEL), lambda i, k: (i, 0)),
                   pl.BlockSpec((SUBLANES, D_MODEL), lambda i, k: (0, 0))),
        scratch_shapes=[pltpu.VMEM((tm, D_MODEL), F32), pltpu.VMEM((tm, D_MODEL), F32),
                        pltpu.SemaphoreType.DMA((2,))],
        compiler_params=_params("arbitrary", "arbitrary"))(dz_a, dz_b, w_qkv, w_rest, x, dx1, g1)


def _mm_grad(at, bs, tn, name):
    m, kdim = at.shape
    nblk = [b.shape[1] // tn for b in bs]
    starts = [sum(nblk[:t]) for t in range(len(bs))]

    def body(a_ref, *refs):
        b_refs, o_ref = refs[:len(bs)], refs[len(bs)]
        j = pl.program_id(0)
        for t, b_ref in enumerate(b_refs):
            @pl.when((j >= starts[t]) & (j < starts[t] + nblk[t]))
            def _():
                o_ref[...] = _dot(a_ref[...], b_ref[...]).astype(BF16)

    def b_spec(t):
        return pl.BlockSpec((kdim, tn), lambda j: (0, jnp.clip(j - starts[t], 0, nblk[t] - 1)))

    return pl.pallas_call(
        body, name=name,
        out_shape=jax.ShapeDtypeStruct((m, sum(nblk) * tn), BF16),
        grid=(sum(nblk),),
        in_specs=[_resident((m, kdim))] + [b_spec(t) for t in range(len(bs))],
        out_specs=pl.BlockSpec((m, tn), lambda j: (0, j)),
        compiler_params=_params("parallel"))(at, *bs)


def _place():
    return lax.axis_index("x"), lax.axis_index("y"), lax.axis_index("c")


def _all_gather(shards):
    na = len(shards)

    def body(*refs):
        ins, outs = refs[:na], refs[na:2 * na]
        send_sems, recv_sems, local_sems = refs[2 * na:]
        x, y, c = _place()
        me, sibling = (x, y, c), (x, y, 1 - c)
        chips = [(1 - x, y), (x, 1 - y), (1 - x, 1 - y)]

        def copy(t, k, block, to, src=None):
            dst = outs[t].at[4 * block[0] + 2 * block[1] + block[2]]
            return pltpu.make_async_remote_copy(
                src_ref=dst if src is None else src, dst_ref=dst, send_sem=send_sems.at[t, k],
                recv_sem=recv_sems.at[t, k], device_id=to, device_id_type=MESH)

        mine = [pltpu.make_async_copy(ins[t], outs[t].at[4 * x + 2 * y + c], local_sems.at[t]) for t in range(na)]
        for cp in mine:
            cp.start()
        first = []
        for j, chip in enumerate(chips):
            first += [copy(t, 1 + j, me, (*chip, c), src=ins[t]) for t in range(na)]
        first += [copy(t, 0, me, sibling, src=ins[t]) for t in range(na)]
        for cp in first:
            cp.start()
        passed = []
        for j, chip in enumerate(chips):
            for t in range(na):
                copy(t, 1 + j, (*chip, c), me).wait_recv()
                passed.append(copy(t, 4 + j, (*chip, c), sibling))
                passed[-1].start()
        for t in range(na):
            copy(t, 0, sibling, me).wait_recv()
        for j, chip in enumerate(chips):
            for t in range(na):
                copy(t, 4 + j, (*chip, 1 - c), me).wait_recv()
        for cp in first + passed:
            cp.wait_send()
        for cp in mine:
            cp.wait()

    return pl.pallas_call(
        body, name="all_gather_weights",
        out_shape=tuple(jax.ShapeDtypeStruct((N_DEV,) + a.shape, a.dtype) for a in shards),
        in_specs=[ANY] * na, out_specs=tuple([ANY] * na),
        scratch_shapes=[pltpu.SemaphoreType.DMA((na, 7)), pltpu.SemaphoreType.DMA((na, 7)),
                        pltpu.SemaphoreType.DMA((na,))])(*shards)


def _all_reduce_slab(slab, name):
    def body(in_ref, out_ref, gath_ref, send_sems, recv_sems):
        x, y, c = _place()
        me = 4 * x + 2 * y + c
        gath_ref[me] = in_ref[...]
        copies = []
        for k in range(1, N_DEV):
            peer = (x ^ (k >> 2), y ^ ((k >> 1) & 1), c ^ (k & 1))
            copies.append(pltpu.make_async_remote_copy(
                src_ref=in_ref, dst_ref=gath_ref.at[me], send_sem=send_sems.at[k - 1],
                recv_sem=recv_sems.at[k - 1], device_id=peer, device_id_type=MESH))
        for cp in copies:
            cp.start()
        for cp in copies:
            cp.wait_recv()
        for cp in copies:
            cp.wait_send()
        total = gath_ref[0]
        for d in range(1, N_DEV):
            total = total + gath_ref[d]
        out_ref[...] = total

    vmem = pl.BlockSpec(memory_space=pltpu.VMEM)
    return pl.pallas_call(
        body, name=name,
        out_shape=jax.ShapeDtypeStruct(slab.shape, F32),
        in_specs=[vmem], out_specs=vmem,
        scratch_shapes=[pltpu.VMEM((N_DEV,) + slab.shape, F32),
                        pltpu.SemaphoreType.DMA((N_DEV - 1,)), pltpu.SemaphoreType.DMA((N_DEV - 1,))])(slab)


def _pair_exchange(grads):
    na = len(grads)

    def body(*refs):
        ins, outs = refs[:na], refs[na:2 * na]
        send_sems, recv_sems = refs[2 * na:]
        x, y, c = _place()
        copies = [pltpu.make_async_remote_copy(
            src_ref=ins[t].at[:, 1 - c], dst_ref=outs[t], send_sem=send_sems.at[t], recv_sem=recv_sems.at[t],
            device_id=(x, y, 1 - c), device_id_type=MESH) for t in range(na)]
        for cp in copies:
            cp.start()
        for cp in copies:
            cp.wait()

    return pl.pallas_call(
        body, name="pair_exchange",
        out_shape=tuple(jax.ShapeDtypeStruct((4,) + g.shape[2:], g.dtype) for g in grads),
        in_specs=[ANY] * na, out_specs=tuple([ANY] * na),
        scratch_shapes=[pltpu.SemaphoreType.DMA((na,)), pltpu.SemaphoreType.DMA((na,))])(*grads)


def _pair_sum(g, r, place, tr, name):
    _, _, rows, cols = g.shape

    def body(place_ref, g_ref, r_ref, pb_ref, own_ref):
        tot = g_ref[0, 0].astype(F32) + r_ref[0].astype(F32)
        pb_ref[0] = tot.astype(BF16)

        @pl.when(pl.program_id(1) == place_ref[1])
        def _():
            own_ref[...] = tot

    grid_spec = pltpu.PrefetchScalarGridSpec(
        num_scalar_prefetch=1, grid=(rows // tr, 4),
        in_specs=[pl.BlockSpec((1, 1, tr, cols), lambda i, q, place_ref: (q, place_ref[0], i, 0)),
                  pl.BlockSpec((1, tr, cols), lambda i, q, place_ref: (q, i, 0))],
        out_specs=(pl.BlockSpec((1, tr, cols), lambda i, q, place_ref: (q, i, 0)),
                   pl.BlockSpec((tr, cols), lambda i, q, place_ref: (i, 0))))
    return pl.pallas_call(
        body, name=name, grid_spec=grid_spec,
        out_shape=(jax.ShapeDtypeStruct((4, rows, cols), BF16), jax.ShapeDtypeStruct((rows, cols), F32)),
        compiler_params=_params("arbitrary", "arbitrary"))(place, g, r)


def _chip_exchange(sums):
    na = len(sums)

    def body(*refs):
        ins, outs = refs[:na], refs[na:2 * na]
        send_sems, recv_sems = refs[2 * na:]
        x, y, c = _place()
        copies = []
        for k in (1, 2, 3):
            px, py = x ^ (k >> 1), y ^ (k & 1)
            copies += [pltpu.make_async_remote_copy(
                src_ref=ins[t].at[2 * px + py], dst_ref=outs[t].at[k - 1], send_sem=send_sems.at[t, k - 1],
                recv_sem=recv_sems.at[t, k - 1], device_id=(px, py, c), device_id_type=MESH) for t in range(na)]
        for cp in copies:
            cp.start()
        for cp in copies:
            cp.wait()

    return pl.pallas_call(
        body, name="chip_exchange",
        out_shape=tuple(jax.ShapeDtypeStruct((3,) + g.shape[1:], g.dtype) for g in sums),
        in_specs=[ANY] * na, out_specs=tuple([ANY] * na),
        scratch_shapes=[pltpu.SemaphoreType.DMA((na, 3)), pltpu.SemaphoreType.DMA((na, 3))])(*sums)


def _adamw_math(w, g, m, v):
    m = ADAM_B1 * m + (1.0 - ADAM_B1) * g
    v = ADAM_B2 * v + (1.0 - ADAM_B2) * (g * g)
    m_hat = m / (1.0 - ADAM_B1 ** ADAM_STEP)
    v_hat = v / (1.0 - ADAM_B2 ** ADAM_STEP)
    return -ADAM_LR * (m_hat / (jnp.sqrt(v_hat) + ADAM_EPS) + ADAM_WD * w), m, v


def _adamw(own, others, w, m, v, tr, name):
    rows, cols = w.shape
    blk = pl.BlockSpec((tr, cols), lambda i: (i, 0))

    def body(own_ref, oth_ref, w_ref, m_ref, v_ref, g_ref, d_ref, nm_ref, nv_ref):
        g = own_ref[...]
        for k in range(3):
            g = g + oth_ref[k].astype(F32)
        g_ref[...] = g
        d_ref[...], nm_ref[...], nv_ref[...] = _adamw_math(w_ref[...], g, m_ref[...], v_ref[...])

    out = jax.ShapeDtypeStruct((rows, cols), F32)
    return pl.pallas_call(
        body, name=name, out_shape=(out, out, out, out), grid=(rows // tr,),
        in_specs=[blk, pl.BlockSpec((3, tr, cols), lambda i: (0, i, 0)), blk, blk, blk],
        out_specs=(blk, blk, blk, blk),
        compiler_params=_params("parallel"))(own, others, w, m, v)


def _adamw_slab(w, g, m, v):
    def body(w_ref, g_ref, m_ref, v_ref, d_ref, nm_ref, nv_ref):
        d_ref[...], nm_ref[...], nv_ref[...] = _adamw_math(w_ref[...], g_ref[...], m_ref[...], v_ref[...])

    out = jax.ShapeDtypeStruct(w.shape, F32)
    vmem = pl.BlockSpec(memory_space=pltpu.VMEM)
    return pl.pallas_call(body, name="adamw_small", out_shape=(out, out, out),
                          in_specs=[vmem] * 4, out_specs=(vmem, vmem, vmem))(w, g, m, v)


def _row(v, width=D_MODEL):
    v = v.reshape(1, -1)
    return jnp.pad(v, ((0, 0), (0, width - v.shape[1])))


def _local_step(x, p, target, g1, w_qkv, w_rest, gq, gk, sinks, conv_w, w_out, g2, w_pg, b_pg, w_pp, g3):
    s = x.shape[0]
    tm = min(512, s)
    ra, rbm, rbp = _rope_tables(s)
    gq2 = jnp.tile(gq.reshape(1, HEAD), (1, 2))
    gk2 = jnp.tile(gk.reshape(1, HEAD), (1, 2))
    conv_wp = jnp.pad(conv_w, ((0, SUBLANES - conv_w.shape[0]), (0, 0)))

    h, ht, z_a = _fwd_in_a(x, g1, w_qkv, min(512, s))
    z_b = _mm_nn(h, w_rest, min(512, s), 1024, "fwd_in_b")
    qn, k2, v2 = _qk_prep(z_a, ra, rbm, rbp, gq2, gk2, min(256, s))
    a, mix, mixt = _attn_fwd(qn, k2, v2, z_b, conv_wp, sinks)
    x1, hn2, hn2t = _fwd_out(mix, w_out, x, g2, tm)
    dy, dgp, dt, pt, acc_ple = _ple(hn2, w_pg, b_pg, p, w_pp, g3, x1, target, min(256, s))

    dx1, dx1b, acc_g2 = _gate_bwd(dgp, w_pg, x1, dy, g2, tm)
    dmix = _mm_nt(dx1b, w_out, tm, "out_bwd")
    dq, dkc, dkp, dvc, dvp, dz_b, acc_attn = _attn_bwd(qn, k2, v2, a, z_b, dmix, conv_wp, sinks)
    dz_a, acc_qk = _qkv_bwd(z_a, dq, dkc, dkp, dvc, dvp, ra, rbm, rbp, gq2, gk2)
    grad_x, acc_g1 = _in_bwd(dz_a, dz_b, w_qkv, w_rest, x, dx1, g1, min(512, s))

    gw_in = _mm_grad(ht, [dz_a, dz_b], 512, "grad_w_in")
    gw_out = _mm_grad(mixt, [dx1b], 512, "grad_w_out")
    gw_pg = _mm_grad(hn2t, [dgp], 512, "grad_w_ple_gate")
    gw_pp = _mm_grad(pt, [dt], 512, "grad_w_ple_proj")

    fold = lambda v: _row((v[:HEAD] + v[HEAD:]))
    rows = [acc_g1[0:1], acc_g2[0:1], acc_ple[0:1], acc_ple[1:2], fold(acc_qk[0]), fold(acc_qk[1]),
            _row(acc_attn[0, :N_Q_HEADS]), _row(acc_attn[1]), _row(acc_attn[2]), _row(acc_attn[3]), acc_ple[2:3]]
    return grad_x, (gw_in, gw_out, gw_pg, gw_pp), rows


ROW_CONV, ROW_LOSS = 7, 10


def _slab(rows):
    rows = list(rows)
    return jnp.concatenate(rows + [jnp.zeros((SLAB_ROWS - len(rows), D_MODEL), F32)], axis=0)


def kernel(x, p, norm_gain, w_in, q_norm_gain, k_norm_gain, attn_sinks, conv_w, w_out, ple_gate_norm_gain, w_ple_gate, b_ple_gate, w_ple_proj, ple_norm_gain, loss_target, m_norm_gain, m_w_in, m_q_norm_gain, m_k_norm_gain, m_attn_sinks, m_conv_w, m_w_out, m_ple_gate_norm_gain, m_w_ple_gate, m_b_ple_gate, m_w_ple_proj, m_ple_norm_gain, v_norm_gain, v_w_in, v_q_norm_gain, v_k_norm_gain, v_attn_sinks, v_conv_w, v_w_out, v_ple_gate_norm_gain, v_w_ple_gate, v_b_ple_gate, v_w_ple_proj, v_ple_norm_gain):
    me = 4 * lax.axis_index("x") + 2 * lax.axis_index("y") + lax.axis_index("c")
    conv_cols = conv_w.shape[2]

    g_in, g_out, g_pg, g_pp = _all_gather(
        [w_in[0].astype(BF16), w_out[0].astype(BF16), w_ple_gate[0].astype(BF16), w_ple_proj[0].astype(BF16)])
    split = QKV_W - SHARD_IN
    w_qkv = jnp.concatenate([g_in[0], g_in[1][:, :split]], axis=1)
    w_rest = jnp.concatenate([g_in[1][:, split:]] + [g_in[d] for d in range(2, N_DEV)], axis=1)
    w_out_f = g_out.reshape(D_MODEL, D_MODEL)
    w_pg_f = g_pg.reshape(D_MODEL, D_MODEL)
    w_pp_f = jnp.transpose(g_pp, (1, 0, 2)).reshape(PLE_DIM, D_MODEL)
    conv_rows = [lax.dynamic_update_slice(jnp.zeros((1, D_MODEL), F32), conv_w[0, t:t + 1], (0, conv_cols * me))
                 for t in range(3)]
    conv_full = _all_reduce_slab(_slab(conv_rows), "gather_conv_w")[0:3, :ATTN_W]

    grad_x, (gw_in, gw_out, gw_pg, gw_pp), rows = _local_step(
        x[0], p[0, 0], loss_target[0], norm_gain, w_qkv, w_rest, q_norm_gain[0], k_norm_gain[0], attn_sinks,
        conv_full, w_out_f, ple_gate_norm_gain, w_pg_f, b_ple_gate, w_pp_f, ple_norm_gain)

    red = _all_reduce_slab(_slab(rows), "reduce_small")
    loss = jnp.sum(red[ROW_LOSS])
    g_conv = [lax.dynamic_slice(red[ROW_CONV + t:ROW_CONV + t + 1], (0, conv_cols * me), (1, conv_cols))
              for t in range(3)]
    small = [norm_gain, ple_gate_norm_gain, b_ple_gate, ple_norm_gain, q_norm_gain, k_norm_gain, attn_sinks]
    small_m = [m_norm_gain, m_ple_gate_norm_gain, m_b_ple_gate, m_ple_norm_gain, m_q_norm_gain, m_k_norm_gain,
               m_attn_sinks]
    small_v = [v_norm_gain, v_ple_gate_norm_gain, v_b_ple_gate, v_ple_norm_gain, v_q_norm_gain, v_k_norm_gain,
               v_attn_sinks]
    pack = lambda vs, cw: _slab([_row(t) for t in vs] + [_row(cw[0, t]) for t in range(3)])
    g_slab = _slab([red[t:t + 1] for t in range(ROW_CONV)] + [_row(t) for t in g_conv])
    d_slab, m_slab, v_slab = _adamw_slab(pack(small, conv_w), g_slab, pack(small_m, m_conv_w), pack(small_v, v_conv_w))

    def unpack(slab_):
        outs = [slab_[t:t + 1, :w.shape[1]] for t, w in enumerate(small)]
        return outs, slab_[ROW_CONV:ROW_CONV + 3, :conv_cols][None]

    gw_in_t = jnp.transpose(gw_in.reshape(D_MODEL, N_DEV, SHARD_IN), (1, 0, 2))
    gw_pp_t = jnp.transpose(gw_pp.reshape(PLE_DIM, N_DEV, PLE_DIM), (1, 0, 2))
    grads = [gw_in_t, gw_out.reshape(N_DEV, D_MODEL // N_DEV, D_MODEL),
             gw_pg.reshape(N_DEV, D_MODEL // N_DEV, D_MODEL), gw_pp_t]
    grads = [g.reshape((4, 2) + g.shape[1:]) for g in grads]
    from_sibling = _pair_exchange(grads)
    names = ("w_in", "w_out", "w_ple_gate", "w_ple_proj")
    place = jnp.stack([lax.axis_index("c"), 2 * lax.axis_index("x") + lax.axis_index("y")]).astype(jnp.int32)
    sums = [_pair_sum(g, r, place, 256, "pair_sum_" + nm) for g, r, nm in zip(grads, from_sibling, names)]
    from_chips = _chip_exchange([pb for pb, _ in sums])
    big = []
    for (_, own), oth, w, m, v, name in zip(
            sums, from_chips, (w_in, w_out, w_ple_gate, w_ple_proj), (m_w_in, m_w_out, m_w_ple_gate, m_w_ple_proj),
            (v_w_in, v_w_out, v_w_ple_gate, v_w_ple_proj), names):
        big.append([t[None] for t in _adamw(own, oth, w[0], m[0], v[0], 256, "adamw_" + name)])

    (g_s, g_cv), (d_s, d_cv), (m_s, m_cv), (v_s, v_cv) = (unpack(t) for t in (g_slab, d_slab, m_slab, v_slab))

    def order(sm, cv, k):
        return [sm[0], big[0][k], sm[4], sm[5], sm[6], cv, big[1][k], sm[1], big[2][k], sm[2], big[3][k], sm[3]]

    return (loss, grad_x[None], *order(g_s, g_cv, 0), *order(d_s, d_cv, 1), *order(m_s, m_cv, 2),
            *order(v_s, v_cv, 3))
```

```python
import functools

import jax
import jax.numpy as jnp
from jax import lax
from jax.experimental import pallas as pl
from jax.experimental.pallas import tpu as pltpu

F32, BF16 = jnp.float32, jnp.bfloat16

D_MODEL = 2048
PLE_DIM = 256
ATTN_W = 1024
HEAD = 64
N_Q_HEADS = 16
KV_W = 256
QKV_W = ATTN_W + 2 * KV_W
REST_W = 5 * 1024
IN_W = QKV_W + REST_W
K2_W = 4 * 128
ROT = 16
ROPE_THETA = 500000.0
EPS = 1e-6
NEG_INF = -1e30
BLK = 128
LANES = 128
SUBLANES = 8
N_DEV = 8
SHARD_IN = IN_W // N_DEV
SLAB_ROWS = 16
SUB_ROWS = 128
V7X_VMEM_LIMIT = 52 * 1024 * 1024

ADAM_LR, ADAM_B1, ADAM_B2, ADAM_EPS, ADAM_WD, ADAM_STEP = 0.001, 0.9, 0.999, 1e-08, 0.01, 10
MESH = pl.DeviceIdType.MESH


def _params(*semantics):
    return pltpu.CompilerParams(dimension_semantics=semantics, vmem_limit_bytes=V7X_VMEM_LIMIT)


ANY = pl.BlockSpec(memory_space=pl.ANY)


def _resident(shape):
    return pl.BlockSpec(shape, lambda *_: (0,) * len(shape), pipeline_mode=pl.Buffered(1))


def _dot(a, b):
    return jnp.dot(a, b, preferred_element_type=F32)


def _dot_nt(a, b):
    return lax.dot_general(a, b, (((1,), (1,)), ((), ())), preferred_element_type=F32)


def _rms(xf):
    r = lax.rsqrt(jnp.mean(xf * xf, axis=-1, keepdims=True) + EPS)
    return xf * r, r


def _rms_bwd(dxn, xn, r):
    return r * (dxn - xn * jnp.mean(dxn * xn, axis=-1, keepdims=True))


def _sig(g):
    return jax.nn.sigmoid(g)


def _dsilu(g, sg):
    return sg * (1.0 + g * (1.0 - sg))


def _low_half(shape):
    return lax.broadcasted_iota(jnp.int32, shape, len(shape) - 1) < HEAD


def _half_sums(v):
    lo = _low_half(v.shape)
    s_lo = jnp.sum(jnp.where(lo, v, 0.0), axis=-1, keepdims=True)
    s_hi = jnp.sum(jnp.where(lo, 0.0, v), axis=-1, keepdims=True)
    return jnp.where(lo, s_lo, s_hi)


def _rope(v, a, bm, bp):
    return v * a + pltpu.roll(v, LANES - ROT // 2, 1) * bm + pltpu.roll(v, ROT // 2, 1) * bp


def _rope_t(dy, a, bm, bp):
    return dy * a + pltpu.roll(dy * bm, ROT // 2, 1) + pltpu.roll(dy * bp, LANES - ROT // 2, 1)


def _dup_halves(v):
    lo = _low_half(v.shape)
    a = jnp.where(lo, v, 0.0)
    b = jnp.where(lo, 0.0, v)
    return a + pltpu.roll(a, HEAD, 1), b + pltpu.roll(b, HEAD, 1)


def _rope_tables(s):
    half = ROT // 2
    lane = lax.broadcasted_iota(jnp.int32, (s, LANES), 1) % HEAD
    pos = lax.broadcasted_iota(jnp.int32, (s, LANES), 0).astype(F32)
    inv_freq = jnp.power(jnp.float32(ROPE_THETA), -(lane % half).astype(F32) * 2.0 / ROT)
    ang = pos * inv_freq
    cos, sin = jnp.cos(ang), jnp.sin(ang)
    a = jnp.where(lane < ROT, cos, 1.0)
    bm = jnp.where(lane < half, -sin, 0.0)
    bp = jnp.where((lane >= half) & (lane < ROT), sin, 0.0)
    return a, bm, bp


def _fwd_in_a(x, g1, w_qkv, tm):
    s = x.shape[0]

    def body(x_ref, g_ref, w_ref, h_ref, ht_ref, z_ref):
        xn, _ = _rms(x_ref[...])
        h = (xn * g_ref[...]).astype(BF16)
        h_ref[...] = h
        ht_ref[...] = h.T
        z_ref[...] = _dot(h, w_ref[...])

    return pl.pallas_call(
        body, name="fwd_in_a",
        out_shape=(jax.ShapeDtypeStruct((s, D_MODEL), BF16), jax.ShapeDtypeStruct((D_MODEL, s), BF16),
                   jax.ShapeDtypeStruct((s, QKV_W), F32)),
        grid=(s // tm,),
        in_specs=[pl.BlockSpec((tm, D_MODEL), lambda i: (i, 0)),
                  pl.BlockSpec((1, D_MODEL), lambda i: (0, 0)),
                  _resident((D_MODEL, QKV_W))],
        out_specs=(pl.BlockSpec((tm, D_MODEL), lambda i: (i, 0)),
                   pl.BlockSpec((D_MODEL, tm), lambda i: (0, i)),
                   pl.BlockSpec((tm, QKV_W), lambda i: (i, 0))),
        compiler_params=_params("parallel"))(x, g1, w_qkv)


def _mm_nn(a, b, tm, tn, name):
    m, k = a.shape
    n = b.shape[1]

    def body(a_ref, b_ref, o_ref):
        o_ref[...] = _dot(a_ref[...], b_ref[...])

    return pl.pallas_call(
        body, name=name,
        out_shape=jax.ShapeDtypeStruct((m, n), F32),
        grid=(n // tn, m // tm),
        in_specs=[pl.BlockSpec((tm, k), lambda j, i: (i, 0)),
                  pl.BlockSpec((k, tn), lambda j, i: (0, j))],
        out_specs=pl.BlockSpec((tm, tn), lambda j, i: (i, j)),
        compiler_params=_params("parallel", "parallel"))(a, b)


def _qk_prep(z_a, ra, rbm, rbp, gq2, gk2, tm):
    s = z_a.shape[0]

    def body(z_ref, a_ref, bm_ref, bp_ref, gq_ref, gk_ref, q_ref, k2_ref, v2_ref):
        a, bm, bp = a_ref[...], bm_ref[...], bp_ref[...]
        for r in range(ATTN_W // LANES):
            x = z_ref[:, LANES * r:LANES * (r + 1)]
            rr = lax.rsqrt(_half_sums(x * x) * (1.0 / HEAD) + EPS)
            q_ref[:, LANES * r:LANES * (r + 1)] = _rope(x * rr * gq_ref[...], a, bm, bp).astype(BF16)
        for m in range(KV_W // LANES):
            x = z_ref[:, ATTN_W + LANES * m:ATTN_W + LANES * (m + 1)]
            rr = lax.rsqrt(_half_sums(x * x) * (1.0 / HEAD) + EPS)
            k_lo, k_hi = _dup_halves(_rope(x * rr * gk_ref[...], a, bm, bp))
            k2_ref[:, 2 * LANES * m:2 * LANES * m + LANES] = k_lo.astype(BF16)
            k2_ref[:, 2 * LANES * m + LANES:2 * LANES * (m + 1)] = k_hi.astype(BF16)
            v_lo, v_hi = _dup_halves(z_ref[:, ATTN_W + KV_W + LANES * m:ATTN_W + KV_W + LANES * (m + 1)])
            v2_ref[:, 2 * LANES * m:2 * LANES * m + LANES] = v_lo.astype(BF16)
            v2_ref[:, 2 * LANES * m + LANES:2 * LANES * (m + 1)] = v_hi.astype(BF16)

    row = lambda w: pl.BlockSpec((tm, w), lambda i: (i, 0))
    one = pl.BlockSpec((1, LANES), lambda i: (0, 0))
    return pl.pallas_call(
        body, name="qk_prep",
        out_shape=(jax.ShapeDtypeStruct((s, ATTN_W), BF16), jax.ShapeDtypeStruct((s, K2_W), BF16),
                   jax.ShapeDtypeStruct((s, K2_W), BF16)),
        grid=(s // tm,),
        in_specs=[row(QKV_W), row(LANES), row(LANES), row(LANES), one, one],
        out_specs=(row(ATTN_W), row(K2_W), row(K2_W)),
        compiler_params=_params("parallel"))(z_a, ra, rbm, rbp, gq2, gk2)


def _window_mask(n):
    row = lax.broadcasted_iota(jnp.int32, (BLK, 2 * BLK), 0)
    col = lax.broadcasted_iota(jnp.int32, (BLK, 2 * BLK), 1)
    return (col > row) & (col <= row + BLK) & ((col >= BLK) | (n > 0))


def _head_probs(qm, kw, valid, sink):
    sc = jnp.where(valid, _dot_nt(qm, kw) * (HEAD ** -0.5), NEG_INF)
    mx = jnp.maximum(jnp.max(sc, axis=-1, keepdims=True), sink)
    ex = jnp.exp(sc - mx)
    den = jnp.sum(ex, axis=-1, keepdims=True) + jnp.exp(sink - mx)
    return ex / den, mx, den


def _conv_fwd(zb_ref, zbp_ref, cw_ref, ext_ref, n):
    u = zb_ref[:, 2048:3072] * zb_ref[:, 3072:4096]
    pu = zbp_ref[:, 2048:3072] * zbp_ref[:, 3072:4096]
    ext_ref[0:SUBLANES, :] = jnp.where(n > 0, pu, 0.0)
    ext_ref[SUBLANES:SUBLANES + BLK, :] = u
    um1 = ext_ref[SUBLANES - 1:SUBLANES - 1 + BLK, :]
    um2 = ext_ref[SUBLANES - 2:SUBLANES - 2 + BLK, :]
    cv = cw_ref[0:1, :] * um2 + cw_ref[1:2, :] * um1 + cw_ref[2:3, :] * u
    return u, um1, um2, cv


def _prev_rows(n):
    return (jnp.maximum(n * (BLK // SUBLANES) - 1, 0), 0)


def _attn_fwd(qn, k2, v2, z_b, conv_wp, sinks):
    s = qn.shape[0]
    nb = s // BLK

    def body(sink_ref, q_ref, kc_ref, kp_ref, vc_ref, vp_ref, zb_ref, zbp_ref, cw_ref, a_ref, mix_ref, mixt_ref,
             ext_ref):
        n = pl.program_id(0)
        valid = _window_mask(n)
        lo = _low_half((BLK, LANES))
        for r in range(ATTN_W // LANES):
            kvh = r // 2
            cols = slice(LANES * kvh, LANES * (kvh + 1))
            qp = q_ref[:, LANES * r:LANES * (r + 1)]
            kw = jnp.concatenate([kp_ref[:, cols], kc_ref[:, cols]], axis=0)
            vw = jnp.concatenate([vp_ref[:, cols], vc_ref[:, cols]], axis=0)
            outs = []
            for e in range(2):
                qm = jnp.where(lo if e == 0 else jnp.logical_not(lo), qp, jnp.zeros_like(qp))
                p, _, _ = _head_probs(qm, kw, valid, sink_ref[0, 2 * r + e])
                outs.append(_dot(p.astype(BF16), vw))
            a = jnp.where(lo, outs[0], outs[1])
            a_ref[:, LANES * r:LANES * (r + 1)] = a
            g = zb_ref[:, LANES * r:LANES * (r + 1)]
            mix_ref[:, LANES * r:LANES * (r + 1)] = (a * (g * _sig(g))).astype(BF16)
        _, _, _, cv = _conv_fwd(zb_ref, zbp_ref, cw_ref, ext_ref, n)
        gc = zb_ref[:, 4096:5120]
        mix_ref[:, ATTN_W:D_MODEL] = (zb_ref[:, 1024:2048] * cv * (gc * _sig(gc))).astype(BF16)
        mixt_ref[...] = mix_ref[...].T

    cur = lambda w: pl.BlockSpec((BLK, w), lambda n: (n, 0))
    prev = lambda w: pl.BlockSpec((BLK, w), lambda n: (jnp.maximum(n - 1, 0), 0))
    return pl.pallas_call(
        body, name="attn_fwd",
        out_shape=(jax.ShapeDtypeStruct((s, ATTN_W), F32), jax.ShapeDtypeStruct((s, D_MODEL), BF16),
                   jax.ShapeDtypeStruct((D_MODEL, s), BF16)),
        grid=(nb,),
        in_specs=[pl.BlockSpec(memory_space=pltpu.SMEM),
                  cur(ATTN_W), cur(K2_W), prev(K2_W), cur(K2_W), prev(K2_W), cur(REST_W),
                  pl.BlockSpec((SUBLANES, REST_W), _prev_rows),
                  pl.BlockSpec((SUBLANES, ATTN_W), lambda n: (0, 0))],
        out_specs=(cur(ATTN_W), cur(D_MODEL), pl.BlockSpec((D_MODEL, BLK), lambda n: (0, n))),
        scratch_shapes=[pltpu.VMEM((BLK + 2 * SUBLANES, ATTN_W), F32)],
        compiler_params=_params("parallel"))(sinks, qn, k2, k2, v2, v2, z_b, z_b, conv_wp)


def _fwd_out(mix, w_out, x, g2, tm):
    s = x.shape[0]

    def body(m_ref, w_ref, x_ref, g_ref, x1_ref, h_ref, ht_ref):
        x1 = x_ref[...] + _dot(m_ref[...], w_ref[...])
        x1_ref[...] = x1
        xn, _ = _rms(x1)
        h = (xn * g_ref[...]).astype(BF16)
        h_ref[...] = h
        ht_ref[...] = h.T

    row = pl.BlockSpec((tm, D_MODEL), lambda i: (i, 0))
    return pl.pallas_call(
        body, name="fwd_out",
        out_shape=(jax.ShapeDtypeStruct((s, D_MODEL), F32), jax.ShapeDtypeStruct((s, D_MODEL), BF16),
                   jax.ShapeDtypeStruct((D_MODEL, s), BF16)),
        grid=(s // tm,),
        in_specs=[row, _resident((D_MODEL, D_MODEL)), row, pl.BlockSpec((1, D_MODEL), lambda i: (0, 0))],
        out_specs=(row, row, pl.BlockSpec((D_MODEL, tm), lambda i: (0, i))),
        compiler_params=_params("parallel"))(mix, w_out, x, g2)


def _ple(hn2, w_pg, b_pg, p, w_pp, g3, x1, target, tm):
    s = x1.shape[0]

    def body(h_ref, wg_ref, b_ref, p_ref, wp_ref, g3_ref, x1_ref, t_ref, dy_ref, dgp_ref, dt_ref, pt_ref, acc_ref):
        gate = _sig(_dot(h_ref[...], wg_ref[...]) + b_ref[...])
        pb = p_ref[...].astype(BF16)
        pt_ref[...] = pb.T
        t = _dot(pb, wp_ref[...])
        tn, r3 = _rms(t)
        e = tn * g3_ref[...]
        diff = x1_ref[...] + gate * e - t_ref[...]
        dy = diff * (1.0 / D_MODEL)
        dy_ref[...] = dy
        dgp = dy * e * (gate * (1.0 - gate))
        dgp_ref[...] = dgp.astype(BF16)
        de = dy * gate
        dt_ref[...] = _rms_bwd(de * g3_ref[...], tn, r3).astype(BF16)

        @pl.when(pl.program_id(0) == 0)
        def _():
            acc_ref[...] = jnp.zeros_like(acc_ref)

        acc_ref[0:1, :] += jnp.sum(dgp, axis=0, keepdims=True)
        acc_ref[1:2, :] += jnp.sum(de * tn, axis=0, keepdims=True)
        acc_ref[2:3, :] += jnp.sum(diff * diff, axis=0, keepdims=True) * (0.5 / D_MODEL)

    row = pl.BlockSpec((tm, D_MODEL), lambda i: (i, 0))
    vec = pl.BlockSpec((1, D_MODEL), lambda i: (0, 0))
    return pl.pallas_call(
        body, name="ple",
        out_shape=(jax.ShapeDtypeStruct((s, D_MODEL), F32), jax.ShapeDtypeStruct((s, D_MODEL), BF16),
                   jax.ShapeDtypeStruct((s, D_MODEL), BF16), jax.ShapeDtypeStruct((PLE_DIM, s), BF16),
                   jax.ShapeDtypeStruct((SUBLANES, D_MODEL), F32)),
        grid=(s // tm,),
        in_specs=[row, _resident((D_MODEL, D_MODEL)), vec, pl.BlockSpec((tm, PLE_DIM), lambda i: (i, 0)),
                  _resident((PLE_DIM, D_MODEL)), vec, row, row],
        out_specs=(row, row, row, pl.BlockSpec((PLE_DIM, tm), lambda i: (0, i)),
                   pl.BlockSpec((SUBLANES, D_MODEL), lambda i: (0, 0))),
        compiler_params=_params("arbitrary"))(hn2, w_pg, b_pg, p, w_pp, g3, x1, target)


def _gate_bwd(dgp, w_pg, x1, dy, g2, tm):
    s = x1.shape[0]

    def body(d_ref, w_ref, x1_ref, dy_ref, g_ref, dx_ref, dxb_ref, acc_ref):
        dh = _dot_nt(d_ref[...], w_ref[...])
        xn, r = _rms(x1_ref[...])
        dx1 = dy_ref[...] + _rms_bwd(dh * g_ref[...], xn, r)
        dx_ref[...] = dx1
        dxb_ref[...] = dx1.astype(BF16)

        @pl.when(pl.program_id(0) == 0)
        def _():
            acc_ref[...] = jnp.zeros_like(acc_ref)

        acc_ref[0:1, :] += jnp.sum(dh * xn, axis=0, keepdims=True)

    row = pl.BlockSpec((tm, D_MODEL), lambda i: (i, 0))
    return pl.pallas_call(
        body, name="gate_bwd",
        out_shape=(jax.ShapeDtypeStruct((s, D_MODEL), F32), jax.ShapeDtypeStruct((s, D_MODEL), BF16),
                   jax.ShapeDtypeStruct((SUBLANES, D_MODEL), F32)),
        grid=(s // tm,),
        in_specs=[row, _resident((D_MODEL, D_MODEL)), row, row, pl.BlockSpec((1, D_MODEL), lambda i: (0, 0))],
        out_specs=(row, row, pl.BlockSpec((SUBLANES, D_MODEL), lambda i: (0, 0))),
        compiler_params=_params("arbitrary"))(dgp, w_pg, x1, dy, g2)


def _mm_nt(a, b, tm, name):
    m, k = a.shape
    n = b.shape[0]

    def body(a_ref, b_ref, o_ref):
        o_ref[...] = _dot_nt(a_ref[...], b_ref[...])

    return pl.pallas_call(
        body, name=name,
        out_shape=jax.ShapeDtypeStruct((m, n), F32),
        grid=(m // tm,),
        in_specs=[pl.BlockSpec((tm, k), lambda i: (i, 0)), _resident((n, k))],
        out_specs=pl.BlockSpec((tm, n), lambda i: (i, 0)),
        compiler_params=_params("parallel"))(a, b)


def _attn_bwd(qn, k2, v2, a, z_b, dmix, conv_wp, sinks):
    s = qn.shape[0]
    nb = s // BLK

    def body(sink_ref, q_ref, kc_ref, kp_ref, vc_ref, vp_ref, a_ref, zb_ref, zbp_ref, zbn_ref, dm_ref, dmn_ref,
             cw_ref, dq_ref, dkc_ref, dkp_ref, dvc_ref, dvp_ref, dzb_ref, acc_ref, ext_ref):
        n = pl.program_id(0)
        valid = _window_mask(n)
        lo = _low_half((BLK, LANES))
        lane = lax.broadcasted_iota(jnp.int32, (1, ATTN_W), 1)

        @pl.when(n == 0)
        def _():
            acc_ref[...] = jnp.zeros_like(acc_ref)

        dsink = jnp.zeros((1, ATTN_W), F32)
        for kvh in range(K2_W // LANES):
            cols = slice(LANES * kvh, LANES * (kvh + 1))
            kw = jnp.concatenate([kp_ref[:, cols], kc_ref[:, cols]], axis=0)
            vw = jnp.concatenate([vp_ref[:, cols], vc_ref[:, cols]], axis=0)
            dk2 = jnp.zeros((2 * BLK, LANES), F32)
            dv2 = jnp.zeros((2 * BLK, LANES), F32)
            for r in (2 * kvh, 2 * kvh + 1):
                rc = slice(LANES * r, LANES * (r + 1))
                g = zb_ref[:, rc]
                sg = _sig(g)
                dm = dm_ref[:, rc]
                av = a_ref[:, rc]
                da = dm * (g * sg)
                dzb_ref[:, rc] = (dm * av * _dsilu(g, sg)).astype(BF16)
                qp = q_ref[:, rc]
                dqs = []
                for e in range(2):
                    half = lo if e == 0 else jnp.logical_not(lo)
                    sink = sink_ref[0, 2 * r + e]
                    qm = jnp.where(half, qp, jnp.zeros_like(qp))
                    p, mx, den = _head_probs(qm, kw, valid, sink)
                    do = jnp.where(half, da, 0.0)
                    delta = jnp.sum(do * av, axis=-1, keepdims=True)
                    dob = do.astype(BF16)
                    ds = p * (_dot_nt(dob, vw) - delta) * (HEAD ** -0.5)
                    dsb = ds.astype(BF16)
                    dqs.append(_dot(dsb, kw))
                    dk2 = dk2 + _dot(ds.T.astype(BF16), qm)
                    dv2 = dv2 + _dot(p.T.astype(BF16), dob)
                    dsk = -jnp.sum(jnp.exp(sink - mx) / den * delta, axis=0, keepdims=True)
                    dsink = dsink + jnp.where(lane == 2 * r + e, dsk, 0.0)
                dq_ref[:, rc] = jnp.where(lo, dqs[0], dqs[1])
            dkp_ref[:, cols] = dk2[0:BLK]
            dkc_ref[:, cols] = dk2[BLK:2 * BLK]
            dvp_ref[:, cols] = dv2[0:BLK]
            dvc_ref[:, cols] = dv2[BLK:2 * BLK]
        acc_ref[0:1, :] += dsink

        u, um1, um2, cv = _conv_fwd(zb_ref, zbp_ref, cw_ref, ext_ref, n)
        cb = zb_ref[:, 1024:2048]
        gc = zb_ref[:, 4096:5120]
        sgc = _sig(gc)
        dmc = dm_ref[:, ATTN_W:D_MODEL]
        t = dmc * (gc * sgc)
        dcv = t * cb
        dzb_ref[:, 1024:2048] = (t * cv).astype(BF16)
        dzb_ref[:, 4096:5120] = (dmc * cb * cv * _dsilu(gc, sgc)).astype(BF16)
        gcn = zbn_ref[:, 4096:5120]
        dcvn = dmn_ref[:, ATTN_W:D_MODEL] * (gcn * _sig(gcn)) * zbn_ref[:, 1024:2048]
        ext_ref[0:BLK, :] = dcv
        ext_ref[BLK:BLK + SUBLANES, :] = jnp.where(n < nb - 1, dcvn, 0.0)
        du = (cw_ref[2:3, :] * dcv + cw_ref[1:2, :] * ext_ref[1:1 + BLK, :]
              + cw_ref[0:1, :] * ext_ref[2:2 + BLK, :])
        dzb_ref[:, 2048:3072] = (du * zb_ref[:, 3072:4096]).astype(BF16)
        dzb_ref[:, 3072:4096] = (du * zb_ref[:, 2048:3072]).astype(BF16)
        acc_ref[1:2, :] += jnp.sum(dcv * um2, axis=0, keepdims=True)
        acc_ref[2:3, :] += jnp.sum(dcv * um1, axis=0, keepdims=True)
        acc_ref[3:4, :] += jnp.sum(dcv * u, axis=0, keepdims=True)

    cur = lambda w: pl.BlockSpec((BLK, w), lambda n: (n, 0))
    prev = lambda w: pl.BlockSpec((BLK, w), lambda n: (jnp.maximum(n - 1, 0), 0))
    nxt = lambda w: pl.BlockSpec(
        (SUBLANES, w), lambda n: (jnp.minimum((n + 1) * (BLK // SUBLANES), nb * (BLK // SUBLANES) - 1), 0))
    f32 = lambda w: jax.ShapeDtypeStruct((s, w), F32)
    return pl.pallas_call(
        body, name="attn_bwd",
        out_shape=(f32(ATTN_W), f32(K2_W), f32(K2_W), f32(K2_W), f32(K2_W),
                   jax.ShapeDtypeStruct((s, REST_W), BF16), jax.ShapeDtypeStruct((SUBLANES, ATTN_W), F32)),
        grid=(nb,),
        in_specs=[pl.BlockSpec(memory_space=pltpu.SMEM),
                  cur(ATTN_W), cur(K2_W), prev(K2_W), cur(K2_W), prev(K2_W), cur(ATTN_W), cur(REST_W),
                  pl.BlockSpec((SUBLANES, REST_W), _prev_rows), nxt(REST_W), cur(D_MODEL), nxt(D_MODEL),
                  pl.BlockSpec((SUBLANES, ATTN_W), lambda n: (0, 0))],
        out_specs=(cur(ATTN_W), cur(K2_W), cur(K2_W), cur(K2_W), cur(K2_W), cur(REST_W),
                   pl.BlockSpec((SUBLANES, ATTN_W), lambda n: (0, 0))),
        scratch_shapes=[pltpu.VMEM((BLK + 2 * SUBLANES, ATTN_W), F32)],
        compiler_params=_params("arbitrary"))(sinks, qn, k2, k2, v2, v2, a, z_b, z_b, z_b, dmix, dmix, conv_wp)


def _qkv_bwd(z_a, dq, dkc, dkp, dvc, dvp, ra, rbm, rbp, gq2, gk2):
    s = z_a.shape[0]
    nb = s // BLK

    def body(z_ref, dq_ref, dkc_ref, dkp_ref, dvc_ref, dvp_ref, a_ref, bm_ref, bp_ref, gq_ref, gk_ref,
             dz_ref, acc_ref):
        n = pl.program_id(0)
        a, bm, bp = a_ref[...], bm_ref[...], bp_ref[...]
        lo = _low_half((BLK, LANES))
        last = n == nb - 1

        @pl.when(n == 0)
        def _():
            acc_ref[...] = jnp.zeros_like(acc_ref)

        def norm_bwd(x, dy, gain):
            rr = lax.rsqrt(_half_sums(x * x) * (1.0 / HEAD) + EPS)
            xh = x * rr
            dxg = _rope_t(dy, a, bm, bp)
            dxh = dxg * gain
            dx = rr * (dxh - xh * (_half_sums(dxh * xh) * (1.0 / HEAD)))
            return dx, jnp.sum(dxg * xh, axis=0, keepdims=True)

        def folded(cur_ref, prev_ref, m):
            parts = []
            for h in (2 * m, 2 * m + 1):
                v = cur_ref[:, LANES * h:LANES * (h + 1)] + jnp.where(
                    last, 0.0, prev_ref[:, LANES * h:LANES * (h + 1)])
                parts.append(v + pltpu.roll(v, HEAD, 1))
            return jnp.where(lo, parts[0], parts[1])

        gq_acc = jnp.zeros((1, LANES), F32)
        for r in range(ATTN_W // LANES):
            rc = slice(LANES * r, LANES * (r + 1))
            dx, gg = norm_bwd(z_ref[:, rc], dq_ref[:, rc], gq_ref[...])
            dz_ref[:, rc] = dx.astype(BF16)
            gq_acc = gq_acc + gg
        acc_ref[0:1, :] += gq_acc
        gk_acc = jnp.zeros((1, LANES), F32)
        for m in range(KV_W // LANES):
            kc = slice(ATTN_W + LANES * m, ATTN_W + LANES * (m + 1))
            dx, gg = norm_bwd(z_ref[:, kc], folded(dkc_ref, dkp_ref, m), gk_ref[...])
            dz_ref[:, kc] = dx.astype(BF16)
            gk_acc = gk_acc + gg
            vc = slice(ATTN_W + KV_W + LANES * m, ATTN_W + KV_W + LANES * (m + 1))
            dz_ref[:, vc] = folded(dvc_ref, dvp_ref, m).astype(BF16)
        acc_ref[1:2, :] += gk_acc

    cur = lambda w: pl.BlockSpec((BLK, w), lambda n: (n, 0))
    nxt = lambda w: pl.BlockSpec((BLK, w), lambda n: (jnp.minimum(n + 1, nb - 1), 0))
    one = pl.BlockSpec((1, LANES), lambda n: (0, 0))
    return pl.pallas_call(
        body, name="qkv_bwd",
        out_shape=(jax.ShapeDtypeStruct((s, QKV_W), BF16), jax.ShapeDtypeStruct((SUBLANES, LANES), F32)),
        grid=(nb,),
        in_specs=[cur(QKV_W), cur(ATTN_W), cur(K2_W), nxt(K2_W), cur(K2_W), nxt(K2_W),
                  cur(LANES), cur(LANES), cur(LANES), one, one],
        out_specs=(cur(QKV_W), pl.BlockSpec((SUBLANES, LANES), lambda n: (0, 0))),
        compiler_params=_params("arbitrary"))(z_a, dq, dkc, dkp, dvc, dvp, ra, rbm, rbp, gq2, gk2)


REST_CHUNK = 1280
N_REST_CHUNKS = REST_W // REST_CHUNK


def _in_bwd(dz_a, dz_b, w_qkv, w_rest, x, dx1, g1, tm):
    s = x.shape[0]
    nk = 1 + N_REST_CHUNKS

    def body(da_ref, db_ref, wa_ref, wb_ref, x_hbm, dx1_hbm, g_ref, gx_ref, acc_ref, x_buf, dx1_buf, sems):
        i, k = pl.program_id(0), pl.program_id(1)
        rows = pl.ds(pl.multiple_of(i * tm, tm), tm)
        fetch = [pltpu.make_async_copy(x_hbm.at[rows], x_buf, sems.at[0]),
                 pltpu.make_async_copy(dx1_hbm.at[rows], dx1_buf, sems.at[1])]

        sub = min(SUB_ROWS, tm)
        blocks = [slice(r, r + sub) for r in range(0, tm, sub)]

        @pl.when(k == 0)
        def _():
            for cp in fetch:
                cp.start()
            gx_ref[...] = _dot_nt(da_ref[...], wa_ref[...])

        @pl.when(k > 0)
        def _():
            gx_ref[...] += _dot_nt(db_ref[...], wb_ref[...])

        @pl.when((i == 0) & (k == 0))
        def _():
            acc_ref[...] = jnp.zeros_like(acc_ref)

        @pl.when(k == nk - 1)
        def _():
            for cp in fetch:
                cp.wait()
            for rb in blocks:
                dh = gx_ref[rb, :]
                xn, r = _rms(x_buf[rb, :])
                gx_ref[rb, :] = dx1_buf[rb, :] + _rms_bwd(dh * g_ref[...], xn, r)
                acc_ref[0:1, :] += jnp.sum(dh * xn, axis=0, keepdims=True)

    kb = lambda i, k: jnp.maximum(k - 1, 0)
    return pl.pallas_call(
        body, name="in_bwd",
        out_shape=(jax.ShapeDtypeStruct((s, D_MODEL), F32), jax.ShapeDtypeStruct((SUBLANES, D_MODEL), F32)),
        grid=(s // tm, nk),
        in_specs=[pl.BlockSpec((tm, QKV_W), lambda i, k: (i, 0)),
                  pl.BlockSpec((tm, REST_CHUNK), lambda i, k: (i, kb(i, k))),
                  _resident((D_MODEL, QKV_W)),
                  pl.BlockSpec((D_MODEL, REST_CHUNK), lambda i, k: (0, kb(i, k))),
                  ANY, ANY, pl.BlockSpec((1, D_MODEL), lambda i, k: (0, 0))],
        out_specs=(pl.BlockSpec((tm, D_MODEL), lambda i, k: (i, 0)),
                   pl.BlockSpec((SUBLANES, D_MODEL), lambda i, k: (0, 0))),
        scratch_shapes=[pltpu.VMEM((tm, D_MODEL), F32), pltpu.VMEM((tm, D_MODEL), F32),
                        pltpu.SemaphoreType.DMA((2,))],
        compiler_params=_params("arbitrary", "arbitrary"))(dz_a, dz_b, w_qkv, w_rest, x, dx1, g1)


def _mm_grad(at, bs, tn, name):
    m, kdim = at.shape
    nblk = [b.shape[1] // tn for b in bs]
    starts = [sum(nblk[:t]) for t in range(len(bs))]

    def body(a_ref, *refs):
        b_refs, o_ref = refs[:len(bs)], refs[len(bs)]
        j = pl.program_id(0)
        for t, b_ref in enumerate(b_refs):
            @pl.when((j >= starts[t]) & (j < starts[t] + nblk[t]))
            def _():
                o_ref[...] = _dot(a_ref[...], b_ref[...]).astype(BF16)

    def b_spec(t):
        return pl.BlockSpec((kdim, tn), lambda j: (0, jnp.clip(j - starts[t], 0, nblk[t] - 1)))

    return pl.pallas_call(
        body, name=name,
        out_shape=jax.ShapeDtypeStruct((m, sum(nblk) * tn), BF16),
        grid=(sum(nblk),),
        in_specs=[_resident((m, kdim))] + [b_spec(t) for t in range(len(bs))],
        out_specs=pl.BlockSpec((m, tn), lambda j: (0, j)),
        compiler_params=_params("parallel"))(at, *bs)


def _place():
    return lax.axis_index("x"), lax.axis_index("y"), lax.axis_index("c")


def _all_gather(shards):
    na = len(shards)

    def body(*refs):
        ins, outs = refs[:na], refs[na:2 * na]
        send_sems, recv_sems, local_sems = refs[2 * na:]
        x, y, c = _place()
        me, sibling = (x, y, c), (x, y, 1 - c)
        chips = [(1 - x, y), (x, 1 - y), (1 - x, 1 - y)]

        def copy(t, k, block, to, src=None):
            dst = outs[t].at[4 * block[0] + 2 * block[1] + block[2]]
            return pltpu.make_async_remote_copy(
                src_ref=dst if src is None else src, dst_ref=dst, send_sem=send_sems.at[t, k],
                recv_sem=recv_sems.at[t, k], device_id=to, device_id_type=MESH)

        mine = [pltpu.make_async_copy(ins[t], outs[t].at[4 * x + 2 * y + c], local_sems.at[t]) for t in range(na)]
        for cp in mine:
            cp.start()
        first = []
        for j, chip in enumerate(chips):
            first += [copy(t, 1 + j, me, (*chip, c), src=ins[t]) for t in range(na)]
        first += [copy(t, 0, me, sibling, src=ins[t]) for t in range(na)]
        for cp in first:
            cp.start()
        passed = []
        for j, chip in enumerate(chips):
            for t in range(na):
                copy(t, 1 + j, (*chip, c), me).wait_recv()
                passed.append(copy(t, 4 + j, (*chip, c), sibling))
                passed[-1].start()
        for t in range(na):
            copy(t, 0, sibling, me).wait_recv()
        for j, chip in enumerate(chips):
            for t in range(na):
                copy(t, 4 + j, (*chip, 1 - c), me).wait_recv()
        for cp in first + passed:
            cp.wait_send()
        for cp in mine:
            cp.wait()

    return pl.pallas_call(
        body, name="all_gather_weights",
        out_shape=tuple(jax.ShapeDtypeStruct((N_DEV,) + a.shape, a.dtype) for a in shards),
        in_specs=[ANY] * na, out_specs=tuple([ANY] * na),
        scratch_shapes=[pltpu.SemaphoreType.DMA((na, 7)), pltpu.SemaphoreType.DMA((na, 7)),
                        pltpu.SemaphoreType.DMA((na,))])(*shards)


def _all_reduce_slab(slab, name):
    def body(in_ref, out_ref, gath_ref, send_sems, recv_sems):
        x, y, c = _place()
        me = 4 * x + 2 * y + c
        gath_ref[me] = in_ref[...]
        copies = []
        for k in range(1, N_DEV):
            peer = (x ^ (k >> 2), y ^ ((k >> 1) & 1), c ^ (k & 1))
            copies.append(pltpu.make_async_remote_copy(
                src_ref=in_ref, dst_ref=gath_ref.at[me], send_sem=send_sems.at[k - 1],
                recv_sem=recv_sems.at[k - 1], device_id=peer, device_id_type=MESH))
        for cp in copies:
            cp.start()
        for cp in copies:
            cp.wait_recv()
        for cp in copies:
            cp.wait_send()
        total = gath_ref[0]
        for d in range(1, N_DEV):
            total = total + gath_ref[d]
        out_ref[...] = total

    vmem = pl.BlockSpec(memory_space=pltpu.VMEM)
    return pl.pallas_call(
        body, name=name,
        out_shape=jax.ShapeDtypeStruct(slab.shape, F32),
        in_specs=[vmem], out_specs=vmem,
        scratch_shapes=[pltpu.VMEM((N_DEV,) + slab.shape, F32),
                        pltpu.SemaphoreType.DMA((N_DEV - 1,)), pltpu.SemaphoreType.DMA((N_DEV - 1,))])(slab)


def _pair_exchange(grads):
    na = len(grads)

    def body(*refs):
        ins, outs = refs[:na], refs[na:2 * na]
        send_sems, recv_sems = refs[2 * na:]
        x, y, c = _place()
        copies = [pltpu.make_async_remote_copy(
            src_ref=ins[t].at[:, 1 - c], dst_ref=outs[t], send_sem=send_sems.at[t], recv_sem=recv_sems.at[t],
            device_id=(x, y, 1 - c), device_id_type=MESH) for t in range(na)]
        for cp in copies:
            cp.start()
        for cp in copies:
            cp.wait()

    return pl.pallas_call(
        body, name="pair_exchange",
        out_shape=tuple(jax.ShapeDtypeStruct((4,) + g.shape[2:], g.dtype) for g in grads),
        in_specs=[ANY] * na, out_specs=tuple([ANY] * na),
        scratch_shapes=[pltpu.SemaphoreType.DMA((na,)), pltpu.SemaphoreType.DMA((na,))])(*grads)


def _pair_sum(g, r, place, tr, name):
    _, _, rows, cols = g.shape

    def body(place_ref, g_ref, r_ref, pb_ref, own_ref):
        tot = g_ref[0, 0].astype(F32) + r_ref[0].astype(F32)
        pb_ref[0] = tot.astype(BF16)

        @pl.when(pl.program_id(1) == place_ref[1])
        def _():
            own_ref[...] = tot

    grid_spec = pltpu.PrefetchScalarGridSpec(
        num_scalar_prefetch=1, grid=(rows // tr, 4),
        in_specs=[pl.BlockSpec((1, 1, tr, cols), lambda i, q, place_ref: (q, place_ref[0], i, 0)),
                  pl.BlockSpec((1, tr, cols), lambda i, q, place_ref: (q, i, 0))],
        out_specs=(pl.BlockSpec((1, tr, cols), lambda i, q, place_ref: (q, i, 0)),
                   pl.BlockSpec((tr, cols), lambda i, q, place_ref: (i, 0))))
    return pl.pallas_call(
        body, name=name, grid_spec=grid_spec,
        out_shape=(jax.ShapeDtypeStruct((4, rows, cols), BF16), jax.ShapeDtypeStruct((rows, cols), F32)),
        compiler_params=_params("arbitrary", "arbitrary"))(place, g, r)


def _chip_exchange(sums):
    na = len(sums)

    def body(*refs):
        ins, outs = refs[:na], refs[na:2 * na]
        send_sems, recv_sems = refs[2 * na:]
        x, y, c = _place()
        copies = []
        for k in (1, 2, 3):
            px, py = x ^ (k >> 1), y ^ (k & 1)
            copies += [pltpu.make_async_remote_copy(
                src_ref=ins[t].at[2 * px + py], dst_ref=outs[t].at[k - 1], send_sem=send_sems.at[t, k - 1],
                recv_sem=recv_sems.at[t, k - 1], device_id=(px, py, c), device_id_type=MESH) for t in range(na)]
        for cp in copies:
            cp.start()
        for cp in copies:
            cp.wait()

    return pl.pallas_call(
        body, name="chip_exchange",
        out_shape=tuple(jax.ShapeDtypeStruct((3,) + g.shape[1:], g.dtype) for g in sums),
        in_specs=[ANY] * na, out_specs=tuple([ANY] * na),
        scratch_shapes=[pltpu.SemaphoreType.DMA((na, 3)), pltpu.SemaphoreType.DMA((na, 3))])(*sums)


HBM = pl.BlockSpec(memory_space=pltpu.HBM)
SEM = pl.BlockSpec(memory_space=pltpu.SEMAPHORE)
SIDE_EFFECT = pltpu.CompilerParams(has_side_effects=pltpu.SideEffectType.DATAFLOW_SIDE_EFFECTING)
TOKEN = jax.ShapeDtypeStruct((SUBLANES, LANES), F32)


def _hbm(a):
    return pltpu.with_memory_space_constraint(a, pltpu.HBM)


def _hbm_like(arrays):
    return tuple(pltpu.HBM(a.shape, a.dtype) for a in arrays)


def _block_of(px, py, pc):
    return 4 * px + 2 * py + pc


def _gather_start(shards, after):
    na = len(shards)
    lands = [_hbm(lax.empty((N_DEV,) + a.shape, a.dtype)) for a in shards]

    def body(*refs):
        ins, land = refs[:na], refs[na:2 * na]
        send_sems, recv_ici, recv_d2d = refs[2 * na + 1:2 * na + 4]
        token, local_sems = refs[-2], refs[-1]
        x, y, c = _place()
        mine = [pltpu.make_async_copy(ins[t], land[t].at[_block_of(x, y, c)], local_sems.at[t]) for t in range(na)]
        for cp in mine:
            cp.start()
        for k, peer in enumerate([(x, y, 1 - c), (1 - x, y, c), (x, 1 - y, c), (1 - x, 1 - y, c)]):
            for t in range(na):
                pltpu.make_async_remote_copy(
                    src_ref=ins[t], dst_ref=land[t].at[_block_of(x, y, c)], send_sem=send_sems.at[4 * t + k],
                    recv_sem=recv_d2d.at[4 * t] if k == 0 else recv_ici.at[3 * t + k - 1],
                    device_id=peer, device_id_type=MESH).start()
        for cp in mine:
            cp.wait()
        token[...] = jnp.zeros_like(token)

    out = pl.pallas_call(
        body, name="gather_start",
        out_shape=(pltpu.SemaphoreType.DMA((4 * na,)), pltpu.SemaphoreType.DMA((3 * na,)),
                   pltpu.SemaphoreType.DMA((4 * na,)), *_hbm_like(shards), *_hbm_like(lands), TOKEN),
        in_specs=[HBM] * (2 * na) + [ANY],
        out_specs=(SEM, SEM, SEM, *[HBM] * (2 * na), pl.BlockSpec(memory_space=pltpu.VMEM)),
        input_output_aliases={i: 3 + i for i in range(2 * na)},
        scratch_shapes=[pltpu.SemaphoreType.DMA((na,))],
        compiler_params=SIDE_EFFECT)(*[_hbm(a) for a in shards], *lands, after)
    send_sems, recv_ici, recv_d2d = out[:3]
    state = dict(send=send_sems, ici=recv_ici, d2d=recv_d2d, shards=out[3:3 + na], lands=out[3 + na:3 + 2 * na])
    return state, out[-1]


def _gather_forward(state, after):
    lands = state["lands"]
    na = len(lands)

    def body(*refs):
        land = refs[:na]
        recv_ici, recv_d2d = refs[na], refs[na + 1]
        fwd_sems, token = refs[-2], refs[-1]
        x, y, c = _place()
        for j, chip in enumerate([(1 - x, y), (x, 1 - y), (1 - x, 1 - y)]):
            for t in range(na):
                blk = land[t].at[_block_of(*chip, c)]
                pltpu.make_async_remote_copy(
                    src_ref=blk, dst_ref=blk, send_sem=fwd_sems.at[3 * t + j], recv_sem=recv_ici.at[3 * t + j],
                    device_id=(x, y, c), device_id_type=MESH).wait_recv()
                pltpu.make_async_remote_copy(
                    src_ref=blk, dst_ref=blk, send_sem=fwd_sems.at[3 * t + j], recv_sem=recv_d2d.at[4 * t + 1 + j],
                    device_id=(x, y, 1 - c), device_id_type=MESH).start()
        token[...] = jnp.zeros_like(token)

    out = pl.pallas_call(
        body, name="gather_forward",
        out_shape=(*_hbm_like(lands), pltpu.SemaphoreType.DMA((3 * na,)), TOKEN),
        in_specs=[HBM] * na + [SEM, SEM, ANY],
        out_specs=(*[HBM] * na, SEM, pl.BlockSpec(memory_space=pltpu.VMEM)),
        input_output_aliases={i: i for i in range(na)},
        compiler_params=SIDE_EFFECT)(*lands, state["ici"], state["d2d"], after)
    return dict(state, lands=out[:na], fwd=out[na]), out[-1]


def _gather_wait(state, after):
    shards, lands = state["shards"], state["lands"]
    na = len(lands)

    def body(*refs):
        ins, land = refs[:na], refs[na:2 * na]
        send_sems, fwd_sems, recv_d2d = refs[2 * na:2 * na + 3]
        x, y, c = _place()
        chips = [(1 - x, y), (x, 1 - y), (1 - x, 1 - y)]
        for t in range(na):
            mine = land[t].at[_block_of(x, y, c)]
            for k in range(4):
                pltpu.make_async_remote_copy(
                    src_ref=ins[t], dst_ref=mine, send_sem=send_sems.at[4 * t + k], recv_sem=recv_d2d.at[4 * t],
                    device_id=(x, y, c), device_id_type=MESH).wait_send()
            for j, chip in enumerate(chips):
                blk = land[t].at[_block_of(*chip, c)]
                pltpu.make_async_remote_copy(
                    src_ref=blk, dst_ref=blk, send_sem=fwd_sems.at[3 * t + j], recv_sem=recv_d2d.at[4 * t + 1 + j],
                    device_id=(x, y, c), device_id_type=MESH).wait_send()
            for k, blk_id in enumerate([_block_of(x, y, 1 - c)] + [_block_of(*chip, 1 - c) for chip in chips]):
                blk = land[t].at[blk_id]
                pltpu.make_async_remote_copy(
                    src_ref=blk, dst_ref=blk, send_sem=send_sems.at[4 * t], recv_sem=recv_d2d.at[4 * t + k],
                    device_id=(x, y, c), device_id_type=MESH).wait_recv()

    out = pl.pallas_call(
        body, name="gather_wait",
        out_shape=(*_hbm_like(shards), *_hbm_like(lands)),
        in_specs=[HBM] * (2 * na) + [SEM, SEM, SEM, ANY],
        out_specs=tuple([HBM] * (2 * na)),
        input_output_aliases={i: i for i in range(2 * na)},
        compiler_params=SIDE_EFFECT)(*shards, *lands, state["send"], state["fwd"], state["d2d"], after)
    return out[na:]


def _to_sibling(srcs, lands, send_sems, recv_sems):
    x, y, c = _place()
    return [pltpu.make_async_remote_copy(
        src_ref=srcs[t].at[:, 1 - c], dst_ref=lands[t], send_sem=send_sems.at[t], recv_sem=recv_sems.at[t],
        device_id=(x, y, 1 - c), device_id_type=MESH) for t in range(len(srcs))]


def _to_chips(srcs, lands, send_sems, recv_sems):
    x, y, c = _place()
    copies = []
    for k in (1, 2, 3):
        px, py = x ^ (k >> 1), y ^ (k & 1)
        copies += [pltpu.make_async_remote_copy(
            src_ref=srcs[t].at[2 * px + py], dst_ref=lands[t].at[k - 1], send_sem=send_sems.at[3 * t + k - 1],
            recv_sem=recv_sems.at[3 * t + k - 1], device_id=(px, py, c), device_id_type=MESH) for t in range(len(srcs))]
    return copies


def _exchange_start(name, srcs, land_shapes, copies, per_array, after):
    na = len(srcs)
    lands = [_hbm(lax.empty(shp, a.dtype)) for shp, a in zip(land_shapes, srcs)]

    def body(*refs):
        token = refs[-1]
        for cp in copies(refs[:na], refs[na:2 * na], refs[2 * na + 1], refs[2 * na + 2]):
            cp.start()
        token[...] = jnp.zeros_like(token)

    out = pl.pallas_call(
        body, name=name,
        out_shape=(pltpu.SemaphoreType.DMA((na * per_array,)), pltpu.SemaphoreType.DMA((na * per_array,)),
                   *_hbm_like(srcs), *_hbm_like(lands), TOKEN),
        in_specs=[HBM] * (2 * na) + [ANY],
        out_specs=(SEM, SEM, *[HBM] * (2 * na), pl.BlockSpec(memory_space=pltpu.VMEM)),
        input_output_aliases={i: 2 + i for i in range(2 * na)},
        compiler_params=SIDE_EFFECT)(*[_hbm(a) for a in srcs], *lands, after)
    return dict(send=out[0], recv=out[1], srcs=out[2:2 + na], lands=out[2 + na:2 + 2 * na]), out[-1]


def _exchange_wait(name, state, copies, after):
    srcs, lands = state["srcs"], state["lands"]
    na = len(srcs)

    def body(*refs):
        for cp in copies(refs[:na], refs[na:2 * na], refs[2 * na], refs[2 * na + 1]):
            cp.wait_send()
            cp.wait_recv()

    out = pl.pallas_call(
        body, name=name,
        out_shape=(*_hbm_like(srcs), *_hbm_like(lands)),
        in_specs=[HBM] * (2 * na) + [SEM, SEM, ANY],
        out_specs=tuple([HBM] * (2 * na)),
        input_output_aliases={i: i for i in range(2 * na)},
        compiler_params=SIDE_EFFECT)(*srcs, *lands, state["send"], state["recv"], after)
    return out[na:]


def _adamw_math(w, g, m, v):
    m = ADAM_B1 * m + (1.0 - ADAM_B1) * g
    v = ADAM_B2 * v + (1.0 - ADAM_B2) * (g * g)
    m_hat = m / (1.0 - ADAM_B1 ** ADAM_STEP)
    v_hat = v / (1.0 - ADAM_B2 ** ADAM_STEP)
    return -ADAM_LR * (m_hat / (jnp.sqrt(v_hat) + ADAM_EPS) + ADAM_WD * w), m, v


def _adamw(own, others, w, m, v, tr, name):
    rows, cols = w.shape
    blk = pl.BlockSpec((tr, cols), lambda i: (i, 0))

    def body(own_ref, oth_ref, w_ref, m_ref, v_ref, g_ref, d_ref, nm_ref, nv_ref):
        g = own_ref[...]
        for k in range(3):
            g = g + oth_ref[k].astype(F32)
        g_ref[...] = g
        d_ref[...], nm_ref[...], nv_ref[...] = _adamw_math(w_ref[...], g, m_ref[...], v_ref[...])

    out = jax.ShapeDtypeStruct((rows, cols), F32)
    return pl.pallas_call(
        body, name=name, out_shape=(out, out, out, out), grid=(rows // tr,),
        in_specs=[blk, pl.BlockSpec((3, tr, cols), lambda i: (0, i, 0)), blk, blk, blk],
        out_specs=(blk, blk, blk, blk),
        compiler_params=_params("parallel"))(own, others, w, m, v)


def _adamw_slab(w, g, m, v):
    def body(w_ref, g_ref, m_ref, v_ref, d_ref, nm_ref, nv_ref):
        d_ref[...], nm_ref[...], nv_ref[...] = _adamw_math(w_ref[...], g_ref[...], m_ref[...], v_ref[...])

    out = jax.ShapeDtypeStruct(w.shape, F32)
    vmem = pl.BlockSpec(memory_space=pltpu.VMEM)
    return pl.pallas_call(body, name="adamw_small", out_shape=(out, out, out),
                          in_specs=[vmem] * 4, out_specs=(vmem, vmem, vmem))(w, g, m, v)


def _row(v, width=D_MODEL):
    v = v.reshape(1, -1)
    return jnp.pad(v, ((0, 0), (0, width - v.shape[1])))


def _tables(s, gq, gk, conv_w):
    gq2 = jnp.tile(gq.reshape(1, HEAD), (1, 2))
    gk2 = jnp.tile(gk.reshape(1, HEAD), (1, 2))
    conv_wp = jnp.pad(conv_w, ((0, SUBLANES - conv_w.shape[0]), (0, 0)))
    return _rope_tables(s), gq2, gk2, conv_wp


def _forward_in(x, g1, w_qkv, w_rest):
    s = x.shape[0]
    h, ht, z_a = _fwd_in_a(x, g1, w_qkv, min(512, s))
    z_b = _mm_nn(h, w_rest, min(512, s), 1024, "fwd_in_b")
    return ht, z_a, z_b


def _forward_attn(z_a, z_b, rope, gq2, gk2, conv_wp, sinks):
    s = z_a.shape[0]
    qn, k2, v2 = _qk_prep(z_a, *rope, gq2, gk2, min(256, s))
    a, mix, mixt = _attn_fwd(qn, k2, v2, z_b, conv_wp, sinks)
    return qn, k2, v2, a, mix, mixt


def _forward_out(x, p, target, mix, mixt, w_out, g2, w_pg, b_pg, w_pp, g3):
    s = x.shape[0]
    tm = min(512, s)
    x1, hn2, hn2t = _fwd_out(mix, w_out, x, g2, tm)
    dy, dgp, dt, pt, acc_ple = _ple(hn2, w_pg, b_pg, p, w_pp, g3, x1, target, min(256, s))
    dx1, dx1b, acc_g2 = _gate_bwd(dgp, w_pg, x1, dy, g2, tm)
    gw_out = _mm_grad(mixt, [dx1b], 512, "grad_w_out")
    gw_pg = _mm_grad(hn2t, [dgp], 512, "grad_w_ple_gate")
    gw_pp = _mm_grad(pt, [dt], 512, "grad_w_ple_proj")
    return dx1, dx1b, (gw_out, gw_pg, gw_pp), acc_ple, acc_g2


def _backward_in(x, dx1, dmix, g1, w_qkv, w_rest, ht, z_a, z_b, qn, k2, v2, a, rope, gq2, gk2, conv_wp, sinks):
    s = x.shape[0]
    dq, dkc, dkp, dvc, dvp, dz_b, acc_attn = _attn_bwd(qn, k2, v2, a, z_b, dmix, conv_wp, sinks)
    dz_a, acc_qk = _qkv_bwd(z_a, dq, dkc, dkp, dvc, dvp, *rope, gq2, gk2)
    grad_x, acc_g1 = _in_bwd(dz_a, dz_b, w_qkv, w_rest, x, dx1, g1, min(512, s))
    gw_in = _mm_grad(ht, [dz_a, dz_b], 512, "grad_w_in")
    return grad_x, gw_in, acc_attn, acc_qk, acc_g1


def _small_rows(acc_g1, acc_g2, acc_ple, acc_qk, acc_attn):
    fold = lambda v: _row((v[:HEAD] + v[HEAD:]))
    return [acc_g1[0:1], acc_g2[0:1], acc_ple[0:1], acc_ple[1:2], fold(acc_qk[0]), fold(acc_qk[1]),
            _row(acc_attn[0, :N_Q_HEADS]), _row(acc_attn[1]), _row(acc_attn[2]), _row(acc_attn[3]), acc_ple[2:3]]


def _local_step(x, p, target, g1, w_qkv, w_rest, gq, gk, sinks, conv_w, w_out, g2, w_pg, b_pg, w_pp, g3):
    rope, gq2, gk2, conv_wp = _tables(x.shape[0], gq, gk, conv_w)
    ht, z_a, z_b = _forward_in(x, g1, w_qkv, w_rest)
    qn, k2, v2, a, mix, mixt = _forward_attn(z_a, z_b, rope, gq2, gk2, conv_wp, sinks)
    dx1, dx1b, (gw_out, gw_pg, gw_pp), acc_ple, acc_g2 = _forward_out(
        x, p, target, mix, mixt, w_out, g2, w_pg, b_pg, w_pp, g3)
    dmix = _mm_nt(dx1b, w_out, min(512, x.shape[0]), "out_bwd")
    grad_x, gw_in, acc_attn, acc_qk, acc_g1 = _backward_in(
        x, dx1, dmix, g1, w_qkv, w_rest, ht, z_a, z_b, qn, k2, v2, a, rope, gq2, gk2, conv_wp, sinks)
    return grad_x, (gw_in, gw_out, gw_pg, gw_pp), _small_rows(acc_g1, acc_g2, acc_ple, acc_qk, acc_attn)


ROW_CONV, ROW_LOSS = 7, 10


def _slab(rows):
    rows = list(rows)
    return jnp.concatenate(rows + [jnp.zeros((SLAB_ROWS - len(rows), D_MODEL), F32)], axis=0)


def _by_owner(g):
    return g.reshape((4, 2) + g.shape[1:])


def kernel(x, p, norm_gain, w_in, q_norm_gain, k_norm_gain, attn_sinks, conv_w, w_out, ple_gate_norm_gain, w_ple_gate, b_ple_gate, w_ple_proj, ple_norm_gain, loss_target, m_norm_gain, m_w_in, m_q_norm_gain, m_k_norm_gain, m_attn_sinks, m_conv_w, m_w_out, m_ple_gate_norm_gain, m_w_ple_gate, m_b_ple_gate, m_w_ple_proj, m_ple_norm_gain, v_norm_gain, v_w_in, v_q_norm_gain, v_k_norm_gain, v_attn_sinks, v_conv_w, v_w_out, v_ple_gate_norm_gain, v_w_ple_gate, v_b_ple_gate, v_w_ple_proj, v_ple_norm_gain):
    me = 4 * lax.axis_index("x") + 2 * lax.axis_index("y") + lax.axis_index("c")
    place = jnp.stack([lax.axis_index("c"), 2 * lax.axis_index("x") + lax.axis_index("y")]).astype(jnp.int32)
    conv_cols = conv_w.shape[2]
    xs, ps, target = x[0], p[0, 0], loss_target[0]
    zero = lambda token: token[0:1, 0:1]

    (g_in,) = _all_gather([w_in[0].astype(BF16)])
    late, started = _gather_start(
        [w_out[0].astype(BF16), w_ple_gate[0].astype(BF16), w_ple_proj[0].astype(BF16)], g_in)
    split = QKV_W - SHARD_IN
    w_qkv = jnp.concatenate([g_in[0], g_in[1][:, :split]], axis=1)
    w_rest = jnp.concatenate([g_in[1][:, split:]] + [g_in[d] for d in range(2, N_DEV)], axis=1)
    conv_rows = [lax.dynamic_update_slice(jnp.zeros((1, D_MODEL), F32), conv_w[0, t:t + 1], (0, conv_cols * me))
                 for t in range(3)]
    conv_full = _all_reduce_slab(_slab(conv_rows), "gather_conv_w")[0:3, :ATTN_W]
    rope, gq2, gk2, conv_wp = _tables(xs.shape[0], q_norm_gain[0], k_norm_gain[0], conv_full)

    g1 = norm_gain + zero(started)
    ht, z_a, z_b = _forward_in(xs, g1, w_qkv, w_rest)
    late, forwarded = _gather_forward(late, z_b)
    qn, k2, v2, a, mix, mixt = _forward_attn(z_a, z_b, rope, gq2 + zero(forwarded), gk2, conv_wp, attn_sinks)
    g_out, g_pg, g_pp = _gather_wait(late, mix)
    w_out_f = g_out.reshape(D_MODEL, D_MODEL)
    w_pg_f = g_pg.reshape(D_MODEL, D_MODEL)
    w_pp_f = jnp.transpose(g_pp, (1, 0, 2)).reshape(PLE_DIM, D_MODEL)

    dx1, dx1b, (gw_out, gw_pg, gw_pp), acc_ple, acc_g2 = _forward_out(
        xs, ps, target, mix, mixt, w_out_f, ple_gate_norm_gain, w_pg_f, b_ple_gate, w_pp_f, ple_norm_gain)

    names = ("w_out", "w_ple_gate", "w_ple_proj")
    gw_pp_t = jnp.transpose(gw_pp.reshape(PLE_DIM, N_DEV, PLE_DIM), (1, 0, 2))
    grads = [_by_owner(gw_out.reshape(N_DEV, D_MODEL // N_DEV, D_MODEL)),
             _by_owner(gw_pg.reshape(N_DEV, D_MODEL // N_DEV, D_MODEL)), _by_owner(gw_pp_t)]
    pairs, _ = _exchange_start("pair_start", grads, [(4,) + g.shape[2:] for g in grads], _to_sibling, 1, dx1b)
    dmix = _mm_nt(dx1b, w_out_f, min(512, xs.shape[0]), "out_bwd")
    from_sibling = _exchange_wait("pair_wait", pairs, _to_sibling, dmix)
    sums = [_pair_sum(g, r, place, 256, "pair_sum_" + nm) for g, r, nm in zip(pairs["srcs"], from_sibling, names)]
    chips, sent = _exchange_start("chip_start", [pb for pb, _ in sums], [(3,) + pb.shape[1:] for pb, _ in sums],
                                  _to_chips, 3, sums[-1][1])

    grad_x, gw_in, acc_attn, acc_qk, acc_g1 = _backward_in(
        xs, dx1, dmix, norm_gain, w_qkv, w_rest, ht, z_a, z_b, qn, k2, v2, a, rope, gq2, gk2, conv_wp,
        attn_sinks + zero(sent))
    from_chips = _exchange_wait("chip_wait", chips, _to_chips, gw_in)
    big = {}
    for (_, own), oth, w, m, v, nm in zip(sums, from_chips, (w_out, w_ple_gate, w_ple_proj),
                                          (m_w_out, m_w_ple_gate, m_w_ple_proj),
                                          (v_w_out, v_w_ple_gate, v_w_ple_proj), names):
        big[nm] = [t[None] for t in _adamw(own, oth, w[0], m[0], v[0], 256, "adamw_" + nm)]

    gw_in_t = [_by_owner(jnp.transpose(gw_in.reshape(D_MODEL, N_DEV, SHARD_IN), (1, 0, 2)))]
    (from_sibling_in,) = _pair_exchange(gw_in_t)
    pb_in, own_in = _pair_sum(gw_in_t[0], from_sibling_in, place, 256, "pair_sum_w_in")
    (from_chips_in,) = _chip_exchange([pb_in])
    big["w_in"] = [t[None] for t in _adamw(own_in, from_chips_in, w_in[0], m_w_in[0], v_w_in[0], 256, "adamw_w_in")]

    red = _all_reduce_slab(_slab(_small_rows(acc_g1, acc_g2, acc_ple, acc_qk, acc_attn)), "reduce_small")
    loss = jnp.sum(red[ROW_LOSS])
    g_conv = [lax.dynamic_slice(red[ROW_CONV + t:ROW_CONV + t + 1], (0, conv_cols * me), (1, conv_cols))
              for t in range(3)]
    small = [norm_gain, ple_gate_norm_gain, b_ple_gate, ple_norm_gain, q_norm_gain, k_norm_gain, attn_sinks]
    small_m = [m_norm_gain, m_ple_gate_norm_gain, m_b_ple_gate, m_ple_norm_gain, m_q_norm_gain, m_k_norm_gain,
               m_attn_sinks]
    small_v = [v_norm_gain, v_ple_gate_norm_gain, v_b_ple_gate, v_ple_norm_gain, v_q_norm_gain, v_k_norm_gain,
               v_attn_sinks]
    pack = lambda vs, cw: _slab([_row(t) for t in vs] + [_row(cw[0, t]) for t in range(3)])
    g_slab = _slab([red[t:t + 1] for t in range(ROW_CONV)] + [_row(t) for t in g_conv])
    d_slab, m_slab, v_slab = _adamw_slab(pack(small, conv_w), g_slab, pack(small_m, m_conv_w), pack(small_v, v_conv_w))

    def unpack(slab_):
        outs = [slab_[t:t + 1, :w.shape[1]] for t, w in enumerate(small)]
        return outs, slab_[ROW_CONV:ROW_CONV + 3, :conv_cols][None]

    (g_s, g_cv), (d_s, d_cv), (m_s, m_cv), (v_s, v_cv) = (unpack(t) for t in (g_slab, d_slab, m_slab, v_slab))

    def order(sm, cv, k):
        return [sm[0], big["w_in"][k], sm[4], sm[5], sm[6], cv, big["w_out"][k], sm[1], big["w_ple_gate"][k], sm[2],
                big["w_ple_proj"][k], sm[3]]

    return (loss, grad_x[None], *order(g_s, g_cv, 0), *order(d_s, d_cv, 1), *order(m_s, m_cv, 2),
            *order(v_s, v_cv, 3))
```

```python
import functools

import jax
import jax.numpy as jnp
from jax import lax
from jax.experimental import pallas as pl
from jax.experimental.pallas import tpu as pltpu

F32, BF16 = jnp.float32, jnp.bfloat16

D_MODEL = 2048
PLE_DIM = 256
ATTN_W = 1024
HEAD = 64
N_Q_HEADS = 16
KV_W = 256
QKV_W = ATTN_W + 2 * KV_W
REST_W = 5 * 1024
IN_W = QKV_W + REST_W
K2_W = 4 * 128
ROT = 16
ROPE_THETA = 500000.0
EPS = 1e-6
NEG_INF = -1e30
BLK = 128
LANES = 128
SUBLANES = 8
N_DEV = 8
SHARD_IN = IN_W // N_DEV
SLAB_ROWS = 16
SUB_ROWS = 128
V7X_VMEM_LIMIT = 52 * 1024 * 1024

ADAM_LR, ADAM_B1, ADAM_B2, ADAM_EPS, ADAM_WD, ADAM_STEP = 0.001, 0.9, 0.999, 1e-08, 0.01, 10
MESH = pl.DeviceIdType.MESH


def _params(*semantics):
    return pltpu.CompilerParams(dimension_semantics=semantics, vmem_limit_bytes=V7X_VMEM_LIMIT)


ANY = pl.BlockSpec(memory_space=pl.ANY)


def _resident(shape):
    return pl.BlockSpec(shape, lambda *_: (0,) * len(shape), pipeline_mode=pl.Buffered(1))


def _dot(a, b):
    return jnp.dot(a, b, preferred_element_type=F32)


def _dot_nt(a, b):
    return lax.dot_general(a, b, (((1,), (1,)), ((), ())), preferred_element_type=F32)


def _rms(xf):
    r = lax.rsqrt(jnp.mean(xf * xf, axis=-1, keepdims=True) + EPS)
    return xf * r, r


def _rms_bwd(dxn, xn, r):
    return r * (dxn - xn * jnp.mean(dxn * xn, axis=-1, keepdims=True))


def _sig(g):
    return jax.nn.sigmoid(g)


def _dsilu(g, sg):
    return sg * (1.0 + g * (1.0 - sg))


def _low_half(shape):
    return lax.broadcasted_iota(jnp.int32, shape, len(shape) - 1) < HEAD


def _half_sums(v):
    lo = _low_half(v.shape)
    s_lo = jnp.sum(jnp.where(lo, v, 0.0), axis=-1, keepdims=True)
    s_hi = jnp.sum(jnp.where(lo, 0.0, v), axis=-1, keepdims=True)
    return jnp.where(lo, s_lo, s_hi)


def _rope(v, a, bm, bp):
    return v * a + pltpu.roll(v, LANES - ROT // 2, 1) * bm + pltpu.roll(v, ROT // 2, 1) * bp


def _rope_t(dy, a, bm, bp):
    return dy * a + pltpu.roll(dy * bm, ROT // 2, 1) + pltpu.roll(dy * bp, LANES - ROT // 2, 1)


def _dup_halves(v):
    lo = _low_half(v.shape)
    a = jnp.where(lo, v, 0.0)
    b = jnp.where(lo, 0.0, v)
    return a + pltpu.roll(a, HEAD, 1), b + pltpu.roll(b, HEAD, 1)


def _rope_tables(s):
    half = ROT // 2
    lane = lax.broadcasted_iota(jnp.int32, (s, LANES), 1) % HEAD
    pos = lax.broadcasted_iota(jnp.int32, (s, LANES), 0).astype(F32)
    inv_freq = jnp.power(jnp.float32(ROPE_THETA), -(lane % half).astype(F32) * 2.0 / ROT)
    ang = pos * inv_freq
    cos, sin = jnp.cos(ang), jnp.sin(ang)
    a = jnp.where(lane < ROT, cos, 1.0)
    bm = jnp.where(lane < half, -sin, 0.0)
    bp = jnp.where((lane >= half) & (lane < ROT), sin, 0.0)
    return a, bm, bp


def _fwd_in_a(x, g1, w_qkv, tm):
    s = x.shape[0]

    def body(x_ref, g_ref, w_ref, h_ref, ht_ref, z_ref):
        xn, _ = _rms(x_ref[...])
        h = (xn * g_ref[...]).astype(BF16)
        h_ref[...] = h
        ht_ref[...] = h.T
        z_ref[...] = _dot(h, w_ref[...])

    return pl.pallas_call(
        body, name="fwd_in_a",
        out_shape=(jax.ShapeDtypeStruct((s, D_MODEL), BF16), jax.ShapeDtypeStruct((D_MODEL, s), BF16),
                   jax.ShapeDtypeStruct((s, QKV_W), F32)),
        grid=(s // tm,),
        in_specs=[pl.BlockSpec((tm, D_MODEL), lambda i: (i, 0)),
                  pl.BlockSpec((1, D_MODEL), lambda i: (0, 0)),
                  _resident((D_MODEL, QKV_W))],
        out_specs=(pl.BlockSpec((tm, D_MODEL), lambda i: (i, 0)),
                   pl.BlockSpec((D_MODEL, tm), lambda i: (0, i)),
                   pl.BlockSpec((tm, QKV_W), lambda i: (i, 0))),
        compiler_params=_params("parallel"))(x, g1, w_qkv)


def _mm_nn(a, b, tm, tn, name):
    m, k = a.shape
    n = b.shape[1]

    def body(a_ref, b_ref, o_ref):
        o_ref[...] = _dot(a_ref[...], b_ref[...])

    return pl.pallas_call(
        body, name=name,
        out_shape=jax.ShapeDtypeStruct((m, n), F32),
        grid=(n // tn, m // tm),
        in_specs=[pl.BlockSpec((tm, k), lambda j, i: (i, 0)),
                  pl.BlockSpec((k, tn), lambda j, i: (0, j))],
        out_specs=pl.BlockSpec((tm, tn), lambda j, i: (i, j)),
        compiler_params=_params("parallel", "parallel"))(a, b)


def _qk_prep(z_a, ra, rbm, rbp, gq2, gk2, tm):
    s = z_a.shape[0]

    def body(z_ref, a_ref, bm_ref, bp_ref, gq_ref, gk_ref, q_ref, k2_ref, v2_ref):
        a, bm, bp = a_ref[...], bm_ref[...], bp_ref[...]
        for r in range(ATTN_W // LANES):
            x = z_ref[:, LANES * r:LANES * (r + 1)]
            rr = lax.rsqrt(_half_sums(x * x) * (1.0 / HEAD) + EPS)
            q_ref[:, LANES * r:LANES * (r + 1)] = _rope(x * rr * gq_ref[...], a, bm, bp).astype(BF16)
        for m in range(KV_W // LANES):
            x = z_ref[:, ATTN_W + LANES * m:ATTN_W + LANES * (m + 1)]
            rr = lax.rsqrt(_half_sums(x * x) * (1.0 / HEAD) + EPS)
            k_lo, k_hi = _dup_halves(_rope(x * rr * gk_ref[...], a, bm, bp))
            k2_ref[:, 2 * LANES * m:2 * LANES * m + LANES] = k_lo.astype(BF16)
            k2_ref[:, 2 * LANES * m + LANES:2 * LANES * (m + 1)] = k_hi.astype(BF16)
            v_lo, v_hi = _dup_halves(z_ref[:, ATTN_W + KV_W + LANES * m:ATTN_W + KV_W + LANES * (m + 1)])
            v2_ref[:, 2 * LANES * m:2 * LANES * m + LANES] = v_lo.astype(BF16)
            v2_ref[:, 2 * LANES * m + LANES:2 * LANES * (m + 1)] = v_hi.astype(BF16)

    row = lambda w: pl.BlockSpec((tm, w), lambda i: (i, 0))
    one = pl.BlockSpec((1, LANES), lambda i: (0, 0))
    return pl.pallas_call(
        body, name="qk_prep",
        out_shape=(jax.ShapeDtypeStruct((s, ATTN_W), BF16), jax.ShapeDtypeStruct((s, K2_W), BF16),
                   jax.ShapeDtypeStruct((s, K2_W), BF16)),
        grid=(s // tm,),
        in_specs=[row(QKV_W), row(LANES), row(LANES), row(LANES), one, one],
        out_specs=(row(ATTN_W), row(K2_W), row(K2_W)),
        compiler_params=_params("parallel"))(z_a, ra, rbm, rbp, gq2, gk2)


def _window_mask(n):
    row = lax.broadcasted_iota(jnp.int32, (BLK, 2 * BLK), 0)
    col = lax.broadcasted_iota(jnp.int32, (BLK, 2 * BLK), 1)
    return (col > row) & (col <= row + BLK) & ((col >= BLK) | (n > 0))


def _head_probs(qm, kw, valid, sink):
    sc = jnp.where(valid, _dot_nt(qm, kw) * (HEAD ** -0.5), NEG_INF)
    mx = jnp.maximum(jnp.max(sc, axis=-1, keepdims=True), sink)
    ex = jnp.exp(sc - mx)
    den = jnp.sum(ex, axis=-1, keepdims=True) + jnp.exp(sink - mx)
    return ex / den, mx, den


def _conv_fwd(zb_ref, zbp_ref, cw_ref, ext_ref, n):
    u = zb_ref[:, 2048:3072] * zb_ref[:, 3072:4096]
    pu = zbp_ref[:, 2048:3072] * zbp_ref[:, 3072:4096]
    ext_ref[0:SUBLANES, :] = jnp.where(n > 0, pu, 0.0)
    ext_ref[SUBLANES:SUBLANES + BLK, :] = u
    um1 = ext_ref[SUBLANES - 1:SUBLANES - 1 + BLK, :]
    um2 = ext_ref[SUBLANES - 2:SUBLANES - 2 + BLK, :]
    cv = cw_ref[0:1, :] * um2 + cw_ref[1:2, :] * um1 + cw_ref[2:3, :] * u
    return u, um1, um2, cv


def _prev_rows(n):
    return (jnp.maximum(n * (BLK // SUBLANES) - 1, 0), 0)


def _attn_fwd(qn, k2, v2, z_b, conv_wp, sinks):
    s = qn.shape[0]
    nb = s // BLK

    def body(sink_ref, q_ref, kc_ref, kp_ref, vc_ref, vp_ref, zb_ref, zbp_ref, cw_ref, a_ref, mix_ref, mixt_ref,
             ext_ref):
        n = pl.program_id(0)
        valid = _window_mask(n)
        lo = _low_half((BLK, LANES))
        for r in range(ATTN_W // LANES):
            kvh = r // 2
            cols = slice(LANES * kvh, LANES * (kvh + 1))
            qp = q_ref[:, LANES * r:LANES * (r + 1)]
            kw = jnp.concatenate([kp_ref[:, cols], kc_ref[:, cols]], axis=0)
            vw = jnp.concatenate([vp_ref[:, cols], vc_ref[:, cols]], axis=0)
            outs = []
            for e in range(2):
                qm = jnp.where(lo if e == 0 else jnp.logical_not(lo), qp, jnp.zeros_like(qp))
                p, _, _ = _head_probs(qm, kw, valid, sink_ref[0, 2 * r + e])
                outs.append(_dot(p.astype(BF16), vw))
            a = jnp.where(lo, outs[0], outs[1])
            a_ref[:, LANES * r:LANES * (r + 1)] = a
            g = zb_ref[:, LANES * r:LANES * (r + 1)]
            mix_ref[:, LANES * r:LANES * (r + 1)] = (a * (g * _sig(g))).astype(BF16)
        _, _, _, cv = _conv_fwd(zb_ref, zbp_ref, cw_ref, ext_ref, n)
        gc = zb_ref[:, 4096:5120]
        mix_ref[:, ATTN_W:D_MODEL] = (zb_ref[:, 1024:2048] * cv * (gc * _sig(gc))).astype(BF16)
        mixt_ref[...] = mix_ref[...].T

    cur = lambda w: pl.BlockSpec((BLK, w), lambda n: (n, 0))
    prev = lambda w: pl.BlockSpec((BLK, w), lambda n: (jnp.maximum(n - 1, 0), 0))
    return pl.pallas_call(
        body, name="attn_fwd",
        out_shape=(jax.ShapeDtypeStruct((s, ATTN_W), F32), jax.ShapeDtypeStruct((s, D_MODEL), BF16),
                   jax.ShapeDtypeStruct((D_MODEL, s), BF16)),
        grid=(nb,),
        in_specs=[pl.BlockSpec(memory_space=pltpu.SMEM),
                  cur(ATTN_W), cur(K2_W), prev(K2_W), cur(K2_W), prev(K2_W), cur(REST_W),
                  pl.BlockSpec((SUBLANES, REST_W), _prev_rows),
                  pl.BlockSpec((SUBLANES, ATTN_W), lambda n: (0, 0))],
        out_specs=(cur(ATTN_W), cur(D_MODEL), pl.BlockSpec((D_MODEL, BLK), lambda n: (0, n))),
        scratch_shapes=[pltpu.VMEM((BLK + 2 * SUBLANES, ATTN_W), F32)],
        compiler_params=_params("parallel"))(sinks, qn, k2, k2, v2, v2, z_b, z_b, conv_wp)


def _fwd_out(mix, w_out, x, g2, tm):
    s = x.shape[0]

    def body(m_ref, w_ref, x_ref, g_ref, x1_ref, h_ref, ht_ref):
        x1 = x_ref[...] + _dot(m_ref[...], w_ref[...])
        x1_ref[...] = x1
        xn, _ = _rms(x1)
        h = (xn * g_ref[...]).astype(BF16)
        h_ref[...] = h
        ht_ref[...] = h.T

    row = pl.BlockSpec((tm, D_MODEL), lambda i: (i, 0))
    return pl.pallas_call(
        body, name="fwd_out",
        out_shape=(jax.ShapeDtypeStruct((s, D_MODEL), F32), jax.ShapeDtypeStruct((s, D_MODEL), BF16),
                   jax.ShapeDtypeStruct((D_MODEL, s), BF16)),
        grid=(s // tm,),
        in_specs=[row, _resident((D_MODEL, D_MODEL)), row, pl.BlockSpec((1, D_MODEL), lambda i: (0, 0))],
        out_specs=(row, row, pl.BlockSpec((D_MODEL, tm), lambda i: (0, i))),
        compiler_params=_params("parallel"))(mix, w_out, x, g2)


def _ple(hn2, w_pg, b_pg, p, w_pp, g3, x1, target, tm):
    s = x1.shape[0]

    def body(h_ref, wg_ref, b_ref, p_ref, wp_ref, g3_ref, x1_ref, t_ref, dy_ref, dgp_ref, dt_ref, pt_ref, acc_ref):
        gate = _sig(_dot(h_ref[...], wg_ref[...]) + b_ref[...])
        pb = p_ref[...].astype(BF16)
        pt_ref[...] = pb.T
        t = _dot(pb, wp_ref[...])
        tn, r3 = _rms(t)
        e = tn * g3_ref[...]
        diff = x1_ref[...] + gate * e - t_ref[...]
        dy = diff * (1.0 / D_MODEL)
        dy_ref[...] = dy
        dgp = dy * e * (gate * (1.0 - gate))
        dgp_ref[...] = dgp.astype(BF16)
        de = dy * gate
        dt_ref[...] = _rms_bwd(de * g3_ref[...], tn, r3).astype(BF16)

        @pl.when(pl.program_id(0) == 0)
        def _():
            acc_ref[...] = jnp.zeros_like(acc_ref)

        acc_ref[0:1, :] += jnp.sum(dgp, axis=0, keepdims=True)
        acc_ref[1:2, :] += jnp.sum(de * tn, axis=0, keepdims=True)
        acc_ref[2:3, :] += jnp.sum(diff * diff, axis=0, keepdims=True) * (0.5 / D_MODEL)

    row = pl.BlockSpec((tm, D_MODEL), lambda i: (i, 0))
    vec = pl.BlockSpec((1, D_MODEL), lambda i: (0, 0))
    return pl.pallas_call(
        body, name="ple",
        out_shape=(jax.ShapeDtypeStruct((s, D_MODEL), F32), jax.ShapeDtypeStruct((s, D_MODEL), BF16),
                   jax.ShapeDtypeStruct((s, D_MODEL), BF16), jax.ShapeDtypeStruct((PLE_DIM, s), BF16),
                   jax.ShapeDtypeStruct((SUBLANES, D_MODEL), F32)),
        grid=(s // tm,),
        in_specs=[row, _resident((D_MODEL, D_MODEL)), vec, pl.BlockSpec((tm, PLE_DIM), lambda i: (i, 0)),
                  _resident((PLE_DIM, D_MODEL)), vec, row, row],
        out_specs=(row, row, row, pl.BlockSpec((PLE_DIM, tm), lambda i: (0, i)),
                   pl.BlockSpec((SUBLANES, D_MODEL), lambda i: (0, 0))),
        compiler_params=_params("arbitrary"))(hn2, w_pg, b_pg, p, w_pp, g3, x1, target)


def _gate_bwd(dgp, w_pg, x1, dy, g2, tm):
    s = x1.shape[0]

    def body(d_ref, w_ref, x1_ref, dy_ref, g_ref, dx_ref, dxb_ref, acc_ref):
        dh = _dot_nt(d_ref[...], w_ref[...])
        xn, r = _rms(x1_ref[...])
        dx1 = dy_ref[...] + _rms_bwd(dh * g_ref[...], xn, r)
        dx_ref[...] = dx1
        dxb_ref[...] = dx1.astype(BF16)

        @pl.when(pl.program_id(0) == 0)
        def _():
            acc_ref[...] = jnp.zeros_like(acc_ref)

        acc_ref[0:1, :] += jnp.sum(dh * xn, axis=0, keepdims=True)

    row = pl.BlockSpec((tm, D_MODEL), lambda i: (i, 0))
    return pl.pallas_call(
        body, name="gate_bwd",
        out_shape=(jax.ShapeDtypeStruct((s, D_MODEL), F32), jax.ShapeDtypeStruct((s, D_MODEL), BF16),
                   jax.ShapeDtypeStruct((SUBLANES, D_MODEL), F32)),
        grid=(s // tm,),
        in_specs=[row, _resident((D_MODEL, D_MODEL)), row, row, pl.BlockSpec((1, D_MODEL), lambda i: (0, 0))],
        out_specs=(row, row, pl.BlockSpec((SUBLANES, D_MODEL), lambda i: (0, 0))),
        compiler_params=_params("arbitrary"))(dgp, w_pg, x1, dy, g2)


def _mm_nt(a, b, tm, name):
    m, k = a.shape
    n = b.shape[0]

    def body(a_ref, b_ref, o_ref):
        o_ref[...] = _dot_nt(a_ref[...], b_ref[...])

    return pl.pallas_call(
        body, name=name,
        out_shape=jax.ShapeDtypeStruct((m, n), F32),
        grid=(m // tm,),
        in_specs=[pl.BlockSpec((tm, k), lambda i: (i, 0)), _resident((n, k))],
        out_specs=pl.BlockSpec((tm, n), lambda i: (i, 0)),
        compiler_params=_params("parallel"))(a, b)


def _attn_bwd(qn, k2, v2, a, z_b, dmix, conv_wp, sinks):
    s = qn.shape[0]
    nb = s // BLK

    def body(sink_ref, q_ref, kc_ref, kp_ref, vc_ref, vp_ref, a_ref, zb_ref, zbp_ref, zbn_ref, dm_ref, dmn_ref,
             cw_ref, dq_ref, dkc_ref, dkp_ref, dvc_ref, dvp_ref, dzb_ref, acc_ref, ext_ref):
        n = pl.program_id(0)
        valid = _window_mask(n)
        lo = _low_half((BLK, LANES))
        lane = lax.broadcasted_iota(jnp.int32, (1, ATTN_W), 1)

        @pl.when(n == 0)
        def _():
            acc_ref[...] = jnp.zeros_like(acc_ref)

        dsink = jnp.zeros((1, ATTN_W), F32)
        for kvh in range(K2_W // LANES):
            cols = slice(LANES * kvh, LANES * (kvh + 1))
            kw = jnp.concatenate([kp_ref[:, cols], kc_ref[:, cols]], axis=0)
            vw = jnp.concatenate([vp_ref[:, cols], vc_ref[:, cols]], axis=0)
            dk2 = jnp.zeros((2 * BLK, LANES), F32)
            dv2 = jnp.zeros((2 * BLK, LANES), F32)
            for r in (2 * kvh, 2 * kvh + 1):
                rc = slice(LANES * r, LANES * (r + 1))
                g = zb_ref[:, rc]
                sg = _sig(g)
                dm = dm_ref[:, rc]
                av = a_ref[:, rc]
                da = dm * (g * sg)
                dzb_ref[:, rc] = (dm * av * _dsilu(g, sg)).astype(BF16)
                qp = q_ref[:, rc]
                dqs = []
                for e in range(2):
                    half = lo if e == 0 else jnp.logical_not(lo)
                    sink = sink_ref[0, 2 * r + e]
                    qm = jnp.where(half, qp, jnp.zeros_like(qp))
                    p, mx, den = _head_probs(qm, kw, valid, sink)
                    do = jnp.where(half, da, 0.0)
                    delta = jnp.sum(do * av, axis=-1, keepdims=True)
                    dob = do.astype(BF16)
                    ds = p * (_dot_nt(dob, vw) - delta) * (HEAD ** -0.5)
                    dsb = ds.astype(BF16)
                    dqs.append(_dot(dsb, kw))
                    dk2 = dk2 + _dot(ds.T.astype(BF16), qm)
                    dv2 = dv2 + _dot(p.T.astype(BF16), dob)
                    dsk = -jnp.sum(jnp.exp(sink - mx) / den * delta, axis=0, keepdims=True)
                    dsink = dsink + jnp.where(lane == 2 * r + e, dsk, 0.0)
                dq_ref[:, rc] = jnp.where(lo, dqs[0], dqs[1])
            dkp_ref[:, cols] = dk2[0:BLK]
            dkc_ref[:, cols] = dk2[BLK:2 * BLK]
            dvp_ref[:, cols] = dv2[0:BLK]
            dvc_ref[:, cols] = dv2[BLK:2 * BLK]
        acc_ref[0:1, :] += dsink

        u, um1, um2, cv = _conv_fwd(zb_ref, zbp_ref, cw_ref, ext_ref, n)
        cb = zb_ref[:, 1024:2048]
        gc = zb_ref[:, 4096:5120]
        sgc = _sig(gc)
        dmc = dm_ref[:, ATTN_W:D_MODEL]
        t = dmc * (gc * sgc)
        dcv = t * cb
        dzb_ref[:, 1024:2048] = (t * cv).astype(BF16)
        dzb_ref[:, 4096:5120] = (dmc * cb * cv * _dsilu(gc, sgc)).astype(BF16)
        gcn = zbn_ref[:, 4096:5120]
        dcvn = dmn_ref[:, ATTN_W:D_MODEL] * (gcn * _sig(gcn)) * zbn_ref[:, 1024:2048]
        ext_ref[0:BLK, :] = dcv
        ext_ref[BLK:BLK + SUBLANES, :] = jnp.where(n < nb - 1, dcvn, 0.0)
        du = (cw_ref[2:3, :] * dcv + cw_ref[1:2, :] * ext_ref[1:1 + BLK, :]
              + cw_ref[0:1, :] * ext_ref[2:2 + BLK, :])
        dzb_ref[:, 2048:3072] = (du * zb_ref[:, 3072:4096]).astype(BF16)
        dzb_ref[:, 3072:4096] = (du * zb_ref[:, 2048:3072]).astype(BF16)
        acc_ref[1:2, :] += jnp.sum(dcv * um2, axis=0, keepdims=True)
        acc_ref[2:3, :] += jnp.sum(dcv * um1, axis=0, keepdims=True)
        acc_ref[3:4, :] += jnp.sum(dcv * u, axis=0, keepdims=True)

    cur = lambda w: pl.BlockSpec((BLK, w), lambda n: (n, 0))
    prev = lambda w: pl.BlockSpec((BLK, w), lambda n: (jnp.maximum(n - 1, 0), 0))
    nxt = lambda w: pl.BlockSpec(
        (SUBLANES, w), lambda n: (jnp.minimum((n + 1) * (BLK // SUBLANES), nb * (BLK // SUBLANES) - 1), 0))
    f32 = lambda w: jax.ShapeDtypeStruct((s, w), F32)
    return pl.pallas_call(
        body, name="attn_bwd",
        out_shape=(f32(ATTN_W), f32(K2_W), f32(K2_W), f32(K2_W), f32(K2_W),
                   jax.ShapeDtypeStruct((s, REST_W), BF16), jax.ShapeDtypeStruct((SUBLANES, ATTN_W), F32)),
        grid=(nb,),
        in_specs=[pl.BlockSpec(memory_space=pltpu.SMEM),
                  cur(ATTN_W), cur(K2_W), prev(K2_W), cur(K2_W), prev(K2_W), cur(ATTN_W), cur(REST_W),
                  pl.BlockSpec((SUBLANES, REST_W), _prev_rows), nxt(REST_W), cur(D_MODEL), nxt(D_MODEL),
                  pl.BlockSpec((SUBLANES, ATTN_W), lambda n: (0, 0))],
        out_specs=(cur(ATTN_W), cur(K2_W), cur(K2_W), cur(K2_W), cur(K2_W), cur(REST_W),
                   pl.BlockSpec((SUBLANES, ATTN_W), lambda n: (0, 0))),
        scratch_shapes=[pltpu.VMEM((BLK + 2 * SUBLANES, ATTN_W), F32)],
        compiler_params=_params("arbitrary"))(sinks, qn, k2, k2, v2, v2, a, z_b, z_b, z_b, dmix, dmix, conv_wp)


def _qkv_bwd(z_a, dq, dkc, dkp, dvc, dvp, ra, rbm, rbp, gq2, gk2):
    s = z_a.shape[0]
    nb = s // BLK

    def body(z_ref, dq_ref, dkc_ref, dkp_ref, dvc_ref, dvp_ref, a_ref, bm_ref, bp_ref, gq_ref, gk_ref,
             dz_ref, acc_ref):
        n = pl.program_id(0)
        a, bm, bp = a_ref[...], bm_ref[...], bp_ref[...]
        lo = _low_half((BLK, LANES))
        last = n == nb - 1

        @pl.when(n == 0)
        def _():
            acc_ref[...] = jnp.zeros_like(acc_ref)

        def norm_bwd(x, dy, gain):
            rr = lax.rsqrt(_half_sums(x * x) * (1.0 / HEAD) + EPS)
            xh = x * rr
            dxg = _rope_t(dy, a, bm, bp)
            dxh = dxg * gain
            dx = rr * (dxh - xh * (_half_sums(dxh * xh) * (1.0 / HEAD)))
            return dx, jnp.sum(dxg * xh, axis=0, keepdims=True)

        def folded(cur_ref, prev_ref, m):
            parts = []
            for h in (2 * m, 2 * m + 1):
                v = cur_ref[:, LANES * h:LANES * (h + 1)] + jnp.where(
                    last, 0.0, prev_ref[:, LANES * h:LANES * (h + 1)])
                parts.append(v + pltpu.roll(v, HEAD, 1))
            return jnp.where(lo, parts[0], parts[1])

        gq_acc = jnp.zeros((1, LANES), F32)
        for r in range(ATTN_W // LANES):
            rc = slice(LANES * r, LANES * (r + 1))
            dx, gg = norm_bwd(z_ref[:, rc], dq_ref[:, rc], gq_ref[...])
            dz_ref[:, rc] = dx.astype(BF16)
            gq_acc = gq_acc + gg
        acc_ref[0:1, :] += gq_acc
        gk_acc = jnp.zeros((1, LANES), F32)
        for m in range(KV_W // LANES):
            kc = slice(ATTN_W + LANES * m, ATTN_W + LANES * (m + 1))
            dx, gg = norm_bwd(z_ref[:, kc], folded(dkc_ref, dkp_ref, m), gk_ref[...])
            dz_ref[:, kc] = dx.astype(BF16)
            gk_acc = gk_acc + gg
            vc = slice(ATTN_W + KV_W + LANES * m, ATTN_W + KV_W + LANES * (m + 1))
            dz_ref[:, vc] = folded(dvc_ref, dvp_ref, m).astype(BF16)
        acc_ref[1:2, :] += gk_acc

    cur = lambda w: pl.BlockSpec((BLK, w), lambda n: (n, 0))
    nxt = lambda w: pl.BlockSpec((BLK, w), lambda n: (jnp.minimum(n + 1, nb - 1), 0))
    one = pl.BlockSpec((1, LANES), lambda n: (0, 0))
    return pl.pallas_call(
        body, name="qkv_bwd",
        out_shape=(jax.ShapeDtypeStruct((s, QKV_W), BF16), jax.ShapeDtypeStruct((SUBLANES, LANES), F32)),
        grid=(nb,),
        in_specs=[cur(QKV_W), cur(ATTN_W), cur(K2_W), nxt(K2_W), cur(K2_W), nxt(K2_W),
                  cur(LANES), cur(LANES), cur(LANES), one, one],
        out_specs=(cur(QKV_W), pl.BlockSpec((SUBLANES, LANES), lambda n: (0, 0))),
        compiler_params=_params("arbitrary"))(z_a, dq, dkc, dkp, dvc, dvp, ra, rbm, rbp, gq2, gk2)


REST_CHUNK = 1280
N_REST_CHUNKS = REST_W // REST_CHUNK


def _in_bwd(dz_a, dz_b, w_qkv, w_rest, x, dx1, g1, tm):
    s = x.shape[0]
    nk = 1 + N_REST_CHUNKS

    def body(da_ref, db_ref, wa_ref, wb_ref, x_hbm, dx1_hbm, g_ref, gx_ref, acc_ref, x_buf, dx1_buf, sems):
        i, k = pl.program_id(0), pl.program_id(1)
        rows = pl.ds(pl.multiple_of(i * tm, tm), tm)
        fetch = [pltpu.make_async_copy(x_hbm.at[rows], x_buf, sems.at[0]),
                 pltpu.make_async_copy(dx1_hbm.at[rows], dx1_buf, sems.at[1])]

        sub = min(SUB_ROWS, tm)
        blocks = [slice(r, r + sub) for r in range(0, tm, sub)]

        @pl.when(k == 0)
        def _():
            for cp in fetch:
                cp.start()
            gx_ref[...] = _dot_nt(da_ref[...], wa_ref[...])

        @pl.when(k > 0)
        def _():
            gx_ref[...] += _dot_nt(db_ref[...], wb_ref[...])

        @pl.when((i == 0) & (k == 0))
        def _():
            acc_ref[...] = jnp.zeros_like(acc_ref)

        @pl.when(k == nk - 1)
        def _():
            for cp in fetch:
                cp.wait()
            for rb in blocks:
                dh = gx_ref[rb, :]
                xn, r = _rms(x_buf[rb, :])
                gx_ref[rb, :] = dx1_buf[rb, :] + _rms_bwd(dh * g_ref[...], xn, r)
                acc_ref[0:1, :] += jnp.sum(dh * xn, axis=0, keepdims=True)

    kb = lambda i, k: jnp.maximum(k - 1, 0)
    return pl.pallas_call(
        body, name="in_bwd",
        out_shape=(jax.ShapeDtypeStruct((s, D_MODEL), F32), jax.ShapeDtypeStruct((SUBLANES, D_MODEL), F32)),
        grid=(s // tm, nk),
        in_specs=[pl.BlockSpec((tm, QKV_W), lambda i, k: (i, 0)),
                  pl.BlockSpec((tm, REST_CHUNK), lambda i, k: (i, kb(i, k))),
                  _resident((D_MODEL, QKV_W)),
                  pl.BlockSpec((D_MODEL, REST_CHUNK), lambda i, k: (0, kb(i, k))),
                  ANY, ANY, pl.BlockSpec((1, D_MODEL), lambda i, k: (0, 0))],
        out_specs=(pl.BlockSpec((tm, D_MODEL), lambda i, k: (i, 0)),
                   pl.BlockSpec((SUBLANES, D_MODEL), lambda i, k: (0, 0))),
        scratch_shapes=[pltpu.VMEM((tm, D_MODEL), F32), pltpu.VMEM((tm, D_MODEL), F32),
                        pltpu.SemaphoreType.DMA((2,))],
        compiler_params=_params("arbitrary", "arbitrary"))(dz_a, dz_b, w_qkv, w_rest, x, dx1, g1)


def _mm_grad(at, bs, tn, name):
    m, kdim = at.shape
    nblk = [b.shape[1] // tn for b in bs]
    starts = [sum(nblk[:t]) for t in range(len(bs))]

    def body(a_ref, *refs):
        b_refs, o_ref = refs[:len(bs)], refs[len(bs)]
        j = pl.program_id(0)
        for t, b_ref in enumerate(b_refs):
            @pl.when((j >= starts[t]) & (j < starts[t] + nblk[t]))
            def _():
                o_ref[...] = _dot(a_ref[...], b_ref[...]).astype(BF16)

    def b_spec(t):
        return pl.BlockSpec((kdim, tn), lambda j: (0, jnp.clip(j - starts[t], 0, nblk[t] - 1)))

    return pl.pallas_call(
        body, name=name,
        out_shape=jax.ShapeDtypeStruct((m, sum(nblk) * tn), BF16),
        grid=(sum(nblk),),
        in_specs=[_resident((m, kdim))] + [b_spec(t) for t in range(len(bs))],
        out_specs=pl.BlockSpec((m, tn), lambda j: (0, j)),
        compiler_params=_params("parallel"))(at, *bs)


def _place():
    return lax.axis_index("x"), lax.axis_index("y"), lax.axis_index("c")


def _all_gather(shards):
    na = len(shards)

    def body(*refs):
        ins, outs = refs[:na], refs[na:2 * na]
        send_sems, recv_sems, local_sems = refs[2 * na:]
        x, y, c = _place()
        me, sibling = (x, y, c), (x, y, 1 - c)
        chips = [(1 - x, y), (x, 1 - y), (1 - x, 1 - y)]

        def copy(t, k, block, to, src=None):
            dst = outs[t].at[4 * block[0] + 2 * block[1] + block[2]]
            return pltpu.make_async_remote_copy(
                src_ref=dst if src is None else src, dst_ref=dst, send_sem=send_sems.at[t, k],
                recv_sem=recv_sems.at[t, k], device_id=to, device_id_type=MESH)

        mine = [pltpu.make_async_copy(ins[t], outs[t].at[4 * x + 2 * y + c], local_sems.at[t]) for t in range(na)]
        for cp in mine:
            cp.start()
        first = []
        for j, chip in enumerate(chips):
            first += [copy(t, 1 + j, me, (*chip, c), src=ins[t]) for t in range(na)]
        first += [copy(t, 0, me, sibling, src=ins[t]) for t in range(na)]
        for cp in first:
            cp.start()
        passed = []
        for j, chip in enumerate(chips):
            for t in range(na):
                copy(t, 1 + j, (*chip, c), me).wait_recv()
                passed.append(copy(t, 4 + j, (*chip, c), sibling))
                passed[-1].start()
        for t in range(na):
            copy(t, 0, sibling, me).wait_recv()
        for j, chip in enumerate(chips):
            for t in range(na):
                copy(t, 4 + j, (*chip, 1 - c), me).wait_recv()
        for cp in first + passed:
            cp.wait_send()
        for cp in mine:
            cp.wait()

    return pl.pallas_call(
        body, name="all_gather_weights",
        out_shape=tuple(jax.ShapeDtypeStruct((N_DEV,) + a.shape, a.dtype) for a in shards),
        in_specs=[ANY] * na, out_specs=tuple([ANY] * na),
        scratch_shapes=[pltpu.SemaphoreType.DMA((na, 7)), pltpu.SemaphoreType.DMA((na, 7)),
                        pltpu.SemaphoreType.DMA((na,))])(*shards)


def _all_reduce_slab(slab, name):
    def body(in_ref, out_ref, gath_ref, send_sems, recv_sems):
        x, y, c = _place()
        me = 4 * x + 2 * y + c
        gath_ref[me] = in_ref[...]
        copies = []
        for k in range(1, N_DEV):
            peer = (x ^ (k >> 2), y ^ ((k >> 1) & 1), c ^ (k & 1))
            copies.append(pltpu.make_async_remote_copy(
                src_ref=in_ref, dst_ref=gath_ref.at[me], send_sem=send_sems.at[k - 1],
                recv_sem=recv_sems.at[k - 1], device_id=peer, device_id_type=MESH))
        for cp in copies:
            cp.start()
        for cp in copies:
            cp.wait_recv()
        for cp in copies:
            cp.wait_send()
        total = gath_ref[0]
        for d in range(1, N_DEV):
            total = total + gath_ref[d]
        out_ref[...] = total

    vmem = pl.BlockSpec(memory_space=pltpu.VMEM)
    return pl.pallas_call(
        body, name=name,
        out_shape=jax.ShapeDtypeStruct(slab.shape, F32),
        in_specs=[vmem], out_specs=vmem,
        scratch_shapes=[pltpu.VMEM((N_DEV,) + slab.shape, F32),
                        pltpu.SemaphoreType.DMA((N_DEV - 1,)), pltpu.SemaphoreType.DMA((N_DEV - 1,))])(slab)


def _pair_sum(g, r, place, tr, name):
    _, _, rows, cols = g.shape

    def body(place_ref, g_ref, r_ref, pb_ref, own_ref):
        tot = g_ref[0, 0].astype(F32) + r_ref[0].astype(F32)
        pb_ref[0] = tot.astype(BF16)

        @pl.when(pl.program_id(1) == place_ref[1])
        def _():
            own_ref[...] = tot

    grid_spec = pltpu.PrefetchScalarGridSpec(
        num_scalar_prefetch=1, grid=(rows // tr, 4),
        in_specs=[pl.BlockSpec((1, 1, tr, cols), lambda i, q, place_ref: (q, place_ref[0], i, 0)),
                  pl.BlockSpec((1, tr, cols), lambda i, q, place_ref: (q, i, 0))],
        out_specs=(pl.BlockSpec((1, tr, cols), lambda i, q, place_ref: (q, i, 0)),
                   pl.BlockSpec((tr, cols), lambda i, q, place_ref: (i, 0))))
    return pl.pallas_call(
        body, name=name, grid_spec=grid_spec,
        out_shape=(jax.ShapeDtypeStruct((4, rows, cols), BF16), jax.ShapeDtypeStruct((rows, cols), F32)),
        compiler_params=_params("arbitrary", "arbitrary"))(place, g, r)


HBM = pl.BlockSpec(memory_space=pltpu.HBM)
SEM = pl.BlockSpec(memory_space=pltpu.SEMAPHORE)
SIDE_EFFECT = pltpu.CompilerParams(has_side_effects=pltpu.SideEffectType.DATAFLOW_SIDE_EFFECTING)
TOKEN = jax.ShapeDtypeStruct((SUBLANES, LANES), F32)


def _hbm(a):
    return pltpu.with_memory_space_constraint(a, pltpu.HBM)


def _hbm_like(arrays):
    return tuple(pltpu.HBM(a.shape, a.dtype) for a in arrays)


def _block_of(px, py, pc):
    return 4 * px + 2 * py + pc


def _gather_start(shards, after):
    na = len(shards)
    lands = [_hbm(lax.empty((N_DEV,) + a.shape, a.dtype)) for a in shards]

    def body(*refs):
        ins, land = refs[:na], refs[na:2 * na]
        send_sems, recv_ici, recv_d2d = refs[2 * na + 1:2 * na + 4]
        token = refs[-1]
        x, y, c = _place()
        for k, peer in enumerate([(x, y, 1 - c), (1 - x, y, c), (x, 1 - y, c), (1 - x, 1 - y, c)]):
            for t in range(na):
                pltpu.make_async_remote_copy(
                    src_ref=ins[t], dst_ref=land[t].at[_block_of(x, y, c)], send_sem=send_sems.at[4 * t + k],
                    recv_sem=recv_d2d.at[4 * t] if k == 0 else recv_ici.at[3 * t + k - 1],
                    device_id=peer, device_id_type=MESH).start()
        token[...] = jnp.zeros_like(token)

    out = pl.pallas_call(
        body, name="gather_start",
        out_shape=(pltpu.SemaphoreType.DMA((4 * na,)), pltpu.SemaphoreType.DMA((3 * na,)),
                   pltpu.SemaphoreType.DMA((4 * na,)), *_hbm_like(shards), *_hbm_like(lands), TOKEN),
        in_specs=[HBM] * (2 * na) + [ANY],
        out_specs=(SEM, SEM, SEM, *[HBM] * (2 * na), pl.BlockSpec(memory_space=pltpu.VMEM)),
        input_output_aliases={i: 3 + i for i in range(2 * na)},
        compiler_params=SIDE_EFFECT)(*[_hbm(a) for a in shards], *lands, after)
    send_sems, recv_ici, recv_d2d = out[:3]
    state = dict(send=send_sems, ici=recv_ici, d2d=recv_d2d, shards=out[3:3 + na], lands=out[3 + na:3 + 2 * na])
    return state, out[-1]


def _gather_forward(state, after):
    lands = state["lands"]
    na = len(lands)

    def body(*refs):
        land = refs[:na]
        recv_ici, recv_d2d = refs[na], refs[na + 1]
        fwd_sems, token = refs[-2], refs[-1]
        x, y, c = _place()
        for j, chip in enumerate([(1 - x, y), (x, 1 - y), (1 - x, 1 - y)]):
            for t in range(na):
                blk = land[t].at[_block_of(*chip, c)]
                pltpu.make_async_remote_copy(
                    src_ref=blk, dst_ref=blk, send_sem=fwd_sems.at[3 * t + j], recv_sem=recv_ici.at[3 * t + j],
                    device_id=(x, y, c), device_id_type=MESH).wait_recv()
                pltpu.make_async_remote_copy(
                    src_ref=blk, dst_ref=blk, send_sem=fwd_sems.at[3 * t + j], recv_sem=recv_d2d.at[4 * t + 1 + j],
                    device_id=(x, y, 1 - c), device_id_type=MESH).start()
        token[...] = jnp.zeros_like(token)

    out = pl.pallas_call(
        body, name="gather_forward",
        out_shape=(*_hbm_like(lands), pltpu.SemaphoreType.DMA((3 * na,)), TOKEN),
        in_specs=[HBM] * na + [SEM, SEM, ANY],
        out_specs=(*[HBM] * na, SEM, pl.BlockSpec(memory_space=pltpu.VMEM)),
        input_output_aliases={i: i for i in range(na)},
        compiler_params=SIDE_EFFECT)(*lands, state["ici"], state["d2d"], after)
    return dict(state, lands=out[:na], fwd=out[na]), out[-1]


def _gather_wait(state, after):
    shards, lands = state["shards"], state["lands"]
    na = len(lands)

    def body(*refs):
        ins, land = refs[:na], refs[na:2 * na]
        send_sems, fwd_sems, recv_d2d = refs[2 * na:2 * na + 3]
        local_sems = refs[-1]
        x, y, c = _place()
        chips = [(1 - x, y), (x, 1 - y), (1 - x, 1 - y)]
        own = [pltpu.make_async_copy(ins[t], land[t].at[_block_of(x, y, c)], local_sems.at[t]) for t in range(na)]
        for cp in own:
            cp.start()
        for t in range(na):
            mine = land[t].at[_block_of(x, y, c)]
            for k in range(4):
                pltpu.make_async_remote_copy(
                    src_ref=ins[t], dst_ref=mine, send_sem=send_sems.at[4 * t + k], recv_sem=recv_d2d.at[4 * t],
                    device_id=(x, y, c), device_id_type=MESH).wait_send()
            for j, chip in enumerate(chips):
                blk = land[t].at[_block_of(*chip, c)]
                pltpu.make_async_remote_copy(
                    src_ref=blk, dst_ref=blk, send_sem=fwd_sems.at[3 * t + j], recv_sem=recv_d2d.at[4 * t + 1 + j],
                    device_id=(x, y, c), device_id_type=MESH).wait_send()
            for k, blk_id in enumerate([_block_of(x, y, 1 - c)] + [_block_of(*chip, 1 - c) for chip in chips]):
                blk = land[t].at[blk_id]
                pltpu.make_async_remote_copy(
                    src_ref=blk, dst_ref=blk, send_sem=send_sems.at[4 * t], recv_sem=recv_d2d.at[4 * t + k],
                    device_id=(x, y, c), device_id_type=MESH).wait_recv()
        for cp in own:
            cp.wait()

    out = pl.pallas_call(
        body, name="gather_wait",
        out_shape=(*_hbm_like(shards), *_hbm_like(lands)),
        in_specs=[HBM] * (2 * na) + [SEM, SEM, SEM, ANY],
        out_specs=tuple([HBM] * (2 * na)),
        input_output_aliases={i: i for i in range(2 * na)},
        scratch_shapes=[pltpu.SemaphoreType.DMA((na,))],
        compiler_params=SIDE_EFFECT)(*shards, *lands, state["send"], state["fwd"], state["d2d"], after)
    return out[na:]


def _to_sibling(srcs, lands, send_sems, recv_sems):
    x, y, c = _place()
    return [pltpu.make_async_remote_copy(
        src_ref=srcs[t].at[:, 1 - c], dst_ref=lands[t], send_sem=send_sems.at[t], recv_sem=recv_sems.at[t],
        device_id=(x, y, 1 - c), device_id_type=MESH) for t in range(len(srcs))]


def _to_chips(srcs, lands, send_sems, recv_sems):
    x, y, c = _place()
    copies = []
    for k in (1, 2, 3):
        px, py = x ^ (k >> 1), y ^ (k & 1)
        copies += [pltpu.make_async_remote_copy(
            src_ref=srcs[t].at[2 * px + py], dst_ref=lands[t].at[k - 1], send_sem=send_sems.at[3 * t + k - 1],
            recv_sem=recv_sems.at[3 * t + k - 1], device_id=(px, py, c), device_id_type=MESH) for t in range(len(srcs))]
    return copies


def _exchange_start(name, srcs, land_shapes, copies, per_array, after):
    na = len(srcs)
    lands = [_hbm(lax.empty(shp, a.dtype)) for shp, a in zip(land_shapes, srcs)]

    def body(*refs):
        token = refs[-1]
        for cp in copies(refs[:na], refs[na:2 * na], refs[2 * na + 1], refs[2 * na + 2]):
            cp.start()
        token[...] = jnp.zeros_like(token)

    out = pl.pallas_call(
        body, name=name,
        out_shape=(pltpu.SemaphoreType.DMA((na * per_array,)), pltpu.SemaphoreType.DMA((na * per_array,)),
                   *_hbm_like(srcs), *_hbm_like(lands), TOKEN),
        in_specs=[HBM] * (2 * na) + [ANY],
        out_specs=(SEM, SEM, *[HBM] * (2 * na), pl.BlockSpec(memory_space=pltpu.VMEM)),
        input_output_aliases={i: 2 + i for i in range(2 * na)},
        compiler_params=SIDE_EFFECT)(*[_hbm(a) for a in srcs], *lands, after)
    return dict(send=out[0], recv=out[1], srcs=out[2:2 + na], lands=out[2 + na:2 + 2 * na]), out[-1]


def _exchange_wait(name, state, copies, after):
    srcs, lands = state["srcs"], state["lands"]
    na = len(srcs)

    def body(*refs):
        for cp in copies(refs[:na], refs[na:2 * na], refs[2 * na], refs[2 * na + 1]):
            cp.wait_send()
            cp.wait_recv()

    out = pl.pallas_call(
        body, name=name,
        out_shape=(*_hbm_like(srcs), *_hbm_like(lands)),
        in_specs=[HBM] * (2 * na) + [SEM, SEM, ANY],
        out_specs=tuple([HBM] * (2 * na)),
        input_output_aliases={i: i for i in range(2 * na)},
        compiler_params=SIDE_EFFECT)(*srcs, *lands, state["send"], state["recv"], after)
    return out[na:]


def _adamw_math(w, g, m, v):
    m = ADAM_B1 * m + (1.0 - ADAM_B1) * g
    v = ADAM_B2 * v + (1.0 - ADAM_B2) * (g * g)
    m_hat = m / (1.0 - ADAM_B1 ** ADAM_STEP)
    v_hat = v / (1.0 - ADAM_B2 ** ADAM_STEP)
    return -ADAM_LR * (m_hat / (jnp.sqrt(v_hat) + ADAM_EPS) + ADAM_WD * w), m, v


def _adamw(own, others, w, m, v, tr, name):
    rows, cols = w.shape
    blk = pl.BlockSpec((tr, cols), lambda i: (i, 0))

    def body(own_ref, oth_ref, w_ref, m_ref, v_ref, g_ref, d_ref, nm_ref, nv_ref):
        g = own_ref[...]
        for k in range(3):
            g = g + oth_ref[k].astype(F32)
        g_ref[...] = g
        d_ref[...], nm_ref[...], nv_ref[...] = _adamw_math(w_ref[...], g, m_ref[...], v_ref[...])

    out = jax.ShapeDtypeStruct((rows, cols), F32)
    return pl.pallas_call(
        body, name=name, out_shape=(out, out, out, out), grid=(rows // tr,),
        in_specs=[blk, pl.BlockSpec((3, tr, cols), lambda i: (0, i, 0)), blk, blk, blk],
        out_specs=(blk, blk, blk, blk),
        compiler_params=_params("parallel"))(own, others, w, m, v)


def _adamw_slab(w, g, m, v):
    def body(w_ref, g_ref, m_ref, v_ref, d_ref, nm_ref, nv_ref):
        d_ref[...], nm_ref[...], nv_ref[...] = _adamw_math(w_ref[...], g_ref[...], m_ref[...], v_ref[...])

    out = jax.ShapeDtypeStruct(w.shape, F32)
    vmem = pl.BlockSpec(memory_space=pltpu.VMEM)
    return pl.pallas_call(body, name="adamw_small", out_shape=(out, out, out),
                          in_specs=[vmem] * 4, out_specs=(vmem, vmem, vmem))(w, g, m, v)


def _row(v, width=D_MODEL):
    v = v.reshape(1, -1)
    return jnp.pad(v, ((0, 0), (0, width - v.shape[1])))


def _tables(s, gq, gk, conv_w):
    gq2 = jnp.tile(gq.reshape(1, HEAD), (1, 2))
    gk2 = jnp.tile(gk.reshape(1, HEAD), (1, 2))
    conv_wp = jnp.pad(conv_w, ((0, SUBLANES - conv_w.shape[0]), (0, 0)))
    return _rope_tables(s), gq2, gk2, conv_wp


def _forward_in(x, g1, w_qkv, w_rest):
    s = x.shape[0]
    h, ht, z_a = _fwd_in_a(x, g1, w_qkv, min(512, s))
    z_b = _mm_nn(h, w_rest, min(512, s), 1024, "fwd_in_b")
    return ht, z_a, z_b


def _forward_attn(z_a, z_b, rope, gq2, gk2, conv_wp, sinks):
    s = z_a.shape[0]
    qn, k2, v2 = _qk_prep(z_a, *rope, gq2, gk2, min(256, s))
    a, mix, mixt = _attn_fwd(qn, k2, v2, z_b, conv_wp, sinks)
    return qn, k2, v2, a, mix, mixt


def _forward_out(x, p, target, mix, mixt, w_out, g2, w_pg, b_pg, w_pp, g3):
    s = x.shape[0]
    tm = min(512, s)
    x1, hn2, hn2t = _fwd_out(mix, w_out, x, g2, tm)
    dy, dgp, dt, pt, acc_ple = _ple(hn2, w_pg, b_pg, p, w_pp, g3, x1, target, min(256, s))
    dx1, dx1b, acc_g2 = _gate_bwd(dgp, w_pg, x1, dy, g2, tm)
    gw_out = _mm_grad(mixt, [dx1b], 512, "grad_w_out")
    gw_pg = _mm_grad(hn2t, [dgp], 512, "grad_w_ple_gate")
    gw_pp = _mm_grad(pt, [dt], 512, "grad_w_ple_proj")
    return dx1, dx1b, (gw_out, gw_pg, gw_pp), acc_ple, acc_g2


def _backward_attn(dmix, ht, z_a, z_b, qn, k2, v2, a, rope, gq2, gk2, conv_wp, sinks):
    dq, dkc, dkp, dvc, dvp, dz_b, acc_attn = _attn_bwd(qn, k2, v2, a, z_b, dmix, conv_wp, sinks)
    dz_a, acc_qk = _qkv_bwd(z_a, dq, dkc, dkp, dvc, dvp, *rope, gq2, gk2)
    gw_in = _mm_grad(ht, [dz_a, dz_b], 512, "grad_w_in")
    return dz_a, dz_b, gw_in, acc_attn, acc_qk


def _small_rows(acc_g1, acc_g2, acc_ple, acc_qk, acc_attn):
    fold = lambda v: _row((v[:HEAD] + v[HEAD:]))
    return [acc_g1[0:1], acc_g2[0:1], acc_ple[0:1], acc_ple[1:2], fold(acc_qk[0]), fold(acc_qk[1]),
            _row(acc_attn[0, :N_Q_HEADS]), _row(acc_attn[1]), _row(acc_attn[2]), _row(acc_attn[3]), acc_ple[2:3]]


def _local_step(x, p, target, g1, w_qkv, w_rest, gq, gk, sinks, conv_w, w_out, g2, w_pg, b_pg, w_pp, g3):
    rope, gq2, gk2, conv_wp = _tables(x.shape[0], gq, gk, conv_w)
    ht, z_a, z_b = _forward_in(x, g1, w_qkv, w_rest)
    qn, k2, v2, a, mix, mixt = _forward_attn(z_a, z_b, rope, gq2, gk2, conv_wp, sinks)
    dx1, dx1b, (gw_out, gw_pg, gw_pp), acc_ple, acc_g2 = _forward_out(
        x, p, target, mix, mixt, w_out, g2, w_pg, b_pg, w_pp, g3)
    dmix = _mm_nt(dx1b, w_out, min(512, x.shape[0]), "out_bwd")
    dz_a, dz_b, gw_in, acc_attn, acc_qk = _backward_attn(
        dmix, ht, z_a, z_b, qn, k2, v2, a, rope, gq2, gk2, conv_wp, sinks)
    grad_x, acc_g1 = _in_bwd(dz_a, dz_b, w_qkv, w_rest, x, dx1, g1, min(512, x.shape[0]))
    return grad_x, (gw_in, gw_out, gw_pg, gw_pp), _small_rows(acc_g1, acc_g2, acc_ple, acc_qk, acc_attn)


ROW_CONV, ROW_LOSS = 7, 10


def _slab(rows):
    rows = list(rows)
    return jnp.concatenate(rows + [jnp.zeros((SLAB_ROWS - len(rows), D_MODEL), F32)], axis=0)


def _by_owner(g):
    return g.reshape((4, 2) + g.shape[1:])


def kernel(x, p, norm_gain, w_in, q_norm_gain, k_norm_gain, attn_sinks, conv_w, w_out, ple_gate_norm_gain, w_ple_gate, b_ple_gate, w_ple_proj, ple_norm_gain, loss_target, m_norm_gain, m_w_in, m_q_norm_gain, m_k_norm_gain, m_attn_sinks, m_conv_w, m_w_out, m_ple_gate_norm_gain, m_w_ple_gate, m_b_ple_gate, m_w_ple_proj, m_ple_norm_gain, v_norm_gain, v_w_in, v_q_norm_gain, v_k_norm_gain, v_attn_sinks, v_conv_w, v_w_out, v_ple_gate_norm_gain, v_w_ple_gate, v_b_ple_gate, v_w_ple_proj, v_ple_norm_gain):
    me = 4 * lax.axis_index("x") + 2 * lax.axis_index("y") + lax.axis_index("c")
    place = jnp.stack([lax.axis_index("c"), 2 * lax.axis_index("x") + lax.axis_index("y")]).astype(jnp.int32)
    conv_cols = conv_w.shape[2]
    xs, ps, target = x[0], p[0, 0], loss_target[0]
    zero = lambda token: token[0:1, 0:1]

    (g_in,) = _all_gather([w_in[0].astype(BF16)])
    late, started = _gather_start(
        [w_out[0].astype(BF16), w_ple_gate[0].astype(BF16), w_ple_proj[0].astype(BF16)], g_in)
    split = QKV_W - SHARD_IN
    w_qkv = jnp.concatenate([g_in[0], g_in[1][:, :split]], axis=1)
    w_rest = jnp.concatenate([g_in[1][:, split:]] + [g_in[d] for d in range(2, N_DEV)], axis=1)
    conv_rows = [lax.dynamic_update_slice(jnp.zeros((1, D_MODEL), F32), conv_w[0, t:t + 1], (0, conv_cols * me))
                 for t in range(3)]
    conv_full = _all_reduce_slab(_slab(conv_rows), "gather_conv_w")[0:3, :ATTN_W]
    rope, gq2, gk2, conv_wp = _tables(xs.shape[0], q_norm_gain[0], k_norm_gain[0], conv_full)

    g1 = norm_gain + zero(started)
    ht, z_a, z_b = _forward_in(xs, g1, w_qkv, w_rest)
    late, forwarded = _gather_forward(late, z_b)
    qn, k2, v2, a, mix, mixt = _forward_attn(z_a, z_b, rope, gq2 + zero(forwarded), gk2, conv_wp, attn_sinks)
    g_out, g_pg, g_pp = _gather_wait(late, mix)
    w_out_f = g_out.reshape(D_MODEL, D_MODEL)
    w_pg_f = g_pg.reshape(D_MODEL, D_MODEL)
    w_pp_f = jnp.transpose(g_pp, (1, 0, 2)).reshape(PLE_DIM, D_MODEL)

    dx1, dx1b, (gw_out, gw_pg, gw_pp), acc_ple, acc_g2 = _forward_out(
        xs, ps, target, mix, mixt, w_out_f, ple_gate_norm_gain, w_pg_f, b_ple_gate, w_pp_f, ple_norm_gain)

    names = ("w_out", "w_ple_gate", "w_ple_proj")
    gw_pp_t = jnp.transpose(gw_pp.reshape(PLE_DIM, N_DEV, PLE_DIM), (1, 0, 2))
    grads = [_by_owner(gw_out.reshape(N_DEV, D_MODEL // N_DEV, D_MODEL)),
             _by_owner(gw_pg.reshape(N_DEV, D_MODEL // N_DEV, D_MODEL)), _by_owner(gw_pp_t)]
    pairs, _ = _exchange_start("pair_start", grads, [(4,) + g.shape[2:] for g in grads], _to_sibling, 1, dx1b)
    dmix = _mm_nt(dx1b, w_out_f, min(512, xs.shape[0]), "out_bwd")
    from_sibling = _exchange_wait("pair_wait", pairs, _to_sibling, dmix)
    sums = [_pair_sum(g, r, place, 256, "pair_sum_" + nm) for g, r, nm in zip(pairs["srcs"], from_sibling, names)]
    chips, sent = _exchange_start("chip_start", [pb for pb, _ in sums], [(3,) + pb.shape[1:] for pb, _ in sums],
                                  _to_chips, 3, sums[-1][1])

    dz_a, dz_b, gw_in, acc_attn, acc_qk = _backward_attn(
        dmix, ht, z_a, z_b, qn, k2, v2, a, rope, gq2, gk2, conv_wp, attn_sinks + zero(sent))

    gw_in_t = [_by_owner(jnp.transpose(gw_in.reshape(D_MODEL, N_DEV, SHARD_IN), (1, 0, 2)))]
    pairs_in, _ = _exchange_start("pair_start_w_in", gw_in_t, [(4,) + gw_in_t[0].shape[2:]], _to_sibling, 1, gw_in)
    from_chips = _exchange_wait("chip_wait", chips, _to_chips, gw_in)
    big = {}
    for (_, own), oth, w, m, v, nm in zip(sums, from_chips, (w_out, w_ple_gate, w_ple_proj),
                                          (m_w_out, m_w_ple_gate, m_w_ple_proj),
                                          (v_w_out, v_w_ple_gate, v_w_ple_proj), names):
        big[nm] = [t[None] for t in _adamw(own, oth, w[0], m[0], v[0], 256, "adamw_" + nm)]

    (from_sibling_in,) = _exchange_wait("pair_wait_w_in", pairs_in, _to_sibling, big[names[-1]][0])
    pb_in, own_in = _pair_sum(pairs_in["srcs"][0], from_sibling_in, place, 256, "pair_sum_w_in")
    chips_in, sent_in = _exchange_start("chip_start_w_in", [pb_in], [(3,) + pb_in.shape[1:]], _to_chips, 3, own_in)
    grad_x, acc_g1 = _in_bwd(dz_a, dz_b, w_qkv, w_rest, xs, dx1, norm_gain + zero(sent_in), min(512, xs.shape[0]))
    (from_chips_in,) = _exchange_wait("chip_wait_w_in", chips_in, _to_chips, grad_x)
    big["w_in"] = [t[None] for t in _adamw(own_in, from_chips_in, w_in[0], m_w_in[0], v_w_in[0], 256, "adamw_w_in")]

    red = _all_reduce_slab(_slab(_small_rows(acc_g1, acc_g2, acc_ple, acc_qk, acc_attn)), "reduce_small")
    loss = jnp.sum(red[ROW_LOSS])
    g_conv = [lax.dynamic_slice(red[ROW_CONV + t:ROW_CONV + t + 1], (0, conv_cols * me), (1, conv_cols))
              for t in range(3)]
    small = [norm_gain, ple_gate_norm_gain, b_ple_gate, ple_norm_gain, q_norm_gain, k_norm_gain, attn_sinks]
    small_m = [m_norm_gain, m_ple_gate_norm_gain, m_b_ple_gate, m_ple_norm_gain, m_q_norm_gain, m_k_norm_gain,
               m_attn_sinks]
    small_v = [v_norm_gain, v_ple_gate_norm_gain, v_b_ple_gate, v_ple_norm_gain, v_q_norm_gain, v_k_norm_gain,
               v_attn_sinks]
    pack = lambda vs, cw: _slab([_row(t) for t in vs] + [_row(cw[0, t]) for t in range(3)])
    g_slab = _slab([red[t:t + 1] for t in range(ROW_CONV)] + [_row(t) for t in g_conv])
    d_slab, m_slab, v_slab = _adamw_slab(pack(small, conv_w), g_slab, pack(small_m, m_conv_w), pack(small_v, v_conv_w))

    def unpack(slab_):
        outs = [slab_[t:t + 1, :w.shape[1]] for t, w in enumerate(small)]
        return outs, slab_[ROW_CONV:ROW_CONV + 3, :conv_cols][None]

    (g_s, g_cv), (d_s, d_cv), (m_s, m_cv), (v_s, v_cv) = (unpack(t) for t in (g_slab, d_slab, m_slab, v_slab))

    def order(sm, cv, k):
        return [sm[0], big["w_in"][k], sm[4], sm[5], sm[6], cv, big["w_out"][k], sm[1], big["w_ple_gate"][k], sm[2],
                big["w_ple_proj"][k], sm[3]]

    return (loss, grad_x[None], *order(g_s, g_cv, 0), *order(d_s, d_cv, 1), *order(m_s, m_cv, 2),
            *order(v_s, v_cv, 3))
```

```python
import functools

import jax
import jax.numpy as jnp
from jax import lax
from jax.experimental import pallas as pl
from jax.experimental.pallas import tpu as pltpu

F32, BF16 = jnp.float32, jnp.bfloat16

D_MODEL = 2048
PLE_DIM = 256
ATTN_W = 1024
HEAD = 64
N_Q_HEADS = 16
KV_W = 256
QKV_W = ATTN_W + 2 * KV_W
REST_W = 5 * 1024
IN_W = QKV_W + REST_W
K2_W = 4 * 128
ROT = 16
ROPE_THETA = 500000.0
EPS = 1e-6
NEG_INF = -1e30
BLK = 128
LANES = 128
SUBLANES = 8
N_DEV = 8
SHARD_IN = IN_W // N_DEV
SLAB_ROWS = 16
SUB_ROWS = 128
V7X_VMEM_LIMIT = 52 * 1024 * 1024

ADAM_LR, ADAM_B1, ADAM_B2, ADAM_EPS, ADAM_WD, ADAM_STEP = 0.001, 0.9, 0.999, 1e-08, 0.01, 10
MESH = pl.DeviceIdType.MESH


def _params(*semantics):
    return pltpu.CompilerParams(dimension_semantics=semantics, vmem_limit_bytes=V7X_VMEM_LIMIT)


ANY = pl.BlockSpec(memory_space=pl.ANY)


def _resident(shape):
    return pl.BlockSpec(shape, lambda *_: (0,) * len(shape), pipeline_mode=pl.Buffered(1))


def _dot(a, b):
    return jnp.dot(a, b, preferred_element_type=F32)


def _dot_nt(a, b):
    return lax.dot_general(a, b, (((1,), (1,)), ((), ())), preferred_element_type=F32)


def _rms(xf):
    r = lax.rsqrt(jnp.mean(xf * xf, axis=-1, keepdims=True) + EPS)
    return xf * r, r


def _rms_bwd(dxn, xn, r):
    return r * (dxn - xn * jnp.mean(dxn * xn, axis=-1, keepdims=True))


def _sig(g):
    return jax.nn.sigmoid(g)


def _dsilu(g, sg):
    return sg * (1.0 + g * (1.0 - sg))


def _low_half(shape):
    return lax.broadcasted_iota(jnp.int32, shape, len(shape) - 1) < HEAD


def _half_sums(v):
    lo = _low_half(v.shape)
    s_lo = jnp.sum(jnp.where(lo, v, 0.0), axis=-1, keepdims=True)
    s_hi = jnp.sum(jnp.where(lo, 0.0, v), axis=-1, keepdims=True)
    return jnp.where(lo, s_lo, s_hi)


def _rope(v, a, bm, bp):
    return v * a + pltpu.roll(v, LANES - ROT // 2, 1) * bm + pltpu.roll(v, ROT // 2, 1) * bp


def _rope_t(dy, a, bm, bp):
    return dy * a + pltpu.roll(dy * bm, ROT // 2, 1) + pltpu.roll(dy * bp, LANES - ROT // 2, 1)


def _dup_halves(v):
    lo = _low_half(v.shape)
    a = jnp.where(lo, v, 0.0)
    b = jnp.where(lo, 0.0, v)
    return a + pltpu.roll(a, HEAD, 1), b + pltpu.roll(b, HEAD, 1)


def _rope_tables(s):
    half = ROT // 2
    lane = lax.broadcasted_iota(jnp.int32, (s, LANES), 1) % HEAD
    pos = lax.broadcasted_iota(jnp.int32, (s, LANES), 0).astype(F32)
    inv_freq = jnp.power(jnp.float32(ROPE_THETA), -(lane % half).astype(F32) * 2.0 / ROT)
    ang = pos * inv_freq
    cos, sin = jnp.cos(ang), jnp.sin(ang)
    a = jnp.where(lane < ROT, cos, 1.0)
    bm = jnp.where(lane < half, -sin, 0.0)
    bp = jnp.where((lane >= half) & (lane < ROT), sin, 0.0)
    return a, bm, bp


def _fwd_in_a(x, g1, w_qkv, tm):
    s = x.shape[0]

    def body(x_ref, g_ref, w_ref, h_ref, ht_ref, z_ref):
        xn, _ = _rms(x_ref[...])
        h = (xn * g_ref[...]).astype(BF16)
        h_ref[...] = h
        ht_ref[...] = h.T
        z_ref[...] = _dot(h, w_ref[...])

    return pl.pallas_call(
        body, name="fwd_in_a",
        out_shape=(jax.ShapeDtypeStruct((s, D_MODEL), BF16), jax.ShapeDtypeStruct((D_MODEL, s), BF16),
                   jax.ShapeDtypeStruct((s, QKV_W), F32)),
        grid=(s // tm,),
        in_specs=[pl.BlockSpec((tm, D_MODEL), lambda i: (i, 0)),
                  pl.BlockSpec((1, D_MODEL), lambda i: (0, 0)),
                  _resident((D_MODEL, QKV_W))],
        out_specs=(pl.BlockSpec((tm, D_MODEL), lambda i: (i, 0)),
                   pl.BlockSpec((D_MODEL, tm), lambda i: (0, i)),
                   pl.BlockSpec((tm, QKV_W), lambda i: (i, 0))),
        compiler_params=_params("parallel"))(x, g1, w_qkv)


def _mm_nn(a, b, tm, tn, name):
    m, k = a.shape
    n = b.shape[1]

    def body(a_ref, b_ref, o_ref):
        o_ref[...] = _dot(a_ref[...], b_ref[...])

    return pl.pallas_call(
        body, name=name,
        out_shape=jax.ShapeDtypeStruct((m, n), F32),
        grid=(n // tn, m // tm),
        in_specs=[pl.BlockSpec((tm, k), lambda j, i: (i, 0)),
                  pl.BlockSpec((k, tn), lambda j, i: (0, j))],
        out_specs=pl.BlockSpec((tm, tn), lambda j, i: (i, j)),
        compiler_params=_params("parallel", "parallel"))(a, b)


def _qk_prep(z_a, ra, rbm, rbp, gq2, gk2, tm):
    s = z_a.shape[0]

    def body(z_ref, a_ref, bm_ref, bp_ref, gq_ref, gk_ref, q_ref, k2_ref, v2_ref):
        a, bm, bp = a_ref[...], bm_ref[...], bp_ref[...]
        for r in range(ATTN_W // LANES):
            x = z_ref[:, LANES * r:LANES * (r + 1)]
            rr = lax.rsqrt(_half_sums(x * x) * (1.0 / HEAD) + EPS)
            q_ref[:, LANES * r:LANES * (r + 1)] = _rope(x * rr * gq_ref[...], a, bm, bp).astype(BF16)
        for m in range(KV_W // LANES):
            x = z_ref[:, ATTN_W + LANES * m:ATTN_W + LANES * (m + 1)]
            rr = lax.rsqrt(_half_sums(x * x) * (1.0 / HEAD) + EPS)
            k_lo, k_hi = _dup_halves(_rope(x * rr * gk_ref[...], a, bm, bp))
            k2_ref[:, 2 * LANES * m:2 * LANES * m + LANES] = k_lo.astype(BF16)
            k2_ref[:, 2 * LANES * m + LANES:2 * LANES * (m + 1)] = k_hi.astype(BF16)
            v_lo, v_hi = _dup_halves(z_ref[:, ATTN_W + KV_W + LANES * m:ATTN_W + KV_W + LANES * (m + 1)])
            v2_ref[:, 2 * LANES * m:2 * LANES * m + LANES] = v_lo.astype(BF16)
            v2_ref[:, 2 * LANES * m + LANES:2 * LANES * (m + 1)] = v_hi.astype(BF16)

    row = lambda w: pl.BlockSpec((tm, w), lambda i: (i, 0))
    one = pl.BlockSpec((1, LANES), lambda i: (0, 0))
    return pl.pallas_call(
        body, name="qk_prep",
        out_shape=(jax.ShapeDtypeStruct((s, ATTN_W), BF16), jax.ShapeDtypeStruct((s, K2_W), BF16),
                   jax.ShapeDtypeStruct((s, K2_W), BF16)),
        grid=(s // tm,),
        in_specs=[row(QKV_W), row(LANES), row(LANES), row(LANES), one, one],
        out_specs=(row(ATTN_W), row(K2_W), row(K2_W)),
        compiler_params=_params("parallel"))(z_a, ra, rbm, rbp, gq2, gk2)


def _window_mask(n):
    row = lax.broadcasted_iota(jnp.int32, (BLK, 2 * BLK), 0)
    col = lax.broadcasted_iota(jnp.int32, (BLK, 2 * BLK), 1)
    return (col > row) & (col <= row + BLK) & ((col >= BLK) | (n > 0))


def _head_probs(qm, kw, valid, sink):
    sc = jnp.where(valid, _dot_nt(qm, kw) * (HEAD ** -0.5), NEG_INF)
    mx = jnp.maximum(jnp.max(sc, axis=-1, keepdims=True), sink)
    ex = jnp.exp(sc - mx)
    den = jnp.sum(ex, axis=-1, keepdims=True) + jnp.exp(sink - mx)
    return ex / den, mx, den


def _conv_fwd(zb_ref, zbp_ref, cw_ref, ext_ref, n):
    u = zb_ref[:, 2048:3072] * zb_ref[:, 3072:4096]
    pu = zbp_ref[:, 2048:3072] * zbp_ref[:, 3072:4096]
    ext_ref[0:SUBLANES, :] = jnp.where(n > 0, pu, 0.0)
    ext_ref[SUBLANES:SUBLANES + BLK, :] = u
    um1 = ext_ref[SUBLANES - 1:SUBLANES - 1 + BLK, :]
    um2 = ext_ref[SUBLANES - 2:SUBLANES - 2 + BLK, :]
    cv = cw_ref[0:1, :] * um2 + cw_ref[1:2, :] * um1 + cw_ref[2:3, :] * u
    return u, um1, um2, cv


def _prev_rows(n):
    return (jnp.maximum(n * (BLK // SUBLANES) - 1, 0), 0)


def _attn_fwd(qn, k2, v2, z_b, conv_wp, sinks):
    s = qn.shape[0]
    nb = s // BLK

    def body(sink_ref, q_ref, kc_ref, kp_ref, vc_ref, vp_ref, zb_ref, zbp_ref, cw_ref, a_ref, mix_ref, mixt_ref,
             ext_ref):
        n = pl.program_id(0)
        valid = _window_mask(n)
        lo = _low_half((BLK, LANES))
        for r in range(ATTN_W // LANES):
            kvh = r // 2
            cols = slice(LANES * kvh, LANES * (kvh + 1))
            qp = q_ref[:, LANES * r:LANES * (r + 1)]
            kw = jnp.concatenate([kp_ref[:, cols], kc_ref[:, cols]], axis=0)
            vw = jnp.concatenate([vp_ref[:, cols], vc_ref[:, cols]], axis=0)
            outs = []
            for e in range(2):
                qm = jnp.where(lo if e == 0 else jnp.logical_not(lo), qp, jnp.zeros_like(qp))
                p, _, _ = _head_probs(qm, kw, valid, sink_ref[0, 2 * r + e])
                outs.append(_dot(p.astype(BF16), vw))
            a = jnp.where(lo, outs[0], outs[1])
            a_ref[:, LANES * r:LANES * (r + 1)] = a
            g = zb_ref[:, LANES * r:LANES * (r + 1)]
            mix_ref[:, LANES * r:LANES * (r + 1)] = (a * (g * _sig(g))).astype(BF16)
        _, _, _, cv = _conv_fwd(zb_ref, zbp_ref, cw_ref, ext_ref, n)
        gc = zb_ref[:, 4096:5120]
        mix_ref[:, ATTN_W:D_MODEL] = (zb_ref[:, 1024:2048] * cv * (gc * _sig(gc))).astype(BF16)
        mixt_ref[...] = mix_ref[...].T

    cur = lambda w: pl.BlockSpec((BLK, w), lambda n: (n, 0))
    prev = lambda w: pl.BlockSpec((BLK, w), lambda n: (jnp.maximum(n - 1, 0), 0))
    return pl.pallas_call(
        body, name="attn_fwd",
        out_shape=(jax.ShapeDtypeStruct((s, ATTN_W), F32), jax.ShapeDtypeStruct((s, D_MODEL), BF16),
                   jax.ShapeDtypeStruct((D_MODEL, s), BF16)),
        grid=(nb,),
        in_specs=[pl.BlockSpec(memory_space=pltpu.SMEM),
                  cur(ATTN_W), cur(K2_W), prev(K2_W), cur(K2_W), prev(K2_W), cur(REST_W),
                  pl.BlockSpec((SUBLANES, REST_W), _prev_rows),
                  pl.BlockSpec((SUBLANES, ATTN_W), lambda n: (0, 0))],
        out_specs=(cur(ATTN_W), cur(D_MODEL), pl.BlockSpec((D_MODEL, BLK), lambda n: (0, n))),
        scratch_shapes=[pltpu.VMEM((BLK + 2 * SUBLANES, ATTN_W), F32)],
        compiler_params=_params("parallel"))(sinks, qn, k2, k2, v2, v2, z_b, z_b, conv_wp)


def _fwd_out(mix, w_out, x, g2, tm):
    s = x.shape[0]

    def body(m_ref, w_ref, x_ref, g_ref, x1_ref, h_ref, ht_ref):
        x1 = x_ref[...] + _dot(m_ref[...], w_ref[...])
        x1_ref[...] = x1
        xn, _ = _rms(x1)
        h = (xn * g_ref[...]).astype(BF16)
        h_ref[...] = h
        ht_ref[...] = h.T

    row = pl.BlockSpec((tm, D_MODEL), lambda i: (i, 0))
    return pl.pallas_call(
        body, name="fwd_out",
        out_shape=(jax.ShapeDtypeStruct((s, D_MODEL), F32), jax.ShapeDtypeStruct((s, D_MODEL), BF16),
                   jax.ShapeDtypeStruct((D_MODEL, s), BF16)),
        grid=(s // tm,),
        in_specs=[row, _resident((D_MODEL, D_MODEL)), row, pl.BlockSpec((1, D_MODEL), lambda i: (0, 0))],
        out_specs=(row, row, pl.BlockSpec((D_MODEL, tm), lambda i: (0, i))),
        compiler_params=_params("parallel"))(mix, w_out, x, g2)


def _ple(hn2, w_pg, b_pg, p, w_pp, g3, x1, target, tm):
    s = x1.shape[0]

    def body(h_ref, wg_ref, b_ref, p_ref, wp_ref, g3_ref, x1_ref, t_ref, dy_ref, dgp_ref, dt_ref, pt_ref, acc_ref):
        gate = _sig(_dot(h_ref[...], wg_ref[...]) + b_ref[...])
        pb = p_ref[...].astype(BF16)
        pt_ref[...] = pb.T
        t = _dot(pb, wp_ref[...])
        tn, r3 = _rms(t)
        e = tn * g3_ref[...]
        diff = x1_ref[...] + gate * e - t_ref[...]
        dy = diff * (1.0 / D_MODEL)
        dy_ref[...] = dy
        dgp = dy * e * (gate * (1.0 - gate))
        dgp_ref[...] = dgp.astype(BF16)
        de = dy * gate
        dt_ref[...] = _rms_bwd(de * g3_ref[...], tn, r3).astype(BF16)

        @pl.when(pl.program_id(0) == 0)
        def _():
            acc_ref[...] = jnp.zeros_like(acc_ref)

        acc_ref[0:1, :] += jnp.sum(dgp, axis=0, keepdims=True)
        acc_ref[1:2, :] += jnp.sum(de * tn, axis=0, keepdims=True)
        acc_ref[2:3, :] += jnp.sum(diff * diff, axis=0, keepdims=True) * (0.5 / D_MODEL)

    row = pl.BlockSpec((tm, D_MODEL), lambda i: (i, 0))
    vec = pl.BlockSpec((1, D_MODEL), lambda i: (0, 0))
    return pl.pallas_call(
        body, name="ple",
        out_shape=(jax.ShapeDtypeStruct((s, D_MODEL), F32), jax.ShapeDtypeStruct((s, D_MODEL), BF16),
                   jax.ShapeDtypeStruct((s, D_MODEL), BF16), jax.ShapeDtypeStruct((PLE_DIM, s), BF16),
                   jax.ShapeDtypeStruct((SUBLANES, D_MODEL), F32)),
        grid=(s // tm,),
        in_specs=[row, _resident((D_MODEL, D_MODEL)), vec, pl.BlockSpec((tm, PLE_DIM), lambda i: (i, 0)),
                  _resident((PLE_DIM, D_MODEL)), vec, row, row],
        out_specs=(row, row, row, pl.BlockSpec((PLE_DIM, tm), lambda i: (0, i)),
                   pl.BlockSpec((SUBLANES, D_MODEL), lambda i: (0, 0))),
        compiler_params=_params("arbitrary"))(hn2, w_pg, b_pg, p, w_pp, g3, x1, target)


def _gate_bwd(dgp, w_pg, x1, dy, g2, tm):
    s = x1.shape[0]

    def body(d_ref, w_ref, x1_ref, dy_ref, g_ref, dx_ref, dxb_ref, acc_ref):
        dh = _dot_nt(d_ref[...], w_ref[...])
        xn, r = _rms(x1_ref[...])
        dx1 = dy_ref[...] + _rms_bwd(dh * g_ref[...], xn, r)
        dx_ref[...] = dx1
        dxb_ref[...] = dx1.astype(BF16)

        @pl.when(pl.program_id(0) == 0)
        def _():
            acc_ref[...] = jnp.zeros_like(acc_ref)

        acc_ref[0:1, :] += jnp.sum(dh * xn, axis=0, keepdims=True)

    row = pl.BlockSpec((tm, D_MODEL), lambda i: (i, 0))
    return pl.pallas_call(
        body, name="gate_bwd",
        out_shape=(jax.ShapeDtypeStruct((s, D_MODEL), F32), jax.ShapeDtypeStruct((s, D_MODEL), BF16),
                   jax.ShapeDtypeStruct((SUBLANES, D_MODEL), F32)),
        grid=(s // tm,),
        in_specs=[row, _resident((D_MODEL, D_MODEL)), row, row, pl.BlockSpec((1, D_MODEL), lambda i: (0, 0))],
        out_specs=(row, row, pl.BlockSpec((SUBLANES, D_MODEL), lambda i: (0, 0))),
        compiler_params=_params("arbitrary"))(dgp, w_pg, x1, dy, g2)


def _mm_nt(a, b, tm, name):
    m, k = a.shape
    n = b.shape[0]

    def body(a_ref, b_ref, o_ref):
        o_ref[...] = _dot_nt(a_ref[...], b_ref[...])

    return pl.pallas_call(
        body, name=name,
        out_shape=jax.ShapeDtypeStruct((m, n), F32),
        grid=(m // tm,),
        in_specs=[pl.BlockSpec((tm, k), lambda i: (i, 0)), _resident((n, k))],
        out_specs=pl.BlockSpec((tm, n), lambda i: (i, 0)),
        compiler_params=_params("parallel"))(a, b)


def _attn_bwd(qn, k2, v2, a, z_b, dmix, conv_wp, sinks):
    s = qn.shape[0]
    nb = s // BLK

    def body(sink_ref, q_ref, kc_ref, kp_ref, vc_ref, vp_ref, a_ref, zb_ref, zbp_ref, zbn_ref, dm_ref, dmn_ref,
             cw_ref, dq_ref, dkc_ref, dkp_ref, dvc_ref, dvp_ref, dzb_ref, acc_ref, ext_ref):
        n = pl.program_id(0)
        valid = _window_mask(n)
        lo = _low_half((BLK, LANES))
        lane = lax.broadcasted_iota(jnp.int32, (1, ATTN_W), 1)

        @pl.when(n == 0)
        def _():
            acc_ref[...] = jnp.zeros_like(acc_ref)

        dsink = jnp.zeros((1, ATTN_W), F32)
        for kvh in range(K2_W // LANES):
            cols = slice(LANES * kvh, LANES * (kvh + 1))
            kw = jnp.concatenate([kp_ref[:, cols], kc_ref[:, cols]], axis=0)
            vw = jnp.concatenate([vp_ref[:, cols], vc_ref[:, cols]], axis=0)
            dk2 = jnp.zeros((2 * BLK, LANES), F32)
            dv2 = jnp.zeros((2 * BLK, LANES), F32)
            for r in (2 * kvh, 2 * kvh + 1):
                rc = slice(LANES * r, LANES * (r + 1))
                g = zb_ref[:, rc]
                sg = _sig(g)
                dm = dm_ref[:, rc]
                av = a_ref[:, rc]
                da = dm * (g * sg)
                dzb_ref[:, rc] = (dm * av * _dsilu(g, sg)).astype(BF16)
                qp = q_ref[:, rc]
                dqs = []
                for e in range(2):
                    half = lo if e == 0 else jnp.logical_not(lo)
                    sink = sink_ref[0, 2 * r + e]
                    qm = jnp.where(half, qp, jnp.zeros_like(qp))
                    p, mx, den = _head_probs(qm, kw, valid, sink)
                    do = jnp.where(half, da, 0.0)
                    delta = jnp.sum(do * av, axis=-1, keepdims=True)
                    dob = do.astype(BF16)
                    ds = p * (_dot_nt(dob, vw) - delta) * (HEAD ** -0.5)
                    dsb = ds.astype(BF16)
                    dqs.append(_dot(dsb, kw))
                    dk2 = dk2 + _dot(ds.T.astype(BF16), qm)
                    dv2 = dv2 + _dot(p.T.astype(BF16), dob)
                    dsk = -jnp.sum(jnp.exp(sink - mx) / den * delta, axis=0, keepdims=True)
                    dsink = dsink + jnp.where(lane == 2 * r + e, dsk, 0.0)
                dq_ref[:, rc] = jnp.where(lo, dqs[0], dqs[1])
            dkp_ref[:, cols] = dk2[0:BLK]
            dkc_ref[:, cols] = dk2[BLK:2 * BLK]
            dvp_ref[:, cols] = dv2[0:BLK]
            dvc_ref[:, cols] = dv2[BLK:2 * BLK]
        acc_ref[0:1, :] += dsink

        u, um1, um2, cv = _conv_fwd(zb_ref, zbp_ref, cw_ref, ext_ref, n)
        cb = zb_ref[:, 1024:2048]
        gc = zb_ref[:, 4096:5120]
        sgc = _sig(gc)
        dmc = dm_ref[:, ATTN_W:D_MODEL]
        t = dmc * (gc * sgc)
        dcv = t * cb
        dzb_ref[:, 1024:2048] = (t * cv).astype(BF16)
        dzb_ref[:, 4096:5120] = (dmc * cb * cv * _dsilu(gc, sgc)).astype(BF16)
        gcn = zbn_ref[:, 4096:5120]
        dcvn = dmn_ref[:, ATTN_W:D_MODEL] * (gcn * _sig(gcn)) * zbn_ref[:, 1024:2048]
        ext_ref[0:BLK, :] = dcv
        ext_ref[BLK:BLK + SUBLANES, :] = jnp.where(n < nb - 1, dcvn, 0.0)
        du = (cw_ref[2:3, :] * dcv + cw_ref[1:2, :] * ext_ref[1:1 + BLK, :]
              + cw_ref[0:1, :] * ext_ref[2:2 + BLK, :])
        dzb_ref[:, 2048:3072] = (du * zb_ref[:, 3072:4096]).astype(BF16)
        dzb_ref[:, 3072:4096] = (du * zb_ref[:, 2048:3072]).astype(BF16)
        acc_ref[1:2, :] += jnp.sum(dcv * um2, axis=0, keepdims=True)
        acc_ref[2:3, :] += jnp.sum(dcv * um1, axis=0, keepdims=True)
        acc_ref[3:4, :] += jnp.sum(dcv * u, axis=0, keepdims=True)

    cur = lambda w: pl.BlockSpec((BLK, w), lambda n: (n, 0))
    prev = lambda w: pl.BlockSpec((BLK, w), lambda n: (jnp.maximum(n - 1, 0), 0))
    nxt = lambda w: pl.BlockSpec(
        (SUBLANES, w), lambda n: (jnp.minimum((n + 1) * (BLK // SUBLANES), nb * (BLK // SUBLANES) - 1), 0))
    f32 = lambda w: jax.ShapeDtypeStruct((s, w), F32)
    return pl.pallas_call(
        body, name="attn_bwd",
        out_shape=(f32(ATTN_W), f32(K2_W), f32(K2_W), f32(K2_W), f32(K2_W),
                   jax.ShapeDtypeStruct((s, REST_W), BF16), jax.ShapeDtypeStruct((SUBLANES, ATTN_W), F32)),
        grid=(nb,),
        in_specs=[pl.BlockSpec(memory_space=pltpu.SMEM),
                  cur(ATTN_W), cur(K2_W), prev(K2_W), cur(K2_W), prev(K2_W), cur(ATTN_W), cur(REST_W),
                  pl.BlockSpec((SUBLANES, REST_W), _prev_rows), nxt(REST_W), cur(D_MODEL), nxt(D_MODEL),
                  pl.BlockSpec((SUBLANES, ATTN_W), lambda n: (0, 0))],
        out_specs=(cur(ATTN_W), cur(K2_W), cur(K2_W), cur(K2_W), cur(K2_W), cur(REST_W),
                   pl.BlockSpec((SUBLANES, ATTN_W), lambda n: (0, 0))),
        scratch_shapes=[pltpu.VMEM((BLK + 2 * SUBLANES, ATTN_W), F32)],
        compiler_params=_params("arbitrary"))(sinks, qn, k2, k2, v2, v2, a, z_b, z_b, z_b, dmix, dmix, conv_wp)


def _qkv_bwd(z_a, dq, dkc, dkp, dvc, dvp, ra, rbm, rbp, gq2, gk2):
    s = z_a.shape[0]
    nb = s // BLK

    def body(z_ref, dq_ref, dkc_ref, dkp_ref, dvc_ref, dvp_ref, a_ref, bm_ref, bp_ref, gq_ref, gk_ref,
             dz_ref, acc_ref):
        n = pl.program_id(0)
        a, bm, bp = a_ref[...], bm_ref[...], bp_ref[...]
        lo = _low_half((BLK, LANES))
        last = n == nb - 1

        @pl.when(n == 0)
        def _():
            acc_ref[...] = jnp.zeros_like(acc_ref)

        def norm_bwd(x, dy, gain):
            rr = lax.rsqrt(_half_sums(x * x) * (1.0 / HEAD) + EPS)
            xh = x * rr
            dxg = _rope_t(dy, a, bm, bp)
            dxh = dxg * gain
            dx = rr * (dxh - xh * (_half_sums(dxh * xh) * (1.0 / HEAD)))
            return dx, jnp.sum(dxg * xh, axis=0, keepdims=True)

        def folded(cur_ref, prev_ref, m):
            parts = []
            for h in (2 * m, 2 * m + 1):
                v = cur_ref[:, LANES * h:LANES * (h + 1)] + jnp.where(
                    last, 0.0, prev_ref[:, LANES * h:LANES * (h + 1)])
                parts.append(v + pltpu.roll(v, HEAD, 1))
            return jnp.where(lo, parts[0], parts[1])

        gq_acc = jnp.zeros((1, LANES), F32)
        for r in range(ATTN_W // LANES):
            rc = slice(LANES * r, LANES * (r + 1))
            dx, gg = norm_bwd(z_ref[:, rc], dq_ref[:, rc], gq_ref[...])
            dz_ref[:, rc] = dx.astype(BF16)
            gq_acc = gq_acc + gg
        acc_ref[0:1, :] += gq_acc
        gk_acc = jnp.zeros((1, LANES), F32)
        for m in range(KV_W // LANES):
            kc = slice(ATTN_W + LANES * m, ATTN_W + LANES * (m + 1))
            dx, gg = norm_bwd(z_ref[:, kc], folded(dkc_ref, dkp_ref, m), gk_ref[...])
            dz_ref[:, kc] = dx.astype(BF16)
            gk_acc = gk_acc + gg
            vc = slice(ATTN_W + KV_W + LANES * m, ATTN_W + KV_W + LANES * (m + 1))
            dz_ref[:, vc] = folded(dvc_ref, dvp_ref, m).astype(BF16)
        acc_ref[1:2, :] += gk_acc

    cur = lambda w: pl.BlockSpec((BLK, w), lambda n: (n, 0))
    nxt = lambda w: pl.BlockSpec((BLK, w), lambda n: (jnp.minimum(n + 1, nb - 1), 0))
    one = pl.BlockSpec((1, LANES), lambda n: (0, 0))
    return pl.pallas_call(
        body, name="qkv_bwd",
        out_shape=(jax.ShapeDtypeStruct((s, QKV_W), BF16), jax.ShapeDtypeStruct((SUBLANES, LANES), F32)),
        grid=(nb,),
        in_specs=[cur(QKV_W), cur(ATTN_W), cur(K2_W), nxt(K2_W), cur(K2_W), nxt(K2_W),
                  cur(LANES), cur(LANES), cur(LANES), one, one],
        out_specs=(cur(QKV_W), pl.BlockSpec((SUBLANES, LANES), lambda n: (0, 0))),
        compiler_params=_params("arbitrary"))(z_a, dq, dkc, dkp, dvc, dvp, ra, rbm, rbp, gq2, gk2)


REST_CHUNK = 1280
N_REST_CHUNKS = REST_W // REST_CHUNK


def _in_bwd(dz_a, dz_b, w_qkv, w_rest, x, dx1, g1, tm):
    s = x.shape[0]
    nk = 1 + N_REST_CHUNKS

    def body(da_ref, db_ref, wa_ref, wb_ref, x_hbm, dx1_hbm, g_ref, gx_ref, acc_ref, x_buf, dx1_buf, sems):
        i, k = pl.program_id(0), pl.program_id(1)
        rows = pl.ds(pl.multiple_of(i * tm, tm), tm)
        fetch = [pltpu.make_async_copy(x_hbm.at[rows], x_buf, sems.at[0]),
                 pltpu.make_async_copy(dx1_hbm.at[rows], dx1_buf, sems.at[1])]

        sub = min(SUB_ROWS, tm)
        blocks = [slice(r, r + sub) for r in range(0, tm, sub)]

        @pl.when(k == 0)
        def _():
            for cp in fetch:
                cp.start()
            gx_ref[...] = _dot_nt(da_ref[...], wa_ref[...])

        @pl.when(k > 0)
        def _():
            gx_ref[...] += _dot_nt(db_ref[...], wb_ref[...])

        @pl.when((i == 0) & (k == 0))
        def _():
            acc_ref[...] = jnp.zeros_like(acc_ref)

        @pl.when(k == nk - 1)
        def _():
            for cp in fetch:
                cp.wait()
            for rb in blocks:
                dh = gx_ref[rb, :]
                xn, r = _rms(x_buf[rb, :])
                gx_ref[rb, :] = dx1_buf[rb, :] + _rms_bwd(dh * g_ref[...], xn, r)
                acc_ref[0:1, :] += jnp.sum(dh * xn, axis=0, keepdims=True)

    kb = lambda i, k: jnp.maximum(k - 1, 0)
    return pl.pallas_call(
        body, name="in_bwd",
        out_shape=(jax.ShapeDtypeStruct((s, D_MODEL), F32), jax.ShapeDtypeStruct((SUBLANES, D_MODEL), F32)),
        grid=(s // tm, nk),
        in_specs=[pl.BlockSpec((tm, QKV_W), lambda i, k: (i, 0)),
                  pl.BlockSpec((tm, REST_CHUNK), lambda i, k: (i, kb(i, k))),
                  _resident((D_MODEL, QKV_W)),
                  pl.BlockSpec((D_MODEL, REST_CHUNK), lambda i, k: (0, kb(i, k))),
                  ANY, ANY, pl.BlockSpec((1, D_MODEL), lambda i, k: (0, 0))],
        out_specs=(pl.BlockSpec((tm, D_MODEL), lambda i, k: (i, 0)),
                   pl.BlockSpec((SUBLANES, D_MODEL), lambda i, k: (0, 0))),
        scratch_shapes=[pltpu.VMEM((tm, D_MODEL), F32), pltpu.VMEM((tm, D_MODEL), F32),
                        pltpu.SemaphoreType.DMA((2,))],
        compiler_params=_params("arbitrary", "arbitrary"))(dz_a, dz_b, w_qkv, w_rest, x, dx1, g1)


def _mm_grad(at, bs, tn, name):
    m, kdim = at.shape
    nblk = [b.shape[1] // tn for b in bs]
    starts = [sum(nblk[:t]) for t in range(len(bs))]

    def body(a_ref, *refs):
        b_refs, o_ref = refs[:len(bs)], refs[len(bs)]
        j = pl.program_id(0)
        for t, b_ref in enumerate(b_refs):
            @pl.when((j >= starts[t]) & (j < starts[t] + nblk[t]))
            def _():
                o_ref[...] = _dot(a_ref[...], b_ref[...]).astype(BF16)

    def b_spec(t):
        return pl.BlockSpec((kdim, tn), lambda j: (0, jnp.clip(j - starts[t], 0, nblk[t] - 1)))

    return pl.pallas_call(
        body, name=name,
        out_shape=jax.ShapeDtypeStruct((m, sum(nblk) * tn), BF16),
        grid=(sum(nblk),),
        in_specs=[_resident((m, kdim))] + [b_spec(t) for t in range(len(bs))],
        out_specs=pl.BlockSpec((m, tn), lambda j: (0, j)),
        compiler_params=_params("parallel"))(at, *bs)


def _place():
    return lax.axis_index("x"), lax.axis_index("y"), lax.axis_index("c")


def _all_gather(shards):
    na = len(shards)

    def body(*refs):
        ins, outs = refs[:na], refs[na:2 * na]
        send_sems, recv_sems = refs[2 * na:]
        x, y, c = _place()
        me, sibling = (x, y, c), (x, y, 1 - c)
        chips = [(1 - x, y), (x, 1 - y), (1 - x, 1 - y)]

        def copy(t, k, block, to, src=None):
            dst = outs[t].at[4 * block[0] + 2 * block[1] + block[2]]
            return pltpu.make_async_remote_copy(
                src_ref=dst if src is None else src, dst_ref=dst, send_sem=send_sems.at[t, k],
                recv_sem=recv_sems.at[t, k], device_id=to, device_id_type=MESH)

        first = []
        for j, chip in enumerate(chips):
            first += [copy(t, 1 + j, me, (*chip, c), src=ins[t]) for t in range(na)]
        first += [copy(t, 0, me, sibling, src=ins[t]) for t in range(na)]
        for cp in first:
            cp.start()
        passed = []
        for j, chip in enumerate(chips):
            for t in range(na):
                copy(t, 1 + j, (*chip, c), me).wait_recv()
                passed.append(copy(t, 4 + j, (*chip, c), sibling))
                passed[-1].start()
        for t in range(na):
            copy(t, 0, sibling, me).wait_recv()
        for j, chip in enumerate(chips):
            for t in range(na):
                copy(t, 4 + j, (*chip, 1 - c), me).wait_recv()
        for cp in first + passed:
            cp.wait_send()

    return pl.pallas_call(
        body, name="all_gather_weights",
        out_shape=tuple(jax.ShapeDtypeStruct((N_DEV,) + a.shape, a.dtype) for a in shards),
        in_specs=[ANY] * na, out_specs=tuple([ANY] * na),
        scratch_shapes=[pltpu.SemaphoreType.DMA((na, 7)), pltpu.SemaphoreType.DMA((na, 7))])(*shards)


def _all_reduce_slab(slab, name):
    def body(in_ref, out_ref, gath_ref, send_sems, recv_sems):
        x, y, c = _place()
        me = 4 * x + 2 * y + c
        gath_ref[me] = in_ref[...]
        copies = []
        for k in range(1, N_DEV):
            peer = (x ^ (k >> 2), y ^ ((k >> 1) & 1), c ^ (k & 1))
            copies.append(pltpu.make_async_remote_copy(
                src_ref=in_ref, dst_ref=gath_ref.at[me], send_sem=send_sems.at[k - 1],
                recv_sem=recv_sems.at[k - 1], device_id=peer, device_id_type=MESH))
        for cp in copies:
            cp.start()
        for cp in copies:
            cp.wait_recv()
        for cp in copies:
            cp.wait_send()
        total = gath_ref[0]
        for d in range(1, N_DEV):
            total = total + gath_ref[d]
        out_ref[...] = total

    vmem = pl.BlockSpec(memory_space=pltpu.VMEM)
    return pl.pallas_call(
        body, name=name,
        out_shape=jax.ShapeDtypeStruct(slab.shape, F32),
        in_specs=[vmem], out_specs=vmem,
        scratch_shapes=[pltpu.VMEM((N_DEV,) + slab.shape, F32),
                        pltpu.SemaphoreType.DMA((N_DEV - 1,)), pltpu.SemaphoreType.DMA((N_DEV - 1,))])(slab)


def _pair_sum(g, r, place, tr, name):
    _, _, rows, cols = g.shape

    def body(place_ref, g_ref, r_ref, pb_ref, own_ref):
        tot = g_ref[0, 0].astype(F32) + r_ref[0].astype(F32)
        pb_ref[0] = tot.astype(BF16)

        @pl.when(pl.program_id(1) == place_ref[1])
        def _():
            own_ref[...] = tot

    grid_spec = pltpu.PrefetchScalarGridSpec(
        num_scalar_prefetch=1, grid=(rows // tr, 4),
        in_specs=[pl.BlockSpec((1, 1, tr, cols), lambda i, q, place_ref: (q, place_ref[0], i, 0)),
                  pl.BlockSpec((1, tr, cols), lambda i, q, place_ref: (q, i, 0))],
        out_specs=(pl.BlockSpec((1, tr, cols), lambda i, q, place_ref: (q, i, 0)),
                   pl.BlockSpec((tr, cols), lambda i, q, place_ref: (i, 0))))
    return pl.pallas_call(
        body, name=name, grid_spec=grid_spec,
        out_shape=(jax.ShapeDtypeStruct((4, rows, cols), BF16), jax.ShapeDtypeStruct((rows, cols), F32)),
        compiler_params=_params("arbitrary", "arbitrary"))(place, g, r)


HBM = pl.BlockSpec(memory_space=pltpu.HBM)
SEM = pl.BlockSpec(memory_space=pltpu.SEMAPHORE)
SIDE_EFFECT = pltpu.CompilerParams(has_side_effects=pltpu.SideEffectType.DATAFLOW_SIDE_EFFECTING)
TOKEN = jax.ShapeDtypeStruct((SUBLANES, LANES), F32)


def _hbm(a):
    return pltpu.with_memory_space_constraint(a, pltpu.HBM)


def _hbm_like(arrays):
    return tuple(pltpu.HBM(a.shape, a.dtype) for a in arrays)


def _block_of(px, py, pc):
    return 4 * px + 2 * py + pc


def _gather_start(shards, after):
    na = len(shards)
    lands = [_hbm(lax.empty((N_DEV,) + a.shape, a.dtype)) for a in shards]

    def body(*refs):
        ins, land = refs[:na], refs[na:2 * na]
        send_sems, recv_ici, recv_d2d = refs[2 * na + 1:2 * na + 4]
        token = refs[-1]
        x, y, c = _place()
        for k, peer in enumerate([(x, y, 1 - c), (1 - x, y, c), (x, 1 - y, c), (1 - x, 1 - y, c)]):
            for t in range(na):
                pltpu.make_async_remote_copy(
                    src_ref=ins[t], dst_ref=land[t].at[_block_of(x, y, c)], send_sem=send_sems.at[4 * t + k],
                    recv_sem=recv_d2d.at[4 * t] if k == 0 else recv_ici.at[3 * t + k - 1],
                    device_id=peer, device_id_type=MESH).start()
        token[...] = jnp.zeros_like(token)

    out = pl.pallas_call(
        body, name="gather_start",
        out_shape=(pltpu.SemaphoreType.DMA((4 * na,)), pltpu.SemaphoreType.DMA((3 * na,)),
                   pltpu.SemaphoreType.DMA((4 * na,)), *_hbm_like(shards), *_hbm_like(lands), TOKEN),
        in_specs=[HBM] * (2 * na) + [ANY],
        out_specs=(SEM, SEM, SEM, *[HBM] * (2 * na), pl.BlockSpec(memory_space=pltpu.VMEM)),
        input_output_aliases={i: 3 + i for i in range(2 * na)},
        compiler_params=SIDE_EFFECT)(*[_hbm(a) for a in shards], *lands, after)
    send_sems, recv_ici, recv_d2d = out[:3]
    state = dict(send=send_sems, ici=recv_ici, d2d=recv_d2d, shards=out[3:3 + na], lands=out[3 + na:3 + 2 * na])
    return state, out[-1]


def _gather_forward(state, after):
    lands = state["lands"]
    na = len(lands)

    def body(*refs):
        land = refs[:na]
        recv_ici, recv_d2d = refs[na], refs[na + 1]
        fwd_sems, token = refs[-2], refs[-1]
        x, y, c = _place()
        for j, chip in enumerate([(1 - x, y), (x, 1 - y), (1 - x, 1 - y)]):
            for t in range(na):
                blk = land[t].at[_block_of(*chip, c)]
                pltpu.make_async_remote_copy(
                    src_ref=blk, dst_ref=blk, send_sem=fwd_sems.at[3 * t + j], recv_sem=recv_ici.at[3 * t + j],
                    device_id=(x, y, c), device_id_type=MESH).wait_recv()
                pltpu.make_async_remote_copy(
                    src_ref=blk, dst_ref=blk, send_sem=fwd_sems.at[3 * t + j], recv_sem=recv_d2d.at[4 * t + 1 + j],
                    device_id=(x, y, 1 - c), device_id_type=MESH).start()
        token[...] = jnp.zeros_like(token)

    out = pl.pallas_call(
        body, name="gather_forward",
        out_shape=(*_hbm_like(lands), pltpu.SemaphoreType.DMA((3 * na,)), TOKEN),
        in_specs=[HBM] * na + [SEM, SEM, ANY],
        out_specs=(*[HBM] * na, SEM, pl.BlockSpec(memory_space=pltpu.VMEM)),
        input_output_aliases={i: i for i in range(na)},
        compiler_params=SIDE_EFFECT)(*lands, state["ici"], state["d2d"], after)
    return dict(state, lands=out[:na], fwd=out[na]), out[-1]


def _gather_wait(state, after):
    shards, lands = state["shards"], state["lands"]
    na = len(lands)

    def body(*refs):
        ins, land = refs[:na], refs[na:2 * na]
        send_sems, fwd_sems, recv_d2d = refs[2 * na:2 * na + 3]
        x, y, c = _place()
        chips = [(1 - x, y), (x, 1 - y), (1 - x, 1 - y)]
        for t in range(na):
            mine = land[t].at[_block_of(x, y, c)]
            for k in range(4):
                pltpu.make_async_remote_copy(
                    src_ref=ins[t], dst_ref=mine, send_sem=send_sems.at[4 * t + k], recv_sem=recv_d2d.at[4 * t],
                    device_id=(x, y, c), device_id_type=MESH).wait_send()
            for j, chip in enumerate(chips):
                blk = land[t].at[_block_of(*chip, c)]
                pltpu.make_async_remote_copy(
                    src_ref=blk, dst_ref=blk, send_sem=fwd_sems.at[3 * t + j], recv_sem=recv_d2d.at[4 * t + 1 + j],
                    device_id=(x, y, c), device_id_type=MESH).wait_send()
            for k, blk_id in enumerate([_block_of(x, y, 1 - c)] + [_block_of(*chip, 1 - c) for chip in chips]):
                blk = land[t].at[blk_id]
                pltpu.make_async_remote_copy(
                    src_ref=blk, dst_ref=blk, send_sem=send_sems.at[4 * t], recv_sem=recv_d2d.at[4 * t + k],
                    device_id=(x, y, c), device_id_type=MESH).wait_recv()

    out = pl.pallas_call(
        body, name="gather_wait",
        out_shape=(*_hbm_like(shards), *_hbm_like(lands)),
        in_specs=[HBM] * (2 * na) + [SEM, SEM, SEM, ANY],
        out_specs=tuple([HBM] * (2 * na)),
        input_output_aliases={i: i for i in range(2 * na)},
        compiler_params=SIDE_EFFECT)(*shards, *lands, state["send"], state["fwd"], state["d2d"], after)
    return out[na:]


def _to_sibling(srcs, lands, send_sems, recv_sems):
    x, y, c = _place()
    return [pltpu.make_async_remote_copy(
        src_ref=srcs[t].at[:, 1 - c], dst_ref=lands[t], send_sem=send_sems.at[t], recv_sem=recv_sems.at[t],
        device_id=(x, y, 1 - c), device_id_type=MESH) for t in range(len(srcs))]


def _to_chips(srcs, lands, send_sems, recv_sems):
    x, y, c = _place()
    copies = []
    for k in (1, 2, 3):
        px, py = x ^ (k >> 1), y ^ (k & 1)
        copies += [pltpu.make_async_remote_copy(
            src_ref=srcs[t].at[2 * px + py], dst_ref=lands[t].at[k - 1], send_sem=send_sems.at[3 * t + k - 1],
            recv_sem=recv_sems.at[3 * t + k - 1], device_id=(px, py, c), device_id_type=MESH) for t in range(len(srcs))]
    return copies


def _exchange_start(name, srcs, land_shapes, copies, per_array, after):
    na = len(srcs)
    lands = [_hbm(lax.empty(shp, a.dtype)) for shp, a in zip(land_shapes, srcs)]

    def body(*refs):
        token = refs[-1]
        for cp in copies(refs[:na], refs[na:2 * na], refs[2 * na + 1], refs[2 * na + 2]):
            cp.start()
        token[...] = jnp.zeros_like(token)

    out = pl.pallas_call(
        body, name=name,
        out_shape=(pltpu.SemaphoreType.DMA((na * per_array,)), pltpu.SemaphoreType.DMA((na * per_array,)),
                   *_hbm_like(srcs), *_hbm_like(lands), TOKEN),
        in_specs=[HBM] * (2 * na) + [ANY],
        out_specs=(SEM, SEM, *[HBM] * (2 * na), pl.BlockSpec(memory_space=pltpu.VMEM)),
        input_output_aliases={i: 2 + i for i in range(2 * na)},
        compiler_params=SIDE_EFFECT)(*[_hbm(a) for a in srcs], *lands, after)
    return dict(send=out[0], recv=out[1], srcs=out[2:2 + na], lands=out[2 + na:2 + 2 * na]), out[-1]


def _exchange_wait(name, state, copies, after):
    srcs, lands = state["srcs"], state["lands"]
    na = len(srcs)

    def body(*refs):
        for cp in copies(refs[:na], refs[na:2 * na], refs[2 * na], refs[2 * na + 1]):
            cp.wait_send()
            cp.wait_recv()

    out = pl.pallas_call(
        body, name=name,
        out_shape=(*_hbm_like(srcs), *_hbm_like(lands)),
        in_specs=[HBM] * (2 * na) + [SEM, SEM, ANY],
        out_specs=tuple([HBM] * (2 * na)),
        input_output_aliases={i: i for i in range(2 * na)},
        compiler_params=SIDE_EFFECT)(*srcs, *lands, state["send"], state["recv"], after)
    return out[na:]


def _adamw_math(w, g, m, v):
    m = ADAM_B1 * m + (1.0 - ADAM_B1) * g
    v = ADAM_B2 * v + (1.0 - ADAM_B2) * (g * g)
    m_hat = m / (1.0 - ADAM_B1 ** ADAM_STEP)
    v_hat = v / (1.0 - ADAM_B2 ** ADAM_STEP)
    return -ADAM_LR * (m_hat / (jnp.sqrt(v_hat) + ADAM_EPS) + ADAM_WD * w), m, v


def _adamw(own, others, w, m, v, tr, name):
    rows, cols = w.shape
    blk = pl.BlockSpec((tr, cols), lambda i: (i, 0))

    def body(own_ref, oth_ref, w_ref, m_ref, v_ref, g_ref, d_ref, nm_ref, nv_ref):
        g = own_ref[...]
        for k in range(3):
            g = g + oth_ref[k].astype(F32)
        g_ref[...] = g
        d_ref[...], nm_ref[...], nv_ref[...] = _adamw_math(w_ref[...], g, m_ref[...], v_ref[...])

    out = jax.ShapeDtypeStruct((rows, cols), F32)
    return pl.pallas_call(
        body, name=name, out_shape=(out, out, out, out), grid=(rows // tr,),
        in_specs=[blk, pl.BlockSpec((3, tr, cols), lambda i: (0, i, 0)), blk, blk, blk],
        out_specs=(blk, blk, blk, blk),
        compiler_params=_params("parallel"))(own, others, w, m, v)


def _adamw_slab(w, g, m, v):
    def body(w_ref, g_ref, m_ref, v_ref, d_ref, nm_ref, nv_ref):
        d_ref[...], nm_ref[...], nv_ref[...] = _adamw_math(w_ref[...], g_ref[...], m_ref[...], v_ref[...])

    out = jax.ShapeDtypeStruct(w.shape, F32)
    vmem = pl.BlockSpec(memory_space=pltpu.VMEM)
    return pl.pallas_call(body, name="adamw_small", out_shape=(out, out, out),
                          in_specs=[vmem] * 4, out_specs=(vmem, vmem, vmem))(w, g, m, v)


def _row(v, width=D_MODEL):
    v = v.reshape(1, -1)
    return jnp.pad(v, ((0, 0), (0, width - v.shape[1])))


def _tables(s, gq, gk, conv_w):
    gq2 = jnp.tile(gq.reshape(1, HEAD), (1, 2))
    gk2 = jnp.tile(gk.reshape(1, HEAD), (1, 2))
    conv_wp = jnp.pad(conv_w, ((0, SUBLANES - conv_w.shape[0]), (0, 0)))
    return _rope_tables(s), gq2, gk2, conv_wp


def _forward_in(x, g1, w_qkv, w_rest):
    s = x.shape[0]
    h, ht, z_a = _fwd_in_a(x, g1, w_qkv, min(512, s))
    z_b = _mm_nn(h, w_rest, min(512, s), 1024, "fwd_in_b")
    return ht, z_a, z_b


def _forward_attn(z_a, z_b, rope, gq2, gk2, conv_wp, sinks):
    s = z_a.shape[0]
    qn, k2, v2 = _qk_prep(z_a, *rope, gq2, gk2, min(256, s))
    a, mix, mixt = _attn_fwd(qn, k2, v2, z_b, conv_wp, sinks)
    return qn, k2, v2, a, mix, mixt


def _forward_out(x, p, target, mix, mixt, w_out, g2, w_pg, b_pg, w_pp, g3):
    s = x.shape[0]
    tm = min(512, s)
    x1, hn2, hn2t = _fwd_out(mix, w_out, x, g2, tm)
    dy, dgp, dt, pt, acc_ple = _ple(hn2, w_pg, b_pg, p, w_pp, g3, x1, target, min(256, s))
    dx1, dx1b, acc_g2 = _gate_bwd(dgp, w_pg, x1, dy, g2, tm)
    gw_out = _mm_grad(mixt, [dx1b], 512, "grad_w_out")
    gw_pg = _mm_grad(hn2t, [dgp], 512, "grad_w_ple_gate")
    gw_pp = _mm_grad(pt, [dt], 512, "grad_w_ple_proj")
    return dx1, dx1b, (gw_out, gw_pg, gw_pp), acc_ple, acc_g2


def _backward_attn(dmix, ht, z_a, z_b, qn, k2, v2, a, rope, gq2, gk2, conv_wp, sinks):
    dq, dkc, dkp, dvc, dvp, dz_b, acc_attn = _attn_bwd(qn, k2, v2, a, z_b, dmix, conv_wp, sinks)
    dz_a, acc_qk = _qkv_bwd(z_a, dq, dkc, dkp, dvc, dvp, *rope, gq2, gk2)
    gw_in = _mm_grad(ht, [dz_a, dz_b], 512, "grad_w_in")
    return dz_a, dz_b, gw_in, acc_attn, acc_qk


def _small_rows(acc_g1, acc_g2, acc_ple, acc_qk, acc_attn):
    fold = lambda v: _row((v[:HEAD] + v[HEAD:]))
    return [acc_g1[0:1], acc_g2[0:1], acc_ple[0:1], acc_ple[1:2], fold(acc_qk[0]), fold(acc_qk[1]),
            _row(acc_attn[0, :N_Q_HEADS]), _row(acc_attn[1]), _row(acc_attn[2]), _row(acc_attn[3]), acc_ple[2:3]]


def _local_step(x, p, target, g1, w_qkv, w_rest, gq, gk, sinks, conv_w, w_out, g2, w_pg, b_pg, w_pp, g3):
    rope, gq2, gk2, conv_wp = _tables(x.shape[0], gq, gk, conv_w)
    ht, z_a, z_b = _forward_in(x, g1, w_qkv, w_rest)
    qn, k2, v2, a, mix, mixt = _forward_attn(z_a, z_b, rope, gq2, gk2, conv_wp, sinks)
    dx1, dx1b, (gw_out, gw_pg, gw_pp), acc_ple, acc_g2 = _forward_out(
        x, p, target, mix, mixt, w_out, g2, w_pg, b_pg, w_pp, g3)
    dmix = _mm_nt(dx1b, w_out, min(512, x.shape[0]), "out_bwd")
    dz_a, dz_b, gw_in, acc_attn, acc_qk = _backward_attn(
        dmix, ht, z_a, z_b, qn, k2, v2, a, rope, gq2, gk2, conv_wp, sinks)
    grad_x, acc_g1 = _in_bwd(dz_a, dz_b, w_qkv, w_rest, x, dx1, g1, min(512, x.shape[0]))
    return grad_x, (gw_in, gw_out, gw_pg, gw_pp), _small_rows(acc_g1, acc_g2, acc_ple, acc_qk, acc_attn)


ROW_CONV, ROW_LOSS = 7, 10


def _slab(rows):
    rows = list(rows)
    return jnp.concatenate(rows + [jnp.zeros((SLAB_ROWS - len(rows), D_MODEL), F32)], axis=0)


def _by_owner(g):
    return g.reshape((4, 2) + g.shape[1:])


def kernel(x, p, norm_gain, w_in, q_norm_gain, k_norm_gain, attn_sinks, conv_w, w_out, ple_gate_norm_gain, w_ple_gate, b_ple_gate, w_ple_proj, ple_norm_gain, loss_target, m_norm_gain, m_w_in, m_q_norm_gain, m_k_norm_gain, m_attn_sinks, m_conv_w, m_w_out, m_ple_gate_norm_gain, m_w_ple_gate, m_b_ple_gate, m_w_ple_proj, m_ple_norm_gain, v_norm_gain, v_w_in, v_q_norm_gain, v_k_norm_gain, v_attn_sinks, v_conv_w, v_w_out, v_ple_gate_norm_gain, v_w_ple_gate, v_b_ple_gate, v_w_ple_proj, v_ple_norm_gain):
    me = 4 * lax.axis_index("x") + 2 * lax.axis_index("y") + lax.axis_index("c")
    place = jnp.stack([lax.axis_index("c"), 2 * lax.axis_index("x") + lax.axis_index("y")]).astype(jnp.int32)
    conv_cols = conv_w.shape[2]
    xs, ps, target = x[0], p[0, 0], loss_target[0]
    zero = lambda token: token[0:1, 0:1]

    own_in = w_in[0].astype(BF16)
    own_late = [w_out[0].astype(BF16), w_ple_gate[0].astype(BF16), w_ple_proj[0].astype(BF16)]
    with_own = lambda gathered, own: lax.dynamic_update_slice(gathered, own[None], (me, 0, 0))
    (g_in,) = _all_gather([own_in])
    late, started = _gather_start(own_late, g_in)
    g_in = with_own(g_in, own_in)
    split = QKV_W - SHARD_IN
    w_qkv = jnp.concatenate([g_in[0], g_in[1][:, :split]], axis=1)
    w_rest = jnp.concatenate([g_in[1][:, split:]] + [g_in[d] for d in range(2, N_DEV)], axis=1)
    conv_rows = [lax.dynamic_update_slice(jnp.zeros((1, D_MODEL), F32), conv_w[0, t:t + 1], (0, conv_cols * me))
                 for t in range(3)]
    conv_full = _all_reduce_slab(_slab(conv_rows), "gather_conv_w")[0:3, :ATTN_W]
    rope, gq2, gk2, conv_wp = _tables(xs.shape[0], q_norm_gain[0], k_norm_gain[0], conv_full)

    g1 = norm_gain + zero(started)
    ht, z_a, z_b = _forward_in(xs, g1, w_qkv, w_rest)
    late, forwarded = _gather_forward(late, z_b)
    qn, k2, v2, a, mix, mixt = _forward_attn(z_a, z_b, rope, gq2 + zero(forwarded), gk2, conv_wp, attn_sinks)
    g_out, g_pg, g_pp = (with_own(g, own) for g, own in zip(_gather_wait(late, mix), own_late))
    w_out_f = g_out.reshape(D_MODEL, D_MODEL)
    w_pg_f = g_pg.reshape(D_MODEL, D_MODEL)
    w_pp_f = jnp.transpose(g_pp, (1, 0, 2)).reshape(PLE_DIM, D_MODEL)

    dx1, dx1b, (gw_out, gw_pg, gw_pp), acc_ple, acc_g2 = _forward_out(
        xs, ps, target, mix, mixt, w_out_f, ple_gate_norm_gain, w_pg_f, b_ple_gate, w_pp_f, ple_norm_gain)

    names = ("w_out", "w_ple_gate", "w_ple_proj")
    gw_pp_t = jnp.transpose(gw_pp.reshape(PLE_DIM, N_DEV, PLE_DIM), (1, 0, 2))
    grads = [_by_owner(gw_out.reshape(N_DEV, D_MODEL // N_DEV, D_MODEL)),
             _by_owner(gw_pg.reshape(N_DEV, D_MODEL // N_DEV, D_MODEL)), _by_owner(gw_pp_t)]
    pairs, _ = _exchange_start("pair_start", grads, [(4,) + g.shape[2:] for g in grads], _to_sibling, 1, dx1b)
    dmix = _mm_nt(dx1b, w_out_f, min(512, xs.shape[0]), "out_bwd")
    from_sibling = _exchange_wait("pair_wait", pairs, _to_sibling, dmix)
    sums = [_pair_sum(g, r, place, 256, "pair_sum_" + nm) for g, r, nm in zip(pairs["srcs"], from_sibling, names)]
    chips, sent = _exchange_start("chip_start", [pb for pb, _ in sums], [(3,) + pb.shape[1:] for pb, _ in sums],
                                  _to_chips, 3, sums[-1][1])

    dz_a, dz_b, gw_in, acc_attn, acc_qk = _backward_attn(
        dmix, ht, z_a, z_b, qn, k2, v2, a, rope, gq2, gk2, conv_wp, attn_sinks + zero(sent))

    gw_in_t = [_by_owner(jnp.transpose(gw_in.reshape(D_MODEL, N_DEV, SHARD_IN), (1, 0, 2)))]
    pairs_in, _ = _exchange_start("pair_start_w_in", gw_in_t, [(4,) + gw_in_t[0].shape[2:]], _to_sibling, 1, gw_in)
    from_chips = _exchange_wait("chip_wait", chips, _to_chips, gw_in)
    big = {}
    for (_, own), oth, w, m, v, nm in zip(sums, from_chips, (w_out, w_ple_gate, w_ple_proj),
                                          (m_w_out, m_w_ple_gate, m_w_ple_proj),
                                          (v_w_out, v_w_ple_gate, v_w_ple_proj), names):
        big[nm] = [t[None] for t in _adamw(own, oth, w[0], m[0], v[0], 256, "adamw_" + nm)]

    (from_sibling_in,) = _exchange_wait("pair_wait_w_in", pairs_in, _to_sibling, big[names[-1]][0])
    pb_in, own_in = _pair_sum(pairs_in["srcs"][0], from_sibling_in, place, 256, "pair_sum_w_in")
    chips_in, sent_in = _exchange_start("chip_start_w_in", [pb_in], [(3,) + pb_in.shape[1:]], _to_chips, 3, own_in)
    grad_x, acc_g1 = _in_bwd(dz_a, dz_b, w_qkv, w_rest, xs, dx1, norm_gain + zero(sent_in), min(512, xs.shape[0]))
    (from_chips_in,) = _exchange_wait("chip_wait_w_in", chips_in, _to_chips, grad_x)
    big["w_in"] = [t[None] for t in _adamw(own_in, from_chips_in, w_in[0], m_w_in[0], v_w_in[0], 256, "adamw_w_in")]

    red = _all_reduce_slab(_slab(_small_rows(acc_g1, acc_g2, acc_ple, acc_qk, acc_attn)), "reduce_small")
    loss = jnp.sum(red[ROW_LOSS])
    g_conv = [lax.dynamic_slice(red[ROW_CONV + t:ROW_CONV + t + 1], (0, conv_cols * me), (1, conv_cols))
              for t in range(3)]
    small = [norm_gain, ple_gate_norm_gain, b_ple_gate, ple_norm_gain, q_norm_gain, k_norm_gain, attn_sinks]
    small_m = [m_norm_gain, m_ple_gate_norm_gain, m_b_ple_gate, m_ple_norm_gain, m_q_norm_gain, m_k_norm_gain,
               m_attn_sinks]
    small_v = [v_norm_gain, v_ple_gate_norm_gain, v_b_ple_gate, v_ple_norm_gain, v_q_norm_gain, v_k_norm_gain,
               v_attn_sinks]
    pack = lambda vs, cw: _slab([_row(t) for t in vs] + [_row(cw[0, t]) for t in range(3)])
    g_slab = _slab([red[t:t + 1] for t in range(ROW_CONV)] + [_row(t) for t in g_conv])
    d_slab, m_slab, v_slab = _adamw_slab(pack(small, conv_w), g_slab, pack(small_m, m_conv_w), pack(small_v, v_conv_w))

    def unpack(slab_):
        outs = [slab_[t:t + 1, :w.shape[1]] for t, w in enumerate(small)]
        return outs, slab_[ROW_CONV:ROW_CONV + 3, :conv_cols][None]

    (g_s, g_cv), (d_s, d_cv), (m_s, m_cv), (v_s, v_cv) = (unpack(t) for t in (g_slab, d_slab, m_slab, v_slab))

    def order(sm, cv, k):
        return [sm[0], big["w_in"][k], sm[4], sm[5], sm[6], cv, big["w_out"][k], sm[1], big["w_ple_gate"][k], sm[2],
                big["w_ple_proj"][k], sm[3]]

    return (loss, grad_x[None], *order(g_s, g_cv, 0), *order(d_s, d_cv, 1), *order(m_s, m_cv, 2),
            *order(v_s, v_cv, 3))
```

```python
import functools

import jax
import jax.numpy as jnp
from jax import lax
from jax.experimental import pallas as pl
from jax.experimental.pallas import tpu as pltpu

F32, BF16 = jnp.float32, jnp.bfloat16

D_MODEL = 2048
PLE_DIM = 256
ATTN_W = 1024
HEAD = 64
N_Q_HEADS = 16
KV_W = 256
QKV_W = ATTN_W + 2 * KV_W
REST_W = 5 * 1024
IN_W = QKV_W + REST_W
K2_W = 4 * 128
ROT = 16
ROPE_THETA = 500000.0
EPS = 1e-6
NEG_INF = -1e30
BLK = 128
LANES = 128
SUBLANES = 8
N_DEV = 8
SHARD_IN = IN_W // N_DEV
SLAB_ROWS = 16
SUB_ROWS = 128
V7X_VMEM_LIMIT = 52 * 1024 * 1024

ADAM_LR, ADAM_B1, ADAM_B2, ADAM_EPS, ADAM_WD, ADAM_STEP = 0.001, 0.9, 0.999, 1e-08, 0.01, 10
MESH = pl.DeviceIdType.MESH


def _params(*semantics):
    return pltpu.CompilerParams(dimension_semantics=semantics, vmem_limit_bytes=V7X_VMEM_LIMIT)


ANY = pl.BlockSpec(memory_space=pl.ANY)


def _resident(shape):
    return pl.BlockSpec(shape, lambda *_: (0,) * len(shape), pipeline_mode=pl.Buffered(1))


def _dot(a, b):
    return jnp.dot(a, b, preferred_element_type=F32)


def _dot_nt(a, b):
    return lax.dot_general(a, b, (((1,), (1,)), ((), ())), preferred_element_type=F32)


def _rms(xf):
    r = lax.rsqrt(jnp.mean(xf * xf, axis=-1, keepdims=True) + EPS)
    return xf * r, r


def _rms_bwd(dxn, xn, r):
    return r * (dxn - xn * jnp.mean(dxn * xn, axis=-1, keepdims=True))


def _sig(g):
    return jax.nn.sigmoid(g)


def _dsilu(g, sg):
    return sg * (1.0 + g * (1.0 - sg))


def _low_half(shape):
    return lax.broadcasted_iota(jnp.int32, shape, len(shape) - 1) < HEAD


def _half_sums(v):
    lo = _low_half(v.shape)
    s_lo = jnp.sum(jnp.where(lo, v, 0.0), axis=-1, keepdims=True)
    s_hi = jnp.sum(jnp.where(lo, 0.0, v), axis=-1, keepdims=True)
    return jnp.where(lo, s_lo, s_hi)


def _rope(v, a, bm, bp):
    return v * a + pltpu.roll(v, LANES - ROT // 2, 1) * bm + pltpu.roll(v, ROT // 2, 1) * bp


def _rope_t(dy, a, bm, bp):
    return dy * a + pltpu.roll(dy * bm, ROT // 2, 1) + pltpu.roll(dy * bp, LANES - ROT // 2, 1)


def _dup_halves(v):
    lo = _low_half(v.shape)
    a = jnp.where(lo, v, 0.0)
    b = jnp.where(lo, 0.0, v)
    return a + pltpu.roll(a, HEAD, 1), b + pltpu.roll(b, HEAD, 1)


def _rope_tables(s):
    half = ROT // 2
    lane = lax.broadcasted_iota(jnp.int32, (s, LANES), 1) % HEAD
    pos = lax.broadcasted_iota(jnp.int32, (s, LANES), 0).astype(F32)
    inv_freq = jnp.power(jnp.float32(ROPE_THETA), -(lane % half).astype(F32) * 2.0 / ROT)
    ang = pos * inv_freq
    cos, sin = jnp.cos(ang), jnp.sin(ang)
    a = jnp.where(lane < ROT, cos, 1.0)
    bm = jnp.where(lane < half, -sin, 0.0)
    bp = jnp.where((lane >= half) & (lane < ROT), sin, 0.0)
    return a, bm, bp


def _fwd_in_a(x, g1, w_qkv, tm):
    s = x.shape[0]

    def body(x_ref, g_ref, w_ref, h_ref, ht_ref, z_ref):
        xn, _ = _rms(x_ref[...])
        h = (xn * g_ref[...]).astype(BF16)
        h_ref[...] = h
        ht_ref[...] = h.T
        z_ref[...] = _dot(h, w_ref[...])

    return pl.pallas_call(
        body, name="fwd_in_a",
        out_shape=(jax.ShapeDtypeStruct((s, D_MODEL), BF16), jax.ShapeDtypeStruct((D_MODEL, s), BF16),
                   jax.ShapeDtypeStruct((s, QKV_W), F32)),
        grid=(s // tm,),
        in_specs=[pl.BlockSpec((tm, D_MODEL), lambda i: (i, 0)),
                  pl.BlockSpec((1, D_MODEL), lambda i: (0, 0)),
                  _resident((D_MODEL, QKV_W))],
        out_specs=(pl.BlockSpec((tm, D_MODEL), lambda i: (i, 0)),
                   pl.BlockSpec((D_MODEL, tm), lambda i: (0, i)),
                   pl.BlockSpec((tm, QKV_W), lambda i: (i, 0))),
        compiler_params=_params("parallel"))(x, g1, w_qkv)


def _mm_nn(a, b, tm, tn, name):
    m, k = a.shape
    n = b.shape[1]

    def body(a_ref, b_ref, o_ref):
        o_ref[...] = _dot(a_ref[...], b_ref[...])

    return pl.pallas_call(
        body, name=name,
        out_shape=jax.ShapeDtypeStruct((m, n), F32),
        grid=(n // tn, m // tm),
        in_specs=[pl.BlockSpec((tm, k), lambda j, i: (i, 0)),
                  pl.BlockSpec((k, tn), lambda j, i: (0, j))],
        out_specs=pl.BlockSpec((tm, tn), lambda j, i: (i, j)),
        compiler_params=_params("parallel", "parallel"))(a, b)


def _qk_prep(z_a, ra, rbm, rbp, gq2, gk2, tm):
    s = z_a.shape[0]

    def body(z_ref, a_ref, bm_ref, bp_ref, gq_ref, gk_ref, q_ref, k2_ref, v2_ref):
        a, bm, bp = a_ref[...], bm_ref[...], bp_ref[...]
        for r in range(ATTN_W // LANES):
            x = z_ref[:, LANES * r:LANES * (r + 1)]
            rr = lax.rsqrt(_half_sums(x * x) * (1.0 / HEAD) + EPS)
            q_ref[:, LANES * r:LANES * (r + 1)] = _rope(x * rr * gq_ref[...], a, bm, bp).astype(BF16)
        for m in range(KV_W // LANES):
            x = z_ref[:, ATTN_W + LANES * m:ATTN_W + LANES * (m + 1)]
            rr = lax.rsqrt(_half_sums(x * x) * (1.0 / HEAD) + EPS)
            k_lo, k_hi = _dup_halves(_rope(x * rr * gk_ref[...], a, bm, bp))
            k2_ref[:, 2 * LANES * m:2 * LANES * m + LANES] = k_lo.astype(BF16)
            k2_ref[:, 2 * LANES * m + LANES:2 * LANES * (m + 1)] = k_hi.astype(BF16)
            v_lo, v_hi = _dup_halves(z_ref[:, ATTN_W + KV_W + LANES * m:ATTN_W + KV_W + LANES * (m + 1)])
            v2_ref[:, 2 * LANES * m:2 * LANES * m + LANES] = v_lo.astype(BF16)
            v2_ref[:, 2 * LANES * m + LANES:2 * LANES * (m + 1)] = v_hi.astype(BF16)

    row = lambda w: pl.BlockSpec((tm, w), lambda i: (i, 0))
    one = pl.BlockSpec((1, LANES), lambda i: (0, 0))
    return pl.pallas_call(
        body, name="qk_prep",
        out_shape=(jax.ShapeDtypeStruct((s, ATTN_W), BF16), jax.ShapeDtypeStruct((s, K2_W), BF16),
                   jax.ShapeDtypeStruct((s, K2_W), BF16)),
        grid=(s // tm,),
        in_specs=[row(QKV_W), row(LANES), row(LANES), row(LANES), one, one],
        out_specs=(row(ATTN_W), row(K2_W), row(K2_W)),
        compiler_params=_params("parallel"))(z_a, ra, rbm, rbp, gq2, gk2)


GROUP = 4


def _window_mask(n):
    row = lax.broadcasted_iota(jnp.int32, (GROUP * BLK, 2 * BLK), 0) % BLK
    col = lax.broadcasted_iota(jnp.int32, (GROUP * BLK, 2 * BLK), 1)
    return (col > row) & (col <= row + BLK) & ((col >= BLK) | (n > 0))


def _stack_heads(pairs, zero):
    lo = _low_half(pairs[0].shape)
    parts = []
    for v in pairs:
        parts += [jnp.where(lo, v, zero), jnp.where(lo, zero, v)]
    return jnp.concatenate(parts, axis=0)


def _unstack_heads(v4):
    lo = _low_half((BLK, LANES))
    return [jnp.where(lo, v4[2 * i * BLK:(2 * i + 1) * BLK], v4[(2 * i + 1) * BLK:(2 * i + 2) * BLK]) for i in range(2)]


def _group_sinks(sink_ref, kvh):
    slot = lax.broadcasted_iota(jnp.int32, (GROUP * BLK, 1), 0) // BLK
    col = jnp.zeros((GROUP * BLK, 1), F32)
    for i in range(GROUP):
        col = jnp.where(slot == i, sink_ref[0, GROUP * kvh + i], col)
    return col, slot


def _head_probs(qm, kw, valid, sink):
    sc = jnp.where(valid, _dot_nt(qm, kw) * (HEAD ** -0.5), NEG_INF)
    mx = jnp.maximum(jnp.max(sc, axis=-1, keepdims=True), sink)
    ex = jnp.exp(sc - mx)
    den = jnp.sum(ex, axis=-1, keepdims=True) + jnp.exp(sink - mx)
    return ex / den, mx, den


def _conv_fwd(zb_ref, zbp_ref, cw_ref, ext_ref, n):
    u = zb_ref[:, 2048:3072] * zb_ref[:, 3072:4096]
    pu = zbp_ref[:, 2048:3072] * zbp_ref[:, 3072:4096]
    ext_ref[0:SUBLANES, :] = jnp.where(n > 0, pu, 0.0)
    ext_ref[SUBLANES:SUBLANES + BLK, :] = u
    um1 = ext_ref[SUBLANES - 1:SUBLANES - 1 + BLK, :]
    um2 = ext_ref[SUBLANES - 2:SUBLANES - 2 + BLK, :]
    cv = cw_ref[0:1, :] * um2 + cw_ref[1:2, :] * um1 + cw_ref[2:3, :] * u
    return u, um1, um2, cv


def _prev_rows(n):
    return (jnp.maximum(n * (BLK // SUBLANES) - 1, 0), 0)


def _attn_fwd(qn, k2, v2, z_b, conv_wp, sinks):
    s = qn.shape[0]
    nb = s // BLK

    def body(sink_ref, q_ref, kc_ref, kp_ref, vc_ref, vp_ref, zb_ref, zbp_ref, cw_ref, a_ref, mix_ref, mixt_ref,
             ext_ref):
        n = pl.program_id(0)
        valid = _window_mask(n)
        for kvh in range(K2_W // LANES):
            cols = slice(LANES * kvh, LANES * (kvh + 1))
            kw = jnp.concatenate([kp_ref[:, cols], kc_ref[:, cols]], axis=0)
            vw = jnp.concatenate([vp_ref[:, cols], vc_ref[:, cols]], axis=0)
            blocks = [slice(LANES * r, LANES * (r + 1)) for r in (2 * kvh, 2 * kvh + 1)]
            q4 = _stack_heads([q_ref[:, rc] for rc in blocks], jnp.zeros((BLK, LANES), BF16))
            p, _, _ = _head_probs(q4, kw, valid, _group_sinks(sink_ref, kvh)[0])
            for rc, a in zip(blocks, _unstack_heads(_dot(p.astype(BF16), vw))):
                a_ref[:, rc] = a
                g = zb_ref[:, rc]
                mix_ref[:, rc] = (a * (g * _sig(g))).astype(BF16)
        _, _, _, cv = _conv_fwd(zb_ref, zbp_ref, cw_ref, ext_ref, n)
        gc = zb_ref[:, 4096:5120]
        mix_ref[:, ATTN_W:D_MODEL] = (zb_ref[:, 1024:2048] * cv * (gc * _sig(gc))).astype(BF16)
        mixt_ref[...] = mix_ref[...].T

    cur = lambda w: pl.BlockSpec((BLK, w), lambda n: (n, 0))
    prev = lambda w: pl.BlockSpec((BLK, w), lambda n: (jnp.maximum(n - 1, 0), 0))
    return pl.pallas_call(
        body, name="attn_fwd",
        out_shape=(jax.ShapeDtypeStruct((s, ATTN_W), F32), jax.ShapeDtypeStruct((s, D_MODEL), BF16),
                   jax.ShapeDtypeStruct((D_MODEL, s), BF16)),
        grid=(nb,),
        in_specs=[pl.BlockSpec(memory_space=pltpu.SMEM),
                  cur(ATTN_W), cur(K2_W), prev(K2_W), cur(K2_W), prev(K2_W), cur(REST_W),
                  pl.BlockSpec((SUBLANES, REST_W), _prev_rows),
                  pl.BlockSpec((SUBLANES, ATTN_W), lambda n: (0, 0))],
        out_specs=(cur(ATTN_W), cur(D_MODEL), pl.BlockSpec((D_MODEL, BLK), lambda n: (0, n))),
        scratch_shapes=[pltpu.VMEM((BLK + 2 * SUBLANES, ATTN_W), F32)],
        compiler_params=_params("parallel"))(sinks, qn, k2, k2, v2, v2, z_b, z_b, conv_wp)


def _fwd_out(mix, w_out, x, g2, tm):
    s = x.shape[0]

    def body(m_ref, w_ref, x_ref, g_ref, x1_ref, h_ref, ht_ref):
        x1 = x_ref[...] + _dot(m_ref[...], w_ref[...])
        x1_ref[...] = x1
        xn, _ = _rms(x1)
        h = (xn * g_ref[...]).astype(BF16)
        h_ref[...] = h
        ht_ref[...] = h.T

    row = pl.BlockSpec((tm, D_MODEL), lambda i: (i, 0))
    return pl.pallas_call(
        body, name="fwd_out",
        out_shape=(jax.ShapeDtypeStruct((s, D_MODEL), F32), jax.ShapeDtypeStruct((s, D_MODEL), BF16),
                   jax.ShapeDtypeStruct((D_MODEL, s), BF16)),
        grid=(s // tm,),
        in_specs=[row, _resident((D_MODEL, D_MODEL)), row, pl.BlockSpec((1, D_MODEL), lambda i: (0, 0))],
        out_specs=(row, row, pl.BlockSpec((D_MODEL, tm), lambda i: (0, i))),
        compiler_params=_params("parallel"))(mix, w_out, x, g2)


def _ple(hn2, w_pg, b_pg, p, w_pp, g3, x1, target, tm):
    s = x1.shape[0]

    def body(h_ref, wg_ref, b_ref, p_ref, wp_ref, g3_ref, x1_ref, t_ref, dy_ref, dgp_ref, dt_ref, pt_ref, acc_ref):
        gate = _sig(_dot(h_ref[...], wg_ref[...]) + b_ref[...])
        pb = p_ref[...].astype(BF16)
        pt_ref[...] = pb.T
        t = _dot(pb, wp_ref[...])
        tn, r3 = _rms(t)
        e = tn * g3_ref[...]
        diff = x1_ref[...] + gate * e - t_ref[...]
        dy = diff * (1.0 / D_MODEL)
        dy_ref[...] = dy
        dgp = dy * e * (gate * (1.0 - gate))
        dgp_ref[...] = dgp.astype(BF16)
        de = dy * gate
        dt_ref[...] = _rms_bwd(de * g3_ref[...], tn, r3).astype(BF16)

        @pl.when(pl.program_id(0) == 0)
        def _():
            acc_ref[...] = jnp.zeros_like(acc_ref)

        acc_ref[0:1, :] += jnp.sum(dgp, axis=0, keepdims=True)
        acc_ref[1:2, :] += jnp.sum(de * tn, axis=0, keepdims=True)
        acc_ref[2:3, :] += jnp.sum(diff * diff, axis=0, keepdims=True) * (0.5 / D_MODEL)

    row = pl.BlockSpec((tm, D_MODEL), lambda i: (i, 0))
    vec = pl.BlockSpec((1, D_MODEL), lambda i: (0, 0))
    return pl.pallas_call(
        body, name="ple",
        out_shape=(jax.ShapeDtypeStruct((s, D_MODEL), F32), jax.ShapeDtypeStruct((s, D_MODEL), BF16),
                   jax.ShapeDtypeStruct((s, D_MODEL), BF16), jax.ShapeDtypeStruct((PLE_DIM, s), BF16),
                   jax.ShapeDtypeStruct((SUBLANES, D_MODEL), F32)),
        grid=(s // tm,),
        in_specs=[row, _resident((D_MODEL, D_MODEL)), vec, pl.BlockSpec((tm, PLE_DIM), lambda i: (i, 0)),
                  _resident((PLE_DIM, D_MODEL)), vec, row, row],
        out_specs=(row, row, row, pl.BlockSpec((PLE_DIM, tm), lambda i: (0, i)),
                   pl.BlockSpec((SUBLANES, D_MODEL), lambda i: (0, 0))),
        compiler_params=_params("arbitrary"))(hn2, w_pg, b_pg, p, w_pp, g3, x1, target)


def _gate_bwd(dgp, w_pg, x1, dy, g2, tm):
    s = x1.shape[0]

    def body(d_ref, w_ref, x1_ref, dy_ref, g_ref, dx_ref, dxb_ref, acc_ref):
        dh = _dot_nt(d_ref[...], w_ref[...])
        xn, r = _rms(x1_ref[...])
        dx1 = dy_ref[...] + _rms_bwd(dh * g_ref[...], xn, r)
        dx_ref[...] = dx1
        dxb_ref[...] = dx1.astype(BF16)

        @pl.when(pl.program_id(0) == 0)
        def _():
            acc_ref[...] = jnp.zeros_like(acc_ref)

        acc_ref[0:1, :] += jnp.sum(dh * xn, axis=0, keepdims=True)

    row = pl.BlockSpec((tm, D_MODEL), lambda i: (i, 0))
    return pl.pallas_call(
        body, name="gate_bwd",
        out_shape=(jax.ShapeDtypeStruct((s, D_MODEL), F32), jax.ShapeDtypeStruct((s, D_MODEL), BF16),
                   jax.ShapeDtypeStruct((SUBLANES, D_MODEL), F32)),
        grid=(s // tm,),
        in_specs=[row, _resident((D_MODEL, D_MODEL)), row, row, pl.BlockSpec((1, D_MODEL), lambda i: (0, 0))],
        out_specs=(row, row, pl.BlockSpec((SUBLANES, D_MODEL), lambda i: (0, 0))),
        compiler_params=_params("arbitrary"))(dgp, w_pg, x1, dy, g2)


def _mm_nt(a, b, tm, name):
    m, k = a.shape
    n = b.shape[0]

    def body(a_ref, b_ref, o_ref):
        o_ref[...] = _dot_nt(a_ref[...], b_ref[...])

    return pl.pallas_call(
        body, name=name,
        out_shape=jax.ShapeDtypeStruct((m, n), F32),
        grid=(m // tm,),
        in_specs=[pl.BlockSpec((tm, k), lambda i: (i, 0)), _resident((n, k))],
        out_specs=pl.BlockSpec((tm, n), lambda i: (i, 0)),
        compiler_params=_params("parallel"))(a, b)


def _attn_bwd(qn, k2, v2, a, z_b, dmix, conv_wp, sinks):
    s = qn.shape[0]
    nb = s // BLK

    def body(sink_ref, q_ref, kc_ref, kp_ref, vc_ref, vp_ref, a_ref, zb_ref, zbp_ref, zbn_ref, dm_ref, dmn_ref,
             cw_ref, dq_ref, dkc_ref, dkp_ref, dvc_ref, dvp_ref, dzb_ref, acc_ref, ext_ref):
        n = pl.program_id(0)
        valid = _window_mask(n)
        lane = lax.broadcasted_iota(jnp.int32, (1, ATTN_W), 1)

        @pl.when(n == 0)
        def _():
            acc_ref[...] = jnp.zeros_like(acc_ref)

        dsink = jnp.zeros((1, ATTN_W), F32)
        for kvh in range(K2_W // LANES):
            cols = slice(LANES * kvh, LANES * (kvh + 1))
            kw = jnp.concatenate([kp_ref[:, cols], kc_ref[:, cols]], axis=0)
            vw = jnp.concatenate([vp_ref[:, cols], vc_ref[:, cols]], axis=0)
            blocks = [slice(LANES * r, LANES * (r + 1)) for r in (2 * kvh, 2 * kvh + 1)]
            das, avs = [], []
            for rc in blocks:
                g = zb_ref[:, rc]
                sg = _sig(g)
                dm = dm_ref[:, rc]
                av = a_ref[:, rc]
                das.append(dm * (g * sg))
                avs += [av, av]
                dzb_ref[:, rc] = (dm * av * _dsilu(g, sg)).astype(BF16)
            q4 = _stack_heads([q_ref[:, rc] for rc in blocks], jnp.zeros((BLK, LANES), BF16))
            sink, slot = _group_sinks(sink_ref, kvh)
            p, mx, den = _head_probs(q4, kw, valid, sink)
            do4 = _stack_heads(das, 0.0)
            delta = jnp.sum(do4 * jnp.concatenate(avs, axis=0), axis=-1, keepdims=True)
            dob = do4.astype(BF16)
            ds = p * (_dot_nt(dob, vw) - delta) * (HEAD ** -0.5)
            for rc, dq in zip(blocks, _unstack_heads(_dot(ds.astype(BF16), kw))):
                dq_ref[:, rc] = dq
            dk2 = _dot(ds.T.astype(BF16), q4)
            dv2 = _dot(p.T.astype(BF16), dob)
            dkp_ref[:, cols] = dk2[0:BLK]
            dkc_ref[:, cols] = dk2[BLK:2 * BLK]
            dvp_ref[:, cols] = dv2[0:BLK]
            dvc_ref[:, cols] = dv2[BLK:2 * BLK]
            dsk = jnp.exp(sink - mx) / den * delta
            for i in range(GROUP):
                dsink = dsink - jnp.where(lane == GROUP * kvh + i,
                                          jnp.sum(jnp.where(slot == i, dsk, 0.0), axis=0, keepdims=True), 0.0)
        acc_ref[0:1, :] += dsink

        u, um1, um2, cv = _conv_fwd(zb_ref, zbp_ref, cw_ref, ext_ref, n)
        cb = zb_ref[:, 1024:2048]
        gc = zb_ref[:, 4096:5120]
        sgc = _sig(gc)
        dmc = dm_ref[:, ATTN_W:D_MODEL]
        t = dmc * (gc * sgc)
        dcv = t * cb
        dzb_ref[:, 1024:2048] = (t * cv).astype(BF16)
        dzb_ref[:, 4096:5120] = (dmc * cb * cv * _dsilu(gc, sgc)).astype(BF16)
        gcn = zbn_ref[:, 4096:5120]
        dcvn = dmn_ref[:, ATTN_W:D_MODEL] * (gcn * _sig(gcn)) * zbn_ref[:, 1024:2048]
        ext_ref[0:BLK, :] = dcv
        ext_ref[BLK:BLK + SUBLANES, :] = jnp.where(n < nb - 1, dcvn, 0.0)
        du = (cw_ref[2:3, :] * dcv + cw_ref[1:2, :] * ext_ref[1:1 + BLK, :]
              + cw_ref[0:1, :] * ext_ref[2:2 + BLK, :])
        dzb_ref[:, 2048:3072] = (du * zb_ref[:, 3072:4096]).astype(BF16)
        dzb_ref[:, 3072:4096] = (du * zb_ref[:, 2048:3072]).astype(BF16)
        acc_ref[1:2, :] += jnp.sum(dcv * um2, axis=0, keepdims=True)
        acc_ref[2:3, :] += jnp.sum(dcv * um1, axis=0, keepdims=True)
        acc_ref[3:4, :] += jnp.sum(dcv * u, axis=0, keepdims=True)

    cur = lambda w: pl.BlockSpec((BLK, w), lambda n: (n, 0))
    prev = lambda w: pl.BlockSpec((BLK, w), lambda n: (jnp.maximum(n - 1, 0), 0))
    nxt = lambda w: pl.BlockSpec(
        (SUBLANES, w), lambda n: (jnp.minimum((n + 1) * (BLK // SUBLANES), nb * (BLK // SUBLANES) - 1), 0))
    f32 = lambda w: jax.ShapeDtypeStruct((s, w), F32)
    return pl.pallas_call(
        body, name="attn_bwd",
        out_shape=(f32(ATTN_W), f32(K2_W), f32(K2_W), f32(K2_W), f32(K2_W),
                   jax.ShapeDtypeStruct((s, REST_W), BF16), jax.ShapeDtypeStruct((SUBLANES, ATTN_W), F32)),
        grid=(nb,),
        in_specs=[pl.BlockSpec(memory_space=pltpu.SMEM),
                  cur(ATTN_W), cur(K2_W), prev(K2_W), cur(K2_W), prev(K2_W), cur(ATTN_W), cur(REST_W),
                  pl.BlockSpec((SUBLANES, REST_W), _prev_rows), nxt(REST_W), cur(D_MODEL), nxt(D_MODEL),
                  pl.BlockSpec((SUBLANES, ATTN_W), lambda n: (0, 0))],
        out_specs=(cur(ATTN_W), cur(K2_W), cur(K2_W), cur(K2_W), cur(K2_W), cur(REST_W),
                   pl.BlockSpec((SUBLANES, ATTN_W), lambda n: (0, 0))),
        scratch_shapes=[pltpu.VMEM((BLK + 2 * SUBLANES, ATTN_W), F32)],
        compiler_params=_params("arbitrary"))(sinks, qn, k2, k2, v2, v2, a, z_b, z_b, z_b, dmix, dmix, conv_wp)


def _qkv_bwd(z_a, dq, dkc, dkp, dvc, dvp, ra, rbm, rbp, gq2, gk2):
    s = z_a.shape[0]
    nb = s // BLK

    def body(z_ref, dq_ref, dkc_ref, dkp_ref, dvc_ref, dvp_ref, a_ref, bm_ref, bp_ref, gq_ref, gk_ref,
             dz_ref, acc_ref):
        n = pl.program_id(0)
        a, bm, bp = a_ref[...], bm_ref[...], bp_ref[...]
        lo = _low_half((BLK, LANES))
        last = n == nb - 1

        @pl.when(n == 0)
        def _():
            acc_ref[...] = jnp.zeros_like(acc_ref)

        def norm_bwd(x, dy, gain):
            rr = lax.rsqrt(_half_sums(x * x) * (1.0 / HEAD) + EPS)
            xh = x * rr
            dxg = _rope_t(dy, a, bm, bp)
            dxh = dxg * gain
            dx = rr * (dxh - xh * (_half_sums(dxh * xh) * (1.0 / HEAD)))
            return dx, jnp.sum(dxg * xh, axis=0, keepdims=True)

        def folded(cur_ref, prev_ref, m):
            parts = []
            for h in (2 * m, 2 * m + 1):
                v = cur_ref[:, LANES * h:LANES * (h + 1)] + jnp.where(
                    last, 0.0, prev_ref[:, LANES * h:LANES * (h + 1)])
                parts.append(v + pltpu.roll(v, HEAD, 1))
            return jnp.where(lo, parts[0], parts[1])

        gq_acc = jnp.zeros((1, LANES), F32)
        for r in range(ATTN_W // LANES):
            rc = slice(LANES * r, LANES * (r + 1))
            dx, gg = norm_bwd(z_ref[:, rc], dq_ref[:, rc], gq_ref[...])
            dz_ref[:, rc] = dx.astype(BF16)
            gq_acc = gq_acc + gg
        acc_ref[0:1, :] += gq_acc
        gk_acc = jnp.zeros((1, LANES), F32)
        for m in range(KV_W // LANES):
            kc = slice(ATTN_W + LANES * m, ATTN_W + LANES * (m + 1))
            dx, gg = norm_bwd(z_ref[:, kc], folded(dkc_ref, dkp_ref, m), gk_ref[...])
            dz_ref[:, kc] = dx.astype(BF16)
            gk_acc = gk_acc + gg
            vc = slice(ATTN_W + KV_W + LANES * m, ATTN_W + KV_W + LANES * (m + 1))
            dz_ref[:, vc] = folded(dvc_ref, dvp_ref, m).astype(BF16)
        acc_ref[1:2, :] += gk_acc

    cur = lambda w: pl.BlockSpec((BLK, w), lambda n: (n, 0))
    nxt = lambda w: pl.BlockSpec((BLK, w), lambda n: (jnp.minimum(n + 1, nb - 1), 0))
    one = pl.BlockSpec((1, LANES), lambda n: (0, 0))
    return pl.pallas_call(
        body, name="qkv_bwd",
        out_shape=(jax.ShapeDtypeStruct((s, QKV_W), BF16), jax.ShapeDtypeStruct((SUBLANES, LANES), F32)),
        grid=(nb,),
        in_specs=[cur(QKV_W), cur(ATTN_W), cur(K2_W), nxt(K2_W), cur(K2_W), nxt(K2_W),
                  cur(LANES), cur(LANES), cur(LANES), one, one],
        out_specs=(cur(QKV_W), pl.BlockSpec((SUBLANES, LANES), lambda n: (0, 0))),
        compiler_params=_params("arbitrary"))(z_a, dq, dkc, dkp, dvc, dvp, ra, rbm, rbp, gq2, gk2)


REST_CHUNK = 1280
N_REST_CHUNKS = REST_W // REST_CHUNK


def _in_bwd(dz_a, dz_b, w_qkv, w_rest, x, dx1, g1, tm):
    s = x.shape[0]
    nk = 1 + N_REST_CHUNKS

    def body(da_ref, db_ref, wa_ref, wb_ref, x_hbm, dx1_hbm, g_ref, gx_ref, acc_ref, x_buf, dx1_buf, sems):
        i, k = pl.program_id(0), pl.program_id(1)
        rows = pl.ds(pl.multiple_of(i * tm, tm), tm)
        fetch = [pltpu.make_async_copy(x_hbm.at[rows], x_buf, sems.at[0]),
                 pltpu.make_async_copy(dx1_hbm.at[rows], dx1_buf, sems.at[1])]

        sub = min(SUB_ROWS, tm)
        blocks = [slice(r, r + sub) for r in range(0, tm, sub)]

        @pl.when(k == 0)
        def _():
            for cp in fetch:
                cp.start()
            gx_ref[...] = _dot_nt(da_ref[...], wa_ref[...])

        @pl.when(k > 0)
        def _():
            gx_ref[...] += _dot_nt(db_ref[...], wb_ref[...])

        @pl.when((i == 0) & (k == 0))
        def _():
            acc_ref[...] = jnp.zeros_like(acc_ref)

        @pl.when(k == nk - 1)
        def _():
            for cp in fetch:
                cp.wait()
            for rb in blocks:
                dh = gx_ref[rb, :]
                xn, r = _rms(x_buf[rb, :])
                gx_ref[rb, :] = dx1_buf[rb, :] + _rms_bwd(dh * g_ref[...], xn, r)
                acc_ref[0:1, :] += jnp.sum(dh * xn, axis=0, keepdims=True)

    kb = lambda i, k: jnp.maximum(k - 1, 0)
    return pl.pallas_call(
        body, name="in_bwd",
        out_shape=(jax.ShapeDtypeStruct((s, D_MODEL), F32), jax.ShapeDtypeStruct((SUBLANES, D_MODEL), F32)),
        grid=(s // tm, nk),
        in_specs=[pl.BlockSpec((tm, QKV_W), lambda i, k: (i, 0)),
                  pl.BlockSpec((tm, REST_CHUNK), lambda i, k: (i, kb(i, k))),
                  _resident((D_MODEL, QKV_W)),
                  pl.BlockSpec((D_MODEL, REST_CHUNK), lambda i, k: (0, kb(i, k))),
                  ANY, ANY, pl.BlockSpec((1, D_MODEL), lambda i, k: (0, 0))],
        out_specs=(pl.BlockSpec((tm, D_MODEL), lambda i, k: (i, 0)),
                   pl.BlockSpec((SUBLANES, D_MODEL), lambda i, k: (0, 0))),
        scratch_shapes=[pltpu.VMEM((tm, D_MODEL), F32), pltpu.VMEM((tm, D_MODEL), F32),
                        pltpu.SemaphoreType.DMA((2,))],
        compiler_params=_params("arbitrary", "arbitrary"))(dz_a, dz_b, w_qkv, w_rest, x, dx1, g1)


def _mm_grad(at, bs, tn, name):
    m, kdim = at.shape
    nblk = [b.shape[1] // tn for b in bs]
    starts = [sum(nblk[:t]) for t in range(len(bs))]

    def body(a_ref, *refs):
        b_refs, o_ref = refs[:len(bs)], refs[len(bs)]
        j = pl.program_id(0)
        for t, b_ref in enumerate(b_refs):
            @pl.when((j >= starts[t]) & (j < starts[t] + nblk[t]))
            def _():
                o_ref[...] = _dot(a_ref[...], b_ref[...]).astype(BF16)

    def b_spec(t):
        return pl.BlockSpec((kdim, tn), lambda j: (0, jnp.clip(j - starts[t], 0, nblk[t] - 1)))

    return pl.pallas_call(
        body, name=name,
        out_shape=jax.ShapeDtypeStruct((m, sum(nblk) * tn), BF16),
        grid=(sum(nblk),),
        in_specs=[_resident((m, kdim))] + [b_spec(t) for t in range(len(bs))],
        out_specs=pl.BlockSpec((m, tn), lambda j: (0, j)),
        compiler_params=_params("parallel"))(at, *bs)


def _place():
    return lax.axis_index("x"), lax.axis_index("y"), lax.axis_index("c")


def _all_gather(shards):
    na = len(shards)

    def body(*refs):
        ins, outs = refs[:na], refs[na:2 * na]
        send_sems, recv_sems = refs[2 * na:]
        x, y, c = _place()
        me, sibling = (x, y, c), (x, y, 1 - c)
        chips = [(1 - x, y), (x, 1 - y), (1 - x, 1 - y)]

        def copy(t, k, block, to, src=None):
            dst = outs[t].at[4 * block[0] + 2 * block[1] + block[2]]
            return pltpu.make_async_remote_copy(
                src_ref=dst if src is None else src, dst_ref=dst, send_sem=send_sems.at[t, k],
                recv_sem=recv_sems.at[t, k], device_id=to, device_id_type=MESH)

        first = []
        for j, chip in enumerate(chips):
            first += [copy(t, 1 + j, me, (*chip, c), src=ins[t]) for t in range(na)]
        first += [copy(t, 0, me, sibling, src=ins[t]) for t in range(na)]
        for cp in first:
            cp.start()
        passed = []
        for j, chip in enumerate(chips):
            for t in range(na):
                copy(t, 1 + j, (*chip, c), me).wait_recv()
                passed.append(copy(t, 4 + j, (*chip, c), sibling))
                passed[-1].start()
        for t in range(na):
            copy(t, 0, sibling, me).wait_recv()
        for j, chip in enumerate(chips):
            for t in range(na):
                copy(t, 4 + j, (*chip, 1 - c), me).wait_recv()
        for cp in first + passed:
            cp.wait_send()

    return pl.pallas_call(
        body, name="all_gather_weights",
        out_shape=tuple(jax.ShapeDtypeStruct((N_DEV,) + a.shape, a.dtype) for a in shards),
        in_specs=[ANY] * na, out_specs=tuple([ANY] * na),
        scratch_shapes=[pltpu.SemaphoreType.DMA((na, 7)), pltpu.SemaphoreType.DMA((na, 7))])(*shards)


def _all_reduce_slab(slab, name):
    def body(in_ref, out_ref, gath_ref, send_sems, recv_sems):
        x, y, c = _place()
        me = 4 * x + 2 * y + c
        gath_ref[me] = in_ref[...]
        copies = []
        for k in range(1, N_DEV):
            peer = (x ^ (k >> 2), y ^ ((k >> 1) & 1), c ^ (k & 1))
            copies.append(pltpu.make_async_remote_copy(
                src_ref=in_ref, dst_ref=gath_ref.at[me], send_sem=send_sems.at[k - 1],
                recv_sem=recv_sems.at[k - 1], device_id=peer, device_id_type=MESH))
        for cp in copies:
            cp.start()
        for cp in copies:
            cp.wait_recv()
        for cp in copies:
            cp.wait_send()
        total = gath_ref[0]
        for d in range(1, N_DEV):
            total = total + gath_ref[d]
        out_ref[...] = total

    vmem = pl.BlockSpec(memory_space=pltpu.VMEM)
    return pl.pallas_call(
        body, name=name,
        out_shape=jax.ShapeDtypeStruct(slab.shape, F32),
        in_specs=[vmem], out_specs=vmem,
        scratch_shapes=[pltpu.VMEM((N_DEV,) + slab.shape, F32),
                        pltpu.SemaphoreType.DMA((N_DEV - 1,)), pltpu.SemaphoreType.DMA((N_DEV - 1,))])(slab)


def _pair_sum(g, r, place, tr, name):
    _, _, rows, cols = g.shape

    def body(place_ref, g_ref, r_ref, pb_ref, own_ref):
        tot = g_ref[0, 0].astype(F32) + r_ref[0].astype(F32)
        pb_ref[0] = tot.astype(BF16)

        @pl.when(pl.program_id(1) == place_ref[1])
        def _():
            own_ref[...] = tot

    grid_spec = pltpu.PrefetchScalarGridSpec(
        num_scalar_prefetch=1, grid=(rows // tr, 4),
        in_specs=[pl.BlockSpec((1, 1, tr, cols), lambda i, q, place_ref: (q, place_ref[0], i, 0)),
                  pl.BlockSpec((1, tr, cols), lambda i, q, place_ref: (q, i, 0))],
        out_specs=(pl.BlockSpec((1, tr, cols), lambda i, q, place_ref: (q, i, 0)),
                   pl.BlockSpec((tr, cols), lambda i, q, place_ref: (i, 0))))
    return pl.pallas_call(
        body, name=name, grid_spec=grid_spec,
        out_shape=(jax.ShapeDtypeStruct((4, rows, cols), BF16), jax.ShapeDtypeStruct((rows, cols), F32)),
        compiler_params=_params("arbitrary", "arbitrary"))(place, g, r)


HBM = pl.BlockSpec(memory_space=pltpu.HBM)
SEM = pl.BlockSpec(memory_space=pltpu.SEMAPHORE)
SIDE_EFFECT = pltpu.CompilerParams(has_side_effects=pltpu.SideEffectType.DATAFLOW_SIDE_EFFECTING)
TOKEN = jax.ShapeDtypeStruct((SUBLANES, LANES), F32)


def _hbm(a):
    return pltpu.with_memory_space_constraint(a, pltpu.HBM)


def _hbm_like(arrays):
    return tuple(pltpu.HBM(a.shape, a.dtype) for a in arrays)


def _block_of(px, py, pc):
    return 4 * px + 2 * py + pc


def _gather_start(shards, after):
    na = len(shards)
    lands = [_hbm(lax.empty((N_DEV,) + a.shape, a.dtype)) for a in shards]

    def body(*refs):
        ins, land = refs[:na], refs[na:2 * na]
        send_sems, recv_ici, recv_d2d = refs[2 * na + 1:2 * na + 4]
        token = refs[-1]
        x, y, c = _place()
        for k, peer in enumerate([(x, y, 1 - c), (1 - x, y, c), (x, 1 - y, c), (1 - x, 1 - y, c)]):
            for t in range(na):
                pltpu.make_async_remote_copy(
                    src_ref=ins[t], dst_ref=land[t].at[_block_of(x, y, c)], send_sem=send_sems.at[4 * t + k],
                    recv_sem=recv_d2d.at[4 * t] if k == 0 else recv_ici.at[3 * t + k - 1],
                    device_id=peer, device_id_type=MESH).start()
        token[...] = jnp.zeros_like(token)

    out = pl.pallas_call(
        body, name="gather_start",
        out_shape=(pltpu.SemaphoreType.DMA((4 * na,)), pltpu.SemaphoreType.DMA((3 * na,)),
                   pltpu.SemaphoreType.DMA((4 * na,)), *_hbm_like(shards), *_hbm_like(lands), TOKEN),
        in_specs=[HBM] * (2 * na) + [ANY],
        out_specs=(SEM, SEM, SEM, *[HBM] * (2 * na), pl.BlockSpec(memory_space=pltpu.VMEM)),
        input_output_aliases={i: 3 + i for i in range(2 * na)},
        compiler_params=SIDE_EFFECT)(*[_hbm(a) for a in shards], *lands, after)
    send_sems, recv_ici, recv_d2d = out[:3]
    state = dict(send=send_sems, ici=recv_ici, d2d=recv_d2d, shards=out[3:3 + na], lands=out[3 + na:3 + 2 * na])
    return state, out[-1]


def _gather_forward(state, after):
    lands = state["lands"]
    na = len(lands)

    def body(*refs):
        land = refs[:na]
        recv_ici, recv_d2d = refs[na], refs[na + 1]
        fwd_sems, token = refs[-2], refs[-1]
        x, y, c = _place()
        for j, chip in enumerate([(1 - x, y), (x, 1 - y), (1 - x, 1 - y)]):
            for t in range(na):
                blk = land[t].at[_block_of(*chip, c)]
                pltpu.make_async_remote_copy(
                    src_ref=blk, dst_ref=blk, send_sem=fwd_sems.at[3 * t + j], recv_sem=recv_ici.at[3 * t + j],
                    device_id=(x, y, c), device_id_type=MESH).wait_recv()
                pltpu.make_async_remote_copy(
                    src_ref=blk, dst_ref=blk, send_sem=fwd_sems.at[3 * t + j], recv_sem=recv_d2d.at[4 * t + 1 + j],
                    device_id=(x, y, 1 - c), device_id_type=MESH).start()
        token[...] = jnp.zeros_like(token)

    out = pl.pallas_call(
        body, name="gather_forward",
        out_shape=(*_hbm_like(lands), pltpu.SemaphoreType.DMA((3 * na,)), TOKEN),
        in_specs=[HBM] * na + [SEM, SEM, ANY],
        out_specs=(*[HBM] * na, SEM, pl.BlockSpec(memory_space=pltpu.VMEM)),
        input_output_aliases={i: i for i in range(na)},
        compiler_params=SIDE_EFFECT)(*lands, state["ici"], state["d2d"], after)
    return dict(state, lands=out[:na], fwd=out[na]), out[-1]


def _gather_wait(state, after):
    shards, lands = state["shards"], state["lands"]
    na = len(lands)

    def body(*refs):
        ins, land = refs[:na], refs[na:2 * na]
        send_sems, fwd_sems, recv_d2d = refs[2 * na:2 * na + 3]
        x, y, c = _place()
        chips = [(1 - x, y), (x, 1 - y), (1 - x, 1 - y)]
        for t in range(na):
            mine = land[t].at[_block_of(x, y, c)]
            for k in range(4):
                pltpu.make_async_remote_copy(
                    src_ref=ins[t], dst_ref=mine, send_sem=send_sems.at[4 * t + k], recv_sem=recv_d2d.at[4 * t],
                    device_id=(x, y, c), device_id_type=MESH).wait_send()
            for j, chip in enumerate(chips):
                blk = land[t].at[_block_of(*chip, c)]
                pltpu.make_async_remote_copy(
                    src_ref=blk, dst_ref=blk, send_sem=fwd_sems.at[3 * t + j], recv_sem=recv_d2d.at[4 * t + 1 + j],
                    device_id=(x, y, c), device_id_type=MESH).wait_send()
            for k, blk_id in enumerate([_block_of(x, y, 1 - c)] + [_block_of(*chip, 1 - c) for chip in chips]):
                blk = land[t].at[blk_id]
                pltpu.make_async_remote_copy(
                    src_ref=blk, dst_ref=blk, send_sem=send_sems.at[4 * t], recv_sem=recv_d2d.at[4 * t + k],
                    device_id=(x, y, c), device_id_type=MESH).wait_recv()

    out = pl.pallas_call(
        body, name="gather_wait",
        out_shape=(*_hbm_like(shards), *_hbm_like(lands)),
        in_specs=[HBM] * (2 * na) + [SEM, SEM, SEM, ANY],
        out_specs=tuple([HBM] * (2 * na)),
        input_output_aliases={i: i for i in range(2 * na)},
        compiler_params=SIDE_EFFECT)(*shards, *lands, state["send"], state["fwd"], state["d2d"], after)
    return out[na:]


def _to_sibling(srcs, lands, send_sems, recv_sems):
    x, y, c = _place()
    return [pltpu.make_async_remote_copy(
        src_ref=srcs[t].at[:, 1 - c], dst_ref=lands[t], send_sem=send_sems.at[t], recv_sem=recv_sems.at[t],
        device_id=(x, y, 1 - c), device_id_type=MESH) for t in range(len(srcs))]


def _to_chips(srcs, lands, send_sems, recv_sems):
    x, y, c = _place()
    copies = []
    for k in (1, 2, 3):
        px, py = x ^ (k >> 1), y ^ (k & 1)
        copies += [pltpu.make_async_remote_copy(
            src_ref=srcs[t].at[2 * px + py], dst_ref=lands[t].at[k - 1], send_sem=send_sems.at[3 * t + k - 1],
            recv_sem=recv_sems.at[3 * t + k - 1], device_id=(px, py, c), device_id_type=MESH) for t in range(len(srcs))]
    return copies


def _exchange_start(name, srcs, land_shapes, copies, per_array, after):
    na = len(srcs)
    lands = [_hbm(lax.empty(shp, a.dtype)) for shp, a in zip(land_shapes, srcs)]

    def body(*refs):
        token = refs[-1]
        for cp in copies(refs[:na], refs[na:2 * na], refs[2 * na + 1], refs[2 * na + 2]):
            cp.start()
        token[...] = jnp.zeros_like(token)

    out = pl.pallas_call(
        body, name=name,
        out_shape=(pltpu.SemaphoreType.DMA((na * per_array,)), pltpu.SemaphoreType.DMA((na * per_array,)),
                   *_hbm_like(srcs), *_hbm_like(lands), TOKEN),
        in_specs=[HBM] * (2 * na) + [ANY],
        out_specs=(SEM, SEM, *[HBM] * (2 * na), pl.BlockSpec(memory_space=pltpu.VMEM)),
        input_output_aliases={i: 2 + i for i in range(2 * na)},
        compiler_params=SIDE_EFFECT)(*[_hbm(a) for a in srcs], *lands, after)
    return dict(send=out[0], recv=out[1], srcs=out[2:2 + na], lands=out[2 + na:2 + 2 * na]), out[-1]


def _exchange_wait(name, state, copies, after):
    srcs, lands = state["srcs"], state["lands"]
    na = len(srcs)

    def body(*refs):
        for cp in copies(refs[:na], refs[na:2 * na], refs[2 * na], refs[2 * na + 1]):
            cp.wait_send()
            cp.wait_recv()

    out = pl.pallas_call(
        body, name=name,
        out_shape=(*_hbm_like(srcs), *_hbm_like(lands)),
        in_specs=[HBM] * (2 * na) + [SEM, SEM, ANY],
        out_specs=tuple([HBM] * (2 * na)),
        input_output_aliases={i: i for i in range(2 * na)},
        compiler_params=SIDE_EFFECT)(*srcs, *lands, state["send"], state["recv"], after)
    return out[na:]


def _adamw_math(w, g, m, v):
    m = ADAM_B1 * m + (1.0 - ADAM_B1) * g
    v = ADAM_B2 * v + (1.0 - ADAM_B2) * (g * g)
    m_hat = m / (1.0 - ADAM_B1 ** ADAM_STEP)
    v_hat = v / (1.0 - ADAM_B2 ** ADAM_STEP)
    return -ADAM_LR * (m_hat / (jnp.sqrt(v_hat) + ADAM_EPS) + ADAM_WD * w), m, v


def _adamw(own, others, w, m, v, tr, name):
    rows, cols = w.shape
    blk = pl.BlockSpec((tr, cols), lambda i: (i, 0))

    def body(own_ref, oth_ref, w_ref, m_ref, v_ref, g_ref, d_ref, nm_ref, nv_ref):
        g = own_ref[...]
        for k in range(3):
            g = g + oth_ref[k].astype(F32)
        g_ref[...] = g
        d_ref[...], nm_ref[...], nv_ref[...] = _adamw_math(w_ref[...], g, m_ref[...], v_ref[...])

    out = jax.ShapeDtypeStruct((rows, cols), F32)
    return pl.pallas_call(
        body, name=name, out_shape=(out, out, out, out), grid=(rows // tr,),
        in_specs=[blk, pl.BlockSpec((3, tr, cols), lambda i: (0, i, 0)), blk, blk, blk],
        out_specs=(blk, blk, blk, blk),
        compiler_params=_params("parallel"))(own, others, w, m, v)


def _adamw_slab(w, g, m, v):
    def body(w_ref, g_ref, m_ref, v_ref, d_ref, nm_ref, nv_ref):
        d_ref[...], nm_ref[...], nv_ref[...] = _adamw_math(w_ref[...], g_ref[...], m_ref[...], v_ref[...])

    out = jax.ShapeDtypeStruct(w.shape, F32)
    vmem = pl.BlockSpec(memory_space=pltpu.VMEM)
    return pl.pallas_call(body, name="adamw_small", out_shape=(out, out, out),
                          in_specs=[vmem] * 4, out_specs=(vmem, vmem, vmem))(w, g, m, v)


def _row(v, width=D_MODEL):
    v = v.reshape(1, -1)
    return jnp.pad(v, ((0, 0), (0, width - v.shape[1])))


def _tables(s, gq, gk, conv_w):
    gq2 = jnp.tile(gq.reshape(1, HEAD), (1, 2))
    gk2 = jnp.tile(gk.reshape(1, HEAD), (1, 2))
    conv_wp = jnp.pad(conv_w, ((0, SUBLANES - conv_w.shape[0]), (0, 0)))
    return _rope_tables(s), gq2, gk2, conv_wp


def _forward_in(x, g1, w_qkv, w_rest):
    s = x.shape[0]
    h, ht, z_a = _fwd_in_a(x, g1, w_qkv, min(512, s))
    z_b = _mm_nn(h, w_rest, min(512, s), 1024, "fwd_in_b")
    return ht, z_a, z_b


def _forward_attn(z_a, z_b, rope, gq2, gk2, conv_wp, sinks):
    s = z_a.shape[0]
    qn, k2, v2 = _qk_prep(z_a, *rope, gq2, gk2, min(256, s))
    a, mix, mixt = _attn_fwd(qn, k2, v2, z_b, conv_wp, sinks)
    return qn, k2, v2, a, mix, mixt


def _forward_out(x, p, target, mix, mixt, w_out, g2, w_pg, b_pg, w_pp, g3):
    s = x.shape[0]
    tm = min(512, s)
    x1, hn2, hn2t = _fwd_out(mix, w_out, x, g2, tm)
    dy, dgp, dt, pt, acc_ple = _ple(hn2, w_pg, b_pg, p, w_pp, g3, x1, target, min(256, s))
    dx1, dx1b, acc_g2 = _gate_bwd(dgp, w_pg, x1, dy, g2, tm)
    gw_out = _mm_grad(mixt, [dx1b], 512, "grad_w_out")
    gw_pg = _mm_grad(hn2t, [dgp], 512, "grad_w_ple_gate")
    gw_pp = _mm_grad(pt, [dt], 512, "grad_w_ple_proj")
    return dx1, dx1b, (gw_out, gw_pg, gw_pp), acc_ple, acc_g2


def _backward_attn(dmix, ht, z_a, z_b, qn, k2, v2, a, rope, gq2, gk2, conv_wp, sinks):
    dq, dkc, dkp, dvc, dvp, dz_b, acc_attn = _attn_bwd(qn, k2, v2, a, z_b, dmix, conv_wp, sinks)
    dz_a, acc_qk = _qkv_bwd(z_a, dq, dkc, dkp, dvc, dvp, *rope, gq2, gk2)
    gw_in = _mm_grad(ht, [dz_a, dz_b], 512, "grad_w_in")
    return dz_a, dz_b, gw_in, acc_attn, acc_qk


def _small_rows(acc_g1, acc_g2, acc_ple, acc_qk, acc_attn):
    fold = lambda v: _row((v[:HEAD] + v[HEAD:]))
    return [acc_g1[0:1], acc_g2[0:1], acc_ple[0:1], acc_ple[1:2], fold(acc_qk[0]), fold(acc_qk[1]),
            _row(acc_attn[0, :N_Q_HEADS]), _row(acc_attn[1]), _row(acc_attn[2]), _row(acc_attn[3]), acc_ple[2:3]]


def _local_step(x, p, target, g1, w_qkv, w_rest, gq, gk, sinks, conv_w, w_out, g2, w_pg, b_pg, w_pp, g3):
    rope, gq2, gk2, conv_wp = _tables(x.shape[0], gq, gk, conv_w)
    ht, z_a, z_b = _forward_in(x, g1, w_qkv, w_rest)
    qn, k2, v2, a, mix, mixt = _forward_attn(z_a, z_b, rope, gq2, gk2, conv_wp, sinks)
    dx1, dx1b, (gw_out, gw_pg, gw_pp), acc_ple, acc_g2 = _forward_out(
        x, p, target, mix, mixt, w_out, g2, w_pg, b_pg, w_pp, g3)
    dmix = _mm_nt(dx1b, w_out, min(512, x.shape[0]), "out_bwd")
    dz_a, dz_b, gw_in, acc_attn, acc_qk = _backward_attn(
        dmix, ht, z_a, z_b, qn, k2, v2, a, rope, gq2, gk2, conv_wp, sinks)
    grad_x, acc_g1 = _in_bwd(dz_a, dz_b, w_qkv, w_rest, x, dx1, g1, min(512, x.shape[0]))
    return grad_x, (gw_in, gw_out, gw_pg, gw_pp), _small_rows(acc_g1, acc_g2, acc_ple, acc_qk, acc_attn)


ROW_CONV, ROW_LOSS = 7, 10


def _slab(rows):
    rows = list(rows)
    return jnp.concatenate(rows + [jnp.zeros((SLAB_ROWS - len(rows), D_MODEL), F32)], axis=0)


def _by_owner(g):
    return g.reshape((4, 2) + g.shape[1:])


def kernel(x, p, norm_gain, w_in, q_norm_gain, k_norm_gain, attn_sinks, conv_w, w_out, ple_gate_norm_gain, w_ple_gate, b_ple_gate, w_ple_proj, ple_norm_gain, loss_target, m_norm_gain, m_w_in, m_q_norm_gain, m_k_norm_gain, m_attn_sinks, m_conv_w, m_w_out, m_ple_gate_norm_gain, m_w_ple_gate, m_b_ple_gate, m_w_ple_proj, m_ple_norm_gain, v_norm_gain, v_w_in, v_q_norm_gain, v_k_norm_gain, v_attn_sinks, v_conv_w, v_w_out, v_ple_gate_norm_gain, v_w_ple_gate, v_b_ple_gate, v_w_ple_proj, v_ple_norm_gain):
    me = 4 * lax.axis_index("x") + 2 * lax.axis_index("y") + lax.axis_index("c")
    place = jnp.stack([lax.axis_index("c"), 2 * lax.axis_index("x") + lax.axis_index("y")]).astype(jnp.int32)
    conv_cols = conv_w.shape[2]
    xs, ps, target = x[0], p[0, 0], loss_target[0]
    zero = lambda token: token[0:1, 0:1]

    own_in = w_in[0].astype(BF16)
    own_late = [w_out[0].astype(BF16), w_ple_gate[0].astype(BF16), w_ple_proj[0].astype(BF16)]
    with_own = lambda gathered, own: lax.dynamic_update_slice(gathered, own[None], (me, 0, 0))
    (g_in,) = _all_gather([own_in])
    late, started = _gather_start(own_late, g_in)
    g_in = with_own(g_in, own_in)
    split = QKV_W - SHARD_IN
    w_qkv = jnp.concatenate([g_in[0], g_in[1][:, :split]], axis=1)
    w_rest = jnp.concatenate([g_in[1][:, split:]] + [g_in[d] for d in range(2, N_DEV)], axis=1)
    conv_rows = [lax.dynamic_update_slice(jnp.zeros((1, D_MODEL), F32), conv_w[0, t:t + 1], (0, conv_cols * me))
                 for t in range(3)]
    conv_full = _all_reduce_slab(_slab(conv_rows), "gather_conv_w")[0:3, :ATTN_W]
    rope, gq2, gk2, conv_wp = _tables(xs.shape[0], q_norm_gain[0], k_norm_gain[0], conv_full)

    g1 = norm_gain + zero(started)
    ht, z_a, z_b = _forward_in(xs, g1, w_qkv, w_rest)
    late, forwarded = _gather_forward(late, z_b)
    qn, k2, v2, a, mix, mixt = _forward_attn(z_a, z_b, rope, gq2 + zero(forwarded), gk2, conv_wp, attn_sinks)
    g_out, g_pg, g_pp = (with_own(g, own) for g, own in zip(_gather_wait(late, mix), own_late))
    w_out_f = g_out.reshape(D_MODEL, D_MODEL)
    w_pg_f = g_pg.reshape(D_MODEL, D_MODEL)
    w_pp_f = jnp.transpose(g_pp, (1, 0, 2)).reshape(PLE_DIM, D_MODEL)

    dx1, dx1b, (gw_out, gw_pg, gw_pp), acc_ple, acc_g2 = _forward_out(
        xs, ps, target, mix, mixt, w_out_f, ple_gate_norm_gain, w_pg_f, b_ple_gate, w_pp_f, ple_norm_gain)

    names = ("w_out", "w_ple_gate", "w_ple_proj")
    gw_pp_t = jnp.transpose(gw_pp.reshape(PLE_DIM, N_DEV, PLE_DIM), (1, 0, 2))
    grads = [_by_owner(gw_out.reshape(N_DEV, D_MODEL // N_DEV, D_MODEL)),
             _by_owner(gw_pg.reshape(N_DEV, D_MODEL // N_DEV, D_MODEL)), _by_owner(gw_pp_t)]
    pairs, _ = _exchange_start("pair_start", grads, [(4,) + g.shape[2:] for g in grads], _to_sibling, 1, dx1b)
    dmix = _mm_nt(dx1b, w_out_f, min(512, xs.shape[0]), "out_bwd")
    from_sibling = _exchange_wait("pair_wait", pairs, _to_sibling, dmix)
    sums = [_pair_sum(g, r, place, 256, "pair_sum_" + nm) for g, r, nm in zip(pairs["srcs"], from_sibling, names)]
    chips, sent = _exchange_start("chip_start", [pb for pb, _ in sums], [(3,) + pb.shape[1:] for pb, _ in sums],
                                  _to_chips, 3, sums[-1][1])

    dz_a, dz_b, gw_in, acc_attn, acc_qk = _backward_attn(
        dmix, ht, z_a, z_b, qn, k2, v2, a, rope, gq2, gk2, conv_wp, attn_sinks + zero(sent))

    gw_in_t = [_by_owner(jnp.transpose(gw_in.reshape(D_MODEL, N_DEV, SHARD_IN), (1, 0, 2)))]
    pairs_in, _ = _exchange_start("pair_start_w_in", gw_in_t, [(4,) + gw_in_t[0].shape[2:]], _to_sibling, 1, gw_in)
    from_chips = _exchange_wait("chip_wait", chips, _to_chips, gw_in)
    big = {}
    for (_, own), oth, w, m, v, nm in zip(sums, from_chips, (w_out, w_ple_gate, w_ple_proj),
                                          (m_w_out, m_w_ple_gate, m_w_ple_proj),
                                          (v_w_out, v_w_ple_gate, v_w_ple_proj), names):
        big[nm] = [t[None] for t in _adamw(own, oth, w[0], m[0], v[0], 256, "adamw_" + nm)]

    (from_sibling_in,) = _exchange_wait("pair_wait_w_in", pairs_in, _to_sibling, big[names[-1]][0])
    pb_in, own_in = _pair_sum(pairs_in["srcs"][0], from_sibling_in, place, 256, "pair_sum_w_in")
    chips_in, sent_in = _exchange_start("chip_start_w_in", [pb_in], [(3,) + pb_in.shape[1:]], _to_chips, 3, own_in)
    grad_x, acc_g1 = _in_bwd(dz_a, dz_b, w_qkv, w_rest, xs, dx1, norm_gain + zero(sent_in), min(512, xs.shape[0]))
    (from_chips_in,) = _exchange_wait("chip_wait_w_in", chips_in, _to_chips, grad_x)
    big["w_in"] = [t[None] for t in _adamw(own_in, from_chips_in, w_in[0], m_w_in[0], v_w_in[0], 256, "adamw_w_in")]

    red = _all_reduce_slab(_slab(_small_rows(acc_g1, acc_g2, acc_ple, acc_qk, acc_attn)), "reduce_small")
    loss = jnp.sum(red[ROW_LOSS])
    g_conv = [lax.dynamic_slice(red[ROW_CONV + t:ROW_CONV + t + 1], (0, conv_cols * me), (1, conv_cols))
              for t in range(3)]
    small = [norm_gain, ple_gate_norm_gain, b_ple_gate, ple_norm_gain, q_norm_gain, k_norm_gain, attn_sinks]
    small_m = [m_norm_gain, m_ple_gate_norm_gain, m_b_ple_gate, m_ple_norm_gain, m_q_norm_gain, m_k_norm_gain,
               m_attn_sinks]
    small_v = [v_norm_gain, v_ple_gate_norm_gain, v_b_ple_gate, v_ple_norm_gain, v_q_norm_gain, v_k_norm_gain,
               v_attn_sinks]
    pack = lambda vs, cw: _slab([_row(t) for t in vs] + [_row(cw[0, t]) for t in range(3)])
    g_slab = _slab([red[t:t + 1] for t in range(ROW_CONV)] + [_row(t) for t in g_conv])
    d_slab, m_slab, v_slab = _adamw_slab(pack(small, conv_w), g_slab, pack(small_m, m_conv_w), pack(small_v, v_conv_w))

    def unpack(slab_):
        outs = [slab_[t:t + 1, :w.shape[1]] for t, w in enumerate(small)]
        return outs, slab_[ROW_CONV:ROW_CONV + 3, :conv_cols][None]

    (g_s, g_cv), (d_s, d_cv), (m_s, m_cv), (v_s, v_cv) = (unpack(t) for t in (g_slab, d_slab, m_slab, v_slab))

    def order(sm, cv, k):
        return [sm[0], big["w_in"][k], sm[4], sm[5], sm[6], cv, big["w_out"][k], sm[1], big["w_ple_gate"][k], sm[2],
                big["w_ple_proj"][k], sm[3]]

    return (loss, grad_x[None], *order(g_s, g_cv, 0), *order(d_s, d_cv, 1), *order(m_s, m_cv, 2),
            *order(v_s, v_cv, 3))
```

```python
import functools

import jax
import jax.numpy as jnp
from jax import lax
from jax.experimental import pallas as pl
from jax.experimental.pallas import tpu as pltpu

F32, BF16 = jnp.float32, jnp.bfloat16

D_MODEL = 2048
PLE_DIM = 256
ATTN_W = 1024
HEAD = 64
N_Q_HEADS = 16
KV_W = 256
QKV_W = ATTN_W + 2 * KV_W
REST_W = 5 * 1024
IN_W = QKV_W + REST_W
K2_W = 4 * 128
ROT = 16
ROPE_THETA = 500000.0
EPS = 1e-6
NEG_INF = -1e30
BLK = 128
LANES = 128
SUBLANES = 8
N_DEV = 8
SHARD_IN = IN_W // N_DEV
SLAB_ROWS = 16
SUB_ROWS = 128
V7X_VMEM_LIMIT = 52 * 1024 * 1024

ADAM_LR, ADAM_B1, ADAM_B2, ADAM_EPS, ADAM_WD, ADAM_STEP = 0.001, 0.9, 0.999, 1e-08, 0.01, 10
MESH = pl.DeviceIdType.MESH


def _params(*semantics):
    return pltpu.CompilerParams(dimension_semantics=semantics, vmem_limit_bytes=V7X_VMEM_LIMIT)


ANY = pl.BlockSpec(memory_space=pl.ANY)


def _resident(shape):
    return pl.BlockSpec(shape, lambda *_: (0,) * len(shape), pipeline_mode=pl.Buffered(1))


def _dot(a, b):
    return jnp.dot(a, b, preferred_element_type=F32)


def _dot_nt(a, b):
    return lax.dot_general(a, b, (((1,), (1,)), ((), ())), preferred_element_type=F32)


def _rms(xf):
    r = lax.rsqrt(jnp.mean(xf * xf, axis=-1, keepdims=True) + EPS)
    return xf * r, r


def _rms_bwd(dxn, xn, r):
    return r * (dxn - xn * jnp.mean(dxn * xn, axis=-1, keepdims=True))


def _sig(g):
    return jax.nn.sigmoid(g)


def _dsilu(g, sg):
    return sg * (1.0 + g * (1.0 - sg))


def _low_half(shape):
    return lax.broadcasted_iota(jnp.int32, shape, len(shape) - 1) < HEAD


def _half_sums(v):
    lo = _low_half(v.shape)
    s_lo = jnp.sum(jnp.where(lo, v, 0.0), axis=-1, keepdims=True)
    s_hi = jnp.sum(jnp.where(lo, 0.0, v), axis=-1, keepdims=True)
    return jnp.where(lo, s_lo, s_hi)


def _rope(v, a, bm, bp):
    return v * a + pltpu.roll(v, LANES - ROT // 2, 1) * bm + pltpu.roll(v, ROT // 2, 1) * bp


def _rope_t(dy, a, bm, bp):
    return dy * a + pltpu.roll(dy * bm, ROT // 2, 1) + pltpu.roll(dy * bp, LANES - ROT // 2, 1)


def _dup_halves(v):
    lo = _low_half(v.shape)
    a = jnp.where(lo, v, 0.0)
    b = jnp.where(lo, 0.0, v)
    return a + pltpu.roll(a, HEAD, 1), b + pltpu.roll(b, HEAD, 1)


def _rope_tables(s):
    half = ROT // 2
    lane = lax.broadcasted_iota(jnp.int32, (s, LANES), 1) % HEAD
    pos = lax.broadcasted_iota(jnp.int32, (s, LANES), 0).astype(F32)
    inv_freq = jnp.power(jnp.float32(ROPE_THETA), -(lane % half).astype(F32) * 2.0 / ROT)
    ang = pos * inv_freq
    cos, sin = jnp.cos(ang), jnp.sin(ang)
    a = jnp.where(lane < ROT, cos, 1.0)
    bm = jnp.where(lane < half, -sin, 0.0)
    bp = jnp.where((lane >= half) & (lane < ROT), sin, 0.0)
    return a, bm, bp


def _fwd_in_a(x, g1, w_qkv, tm):
    s = x.shape[0]

    def body(x_ref, g_ref, w_ref, h_ref, ht_ref, z_ref):
        xn, _ = _rms(x_ref[...])
        h = (xn * g_ref[...]).astype(BF16)
        h_ref[...] = h
        ht_ref[...] = h.T
        z_ref[...] = _dot(h, w_ref[...])

    return pl.pallas_call(
        body, name="fwd_in_a",
        out_shape=(jax.ShapeDtypeStruct((s, D_MODEL), BF16), jax.ShapeDtypeStruct((D_MODEL, s), BF16),
                   jax.ShapeDtypeStruct((s, QKV_W), F32)),
        grid=(s // tm,),
        in_specs=[pl.BlockSpec((tm, D_MODEL), lambda i: (i, 0)),
                  pl.BlockSpec((1, D_MODEL), lambda i: (0, 0)),
                  _resident((D_MODEL, QKV_W))],
        out_specs=(pl.BlockSpec((tm, D_MODEL), lambda i: (i, 0)),
                   pl.BlockSpec((D_MODEL, tm), lambda i: (0, i)),
                   pl.BlockSpec((tm, QKV_W), lambda i: (i, 0))),
        compiler_params=_params("parallel"))(x, g1, w_qkv)


def _mm_nn(a, b, tm, tn, name):
    m, k = a.shape
    n = b.shape[1]

    def body(a_ref, b_ref, o_ref):
        o_ref[...] = _dot(a_ref[...], b_ref[...])

    return pl.pallas_call(
        body, name=name,
        out_shape=jax.ShapeDtypeStruct((m, n), F32),
        grid=(n // tn, m // tm),
        in_specs=[pl.BlockSpec((tm, k), lambda j, i: (i, 0)),
                  pl.BlockSpec((k, tn), lambda j, i: (0, j))],
        out_specs=pl.BlockSpec((tm, tn), lambda j, i: (i, j)),
        compiler_params=_params("parallel", "parallel"))(a, b)


def _qk_prep(z_a, ra, rbm, rbp, gq2, gk2, tm):
    s = z_a.shape[0]

    def body(z_ref, a_ref, bm_ref, bp_ref, gq_ref, gk_ref, q_ref, k2_ref, v2_ref):
        a, bm, bp = a_ref[...], bm_ref[...], bp_ref[...]
        for r in range(ATTN_W // LANES):
            x = z_ref[:, LANES * r:LANES * (r + 1)]
            rr = lax.rsqrt(_half_sums(x * x) * (1.0 / HEAD) + EPS)
            q_ref[:, LANES * r:LANES * (r + 1)] = _rope(x * rr * gq_ref[...], a, bm, bp).astype(BF16)
        for m in range(KV_W // LANES):
            x = z_ref[:, ATTN_W + LANES * m:ATTN_W + LANES * (m + 1)]
            rr = lax.rsqrt(_half_sums(x * x) * (1.0 / HEAD) + EPS)
            k_lo, k_hi = _dup_halves(_rope(x * rr * gk_ref[...], a, bm, bp))
            k2_ref[:, 2 * LANES * m:2 * LANES * m + LANES] = k_lo.astype(BF16)
            k2_ref[:, 2 * LANES * m + LANES:2 * LANES * (m + 1)] = k_hi.astype(BF16)
            v_lo, v_hi = _dup_halves(z_ref[:, ATTN_W + KV_W + LANES * m:ATTN_W + KV_W + LANES * (m + 1)])
            v2_ref[:, 2 * LANES * m:2 * LANES * m + LANES] = v_lo.astype(BF16)
            v2_ref[:, 2 * LANES * m + LANES:2 * LANES * (m + 1)] = v_hi.astype(BF16)

    row = lambda w: pl.BlockSpec((tm, w), lambda i: (i, 0))
    one = pl.BlockSpec((1, LANES), lambda i: (0, 0))
    return pl.pallas_call(
        body, name="qk_prep",
        out_shape=(jax.ShapeDtypeStruct((s, ATTN_W), BF16), jax.ShapeDtypeStruct((s, K2_W), BF16),
                   jax.ShapeDtypeStruct((s, K2_W), BF16)),
        grid=(s // tm,),
        in_specs=[row(QKV_W), row(LANES), row(LANES), row(LANES), one, one],
        out_specs=(row(ATTN_W), row(K2_W), row(K2_W)),
        compiler_params=_params("parallel"))(z_a, ra, rbm, rbp, gq2, gk2)


GROUP = 4


def _window_mask(n):
    row = lax.broadcasted_iota(jnp.int32, (GROUP * BLK, 2 * BLK), 0) % BLK
    col = lax.broadcasted_iota(jnp.int32, (GROUP * BLK, 2 * BLK), 1)
    return (col > row) & (col <= row + BLK) & ((col >= BLK) | (n > 0))


def _stack_heads(pairs, zero):
    lo = _low_half(pairs[0].shape)
    parts = []
    for v in pairs:
        parts += [jnp.where(lo, v, zero), jnp.where(lo, zero, v)]
    return jnp.concatenate(parts, axis=0)


def _unstack_heads(v4):
    lo = _low_half((BLK, LANES))
    return [jnp.where(lo, v4[2 * i * BLK:(2 * i + 1) * BLK], v4[(2 * i + 1) * BLK:(2 * i + 2) * BLK]) for i in range(2)]


def _group_sinks(sink_ref, kvh):
    slot = lax.broadcasted_iota(jnp.int32, (GROUP * BLK, 1), 0) // BLK
    col = jnp.zeros((GROUP * BLK, 1), F32)
    for i in range(GROUP):
        col = jnp.where(slot == i, sink_ref[0, GROUP * kvh + i], col)
    return col, slot


def _head_probs(qm, kw, valid, sink):
    sc = jnp.where(valid, _dot_nt(qm, kw) * (HEAD ** -0.5), NEG_INF)
    mx = jnp.maximum(jnp.max(sc, axis=-1, keepdims=True), sink)
    ex = jnp.exp(sc - mx)
    den = jnp.sum(ex, axis=-1, keepdims=True) + jnp.exp(sink - mx)
    return ex / den, mx, den


def _conv_fwd(zb_ref, zbp_ref, cw_ref, ext_ref, n):
    u = zb_ref[:, 2048:3072] * zb_ref[:, 3072:4096]
    pu = zbp_ref[:, 2048:3072] * zbp_ref[:, 3072:4096]
    ext_ref[0:SUBLANES, :] = jnp.where(n > 0, pu, 0.0)
    ext_ref[SUBLANES:SUBLANES + BLK, :] = u
    um1 = ext_ref[SUBLANES - 1:SUBLANES - 1 + BLK, :]
    um2 = ext_ref[SUBLANES - 2:SUBLANES - 2 + BLK, :]
    cv = cw_ref[0:1, :] * um2 + cw_ref[1:2, :] * um1 + cw_ref[2:3, :] * u
    return u, um1, um2, cv


def _prev_rows(n):
    return (jnp.maximum(n * (BLK // SUBLANES) - 1, 0), 0)


def _attn_fwd(qn, k2, v2, z_b, conv_wp, sinks):
    s = qn.shape[0]
    nb = s // BLK

    def body(sink_ref, q_ref, kc_ref, kp_ref, vc_ref, vp_ref, zb_ref, zbp_ref, cw_ref, a_ref, mix_ref, mixt_ref,
             ext_ref):
        n = pl.program_id(0)
        valid = _window_mask(n)
        for kvh in range(K2_W // LANES):
            cols = slice(LANES * kvh, LANES * (kvh + 1))
            kw = jnp.concatenate([kp_ref[:, cols], kc_ref[:, cols]], axis=0)
            vw = jnp.concatenate([vp_ref[:, cols], vc_ref[:, cols]], axis=0)
            blocks = [slice(LANES * r, LANES * (r + 1)) for r in (2 * kvh, 2 * kvh + 1)]
            q4 = _stack_heads([q_ref[:, rc] for rc in blocks], jnp.zeros((BLK, LANES), BF16))
            p, _, _ = _head_probs(q4, kw, valid, _group_sinks(sink_ref, kvh)[0])
            for rc, a in zip(blocks, _unstack_heads(_dot(p.astype(BF16), vw))):
                a_ref[:, rc] = a
                g = zb_ref[:, rc]
                mix_ref[:, rc] = (a * (g * _sig(g))).astype(BF16)
        _, _, _, cv = _conv_fwd(zb_ref, zbp_ref, cw_ref, ext_ref, n)
        gc = zb_ref[:, 4096:5120]
        mix_ref[:, ATTN_W:D_MODEL] = (zb_ref[:, 1024:2048] * cv * (gc * _sig(gc))).astype(BF16)
        mixt_ref[...] = mix_ref[...].T

    cur = lambda w: pl.BlockSpec((BLK, w), lambda n: (n, 0))
    prev = lambda w: pl.BlockSpec((BLK, w), lambda n: (jnp.maximum(n - 1, 0), 0))
    return pl.pallas_call(
        body, name="attn_fwd",
        out_shape=(jax.ShapeDtypeStruct((s, ATTN_W), F32), jax.ShapeDtypeStruct((s, D_MODEL), BF16),
                   jax.ShapeDtypeStruct((D_MODEL, s), BF16)),
        grid=(nb,),
        in_specs=[pl.BlockSpec(memory_space=pltpu.SMEM),
                  cur(ATTN_W), cur(K2_W), prev(K2_W), cur(K2_W), prev(K2_W), cur(REST_W),
                  pl.BlockSpec((SUBLANES, REST_W), _prev_rows),
                  pl.BlockSpec((SUBLANES, ATTN_W), lambda n: (0, 0))],
        out_specs=(cur(ATTN_W), cur(D_MODEL), pl.BlockSpec((D_MODEL, BLK), lambda n: (0, n))),
        scratch_shapes=[pltpu.VMEM((BLK + 2 * SUBLANES, ATTN_W), F32)],
        compiler_params=_params("parallel"))(sinks, qn, k2, k2, v2, v2, z_b, z_b, conv_wp)


def _fwd_out(mix, w_out, x, g2, tm):
    s = x.shape[0]

    def body(m_ref, w_ref, x_ref, g_ref, x1_ref, h_ref, ht_ref):
        x1 = x_ref[...] + _dot(m_ref[...], w_ref[...])
        x1_ref[...] = x1
        xn, _ = _rms(x1)
        h = (xn * g_ref[...]).astype(BF16)
        h_ref[...] = h
        ht_ref[...] = h.T

    row = pl.BlockSpec((tm, D_MODEL), lambda i: (i, 0))
    return pl.pallas_call(
        body, name="fwd_out",
        out_shape=(jax.ShapeDtypeStruct((s, D_MODEL), F32), jax.ShapeDtypeStruct((s, D_MODEL), BF16),
                   jax.ShapeDtypeStruct((D_MODEL, s), BF16)),
        grid=(s // tm,),
        in_specs=[row, _resident((D_MODEL, D_MODEL)), row, pl.BlockSpec((1, D_MODEL), lambda i: (0, 0))],
        out_specs=(row, row, pl.BlockSpec((D_MODEL, tm), lambda i: (0, i))),
        compiler_params=_params("parallel"))(mix, w_out, x, g2)


def _ple(hn2, w_pg, b_pg, p, w_pp, g3, x1, target, tm):
    s = x1.shape[0]

    def body(h_ref, wg_ref, b_ref, p_ref, wp_ref, g3_ref, x1_ref, t_ref, dy_ref, dgp_ref, dt_ref, pt_ref, acc_ref):
        gate = _sig(_dot(h_ref[...], wg_ref[...]) + b_ref[...])
        pb = p_ref[...].astype(BF16)
        pt_ref[...] = pb.T
        t = _dot(pb, wp_ref[...])
        tn, r3 = _rms(t)
        e = tn * g3_ref[...]
        diff = x1_ref[...] + gate * e - t_ref[...]
        dy = diff * (1.0 / D_MODEL)
        dy_ref[...] = dy
        dgp = dy * e * (gate * (1.0 - gate))
        dgp_ref[...] = dgp.astype(BF16)
        de = dy * gate
        dt_ref[...] = _rms_bwd(de * g3_ref[...], tn, r3).astype(BF16)

        @pl.when(pl.program_id(0) == 0)
        def _():
            acc_ref[...] = jnp.zeros_like(acc_ref)

        acc_ref[0:1, :] += jnp.sum(dgp, axis=0, keepdims=True)
        acc_ref[1:2, :] += jnp.sum(de * tn, axis=0, keepdims=True)
        acc_ref[2:3, :] += jnp.sum(diff * diff, axis=0, keepdims=True) * (0.5 / D_MODEL)

    row = pl.BlockSpec((tm, D_MODEL), lambda i: (i, 0))
    vec = pl.BlockSpec((1, D_MODEL), lambda i: (0, 0))
    return pl.pallas_call(
        body, name="ple",
        out_shape=(jax.ShapeDtypeStruct((s, D_MODEL), F32), jax.ShapeDtypeStruct((s, D_MODEL), BF16),
                   jax.ShapeDtypeStruct((s, D_MODEL), BF16), jax.ShapeDtypeStruct((PLE_DIM, s), BF16),
                   jax.ShapeDtypeStruct((SUBLANES, D_MODEL), F32)),
        grid=(s // tm,),
        in_specs=[row, _resident((D_MODEL, D_MODEL)), vec, pl.BlockSpec((tm, PLE_DIM), lambda i: (i, 0)),
                  _resident((PLE_DIM, D_MODEL)), vec, row, row],
        out_specs=(row, row, row, pl.BlockSpec((PLE_DIM, tm), lambda i: (0, i)),
                   pl.BlockSpec((SUBLANES, D_MODEL), lambda i: (0, 0))),
        compiler_params=_params("arbitrary"))(hn2, w_pg, b_pg, p, w_pp, g3, x1, target)


def _gate_bwd(dgp, w_pg, x1, dy, g2, tm):
    s = x1.shape[0]

    def body(d_ref, w_ref, x1_ref, dy_ref, g_ref, dx_ref, dxb_ref, acc_ref):
        dh = _dot_nt(d_ref[...], w_ref[...])
        xn, r = _rms(x1_ref[...])
        dx1 = dy_ref[...] + _rms_bwd(dh * g_ref[...], xn, r)
        dx_ref[...] = dx1
        dxb_ref[...] = dx1.astype(BF16)

        @pl.when(pl.program_id(0) == 0)
        def _():
            acc_ref[...] = jnp.zeros_like(acc_ref)

        acc_ref[0:1, :] += jnp.sum(dh * xn, axis=0, keepdims=True)

    row = pl.BlockSpec((tm, D_MODEL), lambda i: (i, 0))
    return pl.pallas_call(
        body, name="gate_bwd",
        out_shape=(jax.ShapeDtypeStruct((s, D_MODEL), F32), jax.ShapeDtypeStruct((s, D_MODEL), BF16),
                   jax.ShapeDtypeStruct((SUBLANES, D_MODEL), F32)),
        grid=(s // tm,),
        in_specs=[row, _resident((D_MODEL, D_MODEL)), row, row, pl.BlockSpec((1, D_MODEL), lambda i: (0, 0))],
        out_specs=(row, row, pl.BlockSpec((SUBLANES, D_MODEL), lambda i: (0, 0))),
        compiler_params=_params("arbitrary"))(dgp, w_pg, x1, dy, g2)


def _mm_nt(a, b, tm, name):
    m, k = a.shape
    n = b.shape[0]

    def body(a_ref, b_ref, o_ref):
        o_ref[...] = _dot_nt(a_ref[...], b_ref[...])

    return pl.pallas_call(
        body, name=name,
        out_shape=jax.ShapeDtypeStruct((m, n), F32),
        grid=(m // tm,),
        in_specs=[pl.BlockSpec((tm, k), lambda i: (i, 0)), _resident((n, k))],
        out_specs=pl.BlockSpec((tm, n), lambda i: (i, 0)),
        compiler_params=_params("parallel"))(a, b)


def _attn_bwd(qn, k2, v2, a, z_b, dmix, conv_wp, sinks):
    s = qn.shape[0]
    nb = s // BLK

    def body(sink_ref, q_ref, kc_ref, kp_ref, vc_ref, vp_ref, a_ref, zb_ref, zbp_ref, zbn_ref, dm_ref, dmn_ref,
             cw_ref, dq_ref, dkc_ref, dkp_ref, dvc_ref, dvp_ref, dzb_ref, acc_ref, ext_ref):
        n = pl.program_id(0)
        valid = _window_mask(n)
        lane = lax.broadcasted_iota(jnp.int32, (1, ATTN_W), 1)

        @pl.when(n == 0)
        def _():
            acc_ref[...] = jnp.zeros_like(acc_ref)

        dsink = jnp.zeros((1, ATTN_W), F32)
        for kvh in range(K2_W // LANES):
            cols = slice(LANES * kvh, LANES * (kvh + 1))
            kw = jnp.concatenate([kp_ref[:, cols], kc_ref[:, cols]], axis=0)
            vw = jnp.concatenate([vp_ref[:, cols], vc_ref[:, cols]], axis=0)
            blocks = [slice(LANES * r, LANES * (r + 1)) for r in (2 * kvh, 2 * kvh + 1)]
            das, avs = [], []
            for rc in blocks:
                g = zb_ref[:, rc]
                sg = _sig(g)
                dm = dm_ref[:, rc]
                av = a_ref[:, rc]
                das.append(dm * (g * sg))
                avs += [av, av]
                dzb_ref[:, rc] = (dm * av * _dsilu(g, sg)).astype(BF16)
            q4 = _stack_heads([q_ref[:, rc] for rc in blocks], jnp.zeros((BLK, LANES), BF16))
            sink, slot = _group_sinks(sink_ref, kvh)
            p, mx, den = _head_probs(q4, kw, valid, sink)
            do4 = _stack_heads(das, 0.0)
            delta = jnp.sum(do4 * jnp.concatenate(avs, axis=0), axis=-1, keepdims=True)
            dob = do4.astype(BF16)
            ds = p * (_dot_nt(dob, vw) - delta) * (HEAD ** -0.5)
            for rc, dq in zip(blocks, _unstack_heads(_dot(ds.astype(BF16), kw))):
                dq_ref[:, rc] = dq
            dk2 = _dot(ds.T.astype(BF16), q4)
            dv2 = _dot(p.T.astype(BF16), dob)
            dkp_ref[:, cols] = dk2[0:BLK]
            dkc_ref[:, cols] = dk2[BLK:2 * BLK]
            dvp_ref[:, cols] = dv2[0:BLK]
            dvc_ref[:, cols] = dv2[BLK:2 * BLK]
            dsk = jnp.exp(sink - mx) / den * delta
            for i in range(GROUP):
                dsink = dsink - jnp.where(lane == GROUP * kvh + i,
                                          jnp.sum(jnp.where(slot == i, dsk, 0.0), axis=0, keepdims=True), 0.0)
        acc_ref[0:1, :] += dsink

        u, um1, um2, cv = _conv_fwd(zb_ref, zbp_ref, cw_ref, ext_ref, n)
        cb = zb_ref[:, 1024:2048]
        gc = zb_ref[:, 4096:5120]
        sgc = _sig(gc)
        dmc = dm_ref[:, ATTN_W:D_MODEL]
        t = dmc * (gc * sgc)
        dcv = t * cb
        dzb_ref[:, 1024:2048] = (t * cv).astype(BF16)
        dzb_ref[:, 4096:5120] = (dmc * cb * cv * _dsilu(gc, sgc)).astype(BF16)
        gcn = zbn_ref[:, 4096:5120]
        dcvn = dmn_ref[:, ATTN_W:D_MODEL] * (gcn * _sig(gcn)) * zbn_ref[:, 1024:2048]
        ext_ref[0:BLK, :] = dcv
        ext_ref[BLK:BLK + SUBLANES, :] = jnp.where(n < nb - 1, dcvn, 0.0)
        du = (cw_ref[2:3, :] * dcv + cw_ref[1:2, :] * ext_ref[1:1 + BLK, :]
              + cw_ref[0:1, :] * ext_ref[2:2 + BLK, :])
        dzb_ref[:, 2048:3072] = (du * zb_ref[:, 3072:4096]).astype(BF16)
        dzb_ref[:, 3072:4096] = (du * zb_ref[:, 2048:3072]).astype(BF16)
        acc_ref[1:2, :] += jnp.sum(dcv * um2, axis=0, keepdims=True)
        acc_ref[2:3, :] += jnp.sum(dcv * um1, axis=0, keepdims=True)
        acc_ref[3:4, :] += jnp.sum(dcv * u, axis=0, keepdims=True)

    cur = lambda w: pl.BlockSpec((BLK, w), lambda n: (n, 0))
    prev = lambda w: pl.BlockSpec((BLK, w), lambda n: (jnp.maximum(n - 1, 0), 0))
    nxt = lambda w: pl.BlockSpec(
        (SUBLANES, w), lambda n: (jnp.minimum((n + 1) * (BLK // SUBLANES), nb * (BLK // SUBLANES) - 1), 0))
    f32 = lambda w: jax.ShapeDtypeStruct((s, w), F32)
    return pl.pallas_call(
        body, name="attn_bwd",
        out_shape=(f32(ATTN_W), f32(K2_W), f32(K2_W), f32(K2_W), f32(K2_W),
                   jax.ShapeDtypeStruct((s, REST_W), BF16), jax.ShapeDtypeStruct((SUBLANES, ATTN_W), F32)),
        grid=(nb,),
        in_specs=[pl.BlockSpec(memory_space=pltpu.SMEM),
                  cur(ATTN_W), cur(K2_W), prev(K2_W), cur(K2_W), prev(K2_W), cur(ATTN_W), cur(REST_W),
                  pl.BlockSpec((SUBLANES, REST_W), _prev_rows), nxt(REST_W), cur(D_MODEL), nxt(D_MODEL),
                  pl.BlockSpec((SUBLANES, ATTN_W), lambda n: (0, 0))],
        out_specs=(cur(ATTN_W), cur(K2_W), cur(K2_W), cur(K2_W), cur(K2_W), cur(REST_W),
                   pl.BlockSpec((SUBLANES, ATTN_W), lambda n: (0, 0))),
        scratch_shapes=[pltpu.VMEM((BLK + 2 * SUBLANES, ATTN_W), F32)],
        compiler_params=_params("arbitrary"))(sinks, qn, k2, k2, v2, v2, a, z_b, z_b, z_b, dmix, dmix, conv_wp)


def _qkv_bwd(z_a, dq, dkc, dkp, dvc, dvp, ra, rbm, rbp, gq2, gk2):
    s = z_a.shape[0]
    nb = s // BLK

    def body(z_ref, dq_ref, dkc_ref, dkp_ref, dvc_ref, dvp_ref, a_ref, bm_ref, bp_ref, gq_ref, gk_ref,
             dz_ref, acc_ref):
        n = pl.program_id(0)
        a, bm, bp = a_ref[...], bm_ref[...], bp_ref[...]
        lo = _low_half((BLK, LANES))
        last = n == nb - 1

        @pl.when(n == 0)
        def _():
            acc_ref[...] = jnp.zeros_like(acc_ref)

        def norm_bwd(x, dy, gain):
            rr = lax.rsqrt(_half_sums(x * x) * (1.0 / HEAD) + EPS)
            xh = x * rr
            dxg = _rope_t(dy, a, bm, bp)
            dxh = dxg * gain
            dx = rr * (dxh - xh * (_half_sums(dxh * xh) * (1.0 / HEAD)))
            return dx, jnp.sum(dxg * xh, axis=0, keepdims=True)

        def folded(cur_ref, prev_ref, m):
            parts = []
            for h in (2 * m, 2 * m + 1):
                v = cur_ref[:, LANES * h:LANES * (h + 1)] + jnp.where(
                    last, 0.0, prev_ref[:, LANES * h:LANES * (h + 1)])
                parts.append(v + pltpu.roll(v, HEAD, 1))
            return jnp.where(lo, parts[0], parts[1])

        gq_acc = jnp.zeros((1, LANES), F32)
        for r in range(ATTN_W // LANES):
            rc = slice(LANES * r, LANES * (r + 1))
            dx, gg = norm_bwd(z_ref[:, rc], dq_ref[:, rc], gq_ref[...])
            dz_ref[:, rc] = dx.astype(BF16)
            gq_acc = gq_acc + gg
        acc_ref[0:1, :] += gq_acc
        gk_acc = jnp.zeros((1, LANES), F32)
        for m in range(KV_W // LANES):
            kc = slice(ATTN_W + LANES * m, ATTN_W + LANES * (m + 1))
            dx, gg = norm_bwd(z_ref[:, kc], folded(dkc_ref, dkp_ref, m), gk_ref[...])
            dz_ref[:, kc] = dx.astype(BF16)
            gk_acc = gk_acc + gg
            vc = slice(ATTN_W + KV_W + LANES * m, ATTN_W + KV_W + LANES * (m + 1))
            dz_ref[:, vc] = folded(dvc_ref, dvp_ref, m).astype(BF16)
        acc_ref[1:2, :] += gk_acc

    cur = lambda w: pl.BlockSpec((BLK, w), lambda n: (n, 0))
    nxt = lambda w: pl.BlockSpec((BLK, w), lambda n: (jnp.minimum(n + 1, nb - 1), 0))
    one = pl.BlockSpec((1, LANES), lambda n: (0, 0))
    return pl.pallas_call(
        body, name="qkv_bwd",
        out_shape=(jax.ShapeDtypeStruct((s, QKV_W), BF16), jax.ShapeDtypeStruct((SUBLANES, LANES), F32)),
        grid=(nb,),
        in_specs=[cur(QKV_W), cur(ATTN_W), cur(K2_W), nxt(K2_W), cur(K2_W), nxt(K2_W),
                  cur(LANES), cur(LANES), cur(LANES), one, one],
        out_specs=(cur(QKV_W), pl.BlockSpec((SUBLANES, LANES), lambda n: (0, 0))),
        compiler_params=_params("arbitrary"))(z_a, dq, dkc, dkp, dvc, dvp, ra, rbm, rbp, gq2, gk2)


REST_CHUNK = 1280
N_REST_CHUNKS = REST_W // REST_CHUNK


def _in_bwd(dz_a, dz_b, w_qkv, w_rest, x, dx1, g1, tm):
    s = x.shape[0]
    nk = 1 + N_REST_CHUNKS

    def body(da_ref, db_ref, wa_ref, wb_ref, x_hbm, dx1_hbm, g_ref, gx_ref, acc_ref, x_buf, dx1_buf, sems):
        i, k = pl.program_id(0), pl.program_id(1)
        rows = pl.ds(pl.multiple_of(i * tm, tm), tm)
        fetch = [pltpu.make_async_copy(x_hbm.at[rows], x_buf, sems.at[0]),
                 pltpu.make_async_copy(dx1_hbm.at[rows], dx1_buf, sems.at[1])]

        sub = min(SUB_ROWS, tm)
        blocks = [slice(r, r + sub) for r in range(0, tm, sub)]

        @pl.when(k == 0)
        def _():
            for cp in fetch:
                cp.start()
            gx_ref[...] = _dot_nt(da_ref[...], wa_ref[...])

        @pl.when(k > 0)
        def _():
            gx_ref[...] += _dot_nt(db_ref[...], wb_ref[...])

        @pl.when((i == 0) & (k == 0))
        def _():
            acc_ref[...] = jnp.zeros_like(acc_ref)

        @pl.when(k == nk - 1)
        def _():
            for cp in fetch:
                cp.wait()
            for rb in blocks:
                dh = gx_ref[rb, :]
                xn, r = _rms(x_buf[rb, :])
                gx_ref[rb, :] = dx1_buf[rb, :] + _rms_bwd(dh * g_ref[...], xn, r)
                acc_ref[0:1, :] += jnp.sum(dh * xn, axis=0, keepdims=True)

    kb = lambda i, k: jnp.maximum(k - 1, 0)
    return pl.pallas_call(
        body, name="in_bwd",
        out_shape=(jax.ShapeDtypeStruct((s, D_MODEL), F32), jax.ShapeDtypeStruct((SUBLANES, D_MODEL), F32)),
        grid=(s // tm, nk),
        in_specs=[pl.BlockSpec((tm, QKV_W), lambda i, k: (i, 0)),
                  pl.BlockSpec((tm, REST_CHUNK), lambda i, k: (i, kb(i, k))),
                  _resident((D_MODEL, QKV_W)),
                  pl.BlockSpec((D_MODEL, REST_CHUNK), lambda i, k: (0, kb(i, k))),
                  ANY, ANY, pl.BlockSpec((1, D_MODEL), lambda i, k: (0, 0))],
        out_specs=(pl.BlockSpec((tm, D_MODEL), lambda i, k: (i, 0)),
                   pl.BlockSpec((SUBLANES, D_MODEL), lambda i, k: (0, 0))),
        scratch_shapes=[pltpu.VMEM((tm, D_MODEL), F32), pltpu.VMEM((tm, D_MODEL), F32),
                        pltpu.SemaphoreType.DMA((2,))],
        compiler_params=_params("arbitrary", "arbitrary"))(dz_a, dz_b, w_qkv, w_rest, x, dx1, g1)


def _mm_grad(at, bs, tn, name):
    m, kdim = at.shape
    nblk = [b.shape[1] // tn for b in bs]
    starts = [sum(nblk[:t]) for t in range(len(bs))]

    def body(a_ref, *refs):
        b_refs, o_ref = refs[:len(bs)], refs[len(bs)]
        j = pl.program_id(0)
        for t, b_ref in enumerate(b_refs):
            @pl.when((j >= starts[t]) & (j < starts[t] + nblk[t]))
            def _():
                o_ref[...] = _dot(a_ref[...], b_ref[...]).astype(BF16)

    def b_spec(t):
        return pl.BlockSpec((kdim, tn), lambda j: (0, jnp.clip(j - starts[t], 0, nblk[t] - 1)))

    return pl.pallas_call(
        body, name=name,
        out_shape=jax.ShapeDtypeStruct((m, sum(nblk) * tn), BF16),
        grid=(sum(nblk),),
        in_specs=[_resident((m, kdim))] + [b_spec(t) for t in range(len(bs))],
        out_specs=pl.BlockSpec((m, tn), lambda j: (0, j)),
        compiler_params=_params("parallel"))(at, *bs)


def _place():
    return lax.axis_index("x"), lax.axis_index("y"), lax.axis_index("c")


def _all_gather(shards):
    na = len(shards)

    def body(*refs):
        ins, outs = refs[:na], refs[na:2 * na]
        send_sems, recv_sems = refs[2 * na:]
        x, y, c = _place()
        me, sibling = (x, y, c), (x, y, 1 - c)
        chips = [(1 - x, y), (x, 1 - y), (1 - x, 1 - y)]

        def copy(t, k, block, to, src=None):
            dst = outs[t].at[4 * block[0] + 2 * block[1] + block[2]]
            return pltpu.make_async_remote_copy(
                src_ref=dst if src is None else src, dst_ref=dst, send_sem=send_sems.at[t, k],
                recv_sem=recv_sems.at[t, k], device_id=to, device_id_type=MESH)

        first = []
        for j, chip in enumerate(chips):
            first += [copy(t, 1 + j, me, (*chip, c), src=ins[t]) for t in range(na)]
        first += [copy(t, 0, me, sibling, src=ins[t]) for t in range(na)]
        for cp in first:
            cp.start()
        passed = []
        for j, chip in enumerate(chips):
            for t in range(na):
                copy(t, 1 + j, (*chip, c), me).wait_recv()
                passed.append(copy(t, 4 + j, (*chip, c), sibling))
                passed[-1].start()
        for t in range(na):
            copy(t, 0, sibling, me).wait_recv()
        for j, chip in enumerate(chips):
            for t in range(na):
                copy(t, 4 + j, (*chip, 1 - c), me).wait_recv()
        for cp in first + passed:
            cp.wait_send()

    return pl.pallas_call(
        body, name="all_gather_weights",
        out_shape=tuple(jax.ShapeDtypeStruct((N_DEV,) + a.shape, a.dtype) for a in shards),
        in_specs=[ANY] * na, out_specs=tuple([ANY] * na),
        scratch_shapes=[pltpu.SemaphoreType.DMA((na, 7)), pltpu.SemaphoreType.DMA((na, 7))])(*shards)


def _all_reduce_slab(slab, name):
    def body(in_ref, out_ref, gath_ref, send_sems, recv_sems):
        x, y, c = _place()
        me = 4 * x + 2 * y + c
        gath_ref[me] = in_ref[...]
        copies = []
        for k in range(1, N_DEV):
            peer = (x ^ (k >> 2), y ^ ((k >> 1) & 1), c ^ (k & 1))
            copies.append(pltpu.make_async_remote_copy(
                src_ref=in_ref, dst_ref=gath_ref.at[me], send_sem=send_sems.at[k - 1],
                recv_sem=recv_sems.at[k - 1], device_id=peer, device_id_type=MESH))
        for cp in copies:
            cp.start()
        for cp in copies:
            cp.wait_recv()
        for cp in copies:
            cp.wait_send()
        total = gath_ref[0]
        for d in range(1, N_DEV):
            total = total + gath_ref[d]
        out_ref[...] = total

    vmem = pl.BlockSpec(memory_space=pltpu.VMEM)
    return pl.pallas_call(
        body, name=name,
        out_shape=jax.ShapeDtypeStruct(slab.shape, F32),
        in_specs=[vmem], out_specs=vmem,
        scratch_shapes=[pltpu.VMEM((N_DEV,) + slab.shape, F32),
                        pltpu.SemaphoreType.DMA((N_DEV - 1,)), pltpu.SemaphoreType.DMA((N_DEV - 1,))])(slab)


def _pair_sum(g, r, place, tr, name):
    _, _, rows, cols = g.shape

    def body(place_ref, g_ref, r_ref, pb_ref, own_ref):
        tot = g_ref[0, 0].astype(F32) + r_ref[0].astype(F32)
        pb_ref[0] = tot.astype(BF16)

        @pl.when(pl.program_id(1) == place_ref[1])
        def _():
            own_ref[...] = tot

    grid_spec = pltpu.PrefetchScalarGridSpec(
        num_scalar_prefetch=1, grid=(rows // tr, 4),
        in_specs=[pl.BlockSpec((1, 1, tr, cols), lambda i, q, place_ref: (q, place_ref[0], i, 0)),
                  pl.BlockSpec((1, tr, cols), lambda i, q, place_ref: (q, i, 0))],
        out_specs=(pl.BlockSpec((1, tr, cols), lambda i, q, place_ref: (q, i, 0)),
                   pl.BlockSpec((tr, cols), lambda i, q, place_ref: (i, 0))))
    return pl.pallas_call(
        body, name=name, grid_spec=grid_spec,
        out_shape=(jax.ShapeDtypeStruct((4, rows, cols), BF16), jax.ShapeDtypeStruct((rows, cols), F32)),
        compiler_params=_params("arbitrary", "arbitrary"))(place, g, r)


HBM = pl.BlockSpec(memory_space=pltpu.HBM)
SEM = pl.BlockSpec(memory_space=pltpu.SEMAPHORE)
SIDE_EFFECT = pltpu.CompilerParams(has_side_effects=pltpu.SideEffectType.DATAFLOW_SIDE_EFFECTING)
TOKEN = jax.ShapeDtypeStruct((SUBLANES, LANES), F32)


def _hbm(a):
    return pltpu.with_memory_space_constraint(a, pltpu.HBM)


def _hbm_like(arrays):
    return tuple(pltpu.HBM(a.shape, a.dtype) for a in arrays)


def _block_of(px, py, pc):
    return 4 * px + 2 * py + pc


def _gather_start(shards, after):
    na = len(shards)
    lands = [_hbm(lax.empty((N_DEV,) + a.shape, a.dtype)) for a in shards]

    def body(*refs):
        ins, land = refs[:na], refs[na:2 * na]
        send_sems, recv_ici, recv_d2d = refs[2 * na + 1:2 * na + 4]
        token = refs[-1]
        x, y, c = _place()
        for k, peer in enumerate([(x, y, 1 - c), (1 - x, y, c), (x, 1 - y, c), (1 - x, 1 - y, c)]):
            for t in range(na):
                pltpu.make_async_remote_copy(
                    src_ref=ins[t], dst_ref=land[t].at[_block_of(x, y, c)], send_sem=send_sems.at[4 * t + k],
                    recv_sem=recv_d2d.at[4 * t] if k == 0 else recv_ici.at[3 * t + k - 1],
                    device_id=peer, device_id_type=MESH).start()
        token[...] = jnp.zeros_like(token)

    out = pl.pallas_call(
        body, name="gather_start",
        out_shape=(pltpu.SemaphoreType.DMA((4 * na,)), pltpu.SemaphoreType.DMA((3 * na,)),
                   pltpu.SemaphoreType.DMA((4 * na,)), *_hbm_like(lands), TOKEN),
        in_specs=[ANY] * na + [HBM] * na + [ANY],
        out_specs=(SEM, SEM, SEM, *[HBM] * na, pl.BlockSpec(memory_space=pltpu.VMEM)),
        input_output_aliases={na + i: 3 + i for i in range(na)},
        compiler_params=SIDE_EFFECT)(*shards, *lands, after)
    send_sems, recv_ici, recv_d2d = out[:3]
    state = dict(send=send_sems, ici=recv_ici, d2d=recv_d2d, shards=list(shards), lands=out[3:3 + na])
    return state, out[-1]


def _gather_forward(state, after):
    lands = state["lands"]
    na = len(lands)

    def body(*refs):
        land = refs[:na]
        recv_ici, recv_d2d = refs[na], refs[na + 1]
        fwd_sems, token = refs[-2], refs[-1]
        x, y, c = _place()
        for j, chip in enumerate([(1 - x, y), (x, 1 - y), (1 - x, 1 - y)]):
            for t in range(na):
                blk = land[t].at[_block_of(*chip, c)]
                pltpu.make_async_remote_copy(
                    src_ref=blk, dst_ref=blk, send_sem=fwd_sems.at[3 * t + j], recv_sem=recv_ici.at[3 * t + j],
                    device_id=(x, y, c), device_id_type=MESH).wait_recv()
                pltpu.make_async_remote_copy(
                    src_ref=blk, dst_ref=blk, send_sem=fwd_sems.at[3 * t + j], recv_sem=recv_d2d.at[4 * t + 1 + j],
                    device_id=(x, y, 1 - c), device_id_type=MESH).start()
        token[...] = jnp.zeros_like(token)

    out = pl.pallas_call(
        body, name="gather_forward",
        out_shape=(*_hbm_like(lands), pltpu.SemaphoreType.DMA((3 * na,)), TOKEN),
        in_specs=[HBM] * na + [SEM, SEM, ANY],
        out_specs=(*[HBM] * na, SEM, pl.BlockSpec(memory_space=pltpu.VMEM)),
        input_output_aliases={i: i for i in range(na)},
        compiler_params=SIDE_EFFECT)(*lands, state["ici"], state["d2d"], after)
    return dict(state, lands=out[:na], fwd=out[na]), out[-1]


def _gather_wait(state, after):
    shards, lands = state["shards"], state["lands"]
    na = len(lands)

    def body(*refs):
        ins, land = refs[:na], refs[na:2 * na]
        send_sems, fwd_sems, recv_d2d = refs[2 * na:2 * na + 3]
        x, y, c = _place()
        chips = [(1 - x, y), (x, 1 - y), (1 - x, 1 - y)]
        for t in range(na):
            mine = land[t].at[_block_of(x, y, c)]
            for k in range(4):
                pltpu.make_async_remote_copy(
                    src_ref=ins[t], dst_ref=mine, send_sem=send_sems.at[4 * t + k], recv_sem=recv_d2d.at[4 * t],
                    device_id=(x, y, c), device_id_type=MESH).wait_send()
            for j, chip in enumerate(chips):
                blk = land[t].at[_block_of(*chip, c)]
                pltpu.make_async_remote_copy(
                    src_ref=blk, dst_ref=blk, send_sem=fwd_sems.at[3 * t + j], recv_sem=recv_d2d.at[4 * t + 1 + j],
                    device_id=(x, y, c), device_id_type=MESH).wait_send()
            for k, blk_id in enumerate([_block_of(x, y, 1 - c)] + [_block_of(*chip, 1 - c) for chip in chips]):
                blk = land[t].at[blk_id]
                pltpu.make_async_remote_copy(
                    src_ref=blk, dst_ref=blk, send_sem=send_sems.at[4 * t], recv_sem=recv_d2d.at[4 * t + k],
                    device_id=(x, y, c), device_id_type=MESH).wait_recv()

    out = pl.pallas_call(
        body, name="gather_wait",
        out_shape=_hbm_like(lands),
        in_specs=[ANY] * na + [HBM] * na + [SEM, SEM, SEM, ANY],
        out_specs=tuple([HBM] * na),
        input_output_aliases={na + i: i for i in range(na)},
        compiler_params=SIDE_EFFECT)(*shards, *lands, state["send"], state["fwd"], state["d2d"], after)
    return out


def _to_sibling(srcs, lands, send_sems, recv_sems):
    x, y, c = _place()
    return [pltpu.make_async_remote_copy(
        src_ref=srcs[t].at[:, 1 - c], dst_ref=lands[t], send_sem=send_sems.at[t], recv_sem=recv_sems.at[t],
        device_id=(x, y, 1 - c), device_id_type=MESH) for t in range(len(srcs))]


def _to_chips(srcs, lands, send_sems, recv_sems):
    x, y, c = _place()
    copies = []
    for k in (1, 2, 3):
        px, py = x ^ (k >> 1), y ^ (k & 1)
        copies += [pltpu.make_async_remote_copy(
            src_ref=srcs[t].at[2 * px + py], dst_ref=lands[t].at[k - 1], send_sem=send_sems.at[3 * t + k - 1],
            recv_sem=recv_sems.at[3 * t + k - 1], device_id=(px, py, c), device_id_type=MESH) for t in range(len(srcs))]
    return copies


def _exchange_start(name, srcs, land_shapes, copies, per_array, after):
    na = len(srcs)
    lands = [_hbm(lax.empty(shp, a.dtype)) for shp, a in zip(land_shapes, srcs)]

    def body(*refs):
        token = refs[-1]
        for cp in copies(refs[:na], refs[na:2 * na], refs[2 * na + 1], refs[2 * na + 2]):
            cp.start()
        token[...] = jnp.zeros_like(token)

    out = pl.pallas_call(
        body, name=name,
        out_shape=(pltpu.SemaphoreType.DMA((na * per_array,)), pltpu.SemaphoreType.DMA((na * per_array,)),
                   *_hbm_like(lands), TOKEN),
        in_specs=[ANY] * na + [HBM] * na + [ANY],
        out_specs=(SEM, SEM, *[HBM] * na, pl.BlockSpec(memory_space=pltpu.VMEM)),
        input_output_aliases={na + i: 2 + i for i in range(na)},
        compiler_params=SIDE_EFFECT)(*srcs, *lands, after)
    return dict(send=out[0], recv=out[1], srcs=list(srcs), lands=out[2:2 + na]), out[-1]


def _exchange_wait(name, state, copies, after):
    srcs, lands = state["srcs"], state["lands"]
    na = len(srcs)

    def body(*refs):
        for cp in copies(refs[:na], refs[na:2 * na], refs[2 * na], refs[2 * na + 1]):
            cp.wait_send()
            cp.wait_recv()

    out = pl.pallas_call(
        body, name=name,
        out_shape=_hbm_like(lands),
        in_specs=[ANY] * na + [HBM] * na + [SEM, SEM, ANY],
        out_specs=tuple([HBM] * na),
        input_output_aliases={na + i: i for i in range(na)},
        compiler_params=SIDE_EFFECT)(*srcs, *lands, state["send"], state["recv"], after)
    return out


def _adamw_math(w, g, m, v):
    m = ADAM_B1 * m + (1.0 - ADAM_B1) * g
    v = ADAM_B2 * v + (1.0 - ADAM_B2) * (g * g)
    m_hat = m / (1.0 - ADAM_B1 ** ADAM_STEP)
    v_hat = v / (1.0 - ADAM_B2 ** ADAM_STEP)
    return -ADAM_LR * (m_hat / (jnp.sqrt(v_hat) + ADAM_EPS) + ADAM_WD * w), m, v


def _adamw(own, others, w, m, v, tr, name):
    rows, cols = w.shape
    blk = pl.BlockSpec((tr, cols), lambda i: (i, 0))

    def body(own_ref, oth_ref, w_ref, m_ref, v_ref, g_ref, d_ref, nm_ref, nv_ref):
        g = own_ref[...]
        for k in range(3):
            g = g + oth_ref[k].astype(F32)
        g_ref[...] = g
        d_ref[...], nm_ref[...], nv_ref[...] = _adamw_math(w_ref[...], g, m_ref[...], v_ref[...])

    out = jax.ShapeDtypeStruct((rows, cols), F32)
    return pl.pallas_call(
        body, name=name, out_shape=(out, out, out, out), grid=(rows // tr,),
        in_specs=[blk, pl.BlockSpec((3, tr, cols), lambda i: (0, i, 0)), blk, blk, blk],
        out_specs=(blk, blk, blk, blk),
        compiler_params=_params("parallel"))(own, others, w, m, v)


def _adamw_slab(w, g, m, v):
    def body(w_ref, g_ref, m_ref, v_ref, d_ref, nm_ref, nv_ref):
        d_ref[...], nm_ref[...], nv_ref[...] = _adamw_math(w_ref[...], g_ref[...], m_ref[...], v_ref[...])

    out = jax.ShapeDtypeStruct(w.shape, F32)
    vmem = pl.BlockSpec(memory_space=pltpu.VMEM)
    return pl.pallas_call(body, name="adamw_small", out_shape=(out, out, out),
                          in_specs=[vmem] * 4, out_specs=(vmem, vmem, vmem))(w, g, m, v)


def _row(v, width=D_MODEL):
    v = v.reshape(1, -1)
    return jnp.pad(v, ((0, 0), (0, width - v.shape[1])))


def _tables(s, gq, gk, conv_w):
    gq2 = jnp.tile(gq.reshape(1, HEAD), (1, 2))
    gk2 = jnp.tile(gk.reshape(1, HEAD), (1, 2))
    conv_wp = jnp.pad(conv_w, ((0, SUBLANES - conv_w.shape[0]), (0, 0)))
    return _rope_tables(s), gq2, gk2, conv_wp


def _forward_in(x, g1, w_qkv, w_rest):
    s = x.shape[0]
    h, ht, z_a = _fwd_in_a(x, g1, w_qkv, min(512, s))
    z_b = _mm_nn(h, w_rest, min(512, s), 1024, "fwd_in_b")
    return ht, z_a, z_b


def _forward_attn(z_a, z_b, rope, gq2, gk2, conv_wp, sinks):
    s = z_a.shape[0]
    qn, k2, v2 = _qk_prep(z_a, *rope, gq2, gk2, min(256, s))
    a, mix, mixt = _attn_fwd(qn, k2, v2, z_b, conv_wp, sinks)
    return qn, k2, v2, a, mix, mixt


def _forward_out(x, p, target, mix, mixt, w_out, g2, w_pg, b_pg, w_pp, g3):
    s = x.shape[0]
    tm = min(512, s)
    x1, hn2, hn2t = _fwd_out(mix, w_out, x, g2, tm)
    dy, dgp, dt, pt, acc_ple = _ple(hn2, w_pg, b_pg, p, w_pp, g3, x1, target, min(256, s))
    dx1, dx1b, acc_g2 = _gate_bwd(dgp, w_pg, x1, dy, g2, tm)
    gw_out = _mm_grad(mixt, [dx1b], 512, "grad_w_out")
    gw_pg = _mm_grad(hn2t, [dgp], 512, "grad_w_ple_gate")
    gw_pp = _mm_grad(pt, [dt], 512, "grad_w_ple_proj")
    return dx1, dx1b, (gw_out, gw_pg, gw_pp), acc_ple, acc_g2


def _backward_attn(dmix, ht, z_a, z_b, qn, k2, v2, a, rope, gq2, gk2, conv_wp, sinks):
    dq, dkc, dkp, dvc, dvp, dz_b, acc_attn = _attn_bwd(qn, k2, v2, a, z_b, dmix, conv_wp, sinks)
    dz_a, acc_qk = _qkv_bwd(z_a, dq, dkc, dkp, dvc, dvp, *rope, gq2, gk2)
    gw_in = _mm_grad(ht, [dz_a, dz_b], 512, "grad_w_in")
    return dz_a, dz_b, gw_in, acc_attn, acc_qk


def _small_rows(acc_g1, acc_g2, acc_ple, acc_qk, acc_attn):
    fold = lambda v: _row((v[:HEAD] + v[HEAD:]))
    return [acc_g1[0:1], acc_g2[0:1], acc_ple[0:1], acc_ple[1:2], fold(acc_qk[0]), fold(acc_qk[1]),
            _row(acc_attn[0, :N_Q_HEADS]), _row(acc_attn[1]), _row(acc_attn[2]), _row(acc_attn[3]), acc_ple[2:3]]


def _local_step(x, p, target, g1, w_qkv, w_rest, gq, gk, sinks, conv_w, w_out, g2, w_pg, b_pg, w_pp, g3):
    rope, gq2, gk2, conv_wp = _tables(x.shape[0], gq, gk, conv_w)
    ht, z_a, z_b = _forward_in(x, g1, w_qkv, w_rest)
    qn, k2, v2, a, mix, mixt = _forward_attn(z_a, z_b, rope, gq2, gk2, conv_wp, sinks)
    dx1, dx1b, (gw_out, gw_pg, gw_pp), acc_ple, acc_g2 = _forward_out(
        x, p, target, mix, mixt, w_out, g2, w_pg, b_pg, w_pp, g3)
    dmix = _mm_nt(dx1b, w_out, min(512, x.shape[0]), "out_bwd")
    dz_a, dz_b, gw_in, acc_attn, acc_qk = _backward_attn(
        dmix, ht, z_a, z_b, qn, k2, v2, a, rope, gq2, gk2, conv_wp, sinks)
    grad_x, acc_g1 = _in_bwd(dz_a, dz_b, w_qkv, w_rest, x, dx1, g1, min(512, x.shape[0]))
    return grad_x, (gw_in, gw_out, gw_pg, gw_pp), _small_rows(acc_g1, acc_g2, acc_ple, acc_qk, acc_attn)


ROW_CONV, ROW_LOSS = 7, 10


def _slab(rows):
    rows = list(rows)
    return jnp.concatenate(rows + [jnp.zeros((SLAB_ROWS - len(rows), D_MODEL), F32)], axis=0)


def _by_owner(g):
    return g.reshape((4, 2) + g.shape[1:])


def kernel(x, p, norm_gain, w_in, q_norm_gain, k_norm_gain, attn_sinks, conv_w, w_out, ple_gate_norm_gain, w_ple_gate, b_ple_gate, w_ple_proj, ple_norm_gain, loss_target, m_norm_gain, m_w_in, m_q_norm_gain, m_k_norm_gain, m_attn_sinks, m_conv_w, m_w_out, m_ple_gate_norm_gain, m_w_ple_gate, m_b_ple_gate, m_w_ple_proj, m_ple_norm_gain, v_norm_gain, v_w_in, v_q_norm_gain, v_k_norm_gain, v_attn_sinks, v_conv_w, v_w_out, v_ple_gate_norm_gain, v_w_ple_gate, v_b_ple_gate, v_w_ple_proj, v_ple_norm_gain):
    me = 4 * lax.axis_index("x") + 2 * lax.axis_index("y") + lax.axis_index("c")
    place = jnp.stack([lax.axis_index("c"), 2 * lax.axis_index("x") + lax.axis_index("y")]).astype(jnp.int32)
    conv_cols = conv_w.shape[2]
    xs, ps, target = x[0], p[0, 0], loss_target[0]
    zero = lambda token: token[0:1, 0:1]

    own_in = w_in[0].astype(BF16)
    own_late = [w_out[0].astype(BF16), w_ple_gate[0].astype(BF16), w_ple_proj[0].astype(BF16)]
    with_own = lambda gathered, own: lax.dynamic_update_slice(gathered, own[None], (me, 0, 0))
    (g_in,) = _all_gather([own_in])
    late, started = _gather_start(own_late, g_in)
    g_in = with_own(g_in, own_in)
    split = QKV_W - SHARD_IN
    w_qkv = jnp.concatenate([g_in[0], g_in[1][:, :split]], axis=1)
    w_rest = jnp.concatenate([g_in[1][:, split:]] + [g_in[d] for d in range(2, N_DEV)], axis=1)
    conv_rows = [lax.dynamic_update_slice(jnp.zeros((1, D_MODEL), F32), conv_w[0, t:t + 1], (0, conv_cols * me))
                 for t in range(3)]
    conv_full = _all_reduce_slab(_slab(conv_rows), "gather_conv_w")[0:3, :ATTN_W]
    rope, gq2, gk2, conv_wp = _tables(xs.shape[0], q_norm_gain[0], k_norm_gain[0], conv_full)

    g1 = norm_gain + zero(started)
    ht, z_a, z_b = _forward_in(xs, g1, w_qkv, w_rest)
    late, forwarded = _gather_forward(late, z_b)
    qn, k2, v2, a, mix, mixt = _forward_attn(z_a, z_b, rope, gq2 + zero(forwarded), gk2, conv_wp, attn_sinks)
    g_out, g_pg, g_pp = (with_own(g, own) for g, own in zip(_gather_wait(late, mix), own_late))
    w_out_f = g_out.reshape(D_MODEL, D_MODEL)
    w_pg_f = g_pg.reshape(D_MODEL, D_MODEL)
    w_pp_f = jnp.transpose(g_pp, (1, 0, 2)).reshape(PLE_DIM, D_MODEL)

    dx1, dx1b, (gw_out, gw_pg, gw_pp), acc_ple, acc_g2 = _forward_out(
        xs, ps, target, mix, mixt, w_out_f, ple_gate_norm_gain, w_pg_f, b_ple_gate, w_pp_f, ple_norm_gain)

    names = ("w_out", "w_ple_gate", "w_ple_proj")
    gw_pp_t = jnp.transpose(gw_pp.reshape(PLE_DIM, N_DEV, PLE_DIM), (1, 0, 2))
    grads = [_by_owner(gw_out.reshape(N_DEV, D_MODEL // N_DEV, D_MODEL)),
             _by_owner(gw_pg.reshape(N_DEV, D_MODEL // N_DEV, D_MODEL)), _by_owner(gw_pp_t)]
    pairs, _ = _exchange_start("pair_start", grads, [(4,) + g.shape[2:] for g in grads], _to_sibling, 1, dx1b)
    dmix = _mm_nt(dx1b, w_out_f, min(512, xs.shape[0]), "out_bwd")
    from_sibling = _exchange_wait("pair_wait", pairs, _to_sibling, dmix)
    sums = [_pair_sum(g, r, place, 256, "pair_sum_" + nm) for g, r, nm in zip(pairs["srcs"], from_sibling, names)]
    chips, sent = _exchange_start("chip_start", [pb for pb, _ in sums], [(3,) + pb.shape[1:] for pb, _ in sums],
                                  _to_chips, 3, sums[-1][1])

    dz_a, dz_b, gw_in, acc_attn, acc_qk = _backward_attn(
        dmix, ht, z_a, z_b, qn, k2, v2, a, rope, gq2, gk2, conv_wp, attn_sinks + zero(sent))

    gw_in_t = [_by_owner(jnp.transpose(gw_in.reshape(D_MODEL, N_DEV, SHARD_IN), (1, 0, 2)))]
    pairs_in, _ = _exchange_start("pair_start_w_in", gw_in_t, [(4,) + gw_in_t[0].shape[2:]], _to_sibling, 1, gw_in)
    from_chips = _exchange_wait("chip_wait", chips, _to_chips, gw_in)
    big = {}
    for (_, own), oth, w, m, v, nm in zip(sums, from_chips, (w_out, w_ple_gate, w_ple_proj),
                                          (m_w_out, m_w_ple_gate, m_w_ple_proj),
                                          (v_w_out, v_w_ple_gate, v_w_ple_proj), names):
        big[nm] = [t[None] for t in _adamw(own, oth, w[0], m[0], v[0], 256, "adamw_" + nm)]

    (from_sibling_in,) = _exchange_wait("pair_wait_w_in", pairs_in, _to_sibling, big[names[-1]][0])
    pb_in, own_in = _pair_sum(pairs_in["srcs"][0], from_sibling_in, place, 256, "pair_sum_w_in")
    chips_in, sent_in = _exchange_start("chip_start_w_in", [pb_in], [(3,) + pb_in.shape[1:]], _to_chips, 3, own_in)
    grad_x, acc_g1 = _in_bwd(dz_a, dz_b, w_qkv, w_rest, xs, dx1, norm_gain + zero(sent_in), min(512, xs.shape[0]))
    (from_chips_in,) = _exchange_wait("chip_wait_w_in", chips_in, _to_chips, grad_x)
    big["w_in"] = [t[None] for t in _adamw(own_in, from_chips_in, w_in[0], m_w_in[0], v_w_in[0], 256, "adamw_w_in")]

    red = _all_reduce_slab(_slab(_small_rows(acc_g1, acc_g2, acc_ple, acc_qk, acc_attn)), "reduce_small")
    loss = jnp.sum(red[ROW_LOSS])
    g_conv = [lax.dynamic_slice(red[ROW_CONV + t:ROW_CONV + t + 1], (0, conv_cols * me), (1, conv_cols))
              for t in range(3)]
    small = [norm_gain, ple_gate_norm_gain, b_ple_gate, ple_norm_gain, q_norm_gain, k_norm_gain, attn_sinks]
    small_m = [m_norm_gain, m_ple_gate_norm_gain, m_b_ple_gate, m_ple_norm_gain, m_q_norm_gain, m_k_norm_gain,
               m_attn_sinks]
    small_v = [v_norm_gain, v_ple_gate_norm_gain, v_b_ple_gate, v_ple_norm_gain, v_q_norm_gain, v_k_norm_gain,
               v_attn_sinks]
    pack = lambda vs, cw: _slab([_row(t) for t in vs] + [_row(cw[0, t]) for t in range(3)])
    g_slab = _slab([red[t:t + 1] for t in range(ROW_CONV)] + [_row(t) for t in g_conv])
    d_slab, m_slab, v_slab = _adamw_slab(pack(small, conv_w), g_slab, pack(small_m, m_conv_w), pack(small_v, v_conv_w))

    def unpack(slab_):
        outs = [slab_[t:t + 1, :w.shape[1]] for t, w in enumerate(small)]
        return outs, slab_[ROW_CONV:ROW_CONV + 3, :conv_cols][None]

    (g_s, g_cv), (d_s, d_cv), (m_s, m_cv), (v_s, v_cv) = (unpack(t) for t in (g_slab, d_slab, m_slab, v_slab))

    def order(sm, cv, k):
        return [sm[0], big["w_in"][k], sm[4], sm[5], sm[6], cv, big["w_out"][k], sm[1], big["w_ple_gate"][k], sm[2],
                big["w_ple_proj"][k], sm[3]]

    return (loss, grad_x[None], *order(g_s, g_cv, 0), *order(d_s, d_cv, 1), *order(m_s, m_cv, 2),
            *order(v_s, v_cv, 3))
```

```python
import jax
import jax.numpy as jnp
from jax import lax
from jax.experimental import pallas as pl
from jax.experimental.pallas import tpu as pltpu

F32, BF16 = jnp.float32, jnp.bfloat16

D_MODEL = 2048
PLE_DIM = 256
ATTN_W = 1024
HEAD = 64
N_Q_HEADS = 16
KV_W = 256
QKV_W = ATTN_W + 2 * KV_W
REST_W = 5 * 1024
IN_W = QKV_W + REST_W
GATE_A0, CONV_B0, CONV_C0, CONV_H0, GATE_C0 = (QKV_W + 1024 * t for t in range(5))
K2_W = 4 * 128
ROT = 16
ROPE_THETA = 500000.0
EPS = 1e-6
NEG_INF = -1e30
BLK = 128
LANES = 128
SUBLANES = 8
N_DEV = 8
SHARD_IN = IN_W // N_DEV
PAIR_W = 2 * SHARD_IN
N_PAIRS = IN_W // PAIR_W
SLAB_ROWS = 16
SUB_ROWS = 128
V7X_VMEM_LIMIT = 52 * 1024 * 1024

ADAM_LR, ADAM_B1, ADAM_B2, ADAM_EPS, ADAM_WD, ADAM_STEP = 0.001, 0.9, 0.999, 1e-08, 0.01, 10
MESH = pl.DeviceIdType.MESH


def _params(*semantics):
    return pltpu.CompilerParams(dimension_semantics=semantics, vmem_limit_bytes=V7X_VMEM_LIMIT)


ANY = pl.BlockSpec(memory_space=pl.ANY)


def _resident(shape):
    return pl.BlockSpec(shape, lambda *_: (0,) * len(shape), pipeline_mode=pl.Buffered(1))


def _dot(a, b):
    return jnp.dot(a, b, preferred_element_type=F32)


def _dot_nt(a, b):
    return lax.dot_general(a, b, (((1,), (1,)), ((), ())), preferred_element_type=F32)


def _rms(xf):
    r = lax.rsqrt(jnp.mean(xf * xf, axis=-1, keepdims=True) + EPS)
    return xf * r, r


def _rms_bwd(dxn, xn, r):
    return r * (dxn - xn * jnp.mean(dxn * xn, axis=-1, keepdims=True))


def _sig(g):
    return jax.nn.sigmoid(g)


def _dsilu(g, sg):
    return sg * (1.0 + g * (1.0 - sg))


def _low_half(shape):
    return lax.broadcasted_iota(jnp.int32, shape, len(shape) - 1) < HEAD


def _half_sums(v):
    lo = _low_half(v.shape)
    s_lo = jnp.sum(jnp.where(lo, v, 0.0), axis=-1, keepdims=True)
    s_hi = jnp.sum(jnp.where(lo, 0.0, v), axis=-1, keepdims=True)
    return jnp.where(lo, s_lo, s_hi)


def _rope(v, a, bm, bp):
    return v * a + pltpu.roll(v, LANES - ROT // 2, 1) * bm + pltpu.roll(v, ROT // 2, 1) * bp


def _rope_t(dy, a, bm, bp):
    return dy * a + pltpu.roll(dy * bm, ROT // 2, 1) + pltpu.roll(dy * bp, LANES - ROT // 2, 1)


def _dup_halves(v):
    lo = _low_half(v.shape)
    a = jnp.where(lo, v, 0.0)
    b = jnp.where(lo, 0.0, v)
    return a + pltpu.roll(a, HEAD, 1), b + pltpu.roll(b, HEAD, 1)


def _rope_tables(s):
    half = ROT // 2
    lane = lax.broadcasted_iota(jnp.int32, (s, LANES), 1) % HEAD
    pos = lax.broadcasted_iota(jnp.int32, (s, LANES), 0).astype(F32)
    inv_freq = jnp.power(jnp.float32(ROPE_THETA), -(lane % half).astype(F32) * 2.0 / ROT)
    ang = pos * inv_freq
    cos, sin = jnp.cos(ang), jnp.sin(ang)
    a = jnp.where(lane < ROT, cos, 1.0)
    bm = jnp.where(lane < half, -sin, 0.0)
    bp = jnp.where((lane >= half) & (lane < ROT), sin, 0.0)
    return a, bm, bp


def _prenorm(x, g1, tm):
    s = x.shape[0]

    def body(x_ref, g_ref, h_ref, ht_ref):
        xn, _ = _rms(x_ref[...])
        h = (xn * g_ref[...]).astype(BF16)
        h_ref[...] = h
        ht_ref[...] = h.T

    return pl.pallas_call(
        body, name="prenorm",
        out_shape=(jax.ShapeDtypeStruct((s, D_MODEL), BF16), jax.ShapeDtypeStruct((D_MODEL, s), BF16)),
        grid=(s // tm,),
        in_specs=[pl.BlockSpec((tm, D_MODEL), lambda i: (i, 0)), pl.BlockSpec((1, D_MODEL), lambda i: (0, 0))],
        out_specs=(pl.BlockSpec((tm, D_MODEL), lambda i: (i, 0)), pl.BlockSpec((D_MODEL, tm), lambda i: (0, i))),
        compiler_params=_params("parallel"))(x, g1)


def _fwd_in_pair(h, shards, z, w_pairs, pair, tm, name):
    s = h.shape[0]

    def body(pair_ref, h_ref, lo_ref, hi_ref, z_in, wp_in, z_ref, wp_ref):
        @pl.when(pl.program_id(0) == 0)
        def _():
            wp_ref[0] = jnp.concatenate([lo_ref[0], hi_ref[0]], axis=1)

        z_ref[...] = _dot(h_ref[...], wp_ref[0])

    grid_spec = pltpu.PrefetchScalarGridSpec(
        num_scalar_prefetch=1, grid=(s // tm,),
        in_specs=[pl.BlockSpec((tm, D_MODEL), lambda i, p: (i, 0)),
                  pl.BlockSpec((1, D_MODEL, SHARD_IN), lambda i, p: (2 * p[0], 0, 0)),
                  pl.BlockSpec((1, D_MODEL, SHARD_IN), lambda i, p: (2 * p[0] + 1, 0, 0)), ANY, ANY],
        out_specs=(pl.BlockSpec((tm, PAIR_W), lambda i, p: (i, p[0])),
                   pl.BlockSpec((1, D_MODEL, PAIR_W), lambda i, p: (p[0], 0, 0))))
    return pl.pallas_call(
        body, name=name, grid_spec=grid_spec,
        out_shape=(jax.ShapeDtypeStruct(z.shape, z.dtype), jax.ShapeDtypeStruct(w_pairs.shape, w_pairs.dtype)),
        input_output_aliases={4: 0, 5: 1},
        compiler_params=_params("arbitrary"))(pair, h, shards, shards, z, w_pairs)


def _qk_prep(z, ra, rbm, rbp, gq2, gk2, tm):
    s = z.shape[0]

    def body(z_ref, a_ref, bm_ref, bp_ref, gq_ref, gk_ref, q_ref, k2_ref, v2_ref):
        a, bm, bp = a_ref[...], bm_ref[...], bp_ref[...]
        for r in range(ATTN_W // LANES):
            x = z_ref[:, LANES * r:LANES * (r + 1)]
            rr = lax.rsqrt(_half_sums(x * x) * (1.0 / HEAD) + EPS)
            q_ref[:, LANES * r:LANES * (r + 1)] = _rope(x * rr * gq_ref[...], a, bm, bp).astype(BF16)
        for m in range(KV_W // LANES):
            x = z_ref[:, ATTN_W + LANES * m:ATTN_W + LANES * (m + 1)]
            rr = lax.rsqrt(_half_sums(x * x) * (1.0 / HEAD) + EPS)
            k_lo, k_hi = _dup_halves(_rope(x * rr * gk_ref[...], a, bm, bp))
            k2_ref[:, 2 * LANES * m:2 * LANES * m + LANES] = k_lo.astype(BF16)
            k2_ref[:, 2 * LANES * m + LANES:2 * LANES * (m + 1)] = k_hi.astype(BF16)
            v_lo, v_hi = _dup_halves(z_ref[:, ATTN_W + KV_W + LANES * m:ATTN_W + KV_W + LANES * (m + 1)])
            v2_ref[:, 2 * LANES * m:2 * LANES * m + LANES] = v_lo.astype(BF16)
            v2_ref[:, 2 * LANES * m + LANES:2 * LANES * (m + 1)] = v_hi.astype(BF16)

    row = lambda w: pl.BlockSpec((tm, w), lambda i: (i, 0))
    one = pl.BlockSpec((1, LANES), lambda i: (0, 0))
    return pl.pallas_call(
        body, name="qk_prep",
        out_shape=(jax.ShapeDtypeStruct((s, ATTN_W), BF16), jax.ShapeDtypeStruct((s, K2_W), BF16),
                   jax.ShapeDtypeStruct((s, K2_W), BF16)),
        grid=(s // tm,),
        in_specs=[row(PAIR_W), row(LANES), row(LANES), row(LANES), one, one],
        out_specs=(row(ATTN_W), row(K2_W), row(K2_W)),
        compiler_params=_params("parallel"))(z, ra, rbm, rbp, gq2, gk2)


GROUP = 4


def _window_mask(n):
    row = lax.broadcasted_iota(jnp.int32, (GROUP * BLK, 2 * BLK), 0) % BLK
    col = lax.broadcasted_iota(jnp.int32, (GROUP * BLK, 2 * BLK), 1)
    return (col > row) & (col <= row + BLK) & ((col >= BLK) | (n > 0))


def _stack_heads(pairs, zero):
    lo = _low_half(pairs[0].shape)
    parts = []
    for v in pairs:
        parts += [jnp.where(lo, v, zero), jnp.where(lo, zero, v)]
    return jnp.concatenate(parts, axis=0)


def _unstack_heads(v4):
    lo = _low_half((BLK, LANES))
    return [jnp.where(lo, v4[2 * i * BLK:(2 * i + 1) * BLK], v4[(2 * i + 1) * BLK:(2 * i + 2) * BLK]) for i in range(2)]


def _group_sinks(sink_ref, kvh):
    slot = lax.broadcasted_iota(jnp.int32, (GROUP * BLK, 1), 0) // BLK
    col = jnp.zeros((GROUP * BLK, 1), F32)
    for i in range(GROUP):
        col = jnp.where(slot == i, sink_ref[0, GROUP * kvh + i], col)
    return col, slot


def _head_probs(qm, kw, valid, sink):
    sc = jnp.where(valid, _dot_nt(qm, kw) * (HEAD ** -0.5), NEG_INF)
    mx = jnp.maximum(jnp.max(sc, axis=-1, keepdims=True), sink)
    ex = jnp.exp(sc - mx)
    den = jnp.sum(ex, axis=-1, keepdims=True) + jnp.exp(sink - mx)
    return ex / den, mx, den


def _cols(start, width=ATTN_W):
    return slice(start, start + width)


def _conv_fwd(z_ref, zp_ref, cw_ref, ext_ref, n):
    u = z_ref[:, _cols(CONV_C0)] * z_ref[:, _cols(CONV_H0)]
    pu = zp_ref[:, _cols(CONV_C0)] * zp_ref[:, _cols(CONV_H0)]
    ext_ref[0:SUBLANES, :] = jnp.where(n > 0, pu, 0.0)
    ext_ref[SUBLANES:SUBLANES + BLK, :] = u
    um1 = ext_ref[SUBLANES - 1:SUBLANES - 1 + BLK, :]
    um2 = ext_ref[SUBLANES - 2:SUBLANES - 2 + BLK, :]
    cv = cw_ref[0:1, :] * um2 + cw_ref[1:2, :] * um1 + cw_ref[2:3, :] * u
    return u, um1, um2, cv


def _prev_rows(n):
    return (jnp.maximum(n * (BLK // SUBLANES) - 1, 0), 0)


def _attn_fwd(qn, k2, v2, z, conv_wp, sinks):
    s = qn.shape[0]
    nb = s // BLK

    def body(sink_ref, q_ref, kc_ref, kp_ref, vc_ref, vp_ref, z_ref, zp_ref, cw_ref, a_ref, mix_ref, mixt_ref,
             ext_ref):
        n = pl.program_id(0)
        valid = _window_mask(n)
        for kvh in range(K2_W // LANES):
            cols = slice(LANES * kvh, LANES * (kvh + 1))
            kw = jnp.concatenate([kp_ref[:, cols], kc_ref[:, cols]], axis=0)
            vw = jnp.concatenate([vp_ref[:, cols], vc_ref[:, cols]], axis=0)
            blocks = [slice(LANES * r, LANES * (r + 1)) for r in (2 * kvh, 2 * kvh + 1)]
            q4 = _stack_heads([q_ref[:, rc] for rc in blocks], jnp.zeros((BLK, LANES), BF16))
            p, _, _ = _head_probs(q4, kw, valid, _group_sinks(sink_ref, kvh)[0])
            for rc, a in zip(blocks, _unstack_heads(_dot(p.astype(BF16), vw))):
                a_ref[:, rc] = a
                g = z_ref[:, _cols(GATE_A0 + rc.start, LANES)]
                mix_ref[:, rc] = (a * (g * _sig(g))).astype(BF16)
        _, _, _, cv = _conv_fwd(z_ref, zp_ref, cw_ref, ext_ref, n)
        gc = z_ref[:, _cols(GATE_C0)]
        mix_ref[:, ATTN_W:D_MODEL] = (z_ref[:, _cols(CONV_B0)] * cv * (gc * _sig(gc))).astype(BF16)
        mixt_ref[...] = mix_ref[...].T

    cur = lambda w: pl.BlockSpec((BLK, w), lambda n: (n, 0))
    prev = lambda w: pl.BlockSpec((BLK, w), lambda n: (jnp.maximum(n - 1, 0), 0))
    return pl.pallas_call(
        body, name="attn_fwd",
        out_shape=(jax.ShapeDtypeStruct((s, ATTN_W), F32), jax.ShapeDtypeStruct((s, D_MODEL), BF16),
                   jax.ShapeDtypeStruct((D_MODEL, s), BF16)),
        grid=(nb,),
        in_specs=[pl.BlockSpec(memory_space=pltpu.SMEM),
                  cur(ATTN_W), cur(K2_W), prev(K2_W), cur(K2_W), prev(K2_W), cur(IN_W),
                  pl.BlockSpec((SUBLANES, IN_W), _prev_rows),
                  pl.BlockSpec((SUBLANES, ATTN_W), lambda n: (0, 0))],
        out_specs=(cur(ATTN_W), cur(D_MODEL), pl.BlockSpec((D_MODEL, BLK), lambda n: (0, n))),
        scratch_shapes=[pltpu.VMEM((BLK + 2 * SUBLANES, ATTN_W), F32)],
        compiler_params=_params("parallel"))(sinks, qn, k2, k2, v2, v2, z, z, conv_wp)


def _fwd_out(mix, w_out, x, g2, tm):
    s = x.shape[0]

    def body(m_ref, w_ref, x_ref, g_ref, x1_ref, h_ref, ht_ref):
        x1 = x_ref[...] + _dot(m_ref[...], w_ref[...])
        x1_ref[...] = x1
        xn, _ = _rms(x1)
        h = (xn * g_ref[...]).astype(BF16)
        h_ref[...] = h
        ht_ref[...] = h.T

    row = pl.BlockSpec((tm, D_MODEL), lambda i: (i, 0))
    return pl.pallas_call(
        body, name="fwd_out",
        out_shape=(jax.ShapeDtypeStruct((s, D_MODEL), F32), jax.ShapeDtypeStruct((s, D_MODEL), BF16),
                   jax.ShapeDtypeStruct((D_MODEL, s), BF16)),
        grid=(s // tm,),
        in_specs=[row, _resident((D_MODEL, D_MODEL)), row, pl.BlockSpec((1, D_MODEL), lambda i: (0, 0))],
        out_specs=(row, row, pl.BlockSpec((D_MODEL, tm), lambda i: (0, i))),
        compiler_params=_params("parallel"))(mix, w_out, x, g2)


def _ple(hn2, w_pg, b_pg, p, w_pp, g3, x1, target, tm):
    s = x1.shape[0]

    def body(h_ref, wg_ref, b_ref, p_ref, wp_ref, g3_ref, x1_ref, t_ref, dy_ref, dgp_ref, dt_ref, pt_ref, acc_ref):
        gate = _sig(_dot(h_ref[...], wg_ref[...]) + b_ref[...])
        pb = p_ref[...].astype(BF16)
        pt_ref[...] = pb.T
        t = _dot(pb, wp_ref[...])
        tn, r3 = _rms(t)
        e = tn * g3_ref[...]
        diff = x1_ref[...] + gate * e - t_ref[...]
        dy = diff * (1.0 / D_MODEL)
        dy_ref[...] = dy
        dgp = dy * e * (gate * (1.0 - gate))
        dgp_ref[...] = dgp.astype(BF16)
        de = dy * gate
        dt_ref[...] = _rms_bwd(de * g3_ref[...], tn, r3).astype(BF16)

        @pl.when(pl.program_id(0) == 0)
        def _():
            acc_ref[...] = jnp.zeros_like(acc_ref)

        acc_ref[0:1, :] += jnp.sum(dgp, axis=0, keepdims=True)
        acc_ref[1:2, :] += jnp.sum(de * tn, axis=0, keepdims=True)
        acc_ref[2:3, :] += jnp.sum(diff * diff, axis=0, keepdims=True) * (0.5 / D_MODEL)

    row = pl.BlockSpec((tm, D_MODEL), lambda i: (i, 0))
    vec = pl.BlockSpec((1, D_MODEL), lambda i: (0, 0))
    return pl.pallas_call(
        body, name="ple",
        out_shape=(jax.ShapeDtypeStruct((s, D_MODEL), F32), jax.ShapeDtypeStruct((s, D_MODEL), BF16),
                   jax.ShapeDtypeStruct((s, D_MODEL), BF16), jax.ShapeDtypeStruct((PLE_DIM, s), BF16),
                   jax.ShapeDtypeStruct((SUBLANES, D_MODEL), F32)),
        grid=(s // tm,),
        in_specs=[row, _resident((D_MODEL, D_MODEL)), vec, pl.BlockSpec((tm, PLE_DIM), lambda i: (i, 0)),
                  _resident((PLE_DIM, D_MODEL)), vec, row, row],
        out_specs=(row, row, row, pl.BlockSpec((PLE_DIM, tm), lambda i: (0, i)),
                   pl.BlockSpec((SUBLANES, D_MODEL), lambda i: (0, 0))),
        compiler_params=_params("arbitrary"))(hn2, w_pg, b_pg, p, w_pp, g3, x1, target)


def _gate_bwd(dgp, w_pg, x1, dy, g2, tm):
    s = x1.shape[0]

    def body(d_ref, w_ref, x1_ref, dy_ref, g_ref, dx_ref, dxb_ref, acc_ref):
        dh = _dot_nt(d_ref[...], w_ref[...])
        xn, r = _rms(x1_ref[...])
        dx1 = dy_ref[...] + _rms_bwd(dh * g_ref[...], xn, r)
        dx_ref[...] = dx1
        dxb_ref[...] = dx1.astype(BF16)

        @pl.when(pl.program_id(0) == 0)
        def _():
            acc_ref[...] = jnp.zeros_like(acc_ref)

        acc_ref[0:1, :] += jnp.sum(dh * xn, axis=0, keepdims=True)

    row = pl.BlockSpec((tm, D_MODEL), lambda i: (i, 0))
    return pl.pallas_call(
        body, name="gate_bwd",
        out_shape=(jax.ShapeDtypeStruct((s, D_MODEL), F32), jax.ShapeDtypeStruct((s, D_MODEL), BF16),
                   jax.ShapeDtypeStruct((SUBLANES, D_MODEL), F32)),
        grid=(s // tm,),
        in_specs=[row, _resident((D_MODEL, D_MODEL)), row, row, pl.BlockSpec((1, D_MODEL), lambda i: (0, 0))],
        out_specs=(row, row, pl.BlockSpec((SUBLANES, D_MODEL), lambda i: (0, 0))),
        compiler_params=_params("arbitrary"))(dgp, w_pg, x1, dy, g2)


def _mm_nt(a, b, tm, name):
    m, k = a.shape
    n = b.shape[0]

    def body(a_ref, b_ref, o_ref):
        o_ref[...] = _dot_nt(a_ref[...], b_ref[...])

    return pl.pallas_call(
        body, name=name,
        out_shape=jax.ShapeDtypeStruct((m, n), F32),
        grid=(m // tm,),
        in_specs=[pl.BlockSpec((tm, k), lambda i: (i, 0)), _resident((n, k))],
        out_specs=pl.BlockSpec((tm, n), lambda i: (i, 0)),
        compiler_params=_params("parallel"))(a, b)


def _attn_bwd(qn, k2, v2, a, z, dmix, conv_wp, sinks):
    s = qn.shape[0]
    nb = s // BLK

    def body(sink_ref, q_ref, kc_ref, kp_ref, vc_ref, vp_ref, a_ref, z_ref, zp_ref, zn_ref, dm_ref, dmn_ref,
             cw_ref, dq_ref, dkc_ref, dkp_ref, dvc_ref, dvp_ref, dz_ref, acc_ref, ext_ref):
        n = pl.program_id(0)
        valid = _window_mask(n)
        lane = lax.broadcasted_iota(jnp.int32, (1, ATTN_W), 1)

        @pl.when(n == 0)
        def _():
            acc_ref[...] = jnp.zeros_like(acc_ref)

        dz_ref[:, 0:QKV_W] = jnp.zeros((BLK, QKV_W), BF16)
        dsink = jnp.zeros((1, ATTN_W), F32)
        for kvh in range(K2_W // LANES):
            cols = slice(LANES * kvh, LANES * (kvh + 1))
            kw = jnp.concatenate([kp_ref[:, cols], kc_ref[:, cols]], axis=0)
            vw = jnp.concatenate([vp_ref[:, cols], vc_ref[:, cols]], axis=0)
            blocks = [slice(LANES * r, LANES * (r + 1)) for r in (2 * kvh, 2 * kvh + 1)]
            das, avs = [], []
            for rc in blocks:
                g = z_ref[:, _cols(GATE_A0 + rc.start, LANES)]
                sg = _sig(g)
                dm = dm_ref[:, rc]
                av = a_ref[:, rc]
                das.append(dm * (g * sg))
                avs += [av, av]
                dz_ref[:, _cols(GATE_A0 + rc.start, LANES)] = (dm * av * _dsilu(g, sg)).astype(BF16)
            q4 = _stack_heads([q_ref[:, rc] for rc in blocks], jnp.zeros((BLK, LANES), BF16))
            sink, slot = _group_sinks(sink_ref, kvh)
            p, mx, den = _head_probs(q4, kw, valid, sink)
            do4 = _stack_heads(das, 0.0)
            delta = jnp.sum(do4 * jnp.concatenate(avs, axis=0), axis=-1, keepdims=True)
            dob = do4.astype(BF16)
            ds = p * (_dot_nt(dob, vw) - delta) * (HEAD ** -0.5)
            for rc, dq in zip(blocks, _unstack_heads(_dot(ds.astype(BF16), kw))):
                dq_ref[:, rc] = dq
            dk2 = _dot(ds.T.astype(BF16), q4)
            dv2 = _dot(p.T.astype(BF16), dob)
            dkp_ref[:, cols] = dk2[0:BLK]
            dkc_ref[:, cols] = dk2[BLK:2 * BLK]
            dvp_ref[:, cols] = dv2[0:BLK]
            dvc_ref[:, cols] = dv2[BLK:2 * BLK]
            dsk = jnp.exp(sink - mx) / den * delta
            for i in range(GROUP):
                dsink = dsink - jnp.where(lane == GROUP * kvh + i,
                                          jnp.sum(jnp.where(slot == i, dsk, 0.0), axis=0, keepdims=True), 0.0)
        acc_ref[0:1, :] += dsink

        u, um1, um2, cv = _conv_fwd(z_ref, zp_ref, cw_ref, ext_ref, n)
        cb = z_ref[:, _cols(CONV_B0)]
        gc = z_ref[:, _cols(GATE_C0)]
        sgc = _sig(gc)
        dmc = dm_ref[:, ATTN_W:D_MODEL]
        t = dmc * (gc * sgc)
        dcv = t * cb
        dz_ref[:, _cols(CONV_B0)] = (t * cv).astype(BF16)
        dz_ref[:, _cols(GATE_C0)] = (dmc * cb * cv * _dsilu(gc, sgc)).astype(BF16)
        gcn = zn_ref[:, _cols(GATE_C0)]
        dcvn = dmn_ref[:, ATTN_W:D_MODEL] * (gcn * _sig(gcn)) * zn_ref[:, _cols(CONV_B0)]
        ext_ref[0:BLK, :] = dcv
        ext_ref[BLK:BLK + SUBLANES, :] = jnp.where(n < nb - 1, dcvn, 0.0)
        du = (cw_ref[2:3, :] * dcv + cw_ref[1:2, :] * ext_ref[1:1 + BLK, :]
              + cw_ref[0:1, :] * ext_ref[2:2 + BLK, :])
        dz_ref[:, _cols(CONV_C0)] = (du * z_ref[:, _cols(CONV_H0)]).astype(BF16)
        dz_ref[:, _cols(CONV_H0)] = (du * z_ref[:, _cols(CONV_C0)]).astype(BF16)
        acc_ref[1:2, :] += jnp.sum(dcv * um2, axis=0, keepdims=True)
        acc_ref[2:3, :] += jnp.sum(dcv * um1, axis=0, keepdims=True)
        acc_ref[3:4, :] += jnp.sum(dcv * u, axis=0, keepdims=True)

    cur = lambda w: pl.BlockSpec((BLK, w), lambda n: (n, 0))
    prev = lambda w: pl.BlockSpec((BLK, w), lambda n: (jnp.maximum(n - 1, 0), 0))
    nxt = lambda w: pl.BlockSpec(
        (SUBLANES, w), lambda n: (jnp.minimum((n + 1) * (BLK // SUBLANES), nb * (BLK // SUBLANES) - 1), 0))
    f32 = lambda w: jax.ShapeDtypeStruct((s, w), F32)
    return pl.pallas_call(
        body, name="attn_bwd",
        out_shape=(f32(ATTN_W), f32(K2_W), f32(K2_W), f32(K2_W), f32(K2_W),
                   jax.ShapeDtypeStruct((s, IN_W), BF16), jax.ShapeDtypeStruct((SUBLANES, ATTN_W), F32)),
        grid=(nb,),
        in_specs=[pl.BlockSpec(memory_space=pltpu.SMEM),
                  cur(ATTN_W), cur(K2_W), prev(K2_W), cur(K2_W), prev(K2_W), cur(ATTN_W), cur(IN_W),
                  pl.BlockSpec((SUBLANES, IN_W), _prev_rows), nxt(IN_W), cur(D_MODEL), nxt(D_MODEL),
                  pl.BlockSpec((SUBLANES, ATTN_W), lambda n: (0, 0))],
        out_specs=(cur(ATTN_W), cur(K2_W), cur(K2_W), cur(K2_W), cur(K2_W), cur(IN_W),
                   pl.BlockSpec((SUBLANES, ATTN_W), lambda n: (0, 0))),
        scratch_shapes=[pltpu.VMEM((BLK + 2 * SUBLANES, ATTN_W), F32)],
        compiler_params=_params("arbitrary"))(sinks, qn, k2, k2, v2, v2, a, z, z, z, dmix, dmix, conv_wp)


def _qkv_bwd(z, dz, dq, dkc, dkp, dvc, dvp, ra, rbm, rbp, gq2, gk2):
    s = z.shape[0]
    nb = s // BLK

    def body(z_ref, dz_in, dq_ref, dkc_ref, dkp_ref, dvc_ref, dvp_ref, a_ref, bm_ref, bp_ref, gq_ref, gk_ref,
             dz_ref, acc_ref):
        n = pl.program_id(0)
        a, bm, bp = a_ref[...], bm_ref[...], bp_ref[...]
        lo = _low_half((BLK, LANES))
        last = n == nb - 1

        @pl.when(n == 0)
        def _():
            acc_ref[...] = jnp.zeros_like(acc_ref)

        def norm_bwd(x, dy, gain):
            rr = lax.rsqrt(_half_sums(x * x) * (1.0 / HEAD) + EPS)
            xh = x * rr
            dxg = _rope_t(dy, a, bm, bp)
            dxh = dxg * gain
            dx = rr * (dxh - xh * (_half_sums(dxh * xh) * (1.0 / HEAD)))
            return dx, jnp.sum(dxg * xh, axis=0, keepdims=True)

        def folded(cur_ref, prev_ref, m):
            parts = []
            for h in (2 * m, 2 * m + 1):
                v = cur_ref[:, LANES * h:LANES * (h + 1)] + jnp.where(
                    last, 0.0, prev_ref[:, LANES * h:LANES * (h + 1)])
                parts.append(v + pltpu.roll(v, HEAD, 1))
            return jnp.where(lo, parts[0], parts[1])

        gq_acc = jnp.zeros((1, LANES), F32)
        for r in range(ATTN_W // LANES):
            rc = slice(LANES * r, LANES * (r + 1))
            dx, gg = norm_bwd(z_ref[:, rc], dq_ref[:, rc], gq_ref[...])
            dz_ref[:, rc] = dx.astype(BF16)
            gq_acc = gq_acc + gg
        acc_ref[0:1, :] += gq_acc
        gk_acc = jnp.zeros((1, LANES), F32)
        for m in range(KV_W // LANES):
            kc = slice(ATTN_W + LANES * m, ATTN_W + LANES * (m + 1))
            dx, gg = norm_bwd(z_ref[:, kc], folded(dkc_ref, dkp_ref, m), gk_ref[...])
            dz_ref[:, kc] = dx.astype(BF16)
            gk_acc = gk_acc + gg
            vc = slice(ATTN_W + KV_W + LANES * m, ATTN_W + KV_W + LANES * (m + 1))
            dz_ref[:, vc] = folded(dvc_ref, dvp_ref, m).astype(BF16)
        acc_ref[1:2, :] += gk_acc

    cur = lambda w: pl.BlockSpec((BLK, w), lambda n: (n, 0))
    nxt = lambda w: pl.BlockSpec((BLK, w), lambda n: (jnp.minimum(n + 1, nb - 1), 0))
    one = pl.BlockSpec((1, LANES), lambda n: (0, 0))
    return pl.pallas_call(
        body, name="qkv_bwd",
        out_shape=(jax.ShapeDtypeStruct(dz.shape, dz.dtype), jax.ShapeDtypeStruct((SUBLANES, LANES), F32)),
        grid=(nb,),
        in_specs=[cur(PAIR_W), ANY, cur(ATTN_W), cur(K2_W), nxt(K2_W), cur(K2_W), nxt(K2_W),
                  cur(LANES), cur(LANES), cur(LANES), one, one],
        out_specs=(cur(QKV_W), pl.BlockSpec((SUBLANES, LANES), lambda n: (0, 0))),
        input_output_aliases={1: 0},
        compiler_params=_params("arbitrary"))(z, dz, dq, dkc, dkp, dvc, dvp, ra, rbm, rbp, gq2, gk2)


def _in_bwd(dz, w_pairs, x, dx1, g1, tm):
    s = x.shape[0]

    def body(d_ref, w_ref, x_hbm, dx1_hbm, g_ref, gx_ref, acc_ref, x_buf, dx1_buf, sems):
        i, k = pl.program_id(0), pl.program_id(1)
        rows = pl.ds(pl.multiple_of(i * tm, tm), tm)
        fetch = [pltpu.make_async_copy(x_hbm.at[rows], x_buf, sems.at[0]),
                 pltpu.make_async_copy(dx1_hbm.at[rows], dx1_buf, sems.at[1])]
        sub = min(SUB_ROWS, tm)
        blocks = [slice(r, r + sub) for r in range(0, tm, sub)]

        @pl.when(k == 0)
        def _():
            for cp in fetch:
                cp.start()
            gx_ref[...] = _dot_nt(d_ref[...], w_ref[0])

        @pl.when(k > 0)
        def _():
            gx_ref[...] += _dot_nt(d_ref[...], w_ref[0])

        @pl.when((i == 0) & (k == 0))
        def _():
            acc_ref[...] = jnp.zeros_like(acc_ref)

        @pl.when(k == N_PAIRS - 1)
        def _():
            for cp in fetch:
                cp.wait()
            for rb in blocks:
                dh = gx_ref[rb, :]
                xn, r = _rms(x_buf[rb, :])
                gx_ref[rb, :] = dx1_buf[rb, :] + _rms_bwd(dh * g_ref[...], xn, r)
                acc_ref[0:1, :] += jnp.sum(dh * xn, axis=0, keepdims=True)

    return pl.pallas_call(
        body, name="in_bwd",
        out_shape=(jax.ShapeDtypeStruct((s, D_MODEL), F32), jax.ShapeDtypeStruct((SUBLANES, D_MODEL), F32)),
        grid=(s // tm, N_PAIRS),
        in_specs=[pl.BlockSpec((tm, PAIR_W), lambda i, k: (i, k)),
                  pl.BlockSpec((1, D_MODEL, PAIR_W), lambda i, k: (k, 0, 0)),
                  ANY, ANY, pl.BlockSpec((1, D_MODEL), lambda i, k: (0, 0))],
        out_specs=(pl.BlockSpec((tm, D_MODEL), lambda i, k: (i, 0)),
                   pl.BlockSpec((SUBLANES, D_MODEL), lambda i, k: (0, 0))),
        scratch_shapes=[pltpu.VMEM((tm, D_MODEL), F32), pltpu.VMEM((tm, D_MODEL), F32),
                        pltpu.SemaphoreType.DMA((2,))],
        compiler_params=_params("arbitrary", "arbitrary"))(dz, w_pairs, x, dx1, g1)


def _mm_grad(at, bs, tn, name):
    m, kdim = at.shape
    nblk = [b.shape[1] // tn for b in bs]
    starts = [sum(nblk[:t]) for t in range(len(bs))]

    def body(a_ref, *refs):
        b_refs, o_ref = refs[:len(bs)], refs[len(bs)]
        j = pl.program_id(0)
        for t, b_ref in enumerate(b_refs):
            @pl.when((j >= starts[t]) & (j < starts[t] + nblk[t]))
            def _():
                o_ref[...] = _dot(a_ref[...], b_ref[...]).astype(BF16)

    def b_spec(t):
        return pl.BlockSpec((kdim, tn), lambda j: (0, jnp.clip(j - starts[t], 0, nblk[t] - 1)))

    return pl.pallas_call(
        body, name=name,
        out_shape=jax.ShapeDtypeStruct((m, sum(nblk) * tn), BF16),
        grid=(sum(nblk),),
        in_specs=[_resident((m, kdim))] + [b_spec(t) for t in range(len(bs))],
        out_specs=pl.BlockSpec((m, tn), lambda j: (0, j)),
        compiler_params=_params("parallel"))(at, *bs)


def _place():
    return lax.axis_index("x"), lax.axis_index("y"), lax.axis_index("c")


def _all_reduce_slab(slab, name):
    def body(in_ref, out_ref, gath_ref, send_sems, recv_sems):
        x, y, c = _place()
        me = 4 * x + 2 * y + c
        gath_ref[me] = in_ref[...]
        copies = []
        for k in range(1, N_DEV):
            peer = (x ^ (k >> 2), y ^ ((k >> 1) & 1), c ^ (k & 1))
            copies.append(pltpu.make_async_remote_copy(
                src_ref=in_ref, dst_ref=gath_ref.at[me], send_sem=send_sems.at[k - 1],
                recv_sem=recv_sems.at[k - 1], device_id=peer, device_id_type=MESH))
        for cp in copies:
            cp.start()
        for cp in copies:
            cp.wait_recv()
        for cp in copies:
            cp.wait_send()
        total = gath_ref[0]
        for d in range(1, N_DEV):
            total = total + gath_ref[d]
        out_ref[...] = total

    vmem = pl.BlockSpec(memory_space=pltpu.VMEM)
    return pl.pallas_call(
        body, name=name,
        out_shape=jax.ShapeDtypeStruct(slab.shape, F32),
        in_specs=[vmem], out_specs=vmem,
        scratch_shapes=[pltpu.VMEM((N_DEV,) + slab.shape, F32),
                        pltpu.SemaphoreType.DMA((N_DEV - 1,)), pltpu.SemaphoreType.DMA((N_DEV - 1,))])(slab)


def _pair_sum(g, r, place, tr, name):
    _, _, rows, cols = g.shape

    def body(place_ref, g_ref, r_ref, pb_ref, own_ref):
        tot = g_ref[0, 0].astype(F32) + r_ref[0].astype(F32)
        pb_ref[0] = tot.astype(BF16)

        @pl.when(pl.program_id(1) == place_ref[1])
        def _():
            own_ref[...] = tot

    grid_spec = pltpu.PrefetchScalarGridSpec(
        num_scalar_prefetch=1, grid=(rows // tr, 4),
        in_specs=[pl.BlockSpec((1, 1, tr, cols), lambda i, q, place_ref: (q, place_ref[0], i, 0)),
                  pl.BlockSpec((1, tr, cols), lambda i, q, place_ref: (q, i, 0))],
        out_specs=(pl.BlockSpec((1, tr, cols), lambda i, q, place_ref: (q, i, 0)),
                   pl.BlockSpec((tr, cols), lambda i, q, place_ref: (i, 0))))
    return pl.pallas_call(
        body, name=name, grid_spec=grid_spec,
        out_shape=(jax.ShapeDtypeStruct((4, rows, cols), BF16), jax.ShapeDtypeStruct((rows, cols), F32)),
        compiler_params=_params("arbitrary", "arbitrary"))(place, g, r)


HBM = pl.BlockSpec(memory_space=pltpu.HBM)
SEM = pl.BlockSpec(memory_space=pltpu.SEMAPHORE)
SIDE_EFFECT = pltpu.CompilerParams(has_side_effects=pltpu.SideEffectType.DATAFLOW_SIDE_EFFECTING)
TOKEN = jax.ShapeDtypeStruct((SUBLANES, LANES), F32)


def _hbm(a):
    return pltpu.with_memory_space_constraint(a, pltpu.HBM)


def _hbm_like(arrays):
    return tuple(pltpu.HBM(a.shape, a.dtype) for a in arrays)


def _block_of(px, py, pc):
    return 4 * px + 2 * py + pc


def _gather_start(shards, after):
    na = len(shards)
    lands = [_hbm(lax.empty((N_DEV,) + a.shape, a.dtype)) for a in shards]

    def body(*refs):
        ins, land = refs[:na], refs[na:2 * na]
        send_sems, recv_ici, recv_d2d = refs[2 * na + 1:2 * na + 4]
        token = refs[-1]
        x, y, c = _place()
        for k, peer in enumerate([(x, y, 1 - c), (1 - x, y, c), (x, 1 - y, c), (1 - x, 1 - y, c)]):
            for t in range(na):
                pltpu.make_async_remote_copy(
                    src_ref=ins[t], dst_ref=land[t].at[_block_of(x, y, c)], send_sem=send_sems.at[4 * t + k],
                    recv_sem=recv_d2d.at[4 * t] if k == 0 else recv_ici.at[3 * t + k - 1],
                    device_id=peer, device_id_type=MESH).start()
        token[...] = jnp.zeros_like(token)

    out = pl.pallas_call(
        body, name="gather_start",
        out_shape=(pltpu.SemaphoreType.DMA((4 * na,)), pltpu.SemaphoreType.DMA((3 * na,)),
                   pltpu.SemaphoreType.DMA((4 * na,)), *_hbm_like(lands), TOKEN),
        in_specs=[ANY] * na + [HBM] * na + [ANY],
        out_specs=(SEM, SEM, SEM, *[HBM] * na, pl.BlockSpec(memory_space=pltpu.VMEM)),
        input_output_aliases={na + i: 3 + i for i in range(na)},
        compiler_params=SIDE_EFFECT)(*shards, *lands, after)
    send_sems, recv_ici, recv_d2d = out[:3]
    state = dict(send=send_sems, ici=recv_ici, d2d=recv_d2d, shards=list(shards), lands=out[3:3 + na])
    return state, out[-1]


def _gather_forward(state, after):
    lands = state["lands"]
    na = len(lands)

    def body(*refs):
        land = refs[:na]
        recv_ici, recv_d2d = refs[na], refs[na + 1]
        fwd_sems, token = refs[-2], refs[-1]
        x, y, c = _place()
        for j, chip in enumerate([(1 - x, y), (x, 1 - y), (1 - x, 1 - y)]):
            for t in range(na):
                blk = land[t].at[_block_of(*chip, c)]
                pltpu.make_async_remote_copy(
                    src_ref=blk, dst_ref=blk, send_sem=fwd_sems.at[3 * t + j], recv_sem=recv_ici.at[3 * t + j],
                    device_id=(x, y, c), device_id_type=MESH).wait_recv()
                pltpu.make_async_remote_copy(
                    src_ref=blk, dst_ref=blk, send_sem=fwd_sems.at[3 * t + j], recv_sem=recv_d2d.at[4 * t + 1 + j],
                    device_id=(x, y, 1 - c), device_id_type=MESH).start()
        token[...] = jnp.zeros_like(token)

    out = pl.pallas_call(
        body, name="gather_forward",
        out_shape=(*_hbm_like(lands), pltpu.SemaphoreType.DMA((3 * na,)), TOKEN),
        in_specs=[HBM] * na + [SEM, SEM, ANY],
        out_specs=(*[HBM] * na, SEM, pl.BlockSpec(memory_space=pltpu.VMEM)),
        input_output_aliases={i: i for i in range(na)},
        compiler_params=SIDE_EFFECT)(*lands, state["ici"], state["d2d"], after)
    return dict(state, lands=out[:na], fwd=out[na]), out[-1]


def _gather_wait(state, after):
    shards, lands = state["shards"], state["lands"]
    na = len(lands)

    def body(*refs):
        ins, land = refs[:na], refs[na:2 * na]
        send_sems, fwd_sems, recv_d2d = refs[2 * na:2 * na + 3]
        x, y, c = _place()
        chips = [(1 - x, y), (x, 1 - y), (1 - x, 1 - y)]
        for t in range(na):
            mine = land[t].at[_block_of(x, y, c)]
            for k in range(4):
                pltpu.make_async_remote_copy(
                    src_ref=ins[t], dst_ref=mine, send_sem=send_sems.at[4 * t + k], recv_sem=recv_d2d.at[4 * t],
                    device_id=(x, y, c), device_id_type=MESH).wait_send()
            for j, chip in enumerate(chips):
                blk = land[t].at[_block_of(*chip, c)]
                pltpu.make_async_remote_copy(
                    src_ref=blk, dst_ref=blk, send_sem=fwd_sems.at[3 * t + j], recv_sem=recv_d2d.at[4 * t + 1 + j],
                    device_id=(x, y, c), device_id_type=MESH).wait_send()
            for k, blk_id in enumerate([_block_of(x, y, 1 - c)] + [_block_of(*chip, 1 - c) for chip in chips]):
                blk = land[t].at[blk_id]
                pltpu.make_async_remote_copy(
                    src_ref=blk, dst_ref=blk, send_sem=send_sems.at[4 * t], recv_sem=recv_d2d.at[4 * t + k],
                    device_id=(x, y, c), device_id_type=MESH).wait_recv()

    out = pl.pallas_call(
        body, name="gather_wait",
        out_shape=_hbm_like(lands),
        in_specs=[ANY] * na + [HBM] * na + [SEM, SEM, SEM, ANY],
        out_specs=tuple([HBM] * na),
        input_output_aliases={na + i: i for i in range(na)},
        compiler_params=SIDE_EFFECT)(*shards, *lands, state["send"], state["fwd"], state["d2d"], after)
    return out


def _gather_from_sibling(state, after):
    (land,) = state["lands"]

    def body(land_ref, recv_d2d, after_ref, out_ref):
        x, y, c = _place()
        blk = land_ref.at[_block_of(x, y, 1 - c)]
        pltpu.make_async_remote_copy(src_ref=blk, dst_ref=blk, send_sem=recv_d2d.at[0], recv_sem=recv_d2d.at[0],
                                     device_id=(x, y, c), device_id_type=MESH).wait_recv()

    out = pl.pallas_call(
        body, name="gather_from_sibling", out_shape=pltpu.HBM(land.shape, land.dtype),
        in_specs=[HBM, SEM, ANY], out_specs=HBM, input_output_aliases={0: 0},
        compiler_params=SIDE_EFFECT)(land, state["d2d"], after)
    return dict(state, lands=[out])


def _gather_from_chip(state, j, after, last):
    (shard,), (land,) = state["shards"], state["lands"]

    def chip_blocks(land_ref):
        x, y, c = _place()
        chip = [(1 - x, y), (x, 1 - y), (1 - x, 1 - y)][j]
        return (x, y, c), land_ref.at[_block_of(*chip, c)], land_ref.at[_block_of(*chip, 1 - c)]

    def forward(land_ref, recv_ici, recv_d2d, after_ref, out_ref, fwd_sem):
        (x, y, c), mine, _ = chip_blocks(land_ref)
        pltpu.make_async_remote_copy(src_ref=mine, dst_ref=mine, send_sem=fwd_sem.at[0], recv_sem=recv_ici.at[j],
                                     device_id=(x, y, c), device_id_type=MESH).wait_recv()
        pltpu.make_async_remote_copy(src_ref=mine, dst_ref=mine, send_sem=fwd_sem.at[0], recv_sem=recv_d2d.at[1 + j],
                                     device_id=(x, y, 1 - c), device_id_type=MESH).start()

    land, fwd_sem = pl.pallas_call(
        forward, name="gather_pass_chip_" + str(j),
        out_shape=(pltpu.HBM(land.shape, land.dtype), pltpu.SemaphoreType.DMA((1,))),
        in_specs=[HBM, SEM, SEM, ANY], out_specs=(HBM, SEM), input_output_aliases={0: 0},
        compiler_params=SIDE_EFFECT)(land, state["ici"], state["d2d"], after)

    def arrive(land_ref, fwd_sem, recv_d2d, shard_ref, send_sems, out_ref):
        (x, y, c), mine, theirs = chip_blocks(land_ref)
        pltpu.make_async_remote_copy(src_ref=theirs, dst_ref=theirs, send_sem=fwd_sem.at[0],
                                     recv_sem=recv_d2d.at[1 + j], device_id=(x, y, c),
                                     device_id_type=MESH).wait_recv()
        pltpu.make_async_remote_copy(src_ref=mine, dst_ref=mine, send_sem=fwd_sem.at[0], recv_sem=recv_d2d.at[1 + j],
                                     device_id=(x, y, c), device_id_type=MESH).wait_send()
        for k in range(4 if last else 0):
            pltpu.make_async_remote_copy(
                src_ref=shard_ref, dst_ref=land_ref.at[_block_of(x, y, c)], send_sem=send_sems.at[k],
                recv_sem=recv_d2d.at[0], device_id=(x, y, c), device_id_type=MESH).wait_send()

    land = pl.pallas_call(
        arrive, name="gather_take_chip_" + str(j), out_shape=pltpu.HBM(land.shape, land.dtype),
        in_specs=[HBM, SEM, SEM, ANY, SEM], out_specs=HBM, input_output_aliases={0: 0},
        compiler_params=SIDE_EFFECT)(land, fwd_sem, state["d2d"], shard, state["send"])
    return dict(state, lands=[land])


def _to_sibling(srcs, lands, send_sems, recv_sems):
    x, y, c = _place()
    return [pltpu.make_async_remote_copy(
        src_ref=srcs[t].at[:, 1 - c], dst_ref=lands[t], send_sem=send_sems.at[t], recv_sem=recv_sems.at[t],
        device_id=(x, y, 1 - c), device_id_type=MESH) for t in range(len(srcs))]


def _to_chips(srcs, lands, send_sems, recv_sems):
    x, y, c = _place()
    copies = []
    for k in (1, 2, 3):
        px, py = x ^ (k >> 1), y ^ (k & 1)
        copies += [pltpu.make_async_remote_copy(
            src_ref=srcs[t].at[2 * px + py], dst_ref=lands[t].at[k - 1], send_sem=send_sems.at[3 * t + k - 1],
            recv_sem=recv_sems.at[3 * t + k - 1], device_id=(px, py, c), device_id_type=MESH) for t in range(len(srcs))]
    return copies


def _exchange_start(name, srcs, land_shapes, copies, per_array, after):
    na = len(srcs)
    lands = [_hbm(lax.empty(shp, a.dtype)) for shp, a in zip(land_shapes, srcs)]

    def body(*refs):
        token = refs[-1]
        for cp in copies(refs[:na], refs[na:2 * na], refs[2 * na + 1], refs[2 * na + 2]):
            cp.start()
        token[...] = jnp.zeros_like(token)

    out = pl.pallas_call(
        body, name=name,
        out_shape=(pltpu.SemaphoreType.DMA((na * per_array,)), pltpu.SemaphoreType.DMA((na * per_array,)),
                   *_hbm_like(lands), TOKEN),
        in_specs=[ANY] * na + [HBM] * na + [ANY],
        out_specs=(SEM, SEM, *[HBM] * na, pl.BlockSpec(memory_space=pltpu.VMEM)),
        input_output_aliases={na + i: 2 + i for i in range(na)},
        compiler_params=SIDE_EFFECT)(*srcs, *lands, after)
    return dict(send=out[0], recv=out[1], srcs=list(srcs), lands=out[2:2 + na]), out[-1]


def _exchange_wait(name, state, copies, after):
    srcs, lands = state["srcs"], state["lands"]
    na = len(srcs)

    def body(*refs):
        for cp in copies(refs[:na], refs[na:2 * na], refs[2 * na], refs[2 * na + 1]):
            cp.wait_send()
            cp.wait_recv()

    out = pl.pallas_call(
        body, name=name,
        out_shape=_hbm_like(lands),
        in_specs=[ANY] * na + [HBM] * na + [SEM, SEM, ANY],
        out_specs=tuple([HBM] * na),
        input_output_aliases={na + i: i for i in range(na)},
        compiler_params=SIDE_EFFECT)(*srcs, *lands, state["send"], state["recv"], after)
    return out


def _adamw_math(w, g, m, v):
    m = ADAM_B1 * m + (1.0 - ADAM_B1) * g
    v = ADAM_B2 * v + (1.0 - ADAM_B2) * (g * g)
    m_hat = m / (1.0 - ADAM_B1 ** ADAM_STEP)
    v_hat = v / (1.0 - ADAM_B2 ** ADAM_STEP)
    return -ADAM_LR * (m_hat / (jnp.sqrt(v_hat) + ADAM_EPS) + ADAM_WD * w), m, v


def _adamw(own, others, w, m, v, tr, name):
    rows, cols = w.shape
    blk = pl.BlockSpec((tr, cols), lambda i: (i, 0))

    def body(own_ref, oth_ref, w_ref, m_ref, v_ref, g_ref, d_ref, nm_ref, nv_ref):
        g = own_ref[...]
        for k in range(3):
            g = g + oth_ref[k].astype(F32)
        g_ref[...] = g
        d_ref[...], nm_ref[...], nv_ref[...] = _adamw_math(w_ref[...], g, m_ref[...], v_ref[...])

    out = jax.ShapeDtypeStruct((rows, cols), F32)
    return pl.pallas_call(
        body, name=name, out_shape=(out, out, out, out), grid=(rows // tr,),
        in_specs=[blk, pl.BlockSpec((3, tr, cols), lambda i: (0, i, 0)), blk, blk, blk],
        out_specs=(blk, blk, blk, blk),
        compiler_params=_params("parallel"))(own, others, w, m, v)


def _adamw_slab(w, g, m, v):
    def body(w_ref, g_ref, m_ref, v_ref, d_ref, nm_ref, nv_ref):
        d_ref[...], nm_ref[...], nv_ref[...] = _adamw_math(w_ref[...], g_ref[...], m_ref[...], v_ref[...])

    out = jax.ShapeDtypeStruct(w.shape, F32)
    vmem = pl.BlockSpec(memory_space=pltpu.VMEM)
    return pl.pallas_call(body, name="adamw_small", out_shape=(out, out, out),
                          in_specs=[vmem] * 4, out_specs=(vmem, vmem, vmem))(w, g, m, v)


def _row(v, width=D_MODEL):
    v = v.reshape(1, -1)
    return jnp.pad(v, ((0, 0), (0, width - v.shape[1])))


def _tables(s, gq, gk, conv_w):
    gq2 = jnp.tile(gq.reshape(1, HEAD), (1, 2))
    gk2 = jnp.tile(gk.reshape(1, HEAD), (1, 2))
    conv_wp = jnp.pad(conv_w, ((0, SUBLANES - conv_w.shape[0]), (0, 0)))
    return _rope_tables(s), gq2, gk2, conv_wp


def _pair_id(q):
    return jnp.full((1,), q, jnp.int32)


def _forward_in(x, g1, shards):
    s = x.shape[0]
    h, ht = _prenorm(x, g1, min(512, s))
    z, w_pairs = lax.empty((s, IN_W), F32), lax.empty((N_PAIRS, D_MODEL, PAIR_W), BF16)
    for q in range(N_PAIRS):
        z, w_pairs = _fwd_in_pair(h, shards, z, w_pairs, _pair_id(q), min(512, s), "fwd_in_" + str(q))
    return ht, z, w_pairs


def _forward_attn(z, rope, gq2, gk2, conv_wp, sinks):
    s = z.shape[0]
    qn, k2, v2 = _qk_prep(z, *rope, gq2, gk2, min(256, s))
    a, mix, mixt = _attn_fwd(qn, k2, v2, z, conv_wp, sinks)
    return qn, k2, v2, a, mix, mixt


def _forward_out(x, p, target, mix, mixt, w_out, g2, w_pg, b_pg, w_pp, g3):
    s = x.shape[0]
    tm = min(512, s)
    x1, hn2, hn2t = _fwd_out(mix, w_out, x, g2, tm)
    dy, dgp, dt, pt, acc_ple = _ple(hn2, w_pg, b_pg, p, w_pp, g3, x1, target, min(256, s))
    dx1, dx1b, acc_g2 = _gate_bwd(dgp, w_pg, x1, dy, g2, tm)
    gw_out = _mm_grad(mixt, [dx1b], 512, "grad_w_out")
    gw_pg = _mm_grad(hn2t, [dgp], 512, "grad_w_ple_gate")
    gw_pp = _mm_grad(pt, [dt], 512, "grad_w_ple_proj")
    return dx1, dx1b, (gw_out, gw_pg, gw_pp), acc_ple, acc_g2


def _backward_attn(dmix, ht, z, qn, k2, v2, a, rope, gq2, gk2, conv_wp, sinks):
    dq, dkc, dkp, dvc, dvp, dz, acc_attn = _attn_bwd(qn, k2, v2, a, z, dmix, conv_wp, sinks)
    dz, acc_qk = _qkv_bwd(z, dz, dq, dkc, dkp, dvc, dvp, *rope, gq2, gk2)
    gw_in = _mm_grad(ht, [dz], 512, "grad_w_in")
    return dz, gw_in, acc_attn, acc_qk


def _small_rows(acc_g1, acc_g2, acc_ple, acc_qk, acc_attn):
    fold = lambda v: _row((v[:HEAD] + v[HEAD:]))
    return [acc_g1[0:1], acc_g2[0:1], acc_ple[0:1], acc_ple[1:2], fold(acc_qk[0]), fold(acc_qk[1]),
            _row(acc_attn[0, :N_Q_HEADS]), _row(acc_attn[1]), _row(acc_attn[2]), _row(acc_attn[3]), acc_ple[2:3]]


def _local_step(x, p, target, g1, shards, gq, gk, sinks, conv_w, w_out, g2, w_pg, b_pg, w_pp, g3):
    rope, gq2, gk2, conv_wp = _tables(x.shape[0], gq, gk, conv_w)
    ht, z, w_pairs = _forward_in(x, g1, shards)
    qn, k2, v2, a, mix, mixt = _forward_attn(z, rope, gq2, gk2, conv_wp, sinks)
    dx1, dx1b, (gw_out, gw_pg, gw_pp), acc_ple, acc_g2 = _forward_out(
        x, p, target, mix, mixt, w_out, g2, w_pg, b_pg, w_pp, g3)
    dmix = _mm_nt(dx1b, w_out, min(512, x.shape[0]), "out_bwd")
    dz, gw_in, acc_attn, acc_qk = _backward_attn(dmix, ht, z, qn, k2, v2, a, rope, gq2, gk2, conv_wp, sinks)
    grad_x, acc_g1 = _in_bwd(dz, w_pairs, x, dx1, g1, min(512, x.shape[0]))
    return grad_x, (gw_in, gw_out, gw_pg, gw_pp), _small_rows(acc_g1, acc_g2, acc_ple, acc_qk, acc_attn)


ROW_CONV, ROW_LOSS = 7, 10


def _slab(rows):
    rows = list(rows)
    return jnp.concatenate(rows + [jnp.zeros((SLAB_ROWS - len(rows), D_MODEL), F32)], axis=0)


def _by_owner(g):
    return g.reshape((4, 2) + g.shape[1:])


def kernel(x, p, norm_gain, w_in, q_norm_gain, k_norm_gain, attn_sinks, conv_w, w_out, ple_gate_norm_gain, w_ple_gate, b_ple_gate, w_ple_proj, ple_norm_gain, loss_target, m_norm_gain, m_w_in, m_q_norm_gain, m_k_norm_gain, m_attn_sinks, m_conv_w, m_w_out, m_ple_gate_norm_gain, m_w_ple_gate, m_b_ple_gate, m_w_ple_proj, m_ple_norm_gain, v_norm_gain, v_w_in, v_q_norm_gain, v_k_norm_gain, v_attn_sinks, v_conv_w, v_w_out, v_ple_gate_norm_gain, v_w_ple_gate, v_b_ple_gate, v_w_ple_proj, v_ple_norm_gain):
    me = 4 * lax.axis_index("x") + 2 * lax.axis_index("y") + lax.axis_index("c")
    place = jnp.stack([lax.axis_index("c"), 2 * lax.axis_index("x") + lax.axis_index("y")]).astype(jnp.int32)
    conv_cols = conv_w.shape[2]
    xs, ps, target = x[0], p[0, 0], loss_target[0]
    zero = lambda token: token[0:1, 0:1]

    shard_in = w_in[0].astype(BF16)
    own_late = [w_out[0].astype(BF16), w_ple_gate[0].astype(BF16), w_ple_proj[0].astype(BF16)]
    with_own = lambda gathered, own: lax.dynamic_update_slice(gathered, own[None], (me, 0, 0))
    early, started = _gather_start([shard_in], shard_in)
    late, started_late = _gather_start(own_late, started)
    tm = min(512, xs.shape[0])
    h, ht = _prenorm(xs, norm_gain + zero(started) + zero(started_late), tm)
    conv_rows = [lax.dynamic_update_slice(jnp.zeros((1, D_MODEL), F32), conv_w[0, t:t + 1], (0, conv_cols * me))
                 for t in range(3)]
    conv_full = _all_reduce_slab(_slab(conv_rows), "gather_conv_w")[0:3, :ATTN_W]
    rope, gq2, gk2, conv_wp = _tables(xs.shape[0], q_norm_gain[0], k_norm_gain[0], conv_full)

    z, w_pairs = lax.empty((xs.shape[0], IN_W), F32), lax.empty((N_PAIRS, D_MODEL, PAIR_W), BF16)
    early = _gather_from_sibling(early, h)
    early = dict(early, lands=[with_own(early["lands"][0], shard_in)])
    z, w_pairs = _fwd_in_pair(h, early["lands"][0], z, w_pairs, place[1:2], tm, "fwd_in_own")
    for j, flip in enumerate((2, 1, 3)):
        early = _gather_from_chip(early, j, z, last=j == 2)
        z, w_pairs = _fwd_in_pair(h, early["lands"][0], z, w_pairs, place[1:2] ^ flip, tm, "fwd_in_chip_" + str(j))
    late, forwarded = _gather_forward(late, z)
    qn, k2, v2, a, mix, mixt = _forward_attn(z, rope, gq2 + zero(forwarded), gk2, conv_wp, attn_sinks)
    g_out, g_pg, g_pp = (with_own(g, own) for g, own in zip(_gather_wait(late, mix), own_late))
    w_out_f = g_out.reshape(D_MODEL, D_MODEL)
    w_pg_f = g_pg.reshape(D_MODEL, D_MODEL)
    w_pp_f = jnp.transpose(g_pp, (1, 0, 2)).reshape(PLE_DIM, D_MODEL)

    dx1, dx1b, (gw_out, gw_pg, gw_pp), acc_ple, acc_g2 = _forward_out(
        xs, ps, target, mix, mixt, w_out_f, ple_gate_norm_gain, w_pg_f, b_ple_gate, w_pp_f, ple_norm_gain)

    names = ("w_out", "w_ple_gate", "w_ple_proj")
    gw_pp_t = jnp.transpose(gw_pp.reshape(PLE_DIM, N_DEV, PLE_DIM), (1, 0, 2))
    grads = [_by_owner(gw_out.reshape(N_DEV, D_MODEL // N_DEV, D_MODEL)),
             _by_owner(gw_pg.reshape(N_DEV, D_MODEL // N_DEV, D_MODEL)), _by_owner(gw_pp_t)]
    pairs, _ = _exchange_start("pair_start", grads, [(4,) + g.shape[2:] for g in grads], _to_sibling, 1, dx1b)
    dmix = _mm_nt(dx1b, w_out_f, min(512, xs.shape[0]), "out_bwd")
    from_sibling = _exchange_wait("pair_wait", pairs, _to_sibling, dmix)
    sums = [_pair_sum(g, r, place, 256, "pair_sum_" + nm) for g, r, nm in zip(pairs["srcs"], from_sibling, names)]
    chips, sent = _exchange_start("chip_start", [pb for pb, _ in sums], [(3,) + pb.shape[1:] for pb, _ in sums],
                                  _to_chips, 3, sums[-1][1])

    dz, gw_in, acc_attn, acc_qk = _backward_attn(
        dmix, ht, z, qn, k2, v2, a, rope, gq2, gk2, conv_wp, attn_sinks + zero(sent))

    gw_in_t = [_by_owner(jnp.transpose(gw_in.reshape(D_MODEL, N_DEV, SHARD_IN), (1, 0, 2)))]
    pairs_in, _ = _exchange_start("pair_start_w_in", gw_in_t, [(4,) + gw_in_t[0].shape[2:]], _to_sibling, 1, gw_in)
    from_chips = _exchange_wait("chip_wait", chips, _to_chips, gw_in)
    big = {}
    for (_, own), oth, w, m, v, nm in zip(sums, from_chips, (w_out, w_ple_gate, w_ple_proj),
                                          (m_w_out, m_w_ple_gate, m_w_ple_proj),
                                          (v_w_out, v_w_ple_gate, v_w_ple_proj), names):
        big[nm] = [t[None] for t in _adamw(own, oth, w[0], m[0], v[0], 256, "adamw_" + nm)]

    (from_sibling_in,) = _exchange_wait("pair_wait_w_in", pairs_in, _to_sibling, big[names[-1]][0])
    pb_in, own_in = _pair_sum(pairs_in["srcs"][0], from_sibling_in, place, 256, "pair_sum_w_in")
    chips_in, sent_in = _exchange_start("chip_start_w_in", [pb_in], [(3,) + pb_in.shape[1:]], _to_chips, 3, own_in)
    grad_x, acc_g1 = _in_bwd(dz, w_pairs, xs, dx1, norm_gain + zero(sent_in), tm)
    (from_chips_in,) = _exchange_wait("chip_wait_w_in", chips_in, _to_chips, grad_x)
    big["w_in"] = [t[None] for t in _adamw(own_in, from_chips_in, w_in[0], m_w_in[0], v_w_in[0], 256, "adamw_w_in")]

    red = _all_reduce_slab(_slab(_small_rows(acc_g1, acc_g2, acc_ple, acc_qk, acc_attn)), "reduce_small")
    loss = jnp.sum(red[ROW_LOSS])
    g_conv = [lax.dynamic_slice(red[ROW_CONV + t:ROW_CONV + t + 1], (0, conv_cols * me), (1, conv_cols))
              for t in range(3)]
    small = [norm_gain, ple_gate_norm_gain, b_ple_gate, ple_norm_gain, q_norm_gain, k_norm_gain, attn_sinks]
    small_m = [m_norm_gain, m_ple_gate_norm_gain, m_b_ple_gate, m_ple_norm_gain, m_q_norm_gain, m_k_norm_gain,
               m_attn_sinks]
    small_v = [v_norm_gain, v_ple_gate_norm_gain, v_b_ple_gate, v_ple_norm_gain, v_q_norm_gain, v_k_norm_gain,
               v_attn_sinks]
    pack = lambda vs, cw: _slab([_row(t) for t in vs] + [_row(cw[0, t]) for t in range(3)])
    g_slab = _slab([red[t:t + 1] for t in range(ROW_CONV)] + [_row(t) for t in g_conv])
    d_slab, m_slab, v_slab = _adamw_slab(pack(small, conv_w), g_slab, pack(small_m, m_conv_w), pack(small_v, v_conv_w))

    def unpack(slab_):
        outs = [slab_[t:t + 1, :w.shape[1]] for t, w in enumerate(small)]
        return outs, slab_[ROW_CONV:ROW_CONV + 3, :conv_cols][None]

    (g_s, g_cv), (d_s, d_cv), (m_s, m_cv), (v_s, v_cv) = (unpack(t) for t in (g_slab, d_slab, m_slab, v_slab))

    def order(sm, cv, k):
        return [sm[0], big["w_in"][k], sm[4], sm[5], sm[6], cv, big["w_out"][k], sm[1], big["w_ple_gate"][k], sm[2],
                big["w_ple_proj"][k], sm[3]]

    return (loss, grad_x[None], *order(g_s, g_cv, 0), *order(d_s, d_cv, 1), *order(m_s, m_cv, 2),
            *order(v_s, v_cv, 3))
```

```python
import jax
import jax.numpy as jnp
from jax import lax
from jax.experimental import pallas as pl
from jax.experimental.pallas import tpu as pltpu

F32, BF16 = jnp.float32, jnp.bfloat16

D_MODEL = 2048
PLE_DIM = 256
ATTN_W = 1024
HEAD = 64
N_Q_HEADS = 16
KV_W = 256
QKV_W = ATTN_W + 2 * KV_W
REST_W = 5 * 1024
IN_W = QKV_W + REST_W
GATE_A0, CONV_B0, CONV_C0, CONV_H0, GATE_C0 = (QKV_W + 1024 * t for t in range(5))
K2_W = 4 * 128
ROT = 16
ROPE_THETA = 500000.0
EPS = 1e-6
NEG_INF = -1e30
BLK = 128
LANES = 128
SUBLANES = 8
N_DEV = 8
SHARD_IN = IN_W // N_DEV
PAIR_W = 2 * SHARD_IN
N_PAIRS = IN_W // PAIR_W
SLAB_ROWS = 16
SUB_ROWS = 128
V7X_VMEM_LIMIT = 52 * 1024 * 1024

ADAM_LR, ADAM_B1, ADAM_B2, ADAM_EPS, ADAM_WD, ADAM_STEP = 0.001, 0.9, 0.999, 1e-08, 0.01, 10
MESH = pl.DeviceIdType.MESH


def _params(*semantics):
    return pltpu.CompilerParams(dimension_semantics=semantics, vmem_limit_bytes=V7X_VMEM_LIMIT)


ANY = pl.BlockSpec(memory_space=pl.ANY)


def _resident(shape):
    return pl.BlockSpec(shape, lambda *_: (0,) * len(shape), pipeline_mode=pl.Buffered(1))


def _dot(a, b):
    return jnp.dot(a, b, preferred_element_type=F32)


def _dot_nt(a, b):
    return lax.dot_general(a, b, (((1,), (1,)), ((), ())), preferred_element_type=F32)


def _rms(xf):
    r = lax.rsqrt(jnp.mean(xf * xf, axis=-1, keepdims=True) + EPS)
    return xf * r, r


def _rms_bwd(dxn, xn, r):
    return r * (dxn - xn * jnp.mean(dxn * xn, axis=-1, keepdims=True))


def _sig(g):
    return jax.nn.sigmoid(g)


def _dsilu(g, sg):
    return sg * (1.0 + g * (1.0 - sg))


def _low_half(shape):
    return lax.broadcasted_iota(jnp.int32, shape, len(shape) - 1) < HEAD


def _half_sums(v):
    lo = _low_half(v.shape)
    s_lo = jnp.sum(jnp.where(lo, v, 0.0), axis=-1, keepdims=True)
    s_hi = jnp.sum(jnp.where(lo, 0.0, v), axis=-1, keepdims=True)
    return jnp.where(lo, s_lo, s_hi)


def _rope(v, a, bm, bp):
    return v * a + pltpu.roll(v, LANES - ROT // 2, 1) * bm + pltpu.roll(v, ROT // 2, 1) * bp


def _rope_t(dy, a, bm, bp):
    return dy * a + pltpu.roll(dy * bm, ROT // 2, 1) + pltpu.roll(dy * bp, LANES - ROT // 2, 1)


def _dup_halves(v):
    lo = _low_half(v.shape)
    a = jnp.where(lo, v, 0.0)
    b = jnp.where(lo, 0.0, v)
    return a + pltpu.roll(a, HEAD, 1), b + pltpu.roll(b, HEAD, 1)


def _rope_tables(s):
    half = ROT // 2
    lane = lax.broadcasted_iota(jnp.int32, (s, LANES), 1) % HEAD
    pos = lax.broadcasted_iota(jnp.int32, (s, LANES), 0).astype(F32)
    inv_freq = jnp.power(jnp.float32(ROPE_THETA), -(lane % half).astype(F32) * 2.0 / ROT)
    ang = pos * inv_freq
    cos, sin = jnp.cos(ang), jnp.sin(ang)
    a = jnp.where(lane < ROT, cos, 1.0)
    bm = jnp.where(lane < half, -sin, 0.0)
    bp = jnp.where((lane >= half) & (lane < ROT), sin, 0.0)
    return a, bm, bp


def _prenorm(x, g1, tm):
    s = x.shape[0]

    def body(x_ref, g_ref, h_ref):
        xn, _ = _rms(x_ref[...])
        h_ref[...] = (xn * g_ref[...]).astype(BF16)

    return pl.pallas_call(
        body, name="prenorm",
        out_shape=jax.ShapeDtypeStruct((s, D_MODEL), BF16),
        grid=(s // tm,),
        in_specs=[pl.BlockSpec((tm, D_MODEL), lambda i: (i, 0)), pl.BlockSpec((1, D_MODEL), lambda i: (0, 0))],
        out_specs=pl.BlockSpec((tm, D_MODEL), lambda i: (i, 0)),
        compiler_params=_params("parallel"))(x, g1)


def _fwd_in_pair(h, shards, z, w_pairs, pair, tm, name):
    s = h.shape[0]

    def body(pair_ref, h_ref, lo_ref, hi_ref, z_in, wp_in, z_ref, wp_ref):
        @pl.when(pl.program_id(0) == 0)
        def _():
            wp_ref[0, 0:SHARD_IN, :] = lo_ref[0]
            wp_ref[0, SHARD_IN:PAIR_W, :] = hi_ref[0]

        z_ref[...] = _dot_nt(h_ref[...], wp_ref[0])

    grid_spec = pltpu.PrefetchScalarGridSpec(
        num_scalar_prefetch=1, grid=(s // tm,),
        in_specs=[pl.BlockSpec((tm, D_MODEL), lambda i, p: (i, 0)),
                  pl.BlockSpec((1, SHARD_IN, D_MODEL), lambda i, p: (2 * p[0], 0, 0)),
                  pl.BlockSpec((1, SHARD_IN, D_MODEL), lambda i, p: (2 * p[0] + 1, 0, 0)), ANY, ANY],
        out_specs=(pl.BlockSpec((tm, PAIR_W), lambda i, p: (i, p[0])),
                   pl.BlockSpec((1, PAIR_W, D_MODEL), lambda i, p: (p[0], 0, 0))))
    return pl.pallas_call(
        body, name=name, grid_spec=grid_spec,
        out_shape=(jax.ShapeDtypeStruct(z.shape, z.dtype), jax.ShapeDtypeStruct(w_pairs.shape, w_pairs.dtype)),
        input_output_aliases={4: 0, 5: 1},
        compiler_params=_params("arbitrary"))(pair, h, shards, shards, z, w_pairs)


def _qk_prep(z, ra, rbm, rbp, gq2, gk2, tm):
    s = z.shape[0]

    def body(z_ref, a_ref, bm_ref, bp_ref, gq_ref, gk_ref, q_ref, k2_ref, v2_ref):
        a, bm, bp = a_ref[...], bm_ref[...], bp_ref[...]
        for r in range(ATTN_W // LANES):
            x = z_ref[:, LANES * r:LANES * (r + 1)]
            rr = lax.rsqrt(_half_sums(x * x) * (1.0 / HEAD) + EPS)
            q_ref[:, LANES * r:LANES * (r + 1)] = _rope(x * rr * gq_ref[...], a, bm, bp).astype(BF16)
        for m in range(KV_W // LANES):
            x = z_ref[:, ATTN_W + LANES * m:ATTN_W + LANES * (m + 1)]
            rr = lax.rsqrt(_half_sums(x * x) * (1.0 / HEAD) + EPS)
            k_lo, k_hi = _dup_halves(_rope(x * rr * gk_ref[...], a, bm, bp))
            k2_ref[:, 2 * LANES * m:2 * LANES * m + LANES] = k_lo.astype(BF16)
            k2_ref[:, 2 * LANES * m + LANES:2 * LANES * (m + 1)] = k_hi.astype(BF16)
            v_lo, v_hi = _dup_halves(z_ref[:, ATTN_W + KV_W + LANES * m:ATTN_W + KV_W + LANES * (m + 1)])
            v2_ref[:, 2 * LANES * m:2 * LANES * m + LANES] = v_lo.astype(BF16)
            v2_ref[:, 2 * LANES * m + LANES:2 * LANES * (m + 1)] = v_hi.astype(BF16)

    row = lambda w: pl.BlockSpec((tm, w), lambda i: (i, 0))
    one = pl.BlockSpec((1, LANES), lambda i: (0, 0))
    return pl.pallas_call(
        body, name="qk_prep",
        out_shape=(jax.ShapeDtypeStruct((s, ATTN_W), BF16), jax.ShapeDtypeStruct((s, K2_W), BF16),
                   jax.ShapeDtypeStruct((s, K2_W), BF16)),
        grid=(s // tm,),
        in_specs=[row(PAIR_W), row(LANES), row(LANES), row(LANES), one, one],
        out_specs=(row(ATTN_W), row(K2_W), row(K2_W)),
        compiler_params=_params("parallel"))(z, ra, rbm, rbp, gq2, gk2)


GROUP = 4


def _window_mask(n):
    row = lax.broadcasted_iota(jnp.int32, (GROUP * BLK, 2 * BLK), 0) % BLK
    col = lax.broadcasted_iota(jnp.int32, (GROUP * BLK, 2 * BLK), 1)
    return (col > row) & (col <= row + BLK) & ((col >= BLK) | (n > 0))


def _stack_heads(pairs, zero):
    lo = _low_half(pairs[0].shape)
    parts = []
    for v in pairs:
        parts += [jnp.where(lo, v, zero), jnp.where(lo, zero, v)]
    return jnp.concatenate(parts, axis=0)


def _unstack_heads(v4):
    lo = _low_half((BLK, LANES))
    return [jnp.where(lo, v4[2 * i * BLK:(2 * i + 1) * BLK], v4[(2 * i + 1) * BLK:(2 * i + 2) * BLK]) for i in range(2)]


def _group_sinks(sink_ref, kvh):
    slot = lax.broadcasted_iota(jnp.int32, (GROUP * BLK, 1), 0) // BLK
    col = jnp.zeros((GROUP * BLK, 1), F32)
    for i in range(GROUP):
        col = jnp.where(slot == i, sink_ref[0, GROUP * kvh + i], col)
    return col, slot


def _head_probs(qm, kw, valid, sink):
    sc = jnp.where(valid, _dot_nt(qm, kw) * (HEAD ** -0.5), NEG_INF)
    mx = jnp.maximum(jnp.max(sc, axis=-1, keepdims=True), sink)
    ex = jnp.exp(sc - mx)
    den = jnp.sum(ex, axis=-1, keepdims=True) + jnp.exp(sink - mx)
    return ex / den, mx, den


def _cols(start, width=ATTN_W):
    return slice(start, start + width)


def _conv_fwd(z_ref, zp_ref, cw_ref, ext_ref, n):
    u = z_ref[:, _cols(CONV_C0)] * z_ref[:, _cols(CONV_H0)]
    pu = zp_ref[:, _cols(CONV_C0)] * zp_ref[:, _cols(CONV_H0)]
    ext_ref[0:SUBLANES, :] = jnp.where(n > 0, pu, 0.0)
    ext_ref[SUBLANES:SUBLANES + BLK, :] = u
    um1 = ext_ref[SUBLANES - 1:SUBLANES - 1 + BLK, :]
    um2 = ext_ref[SUBLANES - 2:SUBLANES - 2 + BLK, :]
    cv = cw_ref[0:1, :] * um2 + cw_ref[1:2, :] * um1 + cw_ref[2:3, :] * u
    return u, um1, um2, cv


def _prev_rows(n):
    return (jnp.maximum(n * (BLK // SUBLANES) - 1, 0), 0)


def _attn_fwd(qn, k2, v2, z, conv_wp, sinks):
    s = qn.shape[0]
    nb = s // BLK

    def body(sink_ref, q_ref, kc_ref, kp_ref, vc_ref, vp_ref, z_ref, zp_ref, cw_ref, a_ref, mix_ref, mixt_ref,
             ext_ref):
        n = pl.program_id(0)
        valid = _window_mask(n)
        for kvh in range(K2_W // LANES):
            cols = slice(LANES * kvh, LANES * (kvh + 1))
            kw = jnp.concatenate([kp_ref[:, cols], kc_ref[:, cols]], axis=0)
            vw = jnp.concatenate([vp_ref[:, cols], vc_ref[:, cols]], axis=0)
            blocks = [slice(LANES * r, LANES * (r + 1)) for r in (2 * kvh, 2 * kvh + 1)]
            q4 = _stack_heads([q_ref[:, rc] for rc in blocks], jnp.zeros((BLK, LANES), BF16))
            p, _, _ = _head_probs(q4, kw, valid, _group_sinks(sink_ref, kvh)[0])
            for rc, a in zip(blocks, _unstack_heads(_dot(p.astype(BF16), vw))):
                a_ref[:, rc] = a
                g = z_ref[:, _cols(GATE_A0 + rc.start, LANES)]
                mix_ref[:, rc] = (a * (g * _sig(g))).astype(BF16)
        _, _, _, cv = _conv_fwd(z_ref, zp_ref, cw_ref, ext_ref, n)
        gc = z_ref[:, _cols(GATE_C0)]
        mix_ref[:, ATTN_W:D_MODEL] = (z_ref[:, _cols(CONV_B0)] * cv * (gc * _sig(gc))).astype(BF16)
        mixt_ref[...] = mix_ref[...].T

    cur = lambda w: pl.BlockSpec((BLK, w), lambda n: (n, 0))
    prev = lambda w: pl.BlockSpec((BLK, w), lambda n: (jnp.maximum(n - 1, 0), 0))
    return pl.pallas_call(
        body, name="attn_fwd",
        out_shape=(jax.ShapeDtypeStruct((s, ATTN_W), F32), jax.ShapeDtypeStruct((s, D_MODEL), BF16),
                   jax.ShapeDtypeStruct((D_MODEL, s), BF16)),
        grid=(nb,),
        in_specs=[pl.BlockSpec(memory_space=pltpu.SMEM),
                  cur(ATTN_W), cur(K2_W), prev(K2_W), cur(K2_W), prev(K2_W), cur(IN_W),
                  pl.BlockSpec((SUBLANES, IN_W), _prev_rows),
                  pl.BlockSpec((SUBLANES, ATTN_W), lambda n: (0, 0))],
        out_specs=(cur(ATTN_W), cur(D_MODEL), pl.BlockSpec((D_MODEL, BLK), lambda n: (0, n))),
        scratch_shapes=[pltpu.VMEM((BLK + 2 * SUBLANES, ATTN_W), F32)],
        compiler_params=_params("parallel"))(sinks, qn, k2, k2, v2, v2, z, z, conv_wp)


def _fwd_out(mix, w_out, x, g2, tm):
    s = x.shape[0]

    def body(m_ref, w_ref, x_ref, g_ref, x1_ref, h_ref, ht_ref):
        x1 = x_ref[...] + _dot(m_ref[...], w_ref[...])
        x1_ref[...] = x1
        xn, _ = _rms(x1)
        h = (xn * g_ref[...]).astype(BF16)
        h_ref[...] = h
        ht_ref[...] = h.T

    row = pl.BlockSpec((tm, D_MODEL), lambda i: (i, 0))
    return pl.pallas_call(
        body, name="fwd_out",
        out_shape=(jax.ShapeDtypeStruct((s, D_MODEL), F32), jax.ShapeDtypeStruct((s, D_MODEL), BF16),
                   jax.ShapeDtypeStruct((D_MODEL, s), BF16)),
        grid=(s // tm,),
        in_specs=[row, _resident((D_MODEL, D_MODEL)), row, pl.BlockSpec((1, D_MODEL), lambda i: (0, 0))],
        out_specs=(row, row, pl.BlockSpec((D_MODEL, tm), lambda i: (0, i))),
        compiler_params=_params("parallel"))(mix, w_out, x, g2)


def _ple(hn2, w_pg, b_pg, p, w_pp, g3, x1, target, tm):
    s = x1.shape[0]

    def body(h_ref, wg_ref, b_ref, p_ref, wp_ref, g3_ref, x1_ref, t_ref, dy_ref, dgp_ref, dt_ref, pt_ref, acc_ref):
        gate = _sig(_dot(h_ref[...], wg_ref[...]) + b_ref[...])
        pb = p_ref[...].astype(BF16)
        pt_ref[...] = pb.T
        t = _dot(pb, wp_ref[...])
        tn, r3 = _rms(t)
        e = tn * g3_ref[...]
        diff = x1_ref[...] + gate * e - t_ref[...]
        dy = diff * (1.0 / D_MODEL)
        dy_ref[...] = dy
        dgp = dy * e * (gate * (1.0 - gate))
        dgp_ref[...] = dgp.astype(BF16)
        de = dy * gate
        dt_ref[...] = _rms_bwd(de * g3_ref[...], tn, r3).astype(BF16)

        @pl.when(pl.program_id(0) == 0)
        def _():
            acc_ref[...] = jnp.zeros_like(acc_ref)

        acc_ref[0:1, :] += jnp.sum(dgp, axis=0, keepdims=True)
        acc_ref[1:2, :] += jnp.sum(de * tn, axis=0, keepdims=True)
        acc_ref[2:3, :] += jnp.sum(diff * diff, axis=0, keepdims=True) * (0.5 / D_MODEL)

    row = pl.BlockSpec((tm, D_MODEL), lambda i: (i, 0))
    vec = pl.BlockSpec((1, D_MODEL), lambda i: (0, 0))
    return pl.pallas_call(
        body, name="ple",
        out_shape=(jax.ShapeDtypeStruct((s, D_MODEL), F32), jax.ShapeDtypeStruct((s, D_MODEL), BF16),
                   jax.ShapeDtypeStruct((s, D_MODEL), BF16), jax.ShapeDtypeStruct((PLE_DIM, s), BF16),
                   jax.ShapeDtypeStruct((SUBLANES, D_MODEL), F32)),
        grid=(s // tm,),
        in_specs=[row, _resident((D_MODEL, D_MODEL)), vec, pl.BlockSpec((tm, PLE_DIM), lambda i: (i, 0)),
                  _resident((PLE_DIM, D_MODEL)), vec, row, row],
        out_specs=(row, row, row, pl.BlockSpec((PLE_DIM, tm), lambda i: (0, i)),
                   pl.BlockSpec((SUBLANES, D_MODEL), lambda i: (0, 0))),
        compiler_params=_params("arbitrary"))(hn2, w_pg, b_pg, p, w_pp, g3, x1, target)


def _gate_bwd(dgp, w_pg, x1, dy, g2, tm):
    s = x1.shape[0]

    def body(d_ref, w_ref, x1_ref, dy_ref, g_ref, dx_ref, dxb_ref, acc_ref):
        dh = _dot_nt(d_ref[...], w_ref[...])
        xn, r = _rms(x1_ref[...])
        dx1 = dy_ref[...] + _rms_bwd(dh * g_ref[...], xn, r)
        dx_ref[...] = dx1
        dxb_ref[...] = dx1.astype(BF16)

        @pl.when(pl.program_id(0) == 0)
        def _():
            acc_ref[...] = jnp.zeros_like(acc_ref)

        acc_ref[0:1, :] += jnp.sum(dh * xn, axis=0, keepdims=True)

    row = pl.BlockSpec((tm, D_MODEL), lambda i: (i, 0))
    return pl.pallas_call(
        body, name="gate_bwd",
        out_shape=(jax.ShapeDtypeStruct((s, D_MODEL), F32), jax.ShapeDtypeStruct((s, D_MODEL), BF16),
                   jax.ShapeDtypeStruct((SUBLANES, D_MODEL), F32)),
        grid=(s // tm,),
        in_specs=[row, _resident((D_MODEL, D_MODEL)), row, row, pl.BlockSpec((1, D_MODEL), lambda i: (0, 0))],
        out_specs=(row, row, pl.BlockSpec((SUBLANES, D_MODEL), lambda i: (0, 0))),
        compiler_params=_params("arbitrary"))(dgp, w_pg, x1, dy, g2)


def _mm_nt(a, b, tm, name, after):
    m, k = a.shape
    n = b.shape[0]

    def body(a_ref, b_ref, after_ref, o_ref):
        o_ref[...] = _dot_nt(a_ref[...], b_ref[...])

    return pl.pallas_call(
        body, name=name,
        out_shape=jax.ShapeDtypeStruct((m, n), F32),
        grid=(m // tm,),
        in_specs=[pl.BlockSpec((tm, k), lambda i: (i, 0)), _resident((n, k)), ANY],
        out_specs=pl.BlockSpec((tm, n), lambda i: (i, 0)),
        compiler_params=_params("parallel"))(a, b, after)


def _attn_bwd(qn, k2, v2, a, z, dmix, conv_wp, sinks):
    s = qn.shape[0]
    nb = s // BLK

    def body(sink_ref, q_ref, kc_ref, kp_ref, vc_ref, vp_ref, a_ref, z_ref, zp_ref, zn_ref, dm_ref, dmn_ref,
             cw_ref, dq_ref, dkc_ref, dkp_ref, dvc_ref, dvp_ref, dz_ref, dzt_ref, acc_ref, ext_ref):
        n = pl.program_id(0)
        valid = _window_mask(n)
        lane = lax.broadcasted_iota(jnp.int32, (1, ATTN_W), 1)

        @pl.when(n == 0)
        def _():
            acc_ref[...] = jnp.zeros_like(acc_ref)

        dz_ref[:, 0:QKV_W] = jnp.zeros((BLK, QKV_W), BF16)
        dsink = jnp.zeros((1, ATTN_W), F32)
        for kvh in range(K2_W // LANES):
            cols = slice(LANES * kvh, LANES * (kvh + 1))
            kw = jnp.concatenate([kp_ref[:, cols], kc_ref[:, cols]], axis=0)
            vw = jnp.concatenate([vp_ref[:, cols], vc_ref[:, cols]], axis=0)
            blocks = [slice(LANES * r, LANES * (r + 1)) for r in (2 * kvh, 2 * kvh + 1)]
            das, avs = [], []
            for rc in blocks:
                g = z_ref[:, _cols(GATE_A0 + rc.start, LANES)]
                sg = _sig(g)
                dm = dm_ref[:, rc]
                av = a_ref[:, rc]
                das.append(dm * (g * sg))
                avs += [av, av]
                dz_ref[:, _cols(GATE_A0 + rc.start, LANES)] = (dm * av * _dsilu(g, sg)).astype(BF16)
            q4 = _stack_heads([q_ref[:, rc] for rc in blocks], jnp.zeros((BLK, LANES), BF16))
            sink, slot = _group_sinks(sink_ref, kvh)
            p, mx, den = _head_probs(q4, kw, valid, sink)
            do4 = _stack_heads(das, 0.0)
            delta = jnp.sum(do4 * jnp.concatenate(avs, axis=0), axis=-1, keepdims=True)
            dob = do4.astype(BF16)
            ds = p * (_dot_nt(dob, vw) - delta) * (HEAD ** -0.5)
            for rc, dq in zip(blocks, _unstack_heads(_dot(ds.astype(BF16), kw))):
                dq_ref[:, rc] = dq
            dk2 = _dot(ds.T.astype(BF16), q4)
            dv2 = _dot(p.T.astype(BF16), dob)
            dkp_ref[:, cols] = dk2[0:BLK]
            dkc_ref[:, cols] = dk2[BLK:2 * BLK]
            dvp_ref[:, cols] = dv2[0:BLK]
            dvc_ref[:, cols] = dv2[BLK:2 * BLK]
            dsk = jnp.exp(sink - mx) / den * delta
            for i in range(GROUP):
                dsink = dsink - jnp.where(lane == GROUP * kvh + i,
                                          jnp.sum(jnp.where(slot == i, dsk, 0.0), axis=0, keepdims=True), 0.0)
        acc_ref[0:1, :] += dsink

        u, um1, um2, cv = _conv_fwd(z_ref, zp_ref, cw_ref, ext_ref, n)
        cb = z_ref[:, _cols(CONV_B0)]
        gc = z_ref[:, _cols(GATE_C0)]
        sgc = _sig(gc)
        dmc = dm_ref[:, ATTN_W:D_MODEL]
        t = dmc * (gc * sgc)
        dcv = t * cb
        dz_ref[:, _cols(CONV_B0)] = (t * cv).astype(BF16)
        dz_ref[:, _cols(GATE_C0)] = (dmc * cb * cv * _dsilu(gc, sgc)).astype(BF16)
        gcn = zn_ref[:, _cols(GATE_C0)]
        dcvn = dmn_ref[:, ATTN_W:D_MODEL] * (gcn * _sig(gcn)) * zn_ref[:, _cols(CONV_B0)]
        ext_ref[0:BLK, :] = dcv
        ext_ref[BLK:BLK + SUBLANES, :] = jnp.where(n < nb - 1, dcvn, 0.0)
        du = (cw_ref[2:3, :] * dcv + cw_ref[1:2, :] * ext_ref[1:1 + BLK, :]
              + cw_ref[0:1, :] * ext_ref[2:2 + BLK, :])
        dz_ref[:, _cols(CONV_C0)] = (du * z_ref[:, _cols(CONV_H0)]).astype(BF16)
        dz_ref[:, _cols(CONV_H0)] = (du * z_ref[:, _cols(CONV_C0)]).astype(BF16)
        acc_ref[1:2, :] += jnp.sum(dcv * um2, axis=0, keepdims=True)
        acc_ref[2:3, :] += jnp.sum(dcv * um1, axis=0, keepdims=True)
        acc_ref[3:4, :] += jnp.sum(dcv * u, axis=0, keepdims=True)
        dzt_ref[...] = dz_ref[...].T

    cur = lambda w: pl.BlockSpec((BLK, w), lambda n: (n, 0))
    prev = lambda w: pl.BlockSpec((BLK, w), lambda n: (jnp.maximum(n - 1, 0), 0))
    nxt = lambda w: pl.BlockSpec(
        (SUBLANES, w), lambda n: (jnp.minimum((n + 1) * (BLK // SUBLANES), nb * (BLK // SUBLANES) - 1), 0))
    f32 = lambda w: jax.ShapeDtypeStruct((s, w), F32)
    return pl.pallas_call(
        body, name="attn_bwd",
        out_shape=(f32(ATTN_W), f32(K2_W), f32(K2_W), f32(K2_W), f32(K2_W),
                   jax.ShapeDtypeStruct((s, IN_W), BF16), jax.ShapeDtypeStruct((IN_W, s), BF16),
                   jax.ShapeDtypeStruct((SUBLANES, ATTN_W), F32)),
        grid=(nb,),
        in_specs=[pl.BlockSpec(memory_space=pltpu.SMEM),
                  cur(ATTN_W), cur(K2_W), prev(K2_W), cur(K2_W), prev(K2_W), cur(ATTN_W), cur(IN_W),
                  pl.BlockSpec((SUBLANES, IN_W), _prev_rows), nxt(IN_W), cur(D_MODEL), nxt(D_MODEL),
                  pl.BlockSpec((SUBLANES, ATTN_W), lambda n: (0, 0))],
        out_specs=(cur(ATTN_W), cur(K2_W), cur(K2_W), cur(K2_W), cur(K2_W), cur(IN_W),
                   pl.BlockSpec((IN_W, BLK), lambda n: (0, n)), pl.BlockSpec((SUBLANES, ATTN_W), lambda n: (0, 0))),
        scratch_shapes=[pltpu.VMEM((BLK + 2 * SUBLANES, ATTN_W), F32)],
        compiler_params=_params("arbitrary"))(sinks, qn, k2, k2, v2, v2, a, z, z, z, dmix, dmix, conv_wp)


def _qkv_bwd(z, dz, dzt, dq, dkc, dkp, dvc, dvp, ra, rbm, rbp, gq2, gk2):
    s = z.shape[0]
    nb = s // BLK

    def body(z_ref, dz_in, dzt_in, dq_ref, dkc_ref, dkp_ref, dvc_ref, dvp_ref, a_ref, bm_ref, bp_ref, gq_ref, gk_ref,
             dz_ref, dzt_ref, acc_ref):
        n = pl.program_id(0)
        a, bm, bp = a_ref[...], bm_ref[...], bp_ref[...]
        lo = _low_half((BLK, LANES))
        last = n == nb - 1

        @pl.when(n == 0)
        def _():
            acc_ref[...] = jnp.zeros_like(acc_ref)

        def norm_bwd(x, dy, gain):
            rr = lax.rsqrt(_half_sums(x * x) * (1.0 / HEAD) + EPS)
            xh = x * rr
            dxg = _rope_t(dy, a, bm, bp)
            dxh = dxg * gain
            dx = rr * (dxh - xh * (_half_sums(dxh * xh) * (1.0 / HEAD)))
            return dx, jnp.sum(dxg * xh, axis=0, keepdims=True)

        def folded(cur_ref, prev_ref, m):
            parts = []
            for h in (2 * m, 2 * m + 1):
                v = cur_ref[:, LANES * h:LANES * (h + 1)] + jnp.where(
                    last, 0.0, prev_ref[:, LANES * h:LANES * (h + 1)])
                parts.append(v + pltpu.roll(v, HEAD, 1))
            return jnp.where(lo, parts[0], parts[1])

        gq_acc = jnp.zeros((1, LANES), F32)
        for r in range(ATTN_W // LANES):
            rc = slice(LANES * r, LANES * (r + 1))
            dx, gg = norm_bwd(z_ref[:, rc], dq_ref[:, rc], gq_ref[...])
            dz_ref[:, rc] = dx.astype(BF16)
            gq_acc = gq_acc + gg
        acc_ref[0:1, :] += gq_acc
        gk_acc = jnp.zeros((1, LANES), F32)
        for m in range(KV_W // LANES):
            kc = slice(ATTN_W + LANES * m, ATTN_W + LANES * (m + 1))
            dx, gg = norm_bwd(z_ref[:, kc], folded(dkc_ref, dkp_ref, m), gk_ref[...])
            dz_ref[:, kc] = dx.astype(BF16)
            gk_acc = gk_acc + gg
            vc = slice(ATTN_W + KV_W + LANES * m, ATTN_W + KV_W + LANES * (m + 1))
            dz_ref[:, vc] = folded(dvc_ref, dvp_ref, m).astype(BF16)
        acc_ref[1:2, :] += gk_acc
        dzt_ref[...] = dz_ref[...].T

    cur = lambda w: pl.BlockSpec((BLK, w), lambda n: (n, 0))
    nxt = lambda w: pl.BlockSpec((BLK, w), lambda n: (jnp.minimum(n + 1, nb - 1), 0))
    one = pl.BlockSpec((1, LANES), lambda n: (0, 0))
    return pl.pallas_call(
        body, name="qkv_bwd",
        out_shape=(jax.ShapeDtypeStruct(dz.shape, dz.dtype), jax.ShapeDtypeStruct(dzt.shape, dzt.dtype),
                   jax.ShapeDtypeStruct((SUBLANES, LANES), F32)),
        grid=(nb,),
        in_specs=[cur(PAIR_W), ANY, ANY, cur(ATTN_W), cur(K2_W), nxt(K2_W), cur(K2_W), nxt(K2_W),
                  cur(LANES), cur(LANES), cur(LANES), one, one],
        out_specs=(cur(QKV_W), pl.BlockSpec((QKV_W, BLK), lambda n: (0, n)),
                   pl.BlockSpec((SUBLANES, LANES), lambda n: (0, 0))),
        input_output_aliases={1: 0, 2: 1},
        compiler_params=_params("arbitrary"))(z, dz, dzt, dq, dkc, dkp, dvc, dvp, ra, rbm, rbp, gq2, gk2)


def _in_bwd(dz, w_pairs, x, dx1, g1, tm):
    s = x.shape[0]

    def body(d_ref, w_ref, x_hbm, dx1_hbm, g_ref, gx_ref, acc_ref, x_buf, dx1_buf, sems):
        i, k = pl.program_id(0), pl.program_id(1)
        rows = pl.ds(pl.multiple_of(i * tm, tm), tm)
        fetch = [pltpu.make_async_copy(x_hbm.at[rows], x_buf, sems.at[0]),
                 pltpu.make_async_copy(dx1_hbm.at[rows], dx1_buf, sems.at[1])]
        sub = min(SUB_ROWS, tm)
        blocks = [slice(r, r + sub) for r in range(0, tm, sub)]

        @pl.when(k == 0)
        def _():
            for cp in fetch:
                cp.start()
            gx_ref[...] = _dot(d_ref[...], w_ref[0])

        @pl.when(k > 0)
        def _():
            gx_ref[...] += _dot(d_ref[...], w_ref[0])

        @pl.when((i == 0) & (k == 0))
        def _():
            acc_ref[...] = jnp.zeros_like(acc_ref)

        @pl.when(k == N_PAIRS - 1)
        def _():
            for cp in fetch:
                cp.wait()
            for rb in blocks:
                dh = gx_ref[rb, :]
                xn, r = _rms(x_buf[rb, :])
                gx_ref[rb, :] = dx1_buf[rb, :] + _rms_bwd(dh * g_ref[...], xn, r)
                acc_ref[0:1, :] += jnp.sum(dh * xn, axis=0, keepdims=True)

    return pl.pallas_call(
        body, name="in_bwd",
        out_shape=(jax.ShapeDtypeStruct((s, D_MODEL), F32), jax.ShapeDtypeStruct((SUBLANES, D_MODEL), F32)),
        grid=(s // tm, N_PAIRS),
        in_specs=[pl.BlockSpec((tm, PAIR_W), lambda i, k: (i, k)),
                  pl.BlockSpec((1, PAIR_W, D_MODEL), lambda i, k: (k, 0, 0)),
                  ANY, ANY, pl.BlockSpec((1, D_MODEL), lambda i, k: (0, 0))],
        out_specs=(pl.BlockSpec((tm, D_MODEL), lambda i, k: (i, 0)),
                   pl.BlockSpec((SUBLANES, D_MODEL), lambda i, k: (0, 0))),
        scratch_shapes=[pltpu.VMEM((tm, D_MODEL), F32), pltpu.VMEM((tm, D_MODEL), F32),
                        pltpu.SemaphoreType.DMA((2,))],
        compiler_params=_params("arbitrary", "arbitrary"))(dz, w_pairs, x, dx1, g1)


def _mm_grad(at, bs, tn, name):
    m, kdim = at.shape
    nblk = [b.shape[1] // tn for b in bs]
    starts = [sum(nblk[:t]) for t in range(len(bs))]

    def body(a_ref, *refs):
        b_refs, o_ref = refs[:len(bs)], refs[len(bs)]
        j = pl.program_id(0)
        for t, b_ref in enumerate(b_refs):
            @pl.when((j >= starts[t]) & (j < starts[t] + nblk[t]))
            def _():
                o_ref[...] = _dot(a_ref[...], b_ref[...]).astype(BF16)

    def b_spec(t):
        return pl.BlockSpec((kdim, tn), lambda j: (0, jnp.clip(j - starts[t], 0, nblk[t] - 1)))

    return pl.pallas_call(
        body, name=name,
        out_shape=jax.ShapeDtypeStruct((m, sum(nblk) * tn), BF16),
        grid=(sum(nblk),),
        in_specs=[_resident((m, kdim))] + [b_spec(t) for t in range(len(bs))],
        out_specs=pl.BlockSpec((m, tn), lambda j: (0, j)),
        compiler_params=_params("parallel"))(at, *bs)


def _grad_w_in(dzt, h):
    kdim = h.shape[0]

    def body(d_ref, h_ref, o_ref):
        o_ref[0] = _dot(d_ref[...], h_ref[...]).astype(BF16)

    return pl.pallas_call(
        body, name="grad_w_in",
        out_shape=jax.ShapeDtypeStruct((N_DEV, SHARD_IN, D_MODEL), BF16),
        grid=(N_DEV,),
        in_specs=[pl.BlockSpec((SHARD_IN, kdim), lambda j: (j, 0)), _resident((kdim, D_MODEL))],
        out_specs=pl.BlockSpec((1, SHARD_IN, D_MODEL), lambda j: (j, 0, 0)),
        compiler_params=_params("parallel"))(dzt, h)


def _place():
    return lax.axis_index("x"), lax.axis_index("y"), lax.axis_index("c")


def _all_reduce_slab(slab, name):
    def body(in_ref, out_ref, gath_ref, send_sems, recv_sems):
        x, y, c = _place()
        me = 4 * x + 2 * y + c
        gath_ref[me] = in_ref[...]
        copies = []
        for k in range(1, N_DEV):
            peer = (x ^ (k >> 2), y ^ ((k >> 1) & 1), c ^ (k & 1))
            copies.append(pltpu.make_async_remote_copy(
                src_ref=in_ref, dst_ref=gath_ref.at[me], send_sem=send_sems.at[k - 1],
                recv_sem=recv_sems.at[k - 1], device_id=peer, device_id_type=MESH))
        for cp in copies:
            cp.start()
        for cp in copies:
            cp.wait_recv()
        for cp in copies:
            cp.wait_send()
        total = gath_ref[0]
        for d in range(1, N_DEV):
            total = total + gath_ref[d]
        out_ref[...] = total

    vmem = pl.BlockSpec(memory_space=pltpu.VMEM)
    return pl.pallas_call(
        body, name=name,
        out_shape=jax.ShapeDtypeStruct(slab.shape, F32),
        in_specs=[vmem], out_specs=vmem,
        scratch_shapes=[pltpu.VMEM((N_DEV,) + slab.shape, F32),
                        pltpu.SemaphoreType.DMA((N_DEV - 1,)), pltpu.SemaphoreType.DMA((N_DEV - 1,))])(slab)


def _pair_sum(g, r, place, tr, name):
    _, _, rows, cols = g.shape

    def body(place_ref, g_ref, r_ref, pb_ref, own_ref):
        tot = g_ref[0, 0].astype(F32) + r_ref[0].astype(F32)
        pb_ref[0] = tot.astype(BF16)

        @pl.when(pl.program_id(1) == place_ref[1])
        def _():
            own_ref[...] = tot

    grid_spec = pltpu.PrefetchScalarGridSpec(
        num_scalar_prefetch=1, grid=(rows // tr, 4),
        in_specs=[pl.BlockSpec((1, 1, tr, cols), lambda i, q, place_ref: (q, place_ref[0], i, 0)),
                  pl.BlockSpec((1, tr, cols), lambda i, q, place_ref: (q, i, 0))],
        out_specs=(pl.BlockSpec((1, tr, cols), lambda i, q, place_ref: (q, i, 0)),
                   pl.BlockSpec((tr, cols), lambda i, q, place_ref: (i, 0))))
    return pl.pallas_call(
        body, name=name, grid_spec=grid_spec,
        out_shape=(jax.ShapeDtypeStruct((4, rows, cols), BF16), jax.ShapeDtypeStruct((rows, cols), F32)),
        compiler_params=_params("arbitrary", "arbitrary"))(place, g, r)


HBM = pl.BlockSpec(memory_space=pltpu.HBM)
SEM = pl.BlockSpec(memory_space=pltpu.SEMAPHORE)
SIDE_EFFECT = pltpu.CompilerParams(has_side_effects=pltpu.SideEffectType.DATAFLOW_SIDE_EFFECTING)
TOKEN = jax.ShapeDtypeStruct((SUBLANES, LANES), F32)


def _hbm(a):
    return pltpu.with_memory_space_constraint(a, pltpu.HBM)


def _hbm_like(arrays):
    return tuple(pltpu.HBM(a.shape, a.dtype) for a in arrays)


def _block_of(px, py, pc):
    return 4 * px + 2 * py + pc


def _gather_start(shards, after):
    na = len(shards)
    lands = [_hbm(lax.empty((N_DEV,) + a.shape, a.dtype)) for a in shards]

    def body(*refs):
        ins, land = refs[:na], refs[na:2 * na]
        send_sems, recv_ici, recv_d2d = refs[2 * na + 1:2 * na + 4]
        token = refs[-1]
        x, y, c = _place()
        for k, peer in enumerate([(x, y, 1 - c), (1 - x, y, c), (x, 1 - y, c), (1 - x, 1 - y, c)]):
            for t in range(na):
                pltpu.make_async_remote_copy(
                    src_ref=ins[t], dst_ref=land[t].at[_block_of(x, y, c)], send_sem=send_sems.at[4 * t + k],
                    recv_sem=recv_d2d.at[4 * t] if k == 0 else recv_ici.at[3 * t + k - 1],
                    device_id=peer, device_id_type=MESH).start()
        token[...] = jnp.zeros_like(token)

    out = pl.pallas_call(
        body, name="gather_start",
        out_shape=(pltpu.SemaphoreType.DMA((4 * na,)), pltpu.SemaphoreType.DMA((3 * na,)),
                   pltpu.SemaphoreType.DMA((4 * na,)), *_hbm_like(lands), TOKEN),
        in_specs=[ANY] * na + [HBM] * na + [ANY],
        out_specs=(SEM, SEM, SEM, *[HBM] * na, pl.BlockSpec(memory_space=pltpu.VMEM)),
        input_output_aliases={na + i: 3 + i for i in range(na)},
        compiler_params=SIDE_EFFECT)(*shards, *lands, after)
    send_sems, recv_ici, recv_d2d = out[:3]
    state = dict(send=send_sems, ici=recv_ici, d2d=recv_d2d, shards=list(shards), lands=out[3:3 + na])
    return state, out[-1]


def _gather_forward(state, after):
    lands = state["lands"]
    na = len(lands)

    def body(*refs):
        land = refs[:na]
        recv_ici, recv_d2d = refs[na], refs[na + 1]
        fwd_sems, token = refs[-2], refs[-1]
        x, y, c = _place()
        for j, chip in enumerate([(1 - x, y), (x, 1 - y), (1 - x, 1 - y)]):
            for t in range(na):
                blk = land[t].at[_block_of(*chip, c)]
                pltpu.make_async_remote_copy(
                    src_ref=blk, dst_ref=blk, send_sem=fwd_sems.at[3 * t + j], recv_sem=recv_ici.at[3 * t + j],
                    device_id=(x, y, c), device_id_type=MESH).wait_recv()
                pltpu.make_async_remote_copy(
                    src_ref=blk, dst_ref=blk, send_sem=fwd_sems.at[3 * t + j], recv_sem=recv_d2d.at[4 * t + 1 + j],
                    device_id=(x, y, 1 - c), device_id_type=MESH).start()
        token[...] = jnp.zeros_like(token)

    out = pl.pallas_call(
        body, name="gather_forward",
        out_shape=(*_hbm_like(lands), pltpu.SemaphoreType.DMA((3 * na,)), TOKEN),
        in_specs=[HBM] * na + [SEM, SEM, ANY],
        out_specs=(*[HBM] * na, SEM, pl.BlockSpec(memory_space=pltpu.VMEM)),
        input_output_aliases={i: i for i in range(na)},
        compiler_params=SIDE_EFFECT)(*lands, state["ici"], state["d2d"], after)
    return dict(state, lands=out[:na], fwd=out[na]), out[-1]


def _gather_wait(state, after):
    shards, lands = state["shards"], state["lands"]
    na = len(lands)

    def body(*refs):
        ins, land = refs[:na], refs[na:2 * na]
        send_sems, fwd_sems, recv_d2d = refs[2 * na:2 * na + 3]
        x, y, c = _place()
        chips = [(1 - x, y), (x, 1 - y), (1 - x, 1 - y)]
        for t in range(na):
            mine = land[t].at[_block_of(x, y, c)]
            for k in range(4):
                pltpu.make_async_remote_copy(
                    src_ref=ins[t], dst_ref=mine, send_sem=send_sems.at[4 * t + k], recv_sem=recv_d2d.at[4 * t],
                    device_id=(x, y, c), device_id_type=MESH).wait_send()
            for j, chip in enumerate(chips):
                blk = land[t].at[_block_of(*chip, c)]
                pltpu.make_async_remote_copy(
                    src_ref=blk, dst_ref=blk, send_sem=fwd_sems.at[3 * t + j], recv_sem=recv_d2d.at[4 * t + 1 + j],
                    device_id=(x, y, c), device_id_type=MESH).wait_send()
            for k, blk_id in enumerate([_block_of(x, y, 1 - c)] + [_block_of(*chip, 1 - c) for chip in chips]):
                blk = land[t].at[blk_id]
                pltpu.make_async_remote_copy(
                    src_ref=blk, dst_ref=blk, send_sem=send_sems.at[4 * t], recv_sem=recv_d2d.at[4 * t + k],
                    device_id=(x, y, c), device_id_type=MESH).wait_recv()

    out = pl.pallas_call(
        body, name="gather_wait",
        out_shape=_hbm_like(lands),
        in_specs=[ANY] * na + [HBM] * na + [SEM, SEM, SEM, ANY],
        out_specs=tuple([HBM] * na),
        input_output_aliases={na + i: i for i in range(na)},
        compiler_params=SIDE_EFFECT)(*shards, *lands, state["send"], state["fwd"], state["d2d"], after)
    return out


def _gather_from_sibling(state, after):
    (land,) = state["lands"]

    def body(land_ref, recv_d2d, after_ref, out_ref):
        x, y, c = _place()
        blk = land_ref.at[_block_of(x, y, 1 - c)]
        pltpu.make_async_remote_copy(src_ref=blk, dst_ref=blk, send_sem=recv_d2d.at[0], recv_sem=recv_d2d.at[0],
                                     device_id=(x, y, c), device_id_type=MESH).wait_recv()

    out = pl.pallas_call(
        body, name="gather_from_sibling", out_shape=pltpu.HBM(land.shape, land.dtype),
        in_specs=[HBM, SEM, ANY], out_specs=HBM, input_output_aliases={0: 0},
        compiler_params=SIDE_EFFECT)(land, state["d2d"], after)
    return dict(state, lands=[out])


def _gather_from_chip(state, j, after, last):
    (shard,), (land,) = state["shards"], state["lands"]

    def chip_blocks(land_ref):
        x, y, c = _place()
        chip = [(1 - x, y), (x, 1 - y), (1 - x, 1 - y)][j]
        return (x, y, c), land_ref.at[_block_of(*chip, c)], land_ref.at[_block_of(*chip, 1 - c)]

    def forward(land_ref, recv_ici, recv_d2d, after_ref, out_ref, fwd_sem):
        (x, y, c), mine, _ = chip_blocks(land_ref)
        pltpu.make_async_remote_copy(src_ref=mine, dst_ref=mine, send_sem=fwd_sem.at[0], recv_sem=recv_ici.at[j],
                                     device_id=(x, y, c), device_id_type=MESH).wait_recv()
        pltpu.make_async_remote_copy(src_ref=mine, dst_ref=mine, send_sem=fwd_sem.at[0], recv_sem=recv_d2d.at[1 + j],
                                     device_id=(x, y, 1 - c), device_id_type=MESH).start()

    land, fwd_sem = pl.pallas_call(
        forward, name="gather_pass_chip_" + str(j),
        out_shape=(pltpu.HBM(land.shape, land.dtype), pltpu.SemaphoreType.DMA((1,))),
        in_specs=[HBM, SEM, SEM, ANY], out_specs=(HBM, SEM), input_output_aliases={0: 0},
        compiler_params=SIDE_EFFECT)(land, state["ici"], state["d2d"], after)

    def arrive(land_ref, fwd_sem, recv_d2d, shard_ref, send_sems, out_ref):
        (x, y, c), mine, theirs = chip_blocks(land_ref)
        pltpu.make_async_remote_copy(src_ref=theirs, dst_ref=theirs, send_sem=fwd_sem.at[0],
                                     recv_sem=recv_d2d.at[1 + j], device_id=(x, y, c),
                                     device_id_type=MESH).wait_recv()
        pltpu.make_async_remote_copy(src_ref=mine, dst_ref=mine, send_sem=fwd_sem.at[0], recv_sem=recv_d2d.at[1 + j],
                                     device_id=(x, y, c), device_id_type=MESH).wait_send()
        for k in range(4 if last else 0):
            pltpu.make_async_remote_copy(
                src_ref=shard_ref, dst_ref=land_ref.at[_block_of(x, y, c)], send_sem=send_sems.at[k],
                recv_sem=recv_d2d.at[0], device_id=(x, y, c), device_id_type=MESH).wait_send()

    land = pl.pallas_call(
        arrive, name="gather_take_chip_" + str(j), out_shape=pltpu.HBM(land.shape, land.dtype),
        in_specs=[HBM, SEM, SEM, ANY, SEM], out_specs=HBM, input_output_aliases={0: 0},
        compiler_params=SIDE_EFFECT)(land, fwd_sem, state["d2d"], shard, state["send"])
    return dict(state, lands=[land])


def _to_sibling(srcs, lands, send_sems, recv_sems):
    x, y, c = _place()
    return [pltpu.make_async_remote_copy(
        src_ref=srcs[t].at[:, 1 - c], dst_ref=lands[t], send_sem=send_sems.at[t], recv_sem=recv_sems.at[t],
        device_id=(x, y, 1 - c), device_id_type=MESH) for t in range(len(srcs))]


def _to_chips(srcs, lands, send_sems, recv_sems):
    x, y, c = _place()
    copies = []
    for k in (1, 2, 3):
        px, py = x ^ (k >> 1), y ^ (k & 1)
        copies += [pltpu.make_async_remote_copy(
            src_ref=srcs[t].at[2 * px + py], dst_ref=lands[t].at[k - 1], send_sem=send_sems.at[3 * t + k - 1],
            recv_sem=recv_sems.at[3 * t + k - 1], device_id=(px, py, c), device_id_type=MESH) for t in range(len(srcs))]
    return copies


def _exchange_start(name, srcs, land_shapes, copies, per_array, after):
    na = len(srcs)
    lands = [_hbm(lax.empty(shp, a.dtype)) for shp, a in zip(land_shapes, srcs)]

    def body(*refs):
        token = refs[-1]
        for cp in copies(refs[:na], refs[na:2 * na], refs[2 * na + 1], refs[2 * na + 2]):
            cp.start()
        token[...] = jnp.zeros_like(token)

    out = pl.pallas_call(
        body, name=name,
        out_shape=(pltpu.SemaphoreType.DMA((na * per_array,)), pltpu.SemaphoreType.DMA((na * per_array,)),
                   *_hbm_like(lands), TOKEN),
        in_specs=[ANY] * na + [HBM] * na + [ANY],
        out_specs=(SEM, SEM, *[HBM] * na, pl.BlockSpec(memory_space=pltpu.VMEM)),
        input_output_aliases={na + i: 2 + i for i in range(na)},
        compiler_params=SIDE_EFFECT)(*srcs, *lands, after)
    return dict(send=out[0], recv=out[1], srcs=list(srcs), lands=out[2:2 + na]), out[-1]


def _exchange_wait(name, state, copies, after):
    srcs, lands = state["srcs"], state["lands"]
    na = len(srcs)

    def body(*refs):
        for cp in copies(refs[:na], refs[na:2 * na], refs[2 * na], refs[2 * na + 1]):
            cp.wait_send()
            cp.wait_recv()

    out = pl.pallas_call(
        body, name=name,
        out_shape=_hbm_like(lands),
        in_specs=[ANY] * na + [HBM] * na + [SEM, SEM, ANY],
        out_specs=tuple([HBM] * na),
        input_output_aliases={na + i: i for i in range(na)},
        compiler_params=SIDE_EFFECT)(*srcs, *lands, state["send"], state["recv"], after)
    return out


def _adamw_math(w, g, m, v):
    m = ADAM_B1 * m + (1.0 - ADAM_B1) * g
    v = ADAM_B2 * v + (1.0 - ADAM_B2) * (g * g)
    m_hat = m / (1.0 - ADAM_B1 ** ADAM_STEP)
    v_hat = v / (1.0 - ADAM_B2 ** ADAM_STEP)
    return -ADAM_LR * (m_hat / (jnp.sqrt(v_hat) + ADAM_EPS) + ADAM_WD * w), m, v


def _adamw(own, others, w, m, v, tr, name):
    rows, cols = w.shape
    blk = pl.BlockSpec((tr, cols), lambda i: (i, 0))

    def body(own_ref, oth_ref, w_ref, m_ref, v_ref, g_ref, d_ref, nm_ref, nv_ref):
        g = own_ref[...]
        for k in range(3):
            g = g + oth_ref[k].astype(F32)
        g_ref[...] = g
        d_ref[...], nm_ref[...], nv_ref[...] = _adamw_math(w_ref[...], g, m_ref[...], v_ref[...])

    out = jax.ShapeDtypeStruct((rows, cols), F32)
    return pl.pallas_call(
        body, name=name, out_shape=(out, out, out, out), grid=(rows // tr,),
        in_specs=[blk, pl.BlockSpec((3, tr, cols), lambda i: (0, i, 0)), blk, blk, blk],
        out_specs=(blk, blk, blk, blk),
        compiler_params=_params("parallel"))(own, others, w, m, v)


def _adamw_slab(w, g, m, v):
    def body(w_ref, g_ref, m_ref, v_ref, d_ref, nm_ref, nv_ref):
        d_ref[...], nm_ref[...], nv_ref[...] = _adamw_math(w_ref[...], g_ref[...], m_ref[...], v_ref[...])

    out = jax.ShapeDtypeStruct(w.shape, F32)
    vmem = pl.BlockSpec(memory_space=pltpu.VMEM)
    return pl.pallas_call(body, name="adamw_small", out_shape=(out, out, out),
                          in_specs=[vmem] * 4, out_specs=(vmem, vmem, vmem))(w, g, m, v)


def _row(v, width=D_MODEL):
    v = v.reshape(1, -1)
    return jnp.pad(v, ((0, 0), (0, width - v.shape[1])))


def _tables(s, gq, gk, conv_w):
    gq2 = jnp.tile(gq.reshape(1, HEAD), (1, 2))
    gk2 = jnp.tile(gk.reshape(1, HEAD), (1, 2))
    conv_wp = jnp.pad(conv_w, ((0, SUBLANES - conv_w.shape[0]), (0, 0)))
    return _rope_tables(s), gq2, gk2, conv_wp


def _pair_id(q):
    return jnp.full((1,), q, jnp.int32)


def _forward_in(x, g1, shards):
    s = x.shape[0]
    h = _prenorm(x, g1, min(512, s))
    z, w_pairs = lax.empty((s, IN_W), F32), lax.empty((N_PAIRS, PAIR_W, D_MODEL), BF16)
    for q in range(N_PAIRS):
        z, w_pairs = _fwd_in_pair(h, shards, z, w_pairs, _pair_id(q), min(512, s), "fwd_in_" + str(q))
    return h, z, w_pairs


def _forward_attn(z, rope, gq2, gk2, conv_wp, sinks):
    s = z.shape[0]
    qn, k2, v2 = _qk_prep(z, *rope, gq2, gk2, min(256, s))
    a, mix, mixt = _attn_fwd(qn, k2, v2, z, conv_wp, sinks)
    return qn, k2, v2, a, mix, mixt


def _forward_out(x, p, target, mix, mixt, w_out, g2, w_pg, b_pg, w_pp, g3):
    s = x.shape[0]
    tm = min(512, s)
    x1, hn2, hn2t = _fwd_out(mix, w_out, x, g2, tm)
    dy, dgp, dt, pt, acc_ple = _ple(hn2, w_pg, b_pg, p, w_pp, g3, x1, target, min(256, s))
    dx1, dx1b, acc_g2 = _gate_bwd(dgp, w_pg, x1, dy, g2, tm)
    gw_out = _mm_grad(mixt, [dx1b], 512, "grad_w_out")
    gw_pg = _mm_grad(hn2t, [dgp], 512, "grad_w_ple_gate")
    gw_pp = _mm_grad(pt, [dt], 512, "grad_w_ple_proj")
    return dx1, dx1b, (gw_out, gw_pg, gw_pp), acc_ple, acc_g2


def _backward_attn(dmix, h, z, qn, k2, v2, a, rope, gq2, gk2, conv_wp, sinks):
    dq, dkc, dkp, dvc, dvp, dz, dzt, acc_attn = _attn_bwd(qn, k2, v2, a, z, dmix, conv_wp, sinks)
    dz, dzt, acc_qk = _qkv_bwd(z, dz, dzt, dq, dkc, dkp, dvc, dvp, *rope, gq2, gk2)
    return dz, _grad_w_in(dzt, h), acc_attn, acc_qk


def _small_rows(acc_g1, acc_g2, acc_ple, acc_qk, acc_attn):
    fold = lambda v: _row((v[:HEAD] + v[HEAD:]))
    return [acc_g1[0:1], acc_g2[0:1], acc_ple[0:1], acc_ple[1:2], fold(acc_qk[0]), fold(acc_qk[1]),
            _row(acc_attn[0, :N_Q_HEADS]), _row(acc_attn[1]), _row(acc_attn[2]), _row(acc_attn[3]), acc_ple[2:3]]


def _local_step(x, p, target, g1, shards, gq, gk, sinks, conv_w, w_out, g2, w_pg, b_pg, w_pp, g3):
    rope, gq2, gk2, conv_wp = _tables(x.shape[0], gq, gk, conv_w)
    h, z, w_pairs = _forward_in(x, g1, shards)
    qn, k2, v2, a, mix, mixt = _forward_attn(z, rope, gq2, gk2, conv_wp, sinks)
    dx1, dx1b, (gw_out, gw_pg, gw_pp), acc_ple, acc_g2 = _forward_out(
        x, p, target, mix, mixt, w_out, g2, w_pg, b_pg, w_pp, g3)
    dmix = _mm_nt(dx1b, w_out, min(512, x.shape[0]), "out_bwd", dx1b)
    dz, gw_in, acc_attn, acc_qk = _backward_attn(dmix, h, z, qn, k2, v2, a, rope, gq2, gk2, conv_wp, sinks)
    grad_x, acc_g1 = _in_bwd(dz, w_pairs, x, dx1, g1, min(512, x.shape[0]))
    return grad_x, (gw_in, gw_out, gw_pg, gw_pp), _small_rows(acc_g1, acc_g2, acc_ple, acc_qk, acc_attn)


ROW_CONV, ROW_LOSS = 7, 10


def _slab(rows):
    rows = list(rows)
    return jnp.concatenate(rows + [jnp.zeros((SLAB_ROWS - len(rows), D_MODEL), F32)], axis=0)


def _by_owner(g):
    return g.reshape((4, 2) + g.shape[1:])


def kernel(x, p, norm_gain, w_in, q_norm_gain, k_norm_gain, attn_sinks, conv_w, w_out, ple_gate_norm_gain, w_ple_gate, b_ple_gate, w_ple_proj, ple_norm_gain, loss_target, m_norm_gain, m_w_in, m_q_norm_gain, m_k_norm_gain, m_attn_sinks, m_conv_w, m_w_out, m_ple_gate_norm_gain, m_w_ple_gate, m_b_ple_gate, m_w_ple_proj, m_ple_norm_gain, v_norm_gain, v_w_in, v_q_norm_gain, v_k_norm_gain, v_attn_sinks, v_conv_w, v_w_out, v_ple_gate_norm_gain, v_w_ple_gate, v_b_ple_gate, v_w_ple_proj, v_ple_norm_gain):
    me = 4 * lax.axis_index("x") + 2 * lax.axis_index("y") + lax.axis_index("c")
    place = jnp.stack([lax.axis_index("c"), 2 * lax.axis_index("x") + lax.axis_index("y")]).astype(jnp.int32)
    conv_cols = conv_w.shape[2]
    xs, ps, target = x[0], p[0, 0], loss_target[0]
    zero = lambda token: token[0:1, 0:1]

    shard_in = w_in[0].T.astype(BF16)
    own_late = [w_out[0].astype(BF16), w_ple_gate[0].astype(BF16), w_ple_proj[0].astype(BF16)]
    with_own = lambda gathered, own: lax.dynamic_update_slice(gathered, own[None], (me, 0, 0))
    early, started = _gather_start([shard_in], shard_in)
    late, started_late = _gather_start(own_late, started)
    tm = min(512, xs.shape[0])
    h = _prenorm(xs, norm_gain + zero(started) + zero(started_late), tm)
    conv_rows = [lax.dynamic_update_slice(jnp.zeros((1, D_MODEL), F32), conv_w[0, t:t + 1], (0, conv_cols * me))
                 for t in range(3)]
    conv_full = _all_reduce_slab(_slab(conv_rows), "gather_conv_w")[0:3, :ATTN_W]
    rope, gq2, gk2, conv_wp = _tables(xs.shape[0], q_norm_gain[0], k_norm_gain[0], conv_full)

    z, w_pairs = lax.empty((xs.shape[0], IN_W), F32), lax.empty((N_PAIRS, PAIR_W, D_MODEL), BF16)
    early = _gather_from_sibling(early, h)
    early = dict(early, lands=[with_own(early["lands"][0], shard_in)])
    z, w_pairs = _fwd_in_pair(h, early["lands"][0], z, w_pairs, place[1:2], tm, "fwd_in_own")
    for j, flip in enumerate((2, 1, 3)):
        early = _gather_from_chip(early, j, z, last=j == 2)
        z, w_pairs = _fwd_in_pair(h, early["lands"][0], z, w_pairs, place[1:2] ^ flip, tm, "fwd_in_chip_" + str(j))
    late, forwarded = _gather_forward(late, z)
    qn, k2, v2, a, mix, mixt = _forward_attn(z, rope, gq2 + zero(forwarded), gk2, conv_wp, attn_sinks)
    g_out, g_pg, g_pp = (with_own(g, own) for g, own in zip(_gather_wait(late, mix), own_late))
    w_out_f = g_out.reshape(D_MODEL, D_MODEL)
    w_pg_f = g_pg.reshape(D_MODEL, D_MODEL)
    w_pp_f = jnp.transpose(g_pp, (1, 0, 2)).reshape(PLE_DIM, D_MODEL)

    dx1, dx1b, (gw_out, gw_pg, gw_pp), acc_ple, acc_g2 = _forward_out(
        xs, ps, target, mix, mixt, w_out_f, ple_gate_norm_gain, w_pg_f, b_ple_gate, w_pp_f, ple_norm_gain)

    names = ("w_out", "w_ple_gate", "w_ple_proj")
    gw_pp_t = jnp.transpose(gw_pp.reshape(PLE_DIM, N_DEV, PLE_DIM), (1, 0, 2))
    grads = [_by_owner(gw_out.reshape(N_DEV, D_MODEL // N_DEV, D_MODEL)),
             _by_owner(gw_pg.reshape(N_DEV, D_MODEL // N_DEV, D_MODEL)), _by_owner(gw_pp_t)]
    pairs, paired = _exchange_start("pair_start", grads, [(4,) + g.shape[2:] for g in grads], _to_sibling, 1, dx1b)
    dmix = _mm_nt(dx1b, w_out_f, tm, "out_bwd", paired)
    from_sibling = _exchange_wait("pair_wait", pairs, _to_sibling, dmix)
    sums = [_pair_sum(g, r, place, 256, "pair_sum_" + nm) for g, r, nm in zip(pairs["srcs"], from_sibling, names)]
    chips, sent = _exchange_start("chip_start", [pb for pb, _ in sums], [(3,) + pb.shape[1:] for pb, _ in sums],
                                  _to_chips, 3, sums[-1][1])

    dz, gw_in, acc_attn, acc_qk = _backward_attn(
        dmix, h, z, qn, k2, v2, a, rope, gq2, gk2, conv_wp, attn_sinks + zero(sent))

    gw_in_t = [_by_owner(gw_in)]
    pairs_in, _ = _exchange_start("pair_start_w_in", gw_in_t, [(4,) + gw_in_t[0].shape[2:]], _to_sibling, 1, gw_in)
    from_chips = _exchange_wait("chip_wait", chips, _to_chips, gw_in)
    big = {}
    for (_, own), oth, w, m, v, nm in zip(sums, from_chips, (w_out, w_ple_gate, w_ple_proj),
                                          (m_w_out, m_w_ple_gate, m_w_ple_proj),
                                          (v_w_out, v_w_ple_gate, v_w_ple_proj), names):
        big[nm] = [t[None] for t in _adamw(own, oth, w[0], m[0], v[0], 256, "adamw_" + nm)]

    (from_sibling_in,) = _exchange_wait("pair_wait_w_in", pairs_in, _to_sibling, big[names[-1]][0])
    pb_in, own_in = _pair_sum(pairs_in["srcs"][0], from_sibling_in, place, SHARD_IN // 2, "pair_sum_w_in")
    chips_in, sent_in = _exchange_start("chip_start_w_in", [pb_in], [(3,) + pb_in.shape[1:]], _to_chips, 3, own_in)
    grad_x, acc_g1 = _in_bwd(dz, w_pairs, xs, dx1, norm_gain + zero(sent_in), tm)
    (from_chips_in,) = _exchange_wait("chip_wait_w_in", chips_in, _to_chips, grad_x)
    big["w_in"] = [t.T[None] for t in _adamw(own_in, from_chips_in, w_in[0].T, m_w_in[0].T, v_w_in[0].T, SHARD_IN // 4,
                                             "adamw_w_in")]

    red = _all_reduce_slab(_slab(_small_rows(acc_g1, acc_g2, acc_ple, acc_qk, acc_attn)), "reduce_small")
    loss = jnp.sum(red[ROW_LOSS])
    g_conv = [lax.dynamic_slice(red[ROW_CONV + t:ROW_CONV + t + 1], (0, conv_cols * me), (1, conv_cols))
              for t in range(3)]
    small = [norm_gain, ple_gate_norm_gain, b_ple_gate, ple_norm_gain, q_norm_gain, k_norm_gain, attn_sinks]
    small_m = [m_norm_gain, m_ple_gate_norm_gain, m_b_ple_gate, m_ple_norm_gain, m_q_norm_gain, m_k_norm_gain,
               m_attn_sinks]
    small_v = [v_norm_gain, v_ple_gate_norm_gain, v_b_ple_gate, v_ple_norm_gain, v_q_norm_gain, v_k_norm_gain,
               v_attn_sinks]
    pack = lambda vs, cw: _slab([_row(t) for t in vs] + [_row(cw[0, t]) for t in range(3)])
    g_slab = _slab([red[t:t + 1] for t in range(ROW_CONV)] + [_row(t) for t in g_conv])
    d_slab, m_slab, v_slab = _adamw_slab(pack(small, conv_w), g_slab, pack(small_m, m_conv_w), pack(small_v, v_conv_w))

    def unpack(slab_):
        outs = [slab_[t:t + 1, :w.shape[1]] for t, w in enumerate(small)]
        return outs, slab_[ROW_CONV:ROW_CONV + 3, :conv_cols][None]

    (g_s, g_cv), (d_s, d_cv), (m_s, m_cv), (v_s, v_cv) = (unpack(t) for t in (g_slab, d_slab, m_slab, v_slab))

    def order(sm, cv, k):
        return [sm[0], big["w_in"][k], sm[4], sm[5], sm[6], cv, big["w_out"][k], sm[1], big["w_ple_gate"][k], sm[2],
                big["w_ple_proj"][k], sm[3]]

    return (loss, grad_x[None], *order(g_s, g_cv, 0), *order(d_s, d_cv, 1), *order(m_s, m_cv, 2),
            *order(v_s, v_cv, 3))
```

```python
import jax
import jax.numpy as jnp
from jax import lax
from jax.experimental import pallas as pl
from jax.experimental.pallas import tpu as pltpu

F32, BF16 = jnp.float32, jnp.bfloat16

D_MODEL = 2048
PLE_DIM = 256
ATTN_W = 1024
HEAD = 64
N_Q_HEADS = 16
KV_W = 256
QKV_W = ATTN_W + 2 * KV_W
REST_W = 5 * 1024
IN_W = QKV_W + REST_W
GATE_A0, CONV_B0, CONV_C0, CONV_H0, GATE_C0 = (QKV_W + 1024 * t for t in range(5))
K2_W = 4 * 128
ROT = 16
ROPE_THETA = 500000.0
EPS = 1e-6
NEG_INF = -1e30
BLK = 128
LANES = 128
SUBLANES = 8
N_DEV = 8
SHARD_IN = IN_W // N_DEV
PAIR_W = 2 * SHARD_IN
N_PAIRS = IN_W // PAIR_W
SLAB_ROWS = 16
SUB_ROWS = 128
V7X_VMEM_LIMIT = 52 * 1024 * 1024

ADAM_LR, ADAM_B1, ADAM_B2, ADAM_EPS, ADAM_WD, ADAM_STEP = 0.001, 0.9, 0.999, 1e-08, 0.01, 10
MESH = pl.DeviceIdType.MESH


def _params(*semantics):
    return pltpu.CompilerParams(dimension_semantics=semantics, vmem_limit_bytes=V7X_VMEM_LIMIT)


ANY = pl.BlockSpec(memory_space=pl.ANY)


def _resident(shape):
    return pl.BlockSpec(shape, lambda *_: (0,) * len(shape), pipeline_mode=pl.Buffered(1))


def _dot(a, b):
    return jnp.dot(a, b, preferred_element_type=F32)


def _dot_nt(a, b):
    return lax.dot_general(a, b, (((1,), (1,)), ((), ())), preferred_element_type=F32)


def _rms(xf):
    r = lax.rsqrt(jnp.mean(xf * xf, axis=-1, keepdims=True) + EPS)
    return xf * r, r


def _rms_bwd(dxn, xn, r):
    return r * (dxn - xn * jnp.mean(dxn * xn, axis=-1, keepdims=True))


def _sig(g):
    return jax.nn.sigmoid(g)


def _dsilu(g, sg):
    return sg * (1.0 + g * (1.0 - sg))


def _low_half(shape):
    return lax.broadcasted_iota(jnp.int32, shape, len(shape) - 1) < HEAD


def _half_sums(v):
    lo = _low_half(v.shape)
    s_lo = jnp.sum(jnp.where(lo, v, 0.0), axis=-1, keepdims=True)
    s_hi = jnp.sum(jnp.where(lo, 0.0, v), axis=-1, keepdims=True)
    return jnp.where(lo, s_lo, s_hi)


def _rope(v, a, bm, bp):
    return v * a + pltpu.roll(v, LANES - ROT // 2, 1) * bm + pltpu.roll(v, ROT // 2, 1) * bp


def _rope_t(dy, a, bm, bp):
    return dy * a + pltpu.roll(dy * bm, ROT // 2, 1) + pltpu.roll(dy * bp, LANES - ROT // 2, 1)


def _dup_halves(v):
    lo = _low_half(v.shape)
    a = jnp.where(lo, v, 0.0)
    b = jnp.where(lo, 0.0, v)
    return a + pltpu.roll(a, HEAD, 1), b + pltpu.roll(b, HEAD, 1)


def _rope_tables(s):
    half = ROT // 2
    lane = lax.broadcasted_iota(jnp.int32, (s, LANES), 1) % HEAD
    pos = lax.broadcasted_iota(jnp.int32, (s, LANES), 0).astype(F32)
    inv_freq = jnp.power(jnp.float32(ROPE_THETA), -(lane % half).astype(F32) * 2.0 / ROT)
    ang = pos * inv_freq
    cos, sin = jnp.cos(ang), jnp.sin(ang)
    a = jnp.where(lane < ROT, cos, 1.0)
    bm = jnp.where(lane < half, -sin, 0.0)
    bp = jnp.where((lane >= half) & (lane < ROT), sin, 0.0)
    return a, bm, bp


def _prenorm(x, g1, tm):
    s = x.shape[0]

    def body(x_ref, g_ref, h_ref):
        xn, _ = _rms(x_ref[...])
        h_ref[...] = (xn * g_ref[...]).astype(BF16)

    return pl.pallas_call(
        body, name="prenorm",
        out_shape=jax.ShapeDtypeStruct((s, D_MODEL), BF16),
        grid=(s // tm,),
        in_specs=[pl.BlockSpec((tm, D_MODEL), lambda i: (i, 0)), pl.BlockSpec((1, D_MODEL), lambda i: (0, 0))],
        out_specs=pl.BlockSpec((tm, D_MODEL), lambda i: (i, 0)),
        compiler_params=_params("parallel"))(x, g1)


def _fwd_in_pair(h, shards, z, w_pairs, pair, tm, name):
    s = h.shape[0]

    def body(pair_ref, h_ref, lo_ref, hi_ref, z_in, wp_in, z_ref, wp_ref):
        @pl.when(pl.program_id(0) == 0)
        def _():
            wp_ref[0, 0:SHARD_IN, :] = lo_ref[0]
            wp_ref[0, SHARD_IN:PAIR_W, :] = hi_ref[0]

        z_ref[...] = _dot_nt(h_ref[...], wp_ref[0])

    grid_spec = pltpu.PrefetchScalarGridSpec(
        num_scalar_prefetch=1, grid=(s // tm,),
        in_specs=[pl.BlockSpec((tm, D_MODEL), lambda i, p: (i, 0)),
                  pl.BlockSpec((1, SHARD_IN, D_MODEL), lambda i, p: (2 * p[0], 0, 0)),
                  pl.BlockSpec((1, SHARD_IN, D_MODEL), lambda i, p: (2 * p[0] + 1, 0, 0)), ANY, ANY],
        out_specs=(pl.BlockSpec((tm, PAIR_W), lambda i, p: (i, p[0])),
                   pl.BlockSpec((1, PAIR_W, D_MODEL), lambda i, p: (p[0], 0, 0))))
    return pl.pallas_call(
        body, name=name, grid_spec=grid_spec,
        out_shape=(jax.ShapeDtypeStruct(z.shape, z.dtype), jax.ShapeDtypeStruct(w_pairs.shape, w_pairs.dtype)),
        input_output_aliases={4: 0, 5: 1},
        compiler_params=_params("arbitrary"))(pair, h, shards, shards, z, w_pairs)


def _qk_prep(z, ra, rbm, rbp, gq2, gk2, tm):
    s = z.shape[0]

    def body(z_ref, a_ref, bm_ref, bp_ref, gq_ref, gk_ref, q_ref, k2_ref, v2_ref):
        a, bm, bp = a_ref[...], bm_ref[...], bp_ref[...]
        for r in range(ATTN_W // LANES):
            x = z_ref[:, LANES * r:LANES * (r + 1)]
            rr = lax.rsqrt(_half_sums(x * x) * (1.0 / HEAD) + EPS)
            q_ref[:, LANES * r:LANES * (r + 1)] = _rope(x * rr * gq_ref[...], a, bm, bp).astype(BF16)
        for m in range(KV_W // LANES):
            x = z_ref[:, ATTN_W + LANES * m:ATTN_W + LANES * (m + 1)]
            rr = lax.rsqrt(_half_sums(x * x) * (1.0 / HEAD) + EPS)
            k_lo, k_hi = _dup_halves(_rope(x * rr * gk_ref[...], a, bm, bp))
            k2_ref[:, 2 * LANES * m:2 * LANES * m + LANES] = k_lo.astype(BF16)
            k2_ref[:, 2 * LANES * m + LANES:2 * LANES * (m + 1)] = k_hi.astype(BF16)
            v_lo, v_hi = _dup_halves(z_ref[:, ATTN_W + KV_W + LANES * m:ATTN_W + KV_W + LANES * (m + 1)])
            v2_ref[:, 2 * LANES * m:2 * LANES * m + LANES] = v_lo.astype(BF16)
            v2_ref[:, 2 * LANES * m + LANES:2 * LANES * (m + 1)] = v_hi.astype(BF16)

    row = lambda w: pl.BlockSpec((tm, w), lambda i: (i, 0))
    one = pl.BlockSpec((1, LANES), lambda i: (0, 0))
    return pl.pallas_call(
        body, name="qk_prep",
        out_shape=(jax.ShapeDtypeStruct((s, ATTN_W), BF16), jax.ShapeDtypeStruct((s, K2_W), BF16),
                   jax.ShapeDtypeStruct((s, K2_W), BF16)),
        grid=(s // tm,),
        in_specs=[row(PAIR_W), row(LANES), row(LANES), row(LANES), one, one],
        out_specs=(row(ATTN_W), row(K2_W), row(K2_W)),
        compiler_params=_params("parallel"))(z, ra, rbm, rbp, gq2, gk2)


GROUP = 4


def _window_mask(n):
    row = lax.broadcasted_iota(jnp.int32, (GROUP * BLK, 2 * BLK), 0) % BLK
    col = lax.broadcasted_iota(jnp.int32, (GROUP * BLK, 2 * BLK), 1)
    return (col > row) & (col <= row + BLK) & ((col >= BLK) | (n > 0))


def _stack_heads(pairs, zero):
    lo = _low_half(pairs[0].shape)
    parts = []
    for v in pairs:
        parts += [jnp.where(lo, v, zero), jnp.where(lo, zero, v)]
    return jnp.concatenate(parts, axis=0)


def _unstack_heads(v4):
    lo = _low_half((BLK, LANES))
    return [jnp.where(lo, v4[2 * i * BLK:(2 * i + 1) * BLK], v4[(2 * i + 1) * BLK:(2 * i + 2) * BLK]) for i in range(2)]


def _group_sinks(sink_ref, kvh):
    slot = lax.broadcasted_iota(jnp.int32, (GROUP * BLK, 1), 0) // BLK
    col = jnp.zeros((GROUP * BLK, 1), F32)
    for i in range(GROUP):
        col = jnp.where(slot == i, sink_ref[0, GROUP * kvh + i], col)
    return col, slot


def _head_probs(qm, kw, valid, sink):
    sc = jnp.where(valid, _dot_nt(qm, kw) * (HEAD ** -0.5), NEG_INF)
    mx = jnp.maximum(jnp.max(sc, axis=-1, keepdims=True), sink)
    ex = jnp.exp(sc - mx)
    den = jnp.sum(ex, axis=-1, keepdims=True) + jnp.exp(sink - mx)
    return ex / den, mx, den


def _cols(start, width=ATTN_W):
    return slice(start, start + width)


def _conv_fwd(z_ref, zp_ref, cw_ref, ext_ref, n):
    u = z_ref[:, _cols(CONV_C0)] * z_ref[:, _cols(CONV_H0)]
    pu = zp_ref[:, _cols(CONV_C0)] * zp_ref[:, _cols(CONV_H0)]
    ext_ref[0:SUBLANES, :] = jnp.where(n > 0, pu, 0.0)
    ext_ref[SUBLANES:SUBLANES + BLK, :] = u
    um1 = ext_ref[SUBLANES - 1:SUBLANES - 1 + BLK, :]
    um2 = ext_ref[SUBLANES - 2:SUBLANES - 2 + BLK, :]
    cv = cw_ref[0:1, :] * um2 + cw_ref[1:2, :] * um1 + cw_ref[2:3, :] * u
    return u, um1, um2, cv


def _prev_rows(n):
    return (jnp.maximum(n * (BLK // SUBLANES) - 1, 0), 0)


def _attn_fwd(qn, k2, v2, z, conv_wp, sinks):
    s = qn.shape[0]
    nb = s // BLK

    def body(sink_ref, q_ref, kc_ref, kp_ref, vc_ref, vp_ref, z_ref, zp_ref, cw_ref, a_ref, mix_ref, mixt_ref,
             ext_ref):
        n = pl.program_id(0)
        valid = _window_mask(n)
        for kvh in range(K2_W // LANES):
            cols = slice(LANES * kvh, LANES * (kvh + 1))
            kw = jnp.concatenate([kp_ref[:, cols], kc_ref[:, cols]], axis=0)
            vw = jnp.concatenate([vp_ref[:, cols], vc_ref[:, cols]], axis=0)
            blocks = [slice(LANES * r, LANES * (r + 1)) for r in (2 * kvh, 2 * kvh + 1)]
            q4 = _stack_heads([q_ref[:, rc] for rc in blocks], jnp.zeros((BLK, LANES), BF16))
            p, _, _ = _head_probs(q4, kw, valid, _group_sinks(sink_ref, kvh)[0])
            for rc, a in zip(blocks, _unstack_heads(_dot(p.astype(BF16), vw))):
                a_ref[:, rc] = a
                g = z_ref[:, _cols(GATE_A0 + rc.start, LANES)]
                mix_ref[:, rc] = (a * (g * _sig(g))).astype(BF16)
        _, _, _, cv = _conv_fwd(z_ref, zp_ref, cw_ref, ext_ref, n)
        gc = z_ref[:, _cols(GATE_C0)]
        mix_ref[:, ATTN_W:D_MODEL] = (z_ref[:, _cols(CONV_B0)] * cv * (gc * _sig(gc))).astype(BF16)
        mixt_ref[...] = mix_ref[...].T

    cur = lambda w: pl.BlockSpec((BLK, w), lambda n: (n, 0))
    prev = lambda w: pl.BlockSpec((BLK, w), lambda n: (jnp.maximum(n - 1, 0), 0))
    return pl.pallas_call(
        body, name="attn_fwd",
        out_shape=(jax.ShapeDtypeStruct((s, ATTN_W), F32), jax.ShapeDtypeStruct((s, D_MODEL), BF16),
                   jax.ShapeDtypeStruct((D_MODEL, s), BF16)),
        grid=(nb,),
        in_specs=[pl.BlockSpec(memory_space=pltpu.SMEM),
                  cur(ATTN_W), cur(K2_W), prev(K2_W), cur(K2_W), prev(K2_W), cur(IN_W),
                  pl.BlockSpec((SUBLANES, IN_W), _prev_rows),
                  pl.BlockSpec((SUBLANES, ATTN_W), lambda n: (0, 0))],
        out_specs=(cur(ATTN_W), cur(D_MODEL), pl.BlockSpec((D_MODEL, BLK), lambda n: (0, n))),
        scratch_shapes=[pltpu.VMEM((BLK + 2 * SUBLANES, ATTN_W), F32)],
        compiler_params=_params("parallel"))(sinks, qn, k2, k2, v2, v2, z, z, conv_wp)


def _fwd_out(mix, w_out, x, g2, tm):
    s = x.shape[0]

    def body(m_ref, w_ref, x_ref, g_ref, x1_ref, h_ref, ht_ref):
        x1 = x_ref[...] + _dot(m_ref[...], w_ref[...])
        x1_ref[...] = x1
        xn, _ = _rms(x1)
        h = (xn * g_ref[...]).astype(BF16)
        h_ref[...] = h
        ht_ref[...] = h.T

    row = pl.BlockSpec((tm, D_MODEL), lambda i: (i, 0))
    return pl.pallas_call(
        body, name="fwd_out",
        out_shape=(jax.ShapeDtypeStruct((s, D_MODEL), F32), jax.ShapeDtypeStruct((s, D_MODEL), BF16),
                   jax.ShapeDtypeStruct((D_MODEL, s), BF16)),
        grid=(s // tm,),
        in_specs=[row, _resident((D_MODEL, D_MODEL)), row, pl.BlockSpec((1, D_MODEL), lambda i: (0, 0))],
        out_specs=(row, row, pl.BlockSpec((D_MODEL, tm), lambda i: (0, i))),
        compiler_params=_params("parallel"))(mix, w_out, x, g2)


def _ple(hn2, w_pg, b_pg, p, w_pp, g3, x1, target, tm):
    s = x1.shape[0]

    def body(h_ref, wg_ref, b_ref, p_ref, wp_ref, g3_ref, x1_ref, t_ref, dy_ref, dgp_ref, dt_ref, pt_ref, acc_ref):
        gate = _sig(_dot(h_ref[...], wg_ref[...]) + b_ref[...])
        pb = p_ref[...].astype(BF16)
        pt_ref[...] = pb.T
        t = _dot(pb, wp_ref[...])
        tn, r3 = _rms(t)
        e = tn * g3_ref[...]
        diff = x1_ref[...] + gate * e - t_ref[...]
        dy = diff * (1.0 / D_MODEL)
        dy_ref[...] = dy
        dgp = dy * e * (gate * (1.0 - gate))
        dgp_ref[...] = dgp.astype(BF16)
        de = dy * gate
        dt_ref[...] = _rms_bwd(de * g3_ref[...], tn, r3).astype(BF16)

        @pl.when(pl.program_id(0) == 0)
        def _():
            acc_ref[...] = jnp.zeros_like(acc_ref)

        acc_ref[0:1, :] += jnp.sum(dgp, axis=0, keepdims=True)
        acc_ref[1:2, :] += jnp.sum(de * tn, axis=0, keepdims=True)
        acc_ref[2:3, :] += jnp.sum(diff * diff, axis=0, keepdims=True) * (0.5 / D_MODEL)

    row = pl.BlockSpec((tm, D_MODEL), lambda i: (i, 0))
    vec = pl.BlockSpec((1, D_MODEL), lambda i: (0, 0))
    return pl.pallas_call(
        body, name="ple",
        out_shape=(jax.ShapeDtypeStruct((s, D_MODEL), F32), jax.ShapeDtypeStruct((s, D_MODEL), BF16),
                   jax.ShapeDtypeStruct((s, D_MODEL), BF16), jax.ShapeDtypeStruct((PLE_DIM, s), BF16),
                   jax.ShapeDtypeStruct((SUBLANES, D_MODEL), F32)),
        grid=(s // tm,),
        in_specs=[row, _resident((D_MODEL, D_MODEL)), vec, pl.BlockSpec((tm, PLE_DIM), lambda i: (i, 0)),
                  _resident((PLE_DIM, D_MODEL)), vec, row, row],
        out_specs=(row, row, row, pl.BlockSpec((PLE_DIM, tm), lambda i: (0, i)),
                   pl.BlockSpec((SUBLANES, D_MODEL), lambda i: (0, 0))),
        compiler_params=_params("arbitrary"))(hn2, w_pg, b_pg, p, w_pp, g3, x1, target)


def _gate_bwd(dgp, w_pg, x1, dy, g2, tm):
    s = x1.shape[0]

    def body(d_ref, w_ref, x1_ref, dy_ref, g_ref, dx_ref, dxb_ref, acc_ref):
        dh = _dot_nt(d_ref[...], w_ref[...])
        xn, r = _rms(x1_ref[...])
        dx1 = dy_ref[...] + _rms_bwd(dh * g_ref[...], xn, r)
        dx_ref[...] = dx1
        dxb_ref[...] = dx1.astype(BF16)

        @pl.when(pl.program_id(0) == 0)
        def _():
            acc_ref[...] = jnp.zeros_like(acc_ref)

        acc_ref[0:1, :] += jnp.sum(dh * xn, axis=0, keepdims=True)

    row = pl.BlockSpec((tm, D_MODEL), lambda i: (i, 0))
    return pl.pallas_call(
        body, name="gate_bwd",
        out_shape=(jax.ShapeDtypeStruct((s, D_MODEL), F32), jax.ShapeDtypeStruct((s, D_MODEL), BF16),
                   jax.ShapeDtypeStruct((SUBLANES, D_MODEL), F32)),
        grid=(s // tm,),
        in_specs=[row, _resident((D_MODEL, D_MODEL)), row, row, pl.BlockSpec((1, D_MODEL), lambda i: (0, 0))],
        out_specs=(row, row, pl.BlockSpec((SUBLANES, D_MODEL), lambda i: (0, 0))),
        compiler_params=_params("arbitrary"))(dgp, w_pg, x1, dy, g2)


def _mm_nt(a, b, tm, name, after):
    m, k = a.shape
    n = b.shape[0]

    def body(a_ref, b_ref, after_ref, o_ref):
        o_ref[...] = _dot_nt(a_ref[...], b_ref[...])

    return pl.pallas_call(
        body, name=name,
        out_shape=jax.ShapeDtypeStruct((m, n), F32),
        grid=(m // tm,),
        in_specs=[pl.BlockSpec((tm, k), lambda i: (i, 0)), _resident((n, k)), ANY],
        out_specs=pl.BlockSpec((tm, n), lambda i: (i, 0)),
        compiler_params=_params("parallel"))(a, b, after)


def _attn_bwd(qn, k2, v2, a, z, dmix, conv_wp, sinks):
    s = qn.shape[0]
    nb = s // BLK

    def body(sink_ref, q_ref, kc_ref, kp_ref, vc_ref, vp_ref, a_ref, z_ref, zp_ref, zn_ref, dm_ref, dmn_ref,
             cw_ref, dq_ref, dkc_ref, dkp_ref, dvc_ref, dvp_ref, dz_ref, dzt_ref, acc_ref, ext_ref):
        n = pl.program_id(0)
        valid = _window_mask(n)
        lane = lax.broadcasted_iota(jnp.int32, (1, ATTN_W), 1)

        @pl.when(n == 0)
        def _():
            acc_ref[...] = jnp.zeros_like(acc_ref)

        dz_ref[:, 0:QKV_W] = jnp.zeros((BLK, QKV_W), BF16)
        dsink = jnp.zeros((1, ATTN_W), F32)
        for kvh in range(K2_W // LANES):
            cols = slice(LANES * kvh, LANES * (kvh + 1))
            kw = jnp.concatenate([kp_ref[:, cols], kc_ref[:, cols]], axis=0)
            vw = jnp.concatenate([vp_ref[:, cols], vc_ref[:, cols]], axis=0)
            blocks = [slice(LANES * r, LANES * (r + 1)) for r in (2 * kvh, 2 * kvh + 1)]
            das, avs = [], []
            for rc in blocks:
                g = z_ref[:, _cols(GATE_A0 + rc.start, LANES)]
                sg = _sig(g)
                dm = dm_ref[:, rc]
                av = a_ref[:, rc]
                das.append(dm * (g * sg))
                avs += [av, av]
                dz_ref[:, _cols(GATE_A0 + rc.start, LANES)] = (dm * av * _dsilu(g, sg)).astype(BF16)
            q4 = _stack_heads([q_ref[:, rc] for rc in blocks], jnp.zeros((BLK, LANES), BF16))
            sink, slot = _group_sinks(sink_ref, kvh)
            p, mx, den = _head_probs(q4, kw, valid, sink)
            do4 = _stack_heads(das, 0.0)
            delta = jnp.sum(do4 * jnp.concatenate(avs, axis=0), axis=-1, keepdims=True)
            dob = do4.astype(BF16)
            ds = p * (_dot_nt(dob, vw) - delta) * (HEAD ** -0.5)
            for rc, dq in zip(blocks, _unstack_heads(_dot(ds.astype(BF16), kw))):
                dq_ref[:, rc] = dq
            dk2 = _dot(ds.T.astype(BF16), q4)
            dv2 = _dot(p.T.astype(BF16), dob)
            dkp_ref[:, cols] = dk2[0:BLK]
            dkc_ref[:, cols] = dk2[BLK:2 * BLK]
            dvp_ref[:, cols] = dv2[0:BLK]
            dvc_ref[:, cols] = dv2[BLK:2 * BLK]
            dsk = jnp.exp(sink - mx) / den * delta
            for i in range(GROUP):
                dsink = dsink - jnp.where(lane == GROUP * kvh + i,
                                          jnp.sum(jnp.where(slot == i, dsk, 0.0), axis=0, keepdims=True), 0.0)
        acc_ref[0:1, :] += dsink

        u, um1, um2, cv = _conv_fwd(z_ref, zp_ref, cw_ref, ext_ref, n)
        cb = z_ref[:, _cols(CONV_B0)]
        gc = z_ref[:, _cols(GATE_C0)]
        sgc = _sig(gc)
        dmc = dm_ref[:, ATTN_W:D_MODEL]
        t = dmc * (gc * sgc)
        dcv = t * cb
        dz_ref[:, _cols(CONV_B0)] = (t * cv).astype(BF16)
        dz_ref[:, _cols(GATE_C0)] = (dmc * cb * cv * _dsilu(gc, sgc)).astype(BF16)
        gcn = zn_ref[:, _cols(GATE_C0)]
        dcvn = dmn_ref[:, ATTN_W:D_MODEL] * (gcn * _sig(gcn)) * zn_ref[:, _cols(CONV_B0)]
        ext_ref[0:BLK, :] = dcv
        ext_ref[BLK:BLK + SUBLANES, :] = jnp.where(n < nb - 1, dcvn, 0.0)
        du = (cw_ref[2:3, :] * dcv + cw_ref[1:2, :] * ext_ref[1:1 + BLK, :]
              + cw_ref[0:1, :] * ext_ref[2:2 + BLK, :])
        dz_ref[:, _cols(CONV_C0)] = (du * z_ref[:, _cols(CONV_H0)]).astype(BF16)
        dz_ref[:, _cols(CONV_H0)] = (du * z_ref[:, _cols(CONV_C0)]).astype(BF16)
        acc_ref[1:2, :] += jnp.sum(dcv * um2, axis=0, keepdims=True)
        acc_ref[2:3, :] += jnp.sum(dcv * um1, axis=0, keepdims=True)
        acc_ref[3:4, :] += jnp.sum(dcv * u, axis=0, keepdims=True)
        dzt_ref[...] = dz_ref[...].T

    cur = lambda w: pl.BlockSpec((BLK, w), lambda n: (n, 0))
    prev = lambda w: pl.BlockSpec((BLK, w), lambda n: (jnp.maximum(n - 1, 0), 0))
    nxt = lambda w: pl.BlockSpec(
        (SUBLANES, w), lambda n: (jnp.minimum((n + 1) * (BLK // SUBLANES), nb * (BLK // SUBLANES) - 1), 0))
    f32 = lambda w: jax.ShapeDtypeStruct((s, w), F32)
    return pl.pallas_call(
        body, name="attn_bwd",
        out_shape=(f32(ATTN_W), f32(K2_W), f32(K2_W), f32(K2_W), f32(K2_W),
                   jax.ShapeDtypeStruct((s, IN_W), BF16), jax.ShapeDtypeStruct((IN_W, s), BF16),
                   jax.ShapeDtypeStruct((SUBLANES, ATTN_W), F32)),
        grid=(nb,),
        in_specs=[pl.BlockSpec(memory_space=pltpu.SMEM),
                  cur(ATTN_W), cur(K2_W), prev(K2_W), cur(K2_W), prev(K2_W), cur(ATTN_W), cur(IN_W),
                  pl.BlockSpec((SUBLANES, IN_W), _prev_rows), nxt(IN_W), cur(D_MODEL), nxt(D_MODEL),
                  pl.BlockSpec((SUBLANES, ATTN_W), lambda n: (0, 0))],
        out_specs=(cur(ATTN_W), cur(K2_W), cur(K2_W), cur(K2_W), cur(K2_W), cur(IN_W),
                   pl.BlockSpec((IN_W, BLK), lambda n: (0, n)), pl.BlockSpec((SUBLANES, ATTN_W), lambda n: (0, 0))),
        scratch_shapes=[pltpu.VMEM((BLK + 2 * SUBLANES, ATTN_W), F32)],
        compiler_params=_params("arbitrary"))(sinks, qn, k2, k2, v2, v2, a, z, z, z, dmix, dmix, conv_wp)


def _qkv_bwd(z, dz, dzt, dq, dkc, dkp, dvc, dvp, ra, rbm, rbp, gq2, gk2):
    s = z.shape[0]
    nb = s // BLK

    def body(z_ref, dz_in, dzt_in, dq_ref, dkc_ref, dkp_ref, dvc_ref, dvp_ref, a_ref, bm_ref, bp_ref, gq_ref, gk_ref,
             dz_ref, dzt_ref, acc_ref):
        n = pl.program_id(0)
        a, bm, bp = a_ref[...], bm_ref[...], bp_ref[...]
        lo = _low_half((BLK, LANES))
        last = n == nb - 1

        @pl.when(n == 0)
        def _():
            acc_ref[...] = jnp.zeros_like(acc_ref)

        def norm_bwd(x, dy, gain):
            rr = lax.rsqrt(_half_sums(x * x) * (1.0 / HEAD) + EPS)
            xh = x * rr
            dxg = _rope_t(dy, a, bm, bp)
            dxh = dxg * gain
            dx = rr * (dxh - xh * (_half_sums(dxh * xh) * (1.0 / HEAD)))
            return dx, jnp.sum(dxg * xh, axis=0, keepdims=True)

        def folded(cur_ref, prev_ref, m):
            parts = []
            for h in (2 * m, 2 * m + 1):
                v = cur_ref[:, LANES * h:LANES * (h + 1)] + jnp.where(
                    last, 0.0, prev_ref[:, LANES * h:LANES * (h + 1)])
                parts.append(v + pltpu.roll(v, HEAD, 1))
            return jnp.where(lo, parts[0], parts[1])

        gq_acc = jnp.zeros((1, LANES), F32)
        for r in range(ATTN_W // LANES):
            rc = slice(LANES * r, LANES * (r + 1))
            dx, gg = norm_bwd(z_ref[:, rc], dq_ref[:, rc], gq_ref[...])
            dz_ref[:, rc] = dx.astype(BF16)
            gq_acc = gq_acc + gg
        acc_ref[0:1, :] += gq_acc
        gk_acc = jnp.zeros((1, LANES), F32)
        for m in range(KV_W // LANES):
            kc = slice(ATTN_W + LANES * m, ATTN_W + LANES * (m + 1))
            dx, gg = norm_bwd(z_ref[:, kc], folded(dkc_ref, dkp_ref, m), gk_ref[...])
            dz_ref[:, kc] = dx.astype(BF16)
            gk_acc = gk_acc + gg
            vc = slice(ATTN_W + KV_W + LANES * m, ATTN_W + KV_W + LANES * (m + 1))
            dz_ref[:, vc] = folded(dvc_ref, dvp_ref, m).astype(BF16)
        acc_ref[1:2, :] += gk_acc
        dzt_ref[...] = dz_ref[...].T

    cur = lambda w: pl.BlockSpec((BLK, w), lambda n: (n, 0))
    nxt = lambda w: pl.BlockSpec((BLK, w), lambda n: (jnp.minimum(n + 1, nb - 1), 0))
    one = pl.BlockSpec((1, LANES), lambda n: (0, 0))
    return pl.pallas_call(
        body, name="qkv_bwd",
        out_shape=(jax.ShapeDtypeStruct(dz.shape, dz.dtype), jax.ShapeDtypeStruct(dzt.shape, dzt.dtype),
                   jax.ShapeDtypeStruct((SUBLANES, LANES), F32)),
        grid=(nb,),
        in_specs=[cur(PAIR_W), ANY, ANY, cur(ATTN_W), cur(K2_W), nxt(K2_W), cur(K2_W), nxt(K2_W),
                  cur(LANES), cur(LANES), cur(LANES), one, one],
        out_specs=(cur(QKV_W), pl.BlockSpec((QKV_W, BLK), lambda n: (0, n)),
                   pl.BlockSpec((SUBLANES, LANES), lambda n: (0, 0))),
        input_output_aliases={1: 0, 2: 1},
        compiler_params=_params("arbitrary"))(z, dz, dzt, dq, dkc, dkp, dvc, dvp, ra, rbm, rbp, gq2, gk2)


def _in_bwd(dz, w_pairs, x, dx1, g1, tm):
    s = x.shape[0]

    def body(d_ref, w_ref, x_hbm, dx1_hbm, g_ref, gx_ref, acc_ref, x_buf, dx1_buf, sems):
        i, k = pl.program_id(0), pl.program_id(1)
        rows = pl.ds(pl.multiple_of(i * tm, tm), tm)
        fetch = [pltpu.make_async_copy(x_hbm.at[rows], x_buf, sems.at[0]),
                 pltpu.make_async_copy(dx1_hbm.at[rows], dx1_buf, sems.at[1])]
        sub = min(SUB_ROWS, tm)
        blocks = [slice(r, r + sub) for r in range(0, tm, sub)]

        @pl.when(k == 0)
        def _():
            for cp in fetch:
                cp.start()
            gx_ref[...] = _dot(d_ref[...], w_ref[0])

        @pl.when(k > 0)
        def _():
            gx_ref[...] += _dot(d_ref[...], w_ref[0])

        @pl.when((i == 0) & (k == 0))
        def _():
            acc_ref[...] = jnp.zeros_like(acc_ref)

        @pl.when(k == N_PAIRS - 1)
        def _():
            for cp in fetch:
                cp.wait()
            for rb in blocks:
                dh = gx_ref[rb, :]
                xn, r = _rms(x_buf[rb, :])
                gx_ref[rb, :] = dx1_buf[rb, :] + _rms_bwd(dh * g_ref[...], xn, r)
                acc_ref[0:1, :] += jnp.sum(dh * xn, axis=0, keepdims=True)

    return pl.pallas_call(
        body, name="in_bwd",
        out_shape=(jax.ShapeDtypeStruct((s, D_MODEL), F32), jax.ShapeDtypeStruct((SUBLANES, D_MODEL), F32)),
        grid=(s // tm, N_PAIRS),
        in_specs=[pl.BlockSpec((tm, PAIR_W), lambda i, k: (i, k)),
                  pl.BlockSpec((1, PAIR_W, D_MODEL), lambda i, k: (k, 0, 0)),
                  ANY, ANY, pl.BlockSpec((1, D_MODEL), lambda i, k: (0, 0))],
        out_specs=(pl.BlockSpec((tm, D_MODEL), lambda i, k: (i, 0)),
                   pl.BlockSpec((SUBLANES, D_MODEL), lambda i, k: (0, 0))),
        scratch_shapes=[pltpu.VMEM((tm, D_MODEL), F32), pltpu.VMEM((tm, D_MODEL), F32),
                        pltpu.SemaphoreType.DMA((2,))],
        compiler_params=_params("arbitrary", "arbitrary"))(dz, w_pairs, x, dx1, g1)


def _mm_grad(at, bs, tn, name):
    m, kdim = at.shape
    nblk = [b.shape[1] // tn for b in bs]
    starts = [sum(nblk[:t]) for t in range(len(bs))]

    def body(a_ref, *refs):
        b_refs, o_ref = refs[:len(bs)], refs[len(bs)]
        j = pl.program_id(0)
        for t, b_ref in enumerate(b_refs):
            @pl.when((j >= starts[t]) & (j < starts[t] + nblk[t]))
            def _():
                o_ref[...] = _dot(a_ref[...], b_ref[...]).astype(BF16)

    def b_spec(t):
        return pl.BlockSpec((kdim, tn), lambda j: (0, jnp.clip(j - starts[t], 0, nblk[t] - 1)))

    return pl.pallas_call(
        body, name=name,
        out_shape=jax.ShapeDtypeStruct((m, sum(nblk) * tn), BF16),
        grid=(sum(nblk),),
        in_specs=[_resident((m, kdim))] + [b_spec(t) for t in range(len(bs))],
        out_specs=pl.BlockSpec((m, tn), lambda j: (0, j)),
        compiler_params=_params("parallel"))(at, *bs)


def _grad_w_in(dzt, h):
    kdim = h.shape[0]

    def body(d_ref, h_ref, o_ref):
        o_ref[0] = _dot(d_ref[...], h_ref[...]).astype(BF16)

    return pl.pallas_call(
        body, name="grad_w_in",
        out_shape=jax.ShapeDtypeStruct((N_DEV, SHARD_IN, D_MODEL), BF16),
        grid=(N_DEV,),
        in_specs=[pl.BlockSpec((SHARD_IN, kdim), lambda j: (j, 0)), _resident((kdim, D_MODEL))],
        out_specs=pl.BlockSpec((1, SHARD_IN, D_MODEL), lambda j: (j, 0, 0)),
        compiler_params=_params("parallel"))(dzt, h)


def _place():
    return lax.axis_index("x"), lax.axis_index("y"), lax.axis_index("c")


def _pair_sum(g, r, place, tr, name):
    _, _, rows, cols = g.shape

    def body(place_ref, g_ref, r_ref, pb_ref, own_ref):
        tot = g_ref[0, 0].astype(F32) + r_ref[0].astype(F32)
        pb_ref[0] = tot.astype(BF16)

        @pl.when(pl.program_id(1) == place_ref[1])
        def _():
            own_ref[...] = tot

    grid_spec = pltpu.PrefetchScalarGridSpec(
        num_scalar_prefetch=1, grid=(rows // tr, 4),
        in_specs=[pl.BlockSpec((1, 1, tr, cols), lambda i, q, place_ref: (q, place_ref[0], i, 0)),
                  pl.BlockSpec((1, tr, cols), lambda i, q, place_ref: (q, i, 0))],
        out_specs=(pl.BlockSpec((1, tr, cols), lambda i, q, place_ref: (q, i, 0)),
                   pl.BlockSpec((tr, cols), lambda i, q, place_ref: (i, 0))))
    return pl.pallas_call(
        body, name=name, grid_spec=grid_spec,
        out_shape=(jax.ShapeDtypeStruct((4, rows, cols), BF16), jax.ShapeDtypeStruct((rows, cols), F32)),
        compiler_params=_params("arbitrary", "arbitrary"))(place, g, r)


HBM = pl.BlockSpec(memory_space=pltpu.HBM)
SEM = pl.BlockSpec(memory_space=pltpu.SEMAPHORE)
SIDE_EFFECT = pltpu.CompilerParams(has_side_effects=pltpu.SideEffectType.DATAFLOW_SIDE_EFFECTING)
TOKEN = jax.ShapeDtypeStruct((SUBLANES, LANES), F32)


def _hbm(a):
    return pltpu.with_memory_space_constraint(a, pltpu.HBM)


def _hbm_like(arrays):
    return tuple(pltpu.HBM(a.shape, a.dtype) for a in arrays)


def _block_of(px, py, pc):
    return 4 * px + 2 * py + pc


def _gather_start(shards, after):
    na = len(shards)
    lands = [_hbm(lax.empty((N_DEV,) + a.shape, a.dtype)) for a in shards]

    def body(*refs):
        ins, land = refs[:na], refs[na:2 * na]
        send_sems, recv_ici, recv_d2d = refs[2 * na + 1:2 * na + 4]
        token = refs[-1]
        x, y, c = _place()
        for k, peer in enumerate([(x, y, 1 - c), (1 - x, y, c), (x, 1 - y, c), (1 - x, 1 - y, c)]):
            for t in range(na):
                pltpu.make_async_remote_copy(
                    src_ref=ins[t], dst_ref=land[t].at[_block_of(x, y, c)], send_sem=send_sems.at[4 * t + k],
                    recv_sem=recv_d2d.at[4 * t] if k == 0 else recv_ici.at[3 * t + k - 1],
                    device_id=peer, device_id_type=MESH).start()
        token[...] = jnp.zeros_like(token)

    out = pl.pallas_call(
        body, name="gather_start",
        out_shape=(pltpu.SemaphoreType.DMA((4 * na,)), pltpu.SemaphoreType.DMA((3 * na,)),
                   pltpu.SemaphoreType.DMA((4 * na,)), *_hbm_like(lands), TOKEN),
        in_specs=[ANY] * na + [HBM] * na + [ANY],
        out_specs=(SEM, SEM, SEM, *[HBM] * na, pl.BlockSpec(memory_space=pltpu.VMEM)),
        input_output_aliases={na + i: 3 + i for i in range(na)},
        compiler_params=SIDE_EFFECT)(*shards, *lands, after)
    send_sems, recv_ici, recv_d2d = out[:3]
    state = dict(send=send_sems, ici=recv_ici, d2d=recv_d2d, shards=list(shards), lands=out[3:3 + na])
    return state, out[-1]


def _gather_forward(state, after):
    lands = state["lands"]
    na = len(lands)

    def body(*refs):
        land = refs[:na]
        recv_ici, recv_d2d = refs[na], refs[na + 1]
        fwd_sems, token = refs[-2], refs[-1]
        x, y, c = _place()
        for j, chip in enumerate([(1 - x, y), (x, 1 - y), (1 - x, 1 - y)]):
            for t in range(na):
                blk = land[t].at[_block_of(*chip, c)]
                pltpu.make_async_remote_copy(
                    src_ref=blk, dst_ref=blk, send_sem=fwd_sems.at[3 * t + j], recv_sem=recv_ici.at[3 * t + j],
                    device_id=(x, y, c), device_id_type=MESH).wait_recv()
                pltpu.make_async_remote_copy(
                    src_ref=blk, dst_ref=blk, send_sem=fwd_sems.at[3 * t + j], recv_sem=recv_d2d.at[4 * t + 1 + j],
                    device_id=(x, y, 1 - c), device_id_type=MESH).start()
        token[...] = jnp.zeros_like(token)

    out = pl.pallas_call(
        body, name="gather_forward",
        out_shape=(*_hbm_like(lands), pltpu.SemaphoreType.DMA((3 * na,)), TOKEN),
        in_specs=[HBM] * na + [SEM, SEM, ANY],
        out_specs=(*[HBM] * na, SEM, pl.BlockSpec(memory_space=pltpu.VMEM)),
        input_output_aliases={i: i for i in range(na)},
        compiler_params=SIDE_EFFECT)(*lands, state["ici"], state["d2d"], after)
    return dict(state, lands=out[:na], fwd=out[na]), out[-1]


def _gather_wait(state, after):
    shards, lands = state["shards"], state["lands"]
    na = len(lands)

    def body(*refs):
        ins, land = refs[:na], refs[na:2 * na]
        send_sems, fwd_sems, recv_d2d = refs[2 * na:2 * na + 3]
        x, y, c = _place()
        chips = [(1 - x, y), (x, 1 - y), (1 - x, 1 - y)]
        for t in range(na):
            mine = land[t].at[_block_of(x, y, c)]
            for k in range(4):
                pltpu.make_async_remote_copy(
                    src_ref=ins[t], dst_ref=mine, send_sem=send_sems.at[4 * t + k], recv_sem=recv_d2d.at[4 * t],
                    device_id=(x, y, c), device_id_type=MESH).wait_send()
            for j, chip in enumerate(chips):
                blk = land[t].at[_block_of(*chip, c)]
                pltpu.make_async_remote_copy(
                    src_ref=blk, dst_ref=blk, send_sem=fwd_sems.at[3 * t + j], recv_sem=recv_d2d.at[4 * t + 1 + j],
                    device_id=(x, y, c), device_id_type=MESH).wait_send()
            for k, blk_id in enumerate([_block_of(x, y, 1 - c)] + [_block_of(*chip, 1 - c) for chip in chips]):
                blk = land[t].at[blk_id]
                pltpu.make_async_remote_copy(
                    src_ref=blk, dst_ref=blk, send_sem=send_sems.at[4 * t], recv_sem=recv_d2d.at[4 * t + k],
                    device_id=(x, y, c), device_id_type=MESH).wait_recv()

    out = pl.pallas_call(
        body, name="gather_wait",
        out_shape=_hbm_like(lands),
        in_specs=[ANY] * na + [HBM] * na + [SEM, SEM, SEM, ANY],
        out_specs=tuple([HBM] * na),
        input_output_aliases={na + i: i for i in range(na)},
        compiler_params=SIDE_EFFECT)(*shards, *lands, state["send"], state["fwd"], state["d2d"], after)
    return out


def _gather_from_sibling(state, after):
    lands = state["lands"]
    na = len(lands)

    def body(*refs):
        land, recv_d2d = refs[:na], refs[na]
        x, y, c = _place()
        for t in range(na):
            blk = land[t].at[_block_of(x, y, 1 - c)]
            pltpu.make_async_remote_copy(src_ref=blk, dst_ref=blk, send_sem=recv_d2d.at[4 * t],
                                         recv_sem=recv_d2d.at[4 * t], device_id=(x, y, c),
                                         device_id_type=MESH).wait_recv()

    out = pl.pallas_call(
        body, name="gather_from_sibling", out_shape=_hbm_like(lands),
        in_specs=[HBM] * na + [SEM, ANY], out_specs=tuple([HBM] * na),
        input_output_aliases={i: i for i in range(na)},
        compiler_params=SIDE_EFFECT)(*lands, state["d2d"], after)
    return dict(state, lands=list(out))


def _gather_from_chip(state, j, after, last):
    shards, lands = state["shards"], state["lands"]
    na = len(lands)

    def chip_blocks(land_ref):
        x, y, c = _place()
        chip = [(1 - x, y), (x, 1 - y), (1 - x, 1 - y)][j]
        return (x, y, c), land_ref.at[_block_of(*chip, c)], land_ref.at[_block_of(*chip, 1 - c)]

    def forward(*refs):
        land, recv_ici, recv_d2d, fwd_sems = refs[:na], refs[na], refs[na + 1], refs[-1]
        for t in range(na):
            (x, y, c), mine, _ = chip_blocks(land[t])
            pltpu.make_async_remote_copy(src_ref=mine, dst_ref=mine, send_sem=fwd_sems.at[t],
                                         recv_sem=recv_ici.at[3 * t + j], device_id=(x, y, c),
                                         device_id_type=MESH).wait_recv()
            pltpu.make_async_remote_copy(src_ref=mine, dst_ref=mine, send_sem=fwd_sems.at[t],
                                         recv_sem=recv_d2d.at[4 * t + 1 + j], device_id=(x, y, 1 - c),
                                         device_id_type=MESH).start()

    out = pl.pallas_call(
        forward, name="gather_pass_chip_" + str(j),
        out_shape=(*_hbm_like(lands), pltpu.SemaphoreType.DMA((na,))),
        in_specs=[HBM] * na + [SEM, SEM, ANY], out_specs=(*[HBM] * na, SEM),
        input_output_aliases={i: i for i in range(na)},
        compiler_params=SIDE_EFFECT)(*lands, state["ici"], state["d2d"], after)
    lands, fwd_sems = out[:na], out[na]

    def arrive(*refs):
        land, fwd_sems, recv_d2d = refs[:na], refs[na], refs[na + 1]
        shard, send_sems = refs[na + 2:2 * na + 2], refs[2 * na + 2]
        for t in range(na):
            (x, y, c), mine, theirs = chip_blocks(land[t])
            pltpu.make_async_remote_copy(src_ref=theirs, dst_ref=theirs, send_sem=fwd_sems.at[t],
                                         recv_sem=recv_d2d.at[4 * t + 1 + j], device_id=(x, y, c),
                                         device_id_type=MESH).wait_recv()
            pltpu.make_async_remote_copy(src_ref=mine, dst_ref=mine, send_sem=fwd_sems.at[t],
                                         recv_sem=recv_d2d.at[4 * t + 1 + j], device_id=(x, y, c),
                                         device_id_type=MESH).wait_send()
            for k in range(4 if last else 0):
                pltpu.make_async_remote_copy(
                    src_ref=shard[t], dst_ref=land[t].at[_block_of(x, y, c)], send_sem=send_sems.at[4 * t + k],
                    recv_sem=recv_d2d.at[4 * t], device_id=(x, y, c), device_id_type=MESH).wait_send()

    out = pl.pallas_call(
        arrive, name="gather_take_chip_" + str(j), out_shape=_hbm_like(lands),
        in_specs=[HBM] * na + [SEM, SEM] + [ANY] * na + [SEM], out_specs=tuple([HBM] * na),
        input_output_aliases={i: i for i in range(na)},
        compiler_params=SIDE_EFFECT)(*lands, fwd_sems, state["d2d"], *shards, state["send"])
    return dict(state, lands=list(out))


def _to_all(srcs, lands, send_sems, recv_sems):
    x, y, c = _place()
    copies = []
    for k in range(1, N_DEV):
        peer = (x ^ (k >> 2), y ^ ((k >> 1) & 1), c ^ (k & 1))
        copies += [pltpu.make_async_remote_copy(
            src_ref=srcs[t], dst_ref=lands[t].at[_block_of(x, y, c)], send_sem=send_sems.at[7 * t + k - 1],
            recv_sem=recv_sems.at[7 * t + k - 1], device_id=peer, device_id_type=MESH) for t in range(len(srcs))]
    return copies


def _sum_slabs(own, others, me):
    def body(me_ref, own_ref, oth_ref, out_ref):
        total = jnp.where(me_ref[0, 0] == 0, own_ref[...], oth_ref[0])
        for d in range(1, N_DEV):
            total = total + jnp.where(me_ref[0, 0] == d, own_ref[...], oth_ref[d])
        out_ref[...] = total

    vmem = pl.BlockSpec(memory_space=pltpu.VMEM)
    return pl.pallas_call(body, name="sum_slabs", out_shape=jax.ShapeDtypeStruct(own.shape, F32),
                          in_specs=[pl.BlockSpec(memory_space=pltpu.SMEM), vmem, vmem], out_specs=vmem)(me, own, others)


def _to_sibling(srcs, lands, send_sems, recv_sems):
    x, y, c = _place()
    return [pltpu.make_async_remote_copy(
        src_ref=srcs[t].at[:, 1 - c], dst_ref=lands[t], send_sem=send_sems.at[t], recv_sem=recv_sems.at[t],
        device_id=(x, y, 1 - c), device_id_type=MESH) for t in range(len(srcs))]


def _to_chips(srcs, lands, send_sems, recv_sems):
    x, y, c = _place()
    copies = []
    for k in (1, 2, 3):
        px, py = x ^ (k >> 1), y ^ (k & 1)
        copies += [pltpu.make_async_remote_copy(
            src_ref=srcs[t].at[2 * px + py], dst_ref=lands[t].at[k - 1], send_sem=send_sems.at[3 * t + k - 1],
            recv_sem=recv_sems.at[3 * t + k - 1], device_id=(px, py, c), device_id_type=MESH) for t in range(len(srcs))]
    return copies


def _exchange_start(name, srcs, land_shapes, copies, per_array, after):
    na = len(srcs)
    lands = [_hbm(lax.empty(shp, a.dtype)) for shp, a in zip(land_shapes, srcs)]

    def body(*refs):
        token = refs[-1]
        for cp in copies(refs[:na], refs[na:2 * na], refs[2 * na + 1], refs[2 * na + 2]):
            cp.start()
        token[...] = jnp.zeros_like(token)

    out = pl.pallas_call(
        body, name=name,
        out_shape=(pltpu.SemaphoreType.DMA((na * per_array,)), pltpu.SemaphoreType.DMA((na * per_array,)),
                   *_hbm_like(lands), TOKEN),
        in_specs=[ANY] * na + [HBM] * na + [ANY],
        out_specs=(SEM, SEM, *[HBM] * na, pl.BlockSpec(memory_space=pltpu.VMEM)),
        input_output_aliases={na + i: 2 + i for i in range(na)},
        compiler_params=SIDE_EFFECT)(*srcs, *lands, after)
    return dict(send=out[0], recv=out[1], srcs=list(srcs), lands=out[2:2 + na]), out[-1]


def _exchange_wait(name, state, copies, after):
    srcs, lands = state["srcs"], state["lands"]
    na = len(srcs)

    def body(*refs):
        for cp in copies(refs[:na], refs[na:2 * na], refs[2 * na], refs[2 * na + 1]):
            cp.wait_send()
            cp.wait_recv()

    out = pl.pallas_call(
        body, name=name,
        out_shape=_hbm_like(lands),
        in_specs=[ANY] * na + [HBM] * na + [SEM, SEM, ANY],
        out_specs=tuple([HBM] * na),
        input_output_aliases={na + i: i for i in range(na)},
        compiler_params=SIDE_EFFECT)(*srcs, *lands, state["send"], state["recv"], after)
    return out


def _adamw_math(w, g, m, v):
    m = ADAM_B1 * m + (1.0 - ADAM_B1) * g
    v = ADAM_B2 * v + (1.0 - ADAM_B2) * (g * g)
    m_hat = m / (1.0 - ADAM_B1 ** ADAM_STEP)
    v_hat = v / (1.0 - ADAM_B2 ** ADAM_STEP)
    return -ADAM_LR * (m_hat / (jnp.sqrt(v_hat) + ADAM_EPS) + ADAM_WD * w), m, v


def _adamw(own, others, w, m, v, tr, name):
    rows, cols = w.shape
    blk = pl.BlockSpec((tr, cols), lambda i: (i, 0))

    def body(own_ref, oth_ref, w_ref, m_ref, v_ref, g_ref, d_ref, nm_ref, nv_ref):
        g = own_ref[...]
        for k in range(3):
            g = g + oth_ref[k].astype(F32)
        g_ref[...] = g
        d_ref[...], nm_ref[...], nv_ref[...] = _adamw_math(w_ref[...], g, m_ref[...], v_ref[...])

    out = jax.ShapeDtypeStruct((rows, cols), F32)
    return pl.pallas_call(
        body, name=name, out_shape=(out, out, out, out), grid=(rows // tr,),
        in_specs=[blk, pl.BlockSpec((3, tr, cols), lambda i: (0, i, 0)), blk, blk, blk],
        out_specs=(blk, blk, blk, blk),
        compiler_params=_params("parallel"))(own, others, w, m, v)


def _adamw_slab(w, g, m, v):
    def body(w_ref, g_ref, m_ref, v_ref, d_ref, nm_ref, nv_ref):
        d_ref[...], nm_ref[...], nv_ref[...] = _adamw_math(w_ref[...], g_ref[...], m_ref[...], v_ref[...])

    out = jax.ShapeDtypeStruct(w.shape, F32)
    vmem = pl.BlockSpec(memory_space=pltpu.VMEM)
    return pl.pallas_call(body, name="adamw_small", out_shape=(out, out, out),
                          in_specs=[vmem] * 4, out_specs=(vmem, vmem, vmem))(w, g, m, v)


def _row(v, width=D_MODEL):
    v = v.reshape(1, -1)
    return jnp.pad(v, ((0, 0), (0, width - v.shape[1])))


def _tables(s, gq, gk, conv_w):
    gq2 = jnp.tile(gq.reshape(1, HEAD), (1, 2))
    gk2 = jnp.tile(gk.reshape(1, HEAD), (1, 2))
    conv_wp = jnp.pad(conv_w, ((0, SUBLANES - conv_w.shape[0]), (0, 0)))
    return _rope_tables(s), gq2, gk2, conv_wp


def _pair_id(q):
    return jnp.full((1,), q, jnp.int32)


def _forward_in(x, g1, shards):
    s = x.shape[0]
    h = _prenorm(x, g1, min(512, s))
    z, w_pairs = lax.empty((s, IN_W), F32), lax.empty((N_PAIRS, PAIR_W, D_MODEL), BF16)
    for q in range(N_PAIRS):
        z, w_pairs = _fwd_in_pair(h, shards, z, w_pairs, _pair_id(q), min(512, s), "fwd_in_" + str(q))
    return h, z, w_pairs


def _forward_attn(z, rope, gq2, gk2, conv_wp, sinks):
    s = z.shape[0]
    qn, k2, v2 = _qk_prep(z, *rope, gq2, gk2, min(256, s))
    a, mix, mixt = _attn_fwd(qn, k2, v2, z, conv_wp, sinks)
    return qn, k2, v2, a, mix, mixt


def _forward_out(x, p, target, mix, mixt, w_out, g2, w_pg, b_pg, w_pp, g3):
    s = x.shape[0]
    tm = min(512, s)
    x1, hn2, hn2t = _fwd_out(mix, w_out, x, g2, tm)
    dy, dgp, dt, pt, acc_ple = _ple(hn2, w_pg, b_pg, p, w_pp, g3, x1, target, min(256, s))
    dx1, dx1b, acc_g2 = _gate_bwd(dgp, w_pg, x1, dy, g2, tm)
    gw_out = _mm_grad(mixt, [dx1b], 512, "grad_w_out")
    gw_pg = _mm_grad(hn2t, [dgp], 512, "grad_w_ple_gate")
    gw_pp = _mm_grad(pt, [dt], 512, "grad_w_ple_proj")
    return dx1, dx1b, (gw_out, gw_pg, gw_pp), acc_ple, acc_g2


def _backward_attn(dmix, h, z, qn, k2, v2, a, rope, gq2, gk2, conv_wp, sinks):
    dq, dkc, dkp, dvc, dvp, dz, dzt, acc_attn = _attn_bwd(qn, k2, v2, a, z, dmix, conv_wp, sinks)
    dz, dzt, acc_qk = _qkv_bwd(z, dz, dzt, dq, dkc, dkp, dvc, dvp, *rope, gq2, gk2)
    return dz, _grad_w_in(dzt, h), acc_attn, acc_qk


def _small_rows(acc_g1, acc_g2, acc_ple, acc_qk, acc_attn):
    fold = lambda v: _row((v[:HEAD] + v[HEAD:]))
    return [acc_g1[0:1], acc_g2[0:1], acc_ple[0:1], acc_ple[1:2], fold(acc_qk[0]), fold(acc_qk[1]),
            _row(acc_attn[0, :N_Q_HEADS]), _row(acc_attn[1]), _row(acc_attn[2]), _row(acc_attn[3]), acc_ple[2:3]]


def _local_step(x, p, target, g1, shards, gq, gk, sinks, conv_w, w_out, g2, w_pg, b_pg, w_pp, g3):
    rope, gq2, gk2, conv_wp = _tables(x.shape[0], gq, gk, conv_w)
    h, z, w_pairs = _forward_in(x, g1, shards)
    qn, k2, v2, a, mix, mixt = _forward_attn(z, rope, gq2, gk2, conv_wp, sinks)
    dx1, dx1b, (gw_out, gw_pg, gw_pp), acc_ple, acc_g2 = _forward_out(
        x, p, target, mix, mixt, w_out, g2, w_pg, b_pg, w_pp, g3)
    dmix = _mm_nt(dx1b, w_out, min(512, x.shape[0]), "out_bwd", dx1b)
    dz, gw_in, acc_attn, acc_qk = _backward_attn(dmix, h, z, qn, k2, v2, a, rope, gq2, gk2, conv_wp, sinks)
    grad_x, acc_g1 = _in_bwd(dz, w_pairs, x, dx1, g1, min(512, x.shape[0]))
    return grad_x, (gw_in, gw_out, gw_pg, gw_pp), _small_rows(acc_g1, acc_g2, acc_ple, acc_qk, acc_attn)


ROW_CONV, ROW_LOSS = 7, 10


def _slab(rows):
    rows = list(rows)
    return jnp.concatenate(rows + [jnp.zeros((SLAB_ROWS - len(rows), D_MODEL), F32)], axis=0)


def _by_owner(g):
    return g.reshape((4, 2) + g.shape[1:])


def kernel(x, p, norm_gain, w_in, q_norm_gain, k_norm_gain, attn_sinks, conv_w, w_out, ple_gate_norm_gain, w_ple_gate, b_ple_gate, w_ple_proj, ple_norm_gain, loss_target, m_norm_gain, m_w_in, m_q_norm_gain, m_k_norm_gain, m_attn_sinks, m_conv_w, m_w_out, m_ple_gate_norm_gain, m_w_ple_gate, m_b_ple_gate, m_w_ple_proj, m_ple_norm_gain, v_norm_gain, v_w_in, v_q_norm_gain, v_k_norm_gain, v_attn_sinks, v_conv_w, v_w_out, v_ple_gate_norm_gain, v_w_ple_gate, v_b_ple_gate, v_w_ple_proj, v_ple_norm_gain):
    me = 4 * lax.axis_index("x") + 2 * lax.axis_index("y") + lax.axis_index("c")
    place = jnp.stack([lax.axis_index("c"), 2 * lax.axis_index("x") + lax.axis_index("y")]).astype(jnp.int32)
    conv_cols = conv_w.shape[2]
    xs, ps, target = x[0], p[0, 0], loss_target[0]
    zero = lambda token: token[0:1, 0:1]

    shard_in = w_in[0].T.astype(BF16)
    own_late = [w_out[0].astype(BF16), w_ple_gate[0].astype(BF16), w_ple_proj[0].astype(BF16)]
    with_own = lambda gathered, own: lax.dynamic_update_slice(gathered, own[None], (me,) + (0,) * own.ndim)
    tie = lambda *arrays: sum(t[(slice(0, 1),) * t.ndim].reshape(1).astype(F32) for t in arrays)
    early, started = _gather_start([shard_in, conv_w[0]], shard_in)
    tm = min(512, xs.shape[0])
    h = _prenorm(xs, norm_gain + zero(started), tm)

    z, w_pairs = lax.empty((xs.shape[0], IN_W), F32), lax.empty((N_PAIRS, PAIR_W, D_MODEL), BF16)
    early = _gather_from_sibling(early, h)
    early = dict(early, lands=[with_own(early["lands"][0], shard_in), early["lands"][1]])
    z, w_pairs = _fwd_in_pair(h, early["lands"][0], z, w_pairs, place[1:2], tm, "fwd_in_own")
    for j, flip in enumerate((2, 1, 3)):
        early = _gather_from_chip(early, j, z if j != 1 else tie(z, started_late), last=j == 2)
        z, w_pairs = _fwd_in_pair(h, early["lands"][0], z, w_pairs, place[1:2] ^ flip, tm, "fwd_in_chip_" + str(j))
        if j == 0:
            late, started_late = _gather_start(own_late, z)
    conv_full = jnp.transpose(with_own(early["lands"][1], conv_w[0]), (1, 0, 2)).reshape(3, ATTN_W)
    rope, gq2, gk2, conv_wp = _tables(xs.shape[0], q_norm_gain[0], k_norm_gain[0], conv_full)
    late, forwarded = _gather_forward(late, z)
    qn, k2, v2, a, mix, mixt = _forward_attn(z, rope, gq2 + zero(forwarded), gk2, conv_wp, attn_sinks)
    g_out, g_pg, g_pp = (with_own(g, own) for g, own in zip(_gather_wait(late, mix), own_late))
    w_out_f = g_out.reshape(D_MODEL, D_MODEL)
    w_pg_f = g_pg.reshape(D_MODEL, D_MODEL)
    w_pp_f = jnp.transpose(g_pp, (1, 0, 2)).reshape(PLE_DIM, D_MODEL)

    dx1, dx1b, (gw_out, gw_pg, gw_pp), acc_ple, acc_g2 = _forward_out(
        xs, ps, target, mix, mixt, w_out_f, ple_gate_norm_gain, w_pg_f, b_ple_gate, w_pp_f, ple_norm_gain)

    names = ("w_out", "w_ple_gate", "w_ple_proj")
    gw_pp_t = jnp.transpose(gw_pp.reshape(PLE_DIM, N_DEV, PLE_DIM), (1, 0, 2))
    grads = [_by_owner(gw_out.reshape(N_DEV, D_MODEL // N_DEV, D_MODEL)),
             _by_owner(gw_pg.reshape(N_DEV, D_MODEL // N_DEV, D_MODEL)), _by_owner(gw_pp_t)]
    pairs, paired = _exchange_start("pair_start", grads, [(4,) + g.shape[2:] for g in grads], _to_sibling, 1, dx1b)
    dmix = _mm_nt(dx1b, w_out_f, tm, "out_bwd", paired)
    from_sibling = _exchange_wait("pair_wait", pairs, _to_sibling, dmix)
    sums = [_pair_sum(g, r, place, 256, "pair_sum_" + nm) for g, r, nm in zip(pairs["srcs"], from_sibling, names)]
    chips, sent = _exchange_start("chip_start", [pb for pb, _ in sums], [(3,) + pb.shape[1:] for pb, _ in sums],
                                  _to_chips, 3, sums[-1][1])

    dz, gw_in, acc_attn, acc_qk = _backward_attn(
        dmix, h, z, qn, k2, v2, a, rope, gq2, gk2, conv_wp, attn_sinks + zero(sent))

    gw_in_t = [_by_owner(gw_in)]
    pairs_in, _ = _exchange_start("pair_start_w_in", gw_in_t, [(4,) + gw_in_t[0].shape[2:]], _to_sibling, 1, gw_in)
    from_chips = _exchange_wait("chip_wait", chips, _to_chips, gw_in)
    big = {}
    for (_, own), oth, w, m, v, nm in zip(sums, from_chips, (w_out, w_ple_gate, w_ple_proj),
                                          (m_w_out, m_w_ple_gate, m_w_ple_proj),
                                          (v_w_out, v_w_ple_gate, v_w_ple_proj), names):
        big[nm] = [t[None] for t in _adamw(own, oth, w[0], m[0], v[0], 256, "adamw_" + nm)]

    (from_sibling_in,) = _exchange_wait("pair_wait_w_in", pairs_in, _to_sibling, tie(*[big[nm][0] for nm in names]))
    pb_in, own_in = _pair_sum(pairs_in["srcs"][0], from_sibling_in, place, SHARD_IN // 2, "pair_sum_w_in")
    chips_in, sent_in = _exchange_start("chip_start_w_in", [pb_in], [(3,) + pb_in.shape[1:]], _to_chips, 3, own_in)
    grad_x, acc_g1 = _in_bwd(dz, w_pairs, xs, dx1, norm_gain + zero(sent_in), tm)
    (from_chips_in,) = _exchange_wait("chip_wait_w_in", chips_in, _to_chips, grad_x)
    slab = _slab(_small_rows(acc_g1, acc_g2, acc_ple, acc_qk, acc_attn))
    slabs, _ = _exchange_start("small_start", [slab], [(N_DEV,) + slab.shape], _to_all, N_DEV - 1, grad_x)
    big["w_in"] = [t.T[None] for t in _adamw(own_in, from_chips_in, w_in[0].T, m_w_in[0].T, v_w_in[0].T, SHARD_IN // 4,
                                             "adamw_w_in")]
    (all_slabs,) = _exchange_wait("small_wait", slabs, _to_all, big["w_in"][0])
    red = _sum_slabs(slab, all_slabs, me.reshape(1, 1).astype(jnp.int32))
    loss = jnp.sum(red[ROW_LOSS])
    g_conv = [lax.dynamic_slice(red[ROW_CONV + t:ROW_CONV + t + 1], (0, conv_cols * me), (1, conv_cols))
              for t in range(3)]
    small = [norm_gain, ple_gate_norm_gain, b_ple_gate, ple_norm_gain, q_norm_gain, k_norm_gain, attn_sinks]
    small_m = [m_norm_gain, m_ple_gate_norm_gain, m_b_ple_gate, m_ple_norm_gain, m_q_norm_gain, m_k_norm_gain,
               m_attn_sinks]
    small_v = [v_norm_gain, v_ple_gate_norm_gain, v_b_ple_gate, v_ple_norm_gain, v_q_norm_gain, v_k_norm_gain,
               v_attn_sinks]
    pack = lambda vs, cw: _slab([_row(t) for t in vs] + [_row(cw[0, t]) for t in range(3)])
    g_slab = _slab([red[t:t + 1] for t in range(ROW_CONV)] + [_row(t) for t in g_conv])
    d_slab, m_slab, v_slab = _adamw_slab(pack(small, conv_w), g_slab, pack(small_m, m_conv_w), pack(small_v, v_conv_w))

    def unpack(slab_):
        outs = [slab_[t:t + 1, :w.shape[1]] for t, w in enumerate(small)]
        return outs, slab_[ROW_CONV:ROW_CONV + 3, :conv_cols][None]

    (g_s, g_cv), (d_s, d_cv), (m_s, m_cv), (v_s, v_cv) = (unpack(t) for t in (g_slab, d_slab, m_slab, v_slab))

    def order(sm, cv, k):
        return [sm[0], big["w_in"][k], sm[4], sm[5], sm[6], cv, big["w_out"][k], sm[1], big["w_ple_gate"][k], sm[2],
                big["w_ple_proj"][k], sm[3]]

    return (loss, grad_x[None], *order(g_s, g_cv, 0), *order(d_s, d_cv, 1), *order(m_s, m_cv, 2),
            *order(v_s, v_cv, 3))
```

```python
import jax
import jax.numpy as jnp
from jax import lax
from jax.experimental import pallas as pl
from jax.experimental.pallas import tpu as pltpu

F32, BF16 = jnp.float32, jnp.bfloat16

D_MODEL = 2048
PLE_DIM = 256
ATTN_W = 1024
HEAD = 64
N_Q_HEADS = 16
KV_W = 256
QKV_W = ATTN_W + 2 * KV_W
REST_W = 5 * 1024
IN_W = QKV_W + REST_W
GATE_A0, CONV_B0, CONV_C0, CONV_H0, GATE_C0 = (QKV_W + 1024 * t for t in range(5))
K2_W = 4 * 128
ROT = 16
ROPE_THETA = 500000.0
EPS = 1e-6
NEG_INF = -1e30
BLK = 128
LANES = 128
SUBLANES = 8
N_DEV = 8
SHARD_IN = IN_W // N_DEV
PAIR_W = 2 * SHARD_IN
N_PAIRS = IN_W // PAIR_W
SLAB_ROWS = 16
SUB_ROWS = 128
V7X_VMEM_LIMIT = 52 * 1024 * 1024

ADAM_LR, ADAM_B1, ADAM_B2, ADAM_EPS, ADAM_WD, ADAM_STEP = 0.001, 0.9, 0.999, 1e-08, 0.01, 10
MESH = pl.DeviceIdType.MESH


def _params(*semantics):
    return pltpu.CompilerParams(dimension_semantics=semantics, vmem_limit_bytes=V7X_VMEM_LIMIT)


ANY = pl.BlockSpec(memory_space=pl.ANY)


def _resident(shape):
    return pl.BlockSpec(shape, lambda *_: (0,) * len(shape), pipeline_mode=pl.Buffered(1))


def _dot(a, b):
    return jnp.dot(a, b, preferred_element_type=F32)


def _dot_nt(a, b):
    return lax.dot_general(a, b, (((1,), (1,)), ((), ())), preferred_element_type=F32)


def _rms(xf):
    r = lax.rsqrt(jnp.mean(xf * xf, axis=-1, keepdims=True) + EPS)
    return xf * r, r


def _rms_bwd(dxn, xn, r):
    return r * (dxn - xn * jnp.mean(dxn * xn, axis=-1, keepdims=True))


def _sig(g):
    return jax.nn.sigmoid(g)


def _dsilu(g, sg):
    return sg * (1.0 + g * (1.0 - sg))


def _low_half(shape):
    return lax.broadcasted_iota(jnp.int32, shape, len(shape) - 1) < HEAD


def _half_sums(v):
    lo = _low_half(v.shape)
    s_lo = jnp.sum(jnp.where(lo, v, 0.0), axis=-1, keepdims=True)
    s_hi = jnp.sum(jnp.where(lo, 0.0, v), axis=-1, keepdims=True)
    return jnp.where(lo, s_lo, s_hi)


def _rope(v, a, bm, bp):
    return v * a + pltpu.roll(v, LANES - ROT // 2, 1) * bm + pltpu.roll(v, ROT // 2, 1) * bp


def _rope_t(dy, a, bm, bp):
    return dy * a + pltpu.roll(dy * bm, ROT // 2, 1) + pltpu.roll(dy * bp, LANES - ROT // 2, 1)


def _dup_halves(v):
    lo = _low_half(v.shape)
    a = jnp.where(lo, v, 0.0)
    b = jnp.where(lo, 0.0, v)
    return a + pltpu.roll(a, HEAD, 1), b + pltpu.roll(b, HEAD, 1)


def _rope_tables(s):
    half = ROT // 2
    lane = lax.broadcasted_iota(jnp.int32, (s, LANES), 1) % HEAD
    pos = lax.broadcasted_iota(jnp.int32, (s, LANES), 0).astype(F32)
    inv_freq = jnp.power(jnp.float32(ROPE_THETA), -(lane % half).astype(F32) * 2.0 / ROT)
    ang = pos * inv_freq
    cos, sin = jnp.cos(ang), jnp.sin(ang)
    a = jnp.where(lane < ROT, cos, 1.0)
    bm = jnp.where(lane < half, -sin, 0.0)
    bp = jnp.where((lane >= half) & (lane < ROT), sin, 0.0)
    return a, bm, bp


def _prenorm(x, g1, tm):
    s = x.shape[0]

    def body(x_ref, g_ref, h_ref):
        xn, _ = _rms(x_ref[...])
        h_ref[...] = (xn * g_ref[...]).astype(BF16)

    return pl.pallas_call(
        body, name="prenorm",
        out_shape=jax.ShapeDtypeStruct((s, D_MODEL), BF16),
        grid=(s // tm,),
        in_specs=[pl.BlockSpec((tm, D_MODEL), lambda i: (i, 0)), pl.BlockSpec((1, D_MODEL), lambda i: (0, 0))],
        out_specs=pl.BlockSpec((tm, D_MODEL), lambda i: (i, 0)),
        compiler_params=_params("parallel"))(x, g1)


def _fwd_in_pair(h, shards, z, w_pairs, pair, tm, name):
    s = h.shape[0]

    def body(pair_ref, h_ref, lo_ref, hi_ref, z_in, wp_in, z_ref, wp_ref):
        @pl.when(pl.program_id(0) == 0)
        def _():
            wp_ref[0, 0:SHARD_IN, :] = lo_ref[0]
            wp_ref[0, SHARD_IN:PAIR_W, :] = hi_ref[0]

        z_ref[...] = _dot_nt(h_ref[...], wp_ref[0])

    grid_spec = pltpu.PrefetchScalarGridSpec(
        num_scalar_prefetch=1, grid=(s // tm,),
        in_specs=[pl.BlockSpec((tm, D_MODEL), lambda i, p: (i, 0)),
                  pl.BlockSpec((1, SHARD_IN, D_MODEL), lambda i, p: (2 * p[0], 0, 0)),
                  pl.BlockSpec((1, SHARD_IN, D_MODEL), lambda i, p: (2 * p[0] + 1, 0, 0)), ANY, ANY],
        out_specs=(pl.BlockSpec((tm, PAIR_W), lambda i, p: (i, p[0])),
                   pl.BlockSpec((1, PAIR_W, D_MODEL), lambda i, p: (p[0], 0, 0))))
    return pl.pallas_call(
        body, name=name, grid_spec=grid_spec,
        out_shape=(jax.ShapeDtypeStruct(z.shape, z.dtype), jax.ShapeDtypeStruct(w_pairs.shape, w_pairs.dtype)),
        input_output_aliases={4: 0, 5: 1},
        compiler_params=_params("arbitrary"))(pair, h, shards, shards, z, w_pairs)


def _qk_prep(z, ra, rbm, rbp, gq2, gk2, tm):
    s = z.shape[0]

    def body(z_ref, a_ref, bm_ref, bp_ref, gq_ref, gk_ref, q_ref, k2_ref, v2_ref):
        a, bm, bp = a_ref[...], bm_ref[...], bp_ref[...]
        for r in range(ATTN_W // LANES):
            x = z_ref[:, LANES * r:LANES * (r + 1)]
            rr = lax.rsqrt(_half_sums(x * x) * (1.0 / HEAD) + EPS)
            q_ref[:, LANES * r:LANES * (r + 1)] = _rope(x * rr * gq_ref[...], a, bm, bp).astype(BF16)
        for m in range(KV_W // LANES):
            x = z_ref[:, ATTN_W + LANES * m:ATTN_W + LANES * (m + 1)]
            rr = lax.rsqrt(_half_sums(x * x) * (1.0 / HEAD) + EPS)
            k_lo, k_hi = _dup_halves(_rope(x * rr * gk_ref[...], a, bm, bp))
            k2_ref[:, 2 * LANES * m:2 * LANES * m + LANES] = k_lo.astype(BF16)
            k2_ref[:, 2 * LANES * m + LANES:2 * LANES * (m + 1)] = k_hi.astype(BF16)
            v_lo, v_hi = _dup_halves(z_ref[:, ATTN_W + KV_W + LANES * m:ATTN_W + KV_W + LANES * (m + 1)])
            v2_ref[:, 2 * LANES * m:2 * LANES * m + LANES] = v_lo.astype(BF16)
            v2_ref[:, 2 * LANES * m + LANES:2 * LANES * (m + 1)] = v_hi.astype(BF16)

    row = lambda w: pl.BlockSpec((tm, w), lambda i: (i, 0))
    one = pl.BlockSpec((1, LANES), lambda i: (0, 0))
    return pl.pallas_call(
        body, name="qk_prep",
        out_shape=(jax.ShapeDtypeStruct((s, ATTN_W), BF16), jax.ShapeDtypeStruct((s, K2_W), BF16),
                   jax.ShapeDtypeStruct((s, K2_W), BF16)),
        grid=(s // tm,),
        in_specs=[row(PAIR_W), row(LANES), row(LANES), row(LANES), one, one],
        out_specs=(row(ATTN_W), row(K2_W), row(K2_W)),
        compiler_params=_params("parallel"))(z, ra, rbm, rbp, gq2, gk2)


GROUP = 4


def _window_mask(n):
    row = lax.broadcasted_iota(jnp.int32, (GROUP * BLK, 2 * BLK), 0) % BLK
    col = lax.broadcasted_iota(jnp.int32, (GROUP * BLK, 2 * BLK), 1)
    return (col > row) & (col <= row + BLK) & ((col >= BLK) | (n > 0))


def _stack_heads(pairs, zero):
    lo = _low_half(pairs[0].shape)
    parts = []
    for v in pairs:
        parts += [jnp.where(lo, v, zero), jnp.where(lo, zero, v)]
    return jnp.concatenate(parts, axis=0)


def _unstack_heads(v4):
    lo = _low_half((BLK, LANES))
    return [jnp.where(lo, v4[2 * i * BLK:(2 * i + 1) * BLK], v4[(2 * i + 1) * BLK:(2 * i + 2) * BLK]) for i in range(2)]


def _group_sinks(sink_ref, kvh):
    slot = lax.broadcasted_iota(jnp.int32, (GROUP * BLK, 1), 0) // BLK
    col = jnp.zeros((GROUP * BLK, 1), F32)
    for i in range(GROUP):
        col = jnp.where(slot == i, sink_ref[0, GROUP * kvh + i], col)
    return col, slot


def _head_probs(qm, kw, valid, sink):
    sc = jnp.where(valid, _dot_nt(qm, kw) * (HEAD ** -0.5), NEG_INF)
    mx = jnp.maximum(jnp.max(sc, axis=-1, keepdims=True), sink)
    ex = jnp.exp(sc - mx)
    den = jnp.sum(ex, axis=-1, keepdims=True) + jnp.exp(sink - mx)
    return ex / den, mx, den


def _cols(start, width=ATTN_W):
    return slice(start, start + width)


def _conv_fwd(z_ref, zp_ref, cw_ref, ext_ref, n):
    u = z_ref[:, _cols(CONV_C0)] * z_ref[:, _cols(CONV_H0)]
    pu = zp_ref[:, _cols(CONV_C0)] * zp_ref[:, _cols(CONV_H0)]
    ext_ref[0:SUBLANES, :] = jnp.where(n > 0, pu, 0.0)
    ext_ref[SUBLANES:SUBLANES + BLK, :] = u
    um1 = ext_ref[SUBLANES - 1:SUBLANES - 1 + BLK, :]
    um2 = ext_ref[SUBLANES - 2:SUBLANES - 2 + BLK, :]
    cv = cw_ref[0:1, :] * um2 + cw_ref[1:2, :] * um1 + cw_ref[2:3, :] * u
    return u, um1, um2, cv


def _prev_rows(n):
    return (jnp.maximum(n * (BLK // SUBLANES) - 1, 0), 0)


def _attn_fwd(qn, k2, v2, z, conv_wp, sinks):
    s = qn.shape[0]
    nb = s // BLK

    def body(sink_ref, q_ref, kc_ref, kp_ref, vc_ref, vp_ref, z_ref, zp_ref, cw_ref, a_ref, mix_ref, mixt_ref,
             ext_ref):
        n = pl.program_id(0)
        valid = _window_mask(n)
        for kvh in range(K2_W // LANES):
            cols = slice(LANES * kvh, LANES * (kvh + 1))
            kw = jnp.concatenate([kp_ref[:, cols], kc_ref[:, cols]], axis=0)
            vw = jnp.concatenate([vp_ref[:, cols], vc_ref[:, cols]], axis=0)
            blocks = [slice(LANES * r, LANES * (r + 1)) for r in (2 * kvh, 2 * kvh + 1)]
            q4 = _stack_heads([q_ref[:, rc] for rc in blocks], jnp.zeros((BLK, LANES), BF16))
            p, _, _ = _head_probs(q4, kw, valid, _group_sinks(sink_ref, kvh)[0])
            for rc, a in zip(blocks, _unstack_heads(_dot(p.astype(BF16), vw))):
                a_ref[:, rc] = a
                g = z_ref[:, _cols(GATE_A0 + rc.start, LANES)]
                mix_ref[:, rc] = (a * (g * _sig(g))).astype(BF16)
        _, _, _, cv = _conv_fwd(z_ref, zp_ref, cw_ref, ext_ref, n)
        gc = z_ref[:, _cols(GATE_C0)]
        mix_ref[:, ATTN_W:D_MODEL] = (z_ref[:, _cols(CONV_B0)] * cv * (gc * _sig(gc))).astype(BF16)
        mixt_ref[...] = mix_ref[...].T

    cur = lambda w: pl.BlockSpec((BLK, w), lambda n: (n, 0))
    prev = lambda w: pl.BlockSpec((BLK, w), lambda n: (jnp.maximum(n - 1, 0), 0))
    return pl.pallas_call(
        body, name="attn_fwd",
        out_shape=(jax.ShapeDtypeStruct((s, ATTN_W), F32), jax.ShapeDtypeStruct((s, D_MODEL), BF16),
                   jax.ShapeDtypeStruct((D_MODEL, s), BF16)),
        grid=(nb,),
        in_specs=[pl.BlockSpec(memory_space=pltpu.SMEM),
                  cur(ATTN_W), cur(K2_W), prev(K2_W), cur(K2_W), prev(K2_W), cur(IN_W),
                  pl.BlockSpec((SUBLANES, IN_W), _prev_rows),
                  pl.BlockSpec((SUBLANES, ATTN_W), lambda n: (0, 0))],
        out_specs=(cur(ATTN_W), cur(D_MODEL), pl.BlockSpec((D_MODEL, BLK), lambda n: (0, n))),
        scratch_shapes=[pltpu.VMEM((BLK + 2 * SUBLANES, ATTN_W), F32)],
        compiler_params=_params("parallel"))(sinks, qn, k2, k2, v2, v2, z, z, conv_wp)


def _fwd_out(mix, w_out, x, g2, tm):
    s = x.shape[0]

    def body(m_ref, w_ref, x_ref, g_ref, x1_ref, h_ref, ht_ref):
        x1 = x_ref[...] + _dot(m_ref[...], w_ref[...])
        x1_ref[...] = x1
        xn, _ = _rms(x1)
        h = (xn * g_ref[...]).astype(BF16)
        h_ref[...] = h
        ht_ref[...] = h.T

    row = pl.BlockSpec((tm, D_MODEL), lambda i: (i, 0))
    return pl.pallas_call(
        body, name="fwd_out",
        out_shape=(jax.ShapeDtypeStruct((s, D_MODEL), F32), jax.ShapeDtypeStruct((s, D_MODEL), BF16),
                   jax.ShapeDtypeStruct((D_MODEL, s), BF16)),
        grid=(s // tm,),
        in_specs=[row, _resident((D_MODEL, D_MODEL)), row, pl.BlockSpec((1, D_MODEL), lambda i: (0, 0))],
        out_specs=(row, row, pl.BlockSpec((D_MODEL, tm), lambda i: (0, i))),
        compiler_params=_params("parallel"))(mix, w_out, x, g2)


def _ple(hn2, w_pg, b_pg, p, w_pp, g3, x1, target, tm):
    s = x1.shape[0]

    def body(h_ref, wg_ref, b_ref, p_ref, wp_ref, g3_ref, x1_ref, t_ref, dy_ref, dgp_ref, dt_ref, pt_ref, acc_ref):
        gate = _sig(_dot(h_ref[...], wg_ref[...]) + b_ref[...])
        pb = p_ref[...].astype(BF16)
        pt_ref[...] = pb.T
        t = _dot(pb, wp_ref[...])
        tn, r3 = _rms(t)
        e = tn * g3_ref[...]
        diff = x1_ref[...] + gate * e - t_ref[...]
        dy = diff * (1.0 / D_MODEL)
        dy_ref[...] = dy
        dgp = dy * e * (gate * (1.0 - gate))
        dgp_ref[...] = dgp.astype(BF16)
        de = dy * gate
        dt_ref[...] = _rms_bwd(de * g3_ref[...], tn, r3).astype(BF16)

        @pl.when(pl.program_id(0) == 0)
        def _():
            acc_ref[...] = jnp.zeros_like(acc_ref)

        acc_ref[0:1, :] += jnp.sum(dgp, axis=0, keepdims=True)
        acc_ref[1:2, :] += jnp.sum(de * tn, axis=0, keepdims=True)
        acc_ref[2:3, :] += jnp.sum(diff * diff, axis=0, keepdims=True) * (0.5 / D_MODEL)

    row = pl.BlockSpec((tm, D_MODEL), lambda i: (i, 0))
    vec = pl.BlockSpec((1, D_MODEL), lambda i: (0, 0))
    return pl.pallas_call(
        body, name="ple",
        out_shape=(jax.ShapeDtypeStruct((s, D_MODEL), F32), jax.ShapeDtypeStruct((s, D_MODEL), BF16),
                   jax.ShapeDtypeStruct((s, D_MODEL), BF16), jax.ShapeDtypeStruct((PLE_DIM, s), BF16),
                   jax.ShapeDtypeStruct((SUBLANES, D_MODEL), F32)),
        grid=(s // tm,),
        in_specs=[row, _resident((D_MODEL, D_MODEL)), vec, pl.BlockSpec((tm, PLE_DIM), lambda i: (i, 0)),
                  _resident((PLE_DIM, D_MODEL)), vec, row, row],
        out_specs=(row, row, row, pl.BlockSpec((PLE_DIM, tm), lambda i: (0, i)),
                   pl.BlockSpec((SUBLANES, D_MODEL), lambda i: (0, 0))),
        compiler_params=_params("arbitrary"))(hn2, w_pg, b_pg, p, w_pp, g3, x1, target)


def _gate_bwd(dgp, w_pg, x1, dy, g2, tm):
    s = x1.shape[0]

    def body(d_ref, w_ref, x1_ref, dy_ref, g_ref, dx_ref, dxb_ref, acc_ref):
        dh = _dot_nt(d_ref[...], w_ref[...])
        xn, r = _rms(x1_ref[...])
        dx1 = dy_ref[...] + _rms_bwd(dh * g_ref[...], xn, r)
        dx_ref[...] = dx1
        dxb_ref[...] = dx1.astype(BF16)

        @pl.when(pl.program_id(0) == 0)
        def _():
            acc_ref[...] = jnp.zeros_like(acc_ref)

        acc_ref[0:1, :] += jnp.sum(dh * xn, axis=0, keepdims=True)

    row = pl.BlockSpec((tm, D_MODEL), lambda i: (i, 0))
    return pl.pallas_call(
        body, name="gate_bwd",
        out_shape=(jax.ShapeDtypeStruct((s, D_MODEL), F32), jax.ShapeDtypeStruct((s, D_MODEL), BF16),
                   jax.ShapeDtypeStruct((SUBLANES, D_MODEL), F32)),
        grid=(s // tm,),
        in_specs=[row, _resident((D_MODEL, D_MODEL)), row, row, pl.BlockSpec((1, D_MODEL), lambda i: (0, 0))],
        out_specs=(row, row, pl.BlockSpec((SUBLANES, D_MODEL), lambda i: (0, 0))),
        compiler_params=_params("arbitrary"))(dgp, w_pg, x1, dy, g2)


def _mm_nt(a, b, tm, name, after):
    m, k = a.shape
    n = b.shape[0]

    def body(a_ref, b_ref, after_ref, o_ref):
        o_ref[...] = _dot_nt(a_ref[...], b_ref[...])

    return pl.pallas_call(
        body, name=name,
        out_shape=jax.ShapeDtypeStruct((m, n), F32),
        grid=(m // tm,),
        in_specs=[pl.BlockSpec((tm, k), lambda i: (i, 0)), _resident((n, k)), ANY],
        out_specs=pl.BlockSpec((tm, n), lambda i: (i, 0)),
        compiler_params=_params("parallel"))(a, b, after)


def _attn_bwd(qn, k2, v2, a, z, dmix, conv_wp, sinks):
    s = qn.shape[0]
    nb = s // BLK

    def body(sink_ref, q_ref, kc_ref, kp_ref, vc_ref, vp_ref, a_ref, z_ref, zp_ref, zn_ref, dm_ref, dmn_ref,
             cw_ref, dq_ref, dkc_ref, dkp_ref, dvc_ref, dvp_ref, dz_ref, dzt_ref, acc_ref, ext_ref):
        n = pl.program_id(0)
        valid = _window_mask(n)
        lane = lax.broadcasted_iota(jnp.int32, (1, ATTN_W), 1)

        @pl.when(n == 0)
        def _():
            acc_ref[...] = jnp.zeros_like(acc_ref)

        dz_ref[:, 0:QKV_W] = jnp.zeros((BLK, QKV_W), BF16)
        dsink = jnp.zeros((1, ATTN_W), F32)
        for kvh in range(K2_W // LANES):
            cols = slice(LANES * kvh, LANES * (kvh + 1))
            kw = jnp.concatenate([kp_ref[:, cols], kc_ref[:, cols]], axis=0)
            vw = jnp.concatenate([vp_ref[:, cols], vc_ref[:, cols]], axis=0)
            blocks = [slice(LANES * r, LANES * (r + 1)) for r in (2 * kvh, 2 * kvh + 1)]
            das, avs = [], []
            for rc in blocks:
                g = z_ref[:, _cols(GATE_A0 + rc.start, LANES)]
                sg = _sig(g)
                dm = dm_ref[:, rc]
                av = a_ref[:, rc]
                das.append(dm * (g * sg))
                avs += [av, av]
                dz_ref[:, _cols(GATE_A0 + rc.start, LANES)] = (dm * av * _dsilu(g, sg)).astype(BF16)
            q4 = _stack_heads([q_ref[:, rc] for rc in blocks], jnp.zeros((BLK, LANES), BF16))
            sink, slot = _group_sinks(sink_ref, kvh)
            p, mx, den = _head_probs(q4, kw, valid, sink)
            do4 = _stack_heads(das, 0.0)
            delta = jnp.sum(do4 * jnp.concatenate(avs, axis=0), axis=-1, keepdims=True)
            dob = do4.astype(BF16)
            ds = p * (_dot_nt(dob, vw) - delta) * (HEAD ** -0.5)
            for rc, dq in zip(blocks, _unstack_heads(_dot(ds.astype(BF16), kw))):
                dq_ref[:, rc] = dq
            dk2 = _dot(ds.T.astype(BF16), q4)
            dv2 = _dot(p.T.astype(BF16), dob)
            dkp_ref[:, cols] = dk2[0:BLK]
            dkc_ref[:, cols] = dk2[BLK:2 * BLK]
            dvp_ref[:, cols] = dv2[0:BLK]
            dvc_ref[:, cols] = dv2[BLK:2 * BLK]
            dsk = jnp.exp(sink - mx) / den * delta
            for i in range(GROUP):
                dsink = dsink - jnp.where(lane == GROUP * kvh + i,
                                          jnp.sum(jnp.where(slot == i, dsk, 0.0), axis=0, keepdims=True), 0.0)
        acc_ref[0:1, :] += dsink

        u, um1, um2, cv = _conv_fwd(z_ref, zp_ref, cw_ref, ext_ref, n)
        cb = z_ref[:, _cols(CONV_B0)]
        gc = z_ref[:, _cols(GATE_C0)]
        sgc = _sig(gc)
        dmc = dm_ref[:, ATTN_W:D_MODEL]
        t = dmc * (gc * sgc)
        dcv = t * cb
        dz_ref[:, _cols(CONV_B0)] = (t * cv).astype(BF16)
        dz_ref[:, _cols(GATE_C0)] = (dmc * cb * cv * _dsilu(gc, sgc)).astype(BF16)
        gcn = zn_ref[:, _cols(GATE_C0)]
        dcvn = dmn_ref[:, ATTN_W:D_MODEL] * (gcn * _sig(gcn)) * zn_ref[:, _cols(CONV_B0)]
        ext_ref[0:BLK, :] = dcv
        ext_ref[BLK:BLK + SUBLANES, :] = jnp.where(n < nb - 1, dcvn, 0.0)
        du = (cw_ref[2:3, :] * dcv + cw_ref[1:2, :] * ext_ref[1:1 + BLK, :]
              + cw_ref[0:1, :] * ext_ref[2:2 + BLK, :])
        dz_ref[:, _cols(CONV_C0)] = (du * z_ref[:, _cols(CONV_H0)]).astype(BF16)
        dz_ref[:, _cols(CONV_H0)] = (du * z_ref[:, _cols(CONV_C0)]).astype(BF16)
        acc_ref[1:2, :] += jnp.sum(dcv * um2, axis=0, keepdims=True)
        acc_ref[2:3, :] += jnp.sum(dcv * um1, axis=0, keepdims=True)
        acc_ref[3:4, :] += jnp.sum(dcv * u, axis=0, keepdims=True)
        dzt_ref[...] = dz_ref[...].T

    cur = lambda w: pl.BlockSpec((BLK, w), lambda n: (n, 0))
    prev = lambda w: pl.BlockSpec((BLK, w), lambda n: (jnp.maximum(n - 1, 0), 0))
    nxt = lambda w: pl.BlockSpec(
        (SUBLANES, w), lambda n: (jnp.minimum((n + 1) * (BLK // SUBLANES), nb * (BLK // SUBLANES) - 1), 0))
    f32 = lambda w: jax.ShapeDtypeStruct((s, w), F32)
    return pl.pallas_call(
        body, name="attn_bwd",
        out_shape=(f32(ATTN_W), f32(K2_W), f32(K2_W), f32(K2_W), f32(K2_W),
                   jax.ShapeDtypeStruct((s, IN_W), BF16), jax.ShapeDtypeStruct((IN_W, s), BF16),
                   jax.ShapeDtypeStruct((SUBLANES, ATTN_W), F32)),
        grid=(nb,),
        in_specs=[pl.BlockSpec(memory_space=pltpu.SMEM),
                  cur(ATTN_W), cur(K2_W), prev(K2_W), cur(K2_W), prev(K2_W), cur(ATTN_W), cur(IN_W),
                  pl.BlockSpec((SUBLANES, IN_W), _prev_rows), nxt(IN_W), cur(D_MODEL), nxt(D_MODEL),
                  pl.BlockSpec((SUBLANES, ATTN_W), lambda n: (0, 0))],
        out_specs=(cur(ATTN_W), cur(K2_W), cur(K2_W), cur(K2_W), cur(K2_W), cur(IN_W),
                   pl.BlockSpec((IN_W, BLK), lambda n: (0, n)), pl.BlockSpec((SUBLANES, ATTN_W), lambda n: (0, 0))),
        scratch_shapes=[pltpu.VMEM((BLK + 2 * SUBLANES, ATTN_W), F32)],
        compiler_params=_params("arbitrary"))(sinks, qn, k2, k2, v2, v2, a, z, z, z, dmix, dmix, conv_wp)


def _qkv_bwd(z, dz, dzt, dq, dkc, dkp, dvc, dvp, ra, rbm, rbp, gq2, gk2):
    s = z.shape[0]
    nb = s // BLK

    def body(z_ref, dz_in, dzt_in, dq_ref, dkc_ref, dkp_ref, dvc_ref, dvp_ref, a_ref, bm_ref, bp_ref, gq_ref, gk_ref,
             dz_ref, dzt_ref, acc_ref):
        n = pl.program_id(0)
        a, bm, bp = a_ref[...], bm_ref[...], bp_ref[...]
        lo = _low_half((BLK, LANES))
        last = n == nb - 1

        @pl.when(n == 0)
        def _():
            acc_ref[...] = jnp.zeros_like(acc_ref)

        def norm_bwd(x, dy, gain):
            rr = lax.rsqrt(_half_sums(x * x) * (1.0 / HEAD) + EPS)
            xh = x * rr
            dxg = _rope_t(dy, a, bm, bp)
            dxh = dxg * gain
            dx = rr * (dxh - xh * (_half_sums(dxh * xh) * (1.0 / HEAD)))
            return dx, jnp.sum(dxg * xh, axis=0, keepdims=True)

        def folded(cur_ref, prev_ref, m):
            parts = []
            for h in (2 * m, 2 * m + 1):
                v = cur_ref[:, LANES * h:LANES * (h + 1)] + jnp.where(
                    last, 0.0, prev_ref[:, LANES * h:LANES * (h + 1)])
                parts.append(v + pltpu.roll(v, HEAD, 1))
            return jnp.where(lo, parts[0], parts[1])

        gq_acc = jnp.zeros((1, LANES), F32)
        for r in range(ATTN_W // LANES):
            rc = slice(LANES * r, LANES * (r + 1))
            dx, gg = norm_bwd(z_ref[:, rc], dq_ref[:, rc], gq_ref[...])
            dz_ref[:, rc] = dx.astype(BF16)
            gq_acc = gq_acc + gg
        acc_ref[0:1, :] += gq_acc
        gk_acc = jnp.zeros((1, LANES), F32)
        for m in range(KV_W // LANES):
            kc = slice(ATTN_W + LANES * m, ATTN_W + LANES * (m + 1))
            dx, gg = norm_bwd(z_ref[:, kc], folded(dkc_ref, dkp_ref, m), gk_ref[...])
            dz_ref[:, kc] = dx.astype(BF16)
            gk_acc = gk_acc + gg
            vc = slice(ATTN_W + KV_W + LANES * m, ATTN_W + KV_W + LANES * (m + 1))
            dz_ref[:, vc] = folded(dvc_ref, dvp_ref, m).astype(BF16)
        acc_ref[1:2, :] += gk_acc
        dzt_ref[...] = dz_ref[...].T

    cur = lambda w: pl.BlockSpec((BLK, w), lambda n: (n, 0))
    nxt = lambda w: pl.BlockSpec((BLK, w), lambda n: (jnp.minimum(n + 1, nb - 1), 0))
    one = pl.BlockSpec((1, LANES), lambda n: (0, 0))
    return pl.pallas_call(
        body, name="qkv_bwd",
        out_shape=(jax.ShapeDtypeStruct(dz.shape, dz.dtype), jax.ShapeDtypeStruct(dzt.shape, dzt.dtype),
                   jax.ShapeDtypeStruct((SUBLANES, LANES), F32)),
        grid=(nb,),
        in_specs=[cur(PAIR_W), ANY, ANY, cur(ATTN_W), cur(K2_W), nxt(K2_W), cur(K2_W), nxt(K2_W),
                  cur(LANES), cur(LANES), cur(LANES), one, one],
        out_specs=(cur(QKV_W), pl.BlockSpec((QKV_W, BLK), lambda n: (0, n)),
                   pl.BlockSpec((SUBLANES, LANES), lambda n: (0, 0))),
        input_output_aliases={1: 0, 2: 1},
        compiler_params=_params("arbitrary"))(z, dz, dzt, dq, dkc, dkp, dvc, dvp, ra, rbm, rbp, gq2, gk2)


def _in_bwd(dz, w_pairs, x, dx1, g1, tm):
    s = x.shape[0]

    def body(d_ref, w_ref, x_hbm, dx1_hbm, g_ref, gx_ref, acc_ref, x_buf, dx1_buf, sems):
        i, k = pl.program_id(0), pl.program_id(1)
        rows = pl.ds(pl.multiple_of(i * tm, tm), tm)
        fetch = [pltpu.make_async_copy(x_hbm.at[rows], x_buf, sems.at[0]),
                 pltpu.make_async_copy(dx1_hbm.at[rows], dx1_buf, sems.at[1])]
        sub = min(SUB_ROWS, tm)
        blocks = [slice(r, r + sub) for r in range(0, tm, sub)]

        @pl.when(k == 0)
        def _():
            for cp in fetch:
                cp.start()
            gx_ref[...] = _dot(d_ref[...], w_ref[0])

        @pl.when(k > 0)
        def _():
            gx_ref[...] += _dot(d_ref[...], w_ref[0])

        @pl.when((i == 0) & (k == 0))
        def _():
            acc_ref[...] = jnp.zeros_like(acc_ref)

        @pl.when(k == N_PAIRS - 1)
        def _():
            for cp in fetch:
                cp.wait()
            for rb in blocks:
                dh = gx_ref[rb, :]
                xn, r = _rms(x_buf[rb, :])
                gx_ref[rb, :] = dx1_buf[rb, :] + _rms_bwd(dh * g_ref[...], xn, r)
                acc_ref[0:1, :] += jnp.sum(dh * xn, axis=0, keepdims=True)

    return pl.pallas_call(
        body, name="in_bwd",
        out_shape=(jax.ShapeDtypeStruct((s, D_MODEL), F32), jax.ShapeDtypeStruct((SUBLANES, D_MODEL), F32)),
        grid=(s // tm, N_PAIRS),
        in_specs=[pl.BlockSpec((tm, PAIR_W), lambda i, k: (i, k)),
                  pl.BlockSpec((1, PAIR_W, D_MODEL), lambda i, k: (k, 0, 0)),
                  ANY, ANY, pl.BlockSpec((1, D_MODEL), lambda i, k: (0, 0))],
        out_specs=(pl.BlockSpec((tm, D_MODEL), lambda i, k: (i, 0)),
                   pl.BlockSpec((SUBLANES, D_MODEL), lambda i, k: (0, 0))),
        scratch_shapes=[pltpu.VMEM((tm, D_MODEL), F32), pltpu.VMEM((tm, D_MODEL), F32),
                        pltpu.SemaphoreType.DMA((2,))],
        compiler_params=_params("arbitrary", "arbitrary"))(dz, w_pairs, x, dx1, g1)


def _mm_grad(at, bs, tn, name):
    m, kdim = at.shape
    nblk = [b.shape[1] // tn for b in bs]
    starts = [sum(nblk[:t]) for t in range(len(bs))]

    def body(a_ref, *refs):
        b_refs, o_ref = refs[:len(bs)], refs[len(bs)]
        j = pl.program_id(0)
        for t, b_ref in enumerate(b_refs):
            @pl.when((j >= starts[t]) & (j < starts[t] + nblk[t]))
            def _():
                o_ref[...] = _dot(a_ref[...], b_ref[...]).astype(BF16)

    def b_spec(t):
        return pl.BlockSpec((kdim, tn), lambda j: (0, jnp.clip(j - starts[t], 0, nblk[t] - 1)))

    return pl.pallas_call(
        body, name=name,
        out_shape=jax.ShapeDtypeStruct((m, sum(nblk) * tn), BF16),
        grid=(sum(nblk),),
        in_specs=[_resident((m, kdim))] + [b_spec(t) for t in range(len(bs))],
        out_specs=pl.BlockSpec((m, tn), lambda j: (0, j)),
        compiler_params=_params("parallel"))(at, *bs)


def _grad_w_in(dzt, h):
    kdim = h.shape[0]

    def body(d_ref, h_ref, o_ref):
        o_ref[0] = _dot(d_ref[...], h_ref[...]).astype(BF16)

    return pl.pallas_call(
        body, name="grad_w_in",
        out_shape=jax.ShapeDtypeStruct((N_DEV, SHARD_IN, D_MODEL), BF16),
        grid=(N_DEV,),
        in_specs=[pl.BlockSpec((SHARD_IN, kdim), lambda j: (j, 0)), _resident((kdim, D_MODEL))],
        out_specs=pl.BlockSpec((1, SHARD_IN, D_MODEL), lambda j: (j, 0, 0)),
        compiler_params=_params("parallel"))(dzt, h)


def _place():
    return lax.axis_index("x"), lax.axis_index("y"), lax.axis_index("c")


def _all_reduce_slab(slab, name):
    def body(in_ref, out_ref, gath_ref, send_sems, recv_sems):
        x, y, c = _place()
        me = 4 * x + 2 * y + c
        gath_ref[me] = in_ref[...]
        copies = []
        for k in range(1, N_DEV):
            peer = (x ^ (k >> 2), y ^ ((k >> 1) & 1), c ^ (k & 1))
            copies.append(pltpu.make_async_remote_copy(
                src_ref=in_ref, dst_ref=gath_ref.at[me], send_sem=send_sems.at[k - 1],
                recv_sem=recv_sems.at[k - 1], device_id=peer, device_id_type=MESH))
        for cp in copies:
            cp.start()
        for cp in copies:
            cp.wait_recv()
        for cp in copies:
            cp.wait_send()
        total = gath_ref[0]
        for d in range(1, N_DEV):
            total = total + gath_ref[d]
        out_ref[...] = total

    vmem = pl.BlockSpec(memory_space=pltpu.VMEM)
    return pl.pallas_call(
        body, name=name,
        out_shape=jax.ShapeDtypeStruct(slab.shape, F32),
        in_specs=[vmem], out_specs=vmem,
        scratch_shapes=[pltpu.VMEM((N_DEV,) + slab.shape, F32),
                        pltpu.SemaphoreType.DMA((N_DEV - 1,)), pltpu.SemaphoreType.DMA((N_DEV - 1,))])(slab)


def _pair_sum(g, r, place, tr, name):
    _, _, rows, cols = g.shape

    def body(place_ref, g_ref, r_ref, pb_ref, own_ref):
        tot = g_ref[0, 0].astype(F32) + r_ref[0].astype(F32)
        pb_ref[0] = tot.astype(BF16)

        @pl.when(pl.program_id(1) == place_ref[1])
        def _():
            own_ref[...] = tot

    grid_spec = pltpu.PrefetchScalarGridSpec(
        num_scalar_prefetch=1, grid=(rows // tr, 4),
        in_specs=[pl.BlockSpec((1, 1, tr, cols), lambda i, q, place_ref: (q, place_ref[0], i, 0)),
                  pl.BlockSpec((1, tr, cols), lambda i, q, place_ref: (q, i, 0))],
        out_specs=(pl.BlockSpec((1, tr, cols), lambda i, q, place_ref: (q, i, 0)),
                   pl.BlockSpec((tr, cols), lambda i, q, place_ref: (i, 0))))
    return pl.pallas_call(
        body, name=name, grid_spec=grid_spec,
        out_shape=(jax.ShapeDtypeStruct((4, rows, cols), BF16), jax.ShapeDtypeStruct((rows, cols), F32)),
        compiler_params=_params("arbitrary", "arbitrary"))(place, g, r)


HBM = pl.BlockSpec(memory_space=pltpu.HBM)
SEM = pl.BlockSpec(memory_space=pltpu.SEMAPHORE)
SIDE_EFFECT = pltpu.CompilerParams(has_side_effects=pltpu.SideEffectType.DATAFLOW_SIDE_EFFECTING)
TOKEN = jax.ShapeDtypeStruct((SUBLANES, LANES), F32)


def _hbm(a):
    return pltpu.with_memory_space_constraint(a, pltpu.HBM)


def _hbm_like(arrays):
    return tuple(pltpu.HBM(a.shape, a.dtype) for a in arrays)


def _block_of(px, py, pc):
    return 4 * px + 2 * py + pc


def _gather_start(shards, after):
    na = len(shards)
    lands = [_hbm(lax.empty((N_DEV,) + a.shape, a.dtype)) for a in shards]

    def body(*refs):
        ins, land = refs[:na], refs[na:2 * na]
        send_sems, recv_ici, recv_d2d = refs[2 * na + 1:2 * na + 4]
        token = refs[-1]
        x, y, c = _place()
        for k, peer in enumerate([(x, y, 1 - c), (1 - x, y, c), (x, 1 - y, c), (1 - x, 1 - y, c)]):
            for t in range(na):
                pltpu.make_async_remote_copy(
                    src_ref=ins[t], dst_ref=land[t].at[_block_of(x, y, c)], send_sem=send_sems.at[4 * t + k],
                    recv_sem=recv_d2d.at[4 * t] if k == 0 else recv_ici.at[3 * t + k - 1],
                    device_id=peer, device_id_type=MESH).start()
        token[...] = jnp.zeros_like(token)

    out = pl.pallas_call(
        body, name="gather_start",
        out_shape=(pltpu.SemaphoreType.DMA((4 * na,)), pltpu.SemaphoreType.DMA((3 * na,)),
                   pltpu.SemaphoreType.DMA((4 * na,)), *_hbm_like(lands), TOKEN),
        in_specs=[ANY] * na + [HBM] * na + [ANY],
        out_specs=(SEM, SEM, SEM, *[HBM] * na, pl.BlockSpec(memory_space=pltpu.VMEM)),
        input_output_aliases={na + i: 3 + i for i in range(na)},
        compiler_params=SIDE_EFFECT)(*shards, *lands, after)
    send_sems, recv_ici, recv_d2d = out[:3]
    state = dict(send=send_sems, ici=recv_ici, d2d=recv_d2d, shards=list(shards), lands=out[3:3 + na])
    return state, out[-1]


def _gather_forward(state, after):
    lands = state["lands"]
    na = len(lands)

    def body(*refs):
        land = refs[:na]
        recv_ici, recv_d2d = refs[na], refs[na + 1]
        fwd_sems, token = refs[-2], refs[-1]
        x, y, c = _place()
        for j, chip in enumerate([(1 - x, y), (x, 1 - y), (1 - x, 1 - y)]):
            for t in range(na):
                blk = land[t].at[_block_of(*chip, c)]
                pltpu.make_async_remote_copy(
                    src_ref=blk, dst_ref=blk, send_sem=fwd_sems.at[3 * t + j], recv_sem=recv_ici.at[3 * t + j],
                    device_id=(x, y, c), device_id_type=MESH).wait_recv()
                pltpu.make_async_remote_copy(
                    src_ref=blk, dst_ref=blk, send_sem=fwd_sems.at[3 * t + j], recv_sem=recv_d2d.at[4 * t + 1 + j],
                    device_id=(x, y, 1 - c), device_id_type=MESH).start()
        token[...] = jnp.zeros_like(token)

    out = pl.pallas_call(
        body, name="gather_forward",
        out_shape=(*_hbm_like(lands), pltpu.SemaphoreType.DMA((3 * na,)), TOKEN),
        in_specs=[HBM] * na + [SEM, SEM, ANY],
        out_specs=(*[HBM] * na, SEM, pl.BlockSpec(memory_space=pltpu.VMEM)),
        input_output_aliases={i: i for i in range(na)},
        compiler_params=SIDE_EFFECT)(*lands, state["ici"], state["d2d"], after)
    return dict(state, lands=out[:na], fwd=out[na]), out[-1]


def _gather_wait(state, after):
    shards, lands = state["shards"], state["lands"]
    na = len(lands)

    def body(*refs):
        ins, land = refs[:na], refs[na:2 * na]
        send_sems, fwd_sems, recv_d2d = refs[2 * na:2 * na + 3]
        x, y, c = _place()
        chips = [(1 - x, y), (x, 1 - y), (1 - x, 1 - y)]
        for t in range(na):
            mine = land[t].at[_block_of(x, y, c)]
            for k in range(4):
                pltpu.make_async_remote_copy(
                    src_ref=ins[t], dst_ref=mine, send_sem=send_sems.at[4 * t + k], recv_sem=recv_d2d.at[4 * t],
                    device_id=(x, y, c), device_id_type=MESH).wait_send()
            for j, chip in enumerate(chips):
                blk = land[t].at[_block_of(*chip, c)]
                pltpu.make_async_remote_copy(
                    src_ref=blk, dst_ref=blk, send_sem=fwd_sems.at[3 * t + j], recv_sem=recv_d2d.at[4 * t + 1 + j],
                    device_id=(x, y, c), device_id_type=MESH).wait_send()
            for k, blk_id in enumerate([_block_of(x, y, 1 - c)] + [_block_of(*chip, 1 - c) for chip in chips]):
                blk = land[t].at[blk_id]
                pltpu.make_async_remote_copy(
                    src_ref=blk, dst_ref=blk, send_sem=send_sems.at[4 * t], recv_sem=recv_d2d.at[4 * t + k],
                    device_id=(x, y, c), device_id_type=MESH).wait_recv()

    out = pl.pallas_call(
        body, name="gather_wait",
        out_shape=_hbm_like(lands),
        in_specs=[ANY] * na + [HBM] * na + [SEM, SEM, SEM, ANY],
        out_specs=tuple([HBM] * na),
        input_output_aliases={na + i: i for i in range(na)},
        compiler_params=SIDE_EFFECT)(*shards, *lands, state["send"], state["fwd"], state["d2d"], after)
    return out


def _gather_from_sibling(state, after):
    lands = state["lands"]
    na = len(lands)

    def body(*refs):
        land, recv_d2d = refs[:na], refs[na]
        x, y, c = _place()
        for t in range(na):
            blk = land[t].at[_block_of(x, y, 1 - c)]
            pltpu.make_async_remote_copy(src_ref=blk, dst_ref=blk, send_sem=recv_d2d.at[4 * t],
                                         recv_sem=recv_d2d.at[4 * t], device_id=(x, y, c),
                                         device_id_type=MESH).wait_recv()

    out = pl.pallas_call(
        body, name="gather_from_sibling", out_shape=_hbm_like(lands),
        in_specs=[HBM] * na + [SEM, ANY], out_specs=tuple([HBM] * na),
        input_output_aliases={i: i for i in range(na)},
        compiler_params=SIDE_EFFECT)(*lands, state["d2d"], after)
    return dict(state, lands=list(out))


def _gather_from_chip(state, j, after, last):
    shards, lands = state["shards"], state["lands"]
    na = len(lands)

    def chip_blocks(land_ref):
        x, y, c = _place()
        chip = [(1 - x, y), (x, 1 - y), (1 - x, 1 - y)][j]
        return (x, y, c), land_ref.at[_block_of(*chip, c)], land_ref.at[_block_of(*chip, 1 - c)]

    def forward(*refs):
        land, recv_ici, recv_d2d, fwd_sems = refs[:na], refs[na], refs[na + 1], refs[-1]
        for t in range(na):
            (x, y, c), mine, _ = chip_blocks(land[t])
            pltpu.make_async_remote_copy(src_ref=mine, dst_ref=mine, send_sem=fwd_sems.at[t],
                                         recv_sem=recv_ici.at[3 * t + j], device_id=(x, y, c),
                                         device_id_type=MESH).wait_recv()
            pltpu.make_async_remote_copy(src_ref=mine, dst_ref=mine, send_sem=fwd_sems.at[t],
                                         recv_sem=recv_d2d.at[4 * t + 1 + j], device_id=(x, y, 1 - c),
                                         device_id_type=MESH).start()

    out = pl.pallas_call(
        forward, name="gather_pass_chip_" + str(j),
        out_shape=(*_hbm_like(lands), pltpu.SemaphoreType.DMA((na,))),
        in_specs=[HBM] * na + [SEM, SEM, ANY], out_specs=(*[HBM] * na, SEM),
        input_output_aliases={i: i for i in range(na)},
        compiler_params=SIDE_EFFECT)(*lands, state["ici"], state["d2d"], after)
    lands, fwd_sems = out[:na], out[na]

    def arrive(*refs):
        land, fwd_sems, recv_d2d = refs[:na], refs[na], refs[na + 1]
        shard, send_sems = refs[na + 2:2 * na + 2], refs[2 * na + 2]
        for t in range(na):
            (x, y, c), mine, theirs = chip_blocks(land[t])
            pltpu.make_async_remote_copy(src_ref=theirs, dst_ref=theirs, send_sem=fwd_sems.at[t],
                                         recv_sem=recv_d2d.at[4 * t + 1 + j], device_id=(x, y, c),
                                         device_id_type=MESH).wait_recv()
            pltpu.make_async_remote_copy(src_ref=mine, dst_ref=mine, send_sem=fwd_sems.at[t],
                                         recv_sem=recv_d2d.at[4 * t + 1 + j], device_id=(x, y, c),
                                         device_id_type=MESH).wait_send()
            for k in range(4 if last else 0):
                pltpu.make_async_remote_copy(
                    src_ref=shard[t], dst_ref=land[t].at[_block_of(x, y, c)], send_sem=send_sems.at[4 * t + k],
                    recv_sem=recv_d2d.at[4 * t], device_id=(x, y, c), device_id_type=MESH).wait_send()

    out = pl.pallas_call(
        arrive, name="gather_take_chip_" + str(j), out_shape=_hbm_like(lands),
        in_specs=[HBM] * na + [SEM, SEM] + [ANY] * na + [SEM], out_specs=tuple([HBM] * na),
        input_output_aliases={i: i for i in range(na)},
        compiler_params=SIDE_EFFECT)(*lands, fwd_sems, state["d2d"], *shards, state["send"])
    return dict(state, lands=list(out))


def _to_sibling(srcs, lands, send_sems, recv_sems):
    x, y, c = _place()
    return [pltpu.make_async_remote_copy(
        src_ref=srcs[t].at[:, 1 - c], dst_ref=lands[t], send_sem=send_sems.at[t], recv_sem=recv_sems.at[t],
        device_id=(x, y, 1 - c), device_id_type=MESH) for t in range(len(srcs))]


def _to_chips(srcs, lands, send_sems, recv_sems):
    x, y, c = _place()
    copies = []
    for k in (1, 2, 3):
        px, py = x ^ (k >> 1), y ^ (k & 1)
        copies += [pltpu.make_async_remote_copy(
            src_ref=srcs[t].at[2 * px + py], dst_ref=lands[t].at[k - 1], send_sem=send_sems.at[3 * t + k - 1],
            recv_sem=recv_sems.at[3 * t + k - 1], device_id=(px, py, c), device_id_type=MESH) for t in range(len(srcs))]
    return copies


def _exchange_start(name, srcs, land_shapes, copies, per_array, after):
    na = len(srcs)
    lands = [_hbm(lax.empty(shp, a.dtype)) for shp, a in zip(land_shapes, srcs)]

    def body(*refs):
        token = refs[-1]
        for cp in copies(refs[:na], refs[na:2 * na], refs[2 * na + 1], refs[2 * na + 2]):
            cp.start()
        token[...] = jnp.zeros_like(token)

    out = pl.pallas_call(
        body, name=name,
        out_shape=(pltpu.SemaphoreType.DMA((na * per_array,)), pltpu.SemaphoreType.DMA((na * per_array,)),
                   *_hbm_like(lands), TOKEN),
        in_specs=[ANY] * na + [HBM] * na + [ANY],
        out_specs=(SEM, SEM, *[HBM] * na, pl.BlockSpec(memory_space=pltpu.VMEM)),
        input_output_aliases={na + i: 2 + i for i in range(na)},
        compiler_params=SIDE_EFFECT)(*srcs, *lands, after)
    return dict(send=out[0], recv=out[1], srcs=list(srcs), lands=out[2:2 + na]), out[-1]


def _exchange_wait(name, state, copies, after):
    srcs, lands = state["srcs"], state["lands"]
    na = len(srcs)

    def body(*refs):
        for cp in copies(refs[:na], refs[na:2 * na], refs[2 * na], refs[2 * na + 1]):
            cp.wait_send()
            cp.wait_recv()

    out = pl.pallas_call(
        body, name=name,
        out_shape=_hbm_like(lands),
        in_specs=[ANY] * na + [HBM] * na + [SEM, SEM, ANY],
        out_specs=tuple([HBM] * na),
        input_output_aliases={na + i: i for i in range(na)},
        compiler_params=SIDE_EFFECT)(*srcs, *lands, state["send"], state["recv"], after)
    return out


def _adamw_math(w, g, m, v):
    m = ADAM_B1 * m + (1.0 - ADAM_B1) * g
    v = ADAM_B2 * v + (1.0 - ADAM_B2) * (g * g)
    m_hat = m / (1.0 - ADAM_B1 ** ADAM_STEP)
    v_hat = v / (1.0 - ADAM_B2 ** ADAM_STEP)
    return -ADAM_LR * (m_hat / (jnp.sqrt(v_hat) + ADAM_EPS) + ADAM_WD * w), m, v


def _adamw(own, others, w, m, v, tr, name, after):
    rows, cols = w.shape
    blk = pl.BlockSpec((tr, cols), lambda i: (i, 0))

    def body(own_ref, oth_ref, w_ref, m_ref, v_ref, after_ref, g_ref, d_ref, nm_ref, nv_ref):
        g = own_ref[...]
        for k in range(3):
            g = g + oth_ref[k].astype(F32)
        g_ref[...] = g
        d_ref[...], nm_ref[...], nv_ref[...] = _adamw_math(w_ref[...], g, m_ref[...], v_ref[...])

    out = jax.ShapeDtypeStruct((rows, cols), F32)
    return pl.pallas_call(
        body, name=name, out_shape=(out, out, out, out), grid=(rows // tr,),
        in_specs=[blk, pl.BlockSpec((3, tr, cols), lambda i: (0, i, 0)), blk, blk, blk, ANY],
        out_specs=(blk, blk, blk, blk),
        compiler_params=_params("parallel"))(own, others, w, m, v, after)


def _adamw_slab(w, g, m, v):
    def body(w_ref, g_ref, m_ref, v_ref, d_ref, nm_ref, nv_ref):
        d_ref[...], nm_ref[...], nv_ref[...] = _adamw_math(w_ref[...], g_ref[...], m_ref[...], v_ref[...])

    out = jax.ShapeDtypeStruct(w.shape, F32)
    vmem = pl.BlockSpec(memory_space=pltpu.VMEM)
    return pl.pallas_call(body, name="adamw_small", out_shape=(out, out, out),
                          in_specs=[vmem] * 4, out_specs=(vmem, vmem, vmem))(w, g, m, v)


def _row(v, width=D_MODEL):
    v = v.reshape(1, -1)
    return jnp.pad(v, ((0, 0), (0, width - v.shape[1])))


def _tables(s, gq, gk, conv_w):
    gq2 = jnp.tile(gq.reshape(1, HEAD), (1, 2))
    gk2 = jnp.tile(gk.reshape(1, HEAD), (1, 2))
    conv_wp = jnp.pad(conv_w, ((0, SUBLANES - conv_w.shape[0]), (0, 0)))
    return _rope_tables(s), gq2, gk2, conv_wp


def _pair_id(q):
    return jnp.full((1,), q, jnp.int32)


def _forward_in(x, g1, shards):
    s = x.shape[0]
    h = _prenorm(x, g1, min(512, s))
    z, w_pairs = lax.empty((s, IN_W), F32), lax.empty((N_PAIRS, PAIR_W, D_MODEL), BF16)
    for q in range(N_PAIRS):
        z, w_pairs = _fwd_in_pair(h, shards, z, w_pairs, _pair_id(q), min(512, s), "fwd_in_" + str(q))
    return h, z, w_pairs


def _forward_attn(z, rope, gq2, gk2, conv_wp, sinks):
    s = z.shape[0]
    qn, k2, v2 = _qk_prep(z, *rope, gq2, gk2, min(256, s))
    a, mix, mixt = _attn_fwd(qn, k2, v2, z, conv_wp, sinks)
    return qn, k2, v2, a, mix, mixt


def _forward_out(x, p, target, mix, mixt, w_out, g2, w_pg, b_pg, w_pp, g3):
    s = x.shape[0]
    tm = min(512, s)
    x1, hn2, hn2t = _fwd_out(mix, w_out, x, g2, tm)
    dy, dgp, dt, pt, acc_ple = _ple(hn2, w_pg, b_pg, p, w_pp, g3, x1, target, min(256, s))
    dx1, dx1b, acc_g2 = _gate_bwd(dgp, w_pg, x1, dy, g2, tm)
    gw_out = _mm_grad(mixt, [dx1b], 512, "grad_w_out")
    gw_pg = _mm_grad(hn2t, [dgp], 512, "grad_w_ple_gate")
    gw_pp = _mm_grad(pt, [dt], 512, "grad_w_ple_proj")
    return dx1, dx1b, (gw_out, gw_pg, gw_pp), acc_ple, acc_g2


def _backward_attn(dmix, h, z, qn, k2, v2, a, rope, gq2, gk2, conv_wp, sinks):
    dq, dkc, dkp, dvc, dvp, dz, dzt, acc_attn = _attn_bwd(qn, k2, v2, a, z, dmix, conv_wp, sinks)
    dz, dzt, acc_qk = _qkv_bwd(z, dz, dzt, dq, dkc, dkp, dvc, dvp, *rope, gq2, gk2)
    return dz, _grad_w_in(dzt, h), acc_attn, acc_qk


def _small_rows(acc_g1, acc_g2, acc_ple, acc_qk, acc_attn):
    fold = lambda v: _row((v[:HEAD] + v[HEAD:]))
    return [acc_g1[0:1], acc_g2[0:1], acc_ple[0:1], acc_ple[1:2], fold(acc_qk[0]), fold(acc_qk[1]),
            _row(acc_attn[0, :N_Q_HEADS]), _row(acc_attn[1]), _row(acc_attn[2]), _row(acc_attn[3]), acc_ple[2:3]]


def _local_step(x, p, target, g1, shards, gq, gk, sinks, conv_w, w_out, g2, w_pg, b_pg, w_pp, g3):
    rope, gq2, gk2, conv_wp = _tables(x.shape[0], gq, gk, conv_w)
    h, z, w_pairs = _forward_in(x, g1, shards)
    qn, k2, v2, a, mix, mixt = _forward_attn(z, rope, gq2, gk2, conv_wp, sinks)
    dx1, dx1b, (gw_out, gw_pg, gw_pp), acc_ple, acc_g2 = _forward_out(
        x, p, target, mix, mixt, w_out, g2, w_pg, b_pg, w_pp, g3)
    dmix = _mm_nt(dx1b, w_out, min(512, x.shape[0]), "out_bwd", dx1b)
    dz, gw_in, acc_attn, acc_qk = _backward_attn(dmix, h, z, qn, k2, v2, a, rope, gq2, gk2, conv_wp, sinks)
    grad_x, acc_g1 = _in_bwd(dz, w_pairs, x, dx1, g1, min(512, x.shape[0]))
    return grad_x, (gw_in, gw_out, gw_pg, gw_pp), _small_rows(acc_g1, acc_g2, acc_ple, acc_qk, acc_attn)


ROW_CONV, ROW_LOSS = 7, 10


def _slab(rows):
    rows = list(rows)
    return jnp.concatenate(rows + [jnp.zeros((SLAB_ROWS - len(rows), D_MODEL), F32)], axis=0)


def _by_owner(g):
    return g.reshape((4, 2) + g.shape[1:])


def kernel(x, p, norm_gain, w_in, q_norm_gain, k_norm_gain, attn_sinks, conv_w, w_out, ple_gate_norm_gain, w_ple_gate, b_ple_gate, w_ple_proj, ple_norm_gain, loss_target, m_norm_gain, m_w_in, m_q_norm_gain, m_k_norm_gain, m_attn_sinks, m_conv_w, m_w_out, m_ple_gate_norm_gain, m_w_ple_gate, m_b_ple_gate, m_w_ple_proj, m_ple_norm_gain, v_norm_gain, v_w_in, v_q_norm_gain, v_k_norm_gain, v_attn_sinks, v_conv_w, v_w_out, v_ple_gate_norm_gain, v_w_ple_gate, v_b_ple_gate, v_w_ple_proj, v_ple_norm_gain):
    me = 4 * lax.axis_index("x") + 2 * lax.axis_index("y") + lax.axis_index("c")
    place = jnp.stack([lax.axis_index("c"), 2 * lax.axis_index("x") + lax.axis_index("y")]).astype(jnp.int32)
    conv_cols = conv_w.shape[2]
    xs, ps, target = x[0], p[0, 0], loss_target[0]
    zero = lambda token: token[0:1, 0:1]

    shard_in = w_in[0].T.astype(BF16)
    own_late = [w_out[0].astype(BF16), w_ple_gate[0].astype(BF16), w_ple_proj[0].astype(BF16)]
    with_own = lambda gathered, own: lax.dynamic_update_slice(gathered, own[None], (me,) + (0,) * own.ndim)
    tie = lambda *arrays: sum(t[(slice(0, 1),) * t.ndim].reshape(1).astype(F32) for t in arrays)
    early, started = _gather_start([shard_in, conv_w[0]], shard_in)
    tm = min(512, xs.shape[0])
    h = _prenorm(xs, norm_gain + zero(started), tm)

    z, w_pairs = lax.empty((xs.shape[0], IN_W), F32), lax.empty((N_PAIRS, PAIR_W, D_MODEL), BF16)
    early = _gather_from_sibling(early, h)
    early = dict(early, lands=[with_own(early["lands"][0], shard_in), early["lands"][1]])
    z, w_pairs = _fwd_in_pair(h, early["lands"][0], z, w_pairs, place[1:2], tm, "fwd_in_own")
    for j, flip in enumerate((2, 1, 3)):
        early = _gather_from_chip(early, j, z if j != 1 else tie(z, started_late), last=j == 2)
        z, w_pairs = _fwd_in_pair(h, early["lands"][0], z, w_pairs, place[1:2] ^ flip, tm, "fwd_in_chip_" + str(j))
        if j == 0:
            late, started_late = _gather_start(own_late, z)
    conv_full = jnp.transpose(with_own(early["lands"][1], conv_w[0]), (1, 0, 2)).reshape(3, ATTN_W)
    rope, gq2, gk2, conv_wp = _tables(xs.shape[0], q_norm_gain[0], k_norm_gain[0], conv_full)
    late, forwarded = _gather_forward(late, z)
    qn, k2, v2, a, mix, mixt = _forward_attn(z, rope, gq2 + zero(forwarded), gk2, conv_wp, attn_sinks)
    g_out, g_pg, g_pp = (with_own(g, own) for g, own in zip(_gather_wait(late, mix), own_late))
    w_out_f = g_out.reshape(D_MODEL, D_MODEL)
    w_pg_f = g_pg.reshape(D_MODEL, D_MODEL)
    w_pp_f = jnp.transpose(g_pp, (1, 0, 2)).reshape(PLE_DIM, D_MODEL)

    dx1, dx1b, (gw_out, gw_pg, gw_pp), acc_ple, acc_g2 = _forward_out(
        xs, ps, target, mix, mixt, w_out_f, ple_gate_norm_gain, w_pg_f, b_ple_gate, w_pp_f, ple_norm_gain)

    names = ("w_out", "w_ple_gate", "w_ple_proj")
    gw_pp_t = jnp.transpose(gw_pp.reshape(PLE_DIM, N_DEV, PLE_DIM), (1, 0, 2))
    grads = [_by_owner(gw_out.reshape(N_DEV, D_MODEL // N_DEV, D_MODEL)),
             _by_owner(gw_pg.reshape(N_DEV, D_MODEL // N_DEV, D_MODEL)), _by_owner(gw_pp_t)]
    pairs, paired = _exchange_start("pair_start", grads, [(4,) + g.shape[2:] for g in grads], _to_sibling, 1, dx1b)
    dmix = _mm_nt(dx1b, w_out_f, tm, "out_bwd", paired)
    from_sibling = _exchange_wait("pair_wait", pairs, _to_sibling, dmix)
    sums = [_pair_sum(g, r, place, 256, "pair_sum_" + nm) for g, r, nm in zip(pairs["srcs"], from_sibling, names)]
    chips, sent = _exchange_start("chip_start", [pb for pb, _ in sums], [(3,) + pb.shape[1:] for pb, _ in sums],
                                  _to_chips, 3, sums[-1][1])

    dz, gw_in, acc_attn, acc_qk = _backward_attn(
        dmix, h, z, qn, k2, v2, a, rope, gq2, gk2, conv_wp, attn_sinks + zero(sent))

    gw_in_t = [_by_owner(gw_in)]
    pairs_in, paired_in = _exchange_start("pair_start_w_in", gw_in_t, [(4,) + gw_in_t[0].shape[2:]], _to_sibling, 1,
                                          gw_in)
    from_chips = _exchange_wait("chip_wait", chips, _to_chips, gw_in)
    big = {}
    for (_, own), oth, w, m, v, nm in zip(sums, from_chips, (w_out, w_ple_gate, w_ple_proj),
                                          (m_w_out, m_w_ple_gate, m_w_ple_proj),
                                          (v_w_out, v_w_ple_gate, v_w_ple_proj), names):
        big[nm] = [t[None] for t in _adamw(own, oth, w[0], m[0], v[0], 256, "adamw_" + nm, paired_in)]

    (from_sibling_in,) = _exchange_wait("pair_wait_w_in", pairs_in, _to_sibling, tie(*[big[nm][0] for nm in names]))
    pb_in, own_in = _pair_sum(pairs_in["srcs"][0], from_sibling_in, place, SHARD_IN // 2, "pair_sum_w_in")
    chips_in, sent_in = _exchange_start("chip_start_w_in", [pb_in], [(3,) + pb_in.shape[1:]], _to_chips, 3, own_in)
    grad_x, acc_g1 = _in_bwd(dz, w_pairs, xs, dx1, norm_gain + zero(sent_in), tm)
    (from_chips_in,) = _exchange_wait("chip_wait_w_in", chips_in, _to_chips, grad_x)
    big["w_in"] = [t.T[None] for t in _adamw(own_in, from_chips_in, w_in[0].T, m_w_in[0].T, v_w_in[0].T, SHARD_IN // 4,
                                             "adamw_w_in", grad_x)]

    red = _all_reduce_slab(_slab(_small_rows(acc_g1, acc_g2, acc_ple, acc_qk, acc_attn)), "reduce_small")
    loss = jnp.sum(red[ROW_LOSS])
    g_conv = [lax.dynamic_slice(red[ROW_CONV + t:ROW_CONV + t + 1], (0, conv_cols * me), (1, conv_cols))
              for t in range(3)]
    small = [norm_gain, ple_gate_norm_gain, b_ple_gate, ple_norm_gain, q_norm_gain, k_norm_gain, attn_sinks]
    small_m = [m_norm_gain, m_ple_gate_norm_gain, m_b_ple_gate, m_ple_norm_gain, m_q_norm_gain, m_k_norm_gain,
               m_attn_sinks]
    small_v = [v_norm_gain, v_ple_gate_norm_gain, v_b_ple_gate, v_ple_norm_gain, v_q_norm_gain, v_k_norm_gain,
               v_attn_sinks]
    pack = lambda vs, cw: _slab([_row(t) for t in vs] + [_row(cw[0, t]) for t in range(3)])
    g_slab = _slab([red[t:t + 1] for t in range(ROW_CONV)] + [_row(t) for t in g_conv])
    d_slab, m_slab, v_slab = _adamw_slab(pack(small, conv_w), g_slab, pack(small_m, m_conv_w), pack(small_v, v_conv_w))

    def unpack(slab_):
        outs = [slab_[t:t + 1, :w.shape[1]] for t, w in enumerate(small)]
        return outs, slab_[ROW_CONV:ROW_CONV + 3, :conv_cols][None]

    (g_s, g_cv), (d_s, d_cv), (m_s, m_cv), (v_s, v_cv) = (unpack(t) for t in (g_slab, d_slab, m_slab, v_slab))

    def order(sm, cv, k):
        return [sm[0], big["w_in"][k], sm[4], sm[5], sm[6], cv, big["w_out"][k], sm[1], big["w_ple_gate"][k], sm[2],
                big["w_ple_proj"][k], sm[3]]

    return (loss, grad_x[None], *order(g_s, g_cv, 0), *order(d_s, d_cv, 1), *order(m_s, m_cv, 2),
            *order(v_s, v_cv, 3))
```

```python
import jax
import jax.numpy as jnp
from jax import lax
from jax.experimental import pallas as pl
from jax.experimental.pallas import tpu as pltpu

F32, BF16 = jnp.float32, jnp.bfloat16

D_MODEL = 2048
PLE_DIM = 256
ATTN_W = 1024
HEAD = 64
N_Q_HEADS = 16
KV_W = 256
QKV_W = ATTN_W + 2 * KV_W
REST_W = 5 * 1024
IN_W = QKV_W + REST_W
GATE_A0, CONV_B0, CONV_C0, CONV_H0, GATE_C0 = (QKV_W + 1024 * t for t in range(5))
K2_W = 4 * 128
ROT = 16
ROPE_THETA = 500000.0
EPS = 1e-6
NEG_INF = -1e30
BLK = 128
LANES = 128
SUBLANES = 8
N_DEV = 8
SHARD_IN = IN_W // N_DEV
PAIR_W = 2 * SHARD_IN
N_PAIRS = IN_W // PAIR_W
SLAB_ROWS = 16
SUB_ROWS = 128
V7X_VMEM_LIMIT = 52 * 1024 * 1024

ADAM_LR, ADAM_B1, ADAM_B2, ADAM_EPS, ADAM_WD, ADAM_STEP = 0.001, 0.9, 0.999, 1e-08, 0.01, 10
MESH = pl.DeviceIdType.MESH


def _params(*semantics):
    return pltpu.CompilerParams(dimension_semantics=semantics, vmem_limit_bytes=V7X_VMEM_LIMIT)


ANY = pl.BlockSpec(memory_space=pl.ANY)


def _resident(shape):
    return pl.BlockSpec(shape, lambda *_: (0,) * len(shape), pipeline_mode=pl.Buffered(1))


def _dot(a, b):
    return jnp.dot(a, b, preferred_element_type=F32)


def _dot_nt(a, b):
    return lax.dot_general(a, b, (((1,), (1,)), ((), ())), preferred_element_type=F32)


def _rms(xf):
    r = lax.rsqrt(jnp.mean(xf * xf, axis=-1, keepdims=True) + EPS)
    return xf * r, r


def _rms_bwd(dxn, xn, r):
    return r * (dxn - xn * jnp.mean(dxn * xn, axis=-1, keepdims=True))


def _sig(g):
    return jax.nn.sigmoid(g)


def _dsilu(g, sg):
    return sg * (1.0 + g * (1.0 - sg))


def _low_half(shape):
    return lax.broadcasted_iota(jnp.int32, shape, len(shape) - 1) < HEAD


def _half_sums(v):
    lo = _low_half(v.shape)
    s_lo = jnp.sum(jnp.where(lo, v, 0.0), axis=-1, keepdims=True)
    s_hi = jnp.sum(jnp.where(lo, 0.0, v), axis=-1, keepdims=True)
    return jnp.where(lo, s_lo, s_hi)


def _rope(v, a, bm, bp):
    return v * a + pltpu.roll(v, LANES - ROT // 2, 1) * bm + pltpu.roll(v, ROT // 2, 1) * bp


def _rope_t(dy, a, bm, bp):
    return dy * a + pltpu.roll(dy * bm, ROT // 2, 1) + pltpu.roll(dy * bp, LANES - ROT // 2, 1)


def _dup_halves(v):
    lo = _low_half(v.shape)
    a = jnp.where(lo, v, 0.0)
    b = jnp.where(lo, 0.0, v)
    return a + pltpu.roll(a, HEAD, 1), b + pltpu.roll(b, HEAD, 1)


def _rope_tables(s):
    half = ROT // 2
    lane = lax.broadcasted_iota(jnp.int32, (s, LANES), 1) % HEAD
    pos = lax.broadcasted_iota(jnp.int32, (s, LANES), 0).astype(F32)
    inv_freq = jnp.power(jnp.float32(ROPE_THETA), -(lane % half).astype(F32) * 2.0 / ROT)
    ang = pos * inv_freq
    cos, sin = jnp.cos(ang), jnp.sin(ang)
    a = jnp.where(lane < ROT, cos, 1.0)
    bm = jnp.where(lane < half, -sin, 0.0)
    bp = jnp.where((lane >= half) & (lane < ROT), sin, 0.0)
    return a, bm, bp


def _prenorm(x, g1, tm):
    s = x.shape[0]

    def body(x_ref, g_ref, h_ref):
        xn, _ = _rms(x_ref[...])
        h_ref[...] = (xn * g_ref[...]).astype(BF16)

    return pl.pallas_call(
        body, name="prenorm",
        out_shape=jax.ShapeDtypeStruct((s, D_MODEL), BF16),
        grid=(s // tm,),
        in_specs=[pl.BlockSpec((tm, D_MODEL), lambda i: (i, 0)), pl.BlockSpec((1, D_MODEL), lambda i: (0, 0))],
        out_specs=pl.BlockSpec((tm, D_MODEL), lambda i: (i, 0)),
        compiler_params=_params("parallel"))(x, g1)


def _fwd_in_pair(h, shards, z, w_pairs, pair, tm, name):
    s = h.shape[0]

    def body(pair_ref, h_ref, lo_ref, hi_ref, z_in, wp_in, z_ref, wp_ref):
        @pl.when(pl.program_id(0) == 0)
        def _():
            wp_ref[0, 0:SHARD_IN, :] = lo_ref[0]
            wp_ref[0, SHARD_IN:PAIR_W, :] = hi_ref[0]

        z_ref[...] = _dot_nt(h_ref[...], wp_ref[0])

    grid_spec = pltpu.PrefetchScalarGridSpec(
        num_scalar_prefetch=1, grid=(s // tm,),
        in_specs=[pl.BlockSpec((tm, D_MODEL), lambda i, p: (i, 0)),
                  pl.BlockSpec((1, SHARD_IN, D_MODEL), lambda i, p: (2 * p[0], 0, 0)),
                  pl.BlockSpec((1, SHARD_IN, D_MODEL), lambda i, p: (2 * p[0] + 1, 0, 0)), ANY, ANY],
        out_specs=(pl.BlockSpec((tm, PAIR_W), lambda i, p: (i, p[0])),
                   pl.BlockSpec((1, PAIR_W, D_MODEL), lambda i, p: (p[0], 0, 0))))
    return pl.pallas_call(
        body, name=name, grid_spec=grid_spec,
        out_shape=(jax.ShapeDtypeStruct(z.shape, z.dtype), jax.ShapeDtypeStruct(w_pairs.shape, w_pairs.dtype)),
        input_output_aliases={4: 0, 5: 1},
        compiler_params=_params("arbitrary"))(pair, h, shards, shards, z, w_pairs)


def _qk_prep(z, ra, rbm, rbp, gq2, gk2, tm):
    s = z.shape[0]

    def body(z_ref, a_ref, bm_ref, bp_ref, gq_ref, gk_ref, q_ref, k2_ref, v2_ref):
        a, bm, bp = a_ref[...], bm_ref[...], bp_ref[...]
        for r in range(ATTN_W // LANES):
            x = z_ref[:, LANES * r:LANES * (r + 1)]
            rr = lax.rsqrt(_half_sums(x * x) * (1.0 / HEAD) + EPS)
            q_ref[:, LANES * r:LANES * (r + 1)] = _rope(x * rr * gq_ref[...], a, bm, bp).astype(BF16)
        for m in range(KV_W // LANES):
            x = z_ref[:, ATTN_W + LANES * m:ATTN_W + LANES * (m + 1)]
            rr = lax.rsqrt(_half_sums(x * x) * (1.0 / HEAD) + EPS)
            k_lo, k_hi = _dup_halves(_rope(x * rr * gk_ref[...], a, bm, bp))
            k2_ref[:, 2 * LANES * m:2 * LANES * m + LANES] = k_lo.astype(BF16)
            k2_ref[:, 2 * LANES * m + LANES:2 * LANES * (m + 1)] = k_hi.astype(BF16)
            v_lo, v_hi = _dup_halves(z_ref[:, ATTN_W + KV_W + LANES * m:ATTN_W + KV_W + LANES * (m + 1)])
            v2_ref[:, 2 * LANES * m:2 * LANES * m + LANES] = v_lo.astype(BF16)
            v2_ref[:, 2 * LANES * m + LANES:2 * LANES * (m + 1)] = v_hi.astype(BF16)

    row = lambda w: pl.BlockSpec((tm, w), lambda i: (i, 0))
    one = pl.BlockSpec((1, LANES), lambda i: (0, 0))
    return pl.pallas_call(
        body, name="qk_prep",
        out_shape=(jax.ShapeDtypeStruct((s, ATTN_W), BF16), jax.ShapeDtypeStruct((s, K2_W), BF16),
                   jax.ShapeDtypeStruct((s, K2_W), BF16)),
        grid=(s // tm,),
        in_specs=[row(PAIR_W), row(LANES), row(LANES), row(LANES), one, one],
        out_specs=(row(ATTN_W), row(K2_W), row(K2_W)),
        compiler_params=_params("parallel"))(z, ra, rbm, rbp, gq2, gk2)


GROUP = 4


def _window_mask(n):
    row = lax.broadcasted_iota(jnp.int32, (GROUP * BLK, 2 * BLK), 0) % BLK
    col = lax.broadcasted_iota(jnp.int32, (GROUP * BLK, 2 * BLK), 1)
    return (col > row) & (col <= row + BLK) & ((col >= BLK) | (n > 0))


def _stack_heads(pairs, zero):
    lo = _low_half(pairs[0].shape)
    parts = []
    for v in pairs:
        parts += [jnp.where(lo, v, zero), jnp.where(lo, zero, v)]
    return jnp.concatenate(parts, axis=0)


def _unstack_heads(v4):
    lo = _low_half((BLK, LANES))
    return [jnp.where(lo, v4[2 * i * BLK:(2 * i + 1) * BLK], v4[(2 * i + 1) * BLK:(2 * i + 2) * BLK]) for i in range(2)]


def _group_sinks(sink_ref, kvh):
    slot = lax.broadcasted_iota(jnp.int32, (GROUP * BLK, 1), 0) // BLK
    col = jnp.zeros((GROUP * BLK, 1), F32)
    for i in range(GROUP):
        col = jnp.where(slot == i, sink_ref[0, GROUP * kvh + i], col)
    return col, slot


def _head_probs(qm, kw, valid, sink):
    sc = jnp.where(valid, _dot_nt(qm, kw) * (HEAD ** -0.5), NEG_INF)
    mx = jnp.maximum(jnp.max(sc, axis=-1, keepdims=True), sink)
    ex = jnp.exp(sc - mx)
    den = jnp.sum(ex, axis=-1, keepdims=True) + jnp.exp(sink - mx)
    return ex / den, mx, den


def _cols(start, width=ATTN_W):
    return slice(start, start + width)


def _conv_fwd(z_ref, zp_ref, cw_ref, ext_ref, n):
    u = z_ref[:, _cols(CONV_C0)] * z_ref[:, _cols(CONV_H0)]
    pu = zp_ref[:, _cols(CONV_C0)] * zp_ref[:, _cols(CONV_H0)]
    ext_ref[0:SUBLANES, :] = jnp.where(n > 0, pu, 0.0)
    ext_ref[SUBLANES:SUBLANES + BLK, :] = u
    um1 = ext_ref[SUBLANES - 1:SUBLANES - 1 + BLK, :]
    um2 = ext_ref[SUBLANES - 2:SUBLANES - 2 + BLK, :]
    cv = cw_ref[0:1, :] * um2 + cw_ref[1:2, :] * um1 + cw_ref[2:3, :] * u
    return u, um1, um2, cv


def _prev_rows(n):
    return (jnp.maximum(n * (BLK // SUBLANES) - 1, 0), 0)


def _attn_fwd(qn, k2, v2, z, conv_wp, sinks):
    s = qn.shape[0]
    nb = s // BLK

    def body(sink_ref, q_ref, kc_ref, kp_ref, vc_ref, vp_ref, z_ref, zp_ref, cw_ref, a_ref, mix_ref, mixt_ref,
             ext_ref):
        n = pl.program_id(0)
        valid = _window_mask(n)
        for kvh in range(K2_W // LANES):
            cols = slice(LANES * kvh, LANES * (kvh + 1))
            kw = jnp.concatenate([kp_ref[:, cols], kc_ref[:, cols]], axis=0)
            vw = jnp.concatenate([vp_ref[:, cols], vc_ref[:, cols]], axis=0)
            blocks = [slice(LANES * r, LANES * (r + 1)) for r in (2 * kvh, 2 * kvh + 1)]
            q4 = _stack_heads([q_ref[:, rc] for rc in blocks], jnp.zeros((BLK, LANES), BF16))
            p, _, _ = _head_probs(q4, kw, valid, _group_sinks(sink_ref, kvh)[0])
            for rc, a in zip(blocks, _unstack_heads(_dot(p.astype(BF16), vw))):
                a_ref[:, rc] = a
                g = z_ref[:, _cols(GATE_A0 + rc.start, LANES)]
                mix_ref[:, rc] = (a * (g * _sig(g))).astype(BF16)
        _, _, _, cv = _conv_fwd(z_ref, zp_ref, cw_ref, ext_ref, n)
        gc = z_ref[:, _cols(GATE_C0)]
        mix_ref[:, ATTN_W:D_MODEL] = (z_ref[:, _cols(CONV_B0)] * cv * (gc * _sig(gc))).astype(BF16)
        mixt_ref[...] = mix_ref[...].T

    cur = lambda w: pl.BlockSpec((BLK, w), lambda n: (n, 0))
    prev = lambda w: pl.BlockSpec((BLK, w), lambda n: (jnp.maximum(n - 1, 0), 0))
    return pl.pallas_call(
        body, name="attn_fwd",
        out_shape=(jax.ShapeDtypeStruct((s, ATTN_W), F32), jax.ShapeDtypeStruct((s, D_MODEL), BF16),
                   jax.ShapeDtypeStruct((D_MODEL, s), BF16)),
        grid=(nb,),
        in_specs=[pl.BlockSpec(memory_space=pltpu.SMEM),
                  cur(ATTN_W), cur(K2_W), prev(K2_W), cur(K2_W), prev(K2_W), cur(IN_W),
                  pl.BlockSpec((SUBLANES, IN_W), _prev_rows),
                  pl.BlockSpec((SUBLANES, ATTN_W), lambda n: (0, 0))],
        out_specs=(cur(ATTN_W), cur(D_MODEL), pl.BlockSpec((D_MODEL, BLK), lambda n: (0, n))),
        scratch_shapes=[pltpu.VMEM((BLK + 2 * SUBLANES, ATTN_W), F32)],
        compiler_params=_params("parallel"))(sinks, qn, k2, k2, v2, v2, z, z, conv_wp)


def _fwd_out(mix, w_out, x, g2, tm):
    s = x.shape[0]

    def body(m_ref, w_ref, x_ref, g_ref, x1_ref, h_ref, ht_ref):
        x1 = x_ref[...] + _dot(m_ref[...], w_ref[...])
        x1_ref[...] = x1
        xn, _ = _rms(x1)
        h = (xn * g_ref[...]).astype(BF16)
        h_ref[...] = h
        ht_ref[...] = h.T

    row = pl.BlockSpec((tm, D_MODEL), lambda i: (i, 0))
    return pl.pallas_call(
        body, name="fwd_out",
        out_shape=(jax.ShapeDtypeStruct((s, D_MODEL), F32), jax.ShapeDtypeStruct((s, D_MODEL), BF16),
                   jax.ShapeDtypeStruct((D_MODEL, s), BF16)),
        grid=(s // tm,),
        in_specs=[row, _resident((D_MODEL, D_MODEL)), row, pl.BlockSpec((1, D_MODEL), lambda i: (0, 0))],
        out_specs=(row, row, pl.BlockSpec((D_MODEL, tm), lambda i: (0, i))),
        compiler_params=_params("parallel"))(mix, w_out, x, g2)


def _ple(hn2, w_pg, b_pg, p, w_pp, g3, x1, target, tm):
    s = x1.shape[0]

    def body(h_ref, wg_ref, b_ref, p_ref, wp_ref, g3_ref, x1_ref, t_ref, dy_ref, dgp_ref, dt_ref, pt_ref, acc_ref):
        gate = _sig(_dot(h_ref[...], wg_ref[...]) + b_ref[...])
        pb = p_ref[...].astype(BF16)
        pt_ref[...] = pb.T
        t = _dot(pb, wp_ref[...])
        tn, r3 = _rms(t)
        e = tn * g3_ref[...]
        diff = x1_ref[...] + gate * e - t_ref[...]
        dy = diff * (1.0 / D_MODEL)
        dy_ref[...] = dy
        dgp = dy * e * (gate * (1.0 - gate))
        dgp_ref[...] = dgp.astype(BF16)
        de = dy * gate
        dt_ref[...] = _rms_bwd(de * g3_ref[...], tn, r3).astype(BF16)

        @pl.when(pl.program_id(0) == 0)
        def _():
            acc_ref[...] = jnp.zeros_like(acc_ref)

        acc_ref[0:1, :] += jnp.sum(dgp, axis=0, keepdims=True)
        acc_ref[1:2, :] += jnp.sum(de * tn, axis=0, keepdims=True)
        acc_ref[2:3, :] += jnp.sum(diff * diff, axis=0, keepdims=True) * (0.5 / D_MODEL)

    row = pl.BlockSpec((tm, D_MODEL), lambda i: (i, 0))
    vec = pl.BlockSpec((1, D_MODEL), lambda i: (0, 0))
    return pl.pallas_call(
        body, name="ple",
        out_shape=(jax.ShapeDtypeStruct((s, D_MODEL), F32), jax.ShapeDtypeStruct((s, D_MODEL), BF16),
                   jax.ShapeDtypeStruct((s, D_MODEL), BF16), jax.ShapeDtypeStruct((PLE_DIM, s), BF16),
                   jax.ShapeDtypeStruct((SUBLANES, D_MODEL), F32)),
        grid=(s // tm,),
        in_specs=[row, _resident((D_MODEL, D_MODEL)), vec, pl.BlockSpec((tm, PLE_DIM), lambda i: (i, 0)),
                  _resident((PLE_DIM, D_MODEL)), vec, row, row],
        out_specs=(row, row, row, pl.BlockSpec((PLE_DIM, tm), lambda i: (0, i)),
                   pl.BlockSpec((SUBLANES, D_MODEL), lambda i: (0, 0))),
        compiler_params=_params("arbitrary"))(hn2, w_pg, b_pg, p, w_pp, g3, x1, target)


def _gate_bwd(dgp, w_pg, x1, dy, g2, tm):
    s = x1.shape[0]

    def body(d_ref, w_ref, x1_ref, dy_ref, g_ref, dx_ref, dxb_ref, acc_ref):
        dh = _dot_nt(d_ref[...], w_ref[...])
        xn, r = _rms(x1_ref[...])
        dx1 = dy_ref[...] + _rms_bwd(dh * g_ref[...], xn, r)
        dx_ref[...] = dx1
        dxb_ref[...] = dx1.astype(BF16)

        @pl.when(pl.program_id(0) == 0)
        def _():
            acc_ref[...] = jnp.zeros_like(acc_ref)

        acc_ref[0:1, :] += jnp.sum(dh * xn, axis=0, keepdims=True)

    row = pl.BlockSpec((tm, D_MODEL), lambda i: (i, 0))
    return pl.pallas_call(
        body, name="gate_bwd",
        out_shape=(jax.ShapeDtypeStruct((s, D_MODEL), F32), jax.ShapeDtypeStruct((s, D_MODEL), BF16),
                   jax.ShapeDtypeStruct((SUBLANES, D_MODEL), F32)),
        grid=(s // tm,),
        in_specs=[row, _resident((D_MODEL, D_MODEL)), row, row, pl.BlockSpec((1, D_MODEL), lambda i: (0, 0))],
        out_specs=(row, row, pl.BlockSpec((SUBLANES, D_MODEL), lambda i: (0, 0))),
        compiler_params=_params("arbitrary"))(dgp, w_pg, x1, dy, g2)


def _mm_nt(a, b, tm, name, after):
    m, k = a.shape
    n = b.shape[0]

    def body(a_ref, b_ref, after_ref, o_ref):
        o_ref[...] = _dot_nt(a_ref[...], b_ref[...])

    return pl.pallas_call(
        body, name=name,
        out_shape=jax.ShapeDtypeStruct((m, n), F32),
        grid=(m // tm,),
        in_specs=[pl.BlockSpec((tm, k), lambda i: (i, 0)), _resident((n, k)), ANY],
        out_specs=pl.BlockSpec((tm, n), lambda i: (i, 0)),
        compiler_params=_params("parallel"))(a, b, after)


def _attn_bwd(qn, k2, v2, a, z, dmix, conv_wp, sinks):
    s = qn.shape[0]
    nb = s // BLK

    def body(sink_ref, q_ref, kc_ref, kp_ref, vc_ref, vp_ref, a_ref, z_ref, zp_ref, zn_ref, dm_ref, dmn_ref,
             cw_ref, dq_ref, dkc_ref, dkp_ref, dvc_ref, dvp_ref, dz_ref, dzt_ref, acc_ref, ext_ref):
        n = pl.program_id(0)
        valid = _window_mask(n)
        lane = lax.broadcasted_iota(jnp.int32, (1, ATTN_W), 1)

        @pl.when(n == 0)
        def _():
            acc_ref[...] = jnp.zeros_like(acc_ref)

        dz_ref[:, 0:QKV_W] = jnp.zeros((BLK, QKV_W), BF16)
        dsink = jnp.zeros((1, ATTN_W), F32)
        for kvh in range(K2_W // LANES):
            cols = slice(LANES * kvh, LANES * (kvh + 1))
            kw = jnp.concatenate([kp_ref[:, cols], kc_ref[:, cols]], axis=0)
            vw = jnp.concatenate([vp_ref[:, cols], vc_ref[:, cols]], axis=0)
            blocks = [slice(LANES * r, LANES * (r + 1)) for r in (2 * kvh, 2 * kvh + 1)]
            das, avs = [], []
            for rc in blocks:
                g = z_ref[:, _cols(GATE_A0 + rc.start, LANES)]
                sg = _sig(g)
                dm = dm_ref[:, rc]
                av = a_ref[:, rc]
                das.append(dm * (g * sg))
                avs += [av, av]
                dz_ref[:, _cols(GATE_A0 + rc.start, LANES)] = (dm * av * _dsilu(g, sg)).astype(BF16)
            q4 = _stack_heads([q_ref[:, rc] for rc in blocks], jnp.zeros((BLK, LANES), BF16))
            sink, slot = _group_sinks(sink_ref, kvh)
            p, mx, den = _head_probs(q4, kw, valid, sink)
            do4 = _stack_heads(das, 0.0)
            delta = jnp.sum(do4 * jnp.concatenate(avs, axis=0), axis=-1, keepdims=True)
            dob = do4.astype(BF16)
            ds = p * (_dot_nt(dob, vw) - delta) * (HEAD ** -0.5)
            for rc, dq in zip(blocks, _unstack_heads(_dot(ds.astype(BF16), kw))):
                dq_ref[:, rc] = dq
            dk2 = _dot(ds.T.astype(BF16), q4)
            dv2 = _dot(p.T.astype(BF16), dob)
            dkp_ref[:, cols] = dk2[0:BLK]
            dkc_ref[:, cols] = dk2[BLK:2 * BLK]
            dvp_ref[:, cols] = dv2[0:BLK]
            dvc_ref[:, cols] = dv2[BLK:2 * BLK]
            dsk = jnp.exp(sink - mx) / den * delta
            for i in range(GROUP):
                dsink = dsink - jnp.where(lane == GROUP * kvh + i,
                                          jnp.sum(jnp.where(slot == i, dsk, 0.0), axis=0, keepdims=True), 0.0)
        acc_ref[0:1, :] += dsink

        u, um1, um2, cv = _conv_fwd(z_ref, zp_ref, cw_ref, ext_ref, n)
        cb = z_ref[:, _cols(CONV_B0)]
        gc = z_ref[:, _cols(GATE_C0)]
        sgc = _sig(gc)
        dmc = dm_ref[:, ATTN_W:D_MODEL]
        t = dmc * (gc * sgc)
        dcv = t * cb
        dz_ref[:, _cols(CONV_B0)] = (t * cv).astype(BF16)
        dz_ref[:, _cols(GATE_C0)] = (dmc * cb * cv * _dsilu(gc, sgc)).astype(BF16)
        gcn = zn_ref[:, _cols(GATE_C0)]
        dcvn = dmn_ref[:, ATTN_W:D_MODEL] * (gcn * _sig(gcn)) * zn_ref[:, _cols(CONV_B0)]
        ext_ref[0:BLK, :] = dcv
        ext_ref[BLK:BLK + SUBLANES, :] = jnp.where(n < nb - 1, dcvn, 0.0)
        du = (cw_ref[2:3, :] * dcv + cw_ref[1:2, :] * ext_ref[1:1 + BLK, :]
              + cw_ref[0:1, :] * ext_ref[2:2 + BLK, :])
        dz_ref[:, _cols(CONV_C0)] = (du * z_ref[:, _cols(CONV_H0)]).astype(BF16)
        dz_ref[:, _cols(CONV_H0)] = (du * z_ref[:, _cols(CONV_C0)]).astype(BF16)
        acc_ref[1:2, :] += jnp.sum(dcv * um2, axis=0, keepdims=True)
        acc_ref[2:3, :] += jnp.sum(dcv * um1, axis=0, keepdims=True)
        acc_ref[3:4, :] += jnp.sum(dcv * u, axis=0, keepdims=True)
        dzt_ref[...] = dz_ref[...].T

    cur = lambda w: pl.BlockSpec((BLK, w), lambda n: (n, 0))
    prev = lambda w: pl.BlockSpec((BLK, w), lambda n: (jnp.maximum(n - 1, 0), 0))
    nxt = lambda w: pl.BlockSpec(
        (SUBLANES, w), lambda n: (jnp.minimum((n + 1) * (BLK // SUBLANES), nb * (BLK // SUBLANES) - 1), 0))
    f32 = lambda w: jax.ShapeDtypeStruct((s, w), F32)
    return pl.pallas_call(
        body, name="attn_bwd",
        out_shape=(f32(ATTN_W), f32(K2_W), f32(K2_W), f32(K2_W), f32(K2_W),
                   jax.ShapeDtypeStruct((s, IN_W), BF16), jax.ShapeDtypeStruct((IN_W, s), BF16),
                   jax.ShapeDtypeStruct((SUBLANES, ATTN_W), F32)),
        grid=(nb,),
        in_specs=[pl.BlockSpec(memory_space=pltpu.SMEM),
                  cur(ATTN_W), cur(K2_W), prev(K2_W), cur(K2_W), prev(K2_W), cur(ATTN_W), cur(IN_W),
                  pl.BlockSpec((SUBLANES, IN_W), _prev_rows), nxt(IN_W), cur(D_MODEL), nxt(D_MODEL),
                  pl.BlockSpec((SUBLANES, ATTN_W), lambda n: (0, 0))],
        out_specs=(cur(ATTN_W), cur(K2_W), cur(K2_W), cur(K2_W), cur(K2_W), cur(IN_W),
                   pl.BlockSpec((IN_W, BLK), lambda n: (0, n)), pl.BlockSpec((SUBLANES, ATTN_W), lambda n: (0, 0))),
        scratch_shapes=[pltpu.VMEM((BLK + 2 * SUBLANES, ATTN_W), F32)],
        compiler_params=_params("arbitrary"))(sinks, qn, k2, k2, v2, v2, a, z, z, z, dmix, dmix, conv_wp)


def _qkv_bwd(z, dz, dzt, dq, dkc, dkp, dvc, dvp, ra, rbm, rbp, gq2, gk2):
    s = z.shape[0]
    nb = s // BLK

    def body(z_ref, dz_in, dzt_in, dq_ref, dkc_ref, dkp_ref, dvc_ref, dvp_ref, a_ref, bm_ref, bp_ref, gq_ref, gk_ref,
             dz_ref, dzt_ref, acc_ref):
        n = pl.program_id(0)
        a, bm, bp = a_ref[...], bm_ref[...], bp_ref[...]
        lo = _low_half((BLK, LANES))
        last = n == nb - 1

        @pl.when(n == 0)
        def _():
            acc_ref[...] = jnp.zeros_like(acc_ref)

        def norm_bwd(x, dy, gain):
            rr = lax.rsqrt(_half_sums(x * x) * (1.0 / HEAD) + EPS)
            xh = x * rr
            dxg = _rope_t(dy, a, bm, bp)
            dxh = dxg * gain
            dx = rr * (dxh - xh * (_half_sums(dxh * xh) * (1.0 / HEAD)))
            return dx, jnp.sum(dxg * xh, axis=0, keepdims=True)

        def folded(cur_ref, prev_ref, m):
            parts = []
            for h in (2 * m, 2 * m + 1):
                v = cur_ref[:, LANES * h:LANES * (h + 1)] + jnp.where(
                    last, 0.0, prev_ref[:, LANES * h:LANES * (h + 1)])
                parts.append(v + pltpu.roll(v, HEAD, 1))
            return jnp.where(lo, parts[0], parts[1])

        gq_acc = jnp.zeros((1, LANES), F32)
        for r in range(ATTN_W // LANES):
            rc = slice(LANES * r, LANES * (r + 1))
            dx, gg = norm_bwd(z_ref[:, rc], dq_ref[:, rc], gq_ref[...])
            dz_ref[:, rc] = dx.astype(BF16)
            gq_acc = gq_acc + gg
        acc_ref[0:1, :] += gq_acc
        gk_acc = jnp.zeros((1, LANES), F32)
        for m in range(KV_W // LANES):
            kc = slice(ATTN_W + LANES * m, ATTN_W + LANES * (m + 1))
            dx, gg = norm_bwd(z_ref[:, kc], folded(dkc_ref, dkp_ref, m), gk_ref[...])
            dz_ref[:, kc] = dx.astype(BF16)
            gk_acc = gk_acc + gg
            vc = slice(ATTN_W + KV_W + LANES * m, ATTN_W + KV_W + LANES * (m + 1))
            dz_ref[:, vc] = folded(dvc_ref, dvp_ref, m).astype(BF16)
        acc_ref[1:2, :] += gk_acc
        dzt_ref[...] = dz_ref[...].T

    cur = lambda w: pl.BlockSpec((BLK, w), lambda n: (n, 0))
    nxt = lambda w: pl.BlockSpec((BLK, w), lambda n: (jnp.minimum(n + 1, nb - 1), 0))
    one = pl.BlockSpec((1, LANES), lambda n: (0, 0))
    return pl.pallas_call(
        body, name="qkv_bwd",
        out_shape=(jax.ShapeDtypeStruct(dz.shape, dz.dtype), jax.ShapeDtypeStruct(dzt.shape, dzt.dtype),
                   jax.ShapeDtypeStruct((SUBLANES, LANES), F32)),
        grid=(nb,),
        in_specs=[cur(PAIR_W), ANY, ANY, cur(ATTN_W), cur(K2_W), nxt(K2_W), cur(K2_W), nxt(K2_W),
                  cur(LANES), cur(LANES), cur(LANES), one, one],
        out_specs=(cur(QKV_W), pl.BlockSpec((QKV_W, BLK), lambda n: (0, n)),
                   pl.BlockSpec((SUBLANES, LANES), lambda n: (0, 0))),
        input_output_aliases={1: 0, 2: 1},
        compiler_params=_params("arbitrary"))(z, dz, dzt, dq, dkc, dkp, dvc, dvp, ra, rbm, rbp, gq2, gk2)


def _in_bwd(dz, w_pairs, x, dx1, g1, tm):
    s = x.shape[0]

    def body(d_ref, w_ref, x_hbm, dx1_hbm, g_ref, gx_ref, acc_ref, x_buf, dx1_buf, sems):
        i, k = pl.program_id(0), pl.program_id(1)
        rows = pl.ds(pl.multiple_of(i * tm, tm), tm)
        fetch = [pltpu.make_async_copy(x_hbm.at[rows], x_buf, sems.at[0]),
                 pltpu.make_async_copy(dx1_hbm.at[rows], dx1_buf, sems.at[1])]
        sub = min(SUB_ROWS, tm)
        blocks = [slice(r, r + sub) for r in range(0, tm, sub)]

        @pl.when(k == 0)
        def _():
            for cp in fetch:
                cp.start()
            gx_ref[...] = _dot(d_ref[...], w_ref[0])

        @pl.when(k > 0)
        def _():
            gx_ref[...] += _dot(d_ref[...], w_ref[0])

        @pl.when((i == 0) & (k == 0))
        def _():
            acc_ref[...] = jnp.zeros_like(acc_ref)

        @pl.when(k == N_PAIRS - 1)
        def _():
            for cp in fetch:
                cp.wait()
            for rb in blocks:
                dh = gx_ref[rb, :]
                xn, r = _rms(x_buf[rb, :])
                gx_ref[rb, :] = dx1_buf[rb, :] + _rms_bwd(dh * g_ref[...], xn, r)
                acc_ref[0:1, :] += jnp.sum(dh * xn, axis=0, keepdims=True)

    return pl.pallas_call(
        body, name="in_bwd",
        out_shape=(jax.ShapeDtypeStruct((s, D_MODEL), F32), jax.ShapeDtypeStruct((SUBLANES, D_MODEL), F32)),
        grid=(s // tm, N_PAIRS),
        in_specs=[pl.BlockSpec((tm, PAIR_W), lambda i, k: (i, k)),
                  pl.BlockSpec((1, PAIR_W, D_MODEL), lambda i, k: (k, 0, 0)),
                  ANY, ANY, pl.BlockSpec((1, D_MODEL), lambda i, k: (0, 0))],
        out_specs=(pl.BlockSpec((tm, D_MODEL), lambda i, k: (i, 0)),
                   pl.BlockSpec((SUBLANES, D_MODEL), lambda i, k: (0, 0))),
        scratch_shapes=[pltpu.VMEM((tm, D_MODEL), F32), pltpu.VMEM((tm, D_MODEL), F32),
                        pltpu.SemaphoreType.DMA((2,))],
        compiler_params=_params("arbitrary", "arbitrary"))(dz, w_pairs, x, dx1, g1)


def _mm_grad(at, bs, tn, name):
    m, kdim = at.shape
    nblk = [b.shape[1] // tn for b in bs]
    starts = [sum(nblk[:t]) for t in range(len(bs))]

    def body(a_ref, *refs):
        b_refs, o_ref = refs[:len(bs)], refs[len(bs)]
        j = pl.program_id(0)
        for t, b_ref in enumerate(b_refs):
            @pl.when((j >= starts[t]) & (j < starts[t] + nblk[t]))
            def _():
                o_ref[...] = _dot(a_ref[...], b_ref[...]).astype(BF16)

    def b_spec(t):
        return pl.BlockSpec((kdim, tn), lambda j: (0, jnp.clip(j - starts[t], 0, nblk[t] - 1)))

    return pl.pallas_call(
        body, name=name,
        out_shape=jax.ShapeDtypeStruct((m, sum(nblk) * tn), BF16),
        grid=(sum(nblk),),
        in_specs=[_resident((m, kdim))] + [b_spec(t) for t in range(len(bs))],
        out_specs=pl.BlockSpec((m, tn), lambda j: (0, j)),
        compiler_params=_params("parallel"))(at, *bs)


def _grad_w_in(dzt, h):
    kdim = h.shape[0]

    def body(d_ref, h_ref, o_ref):
        o_ref[0] = _dot(d_ref[...], h_ref[...]).astype(BF16)

    return pl.pallas_call(
        body, name="grad_w_in",
        out_shape=jax.ShapeDtypeStruct((N_DEV, SHARD_IN, D_MODEL), BF16),
        grid=(N_DEV,),
        in_specs=[pl.BlockSpec((SHARD_IN, kdim), lambda j: (j, 0)), _resident((kdim, D_MODEL))],
        out_specs=pl.BlockSpec((1, SHARD_IN, D_MODEL), lambda j: (j, 0, 0)),
        compiler_params=_params("parallel"))(dzt, h)


def _place():
    return lax.axis_index("x"), lax.axis_index("y"), lax.axis_index("c")


ROW_CONV, ROW_LOSS = 7, 10


def _reduce_small(acc_g1, acc_g2, acc_ple, acc_qk, acc_attn):
    def body(g1_ref, g2_ref, ple_ref, qk_ref, attn_ref, out_ref, slab_ref, gath_ref, send_sems, recv_sems):
        x, y, c = _place()
        me = 4 * x + 2 * y + c
        slab_ref[...] = jnp.zeros_like(slab_ref)
        slab_ref[0:1, :] = g1_ref[0:1, :]
        slab_ref[1:2, :] = g2_ref[0:1, :]
        slab_ref[2:4, :] = ple_ref[0:2, :]
        qk = qk_ref[0:2, :]
        slab_ref[4:6, 0:LANES] = jnp.where(_low_half(qk.shape), qk + pltpu.roll(qk, HEAD, 1), 0.0)
        lane = lax.broadcasted_iota(jnp.int32, (1, LANES), 1)
        slab_ref[6:7, 0:LANES] = jnp.where(lane < N_Q_HEADS, attn_ref[0:1, 0:LANES], 0.0)
        slab_ref[ROW_CONV:ROW_CONV + 3, 0:ATTN_W] = attn_ref[1:4, :]
        slab_ref[ROW_LOSS:ROW_LOSS + 1, :] = ple_ref[2:3, :]
        gath_ref[me] = slab_ref[...]
        copies = []
        for k in range(1, N_DEV):
            peer = (x ^ (k >> 2), y ^ ((k >> 1) & 1), c ^ (k & 1))
            copies.append(pltpu.make_async_remote_copy(
                src_ref=slab_ref, dst_ref=gath_ref.at[me], send_sem=send_sems.at[k - 1],
                recv_sem=recv_sems.at[k - 1], device_id=peer, device_id_type=MESH))
        for cp in copies:
            cp.start()
        for cp in copies:
            cp.wait_recv()
        for cp in copies:
            cp.wait_send()
        total = gath_ref[0]
        for d in range(1, N_DEV):
            total = total + gath_ref[d]
        out_ref[...] = total

    vmem = pl.BlockSpec(memory_space=pltpu.VMEM)
    return pl.pallas_call(
        body, name="reduce_small",
        out_shape=jax.ShapeDtypeStruct((SLAB_ROWS, D_MODEL), F32),
        in_specs=[vmem] * 5, out_specs=vmem,
        scratch_shapes=[pltpu.VMEM((SLAB_ROWS, D_MODEL), F32), pltpu.VMEM((N_DEV, SLAB_ROWS, D_MODEL), F32),
                        pltpu.SemaphoreType.DMA((N_DEV - 1,)), pltpu.SemaphoreType.DMA((N_DEV - 1,))])(
            acc_g1, acc_g2, acc_ple, acc_qk, acc_attn)


def _pair_sum(g, r, place, tr, name):
    _, _, rows, cols = g.shape

    def body(place_ref, g_ref, r_ref, pb_ref, own_ref):
        tot = g_ref[0, 0].astype(F32) + r_ref[0].astype(F32)
        pb_ref[0] = tot.astype(BF16)

        @pl.when(pl.program_id(1) == place_ref[1])
        def _():
            own_ref[...] = tot

    grid_spec = pltpu.PrefetchScalarGridSpec(
        num_scalar_prefetch=1, grid=(rows // tr, 4),
        in_specs=[pl.BlockSpec((1, 1, tr, cols), lambda i, q, place_ref: (q, place_ref[0], i, 0)),
                  pl.BlockSpec((1, tr, cols), lambda i, q, place_ref: (q, i, 0))],
        out_specs=(pl.BlockSpec((1, tr, cols), lambda i, q, place_ref: (q, i, 0)),
                   pl.BlockSpec((tr, cols), lambda i, q, place_ref: (i, 0))))
    return pl.pallas_call(
        body, name=name, grid_spec=grid_spec,
        out_shape=(jax.ShapeDtypeStruct((4, rows, cols), BF16), jax.ShapeDtypeStruct((rows, cols), F32)),
        compiler_params=_params("arbitrary", "arbitrary"))(place, g, r)


HBM = pl.BlockSpec(memory_space=pltpu.HBM)
SEM = pl.BlockSpec(memory_space=pltpu.SEMAPHORE)
SIDE_EFFECT = pltpu.CompilerParams(has_side_effects=pltpu.SideEffectType.DATAFLOW_SIDE_EFFECTING)
TOKEN = jax.ShapeDtypeStruct((SUBLANES, LANES), F32)


def _hbm(a):
    return pltpu.with_memory_space_constraint(a, pltpu.HBM)


def _hbm_like(arrays):
    return tuple(pltpu.HBM(a.shape, a.dtype) for a in arrays)


def _block_of(px, py, pc):
    return 4 * px + 2 * py + pc


def _gather_start(shards, after):
    na = len(shards)
    lands = [_hbm(lax.empty((N_DEV,) + a.shape, a.dtype)) for a in shards]

    def body(*refs):
        ins, land = refs[:na], refs[na:2 * na]
        send_sems, recv_ici, recv_d2d = refs[2 * na + 1:2 * na + 4]
        token = refs[-1]
        x, y, c = _place()
        for k, peer in enumerate([(x, y, 1 - c), (1 - x, y, c), (x, 1 - y, c), (1 - x, 1 - y, c)]):
            for t in range(na):
                pltpu.make_async_remote_copy(
                    src_ref=ins[t], dst_ref=land[t].at[_block_of(x, y, c)], send_sem=send_sems.at[4 * t + k],
                    recv_sem=recv_d2d.at[4 * t] if k == 0 else recv_ici.at[3 * t + k - 1],
                    device_id=peer, device_id_type=MESH).start()
        token[...] = jnp.zeros_like(token)

    out = pl.pallas_call(
        body, name="gather_start",
        out_shape=(pltpu.SemaphoreType.DMA((4 * na,)), pltpu.SemaphoreType.DMA((3 * na,)),
                   pltpu.SemaphoreType.DMA((4 * na,)), *_hbm_like(lands), TOKEN),
        in_specs=[ANY] * na + [HBM] * na + [ANY],
        out_specs=(SEM, SEM, SEM, *[HBM] * na, pl.BlockSpec(memory_space=pltpu.VMEM)),
        input_output_aliases={na + i: 3 + i for i in range(na)},
        compiler_params=SIDE_EFFECT)(*shards, *lands, after)
    send_sems, recv_ici, recv_d2d = out[:3]
    state = dict(send=send_sems, ici=recv_ici, d2d=recv_d2d, shards=list(shards), lands=out[3:3 + na])
    return state, out[-1]


def _gather_forward(state, after):
    lands = state["lands"]
    na = len(lands)

    def body(*refs):
        land = refs[:na]
        recv_ici, recv_d2d = refs[na], refs[na + 1]
        fwd_sems, token = refs[-2], refs[-1]
        x, y, c = _place()
        for j, chip in enumerate([(1 - x, y), (x, 1 - y), (1 - x, 1 - y)]):
            for t in range(na):
                blk = land[t].at[_block_of(*chip, c)]
                pltpu.make_async_remote_copy(
                    src_ref=blk, dst_ref=blk, send_sem=fwd_sems.at[3 * t + j], recv_sem=recv_ici.at[3 * t + j],
                    device_id=(x, y, c), device_id_type=MESH).wait_recv()
                pltpu.make_async_remote_copy(
                    src_ref=blk, dst_ref=blk, send_sem=fwd_sems.at[3 * t + j], recv_sem=recv_d2d.at[4 * t + 1 + j],
                    device_id=(x, y, 1 - c), device_id_type=MESH).start()
        token[...] = jnp.zeros_like(token)

    out = pl.pallas_call(
        body, name="gather_forward",
        out_shape=(*_hbm_like(lands), pltpu.SemaphoreType.DMA((3 * na,)), TOKEN),
        in_specs=[HBM] * na + [SEM, SEM, ANY],
        out_specs=(*[HBM] * na, SEM, pl.BlockSpec(memory_space=pltpu.VMEM)),
        input_output_aliases={i: i for i in range(na)},
        compiler_params=SIDE_EFFECT)(*lands, state["ici"], state["d2d"], after)
    return dict(state, lands=out[:na], fwd=out[na]), out[-1]


def _gather_wait(state, after):
    shards, lands = state["shards"], state["lands"]
    na = len(lands)

    def body(*refs):
        ins, land = refs[:na], refs[na:2 * na]
        send_sems, fwd_sems, recv_d2d = refs[2 * na:2 * na + 3]
        x, y, c = _place()
        chips = [(1 - x, y), (x, 1 - y), (1 - x, 1 - y)]
        for t in range(na):
            mine = land[t].at[_block_of(x, y, c)]
            for k in range(4):
                pltpu.make_async_remote_copy(
                    src_ref=ins[t], dst_ref=mine, send_sem=send_sems.at[4 * t + k], recv_sem=recv_d2d.at[4 * t],
                    device_id=(x, y, c), device_id_type=MESH).wait_send()
            for j, chip in enumerate(chips):
                blk = land[t].at[_block_of(*chip, c)]
                pltpu.make_async_remote_copy(
                    src_ref=blk, dst_ref=blk, send_sem=fwd_sems.at[3 * t + j], recv_sem=recv_d2d.at[4 * t + 1 + j],
                    device_id=(x, y, c), device_id_type=MESH).wait_send()
            for k, blk_id in enumerate([_block_of(x, y, 1 - c)] + [_block_of(*chip, 1 - c) for chip in chips]):
                blk = land[t].at[blk_id]
                pltpu.make_async_remote_copy(
                    src_ref=blk, dst_ref=blk, send_sem=send_sems.at[4 * t], recv_sem=recv_d2d.at[4 * t + k],
                    device_id=(x, y, c), device_id_type=MESH).wait_recv()

    out = pl.pallas_call(
        body, name="gather_wait",
        out_shape=_hbm_like(lands),
        in_specs=[ANY] * na + [HBM] * na + [SEM, SEM, SEM, ANY],
        out_specs=tuple([HBM] * na),
        input_output_aliases={na + i: i for i in range(na)},
        compiler_params=SIDE_EFFECT)(*shards, *lands, state["send"], state["fwd"], state["d2d"], after)
    return out


def _gather_from_sibling(state, after):
    lands = state["lands"]
    na = len(lands)

    def body(*refs):
        land, recv_d2d = refs[:na], refs[na]
        x, y, c = _place()
        for t in range(na):
            blk = land[t].at[_block_of(x, y, 1 - c)]
            pltpu.make_async_remote_copy(src_ref=blk, dst_ref=blk, send_sem=recv_d2d.at[4 * t],
                                         recv_sem=recv_d2d.at[4 * t], device_id=(x, y, c),
                                         device_id_type=MESH).wait_recv()

    out = pl.pallas_call(
        body, name="gather_from_sibling", out_shape=_hbm_like(lands),
        in_specs=[HBM] * na + [SEM, ANY], out_specs=tuple([HBM] * na),
        input_output_aliases={i: i for i in range(na)},
        compiler_params=SIDE_EFFECT)(*lands, state["d2d"], after)
    return dict(state, lands=list(out))


def _gather_from_chip(state, j, after, last):
    shards, lands = state["shards"], state["lands"]
    na = len(lands)

    def chip_blocks(land_ref):
        x, y, c = _place()
        chip = [(1 - x, y), (x, 1 - y), (1 - x, 1 - y)][j]
        return (x, y, c), land_ref.at[_block_of(*chip, c)], land_ref.at[_block_of(*chip, 1 - c)]

    def forward(*refs):
        land, recv_ici, recv_d2d, fwd_sems = refs[:na], refs[na], refs[na + 1], refs[-1]
        for t in range(na):
            (x, y, c), mine, _ = chip_blocks(land[t])
            pltpu.make_async_remote_copy(src_ref=mine, dst_ref=mine, send_sem=fwd_sems.at[t],
                                         recv_sem=recv_ici.at[3 * t + j], device_id=(x, y, c),
                                         device_id_type=MESH).wait_recv()
            pltpu.make_async_remote_copy(src_ref=mine, dst_ref=mine, send_sem=fwd_sems.at[t],
                                         recv_sem=recv_d2d.at[4 * t + 1 + j], device_id=(x, y, 1 - c),
                                         device_id_type=MESH).start()

    out = pl.pallas_call(
        forward, name="gather_pass_chip_" + str(j),
        out_shape=(*_hbm_like(lands), pltpu.SemaphoreType.DMA((na,))),
        in_specs=[HBM] * na + [SEM, SEM, ANY], out_specs=(*[HBM] * na, SEM),
        input_output_aliases={i: i for i in range(na)},
        compiler_params=SIDE_EFFECT)(*lands, state["ici"], state["d2d"], after)
    lands, fwd_sems = out[:na], out[na]

    def arrive(*refs):
        land, fwd_sems, recv_d2d = refs[:na], refs[na], refs[na + 1]
        shard, send_sems = refs[na + 2:2 * na + 2], refs[2 * na + 2]
        for t in range(na):
            (x, y, c), mine, theirs = chip_blocks(land[t])
            pltpu.make_async_remote_copy(src_ref=theirs, dst_ref=theirs, send_sem=fwd_sems.at[t],
                                         recv_sem=recv_d2d.at[4 * t + 1 + j], device_id=(x, y, c),
                                         device_id_type=MESH).wait_recv()
            pltpu.make_async_remote_copy(src_ref=mine, dst_ref=mine, send_sem=fwd_sems.at[t],
                                         recv_sem=recv_d2d.at[4 * t + 1 + j], device_id=(x, y, c),
                                         device_id_type=MESH).wait_send()
            for k in range(4 if last else 0):
                pltpu.make_async_remote_copy(
                    src_ref=shard[t], dst_ref=land[t].at[_block_of(x, y, c)], send_sem=send_sems.at[4 * t + k],
                    recv_sem=recv_d2d.at[4 * t], device_id=(x, y, c), device_id_type=MESH).wait_send()

    out = pl.pallas_call(
        arrive, name="gather_take_chip_" + str(j), out_shape=_hbm_like(lands),
        in_specs=[HBM] * na + [SEM, SEM] + [ANY] * na + [SEM], out_specs=tuple([HBM] * na),
        input_output_aliases={i: i for i in range(na)},
        compiler_params=SIDE_EFFECT)(*lands, fwd_sems, state["d2d"], *shards, state["send"])
    return dict(state, lands=list(out))


def _to_sibling(srcs, lands, send_sems, recv_sems):
    x, y, c = _place()
    return [pltpu.make_async_remote_copy(
        src_ref=srcs[t].at[:, 1 - c], dst_ref=lands[t], send_sem=send_sems.at[t], recv_sem=recv_sems.at[t],
        device_id=(x, y, 1 - c), device_id_type=MESH) for t in range(len(srcs))]


def _to_chips(srcs, lands, send_sems, recv_sems):
    x, y, c = _place()
    copies = []
    for k in (1, 2, 3):
        px, py = x ^ (k >> 1), y ^ (k & 1)
        copies += [pltpu.make_async_remote_copy(
            src_ref=srcs[t].at[2 * px + py], dst_ref=lands[t].at[k - 1], send_sem=send_sems.at[3 * t + k - 1],
            recv_sem=recv_sems.at[3 * t + k - 1], device_id=(px, py, c), device_id_type=MESH) for t in range(len(srcs))]
    return copies


def _exchange_start(name, srcs, land_shapes, copies, per_array, after):
    na = len(srcs)
    lands = [_hbm(lax.empty(shp, a.dtype)) for shp, a in zip(land_shapes, srcs)]

    def body(*refs):
        token = refs[-1]
        for cp in copies(refs[:na], refs[na:2 * na], refs[2 * na + 1], refs[2 * na + 2]):
            cp.start()
        token[...] = jnp.zeros_like(token)

    out = pl.pallas_call(
        body, name=name,
        out_shape=(pltpu.SemaphoreType.DMA((na * per_array,)), pltpu.SemaphoreType.DMA((na * per_array,)),
                   *_hbm_like(lands), TOKEN),
        in_specs=[ANY] * na + [HBM] * na + [ANY],
        out_specs=(SEM, SEM, *[HBM] * na, pl.BlockSpec(memory_space=pltpu.VMEM)),
        input_output_aliases={na + i: 2 + i for i in range(na)},
        compiler_params=SIDE_EFFECT)(*srcs, *lands, after)
    return dict(send=out[0], recv=out[1], srcs=list(srcs), lands=out[2:2 + na]), out[-1]


def _exchange_wait(name, state, copies, after):
    srcs, lands = state["srcs"], state["lands"]
    na = len(srcs)

    def body(*refs):
        for cp in copies(refs[:na], refs[na:2 * na], refs[2 * na], refs[2 * na + 1]):
            cp.wait_send()
            cp.wait_recv()

    out = pl.pallas_call(
        body, name=name,
        out_shape=_hbm_like(lands),
        in_specs=[ANY] * na + [HBM] * na + [SEM, SEM, ANY],
        out_specs=tuple([HBM] * na),
        input_output_aliases={na + i: i for i in range(na)},
        compiler_params=SIDE_EFFECT)(*srcs, *lands, state["send"], state["recv"], after)
    return out


def _adamw_math(w, g, m, v):
    m = ADAM_B1 * m + (1.0 - ADAM_B1) * g
    v = ADAM_B2 * v + (1.0 - ADAM_B2) * (g * g)
    m_hat = m / (1.0 - ADAM_B1 ** ADAM_STEP)
    v_hat = v / (1.0 - ADAM_B2 ** ADAM_STEP)
    return -ADAM_LR * (m_hat / (jnp.sqrt(v_hat) + ADAM_EPS) + ADAM_WD * w), m, v


def _adamw(own, others, w, m, v, tr, name, after):
    rows, cols = w.shape
    blk = pl.BlockSpec((tr, cols), lambda i: (i, 0))

    def body(own_ref, oth_ref, w_ref, m_ref, v_ref, after_ref, g_ref, d_ref, nm_ref, nv_ref):
        g = own_ref[...]
        for k in range(3):
            g = g + oth_ref[k].astype(F32)
        g_ref[...] = g
        d_ref[...], nm_ref[...], nv_ref[...] = _adamw_math(w_ref[...], g, m_ref[...], v_ref[...])

    out = jax.ShapeDtypeStruct((rows, cols), F32)
    return pl.pallas_call(
        body, name=name, out_shape=(out, out, out, out), grid=(rows // tr,),
        in_specs=[blk, pl.BlockSpec((3, tr, cols), lambda i: (0, i, 0)), blk, blk, blk, ANY],
        out_specs=(blk, blk, blk, blk),
        compiler_params=_params("parallel"))(own, others, w, m, v, after)


def _adamw_small(red, me, params, moments1, moments2):
    n = len(params)

    def body(me_ref, red_ref, *refs):
        ws, ms, vs = refs[:n], refs[n:2 * n], refs[2 * n:3 * n]
        loss_ref = refs[3 * n]
        outs = refs[3 * n + 1:]
        loss_ref[...] = jnp.sum(red_ref[ROW_LOSS:ROW_LOSS + 1, :], axis=-1, keepdims=True)
        for t in range(n):
            if t < n - 1:
                g = red_ref[t:t + 1, 0:ws[t].shape[1]]
            else:
                g = red_ref[ROW_CONV:ROW_CONV + 3, pl.ds(pl.multiple_of(me_ref[0, 0] * LANES, LANES), LANES)]
            d, nm, nv = _adamw_math(ws[t][...], g, ms[t][...], vs[t][...])
            for o, val in zip(outs[4 * t:4 * t + 4], (g, d, nm, nv)):
                o[...] = val

    vmem = pl.BlockSpec(memory_space=pltpu.VMEM)
    shapes = [jax.ShapeDtypeStruct(w.shape, F32) for w in params for _ in range(4)]
    out = pl.pallas_call(
        body, name="adamw_small", out_shape=(jax.ShapeDtypeStruct((1, 1), F32), *shapes),
        in_specs=[pl.BlockSpec(memory_space=pltpu.SMEM), vmem] + [vmem] * (3 * n),
        out_specs=tuple([vmem] * (1 + 4 * n)))(me, red, *params, *moments1, *moments2)
    return out[0], [list(out[1 + k::4]) for k in range(4)]


def _tables(s, gq, gk, conv_w):
    gq2 = jnp.tile(gq.reshape(1, HEAD), (1, 2))
    gk2 = jnp.tile(gk.reshape(1, HEAD), (1, 2))
    conv_wp = jnp.pad(conv_w, ((0, SUBLANES - conv_w.shape[0]), (0, 0)))
    return _rope_tables(s), gq2, gk2, conv_wp


def _pair_id(q):
    return jnp.full((1,), q, jnp.int32)


def _forward_in(x, g1, shards):
    s = x.shape[0]
    h = _prenorm(x, g1, min(512, s))
    z, w_pairs = lax.empty((s, IN_W), F32), lax.empty((N_PAIRS, PAIR_W, D_MODEL), BF16)
    for q in range(N_PAIRS):
        z, w_pairs = _fwd_in_pair(h, shards, z, w_pairs, _pair_id(q), min(512, s), "fwd_in_" + str(q))
    return h, z, w_pairs


def _forward_attn(z, rope, gq2, gk2, conv_wp, sinks):
    s = z.shape[0]
    qn, k2, v2 = _qk_prep(z, *rope, gq2, gk2, min(256, s))
    a, mix, mixt = _attn_fwd(qn, k2, v2, z, conv_wp, sinks)
    return qn, k2, v2, a, mix, mixt


def _forward_out(x, p, target, mix, mixt, w_out, g2, w_pg, b_pg, w_pp, g3):
    s = x.shape[0]
    tm = min(512, s)
    x1, hn2, hn2t = _fwd_out(mix, w_out, x, g2, tm)
    dy, dgp, dt, pt, acc_ple = _ple(hn2, w_pg, b_pg, p, w_pp, g3, x1, target, min(256, s))
    dx1, dx1b, acc_g2 = _gate_bwd(dgp, w_pg, x1, dy, g2, tm)
    gw_out = _mm_grad(mixt, [dx1b], 512, "grad_w_out")
    gw_pg = _mm_grad(hn2t, [dgp], 512, "grad_w_ple_gate")
    gw_pp = _mm_grad(pt, [dt], 512, "grad_w_ple_proj")
    return dx1, dx1b, (gw_out, gw_pg, gw_pp), acc_ple, acc_g2


def _backward_attn(dmix, h, z, qn, k2, v2, a, rope, gq2, gk2, conv_wp, sinks):
    dq, dkc, dkp, dvc, dvp, dz, dzt, acc_attn = _attn_bwd(qn, k2, v2, a, z, dmix, conv_wp, sinks)
    dz, dzt, acc_qk = _qkv_bwd(z, dz, dzt, dq, dkc, dkp, dvc, dvp, *rope, gq2, gk2)
    return dz, _grad_w_in(dzt, h), acc_attn, acc_qk


def _local_step(x, p, target, g1, shards, gq, gk, sinks, conv_w, w_out, g2, w_pg, b_pg, w_pp, g3):
    rope, gq2, gk2, conv_wp = _tables(x.shape[0], gq, gk, conv_w)
    h, z, w_pairs = _forward_in(x, g1, shards)
    qn, k2, v2, a, mix, mixt = _forward_attn(z, rope, gq2, gk2, conv_wp, sinks)
    dx1, dx1b, (gw_out, gw_pg, gw_pp), acc_ple, acc_g2 = _forward_out(
        x, p, target, mix, mixt, w_out, g2, w_pg, b_pg, w_pp, g3)
    dmix = _mm_nt(dx1b, w_out, min(512, x.shape[0]), "out_bwd", dx1b)
    dz, gw_in, acc_attn, acc_qk = _backward_attn(dmix, h, z, qn, k2, v2, a, rope, gq2, gk2, conv_wp, sinks)
    grad_x, acc_g1 = _in_bwd(dz, w_pairs, x, dx1, g1, min(512, x.shape[0]))
    return grad_x, (gw_in, gw_out, gw_pg, gw_pp), (acc_g1, acc_g2, acc_ple, acc_qk, acc_attn)


def _by_owner(g):
    return g.reshape((4, 2) + g.shape[1:])


def kernel(x, p, norm_gain, w_in, q_norm_gain, k_norm_gain, attn_sinks, conv_w, w_out, ple_gate_norm_gain, w_ple_gate, b_ple_gate, w_ple_proj, ple_norm_gain, loss_target, m_norm_gain, m_w_in, m_q_norm_gain, m_k_norm_gain, m_attn_sinks, m_conv_w, m_w_out, m_ple_gate_norm_gain, m_w_ple_gate, m_b_ple_gate, m_w_ple_proj, m_ple_norm_gain, v_norm_gain, v_w_in, v_q_norm_gain, v_k_norm_gain, v_attn_sinks, v_conv_w, v_w_out, v_ple_gate_norm_gain, v_w_ple_gate, v_b_ple_gate, v_w_ple_proj, v_ple_norm_gain):
    me = 4 * lax.axis_index("x") + 2 * lax.axis_index("y") + lax.axis_index("c")
    place = jnp.stack([lax.axis_index("c"), 2 * lax.axis_index("x") + lax.axis_index("y")]).astype(jnp.int32)
    xs, ps, target = x[0], p[0, 0], loss_target[0]
    zero = lambda token: token[0:1, 0:1]

    shard_in = w_in[0].T.astype(BF16)
    own_late = [w_out[0].astype(BF16), w_ple_gate[0].astype(BF16), w_ple_proj[0].astype(BF16)]
    with_own = lambda gathered, own: lax.dynamic_update_slice(gathered, own[None], (me,) + (0,) * own.ndim)
    tie = lambda *arrays: sum(t[(slice(0, 1),) * t.ndim].reshape(1).astype(F32) for t in arrays)
    early, started = _gather_start([shard_in, conv_w[0]], shard_in)
    tm = min(512, xs.shape[0])
    h = _prenorm(xs, norm_gain + zero(started), tm)

    z, w_pairs = lax.empty((xs.shape[0], IN_W), F32), lax.empty((N_PAIRS, PAIR_W, D_MODEL), BF16)
    early = _gather_from_sibling(early, h)
    early = dict(early, lands=[with_own(early["lands"][0], shard_in), early["lands"][1]])
    z, w_pairs = _fwd_in_pair(h, early["lands"][0], z, w_pairs, place[1:2], tm, "fwd_in_own")
    for j, flip in enumerate((2, 1, 3)):
        early = _gather_from_chip(early, j, z if j != 1 else tie(z, started_late), last=j == 2)
        z, w_pairs = _fwd_in_pair(h, early["lands"][0], z, w_pairs, place[1:2] ^ flip, tm, "fwd_in_chip_" + str(j))
        if j == 0:
            late, started_late = _gather_start(own_late, z)
    conv_full = jnp.transpose(with_own(early["lands"][1], conv_w[0]), (1, 0, 2)).reshape(3, ATTN_W)
    rope, gq2, gk2, conv_wp = _tables(xs.shape[0], q_norm_gain[0], k_norm_gain[0], conv_full)
    late, forwarded = _gather_forward(late, z)
    qn, k2, v2, a, mix, mixt = _forward_attn(z, rope, gq2 + zero(forwarded), gk2, conv_wp, attn_sinks)
    g_out, g_pg, g_pp = (with_own(g, own) for g, own in zip(_gather_wait(late, mix), own_late))
    w_out_f = g_out.reshape(D_MODEL, D_MODEL)
    w_pg_f = g_pg.reshape(D_MODEL, D_MODEL)
    w_pp_f = jnp.transpose(g_pp, (1, 0, 2)).reshape(PLE_DIM, D_MODEL)

    dx1, dx1b, (gw_out, gw_pg, gw_pp), acc_ple, acc_g2 = _forward_out(
        xs, ps, target, mix, mixt, w_out_f, ple_gate_norm_gain, w_pg_f, b_ple_gate, w_pp_f, ple_norm_gain)

    names = ("w_out", "w_ple_gate", "w_ple_proj")
    gw_pp_t = jnp.transpose(gw_pp.reshape(PLE_DIM, N_DEV, PLE_DIM), (1, 0, 2))
    grads = [_by_owner(gw_out.reshape(N_DEV, D_MODEL // N_DEV, D_MODEL)),
             _by_owner(gw_pg.reshape(N_DEV, D_MODEL // N_DEV, D_MODEL)), _by_owner(gw_pp_t)]
    pairs, paired = _exchange_start("pair_start", grads, [(4,) + g.shape[2:] for g in grads], _to_sibling, 1, dx1b)
    dmix = _mm_nt(dx1b, w_out_f, tm, "out_bwd", paired)
    from_sibling = _exchange_wait("pair_wait", pairs, _to_sibling, dmix)
    sums = [_pair_sum(g, r, place, 256, "pair_sum_" + nm) for g, r, nm in zip(pairs["srcs"], from_sibling, names)]
    chips, sent = _exchange_start("chip_start", [pb for pb, _ in sums], [(3,) + pb.shape[1:] for pb, _ in sums],
                                  _to_chips, 3, sums[-1][1])

    dz, gw_in, acc_attn, acc_qk = _backward_attn(
        dmix, h, z, qn, k2, v2, a, rope, gq2, gk2, conv_wp, attn_sinks + zero(sent))

    gw_in_t = [_by_owner(gw_in)]
    pairs_in, paired_in = _exchange_start("pair_start_w_in", gw_in_t, [(4,) + gw_in_t[0].shape[2:]], _to_sibling, 1,
                                          gw_in)
    from_chips = _exchange_wait("chip_wait", chips, _to_chips, gw_in)
    big = {}
    for (_, own), oth, w, m, v, nm in zip(sums, from_chips, (w_out, w_ple_gate, w_ple_proj),
                                          (m_w_out, m_w_ple_gate, m_w_ple_proj),
                                          (v_w_out, v_w_ple_gate, v_w_ple_proj), names):
        big[nm] = [t[None] for t in _adamw(own, oth, w[0], m[0], v[0], 256, "adamw_" + nm, paired_in)]

    (from_sibling_in,) = _exchange_wait("pair_wait_w_in", pairs_in, _to_sibling, tie(*[big[nm][0] for nm in names]))
    pb_in, own_in = _pair_sum(pairs_in["srcs"][0], from_sibling_in, place, SHARD_IN // 2, "pair_sum_w_in")
    chips_in, sent_in = _exchange_start("chip_start_w_in", [pb_in], [(3,) + pb_in.shape[1:]], _to_chips, 3, own_in)
    grad_x, acc_g1 = _in_bwd(dz, w_pairs, xs, dx1, norm_gain + zero(sent_in), tm)
    (from_chips_in,) = _exchange_wait("chip_wait_w_in", chips_in, _to_chips, grad_x)
    big["w_in"] = [t.T[None] for t in _adamw(own_in, from_chips_in, w_in[0].T, m_w_in[0].T, v_w_in[0].T, SHARD_IN // 4,
                                             "adamw_w_in", grad_x)]

    red = _reduce_small(acc_g1, acc_g2, acc_ple, acc_qk, acc_attn)
    small = [norm_gain, ple_gate_norm_gain, b_ple_gate, ple_norm_gain, q_norm_gain, k_norm_gain, attn_sinks]
    small_m = [m_norm_gain, m_ple_gate_norm_gain, m_b_ple_gate, m_ple_norm_gain, m_q_norm_gain, m_k_norm_gain,
               m_attn_sinks]
    small_v = [v_norm_gain, v_ple_gate_norm_gain, v_b_ple_gate, v_ple_norm_gain, v_q_norm_gain, v_k_norm_gain,
               v_attn_sinks]
    loss, kinds = _adamw_small(red, me.reshape(1, 1).astype(jnp.int32), small + [conv_w[0]], small_m + [m_conv_w[0]],
                               small_v + [v_conv_w[0]])

    def order(k):
        sm = kinds[k]
        return [sm[0], big["w_in"][k], sm[4], sm[5], sm[6], sm[7][None], big["w_out"][k], sm[1], big["w_ple_gate"][k],
                sm[2], big["w_ple_proj"][k], sm[3]]

    return (loss[0, 0], grad_x[None], *order(0), *order(1), *order(2), *order(3))
```

```python
import jax
import jax.numpy as jnp
from jax import lax
from jax.experimental import pallas as pl
from jax.experimental.pallas import tpu as pltpu

F32, BF16 = jnp.float32, jnp.bfloat16

D_MODEL = 2048
PLE_DIM = 256
ATTN_W = 1024
HEAD = 64
N_Q_HEADS = 16
KV_W = 256
QKV_W = ATTN_W + 2 * KV_W
REST_W = 5 * 1024
IN_W = QKV_W + REST_W
GATE_A0, CONV_B0, CONV_C0, CONV_H0, GATE_C0 = (QKV_W + 1024 * t for t in range(5))
K2_W = 4 * 128
ROT = 16
ROPE_THETA = 500000.0
EPS = 1e-6
NEG_INF = -1e30
BLK = 128
LANES = 128
SUBLANES = 8
N_DEV = 8
SHARD_IN = IN_W // N_DEV
PAIR_W = 2 * SHARD_IN
N_PAIRS = IN_W // PAIR_W
SLAB_ROWS = 16
SUB_ROWS = 128
V7X_VMEM_LIMIT = 52 * 1024 * 1024

ADAM_LR, ADAM_B1, ADAM_B2, ADAM_EPS, ADAM_WD, ADAM_STEP = 0.001, 0.9, 0.999, 1e-08, 0.01, 10
MESH = pl.DeviceIdType.MESH


def _params(*semantics):
    return pltpu.CompilerParams(dimension_semantics=semantics, vmem_limit_bytes=V7X_VMEM_LIMIT)


ANY = pl.BlockSpec(memory_space=pl.ANY)


def _resident(shape):
    return pl.BlockSpec(shape, lambda *_: (0,) * len(shape), pipeline_mode=pl.Buffered(1))


def _dot(a, b):
    return jnp.dot(a, b, preferred_element_type=F32)


def _dot_nt(a, b):
    return lax.dot_general(a, b, (((1,), (1,)), ((), ())), preferred_element_type=F32)


def _rms(xf):
    r = lax.rsqrt(jnp.mean(xf * xf, axis=-1, keepdims=True) + EPS)
    return xf * r, r


def _rms_bwd(dxn, xn, r):
    return r * (dxn - xn * jnp.mean(dxn * xn, axis=-1, keepdims=True))


def _sig(g):
    return jax.nn.sigmoid(g)


def _dsilu(g, sg):
    return sg * (1.0 + g * (1.0 - sg))


def _low_half(shape):
    return lax.broadcasted_iota(jnp.int32, shape, len(shape) - 1) < HEAD


def _half_sums(v):
    lo = _low_half(v.shape)
    s_lo = jnp.sum(jnp.where(lo, v, 0.0), axis=-1, keepdims=True)
    s_hi = jnp.sum(jnp.where(lo, 0.0, v), axis=-1, keepdims=True)
    return jnp.where(lo, s_lo, s_hi)


def _rope(v, a, bm, bp):
    return v * a + pltpu.roll(v, LANES - ROT // 2, 1) * bm + pltpu.roll(v, ROT // 2, 1) * bp


def _rope_t(dy, a, bm, bp):
    return dy * a + pltpu.roll(dy * bm, ROT // 2, 1) + pltpu.roll(dy * bp, LANES - ROT // 2, 1)


def _dup_halves(v):
    lo = _low_half(v.shape)
    a = jnp.where(lo, v, 0.0)
    b = jnp.where(lo, 0.0, v)
    return a + pltpu.roll(a, HEAD, 1), b + pltpu.roll(b, HEAD, 1)


def _rope_tables(s):
    half = ROT // 2
    lane = lax.broadcasted_iota(jnp.int32, (s, LANES), 1) % HEAD
    pos = lax.broadcasted_iota(jnp.int32, (s, LANES), 0).astype(F32)
    inv_freq = jnp.power(jnp.float32(ROPE_THETA), -(lane % half).astype(F32) * 2.0 / ROT)
    ang = pos * inv_freq
    cos, sin = jnp.cos(ang), jnp.sin(ang)
    a = jnp.where(lane < ROT, cos, 1.0)
    bm = jnp.where(lane < half, -sin, 0.0)
    bp = jnp.where((lane >= half) & (lane < ROT), sin, 0.0)
    return a, bm, bp


def _prenorm(x, g1, tm):
    s = x.shape[0]

    def body(x_ref, g_ref, h_ref):
        xn, _ = _rms(x_ref[...])
        h_ref[...] = (xn * g_ref[...]).astype(BF16)

    return pl.pallas_call(
        body, name="prenorm",
        out_shape=jax.ShapeDtypeStruct((s, D_MODEL), BF16),
        grid=(s // tm,),
        in_specs=[pl.BlockSpec((tm, D_MODEL), lambda i: (i, 0)), pl.BlockSpec((1, D_MODEL), lambda i: (0, 0))],
        out_specs=pl.BlockSpec((tm, D_MODEL), lambda i: (i, 0)),
        compiler_params=_params("parallel"))(x, g1)


def _fwd_in_pair(h, shards, z, w_pairs, pair, tm, name):
    s = h.shape[0]

    def body(pair_ref, h_ref, lo_ref, hi_ref, z_in, wp_in, z_ref, wp_ref):
        @pl.when(pl.program_id(0) == 0)
        def _():
            wp_ref[0, 0:SHARD_IN, :] = lo_ref[0]
            wp_ref[0, SHARD_IN:PAIR_W, :] = hi_ref[0]

        z_ref[...] = _dot_nt(h_ref[...], wp_ref[0])

    grid_spec = pltpu.PrefetchScalarGridSpec(
        num_scalar_prefetch=1, grid=(s // tm,),
        in_specs=[pl.BlockSpec((tm, D_MODEL), lambda i, p: (i, 0)),
                  pl.BlockSpec((1, SHARD_IN, D_MODEL), lambda i, p: (2 * p[0], 0, 0)),
                  pl.BlockSpec((1, SHARD_IN, D_MODEL), lambda i, p: (2 * p[0] + 1, 0, 0)), ANY, ANY],
        out_specs=(pl.BlockSpec((tm, PAIR_W), lambda i, p: (i, p[0])),
                   pl.BlockSpec((1, PAIR_W, D_MODEL), lambda i, p: (p[0], 0, 0))))
    return pl.pallas_call(
        body, name=name, grid_spec=grid_spec,
        out_shape=(jax.ShapeDtypeStruct(z.shape, z.dtype), jax.ShapeDtypeStruct(w_pairs.shape, w_pairs.dtype)),
        input_output_aliases={4: 0, 5: 1},
        compiler_params=_params("arbitrary"))(pair, h, shards, shards, z, w_pairs)


def _qk_prep(z, ra, rbm, rbp, gq2, gk2, tm):
    s = z.shape[0]

    def body(z_ref, a_ref, bm_ref, bp_ref, gq_ref, gk_ref, q_ref, k2_ref, v2_ref):
        a, bm, bp = a_ref[...], bm_ref[...], bp_ref[...]
        for r in range(ATTN_W // LANES):
            x = z_ref[:, LANES * r:LANES * (r + 1)]
            rr = lax.rsqrt(_half_sums(x * x) * (1.0 / HEAD) + EPS)
            q_ref[:, LANES * r:LANES * (r + 1)] = _rope(x * rr * gq_ref[...], a, bm, bp).astype(BF16)
        for m in range(KV_W // LANES):
            x = z_ref[:, ATTN_W + LANES * m:ATTN_W + LANES * (m + 1)]
            rr = lax.rsqrt(_half_sums(x * x) * (1.0 / HEAD) + EPS)
            k_lo, k_hi = _dup_halves(_rope(x * rr * gk_ref[...], a, bm, bp))
            k2_ref[:, 2 * LANES * m:2 * LANES * m + LANES] = k_lo.astype(BF16)
            k2_ref[:, 2 * LANES * m + LANES:2 * LANES * (m + 1)] = k_hi.astype(BF16)
            v_lo, v_hi = _dup_halves(z_ref[:, ATTN_W + KV_W + LANES * m:ATTN_W + KV_W + LANES * (m + 1)])
            v2_ref[:, 2 * LANES * m:2 * LANES * m + LANES] = v_lo.astype(BF16)
            v2_ref[:, 2 * LANES * m + LANES:2 * LANES * (m + 1)] = v_hi.astype(BF16)

    row = lambda w: pl.BlockSpec((tm, w), lambda i: (i, 0))
    one = pl.BlockSpec((1, LANES), lambda i: (0, 0))
    return pl.pallas_call(
        body, name="qk_prep",
        out_shape=(jax.ShapeDtypeStruct((s, ATTN_W), BF16), jax.ShapeDtypeStruct((s, K2_W), BF16),
                   jax.ShapeDtypeStruct((s, K2_W), BF16)),
        grid=(s // tm,),
        in_specs=[row(PAIR_W), row(LANES), row(LANES), row(LANES), one, one],
        out_specs=(row(ATTN_W), row(K2_W), row(K2_W)),
        compiler_params=_params("parallel"))(z, ra, rbm, rbp, gq2, gk2)


GROUP = 4


def _window_mask(n):
    row = lax.broadcasted_iota(jnp.int32, (GROUP * BLK, 2 * BLK), 0) % BLK
    col = lax.broadcasted_iota(jnp.int32, (GROUP * BLK, 2 * BLK), 1)
    return (col > row) & (col <= row + BLK) & ((col >= BLK) | (n > 0))


def _stack_heads(pairs, zero):
    lo = _low_half(pairs[0].shape)
    parts = []
    for v in pairs:
        parts += [jnp.where(lo, v, zero), jnp.where(lo, zero, v)]
    return jnp.concatenate(parts, axis=0)


def _unstack_heads(v4):
    lo = _low_half((BLK, LANES))
    return [jnp.where(lo, v4[2 * i * BLK:(2 * i + 1) * BLK], v4[(2 * i + 1) * BLK:(2 * i + 2) * BLK]) for i in range(2)]


def _group_sinks(sink_ref, kvh):
    slot = lax.broadcasted_iota(jnp.int32, (GROUP * BLK, 1), 0) // BLK
    col = jnp.zeros((GROUP * BLK, 1), F32)
    for i in range(GROUP):
        col = jnp.where(slot == i, sink_ref[0, GROUP * kvh + i], col)
    return col, slot


def _head_probs(qm, kw, valid, sink):
    sc = jnp.where(valid, _dot_nt(qm, kw) * (HEAD ** -0.5), NEG_INF)
    mx = jnp.maximum(jnp.max(sc, axis=-1, keepdims=True), sink)
    ex = jnp.exp(sc - mx)
    den = jnp.sum(ex, axis=-1, keepdims=True) + jnp.exp(sink - mx)
    return ex / den, mx, den


def _cols(start, width=ATTN_W):
    return slice(start, start + width)


def _conv_fwd(z_ref, zp_ref, cw_ref, ext_ref, n):
    u = z_ref[:, _cols(CONV_C0)] * z_ref[:, _cols(CONV_H0)]
    pu = zp_ref[:, _cols(CONV_C0)] * zp_ref[:, _cols(CONV_H0)]
    ext_ref[0:SUBLANES, :] = jnp.where(n > 0, pu, 0.0)
    ext_ref[SUBLANES:SUBLANES + BLK, :] = u
    um1 = ext_ref[SUBLANES - 1:SUBLANES - 1 + BLK, :]
    um2 = ext_ref[SUBLANES - 2:SUBLANES - 2 + BLK, :]
    cv = cw_ref[0:1, :] * um2 + cw_ref[1:2, :] * um1 + cw_ref[2:3, :] * u
    return u, um1, um2, cv


def _prev_rows(n):
    return (jnp.maximum(n * (BLK // SUBLANES) - 1, 0), 0)


def _attn_fwd(qn, k2, v2, z, conv_wp, sinks):
    s = qn.shape[0]
    nb = s // BLK

    def body(sink_ref, q_ref, kc_ref, kp_ref, vc_ref, vp_ref, z_ref, zp_ref, cw_ref, a_ref, mix_ref, mixt_ref,
             ext_ref):
        n = pl.program_id(0)
        valid = _window_mask(n)
        for kvh in range(K2_W // LANES):
            cols = slice(LANES * kvh, LANES * (kvh + 1))
            kw = jnp.concatenate([kp_ref[:, cols], kc_ref[:, cols]], axis=0)
            vw = jnp.concatenate([vp_ref[:, cols], vc_ref[:, cols]], axis=0)
            blocks = [slice(LANES * r, LANES * (r + 1)) for r in (2 * kvh, 2 * kvh + 1)]
            q4 = _stack_heads([q_ref[:, rc] for rc in blocks], jnp.zeros((BLK, LANES), BF16))
            p, _, _ = _head_probs(q4, kw, valid, _group_sinks(sink_ref, kvh)[0])
            for rc, a in zip(blocks, _unstack_heads(_dot(p.astype(BF16), vw))):
                a_ref[:, rc] = a
                g = z_ref[:, _cols(GATE_A0 + rc.start, LANES)]
                mix_ref[:, rc] = (a * (g * _sig(g))).astype(BF16)
        _, _, _, cv = _conv_fwd(z_ref, zp_ref, cw_ref, ext_ref, n)
        gc = z_ref[:, _cols(GATE_C0)]
        mix_ref[:, ATTN_W:D_MODEL] = (z_ref[:, _cols(CONV_B0)] * cv * (gc * _sig(gc))).astype(BF16)
        mixt_ref[...] = mix_ref[...].T

    cur = lambda w: pl.BlockSpec((BLK, w), lambda n: (n, 0))
    prev = lambda w: pl.BlockSpec((BLK, w), lambda n: (jnp.maximum(n - 1, 0), 0))
    return pl.pallas_call(
        body, name="attn_fwd",
        out_shape=(jax.ShapeDtypeStruct((s, ATTN_W), F32), jax.ShapeDtypeStruct((s, D_MODEL), BF16),
                   jax.ShapeDtypeStruct((D_MODEL, s), BF16)),
        grid=(nb,),
        in_specs=[pl.BlockSpec(memory_space=pltpu.SMEM),
                  cur(ATTN_W), cur(K2_W), prev(K2_W), cur(K2_W), prev(K2_W), cur(IN_W),
                  pl.BlockSpec((SUBLANES, IN_W), _prev_rows),
                  pl.BlockSpec((SUBLANES, ATTN_W), lambda n: (0, 0))],
        out_specs=(cur(ATTN_W), cur(D_MODEL), pl.BlockSpec((D_MODEL, BLK), lambda n: (0, n))),
        scratch_shapes=[pltpu.VMEM((BLK + 2 * SUBLANES, ATTN_W), F32)],
        compiler_params=_params("parallel"))(sinks, qn, k2, k2, v2, v2, z, z, conv_wp)


def _fwd_out(mix, w_out, x, g2, tm):
    s = x.shape[0]

    def body(m_ref, w_ref, x_ref, g_ref, x1_ref, h_ref, ht_ref):
        x1 = x_ref[...] + _dot(m_ref[...], w_ref[...])
        x1_ref[...] = x1
        xn, _ = _rms(x1)
        h = (xn * g_ref[...]).astype(BF16)
        h_ref[...] = h
        ht_ref[...] = h.T

    row = pl.BlockSpec((tm, D_MODEL), lambda i: (i, 0))
    return pl.pallas_call(
        body, name="fwd_out",
        out_shape=(jax.ShapeDtypeStruct((s, D_MODEL), F32), jax.ShapeDtypeStruct((s, D_MODEL), BF16),
                   jax.ShapeDtypeStruct((D_MODEL, s), BF16)),
        grid=(s // tm,),
        in_specs=[row, _resident((D_MODEL, D_MODEL)), row, pl.BlockSpec((1, D_MODEL), lambda i: (0, 0))],
        out_specs=(row, row, pl.BlockSpec((D_MODEL, tm), lambda i: (0, i))),
        compiler_params=_params("parallel"))(mix, w_out, x, g2)


def _ple(hn2, w_pg, b_pg, p, w_pp, g3, x1, target, tm):
    s = x1.shape[0]

    def body(h_ref, wg_ref, b_ref, p_ref, wp_ref, g3_ref, x1_ref, t_ref, dy_ref, dgp_ref, dt_ref, pt_ref, acc_ref):
        gate = _sig(_dot(h_ref[...], wg_ref[...]) + b_ref[...])
        pb = p_ref[...].astype(BF16)
        pt_ref[...] = pb.T
        t = _dot(pb, wp_ref[...])
        tn, r3 = _rms(t)
        e = tn * g3_ref[...]
        diff = x1_ref[...] + gate * e - t_ref[...]
        dy = diff * (1.0 / D_MODEL)
        dy_ref[...] = dy
        dgp = dy * e * (gate * (1.0 - gate))
        dgp_ref[...] = dgp.astype(BF16)
        de = dy * gate
        dt_ref[...] = _rms_bwd(de * g3_ref[...], tn, r3).astype(BF16)

        @pl.when(pl.program_id(0) == 0)
        def _():
            acc_ref[...] = jnp.zeros_like(acc_ref)

        acc_ref[0:1, :] += jnp.sum(dgp, axis=0, keepdims=True)
        acc_ref[1:2, :] += jnp.sum(de * tn, axis=0, keepdims=True)
        acc_ref[2:3, :] += jnp.sum(diff * diff, axis=0, keepdims=True) * (0.5 / D_MODEL)

    row = pl.BlockSpec((tm, D_MODEL), lambda i: (i, 0))
    vec = pl.BlockSpec((1, D_MODEL), lambda i: (0, 0))
    return pl.pallas_call(
        body, name="ple",
        out_shape=(jax.ShapeDtypeStruct((s, D_MODEL), F32), jax.ShapeDtypeStruct((s, D_MODEL), BF16),
                   jax.ShapeDtypeStruct((s, D_MODEL), BF16), jax.ShapeDtypeStruct((PLE_DIM, s), BF16),
                   jax.ShapeDtypeStruct((SUBLANES, D_MODEL), F32)),
        grid=(s // tm,),
        in_specs=[row, _resident((D_MODEL, D_MODEL)), vec, pl.BlockSpec((tm, PLE_DIM), lambda i: (i, 0)),
                  _resident((PLE_DIM, D_MODEL)), vec, row, row],
        out_specs=(row, row, row, pl.BlockSpec((PLE_DIM, tm), lambda i: (0, i)),
                   pl.BlockSpec((SUBLANES, D_MODEL), lambda i: (0, 0))),
        compiler_params=_params("arbitrary"))(hn2, w_pg, b_pg, p, w_pp, g3, x1, target)


def _gate_bwd(dgp, w_pg, x1, dy, g2, tm):
    s = x1.shape[0]

    def body(d_ref, w_ref, x1_ref, dy_ref, g_ref, dx_ref, dxb_ref, acc_ref):
        dh = _dot_nt(d_ref[...], w_ref[...])
        xn, r = _rms(x1_ref[...])
        dx1 = dy_ref[...] + _rms_bwd(dh * g_ref[...], xn, r)
        dx_ref[...] = dx1
        dxb_ref[...] = dx1.astype(BF16)

        @pl.when(pl.program_id(0) == 0)
        def _():
            acc_ref[...] = jnp.zeros_like(acc_ref)

        acc_ref[0:1, :] += jnp.sum(dh * xn, axis=0, keepdims=True)

    row = pl.BlockSpec((tm, D_MODEL), lambda i: (i, 0))
    return pl.pallas_call(
        body, name="gate_bwd",
        out_shape=(jax.ShapeDtypeStruct((s, D_MODEL), F32), jax.ShapeDtypeStruct((s, D_MODEL), BF16),
                   jax.ShapeDtypeStruct((SUBLANES, D_MODEL), F32)),
        grid=(s // tm,),
        in_specs=[row, _resident((D_MODEL, D_MODEL)), row, row, pl.BlockSpec((1, D_MODEL), lambda i: (0, 0))],
        out_specs=(row, row, pl.BlockSpec((SUBLANES, D_MODEL), lambda i: (0, 0))),
        compiler_params=_params("arbitrary"))(dgp, w_pg, x1, dy, g2)


def _mm_nt(a, b, tm, name, after):
    m, k = a.shape
    n = b.shape[0]

    def body(a_ref, b_ref, after_ref, o_ref):
        o_ref[...] = _dot_nt(a_ref[...], b_ref[...])

    return pl.pallas_call(
        body, name=name,
        out_shape=jax.ShapeDtypeStruct((m, n), F32),
        grid=(m // tm,),
        in_specs=[pl.BlockSpec((tm, k), lambda i: (i, 0)), _resident((n, k)), ANY],
        out_specs=pl.BlockSpec((tm, n), lambda i: (i, 0)),
        compiler_params=_params("parallel"))(a, b, after)


def _attn_bwd(qn, k2, v2, a, z, dmix, conv_wp, sinks):
    s = qn.shape[0]
    nb = s // BLK

    def body(sink_ref, q_ref, kc_ref, kp_ref, vc_ref, vp_ref, a_ref, z_ref, zp_ref, zn_ref, dm_ref, dmn_ref,
             cw_ref, dq_ref, dkc_ref, dkp_ref, dvc_ref, dvp_ref, dz_ref, dzt_ref, acc_ref, ext_ref):
        n = pl.program_id(0)
        valid = _window_mask(n)
        lane = lax.broadcasted_iota(jnp.int32, (1, ATTN_W), 1)

        @pl.when(n == 0)
        def _():
            acc_ref[...] = jnp.zeros_like(acc_ref)

        dz_ref[:, 0:QKV_W] = jnp.zeros((BLK, QKV_W), BF16)
        dsink = jnp.zeros((1, ATTN_W), F32)
        for kvh in range(K2_W // LANES):
            cols = slice(LANES * kvh, LANES * (kvh + 1))
            kw = jnp.concatenate([kp_ref[:, cols], kc_ref[:, cols]], axis=0)
            vw = jnp.concatenate([vp_ref[:, cols], vc_ref[:, cols]], axis=0)
            blocks = [slice(LANES * r, LANES * (r + 1)) for r in (2 * kvh, 2 * kvh + 1)]
            das, avs = [], []
            for rc in blocks:
                g = z_ref[:, _cols(GATE_A0 + rc.start, LANES)]
                sg = _sig(g)
                dm = dm_ref[:, rc]
                av = a_ref[:, rc]
                das.append(dm * (g * sg))
                avs += [av, av]
                dz_ref[:, _cols(GATE_A0 + rc.start, LANES)] = (dm * av * _dsilu(g, sg)).astype(BF16)
            q4 = _stack_heads([q_ref[:, rc] for rc in blocks], jnp.zeros((BLK, LANES), BF16))
            sink, slot = _group_sinks(sink_ref, kvh)
            p, mx, den = _head_probs(q4, kw, valid, sink)
            do4 = _stack_heads(das, 0.0)
            delta = jnp.sum(do4 * jnp.concatenate(avs, axis=0), axis=-1, keepdims=True)
            dob = do4.astype(BF16)
            ds = p * (_dot_nt(dob, vw) - delta) * (HEAD ** -0.5)
            for rc, dq in zip(blocks, _unstack_heads(_dot(ds.astype(BF16), kw))):
                dq_ref[:, rc] = dq
            dk2 = _dot(ds.T.astype(BF16), q4)
            dv2 = _dot(p.T.astype(BF16), dob)
            dkp_ref[:, cols] = dk2[0:BLK]
            dkc_ref[:, cols] = dk2[BLK:2 * BLK]
            dvp_ref[:, cols] = dv2[0:BLK]
            dvc_ref[:, cols] = dv2[BLK:2 * BLK]
            dsk = jnp.exp(sink - mx) / den * delta
            for i in range(GROUP):
                dsink = dsink - jnp.where(lane == GROUP * kvh + i,
                                          jnp.sum(jnp.where(slot == i, dsk, 0.0), axis=0, keepdims=True), 0.0)
        acc_ref[0:1, :] += dsink

        u, um1, um2, cv = _conv_fwd(z_ref, zp_ref, cw_ref, ext_ref, n)
        cb = z_ref[:, _cols(CONV_B0)]
        gc = z_ref[:, _cols(GATE_C0)]
        sgc = _sig(gc)
        dmc = dm_ref[:, ATTN_W:D_MODEL]
        t = dmc * (gc * sgc)
        dcv = t * cb
        dz_ref[:, _cols(CONV_B0)] = (t * cv).astype(BF16)
        dz_ref[:, _cols(GATE_C0)] = (dmc * cb * cv * _dsilu(gc, sgc)).astype(BF16)
        gcn = zn_ref[:, _cols(GATE_C0)]
        dcvn = dmn_ref[:, ATTN_W:D_MODEL] * (gcn * _sig(gcn)) * zn_ref[:, _cols(CONV_B0)]
        ext_ref[0:BLK, :] = dcv
        ext_ref[BLK:BLK + SUBLANES, :] = jnp.where(n < nb - 1, dcvn, 0.0)
        du = (cw_ref[2:3, :] * dcv + cw_ref[1:2, :] * ext_ref[1:1 + BLK, :]
              + cw_ref[0:1, :] * ext_ref[2:2 + BLK, :])
        dz_ref[:, _cols(CONV_C0)] = (du * z_ref[:, _cols(CONV_H0)]).astype(BF16)
        dz_ref[:, _cols(CONV_H0)] = (du * z_ref[:, _cols(CONV_C0)]).astype(BF16)
        acc_ref[1:2, :] += jnp.sum(dcv * um2, axis=0, keepdims=True)
        acc_ref[2:3, :] += jnp.sum(dcv * um1, axis=0, keepdims=True)
        acc_ref[3:4, :] += jnp.sum(dcv * u, axis=0, keepdims=True)
        dzt_ref[...] = dz_ref[...].T

    cur = lambda w: pl.BlockSpec((BLK, w), lambda n: (n, 0))
    prev = lambda w: pl.BlockSpec((BLK, w), lambda n: (jnp.maximum(n - 1, 0), 0))
    nxt = lambda w: pl.BlockSpec(
        (SUBLANES, w), lambda n: (jnp.minimum((n + 1) * (BLK // SUBLANES), nb * (BLK // SUBLANES) - 1), 0))
    f32 = lambda w: jax.ShapeDtypeStruct((s, w), F32)
    return pl.pallas_call(
        body, name="attn_bwd",
        out_shape=(f32(ATTN_W), f32(K2_W), f32(K2_W), f32(K2_W), f32(K2_W),
                   jax.ShapeDtypeStruct((s, IN_W), BF16), jax.ShapeDtypeStruct((IN_W, s), BF16),
                   jax.ShapeDtypeStruct((SUBLANES, ATTN_W), F32)),
        grid=(nb,),
        in_specs=[pl.BlockSpec(memory_space=pltpu.SMEM),
                  cur(ATTN_W), cur(K2_W), prev(K2_W), cur(K2_W), prev(K2_W), cur(ATTN_W), cur(IN_W),
                  pl.BlockSpec((SUBLANES, IN_W), _prev_rows), nxt(IN_W), cur(D_MODEL), nxt(D_MODEL),
                  pl.BlockSpec((SUBLANES, ATTN_W), lambda n: (0, 0))],
        out_specs=(cur(ATTN_W), cur(K2_W), cur(K2_W), cur(K2_W), cur(K2_W), cur(IN_W),
                   pl.BlockSpec((IN_W, BLK), lambda n: (0, n)), pl.BlockSpec((SUBLANES, ATTN_W), lambda n: (0, 0))),
        scratch_shapes=[pltpu.VMEM((BLK + 2 * SUBLANES, ATTN_W), F32)],
        compiler_params=_params("arbitrary"))(sinks, qn, k2, k2, v2, v2, a, z, z, z, dmix, dmix, conv_wp)


def _qkv_bwd(z, dz, dzt, dq, dkc, dkp, dvc, dvp, ra, rbm, rbp, gq2, gk2):
    s = z.shape[0]
    nb = s // BLK

    def body(z_ref, dz_in, dzt_in, dq_ref, dkc_ref, dkp_ref, dvc_ref, dvp_ref, a_ref, bm_ref, bp_ref, gq_ref, gk_ref,
             dz_ref, dzt_ref, acc_ref):
        n = pl.program_id(0)
        a, bm, bp = a_ref[...], bm_ref[...], bp_ref[...]
        lo = _low_half((BLK, LANES))
        last = n == nb - 1

        @pl.when(n == 0)
        def _():
            acc_ref[...] = jnp.zeros_like(acc_ref)

        def norm_bwd(x, dy, gain):
            rr = lax.rsqrt(_half_sums(x * x) * (1.0 / HEAD) + EPS)
            xh = x * rr
            dxg = _rope_t(dy, a, bm, bp)
            dxh = dxg * gain
            dx = rr * (dxh - xh * (_half_sums(dxh * xh) * (1.0 / HEAD)))
            return dx, jnp.sum(dxg * xh, axis=0, keepdims=True)

        def folded(cur_ref, prev_ref, m):
            parts = []
            for h in (2 * m, 2 * m + 1):
                v = cur_ref[:, LANES * h:LANES * (h + 1)] + jnp.where(
                    last, 0.0, prev_ref[:, LANES * h:LANES * (h + 1)])
                parts.append(v + pltpu.roll(v, HEAD, 1))
            return jnp.where(lo, parts[0], parts[1])

        gq_acc = jnp.zeros((1, LANES), F32)
        for r in range(ATTN_W // LANES):
            rc = slice(LANES * r, LANES * (r + 1))
            dx, gg = norm_bwd(z_ref[:, rc], dq_ref[:, rc], gq_ref[...])
            dz_ref[:, rc] = dx.astype(BF16)
            gq_acc = gq_acc + gg
        acc_ref[0:1, :] += gq_acc
        gk_acc = jnp.zeros((1, LANES), F32)
        for m in range(KV_W // LANES):
            kc = slice(ATTN_W + LANES * m, ATTN_W + LANES * (m + 1))
            dx, gg = norm_bwd(z_ref[:, kc], folded(dkc_ref, dkp_ref, m), gk_ref[...])
            dz_ref[:, kc] = dx.astype(BF16)
            gk_acc = gk_acc + gg
            vc = slice(ATTN_W + KV_W + LANES * m, ATTN_W + KV_W + LANES * (m + 1))
            dz_ref[:, vc] = folded(dvc_ref, dvp_ref, m).astype(BF16)
        acc_ref[1:2, :] += gk_acc
        dzt_ref[...] = dz_ref[...].T

    cur = lambda w: pl.BlockSpec((BLK, w), lambda n: (n, 0))
    nxt = lambda w: pl.BlockSpec((BLK, w), lambda n: (jnp.minimum(n + 1, nb - 1), 0))
    one = pl.BlockSpec((1, LANES), lambda n: (0, 0))
    return pl.pallas_call(
        body, name="qkv_bwd",
        out_shape=(jax.ShapeDtypeStruct(dz.shape, dz.dtype), jax.ShapeDtypeStruct(dzt.shape, dzt.dtype),
                   jax.ShapeDtypeStruct((SUBLANES, LANES), F32)),
        grid=(nb,),
        in_specs=[cur(PAIR_W), ANY, ANY, cur(ATTN_W), cur(K2_W), nxt(K2_W), cur(K2_W), nxt(K2_W),
                  cur(LANES), cur(LANES), cur(LANES), one, one],
        out_specs=(cur(QKV_W), pl.BlockSpec((QKV_W, BLK), lambda n: (0, n)),
                   pl.BlockSpec((SUBLANES, LANES), lambda n: (0, 0))),
        input_output_aliases={1: 0, 2: 1},
        compiler_params=_params("arbitrary"))(z, dz, dzt, dq, dkc, dkp, dvc, dvp, ra, rbm, rbp, gq2, gk2)


def _in_bwd(dz, w_pairs, x, dx1, g1, tm):
    s = x.shape[0]
    sub = min(SUB_ROWS, tm)
    nsub = tm // sub
    half = max(tm // 2, sub)

    def body(d_ref, w_ref, x_hbm, dx1_hbm, g_ref, gx_ref, acc_ref, x_buf, dx1_buf, sems):
        i, k = pl.program_id(0), pl.program_id(1)

        def fetch(r):
            rows = pl.ds(pl.multiple_of(i * tm, tm) + r * sub, sub)
            slot = r % 2
            return [pltpu.make_async_copy(x_hbm.at[rows], x_buf.at[slot], sems.at[0, slot]),
                    pltpu.make_async_copy(dx1_hbm.at[rows], dx1_buf.at[slot], sems.at[1, slot])]

        def matmul(first):
            for r0 in range(0, tm, half):
                rb = slice(r0, r0 + half)
                if first:
                    gx_ref[rb, :] = _dot(d_ref[rb, :], w_ref[0])
                else:
                    gx_ref[rb, :] += _dot(d_ref[rb, :], w_ref[0])

        @pl.when(k == 0)
        def _():
            matmul(True)

        @pl.when((k > 0) & (k < N_PAIRS - 1))
        def _():
            matmul(False)

        @pl.when((i == 0) & (k == 0))
        def _():
            acc_ref[...] = jnp.zeros_like(acc_ref)

        @pl.when(k == N_PAIRS - 1)
        def _():
            for cp in fetch(0):
                cp.start()
            matmul(False)
            for r in range(nsub):
                for cp in fetch(r):
                    cp.wait()
                if r + 1 < nsub:
                    for cp in fetch(r + 1):
                        cp.start()
                rb = slice(r * sub, (r + 1) * sub)
                dh = gx_ref[rb, :]
                xn, rr = _rms(x_buf[r % 2])
                gx_ref[rb, :] = dx1_buf[r % 2] + _rms_bwd(dh * g_ref[...], xn, rr)
                acc_ref[0:1, :] += jnp.sum(dh * xn, axis=0, keepdims=True)

    return pl.pallas_call(
        body, name="in_bwd",
        out_shape=(jax.ShapeDtypeStruct((s, D_MODEL), F32), jax.ShapeDtypeStruct((SUBLANES, D_MODEL), F32)),
        grid=(s // tm, N_PAIRS),
        in_specs=[pl.BlockSpec((tm, PAIR_W), lambda i, k: (i, k)),
                  pl.BlockSpec((1, PAIR_W, D_MODEL), lambda i, k: (k, 0, 0)),
                  ANY, ANY, pl.BlockSpec((1, D_MODEL), lambda i, k: (0, 0))],
        out_specs=(pl.BlockSpec((tm, D_MODEL), lambda i, k: (i, 0)),
                   pl.BlockSpec((SUBLANES, D_MODEL), lambda i, k: (0, 0))),
        scratch_shapes=[pltpu.VMEM((2, sub, D_MODEL), F32), pltpu.VMEM((2, sub, D_MODEL), F32),
                        pltpu.SemaphoreType.DMA((2, 2))],
        compiler_params=_params("arbitrary", "arbitrary"))(dz, w_pairs, x, dx1, g1)


def _mm_grad(at, bs, tn, name):
    m, kdim = at.shape
    nblk = [b.shape[1] // tn for b in bs]
    starts = [sum(nblk[:t]) for t in range(len(bs))]

    def body(a_ref, *refs):
        b_refs, o_ref = refs[:len(bs)], refs[len(bs)]
        j = pl.program_id(0)
        for t, b_ref in enumerate(b_refs):
            @pl.when((j >= starts[t]) & (j < starts[t] + nblk[t]))
            def _():
                o_ref[...] = _dot(a_ref[...], b_ref[...]).astype(BF16)

    def b_spec(t):
        return pl.BlockSpec((kdim, tn), lambda j: (0, jnp.clip(j - starts[t], 0, nblk[t] - 1)))

    return pl.pallas_call(
        body, name=name,
        out_shape=jax.ShapeDtypeStruct((m, sum(nblk) * tn), BF16),
        grid=(sum(nblk),),
        in_specs=[_resident((m, kdim))] + [b_spec(t) for t in range(len(bs))],
        out_specs=pl.BlockSpec((m, tn), lambda j: (0, j)),
        compiler_params=_params("parallel"))(at, *bs)


def _grad_w_in(dzt, h):
    kdim = h.shape[0]

    def body(d_ref, h_ref, o_ref):
        o_ref[0] = _dot(d_ref[...], h_ref[...]).astype(BF16)

    return pl.pallas_call(
        body, name="grad_w_in",
        out_shape=jax.ShapeDtypeStruct((N_DEV, SHARD_IN, D_MODEL), BF16),
        grid=(N_DEV,),
        in_specs=[pl.BlockSpec((SHARD_IN, kdim), lambda j: (j, 0)), _resident((kdim, D_MODEL))],
        out_specs=pl.BlockSpec((1, SHARD_IN, D_MODEL), lambda j: (j, 0, 0)),
        compiler_params=_params("parallel"))(dzt, h)


def _place():
    return lax.axis_index("x"), lax.axis_index("y"), lax.axis_index("c")


ROW_CONV, ROW_LOSS = 7, 10


def _reduce_small(acc_g1, acc_g2, acc_ple, acc_qk, acc_attn):
    def body(g1_ref, g2_ref, ple_ref, qk_ref, attn_ref, out_ref, slab_ref, gath_ref, send_sems, recv_sems):
        x, y, c = _place()
        me = 4 * x + 2 * y + c
        slab_ref[...] = jnp.zeros_like(slab_ref)
        slab_ref[0:1, :] = g1_ref[0:1, :]
        slab_ref[1:2, :] = g2_ref[0:1, :]
        slab_ref[2:4, :] = ple_ref[0:2, :]
        qk = qk_ref[0:2, :]
        slab_ref[4:6, 0:LANES] = jnp.where(_low_half(qk.shape), qk + pltpu.roll(qk, HEAD, 1), 0.0)
        lane = lax.broadcasted_iota(jnp.int32, (1, LANES), 1)
        slab_ref[6:7, 0:LANES] = jnp.where(lane < N_Q_HEADS, attn_ref[0:1, 0:LANES], 0.0)
        slab_ref[ROW_CONV:ROW_CONV + 3, 0:ATTN_W] = attn_ref[1:4, :]
        slab_ref[ROW_LOSS:ROW_LOSS + 1, :] = ple_ref[2:3, :]
        gath_ref[me] = slab_ref[...]
        copies = []
        for k in range(1, N_DEV):
            peer = (x ^ (k >> 2), y ^ ((k >> 1) & 1), c ^ (k & 1))
            copies.append(pltpu.make_async_remote_copy(
                src_ref=slab_ref, dst_ref=gath_ref.at[me], send_sem=send_sems.at[k - 1],
                recv_sem=recv_sems.at[k - 1], device_id=peer, device_id_type=MESH))
        for cp in copies:
            cp.start()
        for cp in copies:
            cp.wait_recv()
        for cp in copies:
            cp.wait_send()
        total = gath_ref[0]
        for d in range(1, N_DEV):
            total = total + gath_ref[d]
        out_ref[...] = total

    vmem = pl.BlockSpec(memory_space=pltpu.VMEM)
    return pl.pallas_call(
        body, name="reduce_small",
        out_shape=jax.ShapeDtypeStruct((SLAB_ROWS, D_MODEL), F32),
        in_specs=[vmem] * 5, out_specs=vmem,
        scratch_shapes=[pltpu.VMEM((SLAB_ROWS, D_MODEL), F32), pltpu.VMEM((N_DEV, SLAB_ROWS, D_MODEL), F32),
                        pltpu.SemaphoreType.DMA((N_DEV - 1,)), pltpu.SemaphoreType.DMA((N_DEV - 1,))])(
            acc_g1, acc_g2, acc_ple, acc_qk, acc_attn)


def _pair_sum(g, r, place, tr, name):
    _, _, rows, cols = g.shape

    def body(place_ref, g_ref, r_ref, pb_ref, own_ref):
        tot = g_ref[0, 0].astype(F32) + r_ref[0].astype(F32)
        pb_ref[0] = tot.astype(BF16)

        @pl.when(pl.program_id(1) == place_ref[1])
        def _():
            own_ref[...] = tot

    grid_spec = pltpu.PrefetchScalarGridSpec(
        num_scalar_prefetch=1, grid=(rows // tr, 4),
        in_specs=[pl.BlockSpec((1, 1, tr, cols), lambda i, q, place_ref: (q, place_ref[0], i, 0)),
                  pl.BlockSpec((1, tr, cols), lambda i, q, place_ref: (q, i, 0))],
        out_specs=(pl.BlockSpec((1, tr, cols), lambda i, q, place_ref: (q, i, 0)),
                   pl.BlockSpec((tr, cols), lambda i, q, place_ref: (i, 0))))
    return pl.pallas_call(
        body, name=name, grid_spec=grid_spec,
        out_shape=(jax.ShapeDtypeStruct((4, rows, cols), BF16), jax.ShapeDtypeStruct((rows, cols), F32)),
        compiler_params=_params("arbitrary", "arbitrary"))(place, g, r)


HBM = pl.BlockSpec(memory_space=pltpu.HBM)
SEM = pl.BlockSpec(memory_space=pltpu.SEMAPHORE)
SIDE_EFFECT = pltpu.CompilerParams(has_side_effects=pltpu.SideEffectType.DATAFLOW_SIDE_EFFECTING)
TOKEN = jax.ShapeDtypeStruct((SUBLANES, LANES), F32)


def _hbm(a):
    return pltpu.with_memory_space_constraint(a, pltpu.HBM)


def _hbm_like(arrays):
    return tuple(pltpu.HBM(a.shape, a.dtype) for a in arrays)


def _block_of(px, py, pc):
    return 4 * px + 2 * py + pc


def _gather_start(shards, after):
    na = len(shards)
    lands = [_hbm(lax.empty((N_DEV,) + a.shape, a.dtype)) for a in shards]

    def body(*refs):
        ins, land = refs[:na], refs[na:2 * na]
        send_sems, recv_ici, recv_d2d = refs[2 * na + 1:2 * na + 4]
        token = refs[-1]
        x, y, c = _place()
        for k, peer in enumerate([(x, y, 1 - c), (1 - x, y, c), (x, 1 - y, c), (1 - x, 1 - y, c)]):
            for t in range(na):
                pltpu.make_async_remote_copy(
                    src_ref=ins[t], dst_ref=land[t].at[_block_of(x, y, c)], send_sem=send_sems.at[4 * t + k],
                    recv_sem=recv_d2d.at[4 * t] if k == 0 else recv_ici.at[3 * t + k - 1],
                    device_id=peer, device_id_type=MESH).start()
        token[...] = jnp.zeros_like(token)

    out = pl.pallas_call(
        body, name="gather_start",
        out_shape=(pltpu.SemaphoreType.DMA((4 * na,)), pltpu.SemaphoreType.DMA((3 * na,)),
                   pltpu.SemaphoreType.DMA((4 * na,)), *_hbm_like(lands), TOKEN),
        in_specs=[ANY] * na + [HBM] * na + [ANY],
        out_specs=(SEM, SEM, SEM, *[HBM] * na, pl.BlockSpec(memory_space=pltpu.VMEM)),
        input_output_aliases={na + i: 3 + i for i in range(na)},
        compiler_params=SIDE_EFFECT)(*shards, *lands, after)
    send_sems, recv_ici, recv_d2d = out[:3]
    state = dict(send=send_sems, ici=recv_ici, d2d=recv_d2d, shards=list(shards), lands=out[3:3 + na])
    return state, out[-1]


def _gather_forward(state, after):
    lands = state["lands"]
    na = len(lands)

    def body(*refs):
        land = refs[:na]
        recv_ici, recv_d2d = refs[na], refs[na + 1]
        fwd_sems, token = refs[-2], refs[-1]
        x, y, c = _place()
        for j, chip in enumerate([(1 - x, y), (x, 1 - y), (1 - x, 1 - y)]):
            for t in range(na):
                blk = land[t].at[_block_of(*chip, c)]
                pltpu.make_async_remote_copy(
                    src_ref=blk, dst_ref=blk, send_sem=fwd_sems.at[3 * t + j], recv_sem=recv_ici.at[3 * t + j],
                    device_id=(x, y, c), device_id_type=MESH).wait_recv()
                pltpu.make_async_remote_copy(
                    src_ref=blk, dst_ref=blk, send_sem=fwd_sems.at[3 * t + j], recv_sem=recv_d2d.at[4 * t + 1 + j],
                    device_id=(x, y, 1 - c), device_id_type=MESH).start()
        token[...] = jnp.zeros_like(token)

    out = pl.pallas_call(
        body, name="gather_forward",
        out_shape=(*_hbm_like(lands), pltpu.SemaphoreType.DMA((3 * na,)), TOKEN),
        in_specs=[HBM] * na + [SEM, SEM, ANY],
        out_specs=(*[HBM] * na, SEM, pl.BlockSpec(memory_space=pltpu.VMEM)),
        input_output_aliases={i: i for i in range(na)},
        compiler_params=SIDE_EFFECT)(*lands, state["ici"], state["d2d"], after)
    return dict(state, lands=out[:na], fwd=out[na]), out[-1]


def _gather_wait(state, after):
    shards, lands = state["shards"], state["lands"]
    na = len(lands)

    def body(*refs):
        ins, land = refs[:na], refs[na:2 * na]
        send_sems, fwd_sems, recv_d2d = refs[2 * na:2 * na + 3]
        x, y, c = _place()
        chips = [(1 - x, y), (x, 1 - y), (1 - x, 1 - y)]
        for t in range(na):
            mine = land[t].at[_block_of(x, y, c)]
            for k in range(4):
                pltpu.make_async_remote_copy(
                    src_ref=ins[t], dst_ref=mine, send_sem=send_sems.at[4 * t + k], recv_sem=recv_d2d.at[4 * t],
                    device_id=(x, y, c), device_id_type=MESH).wait_send()
            for j, chip in enumerate(chips):
                blk = land[t].at[_block_of(*chip, c)]
                pltpu.make_async_remote_copy(
                    src_ref=blk, dst_ref=blk, send_sem=fwd_sems.at[3 * t + j], recv_sem=recv_d2d.at[4 * t + 1 + j],
                    device_id=(x, y, c), device_id_type=MESH).wait_send()
            for k, blk_id in enumerate([_block_of(x, y, 1 - c)] + [_block_of(*chip, 1 - c) for chip in chips]):
                blk = land[t].at[blk_id]
                pltpu.make_async_remote_copy(
                    src_ref=blk, dst_ref=blk, send_sem=send_sems.at[4 * t], recv_sem=recv_d2d.at[4 * t + k],
                    device_id=(x, y, c), device_id_type=MESH).wait_recv()

    out = pl.pallas_call(
        body, name="gather_wait",
        out_shape=_hbm_like(lands),
        in_specs=[ANY] * na + [HBM] * na + [SEM, SEM, SEM, ANY],
        out_specs=tuple([HBM] * na),
        input_output_aliases={na + i: i for i in range(na)},
        compiler_params=SIDE_EFFECT)(*shards, *lands, state["send"], state["fwd"], state["d2d"], after)
    return out


def _gather_from_sibling(state, after):
    lands = state["lands"]
    na = len(lands)

    def body(*refs):
        land, recv_d2d = refs[:na], refs[na]
        x, y, c = _place()
        for t in range(na):
            blk = land[t].at[_block_of(x, y, 1 - c)]
            pltpu.make_async_remote_copy(src_ref=blk, dst_ref=blk, send_sem=recv_d2d.at[4 * t],
                                         recv_sem=recv_d2d.at[4 * t], device_id=(x, y, c),
                                         device_id_type=MESH).wait_recv()

    out = pl.pallas_call(
        body, name="gather_from_sibling", out_shape=_hbm_like(lands),
        in_specs=[HBM] * na + [SEM, ANY], out_specs=tuple([HBM] * na),
        input_output_aliases={i: i for i in range(na)},
        compiler_params=SIDE_EFFECT)(*lands, state["d2d"], after)
    return dict(state, lands=list(out))


def _gather_from_chip(state, j, after, last):
    shards, lands = state["shards"], state["lands"]
    na = len(lands)

    def chip_blocks(land_ref):
        x, y, c = _place()
        chip = [(1 - x, y), (x, 1 - y), (1 - x, 1 - y)][j]
        return (x, y, c), land_ref.at[_block_of(*chip, c)], land_ref.at[_block_of(*chip, 1 - c)]

    def forward(*refs):
        land, recv_ici, recv_d2d, fwd_sems = refs[:na], refs[na], refs[na + 1], refs[-1]
        for t in range(na):
            (x, y, c), mine, _ = chip_blocks(land[t])
            pltpu.make_async_remote_copy(src_ref=mine, dst_ref=mine, send_sem=fwd_sems.at[t],
                                         recv_sem=recv_ici.at[3 * t + j], device_id=(x, y, c),
                                         device_id_type=MESH).wait_recv()
            pltpu.make_async_remote_copy(src_ref=mine, dst_ref=mine, send_sem=fwd_sems.at[t],
                                         recv_sem=recv_d2d.at[4 * t + 1 + j], device_id=(x, y, 1 - c),
                                         device_id_type=MESH).start()

    out = pl.pallas_call(
        forward, name="gather_pass_chip_" + str(j),
        out_shape=(*_hbm_like(lands), pltpu.SemaphoreType.DMA((na,))),
        in_specs=[HBM] * na + [SEM, SEM, ANY], out_specs=(*[HBM] * na, SEM),
        input_output_aliases={i: i for i in range(na)},
        compiler_params=SIDE_EFFECT)(*lands, state["ici"], state["d2d"], after)
    lands, fwd_sems = out[:na], out[na]

    def arrive(*refs):
        land, fwd_sems, recv_d2d = refs[:na], refs[na], refs[na + 1]
        shard, send_sems = refs[na + 2:2 * na + 2], refs[2 * na + 2]
        for t in range(na):
            (x, y, c), mine, theirs = chip_blocks(land[t])
            pltpu.make_async_remote_copy(src_ref=theirs, dst_ref=theirs, send_sem=fwd_sems.at[t],
                                         recv_sem=recv_d2d.at[4 * t + 1 + j], device_id=(x, y, c),
                                         device_id_type=MESH).wait_recv()
            pltpu.make_async_remote_copy(src_ref=mine, dst_ref=mine, send_sem=fwd_sems.at[t],
                                         recv_sem=recv_d2d.at[4 * t + 1 + j], device_id=(x, y, c),
                                         device_id_type=MESH).wait_send()
            for k in range(4 if last else 0):
                pltpu.make_async_remote_copy(
                    src_ref=shard[t], dst_ref=land[t].at[_block_of(x, y, c)], send_sem=send_sems.at[4 * t + k],
                    recv_sem=recv_d2d.at[4 * t], device_id=(x, y, c), device_id_type=MESH).wait_send()

    out = pl.pallas_call(
        arrive, name="gather_take_chip_" + str(j), out_shape=_hbm_like(lands),
        in_specs=[HBM] * na + [SEM, SEM] + [ANY] * na + [SEM], out_specs=tuple([HBM] * na),
        input_output_aliases={i: i for i in range(na)},
        compiler_params=SIDE_EFFECT)(*lands, fwd_sems, state["d2d"], *shards, state["send"])
    return dict(state, lands=list(out))


def _to_sibling(srcs, lands, send_sems, recv_sems):
    x, y, c = _place()
    return [pltpu.make_async_remote_copy(
        src_ref=srcs[t].at[:, 1 - c], dst_ref=lands[t], send_sem=send_sems.at[t], recv_sem=recv_sems.at[t],
        device_id=(x, y, 1 - c), device_id_type=MESH) for t in range(len(srcs))]


def _to_chips(srcs, lands, send_sems, recv_sems):
    x, y, c = _place()
    copies = []
    for k in (1, 2, 3):
        px, py = x ^ (k >> 1), y ^ (k & 1)
        copies += [pltpu.make_async_remote_copy(
            src_ref=srcs[t].at[2 * px + py], dst_ref=lands[t].at[k - 1], send_sem=send_sems.at[3 * t + k - 1],
            recv_sem=recv_sems.at[3 * t + k - 1], device_id=(px, py, c), device_id_type=MESH) for t in range(len(srcs))]
    return copies


def _exchange_start(name, srcs, land_shapes, copies, per_array, after):
    na = len(srcs)
    lands = [_hbm(lax.empty(shp, a.dtype)) for shp, a in zip(land_shapes, srcs)]

    def body(*refs):
        token = refs[-1]
        for cp in copies(refs[:na], refs[na:2 * na], refs[2 * na + 1], refs[2 * na + 2]):
            cp.start()
        token[...] = jnp.zeros_like(token)

    out = pl.pallas_call(
        body, name=name,
        out_shape=(pltpu.SemaphoreType.DMA((na * per_array,)), pltpu.SemaphoreType.DMA((na * per_array,)),
                   *_hbm_like(lands), TOKEN),
        in_specs=[ANY] * na + [HBM] * na + [ANY],
        out_specs=(SEM, SEM, *[HBM] * na, pl.BlockSpec(memory_space=pltpu.VMEM)),
        input_output_aliases={na + i: 2 + i for i in range(na)},
        compiler_params=SIDE_EFFECT)(*srcs, *lands, after)
    return dict(send=out[0], recv=out[1], srcs=list(srcs), lands=out[2:2 + na]), out[-1]


def _exchange_wait(name, state, copies, after):
    srcs, lands = state["srcs"], state["lands"]
    na = len(srcs)

    def body(*refs):
        for cp in copies(refs[:na], refs[na:2 * na], refs[2 * na], refs[2 * na + 1]):
            cp.wait_send()
            cp.wait_recv()

    out = pl.pallas_call(
        body, name=name,
        out_shape=_hbm_like(lands),
        in_specs=[ANY] * na + [HBM] * na + [SEM, SEM, ANY],
        out_specs=tuple([HBM] * na),
        input_output_aliases={na + i: i for i in range(na)},
        compiler_params=SIDE_EFFECT)(*srcs, *lands, state["send"], state["recv"], after)
    return out


def _adamw_math(w, g, m, v):
    m = ADAM_B1 * m + (1.0 - ADAM_B1) * g
    v = ADAM_B2 * v + (1.0 - ADAM_B2) * (g * g)
    m_hat = m / (1.0 - ADAM_B1 ** ADAM_STEP)
    v_hat = v / (1.0 - ADAM_B2 ** ADAM_STEP)
    return -ADAM_LR * (m_hat / (jnp.sqrt(v_hat) + ADAM_EPS) + ADAM_WD * w), m, v


def _adamw(own, others, w, m, v, tr, name, after):
    rows, cols = w.shape
    blk = pl.BlockSpec((tr, cols), lambda i: (i, 0))

    def body(own_ref, oth_ref, w_ref, m_ref, v_ref, after_ref, g_ref, d_ref, nm_ref, nv_ref):
        g = own_ref[...]
        for k in range(3):
            g = g + oth_ref[k].astype(F32)
        g_ref[...] = g
        d_ref[...], nm_ref[...], nv_ref[...] = _adamw_math(w_ref[...], g, m_ref[...], v_ref[...])

    out = jax.ShapeDtypeStruct((rows, cols), F32)
    return pl.pallas_call(
        body, name=name, out_shape=(out, out, out, out), grid=(rows // tr,),
        in_specs=[blk, pl.BlockSpec((3, tr, cols), lambda i: (0, i, 0)), blk, blk, blk, ANY],
        out_specs=(blk, blk, blk, blk),
        compiler_params=_params("parallel"))(own, others, w, m, v, after)


def _adamw_small(red, me, params, moments1, moments2):
    n = len(params)

    def body(me_ref, red_ref, *refs):
        ws, ms, vs = refs[:n], refs[n:2 * n], refs[2 * n:3 * n]
        loss_ref = refs[3 * n]
        outs = refs[3 * n + 1:]
        loss_ref[...] = jnp.sum(red_ref[ROW_LOSS:ROW_LOSS + 1, :], axis=-1, keepdims=True)
        for t in range(n):
            if t < n - 1:
                g = red_ref[t:t + 1, 0:ws[t].shape[1]]
            else:
                g = red_ref[ROW_CONV:ROW_CONV + 3, pl.ds(pl.multiple_of(me_ref[0, 0] * LANES, LANES), LANES)]
            d, nm, nv = _adamw_math(ws[t][...], g, ms[t][...], vs[t][...])
            for o, val in zip(outs[4 * t:4 * t + 4], (g, d, nm, nv)):
                o[...] = val

    vmem = pl.BlockSpec(memory_space=pltpu.VMEM)
    shapes = [jax.ShapeDtypeStruct(w.shape, F32) for w in params for _ in range(4)]
    out = pl.pallas_call(
        body, name="adamw_small", out_shape=(jax.ShapeDtypeStruct((1, 1), F32), *shapes),
        in_specs=[pl.BlockSpec(memory_space=pltpu.SMEM), vmem] + [vmem] * (3 * n),
        out_specs=tuple([vmem] * (1 + 4 * n)))(me, red, *params, *moments1, *moments2)
    return out[0], [list(out[1 + k::4]) for k in range(4)]


def _tables(s, gq, gk, conv_w):
    gq2 = jnp.tile(gq.reshape(1, HEAD), (1, 2))
    gk2 = jnp.tile(gk.reshape(1, HEAD), (1, 2))
    conv_wp = jnp.pad(conv_w, ((0, SUBLANES - conv_w.shape[0]), (0, 0)))
    return _rope_tables(s), gq2, gk2, conv_wp


def _pair_id(q):
    return jnp.full((1,), q, jnp.int32)


def _forward_in(x, g1, shards):
    s = x.shape[0]
    h = _prenorm(x, g1, min(512, s))
    z, w_pairs = lax.empty((s, IN_W), F32), lax.empty((N_PAIRS, PAIR_W, D_MODEL), BF16)
    for q in range(N_PAIRS):
        z, w_pairs = _fwd_in_pair(h, shards, z, w_pairs, _pair_id(q), min(512, s), "fwd_in_" + str(q))
    return h, z, w_pairs


def _forward_attn(z, rope, gq2, gk2, conv_wp, sinks):
    s = z.shape[0]
    qn, k2, v2 = _qk_prep(z, *rope, gq2, gk2, min(256, s))
    a, mix, mixt = _attn_fwd(qn, k2, v2, z, conv_wp, sinks)
    return qn, k2, v2, a, mix, mixt


def _forward_out(x, p, target, mix, mixt, w_out, g2, w_pg, b_pg, w_pp, g3):
    s = x.shape[0]
    tm = min(512, s)
    x1, hn2, hn2t = _fwd_out(mix, w_out, x, g2, tm)
    dy, dgp, dt, pt, acc_ple = _ple(hn2, w_pg, b_pg, p, w_pp, g3, x1, target, min(256, s))
    dx1, dx1b, acc_g2 = _gate_bwd(dgp, w_pg, x1, dy, g2, tm)
    gw_out = _mm_grad(mixt, [dx1b], 512, "grad_w_out")
    gw_pg = _mm_grad(hn2t, [dgp], 512, "grad_w_ple_gate")
    gw_pp = _mm_grad(pt, [dt], 512, "grad_w_ple_proj")
    return dx1, dx1b, (gw_out, gw_pg, gw_pp), acc_ple, acc_g2


def _backward_attn(dmix, h, z, qn, k2, v2, a, rope, gq2, gk2, conv_wp, sinks):
    dq, dkc, dkp, dvc, dvp, dz, dzt, acc_attn = _attn_bwd(qn, k2, v2, a, z, dmix, conv_wp, sinks)
    dz, dzt, acc_qk = _qkv_bwd(z, dz, dzt, dq, dkc, dkp, dvc, dvp, *rope, gq2, gk2)
    return dz, _grad_w_in(dzt, h), acc_attn, acc_qk


def _local_step(x, p, target, g1, shards, gq, gk, sinks, conv_w, w_out, g2, w_pg, b_pg, w_pp, g3):
    rope, gq2, gk2, conv_wp = _tables(x.shape[0], gq, gk, conv_w)
    h, z, w_pairs = _forward_in(x, g1, shards)
    qn, k2, v2, a, mix, mixt = _forward_attn(z, rope, gq2, gk2, conv_wp, sinks)
    dx1, dx1b, (gw_out, gw_pg, gw_pp), acc_ple, acc_g2 = _forward_out(
        x, p, target, mix, mixt, w_out, g2, w_pg, b_pg, w_pp, g3)
    dmix = _mm_nt(dx1b, w_out, min(512, x.shape[0]), "out_bwd", dx1b)
    dz, gw_in, acc_attn, acc_qk = _backward_attn(dmix, h, z, qn, k2, v2, a, rope, gq2, gk2, conv_wp, sinks)
    grad_x, acc_g1 = _in_bwd(dz, w_pairs, x, dx1, g1, min(1024, x.shape[0]))
    return grad_x, (gw_in, gw_out, gw_pg, gw_pp), (acc_g1, acc_g2, acc_ple, acc_qk, acc_attn)


def _by_owner(g):
    return g.reshape((4, 2) + g.shape[1:])


def kernel(x, p, norm_gain, w_in, q_norm_gain, k_norm_gain, attn_sinks, conv_w, w_out, ple_gate_norm_gain, w_ple_gate, b_ple_gate, w_ple_proj, ple_norm_gain, loss_target, m_norm_gain, m_w_in, m_q_norm_gain, m_k_norm_gain, m_attn_sinks, m_conv_w, m_w_out, m_ple_gate_norm_gain, m_w_ple_gate, m_b_ple_gate, m_w_ple_proj, m_ple_norm_gain, v_norm_gain, v_w_in, v_q_norm_gain, v_k_norm_gain, v_attn_sinks, v_conv_w, v_w_out, v_ple_gate_norm_gain, v_w_ple_gate, v_b_ple_gate, v_w_ple_proj, v_ple_norm_gain):
    me = 4 * lax.axis_index("x") + 2 * lax.axis_index("y") + lax.axis_index("c")
    place = jnp.stack([lax.axis_index("c"), 2 * lax.axis_index("x") + lax.axis_index("y")]).astype(jnp.int32)
    xs, ps, target = x[0], p[0, 0], loss_target[0]
    zero = lambda token: token[0:1, 0:1]

    shard_in = w_in[0].T.astype(BF16)
    own_late = [w_out[0].astype(BF16), w_ple_gate[0].astype(BF16), w_ple_proj[0].astype(BF16)]
    with_own = lambda gathered, own: lax.dynamic_update_slice(gathered, own[None], (me,) + (0,) * own.ndim)
    tie = lambda *arrays: sum(t[(slice(0, 1),) * t.ndim].reshape(1).astype(F32) for t in arrays)
    early, started = _gather_start([shard_in, conv_w[0]], shard_in)
    tm = min(512, xs.shape[0])
    h = _prenorm(xs, norm_gain + zero(started), tm)

    z, w_pairs = lax.empty((xs.shape[0], IN_W), F32), lax.empty((N_PAIRS, PAIR_W, D_MODEL), BF16)
    early = _gather_from_sibling(early, h)
    early = dict(early, lands=[with_own(early["lands"][0], shard_in), early["lands"][1]])
    z, w_pairs = _fwd_in_pair(h, early["lands"][0], z, w_pairs, place[1:2], tm, "fwd_in_own")
    for j, flip in enumerate((2, 1, 3)):
        early = _gather_from_chip(early, j, z if j != 1 else tie(z, started_late), last=j == 2)
        z, w_pairs = _fwd_in_pair(h, early["lands"][0], z, w_pairs, place[1:2] ^ flip, tm, "fwd_in_chip_" + str(j))
        if j == 0:
            late, started_late = _gather_start(own_late, z)
    conv_full = jnp.transpose(with_own(early["lands"][1], conv_w[0]), (1, 0, 2)).reshape(3, ATTN_W)
    rope, gq2, gk2, conv_wp = _tables(xs.shape[0], q_norm_gain[0], k_norm_gain[0], conv_full)
    late, forwarded = _gather_forward(late, z)
    qn, k2, v2, a, mix, mixt = _forward_attn(z, rope, gq2 + zero(forwarded), gk2, conv_wp, attn_sinks)
    g_out, g_pg, g_pp = (with_own(g, own) for g, own in zip(_gather_wait(late, mix), own_late))
    w_out_f = g_out.reshape(D_MODEL, D_MODEL)
    w_pg_f = g_pg.reshape(D_MODEL, D_MODEL)
    w_pp_f = jnp.transpose(g_pp, (1, 0, 2)).reshape(PLE_DIM, D_MODEL)

    dx1, dx1b, (gw_out, gw_pg, gw_pp), acc_ple, acc_g2 = _forward_out(
        xs, ps, target, mix, mixt, w_out_f, ple_gate_norm_gain, w_pg_f, b_ple_gate, w_pp_f, ple_norm_gain)

    names = ("w_out", "w_ple_gate", "w_ple_proj")
    gw_pp_t = jnp.transpose(gw_pp.reshape(PLE_DIM, N_DEV, PLE_DIM), (1, 0, 2))
    grads = [_by_owner(gw_out.reshape(N_DEV, D_MODEL // N_DEV, D_MODEL)),
             _by_owner(gw_pg.reshape(N_DEV, D_MODEL // N_DEV, D_MODEL)), _by_owner(gw_pp_t)]
    pairs, paired = _exchange_start("pair_start", grads, [(4,) + g.shape[2:] for g in grads], _to_sibling, 1, dx1b)
    dmix = _mm_nt(dx1b, w_out_f, tm, "out_bwd", paired)
    from_sibling = _exchange_wait("pair_wait", pairs, _to_sibling, dmix)
    sums = [_pair_sum(g, r, place, 256, "pair_sum_" + nm) for g, r, nm in zip(pairs["srcs"], from_sibling, names)]
    chips, sent = _exchange_start("chip_start", [pb for pb, _ in sums], [(3,) + pb.shape[1:] for pb, _ in sums],
                                  _to_chips, 3, sums[-1][1])

    dz, gw_in, acc_attn, acc_qk = _backward_attn(
        dmix, h, z, qn, k2, v2, a, rope, gq2, gk2, conv_wp, attn_sinks + zero(sent))

    gw_in_t = [_by_owner(gw_in)]
    pairs_in, paired_in = _exchange_start("pair_start_w_in", gw_in_t, [(4,) + gw_in_t[0].shape[2:]], _to_sibling, 1,
                                          gw_in)
    from_chips = _exchange_wait("chip_wait", chips, _to_chips, gw_in)
    big = {}
    for (_, own), oth, w, m, v, nm in zip(sums, from_chips, (w_out, w_ple_gate, w_ple_proj),
                                          (m_w_out, m_w_ple_gate, m_w_ple_proj),
                                          (v_w_out, v_w_ple_gate, v_w_ple_proj), names):
        big[nm] = [t[None] for t in _adamw(own, oth, w[0], m[0], v[0], 256, "adamw_" + nm, paired_in)]

    (from_sibling_in,) = _exchange_wait("pair_wait_w_in", pairs_in, _to_sibling, tie(*[big[nm][0] for nm in names]))
    pb_in, own_in = _pair_sum(pairs_in["srcs"][0], from_sibling_in, place, SHARD_IN // 2, "pair_sum_w_in")
    chips_in, sent_in = _exchange_start("chip_start_w_in", [pb_in], [(3,) + pb_in.shape[1:]], _to_chips, 3, own_in)
    grad_x, acc_g1 = _in_bwd(dz, w_pairs, xs, dx1, norm_gain + zero(sent_in), min(1024, xs.shape[0]))
    (from_chips_in,) = _exchange_wait("chip_wait_w_in", chips_in, _to_chips, grad_x)
    big["w_in"] = [t.T[None] for t in _adamw(own_in, from_chips_in, w_in[0].T, m_w_in[0].T, v_w_in[0].T, SHARD_IN // 4,
                                             "adamw_w_in", grad_x)]

    red = _reduce_small(acc_g1, acc_g2, acc_ple, acc_qk, acc_attn)
    small = [norm_gain, ple_gate_norm_gain, b_ple_gate, ple_norm_gain, q_norm_gain, k_norm_gain, attn_sinks]
    small_m = [m_norm_gain, m_ple_gate_norm_gain, m_b_ple_gate, m_ple_norm_gain, m_q_norm_gain, m_k_norm_gain,
               m_attn_sinks]
    small_v = [v_norm_gain, v_ple_gate_norm_gain, v_b_ple_gate, v_ple_norm_gain, v_q_norm_gain, v_k_norm_gain,
               v_attn_sinks]
    loss, kinds = _adamw_small(red, me.reshape(1, 1).astype(jnp.int32), small + [conv_w[0]], small_m + [m_conv_w[0]],
                               small_v + [v_conv_w[0]])

    def order(k):
        sm = kinds[k]
        return [sm[0], big["w_in"][k], sm[4], sm[5], sm[6], sm[7][None], big["w_out"][k], sm[1], big["w_ple_gate"][k],
                sm[2], big["w_ple_proj"][k], sm[3]]

    return (loss[0, 0], grad_x[None], *order(0), *order(1), *order(2), *order(3))
```

```python
import jax
import jax.numpy as jnp
from jax import lax
from jax.experimental import pallas as pl
from jax.experimental.pallas import tpu as pltpu

F32, BF16 = jnp.float32, jnp.bfloat16

D_MODEL = 2048
PLE_DIM = 256
ATTN_W = 1024
HEAD = 64
N_Q_HEADS = 16
KV_W = 256
QKV_W = ATTN_W + 2 * KV_W
REST_W = 5 * 1024
IN_W = QKV_W + REST_W
GATE_A0, CONV_B0, CONV_C0, CONV_H0, GATE_C0 = (QKV_W + 1024 * t for t in range(5))
K2_W = 4 * 128
ROT = 16
ROPE_THETA = 500000.0
EPS = 1e-6
NEG_INF = -1e30
BLK = 128
LANES = 128
SUBLANES = 8
N_DEV = 8
SHARD_IN = IN_W // N_DEV
PAIR_W = 2 * SHARD_IN
N_PAIRS = IN_W // PAIR_W
SLAB_ROWS = 16
SUB_ROWS = 128
V7X_VMEM_LIMIT = 52 * 1024 * 1024

ADAM_LR, ADAM_B1, ADAM_B2, ADAM_EPS, ADAM_WD, ADAM_STEP = 0.001, 0.9, 0.999, 1e-08, 0.01, 10
MESH = pl.DeviceIdType.MESH


def _params(*semantics):
    return pltpu.CompilerParams(dimension_semantics=semantics, vmem_limit_bytes=V7X_VMEM_LIMIT)


ANY = pl.BlockSpec(memory_space=pl.ANY)


def _resident(shape):
    return pl.BlockSpec(shape, lambda *_: (0,) * len(shape), pipeline_mode=pl.Buffered(1))


def _dot(a, b):
    return jnp.dot(a, b, preferred_element_type=F32)


def _dot_nt(a, b):
    return lax.dot_general(a, b, (((1,), (1,)), ((), ())), preferred_element_type=F32)


def _rms(xf):
    r = lax.rsqrt(jnp.mean(xf * xf, axis=-1, keepdims=True) + EPS)
    return xf * r, r


def _rms_bwd(dxn, xn, r):
    return r * (dxn - xn * jnp.mean(dxn * xn, axis=-1, keepdims=True))


def _sig(g):
    return jax.nn.sigmoid(g)


def _dsilu(g, sg):
    return sg * (1.0 + g * (1.0 - sg))


def _low_half(shape):
    return lax.broadcasted_iota(jnp.int32, shape, len(shape) - 1) < HEAD


def _half_sums(v):
    lo = _low_half(v.shape)
    s_lo = jnp.sum(jnp.where(lo, v, 0.0), axis=-1, keepdims=True)
    s_hi = jnp.sum(jnp.where(lo, 0.0, v), axis=-1, keepdims=True)
    return jnp.where(lo, s_lo, s_hi)


def _rope(v, a, bm, bp):
    return v * a + pltpu.roll(v, LANES - ROT // 2, 1) * bm + pltpu.roll(v, ROT // 2, 1) * bp


def _rope_t(dy, a, bm, bp):
    return dy * a + pltpu.roll(dy * bm, ROT // 2, 1) + pltpu.roll(dy * bp, LANES - ROT // 2, 1)


def _dup_halves(v):
    lo = _low_half(v.shape)
    a = jnp.where(lo, v, 0.0)
    b = jnp.where(lo, 0.0, v)
    return a + pltpu.roll(a, HEAD, 1), b + pltpu.roll(b, HEAD, 1)


def _rope_tables(s):
    half = ROT // 2
    lane = lax.broadcasted_iota(jnp.int32, (s, LANES), 1) % HEAD
    pos = lax.broadcasted_iota(jnp.int32, (s, LANES), 0).astype(F32)
    inv_freq = jnp.power(jnp.float32(ROPE_THETA), -(lane % half).astype(F32) * 2.0 / ROT)
    ang = pos * inv_freq
    cos, sin = jnp.cos(ang), jnp.sin(ang)
    a = jnp.where(lane < ROT, cos, 1.0)
    bm = jnp.where(lane < half, -sin, 0.0)
    bp = jnp.where((lane >= half) & (lane < ROT), sin, 0.0)
    return a, bm, bp


def _prenorm(x, g1, tm, after):
    s = x.shape[0]

    def body(x_ref, g_ref, after_ref, h_ref):
        xn, _ = _rms(x_ref[...])
        h_ref[...] = (xn * g_ref[...]).astype(BF16)

    return pl.pallas_call(
        body, name="prenorm",
        out_shape=jax.ShapeDtypeStruct((s, D_MODEL), BF16),
        grid=(s // tm,),
        in_specs=[pl.BlockSpec((tm, D_MODEL), lambda i: (i, 0)), pl.BlockSpec((1, D_MODEL), lambda i: (0, 0)), ANY],
        out_specs=pl.BlockSpec((tm, D_MODEL), lambda i: (i, 0)),
        compiler_params=_params("parallel"))(x, g1, after)


def _fwd_in_pair(h, shards, z, w_pairs, pair, tm, name):
    s = h.shape[0]

    def body(pair_ref, h_ref, lo_ref, hi_ref, z_in, wp_in, z_ref, wp_ref):
        @pl.when(pl.program_id(0) == 0)
        def _():
            wp_ref[0, 0:SHARD_IN, :] = lo_ref[0]
            wp_ref[0, SHARD_IN:PAIR_W, :] = hi_ref[0]

        z_ref[...] = _dot_nt(h_ref[...], wp_ref[0])

    grid_spec = pltpu.PrefetchScalarGridSpec(
        num_scalar_prefetch=1, grid=(s // tm,),
        in_specs=[pl.BlockSpec((tm, D_MODEL), lambda i, p: (i, 0)),
                  pl.BlockSpec((1, SHARD_IN, D_MODEL), lambda i, p: (2 * p[0], 0, 0)),
                  pl.BlockSpec((1, SHARD_IN, D_MODEL), lambda i, p: (2 * p[0] + 1, 0, 0)), ANY, ANY],
        out_specs=(pl.BlockSpec((tm, PAIR_W), lambda i, p: (i, p[0])),
                   pl.BlockSpec((1, PAIR_W, D_MODEL), lambda i, p: (p[0], 0, 0))))
    return pl.pallas_call(
        body, name=name, grid_spec=grid_spec,
        out_shape=(jax.ShapeDtypeStruct(z.shape, z.dtype), jax.ShapeDtypeStruct(w_pairs.shape, w_pairs.dtype)),
        input_output_aliases={4: 0, 5: 1},
        compiler_params=_params("arbitrary"))(pair, h, shards, shards, z, w_pairs)


def _qk_prep(z, ra, rbm, rbp, gq2, gk2, tm, after):
    s = z.shape[0]

    def body(z_ref, a_ref, bm_ref, bp_ref, gq_ref, gk_ref, after_ref, q_ref, k2_ref, v2_ref):
        a, bm, bp = a_ref[...], bm_ref[...], bp_ref[...]
        for r in range(ATTN_W // LANES):
            x = z_ref[:, LANES * r:LANES * (r + 1)]
            rr = lax.rsqrt(_half_sums(x * x) * (1.0 / HEAD) + EPS)
            q_ref[:, LANES * r:LANES * (r + 1)] = _rope(x * rr * gq_ref[...], a, bm, bp).astype(BF16)
        for m in range(KV_W // LANES):
            x = z_ref[:, ATTN_W + LANES * m:ATTN_W + LANES * (m + 1)]
            rr = lax.rsqrt(_half_sums(x * x) * (1.0 / HEAD) + EPS)
            k_lo, k_hi = _dup_halves(_rope(x * rr * gk_ref[...], a, bm, bp))
            k2_ref[:, 2 * LANES * m:2 * LANES * m + LANES] = k_lo.astype(BF16)
            k2_ref[:, 2 * LANES * m + LANES:2 * LANES * (m + 1)] = k_hi.astype(BF16)
            v_lo, v_hi = _dup_halves(z_ref[:, ATTN_W + KV_W + LANES * m:ATTN_W + KV_W + LANES * (m + 1)])
            v2_ref[:, 2 * LANES * m:2 * LANES * m + LANES] = v_lo.astype(BF16)
            v2_ref[:, 2 * LANES * m + LANES:2 * LANES * (m + 1)] = v_hi.astype(BF16)

    row = lambda w: pl.BlockSpec((tm, w), lambda i: (i, 0))
    one = pl.BlockSpec((1, LANES), lambda i: (0, 0))
    return pl.pallas_call(
        body, name="qk_prep",
        out_shape=(jax.ShapeDtypeStruct((s, ATTN_W), BF16), jax.ShapeDtypeStruct((s, K2_W), BF16),
                   jax.ShapeDtypeStruct((s, K2_W), BF16)),
        grid=(s // tm,),
        in_specs=[row(PAIR_W), row(LANES), row(LANES), row(LANES), one, one, ANY],
        out_specs=(row(ATTN_W), row(K2_W), row(K2_W)),
        compiler_params=_params("parallel"))(z, ra, rbm, rbp, gq2, gk2, after)


GROUP = 4


def _window_mask(n):
    row = lax.broadcasted_iota(jnp.int32, (GROUP * BLK, 2 * BLK), 0) % BLK
    col = lax.broadcasted_iota(jnp.int32, (GROUP * BLK, 2 * BLK), 1)
    return (col > row) & (col <= row + BLK) & ((col >= BLK) | (n > 0))


def _stack_heads(pairs, zero):
    lo = _low_half(pairs[0].shape)
    parts = []
    for v in pairs:
        parts += [jnp.where(lo, v, zero), jnp.where(lo, zero, v)]
    return jnp.concatenate(parts, axis=0)


def _unstack_heads(v4):
    lo = _low_half((BLK, LANES))
    return [jnp.where(lo, v4[2 * i * BLK:(2 * i + 1) * BLK], v4[(2 * i + 1) * BLK:(2 * i + 2) * BLK]) for i in range(2)]


def _group_sinks(sink_ref, kvh):
    slot = lax.broadcasted_iota(jnp.int32, (GROUP * BLK, 1), 0) // BLK
    col = jnp.zeros((GROUP * BLK, 1), F32)
    for i in range(GROUP):
        col = jnp.where(slot == i, sink_ref[0, GROUP * kvh + i], col)
    return col, slot


def _head_probs(qm, kw, valid, sink):
    sc = jnp.where(valid, _dot_nt(qm, kw) * (HEAD ** -0.5), NEG_INF)
    mx = jnp.maximum(jnp.max(sc, axis=-1, keepdims=True), sink)
    ex = jnp.exp(sc - mx)
    den = jnp.sum(ex, axis=-1, keepdims=True) + jnp.exp(sink - mx)
    return ex / den, mx, den


def _cols(start, width=ATTN_W):
    return slice(start, start + width)


def _conv_fwd(z_ref, zp_ref, cw_ref, ext_ref, n):
    u = z_ref[:, _cols(CONV_C0)] * z_ref[:, _cols(CONV_H0)]
    pu = zp_ref[:, _cols(CONV_C0)] * zp_ref[:, _cols(CONV_H0)]
    ext_ref[0:SUBLANES, :] = jnp.where(n > 0, pu, 0.0)
    ext_ref[SUBLANES:SUBLANES + BLK, :] = u
    um1 = ext_ref[SUBLANES - 1:SUBLANES - 1 + BLK, :]
    um2 = ext_ref[SUBLANES - 2:SUBLANES - 2 + BLK, :]
    cv = cw_ref[0:1, :] * um2 + cw_ref[1:2, :] * um1 + cw_ref[2:3, :] * u
    return u, um1, um2, cv


def _prev_rows(n):
    return (jnp.maximum(n * (BLK // SUBLANES) - 1, 0), 0)


def _attn_fwd(qn, k2, v2, z, conv_wp, sinks):
    s = qn.shape[0]
    nb = s // BLK

    def body(sink_ref, q_ref, kc_ref, kp_ref, vc_ref, vp_ref, z_ref, zp_ref, cw_ref, a_ref, mix_ref, mixt_ref,
             ext_ref):
        n = pl.program_id(0)
        valid = _window_mask(n)
        for kvh in range(K2_W // LANES):
            cols = slice(LANES * kvh, LANES * (kvh + 1))
            kw = jnp.concatenate([kp_ref[:, cols], kc_ref[:, cols]], axis=0)
            vw = jnp.concatenate([vp_ref[:, cols], vc_ref[:, cols]], axis=0)
            blocks = [slice(LANES * r, LANES * (r + 1)) for r in (2 * kvh, 2 * kvh + 1)]
            q4 = _stack_heads([q_ref[:, rc] for rc in blocks], jnp.zeros((BLK, LANES), BF16))
            p, _, _ = _head_probs(q4, kw, valid, _group_sinks(sink_ref, kvh)[0])
            for rc, a in zip(blocks, _unstack_heads(_dot(p.astype(BF16), vw))):
                a_ref[:, rc] = a
                g = z_ref[:, _cols(GATE_A0 + rc.start, LANES)]
                mix_ref[:, rc] = (a * (g * _sig(g))).astype(BF16)
        _, _, _, cv = _conv_fwd(z_ref, zp_ref, cw_ref, ext_ref, n)
        gc = z_ref[:, _cols(GATE_C0)]
        mix_ref[:, ATTN_W:D_MODEL] = (z_ref[:, _cols(CONV_B0)] * cv * (gc * _sig(gc))).astype(BF16)
        mixt_ref[...] = mix_ref[...].T

    cur = lambda w: pl.BlockSpec((BLK, w), lambda n: (n, 0))
    prev = lambda w: pl.BlockSpec((BLK, w), lambda n: (jnp.maximum(n - 1, 0), 0))
    return pl.pallas_call(
        body, name="attn_fwd",
        out_shape=(jax.ShapeDtypeStruct((s, ATTN_W), F32), jax.ShapeDtypeStruct((s, D_MODEL), BF16),
                   jax.ShapeDtypeStruct((D_MODEL, s), BF16)),
        grid=(nb,),
        in_specs=[pl.BlockSpec(memory_space=pltpu.SMEM),
                  cur(ATTN_W), cur(K2_W), prev(K2_W), cur(K2_W), prev(K2_W), cur(IN_W),
                  pl.BlockSpec((SUBLANES, IN_W), _prev_rows),
                  pl.BlockSpec((SUBLANES, ATTN_W), lambda n: (0, 0))],
        out_specs=(cur(ATTN_W), cur(D_MODEL), pl.BlockSpec((D_MODEL, BLK), lambda n: (0, n))),
        scratch_shapes=[pltpu.VMEM((BLK + 2 * SUBLANES, ATTN_W), F32)],
        compiler_params=_params("parallel"))(sinks, qn, k2, k2, v2, v2, z, z, conv_wp)


def _fwd_out(mix, w_out, x, g2, tm):
    s = x.shape[0]

    def body(m_ref, w_ref, x_ref, g_ref, x1_ref, h_ref, ht_ref):
        x1 = x_ref[...] + _dot(m_ref[...], w_ref[...])
        x1_ref[...] = x1
        xn, _ = _rms(x1)
        h = (xn * g_ref[...]).astype(BF16)
        h_ref[...] = h
        ht_ref[...] = h.T

    row = pl.BlockSpec((tm, D_MODEL), lambda i: (i, 0))
    return pl.pallas_call(
        body, name="fwd_out",
        out_shape=(jax.ShapeDtypeStruct((s, D_MODEL), F32), jax.ShapeDtypeStruct((s, D_MODEL), BF16),
                   jax.ShapeDtypeStruct((D_MODEL, s), BF16)),
        grid=(s // tm,),
        in_specs=[row, _resident((D_MODEL, D_MODEL)), row, pl.BlockSpec((1, D_MODEL), lambda i: (0, 0))],
        out_specs=(row, row, pl.BlockSpec((D_MODEL, tm), lambda i: (0, i))),
        compiler_params=_params("parallel"))(mix, w_out, x, g2)


def _ple(hn2, w_pg, b_pg, p, w_pp, g3, x1, target, tm):
    s = x1.shape[0]

    def body(h_ref, wg_ref, b_ref, p_ref, wp_ref, g3_ref, x1_ref, t_ref, dy_ref, dgp_ref, dt_ref, pt_ref, acc_ref):
        gate = _sig(_dot(h_ref[...], wg_ref[...]) + b_ref[...])
        pb = p_ref[...].astype(BF16)
        pt_ref[...] = pb.T
        t = _dot(pb, wp_ref[...])
        tn, r3 = _rms(t)
        e = tn * g3_ref[...]
        diff = x1_ref[...] + gate * e - t_ref[...]
        dy = diff * (1.0 / D_MODEL)
        dy_ref[...] = dy
        dgp = dy * e * (gate * (1.0 - gate))
        dgp_ref[...] = dgp.astype(BF16)
        de = dy * gate
        dt_ref[...] = _rms_bwd(de * g3_ref[...], tn, r3).astype(BF16)

        @pl.when(pl.program_id(0) == 0)
        def _():
            acc_ref[...] = jnp.zeros_like(acc_ref)

        acc_ref[0:1, :] += jnp.sum(dgp, axis=0, keepdims=True)
        acc_ref[1:2, :] += jnp.sum(de * tn, axis=0, keepdims=True)
        acc_ref[2:3, :] += jnp.sum(diff * diff, axis=0, keepdims=True) * (0.5 / D_MODEL)

    row = pl.BlockSpec((tm, D_MODEL), lambda i: (i, 0))
    vec = pl.BlockSpec((1, D_MODEL), lambda i: (0, 0))
    return pl.pallas_call(
        body, name="ple",
        out_shape=(jax.ShapeDtypeStruct((s, D_MODEL), F32), jax.ShapeDtypeStruct((s, D_MODEL), BF16),
                   jax.ShapeDtypeStruct((s, D_MODEL), BF16), jax.ShapeDtypeStruct((PLE_DIM, s), BF16),
                   jax.ShapeDtypeStruct((SUBLANES, D_MODEL), F32)),
        grid=(s // tm,),
        in_specs=[row, _resident((D_MODEL, D_MODEL)), vec, pl.BlockSpec((tm, PLE_DIM), lambda i: (i, 0)),
                  _resident((PLE_DIM, D_MODEL)), vec, row, row],
        out_specs=(row, row, row, pl.BlockSpec((PLE_DIM, tm), lambda i: (0, i)),
                   pl.BlockSpec((SUBLANES, D_MODEL), lambda i: (0, 0))),
        compiler_params=_params("arbitrary"))(hn2, w_pg, b_pg, p, w_pp, g3, x1, target)


def _gate_bwd(dgp, w_pg, x1, dy, g2, tm):
    s = x1.shape[0]

    def body(d_ref, w_ref, x1_ref, dy_ref, g_ref, dx_ref, dxb_ref, acc_ref):
        dh = _dot_nt(d_ref[...], w_ref[...])
        xn, r = _rms(x1_ref[...])
        dx1 = dy_ref[...] + _rms_bwd(dh * g_ref[...], xn, r)
        dx_ref[...] = dx1
        dxb_ref[...] = dx1.astype(BF16)

        @pl.when(pl.program_id(0) == 0)
        def _():
            acc_ref[...] = jnp.zeros_like(acc_ref)

        acc_ref[0:1, :] += jnp.sum(dh * xn, axis=0, keepdims=True)

    row = pl.BlockSpec((tm, D_MODEL), lambda i: (i, 0))
    return pl.pallas_call(
        body, name="gate_bwd",
        out_shape=(jax.ShapeDtypeStruct((s, D_MODEL), F32), jax.ShapeDtypeStruct((s, D_MODEL), BF16),
                   jax.ShapeDtypeStruct((SUBLANES, D_MODEL), F32)),
        grid=(s // tm,),
        in_specs=[row, _resident((D_MODEL, D_MODEL)), row, row, pl.BlockSpec((1, D_MODEL), lambda i: (0, 0))],
        out_specs=(row, row, pl.BlockSpec((SUBLANES, D_MODEL), lambda i: (0, 0))),
        compiler_params=_params("arbitrary"))(dgp, w_pg, x1, dy, g2)


def _mm_nt(a, b, tm, name, after):
    m, k = a.shape
    n = b.shape[0]

    def body(a_ref, b_ref, after_ref, o_ref):
        o_ref[...] = _dot_nt(a_ref[...], b_ref[...])

    return pl.pallas_call(
        body, name=name,
        out_shape=jax.ShapeDtypeStruct((m, n), F32),
        grid=(m // tm,),
        in_specs=[pl.BlockSpec((tm, k), lambda i: (i, 0)), _resident((n, k)), ANY],
        out_specs=pl.BlockSpec((tm, n), lambda i: (i, 0)),
        compiler_params=_params("parallel"))(a, b, after)


def _attn_bwd(qn, k2, v2, a, z, dmix, conv_wp, sinks, after):
    s = qn.shape[0]
    nb = s // BLK

    def body(sink_ref, q_ref, kc_ref, kp_ref, vc_ref, vp_ref, a_ref, z_ref, zp_ref, zn_ref, dm_ref, dmn_ref,
             cw_ref, after_ref, dq_ref, dkc_ref, dkp_ref, dvc_ref, dvp_ref, dz_ref, dzt_ref, acc_ref, ext_ref):
        n = pl.program_id(0)
        valid = _window_mask(n)
        lane = lax.broadcasted_iota(jnp.int32, (1, ATTN_W), 1)

        @pl.when(n == 0)
        def _():
            acc_ref[...] = jnp.zeros_like(acc_ref)

        dz_ref[:, 0:QKV_W] = jnp.zeros((BLK, QKV_W), BF16)
        dsink = jnp.zeros((1, ATTN_W), F32)
        for kvh in range(K2_W // LANES):
            cols = slice(LANES * kvh, LANES * (kvh + 1))
            kw = jnp.concatenate([kp_ref[:, cols], kc_ref[:, cols]], axis=0)
            vw = jnp.concatenate([vp_ref[:, cols], vc_ref[:, cols]], axis=0)
            blocks = [slice(LANES * r, LANES * (r + 1)) for r in (2 * kvh, 2 * kvh + 1)]
            das, avs = [], []
            for rc in blocks:
                g = z_ref[:, _cols(GATE_A0 + rc.start, LANES)]
                sg = _sig(g)
                dm = dm_ref[:, rc]
                av = a_ref[:, rc]
                das.append(dm * (g * sg))
                avs += [av, av]
                dz_ref[:, _cols(GATE_A0 + rc.start, LANES)] = (dm * av * _dsilu(g, sg)).astype(BF16)
            q4 = _stack_heads([q_ref[:, rc] for rc in blocks], jnp.zeros((BLK, LANES), BF16))
            sink, slot = _group_sinks(sink_ref, kvh)
            p, mx, den = _head_probs(q4, kw, valid, sink)
            do4 = _stack_heads(das, 0.0)
            delta = jnp.sum(do4 * jnp.concatenate(avs, axis=0), axis=-1, keepdims=True)
            dob = do4.astype(BF16)
            ds = p * (_dot_nt(dob, vw) - delta) * (HEAD ** -0.5)
            for rc, dq in zip(blocks, _unstack_heads(_dot(ds.astype(BF16), kw))):
                dq_ref[:, rc] = dq
            dk2 = _dot(ds.T.astype(BF16), q4)
            dv2 = _dot(p.T.astype(BF16), dob)
            dkp_ref[:, cols] = dk2[0:BLK]
            dkc_ref[:, cols] = dk2[BLK:2 * BLK]
            dvp_ref[:, cols] = dv2[0:BLK]
            dvc_ref[:, cols] = dv2[BLK:2 * BLK]
            dsk = jnp.exp(sink - mx) / den * delta
            for i in range(GROUP):
                dsink = dsink - jnp.where(lane == GROUP * kvh + i,
                                          jnp.sum(jnp.where(slot == i, dsk, 0.0), axis=0, keepdims=True), 0.0)
        acc_ref[0:1, :] += dsink

        u, um1, um2, cv = _conv_fwd(z_ref, zp_ref, cw_ref, ext_ref, n)
        cb = z_ref[:, _cols(CONV_B0)]
        gc = z_ref[:, _cols(GATE_C0)]
        sgc = _sig(gc)
        dmc = dm_ref[:, ATTN_W:D_MODEL]
        t = dmc * (gc * sgc)
        dcv = t * cb
        dz_ref[:, _cols(CONV_B0)] = (t * cv).astype(BF16)
        dz_ref[:, _cols(GATE_C0)] = (dmc * cb * cv * _dsilu(gc, sgc)).astype(BF16)
        gcn = zn_ref[:, _cols(GATE_C0)]
        dcvn = dmn_ref[:, ATTN_W:D_MODEL] * (gcn * _sig(gcn)) * zn_ref[:, _cols(CONV_B0)]
        ext_ref[0:BLK, :] = dcv
        ext_ref[BLK:BLK + SUBLANES, :] = jnp.where(n < nb - 1, dcvn, 0.0)
        du = (cw_ref[2:3, :] * dcv + cw_ref[1:2, :] * ext_ref[1:1 + BLK, :]
              + cw_ref[0:1, :] * ext_ref[2:2 + BLK, :])
        dz_ref[:, _cols(CONV_C0)] = (du * z_ref[:, _cols(CONV_H0)]).astype(BF16)
        dz_ref[:, _cols(CONV_H0)] = (du * z_ref[:, _cols(CONV_C0)]).astype(BF16)
        acc_ref[1:2, :] += jnp.sum(dcv * um2, axis=0, keepdims=True)
        acc_ref[2:3, :] += jnp.sum(dcv * um1, axis=0, keepdims=True)
        acc_ref[3:4, :] += jnp.sum(dcv * u, axis=0, keepdims=True)
        dzt_ref[...] = dz_ref[...].T

    cur = lambda w: pl.BlockSpec((BLK, w), lambda n: (n, 0))
    prev = lambda w: pl.BlockSpec((BLK, w), lambda n: (jnp.maximum(n - 1, 0), 0))
    nxt = lambda w: pl.BlockSpec(
        (SUBLANES, w), lambda n: (jnp.minimum((n + 1) * (BLK // SUBLANES), nb * (BLK // SUBLANES) - 1), 0))
    f32 = lambda w: jax.ShapeDtypeStruct((s, w), F32)
    return pl.pallas_call(
        body, name="attn_bwd",
        out_shape=(f32(ATTN_W), f32(K2_W), f32(K2_W), f32(K2_W), f32(K2_W),
                   jax.ShapeDtypeStruct((s, IN_W), BF16), jax.ShapeDtypeStruct((IN_W, s), BF16),
                   jax.ShapeDtypeStruct((SUBLANES, ATTN_W), F32)),
        grid=(nb,),
        in_specs=[pl.BlockSpec(memory_space=pltpu.SMEM),
                  cur(ATTN_W), cur(K2_W), prev(K2_W), cur(K2_W), prev(K2_W), cur(ATTN_W), cur(IN_W),
                  pl.BlockSpec((SUBLANES, IN_W), _prev_rows), nxt(IN_W), cur(D_MODEL), nxt(D_MODEL),
                  pl.BlockSpec((SUBLANES, ATTN_W), lambda n: (0, 0)), ANY],
        out_specs=(cur(ATTN_W), cur(K2_W), cur(K2_W), cur(K2_W), cur(K2_W), cur(IN_W),
                   pl.BlockSpec((IN_W, BLK), lambda n: (0, n)), pl.BlockSpec((SUBLANES, ATTN_W), lambda n: (0, 0))),
        scratch_shapes=[pltpu.VMEM((BLK + 2 * SUBLANES, ATTN_W), F32)],
        compiler_params=_params("arbitrary"))(sinks, qn, k2, k2, v2, v2, a, z, z, z, dmix, dmix, conv_wp, after)


def _qkv_bwd(z, dz, dzt, dq, dkc, dkp, dvc, dvp, ra, rbm, rbp, gq2, gk2):
    s = z.shape[0]
    nb = s // BLK

    def body(z_ref, dz_in, dzt_in, dq_ref, dkc_ref, dkp_ref, dvc_ref, dvp_ref, a_ref, bm_ref, bp_ref, gq_ref, gk_ref,
             dz_ref, dzt_ref, acc_ref):
        n = pl.program_id(0)
        a, bm, bp = a_ref[...], bm_ref[...], bp_ref[...]
        lo = _low_half((BLK, LANES))
        last = n == nb - 1

        @pl.when(n == 0)
        def _():
            acc_ref[...] = jnp.zeros_like(acc_ref)

        def norm_bwd(x, dy, gain):
            rr = lax.rsqrt(_half_sums(x * x) * (1.0 / HEAD) + EPS)
            xh = x * rr
            dxg = _rope_t(dy, a, bm, bp)
            dxh = dxg * gain
            dx = rr * (dxh - xh * (_half_sums(dxh * xh) * (1.0 / HEAD)))
            return dx, jnp.sum(dxg * xh, axis=0, keepdims=True)

        def folded(cur_ref, prev_ref, m):
            parts = []
            for h in (2 * m, 2 * m + 1):
                v = cur_ref[:, LANES * h:LANES * (h + 1)] + jnp.where(
                    last, 0.0, prev_ref[:, LANES * h:LANES * (h + 1)])
                parts.append(v + pltpu.roll(v, HEAD, 1))
            return jnp.where(lo, parts[0], parts[1])

        gq_acc = jnp.zeros((1, LANES), F32)
        for r in range(ATTN_W // LANES):
            rc = slice(LANES * r, LANES * (r + 1))
            dx, gg = norm_bwd(z_ref[:, rc], dq_ref[:, rc], gq_ref[...])
            dz_ref[:, rc] = dx.astype(BF16)
            gq_acc = gq_acc + gg
        acc_ref[0:1, :] += gq_acc
        gk_acc = jnp.zeros((1, LANES), F32)
        for m in range(KV_W // LANES):
            kc = slice(ATTN_W + LANES * m, ATTN_W + LANES * (m + 1))
            dx, gg = norm_bwd(z_ref[:, kc], folded(dkc_ref, dkp_ref, m), gk_ref[...])
            dz_ref[:, kc] = dx.astype(BF16)
            gk_acc = gk_acc + gg
            vc = slice(ATTN_W + KV_W + LANES * m, ATTN_W + KV_W + LANES * (m + 1))
            dz_ref[:, vc] = folded(dvc_ref, dvp_ref, m).astype(BF16)
        acc_ref[1:2, :] += gk_acc
        dzt_ref[...] = dz_ref[...].T

    cur = lambda w: pl.BlockSpec((BLK, w), lambda n: (n, 0))
    nxt = lambda w: pl.BlockSpec((BLK, w), lambda n: (jnp.minimum(n + 1, nb - 1), 0))
    one = pl.BlockSpec((1, LANES), lambda n: (0, 0))
    return pl.pallas_call(
        body, name="qkv_bwd",
        out_shape=(jax.ShapeDtypeStruct(dz.shape, dz.dtype), jax.ShapeDtypeStruct(dzt.shape, dzt.dtype),
                   jax.ShapeDtypeStruct((SUBLANES, LANES), F32)),
        grid=(nb,),
        in_specs=[cur(PAIR_W), ANY, ANY, cur(ATTN_W), cur(K2_W), nxt(K2_W), cur(K2_W), nxt(K2_W),
                  cur(LANES), cur(LANES), cur(LANES), one, one],
        out_specs=(cur(QKV_W), pl.BlockSpec((QKV_W, BLK), lambda n: (0, n)),
                   pl.BlockSpec((SUBLANES, LANES), lambda n: (0, 0))),
        input_output_aliases={1: 0, 2: 1},
        compiler_params=_params("arbitrary"))(z, dz, dzt, dq, dkc, dkp, dvc, dvp, ra, rbm, rbp, gq2, gk2)


def _in_bwd(dz, w_pairs, x, dx1, g1, tm, after):
    s = x.shape[0]

    def body(d_ref, w_ref, x_hbm, dx1_hbm, g_ref, after_ref, gx_ref, acc_ref, x_buf, dx1_buf, sems):
        i, k = pl.program_id(0), pl.program_id(1)
        rows = pl.ds(pl.multiple_of(i * tm, tm), tm)
        fetch = [pltpu.make_async_copy(x_hbm.at[rows], x_buf, sems.at[0]),
                 pltpu.make_async_copy(dx1_hbm.at[rows], dx1_buf, sems.at[1])]
        sub = min(SUB_ROWS, tm)
        blocks = [slice(r, r + sub) for r in range(0, tm, sub)]

        @pl.when(k == 0)
        def _():
            for cp in fetch:
                cp.start()
            gx_ref[...] = _dot(d_ref[...], w_ref[0])

        @pl.when(k > 0)
        def _():
            gx_ref[...] += _dot(d_ref[...], w_ref[0])

        @pl.when((i == 0) & (k == 0))
        def _():
            acc_ref[...] = jnp.zeros_like(acc_ref)

        @pl.when(k == N_PAIRS - 1)
        def _():
            for cp in fetch:
                cp.wait()
            for rb in blocks:
                dh = gx_ref[rb, :]
                xn, r = _rms(x_buf[rb, :])
                gx_ref[rb, :] = dx1_buf[rb, :] + _rms_bwd(dh * g_ref[...], xn, r)
                acc_ref[0:1, :] += jnp.sum(dh * xn, axis=0, keepdims=True)

    return pl.pallas_call(
        body, name="in_bwd",
        out_shape=(jax.ShapeDtypeStruct((s, D_MODEL), F32), jax.ShapeDtypeStruct((SUBLANES, D_MODEL), F32)),
        grid=(s // tm, N_PAIRS),
        in_specs=[pl.BlockSpec((tm, PAIR_W), lambda i, k: (i, k)),
                  pl.BlockSpec((1, PAIR_W, D_MODEL), lambda i, k: (k, 0, 0)),
                  ANY, ANY, pl.BlockSpec((1, D_MODEL), lambda i, k: (0, 0)), ANY],
        out_specs=(pl.BlockSpec((tm, D_MODEL), lambda i, k: (i, 0)),
                   pl.BlockSpec((SUBLANES, D_MODEL), lambda i, k: (0, 0))),
        scratch_shapes=[pltpu.VMEM((tm, D_MODEL), F32), pltpu.VMEM((tm, D_MODEL), F32),
                        pltpu.SemaphoreType.DMA((2,))],
        compiler_params=_params("arbitrary", "arbitrary"))(dz, w_pairs, x, dx1, g1, after)


def _mm_grad(at, bs, tn, name):
    m, kdim = at.shape
    nblk = [b.shape[1] // tn for b in bs]
    starts = [sum(nblk[:t]) for t in range(len(bs))]

    def body(a_ref, *refs):
        b_refs, o_ref = refs[:len(bs)], refs[len(bs)]
        j = pl.program_id(0)
        for t, b_ref in enumerate(b_refs):
            @pl.when((j >= starts[t]) & (j < starts[t] + nblk[t]))
            def _():
                o_ref[...] = _dot(a_ref[...], b_ref[...]).astype(BF16)

    def b_spec(t):
        return pl.BlockSpec((kdim, tn), lambda j: (0, jnp.clip(j - starts[t], 0, nblk[t] - 1)))

    return pl.pallas_call(
        body, name=name,
        out_shape=jax.ShapeDtypeStruct((m, sum(nblk) * tn), BF16),
        grid=(sum(nblk),),
        in_specs=[_resident((m, kdim))] + [b_spec(t) for t in range(len(bs))],
        out_specs=pl.BlockSpec((m, tn), lambda j: (0, j)),
        compiler_params=_params("parallel"))(at, *bs)


def _grad_w_in(dzt, h):
    kdim = h.shape[0]

    def body(d_ref, h_ref, o_ref):
        o_ref[0] = _dot(d_ref[...], h_ref[...]).astype(BF16)

    return pl.pallas_call(
        body, name="grad_w_in",
        out_shape=jax.ShapeDtypeStruct((N_DEV, SHARD_IN, D_MODEL), BF16),
        grid=(N_DEV,),
        in_specs=[pl.BlockSpec((SHARD_IN, kdim), lambda j: (j, 0)), _resident((kdim, D_MODEL))],
        out_specs=pl.BlockSpec((1, SHARD_IN, D_MODEL), lambda j: (j, 0, 0)),
        compiler_params=_params("parallel"))(dzt, h)


def _place():
    return lax.axis_index("x"), lax.axis_index("y"), lax.axis_index("c")


ROW_CONV, ROW_LOSS = 7, 10


def _reduce_small(acc_g1, acc_g2, acc_ple, acc_qk, acc_attn):
    def body(g1_ref, g2_ref, ple_ref, qk_ref, attn_ref, out_ref, slab_ref, gath_ref, send_sems, recv_sems):
        x, y, c = _place()
        me = 4 * x + 2 * y + c
        slab_ref[...] = jnp.zeros_like(slab_ref)
        slab_ref[0:1, :] = g1_ref[0:1, :]
        slab_ref[1:2, :] = g2_ref[0:1, :]
        slab_ref[2:4, :] = ple_ref[0:2, :]
        qk = qk_ref[0:2, :]
        slab_ref[4:6, 0:LANES] = jnp.where(_low_half(qk.shape), qk + pltpu.roll(qk, HEAD, 1), 0.0)
        lane = lax.broadcasted_iota(jnp.int32, (1, LANES), 1)
        slab_ref[6:7, 0:LANES] = jnp.where(lane < N_Q_HEADS, attn_ref[0:1, 0:LANES], 0.0)
        slab_ref[ROW_CONV:ROW_CONV + 3, 0:ATTN_W] = attn_ref[1:4, :]
        slab_ref[ROW_LOSS:ROW_LOSS + 1, :] = ple_ref[2:3, :]
        gath_ref[me] = slab_ref[...]
        copies = []
        for k in range(1, N_DEV):
            peer = (x ^ (k >> 2), y ^ ((k >> 1) & 1), c ^ (k & 1))
            copies.append(pltpu.make_async_remote_copy(
                src_ref=slab_ref, dst_ref=gath_ref.at[me], send_sem=send_sems.at[k - 1],
                recv_sem=recv_sems.at[k - 1], device_id=peer, device_id_type=MESH))
        for cp in copies:
            cp.start()
        for cp in copies:
            cp.wait_recv()
        for cp in copies:
            cp.wait_send()
        total = gath_ref[0]
        for d in range(1, N_DEV):
            total = total + gath_ref[d]
        out_ref[...] = total

    vmem = pl.BlockSpec(memory_space=pltpu.VMEM)
    return pl.pallas_call(
        body, name="reduce_small",
        out_shape=jax.ShapeDtypeStruct((SLAB_ROWS, D_MODEL), F32),
        in_specs=[vmem] * 5, out_specs=vmem,
        scratch_shapes=[pltpu.VMEM((SLAB_ROWS, D_MODEL), F32), pltpu.VMEM((N_DEV, SLAB_ROWS, D_MODEL), F32),
                        pltpu.SemaphoreType.DMA((N_DEV - 1,)), pltpu.SemaphoreType.DMA((N_DEV - 1,))])(
            acc_g1, acc_g2, acc_ple, acc_qk, acc_attn)


def _pair_sum(g, r, place, tr, name):
    _, _, rows, cols = g.shape

    def body(place_ref, g_ref, r_ref, pb_ref, own_ref):
        tot = g_ref[0, 0].astype(F32) + r_ref[0].astype(F32)
        pb_ref[0] = tot.astype(BF16)

        @pl.when(pl.program_id(1) == place_ref[1])
        def _():
            own_ref[...] = tot

    grid_spec = pltpu.PrefetchScalarGridSpec(
        num_scalar_prefetch=1, grid=(rows // tr, 4),
        in_specs=[pl.BlockSpec((1, 1, tr, cols), lambda i, q, place_ref: (q, place_ref[0], i, 0)),
                  pl.BlockSpec((1, tr, cols), lambda i, q, place_ref: (q, i, 0))],
        out_specs=(pl.BlockSpec((1, tr, cols), lambda i, q, place_ref: (q, i, 0)),
                   pl.BlockSpec((tr, cols), lambda i, q, place_ref: (i, 0))))
    return pl.pallas_call(
        body, name=name, grid_spec=grid_spec,
        out_shape=(jax.ShapeDtypeStruct((4, rows, cols), BF16), jax.ShapeDtypeStruct((rows, cols), F32)),
        compiler_params=_params("arbitrary", "arbitrary"))(place, g, r)


HBM = pl.BlockSpec(memory_space=pltpu.HBM)
SEM = pl.BlockSpec(memory_space=pltpu.SEMAPHORE)
SIDE_EFFECT = pltpu.CompilerParams(has_side_effects=pltpu.SideEffectType.DATAFLOW_SIDE_EFFECTING)
TOKEN = jax.ShapeDtypeStruct((SUBLANES, LANES), F32)


def _hbm(a):
    return pltpu.with_memory_space_constraint(a, pltpu.HBM)


def _hbm_like(arrays):
    return tuple(pltpu.HBM(a.shape, a.dtype) for a in arrays)


def _block_of(px, py, pc):
    return 4 * px + 2 * py + pc


def _gather_start(shards, after):
    na = len(shards)
    lands = [_hbm(lax.empty((N_DEV,) + a.shape, a.dtype)) for a in shards]

    def body(*refs):
        ins, land = refs[:na], refs[na:2 * na]
        send_sems, recv_ici, recv_d2d = refs[2 * na + 1:2 * na + 4]
        token = refs[-1]
        x, y, c = _place()
        for k, peer in enumerate([(x, y, 1 - c), (1 - x, y, c), (x, 1 - y, c), (1 - x, 1 - y, c)]):
            for t in range(na):
                pltpu.make_async_remote_copy(
                    src_ref=ins[t], dst_ref=land[t].at[_block_of(x, y, c)], send_sem=send_sems.at[4 * t + k],
                    recv_sem=recv_d2d.at[4 * t] if k == 0 else recv_ici.at[3 * t + k - 1],
                    device_id=peer, device_id_type=MESH).start()
        token[...] = jnp.zeros_like(token)

    out = pl.pallas_call(
        body, name="gather_start",
        out_shape=(pltpu.SemaphoreType.DMA((4 * na,)), pltpu.SemaphoreType.DMA((3 * na,)),
                   pltpu.SemaphoreType.DMA((4 * na,)), *_hbm_like(lands), TOKEN),
        in_specs=[ANY] * na + [HBM] * na + [ANY],
        out_specs=(SEM, SEM, SEM, *[HBM] * na, pl.BlockSpec(memory_space=pltpu.VMEM)),
        input_output_aliases={na + i: 3 + i for i in range(na)},
        compiler_params=SIDE_EFFECT)(*shards, *lands, after)
    send_sems, recv_ici, recv_d2d = out[:3]
    state = dict(send=send_sems, ici=recv_ici, d2d=recv_d2d, shards=list(shards), lands=out[3:3 + na])
    return state, out[-1]


def _gather_forward(state, after):
    lands = state["lands"]
    na = len(lands)

    def body(*refs):
        land = refs[:na]
        recv_ici, recv_d2d = refs[na], refs[na + 1]
        fwd_sems, token = refs[-2], refs[-1]
        x, y, c = _place()
        for j, chip in enumerate([(1 - x, y), (x, 1 - y), (1 - x, 1 - y)]):
            for t in range(na):
                blk = land[t].at[_block_of(*chip, c)]
                pltpu.make_async_remote_copy(
                    src_ref=blk, dst_ref=blk, send_sem=fwd_sems.at[3 * t + j], recv_sem=recv_ici.at[3 * t + j],
                    device_id=(x, y, c), device_id_type=MESH).wait_recv()
                pltpu.make_async_remote_copy(
                    src_ref=blk, dst_ref=blk, send_sem=fwd_sems.at[3 * t + j], recv_sem=recv_d2d.at[4 * t + 1 + j],
                    device_id=(x, y, 1 - c), device_id_type=MESH).start()
        token[...] = jnp.zeros_like(token)

    out = pl.pallas_call(
        body, name="gather_forward",
        out_shape=(*_hbm_like(lands), pltpu.SemaphoreType.DMA((3 * na,)), TOKEN),
        in_specs=[HBM] * na + [SEM, SEM, ANY],
        out_specs=(*[HBM] * na, SEM, pl.BlockSpec(memory_space=pltpu.VMEM)),
        input_output_aliases={i: i for i in range(na)},
        compiler_params=SIDE_EFFECT)(*lands, state["ici"], state["d2d"], after)
    return dict(state, lands=out[:na], fwd=out[na]), out[-1]


def _gather_wait(state, after):
    shards, lands = state["shards"], state["lands"]
    na = len(lands)

    def body(*refs):
        ins, land = refs[:na], refs[na:2 * na]
        send_sems, fwd_sems, recv_d2d = refs[2 * na:2 * na + 3]
        x, y, c = _place()
        chips = [(1 - x, y), (x, 1 - y), (1 - x, 1 - y)]
        for t in range(na):
            mine = land[t].at[_block_of(x, y, c)]
            for k in range(4):
                pltpu.make_async_remote_copy(
                    src_ref=ins[t], dst_ref=mine, send_sem=send_sems.at[4 * t + k], recv_sem=recv_d2d.at[4 * t],
                    device_id=(x, y, c), device_id_type=MESH).wait_send()
            for j, chip in enumerate(chips):
                blk = land[t].at[_block_of(*chip, c)]
                pltpu.make_async_remote_copy(
                    src_ref=blk, dst_ref=blk, send_sem=fwd_sems.at[3 * t + j], recv_sem=recv_d2d.at[4 * t + 1 + j],
                    device_id=(x, y, c), device_id_type=MESH).wait_send()
            for k, blk_id in enumerate([_block_of(x, y, 1 - c)] + [_block_of(*chip, 1 - c) for chip in chips]):
                blk = land[t].at[blk_id]
                pltpu.make_async_remote_copy(
                    src_ref=blk, dst_ref=blk, send_sem=send_sems.at[4 * t], recv_sem=recv_d2d.at[4 * t + k],
                    device_id=(x, y, c), device_id_type=MESH).wait_recv()

    out = pl.pallas_call(
        body, name="gather_wait",
        out_shape=_hbm_like(lands),
        in_specs=[ANY] * na + [HBM] * na + [SEM, SEM, SEM, ANY],
        out_specs=tuple([HBM] * na),
        input_output_aliases={na + i: i for i in range(na)},
        compiler_params=SIDE_EFFECT)(*shards, *lands, state["send"], state["fwd"], state["d2d"], after)
    return out


def _gather_from_sibling(state, after):
    lands = state["lands"]
    na = len(lands)

    def body(*refs):
        land, recv_d2d = refs[:na], refs[na]
        x, y, c = _place()
        for t in range(na):
            blk = land[t].at[_block_of(x, y, 1 - c)]
            pltpu.make_async_remote_copy(src_ref=blk, dst_ref=blk, send_sem=recv_d2d.at[4 * t],
                                         recv_sem=recv_d2d.at[4 * t], device_id=(x, y, c),
                                         device_id_type=MESH).wait_recv()

    out = pl.pallas_call(
        body, name="gather_from_sibling", out_shape=_hbm_like(lands),
        in_specs=[HBM] * na + [SEM, ANY], out_specs=tuple([HBM] * na),
        input_output_aliases={i: i for i in range(na)},
        compiler_params=SIDE_EFFECT)(*lands, state["d2d"], after)
    return dict(state, lands=list(out))


def _gather_from_chip(state, j, afters, last):
    shards, lands = state["shards"], state["lands"]
    na = len(lands)

    def chip_blocks(land_ref):
        x, y, c = _place()
        chip = [(1 - x, y), (x, 1 - y), (1 - x, 1 - y)][j]
        return (x, y, c), land_ref.at[_block_of(*chip, c)], land_ref.at[_block_of(*chip, 1 - c)]

    def forward(*refs):
        land, recv_ici, recv_d2d, fwd_sems = refs[:na], refs[na], refs[na + 1], refs[-1]
        for t in range(na):
            (x, y, c), mine, _ = chip_blocks(land[t])
            pltpu.make_async_remote_copy(src_ref=mine, dst_ref=mine, send_sem=fwd_sems.at[t],
                                         recv_sem=recv_ici.at[3 * t + j], device_id=(x, y, c),
                                         device_id_type=MESH).wait_recv()
            pltpu.make_async_remote_copy(src_ref=mine, dst_ref=mine, send_sem=fwd_sems.at[t],
                                         recv_sem=recv_d2d.at[4 * t + 1 + j], device_id=(x, y, 1 - c),
                                         device_id_type=MESH).start()

    out = pl.pallas_call(
        forward, name="gather_pass_chip_" + str(j),
        out_shape=(*_hbm_like(lands), pltpu.SemaphoreType.DMA((na,))),
        in_specs=[HBM] * na + [SEM, SEM] + [ANY] * len(afters), out_specs=(*[HBM] * na, SEM),
        input_output_aliases={i: i for i in range(na)},
        compiler_params=SIDE_EFFECT)(*lands, state["ici"], state["d2d"], *afters)
    lands, fwd_sems = out[:na], out[na]

    def arrive(*refs):
        land, fwd_sems, recv_d2d = refs[:na], refs[na], refs[na + 1]
        shard, send_sems = refs[na + 2:2 * na + 2], refs[2 * na + 2]
        for t in range(na):
            (x, y, c), mine, theirs = chip_blocks(land[t])
            pltpu.make_async_remote_copy(src_ref=theirs, dst_ref=theirs, send_sem=fwd_sems.at[t],
                                         recv_sem=recv_d2d.at[4 * t + 1 + j], device_id=(x, y, c),
                                         device_id_type=MESH).wait_recv()
            pltpu.make_async_remote_copy(src_ref=mine, dst_ref=mine, send_sem=fwd_sems.at[t],
                                         recv_sem=recv_d2d.at[4 * t + 1 + j], device_id=(x, y, c),
                                         device_id_type=MESH).wait_send()
            for k in range(4 if last else 0):
                pltpu.make_async_remote_copy(
                    src_ref=shard[t], dst_ref=land[t].at[_block_of(x, y, c)], send_sem=send_sems.at[4 * t + k],
                    recv_sem=recv_d2d.at[4 * t], device_id=(x, y, c), device_id_type=MESH).wait_send()

    out = pl.pallas_call(
        arrive, name="gather_take_chip_" + str(j), out_shape=_hbm_like(lands),
        in_specs=[HBM] * na + [SEM, SEM] + [ANY] * na + [SEM], out_specs=tuple([HBM] * na),
        input_output_aliases={i: i for i in range(na)},
        compiler_params=SIDE_EFFECT)(*lands, fwd_sems, state["d2d"], *shards, state["send"])
    return dict(state, lands=list(out))


def _to_sibling(srcs, lands, send_sems, recv_sems):
    x, y, c = _place()
    return [pltpu.make_async_remote_copy(
        src_ref=srcs[t].at[:, 1 - c], dst_ref=lands[t], send_sem=send_sems.at[t], recv_sem=recv_sems.at[t],
        device_id=(x, y, 1 - c), device_id_type=MESH) for t in range(len(srcs))]


def _to_chips(srcs, lands, send_sems, recv_sems):
    x, y, c = _place()
    copies = []
    for k in (1, 2, 3):
        px, py = x ^ (k >> 1), y ^ (k & 1)
        copies += [pltpu.make_async_remote_copy(
            src_ref=srcs[t].at[2 * px + py], dst_ref=lands[t].at[k - 1], send_sem=send_sems.at[3 * t + k - 1],
            recv_sem=recv_sems.at[3 * t + k - 1], device_id=(px, py, c), device_id_type=MESH) for t in range(len(srcs))]
    return copies


def _exchange_start(name, srcs, land_shapes, copies, per_array, after):
    na = len(srcs)
    lands = [_hbm(lax.empty(shp, a.dtype)) for shp, a in zip(land_shapes, srcs)]

    def body(*refs):
        token = refs[-1]
        for cp in copies(refs[:na], refs[na:2 * na], refs[2 * na + 1], refs[2 * na + 2]):
            cp.start()
        token[...] = jnp.zeros_like(token)

    out = pl.pallas_call(
        body, name=name,
        out_shape=(pltpu.SemaphoreType.DMA((na * per_array,)), pltpu.SemaphoreType.DMA((na * per_array,)),
                   *_hbm_like(lands), TOKEN),
        in_specs=[ANY] * na + [HBM] * na + [ANY],
        out_specs=(SEM, SEM, *[HBM] * na, pl.BlockSpec(memory_space=pltpu.VMEM)),
        input_output_aliases={na + i: 2 + i for i in range(na)},
        compiler_params=SIDE_EFFECT)(*srcs, *lands, after)
    return dict(send=out[0], recv=out[1], srcs=list(srcs), lands=out[2:2 + na]), out[-1]


def _exchange_wait(name, state, copies, afters):
    srcs, lands = state["srcs"], state["lands"]
    na = len(srcs)

    def body(*refs):
        for cp in copies(refs[:na], refs[na:2 * na], refs[2 * na], refs[2 * na + 1]):
            cp.wait_send()
            cp.wait_recv()

    out = pl.pallas_call(
        body, name=name,
        out_shape=_hbm_like(lands),
        in_specs=[ANY] * na + [HBM] * na + [SEM, SEM] + [ANY] * len(afters),
        out_specs=tuple([HBM] * na),
        input_output_aliases={na + i: i for i in range(na)},
        compiler_params=SIDE_EFFECT)(*srcs, *lands, state["send"], state["recv"], *afters)
    return out


def _adamw_math(w, g, m, v):
    m = ADAM_B1 * m + (1.0 - ADAM_B1) * g
    v = ADAM_B2 * v + (1.0 - ADAM_B2) * (g * g)
    m_hat = m / (1.0 - ADAM_B1 ** ADAM_STEP)
    v_hat = v / (1.0 - ADAM_B2 ** ADAM_STEP)
    return -ADAM_LR * (m_hat / (jnp.sqrt(v_hat) + ADAM_EPS) + ADAM_WD * w), m, v


def _adamw(own, others, w, m, v, tr, name, after):
    rows, cols = w.shape
    blk = pl.BlockSpec((tr, cols), lambda i: (i, 0))

    def body(own_ref, oth_ref, w_ref, m_ref, v_ref, after_ref, g_ref, d_ref, nm_ref, nv_ref):
        g = own_ref[...]
        for k in range(3):
            g = g + oth_ref[k].astype(F32)
        g_ref[...] = g
        d_ref[...], nm_ref[...], nv_ref[...] = _adamw_math(w_ref[...], g, m_ref[...], v_ref[...])

    out = jax.ShapeDtypeStruct((rows, cols), F32)
    return pl.pallas_call(
        body, name=name, out_shape=(out, out, out, out), grid=(rows // tr,),
        in_specs=[blk, pl.BlockSpec((3, tr, cols), lambda i: (0, i, 0)), blk, blk, blk, ANY],
        out_specs=(blk, blk, blk, blk),
        compiler_params=_params("parallel"))(own, others, w, m, v, after)


def _adamw_small(red, me, params, moments1, moments2):
    n = len(params)

    def body(me_ref, red_ref, *refs):
        ws, ms, vs = refs[:n], refs[n:2 * n], refs[2 * n:3 * n]
        loss_ref = refs[3 * n]
        outs = refs[3 * n + 1:]
        loss_ref[...] = jnp.sum(red_ref[ROW_LOSS:ROW_LOSS + 1, :], axis=-1, keepdims=True)
        for t in range(n - 1):
            g = red_ref[t:t + 1, 0:ws[t].shape[1]]
            d, nm, nv = _adamw_math(ws[t][...], g, ms[t][...], vs[t][...])
            for o, val in zip(outs[4 * t:4 * t + 4], (g, d, nm, nv)):
                o[...] = val
        mine = pl.ds(pl.multiple_of(me_ref[0, 0] * LANES, LANES), LANES)
        for tap in range(ws[-1].shape[0]):
            g = red_ref[ROW_CONV + tap:ROW_CONV + tap + 1, mine]
            d, nm, nv = _adamw_math(ws[-1][tap], g, ms[-1][tap], vs[-1][tap])
            for o, val in zip(outs[4 * (n - 1):], (g, d, nm, nv)):
                o[tap] = val

    vmem = pl.BlockSpec(memory_space=pltpu.VMEM)
    shapes = [jax.ShapeDtypeStruct(w.shape, F32) for w in params for _ in range(4)]
    out = pl.pallas_call(
        body, name="adamw_small", out_shape=(jax.ShapeDtypeStruct((1, 1), F32), *shapes),
        in_specs=[pl.BlockSpec(memory_space=pltpu.SMEM), vmem] + [vmem] * (3 * n),
        out_specs=tuple([vmem] * (1 + 4 * n)))(me, red, *params, *moments1, *moments2)
    return out[0], [list(out[1 + k::4]) for k in range(4)]


def _tables(s, gq, gk, conv_w):
    gq2 = jnp.tile(gq.reshape(1, HEAD), (1, 2))
    gk2 = jnp.tile(gk.reshape(1, HEAD), (1, 2))
    conv_wp = jnp.pad(conv_w, ((0, SUBLANES - conv_w.shape[0]), (0, 0)))
    return _rope_tables(s), gq2, gk2, conv_wp


def _pair_id(q):
    return jnp.full((1,), q, jnp.int32)


def _forward_in(x, g1, shards):
    s = x.shape[0]
    h = _prenorm(x, g1, min(512, s), x)
    z, w_pairs = lax.empty((s, IN_W), F32), lax.empty((N_PAIRS, PAIR_W, D_MODEL), BF16)
    for q in range(N_PAIRS):
        z, w_pairs = _fwd_in_pair(h, shards, z, w_pairs, _pair_id(q), min(512, s), "fwd_in_" + str(q))
    return h, z, w_pairs


def _forward_attn(z, rope, gq2, gk2, conv_wp, sinks, after):
    s = z.shape[0]
    qn, k2, v2 = _qk_prep(z, *rope, gq2, gk2, min(256, s), after)
    a, mix, mixt = _attn_fwd(qn, k2, v2, z, conv_wp, sinks)
    return qn, k2, v2, a, mix, mixt


def _forward_out(x, p, target, mix, mixt, w_out, g2, w_pg, b_pg, w_pp, g3):
    s = x.shape[0]
    tm = min(512, s)
    x1, hn2, hn2t = _fwd_out(mix, w_out, x, g2, tm)
    dy, dgp, dt, pt, acc_ple = _ple(hn2, w_pg, b_pg, p, w_pp, g3, x1, target, min(256, s))
    dx1, dx1b, acc_g2 = _gate_bwd(dgp, w_pg, x1, dy, g2, tm)
    gw_out = _mm_grad(mixt, [dx1b], 512, "grad_w_out")
    gw_pg = _mm_grad(hn2t, [dgp], 512, "grad_w_ple_gate")
    gw_pp = _mm_grad(pt, [dt], 512, "grad_w_ple_proj")
    return dx1, dx1b, (gw_out, gw_pg, gw_pp), acc_ple, acc_g2


def _backward_attn(dmix, h, z, qn, k2, v2, a, rope, gq2, gk2, conv_wp, sinks, after):
    dq, dkc, dkp, dvc, dvp, dz, dzt, acc_attn = _attn_bwd(qn, k2, v2, a, z, dmix, conv_wp, sinks, after)
    dz, dzt, acc_qk = _qkv_bwd(z, dz, dzt, dq, dkc, dkp, dvc, dvp, *rope, gq2, gk2)
    return dz, _grad_w_in(dzt, h), acc_attn, acc_qk


def _local_step(x, p, target, g1, shards, gq, gk, sinks, conv_w, w_out, g2, w_pg, b_pg, w_pp, g3):
    rope, gq2, gk2, conv_wp = _tables(x.shape[0], gq, gk, conv_w)
    h, z, w_pairs = _forward_in(x, g1, shards)
    qn, k2, v2, a, mix, mixt = _forward_attn(z, rope, gq2, gk2, conv_wp, sinks, z)
    dx1, dx1b, (gw_out, gw_pg, gw_pp), acc_ple, acc_g2 = _forward_out(
        x, p, target, mix, mixt, w_out, g2, w_pg, b_pg, w_pp, g3)
    dmix = _mm_nt(dx1b, w_out, min(512, x.shape[0]), "out_bwd", dx1b)
    dz, gw_in, acc_attn, acc_qk = _backward_attn(dmix, h, z, qn, k2, v2, a, rope, gq2, gk2, conv_wp, sinks, dmix)
    grad_x, acc_g1 = _in_bwd(dz, w_pairs, x, dx1, g1, min(512, x.shape[0]), dx1)
    return grad_x, (gw_in, gw_out, gw_pg, gw_pp), (acc_g1, acc_g2, acc_ple, acc_qk, acc_attn)


def _by_owner(g):
    return g.reshape((4, 2) + g.shape[1:])


def kernel(x, p, norm_gain, w_in, q_norm_gain, k_norm_gain, attn_sinks, conv_w, w_out, ple_gate_norm_gain, w_ple_gate, b_ple_gate, w_ple_proj, ple_norm_gain, loss_target, m_norm_gain, m_w_in, m_q_norm_gain, m_k_norm_gain, m_attn_sinks, m_conv_w, m_w_out, m_ple_gate_norm_gain, m_w_ple_gate, m_b_ple_gate, m_w_ple_proj, m_ple_norm_gain, v_norm_gain, v_w_in, v_q_norm_gain, v_k_norm_gain, v_attn_sinks, v_conv_w, v_w_out, v_ple_gate_norm_gain, v_w_ple_gate, v_b_ple_gate, v_w_ple_proj, v_ple_norm_gain):
    me = 4 * lax.axis_index("x") + 2 * lax.axis_index("y") + lax.axis_index("c")
    place = jnp.stack([lax.axis_index("c"), 2 * lax.axis_index("x") + lax.axis_index("y")]).astype(jnp.int32)
    xs, ps, target = x[0], p[0, 0], loss_target[0]

    shard_in = w_in[0].T.astype(BF16)
    own_late = [w_out[0].astype(BF16), w_ple_gate[0].astype(BF16), w_ple_proj[0].astype(BF16)]
    with_own = lambda gathered, own: lax.dynamic_update_slice(gathered, own[None], (me,) + (0,) * own.ndim)
    early, started = _gather_start([shard_in, conv_w[0]], shard_in)
    tm = min(512, xs.shape[0])
    h = _prenorm(xs, norm_gain, tm, started)

    z, w_pairs = lax.empty((xs.shape[0], IN_W), F32), lax.empty((N_PAIRS, PAIR_W, D_MODEL), BF16)
    early = _gather_from_sibling(early, h)
    early = dict(early, lands=[with_own(early["lands"][0], shard_in), early["lands"][1]])
    z, w_pairs = _fwd_in_pair(h, early["lands"][0], z, w_pairs, place[1:2], tm, "fwd_in_own")
    for j, flip in enumerate((2, 1, 3)):
        early = _gather_from_chip(early, j, (z,) if j != 1 else (z, started_late), last=j == 2)
        z, w_pairs = _fwd_in_pair(h, early["lands"][0], z, w_pairs, place[1:2] ^ flip, tm, "fwd_in_chip_" + str(j))
        if j == 0:
            late, started_late = _gather_start(own_late, z)
    conv_full = jnp.transpose(with_own(early["lands"][1], conv_w[0]), (1, 0, 2)).reshape(3, ATTN_W)
    rope, gq2, gk2, conv_wp = _tables(xs.shape[0], q_norm_gain[0], k_norm_gain[0], conv_full)
    late, forwarded = _gather_forward(late, z)
    qn, k2, v2, a, mix, mixt = _forward_attn(z, rope, gq2, gk2, conv_wp, attn_sinks, forwarded)
    g_out, g_pg, g_pp = (with_own(g, own) for g, own in zip(_gather_wait(late, mix), own_late))
    w_out_f = g_out.reshape(D_MODEL, D_MODEL)
    w_pg_f = g_pg.reshape(D_MODEL, D_MODEL)
    w_pp_f = jnp.transpose(g_pp, (1, 0, 2)).reshape(PLE_DIM, D_MODEL)

    dx1, dx1b, (gw_out, gw_pg, gw_pp), acc_ple, acc_g2 = _forward_out(
        xs, ps, target, mix, mixt, w_out_f, ple_gate_norm_gain, w_pg_f, b_ple_gate, w_pp_f, ple_norm_gain)

    names = ("w_out", "w_ple_gate", "w_ple_proj")
    gw_pp_t = jnp.transpose(gw_pp.reshape(PLE_DIM, N_DEV, PLE_DIM), (1, 0, 2))
    grads = [_by_owner(gw_out.reshape(N_DEV, D_MODEL // N_DEV, D_MODEL)),
             _by_owner(gw_pg.reshape(N_DEV, D_MODEL // N_DEV, D_MODEL)), _by_owner(gw_pp_t)]
    pairs, paired = _exchange_start("pair_start", grads, [(4,) + g.shape[2:] for g in grads], _to_sibling, 1, dx1b)
    dmix = _mm_nt(dx1b, w_out_f, tm, "out_bwd", paired)
    from_sibling = _exchange_wait("pair_wait", pairs, _to_sibling, (dmix,))
    sums = [_pair_sum(g, r, place, 256, "pair_sum_" + nm) for g, r, nm in zip(pairs["srcs"], from_sibling, names)]
    chips, sent = _exchange_start("chip_start", [pb for pb, _ in sums], [(3,) + pb.shape[1:] for pb, _ in sums],
                                  _to_chips, 3, sums[-1][1])

    dz, gw_in, acc_attn, acc_qk = _backward_attn(
        dmix, h, z, qn, k2, v2, a, rope, gq2, gk2, conv_wp, attn_sinks, sent)

    gw_in_t = [_by_owner(gw_in)]
    pairs_in, paired_in = _exchange_start("pair_start_w_in", gw_in_t, [(4,) + gw_in_t[0].shape[2:]], _to_sibling, 1,
                                          gw_in)
    from_chips = _exchange_wait("chip_wait", chips, _to_chips, (gw_in,))
    big = {}
    for (_, own), oth, w, m, v, nm in zip(sums, from_chips, (w_out, w_ple_gate, w_ple_proj),
                                          (m_w_out, m_w_ple_gate, m_w_ple_proj),
                                          (v_w_out, v_w_ple_gate, v_w_ple_proj), names):
        big[nm] = [t[None] for t in _adamw(own, oth, w[0], m[0], v[0], 256, "adamw_" + nm, paired_in)]

    (from_sibling_in,) = _exchange_wait("pair_wait_w_in", pairs_in, _to_sibling, [big[nm][0] for nm in names])
    pb_in, own_in = _pair_sum(pairs_in["srcs"][0], from_sibling_in, place, SHARD_IN // 2, "pair_sum_w_in")
    chips_in, sent_in = _exchange_start("chip_start_w_in", [pb_in], [(3,) + pb_in.shape[1:]], _to_chips, 3, own_in)
    grad_x, acc_g1 = _in_bwd(dz, w_pairs, xs, dx1, norm_gain, tm, sent_in)
    (from_chips_in,) = _exchange_wait("chip_wait_w_in", chips_in, _to_chips, (grad_x,))
    big["w_in"] = [t.T[None] for t in _adamw(own_in, from_chips_in, w_in[0].T, m_w_in[0].T, v_w_in[0].T, SHARD_IN // 4,
                                             "adamw_w_in", grad_x)]

    red = _reduce_small(acc_g1, acc_g2, acc_ple, acc_qk, acc_attn)
    small = [norm_gain, ple_gate_norm_gain, b_ple_gate, ple_norm_gain, q_norm_gain, k_norm_gain, attn_sinks]
    small_m = [m_norm_gain, m_ple_gate_norm_gain, m_b_ple_gate, m_ple_norm_gain, m_q_norm_gain, m_k_norm_gain,
               m_attn_sinks]
    small_v = [v_norm_gain, v_ple_gate_norm_gain, v_b_ple_gate, v_ple_norm_gain, v_q_norm_gain, v_k_norm_gain,
               v_attn_sinks]
    taps_first = lambda t: jnp.transpose(t, (1, 0, 2))
    loss, kinds = _adamw_small(red, me.reshape(1, 1).astype(jnp.int32), small + [taps_first(conv_w)],
                               small_m + [taps_first(m_conv_w)], small_v + [taps_first(v_conv_w)])

    def order(k):
        sm = kinds[k]
        return [sm[0], big["w_in"][k], sm[4], sm[5], sm[6], taps_first(sm[7]), big["w_out"][k], sm[1],
                big["w_ple_gate"][k], sm[2], big["w_ple_proj"][k], sm[3]]

    return (loss[0, 0], grad_x[None], *order(0), *order(1), *order(2), *order(3))
```

```python
import jax
import jax.numpy as jnp
from jax import lax
from jax.experimental import pallas as pl
from jax.experimental.pallas import tpu as pltpu

F32, BF16 = jnp.float32, jnp.bfloat16

D_MODEL = 2048
PLE_DIM = 256
ATTN_W = 1024
HEAD = 64
N_Q_HEADS = 16
KV_W = 256
QKV_W = ATTN_W + 2 * KV_W
REST_W = 5 * 1024
IN_W = QKV_W + REST_W
GATE_A0, CONV_B0, CONV_C0, CONV_H0, GATE_C0 = (QKV_W + 1024 * t for t in range(5))
K2_W = 4 * 128
ROT = 16
ROPE_THETA = 500000.0
EPS = 1e-6
NEG_INF = -1e30
BLK = 128
LANES = 128
SUBLANES = 8
N_DEV = 8
SHARD_IN = IN_W // N_DEV
PAIR_W = 2 * SHARD_IN
N_PAIRS = IN_W // PAIR_W
SLAB_ROWS = 8
SUB_ROWS = 128
V7X_VMEM_LIMIT = 52 * 1024 * 1024

ADAM_LR, ADAM_B1, ADAM_B2, ADAM_EPS, ADAM_WD, ADAM_STEP = 0.001, 0.9, 0.999, 1e-08, 0.01, 10
MESH = pl.DeviceIdType.MESH


def _params(*semantics):
    return pltpu.CompilerParams(dimension_semantics=semantics, vmem_limit_bytes=V7X_VMEM_LIMIT)


ANY = pl.BlockSpec(memory_space=pl.ANY)


def _resident(shape):
    return pl.BlockSpec(shape, lambda *_: (0,) * len(shape), pipeline_mode=pl.Buffered(1))


def _dot(a, b):
    return jnp.dot(a, b, preferred_element_type=F32)


def _dot_nt(a, b):
    return lax.dot_general(a, b, (((1,), (1,)), ((), ())), preferred_element_type=F32)


def _rms(xf):
    r = lax.rsqrt(jnp.mean(xf * xf, axis=-1, keepdims=True) + EPS)
    return xf * r, r


def _rms_bwd(dxn, xn, r):
    return r * (dxn - xn * jnp.mean(dxn * xn, axis=-1, keepdims=True))


def _sig(g):
    return jax.nn.sigmoid(g)


def _dsilu(g, sg):
    return sg * (1.0 + g * (1.0 - sg))


def _low_half(shape):
    return lax.broadcasted_iota(jnp.int32, shape, len(shape) - 1) < HEAD


def _half_sums(v):
    lo = _low_half(v.shape)
    s_lo = jnp.sum(jnp.where(lo, v, 0.0), axis=-1, keepdims=True)
    s_hi = jnp.sum(jnp.where(lo, 0.0, v), axis=-1, keepdims=True)
    return jnp.where(lo, s_lo, s_hi)


def _rope(v, a, bm, bp):
    return v * a + pltpu.roll(v, LANES - ROT // 2, 1) * bm + pltpu.roll(v, ROT // 2, 1) * bp


def _rope_t(dy, a, bm, bp):
    return dy * a + pltpu.roll(dy * bm, ROT // 2, 1) + pltpu.roll(dy * bp, LANES - ROT // 2, 1)


def _dup_halves(v):
    lo = _low_half(v.shape)
    a = jnp.where(lo, v, 0.0)
    b = jnp.where(lo, 0.0, v)
    return a + pltpu.roll(a, HEAD, 1), b + pltpu.roll(b, HEAD, 1)


def _rope_tables(s):
    half = ROT // 2
    lane = lax.broadcasted_iota(jnp.int32, (s, LANES), 1) % HEAD
    pos = lax.broadcasted_iota(jnp.int32, (half, s), 1).astype(F32)
    freq = lax.broadcasted_iota(jnp.int32, (half, s), 0).astype(F32)
    ang = pos * jnp.power(jnp.float32(ROPE_THETA), -freq * 2.0 / ROT)
    cos, sin = lax.optimization_barrier((jnp.cos(ang), jnp.sin(ang)))
    cos, sin = (jnp.tile(t.T, (1, LANES // half)) for t in (cos, sin))
    a = jnp.where(lane < ROT, cos, 1.0)
    bm = jnp.where(lane < half, -sin, 0.0)
    bp = jnp.where((lane >= half) & (lane < ROT), sin, 0.0)
    return a, bm, bp


def _prenorm(x, g1, tm, after):
    s = x.shape[0]

    def body(x_ref, g_ref, after_ref, h_ref):
        xn, _ = _rms(x_ref[...])
        h_ref[...] = (xn * g_ref[...]).astype(BF16)

    return pl.pallas_call(
        body, name="prenorm",
        out_shape=jax.ShapeDtypeStruct((s, D_MODEL), BF16),
        grid=(s // tm,),
        in_specs=[pl.BlockSpec((tm, D_MODEL), lambda i: (i, 0)), pl.BlockSpec((1, D_MODEL), lambda i: (0, 0)), ANY],
        out_specs=pl.BlockSpec((tm, D_MODEL), lambda i: (i, 0)),
        compiler_params=_params("parallel"))(x, g1, after)


def _fwd_in_pair(h, shards, z, w_pairs, pair, tm, name, own=None):
    s = h.shape[0]

    def body(pair_ref, h_ref, lo_ref, hi_ref, z_in, wp_in, z_ref, wp_ref):
        @pl.when(pl.program_id(0) == 0)
        def _():
            wp_ref[0, 0:SHARD_IN, :] = lo_ref[0]
            wp_ref[0, SHARD_IN:PAIR_W, :] = hi_ref[0]

        z_ref[...] = _dot_nt(h_ref[...], wp_ref[0])

    def body_own(pair_ref, h_ref, own_ref, other_ref, z_in, wp_in, z_ref, wp_ref):
        @pl.when(pl.program_id(0) == 0)
        def _():
            first = pl.multiple_of(pair_ref[1] * SHARD_IN, SHARD_IN)
            wp_ref[0, pl.ds(first, SHARD_IN), :] = own_ref[...]
            wp_ref[0, pl.ds(SHARD_IN - first, SHARD_IN), :] = other_ref[0]

        z_ref[...] = _dot_nt(h_ref[...], wp_ref[0])

    if own is None:
        blocks = [pl.BlockSpec((1, SHARD_IN, D_MODEL), lambda i, p: (2 * p[0], 0, 0)),
                  pl.BlockSpec((1, SHARD_IN, D_MODEL), lambda i, p: (2 * p[0] + 1, 0, 0))]
        operands = (shards, shards)
    else:
        blocks = [pl.BlockSpec((SHARD_IN, D_MODEL), lambda i, p: (0, 0)),
                  pl.BlockSpec((1, SHARD_IN, D_MODEL), lambda i, p: (2 * p[0] + 1 - p[1], 0, 0))]
        operands = (own, shards)
    grid_spec = pltpu.PrefetchScalarGridSpec(
        num_scalar_prefetch=1, grid=(s // tm,),
        in_specs=[pl.BlockSpec((tm, D_MODEL), lambda i, p: (i, 0)), *blocks, ANY, ANY],
        out_specs=(pl.BlockSpec((tm, PAIR_W), lambda i, p: (i, p[0])),
                   pl.BlockSpec((1, PAIR_W, D_MODEL), lambda i, p: (p[0], 0, 0))))
    return pl.pallas_call(
        body if own is None else body_own, name=name, grid_spec=grid_spec,
        out_shape=(jax.ShapeDtypeStruct(z.shape, z.dtype), jax.ShapeDtypeStruct(w_pairs.shape, w_pairs.dtype)),
        input_output_aliases={4: 0, 5: 1},
        compiler_params=_params("arbitrary"))(pair, h, *operands, z, w_pairs)


def _qk_prep(z, ra, rbm, rbp, gq2, gk2, tm, after):
    s = z.shape[0]

    def body(z_ref, a_ref, bm_ref, bp_ref, gq_ref, gk_ref, after_ref, q_ref, k2_ref, v2_ref):
        a, bm, bp = a_ref[...], bm_ref[...], bp_ref[...]
        for r in range(ATTN_W // LANES):
            x = z_ref[:, LANES * r:LANES * (r + 1)]
            rr = lax.rsqrt(_half_sums(x * x) * (1.0 / HEAD) + EPS)
            q_ref[:, LANES * r:LANES * (r + 1)] = _rope(x * rr * gq_ref[...], a, bm, bp).astype(BF16)
        for m in range(KV_W // LANES):
            x = z_ref[:, ATTN_W + LANES * m:ATTN_W + LANES * (m + 1)]
            rr = lax.rsqrt(_half_sums(x * x) * (1.0 / HEAD) + EPS)
            k_lo, k_hi = _dup_halves(_rope(x * rr * gk_ref[...], a, bm, bp))
            k2_ref[:, 2 * LANES * m:2 * LANES * m + LANES] = k_lo.astype(BF16)
            k2_ref[:, 2 * LANES * m + LANES:2 * LANES * (m + 1)] = k_hi.astype(BF16)
            v_lo, v_hi = _dup_halves(z_ref[:, ATTN_W + KV_W + LANES * m:ATTN_W + KV_W + LANES * (m + 1)])
            v2_ref[:, 2 * LANES * m:2 * LANES * m + LANES] = v_lo.astype(BF16)
            v2_ref[:, 2 * LANES * m + LANES:2 * LANES * (m + 1)] = v_hi.astype(BF16)

    row = lambda w: pl.BlockSpec((tm, w), lambda i: (i, 0))
    one = pl.BlockSpec((1, LANES), lambda i: (0, 0))
    return pl.pallas_call(
        body, name="qk_prep",
        out_shape=(jax.ShapeDtypeStruct((s, ATTN_W), BF16), jax.ShapeDtypeStruct((s, K2_W), BF16),
                   jax.ShapeDtypeStruct((s, K2_W), BF16)),
        grid=(s // tm,),
        in_specs=[row(PAIR_W), row(LANES), row(LANES), row(LANES), one, one, ANY],
        out_specs=(row(ATTN_W), row(K2_W), row(K2_W)),
        compiler_params=_params("parallel"))(z, ra, rbm, rbp, gq2, gk2, after)


GROUP = 4


def _window_mask(n):
    row = lax.broadcasted_iota(jnp.int32, (GROUP * BLK, 2 * BLK), 0) % BLK
    col = lax.broadcasted_iota(jnp.int32, (GROUP * BLK, 2 * BLK), 1)
    return (col > row) & (col <= row + BLK) & ((col >= BLK) | (n > 0))


def _stack_heads(pairs, zero):
    lo = _low_half(pairs[0].shape)
    parts = []
    for v in pairs:
        parts += [jnp.where(lo, v, zero), jnp.where(lo, zero, v)]
    return jnp.concatenate(parts, axis=0)


def _unstack_heads(v4):
    lo = _low_half((BLK, LANES))
    return [jnp.where(lo, v4[2 * i * BLK:(2 * i + 1) * BLK], v4[(2 * i + 1) * BLK:(2 * i + 2) * BLK]) for i in range(2)]


def _group_sinks(sink_ref, kvh):
    slot = lax.broadcasted_iota(jnp.int32, (GROUP * BLK, 1), 0) // BLK
    col = jnp.zeros((GROUP * BLK, 1), F32)
    for i in range(GROUP):
        col = jnp.where(slot == i, sink_ref[0, GROUP * kvh + i], col)
    return col, slot


def _head_probs(qm, kw, valid, sink):
    sc = jnp.where(valid, _dot_nt(qm, kw) * (HEAD ** -0.5), NEG_INF)
    mx = jnp.maximum(jnp.max(sc, axis=-1, keepdims=True), sink)
    ex = jnp.exp(sc - mx)
    den = jnp.sum(ex, axis=-1, keepdims=True) + jnp.exp(sink - mx)
    return ex / den, mx, den


def _cols(start, width=ATTN_W):
    return slice(start, start + width)


def _conv_fwd(z_ref, zp_ref, cw_ref, ext_ref, n):
    u = z_ref[:, _cols(CONV_C0)] * z_ref[:, _cols(CONV_H0)]
    pu = zp_ref[:, _cols(CONV_C0)] * zp_ref[:, _cols(CONV_H0)]
    ext_ref[0:SUBLANES, :] = jnp.where(n > 0, pu, 0.0)
    ext_ref[SUBLANES:SUBLANES + BLK, :] = u
    um1 = ext_ref[SUBLANES - 1:SUBLANES - 1 + BLK, :]
    um2 = ext_ref[SUBLANES - 2:SUBLANES - 2 + BLK, :]
    cv = cw_ref[0:1, :] * um2 + cw_ref[1:2, :] * um1 + cw_ref[2:3, :] * u
    return u, um1, um2, cv


def _prev_rows(n):
    return (jnp.maximum(n * (BLK // SUBLANES) - 1, 0), 0)


def _attn_fwd(qn, k2, v2, z, conv_wp, sinks):
    s = qn.shape[0]
    nb = s // BLK

    def body(sink_ref, q_ref, kc_ref, kp_ref, vc_ref, vp_ref, z_ref, zp_ref, cw_ref, a_ref, mix_ref, mixt_ref,
             ext_ref):
        n = pl.program_id(0)
        valid = _window_mask(n)
        for kvh in range(K2_W // LANES):
            cols = slice(LANES * kvh, LANES * (kvh + 1))
            kw = jnp.concatenate([kp_ref[:, cols], kc_ref[:, cols]], axis=0)
            vw = jnp.concatenate([vp_ref[:, cols], vc_ref[:, cols]], axis=0)
            blocks = [slice(LANES * r, LANES * (r + 1)) for r in (2 * kvh, 2 * kvh + 1)]
            q4 = _stack_heads([q_ref[:, rc] for rc in blocks], jnp.zeros((BLK, LANES), BF16))
            p, _, _ = _head_probs(q4, kw, valid, _group_sinks(sink_ref, kvh)[0])
            for rc, a in zip(blocks, _unstack_heads(_dot(p.astype(BF16), vw))):
                a_ref[:, rc] = a
                g = z_ref[:, _cols(GATE_A0 + rc.start, LANES)]
                mix_ref[:, rc] = (a * (g * _sig(g))).astype(BF16)
        _, _, _, cv = _conv_fwd(z_ref, zp_ref, cw_ref, ext_ref, n)
        gc = z_ref[:, _cols(GATE_C0)]
        mix_ref[:, ATTN_W:D_MODEL] = (z_ref[:, _cols(CONV_B0)] * cv * (gc * _sig(gc))).astype(BF16)
        mixt_ref[...] = mix_ref[...].T

    cur = lambda w: pl.BlockSpec((BLK, w), lambda n: (n, 0))
    prev = lambda w: pl.BlockSpec((BLK, w), lambda n: (jnp.maximum(n - 1, 0), 0))
    return pl.pallas_call(
        body, name="attn_fwd",
        out_shape=(jax.ShapeDtypeStruct((s, ATTN_W), F32), jax.ShapeDtypeStruct((s, D_MODEL), BF16),
                   jax.ShapeDtypeStruct((D_MODEL, s), BF16)),
        grid=(nb,),
        in_specs=[pl.BlockSpec(memory_space=pltpu.SMEM),
                  cur(ATTN_W), cur(K2_W), prev(K2_W), cur(K2_W), prev(K2_W), cur(IN_W),
                  pl.BlockSpec((SUBLANES, IN_W), _prev_rows),
                  pl.BlockSpec((SUBLANES, ATTN_W), lambda n: (0, 0))],
        out_specs=(cur(ATTN_W), cur(D_MODEL), pl.BlockSpec((D_MODEL, BLK), lambda n: (0, n))),
        scratch_shapes=[pltpu.VMEM((BLK + 2 * SUBLANES, ATTN_W), F32)],
        compiler_params=_params("parallel"))(sinks, qn, k2, k2, v2, v2, z, z, conv_wp)


def _fwd_out(mix, w_out, x, g2, tm):
    s = x.shape[0]

    def body(m_ref, w_ref, x_ref, g_ref, x1_ref, h_ref, ht_ref):
        x1 = x_ref[...] + _dot(m_ref[...], w_ref[...])
        x1_ref[...] = x1
        xn, _ = _rms(x1)
        h = (xn * g_ref[...]).astype(BF16)
        h_ref[...] = h
        ht_ref[...] = h.T

    row = pl.BlockSpec((tm, D_MODEL), lambda i: (i, 0))
    return pl.pallas_call(
        body, name="fwd_out",
        out_shape=(jax.ShapeDtypeStruct((s, D_MODEL), F32), jax.ShapeDtypeStruct((s, D_MODEL), BF16),
                   jax.ShapeDtypeStruct((D_MODEL, s), BF16)),
        grid=(s // tm,),
        in_specs=[row, _resident((D_MODEL, D_MODEL)), row, pl.BlockSpec((1, D_MODEL), lambda i: (0, 0))],
        out_specs=(row, row, pl.BlockSpec((D_MODEL, tm), lambda i: (0, i))),
        compiler_params=_params("parallel"))(mix, w_out, x, g2)


def _ple(hn2, w_pg, b_pg, p, w_pp, g3, x1, target, tm):
    s = x1.shape[0]

    def body(h_ref, wg_ref, b_ref, p_ref, wp_ref, g3_ref, x1_ref, t_ref, dy_ref, dgp_ref, dt_ref, pt_ref, acc_ref):
        gate = _sig(_dot(h_ref[...], wg_ref[...]) + b_ref[...])
        pb = p_ref[...].astype(BF16)
        pt_ref[...] = pb.T
        t = _dot(pb, wp_ref[...])
        tn, r3 = _rms(t)
        e = tn * g3_ref[...]
        diff = x1_ref[...] + gate * e - t_ref[...]
        dy = diff * (1.0 / D_MODEL)
        dy_ref[...] = dy
        dgp = dy * e * (gate * (1.0 - gate))
        dgp_ref[...] = dgp.astype(BF16)
        de = dy * gate
        dt_ref[...] = _rms_bwd(de * g3_ref[...], tn, r3).astype(BF16)

        @pl.when(pl.program_id(0) == 0)
        def _():
            acc_ref[...] = jnp.zeros_like(acc_ref)

        acc_ref[0:1, :] += jnp.sum(dgp, axis=0, keepdims=True)
        acc_ref[1:2, :] += jnp.sum(de * tn, axis=0, keepdims=True)
        acc_ref[2:3, :] += jnp.sum(diff * diff, axis=0, keepdims=True) * (0.5 / D_MODEL)

    row = pl.BlockSpec((tm, D_MODEL), lambda i: (i, 0))
    vec = pl.BlockSpec((1, D_MODEL), lambda i: (0, 0))
    return pl.pallas_call(
        body, name="ple",
        out_shape=(jax.ShapeDtypeStruct((s, D_MODEL), F32), jax.ShapeDtypeStruct((s, D_MODEL), BF16),
                   jax.ShapeDtypeStruct((s, D_MODEL), BF16), jax.ShapeDtypeStruct((PLE_DIM, s), BF16),
                   jax.ShapeDtypeStruct((SUBLANES, D_MODEL), F32)),
        grid=(s // tm,),
        in_specs=[row, _resident((D_MODEL, D_MODEL)), vec, pl.BlockSpec((tm, PLE_DIM), lambda i: (i, 0)),
                  _resident((PLE_DIM, D_MODEL)), vec, row, row],
        out_specs=(row, row, row, pl.BlockSpec((PLE_DIM, tm), lambda i: (0, i)),
                   pl.BlockSpec((SUBLANES, D_MODEL), lambda i: (0, 0))),
        compiler_params=_params("arbitrary"))(hn2, w_pg, b_pg, p, w_pp, g3, x1, target)


def _gate_bwd(dgp, w_pg, x1, dy, g2, tm):
    s = x1.shape[0]

    def body(d_ref, w_ref, x1_ref, dy_ref, g_ref, dx_ref, dxb_ref, acc_ref):
        dh = _dot_nt(d_ref[...], w_ref[...])
        xn, r = _rms(x1_ref[...])
        dx1 = dy_ref[...] + _rms_bwd(dh * g_ref[...], xn, r)
        dx_ref[...] = dx1
        dxb_ref[...] = dx1.astype(BF16)

        @pl.when(pl.program_id(0) == 0)
        def _():
            acc_ref[...] = jnp.zeros_like(acc_ref)

        acc_ref[0:1, :] += jnp.sum(dh * xn, axis=0, keepdims=True)

    row = pl.BlockSpec((tm, D_MODEL), lambda i: (i, 0))
    return pl.pallas_call(
        body, name="gate_bwd",
        out_shape=(jax.ShapeDtypeStruct((s, D_MODEL), F32), jax.ShapeDtypeStruct((s, D_MODEL), BF16),
                   jax.ShapeDtypeStruct((SUBLANES, D_MODEL), F32)),
        grid=(s // tm,),
        in_specs=[row, _resident((D_MODEL, D_MODEL)), row, row, pl.BlockSpec((1, D_MODEL), lambda i: (0, 0))],
        out_specs=(row, row, pl.BlockSpec((SUBLANES, D_MODEL), lambda i: (0, 0))),
        compiler_params=_params("arbitrary"))(dgp, w_pg, x1, dy, g2)


def _mm_nt(a, b, tm, name, after):
    m, k = a.shape
    n = b.shape[0]

    def body(a_ref, b_ref, after_ref, o_ref):
        o_ref[...] = _dot_nt(a_ref[...], b_ref[...])

    return pl.pallas_call(
        body, name=name,
        out_shape=jax.ShapeDtypeStruct((m, n), F32),
        grid=(m // tm,),
        in_specs=[pl.BlockSpec((tm, k), lambda i: (i, 0)), _resident((n, k)), ANY],
        out_specs=pl.BlockSpec((tm, n), lambda i: (i, 0)),
        compiler_params=_params("parallel"))(a, b, after)


def _attn_bwd(qn, k2, v2, a, z, dmix, conv_wp, sinks, after):
    s = qn.shape[0]
    nb = s // BLK

    def body(sink_ref, q_ref, kc_ref, kp_ref, vc_ref, vp_ref, a_ref, z_ref, zp_ref, zn_ref, dm_ref, dmn_ref,
             cw_ref, after_ref, dq_ref, dkc_ref, dkp_ref, dvc_ref, dvp_ref, dz_ref, dzt_ref, acc_ref, ext_ref):
        n = pl.program_id(0)
        valid = _window_mask(n)
        lane = lax.broadcasted_iota(jnp.int32, (1, ATTN_W), 1)

        @pl.when(n == 0)
        def _():
            acc_ref[...] = jnp.zeros_like(acc_ref)

        dz_ref[:, 0:QKV_W] = jnp.zeros((BLK, QKV_W), BF16)
        dsink = jnp.zeros((1, ATTN_W), F32)
        for kvh in range(K2_W // LANES):
            cols = slice(LANES * kvh, LANES * (kvh + 1))
            kw = jnp.concatenate([kp_ref[:, cols], kc_ref[:, cols]], axis=0)
            vw = jnp.concatenate([vp_ref[:, cols], vc_ref[:, cols]], axis=0)
            blocks = [slice(LANES * r, LANES * (r + 1)) for r in (2 * kvh, 2 * kvh + 1)]
            das, avs = [], []
            for rc in blocks:
                g = z_ref[:, _cols(GATE_A0 + rc.start, LANES)]
                sg = _sig(g)
                dm = dm_ref[:, rc]
                av = a_ref[:, rc]
                das.append(dm * (g * sg))
                avs += [av, av]
                dz_ref[:, _cols(GATE_A0 + rc.start, LANES)] = (dm * av * _dsilu(g, sg)).astype(BF16)
            q4 = _stack_heads([q_ref[:, rc] for rc in blocks], jnp.zeros((BLK, LANES), BF16))
            sink, slot = _group_sinks(sink_ref, kvh)
            p, mx, den = _head_probs(q4, kw, valid, sink)
            do4 = _stack_heads(das, 0.0)
            delta = jnp.sum(do4 * jnp.concatenate(avs, axis=0), axis=-1, keepdims=True)
            dob = do4.astype(BF16)
            ds = p * (_dot_nt(dob, vw) - delta) * (HEAD ** -0.5)
            for rc, dq in zip(blocks, _unstack_heads(_dot(ds.astype(BF16), kw))):
                dq_ref[:, rc] = dq
            dk2 = _dot(ds.T.astype(BF16), q4)
            dv2 = _dot(p.T.astype(BF16), dob)
            dkp_ref[:, cols] = dk2[0:BLK]
            dkc_ref[:, cols] = dk2[BLK:2 * BLK]
            dvp_ref[:, cols] = dv2[0:BLK]
            dvc_ref[:, cols] = dv2[BLK:2 * BLK]
            dsk = jnp.exp(sink - mx) / den * delta
            for i in range(GROUP):
                dsink = dsink - jnp.where(lane == GROUP * kvh + i,
                                          jnp.sum(jnp.where(slot == i, dsk, 0.0), axis=0, keepdims=True), 0.0)
        acc_ref[0:1, :] += dsink

        u, um1, um2, cv = _conv_fwd(z_ref, zp_ref, cw_ref, ext_ref, n)
        cb = z_ref[:, _cols(CONV_B0)]
        gc = z_ref[:, _cols(GATE_C0)]
        sgc = _sig(gc)
        dmc = dm_ref[:, ATTN_W:D_MODEL]
        t = dmc * (gc * sgc)
        dcv = t * cb
        dz_ref[:, _cols(CONV_B0)] = (t * cv).astype(BF16)
        dz_ref[:, _cols(GATE_C0)] = (dmc * cb * cv * _dsilu(gc, sgc)).astype(BF16)
        gcn = zn_ref[:, _cols(GATE_C0)]
        dcvn = dmn_ref[:, ATTN_W:D_MODEL] * (gcn * _sig(gcn)) * zn_ref[:, _cols(CONV_B0)]
        ext_ref[0:BLK, :] = dcv
        ext_ref[BLK:BLK + SUBLANES, :] = jnp.where(n < nb - 1, dcvn, 0.0)
        du = (cw_ref[2:3, :] * dcv + cw_ref[1:2, :] * ext_ref[1:1 + BLK, :]
              + cw_ref[0:1, :] * ext_ref[2:2 + BLK, :])
        dz_ref[:, _cols(CONV_C0)] = (du * z_ref[:, _cols(CONV_H0)]).astype(BF16)
        dz_ref[:, _cols(CONV_H0)] = (du * z_ref[:, _cols(CONV_C0)]).astype(BF16)
        acc_ref[1:2, :] += jnp.sum(dcv * um2, axis=0, keepdims=True)
        acc_ref[2:3, :] += jnp.sum(dcv * um1, axis=0, keepdims=True)
        acc_ref[3:4, :] += jnp.sum(dcv * u, axis=0, keepdims=True)
        dzt_ref[...] = dz_ref[...].T

    cur = lambda w: pl.BlockSpec((BLK, w), lambda n: (n, 0))
    prev = lambda w: pl.BlockSpec((BLK, w), lambda n: (jnp.maximum(n - 1, 0), 0))
    nxt = lambda w: pl.BlockSpec(
        (SUBLANES, w), lambda n: (jnp.minimum((n + 1) * (BLK // SUBLANES), nb * (BLK // SUBLANES) - 1), 0))
    f32 = lambda w: jax.ShapeDtypeStruct((s, w), F32)
    return pl.pallas_call(
        body, name="attn_bwd",
        out_shape=(f32(ATTN_W), f32(K2_W), f32(K2_W), f32(K2_W), f32(K2_W),
                   jax.ShapeDtypeStruct((s, IN_W), BF16), jax.ShapeDtypeStruct((IN_W, s), BF16),
                   jax.ShapeDtypeStruct((SUBLANES, ATTN_W), F32)),
        grid=(nb,),
        in_specs=[pl.BlockSpec(memory_space=pltpu.SMEM),
                  cur(ATTN_W), cur(K2_W), prev(K2_W), cur(K2_W), prev(K2_W), cur(ATTN_W), cur(IN_W),
                  pl.BlockSpec((SUBLANES, IN_W), _prev_rows), nxt(IN_W), cur(D_MODEL), nxt(D_MODEL),
                  pl.BlockSpec((SUBLANES, ATTN_W), lambda n: (0, 0)), ANY],
        out_specs=(cur(ATTN_W), cur(K2_W), cur(K2_W), cur(K2_W), cur(K2_W), cur(IN_W),
                   pl.BlockSpec((IN_W, BLK), lambda n: (0, n)), pl.BlockSpec((SUBLANES, ATTN_W), lambda n: (0, 0))),
        scratch_shapes=[pltpu.VMEM((BLK + 2 * SUBLANES, ATTN_W), F32)],
        compiler_params=_params("arbitrary"))(sinks, qn, k2, k2, v2, v2, a, z, z, z, dmix, dmix, conv_wp, after)


def _qkv_bwd(z, dz, dzt, dq, dkc, dkp, dvc, dvp, ra, rbm, rbp, gq2, gk2):
    s = z.shape[0]
    nb = s // BLK

    def body(z_ref, dz_in, dzt_in, dq_ref, dkc_ref, dkp_ref, dvc_ref, dvp_ref, a_ref, bm_ref, bp_ref, gq_ref, gk_ref,
             dz_ref, dzt_ref, acc_ref):
        n = pl.program_id(0)
        a, bm, bp = a_ref[...], bm_ref[...], bp_ref[...]
        lo = _low_half((BLK, LANES))
        last = n == nb - 1

        @pl.when(n == 0)
        def _():
            acc_ref[...] = jnp.zeros_like(acc_ref)

        def norm_bwd(x, dy, gain):
            rr = lax.rsqrt(_half_sums(x * x) * (1.0 / HEAD) + EPS)
            xh = x * rr
            dxg = _rope_t(dy, a, bm, bp)
            dxh = dxg * gain
            dx = rr * (dxh - xh * (_half_sums(dxh * xh) * (1.0 / HEAD)))
            return dx, jnp.sum(dxg * xh, axis=0, keepdims=True)

        def folded(cur_ref, prev_ref, m):
            parts = []
            for h in (2 * m, 2 * m + 1):
                v = cur_ref[:, LANES * h:LANES * (h + 1)] + jnp.where(
                    last, 0.0, prev_ref[:, LANES * h:LANES * (h + 1)])
                parts.append(v + pltpu.roll(v, HEAD, 1))
            return jnp.where(lo, parts[0], parts[1])

        gq_acc = jnp.zeros((1, LANES), F32)
        for r in range(ATTN_W // LANES):
            rc = slice(LANES * r, LANES * (r + 1))
            dx, gg = norm_bwd(z_ref[:, rc], dq_ref[:, rc], gq_ref[...])
            dz_ref[:, rc] = dx.astype(BF16)
            gq_acc = gq_acc + gg
        acc_ref[0:1, :] += gq_acc
        gk_acc = jnp.zeros((1, LANES), F32)
        for m in range(KV_W // LANES):
            kc = slice(ATTN_W + LANES * m, ATTN_W + LANES * (m + 1))
            dx, gg = norm_bwd(z_ref[:, kc], folded(dkc_ref, dkp_ref, m), gk_ref[...])
            dz_ref[:, kc] = dx.astype(BF16)
            gk_acc = gk_acc + gg
            vc = slice(ATTN_W + KV_W + LANES * m, ATTN_W + KV_W + LANES * (m + 1))
            dz_ref[:, vc] = folded(dvc_ref, dvp_ref, m).astype(BF16)
        acc_ref[1:2, :] += gk_acc
        dzt_ref[...] = dz_ref[...].T

    cur = lambda w: pl.BlockSpec((BLK, w), lambda n: (n, 0))
    nxt = lambda w: pl.BlockSpec((BLK, w), lambda n: (jnp.minimum(n + 1, nb - 1), 0))
    one = pl.BlockSpec((1, LANES), lambda n: (0, 0))
    return pl.pallas_call(
        body, name="qkv_bwd",
        out_shape=(jax.ShapeDtypeStruct(dz.shape, dz.dtype), jax.ShapeDtypeStruct(dzt.shape, dzt.dtype),
                   jax.ShapeDtypeStruct((SUBLANES, LANES), F32)),
        grid=(nb,),
        in_specs=[cur(PAIR_W), ANY, ANY, cur(ATTN_W), cur(K2_W), nxt(K2_W), cur(K2_W), nxt(K2_W),
                  cur(LANES), cur(LANES), cur(LANES), one, one],
        out_specs=(cur(QKV_W), pl.BlockSpec((QKV_W, BLK), lambda n: (0, n)),
                   pl.BlockSpec((SUBLANES, LANES), lambda n: (0, 0))),
        input_output_aliases={1: 0, 2: 1},
        compiler_params=_params("arbitrary"))(z, dz, dzt, dq, dkc, dkp, dvc, dvp, ra, rbm, rbp, gq2, gk2)


def _in_bwd(dz, w_pairs, x, dx1, g1, tm, after):
    s = x.shape[0]

    def body(d_ref, w_ref, x_hbm, dx1_hbm, g_ref, after_ref, gx_ref, acc_ref, x_buf, dx1_buf, sems):
        i, k = pl.program_id(0), pl.program_id(1)
        rows = pl.ds(pl.multiple_of(i * tm, tm), tm)
        fetch = [pltpu.make_async_copy(x_hbm.at[rows], x_buf, sems.at[0]),
                 pltpu.make_async_copy(dx1_hbm.at[rows], dx1_buf, sems.at[1])]
        sub = min(SUB_ROWS, tm)
        blocks = [slice(r, r + sub) for r in range(0, tm, sub)]

        @pl.when(k == 0)
        def _():
            for cp in fetch:
                cp.start()
            gx_ref[...] = _dot(d_ref[...], w_ref[0])

        @pl.when(k > 0)
        def _():
            gx_ref[...] += _dot(d_ref[...], w_ref[0])

        @pl.when((i == 0) & (k == 0))
        def _():
            acc_ref[...] = jnp.zeros_like(acc_ref)

        @pl.when(k == N_PAIRS - 1)
        def _():
            for cp in fetch:
                cp.wait()
            for rb in blocks:
                dh = gx_ref[rb, :]
                xn, r = _rms(x_buf[rb, :])
                gx_ref[rb, :] = dx1_buf[rb, :] + _rms_bwd(dh * g_ref[...], xn, r)
                acc_ref[0:1, :] += jnp.sum(dh * xn, axis=0, keepdims=True)

    return pl.pallas_call(
        body, name="in_bwd",
        out_shape=(jax.ShapeDtypeStruct((s, D_MODEL), F32), jax.ShapeDtypeStruct((SUBLANES, D_MODEL), F32)),
        grid=(s // tm, N_PAIRS),
        in_specs=[pl.BlockSpec((tm, PAIR_W), lambda i, k: (i, k)),
                  pl.BlockSpec((1, PAIR_W, D_MODEL), lambda i, k: (k, 0, 0)),
                  ANY, ANY, pl.BlockSpec((1, D_MODEL), lambda i, k: (0, 0)), ANY],
        out_specs=(pl.BlockSpec((tm, D_MODEL), lambda i, k: (i, 0)),
                   pl.BlockSpec((SUBLANES, D_MODEL), lambda i, k: (0, 0))),
        scratch_shapes=[pltpu.VMEM((tm, D_MODEL), F32), pltpu.VMEM((tm, D_MODEL), F32),
                        pltpu.SemaphoreType.DMA((2,))],
        compiler_params=_params("arbitrary", "arbitrary"))(dz, w_pairs, x, dx1, g1, after)


def _mm_grad(at, bs, tn, name):
    m, kdim = at.shape
    nblk = [b.shape[1] // tn for b in bs]
    starts = [sum(nblk[:t]) for t in range(len(bs))]

    def body(a_ref, *refs):
        b_refs, o_ref = refs[:len(bs)], refs[len(bs)]
        j = pl.program_id(0)
        for t, b_ref in enumerate(b_refs):
            @pl.when((j >= starts[t]) & (j < starts[t] + nblk[t]))
            def _():
                o_ref[...] = _dot(a_ref[...], b_ref[...]).astype(BF16)

    def b_spec(t):
        return pl.BlockSpec((kdim, tn), lambda j: (0, jnp.clip(j - starts[t], 0, nblk[t] - 1)))

    return pl.pallas_call(
        body, name=name,
        out_shape=jax.ShapeDtypeStruct((m, sum(nblk) * tn), BF16),
        grid=(sum(nblk),),
        in_specs=[_resident((m, kdim))] + [b_spec(t) for t in range(len(bs))],
        out_specs=pl.BlockSpec((m, tn), lambda j: (0, j)),
        compiler_params=_params("parallel"))(at, *bs)


def _grad_w_in(dzt, h):
    kdim = h.shape[0]

    def body(d_ref, h_ref, o_ref):
        o_ref[0] = _dot(d_ref[...], h_ref[...]).astype(BF16)

    return pl.pallas_call(
        body, name="grad_w_in",
        out_shape=jax.ShapeDtypeStruct((N_DEV, SHARD_IN, D_MODEL), BF16),
        grid=(N_DEV,),
        in_specs=[pl.BlockSpec((SHARD_IN, kdim), lambda j: (j, 0)), _resident((kdim, D_MODEL))],
        out_specs=pl.BlockSpec((1, SHARD_IN, D_MODEL), lambda j: (j, 0, 0)),
        compiler_params=_params("parallel"))(dzt, h)


def _place():
    return lax.axis_index("x"), lax.axis_index("y"), lax.axis_index("c")


ROW_TAPS, ROW_MISC = 4, 5
Q_AT, K_AT, SINK_AT, LOSS_AT = (ATTN_W + LANES * t for t in range(4))
SMALL_AT = [(0, 0), (1, 0), (2, 0), (3, 0), (ROW_MISC, Q_AT), (ROW_MISC, K_AT), (ROW_MISC, SINK_AT)]


def _tap_at(tap):
    return ROW_TAPS + tap // 2, ATTN_W * (tap % 2)


def _reduce_small(acc_g1, acc_g2, acc_ple, acc_qk, acc_attn):
    def body(g1_ref, g2_ref, ple_ref, qk_ref, attn_ref, out_ref, slab_ref, gath_ref, send_sems, recv_sems):
        x, y, c = _place()
        me = 4 * x + 2 * y + c
        slab_ref[...] = jnp.zeros_like(slab_ref)
        slab_ref[0:1, :] = g1_ref[0:1, :]
        slab_ref[1:2, :] = g2_ref[0:1, :]
        slab_ref[2:4, :] = ple_ref[0:2, :]
        qk = qk_ref[0:2, :]
        qk = jnp.where(_low_half(qk.shape), qk + pltpu.roll(qk, HEAD, 1), 0.0)
        misc = slab_ref.at[ROW_MISC:ROW_MISC + 1]
        misc[:, Q_AT:Q_AT + LANES] = qk[0:1]
        misc[:, K_AT:K_AT + LANES] = qk[1:2]
        lane = lax.broadcasted_iota(jnp.int32, (1, LANES), 1)
        misc[:, SINK_AT:SINK_AT + LANES] = jnp.where(lane < N_Q_HEADS, attn_ref[0:1, 0:LANES], 0.0)
        misc[:, LOSS_AT:LOSS_AT + LANES] = sum(
            ple_ref[2:3, LANES * t:LANES * (t + 1)] for t in range(D_MODEL // LANES))
        for tap in range(3):
            row, at = _tap_at(tap)
            slab_ref[row:row + 1, at:at + ATTN_W] = attn_ref[1 + tap:2 + tap, :]
        gath_ref[me] = slab_ref[...]
        copies = []
        for k in range(1, N_DEV):
            peer = (x ^ (k >> 2), y ^ ((k >> 1) & 1), c ^ (k & 1))
            copies.append(pltpu.make_async_remote_copy(
                src_ref=slab_ref, dst_ref=gath_ref.at[me], send_sem=send_sems.at[k - 1],
                recv_sem=recv_sems.at[k - 1], device_id=peer, device_id_type=MESH))
        for cp in copies:
            cp.start()
        for cp in copies:
            cp.wait_recv()
        for cp in copies:
            cp.wait_send()
        total = gath_ref[0]
        for d in range(1, N_DEV):
            total = total + gath_ref[d]
        out_ref[...] = total

    vmem = pl.BlockSpec(memory_space=pltpu.VMEM)
    return pl.pallas_call(
        body, name="reduce_small",
        out_shape=jax.ShapeDtypeStruct((SLAB_ROWS, D_MODEL), F32),
        in_specs=[vmem] * 5, out_specs=vmem,
        scratch_shapes=[pltpu.VMEM((SLAB_ROWS, D_MODEL), F32), pltpu.VMEM((N_DEV, SLAB_ROWS, D_MODEL), F32),
                        pltpu.SemaphoreType.DMA((N_DEV - 1,)), pltpu.SemaphoreType.DMA((N_DEV - 1,))])(
            acc_g1, acc_g2, acc_ple, acc_qk, acc_attn)


def _pair_sum(g, r, place, tr, name):
    _, _, rows, cols = g.shape

    def body(place_ref, g_ref, r_ref, pb_ref, own_ref):
        tot = g_ref[0, 0].astype(F32) + r_ref[0].astype(F32)
        pb_ref[0] = tot.astype(BF16)

        @pl.when(pl.program_id(1) == place_ref[1])
        def _():
            own_ref[...] = tot

    grid_spec = pltpu.PrefetchScalarGridSpec(
        num_scalar_prefetch=1, grid=(rows // tr, 4),
        in_specs=[pl.BlockSpec((1, 1, tr, cols), lambda i, q, place_ref: (q, place_ref[0], i, 0)),
                  pl.BlockSpec((1, tr, cols), lambda i, q, place_ref: (q, i, 0))],
        out_specs=(pl.BlockSpec((1, tr, cols), lambda i, q, place_ref: (q, i, 0)),
                   pl.BlockSpec((tr, cols), lambda i, q, place_ref: (i, 0))))
    return pl.pallas_call(
        body, name=name, grid_spec=grid_spec,
        out_shape=(jax.ShapeDtypeStruct((4, rows, cols), BF16), jax.ShapeDtypeStruct((rows, cols), F32)),
        compiler_params=_params("arbitrary", "arbitrary"))(place, g, r)


HBM = pl.BlockSpec(memory_space=pltpu.HBM)
SEM = pl.BlockSpec(memory_space=pltpu.SEMAPHORE)
SIDE_EFFECT = pltpu.CompilerParams(has_side_effects=pltpu.SideEffectType.DATAFLOW_SIDE_EFFECTING)
TOKEN = jax.ShapeDtypeStruct((SUBLANES, LANES), F32)


def _hbm(a):
    return pltpu.with_memory_space_constraint(a, pltpu.HBM)


def _hbm_like(arrays):
    return tuple(pltpu.HBM(a.shape, a.dtype) for a in arrays)


def _block_of(px, py, pc):
    return 4 * px + 2 * py + pc


def _gather_start(shards, after):
    na = len(shards)
    lands = [_hbm(lax.empty((N_DEV,) + a.shape, a.dtype)) for a in shards]

    def body(*refs):
        ins, land = refs[:na], refs[na:2 * na]
        send_sems, recv_ici, recv_d2d = refs[2 * na + 1:2 * na + 4]
        token = refs[-1]
        x, y, c = _place()
        for k, peer in enumerate([(x, y, 1 - c), (1 - x, y, c), (x, 1 - y, c), (1 - x, 1 - y, c)]):
            for t in range(na):
                pltpu.make_async_remote_copy(
                    src_ref=ins[t], dst_ref=land[t].at[_block_of(x, y, c)], send_sem=send_sems.at[4 * t + k],
                    recv_sem=recv_d2d.at[4 * t] if k == 0 else recv_ici.at[3 * t + k - 1],
                    device_id=peer, device_id_type=MESH).start()
        token[...] = jnp.zeros_like(token)

    out = pl.pallas_call(
        body, name="gather_start",
        out_shape=(pltpu.SemaphoreType.DMA((4 * na,)), pltpu.SemaphoreType.DMA((3 * na,)),
                   pltpu.SemaphoreType.DMA((4 * na,)), *_hbm_like(lands), TOKEN),
        in_specs=[ANY] * na + [HBM] * na + [ANY],
        out_specs=(SEM, SEM, SEM, *[HBM] * na, pl.BlockSpec(memory_space=pltpu.VMEM)),
        input_output_aliases={na + i: 3 + i for i in range(na)},
        compiler_params=SIDE_EFFECT)(*shards, *lands, after)
    send_sems, recv_ici, recv_d2d = out[:3]
    state = dict(send=send_sems, ici=recv_ici, d2d=recv_d2d, shards=list(shards), lands=out[3:3 + na])
    return state, out[-1]


def _gather_forward(state, after):
    lands = state["lands"]
    na = len(lands)

    def body(*refs):
        land = refs[:na]
        recv_ici, recv_d2d = refs[na], refs[na + 1]
        fwd_sems, token = refs[-2], refs[-1]
        x, y, c = _place()
        for j, chip in enumerate([(1 - x, y), (x, 1 - y), (1 - x, 1 - y)]):
            for t in range(na):
                blk = land[t].at[_block_of(*chip, c)]
                pltpu.make_async_remote_copy(
                    src_ref=blk, dst_ref=blk, send_sem=fwd_sems.at[3 * t + j], recv_sem=recv_ici.at[3 * t + j],
                    device_id=(x, y, c), device_id_type=MESH).wait_recv()
                pltpu.make_async_remote_copy(
                    src_ref=blk, dst_ref=blk, send_sem=fwd_sems.at[3 * t + j], recv_sem=recv_d2d.at[4 * t + 1 + j],
                    device_id=(x, y, 1 - c), device_id_type=MESH).start()
        token[...] = jnp.zeros_like(token)

    out = pl.pallas_call(
        body, name="gather_forward",
        out_shape=(*_hbm_like(lands), pltpu.SemaphoreType.DMA((3 * na,)), TOKEN),
        in_specs=[HBM] * na + [SEM, SEM, ANY],
        out_specs=(*[HBM] * na, SEM, pl.BlockSpec(memory_space=pltpu.VMEM)),
        input_output_aliases={i: i for i in range(na)},
        compiler_params=SIDE_EFFECT)(*lands, state["ici"], state["d2d"], after)
    return dict(state, lands=out[:na], fwd=out[na]), out[-1]


def _gather_wait(state, after):
    shards, lands = state["shards"], state["lands"]
    na = len(lands)

    def body(*refs):
        ins, land = refs[:na], refs[na:2 * na]
        send_sems, fwd_sems, recv_d2d = refs[2 * na:2 * na + 3]
        x, y, c = _place()
        chips = [(1 - x, y), (x, 1 - y), (1 - x, 1 - y)]
        for t in range(na):
            mine = land[t].at[_block_of(x, y, c)]
            for k in range(4):
                pltpu.make_async_remote_copy(
                    src_ref=ins[t], dst_ref=mine, send_sem=send_sems.at[4 * t + k], recv_sem=recv_d2d.at[4 * t],
                    device_id=(x, y, c), device_id_type=MESH).wait_send()
            for j, chip in enumerate(chips):
                blk = land[t].at[_block_of(*chip, c)]
                pltpu.make_async_remote_copy(
                    src_ref=blk, dst_ref=blk, send_sem=fwd_sems.at[3 * t + j], recv_sem=recv_d2d.at[4 * t + 1 + j],
                    device_id=(x, y, c), device_id_type=MESH).wait_send()
            for k, blk_id in enumerate([_block_of(x, y, 1 - c)] + [_block_of(*chip, 1 - c) for chip in chips]):
                blk = land[t].at[blk_id]
                pltpu.make_async_remote_copy(
                    src_ref=blk, dst_ref=blk, send_sem=send_sems.at[4 * t], recv_sem=recv_d2d.at[4 * t + k],
                    device_id=(x, y, c), device_id_type=MESH).wait_recv()

    out = pl.pallas_call(
        body, name="gather_wait",
        out_shape=_hbm_like(lands),
        in_specs=[ANY] * na + [HBM] * na + [SEM, SEM, SEM, ANY],
        out_specs=tuple([HBM] * na),
        input_output_aliases={na + i: i for i in range(na)},
        compiler_params=SIDE_EFFECT)(*shards, *lands, state["send"], state["fwd"], state["d2d"], after)
    return out


def _gather_from_sibling(state, after):
    lands = state["lands"]
    na = len(lands)

    def body(*refs):
        land, recv_d2d = refs[:na], refs[na]
        x, y, c = _place()
        for t in range(na):
            blk = land[t].at[_block_of(x, y, 1 - c)]
            pltpu.make_async_remote_copy(src_ref=blk, dst_ref=blk, send_sem=recv_d2d.at[4 * t],
                                         recv_sem=recv_d2d.at[4 * t], device_id=(x, y, c),
                                         device_id_type=MESH).wait_recv()

    out = pl.pallas_call(
        body, name="gather_from_sibling", out_shape=_hbm_like(lands),
        in_specs=[HBM] * na + [SEM, ANY], out_specs=tuple([HBM] * na),
        input_output_aliases={i: i for i in range(na)},
        compiler_params=SIDE_EFFECT)(*lands, state["d2d"], after)
    return dict(state, lands=list(out))


def _gather_from_chip(state, j, afters, last):
    shards, lands = state["shards"], state["lands"]
    na = len(lands)

    def chip_blocks(land_ref):
        x, y, c = _place()
        chip = [(1 - x, y), (x, 1 - y), (1 - x, 1 - y)][j]
        return (x, y, c), land_ref.at[_block_of(*chip, c)], land_ref.at[_block_of(*chip, 1 - c)]

    def forward(*refs):
        land, recv_ici, recv_d2d, fwd_sems = refs[:na], refs[na], refs[na + 1], refs[-1]
        for t in range(na):
            (x, y, c), mine, _ = chip_blocks(land[t])
            pltpu.make_async_remote_copy(src_ref=mine, dst_ref=mine, send_sem=fwd_sems.at[t],
                                         recv_sem=recv_ici.at[3 * t + j], device_id=(x, y, c),
                                         device_id_type=MESH).wait_recv()
            pltpu.make_async_remote_copy(src_ref=mine, dst_ref=mine, send_sem=fwd_sems.at[t],
                                         recv_sem=recv_d2d.at[4 * t + 1 + j], device_id=(x, y, 1 - c),
                                         device_id_type=MESH).start()

    out = pl.pallas_call(
        forward, name="gather_pass_chip_" + str(j),
        out_shape=(*_hbm_like(lands), pltpu.SemaphoreType.DMA((na,))),
        in_specs=[HBM] * na + [SEM, SEM] + [ANY] * len(afters), out_specs=(*[HBM] * na, SEM),
        input_output_aliases={i: i for i in range(na)},
        compiler_params=SIDE_EFFECT)(*lands, state["ici"], state["d2d"], *afters)
    lands, fwd_sems = out[:na], out[na]

    def arrive(*refs):
        land, fwd_sems, recv_d2d = refs[:na], refs[na], refs[na + 1]
        shard, send_sems = refs[na + 2:2 * na + 2], refs[2 * na + 2]
        for t in range(na):
            (x, y, c), mine, theirs = chip_blocks(land[t])
            pltpu.make_async_remote_copy(src_ref=theirs, dst_ref=theirs, send_sem=fwd_sems.at[t],
                                         recv_sem=recv_d2d.at[4 * t + 1 + j], device_id=(x, y, c),
                                         device_id_type=MESH).wait_recv()
            pltpu.make_async_remote_copy(src_ref=mine, dst_ref=mine, send_sem=fwd_sems.at[t],
                                         recv_sem=recv_d2d.at[4 * t + 1 + j], device_id=(x, y, c),
                                         device_id_type=MESH).wait_send()
            for k in range(4 if last else 0):
                pltpu.make_async_remote_copy(
                    src_ref=shard[t], dst_ref=land[t].at[_block_of(x, y, c)], send_sem=send_sems.at[4 * t + k],
                    recv_sem=recv_d2d.at[4 * t], device_id=(x, y, c), device_id_type=MESH).wait_send()

    out = pl.pallas_call(
        arrive, name="gather_take_chip_" + str(j), out_shape=_hbm_like(lands),
        in_specs=[HBM] * na + [SEM, SEM] + [ANY] * na + [SEM], out_specs=tuple([HBM] * na),
        input_output_aliases={i: i for i in range(na)},
        compiler_params=SIDE_EFFECT)(*lands, fwd_sems, state["d2d"], *shards, state["send"])
    return dict(state, lands=list(out))


def _to_sibling(srcs, lands, send_sems, recv_sems):
    x, y, c = _place()
    return [pltpu.make_async_remote_copy(
        src_ref=srcs[t].at[:, 1 - c], dst_ref=lands[t], send_sem=send_sems.at[t], recv_sem=recv_sems.at[t],
        device_id=(x, y, 1 - c), device_id_type=MESH) for t in range(len(srcs))]


def _to_chips(srcs, lands, send_sems, recv_sems):
    x, y, c = _place()
    copies = []
    for k in (1, 2, 3):
        px, py = x ^ (k >> 1), y ^ (k & 1)
        copies += [pltpu.make_async_remote_copy(
            src_ref=srcs[t].at[2 * px + py], dst_ref=lands[t].at[k - 1], send_sem=send_sems.at[3 * t + k - 1],
            recv_sem=recv_sems.at[3 * t + k - 1], device_id=(px, py, c), device_id_type=MESH) for t in range(len(srcs))]
    return copies


def _exchange_start(name, srcs, land_shapes, copies, per_array, after):
    na = len(srcs)
    lands = [_hbm(lax.empty(shp, a.dtype)) for shp, a in zip(land_shapes, srcs)]

    def body(*refs):
        token = refs[-1]
        for cp in copies(refs[:na], refs[na:2 * na], refs[2 * na + 1], refs[2 * na + 2]):
            cp.start()
        token[...] = jnp.zeros_like(token)

    out = pl.pallas_call(
        body, name=name,
        out_shape=(pltpu.SemaphoreType.DMA((na * per_array,)), pltpu.SemaphoreType.DMA((na * per_array,)),
                   *_hbm_like(lands), TOKEN),
        in_specs=[ANY] * na + [HBM] * na + [ANY],
        out_specs=(SEM, SEM, *[HBM] * na, pl.BlockSpec(memory_space=pltpu.VMEM)),
        input_output_aliases={na + i: 2 + i for i in range(na)},
        compiler_params=SIDE_EFFECT)(*srcs, *lands, after)
    return dict(send=out[0], recv=out[1], srcs=list(srcs), lands=out[2:2 + na]), out[-1]


def _exchange_wait(name, state, copies, afters):
    srcs, lands = state["srcs"], state["lands"]
    na = len(srcs)

    def body(*refs):
        for cp in copies(refs[:na], refs[na:2 * na], refs[2 * na], refs[2 * na + 1]):
            cp.wait_send()
            cp.wait_recv()

    out = pl.pallas_call(
        body, name=name,
        out_shape=_hbm_like(lands),
        in_specs=[ANY] * na + [HBM] * na + [SEM, SEM] + [ANY] * len(afters),
        out_specs=tuple([HBM] * na),
        input_output_aliases={na + i: i for i in range(na)},
        compiler_params=SIDE_EFFECT)(*srcs, *lands, state["send"], state["recv"], *afters)
    return out


def _adamw_math(w, g, m, v):
    m = ADAM_B1 * m + (1.0 - ADAM_B1) * g
    v = ADAM_B2 * v + (1.0 - ADAM_B2) * (g * g)
    m_hat = m / (1.0 - ADAM_B1 ** ADAM_STEP)
    v_hat = v / (1.0 - ADAM_B2 ** ADAM_STEP)
    return -ADAM_LR * (m_hat / (jnp.sqrt(v_hat) + ADAM_EPS) + ADAM_WD * w), m, v


def _adamw(own, others, w, m, v, tr, name, after):
    rows, cols = w.shape
    blk = pl.BlockSpec((tr, cols), lambda i: (i, 0))

    def body(own_ref, oth_ref, w_ref, m_ref, v_ref, after_ref, g_ref, d_ref, nm_ref, nv_ref):
        g = own_ref[...]
        for k in range(3):
            g = g + oth_ref[k].astype(F32)
        g_ref[...] = g
        d_ref[...], nm_ref[...], nv_ref[...] = _adamw_math(w_ref[...], g, m_ref[...], v_ref[...])

    out = jax.ShapeDtypeStruct((rows, cols), F32)
    return pl.pallas_call(
        body, name=name, out_shape=(out, out, out, out), grid=(rows // tr,),
        in_specs=[blk, pl.BlockSpec((3, tr, cols), lambda i: (0, i, 0)), blk, blk, blk, ANY],
        out_specs=(blk, blk, blk, blk),
        compiler_params=_params("parallel"))(own, others, w, m, v, after)


def _adamw_small(red, me, params, moments1, moments2):
    n = len(params)

    def body(me_ref, red_ref, *refs):
        ws, ms, vs = refs[:n], refs[n:2 * n], refs[2 * n:3 * n]
        loss_ref = refs[3 * n]
        outs = refs[3 * n + 1:]
        loss_ref[...] = jnp.sum(red_ref[ROW_MISC:ROW_MISC + 1, LOSS_AT:LOSS_AT + LANES], axis=-1, keepdims=True)
        for t, (row, at) in enumerate(SMALL_AT):
            g = red_ref[row:row + 1, at:at + ws[t].shape[1]]
            d, nm, nv = _adamw_math(ws[t][...], g, ms[t][...], vs[t][...])
            for o, val in zip(outs[4 * t:4 * t + 4], (g, d, nm, nv)):
                o[...] = val
        for tap in range(ws[-1].shape[0]):
            row, at = _tap_at(tap)
            g = red_ref[row:row + 1, pl.ds(pl.multiple_of(at + me_ref[0, 0] * LANES, LANES), LANES)]
            d, nm, nv = _adamw_math(ws[-1][tap], g, ms[-1][tap], vs[-1][tap])
            for o, val in zip(outs[4 * (n - 1):], (g, d, nm, nv)):
                o[tap] = val

    vmem = pl.BlockSpec(memory_space=pltpu.VMEM)
    shapes = [jax.ShapeDtypeStruct(w.shape, F32) for w in params for _ in range(4)]
    out = pl.pallas_call(
        body, name="adamw_small", out_shape=(jax.ShapeDtypeStruct((1, 1), F32), *shapes),
        in_specs=[pl.BlockSpec(memory_space=pltpu.SMEM), vmem] + [vmem] * (3 * n),
        out_specs=tuple([vmem] * (1 + 4 * n)))(me, red, *params, *moments1, *moments2)
    return out[0], [list(out[1 + k::4]) for k in range(4)]


def _tables(s, gq, gk, conv_w):
    gq2 = jnp.tile(gq.reshape(1, HEAD), (1, 2))
    gk2 = jnp.tile(gk.reshape(1, HEAD), (1, 2))
    conv_wp = jnp.pad(conv_w, ((0, SUBLANES - conv_w.shape[0]), (0, 0)))
    return _rope_tables(s), gq2, gk2, conv_wp


def _pair_id(q):
    return jnp.array([q, 0], jnp.int32)


def _forward_in(x, g1, shards):
    s = x.shape[0]
    h = _prenorm(x, g1, min(512, s), x)
    z, w_pairs = lax.empty((s, IN_W), F32), lax.empty((N_PAIRS, PAIR_W, D_MODEL), BF16)
    for q in range(N_PAIRS):
        z, w_pairs = _fwd_in_pair(h, shards, z, w_pairs, _pair_id(q), min(512, s), "fwd_in_" + str(q),
                                  own=shards[0] if q == 0 else None)
    return h, z, w_pairs


def _forward_attn(z, rope, gq2, gk2, conv_wp, sinks, after):
    s = z.shape[0]
    qn, k2, v2 = _qk_prep(z, *rope, gq2, gk2, min(256, s), after)
    a, mix, mixt = _attn_fwd(qn, k2, v2, z, conv_wp, sinks)
    return qn, k2, v2, a, mix, mixt


def _forward_out(x, p, target, mix, mixt, w_out, g2, w_pg, b_pg, w_pp, g3):
    s = x.shape[0]
    tm = min(512, s)
    x1, hn2, hn2t = _fwd_out(mix, w_out, x, g2, tm)
    dy, dgp, dt, pt, acc_ple = _ple(hn2, w_pg, b_pg, p, w_pp, g3, x1, target, min(256, s))
    dx1, dx1b, acc_g2 = _gate_bwd(dgp, w_pg, x1, dy, g2, tm)
    gw_out = _mm_grad(mixt, [dx1b], 512, "grad_w_out")
    gw_pg = _mm_grad(hn2t, [dgp], 512, "grad_w_ple_gate")
    gw_pp = _mm_grad(pt, [dt], 512, "grad_w_ple_proj")
    return dx1, dx1b, (gw_out, gw_pg, gw_pp), acc_ple, acc_g2


def _backward_attn(dmix, h, z, qn, k2, v2, a, rope, gq2, gk2, conv_wp, sinks, after):
    dq, dkc, dkp, dvc, dvp, dz, dzt, acc_attn = _attn_bwd(qn, k2, v2, a, z, dmix, conv_wp, sinks, after)
    dz, dzt, acc_qk = _qkv_bwd(z, dz, dzt, dq, dkc, dkp, dvc, dvp, *rope, gq2, gk2)
    return dz, _grad_w_in(dzt, h), acc_attn, acc_qk


def _local_step(x, p, target, g1, shards, gq, gk, sinks, conv_w, w_out, g2, w_pg, b_pg, w_pp, g3):
    rope, gq2, gk2, conv_wp = _tables(x.shape[0], gq, gk, conv_w)
    h, z, w_pairs = _forward_in(x, g1, shards)
    qn, k2, v2, a, mix, mixt = _forward_attn(z, rope, gq2, gk2, conv_wp, sinks, z)
    dx1, dx1b, (gw_out, gw_pg, gw_pp), acc_ple, acc_g2 = _forward_out(
        x, p, target, mix, mixt, w_out, g2, w_pg, b_pg, w_pp, g3)
    dmix = _mm_nt(dx1b, w_out, min(512, x.shape[0]), "out_bwd", dx1b)
    dz, gw_in, acc_attn, acc_qk = _backward_attn(dmix, h, z, qn, k2, v2, a, rope, gq2, gk2, conv_wp, sinks, dmix)
    grad_x, acc_g1 = _in_bwd(dz, w_pairs, x, dx1, g1, min(512, x.shape[0]), dx1)
    return grad_x, (gw_in, gw_out, gw_pg, gw_pp), (acc_g1, acc_g2, acc_ple, acc_qk, acc_attn)


def _by_owner(g):
    return g.reshape((4, 2) + g.shape[1:])


def kernel(x, p, norm_gain, w_in, q_norm_gain, k_norm_gain, attn_sinks, conv_w, w_out, ple_gate_norm_gain, w_ple_gate, b_ple_gate, w_ple_proj, ple_norm_gain, loss_target, m_norm_gain, m_w_in, m_q_norm_gain, m_k_norm_gain, m_attn_sinks, m_conv_w, m_w_out, m_ple_gate_norm_gain, m_w_ple_gate, m_b_ple_gate, m_w_ple_proj, m_ple_norm_gain, v_norm_gain, v_w_in, v_q_norm_gain, v_k_norm_gain, v_attn_sinks, v_conv_w, v_w_out, v_ple_gate_norm_gain, v_w_ple_gate, v_b_ple_gate, v_w_ple_proj, v_ple_norm_gain):
    me = 4 * lax.axis_index("x") + 2 * lax.axis_index("y") + lax.axis_index("c")
    place = jnp.stack([lax.axis_index("c"), 2 * lax.axis_index("x") + lax.axis_index("y")]).astype(jnp.int32)
    xs, ps, target = x[0], p[0, 0], loss_target[0]

    shard_in = w_in[0].T.astype(BF16)
    own_late = [w_out[0].astype(BF16), w_ple_gate[0].astype(BF16), w_ple_proj[0].astype(BF16)]
    with_own = lambda gathered, own: lax.dynamic_update_slice(gathered, own[None], (me,) + (0,) * own.ndim)
    early, started = _gather_start([shard_in, conv_w[0]], shard_in)
    tm = min(512, xs.shape[0])
    h = _prenorm(xs, norm_gain, tm, started)

    z, w_pairs = lax.empty((xs.shape[0], IN_W), F32), lax.empty((N_PAIRS, PAIR_W, D_MODEL), BF16)
    early = _gather_from_sibling(early, h)
    pair_of = lambda flip: jnp.stack([place[1] ^ flip, place[0]])
    z, w_pairs = _fwd_in_pair(h, early["lands"][0], z, w_pairs, pair_of(0), tm, "fwd_in_own", own=shard_in)
    for j, flip in enumerate((2, 1, 3)):
        early = _gather_from_chip(early, j, (z,) if j != 1 else (z, started_late), last=j == 2)
        z, w_pairs = _fwd_in_pair(h, early["lands"][0], z, w_pairs, pair_of(flip), tm, "fwd_in_chip_" + str(j))
        if j == 0:
            late, started_late = _gather_start(own_late, z)
    conv_full = jnp.transpose(with_own(early["lands"][1], conv_w[0]), (1, 0, 2)).reshape(3, ATTN_W)
    rope, gq2, gk2, conv_wp = _tables(xs.shape[0], q_norm_gain[0], k_norm_gain[0], conv_full)
    late, forwarded = _gather_forward(late, z)
    qn, k2, v2, a, mix, mixt = _forward_attn(z, rope, gq2, gk2, conv_wp, attn_sinks, forwarded)
    g_out, g_pg, g_pp = (with_own(g, own) for g, own in zip(_gather_wait(late, mix), own_late))
    w_out_f = g_out.reshape(D_MODEL, D_MODEL)
    w_pg_f = g_pg.reshape(D_MODEL, D_MODEL)
    w_pp_f = jnp.transpose(g_pp, (1, 0, 2)).reshape(PLE_DIM, D_MODEL)

    dx1, dx1b, (gw_out, gw_pg, gw_pp), acc_ple, acc_g2 = _forward_out(
        xs, ps, target, mix, mixt, w_out_f, ple_gate_norm_gain, w_pg_f, b_ple_gate, w_pp_f, ple_norm_gain)

    names = ("w_out", "w_ple_gate", "w_ple_proj")
    gw_pp_t = jnp.transpose(gw_pp.reshape(PLE_DIM, N_DEV, PLE_DIM), (1, 0, 2))
    grads = [_by_owner(gw_out.reshape(N_DEV, D_MODEL // N_DEV, D_MODEL)),
             _by_owner(gw_pg.reshape(N_DEV, D_MODEL // N_DEV, D_MODEL)), _by_owner(gw_pp_t)]
    pairs, paired = _exchange_start("pair_start", grads, [(4,) + g.shape[2:] for g in grads], _to_sibling, 1, dx1b)
    dmix = _mm_nt(dx1b, w_out_f, tm, "out_bwd", paired)
    from_sibling = _exchange_wait("pair_wait", pairs, _to_sibling, (dmix,))
    sums = [_pair_sum(g, r, place, 256, "pair_sum_" + nm) for g, r, nm in zip(pairs["srcs"], from_sibling, names)]
    chips, sent = _exchange_start("chip_start", [pb for pb, _ in sums], [(3,) + pb.shape[1:] for pb, _ in sums],
                                  _to_chips, 3, sums[-1][1])

    dz, gw_in, acc_attn, acc_qk = _backward_attn(
        dmix, h, z, qn, k2, v2, a, rope, gq2, gk2, conv_wp, attn_sinks, sent)

    gw_in_t = [_by_owner(gw_in)]
    pairs_in, paired_in = _exchange_start("pair_start_w_in", gw_in_t, [(4,) + gw_in_t[0].shape[2:]], _to_sibling, 1,
                                          gw_in)
    from_chips = _exchange_wait("chip_wait", chips, _to_chips, (gw_in,))
    big = {}
    for (_, own), oth, w, m, v, nm in zip(sums, from_chips, (w_out, w_ple_gate, w_ple_proj),
                                          (m_w_out, m_w_ple_gate, m_w_ple_proj),
                                          (v_w_out, v_w_ple_gate, v_w_ple_proj), names):
        big[nm] = [t[None] for t in _adamw(own, oth, w[0], m[0], v[0], 256, "adamw_" + nm, paired_in)]

    (from_sibling_in,) = _exchange_wait("pair_wait_w_in", pairs_in, _to_sibling, [big[nm][0] for nm in names])
    pb_in, own_in = _pair_sum(pairs_in["srcs"][0], from_sibling_in, place, SHARD_IN // 2, "pair_sum_w_in")
    chips_in, sent_in = _exchange_start("chip_start_w_in", [pb_in], [(3,) + pb_in.shape[1:]], _to_chips, 3, own_in)
    grad_x, acc_g1 = _in_bwd(dz, w_pairs, xs, dx1, norm_gain, tm, sent_in)
    (from_chips_in,) = _exchange_wait("chip_wait_w_in", chips_in, _to_chips, (grad_x,))
    big["w_in"] = [t.T[None] for t in _adamw(own_in, from_chips_in, w_in[0].T, m_w_in[0].T, v_w_in[0].T, SHARD_IN // 4,
                                             "adamw_w_in", grad_x)]

    red = _reduce_small(acc_g1, acc_g2, acc_ple, acc_qk, acc_attn)
    small = [norm_gain, ple_gate_norm_gain, b_ple_gate, ple_norm_gain, q_norm_gain, k_norm_gain, attn_sinks]
    small_m = [m_norm_gain, m_ple_gate_norm_gain, m_b_ple_gate, m_ple_norm_gain, m_q_norm_gain, m_k_norm_gain,
               m_attn_sinks]
    small_v = [v_norm_gain, v_ple_gate_norm_gain, v_b_ple_gate, v_ple_norm_gain, v_q_norm_gain, v_k_norm_gain,
               v_attn_sinks]
    taps_first = lambda t: jnp.transpose(t, (1, 0, 2))
    loss, kinds = _adamw_small(red, me.reshape(1, 1).astype(jnp.int32), small + [taps_first(conv_w)],
                               small_m + [taps_first(m_conv_w)], small_v + [taps_first(v_conv_w)])

    def order(k):
        sm = kinds[k]
        return [sm[0], big["w_in"][k], sm[4], sm[5], sm[6], taps_first(sm[7]), big["w_out"][k], sm[1],
                big["w_ple_gate"][k], sm[2], big["w_ple_proj"][k], sm[3]]

    return (loss[0, 0], grad_x[None], *order(0), *order(1), *order(2), *order(3))
```

```python
import jax
import jax.numpy as jnp
from jax import lax
from jax.experimental import pallas as pl
from jax.experimental.pallas import tpu as pltpu

F32, BF16 = jnp.float32, jnp.bfloat16

D_MODEL = 2048
PLE_DIM = 256
ATTN_W = 1024
HEAD = 64
N_Q_HEADS = 16
KV_W = 256
QKV_W = ATTN_W + 2 * KV_W
REST_W = 5 * 1024
IN_W = QKV_W + REST_W
GATE_A0, CONV_B0, CONV_C0, CONV_H0, GATE_C0 = (QKV_W + 1024 * t for t in range(5))
K2_W = 4 * 128
ROT = 16
ROPE_THETA = 500000.0
EPS = 1e-6
NEG_INF = -1e30
BLK = 128
LANES = 128
SUBLANES = 8
N_DEV = 8
SHARD_IN = IN_W // N_DEV
PAIR_W = 2 * SHARD_IN
N_PAIRS = IN_W // PAIR_W
SLAB_ROWS = 8
SUB_ROWS = 128
V7X_VMEM_LIMIT = 52 * 1024 * 1024

ADAM_LR, ADAM_B1, ADAM_B2, ADAM_EPS, ADAM_WD, ADAM_STEP = 0.001, 0.9, 0.999, 1e-08, 0.01, 10
MESH = pl.DeviceIdType.MESH


def _params(*semantics):
    return pltpu.CompilerParams(dimension_semantics=semantics, vmem_limit_bytes=V7X_VMEM_LIMIT)


ANY = pl.BlockSpec(memory_space=pl.ANY)


def _resident(shape):
    return pl.BlockSpec(shape, lambda *_: (0,) * len(shape), pipeline_mode=pl.Buffered(1))


def _dot(a, b):
    return jnp.dot(a, b, preferred_element_type=F32)


def _dot_nt(a, b):
    return lax.dot_general(a, b, (((1,), (1,)), ((), ())), preferred_element_type=F32)


def _rms(xf):
    r = lax.rsqrt(jnp.mean(xf * xf, axis=-1, keepdims=True) + EPS)
    return xf * r, r


def _rms_bwd(dxn, xn, r):
    return r * (dxn - xn * jnp.mean(dxn * xn, axis=-1, keepdims=True))


def _sig(g):
    return jax.nn.sigmoid(g)


def _dsilu(g, sg):
    return sg * (1.0 + g * (1.0 - sg))


def _low_half(shape):
    return lax.broadcasted_iota(jnp.int32, shape, len(shape) - 1) < HEAD


def _half_sums(v):
    lo = _low_half(v.shape)
    s_lo = jnp.sum(jnp.where(lo, v, 0.0), axis=-1, keepdims=True)
    s_hi = jnp.sum(jnp.where(lo, 0.0, v), axis=-1, keepdims=True)
    return jnp.where(lo, s_lo, s_hi)


def _rope(v, a, bm, bp):
    return v * a + pltpu.roll(v, LANES - ROT // 2, 1) * bm + pltpu.roll(v, ROT // 2, 1) * bp


def _rope_t(dy, a, bm, bp):
    return dy * a + pltpu.roll(dy * bm, ROT // 2, 1) + pltpu.roll(dy * bp, LANES - ROT // 2, 1)


def _dup_halves(v):
    lo = _low_half(v.shape)
    a = jnp.where(lo, v, 0.0)
    b = jnp.where(lo, 0.0, v)
    return a + pltpu.roll(a, HEAD, 1), b + pltpu.roll(b, HEAD, 1)


def _rope_tables(s):
    half = ROT // 2
    lane = lax.broadcasted_iota(jnp.int32, (s, LANES), 1) % HEAD
    pos = lax.broadcasted_iota(jnp.int32, (half, s), 1).astype(F32)
    freq = lax.broadcasted_iota(jnp.int32, (half, s), 0).astype(F32)
    ang = pos * jnp.power(jnp.float32(ROPE_THETA), -freq * 2.0 / ROT)
    cos, sin = lax.optimization_barrier((jnp.cos(ang), jnp.sin(ang)))
    cos, sin = (jnp.tile(t.T, (1, LANES // half)) for t in (cos, sin))
    a = jnp.where(lane < ROT, cos, 1.0)
    bm = jnp.where(lane < half, -sin, 0.0)
    bp = jnp.where((lane >= half) & (lane < ROT), sin, 0.0)
    return a, bm, bp


def _prenorm(x, g1, tm, after):
    s = x.shape[0]

    def body(x_ref, g_ref, after_ref, h_ref):
        xn, _ = _rms(x_ref[...])
        h_ref[...] = (xn * g_ref[...]).astype(BF16)

    return pl.pallas_call(
        body, name="prenorm",
        out_shape=jax.ShapeDtypeStruct((s, D_MODEL), BF16),
        grid=(s // tm,),
        in_specs=[pl.BlockSpec((tm, D_MODEL), lambda i: (i, 0)), pl.BlockSpec((1, D_MODEL), lambda i: (0, 0)), ANY],
        out_specs=pl.BlockSpec((tm, D_MODEL), lambda i: (i, 0)),
        compiler_params=_params("parallel"))(x, g1, after)


def _fwd_in_pair(h, shards, z, w_pairs, pair, tm, name, own=None):
    s = h.shape[0]

    def body(pair_ref, h_ref, lo_ref, hi_ref, z_in, wp_in, z_ref, wp_ref):
        @pl.when(pl.program_id(0) == 0)
        def _():
            wp_ref[0, 0:SHARD_IN, :] = lo_ref[0]
            wp_ref[0, SHARD_IN:PAIR_W, :] = hi_ref[0]

        z_ref[...] = _dot_nt(h_ref[...], wp_ref[0])

    def body_own(pair_ref, h_ref, own_ref, other_ref, z_in, wp_in, z_ref, wp_ref):
        @pl.when(pl.program_id(0) == 0)
        def _():
            first = pl.multiple_of(pair_ref[1] * SHARD_IN, SHARD_IN)
            wp_ref[0, pl.ds(first, SHARD_IN), :] = own_ref[...]
            wp_ref[0, pl.ds(SHARD_IN - first, SHARD_IN), :] = other_ref[0]

        z_ref[...] = _dot_nt(h_ref[...], wp_ref[0])

    if own is None:
        blocks = [pl.BlockSpec((1, SHARD_IN, D_MODEL), lambda i, p: (2 * p[0], 0, 0)),
                  pl.BlockSpec((1, SHARD_IN, D_MODEL), lambda i, p: (2 * p[0] + 1, 0, 0))]
        operands = (shards, shards)
    else:
        blocks = [pl.BlockSpec((SHARD_IN, D_MODEL), lambda i, p: (0, 0)),
                  pl.BlockSpec((1, SHARD_IN, D_MODEL), lambda i, p: (2 * p[0] + 1 - p[1], 0, 0))]
        operands = (own, shards)
    grid_spec = pltpu.PrefetchScalarGridSpec(
        num_scalar_prefetch=1, grid=(s // tm,),
        in_specs=[pl.BlockSpec((tm, D_MODEL), lambda i, p: (i, 0)), *blocks, ANY, ANY],
        out_specs=(pl.BlockSpec((tm, PAIR_W), lambda i, p: (i, p[0])),
                   pl.BlockSpec((1, PAIR_W, D_MODEL), lambda i, p: (p[0], 0, 0))))
    return pl.pallas_call(
        body if own is None else body_own, name=name, grid_spec=grid_spec,
        out_shape=(jax.ShapeDtypeStruct(z.shape, z.dtype), jax.ShapeDtypeStruct(w_pairs.shape, w_pairs.dtype)),
        input_output_aliases={4: 0, 5: 1},
        compiler_params=_params("arbitrary"))(pair, h, *operands, z, w_pairs)


def _qk_prep(z, ra, rbm, rbp, gq2, gk2, tm, after):
    s = z.shape[0]

    def body(z_ref, a_ref, bm_ref, bp_ref, gq_ref, gk_ref, after_ref, q_ref, k2_ref, v2_ref):
        a, bm, bp = a_ref[...], bm_ref[...], bp_ref[...]
        for r in range(ATTN_W // LANES):
            x = z_ref[:, LANES * r:LANES * (r + 1)]
            rr = lax.rsqrt(_half_sums(x * x) * (1.0 / HEAD) + EPS)
            q_ref[:, LANES * r:LANES * (r + 1)] = _rope(x * rr * gq_ref[...], a, bm, bp).astype(BF16)
        for m in range(KV_W // LANES):
            x = z_ref[:, ATTN_W + LANES * m:ATTN_W + LANES * (m + 1)]
            rr = lax.rsqrt(_half_sums(x * x) * (1.0 / HEAD) + EPS)
            k_lo, k_hi = _dup_halves(_rope(x * rr * gk_ref[...], a, bm, bp))
            k2_ref[:, 2 * LANES * m:2 * LANES * m + LANES] = k_lo.astype(BF16)
            k2_ref[:, 2 * LANES * m + LANES:2 * LANES * (m + 1)] = k_hi.astype(BF16)
            v_lo, v_hi = _dup_halves(z_ref[:, ATTN_W + KV_W + LANES * m:ATTN_W + KV_W + LANES * (m + 1)])
            v2_ref[:, 2 * LANES * m:2 * LANES * m + LANES] = v_lo.astype(BF16)
            v2_ref[:, 2 * LANES * m + LANES:2 * LANES * (m + 1)] = v_hi.astype(BF16)

    row = lambda w: pl.BlockSpec((tm, w), lambda i: (i, 0))
    one = pl.BlockSpec((1, LANES), lambda i: (0, 0))
    return pl.pallas_call(
        body, name="qk_prep",
        out_shape=(jax.ShapeDtypeStruct((s, ATTN_W), BF16), jax.ShapeDtypeStruct((s, K2_W), BF16),
                   jax.ShapeDtypeStruct((s, K2_W), BF16)),
        grid=(s // tm,),
        in_specs=[row(PAIR_W), row(LANES), row(LANES), row(LANES), one, one, ANY],
        out_specs=(row(ATTN_W), row(K2_W), row(K2_W)),
        compiler_params=_params("parallel"))(z, ra, rbm, rbp, gq2, gk2, after)


GROUP = 4


def _window_mask(n):
    row = lax.broadcasted_iota(jnp.int32, (GROUP * BLK, 2 * BLK), 0) % BLK
    col = lax.broadcasted_iota(jnp.int32, (GROUP * BLK, 2 * BLK), 1)
    return (col > row) & (col <= row + BLK) & ((col >= BLK) | (n > 0))


def _stack_heads(pairs, zero):
    lo = _low_half(pairs[0].shape)
    parts = []
    for v in pairs:
        parts += [jnp.where(lo, v, zero), jnp.where(lo, zero, v)]
    return jnp.concatenate(parts, axis=0)


def _unstack_heads(v4):
    lo = _low_half((BLK, LANES))
    return [jnp.where(lo, v4[2 * i * BLK:(2 * i + 1) * BLK], v4[(2 * i + 1) * BLK:(2 * i + 2) * BLK]) for i in range(2)]


def _group_sinks(sink_ref, kvh):
    slot = lax.broadcasted_iota(jnp.int32, (GROUP * BLK, 1), 0) // BLK
    col = jnp.zeros((GROUP * BLK, 1), F32)
    for i in range(GROUP):
        col = jnp.where(slot == i, sink_ref[0, GROUP * kvh + i], col)
    return col, slot


def _head_probs(qm, kw, valid, sink):
    sc = jnp.where(valid, _dot_nt(qm, kw) * (HEAD ** -0.5), NEG_INF)
    mx = jnp.maximum(jnp.max(sc, axis=-1, keepdims=True), sink)
    ex = jnp.exp(sc - mx)
    den = jnp.sum(ex, axis=-1, keepdims=True) + jnp.exp(sink - mx)
    return ex / den, mx, den


def _cols(start, width=ATTN_W):
    return slice(start, start + width)


def _conv_fwd(z_ref, zp_ref, cw_ref, ext_ref, n):
    u = z_ref[:, _cols(CONV_C0)] * z_ref[:, _cols(CONV_H0)]
    pu = zp_ref[:, _cols(CONV_C0)] * zp_ref[:, _cols(CONV_H0)]
    ext_ref[0:SUBLANES, :] = jnp.where(n > 0, pu, 0.0)
    ext_ref[SUBLANES:SUBLANES + BLK, :] = u
    um1 = ext_ref[SUBLANES - 1:SUBLANES - 1 + BLK, :]
    um2 = ext_ref[SUBLANES - 2:SUBLANES - 2 + BLK, :]
    cv = cw_ref[0:1, :] * um2 + cw_ref[1:2, :] * um1 + cw_ref[2:3, :] * u
    return u, um1, um2, cv


def _prev_rows(n):
    return (jnp.maximum(n * (BLK // SUBLANES) - 1, 0), 0)


def _attn_fwd(qn, k2, v2, z, conv_wp, sinks):
    s = qn.shape[0]
    nb = s // BLK

    def body(sink_ref, q_ref, kc_ref, kp_ref, vc_ref, vp_ref, z_ref, zp_ref, cw_ref, a_ref, mix_ref, mixt_ref,
             ext_ref):
        n = pl.program_id(0)
        valid = _window_mask(n)
        for kvh in range(K2_W // LANES):
            cols = slice(LANES * kvh, LANES * (kvh + 1))
            kw = jnp.concatenate([kp_ref[:, cols], kc_ref[:, cols]], axis=0)
            vw = jnp.concatenate([vp_ref[:, cols], vc_ref[:, cols]], axis=0)
            blocks = [slice(LANES * r, LANES * (r + 1)) for r in (2 * kvh, 2 * kvh + 1)]
            q4 = _stack_heads([q_ref[:, rc] for rc in blocks], jnp.zeros((BLK, LANES), BF16))
            p, _, _ = _head_probs(q4, kw, valid, _group_sinks(sink_ref, kvh)[0])
            for rc, a in zip(blocks, _unstack_heads(_dot(p.astype(BF16), vw))):
                a_ref[:, rc] = a
                g = z_ref[:, _cols(GATE_A0 + rc.start, LANES)]
                mix_ref[:, rc] = (a * (g * _sig(g))).astype(BF16)
        _, _, _, cv = _conv_fwd(z_ref, zp_ref, cw_ref, ext_ref, n)
        gc = z_ref[:, _cols(GATE_C0)]
        mix_ref[:, ATTN_W:D_MODEL] = (z_ref[:, _cols(CONV_B0)] * cv * (gc * _sig(gc))).astype(BF16)
        mixt_ref[...] = mix_ref[...].T

    cur = lambda w: pl.BlockSpec((BLK, w), lambda n: (n, 0))
    prev = lambda w: pl.BlockSpec((BLK, w), lambda n: (jnp.maximum(n - 1, 0), 0))
    return pl.pallas_call(
        body, name="attn_fwd",
        out_shape=(jax.ShapeDtypeStruct((s, ATTN_W), F32), jax.ShapeDtypeStruct((s, D_MODEL), BF16),
                   jax.ShapeDtypeStruct((D_MODEL, s), BF16)),
        grid=(nb,),
        in_specs=[pl.BlockSpec(memory_space=pltpu.SMEM),
                  cur(ATTN_W), cur(K2_W), prev(K2_W), cur(K2_W), prev(K2_W), cur(IN_W),
                  pl.BlockSpec((SUBLANES, IN_W), _prev_rows),
                  pl.BlockSpec((SUBLANES, ATTN_W), lambda n: (0, 0))],
        out_specs=(cur(ATTN_W), cur(D_MODEL), pl.BlockSpec((D_MODEL, BLK), lambda n: (0, n))),
        scratch_shapes=[pltpu.VMEM((BLK + 2 * SUBLANES, ATTN_W), F32)],
        compiler_params=_params("parallel"))(sinks, qn, k2, k2, v2, v2, z, z, conv_wp)


def _fwd_out(mix, w_out, x, g2, tm):
    s = x.shape[0]

    def body(m_ref, w_ref, x_ref, g_ref, x1_ref, h_ref, ht_ref):
        x1 = x_ref[...] + _dot(m_ref[...], w_ref[...])
        x1_ref[...] = x1
        xn, _ = _rms(x1)
        h = (xn * g_ref[...]).astype(BF16)
        h_ref[...] = h
        ht_ref[...] = h.T

    row = pl.BlockSpec((tm, D_MODEL), lambda i: (i, 0))
    return pl.pallas_call(
        body, name="fwd_out",
        out_shape=(jax.ShapeDtypeStruct((s, D_MODEL), F32), jax.ShapeDtypeStruct((s, D_MODEL), BF16),
                   jax.ShapeDtypeStruct((D_MODEL, s), BF16)),
        grid=(s // tm,),
        in_specs=[row, _resident((D_MODEL, D_MODEL)), row, pl.BlockSpec((1, D_MODEL), lambda i: (0, 0))],
        out_specs=(row, row, pl.BlockSpec((D_MODEL, tm), lambda i: (0, i))),
        compiler_params=_params("parallel"))(mix, w_out, x, g2)


def _ple(hn2, w_pg, b_pg, p, w_pp, g3, x1, target, tm):
    s = x1.shape[0]

    def body(h_ref, wg_ref, b_ref, p_ref, wp_ref, g3_ref, x1_ref, t_ref, dy_ref, dgp_ref, dt_ref, pt_ref, acc_ref):
        gate = _sig(_dot(h_ref[...], wg_ref[...]) + b_ref[...])
        pb = p_ref[...].astype(BF16)
        pt_ref[...] = pb.T
        t = _dot(pb, wp_ref[...])
        tn, r3 = _rms(t)
        e = tn * g3_ref[...]
        diff = x1_ref[...] + gate * e - t_ref[...]
        dy = diff * (1.0 / D_MODEL)
        dy_ref[...] = dy
        dgp = dy * e * (gate * (1.0 - gate))
        dgp_ref[...] = dgp.astype(BF16)
        de = dy * gate
        dt_ref[...] = _rms_bwd(de * g3_ref[...], tn, r3).astype(BF16)

        @pl.when(pl.program_id(0) == 0)
        def _():
            acc_ref[...] = jnp.zeros_like(acc_ref)

        acc_ref[0:1, :] += jnp.sum(dgp, axis=0, keepdims=True)
        acc_ref[1:2, :] += jnp.sum(de * tn, axis=0, keepdims=True)
        acc_ref[2:3, :] += jnp.sum(diff * diff, axis=0, keepdims=True) * (0.5 / D_MODEL)

    row = pl.BlockSpec((tm, D_MODEL), lambda i: (i, 0))
    vec = pl.BlockSpec((1, D_MODEL), lambda i: (0, 0))
    return pl.pallas_call(
        body, name="ple",
        out_shape=(jax.ShapeDtypeStruct((s, D_MODEL), F32), jax.ShapeDtypeStruct((s, D_MODEL), BF16),
                   jax.ShapeDtypeStruct((s, D_MODEL), BF16), jax.ShapeDtypeStruct((PLE_DIM, s), BF16),
                   jax.ShapeDtypeStruct((SUBLANES, D_MODEL), F32)),
        grid=(s // tm,),
        in_specs=[row, _resident((D_MODEL, D_MODEL)), vec, pl.BlockSpec((tm, PLE_DIM), lambda i: (i, 0)),
                  _resident((PLE_DIM, D_MODEL)), vec, row, row],
        out_specs=(row, row, row, pl.BlockSpec((PLE_DIM, tm), lambda i: (0, i)),
                   pl.BlockSpec((SUBLANES, D_MODEL), lambda i: (0, 0))),
        compiler_params=_params("arbitrary"))(hn2, w_pg, b_pg, p, w_pp, g3, x1, target)


def _gate_bwd(dgp, w_pg, x1, dy, g2, tm):
    s = x1.shape[0]

    def body(d_ref, w_ref, x1_ref, dy_ref, g_ref, dx_ref, dxb_ref, acc_ref):
        dh = _dot_nt(d_ref[...], w_ref[...])
        xn, r = _rms(x1_ref[...])
        dx1 = dy_ref[...] + _rms_bwd(dh * g_ref[...], xn, r)
        dx_ref[...] = dx1
        dxb_ref[...] = dx1.astype(BF16)

        @pl.when(pl.program_id(0) == 0)
        def _():
            acc_ref[...] = jnp.zeros_like(acc_ref)

        acc_ref[0:1, :] += jnp.sum(dh * xn, axis=0, keepdims=True)

    row = pl.BlockSpec((tm, D_MODEL), lambda i: (i, 0))
    return pl.pallas_call(
        body, name="gate_bwd",
        out_shape=(jax.ShapeDtypeStruct((s, D_MODEL), F32), jax.ShapeDtypeStruct((s, D_MODEL), BF16),
                   jax.ShapeDtypeStruct((SUBLANES, D_MODEL), F32)),
        grid=(s // tm,),
        in_specs=[row, _resident((D_MODEL, D_MODEL)), row, row, pl.BlockSpec((1, D_MODEL), lambda i: (0, 0))],
        out_specs=(row, row, pl.BlockSpec((SUBLANES, D_MODEL), lambda i: (0, 0))),
        compiler_params=_params("arbitrary"))(dgp, w_pg, x1, dy, g2)


def _mm_nt(a, b, tm, name, after):
    m, k = a.shape
    n = b.shape[0]

    def body(a_ref, b_ref, after_ref, o_ref):
        o_ref[...] = _dot_nt(a_ref[...], b_ref[...])

    return pl.pallas_call(
        body, name=name,
        out_shape=jax.ShapeDtypeStruct((m, n), F32),
        grid=(m // tm,),
        in_specs=[pl.BlockSpec((tm, k), lambda i: (i, 0)), _resident((n, k)), ANY],
        out_specs=pl.BlockSpec((tm, n), lambda i: (i, 0)),
        compiler_params=_params("parallel"))(a, b, after)


def _attn_bwd(qn, k2, v2, a, z, dmix, conv_wp, sinks, after):
    s = qn.shape[0]
    nb = s // BLK

    def body(sink_ref, q_ref, kc_ref, kp_ref, vc_ref, vp_ref, a_ref, z_ref, zp_ref, zn_ref, dm_ref, dmn_ref,
             cw_ref, after_ref, dq_ref, dkc_ref, dkp_ref, dvc_ref, dvp_ref, dz_ref, dzt_ref, acc_ref, ext_ref):
        n = pl.program_id(0)
        valid = _window_mask(n)
        lane = lax.broadcasted_iota(jnp.int32, (1, ATTN_W), 1)

        @pl.when(n == 0)
        def _():
            acc_ref[...] = jnp.zeros_like(acc_ref)

        dz_ref[:, 0:QKV_W] = jnp.zeros((BLK, QKV_W), BF16)
        dsink = jnp.zeros((1, ATTN_W), F32)
        for kvh in range(K2_W // LANES):
            cols = slice(LANES * kvh, LANES * (kvh + 1))
            kw = jnp.concatenate([kp_ref[:, cols], kc_ref[:, cols]], axis=0)
            vw = jnp.concatenate([vp_ref[:, cols], vc_ref[:, cols]], axis=0)
            blocks = [slice(LANES * r, LANES * (r + 1)) for r in (2 * kvh, 2 * kvh + 1)]
            das, avs = [], []
            for rc in blocks:
                g = z_ref[:, _cols(GATE_A0 + rc.start, LANES)]
                sg = _sig(g)
                dm = dm_ref[:, rc]
                av = a_ref[:, rc]
                das.append(dm * (g * sg))
                avs += [av, av]
                dz_ref[:, _cols(GATE_A0 + rc.start, LANES)] = (dm * av * _dsilu(g, sg)).astype(BF16)
            q4 = _stack_heads([q_ref[:, rc] for rc in blocks], jnp.zeros((BLK, LANES), BF16))
            sink, slot = _group_sinks(sink_ref, kvh)
            p, mx, den = _head_probs(q4, kw, valid, sink)
            do4 = _stack_heads(das, 0.0)
            delta = jnp.sum(do4 * jnp.concatenate(avs, axis=0), axis=-1, keepdims=True)
            dob = do4.astype(BF16)
            ds = p * (_dot_nt(dob, vw) - delta) * (HEAD ** -0.5)
            for rc, dq in zip(blocks, _unstack_heads(_dot(ds.astype(BF16), kw))):
                dq_ref[:, rc] = dq
            dk2 = _dot(ds.T.astype(BF16), q4)
            dv2 = _dot(p.T.astype(BF16), dob)
            dkp_ref[:, cols] = dk2[0:BLK]
            dkc_ref[:, cols] = dk2[BLK:2 * BLK]
            dvp_ref[:, cols] = dv2[0:BLK]
            dvc_ref[:, cols] = dv2[BLK:2 * BLK]
            dsk = jnp.exp(sink - mx) / den * delta
            for i in range(GROUP):
                dsink = dsink - jnp.where(lane == GROUP * kvh + i,
                                          jnp.sum(jnp.where(slot == i, dsk, 0.0), axis=0, keepdims=True), 0.0)
        acc_ref[0:1, :] += dsink

        u, um1, um2, cv = _conv_fwd(z_ref, zp_ref, cw_ref, ext_ref, n)
        cb = z_ref[:, _cols(CONV_B0)]
        gc = z_ref[:, _cols(GATE_C0)]
        sgc = _sig(gc)
        dmc = dm_ref[:, ATTN_W:D_MODEL]
        t = dmc * (gc * sgc)
        dcv = t * cb
        dz_ref[:, _cols(CONV_B0)] = (t * cv).astype(BF16)
        dz_ref[:, _cols(GATE_C0)] = (dmc * cb * cv * _dsilu(gc, sgc)).astype(BF16)
        gcn = zn_ref[:, _cols(GATE_C0)]
        dcvn = dmn_ref[:, ATTN_W:D_MODEL] * (gcn * _sig(gcn)) * zn_ref[:, _cols(CONV_B0)]
        ext_ref[0:BLK, :] = dcv
        ext_ref[BLK:BLK + SUBLANES, :] = jnp.where(n < nb - 1, dcvn, 0.0)
        du = (cw_ref[2:3, :] * dcv + cw_ref[1:2, :] * ext_ref[1:1 + BLK, :]
              + cw_ref[0:1, :] * ext_ref[2:2 + BLK, :])
        dz_ref[:, _cols(CONV_C0)] = (du * z_ref[:, _cols(CONV_H0)]).astype(BF16)
        dz_ref[:, _cols(CONV_H0)] = (du * z_ref[:, _cols(CONV_C0)]).astype(BF16)
        acc_ref[1:2, :] += jnp.sum(dcv * um2, axis=0, keepdims=True)
        acc_ref[2:3, :] += jnp.sum(dcv * um1, axis=0, keepdims=True)
        acc_ref[3:4, :] += jnp.sum(dcv * u, axis=0, keepdims=True)
        dzt_ref[...] = dz_ref[...].T

    cur = lambda w: pl.BlockSpec((BLK, w), lambda n: (n, 0))
    prev = lambda w: pl.BlockSpec((BLK, w), lambda n: (jnp.maximum(n - 1, 0), 0))
    nxt = lambda w: pl.BlockSpec(
        (SUBLANES, w), lambda n: (jnp.minimum((n + 1) * (BLK // SUBLANES), nb * (BLK // SUBLANES) - 1), 0))
    f32 = lambda w: jax.ShapeDtypeStruct((s, w), F32)
    return pl.pallas_call(
        body, name="attn_bwd",
        out_shape=(f32(ATTN_W), f32(K2_W), f32(K2_W), f32(K2_W), f32(K2_W),
                   jax.ShapeDtypeStruct((s, IN_W), BF16), jax.ShapeDtypeStruct((IN_W, s), BF16),
                   jax.ShapeDtypeStruct((SUBLANES, ATTN_W), F32)),
        grid=(nb,),
        in_specs=[pl.BlockSpec(memory_space=pltpu.SMEM),
                  cur(ATTN_W), cur(K2_W), prev(K2_W), cur(K2_W), prev(K2_W), cur(ATTN_W), cur(IN_W),
                  pl.BlockSpec((SUBLANES, IN_W), _prev_rows), nxt(IN_W), cur(D_MODEL), nxt(D_MODEL),
                  pl.BlockSpec((SUBLANES, ATTN_W), lambda n: (0, 0)), ANY],
        out_specs=(cur(ATTN_W), cur(K2_W), cur(K2_W), cur(K2_W), cur(K2_W), cur(IN_W),
                   pl.BlockSpec((IN_W, BLK), lambda n: (0, n)), pl.BlockSpec((SUBLANES, ATTN_W), lambda n: (0, 0))),
        scratch_shapes=[pltpu.VMEM((BLK + 2 * SUBLANES, ATTN_W), F32)],
        compiler_params=_params("arbitrary"))(sinks, qn, k2, k2, v2, v2, a, z, z, z, dmix, dmix, conv_wp, after)


def _qkv_bwd(z, dz, dzt, dq, dkc, dkp, dvc, dvp, ra, rbm, rbp, gq2, gk2):
    s = z.shape[0]
    nb = s // BLK

    def body(z_ref, dz_in, dzt_in, dq_ref, dkc_ref, dkp_ref, dvc_ref, dvp_ref, a_ref, bm_ref, bp_ref, gq_ref, gk_ref,
             dz_ref, dzt_ref, acc_ref):
        n = pl.program_id(0)
        a, bm, bp = a_ref[...], bm_ref[...], bp_ref[...]
        lo = _low_half((BLK, LANES))
        last = n == nb - 1

        @pl.when(n == 0)
        def _():
            acc_ref[...] = jnp.zeros_like(acc_ref)

        def norm_bwd(x, dy, gain):
            rr = lax.rsqrt(_half_sums(x * x) * (1.0 / HEAD) + EPS)
            xh = x * rr
            dxg = _rope_t(dy, a, bm, bp)
            dxh = dxg * gain
            dx = rr * (dxh - xh * (_half_sums(dxh * xh) * (1.0 / HEAD)))
            return dx, jnp.sum(dxg * xh, axis=0, keepdims=True)

        def folded(cur_ref, prev_ref, m):
            parts = []
            for h in (2 * m, 2 * m + 1):
                v = cur_ref[:, LANES * h:LANES * (h + 1)] + jnp.where(
                    last, 0.0, prev_ref[:, LANES * h:LANES * (h + 1)])
                parts.append(v + pltpu.roll(v, HEAD, 1))
            return jnp.where(lo, parts[0], parts[1])

        gq_acc = jnp.zeros((1, LANES), F32)
        for r in range(ATTN_W // LANES):
            rc = slice(LANES * r, LANES * (r + 1))
            dx, gg = norm_bwd(z_ref[:, rc], dq_ref[:, rc], gq_ref[...])
            dz_ref[:, rc] = dx.astype(BF16)
            gq_acc = gq_acc + gg
        acc_ref[0:1, :] += gq_acc
        gk_acc = jnp.zeros((1, LANES), F32)
        for m in range(KV_W // LANES):
            kc = slice(ATTN_W + LANES * m, ATTN_W + LANES * (m + 1))
            dx, gg = norm_bwd(z_ref[:, kc], folded(dkc_ref, dkp_ref, m), gk_ref[...])
            dz_ref[:, kc] = dx.astype(BF16)
            gk_acc = gk_acc + gg
            vc = slice(ATTN_W + KV_W + LANES * m, ATTN_W + KV_W + LANES * (m + 1))
            dz_ref[:, vc] = folded(dvc_ref, dvp_ref, m).astype(BF16)
        acc_ref[1:2, :] += gk_acc
        dzt_ref[...] = dz_ref[...].T

    cur = lambda w: pl.BlockSpec((BLK, w), lambda n: (n, 0))
    nxt = lambda w: pl.BlockSpec((BLK, w), lambda n: (jnp.minimum(n + 1, nb - 1), 0))
    one = pl.BlockSpec((1, LANES), lambda n: (0, 0))
    return pl.pallas_call(
        body, name="qkv_bwd",
        out_shape=(jax.ShapeDtypeStruct(dz.shape, dz.dtype), jax.ShapeDtypeStruct(dzt.shape, dzt.dtype),
                   jax.ShapeDtypeStruct((SUBLANES, LANES), F32)),
        grid=(nb,),
        in_specs=[cur(PAIR_W), ANY, ANY, cur(ATTN_W), cur(K2_W), nxt(K2_W), cur(K2_W), nxt(K2_W),
                  cur(LANES), cur(LANES), cur(LANES), one, one],
        out_specs=(cur(QKV_W), pl.BlockSpec((QKV_W, BLK), lambda n: (0, n)),
                   pl.BlockSpec((SUBLANES, LANES), lambda n: (0, 0))),
        input_output_aliases={1: 0, 2: 1},
        compiler_params=_params("arbitrary"))(z, dz, dzt, dq, dkc, dkp, dvc, dvp, ra, rbm, rbp, gq2, gk2)


def _in_bwd(dz, w_pairs, x, dx1, g1, tm, after):
    s = x.shape[0]
    n = s // tm
    sub = tm // N_PAIRS
    stripes = 4

    def body(d_ref, w_ref, x_ref, dx1_ref, g_ref, after_ref, gx_ref, acc_ref, dh_ref):
        i, k = pl.program_id(0), pl.program_id(1)

        def matmul(c):
            cols = slice(c * (D_MODEL // stripes), (c + 1) * (D_MODEL // stripes))
            dh_ref[i % 2, :, cols] += _dot(d_ref[...], w_ref[0, :, cols])

        def norm_bwd(c):
            part = sub // stripes
            mine = slice(c * part, (c + 1) * part)
            rows = pl.ds(pl.multiple_of(k * sub + c * part, part), part)
            dh = dh_ref[(i + 1) % 2, rows, :]
            dh_ref[(i + 1) % 2, rows, :] = jnp.zeros_like(dh)
            xn, r = _rms(x_ref[mine, :])
            gx_ref[rows, :] = dx1_ref[mine, :] + _rms_bwd(dh * g_ref[...], xn, r)
            acc_ref[0:1, :] += jnp.sum(dh * xn, axis=0, keepdims=True)

        @pl.when((i == 0) & (k == 0))
        def _():
            acc_ref[...] = jnp.zeros_like(acc_ref)
            dh_ref[...] = jnp.zeros_like(dh_ref)

        @pl.when(i == 0)
        def _():
            for c in range(stripes):
                matmul(c)

        @pl.when((i > 0) & (i < n))
        def _():
            for c in range(stripes):
                matmul(c)
                norm_bwd(c)

        @pl.when(i == n)
        def _():
            for c in range(stripes):
                norm_bwd(c)

    last = lambda i, k: jnp.where(i == n, N_PAIRS - 1, k)
    rows_before = lambda i, k: (jnp.maximum(i - 1, 0) * N_PAIRS + k, 0)
    return pl.pallas_call(
        body, name="in_bwd",
        out_shape=(jax.ShapeDtypeStruct((s, D_MODEL), F32), jax.ShapeDtypeStruct((SUBLANES, D_MODEL), F32)),
        grid=(n + 1, N_PAIRS),
        in_specs=[pl.BlockSpec((tm, PAIR_W), lambda i, k: (jnp.minimum(i, n - 1), last(i, k))),
                  pl.BlockSpec((1, PAIR_W, D_MODEL), lambda i, k: (last(i, k), 0, 0)),
                  pl.BlockSpec((sub, D_MODEL), rows_before), pl.BlockSpec((sub, D_MODEL), rows_before),
                  pl.BlockSpec((1, D_MODEL), lambda i, k: (0, 0)), ANY],
        out_specs=(pl.BlockSpec((tm, D_MODEL), lambda i, k: (jnp.maximum(i - 1, 0), 0)),
                   pl.BlockSpec((SUBLANES, D_MODEL), lambda i, k: (0, 0))),
        scratch_shapes=[pltpu.VMEM((2, tm, D_MODEL), F32)],
        compiler_params=_params("arbitrary", "arbitrary"))(dz, w_pairs, x, dx1, g1, after)


def _mm_grad(at, bs, tn, name):
    m, kdim = at.shape
    nblk = [b.shape[1] // tn for b in bs]
    starts = [sum(nblk[:t]) for t in range(len(bs))]

    def body(a_ref, *refs):
        b_refs, o_ref = refs[:len(bs)], refs[len(bs)]
        j = pl.program_id(0)
        for t, b_ref in enumerate(b_refs):
            @pl.when((j >= starts[t]) & (j < starts[t] + nblk[t]))
            def _():
                o_ref[...] = _dot(a_ref[...], b_ref[...]).astype(BF16)

    def b_spec(t):
        return pl.BlockSpec((kdim, tn), lambda j: (0, jnp.clip(j - starts[t], 0, nblk[t] - 1)))

    return pl.pallas_call(
        body, name=name,
        out_shape=jax.ShapeDtypeStruct((m, sum(nblk) * tn), BF16),
        grid=(sum(nblk),),
        in_specs=[_resident((m, kdim))] + [b_spec(t) for t in range(len(bs))],
        out_specs=pl.BlockSpec((m, tn), lambda j: (0, j)),
        compiler_params=_params("parallel"))(at, *bs)


def _grad_w_in(dzt, h):
    kdim = h.shape[0]

    def body(d_ref, h_ref, o_ref):
        o_ref[0] = _dot(d_ref[...], h_ref[...]).astype(BF16)

    return pl.pallas_call(
        body, name="grad_w_in",
        out_shape=jax.ShapeDtypeStruct((N_DEV, SHARD_IN, D_MODEL), BF16),
        grid=(N_DEV,),
        in_specs=[pl.BlockSpec((SHARD_IN, kdim), lambda j: (j, 0)), _resident((kdim, D_MODEL))],
        out_specs=pl.BlockSpec((1, SHARD_IN, D_MODEL), lambda j: (j, 0, 0)),
        compiler_params=_params("parallel"))(dzt, h)


def _place():
    return lax.axis_index("x"), lax.axis_index("y"), lax.axis_index("c")


ROW_TAPS, ROW_MISC = 4, 5
Q_AT, K_AT, SINK_AT, LOSS_AT = (ATTN_W + LANES * t for t in range(4))
SMALL_AT = [(0, 0), (1, 0), (2, 0), (3, 0), (ROW_MISC, Q_AT), (ROW_MISC, K_AT), (ROW_MISC, SINK_AT)]


def _tap_at(tap):
    return ROW_TAPS + tap // 2, ATTN_W * (tap % 2)


def _reduce_small(acc_g1, acc_g2, acc_ple, acc_qk, acc_attn):
    def body(g1_ref, g2_ref, ple_ref, qk_ref, attn_ref, out_ref, slab_ref, gath_ref, send_sems, recv_sems):
        x, y, c = _place()
        me = 4 * x + 2 * y + c
        slab_ref[...] = jnp.zeros_like(slab_ref)
        slab_ref[0:1, :] = g1_ref[0:1, :]
        slab_ref[1:2, :] = g2_ref[0:1, :]
        slab_ref[2:4, :] = ple_ref[0:2, :]
        qk = qk_ref[0:2, :]
        qk = jnp.where(_low_half(qk.shape), qk + pltpu.roll(qk, HEAD, 1), 0.0)
        misc = slab_ref.at[ROW_MISC:ROW_MISC + 1]
        misc[:, Q_AT:Q_AT + LANES] = qk[0:1]
        misc[:, K_AT:K_AT + LANES] = qk[1:2]
        lane = lax.broadcasted_iota(jnp.int32, (1, LANES), 1)
        misc[:, SINK_AT:SINK_AT + LANES] = jnp.where(lane < N_Q_HEADS, attn_ref[0:1, 0:LANES], 0.0)
        misc[:, LOSS_AT:LOSS_AT + LANES] = sum(
            ple_ref[2:3, LANES * t:LANES * (t + 1)] for t in range(D_MODEL // LANES))
        for tap in range(3):
            row, at = _tap_at(tap)
            slab_ref[row:row + 1, at:at + ATTN_W] = attn_ref[1 + tap:2 + tap, :]
        gath_ref[me] = slab_ref[...]
        copies = []
        for k in range(1, N_DEV):
            peer = (x ^ (k >> 2), y ^ ((k >> 1) & 1), c ^ (k & 1))
            copies.append(pltpu.make_async_remote_copy(
                src_ref=slab_ref, dst_ref=gath_ref.at[me], send_sem=send_sems.at[k - 1],
                recv_sem=recv_sems.at[k - 1], device_id=peer, device_id_type=MESH))
        for cp in copies:
            cp.start()
        for cp in copies:
            cp.wait_recv()
        for cp in copies:
            cp.wait_send()
        total = gath_ref[0]
        for d in range(1, N_DEV):
            total = total + gath_ref[d]
        out_ref[...] = total

    vmem = pl.BlockSpec(memory_space=pltpu.VMEM)
    return pl.pallas_call(
        body, name="reduce_small",
        out_shape=jax.ShapeDtypeStruct((SLAB_ROWS, D_MODEL), F32),
        in_specs=[vmem] * 5, out_specs=vmem,
        scratch_shapes=[pltpu.VMEM((SLAB_ROWS, D_MODEL), F32), pltpu.VMEM((N_DEV, SLAB_ROWS, D_MODEL), F32),
                        pltpu.SemaphoreType.DMA((N_DEV - 1,)), pltpu.SemaphoreType.DMA((N_DEV - 1,))])(
            acc_g1, acc_g2, acc_ple, acc_qk, acc_attn)


def _pair_sum(g, r, place, tr, name):
    _, _, rows, cols = g.shape

    def body(place_ref, g_ref, r_ref, pb_ref, own_ref):
        tot = g_ref[0, 0].astype(F32) + r_ref[0].astype(F32)
        pb_ref[0] = tot.astype(BF16)

        @pl.when(pl.program_id(1) == place_ref[1])
        def _():
            own_ref[...] = tot

    grid_spec = pltpu.PrefetchScalarGridSpec(
        num_scalar_prefetch=1, grid=(rows // tr, 4),
        in_specs=[pl.BlockSpec((1, 1, tr, cols), lambda i, q, place_ref: (q, place_ref[0], i, 0)),
                  pl.BlockSpec((1, tr, cols), lambda i, q, place_ref: (q, i, 0))],
        out_specs=(pl.BlockSpec((1, tr, cols), lambda i, q, place_ref: (q, i, 0)),
                   pl.BlockSpec((tr, cols), lambda i, q, place_ref: (i, 0))))
    return pl.pallas_call(
        body, name=name, grid_spec=grid_spec,
        out_shape=(jax.ShapeDtypeStruct((4, rows, cols), BF16), jax.ShapeDtypeStruct((rows, cols), F32)),
        compiler_params=_params("arbitrary", "arbitrary"))(place, g, r)


HBM = pl.BlockSpec(memory_space=pltpu.HBM)
SEM = pl.BlockSpec(memory_space=pltpu.SEMAPHORE)
SIDE_EFFECT = pltpu.CompilerParams(has_side_effects=pltpu.SideEffectType.DATAFLOW_SIDE_EFFECTING)
TOKEN = jax.ShapeDtypeStruct((SUBLANES, LANES), F32)


def _hbm(a):
    return pltpu.with_memory_space_constraint(a, pltpu.HBM)


def _hbm_like(arrays):
    return tuple(pltpu.HBM(a.shape, a.dtype) for a in arrays)


def _block_of(px, py, pc):
    return 4 * px + 2 * py + pc


def _gather_start(shards, after):
    na = len(shards)
    lands = [_hbm(lax.empty((N_DEV,) + a.shape, a.dtype)) for a in shards]

    def body(*refs):
        ins, land = refs[:na], refs[na:2 * na]
        send_sems, recv_ici, recv_d2d = refs[2 * na + 1:2 * na + 4]
        token = refs[-1]
        x, y, c = _place()
        for k, peer in enumerate([(x, y, 1 - c), (1 - x, y, c), (x, 1 - y, c), (1 - x, 1 - y, c)]):
            for t in range(na):
                pltpu.make_async_remote_copy(
                    src_ref=ins[t], dst_ref=land[t].at[_block_of(x, y, c)], send_sem=send_sems.at[4 * t + k],
                    recv_sem=recv_d2d.at[4 * t] if k == 0 else recv_ici.at[3 * t + k - 1],
                    device_id=peer, device_id_type=MESH).start()
        token[...] = jnp.zeros_like(token)

    out = pl.pallas_call(
        body, name="gather_start",
        out_shape=(pltpu.SemaphoreType.DMA((4 * na,)), pltpu.SemaphoreType.DMA((3 * na,)),
                   pltpu.SemaphoreType.DMA((4 * na,)), *_hbm_like(lands), TOKEN),
        in_specs=[ANY] * na + [HBM] * na + [ANY],
        out_specs=(SEM, SEM, SEM, *[HBM] * na, pl.BlockSpec(memory_space=pltpu.VMEM)),
        input_output_aliases={na + i: 3 + i for i in range(na)},
        compiler_params=SIDE_EFFECT)(*shards, *lands, after)
    send_sems, recv_ici, recv_d2d = out[:3]
    state = dict(send=send_sems, ici=recv_ici, d2d=recv_d2d, shards=list(shards), lands=out[3:3 + na])
    return state, out[-1]


def _gather_forward(state, after):
    lands = state["lands"]
    na = len(lands)

    def body(*refs):
        land = refs[:na]
        recv_ici, recv_d2d = refs[na], refs[na + 1]
        fwd_sems, token = refs[-2], refs[-1]
        x, y, c = _place()
        for j, chip in enumerate([(1 - x, y), (x, 1 - y), (1 - x, 1 - y)]):
            for t in range(na):
                blk = land[t].at[_block_of(*chip, c)]
                pltpu.make_async_remote_copy(
                    src_ref=blk, dst_ref=blk, send_sem=fwd_sems.at[3 * t + j], recv_sem=recv_ici.at[3 * t + j],
                    device_id=(x, y, c), device_id_type=MESH).wait_recv()
                pltpu.make_async_remote_copy(
                    src_ref=blk, dst_ref=blk, send_sem=fwd_sems.at[3 * t + j], recv_sem=recv_d2d.at[4 * t + 1 + j],
                    device_id=(x, y, 1 - c), device_id_type=MESH).start()
        token[...] = jnp.zeros_like(token)

    out = pl.pallas_call(
        body, name="gather_forward",
        out_shape=(*_hbm_like(lands), pltpu.SemaphoreType.DMA((3 * na,)), TOKEN),
        in_specs=[HBM] * na + [SEM, SEM, ANY],
        out_specs=(*[HBM] * na, SEM, pl.BlockSpec(memory_space=pltpu.VMEM)),
        input_output_aliases={i: i for i in range(na)},
        compiler_params=SIDE_EFFECT)(*lands, state["ici"], state["d2d"], after)
    return dict(state, lands=out[:na], fwd=out[na]), out[-1]


def _gather_wait(state, after):
    shards, lands = state["shards"], state["lands"]
    na = len(lands)

    def body(*refs):
        ins, land = refs[:na], refs[na:2 * na]
        send_sems, fwd_sems, recv_d2d = refs[2 * na:2 * na + 3]
        x, y, c = _place()
        chips = [(1 - x, y), (x, 1 - y), (1 - x, 1 - y)]
        for t in range(na):
            mine = land[t].at[_block_of(x, y, c)]
            for k in range(4):
                pltpu.make_async_remote_copy(
                    src_ref=ins[t], dst_ref=mine, send_sem=send_sems.at[4 * t + k], recv_sem=recv_d2d.at[4 * t],
                    device_id=(x, y, c), device_id_type=MESH).wait_send()
            for j, chip in enumerate(chips):
                blk = land[t].at[_block_of(*chip, c)]
                pltpu.make_async_remote_copy(
                    src_ref=blk, dst_ref=blk, send_sem=fwd_sems.at[3 * t + j], recv_sem=recv_d2d.at[4 * t + 1 + j],
                    device_id=(x, y, c), device_id_type=MESH).wait_send()
            for k, blk_id in enumerate([_block_of(x, y, 1 - c)] + [_block_of(*chip, 1 - c) for chip in chips]):
                blk = land[t].at[blk_id]
                pltpu.make_async_remote_copy(
                    src_ref=blk, dst_ref=blk, send_sem=send_sems.at[4 * t], recv_sem=recv_d2d.at[4 * t + k],
                    device_id=(x, y, c), device_id_type=MESH).wait_recv()

    out = pl.pallas_call(
        body, name="gather_wait",
        out_shape=_hbm_like(lands),
        in_specs=[ANY] * na + [HBM] * na + [SEM, SEM, SEM, ANY],
        out_specs=tuple([HBM] * na),
        input_output_aliases={na + i: i for i in range(na)},
        compiler_params=SIDE_EFFECT)(*shards, *lands, state["send"], state["fwd"], state["d2d"], after)
    return out


def _gather_from_sibling(state, after):
    lands = state["lands"]
    na = len(lands)

    def body(*refs):
        land, recv_d2d = refs[:na], refs[na]
        x, y, c = _place()
        for t in range(na):
            blk = land[t].at[_block_of(x, y, 1 - c)]
            pltpu.make_async_remote_copy(src_ref=blk, dst_ref=blk, send_sem=recv_d2d.at[4 * t],
                                         recv_sem=recv_d2d.at[4 * t], device_id=(x, y, c),
                                         device_id_type=MESH).wait_recv()

    out = pl.pallas_call(
        body, name="gather_from_sibling", out_shape=_hbm_like(lands),
        in_specs=[HBM] * na + [SEM, ANY], out_specs=tuple([HBM] * na),
        input_output_aliases={i: i for i in range(na)},
        compiler_params=SIDE_EFFECT)(*lands, state["d2d"], after)
    return dict(state, lands=list(out))


def _gather_from_chip(state, j, afters, last):
    shards, lands = state["shards"], state["lands"]
    na = len(lands)

    def chip_blocks(land_ref):
        x, y, c = _place()
        chip = [(1 - x, y), (x, 1 - y), (1 - x, 1 - y)][j]
        return (x, y, c), land_ref.at[_block_of(*chip, c)], land_ref.at[_block_of(*chip, 1 - c)]

    def forward(*refs):
        land, recv_ici, recv_d2d, fwd_sems = refs[:na], refs[na], refs[na + 1], refs[-1]
        for t in range(na):
            (x, y, c), mine, _ = chip_blocks(land[t])
            pltpu.make_async_remote_copy(src_ref=mine, dst_ref=mine, send_sem=fwd_sems.at[t],
                                         recv_sem=recv_ici.at[3 * t + j], device_id=(x, y, c),
                                         device_id_type=MESH).wait_recv()
            pltpu.make_async_remote_copy(src_ref=mine, dst_ref=mine, send_sem=fwd_sems.at[t],
                                         recv_sem=recv_d2d.at[4 * t + 1 + j], device_id=(x, y, 1 - c),
                                         device_id_type=MESH).start()

    out = pl.pallas_call(
        forward, name="gather_pass_chip_" + str(j),
        out_shape=(*_hbm_like(lands), pltpu.SemaphoreType.DMA((na,))),
        in_specs=[HBM] * na + [SEM, SEM] + [ANY] * len(afters), out_specs=(*[HBM] * na, SEM),
        input_output_aliases={i: i for i in range(na)},
        compiler_params=SIDE_EFFECT)(*lands, state["ici"], state["d2d"], *afters)
    lands, fwd_sems = out[:na], out[na]

    def arrive(*refs):
        land, fwd_sems, recv_d2d = refs[:na], refs[na], refs[na + 1]
        shard, send_sems = refs[na + 2:2 * na + 2], refs[2 * na + 2]
        for t in range(na):
            (x, y, c), mine, theirs = chip_blocks(land[t])
            pltpu.make_async_remote_copy(src_ref=theirs, dst_ref=theirs, send_sem=fwd_sems.at[t],
                                         recv_sem=recv_d2d.at[4 * t + 1 + j], device_id=(x, y, c),
                                         device_id_type=MESH).wait_recv()
            pltpu.make_async_remote_copy(src_ref=mine, dst_ref=mine, send_sem=fwd_sems.at[t],
                                         recv_sem=recv_d2d.at[4 * t + 1 + j], device_id=(x, y, c),
                                         device_id_type=MESH).wait_send()
            for k in range(4 if last else 0):
                pltpu.make_async_remote_copy(
                    src_ref=shard[t], dst_ref=land[t].at[_block_of(x, y, c)], send_sem=send_sems.at[4 * t + k],
                    recv_sem=recv_d2d.at[4 * t], device_id=(x, y, c), device_id_type=MESH).wait_send()

    out = pl.pallas_call(
        arrive, name="gather_take_chip_" + str(j), out_shape=_hbm_like(lands),
        in_specs=[HBM] * na + [SEM, SEM] + [ANY] * na + [SEM], out_specs=tuple([HBM] * na),
        input_output_aliases={i: i for i in range(na)},
        compiler_params=SIDE_EFFECT)(*lands, fwd_sems, state["d2d"], *shards, state["send"])
    return dict(state, lands=list(out))


def _to_sibling(srcs, lands, send_sems, recv_sems):
    x, y, c = _place()
    return [pltpu.make_async_remote_copy(
        src_ref=srcs[t].at[:, 1 - c], dst_ref=lands[t], send_sem=send_sems.at[t], recv_sem=recv_sems.at[t],
        device_id=(x, y, 1 - c), device_id_type=MESH) for t in range(len(srcs))]


def _to_chips(srcs, lands, send_sems, recv_sems):
    x, y, c = _place()
    copies = []
    for k in (1, 2, 3):
        px, py = x ^ (k >> 1), y ^ (k & 1)
        copies += [pltpu.make_async_remote_copy(
            src_ref=srcs[t].at[2 * px + py], dst_ref=lands[t].at[k - 1], send_sem=send_sems.at[3 * t + k - 1],
            recv_sem=recv_sems.at[3 * t + k - 1], device_id=(px, py, c), device_id_type=MESH) for t in range(len(srcs))]
    return copies


def _exchange_start(name, srcs, land_shapes, copies, per_array, after):
    na = len(srcs)
    lands = [_hbm(lax.empty(shp, a.dtype)) for shp, a in zip(land_shapes, srcs)]

    def body(*refs):
        token = refs[-1]
        for cp in copies(refs[:na], refs[na:2 * na], refs[2 * na + 1], refs[2 * na + 2]):
            cp.start()
        token[...] = jnp.zeros_like(token)

    out = pl.pallas_call(
        body, name=name,
        out_shape=(pltpu.SemaphoreType.DMA((na * per_array,)), pltpu.SemaphoreType.DMA((na * per_array,)),
                   *_hbm_like(lands), TOKEN),
        in_specs=[ANY] * na + [HBM] * na + [ANY],
        out_specs=(SEM, SEM, *[HBM] * na, pl.BlockSpec(memory_space=pltpu.VMEM)),
        input_output_aliases={na + i: 2 + i for i in range(na)},
        compiler_params=SIDE_EFFECT)(*srcs, *lands, after)
    return dict(send=out[0], recv=out[1], srcs=list(srcs), lands=out[2:2 + na]), out[-1]


def _exchange_wait(name, state, copies, afters):
    srcs, lands = state["srcs"], state["lands"]
    na = len(srcs)

    def body(*refs):
        for cp in copies(refs[:na], refs[na:2 * na], refs[2 * na], refs[2 * na + 1]):
            cp.wait_send()
            cp.wait_recv()

    out = pl.pallas_call(
        body, name=name,
        out_shape=_hbm_like(lands),
        in_specs=[ANY] * na + [HBM] * na + [SEM, SEM] + [ANY] * len(afters),
        out_specs=tuple([HBM] * na),
        input_output_aliases={na + i: i for i in range(na)},
        compiler_params=SIDE_EFFECT)(*srcs, *lands, state["send"], state["recv"], *afters)
    return out


def _adamw_math(w, g, m, v):
    m = ADAM_B1 * m + (1.0 - ADAM_B1) * g
    v = ADAM_B2 * v + (1.0 - ADAM_B2) * (g * g)
    m_hat = m / (1.0 - ADAM_B1 ** ADAM_STEP)
    v_hat = v / (1.0 - ADAM_B2 ** ADAM_STEP)
    return -ADAM_LR * (m_hat / (jnp.sqrt(v_hat) + ADAM_EPS) + ADAM_WD * w), m, v


def _adamw(own, others, w, m, v, tr, name, after):
    rows, cols = w.shape
    blk = pl.BlockSpec((tr, cols), lambda i: (i, 0))

    def body(own_ref, oth_ref, w_ref, m_ref, v_ref, after_ref, g_ref, d_ref, nm_ref, nv_ref):
        g = own_ref[...]
        for k in range(3):
            g = g + oth_ref[k].astype(F32)
        g_ref[...] = g
        d_ref[...], nm_ref[...], nv_ref[...] = _adamw_math(w_ref[...], g, m_ref[...], v_ref[...])

    out = jax.ShapeDtypeStruct((rows, cols), F32)
    return pl.pallas_call(
        body, name=name, out_shape=(out, out, out, out), grid=(rows // tr,),
        in_specs=[blk, pl.BlockSpec((3, tr, cols), lambda i: (0, i, 0)), blk, blk, blk, ANY],
        out_specs=(blk, blk, blk, blk),
        compiler_params=_params("parallel"))(own, others, w, m, v, after)


def _adamw_small(red, me, params, moments1, moments2):
    n = len(params)

    def body(me_ref, red_ref, *refs):
        ws, ms, vs = refs[:n], refs[n:2 * n], refs[2 * n:3 * n]
        loss_ref = refs[3 * n]
        outs = refs[3 * n + 1:]
        loss_ref[...] = jnp.sum(red_ref[ROW_MISC:ROW_MISC + 1, LOSS_AT:LOSS_AT + LANES], axis=-1, keepdims=True)
        for t, (row, at) in enumerate(SMALL_AT):
            g = red_ref[row:row + 1, at:at + ws[t].shape[1]]
            d, nm, nv = _adamw_math(ws[t][...], g, ms[t][...], vs[t][...])
            for o, val in zip(outs[4 * t:4 * t + 4], (g, d, nm, nv)):
                o[...] = val
        for tap in range(ws[-1].shape[0]):
            row, at = _tap_at(tap)
            g = red_ref[row:row + 1, pl.ds(pl.multiple_of(at + me_ref[0, 0] * LANES, LANES), LANES)]
            d, nm, nv = _adamw_math(ws[-1][tap], g, ms[-1][tap], vs[-1][tap])
            for o, val in zip(outs[4 * (n - 1):], (g, d, nm, nv)):
                o[tap] = val

    vmem = pl.BlockSpec(memory_space=pltpu.VMEM)
    shapes = [jax.ShapeDtypeStruct(w.shape, F32) for w in params for _ in range(4)]
    out = pl.pallas_call(
        body, name="adamw_small", out_shape=(jax.ShapeDtypeStruct((1, 1), F32), *shapes),
        in_specs=[pl.BlockSpec(memory_space=pltpu.SMEM), vmem] + [vmem] * (3 * n),
        out_specs=tuple([vmem] * (1 + 4 * n)))(me, red, *params, *moments1, *moments2)
    return out[0], [list(out[1 + k::4]) for k in range(4)]


def _tables(s, gq, gk, conv_w):
    gq2 = jnp.tile(gq.reshape(1, HEAD), (1, 2))
    gk2 = jnp.tile(gk.reshape(1, HEAD), (1, 2))
    conv_wp = jnp.pad(conv_w, ((0, SUBLANES - conv_w.shape[0]), (0, 0)))
    return _rope_tables(s), gq2, gk2, conv_wp


def _pair_id(q):
    return jnp.array([q, 0], jnp.int32)


def _forward_in(x, g1, shards):
    s = x.shape[0]
    h = _prenorm(x, g1, min(512, s), x)
    z, w_pairs = lax.empty((s, IN_W), F32), lax.empty((N_PAIRS, PAIR_W, D_MODEL), BF16)
    for q in range(N_PAIRS):
        z, w_pairs = _fwd_in_pair(h, shards, z, w_pairs, _pair_id(q), min(512, s), "fwd_in_" + str(q),
                                  own=shards[0] if q == 0 else None)
    return h, z, w_pairs


def _forward_attn(z, rope, gq2, gk2, conv_wp, sinks, after):
    s = z.shape[0]
    qn, k2, v2 = _qk_prep(z, *rope, gq2, gk2, min(256, s), after)
    a, mix, mixt = _attn_fwd(qn, k2, v2, z, conv_wp, sinks)
    return qn, k2, v2, a, mix, mixt


def _forward_out(x, p, target, mix, mixt, w_out, g2, w_pg, b_pg, w_pp, g3):
    s = x.shape[0]
    tm = min(512, s)
    x1, hn2, hn2t = _fwd_out(mix, w_out, x, g2, tm)
    dy, dgp, dt, pt, acc_ple = _ple(hn2, w_pg, b_pg, p, w_pp, g3, x1, target, min(256, s))
    dx1, dx1b, acc_g2 = _gate_bwd(dgp, w_pg, x1, dy, g2, tm)
    gw_out = _mm_grad(mixt, [dx1b], 512, "grad_w_out")
    gw_pg = _mm_grad(hn2t, [dgp], 512, "grad_w_ple_gate")
    gw_pp = _mm_grad(pt, [dt], 512, "grad_w_ple_proj")
    return dx1, dx1b, (gw_out, gw_pg, gw_pp), acc_ple, acc_g2


def _backward_attn(dmix, h, z, qn, k2, v2, a, rope, gq2, gk2, conv_wp, sinks, after):
    dq, dkc, dkp, dvc, dvp, dz, dzt, acc_attn = _attn_bwd(qn, k2, v2, a, z, dmix, conv_wp, sinks, after)
    dz, dzt, acc_qk = _qkv_bwd(z, dz, dzt, dq, dkc, dkp, dvc, dvp, *rope, gq2, gk2)
    return dz, _grad_w_in(dzt, h), acc_attn, acc_qk


def _local_step(x, p, target, g1, shards, gq, gk, sinks, conv_w, w_out, g2, w_pg, b_pg, w_pp, g3):
    rope, gq2, gk2, conv_wp = _tables(x.shape[0], gq, gk, conv_w)
    h, z, w_pairs = _forward_in(x, g1, shards)
    qn, k2, v2, a, mix, mixt = _forward_attn(z, rope, gq2, gk2, conv_wp, sinks, z)
    dx1, dx1b, (gw_out, gw_pg, gw_pp), acc_ple, acc_g2 = _forward_out(
        x, p, target, mix, mixt, w_out, g2, w_pg, b_pg, w_pp, g3)
    dmix = _mm_nt(dx1b, w_out, min(512, x.shape[0]), "out_bwd", dx1b)
    dz, gw_in, acc_attn, acc_qk = _backward_attn(dmix, h, z, qn, k2, v2, a, rope, gq2, gk2, conv_wp, sinks, dmix)
    grad_x, acc_g1 = _in_bwd(dz, w_pairs, x, dx1, g1, min(512, x.shape[0]), dx1)
    return grad_x, (gw_in, gw_out, gw_pg, gw_pp), (acc_g1, acc_g2, acc_ple, acc_qk, acc_attn)


def _by_owner(g):
    return g.reshape((4, 2) + g.shape[1:])


def kernel(x, p, norm_gain, w_in, q_norm_gain, k_norm_gain, attn_sinks, conv_w, w_out, ple_gate_norm_gain, w_ple_gate, b_ple_gate, w_ple_proj, ple_norm_gain, loss_target, m_norm_gain, m_w_in, m_q_norm_gain, m_k_norm_gain, m_attn_sinks, m_conv_w, m_w_out, m_ple_gate_norm_gain, m_w_ple_gate, m_b_ple_gate, m_w_ple_proj, m_ple_norm_gain, v_norm_gain, v_w_in, v_q_norm_gain, v_k_norm_gain, v_attn_sinks, v_conv_w, v_w_out, v_ple_gate_norm_gain, v_w_ple_gate, v_b_ple_gate, v_w_ple_proj, v_ple_norm_gain):
    me = 4 * lax.axis_index("x") + 2 * lax.axis_index("y") + lax.axis_index("c")
    place = jnp.stack([lax.axis_index("c"), 2 * lax.axis_index("x") + lax.axis_index("y")]).astype(jnp.int32)
    xs, ps, target = x[0], p[0, 0], loss_target[0]

    shard_in = w_in[0].T.astype(BF16)
    own_late = [w_out[0].astype(BF16), w_ple_gate[0].astype(BF16), w_ple_proj[0].astype(BF16)]
    with_own = lambda gathered, own: lax.dynamic_update_slice(gathered, own[None], (me,) + (0,) * own.ndim)
    early, started = _gather_start([shard_in, conv_w[0]], shard_in)
    tm = min(512, xs.shape[0])
    h = _prenorm(xs, norm_gain, tm, started)

    z, w_pairs = lax.empty((xs.shape[0], IN_W), F32), lax.empty((N_PAIRS, PAIR_W, D_MODEL), BF16)
    early = _gather_from_sibling(early, h)
    pair_of = lambda flip: jnp.stack([place[1] ^ flip, place[0]])
    z, w_pairs = _fwd_in_pair(h, early["lands"][0], z, w_pairs, pair_of(0), tm, "fwd_in_own", own=shard_in)
    for j, flip in enumerate((2, 1, 3)):
        early = _gather_from_chip(early, j, (z,) if j != 1 else (z, started_late), last=j == 2)
        z, w_pairs = _fwd_in_pair(h, early["lands"][0], z, w_pairs, pair_of(flip), tm, "fwd_in_chip_" + str(j))
        if j == 0:
            late, started_late = _gather_start(own_late, z)
    conv_full = jnp.transpose(with_own(early["lands"][1], conv_w[0]), (1, 0, 2)).reshape(3, ATTN_W)
    rope, gq2, gk2, conv_wp = _tables(xs.shape[0], q_norm_gain[0], k_norm_gain[0], conv_full)
    late, forwarded = _gather_forward(late, z)
    qn, k2, v2, a, mix, mixt = _forward_attn(z, rope, gq2, gk2, conv_wp, attn_sinks, forwarded)
    g_out, g_pg, g_pp = (with_own(g, own) for g, own in zip(_gather_wait(late, mix), own_late))
    w_out_f = g_out.reshape(D_MODEL, D_MODEL)
    w_pg_f = g_pg.reshape(D_MODEL, D_MODEL)
    w_pp_f = jnp.transpose(g_pp, (1, 0, 2)).reshape(PLE_DIM, D_MODEL)

    dx1, dx1b, (gw_out, gw_pg, gw_pp), acc_ple, acc_g2 = _forward_out(
        xs, ps, target, mix, mixt, w_out_f, ple_gate_norm_gain, w_pg_f, b_ple_gate, w_pp_f, ple_norm_gain)

    names = ("w_out", "w_ple_gate", "w_ple_proj")
    gw_pp_t = jnp.transpose(gw_pp.reshape(PLE_DIM, N_DEV, PLE_DIM), (1, 0, 2))
    grads = [_by_owner(gw_out.reshape(N_DEV, D_MODEL // N_DEV, D_MODEL)),
             _by_owner(gw_pg.reshape(N_DEV, D_MODEL // N_DEV, D_MODEL)), _by_owner(gw_pp_t)]
    pairs, paired = _exchange_start("pair_start", grads, [(4,) + g.shape[2:] for g in grads], _to_sibling, 1, dx1b)
    dmix = _mm_nt(dx1b, w_out_f, tm, "out_bwd", paired)
    from_sibling = _exchange_wait("pair_wait", pairs, _to_sibling, (dmix,))
    sums = [_pair_sum(g, r, place, 256, "pair_sum_" + nm) for g, r, nm in zip(pairs["srcs"], from_sibling, names)]
    chips, sent = _exchange_start("chip_start", [pb for pb, _ in sums], [(3,) + pb.shape[1:] for pb, _ in sums],
                                  _to_chips, 3, sums[-1][1])

    dz, gw_in, acc_attn, acc_qk = _backward_attn(
        dmix, h, z, qn, k2, v2, a, rope, gq2, gk2, conv_wp, attn_sinks, sent)

    gw_in_t = [_by_owner(gw_in)]
    pairs_in, paired_in = _exchange_start("pair_start_w_in", gw_in_t, [(4,) + gw_in_t[0].shape[2:]], _to_sibling, 1,
                                          gw_in)
    from_chips = _exchange_wait("chip_wait", chips, _to_chips, (gw_in,))
    big = {}
    for (_, own), oth, w, m, v, nm in zip(sums, from_chips, (w_out, w_ple_gate, w_ple_proj),
                                          (m_w_out, m_w_ple_gate, m_w_ple_proj),
                                          (v_w_out, v_w_ple_gate, v_w_ple_proj), names):
        big[nm] = [t[None] for t in _adamw(own, oth, w[0], m[0], v[0], 256, "adamw_" + nm, paired_in)]

    (from_sibling_in,) = _exchange_wait("pair_wait_w_in", pairs_in, _to_sibling, [big[nm][0] for nm in names])
    pb_in, own_in = _pair_sum(pairs_in["srcs"][0], from_sibling_in, place, SHARD_IN // 2, "pair_sum_w_in")
    chips_in, sent_in = _exchange_start("chip_start_w_in", [pb_in], [(3,) + pb_in.shape[1:]], _to_chips, 3, own_in)
    grad_x, acc_g1 = _in_bwd(dz, w_pairs, xs, dx1, norm_gain, tm, sent_in)
    (from_chips_in,) = _exchange_wait("chip_wait_w_in", chips_in, _to_chips, (grad_x,))
    big["w_in"] = [t.T[None] for t in _adamw(own_in, from_chips_in, w_in[0].T, m_w_in[0].T, v_w_in[0].T, SHARD_IN // 4,
                                             "adamw_w_in", grad_x)]

    red = _reduce_small(acc_g1, acc_g2, acc_ple, acc_qk, acc_attn)
    small = [norm_gain, ple_gate_norm_gain, b_ple_gate, ple_norm_gain, q_norm_gain, k_norm_gain, attn_sinks]
    small_m = [m_norm_gain, m_ple_gate_norm_gain, m_b_ple_gate, m_ple_norm_gain, m_q_norm_gain, m_k_norm_gain,
               m_attn_sinks]
    small_v = [v_norm_gain, v_ple_gate_norm_gain, v_b_ple_gate, v_ple_norm_gain, v_q_norm_gain, v_k_norm_gain,
               v_attn_sinks]
    taps_first = lambda t: jnp.transpose(t, (1, 0, 2))
    loss, kinds = _adamw_small(red, me.reshape(1, 1).astype(jnp.int32), small + [taps_first(conv_w)],
                               small_m + [taps_first(m_conv_w)], small_v + [taps_first(v_conv_w)])

    def order(k):
        sm = kinds[k]
        return [sm[0], big["w_in"][k], sm[4], sm[5], sm[6], taps_first(sm[7]), big["w_out"][k], sm[1],
                big["w_ple_gate"][k], sm[2], big["w_ple_proj"][k], sm[3]]

    return (loss[0, 0], grad_x[None], *order(0), *order(1), *order(2), *order(3))
```

```python
import jax
import jax.numpy as jnp
from jax import lax
from jax.experimental import pallas as pl
from jax.experimental.pallas import tpu as pltpu

F32, BF16 = jnp.float32, jnp.bfloat16

D_MODEL = 2048
PLE_DIM = 256
ATTN_W = 1024
HEAD = 64
N_Q_HEADS = 16
KV_W = 256
QKV_W = ATTN_W + 2 * KV_W
REST_W = 5 * 1024
IN_W = QKV_W + REST_W
GATE_A0, CONV_B0, CONV_C0, CONV_H0, GATE_C0 = (QKV_W + 1024 * t for t in range(5))
K2_W = 4 * 128
ROT = 16
ROPE_THETA = 500000.0
EPS = 1e-6
NEG_INF = -1e30
BLK = 128
LANES = 128
SUBLANES = 8
N_DEV = 8
SHARD_IN = IN_W // N_DEV
PAIR_W = 2 * SHARD_IN
N_PAIRS = IN_W // PAIR_W
SLAB_ROWS = 8
SUB_ROWS = 128
V7X_VMEM_LIMIT = 52 * 1024 * 1024

ADAM_LR, ADAM_B1, ADAM_B2, ADAM_EPS, ADAM_WD, ADAM_STEP = 0.001, 0.9, 0.999, 1e-08, 0.01, 10
MESH = pl.DeviceIdType.MESH


def _params(*semantics):
    return pltpu.CompilerParams(dimension_semantics=semantics, vmem_limit_bytes=V7X_VMEM_LIMIT)


ANY = pl.BlockSpec(memory_space=pl.ANY)


def _resident(shape):
    return pl.BlockSpec(shape, lambda *_: (0,) * len(shape), pipeline_mode=pl.Buffered(1))


def _dot(a, b):
    return jnp.dot(a, b, preferred_element_type=F32)


def _dot_nt(a, b):
    return lax.dot_general(a, b, (((1,), (1,)), ((), ())), preferred_element_type=F32)


def _rms(xf):
    r = lax.rsqrt(jnp.mean(xf * xf, axis=-1, keepdims=True) + EPS)
    return xf * r, r


def _rms_bwd(dxn, xn, r):
    return r * (dxn - xn * jnp.mean(dxn * xn, axis=-1, keepdims=True))


def _sig(g):
    return jax.nn.sigmoid(g)


def _dsilu(g, sg):
    return sg * (1.0 + g * (1.0 - sg))


def _low_half(shape):
    return lax.broadcasted_iota(jnp.int32, shape, len(shape) - 1) < HEAD


def _half_sums(v):
    lo = _low_half(v.shape)
    s_lo = jnp.sum(jnp.where(lo, v, 0.0), axis=-1, keepdims=True)
    s_hi = jnp.sum(jnp.where(lo, 0.0, v), axis=-1, keepdims=True)
    return jnp.where(lo, s_lo, s_hi)


def _rope(v, a, bm, bp):
    return v * a + pltpu.roll(v, LANES - ROT // 2, 1) * bm + pltpu.roll(v, ROT // 2, 1) * bp


def _rope_t(dy, a, bm, bp):
    return dy * a + pltpu.roll(dy * bm, ROT // 2, 1) + pltpu.roll(dy * bp, LANES - ROT // 2, 1)


def _dup_halves(v):
    lo = _low_half(v.shape)
    a = jnp.where(lo, v, 0.0)
    b = jnp.where(lo, 0.0, v)
    return a + pltpu.roll(a, HEAD, 1), b + pltpu.roll(b, HEAD, 1)


def _rope_tables(s):
    half = ROT // 2
    lane = lax.broadcasted_iota(jnp.int32, (s, LANES), 1) % HEAD
    pos = lax.broadcasted_iota(jnp.int32, (half, s), 1).astype(F32)
    freq = lax.broadcasted_iota(jnp.int32, (half, s), 0).astype(F32)
    ang = pos * jnp.power(jnp.float32(ROPE_THETA), -freq * 2.0 / ROT)
    cos, sin = lax.optimization_barrier((jnp.cos(ang), jnp.sin(ang)))
    cos, sin = (jnp.tile(t.T, (1, LANES // half)) for t in (cos, sin))
    a = jnp.where(lane < ROT, cos, 1.0)
    bm = jnp.where(lane < half, -sin, 0.0)
    bp = jnp.where((lane >= half) & (lane < ROT), sin, 0.0)
    return a, bm, bp


def _prenorm(x, g1, tm, after):
    s = x.shape[0]

    def body(x_ref, g_ref, after_ref, h_ref):
        xn, _ = _rms(x_ref[...])
        h_ref[...] = (xn * g_ref[...]).astype(BF16)

    return pl.pallas_call(
        body, name="prenorm",
        out_shape=jax.ShapeDtypeStruct((s, D_MODEL), BF16),
        grid=(s // tm,),
        in_specs=[pl.BlockSpec((tm, D_MODEL), lambda i: (i, 0)), pl.BlockSpec((1, D_MODEL), lambda i: (0, 0)), ANY],
        out_specs=pl.BlockSpec((tm, D_MODEL), lambda i: (i, 0)),
        compiler_params=_params("parallel"))(x, g1, after)


def _fwd_in_pair(h, shards, z, w_pairs, pair, tm, name, own=None):
    s = h.shape[0]

    def body(pair_ref, h_ref, lo_ref, hi_ref, z_in, wp_in, z_ref, wp_ref):
        @pl.when(pl.program_id(0) == 0)
        def _():
            wp_ref[0, 0:SHARD_IN, :] = lo_ref[0]
            wp_ref[0, SHARD_IN:PAIR_W, :] = hi_ref[0]

        z_ref[...] = _dot_nt(h_ref[...], wp_ref[0])

    def body_own(pair_ref, h_ref, own_ref, other_ref, z_in, wp_in, z_ref, wp_ref):
        @pl.when(pl.program_id(0) == 0)
        def _():
            first = pl.multiple_of(pair_ref[1] * SHARD_IN, SHARD_IN)
            wp_ref[0, pl.ds(first, SHARD_IN), :] = own_ref[...]
            wp_ref[0, pl.ds(SHARD_IN - first, SHARD_IN), :] = other_ref[0]

        z_ref[...] = _dot_nt(h_ref[...], wp_ref[0])

    if own is None:
        blocks = [pl.BlockSpec((1, SHARD_IN, D_MODEL), lambda i, p: (2 * p[0], 0, 0)),
                  pl.BlockSpec((1, SHARD_IN, D_MODEL), lambda i, p: (2 * p[0] + 1, 0, 0))]
        operands = (shards, shards)
    else:
        blocks = [pl.BlockSpec((SHARD_IN, D_MODEL), lambda i, p: (0, 0)),
                  pl.BlockSpec((1, SHARD_IN, D_MODEL), lambda i, p: (2 * p[0] + 1 - p[1], 0, 0))]
        operands = (own, shards)
    grid_spec = pltpu.PrefetchScalarGridSpec(
        num_scalar_prefetch=1, grid=(s // tm,),
        in_specs=[pl.BlockSpec((tm, D_MODEL), lambda i, p: (i, 0)), *blocks, ANY, ANY],
        out_specs=(pl.BlockSpec((tm, PAIR_W), lambda i, p: (i, p[0])),
                   pl.BlockSpec((1, PAIR_W, D_MODEL), lambda i, p: (p[0], 0, 0))))
    return pl.pallas_call(
        body if own is None else body_own, name=name, grid_spec=grid_spec,
        out_shape=(jax.ShapeDtypeStruct(z.shape, z.dtype), jax.ShapeDtypeStruct(w_pairs.shape, w_pairs.dtype)),
        input_output_aliases={4: 0, 5: 1},
        compiler_params=_params("arbitrary"))(pair, h, *operands, z, w_pairs)


def _qk_prep(z, ra, rbm, rbp, gq2, gk2, tm, after):
    s = z.shape[0]

    def body(z_ref, a_ref, bm_ref, bp_ref, gq_ref, gk_ref, after_ref, q_ref, k2_ref, v2_ref):
        a, bm, bp = a_ref[...], bm_ref[...], bp_ref[...]
        for r in range(ATTN_W // LANES):
            x = z_ref[:, LANES * r:LANES * (r + 1)]
            rr = lax.rsqrt(_half_sums(x * x) * (1.0 / HEAD) + EPS)
            q_ref[:, LANES * r:LANES * (r + 1)] = _rope(x * rr * gq_ref[...], a, bm, bp).astype(BF16)
        for m in range(KV_W // LANES):
            x = z_ref[:, ATTN_W + LANES * m:ATTN_W + LANES * (m + 1)]
            rr = lax.rsqrt(_half_sums(x * x) * (1.0 / HEAD) + EPS)
            k_lo, k_hi = _dup_halves(_rope(x * rr * gk_ref[...], a, bm, bp))
            k2_ref[:, 2 * LANES * m:2 * LANES * m + LANES] = k_lo.astype(BF16)
            k2_ref[:, 2 * LANES * m + LANES:2 * LANES * (m + 1)] = k_hi.astype(BF16)
            v_lo, v_hi = _dup_halves(z_ref[:, ATTN_W + KV_W + LANES * m:ATTN_W + KV_W + LANES * (m + 1)])
            v2_ref[:, 2 * LANES * m:2 * LANES * m + LANES] = v_lo.astype(BF16)
            v2_ref[:, 2 * LANES * m + LANES:2 * LANES * (m + 1)] = v_hi.astype(BF16)

    row = lambda w: pl.BlockSpec((tm, w), lambda i: (i, 0))
    one = pl.BlockSpec((1, LANES), lambda i: (0, 0))
    return pl.pallas_call(
        body, name="qk_prep",
        out_shape=(jax.ShapeDtypeStruct((s, ATTN_W), BF16), jax.ShapeDtypeStruct((s, K2_W), BF16),
                   jax.ShapeDtypeStruct((s, K2_W), BF16)),
        grid=(s // tm,),
        in_specs=[row(PAIR_W), row(LANES), row(LANES), row(LANES), one, one, ANY],
        out_specs=(row(ATTN_W), row(K2_W), row(K2_W)),
        compiler_params=_params("parallel"))(z, ra, rbm, rbp, gq2, gk2, after)


GROUP = 4


def _window_mask(n):
    row = lax.broadcasted_iota(jnp.int32, (GROUP * BLK, 2 * BLK), 0) % BLK
    col = lax.broadcasted_iota(jnp.int32, (GROUP * BLK, 2 * BLK), 1)
    return (col > row) & (col <= row + BLK) & ((col >= BLK) | (n > 0))


def _stack_heads(pairs, zero):
    lo = _low_half(pairs[0].shape)
    parts = []
    for v in pairs:
        parts += [jnp.where(lo, v, zero), jnp.where(lo, zero, v)]
    return jnp.concatenate(parts, axis=0)


def _unstack_heads(v4):
    lo = _low_half((BLK, LANES))
    return [jnp.where(lo, v4[2 * i * BLK:(2 * i + 1) * BLK], v4[(2 * i + 1) * BLK:(2 * i + 2) * BLK]) for i in range(2)]


def _group_sinks(sink_ref, kvh):
    slot = lax.broadcasted_iota(jnp.int32, (GROUP * BLK, 1), 0) // BLK
    col = jnp.zeros((GROUP * BLK, 1), F32)
    for i in range(GROUP):
        col = jnp.where(slot == i, sink_ref[0, GROUP * kvh + i], col)
    return col, slot


def _head_probs(qm, kw, valid, sink):
    sc = jnp.where(valid, _dot_nt(qm, kw) * (HEAD ** -0.5), NEG_INF)
    mx = jnp.maximum(jnp.max(sc, axis=-1, keepdims=True), sink)
    ex = jnp.exp(sc - mx)
    den = jnp.sum(ex, axis=-1, keepdims=True) + jnp.exp(sink - mx)
    return ex / den, mx, den


def _cols(start, width=ATTN_W):
    return slice(start, start + width)


def _conv_fwd(z_ref, zp_ref, cw_ref, ext_ref, n):
    u = z_ref[:, _cols(CONV_C0)] * z_ref[:, _cols(CONV_H0)]
    pu = zp_ref[:, _cols(CONV_C0)] * zp_ref[:, _cols(CONV_H0)]
    ext_ref[0:SUBLANES, :] = jnp.where(n > 0, pu, 0.0)
    ext_ref[SUBLANES:SUBLANES + BLK, :] = u
    um1 = ext_ref[SUBLANES - 1:SUBLANES - 1 + BLK, :]
    um2 = ext_ref[SUBLANES - 2:SUBLANES - 2 + BLK, :]
    cv = cw_ref[0:1, :] * um2 + cw_ref[1:2, :] * um1 + cw_ref[2:3, :] * u
    return u, um1, um2, cv


def _prev_rows(n):
    return (jnp.maximum(n * (BLK // SUBLANES) - 1, 0), 0)


def _attn_fwd(qn, k2, v2, z, conv_wp, sinks, after):
    s = qn.shape[0]
    nb = s // BLK

    def body(sink_ref, q_ref, kc_ref, kp_ref, vc_ref, vp_ref, z_ref, zp_ref, cw_ref, after_ref, a_ref, mix_ref,
             mixt_ref, ext_ref):
        n = pl.program_id(0)
        valid = _window_mask(n)
        for kvh in range(K2_W // LANES):
            cols = slice(LANES * kvh, LANES * (kvh + 1))
            kw = jnp.concatenate([kp_ref[:, cols], kc_ref[:, cols]], axis=0)
            vw = jnp.concatenate([vp_ref[:, cols], vc_ref[:, cols]], axis=0)
            blocks = [slice(LANES * r, LANES * (r + 1)) for r in (2 * kvh, 2 * kvh + 1)]
            q4 = _stack_heads([q_ref[:, rc] for rc in blocks], jnp.zeros((BLK, LANES), BF16))
            p, _, _ = _head_probs(q4, kw, valid, _group_sinks(sink_ref, kvh)[0])
            for rc, a in zip(blocks, _unstack_heads(_dot(p.astype(BF16), vw))):
                a_ref[:, rc] = a
                g = z_ref[:, _cols(GATE_A0 + rc.start, LANES)]
                mix_ref[:, rc] = (a * (g * _sig(g))).astype(BF16)
        _, _, _, cv = _conv_fwd(z_ref, zp_ref, cw_ref, ext_ref, n)
        gc = z_ref[:, _cols(GATE_C0)]
        mix_ref[:, ATTN_W:D_MODEL] = (z_ref[:, _cols(CONV_B0)] * cv * (gc * _sig(gc))).astype(BF16)
        mixt_ref[...] = mix_ref[...].T

    cur = lambda w: pl.BlockSpec((BLK, w), lambda n: (n, 0))
    prev = lambda w: pl.BlockSpec((BLK, w), lambda n: (jnp.maximum(n - 1, 0), 0))
    return pl.pallas_call(
        body, name="attn_fwd",
        out_shape=(jax.ShapeDtypeStruct((s, ATTN_W), F32), jax.ShapeDtypeStruct((s, D_MODEL), BF16),
                   jax.ShapeDtypeStruct((D_MODEL, s), BF16)),
        grid=(nb,),
        in_specs=[pl.BlockSpec(memory_space=pltpu.SMEM),
                  cur(ATTN_W), cur(K2_W), prev(K2_W), cur(K2_W), prev(K2_W), cur(IN_W),
                  pl.BlockSpec((SUBLANES, IN_W), _prev_rows),
                  pl.BlockSpec((SUBLANES, ATTN_W), lambda n: (0, 0)), ANY],
        out_specs=(cur(ATTN_W), cur(D_MODEL), pl.BlockSpec((D_MODEL, BLK), lambda n: (0, n))),
        scratch_shapes=[pltpu.VMEM((BLK + 2 * SUBLANES, ATTN_W), F32)],
        compiler_params=_params("parallel"))(sinks, qn, k2, k2, v2, v2, z, z, conv_wp, after)


def _fwd_out(mix, w_out, x, g2, tm):
    s = x.shape[0]

    def body(m_ref, w_ref, x_ref, g_ref, x1_ref, h_ref, ht_ref):
        x1 = x_ref[...] + _dot(m_ref[...], w_ref[...])
        x1_ref[...] = x1
        xn, _ = _rms(x1)
        h = (xn * g_ref[...]).astype(BF16)
        h_ref[...] = h
        ht_ref[...] = h.T

    row = pl.BlockSpec((tm, D_MODEL), lambda i: (i, 0))
    return pl.pallas_call(
        body, name="fwd_out",
        out_shape=(jax.ShapeDtypeStruct((s, D_MODEL), F32), jax.ShapeDtypeStruct((s, D_MODEL), BF16),
                   jax.ShapeDtypeStruct((D_MODEL, s), BF16)),
        grid=(s // tm,),
        in_specs=[row, _resident((D_MODEL, D_MODEL)), row, pl.BlockSpec((1, D_MODEL), lambda i: (0, 0))],
        out_specs=(row, row, pl.BlockSpec((D_MODEL, tm), lambda i: (0, i))),
        compiler_params=_params("parallel"))(mix, w_out, x, g2)


def _ple(hn2, w_pg, b_pg, p, w_pp, g3, x1, target, tm):
    s = x1.shape[0]

    def body(h_ref, wg_ref, b_ref, p_ref, wp_ref, g3_ref, x1_ref, t_ref, dy_ref, dgp_ref, dt_ref, pt_ref, acc_ref):
        gate = _sig(_dot(h_ref[...], wg_ref[...]) + b_ref[...])
        pb = p_ref[...].astype(BF16)
        pt_ref[...] = pb.T
        t = _dot(pb, wp_ref[...])
        tn, r3 = _rms(t)
        e = tn * g3_ref[...]
        diff = x1_ref[...] + gate * e - t_ref[...]
        dy = diff * (1.0 / D_MODEL)
        dy_ref[...] = dy
        dgp = dy * e * (gate * (1.0 - gate))
        dgp_ref[...] = dgp.astype(BF16)
        de = dy * gate
        dt_ref[...] = _rms_bwd(de * g3_ref[...], tn, r3).astype(BF16)

        @pl.when(pl.program_id(0) == 0)
        def _():
            acc_ref[...] = jnp.zeros_like(acc_ref)

        acc_ref[0:1, :] += jnp.sum(dgp, axis=0, keepdims=True)
        acc_ref[1:2, :] += jnp.sum(de * tn, axis=0, keepdims=True)
        acc_ref[2:3, :] += jnp.sum(diff * diff, axis=0, keepdims=True) * (0.5 / D_MODEL)

    row = pl.BlockSpec((tm, D_MODEL), lambda i: (i, 0))
    vec = pl.BlockSpec((1, D_MODEL), lambda i: (0, 0))
    return pl.pallas_call(
        body, name="ple",
        out_shape=(jax.ShapeDtypeStruct((s, D_MODEL), F32), jax.ShapeDtypeStruct((s, D_MODEL), BF16),
                   jax.ShapeDtypeStruct((s, D_MODEL), BF16), jax.ShapeDtypeStruct((PLE_DIM, s), BF16),
                   jax.ShapeDtypeStruct((SUBLANES, D_MODEL), F32)),
        grid=(s // tm,),
        in_specs=[row, _resident((D_MODEL, D_MODEL)), vec, pl.BlockSpec((tm, PLE_DIM), lambda i: (i, 0)),
                  _resident((PLE_DIM, D_MODEL)), vec, row, row],
        out_specs=(row, row, row, pl.BlockSpec((PLE_DIM, tm), lambda i: (0, i)),
                   pl.BlockSpec((SUBLANES, D_MODEL), lambda i: (0, 0))),
        compiler_params=_params("arbitrary"))(hn2, w_pg, b_pg, p, w_pp, g3, x1, target)


def _gate_bwd(dgp, w_pg, x1, dy, g2, tm):
    s = x1.shape[0]

    def body(d_ref, w_ref, x1_ref, dy_ref, g_ref, dx_ref, dxb_ref, acc_ref):
        dh = _dot_nt(d_ref[...], w_ref[...])
        xn, r = _rms(x1_ref[...])
        dx1 = dy_ref[...] + _rms_bwd(dh * g_ref[...], xn, r)
        dx_ref[...] = dx1
        dxb_ref[...] = dx1.astype(BF16)

        @pl.when(pl.program_id(0) == 0)
        def _():
            acc_ref[...] = jnp.zeros_like(acc_ref)

        acc_ref[0:1, :] += jnp.sum(dh * xn, axis=0, keepdims=True)

    row = pl.BlockSpec((tm, D_MODEL), lambda i: (i, 0))
    return pl.pallas_call(
        body, name="gate_bwd",
        out_shape=(jax.ShapeDtypeStruct((s, D_MODEL), F32), jax.ShapeDtypeStruct((s, D_MODEL), BF16),
                   jax.ShapeDtypeStruct((SUBLANES, D_MODEL), F32)),
        grid=(s // tm,),
        in_specs=[row, _resident((D_MODEL, D_MODEL)), row, row, pl.BlockSpec((1, D_MODEL), lambda i: (0, 0))],
        out_specs=(row, row, pl.BlockSpec((SUBLANES, D_MODEL), lambda i: (0, 0))),
        compiler_params=_params("arbitrary"))(dgp, w_pg, x1, dy, g2)


def _mm_nt(a, b, tm, name, after):
    m, k = a.shape
    n = b.shape[0]

    def body(a_ref, b_ref, after_ref, o_ref):
        o_ref[...] = _dot_nt(a_ref[...], b_ref[...])

    return pl.pallas_call(
        body, name=name,
        out_shape=jax.ShapeDtypeStruct((m, n), F32),
        grid=(m // tm,),
        in_specs=[pl.BlockSpec((tm, k), lambda i: (i, 0)), _resident((n, k)), ANY],
        out_specs=pl.BlockSpec((tm, n), lambda i: (i, 0)),
        compiler_params=_params("parallel"))(a, b, after)


def _attn_bwd(qn, k2, v2, a, z, dmix, conv_wp, sinks, after):
    s = qn.shape[0]
    nb = s // BLK

    def body(sink_ref, q_ref, kc_ref, kp_ref, vc_ref, vp_ref, a_ref, z_ref, zp_ref, zn_ref, dm_ref, dmn_ref,
             cw_ref, after_ref, dq_ref, dkc_ref, dkp_ref, dvc_ref, dvp_ref, dz_ref, dzt_ref, acc_ref, ext_ref):
        n = pl.program_id(0)
        valid = _window_mask(n)
        lane = lax.broadcasted_iota(jnp.int32, (1, ATTN_W), 1)

        @pl.when(n == 0)
        def _():
            acc_ref[...] = jnp.zeros_like(acc_ref)

        dz_ref[:, 0:QKV_W] = jnp.zeros((BLK, QKV_W), BF16)
        dsink = jnp.zeros((1, ATTN_W), F32)
        for kvh in range(K2_W // LANES):
            cols = slice(LANES * kvh, LANES * (kvh + 1))
            kw = jnp.concatenate([kp_ref[:, cols], kc_ref[:, cols]], axis=0)
            vw = jnp.concatenate([vp_ref[:, cols], vc_ref[:, cols]], axis=0)
            blocks = [slice(LANES * r, LANES * (r + 1)) for r in (2 * kvh, 2 * kvh + 1)]
            das, avs = [], []
            for rc in blocks:
                g = z_ref[:, _cols(GATE_A0 + rc.start, LANES)]
                sg = _sig(g)
                dm = dm_ref[:, rc]
                av = a_ref[:, rc]
                das.append(dm * (g * sg))
                avs += [av, av]
                dz_ref[:, _cols(GATE_A0 + rc.start, LANES)] = (dm * av * _dsilu(g, sg)).astype(BF16)
            q4 = _stack_heads([q_ref[:, rc] for rc in blocks], jnp.zeros((BLK, LANES), BF16))
            sink, slot = _group_sinks(sink_ref, kvh)
            p, mx, den = _head_probs(q4, kw, valid, sink)
            do4 = _stack_heads(das, 0.0)
            delta = jnp.sum(do4 * jnp.concatenate(avs, axis=0), axis=-1, keepdims=True)
            dob = do4.astype(BF16)
            ds = p * (_dot_nt(dob, vw) - delta) * (HEAD ** -0.5)
            for rc, dq in zip(blocks, _unstack_heads(_dot(ds.astype(BF16), kw))):
                dq_ref[:, rc] = dq
            dk2 = _dot(ds.T.astype(BF16), q4)
            dv2 = _dot(p.T.astype(BF16), dob)
            dkp_ref[:, cols] = dk2[0:BLK]
            dkc_ref[:, cols] = dk2[BLK:2 * BLK]
            dvp_ref[:, cols] = dv2[0:BLK]
            dvc_ref[:, cols] = dv2[BLK:2 * BLK]
            dsk = jnp.exp(sink - mx) / den * delta
            for i in range(GROUP):
                dsink = dsink - jnp.where(lane == GROUP * kvh + i,
                                          jnp.sum(jnp.where(slot == i, dsk, 0.0), axis=0, keepdims=True), 0.0)
        acc_ref[0:1, :] += dsink

        u, um1, um2, cv = _conv_fwd(z_ref, zp_ref, cw_ref, ext_ref, n)
        cb = z_ref[:, _cols(CONV_B0)]
        gc = z_ref[:, _cols(GATE_C0)]
        sgc = _sig(gc)
        dmc = dm_ref[:, ATTN_W:D_MODEL]
        t = dmc * (gc * sgc)
        dcv = t * cb
        dz_ref[:, _cols(CONV_B0)] = (t * cv).astype(BF16)
        dz_ref[:, _cols(GATE_C0)] = (dmc * cb * cv * _dsilu(gc, sgc)).astype(BF16)
        gcn = zn_ref[:, _cols(GATE_C0)]
        dcvn = dmn_ref[:, ATTN_W:D_MODEL] * (gcn * _sig(gcn)) * zn_ref[:, _cols(CONV_B0)]
        ext_ref[0:BLK, :] = dcv
        ext_ref[BLK:BLK + SUBLANES, :] = jnp.where(n < nb - 1, dcvn, 0.0)
        du = (cw_ref[2:3, :] * dcv + cw_ref[1:2, :] * ext_ref[1:1 + BLK, :]
              + cw_ref[0:1, :] * ext_ref[2:2 + BLK, :])
        dz_ref[:, _cols(CONV_C0)] = (du * z_ref[:, _cols(CONV_H0)]).astype(BF16)
        dz_ref[:, _cols(CONV_H0)] = (du * z_ref[:, _cols(CONV_C0)]).astype(BF16)
        acc_ref[1:2, :] += jnp.sum(dcv * um2, axis=0, keepdims=True)
        acc_ref[2:3, :] += jnp.sum(dcv * um1, axis=0, keepdims=True)
        acc_ref[3:4, :] += jnp.sum(dcv * u, axis=0, keepdims=True)
        dzt_ref[...] = dz_ref[...].T

    cur = lambda w: pl.BlockSpec((BLK, w), lambda n: (n, 0))
    prev = lambda w: pl.BlockSpec((BLK, w), lambda n: (jnp.maximum(n - 1, 0), 0))
    nxt = lambda w: pl.BlockSpec(
        (SUBLANES, w), lambda n: (jnp.minimum((n + 1) * (BLK // SUBLANES), nb * (BLK // SUBLANES) - 1), 0))
    f32 = lambda w: jax.ShapeDtypeStruct((s, w), F32)
    return pl.pallas_call(
        body, name="attn_bwd",
        out_shape=(f32(ATTN_W), f32(K2_W), f32(K2_W), f32(K2_W), f32(K2_W),
                   jax.ShapeDtypeStruct((s, IN_W), BF16), jax.ShapeDtypeStruct((IN_W, s), BF16),
                   jax.ShapeDtypeStruct((SUBLANES, ATTN_W), F32)),
        grid=(nb,),
        in_specs=[pl.BlockSpec(memory_space=pltpu.SMEM),
                  cur(ATTN_W), cur(K2_W), prev(K2_W), cur(K2_W), prev(K2_W), cur(ATTN_W), cur(IN_W),
                  pl.BlockSpec((SUBLANES, IN_W), _prev_rows), nxt(IN_W), cur(D_MODEL), nxt(D_MODEL),
                  pl.BlockSpec((SUBLANES, ATTN_W), lambda n: (0, 0)), ANY],
        out_specs=(cur(ATTN_W), cur(K2_W), cur(K2_W), cur(K2_W), cur(K2_W), cur(IN_W),
                   pl.BlockSpec((IN_W, BLK), lambda n: (0, n)), pl.BlockSpec((SUBLANES, ATTN_W), lambda n: (0, 0))),
        scratch_shapes=[pltpu.VMEM((BLK + 2 * SUBLANES, ATTN_W), F32)],
        compiler_params=_params("arbitrary"))(sinks, qn, k2, k2, v2, v2, a, z, z, z, dmix, dmix, conv_wp, after)


def _qkv_bwd(z, dz, dzt, dq, dkc, dkp, dvc, dvp, ra, rbm, rbp, gq2, gk2):
    s = z.shape[0]
    nb = s // BLK

    def body(z_ref, dz_in, dzt_in, dq_ref, dkc_ref, dkp_ref, dvc_ref, dvp_ref, a_ref, bm_ref, bp_ref, gq_ref, gk_ref,
             dz_ref, dzt_ref, acc_ref):
        n = pl.program_id(0)
        a, bm, bp = a_ref[...], bm_ref[...], bp_ref[...]
        lo = _low_half((BLK, LANES))
        last = n == nb - 1

        @pl.when(n == 0)
        def _():
            acc_ref[...] = jnp.zeros_like(acc_ref)

        def norm_bwd(x, dy, gain):
            rr = lax.rsqrt(_half_sums(x * x) * (1.0 / HEAD) + EPS)
            xh = x * rr
            dxg = _rope_t(dy, a, bm, bp)
            dxh = dxg * gain
            dx = rr * (dxh - xh * (_half_sums(dxh * xh) * (1.0 / HEAD)))
            return dx, jnp.sum(dxg * xh, axis=0, keepdims=True)

        def folded(cur_ref, prev_ref, m):
            parts = []
            for h in (2 * m, 2 * m + 1):
                v = cur_ref[:, LANES * h:LANES * (h + 1)] + jnp.where(
                    last, 0.0, prev_ref[:, LANES * h:LANES * (h + 1)])
                parts.append(v + pltpu.roll(v, HEAD, 1))
            return jnp.where(lo, parts[0], parts[1])

        gq_acc = jnp.zeros((1, LANES), F32)
        for r in range(ATTN_W // LANES):
            rc = slice(LANES * r, LANES * (r + 1))
            dx, gg = norm_bwd(z_ref[:, rc], dq_ref[:, rc], gq_ref[...])
            dz_ref[:, rc] = dx.astype(BF16)
            gq_acc = gq_acc + gg
        acc_ref[0:1, :] += gq_acc
        gk_acc = jnp.zeros((1, LANES), F32)
        for m in range(KV_W // LANES):
            kc = slice(ATTN_W + LANES * m, ATTN_W + LANES * (m + 1))
            dx, gg = norm_bwd(z_ref[:, kc], folded(dkc_ref, dkp_ref, m), gk_ref[...])
            dz_ref[:, kc] = dx.astype(BF16)
            gk_acc = gk_acc + gg
            vc = slice(ATTN_W + KV_W + LANES * m, ATTN_W + KV_W + LANES * (m + 1))
            dz_ref[:, vc] = folded(dvc_ref, dvp_ref, m).astype(BF16)
        acc_ref[1:2, :] += gk_acc
        dzt_ref[...] = dz_ref[...].T

    cur = lambda w: pl.BlockSpec((BLK, w), lambda n: (n, 0))
    nxt = lambda w: pl.BlockSpec((BLK, w), lambda n: (jnp.minimum(n + 1, nb - 1), 0))
    one = pl.BlockSpec((1, LANES), lambda n: (0, 0))
    return pl.pallas_call(
        body, name="qkv_bwd",
        out_shape=(jax.ShapeDtypeStruct(dz.shape, dz.dtype), jax.ShapeDtypeStruct(dzt.shape, dzt.dtype),
                   jax.ShapeDtypeStruct((SUBLANES, LANES), F32)),
        grid=(nb,),
        in_specs=[cur(PAIR_W), ANY, ANY, cur(ATTN_W), cur(K2_W), nxt(K2_W), cur(K2_W), nxt(K2_W),
                  cur(LANES), cur(LANES), cur(LANES), one, one],
        out_specs=(cur(QKV_W), pl.BlockSpec((QKV_W, BLK), lambda n: (0, n)),
                   pl.BlockSpec((SUBLANES, LANES), lambda n: (0, 0))),
        input_output_aliases={1: 0, 2: 1},
        compiler_params=_params("arbitrary"))(z, dz, dzt, dq, dkc, dkp, dvc, dvp, ra, rbm, rbp, gq2, gk2)


def _in_bwd(dz, w_pairs, x, dx1, g1, tm, after):
    s = x.shape[0]
    n = s // tm
    sub = tm // N_PAIRS
    stripes = 4

    def body(d_ref, w_ref, x_ref, dx1_ref, g_ref, after_ref, gx_ref, acc_ref, dh_ref):
        i, k = pl.program_id(0), pl.program_id(1)

        def matmul(c):
            cols = slice(c * (D_MODEL // stripes), (c + 1) * (D_MODEL // stripes))
            dh_ref[i % 2, :, cols] += _dot(d_ref[...], w_ref[0, :, cols])

        def norm_bwd(c):
            part = sub // stripes
            mine = slice(c * part, (c + 1) * part)
            rows = pl.ds(pl.multiple_of(k * sub + c * part, part), part)
            dh = dh_ref[(i + 1) % 2, rows, :]
            dh_ref[(i + 1) % 2, rows, :] = jnp.zeros_like(dh)
            xn, r = _rms(x_ref[mine, :])
            gx_ref[rows, :] = dx1_ref[mine, :] + _rms_bwd(dh * g_ref[...], xn, r)
            acc_ref[0:1, :] += jnp.sum(dh * xn, axis=0, keepdims=True)

        @pl.when((i == 0) & (k == 0))
        def _():
            acc_ref[...] = jnp.zeros_like(acc_ref)
            dh_ref[...] = jnp.zeros_like(dh_ref)

        @pl.when(i == 0)
        def _():
            for c in range(stripes):
                matmul(c)

        @pl.when((i > 0) & (i < n))
        def _():
            for c in range(stripes):
                matmul(c)
                norm_bwd(c)

        @pl.when(i == n)
        def _():
            for c in range(stripes):
                norm_bwd(c)

    last = lambda i, k: jnp.where(i == n, N_PAIRS - 1, k)
    rows_before = lambda i, k: (jnp.maximum(i - 1, 0) * N_PAIRS + k, 0)
    return pl.pallas_call(
        body, name="in_bwd",
        out_shape=(jax.ShapeDtypeStruct((s, D_MODEL), F32), jax.ShapeDtypeStruct((SUBLANES, D_MODEL), F32)),
        grid=(n + 1, N_PAIRS),
        in_specs=[pl.BlockSpec((tm, PAIR_W), lambda i, k: (jnp.minimum(i, n - 1), last(i, k))),
                  pl.BlockSpec((1, PAIR_W, D_MODEL), lambda i, k: (last(i, k), 0, 0)),
                  pl.BlockSpec((sub, D_MODEL), rows_before), pl.BlockSpec((sub, D_MODEL), rows_before),
                  pl.BlockSpec((1, D_MODEL), lambda i, k: (0, 0)), ANY],
        out_specs=(pl.BlockSpec((tm, D_MODEL), lambda i, k: (jnp.maximum(i - 1, 0), 0)),
                   pl.BlockSpec((SUBLANES, D_MODEL), lambda i, k: (0, 0))),
        scratch_shapes=[pltpu.VMEM((2, tm, D_MODEL), F32)],
        compiler_params=_params("arbitrary", "arbitrary"))(dz, w_pairs, x, dx1, g1, after)


def _mm_grad(at, bs, tn, name):
    m, kdim = at.shape
    nblk = [b.shape[1] // tn for b in bs]
    starts = [sum(nblk[:t]) for t in range(len(bs))]

    def body(a_ref, *refs):
        b_refs, o_ref = refs[:len(bs)], refs[len(bs)]
        j = pl.program_id(0)
        for t, b_ref in enumerate(b_refs):
            @pl.when((j >= starts[t]) & (j < starts[t] + nblk[t]))
            def _():
                o_ref[...] = _dot(a_ref[...], b_ref[...]).astype(BF16)

    def b_spec(t):
        return pl.BlockSpec((kdim, tn), lambda j: (0, jnp.clip(j - starts[t], 0, nblk[t] - 1)))

    return pl.pallas_call(
        body, name=name,
        out_shape=jax.ShapeDtypeStruct((m, sum(nblk) * tn), BF16),
        grid=(sum(nblk),),
        in_specs=[_resident((m, kdim))] + [b_spec(t) for t in range(len(bs))],
        out_specs=pl.BlockSpec((m, tn), lambda j: (0, j)),
        compiler_params=_params("parallel"))(at, *bs)


def _grad_w_in(dzt, h):
    kdim = h.shape[0]

    def body(d_ref, h_ref, o_ref):
        o_ref[0] = _dot(d_ref[...], h_ref[...]).astype(BF16)

    return pl.pallas_call(
        body, name="grad_w_in",
        out_shape=jax.ShapeDtypeStruct((N_DEV, SHARD_IN, D_MODEL), BF16),
        grid=(N_DEV,),
        in_specs=[pl.BlockSpec((SHARD_IN, kdim), lambda j: (j, 0)), _resident((kdim, D_MODEL))],
        out_specs=pl.BlockSpec((1, SHARD_IN, D_MODEL), lambda j: (j, 0, 0)),
        compiler_params=_params("parallel"))(dzt, h)


def _place():
    return lax.axis_index("x"), lax.axis_index("y"), lax.axis_index("c")


ROW_TAPS, ROW_MISC = 4, 5
Q_AT, K_AT, SINK_AT, LOSS_AT = (ATTN_W + LANES * t for t in range(4))
SMALL_AT = [(0, 0), (1, 0), (2, 0), (3, 0), (ROW_MISC, Q_AT), (ROW_MISC, K_AT), (ROW_MISC, SINK_AT)]


def _tap_at(tap):
    return ROW_TAPS + tap // 2, ATTN_W * (tap % 2)


def _reduce_small(acc_g1, acc_g2, acc_ple, acc_qk, acc_attn):
    def body(g1_ref, g2_ref, ple_ref, qk_ref, attn_ref, out_ref, slab_ref, gath_ref, send_sems, recv_sems):
        x, y, c = _place()
        me = 4 * x + 2 * y + c
        slab_ref[...] = jnp.zeros_like(slab_ref)
        slab_ref[0:1, :] = g1_ref[0:1, :]
        slab_ref[1:2, :] = g2_ref[0:1, :]
        slab_ref[2:4, :] = ple_ref[0:2, :]
        qk = qk_ref[0:2, :]
        qk = jnp.where(_low_half(qk.shape), qk + pltpu.roll(qk, HEAD, 1), 0.0)
        misc = slab_ref.at[ROW_MISC:ROW_MISC + 1]
        misc[:, Q_AT:Q_AT + LANES] = qk[0:1]
        misc[:, K_AT:K_AT + LANES] = qk[1:2]
        lane = lax.broadcasted_iota(jnp.int32, (1, LANES), 1)
        misc[:, SINK_AT:SINK_AT + LANES] = jnp.where(lane < N_Q_HEADS, attn_ref[0:1, 0:LANES], 0.0)
        misc[:, LOSS_AT:LOSS_AT + LANES] = sum(
            ple_ref[2:3, LANES * t:LANES * (t + 1)] for t in range(D_MODEL // LANES))
        for tap in range(3):
            row, at = _tap_at(tap)
            slab_ref[row:row + 1, at:at + ATTN_W] = attn_ref[1 + tap:2 + tap, :]
        gath_ref[me] = slab_ref[...]
        copies = []
        for k in range(1, N_DEV):
            peer = (x ^ (k >> 2), y ^ ((k >> 1) & 1), c ^ (k & 1))
            copies.append(pltpu.make_async_remote_copy(
                src_ref=slab_ref, dst_ref=gath_ref.at[me], send_sem=send_sems.at[k - 1],
                recv_sem=recv_sems.at[k - 1], device_id=peer, device_id_type=MESH))
        for cp in copies:
            cp.start()
        for cp in copies:
            cp.wait_recv()
        for cp in copies:
            cp.wait_send()
        total = gath_ref[0]
        for d in range(1, N_DEV):
            total = total + gath_ref[d]
        out_ref[...] = total

    vmem = pl.BlockSpec(memory_space=pltpu.VMEM)
    return pl.pallas_call(
        body, name="reduce_small",
        out_shape=jax.ShapeDtypeStruct((SLAB_ROWS, D_MODEL), F32),
        in_specs=[vmem] * 5, out_specs=vmem,
        scratch_shapes=[pltpu.VMEM((SLAB_ROWS, D_MODEL), F32), pltpu.VMEM((N_DEV, SLAB_ROWS, D_MODEL), F32),
                        pltpu.SemaphoreType.DMA((N_DEV - 1,)), pltpu.SemaphoreType.DMA((N_DEV - 1,))])(
            acc_g1, acc_g2, acc_ple, acc_qk, acc_attn)


def _pair_sum(g, r, place, tr, name):
    _, _, rows, cols = g.shape

    def body(place_ref, g_ref, r_ref, pb_ref, own_ref):
        tot = g_ref[0, 0].astype(F32) + r_ref[0].astype(F32)
        pb_ref[0] = tot.astype(BF16)

        @pl.when(pl.program_id(1) == place_ref[1])
        def _():
            own_ref[...] = tot

    grid_spec = pltpu.PrefetchScalarGridSpec(
        num_scalar_prefetch=1, grid=(rows // tr, 4),
        in_specs=[pl.BlockSpec((1, 1, tr, cols), lambda i, q, place_ref: (q, place_ref[0], i, 0)),
                  pl.BlockSpec((1, tr, cols), lambda i, q, place_ref: (q, i, 0))],
        out_specs=(pl.BlockSpec((1, tr, cols), lambda i, q, place_ref: (q, i, 0)),
                   pl.BlockSpec((tr, cols), lambda i, q, place_ref: (i, 0))))
    return pl.pallas_call(
        body, name=name, grid_spec=grid_spec,
        out_shape=(jax.ShapeDtypeStruct((4, rows, cols), BF16), jax.ShapeDtypeStruct((rows, cols), F32)),
        compiler_params=_params("arbitrary", "arbitrary"))(place, g, r)


HBM = pl.BlockSpec(memory_space=pltpu.HBM)
SEM = pl.BlockSpec(memory_space=pltpu.SEMAPHORE)
SIDE_EFFECT = pltpu.CompilerParams(has_side_effects=pltpu.SideEffectType.DATAFLOW_SIDE_EFFECTING)
TOKEN = jax.ShapeDtypeStruct((SUBLANES, LANES), F32)


def _hbm(a):
    return pltpu.with_memory_space_constraint(a, pltpu.HBM)


def _hbm_like(arrays):
    return tuple(pltpu.HBM(a.shape, a.dtype) for a in arrays)


def _block_of(px, py, pc):
    return 4 * px + 2 * py + pc


def _gather_start(shards, after):
    na = len(shards)
    lands = [_hbm(lax.empty((N_DEV,) + a.shape, a.dtype)) for a in shards]

    def body(*refs):
        ins, land = refs[:na], refs[na:2 * na]
        send_sems, recv_ici, recv_d2d = refs[2 * na + 1:2 * na + 4]
        token = refs[-1]
        x, y, c = _place()
        for k, peer in enumerate([(x, y, 1 - c), (1 - x, y, c), (x, 1 - y, c), (1 - x, 1 - y, c)]):
            for t in range(na):
                pltpu.make_async_remote_copy(
                    src_ref=ins[t], dst_ref=land[t].at[_block_of(x, y, c)], send_sem=send_sems.at[4 * t + k],
                    recv_sem=recv_d2d.at[4 * t] if k == 0 else recv_ici.at[3 * t + k - 1],
                    device_id=peer, device_id_type=MESH).start()
        token[...] = jnp.zeros_like(token)

    out = pl.pallas_call(
        body, name="gather_start",
        out_shape=(pltpu.SemaphoreType.DMA((4 * na,)), pltpu.SemaphoreType.DMA((3 * na,)),
                   pltpu.SemaphoreType.DMA((4 * na,)), *_hbm_like(lands), TOKEN),
        in_specs=[ANY] * na + [HBM] * na + [ANY],
        out_specs=(SEM, SEM, SEM, *[HBM] * na, pl.BlockSpec(memory_space=pltpu.VMEM)),
        input_output_aliases={na + i: 3 + i for i in range(na)},
        compiler_params=SIDE_EFFECT)(*shards, *lands, after)
    send_sems, recv_ici, recv_d2d = out[:3]
    state = dict(send=send_sems, ici=recv_ici, d2d=recv_d2d, shards=list(shards), lands=out[3:3 + na])
    return state, out[-1]


def _gather_forward(state, after):
    lands = state["lands"]
    na = len(lands)

    def body(*refs):
        land = refs[:na]
        recv_ici, recv_d2d = refs[na], refs[na + 1]
        fwd_sems, token = refs[-2], refs[-1]
        x, y, c = _place()
        for j, chip in enumerate([(1 - x, y), (x, 1 - y), (1 - x, 1 - y)]):
            for t in range(na):
                blk = land[t].at[_block_of(*chip, c)]
                pltpu.make_async_remote_copy(
                    src_ref=blk, dst_ref=blk, send_sem=fwd_sems.at[3 * t + j], recv_sem=recv_ici.at[3 * t + j],
                    device_id=(x, y, c), device_id_type=MESH).wait_recv()
                pltpu.make_async_remote_copy(
                    src_ref=blk, dst_ref=blk, send_sem=fwd_sems.at[3 * t + j], recv_sem=recv_d2d.at[4 * t + 1 + j],
                    device_id=(x, y, 1 - c), device_id_type=MESH).start()
        token[...] = jnp.zeros_like(token)

    out = pl.pallas_call(
        body, name="gather_forward",
        out_shape=(*_hbm_like(lands), pltpu.SemaphoreType.DMA((3 * na,)), TOKEN),
        in_specs=[HBM] * na + [SEM, SEM, ANY],
        out_specs=(*[HBM] * na, SEM, pl.BlockSpec(memory_space=pltpu.VMEM)),
        input_output_aliases={i: i for i in range(na)},
        compiler_params=SIDE_EFFECT)(*lands, state["ici"], state["d2d"], after)
    return dict(state, lands=out[:na], fwd=out[na]), out[-1]


def _gather_wait(state, after):
    shards, lands = state["shards"], state["lands"]
    na = len(lands)

    def body(*refs):
        ins, land = refs[:na], refs[na:2 * na]
        send_sems, fwd_sems, recv_d2d = refs[2 * na:2 * na + 3]
        x, y, c = _place()
        chips = [(1 - x, y), (x, 1 - y), (1 - x, 1 - y)]
        for t in range(na):
            mine = land[t].at[_block_of(x, y, c)]
            for k in range(4):
                pltpu.make_async_remote_copy(
                    src_ref=ins[t], dst_ref=mine, send_sem=send_sems.at[4 * t + k], recv_sem=recv_d2d.at[4 * t],
                    device_id=(x, y, c), device_id_type=MESH).wait_send()
            for j, chip in enumerate(chips):
                blk = land[t].at[_block_of(*chip, c)]
                pltpu.make_async_remote_copy(
                    src_ref=blk, dst_ref=blk, send_sem=fwd_sems.at[3 * t + j], recv_sem=recv_d2d.at[4 * t + 1 + j],
                    device_id=(x, y, c), device_id_type=MESH).wait_send()
            for k, blk_id in enumerate([_block_of(x, y, 1 - c)] + [_block_of(*chip, 1 - c) for chip in chips]):
                blk = land[t].at[blk_id]
                pltpu.make_async_remote_copy(
                    src_ref=blk, dst_ref=blk, send_sem=send_sems.at[4 * t], recv_sem=recv_d2d.at[4 * t + k],
                    device_id=(x, y, c), device_id_type=MESH).wait_recv()

    out = pl.pallas_call(
        body, name="gather_wait",
        out_shape=_hbm_like(lands),
        in_specs=[ANY] * na + [HBM] * na + [SEM, SEM, SEM, ANY],
        out_specs=tuple([HBM] * na),
        input_output_aliases={na + i: i for i in range(na)},
        compiler_params=SIDE_EFFECT)(*shards, *lands, state["send"], state["fwd"], state["d2d"], after)
    return out


def _gather_from_sibling(state, after):
    lands = state["lands"]
    na = len(lands)

    def body(*refs):
        land, recv_d2d = refs[:na], refs[na]
        x, y, c = _place()
        for t in range(na):
            blk = land[t].at[_block_of(x, y, 1 - c)]
            pltpu.make_async_remote_copy(src_ref=blk, dst_ref=blk, send_sem=recv_d2d.at[4 * t],
                                         recv_sem=recv_d2d.at[4 * t], device_id=(x, y, c),
                                         device_id_type=MESH).wait_recv()

    out = pl.pallas_call(
        body, name="gather_from_sibling", out_shape=_hbm_like(lands),
        in_specs=[HBM] * na + [SEM, ANY], out_specs=tuple([HBM] * na),
        input_output_aliases={i: i for i in range(na)},
        compiler_params=SIDE_EFFECT)(*lands, state["d2d"], after)
    return dict(state, lands=list(out))


def _gather_from_chip(state, j, afters, last):
    shards, lands = state["shards"], state["lands"]
    na = len(lands)

    def chip_blocks(land_ref):
        x, y, c = _place()
        chip = [(1 - x, y), (x, 1 - y), (1 - x, 1 - y)][j]
        return (x, y, c), land_ref.at[_block_of(*chip, c)], land_ref.at[_block_of(*chip, 1 - c)]

    def forward(*refs):
        land, recv_ici, recv_d2d, fwd_sems = refs[:na], refs[na], refs[na + 1], refs[-1]
        for t in range(na):
            (x, y, c), mine, _ = chip_blocks(land[t])
            pltpu.make_async_remote_copy(src_ref=mine, dst_ref=mine, send_sem=fwd_sems.at[t],
                                         recv_sem=recv_ici.at[3 * t + j], device_id=(x, y, c),
                                         device_id_type=MESH).wait_recv()
            pltpu.make_async_remote_copy(src_ref=mine, dst_ref=mine, send_sem=fwd_sems.at[t],
                                         recv_sem=recv_d2d.at[4 * t + 1 + j], device_id=(x, y, 1 - c),
                                         device_id_type=MESH).start()

    out = pl.pallas_call(
        forward, name="gather_pass_chip_" + str(j),
        out_shape=(*_hbm_like(lands), pltpu.SemaphoreType.DMA((na,))),
        in_specs=[HBM] * na + [SEM, SEM] + [ANY] * len(afters), out_specs=(*[HBM] * na, SEM),
        input_output_aliases={i: i for i in range(na)},
        compiler_params=SIDE_EFFECT)(*lands, state["ici"], state["d2d"], *afters)
    lands, fwd_sems = out[:na], out[na]

    def arrive(*refs):
        land, fwd_sems, recv_d2d = refs[:na], refs[na], refs[na + 1]
        shard, send_sems = refs[na + 2:2 * na + 2], refs[2 * na + 2]
        for t in range(na):
            (x, y, c), mine, theirs = chip_blocks(land[t])
            pltpu.make_async_remote_copy(src_ref=theirs, dst_ref=theirs, send_sem=fwd_sems.at[t],
                                         recv_sem=recv_d2d.at[4 * t + 1 + j], device_id=(x, y, c),
                                         device_id_type=MESH).wait_recv()
            pltpu.make_async_remote_copy(src_ref=mine, dst_ref=mine, send_sem=fwd_sems.at[t],
                                         recv_sem=recv_d2d.at[4 * t + 1 + j], device_id=(x, y, c),
                                         device_id_type=MESH).wait_send()
            for k in range(4 if last else 0):
                pltpu.make_async_remote_copy(
                    src_ref=shard[t], dst_ref=land[t].at[_block_of(x, y, c)], send_sem=send_sems.at[4 * t + k],
                    recv_sem=recv_d2d.at[4 * t], device_id=(x, y, c), device_id_type=MESH).wait_send()

    out = pl.pallas_call(
        arrive, name="gather_take_chip_" + str(j), out_shape=_hbm_like(lands),
        in_specs=[HBM] * na + [SEM, SEM] + [ANY] * na + [SEM], out_specs=tuple([HBM] * na),
        input_output_aliases={i: i for i in range(na)},
        compiler_params=SIDE_EFFECT)(*lands, fwd_sems, state["d2d"], *shards, state["send"])
    return dict(state, lands=list(out))


def _to_sibling(srcs, lands, send_sems, recv_sems):
    x, y, c = _place()
    return [pltpu.make_async_remote_copy(
        src_ref=srcs[t].at[:, 1 - c], dst_ref=lands[t], send_sem=send_sems.at[t], recv_sem=recv_sems.at[t],
        device_id=(x, y, 1 - c), device_id_type=MESH) for t in range(len(srcs))]


def _to_chips(srcs, lands, send_sems, recv_sems):
    x, y, c = _place()
    copies = []
    for k in (1, 2, 3):
        px, py = x ^ (k >> 1), y ^ (k & 1)
        copies += [pltpu.make_async_remote_copy(
            src_ref=srcs[t].at[2 * px + py], dst_ref=lands[t].at[k - 1], send_sem=send_sems.at[3 * t + k - 1],
            recv_sem=recv_sems.at[3 * t + k - 1], device_id=(px, py, c), device_id_type=MESH) for t in range(len(srcs))]
    return copies


def _exchange_start(name, srcs, land_shapes, copies, per_array, after):
    na = len(srcs)
    lands = [_hbm(lax.empty(shp, a.dtype)) for shp, a in zip(land_shapes, srcs)]

    def body(*refs):
        token = refs[-1]
        for cp in copies(refs[:na], refs[na:2 * na], refs[2 * na + 1], refs[2 * na + 2]):
            cp.start()
        token[...] = jnp.zeros_like(token)

    out = pl.pallas_call(
        body, name=name,
        out_shape=(pltpu.SemaphoreType.DMA((na * per_array,)), pltpu.SemaphoreType.DMA((na * per_array,)),
                   *_hbm_like(lands), TOKEN),
        in_specs=[ANY] * na + [HBM] * na + [ANY],
        out_specs=(SEM, SEM, *[HBM] * na, pl.BlockSpec(memory_space=pltpu.VMEM)),
        input_output_aliases={na + i: 2 + i for i in range(na)},
        compiler_params=SIDE_EFFECT)(*srcs, *lands, after)
    return dict(send=out[0], recv=out[1], srcs=list(srcs), lands=out[2:2 + na]), out[-1]


def _exchange_wait(name, state, copies, afters):
    srcs, lands = state["srcs"], state["lands"]
    na = len(srcs)

    def body(*refs):
        for cp in copies(refs[:na], refs[na:2 * na], refs[2 * na], refs[2 * na + 1]):
            cp.wait_send()
            cp.wait_recv()

    out = pl.pallas_call(
        body, name=name,
        out_shape=_hbm_like(lands),
        in_specs=[ANY] * na + [HBM] * na + [SEM, SEM] + [ANY] * len(afters),
        out_specs=tuple([HBM] * na),
        input_output_aliases={na + i: i for i in range(na)},
        compiler_params=SIDE_EFFECT)(*srcs, *lands, state["send"], state["recv"], *afters)
    return out


def _adamw_math(w, g, m, v):
    m = ADAM_B1 * m + (1.0 - ADAM_B1) * g
    v = ADAM_B2 * v + (1.0 - ADAM_B2) * (g * g)
    m_hat = m / (1.0 - ADAM_B1 ** ADAM_STEP)
    v_hat = v / (1.0 - ADAM_B2 ** ADAM_STEP)
    return -ADAM_LR * (m_hat / (jnp.sqrt(v_hat) + ADAM_EPS) + ADAM_WD * w), m, v


def _adamw(own, others, w, m, v, tr, name, after):
    rows, cols = w.shape
    blk = pl.BlockSpec((tr, cols), lambda i: (i, 0))

    def body(own_ref, oth_ref, w_ref, m_ref, v_ref, after_ref, g_ref, d_ref, nm_ref, nv_ref):
        g = own_ref[...]
        for k in range(3):
            g = g + oth_ref[k].astype(F32)
        g_ref[...] = g
        d_ref[...], nm_ref[...], nv_ref[...] = _adamw_math(w_ref[...], g, m_ref[...], v_ref[...])

    out = jax.ShapeDtypeStruct((rows, cols), F32)
    return pl.pallas_call(
        body, name=name, out_shape=(out, out, out, out), grid=(rows // tr,),
        in_specs=[blk, pl.BlockSpec((3, tr, cols), lambda i: (0, i, 0)), blk, blk, blk, ANY],
        out_specs=(blk, blk, blk, blk),
        compiler_params=_params("parallel"))(own, others, w, m, v, after)


def _adamw_small(red, me, params, moments1, moments2):
    n = len(params)

    def body(me_ref, red_ref, *refs):
        ws, ms, vs = refs[:n], refs[n:2 * n], refs[2 * n:3 * n]
        loss_ref = refs[3 * n]
        outs = refs[3 * n + 1:]
        loss_ref[...] = jnp.sum(red_ref[ROW_MISC:ROW_MISC + 1, LOSS_AT:LOSS_AT + LANES], axis=-1, keepdims=True)
        for t, (row, at) in enumerate(SMALL_AT):
            g = red_ref[row:row + 1, at:at + ws[t].shape[1]]
            d, nm, nv = _adamw_math(ws[t][...], g, ms[t][...], vs[t][...])
            for o, val in zip(outs[4 * t:4 * t + 4], (g, d, nm, nv)):
                o[...] = val
        for tap in range(ws[-1].shape[0]):
            row, at = _tap_at(tap)
            g = red_ref[row:row + 1, pl.ds(pl.multiple_of(at + me_ref[0, 0] * LANES, LANES), LANES)]
            d, nm, nv = _adamw_math(ws[-1][tap], g, ms[-1][tap], vs[-1][tap])
            for o, val in zip(outs[4 * (n - 1):], (g, d, nm, nv)):
                o[tap] = val

    vmem = pl.BlockSpec(memory_space=pltpu.VMEM)
    shapes = [jax.ShapeDtypeStruct(w.shape, F32) for w in params for _ in range(4)]
    out = pl.pallas_call(
        body, name="adamw_small", out_shape=(jax.ShapeDtypeStruct((1, 1), F32), *shapes),
        in_specs=[pl.BlockSpec(memory_space=pltpu.SMEM), vmem] + [vmem] * (3 * n),
        out_specs=tuple([vmem] * (1 + 4 * n)))(me, red, *params, *moments1, *moments2)
    return out[0], [list(out[1 + k::4]) for k in range(4)]


def _tables(s, gq, gk, conv_w):
    gq2 = jnp.tile(gq.reshape(1, HEAD), (1, 2))
    gk2 = jnp.tile(gk.reshape(1, HEAD), (1, 2))
    conv_wp = jnp.pad(conv_w, ((0, SUBLANES - conv_w.shape[0]), (0, 0)))
    return _rope_tables(s), gq2, gk2, conv_wp


def _pair_id(q):
    return jnp.array([q, 0], jnp.int32)


def _forward_in(x, g1, shards):
    s = x.shape[0]
    h = _prenorm(x, g1, min(512, s), x)
    z, w_pairs = lax.empty((s, IN_W), F32), lax.empty((N_PAIRS, PAIR_W, D_MODEL), BF16)
    for q in range(N_PAIRS):
        z, w_pairs = _fwd_in_pair(h, shards, z, w_pairs, _pair_id(q), min(512, s), "fwd_in_" + str(q),
                                  own=shards[0] if q == 0 else None)
    return h, z, w_pairs


def _forward_attn(z, rope, gq2, gk2, conv_wp, sinks):
    s = z.shape[0]
    qn, k2, v2 = _qk_prep(z, *rope, gq2, gk2, min(256, s), z)
    a, mix, mixt = _attn_fwd(qn, k2, v2, z, conv_wp, sinks, qn)
    return qn, k2, v2, a, mix, mixt


def _forward_out(x, p, target, mix, mixt, w_out, g2, w_pg, b_pg, w_pp, g3):
    s = x.shape[0]
    tm = min(512, s)
    x1, hn2, hn2t = _fwd_out(mix, w_out, x, g2, tm)
    dy, dgp, dt, pt, acc_ple = _ple(hn2, w_pg, b_pg, p, w_pp, g3, x1, target, min(256, s))
    dx1, dx1b, acc_g2 = _gate_bwd(dgp, w_pg, x1, dy, g2, tm)
    gw_out = _mm_grad(mixt, [dx1b], 512, "grad_w_out")
    gw_pg = _mm_grad(hn2t, [dgp], 512, "grad_w_ple_gate")
    gw_pp = _mm_grad(pt, [dt], 512, "grad_w_ple_proj")
    return dx1, dx1b, (gw_out, gw_pg, gw_pp), acc_ple, acc_g2


def _backward_attn(dmix, h, z, qn, k2, v2, a, rope, gq2, gk2, conv_wp, sinks, after):
    dq, dkc, dkp, dvc, dvp, dz, dzt, acc_attn = _attn_bwd(qn, k2, v2, a, z, dmix, conv_wp, sinks, after)
    dz, dzt, acc_qk = _qkv_bwd(z, dz, dzt, dq, dkc, dkp, dvc, dvp, *rope, gq2, gk2)
    return dz, _grad_w_in(dzt, h), acc_attn, acc_qk


def _local_step(x, p, target, g1, shards, gq, gk, sinks, conv_w, w_out, g2, w_pg, b_pg, w_pp, g3):
    rope, gq2, gk2, conv_wp = _tables(x.shape[0], gq, gk, conv_w)
    h, z, w_pairs = _forward_in(x, g1, shards)
    qn, k2, v2, a, mix, mixt = _forward_attn(z, rope, gq2, gk2, conv_wp, sinks)
    dx1, dx1b, (gw_out, gw_pg, gw_pp), acc_ple, acc_g2 = _forward_out(
        x, p, target, mix, mixt, w_out, g2, w_pg, b_pg, w_pp, g3)
    dmix = _mm_nt(dx1b, w_out, min(512, x.shape[0]), "out_bwd", dx1b)
    dz, gw_in, acc_attn, acc_qk = _backward_attn(dmix, h, z, qn, k2, v2, a, rope, gq2, gk2, conv_wp, sinks, dmix)
    grad_x, acc_g1 = _in_bwd(dz, w_pairs, x, dx1, g1, min(512, x.shape[0]), dx1)
    return grad_x, (gw_in, gw_out, gw_pg, gw_pp), (acc_g1, acc_g2, acc_ple, acc_qk, acc_attn)


def _by_owner(g):
    return g.reshape((4, 2) + g.shape[1:])


def kernel(x, p, norm_gain, w_in, q_norm_gain, k_norm_gain, attn_sinks, conv_w, w_out, ple_gate_norm_gain, w_ple_gate, b_ple_gate, w_ple_proj, ple_norm_gain, loss_target, m_norm_gain, m_w_in, m_q_norm_gain, m_k_norm_gain, m_attn_sinks, m_conv_w, m_w_out, m_ple_gate_norm_gain, m_w_ple_gate, m_b_ple_gate, m_w_ple_proj, m_ple_norm_gain, v_norm_gain, v_w_in, v_q_norm_gain, v_k_norm_gain, v_attn_sinks, v_conv_w, v_w_out, v_ple_gate_norm_gain, v_w_ple_gate, v_b_ple_gate, v_w_ple_proj, v_ple_norm_gain):
    me = 4 * lax.axis_index("x") + 2 * lax.axis_index("y") + lax.axis_index("c")
    place = jnp.stack([lax.axis_index("c"), 2 * lax.axis_index("x") + lax.axis_index("y")]).astype(jnp.int32)
    xs, ps, target = x[0], p[0, 0], loss_target[0]

    shard_in = w_in[0].T.astype(BF16)
    own_late = [w_out[0].astype(BF16), w_ple_gate[0].astype(BF16), w_ple_proj[0].astype(BF16)]
    with_own = lambda gathered, own: lax.dynamic_update_slice(gathered, own[None], (me,) + (0,) * own.ndim)
    early, started = _gather_start([shard_in, conv_w[0]], shard_in)
    tm = min(512, xs.shape[0])
    h = _prenorm(xs, norm_gain, tm, started)

    z, w_pairs = lax.empty((xs.shape[0], IN_W), F32), lax.empty((N_PAIRS, PAIR_W, D_MODEL), BF16)
    early = _gather_from_sibling(early, h)
    pair_of = lambda flip: jnp.stack([place[1] ^ flip, place[0]])
    z, w_pairs = _fwd_in_pair(h, early["lands"][0], z, w_pairs, pair_of(0), tm, "fwd_in_own", own=shard_in)
    for j, flip in enumerate((2, 1, 3)):
        early = _gather_from_chip(early, j, (z,) if j != 2 else (z, started_late), last=j == 2)
        z, w_pairs = _fwd_in_pair(h, early["lands"][0], z, w_pairs, pair_of(flip), tm, "fwd_in_chip_" + str(j))
        if j == 1:
            late, started_late = _gather_start(own_late, z)
    conv_full = jnp.transpose(with_own(early["lands"][1], conv_w[0]), (1, 0, 2)).reshape(3, ATTN_W)
    rope, gq2, gk2, conv_wp = _tables(xs.shape[0], q_norm_gain[0], k_norm_gain[0], conv_full)
    qn, k2, v2 = _qk_prep(z, *rope, gq2, gk2, min(256, xs.shape[0]), z)
    late, forwarded = _gather_forward(late, qn)
    a, mix, mixt = _attn_fwd(qn, k2, v2, z, conv_wp, attn_sinks, forwarded)
    g_out, g_pg, g_pp = (with_own(g, own) for g, own in zip(_gather_wait(late, mix), own_late))
    w_out_f = g_out.reshape(D_MODEL, D_MODEL)
    w_pg_f = g_pg.reshape(D_MODEL, D_MODEL)
    w_pp_f = jnp.transpose(g_pp, (1, 0, 2)).reshape(PLE_DIM, D_MODEL)

    dx1, dx1b, (gw_out, gw_pg, gw_pp), acc_ple, acc_g2 = _forward_out(
        xs, ps, target, mix, mixt, w_out_f, ple_gate_norm_gain, w_pg_f, b_ple_gate, w_pp_f, ple_norm_gain)

    names = ("w_out", "w_ple_gate", "w_ple_proj")
    gw_pp_t = jnp.transpose(gw_pp.reshape(PLE_DIM, N_DEV, PLE_DIM), (1, 0, 2))
    grads = [_by_owner(gw_out.reshape(N_DEV, D_MODEL // N_DEV, D_MODEL)),
             _by_owner(gw_pg.reshape(N_DEV, D_MODEL // N_DEV, D_MODEL)), _by_owner(gw_pp_t)]
    pairs, paired = _exchange_start("pair_start", grads, [(4,) + g.shape[2:] for g in grads], _to_sibling, 1, dx1b)
    dmix = _mm_nt(dx1b, w_out_f, tm, "out_bwd", paired)
    from_sibling = _exchange_wait("pair_wait", pairs, _to_sibling, (dmix,))
    sums = [_pair_sum(g, r, place, 256, "pair_sum_" + nm) for g, r, nm in zip(pairs["srcs"], from_sibling, names)]
    chips, sent = _exchange_start("chip_start", [pb for pb, _ in sums], [(3,) + pb.shape[1:] for pb, _ in sums],
                                  _to_chips, 3, sums[-1][1])

    dz, gw_in, acc_attn, acc_qk = _backward_attn(
        dmix, h, z, qn, k2, v2, a, rope, gq2, gk2, conv_wp, attn_sinks, sent)

    gw_in_t = [_by_owner(gw_in)]
    pairs_in, paired_in = _exchange_start("pair_start_w_in", gw_in_t, [(4,) + gw_in_t[0].shape[2:]], _to_sibling, 1,
                                          gw_in)
    from_chips = _exchange_wait("chip_wait", chips, _to_chips, (gw_in,))
    big = {}
    for (_, own), oth, w, m, v, nm in zip(sums, from_chips, (w_out, w_ple_gate, w_ple_proj),
                                          (m_w_out, m_w_ple_gate, m_w_ple_proj),
                                          (v_w_out, v_w_ple_gate, v_w_ple_proj), names):
        big[nm] = [t[None] for t in _adamw(own, oth, w[0], m[0], v[0], 256, "adamw_" + nm, paired_in)]

    (from_sibling_in,) = _exchange_wait("pair_wait_w_in", pairs_in, _to_sibling, [big[nm][0] for nm in names])
    pb_in, own_in = _pair_sum(pairs_in["srcs"][0], from_sibling_in, place, SHARD_IN // 2, "pair_sum_w_in")
    chips_in, sent_in = _exchange_start("chip_start_w_in", [pb_in], [(3,) + pb_in.shape[1:]], _to_chips, 3, own_in)
    grad_x, acc_g1 = _in_bwd(dz, w_pairs, xs, dx1, norm_gain, tm, sent_in)
    (from_chips_in,) = _exchange_wait("chip_wait_w_in", chips_in, _to_chips, (grad_x,))
    big["w_in"] = [t.T[None] for t in _adamw(own_in, from_chips_in, w_in[0].T, m_w_in[0].T, v_w_in[0].T, SHARD_IN // 4,
                                             "adamw_w_in", grad_x)]

    red = _reduce_small(acc_g1, acc_g2, acc_ple, acc_qk, acc_attn)
    small = [norm_gain, ple_gate_norm_gain, b_ple_gate, ple_norm_gain, q_norm_gain, k_norm_gain, attn_sinks]
    small_m = [m_norm_gain, m_ple_gate_norm_gain, m_b_ple_gate, m_ple_norm_gain, m_q_norm_gain, m_k_norm_gain,
               m_attn_sinks]
    small_v = [v_norm_gain, v_ple_gate_norm_gain, v_b_ple_gate, v_ple_norm_gain, v_q_norm_gain, v_k_norm_gain,
               v_attn_sinks]
    taps_first = lambda t: jnp.transpose(t, (1, 0, 2))
    loss, kinds = _adamw_small(red, me.reshape(1, 1).astype(jnp.int32), small + [taps_first(conv_w)],
                               small_m + [taps_first(m_conv_w)], small_v + [taps_first(v_conv_w)])

    def order(k):
        sm = kinds[k]
        return [sm[0], big["w_in"][k], sm[4], sm[5], sm[6], taps_first(sm[7]), big["w_out"][k], sm[1],
                big["w_ple_gate"][k], sm[2], big["w_ple_proj"][k], sm[3]]

    return (loss[0, 0], grad_x[None], *order(0), *order(1), *order(2), *order(3))
```

```python
import jax
import jax.numpy as jnp
from jax import lax
from jax.experimental import pallas as pl
from jax.experimental.pallas import tpu as pltpu

F32, BF16 = jnp.float32, jnp.bfloat16

D_MODEL = 2048
PLE_DIM = 256
ATTN_W = 1024
HEAD = 64
N_Q_HEADS = 16
KV_W = 256
QKV_W = ATTN_W + 2 * KV_W
REST_W = 5 * 1024
IN_W = QKV_W + REST_W
GATE_A0, CONV_B0, CONV_C0, CONV_H0, GATE_C0 = (QKV_W + 1024 * t for t in range(5))
K2_W = 4 * 128
ROT = 16
ROPE_THETA = 500000.0
EPS = 1e-6
NEG_INF = -1e30
BLK = 128
LANES = 128
SUBLANES = 8
N_DEV = 8
SHARD_IN = IN_W // N_DEV
PAIR_W = 2 * SHARD_IN
N_PAIRS = IN_W // PAIR_W
SLAB_ROWS = 8
PACKED_ROWS = 16
SUB_ROWS = 128
V7X_VMEM_LIMIT = 52 * 1024 * 1024

ADAM_LR, ADAM_B1, ADAM_B2, ADAM_EPS, ADAM_WD, ADAM_STEP = 0.001, 0.9, 0.999, 1e-08, 0.01, 10
MESH = pl.DeviceIdType.MESH


def _params(*semantics):
    return pltpu.CompilerParams(dimension_semantics=semantics, vmem_limit_bytes=V7X_VMEM_LIMIT)


ANY = pl.BlockSpec(memory_space=pl.ANY)


def _resident(shape):
    return pl.BlockSpec(shape, lambda *_: (0,) * len(shape), pipeline_mode=pl.Buffered(1))


def _dot(a, b):
    return jnp.dot(a, b, preferred_element_type=F32)


def _dot_nt(a, b):
    return lax.dot_general(a, b, (((1,), (1,)), ((), ())), preferred_element_type=F32)


def _rms(xf):
    r = lax.rsqrt(jnp.mean(xf * xf, axis=-1, keepdims=True) + EPS)
    return xf * r, r


def _rms_bwd(dxn, xn, r):
    return r * (dxn - xn * jnp.mean(dxn * xn, axis=-1, keepdims=True))


def _sig(g):
    return jax.nn.sigmoid(g)


def _dsilu(g, sg):
    return sg * (1.0 + g * (1.0 - sg))


def _low_half(shape):
    return lax.broadcasted_iota(jnp.int32, shape, len(shape) - 1) < HEAD


def _half_sums(v):
    lo = _low_half(v.shape)
    s_lo = jnp.sum(jnp.where(lo, v, 0.0), axis=-1, keepdims=True)
    s_hi = jnp.sum(jnp.where(lo, 0.0, v), axis=-1, keepdims=True)
    return jnp.where(lo, s_lo, s_hi)


def _rope(v, a, bm, bp):
    return v * a + pltpu.roll(v, LANES - ROT // 2, 1) * bm + pltpu.roll(v, ROT // 2, 1) * bp


def _rope_t(dy, a, bm, bp):
    return dy * a + pltpu.roll(dy * bm, ROT // 2, 1) + pltpu.roll(dy * bp, LANES - ROT // 2, 1)


def _dup_halves(v):
    lo = _low_half(v.shape)
    a = jnp.where(lo, v, 0.0)
    b = jnp.where(lo, 0.0, v)
    return a + pltpu.roll(a, HEAD, 1), b + pltpu.roll(b, HEAD, 1)


def _rope_tables(s):
    half = ROT // 2
    lane = lax.broadcasted_iota(jnp.int32, (s, LANES), 1) % HEAD
    pos = lax.broadcasted_iota(jnp.int32, (half, s), 1).astype(F32)
    freq = lax.broadcasted_iota(jnp.int32, (half, s), 0).astype(F32)
    ang = pos * jnp.power(jnp.float32(ROPE_THETA), -freq * 2.0 / ROT)
    cos, sin = lax.optimization_barrier((jnp.cos(ang), jnp.sin(ang)))
    cos, sin = (jnp.tile(t.T, (1, LANES // half)) for t in (cos, sin))
    a = jnp.where(lane < ROT, cos, 1.0)
    bm = jnp.where(lane < half, -sin, 0.0)
    bp = jnp.where((lane >= half) & (lane < ROT), sin, 0.0)
    return a, bm, bp


def _prenorm(x, g1, tm, after):
    s = x.shape[0]

    def body(x_ref, g_ref, after_ref, h_ref):
        xn, _ = _rms(x_ref[...])
        h_ref[...] = (xn * g_ref[...]).astype(BF16)

    return pl.pallas_call(
        body, name="prenorm",
        out_shape=jax.ShapeDtypeStruct((s, D_MODEL), BF16),
        grid=(s // tm,),
        in_specs=[pl.BlockSpec((tm, D_MODEL), lambda i: (i, 0)), pl.BlockSpec((1, D_MODEL), lambda i: (0, 0)), ANY],
        out_specs=pl.BlockSpec((tm, D_MODEL), lambda i: (i, 0)),
        compiler_params=_params("parallel"))(x, g1, after)


def _fwd_in_pair(h, shards, z, w_pairs, pair, tm, name, own=None):
    s = h.shape[0]

    def body(pair_ref, h_ref, lo_ref, hi_ref, z_in, wp_in, z_ref, wp_ref):
        @pl.when(pl.program_id(0) == 0)
        def _():
            wp_ref[0, 0:SHARD_IN, :] = lo_ref[0]
            wp_ref[0, SHARD_IN:PAIR_W, :] = hi_ref[0]

        z_ref[...] = _dot_nt(h_ref[...], wp_ref[0])

    def body_own(pair_ref, h_ref, own_ref, other_ref, z_in, wp_in, z_ref, wp_ref):
        @pl.when(pl.program_id(0) == 0)
        def _():
            first = pl.multiple_of(pair_ref[1] * SHARD_IN, SHARD_IN)
            wp_ref[0, pl.ds(first, SHARD_IN), :] = own_ref[...]
            wp_ref[0, pl.ds(SHARD_IN - first, SHARD_IN), :] = other_ref[0]

        z_ref[...] = _dot_nt(h_ref[...], wp_ref[0])

    if own is None:
        blocks = [pl.BlockSpec((1, SHARD_IN, D_MODEL), lambda i, p: (2 * p[0], 0, 0)),
                  pl.BlockSpec((1, SHARD_IN, D_MODEL), lambda i, p: (2 * p[0] + 1, 0, 0))]
        operands = (shards, shards)
    else:
        blocks = [pl.BlockSpec((SHARD_IN, D_MODEL), lambda i, p: (0, 0)),
                  pl.BlockSpec((1, SHARD_IN, D_MODEL), lambda i, p: (2 * p[0] + 1 - p[1], 0, 0))]
        operands = (own, shards)
    grid_spec = pltpu.PrefetchScalarGridSpec(
        num_scalar_prefetch=1, grid=(s // tm,),
        in_specs=[pl.BlockSpec((tm, D_MODEL), lambda i, p: (i, 0)), *blocks, ANY, ANY],
        out_specs=(pl.BlockSpec((tm, PAIR_W), lambda i, p: (i, p[0])),
                   pl.BlockSpec((1, PAIR_W, D_MODEL), lambda i, p: (p[0], 0, 0))))
    return pl.pallas_call(
        body if own is None else body_own, name=name, grid_spec=grid_spec,
        out_shape=(jax.ShapeDtypeStruct(z.shape, z.dtype), jax.ShapeDtypeStruct(w_pairs.shape, w_pairs.dtype)),
        input_output_aliases={4: 0, 5: 1},
        compiler_params=_params("arbitrary"))(pair, h, *operands, z, w_pairs)


def _qk_prep(z, ra, rbm, rbp, gq2, gk2, tm, after):
    s = z.shape[0]

    def body(z_ref, a_ref, bm_ref, bp_ref, gq_ref, gk_ref, after_ref, q_ref, k2_ref, v2_ref):
        a, bm, bp = a_ref[...], bm_ref[...], bp_ref[...]
        for r in range(ATTN_W // LANES):
            x = z_ref[:, LANES * r:LANES * (r + 1)]
            rr = lax.rsqrt(_half_sums(x * x) * (1.0 / HEAD) + EPS)
            q_ref[:, LANES * r:LANES * (r + 1)] = _rope(x * rr * gq_ref[...], a, bm, bp).astype(BF16)
        for m in range(KV_W // LANES):
            x = z_ref[:, ATTN_W + LANES * m:ATTN_W + LANES * (m + 1)]
            rr = lax.rsqrt(_half_sums(x * x) * (1.0 / HEAD) + EPS)
            k_lo, k_hi = _dup_halves(_rope(x * rr * gk_ref[...], a, bm, bp))
            k2_ref[:, 2 * LANES * m:2 * LANES * m + LANES] = k_lo.astype(BF16)
            k2_ref[:, 2 * LANES * m + LANES:2 * LANES * (m + 1)] = k_hi.astype(BF16)
            v_lo, v_hi = _dup_halves(z_ref[:, ATTN_W + KV_W + LANES * m:ATTN_W + KV_W + LANES * (m + 1)])
            v2_ref[:, 2 * LANES * m:2 * LANES * m + LANES] = v_lo.astype(BF16)
            v2_ref[:, 2 * LANES * m + LANES:2 * LANES * (m + 1)] = v_hi.astype(BF16)

    row = lambda w: pl.BlockSpec((tm, w), lambda i: (i, 0))
    one = pl.BlockSpec((1, LANES), lambda i: (0, 0))
    return pl.pallas_call(
        body, name="qk_prep",
        out_shape=(jax.ShapeDtypeStruct((s, ATTN_W), BF16), jax.ShapeDtypeStruct((s, K2_W), BF16),
                   jax.ShapeDtypeStruct((s, K2_W), BF16)),
        grid=(s // tm,),
        in_specs=[row(PAIR_W), row(LANES), row(LANES), row(LANES), one, one, ANY],
        out_specs=(row(ATTN_W), row(K2_W), row(K2_W)),
        compiler_params=_params("parallel"))(z, ra, rbm, rbp, gq2, gk2, after)


GROUP = 4


def _window_mask(n):
    row = lax.broadcasted_iota(jnp.int32, (GROUP * BLK, 2 * BLK), 0) % BLK
    col = lax.broadcasted_iota(jnp.int32, (GROUP * BLK, 2 * BLK), 1)
    return (col > row) & (col <= row + BLK) & ((col >= BLK) | (n > 0))


def _stack_heads(pairs, zero):
    lo = _low_half(pairs[0].shape)
    parts = []
    for v in pairs:
        parts += [jnp.where(lo, v, zero), jnp.where(lo, zero, v)]
    return jnp.concatenate(parts, axis=0)


def _unstack_heads(v4):
    lo = _low_half((BLK, LANES))
    return [jnp.where(lo, v4[2 * i * BLK:(2 * i + 1) * BLK], v4[(2 * i + 1) * BLK:(2 * i + 2) * BLK]) for i in range(2)]


def _group_sinks(sink_ref, kvh):
    slot = lax.broadcasted_iota(jnp.int32, (GROUP * BLK, 1), 0) // BLK
    col = jnp.zeros((GROUP * BLK, 1), F32)
    for i in range(GROUP):
        col = jnp.where(slot == i, sink_ref[0, GROUP * kvh + i], col)
    return col, slot


def _head_probs(qm, kw, valid, sink):
    sc = jnp.where(valid, _dot_nt(qm, kw) * (HEAD ** -0.5), NEG_INF)
    mx = jnp.maximum(jnp.max(sc, axis=-1, keepdims=True), sink)
    ex = jnp.exp(sc - mx)
    den = jnp.sum(ex, axis=-1, keepdims=True) + jnp.exp(sink - mx)
    return ex / den, mx, den


def _cols(start, width=ATTN_W):
    return slice(start, start + width)


def _conv_fwd(z_ref, zp_ref, cw_ref, ext_ref, n):
    u = z_ref[:, _cols(CONV_C0)] * z_ref[:, _cols(CONV_H0)]
    pu = zp_ref[:, _cols(CONV_C0)] * zp_ref[:, _cols(CONV_H0)]
    ext_ref[0:SUBLANES, :] = jnp.where(n > 0, pu, 0.0)
    ext_ref[SUBLANES:SUBLANES + BLK, :] = u
    um1 = ext_ref[SUBLANES - 1:SUBLANES - 1 + BLK, :]
    um2 = ext_ref[SUBLANES - 2:SUBLANES - 2 + BLK, :]
    cv = cw_ref[0:1, :] * um2 + cw_ref[1:2, :] * um1 + cw_ref[2:3, :] * u
    return u, um1, um2, cv


def _prev_rows(n):
    return (jnp.maximum(n * (BLK // SUBLANES) - 1, 0), 0)


def _attn_fwd(qn, k2, v2, z, conv_wp, sinks, after):
    s = qn.shape[0]
    nb = s // BLK

    def body(sink_ref, q_ref, kc_ref, kp_ref, vc_ref, vp_ref, z_ref, zp_ref, cw_ref, after_ref, a_ref, mix_ref,
             mixt_ref, ext_ref):
        n = pl.program_id(0)
        valid = _window_mask(n)
        for kvh in range(K2_W // LANES):
            cols = slice(LANES * kvh, LANES * (kvh + 1))
            kw = jnp.concatenate([kp_ref[:, cols], kc_ref[:, cols]], axis=0)
            vw = jnp.concatenate([vp_ref[:, cols], vc_ref[:, cols]], axis=0)
            blocks = [slice(LANES * r, LANES * (r + 1)) for r in (2 * kvh, 2 * kvh + 1)]
            q4 = _stack_heads([q_ref[:, rc] for rc in blocks], jnp.zeros((BLK, LANES), BF16))
            p, _, _ = _head_probs(q4, kw, valid, _group_sinks(sink_ref, kvh)[0])
            for rc, a in zip(blocks, _unstack_heads(_dot(p.astype(BF16), vw))):
                a_ref[:, rc] = a
                g = z_ref[:, _cols(GATE_A0 + rc.start, LANES)]
                mix_ref[:, rc] = (a * (g * _sig(g))).astype(BF16)
        _, _, _, cv = _conv_fwd(z_ref, zp_ref, cw_ref, ext_ref, n)
        gc = z_ref[:, _cols(GATE_C0)]
        mix_ref[:, ATTN_W:D_MODEL] = (z_ref[:, _cols(CONV_B0)] * cv * (gc * _sig(gc))).astype(BF16)
        mixt_ref[...] = mix_ref[...].T

    cur = lambda w: pl.BlockSpec((BLK, w), lambda n: (n, 0))
    prev = lambda w: pl.BlockSpec((BLK, w), lambda n: (jnp.maximum(n - 1, 0), 0))
    return pl.pallas_call(
        body, name="attn_fwd",
        out_shape=(jax.ShapeDtypeStruct((s, ATTN_W), F32), jax.ShapeDtypeStruct((s, D_MODEL), BF16),
                   jax.ShapeDtypeStruct((D_MODEL, s), BF16)),
        grid=(nb,),
        in_specs=[pl.BlockSpec(memory_space=pltpu.SMEM),
                  cur(ATTN_W), cur(K2_W), prev(K2_W), cur(K2_W), prev(K2_W), cur(IN_W),
                  pl.BlockSpec((SUBLANES, IN_W), _prev_rows),
                  pl.BlockSpec((SUBLANES, ATTN_W), lambda n: (0, 0)), ANY],
        out_specs=(cur(ATTN_W), cur(D_MODEL), pl.BlockSpec((D_MODEL, BLK), lambda n: (0, n))),
        scratch_shapes=[pltpu.VMEM((BLK + 2 * SUBLANES, ATTN_W), F32)],
        compiler_params=_params("parallel"))(sinks, qn, k2, k2, v2, v2, z, z, conv_wp, after)


def _fwd_out(mix, w_out, x, g2, tm):
    s = x.shape[0]

    def body(m_ref, w_ref, x_ref, g_ref, x1_ref, h_ref, ht_ref):
        x1 = x_ref[...] + _dot(m_ref[...], w_ref[...])
        x1_ref[...] = x1
        xn, _ = _rms(x1)
        h = (xn * g_ref[...]).astype(BF16)
        h_ref[...] = h
        ht_ref[...] = h.T

    row = pl.BlockSpec((tm, D_MODEL), lambda i: (i, 0))
    return pl.pallas_call(
        body, name="fwd_out",
        out_shape=(jax.ShapeDtypeStruct((s, D_MODEL), F32), jax.ShapeDtypeStruct((s, D_MODEL), BF16),
                   jax.ShapeDtypeStruct((D_MODEL, s), BF16)),
        grid=(s // tm,),
        in_specs=[row, _resident((D_MODEL, D_MODEL)), row, pl.BlockSpec((1, D_MODEL), lambda i: (0, 0))],
        out_specs=(row, row, pl.BlockSpec((D_MODEL, tm), lambda i: (0, i))),
        compiler_params=_params("parallel"))(mix, w_out, x, g2)


def _ple(hn2, w_pg, b_pg, p, w_pp, g3, x1, target, tm):
    s = x1.shape[0]

    def body(h_ref, wg_ref, b_ref, p_ref, wp_ref, g3_ref, x1_ref, t_ref, dy_ref, dgp_ref, dt_ref, pt_ref, acc_ref):
        gate = _sig(_dot(h_ref[...], wg_ref[...]) + b_ref[...])
        pb = p_ref[...].astype(BF16)
        pt_ref[...] = pb.T
        t = _dot(pb, wp_ref[...])
        tn, r3 = _rms(t)
        e = tn * g3_ref[...]
        diff = x1_ref[...] + gate * e - t_ref[...]
        dy = diff * (1.0 / D_MODEL)
        dy_ref[...] = dy
        dgp = dy * e * (gate * (1.0 - gate))
        dgp_ref[...] = dgp.astype(BF16)
        de = dy * gate
        dt_ref[...] = _rms_bwd(de * g3_ref[...], tn, r3).astype(BF16)

        @pl.when(pl.program_id(0) == 0)
        def _():
            acc_ref[...] = jnp.zeros_like(acc_ref)

        acc_ref[0:1, :] += jnp.sum(dgp, axis=0, keepdims=True)
        acc_ref[1:2, :] += jnp.sum(de * tn, axis=0, keepdims=True)
        acc_ref[2:3, :] += jnp.sum(diff * diff, axis=0, keepdims=True) * (0.5 / D_MODEL)

    row = pl.BlockSpec((tm, D_MODEL), lambda i: (i, 0))
    vec = pl.BlockSpec((1, D_MODEL), lambda i: (0, 0))
    return pl.pallas_call(
        body, name="ple",
        out_shape=(jax.ShapeDtypeStruct((s, D_MODEL), F32), jax.ShapeDtypeStruct((s, D_MODEL), BF16),
                   jax.ShapeDtypeStruct((s, D_MODEL), BF16), jax.ShapeDtypeStruct((PLE_DIM, s), BF16),
                   jax.ShapeDtypeStruct((SUBLANES, D_MODEL), F32)),
        grid=(s // tm,),
        in_specs=[row, _resident((D_MODEL, D_MODEL)), vec, pl.BlockSpec((tm, PLE_DIM), lambda i: (i, 0)),
                  _resident((PLE_DIM, D_MODEL)), vec, row, row],
        out_specs=(row, row, row, pl.BlockSpec((PLE_DIM, tm), lambda i: (0, i)),
                   pl.BlockSpec((SUBLANES, D_MODEL), lambda i: (0, 0))),
        compiler_params=_params("arbitrary"))(hn2, w_pg, b_pg, p, w_pp, g3, x1, target)


def _gate_bwd(dgp, w_pg, x1, dy, g2, tm):
    s = x1.shape[0]

    def body(d_ref, w_ref, x1_ref, dy_ref, g_ref, dx_ref, dxb_ref, acc_ref):
        dh = _dot_nt(d_ref[...], w_ref[...])
        xn, r = _rms(x1_ref[...])
        dx1 = dy_ref[...] + _rms_bwd(dh * g_ref[...], xn, r)
        dx_ref[...] = dx1
        dxb_ref[...] = dx1.astype(BF16)

        @pl.when(pl.program_id(0) == 0)
        def _():
            acc_ref[...] = jnp.zeros_like(acc_ref)

        acc_ref[0:1, :] += jnp.sum(dh * xn, axis=0, keepdims=True)

    row = pl.BlockSpec((tm, D_MODEL), lambda i: (i, 0))
    return pl.pallas_call(
        body, name="gate_bwd",
        out_shape=(jax.ShapeDtypeStruct((s, D_MODEL), F32), jax.ShapeDtypeStruct((s, D_MODEL), BF16),
                   jax.ShapeDtypeStruct((SUBLANES, D_MODEL), F32)),
        grid=(s // tm,),
        in_specs=[row, _resident((D_MODEL, D_MODEL)), row, row, pl.BlockSpec((1, D_MODEL), lambda i: (0, 0))],
        out_specs=(row, row, pl.BlockSpec((SUBLANES, D_MODEL), lambda i: (0, 0))),
        compiler_params=_params("arbitrary"))(dgp, w_pg, x1, dy, g2)


def _mm_nt(a, b, tm, name, after):
    m, k = a.shape
    n = b.shape[0]

    def body(a_ref, b_ref, after_ref, o_ref):
        o_ref[...] = _dot_nt(a_ref[...], b_ref[...])

    return pl.pallas_call(
        body, name=name,
        out_shape=jax.ShapeDtypeStruct((m, n), F32),
        grid=(m // tm,),
        in_specs=[pl.BlockSpec((tm, k), lambda i: (i, 0)), _resident((n, k)), ANY],
        out_specs=pl.BlockSpec((tm, n), lambda i: (i, 0)),
        compiler_params=_params("parallel"))(a, b, after)


def _attn_bwd(qn, k2, v2, a, z, dmix, conv_wp, sinks, after):
    s = qn.shape[0]
    nb = s // BLK

    def body(sink_ref, q_ref, kc_ref, kp_ref, vc_ref, vp_ref, a_ref, z_ref, zp_ref, zn_ref, dm_ref, dmn_ref,
             cw_ref, after_ref, dq_ref, dkc_ref, dkp_ref, dvc_ref, dvp_ref, dz_ref, dzt_ref, acc_ref, ext_ref):
        n = pl.program_id(0)
        valid = _window_mask(n)
        lane = lax.broadcasted_iota(jnp.int32, (1, ATTN_W), 1)

        @pl.when(n == 0)
        def _():
            acc_ref[...] = jnp.zeros_like(acc_ref)

        dz_ref[:, 0:QKV_W] = jnp.zeros((BLK, QKV_W), BF16)
        dsink = jnp.zeros((1, ATTN_W), F32)
        for kvh in range(K2_W // LANES):
            cols = slice(LANES * kvh, LANES * (kvh + 1))
            kw = jnp.concatenate([kp_ref[:, cols], kc_ref[:, cols]], axis=0)
            vw = jnp.concatenate([vp_ref[:, cols], vc_ref[:, cols]], axis=0)
            blocks = [slice(LANES * r, LANES * (r + 1)) for r in (2 * kvh, 2 * kvh + 1)]
            das, avs = [], []
            for rc in blocks:
                g = z_ref[:, _cols(GATE_A0 + rc.start, LANES)]
                sg = _sig(g)
                dm = dm_ref[:, rc]
                av = a_ref[:, rc]
                das.append(dm * (g * sg))
                avs += [av, av]
                dz_ref[:, _cols(GATE_A0 + rc.start, LANES)] = (dm * av * _dsilu(g, sg)).astype(BF16)
            q4 = _stack_heads([q_ref[:, rc] for rc in blocks], jnp.zeros((BLK, LANES), BF16))
            sink, slot = _group_sinks(sink_ref, kvh)
            p, mx, den = _head_probs(q4, kw, valid, sink)
            do4 = _stack_heads(das, 0.0)
            delta = jnp.sum(do4 * jnp.concatenate(avs, axis=0), axis=-1, keepdims=True)
            dob = do4.astype(BF16)
            ds = p * (_dot_nt(dob, vw) - delta) * (HEAD ** -0.5)
            for rc, dq in zip(blocks, _unstack_heads(_dot(ds.astype(BF16), kw))):
                dq_ref[:, rc] = dq
            dk2 = _dot(ds.T.astype(BF16), q4)
            dv2 = _dot(p.T.astype(BF16), dob)
            dkp_ref[:, cols] = dk2[0:BLK]
            dkc_ref[:, cols] = dk2[BLK:2 * BLK]
            dvp_ref[:, cols] = dv2[0:BLK]
            dvc_ref[:, cols] = dv2[BLK:2 * BLK]
            dsk = jnp.exp(sink - mx) / den * delta
            for i in range(GROUP):
                dsink = dsink - jnp.where(lane == GROUP * kvh + i,
                                          jnp.sum(jnp.where(slot == i, dsk, 0.0), axis=0, keepdims=True), 0.0)
        acc_ref[0:1, :] += dsink

        u, um1, um2, cv = _conv_fwd(z_ref, zp_ref, cw_ref, ext_ref, n)
        cb = z_ref[:, _cols(CONV_B0)]
        gc = z_ref[:, _cols(GATE_C0)]
        sgc = _sig(gc)
        dmc = dm_ref[:, ATTN_W:D_MODEL]
        t = dmc * (gc * sgc)
        dcv = t * cb
        dz_ref[:, _cols(CONV_B0)] = (t * cv).astype(BF16)
        dz_ref[:, _cols(GATE_C0)] = (dmc * cb * cv * _dsilu(gc, sgc)).astype(BF16)
        gcn = zn_ref[:, _cols(GATE_C0)]
        dcvn = dmn_ref[:, ATTN_W:D_MODEL] * (gcn * _sig(gcn)) * zn_ref[:, _cols(CONV_B0)]
        ext_ref[0:BLK, :] = dcv
        ext_ref[BLK:BLK + SUBLANES, :] = jnp.where(n < nb - 1, dcvn, 0.0)
        du = (cw_ref[2:3, :] * dcv + cw_ref[1:2, :] * ext_ref[1:1 + BLK, :]
              + cw_ref[0:1, :] * ext_ref[2:2 + BLK, :])
        dz_ref[:, _cols(CONV_C0)] = (du * z_ref[:, _cols(CONV_H0)]).astype(BF16)
        dz_ref[:, _cols(CONV_H0)] = (du * z_ref[:, _cols(CONV_C0)]).astype(BF16)
        acc_ref[1:2, :] += jnp.sum(dcv * um2, axis=0, keepdims=True)
        acc_ref[2:3, :] += jnp.sum(dcv * um1, axis=0, keepdims=True)
        acc_ref[3:4, :] += jnp.sum(dcv * u, axis=0, keepdims=True)
        dzt_ref[...] = dz_ref[...].T

    cur = lambda w: pl.BlockSpec((BLK, w), lambda n: (n, 0))
    prev = lambda w: pl.BlockSpec((BLK, w), lambda n: (jnp.maximum(n - 1, 0), 0))
    nxt = lambda w: pl.BlockSpec(
        (SUBLANES, w), lambda n: (jnp.minimum((n + 1) * (BLK // SUBLANES), nb * (BLK // SUBLANES) - 1), 0))
    f32 = lambda w: jax.ShapeDtypeStruct((s, w), F32)
    return pl.pallas_call(
        body, name="attn_bwd",
        out_shape=(f32(ATTN_W), f32(K2_W), f32(K2_W), f32(K2_W), f32(K2_W),
                   jax.ShapeDtypeStruct((s, IN_W), BF16), jax.ShapeDtypeStruct((IN_W, s), BF16),
                   jax.ShapeDtypeStruct((SUBLANES, ATTN_W), F32)),
        grid=(nb,),
        in_specs=[pl.BlockSpec(memory_space=pltpu.SMEM),
                  cur(ATTN_W), cur(K2_W), prev(K2_W), cur(K2_W), prev(K2_W), cur(ATTN_W), cur(IN_W),
                  pl.BlockSpec((SUBLANES, IN_W), _prev_rows), nxt(IN_W), cur(D_MODEL), nxt(D_MODEL),
                  pl.BlockSpec((SUBLANES, ATTN_W), lambda n: (0, 0)), ANY],
        out_specs=(cur(ATTN_W), cur(K2_W), cur(K2_W), cur(K2_W), cur(K2_W), cur(IN_W),
                   pl.BlockSpec((IN_W, BLK), lambda n: (0, n)), pl.BlockSpec((SUBLANES, ATTN_W), lambda n: (0, 0))),
        scratch_shapes=[pltpu.VMEM((BLK + 2 * SUBLANES, ATTN_W), F32)],
        compiler_params=_params("arbitrary"))(sinks, qn, k2, k2, v2, v2, a, z, z, z, dmix, dmix, conv_wp, after)


def _qkv_bwd(z, dz, dzt, dq, dkc, dkp, dvc, dvp, ra, rbm, rbp, gq2, gk2):
    s = z.shape[0]
    nb = s // BLK

    def body(z_ref, dz_in, dzt_in, dq_ref, dkc_ref, dkp_ref, dvc_ref, dvp_ref, a_ref, bm_ref, bp_ref, gq_ref, gk_ref,
             dz_ref, dzt_ref, acc_ref):
        n = pl.program_id(0)
        a, bm, bp = a_ref[...], bm_ref[...], bp_ref[...]
        lo = _low_half((BLK, LANES))
        last = n == nb - 1

        @pl.when(n == 0)
        def _():
            acc_ref[...] = jnp.zeros_like(acc_ref)

        def norm_bwd(x, dy, gain):
            rr = lax.rsqrt(_half_sums(x * x) * (1.0 / HEAD) + EPS)
            xh = x * rr
            dxg = _rope_t(dy, a, bm, bp)
            dxh = dxg * gain
            dx = rr * (dxh - xh * (_half_sums(dxh * xh) * (1.0 / HEAD)))
            return dx, jnp.sum(dxg * xh, axis=0, keepdims=True)

        def folded(cur_ref, prev_ref, m):
            parts = []
            for h in (2 * m, 2 * m + 1):
                v = cur_ref[:, LANES * h:LANES * (h + 1)] + jnp.where(
                    last, 0.0, prev_ref[:, LANES * h:LANES * (h + 1)])
                parts.append(v + pltpu.roll(v, HEAD, 1))
            return jnp.where(lo, parts[0], parts[1])

        gq_acc = jnp.zeros((1, LANES), F32)
        for r in range(ATTN_W // LANES):
            rc = slice(LANES * r, LANES * (r + 1))
            dx, gg = norm_bwd(z_ref[:, rc], dq_ref[:, rc], gq_ref[...])
            dz_ref[:, rc] = dx.astype(BF16)
            gq_acc = gq_acc + gg
        acc_ref[0:1, :] += gq_acc
        gk_acc = jnp.zeros((1, LANES), F32)
        for m in range(KV_W // LANES):
            kc = slice(ATTN_W + LANES * m, ATTN_W + LANES * (m + 1))
            dx, gg = norm_bwd(z_ref[:, kc], folded(dkc_ref, dkp_ref, m), gk_ref[...])
            dz_ref[:, kc] = dx.astype(BF16)
            gk_acc = gk_acc + gg
            vc = slice(ATTN_W + KV_W + LANES * m, ATTN_W + KV_W + LANES * (m + 1))
            dz_ref[:, vc] = folded(dvc_ref, dvp_ref, m).astype(BF16)
        acc_ref[1:2, :] += gk_acc
        dzt_ref[...] = dz_ref[...].T

    cur = lambda w: pl.BlockSpec((BLK, w), lambda n: (n, 0))
    nxt = lambda w: pl.BlockSpec((BLK, w), lambda n: (jnp.minimum(n + 1, nb - 1), 0))
    one = pl.BlockSpec((1, LANES), lambda n: (0, 0))
    return pl.pallas_call(
        body, name="qkv_bwd",
        out_shape=(jax.ShapeDtypeStruct(dz.shape, dz.dtype), jax.ShapeDtypeStruct(dzt.shape, dzt.dtype),
                   jax.ShapeDtypeStruct((SUBLANES, LANES), F32)),
        grid=(nb,),
        in_specs=[cur(PAIR_W), ANY, ANY, cur(ATTN_W), cur(K2_W), nxt(K2_W), cur(K2_W), nxt(K2_W),
                  cur(LANES), cur(LANES), cur(LANES), one, one],
        out_specs=(cur(QKV_W), pl.BlockSpec((QKV_W, BLK), lambda n: (0, n)),
                   pl.BlockSpec((SUBLANES, LANES), lambda n: (0, 0))),
        input_output_aliases={1: 0, 2: 1},
        compiler_params=_params("arbitrary"))(z, dz, dzt, dq, dkc, dkp, dvc, dvp, ra, rbm, rbp, gq2, gk2)


def _in_bwd(dz, w_pairs, x, dx1, g1, tm, after):
    s = x.shape[0]
    n = s // tm
    sub = tm // N_PAIRS
    stripes = 4

    def body(d_ref, w_ref, x_ref, dx1_ref, g_ref, after_ref, gx_ref, acc_ref, dh_ref):
        i, k = pl.program_id(0), pl.program_id(1)

        def matmul(c):
            cols = slice(c * (D_MODEL // stripes), (c + 1) * (D_MODEL // stripes))
            dh_ref[i % 2, :, cols] += _dot(d_ref[...], w_ref[0, :, cols])

        def norm_bwd(c):
            part = sub // stripes
            mine = slice(c * part, (c + 1) * part)
            rows = pl.ds(pl.multiple_of(k * sub + c * part, part), part)
            dh = dh_ref[(i + 1) % 2, rows, :]
            dh_ref[(i + 1) % 2, rows, :] = jnp.zeros_like(dh)
            xn, r = _rms(x_ref[mine, :])
            gx_ref[rows, :] = dx1_ref[mine, :] + _rms_bwd(dh * g_ref[...], xn, r)
            acc_ref[0:1, :] += jnp.sum(dh * xn, axis=0, keepdims=True)

        @pl.when((i == 0) & (k == 0))
        def _():
            acc_ref[...] = jnp.zeros_like(acc_ref)
            dh_ref[...] = jnp.zeros_like(dh_ref)

        @pl.when(i == 0)
        def _():
            for c in range(stripes):
                matmul(c)

        @pl.when((i > 0) & (i < n))
        def _():
            for c in range(stripes):
                matmul(c)
                norm_bwd(c)

        @pl.when(i == n)
        def _():
            for c in range(stripes):
                norm_bwd(c)

    last = lambda i, k: jnp.where(i == n, N_PAIRS - 1, k)
    rows_before = lambda i, k: (jnp.maximum(i - 1, 0) * N_PAIRS + k, 0)
    return pl.pallas_call(
        body, name="in_bwd",
        out_shape=(jax.ShapeDtypeStruct((s, D_MODEL), F32), jax.ShapeDtypeStruct((SUBLANES, D_MODEL), F32)),
        grid=(n + 1, N_PAIRS),
        in_specs=[pl.BlockSpec((tm, PAIR_W), lambda i, k: (jnp.minimum(i, n - 1), last(i, k))),
                  pl.BlockSpec((1, PAIR_W, D_MODEL), lambda i, k: (last(i, k), 0, 0)),
                  pl.BlockSpec((sub, D_MODEL), rows_before), pl.BlockSpec((sub, D_MODEL), rows_before),
                  pl.BlockSpec((1, D_MODEL), lambda i, k: (0, 0)), ANY],
        out_specs=(pl.BlockSpec((tm, D_MODEL), lambda i, k: (jnp.maximum(i - 1, 0), 0)),
                   pl.BlockSpec((SUBLANES, D_MODEL), lambda i, k: (0, 0))),
        scratch_shapes=[pltpu.VMEM((2, tm, D_MODEL), F32)],
        compiler_params=_params("arbitrary", "arbitrary"))(dz, w_pairs, x, dx1, g1, after)


def _mm_grad(at, bs, tn, name):
    m, kdim = at.shape
    nblk = [b.shape[1] // tn for b in bs]
    starts = [sum(nblk[:t]) for t in range(len(bs))]

    def body(a_ref, *refs):
        b_refs, o_ref = refs[:len(bs)], refs[len(bs)]
        j = pl.program_id(0)
        for t, b_ref in enumerate(b_refs):
            @pl.when((j >= starts[t]) & (j < starts[t] + nblk[t]))
            def _():
                o_ref[...] = _dot(a_ref[...], b_ref[...]).astype(BF16)

    def b_spec(t):
        return pl.BlockSpec((kdim, tn), lambda j: (0, jnp.clip(j - starts[t], 0, nblk[t] - 1)))

    return pl.pallas_call(
        body, name=name,
        out_shape=jax.ShapeDtypeStruct((m, sum(nblk) * tn), BF16),
        grid=(sum(nblk),),
        in_specs=[_resident((m, kdim))] + [b_spec(t) for t in range(len(bs))],
        out_specs=pl.BlockSpec((m, tn), lambda j: (0, j)),
        compiler_params=_params("parallel"))(at, *bs)


def _grad_w_in(dzt, h):
    kdim = h.shape[0]

    def body(d_ref, h_ref, o_ref):
        o_ref[0] = _dot(d_ref[...], h_ref[...]).astype(BF16)

    return pl.pallas_call(
        body, name="grad_w_in",
        out_shape=jax.ShapeDtypeStruct((N_DEV, SHARD_IN, D_MODEL), BF16),
        grid=(N_DEV,),
        in_specs=[pl.BlockSpec((SHARD_IN, kdim), lambda j: (j, 0)), _resident((kdim, D_MODEL))],
        out_specs=pl.BlockSpec((1, SHARD_IN, D_MODEL), lambda j: (j, 0, 0)),
        compiler_params=_params("parallel"))(dzt, h)


def _place():
    return lax.axis_index("x"), lax.axis_index("y"), lax.axis_index("c")


ROW_TAPS, ROW_MISC = 4, 5
Q_AT, K_AT, SINK_AT, LOSS_AT = (ATTN_W + LANES * t for t in range(4))
SMALL_AT = [(0, 0), (1, 0), (2, 0), (3, 0), (ROW_MISC, Q_AT), (ROW_MISC, K_AT), (ROW_MISC, SINK_AT)]


def _tap_at(tap):
    return ROW_TAPS + tap // 2, ATTN_W * (tap % 2)


def _reduce_small(acc_g1, acc_g2, acc_ple, acc_qk, acc_attn):
    def body(g1_ref, g2_ref, ple_ref, qk_ref, attn_ref, out_ref, slab_ref, gath_ref, send_sems, recv_sems):
        x, y, c = _place()
        me = 4 * x + 2 * y + c
        slab_ref[...] = jnp.zeros_like(slab_ref)
        slab_ref[0:1, :] = g1_ref[0:1, :]
        slab_ref[1:2, :] = g2_ref[0:1, :]
        slab_ref[2:4, :] = ple_ref[0:2, :]
        qk = qk_ref[0:2, :]
        qk = jnp.where(_low_half(qk.shape), qk + pltpu.roll(qk, HEAD, 1), 0.0)
        misc = slab_ref.at[ROW_MISC:ROW_MISC + 1]
        misc[:, Q_AT:Q_AT + LANES] = qk[0:1]
        misc[:, K_AT:K_AT + LANES] = qk[1:2]
        lane = lax.broadcasted_iota(jnp.int32, (1, LANES), 1)
        misc[:, SINK_AT:SINK_AT + LANES] = jnp.where(lane < N_Q_HEADS, attn_ref[0:1, 0:LANES], 0.0)
        misc[:, LOSS_AT:LOSS_AT + LANES] = sum(
            ple_ref[2:3, LANES * t:LANES * (t + 1)] for t in range(D_MODEL // LANES))
        for tap in range(3):
            row, at = _tap_at(tap)
            slab_ref[row:row + 1, at:at + ATTN_W] = attn_ref[1 + tap:2 + tap, :]
        gath_ref[me] = slab_ref[...]
        copies = []
        for k in range(1, N_DEV):
            peer = (x ^ (k >> 2), y ^ ((k >> 1) & 1), c ^ (k & 1))
            copies.append(pltpu.make_async_remote_copy(
                src_ref=slab_ref, dst_ref=gath_ref.at[me], send_sem=send_sems.at[k - 1],
                recv_sem=recv_sems.at[k - 1], device_id=peer, device_id_type=MESH))
        for cp in copies:
            cp.start()
        for cp in copies:
            cp.wait_recv()
        for cp in copies:
            cp.wait_send()
        total = gath_ref[0]
        for d in range(1, N_DEV):
            total = total + gath_ref[d]
        out_ref[...] = total

    vmem = pl.BlockSpec(memory_space=pltpu.VMEM)
    return pl.pallas_call(
        body, name="reduce_small",
        out_shape=jax.ShapeDtypeStruct((SLAB_ROWS, D_MODEL), F32),
        in_specs=[vmem] * 5, out_specs=vmem,
        scratch_shapes=[pltpu.VMEM((SLAB_ROWS, D_MODEL), F32), pltpu.VMEM((N_DEV, SLAB_ROWS, D_MODEL), F32),
                        pltpu.SemaphoreType.DMA((N_DEV - 1,)), pltpu.SemaphoreType.DMA((N_DEV - 1,))])(
            acc_g1, acc_g2, acc_ple, acc_qk, acc_attn)


def _pair_sum(g, r, place, tr, name):
    _, _, rows, cols = g.shape

    def body(place_ref, g_ref, r_ref, pb_ref, own_ref):
        tot = g_ref[0, 0].astype(F32) + r_ref[0].astype(F32)
        pb_ref[0] = tot.astype(BF16)

        @pl.when(pl.program_id(1) == place_ref[1])
        def _():
            own_ref[...] = tot

    grid_spec = pltpu.PrefetchScalarGridSpec(
        num_scalar_prefetch=1, grid=(rows // tr, 4),
        in_specs=[pl.BlockSpec((1, 1, tr, cols), lambda i, q, place_ref: (q, place_ref[0], i, 0)),
                  pl.BlockSpec((1, tr, cols), lambda i, q, place_ref: (q, i, 0))],
        out_specs=(pl.BlockSpec((1, tr, cols), lambda i, q, place_ref: (q, i, 0)),
                   pl.BlockSpec((tr, cols), lambda i, q, place_ref: (i, 0))))
    return pl.pallas_call(
        body, name=name, grid_spec=grid_spec,
        out_shape=(jax.ShapeDtypeStruct((4, rows, cols), BF16), jax.ShapeDtypeStruct((rows, cols), F32)),
        compiler_params=_params("arbitrary", "arbitrary"))(place, g, r)


HBM = pl.BlockSpec(memory_space=pltpu.HBM)
SEM = pl.BlockSpec(memory_space=pltpu.SEMAPHORE)
SIDE_EFFECT = pltpu.CompilerParams(has_side_effects=pltpu.SideEffectType.DATAFLOW_SIDE_EFFECTING)
TOKEN = jax.ShapeDtypeStruct((SUBLANES, LANES), F32)


def _hbm(a):
    return pltpu.with_memory_space_constraint(a, pltpu.HBM)


def _hbm_like(arrays):
    return tuple(pltpu.HBM(a.shape, a.dtype) for a in arrays)


def _block_of(px, py, pc):
    return 4 * px + 2 * py + pc


def _relay_parts(rows):
    if rows % (2 * PACKED_ROWS):
        return [pl.ds(0, rows), None]
    return [pl.ds(0, rows // 2), pl.ds(rows // 2, rows // 2)]


def _gather_start(shards, after, relay=False):
    na = len(shards)
    lands = [_hbm(lax.empty((N_DEV,) + a.shape, a.dtype)) for a in shards]

    def body(*refs):
        ins, land = refs[:na], refs[na:2 * na]
        send_sems, recv_ici, recv_d2d = refs[2 * na + 1:2 * na + 4]
        token = refs[-1]
        x, y, c = _place()
        peers = [(x, y, 1 - c), (1 - x, y, c), (x, 1 - y, c), (1 - x, 1 - y, c)]
        for k, peer in enumerate(peers[:3] if relay else peers):
            for t in range(na):
                pltpu.make_async_remote_copy(
                    src_ref=ins[t], dst_ref=land[t].at[_block_of(x, y, c)], send_sem=send_sems.at[4 * t + k],
                    recv_sem=recv_d2d.at[4 * t] if k == 0 else recv_ici.at[3 * t + k - 1],
                    device_id=peer, device_id_type=MESH).start()
        token[...] = jnp.zeros_like(token)

    out = pl.pallas_call(
        body, name="gather_start",
        out_shape=(pltpu.SemaphoreType.DMA((4 * na,)), pltpu.SemaphoreType.DMA((3 * na,)),
                   pltpu.SemaphoreType.DMA((4 * na,)), pltpu.SemaphoreType.DMA((2 * na,)), *_hbm_like(lands), TOKEN),
        in_specs=[ANY] * na + [HBM] * na + [ANY],
        out_specs=(SEM, SEM, SEM, SEM, *[HBM] * na, pl.BlockSpec(memory_space=pltpu.VMEM)),
        input_output_aliases={na + i: 4 + i for i in range(na)},
        compiler_params=SIDE_EFFECT)(*shards, *lands, after)
    send_sems, recv_ici, recv_d2d, recv_relay = out[:4]
    state = dict(send=send_sems, ici=recv_ici, d2d=recv_d2d, relay=recv_relay, relayed=relay, shards=list(shards),
                 lands=out[4:4 + na])
    return state, out[-1]


def _gather_forward(state, after):
    lands = state["lands"]
    na = len(lands)

    def body(*refs):
        land = refs[:na]
        recv_ici, recv_d2d = refs[na], refs[na + 1]
        fwd_sems, token = refs[-2], refs[-1]
        x, y, c = _place()
        for j, chip in enumerate([(1 - x, y), (x, 1 - y), (1 - x, 1 - y)]):
            for t in range(na):
                blk = land[t].at[_block_of(*chip, c)]
                pltpu.make_async_remote_copy(
                    src_ref=blk, dst_ref=blk, send_sem=fwd_sems.at[3 * t + j], recv_sem=recv_ici.at[3 * t + j],
                    device_id=(x, y, c), device_id_type=MESH).wait_recv()
                pltpu.make_async_remote_copy(
                    src_ref=blk, dst_ref=blk, send_sem=fwd_sems.at[3 * t + j], recv_sem=recv_d2d.at[4 * t + 1 + j],
                    device_id=(x, y, 1 - c), device_id_type=MESH).start()
        token[...] = jnp.zeros_like(token)

    out = pl.pallas_call(
        body, name="gather_forward",
        out_shape=(*_hbm_like(lands), pltpu.SemaphoreType.DMA((3 * na,)), TOKEN),
        in_specs=[HBM] * na + [SEM, SEM, ANY],
        out_specs=(*[HBM] * na, SEM, pl.BlockSpec(memory_space=pltpu.VMEM)),
        input_output_aliases={i: i for i in range(na)},
        compiler_params=SIDE_EFFECT)(*lands, state["ici"], state["d2d"], after)
    return dict(state, lands=out[:na], fwd=out[na]), out[-1]


def _gather_wait(state, after):
    shards, lands = state["shards"], state["lands"]
    na = len(lands)

    def body(*refs):
        ins, land = refs[:na], refs[na:2 * na]
        send_sems, fwd_sems, recv_d2d = refs[2 * na:2 * na + 3]
        x, y, c = _place()
        chips = [(1 - x, y), (x, 1 - y), (1 - x, 1 - y)]
        for t in range(na):
            mine = land[t].at[_block_of(x, y, c)]
            for k in range(4):
                pltpu.make_async_remote_copy(
                    src_ref=ins[t], dst_ref=mine, send_sem=send_sems.at[4 * t + k], recv_sem=recv_d2d.at[4 * t],
                    device_id=(x, y, c), device_id_type=MESH).wait_send()
            for j, chip in enumerate(chips):
                blk = land[t].at[_block_of(*chip, c)]
                pltpu.make_async_remote_copy(
                    src_ref=blk, dst_ref=blk, send_sem=fwd_sems.at[3 * t + j], recv_sem=recv_d2d.at[4 * t + 1 + j],
                    device_id=(x, y, c), device_id_type=MESH).wait_send()
            for k, blk_id in enumerate([_block_of(x, y, 1 - c)] + [_block_of(*chip, 1 - c) for chip in chips]):
                blk = land[t].at[blk_id]
                pltpu.make_async_remote_copy(
                    src_ref=blk, dst_ref=blk, send_sem=send_sems.at[4 * t], recv_sem=recv_d2d.at[4 * t + k],
                    device_id=(x, y, c), device_id_type=MESH).wait_recv()

    out = pl.pallas_call(
        body, name="gather_wait",
        out_shape=_hbm_like(lands),
        in_specs=[ANY] * na + [HBM] * na + [SEM, SEM, SEM, ANY],
        out_specs=tuple([HBM] * na),
        input_output_aliases={na + i: i for i in range(na)},
        compiler_params=SIDE_EFFECT)(*shards, *lands, state["send"], state["fwd"], state["d2d"], after)
    return out


def _gather_from_sibling(state, after):
    lands = state["lands"]
    na = len(lands)

    def body(*refs):
        land, recv_d2d = refs[:na], refs[na]
        x, y, c = _place()
        for t in range(na):
            blk = land[t].at[_block_of(x, y, 1 - c)]
            pltpu.make_async_remote_copy(src_ref=blk, dst_ref=blk, send_sem=recv_d2d.at[4 * t],
                                         recv_sem=recv_d2d.at[4 * t], device_id=(x, y, c),
                                         device_id_type=MESH).wait_recv()

    out = pl.pallas_call(
        body, name="gather_from_sibling", out_shape=_hbm_like(lands),
        in_specs=[HBM] * na + [SEM, ANY], out_specs=tuple([HBM] * na),
        input_output_aliases={i: i for i in range(na)},
        compiler_params=SIDE_EFFECT)(*lands, state["d2d"], after)
    return dict(state, lands=list(out))


def _gather_from_chip(state, j, afters, last):
    shards, lands, relayed = state["shards"], state["lands"], state["relayed"]
    na = len(lands)
    parts = [_relay_parts(a.shape[0]) for a in shards]

    def relay_on(land_ref, t, fwd_sems, recv_relay):
        x, y, c = _place()
        blk = chip_blocks(land_ref)[1].at[parts[t][j]]
        return pltpu.make_async_remote_copy(
            src_ref=blk, dst_ref=blk, send_sem=fwd_sems.at[na + t], recv_sem=recv_relay.at[2 * t + j],
            device_id=[(x, 1 - y, c), (1 - x, y, c)][j], device_id_type=MESH)

    def chip_blocks(land_ref):
        x, y, c = _place()
        chip = [(1 - x, y), (x, 1 - y), (1 - x, 1 - y)][j]
        return (x, y, c), land_ref.at[_block_of(*chip, c)], land_ref.at[_block_of(*chip, 1 - c)]

    def forward(*refs):
        land, recv_ici, recv_d2d, recv_relay, fwd_sems = refs[:na], refs[na], refs[na + 1], refs[na + 2], refs[-1]
        for t in range(na):
            (x, y, c), mine, _ = chip_blocks(land[t])
            if relayed and j == 2:
                for half, rows in enumerate(parts[t]):
                    if rows is not None:
                        pltpu.make_async_remote_copy(
                            src_ref=mine.at[rows], dst_ref=mine.at[rows], send_sem=fwd_sems.at[t],
                            recv_sem=recv_relay.at[2 * t + half], device_id=(x, y, c),
                            device_id_type=MESH).wait_recv()
            else:
                pltpu.make_async_remote_copy(src_ref=mine, dst_ref=mine, send_sem=fwd_sems.at[t],
                                             recv_sem=recv_ici.at[3 * t + j], device_id=(x, y, c),
                                             device_id_type=MESH).wait_recv()
            pltpu.make_async_remote_copy(src_ref=mine, dst_ref=mine, send_sem=fwd_sems.at[t],
                                         recv_sem=recv_d2d.at[4 * t + 1 + j], device_id=(x, y, 1 - c),
                                         device_id_type=MESH).start()
            if relayed and j < 2 and parts[t][j] is not None:
                relay_on(land[t], t, fwd_sems, recv_relay).start()

    out = pl.pallas_call(
        forward, name="gather_pass_chip_" + str(j),
        out_shape=(*_hbm_like(lands), pltpu.SemaphoreType.DMA((2 * na,))),
        in_specs=[HBM] * na + [SEM, SEM, SEM] + [ANY] * len(afters), out_specs=(*[HBM] * na, SEM),
        input_output_aliases={i: i for i in range(na)},
        compiler_params=SIDE_EFFECT)(*lands, state["ici"], state["d2d"], state["relay"], *afters)
    lands, fwd_sems = out[:na], out[na]

    def arrive(*refs):
        land, fwd_sems, recv_d2d = refs[:na], refs[na], refs[na + 1]
        shard, send_sems, recv_relay = refs[na + 2:2 * na + 2], refs[2 * na + 2], refs[2 * na + 3]
        for t in range(na):
            (x, y, c), mine, theirs = chip_blocks(land[t])
            pltpu.make_async_remote_copy(src_ref=theirs, dst_ref=theirs, send_sem=fwd_sems.at[t],
                                         recv_sem=recv_d2d.at[4 * t + 1 + j], device_id=(x, y, c),
                                         device_id_type=MESH).wait_recv()
            pltpu.make_async_remote_copy(src_ref=mine, dst_ref=mine, send_sem=fwd_sems.at[t],
                                         recv_sem=recv_d2d.at[4 * t + 1 + j], device_id=(x, y, c),
                                         device_id_type=MESH).wait_send()
            if relayed and j < 2 and parts[t][j] is not None:
                relay_on(land[t], t, fwd_sems, recv_relay).wait_send()
            for k in range((3 if relayed else 4) if last else 0):
                pltpu.make_async_remote_copy(
                    src_ref=shard[t], dst_ref=land[t].at[_block_of(x, y, c)], send_sem=send_sems.at[4 * t + k],
                    recv_sem=recv_d2d.at[4 * t], device_id=(x, y, c), device_id_type=MESH).wait_send()

    out = pl.pallas_call(
        arrive, name="gather_take_chip_" + str(j), out_shape=_hbm_like(lands),
        in_specs=[HBM] * na + [SEM, SEM] + [ANY] * na + [SEM, SEM], out_specs=tuple([HBM] * na),
        input_output_aliases={i: i for i in range(na)},
        compiler_params=SIDE_EFFECT)(*lands, fwd_sems, state["d2d"], *shards, state["send"], state["relay"])
    return dict(state, lands=list(out))


def _to_sibling(srcs, lands, send_sems, recv_sems):
    x, y, c = _place()
    return [pltpu.make_async_remote_copy(
        src_ref=srcs[t].at[:, 1 - c], dst_ref=lands[t], send_sem=send_sems.at[t], recv_sem=recv_sems.at[t],
        device_id=(x, y, 1 - c), device_id_type=MESH) for t in range(len(srcs))]


def _to_chips(srcs, lands, send_sems, recv_sems):
    x, y, c = _place()
    copies = []
    for k in (1, 2, 3):
        px, py = x ^ (k >> 1), y ^ (k & 1)
        copies += [pltpu.make_async_remote_copy(
            src_ref=srcs[t].at[2 * px + py], dst_ref=lands[t].at[k - 1], send_sem=send_sems.at[3 * t + k - 1],
            recv_sem=recv_sems.at[3 * t + k - 1], device_id=(px, py, c), device_id_type=MESH) for t in range(len(srcs))]
    return copies


def _exchange_start(name, srcs, land_shapes, copies, per_array, after):
    na = len(srcs)
    lands = [_hbm(lax.empty(shp, a.dtype)) for shp, a in zip(land_shapes, srcs)]

    def body(*refs):
        token = refs[-1]
        for cp in copies(refs[:na], refs[na:2 * na], refs[2 * na + 1], refs[2 * na + 2]):
            cp.start()
        token[...] = jnp.zeros_like(token)

    out = pl.pallas_call(
        body, name=name,
        out_shape=(pltpu.SemaphoreType.DMA((na * per_array,)), pltpu.SemaphoreType.DMA((na * per_array,)),
                   *_hbm_like(lands), TOKEN),
        in_specs=[ANY] * na + [HBM] * na + [ANY],
        out_specs=(SEM, SEM, *[HBM] * na, pl.BlockSpec(memory_space=pltpu.VMEM)),
        input_output_aliases={na + i: 2 + i for i in range(na)},
        compiler_params=SIDE_EFFECT)(*srcs, *lands, after)
    return dict(send=out[0], recv=out[1], srcs=list(srcs), lands=out[2:2 + na]), out[-1]


def _exchange_wait(name, state, copies, afters):
    srcs, lands = state["srcs"], state["lands"]
    na = len(srcs)

    def body(*refs):
        for cp in copies(refs[:na], refs[na:2 * na], refs[2 * na], refs[2 * na + 1]):
            cp.wait_send()
            cp.wait_recv()

    out = pl.pallas_call(
        body, name=name,
        out_shape=_hbm_like(lands),
        in_specs=[ANY] * na + [HBM] * na + [SEM, SEM] + [ANY] * len(afters),
        out_specs=tuple([HBM] * na),
        input_output_aliases={na + i: i for i in range(na)},
        compiler_params=SIDE_EFFECT)(*srcs, *lands, state["send"], state["recv"], *afters)
    return out


def _adamw_math(w, g, m, v):
    m = ADAM_B1 * m + (1.0 - ADAM_B1) * g
    v = ADAM_B2 * v + (1.0 - ADAM_B2) * (g * g)
    m_hat = m / (1.0 - ADAM_B1 ** ADAM_STEP)
    v_hat = v / (1.0 - ADAM_B2 ** ADAM_STEP)
    return -ADAM_LR * (m_hat / (jnp.sqrt(v_hat) + ADAM_EPS) + ADAM_WD * w), m, v


def _adamw(own, others, w, m, v, tr, name, after):
    rows, cols = w.shape
    blk = pl.BlockSpec((tr, cols), lambda i: (i, 0))

    def body(own_ref, oth_ref, w_ref, m_ref, v_ref, after_ref, g_ref, d_ref, nm_ref, nv_ref):
        g = own_ref[...]
        for k in range(3):
            g = g + oth_ref[k].astype(F32)
        g_ref[...] = g
        d_ref[...], nm_ref[...], nv_ref[...] = _adamw_math(w_ref[...], g, m_ref[...], v_ref[...])

    out = jax.ShapeDtypeStruct((rows, cols), F32)
    return pl.pallas_call(
        body, name=name, out_shape=(out, out, out, out), grid=(rows // tr,),
        in_specs=[blk, pl.BlockSpec((3, tr, cols), lambda i: (0, i, 0)), blk, blk, blk, ANY],
        out_specs=(blk, blk, blk, blk),
        compiler_params=_params("parallel"))(own, others, w, m, v, after)


def _adamw_small(red, me, params, moments1, moments2):
    n = len(params)

    def body(me_ref, red_ref, *refs):
        ws, ms, vs = refs[:n], refs[n:2 * n], refs[2 * n:3 * n]
        loss_ref = refs[3 * n]
        outs = refs[3 * n + 1:]
        loss_ref[...] = jnp.sum(red_ref[ROW_MISC:ROW_MISC + 1, LOSS_AT:LOSS_AT + LANES], axis=-1, keepdims=True)
        for t, (row, at) in enumerate(SMALL_AT):
            g = red_ref[row:row + 1, at:at + ws[t].shape[1]]
            d, nm, nv = _adamw_math(ws[t][...], g, ms[t][...], vs[t][...])
            for o, val in zip(outs[4 * t:4 * t + 4], (g, d, nm, nv)):
                o[...] = val
        for tap in range(ws[-1].shape[0]):
            row, at = _tap_at(tap)
            g = red_ref[row:row + 1, pl.ds(pl.multiple_of(at + me_ref[0, 0] * LANES, LANES), LANES)]
            d, nm, nv = _adamw_math(ws[-1][tap], g, ms[-1][tap], vs[-1][tap])
            for o, val in zip(outs[4 * (n - 1):], (g, d, nm, nv)):
                o[tap] = val

    vmem = pl.BlockSpec(memory_space=pltpu.VMEM)
    shapes = [jax.ShapeDtypeStruct(w.shape, F32) for w in params for _ in range(4)]
    out = pl.pallas_call(
        body, name="adamw_small", out_shape=(jax.ShapeDtypeStruct((1, 1), F32), *shapes),
        in_specs=[pl.BlockSpec(memory_space=pltpu.SMEM), vmem] + [vmem] * (3 * n),
        out_specs=tuple([vmem] * (1 + 4 * n)))(me, red, *params, *moments1, *moments2)
    return out[0], [list(out[1 + k::4]) for k in range(4)]


def _tables(s, gq, gk, conv_w):
    gq2 = jnp.tile(gq.reshape(1, HEAD), (1, 2))
    gk2 = jnp.tile(gk.reshape(1, HEAD), (1, 2))
    conv_wp = jnp.pad(conv_w, ((0, SUBLANES - conv_w.shape[0]), (0, 0)))
    return _rope_tables(s), gq2, gk2, conv_wp


def _pair_id(q):
    return jnp.array([q, 0], jnp.int32)


def _forward_in(x, g1, shards):
    s = x.shape[0]
    h = _prenorm(x, g1, min(512, s), x)
    z, w_pairs = lax.empty((s, IN_W), F32), lax.empty((N_PAIRS, PAIR_W, D_MODEL), BF16)
    for q in range(N_PAIRS):
        z, w_pairs = _fwd_in_pair(h, shards, z, w_pairs, _pair_id(q), min(512, s), "fwd_in_" + str(q),
                                  own=shards[0] if q == 0 else None)
    return h, z, w_pairs


def _forward_attn(z, rope, gq2, gk2, conv_wp, sinks):
    s = z.shape[0]
    qn, k2, v2 = _qk_prep(z, *rope, gq2, gk2, min(256, s), z)
    a, mix, mixt = _attn_fwd(qn, k2, v2, z, conv_wp, sinks, qn)
    return qn, k2, v2, a, mix, mixt


def _forward_out(x, p, target, mix, mixt, w_out, g2, w_pg, b_pg, w_pp, g3):
    s = x.shape[0]
    tm = min(512, s)
    x1, hn2, hn2t = _fwd_out(mix, w_out, x, g2, tm)
    dy, dgp, dt, pt, acc_ple = _ple(hn2, w_pg, b_pg, p, w_pp, g3, x1, target, min(256, s))
    dx1, dx1b, acc_g2 = _gate_bwd(dgp, w_pg, x1, dy, g2, tm)
    gw_out = _mm_grad(mixt, [dx1b], 512, "grad_w_out")
    gw_pg = _mm_grad(hn2t, [dgp], 512, "grad_w_ple_gate")
    gw_pp = _mm_grad(pt, [dt], 512, "grad_w_ple_proj")
    return dx1, dx1b, (gw_out, gw_pg, gw_pp), acc_ple, acc_g2


def _backward_attn(dmix, h, z, qn, k2, v2, a, rope, gq2, gk2, conv_wp, sinks, after):
    dq, dkc, dkp, dvc, dvp, dz, dzt, acc_attn = _attn_bwd(qn, k2, v2, a, z, dmix, conv_wp, sinks, after)
    dz, dzt, acc_qk = _qkv_bwd(z, dz, dzt, dq, dkc, dkp, dvc, dvp, *rope, gq2, gk2)
    return dz, _grad_w_in(dzt, h), acc_attn, acc_qk


def _local_step(x, p, target, g1, shards, gq, gk, sinks, conv_w, w_out, g2, w_pg, b_pg, w_pp, g3):
    rope, gq2, gk2, conv_wp = _tables(x.shape[0], gq, gk, conv_w)
    h, z, w_pairs = _forward_in(x, g1, shards)
    qn, k2, v2, a, mix, mixt = _forward_attn(z, rope, gq2, gk2, conv_wp, sinks)
    dx1, dx1b, (gw_out, gw_pg, gw_pp), acc_ple, acc_g2 = _forward_out(
        x, p, target, mix, mixt, w_out, g2, w_pg, b_pg, w_pp, g3)
    dmix = _mm_nt(dx1b, w_out, min(512, x.shape[0]), "out_bwd", dx1b)
    dz, gw_in, acc_attn, acc_qk = _backward_attn(dmix, h, z, qn, k2, v2, a, rope, gq2, gk2, conv_wp, sinks, dmix)
    grad_x, acc_g1 = _in_bwd(dz, w_pairs, x, dx1, g1, min(512, x.shape[0]), dx1)
    return grad_x, (gw_in, gw_out, gw_pg, gw_pp), (acc_g1, acc_g2, acc_ple, acc_qk, acc_attn)


def _by_owner(g):
    return g.reshape((4, 2) + g.shape[1:])


def kernel(x, p, norm_gain, w_in, q_norm_gain, k_norm_gain, attn_sinks, conv_w, w_out, ple_gate_norm_gain, w_ple_gate, b_ple_gate, w_ple_proj, ple_norm_gain, loss_target, m_norm_gain, m_w_in, m_q_norm_gain, m_k_norm_gain, m_attn_sinks, m_conv_w, m_w_out, m_ple_gate_norm_gain, m_w_ple_gate, m_b_ple_gate, m_w_ple_proj, m_ple_norm_gain, v_norm_gain, v_w_in, v_q_norm_gain, v_k_norm_gain, v_attn_sinks, v_conv_w, v_w_out, v_ple_gate_norm_gain, v_w_ple_gate, v_b_ple_gate, v_w_ple_proj, v_ple_norm_gain):
    me = 4 * lax.axis_index("x") + 2 * lax.axis_index("y") + lax.axis_index("c")
    place = jnp.stack([lax.axis_index("c"), 2 * lax.axis_index("x") + lax.axis_index("y")]).astype(jnp.int32)
    xs, ps, target = x[0], p[0, 0], loss_target[0]

    shard_in = w_in[0].T.astype(BF16)
    own_late = [w_out[0].astype(BF16), w_ple_gate[0].astype(BF16), w_ple_proj[0].astype(BF16)]
    with_own = lambda gathered, own: lax.dynamic_update_slice(gathered, own[None], (me,) + (0,) * own.ndim)
    early, started = _gather_start([shard_in, conv_w[0]], shard_in, relay=True)
    tm = min(512, xs.shape[0])
    h = _prenorm(xs, norm_gain, tm, started)

    z, w_pairs = lax.empty((xs.shape[0], IN_W), F32), lax.empty((N_PAIRS, PAIR_W, D_MODEL), BF16)
    early = _gather_from_sibling(early, h)
    pair_of = lambda flip: jnp.stack([place[1] ^ flip, place[0]])
    z, w_pairs = _fwd_in_pair(h, early["lands"][0], z, w_pairs, pair_of(0), tm, "fwd_in_own", own=shard_in)
    for j, flip in enumerate((2, 1, 3)):
        early = _gather_from_chip(early, j, (z,) if j != 2 else (z, started_late), last=j == 2)
        z, w_pairs = _fwd_in_pair(h, early["lands"][0], z, w_pairs, pair_of(flip), tm, "fwd_in_chip_" + str(j))
        if j == 1:
            late, started_late = _gather_start(own_late, z)
    conv_full = jnp.transpose(with_own(early["lands"][1], conv_w[0]), (1, 0, 2)).reshape(3, ATTN_W)
    rope, gq2, gk2, conv_wp = _tables(xs.shape[0], q_norm_gain[0], k_norm_gain[0], conv_full)
    qn, k2, v2 = _qk_prep(z, *rope, gq2, gk2, min(256, xs.shape[0]), z)
    late, forwarded = _gather_forward(late, qn)
    a, mix, mixt = _attn_fwd(qn, k2, v2, z, conv_wp, attn_sinks, forwarded)
    g_out, g_pg, g_pp = (with_own(g, own) for g, own in zip(_gather_wait(late, mix), own_late))
    w_out_f = g_out.reshape(D_MODEL, D_MODEL)
    w_pg_f = g_pg.reshape(D_MODEL, D_MODEL)
    w_pp_f = jnp.transpose(g_pp, (1, 0, 2)).reshape(PLE_DIM, D_MODEL)

    dx1, dx1b, (gw_out, gw_pg, gw_pp), acc_ple, acc_g2 = _forward_out(
        xs, ps, target, mix, mixt, w_out_f, ple_gate_norm_gain, w_pg_f, b_ple_gate, w_pp_f, ple_norm_gain)

    names = ("w_out", "w_ple_gate", "w_ple_proj")
    gw_pp_t = jnp.transpose(gw_pp.reshape(PLE_DIM, N_DEV, PLE_DIM), (1, 0, 2))
    grads = [_by_owner(gw_out.reshape(N_DEV, D_MODEL // N_DEV, D_MODEL)),
             _by_owner(gw_pg.reshape(N_DEV, D_MODEL // N_DEV, D_MODEL)), _by_owner(gw_pp_t)]
    pairs, paired = _exchange_start("pair_start", grads, [(4,) + g.shape[2:] for g in grads], _to_sibling, 1, dx1b)
    dmix = _mm_nt(dx1b, w_out_f, tm, "out_bwd", paired)
    from_sibling = _exchange_wait("pair_wait", pairs, _to_sibling, (dmix,))
    sums = [_pair_sum(g, r, place, 256, "pair_sum_" + nm) for g, r, nm in zip(pairs["srcs"], from_sibling, names)]
    chips, sent = _exchange_start("chip_start", [pb for pb, _ in sums], [(3,) + pb.shape[1:] for pb, _ in sums],
                                  _to_chips, 3, sums[-1][1])

    dz, gw_in, acc_attn, acc_qk = _backward_attn(
        dmix, h, z, qn, k2, v2, a, rope, gq2, gk2, conv_wp, attn_sinks, sent)

    gw_in_t = [_by_owner(gw_in)]
    pairs_in, paired_in = _exchange_start("pair_start_w_in", gw_in_t, [(4,) + gw_in_t[0].shape[2:]], _to_sibling, 1,
                                          gw_in)
    from_chips = _exchange_wait("chip_wait", chips, _to_chips, (gw_in,))
    big = {}
    for (_, own), oth, w, m, v, nm in zip(sums, from_chips, (w_out, w_ple_gate, w_ple_proj),
                                          (m_w_out, m_w_ple_gate, m_w_ple_proj),
                                          (v_w_out, v_w_ple_gate, v_w_ple_proj), names):
        big[nm] = [t[None] for t in _adamw(own, oth, w[0], m[0], v[0], 256, "adamw_" + nm, paired_in)]

    (from_sibling_in,) = _exchange_wait("pair_wait_w_in", pairs_in, _to_sibling, [big[nm][0] for nm in names])
    pb_in, own_in = _pair_sum(pairs_in["srcs"][0], from_sibling_in, place, SHARD_IN // 2, "pair_sum_w_in")
    chips_in, sent_in = _exchange_start("chip_start_w_in", [pb_in], [(3,) + pb_in.shape[1:]], _to_chips, 3, own_in)
    grad_x, acc_g1 = _in_bwd(dz, w_pairs, xs, dx1, norm_gain, tm, sent_in)
    (from_chips_in,) = _exchange_wait("chip_wait_w_in", chips_in, _to_chips, (grad_x,))
    big["w_in"] = [t.T[None] for t in _adamw(own_in, from_chips_in, w_in[0].T, m_w_in[0].T, v_w_in[0].T, SHARD_IN // 4,
                                             "adamw_w_in", grad_x)]

    red = _reduce_small(acc_g1, acc_g2, acc_ple, acc_qk, acc_attn)
    small = [norm_gain, ple_gate_norm_gain, b_ple_gate, ple_norm_gain, q_norm_gain, k_norm_gain, attn_sinks]
    small_m = [m_norm_gain, m_ple_gate_norm_gain, m_b_ple_gate, m_ple_norm_gain, m_q_norm_gain, m_k_norm_gain,
               m_attn_sinks]
    small_v = [v_norm_gain, v_ple_gate_norm_gain, v_b_ple_gate, v_ple_norm_gain, v_q_norm_gain, v_k_norm_gain,
               v_attn_sinks]
    taps_first = lambda t: jnp.transpose(t, (1, 0, 2))
    loss, kinds = _adamw_small(red, me.reshape(1, 1).astype(jnp.int32), small + [taps_first(conv_w)],
                               small_m + [taps_first(m_conv_w)], small_v + [taps_first(v_conv_w)])

    def order(k):
        sm = kinds[k]
        return [sm[0], big["w_in"][k], sm[4], sm[5], sm[6], taps_first(sm[7]), big["w_out"][k], sm[1],
                big["w_ple_gate"][k], sm[2], big["w_ple_proj"][k], sm[3]]

    return (loss[0, 0], grad_x[None], *order(0), *order(1), *order(2), *order(3))
```

```python
import jax
import jax.numpy as jnp
from jax import lax
from jax.experimental import pallas as pl
from jax.experimental.pallas import tpu as pltpu

F32, BF16 = jnp.float32, jnp.bfloat16

D_MODEL = 2048
PLE_DIM = 256
ATTN_W = 1024
HEAD = 64
N_Q_HEADS = 16
KV_W = 256
QKV_W = ATTN_W + 2 * KV_W
REST_W = 5 * 1024
IN_W = QKV_W + REST_W
GATE_A0, CONV_B0, CONV_C0, CONV_H0, GATE_C0 = (QKV_W + 1024 * t for t in range(5))
K2_W = 4 * 128
ROT = 16
ROPE_THETA = 500000.0
EPS = 1e-6
NEG_INF = -1e30
BLK = 128
LANES = 128
SUBLANES = 8
N_DEV = 8
SHARD_IN = IN_W // N_DEV
PAIR_W = 2 * SHARD_IN
N_PAIRS = IN_W // PAIR_W
SLAB_ROWS = 8
PACKED_ROWS = 16
SUB_ROWS = 128
V7X_VMEM_LIMIT = 52 * 1024 * 1024

ADAM_LR, ADAM_B1, ADAM_B2, ADAM_EPS, ADAM_WD, ADAM_STEP = 0.001, 0.9, 0.999, 1e-08, 0.01, 10
MESH = pl.DeviceIdType.MESH


def _params(*semantics):
    return pltpu.CompilerParams(dimension_semantics=semantics, vmem_limit_bytes=V7X_VMEM_LIMIT)


ANY = pl.BlockSpec(memory_space=pl.ANY)


def _resident(shape):
    return pl.BlockSpec(shape, lambda *_: (0,) * len(shape), pipeline_mode=pl.Buffered(1))


def _dot(a, b):
    return jnp.dot(a, b, preferred_element_type=F32)


def _dot_nt(a, b):
    return lax.dot_general(a, b, (((1,), (1,)), ((), ())), preferred_element_type=F32)


def _rms(xf):
    r = lax.rsqrt(jnp.mean(xf * xf, axis=-1, keepdims=True) + EPS)
    return xf * r, r


def _rms_bwd(dxn, xn, r):
    return r * (dxn - xn * jnp.mean(dxn * xn, axis=-1, keepdims=True))


def _sig(g):
    return jax.nn.sigmoid(g)


def _dsilu(g, sg):
    return sg * (1.0 + g * (1.0 - sg))


def _low_half(shape):
    return lax.broadcasted_iota(jnp.int32, shape, len(shape) - 1) < HEAD


def _half_sums(v):
    lo = _low_half(v.shape)
    s_lo = jnp.sum(jnp.where(lo, v, 0.0), axis=-1, keepdims=True)
    s_hi = jnp.sum(jnp.where(lo, 0.0, v), axis=-1, keepdims=True)
    return jnp.where(lo, s_lo, s_hi)


def _rope(v, a, bm, bp):
    return v * a + pltpu.roll(v, LANES - ROT // 2, 1) * bm + pltpu.roll(v, ROT // 2, 1) * bp


def _rope_t(dy, a, bm, bp):
    return dy * a + pltpu.roll(dy * bm, ROT // 2, 1) + pltpu.roll(dy * bp, LANES - ROT // 2, 1)


def _dup_halves(v):
    lo = _low_half(v.shape)
    a = jnp.where(lo, v, 0.0)
    b = jnp.where(lo, 0.0, v)
    return a + pltpu.roll(a, HEAD, 1), b + pltpu.roll(b, HEAD, 1)


def _rope_tables(s):
    half = ROT // 2
    lane = lax.broadcasted_iota(jnp.int32, (s, LANES), 1) % HEAD
    pos = lax.broadcasted_iota(jnp.int32, (half, s), 1).astype(F32)
    freq = lax.broadcasted_iota(jnp.int32, (half, s), 0).astype(F32)
    ang = pos * jnp.power(jnp.float32(ROPE_THETA), -freq * 2.0 / ROT)
    cos, sin = lax.optimization_barrier((jnp.cos(ang), jnp.sin(ang)))
    cos, sin = (jnp.tile(t.T, (1, LANES // half)) for t in (cos, sin))
    a = jnp.where(lane < ROT, cos, 1.0)
    bm = jnp.where(lane < half, -sin, 0.0)
    bp = jnp.where((lane >= half) & (lane < ROT), sin, 0.0)
    return a, bm, bp


def _prenorm(x, g1, tm, after):
    s = x.shape[0]

    def body(x_ref, g_ref, after_ref, h_ref):
        xn, _ = _rms(x_ref[...])
        h_ref[...] = (xn * g_ref[...]).astype(BF16)

    return pl.pallas_call(
        body, name="prenorm",
        out_shape=jax.ShapeDtypeStruct((s, D_MODEL), BF16),
        grid=(s // tm,),
        in_specs=[pl.BlockSpec((tm, D_MODEL), lambda i: (i, 0)), pl.BlockSpec((1, D_MODEL), lambda i: (0, 0)), ANY],
        out_specs=pl.BlockSpec((tm, D_MODEL), lambda i: (i, 0)),
        compiler_params=_params("parallel"))(x, g1, after)


def _fwd_in_pair(h, shards, z, w_pairs, pair, tm, name, own=None):
    s = h.shape[0]

    def body(pair_ref, h_ref, lo_ref, hi_ref, z_in, wp_in, z_ref, wp_ref):
        @pl.when(pl.program_id(0) == 0)
        def _():
            wp_ref[0, 0:SHARD_IN, :] = lo_ref[0]
            wp_ref[0, SHARD_IN:PAIR_W, :] = hi_ref[0]

        z_ref[...] = _dot_nt(h_ref[...], wp_ref[0])

    def body_own(pair_ref, h_ref, own_ref, other_ref, z_in, wp_in, z_ref, wp_ref):
        @pl.when(pl.program_id(0) == 0)
        def _():
            first = pl.multiple_of(pair_ref[1] * SHARD_IN, SHARD_IN)
            wp_ref[0, pl.ds(first, SHARD_IN), :] = own_ref[...]
            wp_ref[0, pl.ds(SHARD_IN - first, SHARD_IN), :] = other_ref[0]

        z_ref[...] = _dot_nt(h_ref[...], wp_ref[0])

    if own is None:
        blocks = [pl.BlockSpec((1, SHARD_IN, D_MODEL), lambda i, p: (2 * p[0], 0, 0)),
                  pl.BlockSpec((1, SHARD_IN, D_MODEL), lambda i, p: (2 * p[0] + 1, 0, 0))]
        operands = (shards, shards)
    else:
        blocks = [pl.BlockSpec((SHARD_IN, D_MODEL), lambda i, p: (0, 0)),
                  pl.BlockSpec((1, SHARD_IN, D_MODEL), lambda i, p: (2 * p[0] + 1 - p[1], 0, 0))]
        operands = (own, shards)
    grid_spec = pltpu.PrefetchScalarGridSpec(
        num_scalar_prefetch=1, grid=(s // tm,),
        in_specs=[pl.BlockSpec((tm, D_MODEL), lambda i, p: (i, 0)), *blocks, ANY, ANY],
        out_specs=(pl.BlockSpec((tm, PAIR_W), lambda i, p: (i, p[0])),
                   pl.BlockSpec((1, PAIR_W, D_MODEL), lambda i, p: (p[0], 0, 0))))
    return pl.pallas_call(
        body if own is None else body_own, name=name, grid_spec=grid_spec,
        out_shape=(jax.ShapeDtypeStruct(z.shape, z.dtype), jax.ShapeDtypeStruct(w_pairs.shape, w_pairs.dtype)),
        input_output_aliases={4: 0, 5: 1},
        compiler_params=_params("arbitrary"))(pair, h, *operands, z, w_pairs)


def _qk_prep(z, ra, rbm, rbp, gq2, gk2, tm, after):
    s = z.shape[0]

    def body(z_ref, a_ref, bm_ref, bp_ref, gq_ref, gk_ref, after_ref, q_ref, k2_ref, v2_ref):
        a, bm, bp = a_ref[...], bm_ref[...], bp_ref[...]
        for r in range(ATTN_W // LANES):
            x = z_ref[:, LANES * r:LANES * (r + 1)]
            rr = lax.rsqrt(_half_sums(x * x) * (1.0 / HEAD) + EPS)
            q_ref[:, LANES * r:LANES * (r + 1)] = _rope(x * rr * gq_ref[...], a, bm, bp).astype(BF16)
        for m in range(KV_W // LANES):
            x = z_ref[:, ATTN_W + LANES * m:ATTN_W + LANES * (m + 1)]
            rr = lax.rsqrt(_half_sums(x * x) * (1.0 / HEAD) + EPS)
            k_lo, k_hi = _dup_halves(_rope(x * rr * gk_ref[...], a, bm, bp))
            k2_ref[:, 2 * LANES * m:2 * LANES * m + LANES] = k_lo.astype(BF16)
            k2_ref[:, 2 * LANES * m + LANES:2 * LANES * (m + 1)] = k_hi.astype(BF16)
            v_lo, v_hi = _dup_halves(z_ref[:, ATTN_W + KV_W + LANES * m:ATTN_W + KV_W + LANES * (m + 1)])
            v2_ref[:, 2 * LANES * m:2 * LANES * m + LANES] = v_lo.astype(BF16)
            v2_ref[:, 2 * LANES * m + LANES:2 * LANES * (m + 1)] = v_hi.astype(BF16)

    row = lambda w: pl.BlockSpec((tm, w), lambda i: (i, 0))
    one = pl.BlockSpec((1, LANES), lambda i: (0, 0))
    return pl.pallas_call(
        body, name="qk_prep",
        out_shape=(jax.ShapeDtypeStruct((s, ATTN_W), BF16), jax.ShapeDtypeStruct((s, K2_W), BF16),
                   jax.ShapeDtypeStruct((s, K2_W), BF16)),
        grid=(s // tm,),
        in_specs=[row(PAIR_W), row(LANES), row(LANES), row(LANES), one, one, ANY],
        out_specs=(row(ATTN_W), row(K2_W), row(K2_W)),
        compiler_params=_params("parallel"))(z, ra, rbm, rbp, gq2, gk2, after)


GROUP = 4


def _window_mask(n):
    row = lax.broadcasted_iota(jnp.int32, (GROUP * BLK, 2 * BLK), 0) % BLK
    col = lax.broadcasted_iota(jnp.int32, (GROUP * BLK, 2 * BLK), 1)
    return (col > row) & (col <= row + BLK) & ((col >= BLK) | (n > 0))


def _stack_heads(pairs, zero):
    lo = _low_half(pairs[0].shape)
    parts = []
    for v in pairs:
        parts += [jnp.where(lo, v, zero), jnp.where(lo, zero, v)]
    return jnp.concatenate(parts, axis=0)


def _unstack_heads(v4):
    lo = _low_half((BLK, LANES))
    return [jnp.where(lo, v4[2 * i * BLK:(2 * i + 1) * BLK], v4[(2 * i + 1) * BLK:(2 * i + 2) * BLK]) for i in range(2)]


def _group_sinks(sink_ref, kvh):
    slot = lax.broadcasted_iota(jnp.int32, (GROUP * BLK, 1), 0) // BLK
    col = jnp.zeros((GROUP * BLK, 1), F32)
    for i in range(GROUP):
        col = jnp.where(slot == i, sink_ref[0, GROUP * kvh + i], col)
    return col, slot


def _head_probs(qm, kw, valid, sink):
    sc = jnp.where(valid, _dot_nt(qm, kw) * (HEAD ** -0.5), NEG_INF)
    mx = jnp.maximum(jnp.max(sc, axis=-1, keepdims=True), sink)
    ex = jnp.exp(sc - mx)
    den = jnp.sum(ex, axis=-1, keepdims=True) + jnp.exp(sink - mx)
    return ex / den, mx, den


def _cols(start, width=ATTN_W):
    return slice(start, start + width)


def _conv_fwd(z_ref, zp_ref, cw_ref, ext_ref, n):
    u = z_ref[:, _cols(CONV_C0)] * z_ref[:, _cols(CONV_H0)]
    pu = zp_ref[:, _cols(CONV_C0)] * zp_ref[:, _cols(CONV_H0)]
    ext_ref[0:SUBLANES, :] = jnp.where(n > 0, pu, 0.0)
    ext_ref[SUBLANES:SUBLANES + BLK, :] = u
    um1 = ext_ref[SUBLANES - 1:SUBLANES - 1 + BLK, :]
    um2 = ext_ref[SUBLANES - 2:SUBLANES - 2 + BLK, :]
    cv = cw_ref[0:1, :] * um2 + cw_ref[1:2, :] * um1 + cw_ref[2:3, :] * u
    return u, um1, um2, cv


def _prev_rows(n):
    return (jnp.maximum(n * (BLK // SUBLANES) - 1, 0), 0)


def _attn_fwd(qn, k2, v2, z, conv_wp, sinks, after):
    s = qn.shape[0]
    nb = s // BLK

    def body(sink_ref, q_ref, kc_ref, kp_ref, vc_ref, vp_ref, z_ref, zp_ref, cw_ref, after_ref, a_ref, mix_ref,
             mixt_ref, ext_ref):
        n = pl.program_id(0)
        valid = _window_mask(n)
        for kvh in range(K2_W // LANES):
            cols = slice(LANES * kvh, LANES * (kvh + 1))
            kw = jnp.concatenate([kp_ref[:, cols], kc_ref[:, cols]], axis=0)
            vw = jnp.concatenate([vp_ref[:, cols], vc_ref[:, cols]], axis=0)
            blocks = [slice(LANES * r, LANES * (r + 1)) for r in (2 * kvh, 2 * kvh + 1)]
            q4 = _stack_heads([q_ref[:, rc] for rc in blocks], jnp.zeros((BLK, LANES), BF16))
            p, _, _ = _head_probs(q4, kw, valid, _group_sinks(sink_ref, kvh)[0])
            for rc, a in zip(blocks, _unstack_heads(_dot(p.astype(BF16), vw))):
                a_ref[:, rc] = a
                g = z_ref[:, _cols(GATE_A0 + rc.start, LANES)]
                mix_ref[:, rc] = (a * (g * _sig(g))).astype(BF16)
        _, _, _, cv = _conv_fwd(z_ref, zp_ref, cw_ref, ext_ref, n)
        gc = z_ref[:, _cols(GATE_C0)]
        mix_ref[:, ATTN_W:D_MODEL] = (z_ref[:, _cols(CONV_B0)] * cv * (gc * _sig(gc))).astype(BF16)
        mixt_ref[...] = mix_ref[...].T

    cur = lambda w: pl.BlockSpec((BLK, w), lambda n: (n, 0))
    prev = lambda w: pl.BlockSpec((BLK, w), lambda n: (jnp.maximum(n - 1, 0), 0))
    return pl.pallas_call(
        body, name="attn_fwd",
        out_shape=(jax.ShapeDtypeStruct((s, ATTN_W), F32), jax.ShapeDtypeStruct((s, D_MODEL), BF16),
                   jax.ShapeDtypeStruct((D_MODEL, s), BF16)),
        grid=(nb,),
        in_specs=[pl.BlockSpec(memory_space=pltpu.SMEM),
                  cur(ATTN_W), cur(K2_W), prev(K2_W), cur(K2_W), prev(K2_W), cur(IN_W),
                  pl.BlockSpec((SUBLANES, IN_W), _prev_rows),
                  pl.BlockSpec((SUBLANES, ATTN_W), lambda n: (0, 0)), ANY],
        out_specs=(cur(ATTN_W), cur(D_MODEL), pl.BlockSpec((D_MODEL, BLK), lambda n: (0, n))),
        scratch_shapes=[pltpu.VMEM((BLK + 2 * SUBLANES, ATTN_W), F32)],
        compiler_params=_params("parallel"))(sinks, qn, k2, k2, v2, v2, z, z, conv_wp, after)


def _fwd_out(mix, w_out, x, g2, tm):
    s = x.shape[0]

    def body(m_ref, w_ref, x_ref, g_ref, x1_ref, h_ref, ht_ref):
        x1 = x_ref[...] + _dot(m_ref[...], w_ref[...])
        x1_ref[...] = x1
        xn, _ = _rms(x1)
        h = (xn * g_ref[...]).astype(BF16)
        h_ref[...] = h
        ht_ref[...] = h.T

    row = pl.BlockSpec((tm, D_MODEL), lambda i: (i, 0))
    return pl.pallas_call(
        body, name="fwd_out",
        out_shape=(jax.ShapeDtypeStruct((s, D_MODEL), F32), jax.ShapeDtypeStruct((s, D_MODEL), BF16),
                   jax.ShapeDtypeStruct((D_MODEL, s), BF16)),
        grid=(s // tm,),
        in_specs=[row, _resident((D_MODEL, D_MODEL)), row, pl.BlockSpec((1, D_MODEL), lambda i: (0, 0))],
        out_specs=(row, row, pl.BlockSpec((D_MODEL, tm), lambda i: (0, i))),
        compiler_params=_params("parallel"))(mix, w_out, x, g2)


def _ple(hn2, w_pg, b_pg, p, w_pp, g3, x1, target, tm):
    s = x1.shape[0]

    def body(h_ref, wg_ref, b_ref, p_ref, wp_ref, g3_ref, x1_ref, t_ref, dy_ref, dgp_ref, dt_ref, pt_ref, acc_ref):
        gate = _sig(_dot(h_ref[...], wg_ref[...]) + b_ref[...])
        pb = p_ref[...].astype(BF16)
        pt_ref[...] = pb.T
        t = _dot(pb, wp_ref[...])
        tn, r3 = _rms(t)
        e = tn * g3_ref[...]
        diff = x1_ref[...] + gate * e - t_ref[...]
        dy = diff * (1.0 / D_MODEL)
        dy_ref[...] = dy
        dgp = dy * e * (gate * (1.0 - gate))
        dgp_ref[...] = dgp.astype(BF16)
        de = dy * gate
        dt_ref[...] = _rms_bwd(de * g3_ref[...], tn, r3).astype(BF16)

        @pl.when(pl.program_id(0) == 0)
        def _():
            acc_ref[...] = jnp.zeros_like(acc_ref)

        acc_ref[0:1, :] += jnp.sum(dgp, axis=0, keepdims=True)
        acc_ref[1:2, :] += jnp.sum(de * tn, axis=0, keepdims=True)
        acc_ref[2:3, :] += jnp.sum(diff * diff, axis=0, keepdims=True) * (0.5 / D_MODEL)

    row = pl.BlockSpec((tm, D_MODEL), lambda i: (i, 0))
    vec = pl.BlockSpec((1, D_MODEL), lambda i: (0, 0))
    return pl.pallas_call(
        body, name="ple",
        out_shape=(jax.ShapeDtypeStruct((s, D_MODEL), F32), jax.ShapeDtypeStruct((s, D_MODEL), BF16),
                   jax.ShapeDtypeStruct((s, D_MODEL), BF16), jax.ShapeDtypeStruct((PLE_DIM, s), BF16),
                   jax.ShapeDtypeStruct((SUBLANES, D_MODEL), F32)),
        grid=(s // tm,),
        in_specs=[row, _resident((D_MODEL, D_MODEL)), vec, pl.BlockSpec((tm, PLE_DIM), lambda i: (i, 0)),
                  _resident((PLE_DIM, D_MODEL)), vec, row, row],
        out_specs=(row, row, row, pl.BlockSpec((PLE_DIM, tm), lambda i: (0, i)),
                   pl.BlockSpec((SUBLANES, D_MODEL), lambda i: (0, 0))),
        compiler_params=_params("arbitrary"))(hn2, w_pg, b_pg, p, w_pp, g3, x1, target)


def _gate_bwd(dgp, w_pg, x1, dy, g2, tm):
    s = x1.shape[0]

    def body(d_ref, w_ref, x1_ref, dy_ref, g_ref, dx_ref, dxb_ref, acc_ref):
        dh = _dot_nt(d_ref[...], w_ref[...])
        xn, r = _rms(x1_ref[...])
        dx1 = dy_ref[...] + _rms_bwd(dh * g_ref[...], xn, r)
        dx_ref[...] = dx1
        dxb_ref[...] = dx1.astype(BF16)

        @pl.when(pl.program_id(0) == 0)
        def _():
            acc_ref[...] = jnp.zeros_like(acc_ref)

        acc_ref[0:1, :] += jnp.sum(dh * xn, axis=0, keepdims=True)

    row = pl.BlockSpec((tm, D_MODEL), lambda i: (i, 0))
    return pl.pallas_call(
        body, name="gate_bwd",
        out_shape=(jax.ShapeDtypeStruct((s, D_MODEL), F32), jax.ShapeDtypeStruct((s, D_MODEL), BF16),
                   jax.ShapeDtypeStruct((SUBLANES, D_MODEL), F32)),
        grid=(s // tm,),
        in_specs=[row, _resident((D_MODEL, D_MODEL)), row, row, pl.BlockSpec((1, D_MODEL), lambda i: (0, 0))],
        out_specs=(row, row, pl.BlockSpec((SUBLANES, D_MODEL), lambda i: (0, 0))),
        compiler_params=_params("arbitrary"))(dgp, w_pg, x1, dy, g2)


def _mm_nt(a, b, tm, name, after):
    m, k = a.shape
    n = b.shape[0]

    def body(a_ref, b_ref, after_ref, o_ref):
        o_ref[...] = _dot_nt(a_ref[...], b_ref[...])

    return pl.pallas_call(
        body, name=name,
        out_shape=jax.ShapeDtypeStruct((m, n), F32),
        grid=(m // tm,),
        in_specs=[pl.BlockSpec((tm, k), lambda i: (i, 0)), _resident((n, k)), ANY],
        out_specs=pl.BlockSpec((tm, n), lambda i: (i, 0)),
        compiler_params=_params("parallel"))(a, b, after)


def _attn_bwd(qn, k2, v2, a, z, dmix, conv_wp, sinks, after):
    s = qn.shape[0]
    nb = s // BLK

    def body(sink_ref, q_ref, kc_ref, kp_ref, vc_ref, vp_ref, a_ref, z_ref, zp_ref, zn_ref, dm_ref, dmn_ref,
             cw_ref, after_ref, dq_ref, dkc_ref, dkp_ref, dvc_ref, dvp_ref, dz_ref, dzt_ref, acc_ref, ext_ref):
        n = pl.program_id(0)
        valid = _window_mask(n)
        lane = lax.broadcasted_iota(jnp.int32, (1, ATTN_W), 1)

        @pl.when(n == 0)
        def _():
            acc_ref[...] = jnp.zeros_like(acc_ref)

        dz_ref[:, 0:QKV_W] = jnp.zeros((BLK, QKV_W), BF16)
        dsink = jnp.zeros((1, ATTN_W), F32)
        for kvh in range(K2_W // LANES):
            cols = slice(LANES * kvh, LANES * (kvh + 1))
            kw = jnp.concatenate([kp_ref[:, cols], kc_ref[:, cols]], axis=0)
            vw = jnp.concatenate([vp_ref[:, cols], vc_ref[:, cols]], axis=0)
            blocks = [slice(LANES * r, LANES * (r + 1)) for r in (2 * kvh, 2 * kvh + 1)]
            das, avs = [], []
            for rc in blocks:
                g = z_ref[:, _cols(GATE_A0 + rc.start, LANES)]
                sg = _sig(g)
                dm = dm_ref[:, rc]
                av = a_ref[:, rc]
                das.append(dm * (g * sg))
                avs += [av, av]
                dz_ref[:, _cols(GATE_A0 + rc.start, LANES)] = (dm * av * _dsilu(g, sg)).astype(BF16)
            q4 = _stack_heads([q_ref[:, rc] for rc in blocks], jnp.zeros((BLK, LANES), BF16))
            sink, slot = _group_sinks(sink_ref, kvh)
            p, mx, den = _head_probs(q4, kw, valid, sink)
            do4 = _stack_heads(das, 0.0)
            delta = jnp.sum(do4 * jnp.concatenate(avs, axis=0), axis=-1, keepdims=True)
            dob = do4.astype(BF16)
            ds = p * (_dot_nt(dob, vw) - delta) * (HEAD ** -0.5)
            for rc, dq in zip(blocks, _unstack_heads(_dot(ds.astype(BF16), kw))):
                dq_ref[:, rc] = dq
            dk2 = _dot(ds.T.astype(BF16), q4)
            dv2 = _dot(p.T.astype(BF16), dob)
            dkp_ref[:, cols] = dk2[0:BLK]
            dkc_ref[:, cols] = dk2[BLK:2 * BLK]
            dvp_ref[:, cols] = dv2[0:BLK]
            dvc_ref[:, cols] = dv2[BLK:2 * BLK]
            dsk = jnp.exp(sink - mx) / den * delta
            for i in range(GROUP):
                dsink = dsink - jnp.where(lane == GROUP * kvh + i,
                                          jnp.sum(jnp.where(slot == i, dsk, 0.0), axis=0, keepdims=True), 0.0)
        acc_ref[0:1, :] += dsink

        u, um1, um2, cv = _conv_fwd(z_ref, zp_ref, cw_ref, ext_ref, n)
        cb = z_ref[:, _cols(CONV_B0)]
        gc = z_ref[:, _cols(GATE_C0)]
        sgc = _sig(gc)
        dmc = dm_ref[:, ATTN_W:D_MODEL]
        t = dmc * (gc * sgc)
        dcv = t * cb
        dz_ref[:, _cols(CONV_B0)] = (t * cv).astype(BF16)
        dz_ref[:, _cols(GATE_C0)] = (dmc * cb * cv * _dsilu(gc, sgc)).astype(BF16)
        gcn = zn_ref[:, _cols(GATE_C0)]
        dcvn = dmn_ref[:, ATTN_W:D_MODEL] * (gcn * _sig(gcn)) * zn_ref[:, _cols(CONV_B0)]
        ext_ref[0:BLK, :] = dcv
        ext_ref[BLK:BLK + SUBLANES, :] = jnp.where(n < nb - 1, dcvn, 0.0)
        du = (cw_ref[2:3, :] * dcv + cw_ref[1:2, :] * ext_ref[1:1 + BLK, :]
              + cw_ref[0:1, :] * ext_ref[2:2 + BLK, :])
        dz_ref[:, _cols(CONV_C0)] = (du * z_ref[:, _cols(CONV_H0)]).astype(BF16)
        dz_ref[:, _cols(CONV_H0)] = (du * z_ref[:, _cols(CONV_C0)]).astype(BF16)
        acc_ref[1:2, :] += jnp.sum(dcv * um2, axis=0, keepdims=True)
        acc_ref[2:3, :] += jnp.sum(dcv * um1, axis=0, keepdims=True)
        acc_ref[3:4, :] += jnp.sum(dcv * u, axis=0, keepdims=True)
        dzt_ref[...] = dz_ref[...].T

    cur = lambda w: pl.BlockSpec((BLK, w), lambda n: (n, 0))
    prev = lambda w: pl.BlockSpec((BLK, w), lambda n: (jnp.maximum(n - 1, 0), 0))
    nxt = lambda w: pl.BlockSpec(
        (SUBLANES, w), lambda n: (jnp.minimum((n + 1) * (BLK // SUBLANES), nb * (BLK // SUBLANES) - 1), 0))
    f32 = lambda w: jax.ShapeDtypeStruct((s, w), F32)
    return pl.pallas_call(
        body, name="attn_bwd",
        out_shape=(f32(ATTN_W), f32(K2_W), f32(K2_W), f32(K2_W), f32(K2_W),
                   jax.ShapeDtypeStruct((s, IN_W), BF16), jax.ShapeDtypeStruct((IN_W, s), BF16),
                   jax.ShapeDtypeStruct((SUBLANES, ATTN_W), F32)),
        grid=(nb,),
        in_specs=[pl.BlockSpec(memory_space=pltpu.SMEM),
                  cur(ATTN_W), cur(K2_W), prev(K2_W), cur(K2_W), prev(K2_W), cur(ATTN_W), cur(IN_W),
                  pl.BlockSpec((SUBLANES, IN_W), _prev_rows), nxt(IN_W), cur(D_MODEL), nxt(D_MODEL),
                  pl.BlockSpec((SUBLANES, ATTN_W), lambda n: (0, 0)), ANY],
        out_specs=(cur(ATTN_W), cur(K2_W), cur(K2_W), cur(K2_W), cur(K2_W), cur(IN_W),
                   pl.BlockSpec((IN_W, BLK), lambda n: (0, n)), pl.BlockSpec((SUBLANES, ATTN_W), lambda n: (0, 0))),
        scratch_shapes=[pltpu.VMEM((BLK + 2 * SUBLANES, ATTN_W), F32)],
        compiler_params=_params("arbitrary"))(sinks, qn, k2, k2, v2, v2, a, z, z, z, dmix, dmix, conv_wp, after)


def _qkv_bwd(z, dz, dzt, dq, dkc, dkp, dvc, dvp, ra, rbm, rbp, gq2, gk2):
    s = z.shape[0]
    nb = s // BLK

    def body(z_ref, dz_in, dzt_in, dq_ref, dkc_ref, dkp_ref, dvc_ref, dvp_ref, a_ref, bm_ref, bp_ref, gq_ref, gk_ref,
             dz_ref, dzt_ref, acc_ref):
        n = pl.program_id(0)
        a, bm, bp = a_ref[...], bm_ref[...], bp_ref[...]
        lo = _low_half((BLK, LANES))
        last = n == nb - 1

        @pl.when(n == 0)
        def _():
            acc_ref[...] = jnp.zeros_like(acc_ref)

        def norm_bwd(x, dy, gain):
            rr = lax.rsqrt(_half_sums(x * x) * (1.0 / HEAD) + EPS)
            xh = x * rr
            dxg = _rope_t(dy, a, bm, bp)
            dxh = dxg * gain
            dx = rr * (dxh - xh * (_half_sums(dxh * xh) * (1.0 / HEAD)))
            return dx, jnp.sum(dxg * xh, axis=0, keepdims=True)

        def folded(cur_ref, prev_ref, m):
            parts = []
            for h in (2 * m, 2 * m + 1):
                v = cur_ref[:, LANES * h:LANES * (h + 1)] + jnp.where(
                    last, 0.0, prev_ref[:, LANES * h:LANES * (h + 1)])
                parts.append(v + pltpu.roll(v, HEAD, 1))
            return jnp.where(lo, parts[0], parts[1])

        gq_acc = jnp.zeros((1, LANES), F32)
        for r in range(ATTN_W // LANES):
            rc = slice(LANES * r, LANES * (r + 1))
            dx, gg = norm_bwd(z_ref[:, rc], dq_ref[:, rc], gq_ref[...])
            dz_ref[:, rc] = dx.astype(BF16)
            gq_acc = gq_acc + gg
        acc_ref[0:1, :] += gq_acc
        gk_acc = jnp.zeros((1, LANES), F32)
        for m in range(KV_W // LANES):
            kc = slice(ATTN_W + LANES * m, ATTN_W + LANES * (m + 1))
            dx, gg = norm_bwd(z_ref[:, kc], folded(dkc_ref, dkp_ref, m), gk_ref[...])
            dz_ref[:, kc] = dx.astype(BF16)
            gk_acc = gk_acc + gg
            vc = slice(ATTN_W + KV_W + LANES * m, ATTN_W + KV_W + LANES * (m + 1))
            dz_ref[:, vc] = folded(dvc_ref, dvp_ref, m).astype(BF16)
        acc_ref[1:2, :] += gk_acc
        dzt_ref[...] = dz_ref[...].T

    cur = lambda w: pl.BlockSpec((BLK, w), lambda n: (n, 0))
    nxt = lambda w: pl.BlockSpec((BLK, w), lambda n: (jnp.minimum(n + 1, nb - 1), 0))
    one = pl.BlockSpec((1, LANES), lambda n: (0, 0))
    return pl.pallas_call(
        body, name="qkv_bwd",
        out_shape=(jax.ShapeDtypeStruct(dz.shape, dz.dtype), jax.ShapeDtypeStruct(dzt.shape, dzt.dtype),
                   jax.ShapeDtypeStruct((SUBLANES, LANES), F32)),
        grid=(nb,),
        in_specs=[cur(PAIR_W), ANY, ANY, cur(ATTN_W), cur(K2_W), nxt(K2_W), cur(K2_W), nxt(K2_W),
                  cur(LANES), cur(LANES), cur(LANES), one, one],
        out_specs=(cur(QKV_W), pl.BlockSpec((QKV_W, BLK), lambda n: (0, n)),
                   pl.BlockSpec((SUBLANES, LANES), lambda n: (0, 0))),
        input_output_aliases={1: 0, 2: 1},
        compiler_params=_params("arbitrary"))(z, dz, dzt, dq, dkc, dkp, dvc, dvp, ra, rbm, rbp, gq2, gk2)


def _in_bwd(dz, w_pairs, x, dx1, g1, tm, after):
    s = x.shape[0]
    n = s // tm
    sub = tm // N_PAIRS
    stripes = 4

    def body(d_ref, w_ref, x_ref, dx1_ref, g_ref, after_ref, gx_ref, acc_ref, dh_ref):
        i, k = pl.program_id(0), pl.program_id(1)

        def matmul(c):
            cols = slice(c * (D_MODEL // stripes), (c + 1) * (D_MODEL // stripes))
            dh_ref[i % 2, :, cols] += _dot(d_ref[...], w_ref[0, :, cols])

        def norm_bwd(c):
            part = sub // stripes
            mine = slice(c * part, (c + 1) * part)
            rows = pl.ds(pl.multiple_of(k * sub + c * part, part), part)
            dh = dh_ref[(i + 1) % 2, rows, :]
            dh_ref[(i + 1) % 2, rows, :] = jnp.zeros_like(dh)
            xn, r = _rms(x_ref[mine, :])
            gx_ref[rows, :] = dx1_ref[mine, :] + _rms_bwd(dh * g_ref[...], xn, r)
            acc_ref[0:1, :] += jnp.sum(dh * xn, axis=0, keepdims=True)

        @pl.when((i == 0) & (k == 0))
        def _():
            acc_ref[...] = jnp.zeros_like(acc_ref)
            dh_ref[...] = jnp.zeros_like(dh_ref)

        @pl.when(i == 0)
        def _():
            for c in range(stripes):
                matmul(c)

        @pl.when((i > 0) & (i < n))
        def _():
            for c in range(stripes):
                matmul(c)
                norm_bwd(c)

        @pl.when(i == n)
        def _():
            for c in range(stripes):
                norm_bwd(c)

    last = lambda i, k: jnp.where(i == n, N_PAIRS - 1, k)
    rows_before = lambda i, k: (jnp.maximum(i - 1, 0) * N_PAIRS + k, 0)
    return pl.pallas_call(
        body, name="in_bwd",
        out_shape=(jax.ShapeDtypeStruct((s, D_MODEL), F32), jax.ShapeDtypeStruct((SUBLANES, D_MODEL), F32)),
        grid=(n + 1, N_PAIRS),
        in_specs=[pl.BlockSpec((tm, PAIR_W), lambda i, k: (jnp.minimum(i, n - 1), last(i, k))),
                  pl.BlockSpec((1, PAIR_W, D_MODEL), lambda i, k: (last(i, k), 0, 0)),
                  pl.BlockSpec((sub, D_MODEL), rows_before), pl.BlockSpec((sub, D_MODEL), rows_before),
                  pl.BlockSpec((1, D_MODEL), lambda i, k: (0, 0)), ANY],
        out_specs=(pl.BlockSpec((tm, D_MODEL), lambda i, k: (jnp.maximum(i - 1, 0), 0)),
                   pl.BlockSpec((SUBLANES, D_MODEL), lambda i, k: (0, 0))),
        scratch_shapes=[pltpu.VMEM((2, tm, D_MODEL), F32)],
        compiler_params=_params("arbitrary", "arbitrary"))(dz, w_pairs, x, dx1, g1, after)


def _mm_grad(at, bs, tn, name):
    m, kdim = at.shape
    nblk = [b.shape[1] // tn for b in bs]
    starts = [sum(nblk[:t]) for t in range(len(bs))]

    def body(a_ref, *refs):
        b_refs, o_ref = refs[:len(bs)], refs[len(bs)]
        j = pl.program_id(0)
        for t, b_ref in enumerate(b_refs):
            @pl.when((j >= starts[t]) & (j < starts[t] + nblk[t]))
            def _():
                o_ref[...] = _dot(a_ref[...], b_ref[...]).astype(BF16)

    def b_spec(t):
        return pl.BlockSpec((kdim, tn), lambda j: (0, jnp.clip(j - starts[t], 0, nblk[t] - 1)))

    return pl.pallas_call(
        body, name=name,
        out_shape=jax.ShapeDtypeStruct((m, sum(nblk) * tn), BF16),
        grid=(sum(nblk),),
        in_specs=[_resident((m, kdim))] + [b_spec(t) for t in range(len(bs))],
        out_specs=pl.BlockSpec((m, tn), lambda j: (0, j)),
        compiler_params=_params("parallel"))(at, *bs)


def _grad_w_in(dzt, h):
    kdim = h.shape[0]

    def body(d_ref, h_ref, o_ref):
        o_ref[0] = _dot(d_ref[...], h_ref[...]).astype(BF16)

    return pl.pallas_call(
        body, name="grad_w_in",
        out_shape=jax.ShapeDtypeStruct((N_DEV, SHARD_IN, D_MODEL), BF16),
        grid=(N_DEV,),
        in_specs=[pl.BlockSpec((SHARD_IN, kdim), lambda j: (j, 0)), _resident((kdim, D_MODEL))],
        out_specs=pl.BlockSpec((1, SHARD_IN, D_MODEL), lambda j: (j, 0, 0)),
        compiler_params=_params("parallel"))(dzt, h)


def _place():
    return lax.axis_index("x"), lax.axis_index("y"), lax.axis_index("c")


ROW_TAPS, ROW_MISC = 4, 5
Q_AT, K_AT, SINK_AT, LOSS_AT = (ATTN_W + LANES * t for t in range(4))
SMALL_AT = [(0, 0), (1, 0), (2, 0), (3, 0), (ROW_MISC, Q_AT), (ROW_MISC, K_AT), (ROW_MISC, SINK_AT)]


def _tap_at(tap):
    return ROW_TAPS + tap // 2, ATTN_W * (tap % 2)


def _reduce_small(acc_g1, acc_g2, acc_ple, acc_qk, acc_attn):
    def body(g1_ref, g2_ref, ple_ref, qk_ref, attn_ref, out_ref, slab_ref, gath_ref, send_sems, recv_sems):
        x, y, c = _place()
        me = 4 * x + 2 * y + c
        slab_ref[...] = jnp.zeros_like(slab_ref)
        slab_ref[0:1, :] = g1_ref[0:1, :]
        slab_ref[1:2, :] = g2_ref[0:1, :]
        slab_ref[2:4, :] = ple_ref[0:2, :]
        qk = qk_ref[0:2, :]
        qk = jnp.where(_low_half(qk.shape), qk + pltpu.roll(qk, HEAD, 1), 0.0)
        misc = slab_ref.at[ROW_MISC:ROW_MISC + 1]
        misc[:, Q_AT:Q_AT + LANES] = qk[0:1]
        misc[:, K_AT:K_AT + LANES] = qk[1:2]
        lane = lax.broadcasted_iota(jnp.int32, (1, LANES), 1)
        misc[:, SINK_AT:SINK_AT + LANES] = jnp.where(lane < N_Q_HEADS, attn_ref[0:1, 0:LANES], 0.0)
        misc[:, LOSS_AT:LOSS_AT + LANES] = sum(
            ple_ref[2:3, LANES * t:LANES * (t + 1)] for t in range(D_MODEL // LANES))
        for tap in range(3):
            row, at = _tap_at(tap)
            slab_ref[row:row + 1, at:at + ATTN_W] = attn_ref[1 + tap:2 + tap, :]
        gath_ref[me] = slab_ref[...]
        copies = []
        for k in range(1, N_DEV):
            peer = (x ^ (k >> 2), y ^ ((k >> 1) & 1), c ^ (k & 1))
            copies.append(pltpu.make_async_remote_copy(
                src_ref=slab_ref, dst_ref=gath_ref.at[me], send_sem=send_sems.at[k - 1],
                recv_sem=recv_sems.at[k - 1], device_id=peer, device_id_type=MESH))
        for cp in copies:
            cp.start()
        for cp in copies:
            cp.wait_recv()
        for cp in copies:
            cp.wait_send()
        total = gath_ref[0]
        for d in range(1, N_DEV):
            total = total + gath_ref[d]
        out_ref[...] = total

    vmem = pl.BlockSpec(memory_space=pltpu.VMEM)
    return pl.pallas_call(
        body, name="reduce_small",
        out_shape=jax.ShapeDtypeStruct((SLAB_ROWS, D_MODEL), F32),
        in_specs=[vmem] * 5, out_specs=vmem,
        scratch_shapes=[pltpu.VMEM((SLAB_ROWS, D_MODEL), F32), pltpu.VMEM((N_DEV, SLAB_ROWS, D_MODEL), F32),
                        pltpu.SemaphoreType.DMA((N_DEV - 1,)), pltpu.SemaphoreType.DMA((N_DEV - 1,))])(
            acc_g1, acc_g2, acc_ple, acc_qk, acc_attn)


def _pair_sum(g, r, place, tr, name):
    _, _, rows, cols = g.shape

    def body(place_ref, g_ref, r_ref, pb_ref, own_ref):
        tot = g_ref[0, 0].astype(F32) + r_ref[0].astype(F32)
        pb_ref[0] = tot.astype(BF16)

        @pl.when(pl.program_id(1) == place_ref[1])
        def _():
            own_ref[...] = tot

    grid_spec = pltpu.PrefetchScalarGridSpec(
        num_scalar_prefetch=1, grid=(rows // tr, 4),
        in_specs=[pl.BlockSpec((1, 1, tr, cols), lambda i, q, place_ref: (q, place_ref[0], i, 0)),
                  pl.BlockSpec((1, tr, cols), lambda i, q, place_ref: (q, i, 0))],
        out_specs=(pl.BlockSpec((1, tr, cols), lambda i, q, place_ref: (q, i, 0)),
                   pl.BlockSpec((tr, cols), lambda i, q, place_ref: (i, 0))))
    return pl.pallas_call(
        body, name=name, grid_spec=grid_spec,
        out_shape=(jax.ShapeDtypeStruct((4, rows, cols), BF16), jax.ShapeDtypeStruct((rows, cols), F32)),
        compiler_params=_params("arbitrary", "arbitrary"))(place, g, r)


HBM = pl.BlockSpec(memory_space=pltpu.HBM)
SEM = pl.BlockSpec(memory_space=pltpu.SEMAPHORE)
SIDE_EFFECT = pltpu.CompilerParams(has_side_effects=pltpu.SideEffectType.DATAFLOW_SIDE_EFFECTING)
TOKEN = jax.ShapeDtypeStruct((SUBLANES, LANES), F32)


def _hbm(a):
    return pltpu.with_memory_space_constraint(a, pltpu.HBM)


def _hbm_like(arrays):
    return tuple(pltpu.HBM(a.shape, a.dtype) for a in arrays)


def _block_of(px, py, pc):
    return 4 * px + 2 * py + pc


def _relay_parts(rows):
    if rows % (2 * PACKED_ROWS):
        return [pl.ds(0, rows), None]
    return [pl.ds(0, rows // 2), pl.ds(rows // 2, rows // 2)]


def _gather_start(shards, after, relay=False):
    na = len(shards)
    lands = [_hbm(lax.empty((N_DEV,) + a.shape, a.dtype)) for a in shards]

    def body(*refs):
        ins, land = refs[:na], refs[na:2 * na]
        send_sems, recv_ici, recv_d2d = refs[2 * na + 1:2 * na + 4]
        token = refs[-1]
        x, y, c = _place()
        peers = [(x, y, 1 - c), (1 - x, y, c), (x, 1 - y, c), (1 - x, 1 - y, c)]
        for k, peer in enumerate(peers[:3] if relay else peers):
            for t in range(na):
                pltpu.make_async_remote_copy(
                    src_ref=ins[t], dst_ref=land[t].at[_block_of(x, y, c)], send_sem=send_sems.at[4 * t + k],
                    recv_sem=recv_d2d.at[4 * t] if k == 0 else recv_ici.at[3 * t + k - 1],
                    device_id=peer, device_id_type=MESH).start()
        token[...] = jnp.zeros_like(token)

    out = pl.pallas_call(
        body, name="gather_start",
        out_shape=(pltpu.SemaphoreType.DMA((4 * na,)), pltpu.SemaphoreType.DMA((3 * na,)),
                   pltpu.SemaphoreType.DMA((4 * na,)), pltpu.SemaphoreType.DMA((2 * na,)), *_hbm_like(lands), TOKEN),
        in_specs=[ANY] * na + [HBM] * na + [ANY],
        out_specs=(SEM, SEM, SEM, SEM, *[HBM] * na, pl.BlockSpec(memory_space=pltpu.VMEM)),
        input_output_aliases={na + i: 4 + i for i in range(na)},
        compiler_params=SIDE_EFFECT)(*shards, *lands, after)
    send_sems, recv_ici, recv_d2d, recv_relay = out[:4]
    state = dict(send=send_sems, ici=recv_ici, d2d=recv_d2d, relay=recv_relay, relayed=relay, shards=list(shards),
                 lands=out[4:4 + na])
    return state, out[-1]


def _gather_forward(state, after):
    lands = state["lands"]
    na = len(lands)

    def body(*refs):
        land = refs[:na]
        recv_ici, recv_d2d = refs[na], refs[na + 1]
        fwd_sems, token = refs[-2], refs[-1]
        x, y, c = _place()
        for j, chip in enumerate([(1 - x, y), (x, 1 - y), (1 - x, 1 - y)]):
            for t in range(na):
                blk = land[t].at[_block_of(*chip, c)]
                pltpu.make_async_remote_copy(
                    src_ref=blk, dst_ref=blk, send_sem=fwd_sems.at[3 * t + j], recv_sem=recv_ici.at[3 * t + j],
                    device_id=(x, y, c), device_id_type=MESH).wait_recv()
                pltpu.make_async_remote_copy(
                    src_ref=blk, dst_ref=blk, send_sem=fwd_sems.at[3 * t + j], recv_sem=recv_d2d.at[4 * t + 1 + j],
                    device_id=(x, y, 1 - c), device_id_type=MESH).start()
        token[...] = jnp.zeros_like(token)

    out = pl.pallas_call(
        body, name="gather_forward",
        out_shape=(*_hbm_like(lands), pltpu.SemaphoreType.DMA((3 * na,)), TOKEN),
        in_specs=[HBM] * na + [SEM, SEM, ANY],
        out_specs=(*[HBM] * na, SEM, pl.BlockSpec(memory_space=pltpu.VMEM)),
        input_output_aliases={i: i for i in range(na)},
        compiler_params=SIDE_EFFECT)(*lands, state["ici"], state["d2d"], after)
    return dict(state, lands=out[:na], fwd=out[na]), out[-1]


def _gather_wait(state, after):
    shards, lands = state["shards"], state["lands"]
    na = len(lands)

    def body(*refs):
        ins, land = refs[:na], refs[na:2 * na]
        send_sems, fwd_sems, recv_d2d = refs[2 * na:2 * na + 3]
        x, y, c = _place()
        chips = [(1 - x, y), (x, 1 - y), (1 - x, 1 - y)]
        for t in range(na):
            mine = land[t].at[_block_of(x, y, c)]
            for k in range(4):
                pltpu.make_async_remote_copy(
                    src_ref=ins[t], dst_ref=mine, send_sem=send_sems.at[4 * t + k], recv_sem=recv_d2d.at[4 * t],
                    device_id=(x, y, c), device_id_type=MESH).wait_send()
            for j, chip in enumerate(chips):
                blk = land[t].at[_block_of(*chip, c)]
                pltpu.make_async_remote_copy(
                    src_ref=blk, dst_ref=blk, send_sem=fwd_sems.at[3 * t + j], recv_sem=recv_d2d.at[4 * t + 1 + j],
                    device_id=(x, y, c), device_id_type=MESH).wait_send()
            for k, blk_id in enumerate([_block_of(x, y, 1 - c)] + [_block_of(*chip, 1 - c) for chip in chips]):
                blk = land[t].at[blk_id]
                pltpu.make_async_remote_copy(
                    src_ref=blk, dst_ref=blk, send_sem=send_sems.at[4 * t], recv_sem=recv_d2d.at[4 * t + k],
                    device_id=(x, y, c), device_id_type=MESH).wait_recv()

    out = pl.pallas_call(
        body, name="gather_wait",
        out_shape=_hbm_like(lands),
        in_specs=[ANY] * na + [HBM] * na + [SEM, SEM, SEM, ANY],
        out_specs=tuple([HBM] * na),
        input_output_aliases={na + i: i for i in range(na)},
        compiler_params=SIDE_EFFECT)(*shards, *lands, state["send"], state["fwd"], state["d2d"], after)
    return out


def _gather_from_sibling(state, after):
    lands = state["lands"]
    na = len(lands)

    def body(*refs):
        land, recv_d2d = refs[:na], refs[na]
        x, y, c = _place()
        for t in range(na):
            blk = land[t].at[_block_of(x, y, 1 - c)]
            pltpu.make_async_remote_copy(src_ref=blk, dst_ref=blk, send_sem=recv_d2d.at[4 * t],
                                         recv_sem=recv_d2d.at[4 * t], device_id=(x, y, c),
                                         device_id_type=MESH).wait_recv()

    out = pl.pallas_call(
        body, name="gather_from_sibling", out_shape=_hbm_like(lands),
        in_specs=[HBM] * na + [SEM, ANY], out_specs=tuple([HBM] * na),
        input_output_aliases={i: i for i in range(na)},
        compiler_params=SIDE_EFFECT)(*lands, state["d2d"], after)
    return dict(state, lands=list(out))


def _gather_from_chip(state, j, afters, last):
    shards, lands, relayed = state["shards"], state["lands"], state["relayed"]
    na = len(lands)
    parts = [_relay_parts(a.shape[0]) for a in shards]

    def relay_on(land_ref, t, nb, fwd_sems, recv_relay):
        x, y, c = _place()
        blk = chip_blocks(land_ref, nb)[1].at[parts[t][nb]]
        return pltpu.make_async_remote_copy(
            src_ref=blk, dst_ref=blk, send_sem=fwd_sems.at[na + t], recv_sem=recv_relay.at[2 * t + nb],
            device_id=[(x, 1 - y, c), (1 - x, y, c)][nb], device_id_type=MESH)

    def chip_blocks(land_ref, which=j):
        x, y, c = _place()
        chip = [(1 - x, y), (x, 1 - y), (1 - x, 1 - y)][which]
        return (x, y, c), land_ref.at[_block_of(*chip, c)], land_ref.at[_block_of(*chip, 1 - c)]

    def forward(*refs):
        land, recv_ici, recv_d2d, recv_relay, fwd_sems = refs[:na], refs[na], refs[na + 1], refs[na + 2], refs[-1]
        for t in range(na):
            (x, y, c), mine, _ = chip_blocks(land[t])
            if relayed and j == 2:
                for half, rows in enumerate(parts[t]):
                    if rows is not None:
                        pltpu.make_async_remote_copy(
                            src_ref=mine.at[rows], dst_ref=mine.at[rows], send_sem=fwd_sems.at[t],
                            recv_sem=recv_relay.at[2 * t + half], device_id=(x, y, c),
                            device_id_type=MESH).wait_recv()
            else:
                pltpu.make_async_remote_copy(src_ref=mine, dst_ref=mine, send_sem=fwd_sems.at[t],
                                             recv_sem=recv_ici.at[3 * t + j], device_id=(x, y, c),
                                             device_id_type=MESH).wait_recv()
            pltpu.make_async_remote_copy(src_ref=mine, dst_ref=mine, send_sem=fwd_sems.at[t],
                                         recv_sem=recv_d2d.at[4 * t + 1 + j], device_id=(x, y, 1 - c),
                                         device_id_type=MESH).start()
            if relayed and j < 2 and parts[t][j] is not None:
                relay_on(land[t], t, j, fwd_sems, recv_relay).start()

    out = pl.pallas_call(
        forward, name="gather_pass_chip_" + str(j),
        out_shape=(*_hbm_like(lands), pltpu.SemaphoreType.DMA((2 * na,))),
        in_specs=[HBM] * na + [SEM, SEM, SEM] + [ANY] * len(afters), out_specs=(*[HBM] * na, SEM),
        input_output_aliases={i: i for i in range(na)},
        compiler_params=SIDE_EFFECT)(*lands, state["ici"], state["d2d"], state["relay"], *afters)
    lands, fwd_sems = out[:na], out[na]
    relays = state.get("relays", []) + ([fwd_sems] if relayed and j < 2 else [])
    waited = relays if last else []

    def arrive(*refs):
        land, fwd_sems, recv_d2d = refs[:na], refs[na], refs[na + 1]
        shard, send_sems, recv_relay = refs[na + 2:2 * na + 2], refs[2 * na + 2], refs[2 * na + 3]
        for nb, relay_sems in enumerate(refs[2 * na + 4:2 * na + 4 + len(waited)]):
            for t in range(na):
                if parts[t][nb] is not None:
                    relay_on(land[t], t, nb, relay_sems, recv_relay).wait_send()
        for t in range(na):
            (x, y, c), mine, theirs = chip_blocks(land[t])
            pltpu.make_async_remote_copy(src_ref=theirs, dst_ref=theirs, send_sem=fwd_sems.at[t],
                                         recv_sem=recv_d2d.at[4 * t + 1 + j], device_id=(x, y, c),
                                         device_id_type=MESH).wait_recv()
            pltpu.make_async_remote_copy(src_ref=mine, dst_ref=mine, send_sem=fwd_sems.at[t],
                                         recv_sem=recv_d2d.at[4 * t + 1 + j], device_id=(x, y, c),
                                         device_id_type=MESH).wait_send()
            for k in range((3 if relayed else 4) if last else 0):
                pltpu.make_async_remote_copy(
                    src_ref=shard[t], dst_ref=land[t].at[_block_of(x, y, c)], send_sem=send_sems.at[4 * t + k],
                    recv_sem=recv_d2d.at[4 * t], device_id=(x, y, c), device_id_type=MESH).wait_send()

    out = pl.pallas_call(
        arrive, name="gather_take_chip_" + str(j), out_shape=_hbm_like(lands),
        in_specs=[HBM] * na + [SEM, SEM] + [ANY] * na + [SEM, SEM] + [SEM] * len(waited),
        out_specs=tuple([HBM] * na), input_output_aliases={i: i for i in range(na)},
        compiler_params=SIDE_EFFECT)(*lands, fwd_sems, state["d2d"], *shards, state["send"], state["relay"], *waited)
    return dict(state, lands=list(out), relays=relays)


def _to_sibling(srcs, lands, send_sems, recv_sems):
    x, y, c = _place()
    return [pltpu.make_async_remote_copy(
        src_ref=srcs[t].at[:, 1 - c], dst_ref=lands[t], send_sem=send_sems.at[t], recv_sem=recv_sems.at[t],
        device_id=(x, y, 1 - c), device_id_type=MESH) for t in range(len(srcs))]


def _to_chips(srcs, lands, send_sems, recv_sems):
    x, y, c = _place()
    copies = []
    for k in (1, 2, 3):
        px, py = x ^ (k >> 1), y ^ (k & 1)
        copies += [pltpu.make_async_remote_copy(
            src_ref=srcs[t].at[2 * px + py], dst_ref=lands[t].at[k - 1], send_sem=send_sems.at[3 * t + k - 1],
            recv_sem=recv_sems.at[3 * t + k - 1], device_id=(px, py, c), device_id_type=MESH) for t in range(len(srcs))]
    return copies


def _exchange_start(name, srcs, land_shapes, copies, per_array, after):
    na = len(srcs)
    lands = [_hbm(lax.empty(shp, a.dtype)) for shp, a in zip(land_shapes, srcs)]

    def body(*refs):
        token = refs[-1]
        for cp in copies(refs[:na], refs[na:2 * na], refs[2 * na + 1], refs[2 * na + 2]):
            cp.start()
        token[...] = jnp.zeros_like(token)

    out = pl.pallas_call(
        body, name=name,
        out_shape=(pltpu.SemaphoreType.DMA((na * per_array,)), pltpu.SemaphoreType.DMA((na * per_array,)),
                   *_hbm_like(lands), TOKEN),
        in_specs=[ANY] * na + [HBM] * na + [ANY],
        out_specs=(SEM, SEM, *[HBM] * na, pl.BlockSpec(memory_space=pltpu.VMEM)),
        input_output_aliases={na + i: 2 + i for i in range(na)},
        compiler_params=SIDE_EFFECT)(*srcs, *lands, after)
    return dict(send=out[0], recv=out[1], srcs=list(srcs), lands=out[2:2 + na]), out[-1]


def _exchange_wait(name, state, copies, afters):
    srcs, lands = state["srcs"], state["lands"]
    na = len(srcs)

    def body(*refs):
        for cp in copies(refs[:na], refs[na:2 * na], refs[2 * na], refs[2 * na + 1]):
            cp.wait_send()
            cp.wait_recv()

    out = pl.pallas_call(
        body, name=name,
        out_shape=_hbm_like(lands),
        in_specs=[ANY] * na + [HBM] * na + [SEM, SEM] + [ANY] * len(afters),
        out_specs=tuple([HBM] * na),
        input_output_aliases={na + i: i for i in range(na)},
        compiler_params=SIDE_EFFECT)(*srcs, *lands, state["send"], state["recv"], *afters)
    return out


def _adamw_math(w, g, m, v):
    m = ADAM_B1 * m + (1.0 - ADAM_B1) * g
    v = ADAM_B2 * v + (1.0 - ADAM_B2) * (g * g)
    m_hat = m / (1.0 - ADAM_B1 ** ADAM_STEP)
    v_hat = v / (1.0 - ADAM_B2 ** ADAM_STEP)
    return -ADAM_LR * (m_hat / (jnp.sqrt(v_hat) + ADAM_EPS) + ADAM_WD * w), m, v


def _adamw(own, others, w, m, v, tr, name, after):
    rows, cols = w.shape
    blk = pl.BlockSpec((tr, cols), lambda i: (i, 0))

    def body(own_ref, oth_ref, w_ref, m_ref, v_ref, after_ref, g_ref, d_ref, nm_ref, nv_ref):
        g = own_ref[...]
        for k in range(3):
            g = g + oth_ref[k].astype(F32)
        g_ref[...] = g
        d_ref[...], nm_ref[...], nv_ref[...] = _adamw_math(w_ref[...], g, m_ref[...], v_ref[...])

    out = jax.ShapeDtypeStruct((rows, cols), F32)
    return pl.pallas_call(
        body, name=name, out_shape=(out, out, out, out), grid=(rows // tr,),
        in_specs=[blk, pl.BlockSpec((3, tr, cols), lambda i: (0, i, 0)), blk, blk, blk, ANY],
        out_specs=(blk, blk, blk, blk),
        compiler_params=_params("parallel"))(own, others, w, m, v, after)


def _adamw_small(red, me, params, moments1, moments2):
    n = len(params)

    def body(me_ref, red_ref, *refs):
        ws, ms, vs = refs[:n], refs[n:2 * n], refs[2 * n:3 * n]
        loss_ref = refs[3 * n]
        outs = refs[3 * n + 1:]
        loss_ref[...] = jnp.sum(red_ref[ROW_MISC:ROW_MISC + 1, LOSS_AT:LOSS_AT + LANES], axis=-1, keepdims=True)
        for t, (row, at) in enumerate(SMALL_AT):
            g = red_ref[row:row + 1, at:at + ws[t].shape[1]]
            d, nm, nv = _adamw_math(ws[t][...], g, ms[t][...], vs[t][...])
            for o, val in zip(outs[4 * t:4 * t + 4], (g, d, nm, nv)):
                o[...] = val
        for tap in range(ws[-1].shape[0]):
            row, at = _tap_at(tap)
            g = red_ref[row:row + 1, pl.ds(pl.multiple_of(at + me_ref[0, 0] * LANES, LANES), LANES)]
            d, nm, nv = _adamw_math(ws[-1][tap], g, ms[-1][tap], vs[-1][tap])
            for o, val in zip(outs[4 * (n - 1):], (g, d, nm, nv)):
                o[tap] = val

    vmem = pl.BlockSpec(memory_space=pltpu.VMEM)
    shapes = [jax.ShapeDtypeStruct(w.shape, F32) for w in params for _ in range(4)]
    out = pl.pallas_call(
        body, name="adamw_small", out_shape=(jax.ShapeDtypeStruct((1, 1), F32), *shapes),
        in_specs=[pl.BlockSpec(memory_space=pltpu.SMEM), vmem] + [vmem] * (3 * n),
        out_specs=tuple([vmem] * (1 + 4 * n)))(me, red, *params, *moments1, *moments2)
    return out[0], [list(out[1 + k::4]) for k in range(4)]


def _tables(s, gq, gk, conv_w):
    gq2 = jnp.tile(gq.reshape(1, HEAD), (1, 2))
    gk2 = jnp.tile(gk.reshape(1, HEAD), (1, 2))
    conv_wp = jnp.pad(conv_w, ((0, SUBLANES - conv_w.shape[0]), (0, 0)))
    return _rope_tables(s), gq2, gk2, conv_wp


def _pair_id(q):
    return jnp.array([q, 0], jnp.int32)


def _forward_in(x, g1, shards):
    s = x.shape[0]
    h = _prenorm(x, g1, min(512, s), x)
    z, w_pairs = lax.empty((s, IN_W), F32), lax.empty((N_PAIRS, PAIR_W, D_MODEL), BF16)
    for q in range(N_PAIRS):
        z, w_pairs = _fwd_in_pair(h, shards, z, w_pairs, _pair_id(q), min(512, s), "fwd_in_" + str(q),
                                  own=shards[0] if q == 0 else None)
    return h, z, w_pairs


def _forward_attn(z, rope, gq2, gk2, conv_wp, sinks):
    s = z.shape[0]
    qn, k2, v2 = _qk_prep(z, *rope, gq2, gk2, min(256, s), z)
    a, mix, mixt = _attn_fwd(qn, k2, v2, z, conv_wp, sinks, qn)
    return qn, k2, v2, a, mix, mixt


def _forward_out(x, p, target, mix, mixt, w_out, g2, w_pg, b_pg, w_pp, g3):
    s = x.shape[0]
    tm = min(512, s)
    x1, hn2, hn2t = _fwd_out(mix, w_out, x, g2, tm)
    dy, dgp, dt, pt, acc_ple = _ple(hn2, w_pg, b_pg, p, w_pp, g3, x1, target, min(256, s))
    dx1, dx1b, acc_g2 = _gate_bwd(dgp, w_pg, x1, dy, g2, tm)
    gw_out = _mm_grad(mixt, [dx1b], 512, "grad_w_out")
    gw_pg = _mm_grad(hn2t, [dgp], 512, "grad_w_ple_gate")
    gw_pp = _mm_grad(pt, [dt], 512, "grad_w_ple_proj")
    return dx1, dx1b, (gw_out, gw_pg, gw_pp), acc_ple, acc_g2


def _backward_attn(dmix, h, z, qn, k2, v2, a, rope, gq2, gk2, conv_wp, sinks, after):
    dq, dkc, dkp, dvc, dvp, dz, dzt, acc_attn = _attn_bwd(qn, k2, v2, a, z, dmix, conv_wp, sinks, after)
    dz, dzt, acc_qk = _qkv_bwd(z, dz, dzt, dq, dkc, dkp, dvc, dvp, *rope, gq2, gk2)
    return dz, _grad_w_in(dzt, h), acc_attn, acc_qk


def _local_step(x, p, target, g1, shards, gq, gk, sinks, conv_w, w_out, g2, w_pg, b_pg, w_pp, g3):
    rope, gq2, gk2, conv_wp = _tables(x.shape[0], gq, gk, conv_w)
    h, z, w_pairs = _forward_in(x, g1, shards)
    qn, k2, v2, a, mix, mixt = _forward_attn(z, rope, gq2, gk2, conv_wp, sinks)
    dx1, dx1b, (gw_out, gw_pg, gw_pp), acc_ple, acc_g2 = _forward_out(
        x, p, target, mix, mixt, w_out, g2, w_pg, b_pg, w_pp, g3)
    dmix = _mm_nt(dx1b, w_out, min(512, x.shape[0]), "out_bwd", dx1b)
    dz, gw_in, acc_attn, acc_qk = _backward_attn(dmix, h, z, qn, k2, v2, a, rope, gq2, gk2, conv_wp, sinks, dmix)
    grad_x, acc_g1 = _in_bwd(dz, w_pairs, x, dx1, g1, min(512, x.shape[0]), dx1)
    return grad_x, (gw_in, gw_out, gw_pg, gw_pp), (acc_g1, acc_g2, acc_ple, acc_qk, acc_attn)


def _by_owner(g):
    return g.reshape((4, 2) + g.shape[1:])


def kernel(x, p, norm_gain, w_in, q_norm_gain, k_norm_gain, attn_sinks, conv_w, w_out, ple_gate_norm_gain, w_ple_gate, b_ple_gate, w_ple_proj, ple_norm_gain, loss_target, m_norm_gain, m_w_in, m_q_norm_gain, m_k_norm_gain, m_attn_sinks, m_conv_w, m_w_out, m_ple_gate_norm_gain, m_w_ple_gate, m_b_ple_gate, m_w_ple_proj, m_ple_norm_gain, v_norm_gain, v_w_in, v_q_norm_gain, v_k_norm_gain, v_attn_sinks, v_conv_w, v_w_out, v_ple_gate_norm_gain, v_w_ple_gate, v_b_ple_gate, v_w_ple_proj, v_ple_norm_gain):
    me = 4 * lax.axis_index("x") + 2 * lax.axis_index("y") + lax.axis_index("c")
    place = jnp.stack([lax.axis_index("c"), 2 * lax.axis_index("x") + lax.axis_index("y")]).astype(jnp.int32)
    xs, ps, target = x[0], p[0, 0], loss_target[0]

    shard_in = w_in[0].T.astype(BF16)
    own_late = [w_out[0].astype(BF16), w_ple_gate[0].astype(BF16), w_ple_proj[0].astype(BF16)]
    with_own = lambda gathered, own: lax.dynamic_update_slice(gathered, own[None], (me,) + (0,) * own.ndim)
    early, started = _gather_start([shard_in, conv_w[0]], shard_in, relay=True)
    tm = min(512, xs.shape[0])
    h = _prenorm(xs, norm_gain, tm, started)

    z, w_pairs = lax.empty((xs.shape[0], IN_W), F32), lax.empty((N_PAIRS, PAIR_W, D_MODEL), BF16)
    early = _gather_from_sibling(early, h)
    pair_of = lambda flip: jnp.stack([place[1] ^ flip, place[0]])
    z, w_pairs = _fwd_in_pair(h, early["lands"][0], z, w_pairs, pair_of(0), tm, "fwd_in_own", own=shard_in)
    for j, flip in enumerate((2, 1, 3)):
        early = _gather_from_chip(early, j, (z,) if j != 2 else (z, started_late), last=j == 2)
        z, w_pairs = _fwd_in_pair(h, early["lands"][0], z, w_pairs, pair_of(flip), tm, "fwd_in_chip_" + str(j))
        if j == 1:
            late, started_late = _gather_start(own_late, z)
    conv_full = jnp.transpose(with_own(early["lands"][1], conv_w[0]), (1, 0, 2)).reshape(3, ATTN_W)
    rope, gq2, gk2, conv_wp = _tables(xs.shape[0], q_norm_gain[0], k_norm_gain[0], conv_full)
    qn, k2, v2 = _qk_prep(z, *rope, gq2, gk2, min(256, xs.shape[0]), z)
    late, forwarded = _gather_forward(late, qn)
    a, mix, mixt = _attn_fwd(qn, k2, v2, z, conv_wp, attn_sinks, forwarded)
    g_out, g_pg, g_pp = (with_own(g, own) for g, own in zip(_gather_wait(late, mix), own_late))
    w_out_f = g_out.reshape(D_MODEL, D_MODEL)
    w_pg_f = g_pg.reshape(D_MODEL, D_MODEL)
    w_pp_f = jnp.transpose(g_pp, (1, 0, 2)).reshape(PLE_DIM, D_MODEL)

    dx1, dx1b, (gw_out, gw_pg, gw_pp), acc_ple, acc_g2 = _forward_out(
        xs, ps, target, mix, mixt, w_out_f, ple_gate_norm_gain, w_pg_f, b_ple_gate, w_pp_f, ple_norm_gain)

    names = ("w_out", "w_ple_gate", "w_ple_proj")
    gw_pp_t = jnp.transpose(gw_pp.reshape(PLE_DIM, N_DEV, PLE_DIM), (1, 0, 2))
    grads = [_by_owner(gw_out.reshape(N_DEV, D_MODEL // N_DEV, D_MODEL)),
             _by_owner(gw_pg.reshape(N_DEV, D_MODEL // N_DEV, D_MODEL)), _by_owner(gw_pp_t)]
    pairs, paired = _exchange_start("pair_start", grads, [(4,) + g.shape[2:] for g in grads], _to_sibling, 1, dx1b)
    dmix = _mm_nt(dx1b, w_out_f, tm, "out_bwd", paired)
    from_sibling = _exchange_wait("pair_wait", pairs, _to_sibling, (dmix,))
    sums = [_pair_sum(g, r, place, 256, "pair_sum_" + nm) for g, r, nm in zip(pairs["srcs"], from_sibling, names)]
    chips, sent = _exchange_start("chip_start", [pb for pb, _ in sums], [(3,) + pb.shape[1:] for pb, _ in sums],
                                  _to_chips, 3, sums[-1][1])

    dz, gw_in, acc_attn, acc_qk = _backward_attn(
        dmix, h, z, qn, k2, v2, a, rope, gq2, gk2, conv_wp, attn_sinks, sent)

    gw_in_t = [_by_owner(gw_in)]
    pairs_in, paired_in = _exchange_start("pair_start_w_in", gw_in_t, [(4,) + gw_in_t[0].shape[2:]], _to_sibling, 1,
                                          gw_in)
    from_chips = _exchange_wait("chip_wait", chips, _to_chips, (gw_in,))
    big = {}
    for (_, own), oth, w, m, v, nm in zip(sums, from_chips, (w_out, w_ple_gate, w_ple_proj),
                                          (m_w_out, m_w_ple_gate, m_w_ple_proj),
                                          (v_w_out, v_w_ple_gate, v_w_ple_proj), names):
        big[nm] = [t[None] for t in _adamw(own, oth, w[0], m[0], v[0], 256, "adamw_" + nm, paired_in)]

    (from_sibling_in,) = _exchange_wait("pair_wait_w_in", pairs_in, _to_sibling, [big[nm][0] for nm in names])
    pb_in, own_in = _pair_sum(pairs_in["srcs"][0], from_sibling_in, place, SHARD_IN // 2, "pair_sum_w_in")
    chips_in, sent_in = _exchange_start("chip_start_w_in", [pb_in], [(3,) + pb_in.shape[1:]], _to_chips, 3, own_in)
    grad_x, acc_g1 = _in_bwd(dz, w_pairs, xs, dx1, norm_gain, tm, sent_in)
    (from_chips_in,) = _exchange_wait("chip_wait_w_in", chips_in, _to_chips, (grad_x,))
    big["w_in"] = [t.T[None] for t in _adamw(own_in, from_chips_in, w_in[0].T, m_w_in[0].T, v_w_in[0].T, SHARD_IN // 4,
                                             "adamw_w_in", grad_x)]

    red = _reduce_small(acc_g1, acc_g2, acc_ple, acc_qk, acc_attn)
    small = [norm_gain, ple_gate_norm_gain, b_ple_gate, ple_norm_gain, q_norm_gain, k_norm_gain, attn_sinks]
    small_m = [m_norm_gain, m_ple_gate_norm_gain, m_b_ple_gate, m_ple_norm_gain, m_q_norm_gain, m_k_norm_gain,
               m_attn_sinks]
    small_v = [v_norm_gain, v_ple_gate_norm_gain, v_b_ple_gate, v_ple_norm_gain, v_q_norm_gain, v_k_norm_gain,
               v_attn_sinks]
    taps_first = lambda t: jnp.transpose(t, (1, 0, 2))
    loss, kinds = _adamw_small(red, me.reshape(1, 1).astype(jnp.int32), small + [taps_first(conv_w)],
                               small_m + [taps_first(m_conv_w)], small_v + [taps_first(v_conv_w)])

    def order(k):
        sm = kinds[k]
        return [sm[0], big["w_in"][k], sm[4], sm[5], sm[6], taps_first(sm[7]), big["w_out"][k], sm[1],
                big["w_ple_gate"][k], sm[2], big["w_ple_proj"][k], sm[3]]

    return (loss[0, 0], grad_x[None], *order(0), *order(1), *order(2), *order(3))
```

```python
import jax
import jax.numpy as jnp
from jax import lax
from jax.experimental import pallas as pl
from jax.experimental.pallas import tpu as pltpu

F32, BF16 = jnp.float32, jnp.bfloat16

D_MODEL = 2048
PLE_DIM = 256
ATTN_W = 1024
HEAD = 64
N_Q_HEADS = 16
KV_W = 256
QKV_W = ATTN_W + 2 * KV_W
REST_W = 5 * 1024
IN_W = QKV_W + REST_W
GATE_A0, CONV_B0, CONV_C0, CONV_H0, GATE_C0 = (QKV_W + 1024 * t for t in range(5))
K2_W = 4 * 128
ROT = 16
ROPE_THETA = 500000.0
EPS = 1e-6
NEG_INF = -1e30
BLK = 128
LANES = 128
SUBLANES = 8
N_DEV = 8
SHARD_IN = IN_W // N_DEV
PAIR_W = 2 * SHARD_IN
N_PAIRS = IN_W // PAIR_W
SLAB_ROWS = 8
PACKED_ROWS = 16
SUB_ROWS = 128
V7X_VMEM_LIMIT = 52 * 1024 * 1024

ADAM_LR, ADAM_B1, ADAM_B2, ADAM_EPS, ADAM_WD, ADAM_STEP = 0.001, 0.9, 0.999, 1e-08, 0.01, 10
MESH = pl.DeviceIdType.MESH


def _params(*semantics):
    return pltpu.CompilerParams(dimension_semantics=semantics, vmem_limit_bytes=V7X_VMEM_LIMIT)


ANY = pl.BlockSpec(memory_space=pl.ANY)


def _resident(shape):
    return pl.BlockSpec(shape, lambda *_: (0,) * len(shape), pipeline_mode=pl.Buffered(1))


def _dot(a, b):
    return jnp.dot(a, b, preferred_element_type=F32)


def _dot_nt(a, b):
    return lax.dot_general(a, b, (((1,), (1,)), ((), ())), preferred_element_type=F32)


def _rms(xf):
    r = lax.rsqrt(jnp.mean(xf * xf, axis=-1, keepdims=True) + EPS)
    return xf * r, r


def _rms_bwd(dxn, xn, r):
    return r * (dxn - xn * jnp.mean(dxn * xn, axis=-1, keepdims=True))


def _sig(g):
    return jax.nn.sigmoid(g)


def _dsilu(g, sg):
    return sg * (1.0 + g * (1.0 - sg))


def _low_half(shape):
    return lax.broadcasted_iota(jnp.int32, shape, len(shape) - 1) < HEAD


def _half_sums(v):
    lo = _low_half(v.shape)
    s_lo = jnp.sum(jnp.where(lo, v, 0.0), axis=-1, keepdims=True)
    s_hi = jnp.sum(jnp.where(lo, 0.0, v), axis=-1, keepdims=True)
    return jnp.where(lo, s_lo, s_hi)


def _rope(v, a, bm, bp):
    return v * a + pltpu.roll(v, LANES - ROT // 2, 1) * bm + pltpu.roll(v, ROT // 2, 1) * bp


def _rope_t(dy, a, bm, bp):
    return dy * a + pltpu.roll(dy * bm, ROT // 2, 1) + pltpu.roll(dy * bp, LANES - ROT // 2, 1)


def _dup_halves(v):
    lo = _low_half(v.shape)
    a = jnp.where(lo, v, 0.0)
    b = jnp.where(lo, 0.0, v)
    return a + pltpu.roll(a, HEAD, 1), b + pltpu.roll(b, HEAD, 1)


def _rope_tables(s):
    half = ROT // 2
    lane = lax.broadcasted_iota(jnp.int32, (s, LANES), 1) % HEAD
    pos = lax.broadcasted_iota(jnp.int32, (half, s), 1).astype(F32)
    freq = lax.broadcasted_iota(jnp.int32, (half, s), 0).astype(F32)
    ang = pos * jnp.power(jnp.float32(ROPE_THETA), -freq * 2.0 / ROT)
    cos, sin = lax.optimization_barrier((jnp.cos(ang), jnp.sin(ang)))
    cos, sin = (jnp.tile(t.T, (1, LANES // half)) for t in (cos, sin))
    a = jnp.where(lane < ROT, cos, 1.0)
    bm = jnp.where(lane < half, -sin, 0.0)
    bp = jnp.where((lane >= half) & (lane < ROT), sin, 0.0)
    return a, bm, bp


def _prenorm(x, g1, tm, after):
    s = x.shape[0]

    def body(x_ref, g_ref, after_ref, h_ref):
        xn, _ = _rms(x_ref[...])
        h_ref[...] = (xn * g_ref[...]).astype(BF16)

    return pl.pallas_call(
        body, name="prenorm",
        out_shape=jax.ShapeDtypeStruct((s, D_MODEL), BF16),
        grid=(s // tm,),
        in_specs=[pl.BlockSpec((tm, D_MODEL), lambda i: (i, 0)), pl.BlockSpec((1, D_MODEL), lambda i: (0, 0)), ANY],
        out_specs=pl.BlockSpec((tm, D_MODEL), lambda i: (i, 0)),
        compiler_params=_params("parallel"))(x, g1, after)


def _fwd_in_pair(h, shards, z, w_pairs, pair, tm, name, own=None):
    s = h.shape[0]

    def body(pair_ref, h_ref, lo_ref, hi_ref, z_in, wp_in, z_ref, wp_ref):
        @pl.when(pl.program_id(0) == 0)
        def _():
            wp_ref[0, 0:SHARD_IN, :] = lo_ref[0]
            wp_ref[0, SHARD_IN:PAIR_W, :] = hi_ref[0]

        z_ref[...] = _dot_nt(h_ref[...], wp_ref[0])

    def body_own(pair_ref, h_ref, own_ref, other_ref, z_in, wp_in, z_ref, wp_ref):
        @pl.when(pl.program_id(0) == 0)
        def _():
            first = pl.multiple_of(pair_ref[1] * SHARD_IN, SHARD_IN)
            wp_ref[0, pl.ds(first, SHARD_IN), :] = own_ref[...]
            wp_ref[0, pl.ds(SHARD_IN - first, SHARD_IN), :] = other_ref[0]

        z_ref[...] = _dot_nt(h_ref[...], wp_ref[0])

    if own is None:
        blocks = [pl.BlockSpec((1, SHARD_IN, D_MODEL), lambda i, p: (2 * p[0], 0, 0)),
                  pl.BlockSpec((1, SHARD_IN, D_MODEL), lambda i, p: (2 * p[0] + 1, 0, 0))]
        operands = (shards, shards)
    else:
        blocks = [pl.BlockSpec((SHARD_IN, D_MODEL), lambda i, p: (0, 0)),
                  pl.BlockSpec((1, SHARD_IN, D_MODEL), lambda i, p: (2 * p[0] + 1 - p[1], 0, 0))]
        operands = (own, shards)
    grid_spec = pltpu.PrefetchScalarGridSpec(
        num_scalar_prefetch=1, grid=(s // tm,),
        in_specs=[pl.BlockSpec((tm, D_MODEL), lambda i, p: (i, 0)), *blocks, ANY, ANY],
        out_specs=(pl.BlockSpec((tm, PAIR_W), lambda i, p: (i, p[0])),
                   pl.BlockSpec((1, PAIR_W, D_MODEL), lambda i, p: (p[0], 0, 0))))
    return pl.pallas_call(
        body if own is None else body_own, name=name, grid_spec=grid_spec,
        out_shape=(jax.ShapeDtypeStruct(z.shape, z.dtype), jax.ShapeDtypeStruct(w_pairs.shape, w_pairs.dtype)),
        input_output_aliases={4: 0, 5: 1},
        compiler_params=_params("arbitrary"))(pair, h, *operands, z, w_pairs)


def _qk_prep(z, ra, rbm, rbp, gq2, gk2, tm, after):
    s = z.shape[0]

    def body(z_ref, a_ref, bm_ref, bp_ref, gq_ref, gk_ref, after_ref, q_ref, k2_ref, v2_ref):
        a, bm, bp = a_ref[...], bm_ref[...], bp_ref[...]
        for r in range(ATTN_W // LANES):
            x = z_ref[:, LANES * r:LANES * (r + 1)]
            rr = lax.rsqrt(_half_sums(x * x) * (1.0 / HEAD) + EPS)
            q_ref[:, LANES * r:LANES * (r + 1)] = _rope(x * rr * gq_ref[...], a, bm, bp).astype(BF16)
        for m in range(KV_W // LANES):
            x = z_ref[:, ATTN_W + LANES * m:ATTN_W + LANES * (m + 1)]
            rr = lax.rsqrt(_half_sums(x * x) * (1.0 / HEAD) + EPS)
            k_lo, k_hi = _dup_halves(_rope(x * rr * gk_ref[...], a, bm, bp))
            k2_ref[:, 2 * LANES * m:2 * LANES * m + LANES] = k_lo.astype(BF16)
            k2_ref[:, 2 * LANES * m + LANES:2 * LANES * (m + 1)] = k_hi.astype(BF16)
            v_lo, v_hi = _dup_halves(z_ref[:, ATTN_W + KV_W + LANES * m:ATTN_W + KV_W + LANES * (m + 1)])
            v2_ref[:, 2 * LANES * m:2 * LANES * m + LANES] = v_lo.astype(BF16)
            v2_ref[:, 2 * LANES * m + LANES:2 * LANES * (m + 1)] = v_hi.astype(BF16)

    row = lambda w: pl.BlockSpec((tm, w), lambda i: (i, 0))
    one = pl.BlockSpec((1, LANES), lambda i: (0, 0))
    return pl.pallas_call(
        body, name="qk_prep",
        out_shape=(jax.ShapeDtypeStruct((s, ATTN_W), BF16), jax.ShapeDtypeStruct((s, K2_W), BF16),
                   jax.ShapeDtypeStruct((s, K2_W), BF16)),
        grid=(s // tm,),
        in_specs=[row(PAIR_W), row(LANES), row(LANES), row(LANES), one, one, ANY],
        out_specs=(row(ATTN_W), row(K2_W), row(K2_W)),
        compiler_params=_params("parallel"))(z, ra, rbm, rbp, gq2, gk2, after)


GROUP = 4


def _window_mask(n):
    row = lax.broadcasted_iota(jnp.int32, (GROUP * BLK, 2 * BLK), 0) % BLK
    col = lax.broadcasted_iota(jnp.int32, (GROUP * BLK, 2 * BLK), 1)
    return (col > row) & (col <= row + BLK) & ((col >= BLK) | (n > 0))


def _stack_heads(pairs, zero):
    lo = _low_half(pairs[0].shape)
    parts = []
    for v in pairs:
        parts += [jnp.where(lo, v, zero), jnp.where(lo, zero, v)]
    return jnp.concatenate(parts, axis=0)


def _unstack_heads(v4):
    lo = _low_half((BLK, LANES))
    return [jnp.where(lo, v4[2 * i * BLK:(2 * i + 1) * BLK], v4[(2 * i + 1) * BLK:(2 * i + 2) * BLK]) for i in range(2)]


def _group_sinks(sink_ref, kvh):
    slot = lax.broadcasted_iota(jnp.int32, (GROUP * BLK, 1), 0) // BLK
    col = jnp.zeros((GROUP * BLK, 1), F32)
    for i in range(GROUP):
        col = jnp.where(slot == i, sink_ref[0, GROUP * kvh + i], col)
    return col, slot


def _head_probs(qm, kw, valid, sink):
    sc = jnp.where(valid, _dot_nt(qm, kw) * (HEAD ** -0.5), NEG_INF)
    mx = jnp.maximum(jnp.max(sc, axis=-1, keepdims=True), sink)
    ex = jnp.exp(sc - mx)
    den = jnp.sum(ex, axis=-1, keepdims=True) + jnp.exp(sink - mx)
    return ex / den, mx, den


def _cols(start, width=ATTN_W):
    return slice(start, start + width)


def _conv_fwd(z_ref, zp_ref, cw_ref, ext_ref, n):
    u = z_ref[:, _cols(CONV_C0)] * z_ref[:, _cols(CONV_H0)]
    pu = zp_ref[:, _cols(CONV_C0)] * zp_ref[:, _cols(CONV_H0)]
    ext_ref[0:SUBLANES, :] = jnp.where(n > 0, pu, 0.0)
    ext_ref[SUBLANES:SUBLANES + BLK, :] = u
    um1 = ext_ref[SUBLANES - 1:SUBLANES - 1 + BLK, :]
    um2 = ext_ref[SUBLANES - 2:SUBLANES - 2 + BLK, :]
    cv = cw_ref[0:1, :] * um2 + cw_ref[1:2, :] * um1 + cw_ref[2:3, :] * u
    return u, um1, um2, cv


def _prev_rows(n):
    return (jnp.maximum(n * (BLK // SUBLANES) - 1, 0), 0)


def _attn_fwd(qn, k2, v2, z, conv_wp, sinks, after):
    s = qn.shape[0]
    nb = s // BLK

    def body(sink_ref, q_ref, kc_ref, kp_ref, vc_ref, vp_ref, z_ref, zp_ref, cw_ref, after_ref, a_ref, mix_ref,
             mixt_ref, ext_ref):
        n = pl.program_id(0)
        valid = _window_mask(n)
        for kvh in range(K2_W // LANES):
            cols = slice(LANES * kvh, LANES * (kvh + 1))
            kw = jnp.concatenate([kp_ref[:, cols], kc_ref[:, cols]], axis=0)
            vw = jnp.concatenate([vp_ref[:, cols], vc_ref[:, cols]], axis=0)
            blocks = [slice(LANES * r, LANES * (r + 1)) for r in (2 * kvh, 2 * kvh + 1)]
            q4 = _stack_heads([q_ref[:, rc] for rc in blocks], jnp.zeros((BLK, LANES), BF16))
            p, _, _ = _head_probs(q4, kw, valid, _group_sinks(sink_ref, kvh)[0])
            for rc, a in zip(blocks, _unstack_heads(_dot(p.astype(BF16), vw))):
                a_ref[:, rc] = a
                g = z_ref[:, _cols(GATE_A0 + rc.start, LANES)]
                mix_ref[:, rc] = (a * (g * _sig(g))).astype(BF16)
        _, _, _, cv = _conv_fwd(z_ref, zp_ref, cw_ref, ext_ref, n)
        gc = z_ref[:, _cols(GATE_C0)]
        mix_ref[:, ATTN_W:D_MODEL] = (z_ref[:, _cols(CONV_B0)] * cv * (gc * _sig(gc))).astype(BF16)
        mixt_ref[...] = mix_ref[...].T

    cur = lambda w: pl.BlockSpec((BLK, w), lambda n: (n, 0))
    prev = lambda w: pl.BlockSpec((BLK, w), lambda n: (jnp.maximum(n - 1, 0), 0))
    return pl.pallas_call(
        body, name="attn_fwd",
        out_shape=(jax.ShapeDtypeStruct((s, ATTN_W), F32), jax.ShapeDtypeStruct((s, D_MODEL), BF16),
                   jax.ShapeDtypeStruct((D_MODEL, s), BF16)),
        grid=(nb,),
        in_specs=[pl.BlockSpec(memory_space=pltpu.SMEM),
                  cur(ATTN_W), cur(K2_W), prev(K2_W), cur(K2_W), prev(K2_W), cur(IN_W),
                  pl.BlockSpec((SUBLANES, IN_W), _prev_rows),
                  pl.BlockSpec((SUBLANES, ATTN_W), lambda n: (0, 0)), ANY],
        out_specs=(cur(ATTN_W), cur(D_MODEL), pl.BlockSpec((D_MODEL, BLK), lambda n: (0, n))),
        scratch_shapes=[pltpu.VMEM((BLK + 2 * SUBLANES, ATTN_W), F32)],
        compiler_params=_params("parallel"))(sinks, qn, k2, k2, v2, v2, z, z, conv_wp, after)


def _fwd_out(mix, w_out, x, g2, tm):
    s = x.shape[0]

    def body(m_ref, w_ref, x_ref, g_ref, x1_ref, h_ref, ht_ref):
        x1 = x_ref[...] + _dot(m_ref[...], w_ref[...])
        x1_ref[...] = x1
        xn, _ = _rms(x1)
        h = (xn * g_ref[...]).astype(BF16)
        h_ref[...] = h
        ht_ref[...] = h.T

    row = pl.BlockSpec((tm, D_MODEL), lambda i: (i, 0))
    return pl.pallas_call(
        body, name="fwd_out",
        out_shape=(jax.ShapeDtypeStruct((s, D_MODEL), F32), jax.ShapeDtypeStruct((s, D_MODEL), BF16),
                   jax.ShapeDtypeStruct((D_MODEL, s), BF16)),
        grid=(s // tm,),
        in_specs=[row, _resident((D_MODEL, D_MODEL)), row, pl.BlockSpec((1, D_MODEL), lambda i: (0, 0))],
        out_specs=(row, row, pl.BlockSpec((D_MODEL, tm), lambda i: (0, i))),
        compiler_params=_params("parallel"))(mix, w_out, x, g2)


def _ple(hn2, w_pg, b_pg, p, w_pp, g3, x1, target, tm):
    s = x1.shape[0]

    def body(h_ref, wg_ref, b_ref, p_ref, wp_ref, g3_ref, x1_ref, t_ref, dy_ref, dgp_ref, dt_ref, pt_ref, acc_ref):
        gate = _sig(_dot(h_ref[...], wg_ref[...]) + b_ref[...])
        pb = p_ref[...].astype(BF16)
        pt_ref[...] = pb.T
        t = _dot(pb, wp_ref[...])
        tn, r3 = _rms(t)
        e = tn * g3_ref[...]
        diff = x1_ref[...] + gate * e - t_ref[...]
        dy = diff * (1.0 / D_MODEL)
        dy_ref[...] = dy
        dgp = dy * e * (gate * (1.0 - gate))
        dgp_ref[...] = dgp.astype(BF16)
        de = dy * gate
        dt_ref[...] = _rms_bwd(de * g3_ref[...], tn, r3).astype(BF16)

        @pl.when(pl.program_id(0) == 0)
        def _():
            acc_ref[...] = jnp.zeros_like(acc_ref)

        acc_ref[0:1, :] += jnp.sum(dgp, axis=0, keepdims=True)
        acc_ref[1:2, :] += jnp.sum(de * tn, axis=0, keepdims=True)
        acc_ref[2:3, :] += jnp.sum(diff * diff, axis=0, keepdims=True) * (0.5 / D_MODEL)

    row = pl.BlockSpec((tm, D_MODEL), lambda i: (i, 0))
    vec = pl.BlockSpec((1, D_MODEL), lambda i: (0, 0))
    return pl.pallas_call(
        body, name="ple",
        out_shape=(jax.ShapeDtypeStruct((s, D_MODEL), F32), jax.ShapeDtypeStruct((s, D_MODEL), BF16),
                   jax.ShapeDtypeStruct((s, D_MODEL), BF16), jax.ShapeDtypeStruct((PLE_DIM, s), BF16),
                   jax.ShapeDtypeStruct((SUBLANES, D_MODEL), F32)),
        grid=(s // tm,),
        in_specs=[row, _resident((D_MODEL, D_MODEL)), vec, pl.BlockSpec((tm, PLE_DIM), lambda i: (i, 0)),
                  _resident((PLE_DIM, D_MODEL)), vec, row, row],
        out_specs=(row, row, row, pl.BlockSpec((PLE_DIM, tm), lambda i: (0, i)),
                   pl.BlockSpec((SUBLANES, D_MODEL), lambda i: (0, 0))),
        compiler_params=_params("arbitrary"))(hn2, w_pg, b_pg, p, w_pp, g3, x1, target)


def _gate_bwd(dgp, w_pg, x1, dy, g2, tm):
    s = x1.shape[0]

    def body(d_ref, w_ref, x1_ref, dy_ref, g_ref, dx_ref, dxb_ref, acc_ref):
        dh = _dot_nt(d_ref[...], w_ref[...])
        xn, r = _rms(x1_ref[...])
        dx1 = dy_ref[...] + _rms_bwd(dh * g_ref[...], xn, r)
        dx_ref[...] = dx1
        dxb_ref[...] = dx1.astype(BF16)

        @pl.when(pl.program_id(0) == 0)
        def _():
            acc_ref[...] = jnp.zeros_like(acc_ref)

        acc_ref[0:1, :] += jnp.sum(dh * xn, axis=0, keepdims=True)

    row = pl.BlockSpec((tm, D_MODEL), lambda i: (i, 0))
    return pl.pallas_call(
        body, name="gate_bwd",
        out_shape=(jax.ShapeDtypeStruct((s, D_MODEL), F32), jax.ShapeDtypeStruct((s, D_MODEL), BF16),
                   jax.ShapeDtypeStruct((SUBLANES, D_MODEL), F32)),
        grid=(s // tm,),
        in_specs=[row, _resident((D_MODEL, D_MODEL)), row, row, pl.BlockSpec((1, D_MODEL), lambda i: (0, 0))],
        out_specs=(row, row, pl.BlockSpec((SUBLANES, D_MODEL), lambda i: (0, 0))),
        compiler_params=_params("arbitrary"))(dgp, w_pg, x1, dy, g2)


def _mm_nt(a, b, tm, name, after):
    m, k = a.shape
    n = b.shape[0]

    def body(a_ref, b_ref, after_ref, o_ref):
        o_ref[...] = _dot_nt(a_ref[...], b_ref[...])

    return pl.pallas_call(
        body, name=name,
        out_shape=jax.ShapeDtypeStruct((m, n), F32),
        grid=(m // tm,),
        in_specs=[pl.BlockSpec((tm, k), lambda i: (i, 0)), _resident((n, k)), ANY],
        out_specs=pl.BlockSpec((tm, n), lambda i: (i, 0)),
        compiler_params=_params("parallel"))(a, b, after)


def _attn_bwd(qn, k2, v2, a, z, dmix, conv_wp, sinks, after):
    s = qn.shape[0]
    nb = s // BLK

    def body(sink_ref, q_ref, kc_ref, kp_ref, vc_ref, vp_ref, a_ref, z_ref, zp_ref, zn_ref, dm_ref, dmn_ref,
             cw_ref, after_ref, dq_ref, dkc_ref, dkp_ref, dvc_ref, dvp_ref, dz_ref, dzt_ref, acc_ref, ext_ref):
        n = pl.program_id(0)
        valid = _window_mask(n)
        lane = lax.broadcasted_iota(jnp.int32, (1, ATTN_W), 1)

        @pl.when(n == 0)
        def _():
            acc_ref[...] = jnp.zeros_like(acc_ref)

        dz_ref[:, 0:QKV_W] = jnp.zeros((BLK, QKV_W), BF16)
        dsink = jnp.zeros((1, ATTN_W), F32)
        for kvh in range(K2_W // LANES):
            cols = slice(LANES * kvh, LANES * (kvh + 1))
            kw = jnp.concatenate([kp_ref[:, cols], kc_ref[:, cols]], axis=0)
            vw = jnp.concatenate([vp_ref[:, cols], vc_ref[:, cols]], axis=0)
            blocks = [slice(LANES * r, LANES * (r + 1)) for r in (2 * kvh, 2 * kvh + 1)]
            das, avs = [], []
            for rc in blocks:
                g = z_ref[:, _cols(GATE_A0 + rc.start, LANES)]
                sg = _sig(g)
                dm = dm_ref[:, rc]
                av = a_ref[:, rc]
                das.append(dm * (g * sg))
                avs += [av, av]
                dz_ref[:, _cols(GATE_A0 + rc.start, LANES)] = (dm * av * _dsilu(g, sg)).astype(BF16)
            q4 = _stack_heads([q_ref[:, rc] for rc in blocks], jnp.zeros((BLK, LANES), BF16))
            sink, slot = _group_sinks(sink_ref, kvh)
            p, mx, den = _head_probs(q4, kw, valid, sink)
            do4 = _stack_heads(das, 0.0)
            delta = jnp.sum(do4 * jnp.concatenate(avs, axis=0), axis=-1, keepdims=True)
            dob = do4.astype(BF16)
            ds = p * (_dot_nt(dob, vw) - delta) * (HEAD ** -0.5)
            for rc, dq in zip(blocks, _unstack_heads(_dot(ds.astype(BF16), kw))):
                dq_ref[:, rc] = dq
            dk2 = _dot(ds.T.astype(BF16), q4)
            dv2 = _dot(p.T.astype(BF16), dob)
            dkp_ref[:, cols] = dk2[0:BLK]
            dkc_ref[:, cols] = dk2[BLK:2 * BLK]
            dvp_ref[:, cols] = dv2[0:BLK]
            dvc_ref[:, cols] = dv2[BLK:2 * BLK]
            dsk = jnp.exp(sink - mx) / den * delta
            for i in range(GROUP):
                dsink = dsink - jnp.where(lane == GROUP * kvh + i,
                                          jnp.sum(jnp.where(slot == i, dsk, 0.0), axis=0, keepdims=True), 0.0)
        acc_ref[0:1, :] += dsink

        u, um1, um2, cv = _conv_fwd(z_ref, zp_ref, cw_ref, ext_ref, n)
        cb = z_ref[:, _cols(CONV_B0)]
        gc = z_ref[:, _cols(GATE_C0)]
        sgc = _sig(gc)
        dmc = dm_ref[:, ATTN_W:D_MODEL]
        t = dmc * (gc * sgc)
        dcv = t * cb
        dz_ref[:, _cols(CONV_B0)] = (t * cv).astype(BF16)
        dz_ref[:, _cols(GATE_C0)] = (dmc * cb * cv * _dsilu(gc, sgc)).astype(BF16)
        gcn = zn_ref[:, _cols(GATE_C0)]
        dcvn = dmn_ref[:, ATTN_W:D_MODEL] * (gcn * _sig(gcn)) * zn_ref[:, _cols(CONV_B0)]
        ext_ref[0:BLK, :] = dcv
        ext_ref[BLK:BLK + SUBLANES, :] = jnp.where(n < nb - 1, dcvn, 0.0)
        du = (cw_ref[2:3, :] * dcv + cw_ref[1:2, :] * ext_ref[1:1 + BLK, :]
              + cw_ref[0:1, :] * ext_ref[2:2 + BLK, :])
        dz_ref[:, _cols(CONV_C0)] = (du * z_ref[:, _cols(CONV_H0)]).astype(BF16)
        dz_ref[:, _cols(CONV_H0)] = (du * z_ref[:, _cols(CONV_C0)]).astype(BF16)
        acc_ref[1:2, :] += jnp.sum(dcv * um2, axis=0, keepdims=True)
        acc_ref[2:3, :] += jnp.sum(dcv * um1, axis=0, keepdims=True)
        acc_ref[3:4, :] += jnp.sum(dcv * u, axis=0, keepdims=True)
        dzt_ref[...] = dz_ref[...].T

    cur = lambda w: pl.BlockSpec((BLK, w), lambda n: (n, 0))
    prev = lambda w: pl.BlockSpec((BLK, w), lambda n: (jnp.maximum(n - 1, 0), 0))
    nxt = lambda w: pl.BlockSpec(
        (SUBLANES, w), lambda n: (jnp.minimum((n + 1) * (BLK // SUBLANES), nb * (BLK // SUBLANES) - 1), 0))
    f32 = lambda w: jax.ShapeDtypeStruct((s, w), F32)
    return pl.pallas_call(
        body, name="attn_bwd",
        out_shape=(f32(ATTN_W), f32(K2_W), f32(K2_W), f32(K2_W), f32(K2_W),
                   jax.ShapeDtypeStruct((s, IN_W), BF16), jax.ShapeDtypeStruct((IN_W, s), BF16),
                   jax.ShapeDtypeStruct((SUBLANES, ATTN_W), F32)),
        grid=(nb,),
        in_specs=[pl.BlockSpec(memory_space=pltpu.SMEM),
                  cur(ATTN_W), cur(K2_W), prev(K2_W), cur(K2_W), prev(K2_W), cur(ATTN_W), cur(IN_W),
                  pl.BlockSpec((SUBLANES, IN_W), _prev_rows), nxt(IN_W), cur(D_MODEL), nxt(D_MODEL),
                  pl.BlockSpec((SUBLANES, ATTN_W), lambda n: (0, 0)), ANY],
        out_specs=(cur(ATTN_W), cur(K2_W), cur(K2_W), cur(K2_W), cur(K2_W), cur(IN_W),
                   pl.BlockSpec((IN_W, BLK), lambda n: (0, n)), pl.BlockSpec((SUBLANES, ATTN_W), lambda n: (0, 0))),
        scratch_shapes=[pltpu.VMEM((BLK + 2 * SUBLANES, ATTN_W), F32)],
        compiler_params=_params("arbitrary"))(sinks, qn, k2, k2, v2, v2, a, z, z, z, dmix, dmix, conv_wp, after)


def _qkv_bwd(z, dz, dzt, dq, dkc, dkp, dvc, dvp, ra, rbm, rbp, gq2, gk2):
    s = z.shape[0]
    nb = s // BLK

    def body(z_ref, dz_in, dzt_in, dq_ref, dkc_ref, dkp_ref, dvc_ref, dvp_ref, a_ref, bm_ref, bp_ref, gq_ref, gk_ref,
             dz_ref, dzt_ref, acc_ref):
        n = pl.program_id(0)
        a, bm, bp = a_ref[...], bm_ref[...], bp_ref[...]
        lo = _low_half((BLK, LANES))
        last = n == nb - 1

        @pl.when(n == 0)
        def _():
            acc_ref[...] = jnp.zeros_like(acc_ref)

        def norm_bwd(x, dy, gain):
            rr = lax.rsqrt(_half_sums(x * x) * (1.0 / HEAD) + EPS)
            xh = x * rr
            dxg = _rope_t(dy, a, bm, bp)
            dxh = dxg * gain
            dx = rr * (dxh - xh * (_half_sums(dxh * xh) * (1.0 / HEAD)))
            return dx, jnp.sum(dxg * xh, axis=0, keepdims=True)

        def folded(cur_ref, prev_ref, m):
            parts = []
            for h in (2 * m, 2 * m + 1):
                v = cur_ref[:, LANES * h:LANES * (h + 1)] + jnp.where(
                    last, 0.0, prev_ref[:, LANES * h:LANES * (h + 1)])
                parts.append(v + pltpu.roll(v, HEAD, 1))
            return jnp.where(lo, parts[0], parts[1])

        gq_acc = jnp.zeros((1, LANES), F32)
        for r in range(ATTN_W // LANES):
            rc = slice(LANES * r, LANES * (r + 1))
            dx, gg = norm_bwd(z_ref[:, rc], dq_ref[:, rc], gq_ref[...])
            dz_ref[:, rc] = dx.astype(BF16)
            gq_acc = gq_acc + gg
        acc_ref[0:1, :] += gq_acc
        gk_acc = jnp.zeros((1, LANES), F32)
        for m in range(KV_W // LANES):
            kc = slice(ATTN_W + LANES * m, ATTN_W + LANES * (m + 1))
            dx, gg = norm_bwd(z_ref[:, kc], folded(dkc_ref, dkp_ref, m), gk_ref[...])
            dz_ref[:, kc] = dx.astype(BF16)
            gk_acc = gk_acc + gg
            vc = slice(ATTN_W + KV_W + LANES * m, ATTN_W + KV_W + LANES * (m + 1))
            dz_ref[:, vc] = folded(dvc_ref, dvp_ref, m).astype(BF16)
        acc_ref[1:2, :] += gk_acc
        dzt_ref[...] = dz_ref[...].T

    cur = lambda w: pl.BlockSpec((BLK, w), lambda n: (n, 0))
    nxt = lambda w: pl.BlockSpec((BLK, w), lambda n: (jnp.minimum(n + 1, nb - 1), 0))
    one = pl.BlockSpec((1, LANES), lambda n: (0, 0))
    return pl.pallas_call(
        body, name="qkv_bwd",
        out_shape=(jax.ShapeDtypeStruct(dz.shape, dz.dtype), jax.ShapeDtypeStruct(dzt.shape, dzt.dtype),
                   jax.ShapeDtypeStruct((SUBLANES, LANES), F32)),
        grid=(nb,),
        in_specs=[cur(PAIR_W), ANY, ANY, cur(ATTN_W), cur(K2_W), nxt(K2_W), cur(K2_W), nxt(K2_W),
                  cur(LANES), cur(LANES), cur(LANES), one, one],
        out_specs=(cur(QKV_W), pl.BlockSpec((QKV_W, BLK), lambda n: (0, n)),
                   pl.BlockSpec((SUBLANES, LANES), lambda n: (0, 0))),
        input_output_aliases={1: 0, 2: 1},
        compiler_params=_params("arbitrary"))(z, dz, dzt, dq, dkc, dkp, dvc, dvp, ra, rbm, rbp, gq2, gk2)


def _in_bwd(dz, w_pairs, x, dx1, g1, tm, after):
    s = x.shape[0]
    n = s // tm
    sub = tm // N_PAIRS
    stripes = 4

    def body(d_ref, w_ref, x_ref, dx1_ref, g_ref, after_ref, gx_ref, acc_ref, dh_ref):
        i, k = pl.program_id(0), pl.program_id(1)

        def matmul(c):
            cols = slice(c * (D_MODEL // stripes), (c + 1) * (D_MODEL // stripes))
            dh_ref[i % 2, :, cols] += _dot(d_ref[...], w_ref[0, :, cols])

        def norm_bwd(c):
            part = sub // stripes
            mine = slice(c * part, (c + 1) * part)
            rows = pl.ds(pl.multiple_of(k * sub + c * part, part), part)
            dh = dh_ref[(i + 1) % 2, rows, :]
            dh_ref[(i + 1) % 2, rows, :] = jnp.zeros_like(dh)
            xn, r = _rms(x_ref[mine, :])
            gx_ref[rows, :] = dx1_ref[mine, :] + _rms_bwd(dh * g_ref[...], xn, r)
            acc_ref[0:1, :] += jnp.sum(dh * xn, axis=0, keepdims=True)

        @pl.when((i == 0) & (k == 0))
        def _():
            acc_ref[...] = jnp.zeros_like(acc_ref)
            dh_ref[...] = jnp.zeros_like(dh_ref)

        @pl.when(i == 0)
        def _():
            for c in range(stripes):
                matmul(c)

        @pl.when((i > 0) & (i < n))
        def _():
            for c in range(stripes):
                matmul(c)
                norm_bwd(c)

        @pl.when(i == n)
        def _():
            for c in range(stripes):
                norm_bwd(c)

    last = lambda i, k: jnp.where(i == n, N_PAIRS - 1, k)
    rows_before = lambda i, k: (jnp.maximum(i - 1, 0) * N_PAIRS + k, 0)
    return pl.pallas_call(
        body, name="in_bwd",
        out_shape=(jax.ShapeDtypeStruct((s, D_MODEL), F32), jax.ShapeDtypeStruct((SUBLANES, D_MODEL), F32)),
        grid=(n + 1, N_PAIRS),
        in_specs=[pl.BlockSpec((tm, PAIR_W), lambda i, k: (jnp.minimum(i, n - 1), last(i, k))),
                  pl.BlockSpec((1, PAIR_W, D_MODEL), lambda i, k: (last(i, k), 0, 0)),
                  pl.BlockSpec((sub, D_MODEL), rows_before), pl.BlockSpec((sub, D_MODEL), rows_before),
                  pl.BlockSpec((1, D_MODEL), lambda i, k: (0, 0)), ANY],
        out_specs=(pl.BlockSpec((tm, D_MODEL), lambda i, k: (jnp.maximum(i - 1, 0), 0)),
                   pl.BlockSpec((SUBLANES, D_MODEL), lambda i, k: (0, 0))),
        scratch_shapes=[pltpu.VMEM((2, tm, D_MODEL), F32)],
        compiler_params=_params("arbitrary", "arbitrary"))(dz, w_pairs, x, dx1, g1, after)


def _mm_grad(at, bs, tn, name):
    m, kdim = at.shape
    nblk = [b.shape[1] // tn for b in bs]
    starts = [sum(nblk[:t]) for t in range(len(bs))]

    def body(a_ref, *refs):
        b_refs, o_ref = refs[:len(bs)], refs[len(bs)]
        j = pl.program_id(0)
        for t, b_ref in enumerate(b_refs):
            @pl.when((j >= starts[t]) & (j < starts[t] + nblk[t]))
            def _():
                o_ref[...] = _dot(a_ref[...], b_ref[...]).astype(BF16)

    def b_spec(t):
        return pl.BlockSpec((kdim, tn), lambda j: (0, jnp.clip(j - starts[t], 0, nblk[t] - 1)))

    return pl.pallas_call(
        body, name=name,
        out_shape=jax.ShapeDtypeStruct((m, sum(nblk) * tn), BF16),
        grid=(sum(nblk),),
        in_specs=[_resident((m, kdim))] + [b_spec(t) for t in range(len(bs))],
        out_specs=pl.BlockSpec((m, tn), lambda j: (0, j)),
        compiler_params=_params("parallel"))(at, *bs)


def _grad_w_in(dzt, h):
    kdim = h.shape[0]

    def body(d_ref, h_ref, o_ref):
        o_ref[0] = _dot(d_ref[...], h_ref[...]).astype(BF16)

    return pl.pallas_call(
        body, name="grad_w_in",
        out_shape=jax.ShapeDtypeStruct((N_DEV, SHARD_IN, D_MODEL), BF16),
        grid=(N_DEV,),
        in_specs=[pl.BlockSpec((SHARD_IN, kdim), lambda j: (j, 0)), _resident((kdim, D_MODEL))],
        out_specs=pl.BlockSpec((1, SHARD_IN, D_MODEL), lambda j: (j, 0, 0)),
        compiler_params=_params("parallel"))(dzt, h)


def _place():
    return lax.axis_index("x"), lax.axis_index("y"), lax.axis_index("c")


ROW_TAPS, ROW_MISC = 4, 5
Q_AT, K_AT, SINK_AT, LOSS_AT = (ATTN_W + LANES * t for t in range(4))
SMALL_AT = [(0, 0), (1, 0), (2, 0), (3, 0), (ROW_MISC, Q_AT), (ROW_MISC, K_AT), (ROW_MISC, SINK_AT)]


def _tap_at(tap):
    return ROW_TAPS + tap // 2, ATTN_W * (tap % 2)


def _reduce_small(acc_g1, acc_g2, acc_ple, acc_qk, acc_attn):
    def body(g1_ref, g2_ref, ple_ref, qk_ref, attn_ref, out_ref, slab_ref, gath_ref, send_sems, recv_sems):
        x, y, c = _place()
        me = 4 * x + 2 * y + c
        slab_ref[...] = jnp.zeros_like(slab_ref)
        slab_ref[0:1, :] = g1_ref[0:1, :]
        slab_ref[1:2, :] = g2_ref[0:1, :]
        slab_ref[2:4, :] = ple_ref[0:2, :]
        qk = qk_ref[0:2, :]
        qk = jnp.where(_low_half(qk.shape), qk + pltpu.roll(qk, HEAD, 1), 0.0)
        misc = slab_ref.at[ROW_MISC:ROW_MISC + 1]
        misc[:, Q_AT:Q_AT + LANES] = qk[0:1]
        misc[:, K_AT:K_AT + LANES] = qk[1:2]
        lane = lax.broadcasted_iota(jnp.int32, (1, LANES), 1)
        misc[:, SINK_AT:SINK_AT + LANES] = jnp.where(lane < N_Q_HEADS, attn_ref[0:1, 0:LANES], 0.0)
        misc[:, LOSS_AT:LOSS_AT + LANES] = sum(
            ple_ref[2:3, LANES * t:LANES * (t + 1)] for t in range(D_MODEL // LANES))
        for tap in range(3):
            row, at = _tap_at(tap)
            slab_ref[row:row + 1, at:at + ATTN_W] = attn_ref[1 + tap:2 + tap, :]
        gath_ref[me] = slab_ref[...]
        copies = []
        for k in range(1, N_DEV):
            peer = (x ^ (k >> 2), y ^ ((k >> 1) & 1), c ^ (k & 1))
            copies.append(pltpu.make_async_remote_copy(
                src_ref=slab_ref, dst_ref=gath_ref.at[me], send_sem=send_sems.at[k - 1],
                recv_sem=recv_sems.at[k - 1], device_id=peer, device_id_type=MESH))
        for cp in copies:
            cp.start()
        for cp in copies:
            cp.wait_recv()
        for cp in copies:
            cp.wait_send()
        total = gath_ref[0]
        for d in range(1, N_DEV):
            total = total + gath_ref[d]
        out_ref[...] = total

    vmem = pl.BlockSpec(memory_space=pltpu.VMEM)
    return pl.pallas_call(
        body, name="reduce_small",
        out_shape=jax.ShapeDtypeStruct((SLAB_ROWS, D_MODEL), F32),
        in_specs=[vmem] * 5, out_specs=vmem,
        scratch_shapes=[pltpu.VMEM((SLAB_ROWS, D_MODEL), F32), pltpu.VMEM((N_DEV, SLAB_ROWS, D_MODEL), F32),
                        pltpu.SemaphoreType.DMA((N_DEV - 1,)), pltpu.SemaphoreType.DMA((N_DEV - 1,))])(
            acc_g1, acc_g2, acc_ple, acc_qk, acc_attn)


def _pair_sum(g, r, place, tr, name):
    _, _, rows, cols = g.shape

    def body(place_ref, g_ref, r_ref, pb_ref, own_ref):
        tot = g_ref[0, 0].astype(F32) + r_ref[0].astype(F32)
        pb_ref[0] = tot.astype(BF16)

        @pl.when(pl.program_id(1) == place_ref[1])
        def _():
            own_ref[...] = tot

    grid_spec = pltpu.PrefetchScalarGridSpec(
        num_scalar_prefetch=1, grid=(rows // tr, 4),
        in_specs=[pl.BlockSpec((1, 1, tr, cols), lambda i, q, place_ref: (q, place_ref[0], i, 0)),
                  pl.BlockSpec((1, tr, cols), lambda i, q, place_ref: (q, i, 0))],
        out_specs=(pl.BlockSpec((1, tr, cols), lambda i, q, place_ref: (q, i, 0)),
                   pl.BlockSpec((tr, cols), lambda i, q, place_ref: (i, 0))))
    return pl.pallas_call(
        body, name=name, grid_spec=grid_spec,
        out_shape=(jax.ShapeDtypeStruct((4, rows, cols), BF16), jax.ShapeDtypeStruct((rows, cols), F32)),
        compiler_params=_params("arbitrary", "arbitrary"))(place, g, r)


HBM = pl.BlockSpec(memory_space=pltpu.HBM)
SEM = pl.BlockSpec(memory_space=pltpu.SEMAPHORE)
SIDE_EFFECT = pltpu.CompilerParams(has_side_effects=pltpu.SideEffectType.DATAFLOW_SIDE_EFFECTING)
TOKEN = jax.ShapeDtypeStruct((SUBLANES, LANES), F32)


def _hbm(a):
    return pltpu.with_memory_space_constraint(a, pltpu.HBM)


def _hbm_like(arrays):
    return tuple(pltpu.HBM(a.shape, a.dtype) for a in arrays)


def _block_of(px, py, pc):
    return 4 * px + 2 * py + pc


def _relay_parts(rows):
    if rows % (2 * PACKED_ROWS):
        return [pl.ds(0, rows), None]
    return [pl.ds(0, rows // 2), pl.ds(rows // 2, rows // 2)]


def _gather_start(shards, after, relay=False):
    na = len(shards)
    lands = [_hbm(lax.empty((N_DEV,) + a.shape, a.dtype)) for a in shards]

    def body(*refs):
        ins, land = refs[:na], refs[na:2 * na]
        send_sems, recv_ici, recv_d2d = refs[2 * na + 1:2 * na + 4]
        token = refs[-1]
        x, y, c = _place()
        peers = [(x, y, 1 - c), (1 - x, y, c), (x, 1 - y, c), (1 - x, 1 - y, c)]
        for k, peer in enumerate(peers[:3] if relay else peers):
            for t in range(na):
                pltpu.make_async_remote_copy(
                    src_ref=ins[t], dst_ref=land[t].at[_block_of(x, y, c)], send_sem=send_sems.at[4 * t + k],
                    recv_sem=recv_d2d.at[4 * t] if k == 0 else recv_ici.at[3 * t + k - 1],
                    device_id=peer, device_id_type=MESH).start()
        token[...] = jnp.zeros_like(token)

    out = pl.pallas_call(
        body, name="gather_start",
        out_shape=(pltpu.SemaphoreType.DMA((4 * na,)), pltpu.SemaphoreType.DMA((3 * na,)),
                   pltpu.SemaphoreType.DMA((4 * na,)), pltpu.SemaphoreType.DMA((2 * na,)), *_hbm_like(lands), TOKEN),
        in_specs=[ANY] * na + [HBM] * na + [ANY],
        out_specs=(SEM, SEM, SEM, SEM, *[HBM] * na, pl.BlockSpec(memory_space=pltpu.VMEM)),
        input_output_aliases={na + i: 4 + i for i in range(na)},
        compiler_params=SIDE_EFFECT)(*shards, *lands, after)
    send_sems, recv_ici, recv_d2d, recv_relay = out[:4]
    state = dict(send=send_sems, ici=recv_ici, d2d=recv_d2d, relay=recv_relay, relayed=relay, shards=list(shards),
                 lands=out[4:4 + na])
    return state, out[-1]


def _gather_forward(state, after):
    lands = state["lands"]
    na = len(lands)

    def body(*refs):
        land = refs[:na]
        recv_ici, recv_d2d = refs[na], refs[na + 1]
        fwd_sems, token = refs[-2], refs[-1]
        x, y, c = _place()
        for j, chip in enumerate([(1 - x, y), (x, 1 - y), (1 - x, 1 - y)]):
            for t in range(na):
                blk = land[t].at[_block_of(*chip, c)]
                pltpu.make_async_remote_copy(
                    src_ref=blk, dst_ref=blk, send_sem=fwd_sems.at[3 * t + j], recv_sem=recv_ici.at[3 * t + j],
                    device_id=(x, y, c), device_id_type=MESH).wait_recv()
                pltpu.make_async_remote_copy(
                    src_ref=blk, dst_ref=blk, send_sem=fwd_sems.at[3 * t + j], recv_sem=recv_d2d.at[4 * t + 1 + j],
                    device_id=(x, y, 1 - c), device_id_type=MESH).start()
        token[...] = jnp.zeros_like(token)

    out = pl.pallas_call(
        body, name="gather_forward",
        out_shape=(*_hbm_like(lands), pltpu.SemaphoreType.DMA((3 * na,)), TOKEN),
        in_specs=[HBM] * na + [SEM, SEM, ANY],
        out_specs=(*[HBM] * na, SEM, pl.BlockSpec(memory_space=pltpu.VMEM)),
        input_output_aliases={i: i for i in range(na)},
        compiler_params=SIDE_EFFECT)(*lands, state["ici"], state["d2d"], after)
    return dict(state, lands=out[:na], fwd=out[na]), out[-1]


def _gather_wait(state, after):
    shards, lands = state["shards"], state["lands"]
    na = len(lands)

    def body(*refs):
        ins, land = refs[:na], refs[na:2 * na]
        send_sems, fwd_sems, recv_d2d = refs[2 * na:2 * na + 3]
        x, y, c = _place()
        chips = [(1 - x, y), (x, 1 - y), (1 - x, 1 - y)]
        for t in range(na):
            mine = land[t].at[_block_of(x, y, c)]
            for k in range(4):
                pltpu.make_async_remote_copy(
                    src_ref=ins[t], dst_ref=mine, send_sem=send_sems.at[4 * t + k], recv_sem=recv_d2d.at[4 * t],
                    device_id=(x, y, c), device_id_type=MESH).wait_send()
            for j, chip in enumerate(chips):
                blk = land[t].at[_block_of(*chip, c)]
                pltpu.make_async_remote_copy(
                    src_ref=blk, dst_ref=blk, send_sem=fwd_sems.at[3 * t + j], recv_sem=recv_d2d.at[4 * t + 1 + j],
                    device_id=(x, y, c), device_id_type=MESH).wait_send()
            for k, blk_id in enumerate([_block_of(x, y, 1 - c)] + [_block_of(*chip, 1 - c) for chip in chips]):
                blk = land[t].at[blk_id]
                pltpu.make_async_remote_copy(
                    src_ref=blk, dst_ref=blk, send_sem=send_sems.at[4 * t], recv_sem=recv_d2d.at[4 * t + k],
                    device_id=(x, y, c), device_id_type=MESH).wait_recv()

    out = pl.pallas_call(
        body, name="gather_wait",
        out_shape=_hbm_like(lands),
        in_specs=[ANY] * na + [HBM] * na + [SEM, SEM, SEM, ANY],
        out_specs=tuple([HBM] * na),
        input_output_aliases={na + i: i for i in range(na)},
        compiler_params=SIDE_EFFECT)(*shards, *lands, state["send"], state["fwd"], state["d2d"], after)
    return out


def _gather_from_sibling(state, after):
    lands = state["lands"]
    na = len(lands)

    def body(*refs):
        land, recv_d2d = refs[:na], refs[na]
        x, y, c = _place()
        for t in range(na):
            blk = land[t].at[_block_of(x, y, 1 - c)]
            pltpu.make_async_remote_copy(src_ref=blk, dst_ref=blk, send_sem=recv_d2d.at[4 * t],
                                         recv_sem=recv_d2d.at[4 * t], device_id=(x, y, c),
                                         device_id_type=MESH).wait_recv()

    out = pl.pallas_call(
        body, name="gather_from_sibling", out_shape=_hbm_like(lands),
        in_specs=[HBM] * na + [SEM, ANY], out_specs=tuple([HBM] * na),
        input_output_aliases={i: i for i in range(na)},
        compiler_params=SIDE_EFFECT)(*lands, state["d2d"], after)
    return dict(state, lands=list(out))


def _gather_from_chip(state, j, afters, last):
    shards, lands, relayed = state["shards"], state["lands"], state["relayed"]
    na = len(lands)
    parts = [_relay_parts(a.shape[0]) for a in shards]

    def relay_on(land_ref, t, nb, fwd_sems, recv_relay):
        x, y, c = _place()
        blk = chip_blocks(land_ref, nb)[1].at[parts[t][nb]]
        return pltpu.make_async_remote_copy(
            src_ref=blk, dst_ref=blk, send_sem=fwd_sems.at[na + t], recv_sem=recv_relay.at[2 * t + nb],
            device_id=[(x, 1 - y, c), (1 - x, y, c)][nb], device_id_type=MESH)

    def chip_blocks(land_ref, which=j):
        x, y, c = _place()
        chip = [(1 - x, y), (x, 1 - y), (1 - x, 1 - y)][which]
        return (x, y, c), land_ref.at[_block_of(*chip, c)], land_ref.at[_block_of(*chip, 1 - c)]

    def forward(*refs):
        land, recv_ici, recv_d2d, recv_relay, fwd_sems = refs[:na], refs[na], refs[na + 1], refs[na + 2], refs[-1]
        for t in range(na):
            (x, y, c), mine, _ = chip_blocks(land[t])
            if relayed and j == 2:
                for half, rows in enumerate(parts[t]):
                    if rows is not None:
                        pltpu.make_async_remote_copy(
                            src_ref=mine.at[rows], dst_ref=mine.at[rows], send_sem=fwd_sems.at[t],
                            recv_sem=recv_relay.at[2 * t + half], device_id=(x, y, c),
                            device_id_type=MESH).wait_recv()
            else:
                pltpu.make_async_remote_copy(src_ref=mine, dst_ref=mine, send_sem=fwd_sems.at[t],
                                             recv_sem=recv_ici.at[3 * t + j], device_id=(x, y, c),
                                             device_id_type=MESH).wait_recv()
            pltpu.make_async_remote_copy(src_ref=mine, dst_ref=mine, send_sem=fwd_sems.at[t],
                                         recv_sem=recv_d2d.at[4 * t + 1 + j], device_id=(x, y, 1 - c),
                                         device_id_type=MESH).start()
            if relayed and j < 2 and parts[t][j] is not None:
                relay_on(land[t], t, j, fwd_sems, recv_relay).start()

    out = pl.pallas_call(
        forward, name="gather_pass_chip_" + str(j),
        out_shape=(*_hbm_like(lands), pltpu.SemaphoreType.DMA((2 * na,))),
        in_specs=[HBM] * na + [SEM, SEM, SEM] + [ANY] * len(afters), out_specs=(*[HBM] * na, SEM),
        input_output_aliases={i: i for i in range(na)},
        compiler_params=SIDE_EFFECT)(*lands, state["ici"], state["d2d"], state["relay"], *afters)
    passed_sems = out[na]
    relays = state.get("relays", []) + ([passed_sems] if relayed and j < 2 else [])
    waited = relays if last else []

    def arrive(*refs):
        land, fwd_sems, recv_d2d = refs[:na], refs[na], refs[na + 1]
        shard, send_sems, recv_relay = refs[na + 2:2 * na + 2], refs[2 * na + 2], refs[2 * na + 3]
        for nb, relay_sems in enumerate(refs[2 * na + 4:2 * na + 4 + len(waited)]):
            for t in range(na):
                if parts[t][nb] is not None:
                    relay_on(land[t], t, nb, relay_sems, recv_relay).wait_send()
        for t in range(na):
            (x, y, c), mine, theirs = chip_blocks(land[t])
            pltpu.make_async_remote_copy(src_ref=theirs, dst_ref=theirs, send_sem=fwd_sems.at[t],
                                         recv_sem=recv_d2d.at[4 * t + 1 + j], device_id=(x, y, c),
                                         device_id_type=MESH).wait_recv()
            pltpu.make_async_remote_copy(src_ref=mine, dst_ref=mine, send_sem=fwd_sems.at[t],
                                         recv_sem=recv_d2d.at[4 * t + 1 + j], device_id=(x, y, c),
                                         device_id_type=MESH).wait_send()
            for k in range((3 if relayed else 4) if last else 0):
                pltpu.make_async_remote_copy(
                    src_ref=shard[t], dst_ref=land[t].at[_block_of(x, y, c)], send_sem=send_sems.at[4 * t + k],
                    recv_sem=recv_d2d.at[4 * t], device_id=(x, y, c), device_id_type=MESH).wait_send()

    def take(state, afters):
        taken = pl.pallas_call(
            arrive, name="gather_take_chip_" + str(j), out_shape=_hbm_like(lands),
            in_specs=[HBM] * na + [SEM, SEM] + [ANY] * na + [SEM, SEM] + [SEM] * len(waited) + [ANY] * len(afters),
            out_specs=tuple([HBM] * na), input_output_aliases={i: i for i in range(na)},
            compiler_params=SIDE_EFFECT)(*state["lands"], passed_sems, state["d2d"], *shards, state["send"],
                                         state["relay"], *waited, *afters)
        return dict(state, lands=list(taken))

    return dict(state, lands=list(out[:na]), relays=relays), take


def _to_sibling(srcs, lands, send_sems, recv_sems):
    x, y, c = _place()
    return [pltpu.make_async_remote_copy(
        src_ref=srcs[t].at[:, 1 - c], dst_ref=lands[t], send_sem=send_sems.at[t], recv_sem=recv_sems.at[t],
        device_id=(x, y, 1 - c), device_id_type=MESH) for t in range(len(srcs))]


def _to_chips(srcs, lands, send_sems, recv_sems):
    x, y, c = _place()
    copies = []
    for k in (1, 2, 3):
        px, py = x ^ (k >> 1), y ^ (k & 1)
        copies += [pltpu.make_async_remote_copy(
            src_ref=srcs[t].at[2 * px + py], dst_ref=lands[t].at[k - 1], send_sem=send_sems.at[3 * t + k - 1],
            recv_sem=recv_sems.at[3 * t + k - 1], device_id=(px, py, c), device_id_type=MESH) for t in range(len(srcs))]
    return copies


def _exchange_start(name, srcs, land_shapes, copies, per_array, after):
    na = len(srcs)
    lands = [_hbm(lax.empty(shp, a.dtype)) for shp, a in zip(land_shapes, srcs)]

    def body(*refs):
        token = refs[-1]
        for cp in copies(refs[:na], refs[na:2 * na], refs[2 * na + 1], refs[2 * na + 2]):
            cp.start()
        token[...] = jnp.zeros_like(token)

    out = pl.pallas_call(
        body, name=name,
        out_shape=(pltpu.SemaphoreType.DMA((na * per_array,)), pltpu.SemaphoreType.DMA((na * per_array,)),
                   *_hbm_like(lands), TOKEN),
        in_specs=[ANY] * na + [HBM] * na + [ANY],
        out_specs=(SEM, SEM, *[HBM] * na, pl.BlockSpec(memory_space=pltpu.VMEM)),
        input_output_aliases={na + i: 2 + i for i in range(na)},
        compiler_params=SIDE_EFFECT)(*srcs, *lands, after)
    return dict(send=out[0], recv=out[1], srcs=list(srcs), lands=out[2:2 + na]), out[-1]


def _exchange_wait(name, state, copies, afters):
    srcs, lands = state["srcs"], state["lands"]
    na = len(srcs)

    def body(*refs):
        for cp in copies(refs[:na], refs[na:2 * na], refs[2 * na], refs[2 * na + 1]):
            cp.wait_send()
            cp.wait_recv()

    out = pl.pallas_call(
        body, name=name,
        out_shape=_hbm_like(lands),
        in_specs=[ANY] * na + [HBM] * na + [SEM, SEM] + [ANY] * len(afters),
        out_specs=tuple([HBM] * na),
        input_output_aliases={na + i: i for i in range(na)},
        compiler_params=SIDE_EFFECT)(*srcs, *lands, state["send"], state["recv"], *afters)
    return out


def _adamw_math(w, g, m, v):
    m = ADAM_B1 * m + (1.0 - ADAM_B1) * g
    v = ADAM_B2 * v + (1.0 - ADAM_B2) * (g * g)
    m_hat = m / (1.0 - ADAM_B1 ** ADAM_STEP)
    v_hat = v / (1.0 - ADAM_B2 ** ADAM_STEP)
    return -ADAM_LR * (m_hat / (jnp.sqrt(v_hat) + ADAM_EPS) + ADAM_WD * w), m, v


def _adamw(own, others, w, m, v, tr, name, after):
    rows, cols = w.shape
    blk = pl.BlockSpec((tr, cols), lambda i: (i, 0))

    def body(own_ref, oth_ref, w_ref, m_ref, v_ref, after_ref, g_ref, d_ref, nm_ref, nv_ref):
        g = own_ref[...]
        for k in range(3):
            g = g + oth_ref[k].astype(F32)
        g_ref[...] = g
        d_ref[...], nm_ref[...], nv_ref[...] = _adamw_math(w_ref[...], g, m_ref[...], v_ref[...])

    out = jax.ShapeDtypeStruct((rows, cols), F32)
    return pl.pallas_call(
        body, name=name, out_shape=(out, out, out, out), grid=(rows // tr,),
        in_specs=[blk, pl.BlockSpec((3, tr, cols), lambda i: (0, i, 0)), blk, blk, blk, ANY],
        out_specs=(blk, blk, blk, blk),
        compiler_params=_params("parallel"))(own, others, w, m, v, after)


def _adamw_small(red, me, params, moments1, moments2):
    n = len(params)

    def body(me_ref, red_ref, *refs):
        ws, ms, vs = refs[:n], refs[n:2 * n], refs[2 * n:3 * n]
        loss_ref = refs[3 * n]
        outs = refs[3 * n + 1:]
        loss_ref[...] = jnp.sum(red_ref[ROW_MISC:ROW_MISC + 1, LOSS_AT:LOSS_AT + LANES], axis=-1, keepdims=True)
        for t, (row, at) in enumerate(SMALL_AT):
            g = red_ref[row:row + 1, at:at + ws[t].shape[1]]
            d, nm, nv = _adamw_math(ws[t][...], g, ms[t][...], vs[t][...])
            for o, val in zip(outs[4 * t:4 * t + 4], (g, d, nm, nv)):
                o[...] = val
        for tap in range(ws[-1].shape[0]):
            row, at = _tap_at(tap)
            g = red_ref[row:row + 1, pl.ds(pl.multiple_of(at + me_ref[0, 0] * LANES, LANES), LANES)]
            d, nm, nv = _adamw_math(ws[-1][tap], g, ms[-1][tap], vs[-1][tap])
            for o, val in zip(outs[4 * (n - 1):], (g, d, nm, nv)):
                o[tap] = val

    vmem = pl.BlockSpec(memory_space=pltpu.VMEM)
    shapes = [jax.ShapeDtypeStruct(w.shape, F32) for w in params for _ in range(4)]
    out = pl.pallas_call(
        body, name="adamw_small", out_shape=(jax.ShapeDtypeStruct((1, 1), F32), *shapes),
        in_specs=[pl.BlockSpec(memory_space=pltpu.SMEM), vmem] + [vmem] * (3 * n),
        out_specs=tuple([vmem] * (1 + 4 * n)))(me, red, *params, *moments1, *moments2)
    return out[0], [list(out[1 + k::4]) for k in range(4)]


def _tables(s, gq, gk, conv_w):
    gq2 = jnp.tile(gq.reshape(1, HEAD), (1, 2))
    gk2 = jnp.tile(gk.reshape(1, HEAD), (1, 2))
    conv_wp = jnp.pad(conv_w, ((0, SUBLANES - conv_w.shape[0]), (0, 0)))
    return _rope_tables(s), gq2, gk2, conv_wp


def _pair_id(q):
    return jnp.array([q, 0], jnp.int32)


def _forward_in(x, g1, shards):
    s = x.shape[0]
    h = _prenorm(x, g1, min(512, s), x)
    z, w_pairs = lax.empty((s, IN_W), F32), lax.empty((N_PAIRS, PAIR_W, D_MODEL), BF16)
    for q in range(N_PAIRS):
        z, w_pairs = _fwd_in_pair(h, shards, z, w_pairs, _pair_id(q), min(512, s), "fwd_in_" + str(q),
                                  own=shards[0] if q == 0 else None)
    return h, z, w_pairs


def _forward_attn(z, rope, gq2, gk2, conv_wp, sinks):
    s = z.shape[0]
    qn, k2, v2 = _qk_prep(z, *rope, gq2, gk2, min(256, s), z)
    a, mix, mixt = _attn_fwd(qn, k2, v2, z, conv_wp, sinks, qn)
    return qn, k2, v2, a, mix, mixt


def _forward_out(x, p, target, mix, mixt, w_out, g2, w_pg, b_pg, w_pp, g3):
    s = x.shape[0]
    tm = min(512, s)
    x1, hn2, hn2t = _fwd_out(mix, w_out, x, g2, tm)
    dy, dgp, dt, pt, acc_ple = _ple(hn2, w_pg, b_pg, p, w_pp, g3, x1, target, min(256, s))
    dx1, dx1b, acc_g2 = _gate_bwd(dgp, w_pg, x1, dy, g2, tm)
    gw_out = _mm_grad(mixt, [dx1b], 512, "grad_w_out")
    gw_pg = _mm_grad(hn2t, [dgp], 512, "grad_w_ple_gate")
    gw_pp = _mm_grad(pt, [dt], 512, "grad_w_ple_proj")
    return dx1, dx1b, (gw_out, gw_pg, gw_pp), acc_ple, acc_g2


def _backward_attn(dmix, h, z, qn, k2, v2, a, rope, gq2, gk2, conv_wp, sinks, after):
    dq, dkc, dkp, dvc, dvp, dz, dzt, acc_attn = _attn_bwd(qn, k2, v2, a, z, dmix, conv_wp, sinks, after)
    dz, dzt, acc_qk = _qkv_bwd(z, dz, dzt, dq, dkc, dkp, dvc, dvp, *rope, gq2, gk2)
    return dz, _grad_w_in(dzt, h), acc_attn, acc_qk


def _local_step(x, p, target, g1, shards, gq, gk, sinks, conv_w, w_out, g2, w_pg, b_pg, w_pp, g3):
    rope, gq2, gk2, conv_wp = _tables(x.shape[0], gq, gk, conv_w)
    h, z, w_pairs = _forward_in(x, g1, shards)
    qn, k2, v2, a, mix, mixt = _forward_attn(z, rope, gq2, gk2, conv_wp, sinks)
    dx1, dx1b, (gw_out, gw_pg, gw_pp), acc_ple, acc_g2 = _forward_out(
        x, p, target, mix, mixt, w_out, g2, w_pg, b_pg, w_pp, g3)
    dmix = _mm_nt(dx1b, w_out, min(512, x.shape[0]), "out_bwd", dx1b)
    dz, gw_in, acc_attn, acc_qk = _backward_attn(dmix, h, z, qn, k2, v2, a, rope, gq2, gk2, conv_wp, sinks, dmix)
    grad_x, acc_g1 = _in_bwd(dz, w_pairs, x, dx1, g1, min(512, x.shape[0]), dx1)
    return grad_x, (gw_in, gw_out, gw_pg, gw_pp), (acc_g1, acc_g2, acc_ple, acc_qk, acc_attn)


def _by_owner(g):
    return g.reshape((4, 2) + g.shape[1:])


def kernel(x, p, norm_gain, w_in, q_norm_gain, k_norm_gain, attn_sinks, conv_w, w_out, ple_gate_norm_gain, w_ple_gate, b_ple_gate, w_ple_proj, ple_norm_gain, loss_target, m_norm_gain, m_w_in, m_q_norm_gain, m_k_norm_gain, m_attn_sinks, m_conv_w, m_w_out, m_ple_gate_norm_gain, m_w_ple_gate, m_b_ple_gate, m_w_ple_proj, m_ple_norm_gain, v_norm_gain, v_w_in, v_q_norm_gain, v_k_norm_gain, v_attn_sinks, v_conv_w, v_w_out, v_ple_gate_norm_gain, v_w_ple_gate, v_b_ple_gate, v_w_ple_proj, v_ple_norm_gain):
    me = 4 * lax.axis_index("x") + 2 * lax.axis_index("y") + lax.axis_index("c")
    place = jnp.stack([lax.axis_index("c"), 2 * lax.axis_index("x") + lax.axis_index("y")]).astype(jnp.int32)
    xs, ps, target = x[0], p[0, 0], loss_target[0]

    shard_in = w_in[0].T.astype(BF16)
    own_late = [w_out[0].astype(BF16), w_ple_gate[0].astype(BF16), w_ple_proj[0].astype(BF16)]
    with_own = lambda gathered, own: lax.dynamic_update_slice(gathered, own[None], (me,) + (0,) * own.ndim)
    early, started = _gather_start([shard_in, conv_w[0]], shard_in, relay=True)
    tm = min(512, xs.shape[0])
    h = _prenorm(xs, norm_gain, tm, started)

    z, w_pairs = lax.empty((xs.shape[0], IN_W), F32), lax.empty((N_PAIRS, PAIR_W, D_MODEL), BF16)
    early = _gather_from_sibling(early, h)
    pair_of = lambda flip: jnp.stack([place[1] ^ flip, place[0]])
    z, w_pairs = _fwd_in_pair(h, early["lands"][0], z, w_pairs, pair_of(0), tm, "fwd_in_own", own=shard_in)
    early, take = _gather_from_chip(early, 0, (z,), last=False)
    for j, flip in enumerate((2, 1, 3)):
        if j < 2:
            early, take_next = _gather_from_chip(early, j + 1, (z,), last=j == 1)
        if j == 2:
            late, started_late = _gather_start(own_late, z)
        early = take(early, (z,) if j < 2 else (z, started_late))
        z, w_pairs = _fwd_in_pair(h, early["lands"][0], z, w_pairs, pair_of(flip), tm, "fwd_in_chip_" + str(j))
        take = take_next
    conv_full = jnp.transpose(with_own(early["lands"][1], conv_w[0]), (1, 0, 2)).reshape(3, ATTN_W)
    rope, gq2, gk2, conv_wp = _tables(xs.shape[0], q_norm_gain[0], k_norm_gain[0], conv_full)
    qn, k2, v2 = _qk_prep(z, *rope, gq2, gk2, min(256, xs.shape[0]), z)
    late, forwarded = _gather_forward(late, qn)
    a, mix, mixt = _attn_fwd(qn, k2, v2, z, conv_wp, attn_sinks, forwarded)
    g_out, g_pg, g_pp = (with_own(g, own) for g, own in zip(_gather_wait(late, mix), own_late))
    w_out_f = g_out.reshape(D_MODEL, D_MODEL)
    w_pg_f = g_pg.reshape(D_MODEL, D_MODEL)
    w_pp_f = jnp.transpose(g_pp, (1, 0, 2)).reshape(PLE_DIM, D_MODEL)

    dx1, dx1b, (gw_out, gw_pg, gw_pp), acc_ple, acc_g2 = _forward_out(
        xs, ps, target, mix, mixt, w_out_f, ple_gate_norm_gain, w_pg_f, b_ple_gate, w_pp_f, ple_norm_gain)

    names = ("w_out", "w_ple_gate", "w_ple_proj")
    gw_pp_t = jnp.transpose(gw_pp.reshape(PLE_DIM, N_DEV, PLE_DIM), (1, 0, 2))
    grads = [_by_owner(gw_out.reshape(N_DEV, D_MODEL // N_DEV, D_MODEL)),
             _by_owner(gw_pg.reshape(N_DEV, D_MODEL // N_DEV, D_MODEL)), _by_owner(gw_pp_t)]
    pairs, paired = _exchange_start("pair_start", grads, [(4,) + g.shape[2:] for g in grads], _to_sibling, 1, dx1b)
    dmix = _mm_nt(dx1b, w_out_f, tm, "out_bwd", paired)
    from_sibling = _exchange_wait("pair_wait", pairs, _to_sibling, (dmix,))
    sums = [_pair_sum(g, r, place, 256, "pair_sum_" + nm) for g, r, nm in zip(pairs["srcs"], from_sibling, names)]
    chips, sent = _exchange_start("chip_start", [pb for pb, _ in sums], [(3,) + pb.shape[1:] for pb, _ in sums],
                                  _to_chips, 3, sums[-1][1])

    dz, gw_in, acc_attn, acc_qk = _backward_attn(
        dmix, h, z, qn, k2, v2, a, rope, gq2, gk2, conv_wp, attn_sinks, sent)

    gw_in_t = [_by_owner(gw_in)]
    pairs_in, paired_in = _exchange_start("pair_start_w_in", gw_in_t, [(4,) + gw_in_t[0].shape[2:]], _to_sibling, 1,
                                          gw_in)
    from_chips = _exchange_wait("chip_wait", chips, _to_chips, (gw_in,))
    big = {}
    for (_, own), oth, w, m, v, nm in zip(sums, from_chips, (w_out, w_ple_gate, w_ple_proj),
                                          (m_w_out, m_w_ple_gate, m_w_ple_proj),
                                          (v_w_out, v_w_ple_gate, v_w_ple_proj), names):
        big[nm] = [t[None] for t in _adamw(own, oth, w[0], m[0], v[0], 256, "adamw_" + nm, paired_in)]

    (from_sibling_in,) = _exchange_wait("pair_wait_w_in", pairs_in, _to_sibling, [big[nm][0] for nm in names])
    pb_in, own_in = _pair_sum(pairs_in["srcs"][0], from_sibling_in, place, SHARD_IN // 2, "pair_sum_w_in")
    chips_in, sent_in = _exchange_start("chip_start_w_in", [pb_in], [(3,) + pb_in.shape[1:]], _to_chips, 3, own_in)
    grad_x, acc_g1 = _in_bwd(dz, w_pairs, xs, dx1, norm_gain, tm, sent_in)
    (from_chips_in,) = _exchange_wait("chip_wait_w_in", chips_in, _to_chips, (grad_x,))
    big["w_in"] = [t.T[None] for t in _adamw(own_in, from_chips_in, w_in[0].T, m_w_in[0].T, v_w_in[0].T, SHARD_IN // 4,
                                             "adamw_w_in", grad_x)]

    red = _reduce_small(acc_g1, acc_g2, acc_ple, acc_qk, acc_attn)
    small = [norm_gain, ple_gate_norm_gain, b_ple_gate, ple_norm_gain, q_norm_gain, k_norm_gain, attn_sinks]
    small_m = [m_norm_gain, m_ple_gate_norm_gain, m_b_ple_gate, m_ple_norm_gain, m_q_norm_gain, m_k_norm_gain,
               m_attn_sinks]
    small_v = [v_norm_gain, v_ple_gate_norm_gain, v_b_ple_gate, v_ple_norm_gain, v_q_norm_gain, v_k_norm_gain,
               v_attn_sinks]
    taps_first = lambda t: jnp.transpose(t, (1, 0, 2))
    loss, kinds = _adamw_small(red, me.reshape(1, 1).astype(jnp.int32), small + [taps_first(conv_w)],
                               small_m + [taps_first(m_conv_w)], small_v + [taps_first(v_conv_w)])

    def order(k):
        sm = kinds[k]
        return [sm[0], big["w_in"][k], sm[4], sm[5], sm[6], taps_first(sm[7]), big["w_out"][k], sm[1],
                big["w_ple_gate"][k], sm[2], big["w_ple_proj"][k], sm[3]]

    return (loss[0, 0], grad_x[None], *order(0), *order(1), *order(2), *order(3))
```

```python
import jax
import jax.numpy as jnp
from jax import lax
from jax.experimental import pallas as pl
from jax.experimental.pallas import tpu as pltpu

F32, BF16 = jnp.float32, jnp.bfloat16

D_MODEL = 2048
PLE_DIM = 256
ATTN_W = 1024
HEAD = 64
N_Q_HEADS = 16
KV_W = 256
QKV_W = ATTN_W + 2 * KV_W
REST_W = 5 * 1024
IN_W = QKV_W + REST_W
GATE_A0, CONV_B0, CONV_C0, CONV_H0, GATE_C0 = (QKV_W + 1024 * t for t in range(5))
K2_W = 4 * 128
ROT = 16
ROPE_THETA = 500000.0
EPS = 1e-6
NEG_INF = -1e30
BLK = 128
LANES = 128
SUBLANES = 8
N_DEV = 8
SHARD_IN = IN_W // N_DEV
PAIR_W = 2 * SHARD_IN
N_PAIRS = IN_W // PAIR_W
SLAB_ROWS = 8
PACKED_ROWS = 16
SUB_ROWS = 128
V7X_VMEM_LIMIT = 52 * 1024 * 1024

ADAM_LR, ADAM_B1, ADAM_B2, ADAM_EPS, ADAM_WD, ADAM_STEP = 0.001, 0.9, 0.999, 1e-08, 0.01, 10
MESH = pl.DeviceIdType.MESH


def _params(*semantics):
    return pltpu.CompilerParams(dimension_semantics=semantics, vmem_limit_bytes=V7X_VMEM_LIMIT)


ANY = pl.BlockSpec(memory_space=pl.ANY)


def _resident(shape):
    return pl.BlockSpec(shape, lambda *_: (0,) * len(shape), pipeline_mode=pl.Buffered(1))


def _dot(a, b):
    return jnp.dot(a, b, preferred_element_type=F32)


def _dot_nt(a, b):
    return lax.dot_general(a, b, (((1,), (1,)), ((), ())), preferred_element_type=F32)


def _rms(xf):
    r = lax.rsqrt(jnp.mean(xf * xf, axis=-1, keepdims=True) + EPS)
    return xf * r, r


def _rms_bwd(dxn, xn, r):
    return r * (dxn - xn * jnp.mean(dxn * xn, axis=-1, keepdims=True))


def _sig(g):
    return jax.nn.sigmoid(g)


def _dsilu(g, sg):
    return sg * (1.0 + g * (1.0 - sg))


def _low_half(shape):
    return lax.broadcasted_iota(jnp.int32, shape, len(shape) - 1) < HEAD


def _half_sums(v):
    lo = _low_half(v.shape)
    s_lo = jnp.sum(jnp.where(lo, v, 0.0), axis=-1, keepdims=True)
    s_hi = jnp.sum(jnp.where(lo, 0.0, v), axis=-1, keepdims=True)
    return jnp.where(lo, s_lo, s_hi)


def _rope(v, a, bm, bp):
    return v * a + pltpu.roll(v, LANES - ROT // 2, 1) * bm + pltpu.roll(v, ROT // 2, 1) * bp


def _rope_t(dy, a, bm, bp):
    return dy * a + pltpu.roll(dy * bm, ROT // 2, 1) + pltpu.roll(dy * bp, LANES - ROT // 2, 1)


def _dup_halves(v):
    lo = _low_half(v.shape)
    a = jnp.where(lo, v, 0.0)
    b = jnp.where(lo, 0.0, v)
    return a + pltpu.roll(a, HEAD, 1), b + pltpu.roll(b, HEAD, 1)


def _rope_tables(s):
    half = ROT // 2
    lane = lax.broadcasted_iota(jnp.int32, (s, LANES), 1) % HEAD
    pos = lax.broadcasted_iota(jnp.int32, (half, s), 1).astype(F32)
    freq = lax.broadcasted_iota(jnp.int32, (half, s), 0).astype(F32)
    ang = pos * jnp.power(jnp.float32(ROPE_THETA), -freq * 2.0 / ROT)
    cos, sin = lax.optimization_barrier((jnp.cos(ang), jnp.sin(ang)))
    cos, sin = (jnp.tile(t.T, (1, LANES // half)) for t in (cos, sin))
    a = jnp.where(lane < ROT, cos, 1.0)
    bm = jnp.where(lane < half, -sin, 0.0)
    bp = jnp.where((lane >= half) & (lane < ROT), sin, 0.0)
    return a, bm, bp


def _prenorm(x, g1, tm, after):
    s = x.shape[0]

    def body(x_ref, g_ref, after_ref, h_ref):
        xn, _ = _rms(x_ref[...])
        h_ref[...] = (xn * g_ref[...]).astype(BF16)

    return pl.pallas_call(
        body, name="prenorm",
        out_shape=jax.ShapeDtypeStruct((s, D_MODEL), BF16),
        grid=(s // tm,),
        in_specs=[pl.BlockSpec((tm, D_MODEL), lambda i: (i, 0)), pl.BlockSpec((1, D_MODEL), lambda i: (0, 0)), ANY],
        out_specs=pl.BlockSpec((tm, D_MODEL), lambda i: (i, 0)),
        compiler_params=_params("parallel"))(x, g1, after)


def _fwd_in_pair(h, shards, z, w_pairs, pair, tm, name, own=None):
    s = h.shape[0]

    def body(pair_ref, h_ref, lo_ref, hi_ref, z_in, wp_in, z_ref, wp_ref):
        @pl.when(pl.program_id(0) == 0)
        def _():
            wp_ref[0, 0:SHARD_IN, :] = lo_ref[0]
            wp_ref[0, SHARD_IN:PAIR_W, :] = hi_ref[0]

        z_ref[...] = _dot_nt(h_ref[...], wp_ref[0])

    def body_own(pair_ref, h_ref, own_ref, other_ref, z_in, wp_in, z_ref, wp_ref):
        @pl.when(pl.program_id(0) == 0)
        def _():
            first = pl.multiple_of(pair_ref[1] * SHARD_IN, SHARD_IN)
            wp_ref[0, pl.ds(first, SHARD_IN), :] = own_ref[...]
            wp_ref[0, pl.ds(SHARD_IN - first, SHARD_IN), :] = other_ref[0]

        z_ref[...] = _dot_nt(h_ref[...], wp_ref[0])

    if own is None:
        blocks = [pl.BlockSpec((1, SHARD_IN, D_MODEL), lambda i, p: (2 * p[0], 0, 0)),
                  pl.BlockSpec((1, SHARD_IN, D_MODEL), lambda i, p: (2 * p[0] + 1, 0, 0))]
        operands = (shards, shards)
    else:
        blocks = [pl.BlockSpec((SHARD_IN, D_MODEL), lambda i, p: (0, 0)),
                  pl.BlockSpec((1, SHARD_IN, D_MODEL), lambda i, p: (2 * p[0] + 1 - p[1], 0, 0))]
        operands = (own, shards)
    grid_spec = pltpu.PrefetchScalarGridSpec(
        num_scalar_prefetch=1, grid=(s // tm,),
        in_specs=[pl.BlockSpec((tm, D_MODEL), lambda i, p: (i, 0)), *blocks, ANY, ANY],
        out_specs=(pl.BlockSpec((tm, PAIR_W), lambda i, p: (i, p[0])),
                   pl.BlockSpec((1, PAIR_W, D_MODEL), lambda i, p: (p[0], 0, 0))))
    return pl.pallas_call(
        body if own is None else body_own, name=name, grid_spec=grid_spec,
        out_shape=(jax.ShapeDtypeStruct(z.shape, z.dtype), jax.ShapeDtypeStruct(w_pairs.shape, w_pairs.dtype)),
        input_output_aliases={4: 0, 5: 1},
        compiler_params=_params("arbitrary"))(pair, h, *operands, z, w_pairs)


def _qk_prep(z, ra, rbm, rbp, gq2, gk2, tm, after):
    s = z.shape[0]

    def body(z_ref, a_ref, bm_ref, bp_ref, gq_ref, gk_ref, after_ref, q_ref, k2_ref, v2_ref):
        a, bm, bp = a_ref[...], bm_ref[...], bp_ref[...]
        for r in range(ATTN_W // LANES):
            x = z_ref[:, LANES * r:LANES * (r + 1)]
            rr = lax.rsqrt(_half_sums(x * x) * (1.0 / HEAD) + EPS)
            q_ref[:, LANES * r:LANES * (r + 1)] = _rope(x * rr * gq_ref[...], a, bm, bp).astype(BF16)
        for m in range(KV_W // LANES):
            x = z_ref[:, ATTN_W + LANES * m:ATTN_W + LANES * (m + 1)]
            rr = lax.rsqrt(_half_sums(x * x) * (1.0 / HEAD) + EPS)
            k_lo, k_hi = _dup_halves(_rope(x * rr * gk_ref[...], a, bm, bp))
            k2_ref[:, 2 * LANES * m:2 * LANES * m + LANES] = k_lo.astype(BF16)
            k2_ref[:, 2 * LANES * m + LANES:2 * LANES * (m + 1)] = k_hi.astype(BF16)
            v_lo, v_hi = _dup_halves(z_ref[:, ATTN_W + KV_W + LANES * m:ATTN_W + KV_W + LANES * (m + 1)])
            v2_ref[:, 2 * LANES * m:2 * LANES * m + LANES] = v_lo.astype(BF16)
            v2_ref[:, 2 * LANES * m + LANES:2 * LANES * (m + 1)] = v_hi.astype(BF16)

    row = lambda w: pl.BlockSpec((tm, w), lambda i: (i, 0))
    one = pl.BlockSpec((1, LANES), lambda i: (0, 0))
    return pl.pallas_call(
        body, name="qk_prep",
        out_shape=(jax.ShapeDtypeStruct((s, ATTN_W), BF16), jax.ShapeDtypeStruct((s, K2_W), BF16),
                   jax.ShapeDtypeStruct((s, K2_W), BF16)),
        grid=(s // tm,),
        in_specs=[row(PAIR_W), row(LANES), row(LANES), row(LANES), one, one, ANY],
        out_specs=(row(ATTN_W), row(K2_W), row(K2_W)),
        compiler_params=_params("parallel"))(z, ra, rbm, rbp, gq2, gk2, after)


GROUP = 4


def _window_mask(n):
    row = lax.broadcasted_iota(jnp.int32, (GROUP * BLK, 2 * BLK), 0) % BLK
    col = lax.broadcasted_iota(jnp.int32, (GROUP * BLK, 2 * BLK), 1)
    return (col > row) & (col <= row + BLK) & ((col >= BLK) | (n > 0))


def _stack_heads(pairs, zero):
    lo = _low_half(pairs[0].shape)
    parts = []
    for v in pairs:
        parts += [jnp.where(lo, v, zero), jnp.where(lo, zero, v)]
    return jnp.concatenate(parts, axis=0)


def _unstack_heads(v4):
    lo = _low_half((BLK, LANES))
    return [jnp.where(lo, v4[2 * i * BLK:(2 * i + 1) * BLK], v4[(2 * i + 1) * BLK:(2 * i + 2) * BLK]) for i in range(2)]


def _group_sinks(sink_ref, kvh):
    slot = lax.broadcasted_iota(jnp.int32, (GROUP * BLK, 1), 0) // BLK
    col = jnp.zeros((GROUP * BLK, 1), F32)
    for i in range(GROUP):
        col = jnp.where(slot == i, sink_ref[0, GROUP * kvh + i], col)
    return col, slot


def _head_probs(qm, kw, valid, sink):
    sc = jnp.where(valid, _dot_nt(qm, kw) * (HEAD ** -0.5), NEG_INF)
    mx = jnp.maximum(jnp.max(sc, axis=-1, keepdims=True), sink)
    ex = jnp.exp(sc - mx)
    den = jnp.sum(ex, axis=-1, keepdims=True) + jnp.exp(sink - mx)
    return ex / den, mx, den


def _cols(start, width=ATTN_W):
    return slice(start, start + width)


def _conv_fwd(z_ref, zp_ref, cw_ref, ext_ref, n):
    u = z_ref[:, _cols(CONV_C0)] * z_ref[:, _cols(CONV_H0)]
    pu = zp_ref[:, _cols(CONV_C0)] * zp_ref[:, _cols(CONV_H0)]
    ext_ref[0:SUBLANES, :] = jnp.where(n > 0, pu, 0.0)
    ext_ref[SUBLANES:SUBLANES + BLK, :] = u
    um1 = ext_ref[SUBLANES - 1:SUBLANES - 1 + BLK, :]
    um2 = ext_ref[SUBLANES - 2:SUBLANES - 2 + BLK, :]
    cv = cw_ref[0:1, :] * um2 + cw_ref[1:2, :] * um1 + cw_ref[2:3, :] * u
    return u, um1, um2, cv


def _prev_rows(n):
    return (jnp.maximum(n * (BLK // SUBLANES) - 1, 0), 0)


def _attn_fwd(qn, k2, v2, z, conv_wp, sinks, after):
    s = qn.shape[0]
    nb = s // BLK

    def body(sink_ref, q_ref, kc_ref, kp_ref, vc_ref, vp_ref, z_ref, zp_ref, cw_ref, after_ref, a_ref, mix_ref,
             mixt_ref, ext_ref):
        n = pl.program_id(0)
        valid = _window_mask(n)
        for kvh in range(K2_W // LANES):
            cols = slice(LANES * kvh, LANES * (kvh + 1))
            kw = jnp.concatenate([kp_ref[:, cols], kc_ref[:, cols]], axis=0)
            vw = jnp.concatenate([vp_ref[:, cols], vc_ref[:, cols]], axis=0)
            blocks = [slice(LANES * r, LANES * (r + 1)) for r in (2 * kvh, 2 * kvh + 1)]
            q4 = _stack_heads([q_ref[:, rc] for rc in blocks], jnp.zeros((BLK, LANES), BF16))
            p, _, _ = _head_probs(q4, kw, valid, _group_sinks(sink_ref, kvh)[0])
            for rc, a in zip(blocks, _unstack_heads(_dot(p.astype(BF16), vw))):
                a_ref[:, rc] = a
                g = z_ref[:, _cols(GATE_A0 + rc.start, LANES)]
                mix_ref[:, rc] = (a * (g * _sig(g))).astype(BF16)
        _, _, _, cv = _conv_fwd(z_ref, zp_ref, cw_ref, ext_ref, n)
        gc = z_ref[:, _cols(GATE_C0)]
        mix_ref[:, ATTN_W:D_MODEL] = (z_ref[:, _cols(CONV_B0)] * cv * (gc * _sig(gc))).astype(BF16)
        mixt_ref[...] = mix_ref[...].T

    cur = lambda w: pl.BlockSpec((BLK, w), lambda n: (n, 0))
    prev = lambda w: pl.BlockSpec((BLK, w), lambda n: (jnp.maximum(n - 1, 0), 0))
    return pl.pallas_call(
        body, name="attn_fwd",
        out_shape=(jax.ShapeDtypeStruct((s, ATTN_W), F32), jax.ShapeDtypeStruct((s, D_MODEL), BF16),
                   jax.ShapeDtypeStruct((D_MODEL, s), BF16)),
        grid=(nb,),
        in_specs=[pl.BlockSpec(memory_space=pltpu.SMEM),
                  cur(ATTN_W), cur(K2_W), prev(K2_W), cur(K2_W), prev(K2_W), cur(IN_W),
                  pl.BlockSpec((SUBLANES, IN_W), _prev_rows),
                  pl.BlockSpec((SUBLANES, ATTN_W), lambda n: (0, 0)), ANY],
        out_specs=(cur(ATTN_W), cur(D_MODEL), pl.BlockSpec((D_MODEL, BLK), lambda n: (0, n))),
        scratch_shapes=[pltpu.VMEM((BLK + 2 * SUBLANES, ATTN_W), F32)],
        compiler_params=_params("parallel"))(sinks, qn, k2, k2, v2, v2, z, z, conv_wp, after)


def _fwd_out(mix, w_out, x, g2, tm):
    s = x.shape[0]

    def body(m_ref, w_ref, x_ref, g_ref, x1_ref, h_ref, ht_ref):
        x1 = x_ref[...] + _dot(m_ref[...], w_ref[...])
        x1_ref[...] = x1
        xn, _ = _rms(x1)
        h = (xn * g_ref[...]).astype(BF16)
        h_ref[...] = h
        ht_ref[...] = h.T

    row = pl.BlockSpec((tm, D_MODEL), lambda i: (i, 0))
    return pl.pallas_call(
        body, name="fwd_out",
        out_shape=(jax.ShapeDtypeStruct((s, D_MODEL), F32), jax.ShapeDtypeStruct((s, D_MODEL), BF16),
                   jax.ShapeDtypeStruct((D_MODEL, s), BF16)),
        grid=(s // tm,),
        in_specs=[row, _resident((D_MODEL, D_MODEL)), row, pl.BlockSpec((1, D_MODEL), lambda i: (0, 0))],
        out_specs=(row, row, pl.BlockSpec((D_MODEL, tm), lambda i: (0, i))),
        compiler_params=_params("parallel"))(mix, w_out, x, g2)


def _ple(hn2, w_pg, b_pg, p, w_pp, g3, x1, target, tm):
    s = x1.shape[0]

    def body(h_ref, wg_ref, b_ref, p_ref, wp_ref, g3_ref, x1_ref, t_ref, dy_ref, dgp_ref, dt_ref, pt_ref, acc_ref):
        gate = _sig(_dot(h_ref[...], wg_ref[...]) + b_ref[...])
        pb = p_ref[...].astype(BF16)
        pt_ref[...] = pb.T
        t = _dot(pb, wp_ref[...])
        tn, r3 = _rms(t)
        e = tn * g3_ref[...]
        diff = x1_ref[...] + gate * e - t_ref[...]
        dy = diff * (1.0 / D_MODEL)
        dy_ref[...] = dy
        dgp = dy * e * (gate * (1.0 - gate))
        dgp_ref[...] = dgp.astype(BF16)
        de = dy * gate
        dt_ref[...] = _rms_bwd(de * g3_ref[...], tn, r3).astype(BF16)

        @pl.when(pl.program_id(0) == 0)
        def _():
            acc_ref[...] = jnp.zeros_like(acc_ref)

        acc_ref[0:1, :] += jnp.sum(dgp, axis=0, keepdims=True)
        acc_ref[1:2, :] += jnp.sum(de * tn, axis=0, keepdims=True)
        acc_ref[2:3, :] += jnp.sum(diff * diff, axis=0, keepdims=True) * (0.5 / D_MODEL)

    row = pl.BlockSpec((tm, D_MODEL), lambda i: (i, 0))
    vec = pl.BlockSpec((1, D_MODEL), lambda i: (0, 0))
    return pl.pallas_call(
        body, name="ple",
        out_shape=(jax.ShapeDtypeStruct((s, D_MODEL), F32), jax.ShapeDtypeStruct((s, D_MODEL), BF16),
                   jax.ShapeDtypeStruct((s, D_MODEL), BF16), jax.ShapeDtypeStruct((PLE_DIM, s), BF16),
                   jax.ShapeDtypeStruct((SUBLANES, D_MODEL), F32)),
        grid=(s // tm,),
        in_specs=[row, _resident((D_MODEL, D_MODEL)), vec, pl.BlockSpec((tm, PLE_DIM), lambda i: (i, 0)),
                  _resident((PLE_DIM, D_MODEL)), vec, row, row],
        out_specs=(row, row, row, pl.BlockSpec((PLE_DIM, tm), lambda i: (0, i)),
                   pl.BlockSpec((SUBLANES, D_MODEL), lambda i: (0, 0))),
        compiler_params=_params("arbitrary"))(hn2, w_pg, b_pg, p, w_pp, g3, x1, target)


def _gate_bwd(dgp, w_pg, x1, dy, g2, tm):
    s = x1.shape[0]

    def body(d_ref, w_ref, x1_ref, dy_ref, g_ref, dx_ref, dxb_ref, acc_ref):
        dh = _dot_nt(d_ref[...], w_ref[...])
        xn, r = _rms(x1_ref[...])
        dx1 = dy_ref[...] + _rms_bwd(dh * g_ref[...], xn, r)
        dx_ref[...] = dx1
        dxb_ref[...] = dx1.astype(BF16)

        @pl.when(pl.program_id(0) == 0)
        def _():
            acc_ref[...] = jnp.zeros_like(acc_ref)

        acc_ref[0:1, :] += jnp.sum(dh * xn, axis=0, keepdims=True)

    row = pl.BlockSpec((tm, D_MODEL), lambda i: (i, 0))
    return pl.pallas_call(
        body, name="gate_bwd",
        out_shape=(jax.ShapeDtypeStruct((s, D_MODEL), F32), jax.ShapeDtypeStruct((s, D_MODEL), BF16),
                   jax.ShapeDtypeStruct((SUBLANES, D_MODEL), F32)),
        grid=(s // tm,),
        in_specs=[row, _resident((D_MODEL, D_MODEL)), row, row, pl.BlockSpec((1, D_MODEL), lambda i: (0, 0))],
        out_specs=(row, row, pl.BlockSpec((SUBLANES, D_MODEL), lambda i: (0, 0))),
        compiler_params=_params("arbitrary"))(dgp, w_pg, x1, dy, g2)


def _mm_nt(a, b, tm, name, after):
    m, k = a.shape
    n = b.shape[0]

    def body(a_ref, b_ref, after_ref, o_ref):
        o_ref[...] = _dot_nt(a_ref[...], b_ref[...])

    return pl.pallas_call(
        body, name=name,
        out_shape=jax.ShapeDtypeStruct((m, n), F32),
        grid=(m // tm,),
        in_specs=[pl.BlockSpec((tm, k), lambda i: (i, 0)), _resident((n, k)), ANY],
        out_specs=pl.BlockSpec((tm, n), lambda i: (i, 0)),
        compiler_params=_params("parallel"))(a, b, after)


def _attn_bwd(qn, k2, v2, a, z, dmix, conv_wp, sinks, after):
    s = qn.shape[0]
    nb = s // BLK

    def body(sink_ref, q_ref, kc_ref, kp_ref, vc_ref, vp_ref, a_ref, z_ref, zp_ref, zn_ref, dm_ref, dmn_ref,
             cw_ref, after_ref, dq_ref, dkc_ref, dkp_ref, dvc_ref, dvp_ref, dz_ref, dzt_ref, acc_ref, ext_ref):
        n = pl.program_id(0)
        valid = _window_mask(n)
        lane = lax.broadcasted_iota(jnp.int32, (1, ATTN_W), 1)

        @pl.when(n == 0)
        def _():
            acc_ref[...] = jnp.zeros_like(acc_ref)

        dz_ref[:, 0:QKV_W] = jnp.zeros((BLK, QKV_W), BF16)
        dsink = jnp.zeros((1, ATTN_W), F32)
        for kvh in range(K2_W // LANES):
            cols = slice(LANES * kvh, LANES * (kvh + 1))
            kw = jnp.concatenate([kp_ref[:, cols], kc_ref[:, cols]], axis=0)
            vw = jnp.concatenate([vp_ref[:, cols], vc_ref[:, cols]], axis=0)
            blocks = [slice(LANES * r, LANES * (r + 1)) for r in (2 * kvh, 2 * kvh + 1)]
            das, avs = [], []
            for rc in blocks:
                g = z_ref[:, _cols(GATE_A0 + rc.start, LANES)]
                sg = _sig(g)
                dm = dm_ref[:, rc]
                av = a_ref[:, rc]
                das.append(dm * (g * sg))
                avs += [av, av]
                dz_ref[:, _cols(GATE_A0 + rc.start, LANES)] = (dm * av * _dsilu(g, sg)).astype(BF16)
            q4 = _stack_heads([q_ref[:, rc] for rc in blocks], jnp.zeros((BLK, LANES), BF16))
            sink, slot = _group_sinks(sink_ref, kvh)
            p, mx, den = _head_probs(q4, kw, valid, sink)
            do4 = _stack_heads(das, 0.0)
            delta = jnp.sum(do4 * jnp.concatenate(avs, axis=0), axis=-1, keepdims=True)
            dob = do4.astype(BF16)
            ds = p * (_dot_nt(dob, vw) - delta) * (HEAD ** -0.5)
            for rc, dq in zip(blocks, _unstack_heads(_dot(ds.astype(BF16), kw))):
                dq_ref[:, rc] = dq
            dk2 = _dot(ds.T.astype(BF16), q4)
            dv2 = _dot(p.T.astype(BF16), dob)
            dkp_ref[:, cols] = dk2[0:BLK]
            dkc_ref[:, cols] = dk2[BLK:2 * BLK]
            dvp_ref[:, cols] = dv2[0:BLK]
            dvc_ref[:, cols] = dv2[BLK:2 * BLK]
            dsk = jnp.exp(sink - mx) / den * delta
            for i in range(GROUP):
                dsink = dsink - jnp.where(lane == GROUP * kvh + i,
                                          jnp.sum(jnp.where(slot == i, dsk, 0.0), axis=0, keepdims=True), 0.0)
        acc_ref[0:1, :] += dsink

        u, um1, um2, cv = _conv_fwd(z_ref, zp_ref, cw_ref, ext_ref, n)
        cb = z_ref[:, _cols(CONV_B0)]
        gc = z_ref[:, _cols(GATE_C0)]
        sgc = _sig(gc)
        dmc = dm_ref[:, ATTN_W:D_MODEL]
        t = dmc * (gc * sgc)
        dcv = t * cb
        dz_ref[:, _cols(CONV_B0)] = (t * cv).astype(BF16)
        dz_ref[:, _cols(GATE_C0)] = (dmc * cb * cv * _dsilu(gc, sgc)).astype(BF16)
        gcn = zn_ref[:, _cols(GATE_C0)]
        dcvn = dmn_ref[:, ATTN_W:D_MODEL] * (gcn * _sig(gcn)) * zn_ref[:, _cols(CONV_B0)]
        ext_ref[0:BLK, :] = dcv
        ext_ref[BLK:BLK + SUBLANES, :] = jnp.where(n < nb - 1, dcvn, 0.0)
        du = (cw_ref[2:3, :] * dcv + cw_ref[1:2, :] * ext_ref[1:1 + BLK, :]
              + cw_ref[0:1, :] * ext_ref[2:2 + BLK, :])
        dz_ref[:, _cols(CONV_C0)] = (du * z_ref[:, _cols(CONV_H0)]).astype(BF16)
        dz_ref[:, _cols(CONV_H0)] = (du * z_ref[:, _cols(CONV_C0)]).astype(BF16)
        acc_ref[1:2, :] += jnp.sum(dcv * um2, axis=0, keepdims=True)
        acc_ref[2:3, :] += jnp.sum(dcv * um1, axis=0, keepdims=True)
        acc_ref[3:4, :] += jnp.sum(dcv * u, axis=0, keepdims=True)
        dzt_ref[...] = dz_ref[...].T

    cur = lambda w: pl.BlockSpec((BLK, w), lambda n: (n, 0))
    prev = lambda w: pl.BlockSpec((BLK, w), lambda n: (jnp.maximum(n - 1, 0), 0))
    nxt = lambda w: pl.BlockSpec(
        (SUBLANES, w), lambda n: (jnp.minimum((n + 1) * (BLK // SUBLANES), nb * (BLK // SUBLANES) - 1), 0))
    f32 = lambda w: jax.ShapeDtypeStruct((s, w), F32)
    return pl.pallas_call(
        body, name="attn_bwd",
        out_shape=(f32(ATTN_W), f32(K2_W), f32(K2_W), f32(K2_W), f32(K2_W),
                   jax.ShapeDtypeStruct((s, IN_W), BF16), jax.ShapeDtypeStruct((IN_W, s), BF16),
                   jax.ShapeDtypeStruct((SUBLANES, ATTN_W), F32)),
        grid=(nb,),
        in_specs=[pl.BlockSpec(memory_space=pltpu.SMEM),
                  cur(ATTN_W), cur(K2_W), prev(K2_W), cur(K2_W), prev(K2_W), cur(ATTN_W), cur(IN_W),
                  pl.BlockSpec((SUBLANES, IN_W), _prev_rows), nxt(IN_W), cur(D_MODEL), nxt(D_MODEL),
                  pl.BlockSpec((SUBLANES, ATTN_W), lambda n: (0, 0)), ANY],
        out_specs=(cur(ATTN_W), cur(K2_W), cur(K2_W), cur(K2_W), cur(K2_W), cur(IN_W),
                   pl.BlockSpec((IN_W, BLK), lambda n: (0, n)), pl.BlockSpec((SUBLANES, ATTN_W), lambda n: (0, 0))),
        scratch_shapes=[pltpu.VMEM((BLK + 2 * SUBLANES, ATTN_W), F32)],
        compiler_params=_params("arbitrary"))(sinks, qn, k2, k2, v2, v2, a, z, z, z, dmix, dmix, conv_wp, after)


def _qkv_bwd(z, dz, dzt, dq, dkc, dkp, dvc, dvp, ra, rbm, rbp, gq2, gk2):
    s = z.shape[0]
    nb = s // BLK

    def body(z_ref, dz_in, dzt_in, dq_ref, dkc_ref, dkp_ref, dvc_ref, dvp_ref, a_ref, bm_ref, bp_ref, gq_ref, gk_ref,
             dz_ref, dzt_ref, acc_ref):
        n = pl.program_id(0)
        a, bm, bp = a_ref[...], bm_ref[...], bp_ref[...]
        lo = _low_half((BLK, LANES))
        last = n == nb - 1

        @pl.when(n == 0)
        def _():
            acc_ref[...] = jnp.zeros_like(acc_ref)

        def norm_bwd(x, dy, gain):
            rr = lax.rsqrt(_half_sums(x * x) * (1.0 / HEAD) + EPS)
            xh = x * rr
            dxg = _rope_t(dy, a, bm, bp)
            dxh = dxg * gain
            dx = rr * (dxh - xh * (_half_sums(dxh * xh) * (1.0 / HEAD)))
            return dx, jnp.sum(dxg * xh, axis=0, keepdims=True)

        def folded(cur_ref, prev_ref, m):
            parts = []
            for h in (2 * m, 2 * m + 1):
                v = cur_ref[:, LANES * h:LANES * (h + 1)] + jnp.where(
                    last, 0.0, prev_ref[:, LANES * h:LANES * (h + 1)])
                parts.append(v + pltpu.roll(v, HEAD, 1))
            return jnp.where(lo, parts[0], parts[1])

        gq_acc = jnp.zeros((1, LANES), F32)
        for r in range(ATTN_W // LANES):
            rc = slice(LANES * r, LANES * (r + 1))
            dx, gg = norm_bwd(z_ref[:, rc], dq_ref[:, rc], gq_ref[...])
            dz_ref[:, rc] = dx.astype(BF16)
            gq_acc = gq_acc + gg
        acc_ref[0:1, :] += gq_acc
        gk_acc = jnp.zeros((1, LANES), F32)
        for m in range(KV_W // LANES):
            kc = slice(ATTN_W + LANES * m, ATTN_W + LANES * (m + 1))
            dx, gg = norm_bwd(z_ref[:, kc], folded(dkc_ref, dkp_ref, m), gk_ref[...])
            dz_ref[:, kc] = dx.astype(BF16)
            gk_acc = gk_acc + gg
            vc = slice(ATTN_W + KV_W + LANES * m, ATTN_W + KV_W + LANES * (m + 1))
            dz_ref[:, vc] = folded(dvc_ref, dvp_ref, m).astype(BF16)
        acc_ref[1:2, :] += gk_acc
        dzt_ref[...] = dz_ref[...].T

    cur = lambda w: pl.BlockSpec((BLK, w), lambda n: (n, 0))
    nxt = lambda w: pl.BlockSpec((BLK, w), lambda n: (jnp.minimum(n + 1, nb - 1), 0))
    one = pl.BlockSpec((1, LANES), lambda n: (0, 0))
    return pl.pallas_call(
        body, name="qkv_bwd",
        out_shape=(jax.ShapeDtypeStruct(dz.shape, dz.dtype), jax.ShapeDtypeStruct(dzt.shape, dzt.dtype),
                   jax.ShapeDtypeStruct((SUBLANES, LANES), F32)),
        grid=(nb,),
        in_specs=[cur(PAIR_W), ANY, ANY, cur(ATTN_W), cur(K2_W), nxt(K2_W), cur(K2_W), nxt(K2_W),
                  cur(LANES), cur(LANES), cur(LANES), one, one],
        out_specs=(cur(QKV_W), pl.BlockSpec((QKV_W, BLK), lambda n: (0, n)),
                   pl.BlockSpec((SUBLANES, LANES), lambda n: (0, 0))),
        input_output_aliases={1: 0, 2: 1},
        compiler_params=_params("arbitrary"))(z, dz, dzt, dq, dkc, dkp, dvc, dvp, ra, rbm, rbp, gq2, gk2)


def _in_bwd(dz, w_pairs, x, dx1, g1, tm, after):
    s = x.shape[0]
    n = s // tm
    sub = tm // N_PAIRS
    stripes = 4

    def body(d_ref, w_ref, x_ref, dx1_ref, g_ref, after_ref, gx_ref, acc_ref, dh_ref):
        i, k = pl.program_id(0), pl.program_id(1)

        def matmul(c):
            cols = slice(c * (D_MODEL // stripes), (c + 1) * (D_MODEL // stripes))
            dh_ref[i % 2, :, cols] += _dot(d_ref[...], w_ref[0, :, cols])

        def norm_bwd(c):
            part = sub // stripes
            mine = slice(c * part, (c + 1) * part)
            rows = pl.ds(pl.multiple_of(k * sub + c * part, part), part)
            dh = dh_ref[(i + 1) % 2, rows, :]
            dh_ref[(i + 1) % 2, rows, :] = jnp.zeros_like(dh)
            xn, r = _rms(x_ref[mine, :])
            gx_ref[rows, :] = dx1_ref[mine, :] + _rms_bwd(dh * g_ref[...], xn, r)
            acc_ref[0:1, :] += jnp.sum(dh * xn, axis=0, keepdims=True)

        @pl.when((i == 0) & (k == 0))
        def _():
            acc_ref[...] = jnp.zeros_like(acc_ref)
            dh_ref[...] = jnp.zeros_like(dh_ref)

        @pl.when(i == 0)
        def _():
            for c in range(stripes):
                matmul(c)

        @pl.when((i > 0) & (i < n))
        def _():
            for c in range(stripes):
                matmul(c)
                norm_bwd(c)

        @pl.when(i == n)
        def _():
            for c in range(stripes):
                norm_bwd(c)

    last = lambda i, k: jnp.where(i == n, N_PAIRS - 1, k)
    rows_before = lambda i, k: (jnp.maximum(i - 1, 0) * N_PAIRS + k, 0)
    return pl.pallas_call(
        body, name="in_bwd",
        out_shape=(jax.ShapeDtypeStruct((s, D_MODEL), F32), jax.ShapeDtypeStruct((SUBLANES, D_MODEL), F32)),
        grid=(n + 1, N_PAIRS),
        in_specs=[pl.BlockSpec((tm, PAIR_W), lambda i, k: (jnp.minimum(i, n - 1), last(i, k))),
                  pl.BlockSpec((1, PAIR_W, D_MODEL), lambda i, k: (last(i, k), 0, 0)),
                  pl.BlockSpec((sub, D_MODEL), rows_before), pl.BlockSpec((sub, D_MODEL), rows_before),
                  pl.BlockSpec((1, D_MODEL), lambda i, k: (0, 0)), ANY],
        out_specs=(pl.BlockSpec((tm, D_MODEL), lambda i, k: (jnp.maximum(i - 1, 0), 0)),
                   pl.BlockSpec((SUBLANES, D_MODEL), lambda i, k: (0, 0))),
        scratch_shapes=[pltpu.VMEM((2, tm, D_MODEL), F32)],
        compiler_params=_params("arbitrary", "arbitrary"))(dz, w_pairs, x, dx1, g1, after)


def _mm_grad(at, bs, tn, name):
    m, kdim = at.shape
    nblk = [b.shape[1] // tn for b in bs]
    starts = [sum(nblk[:t]) for t in range(len(bs))]

    def body(a_ref, *refs):
        b_refs, o_ref = refs[:len(bs)], refs[len(bs)]
        j = pl.program_id(0)
        for t, b_ref in enumerate(b_refs):
            @pl.when((j >= starts[t]) & (j < starts[t] + nblk[t]))
            def _():
                o_ref[...] = _dot(a_ref[...], b_ref[...]).astype(BF16)

    def b_spec(t):
        return pl.BlockSpec((kdim, tn), lambda j: (0, jnp.clip(j - starts[t], 0, nblk[t] - 1)))

    return pl.pallas_call(
        body, name=name,
        out_shape=jax.ShapeDtypeStruct((m, sum(nblk) * tn), BF16),
        grid=(sum(nblk),),
        in_specs=[_resident((m, kdim))] + [b_spec(t) for t in range(len(bs))],
        out_specs=pl.BlockSpec((m, tn), lambda j: (0, j)),
        compiler_params=_params("parallel"))(at, *bs)


def _grad_w_in(dzt, h):
    kdim = h.shape[0]

    def body(d_ref, h_ref, o_ref):
        o_ref[0] = _dot(d_ref[...], h_ref[...]).astype(BF16)

    return pl.pallas_call(
        body, name="grad_w_in",
        out_shape=jax.ShapeDtypeStruct((N_DEV, SHARD_IN, D_MODEL), BF16),
        grid=(N_DEV,),
        in_specs=[pl.BlockSpec((SHARD_IN, kdim), lambda j: (j, 0)), _resident((kdim, D_MODEL))],
        out_specs=pl.BlockSpec((1, SHARD_IN, D_MODEL), lambda j: (j, 0, 0)),
        compiler_params=_params("parallel"))(dzt, h)


def _place():
    return lax.axis_index("x"), lax.axis_index("y"), lax.axis_index("c")


ROW_TAPS, ROW_MISC = 4, 5
Q_AT, K_AT, SINK_AT, LOSS_AT = (ATTN_W + LANES * t for t in range(4))
SMALL_AT = [(0, 0), (1, 0), (2, 0), (3, 0), (ROW_MISC, Q_AT), (ROW_MISC, K_AT), (ROW_MISC, SINK_AT)]


def _tap_at(tap):
    return ROW_TAPS + tap // 2, ATTN_W * (tap % 2)


def _reduce_small(acc_g1, acc_g2, acc_ple, acc_qk, acc_attn):
    def body(g1_ref, g2_ref, ple_ref, qk_ref, attn_ref, out_ref, slab_ref, gath_ref, send_sems, recv_sems):
        x, y, c = _place()
        me = 4 * x + 2 * y + c
        slab_ref[...] = jnp.zeros_like(slab_ref)
        slab_ref[0:1, :] = g1_ref[0:1, :]
        slab_ref[1:2, :] = g2_ref[0:1, :]
        slab_ref[2:4, :] = ple_ref[0:2, :]
        qk = qk_ref[0:2, :]
        qk = jnp.where(_low_half(qk.shape), qk + pltpu.roll(qk, HEAD, 1), 0.0)
        misc = slab_ref.at[ROW_MISC:ROW_MISC + 1]
        misc[:, Q_AT:Q_AT + LANES] = qk[0:1]
        misc[:, K_AT:K_AT + LANES] = qk[1:2]
        lane = lax.broadcasted_iota(jnp.int32, (1, LANES), 1)
        misc[:, SINK_AT:SINK_AT + LANES] = jnp.where(lane < N_Q_HEADS, attn_ref[0:1, 0:LANES], 0.0)
        misc[:, LOSS_AT:LOSS_AT + LANES] = sum(
            ple_ref[2:3, LANES * t:LANES * (t + 1)] for t in range(D_MODEL // LANES))
        for tap in range(3):
            row, at = _tap_at(tap)
            slab_ref[row:row + 1, at:at + ATTN_W] = attn_ref[1 + tap:2 + tap, :]
        gath_ref[me] = slab_ref[...]
        copies = []
        for k in range(1, N_DEV):
            peer = (x ^ (k >> 2), y ^ ((k >> 1) & 1), c ^ (k & 1))
            copies.append(pltpu.make_async_remote_copy(
                src_ref=slab_ref, dst_ref=gath_ref.at[me], send_sem=send_sems.at[k - 1],
                recv_sem=recv_sems.at[k - 1], device_id=peer, device_id_type=MESH))
        for cp in copies:
            cp.start()
        for cp in copies:
            cp.wait_recv()
        for cp in copies:
            cp.wait_send()
        total = gath_ref[0]
        for d in range(1, N_DEV):
            total = total + gath_ref[d]
        out_ref[...] = total

    vmem = pl.BlockSpec(memory_space=pltpu.VMEM)
    return pl.pallas_call(
        body, name="reduce_small",
        out_shape=jax.ShapeDtypeStruct((SLAB_ROWS, D_MODEL), F32),
        in_specs=[vmem] * 5, out_specs=vmem,
        scratch_shapes=[pltpu.VMEM((SLAB_ROWS, D_MODEL), F32), pltpu.VMEM((N_DEV, SLAB_ROWS, D_MODEL), F32),
                        pltpu.SemaphoreType.DMA((N_DEV - 1,)), pltpu.SemaphoreType.DMA((N_DEV - 1,))])(
            acc_g1, acc_g2, acc_ple, acc_qk, acc_attn)


def _pair_sum(g, r, place, tr, name):
    _, _, rows, cols = g.shape

    def body(place_ref, g_ref, r_ref, pb_ref, own_ref):
        tot = g_ref[0, 0].astype(F32) + r_ref[0].astype(F32)
        pb_ref[0] = tot.astype(BF16)

        @pl.when(pl.program_id(1) == place_ref[1])
        def _():
            own_ref[...] = tot

    grid_spec = pltpu.PrefetchScalarGridSpec(
        num_scalar_prefetch=1, grid=(rows // tr, 4),
        in_specs=[pl.BlockSpec((1, 1, tr, cols), lambda i, q, place_ref: (q, place_ref[0], i, 0)),
                  pl.BlockSpec((1, tr, cols), lambda i, q, place_ref: (q, i, 0))],
        out_specs=(pl.BlockSpec((1, tr, cols), lambda i, q, place_ref: (q, i, 0)),
                   pl.BlockSpec((tr, cols), lambda i, q, place_ref: (i, 0))))
    return pl.pallas_call(
        body, name=name, grid_spec=grid_spec,
        out_shape=(jax.ShapeDtypeStruct((4, rows, cols), BF16), jax.ShapeDtypeStruct((rows, cols), F32)),
        compiler_params=_params("arbitrary", "arbitrary"))(place, g, r)


HBM = pl.BlockSpec(memory_space=pltpu.HBM)
SEM = pl.BlockSpec(memory_space=pltpu.SEMAPHORE)
SIDE_EFFECT = pltpu.CompilerParams(has_side_effects=pltpu.SideEffectType.DATAFLOW_SIDE_EFFECTING)
TOKEN = jax.ShapeDtypeStruct((SUBLANES, LANES), F32)


def _hbm(a):
    return pltpu.with_memory_space_constraint(a, pltpu.HBM)


def _hbm_like(arrays):
    return tuple(pltpu.HBM(a.shape, a.dtype) for a in arrays)


def _block_of(px, py, pc):
    return 4 * px + 2 * py + pc


def _relay_parts(rows):
    if rows % (2 * PACKED_ROWS):
        return [pl.ds(0, rows), None]
    return [pl.ds(0, rows // 2), pl.ds(rows // 2, rows // 2)]


def _gather_start(shards, after, relay=False):
    na = len(shards)
    lands = [_hbm(lax.empty((N_DEV,) + a.shape, a.dtype)) for a in shards]

    def body(*refs):
        ins, land = refs[:na], refs[na:2 * na]
        send_sems, recv_ici, recv_d2d = refs[2 * na + 1:2 * na + 4]
        token = refs[-1]
        x, y, c = _place()
        peers = [(x, y, 1 - c), (1 - x, y, c), (x, 1 - y, c), (1 - x, 1 - y, c)]
        for k, peer in enumerate(peers[:3] if relay else peers):
            for t in range(na):
                pltpu.make_async_remote_copy(
                    src_ref=ins[t], dst_ref=land[t].at[_block_of(x, y, c)], send_sem=send_sems.at[4 * t + k],
                    recv_sem=recv_d2d.at[4 * t] if k == 0 else recv_ici.at[3 * t + k - 1],
                    device_id=peer, device_id_type=MESH).start()
        token[...] = jnp.zeros_like(token)

    out = pl.pallas_call(
        body, name="gather_start",
        out_shape=(pltpu.SemaphoreType.DMA((4 * na,)), pltpu.SemaphoreType.DMA((3 * na,)),
                   pltpu.SemaphoreType.DMA((4 * na,)), pltpu.SemaphoreType.DMA((2 * na,)), *_hbm_like(lands), TOKEN),
        in_specs=[ANY] * na + [HBM] * na + [ANY],
        out_specs=(SEM, SEM, SEM, SEM, *[HBM] * na, pl.BlockSpec(memory_space=pltpu.VMEM)),
        input_output_aliases={na + i: 4 + i for i in range(na)},
        compiler_params=SIDE_EFFECT)(*shards, *lands, after)
    send_sems, recv_ici, recv_d2d, recv_relay = out[:4]
    state = dict(send=send_sems, ici=recv_ici, d2d=recv_d2d, relay=recv_relay, relayed=relay, shards=list(shards),
                 lands=out[4:4 + na])
    return state, out[-1]


def _gather_forward(state, after):
    lands = state["lands"]
    na = len(lands)

    def body(*refs):
        land = refs[:na]
        recv_ici, recv_d2d = refs[na], refs[na + 1]
        fwd_sems, token = refs[-2], refs[-1]
        x, y, c = _place()
        for j, chip in enumerate([(1 - x, y), (x, 1 - y), (1 - x, 1 - y)]):
            for t in range(na):
                blk = land[t].at[_block_of(*chip, c)]
                pltpu.make_async_remote_copy(
                    src_ref=blk, dst_ref=blk, send_sem=fwd_sems.at[3 * t + j], recv_sem=recv_ici.at[3 * t + j],
                    device_id=(x, y, c), device_id_type=MESH).wait_recv()
                pltpu.make_async_remote_copy(
                    src_ref=blk, dst_ref=blk, send_sem=fwd_sems.at[3 * t + j], recv_sem=recv_d2d.at[4 * t + 1 + j],
                    device_id=(x, y, 1 - c), device_id_type=MESH).start()
        token[...] = jnp.zeros_like(token)

    out = pl.pallas_call(
        body, name="gather_forward",
        out_shape=(*_hbm_like(lands), pltpu.SemaphoreType.DMA((3 * na,)), TOKEN),
        in_specs=[HBM] * na + [SEM, SEM, ANY],
        out_specs=(*[HBM] * na, SEM, pl.BlockSpec(memory_space=pltpu.VMEM)),
        input_output_aliases={i: i for i in range(na)},
        compiler_params=SIDE_EFFECT)(*lands, state["ici"], state["d2d"], after)
    return dict(state, lands=out[:na], fwd=out[na]), out[-1]


def _gather_wait(state, after):
    shards, lands = state["shards"], state["lands"]
    na = len(lands)

    def body(*refs):
        ins, land = refs[:na], refs[na:2 * na]
        send_sems, fwd_sems, recv_d2d = refs[2 * na:2 * na + 3]
        x, y, c = _place()
        chips = [(1 - x, y), (x, 1 - y), (1 - x, 1 - y)]
        for t in range(na):
            mine = land[t].at[_block_of(x, y, c)]
            for k in range(4):
                pltpu.make_async_remote_copy(
                    src_ref=ins[t], dst_ref=mine, send_sem=send_sems.at[4 * t + k], recv_sem=recv_d2d.at[4 * t],
                    device_id=(x, y, c), device_id_type=MESH).wait_send()
            for j, chip in enumerate(chips):
                blk = land[t].at[_block_of(*chip, c)]
                pltpu.make_async_remote_copy(
                    src_ref=blk, dst_ref=blk, send_sem=fwd_sems.at[3 * t + j], recv_sem=recv_d2d.at[4 * t + 1 + j],
                    device_id=(x, y, c), device_id_type=MESH).wait_send()
            for k, blk_id in enumerate([_block_of(x, y, 1 - c)] + [_block_of(*chip, 1 - c) for chip in chips]):
                blk = land[t].at[blk_id]
                pltpu.make_async_remote_copy(
                    src_ref=blk, dst_ref=blk, send_sem=send_sems.at[4 * t], recv_sem=recv_d2d.at[4 * t + k],
                    device_id=(x, y, c), device_id_type=MESH).wait_recv()

    out = pl.pallas_call(
        body, name="gather_wait",
        out_shape=_hbm_like(lands),
        in_specs=[ANY] * na + [HBM] * na + [SEM, SEM, SEM, ANY],
        out_specs=tuple([HBM] * na),
        input_output_aliases={na + i: i for i in range(na)},
        compiler_params=SIDE_EFFECT)(*shards, *lands, state["send"], state["fwd"], state["d2d"], after)
    return out


def _gather_from_sibling(state, after):
    lands = state["lands"]
    na = len(lands)

    def body(*refs):
        land, recv_d2d = refs[:na], refs[na]
        x, y, c = _place()
        for t in range(na):
            blk = land[t].at[_block_of(x, y, 1 - c)]
            pltpu.make_async_remote_copy(src_ref=blk, dst_ref=blk, send_sem=recv_d2d.at[4 * t],
                                         recv_sem=recv_d2d.at[4 * t], device_id=(x, y, c),
                                         device_id_type=MESH).wait_recv()

    out = pl.pallas_call(
        body, name="gather_from_sibling", out_shape=_hbm_like(lands),
        in_specs=[HBM] * na + [SEM, ANY], out_specs=tuple([HBM] * na),
        input_output_aliases={i: i for i in range(na)},
        compiler_params=SIDE_EFFECT)(*lands, state["d2d"], after)
    return dict(state, lands=list(out))


def _gather_from_chip(state, j, afters, last):
    shards, lands, relayed = state["shards"], state["lands"], state["relayed"]
    na = len(lands)
    parts = [_relay_parts(a.shape[0]) for a in shards]

    def relay_on(land_ref, t, nb, fwd_sems, recv_relay):
        x, y, c = _place()
        blk = chip_blocks(land_ref, nb)[1].at[parts[t][nb]]
        return pltpu.make_async_remote_copy(
            src_ref=blk, dst_ref=blk, send_sem=fwd_sems.at[na + t], recv_sem=recv_relay.at[2 * t + nb],
            device_id=[(x, 1 - y, c), (1 - x, y, c)][nb], device_id_type=MESH)

    def chip_blocks(land_ref, which=j):
        x, y, c = _place()
        chip = [(1 - x, y), (x, 1 - y), (1 - x, 1 - y)][which]
        return (x, y, c), land_ref.at[_block_of(*chip, c)], land_ref.at[_block_of(*chip, 1 - c)]

    def forward(*refs):
        land, recv_ici, recv_d2d, recv_relay, fwd_sems = refs[:na], refs[na], refs[na + 1], refs[na + 2], refs[-1]
        for t in range(na):
            (x, y, c), mine, _ = chip_blocks(land[t])
            if relayed and j == 2:
                for half, rows in enumerate(parts[t]):
                    if rows is not None:
                        pltpu.make_async_remote_copy(
                            src_ref=mine.at[rows], dst_ref=mine.at[rows], send_sem=fwd_sems.at[t],
                            recv_sem=recv_relay.at[2 * t + half], device_id=(x, y, c),
                            device_id_type=MESH).wait_recv()
            else:
                pltpu.make_async_remote_copy(src_ref=mine, dst_ref=mine, send_sem=fwd_sems.at[t],
                                             recv_sem=recv_ici.at[3 * t + j], device_id=(x, y, c),
                                             device_id_type=MESH).wait_recv()
            pltpu.make_async_remote_copy(src_ref=mine, dst_ref=mine, send_sem=fwd_sems.at[t],
                                         recv_sem=recv_d2d.at[4 * t + 1 + j], device_id=(x, y, 1 - c),
                                         device_id_type=MESH).start()
            if relayed and j < 2 and parts[t][j] is not None:
                relay_on(land[t], t, j, fwd_sems, recv_relay).start()

    out = pl.pallas_call(
        forward, name="gather_pass_chip_" + str(j),
        out_shape=(*_hbm_like(lands), pltpu.SemaphoreType.DMA((2 * na,))),
        in_specs=[HBM] * na + [SEM, SEM, SEM] + [ANY] * len(afters), out_specs=(*[HBM] * na, SEM),
        input_output_aliases={i: i for i in range(na)},
        compiler_params=SIDE_EFFECT)(*lands, state["ici"], state["d2d"], state["relay"], *afters)
    passed_sems = out[na]
    relays = state.get("relays", []) + ([passed_sems] if relayed and j < 2 else [])
    waited = relays if last else []

    def arrive(*refs):
        land, fwd_sems, recv_d2d = refs[:na], refs[na], refs[na + 1]
        shard, send_sems, recv_relay = refs[na + 2:2 * na + 2], refs[2 * na + 2], refs[2 * na + 3]
        for nb, relay_sems in enumerate(refs[2 * na + 4:2 * na + 4 + len(waited)]):
            for t in range(na):
                if parts[t][nb] is not None:
                    relay_on(land[t], t, nb, relay_sems, recv_relay).wait_send()
        for t in range(na):
            (x, y, c), mine, theirs = chip_blocks(land[t])
            pltpu.make_async_remote_copy(src_ref=theirs, dst_ref=theirs, send_sem=fwd_sems.at[t],
                                         recv_sem=recv_d2d.at[4 * t + 1 + j], device_id=(x, y, c),
                                         device_id_type=MESH).wait_recv()
            pltpu.make_async_remote_copy(src_ref=mine, dst_ref=mine, send_sem=fwd_sems.at[t],
                                         recv_sem=recv_d2d.at[4 * t + 1 + j], device_id=(x, y, c),
                                         device_id_type=MESH).wait_send()
            for k in range((3 if relayed else 4) if last else 0):
                pltpu.make_async_remote_copy(
                    src_ref=shard[t], dst_ref=land[t].at[_block_of(x, y, c)], send_sem=send_sems.at[4 * t + k],
                    recv_sem=recv_d2d.at[4 * t], device_id=(x, y, c), device_id_type=MESH).wait_send()

    def take(state, afters):
        taken = pl.pallas_call(
            arrive, name="gather_take_chip_" + str(j), out_shape=_hbm_like(lands),
            in_specs=[HBM] * na + [SEM, SEM] + [ANY] * na + [SEM, SEM] + [SEM] * len(waited) + [ANY] * len(afters),
            out_specs=tuple([HBM] * na), input_output_aliases={i: i for i in range(na)},
            compiler_params=SIDE_EFFECT)(*state["lands"], passed_sems, state["d2d"], *shards, state["send"],
                                         state["relay"], *waited, *afters)
        return dict(state, lands=list(taken))

    return dict(state, lands=list(out[:na]), relays=relays), take


def _to_sibling(srcs, lands, send_sems, recv_sems):
    x, y, c = _place()
    return [pltpu.make_async_remote_copy(
        src_ref=srcs[t].at[:, 1 - c], dst_ref=lands[t], send_sem=send_sems.at[t], recv_sem=recv_sems.at[t],
        device_id=(x, y, 1 - c), device_id_type=MESH) for t in range(len(srcs))]


def _to_chips(srcs, lands, send_sems, recv_sems):
    x, y, c = _place()
    copies = []
    for k in (1, 2, 3):
        px, py = x ^ (k >> 1), y ^ (k & 1)
        copies += [pltpu.make_async_remote_copy(
            src_ref=srcs[t].at[2 * px + py], dst_ref=lands[t].at[k - 1], send_sem=send_sems.at[3 * t + k - 1],
            recv_sem=recv_sems.at[3 * t + k - 1], device_id=(px, py, c), device_id_type=MESH) for t in range(len(srcs))]
    return copies


def _exchange_start(name, srcs, land_shapes, copies, per_array, after):
    na = len(srcs)
    lands = [_hbm(lax.empty(shp, a.dtype)) for shp, a in zip(land_shapes, srcs)]

    def body(*refs):
        token = refs[-1]
        for cp in copies(refs[:na], refs[na:2 * na], refs[2 * na + 1], refs[2 * na + 2]):
            cp.start()
        token[...] = jnp.zeros_like(token)

    out = pl.pallas_call(
        body, name=name,
        out_shape=(pltpu.SemaphoreType.DMA((na * per_array,)), pltpu.SemaphoreType.DMA((na * per_array,)),
                   *_hbm_like(lands), TOKEN),
        in_specs=[ANY] * na + [HBM] * na + [ANY],
        out_specs=(SEM, SEM, *[HBM] * na, pl.BlockSpec(memory_space=pltpu.VMEM)),
        input_output_aliases={na + i: 2 + i for i in range(na)},
        compiler_params=SIDE_EFFECT)(*srcs, *lands, after)
    return dict(send=out[0], recv=out[1], srcs=list(srcs), lands=out[2:2 + na]), out[-1]


def _exchange_wait(name, state, copies, afters):
    srcs, lands = state["srcs"], state["lands"]
    na = len(srcs)

    def body(*refs):
        for cp in copies(refs[:na], refs[na:2 * na], refs[2 * na], refs[2 * na + 1]):
            cp.wait_send()
            cp.wait_recv()

    out = pl.pallas_call(
        body, name=name,
        out_shape=_hbm_like(lands),
        in_specs=[ANY] * na + [HBM] * na + [SEM, SEM] + [ANY] * len(afters),
        out_specs=tuple([HBM] * na),
        input_output_aliases={na + i: i for i in range(na)},
        compiler_params=SIDE_EFFECT)(*srcs, *lands, state["send"], state["recv"], *afters)
    return out


def _adamw_math(w, g, m, v):
    m = ADAM_B1 * m + (1.0 - ADAM_B1) * g
    v = ADAM_B2 * v + (1.0 - ADAM_B2) * (g * g)
    m_hat = m / (1.0 - ADAM_B1 ** ADAM_STEP)
    v_hat = v / (1.0 - ADAM_B2 ** ADAM_STEP)
    return -ADAM_LR * (m_hat / (jnp.sqrt(v_hat) + ADAM_EPS) + ADAM_WD * w), m, v


def _adamw(own, others, w, m, v, tr, name, after):
    rows, cols = w.shape
    blk = pl.BlockSpec((tr, cols), lambda i: (i, 0))

    def body(own_ref, oth_ref, w_ref, m_ref, v_ref, after_ref, g_ref, d_ref, nm_ref, nv_ref):
        g = own_ref[...]
        for k in range(3):
            g = g + oth_ref[k].astype(F32)
        g_ref[...] = g
        d_ref[...], nm_ref[...], nv_ref[...] = _adamw_math(w_ref[...], g, m_ref[...], v_ref[...])

    out = jax.ShapeDtypeStruct((rows, cols), F32)
    return pl.pallas_call(
        body, name=name, out_shape=(out, out, out, out), grid=(rows // tr,),
        in_specs=[blk, pl.BlockSpec((3, tr, cols), lambda i: (0, i, 0)), blk, blk, blk, ANY],
        out_specs=(blk, blk, blk, blk),
        compiler_params=_params("parallel"))(own, others, w, m, v, after)


def _adamw_small(red, me, params, moments1, moments2):
    n = len(params)

    def body(me_ref, red_ref, *refs):
        ws, ms, vs = refs[:n], refs[n:2 * n], refs[2 * n:3 * n]
        loss_ref = refs[3 * n]
        outs = refs[3 * n + 1:]
        loss_ref[...] = jnp.sum(red_ref[ROW_MISC:ROW_MISC + 1, LOSS_AT:LOSS_AT + LANES], axis=-1, keepdims=True)
        for t, (row, at) in enumerate(SMALL_AT):
            g = red_ref[row:row + 1, at:at + ws[t].shape[1]]
            d, nm, nv = _adamw_math(ws[t][...], g, ms[t][...], vs[t][...])
            for o, val in zip(outs[4 * t:4 * t + 4], (g, d, nm, nv)):
                o[...] = val
        for tap in range(ws[-1].shape[0]):
            row, at = _tap_at(tap)
            g = red_ref[row:row + 1, pl.ds(pl.multiple_of(at + me_ref[0, 0] * LANES, LANES), LANES)]
            d, nm, nv = _adamw_math(ws[-1][tap], g, ms[-1][tap], vs[-1][tap])
            for o, val in zip(outs[4 * (n - 1):], (g, d, nm, nv)):
                o[tap] = val

    vmem = pl.BlockSpec(memory_space=pltpu.VMEM)
    shapes = [jax.ShapeDtypeStruct(w.shape, F32) for w in params for _ in range(4)]
    out = pl.pallas_call(
        body, name="adamw_small", out_shape=(jax.ShapeDtypeStruct((1, 1), F32), *shapes),
        in_specs=[pl.BlockSpec(memory_space=pltpu.SMEM), vmem] + [vmem] * (3 * n),
        out_specs=tuple([vmem] * (1 + 4 * n)))(me, red, *params, *moments1, *moments2)
    return out[0], [list(out[1 + k::4]) for k in range(4)]


def _tables(gq, gk, conv_w):
    gq2 = jnp.tile(gq.reshape(1, HEAD), (1, 2))
    gk2 = jnp.tile(gk.reshape(1, HEAD), (1, 2))
    conv_wp = jnp.pad(conv_w, ((0, SUBLANES - conv_w.shape[0]), (0, 0)))
    return gq2, gk2, conv_wp


def _pair_id(q):
    return jnp.array([q, 0], jnp.int32)


def _forward_in(x, g1, shards):
    s = x.shape[0]
    h = _prenorm(x, g1, min(512, s), x)
    z, w_pairs = lax.empty((s, IN_W), F32), lax.empty((N_PAIRS, PAIR_W, D_MODEL), BF16)
    for q in range(N_PAIRS):
        z, w_pairs = _fwd_in_pair(h, shards, z, w_pairs, _pair_id(q), min(512, s), "fwd_in_" + str(q),
                                  own=shards[0] if q == 0 else None)
    return h, z, w_pairs


def _forward_attn(z, rope, gq2, gk2, conv_wp, sinks):
    s = z.shape[0]
    qn, k2, v2 = _qk_prep(z, *rope, gq2, gk2, min(256, s), z)
    a, mix, mixt = _attn_fwd(qn, k2, v2, z, conv_wp, sinks, qn)
    return qn, k2, v2, a, mix, mixt


def _forward_out(x, p, target, mix, mixt, w_out, g2, w_pg, b_pg, w_pp, g3):
    s = x.shape[0]
    tm = min(512, s)
    x1, hn2, hn2t = _fwd_out(mix, w_out, x, g2, tm)
    dy, dgp, dt, pt, acc_ple = _ple(hn2, w_pg, b_pg, p, w_pp, g3, x1, target, min(256, s))
    dx1, dx1b, acc_g2 = _gate_bwd(dgp, w_pg, x1, dy, g2, tm)
    gw_out = _mm_grad(mixt, [dx1b], 512, "grad_w_out")
    gw_pg = _mm_grad(hn2t, [dgp], 512, "grad_w_ple_gate")
    gw_pp = _mm_grad(pt, [dt], 512, "grad_w_ple_proj")
    return dx1, dx1b, (gw_out, gw_pg, gw_pp), acc_ple, acc_g2


def _backward_attn(dmix, h, z, qn, k2, v2, a, rope, gq2, gk2, conv_wp, sinks, after):
    dq, dkc, dkp, dvc, dvp, dz, dzt, acc_attn = _attn_bwd(qn, k2, v2, a, z, dmix, conv_wp, sinks, after)
    dz, dzt, acc_qk = _qkv_bwd(z, dz, dzt, dq, dkc, dkp, dvc, dvp, *rope, gq2, gk2)
    return dz, _grad_w_in(dzt, h), acc_attn, acc_qk


def _local_step(x, p, target, g1, shards, gq, gk, sinks, conv_w, w_out, g2, w_pg, b_pg, w_pp, g3):
    rope, (gq2, gk2, conv_wp) = _rope_tables(x.shape[0]), _tables(gq, gk, conv_w)
    h, z, w_pairs = _forward_in(x, g1, shards)
    qn, k2, v2, a, mix, mixt = _forward_attn(z, rope, gq2, gk2, conv_wp, sinks)
    dx1, dx1b, (gw_out, gw_pg, gw_pp), acc_ple, acc_g2 = _forward_out(
        x, p, target, mix, mixt, w_out, g2, w_pg, b_pg, w_pp, g3)
    dmix = _mm_nt(dx1b, w_out, min(512, x.shape[0]), "out_bwd", dx1b)
    dz, gw_in, acc_attn, acc_qk = _backward_attn(dmix, h, z, qn, k2, v2, a, rope, gq2, gk2, conv_wp, sinks, dmix)
    grad_x, acc_g1 = _in_bwd(dz, w_pairs, x, dx1, g1, min(512, x.shape[0]), dx1)
    return grad_x, (gw_in, gw_out, gw_pg, gw_pp), (acc_g1, acc_g2, acc_ple, acc_qk, acc_attn)


def _by_owner(g):
    return g.reshape((4, 2) + g.shape[1:])


def kernel(x, p, norm_gain, w_in, q_norm_gain, k_norm_gain, attn_sinks, conv_w, w_out, ple_gate_norm_gain, w_ple_gate, b_ple_gate, w_ple_proj, ple_norm_gain, loss_target, m_norm_gain, m_w_in, m_q_norm_gain, m_k_norm_gain, m_attn_sinks, m_conv_w, m_w_out, m_ple_gate_norm_gain, m_w_ple_gate, m_b_ple_gate, m_w_ple_proj, m_ple_norm_gain, v_norm_gain, v_w_in, v_q_norm_gain, v_k_norm_gain, v_attn_sinks, v_conv_w, v_w_out, v_ple_gate_norm_gain, v_w_ple_gate, v_b_ple_gate, v_w_ple_proj, v_ple_norm_gain):
    me = 4 * lax.axis_index("x") + 2 * lax.axis_index("y") + lax.axis_index("c")
    place = jnp.stack([lax.axis_index("c"), 2 * lax.axis_index("x") + lax.axis_index("y")]).astype(jnp.int32)
    xs, ps, target = x[0], p[0, 0], loss_target[0]

    shard_in = w_in[0].T.astype(BF16)
    own_late = [w_out[0].astype(BF16), w_ple_gate[0].astype(BF16), w_ple_proj[0].astype(BF16)]
    with_own = lambda gathered, own: lax.dynamic_update_slice(gathered, own[None], (me,) + (0,) * own.ndim)
    early, started = _gather_start([shard_in, conv_w[0]], shard_in, relay=True)
    tm = min(512, xs.shape[0])
    h = _prenorm(xs, norm_gain, tm, started)

    z, w_pairs = lax.empty((xs.shape[0], IN_W), F32), lax.empty((N_PAIRS, PAIR_W, D_MODEL), BF16)
    early = _gather_from_sibling(early, h)
    pair_of = lambda flip: jnp.stack([place[1] ^ flip, place[0]])
    z, w_pairs = _fwd_in_pair(h, early["lands"][0], z, w_pairs, pair_of(0), tm, "fwd_in_own", own=shard_in)
    rope = _rope_tables(xs.shape[0])
    early, take = _gather_from_chip(early, 0, (z, *rope, *own_late), last=False)
    for j, flip in enumerate((2, 1, 3)):
        if j < 2:
            early, take_next = _gather_from_chip(early, j + 1, (z,), last=j == 1)
        if j == 2:
            late, started_late = _gather_start(own_late, z)
        early = take(early, (z,) if j < 2 else (z, started_late))
        z, w_pairs = _fwd_in_pair(h, early["lands"][0], z, w_pairs, pair_of(flip), tm, "fwd_in_chip_" + str(j))
        take = take_next
    conv_full = jnp.transpose(with_own(early["lands"][1], conv_w[0]), (1, 0, 2)).reshape(3, ATTN_W)
    gq2, gk2, conv_wp = _tables(q_norm_gain[0], k_norm_gain[0], conv_full)
    qn, k2, v2 = _qk_prep(z, *rope, gq2, gk2, min(256, xs.shape[0]), z)
    late, forwarded = _gather_forward(late, qn)
    a, mix, mixt = _attn_fwd(qn, k2, v2, z, conv_wp, attn_sinks, forwarded)
    g_out, g_pg, g_pp = (with_own(g, own) for g, own in zip(_gather_wait(late, mix), own_late))
    w_out_f = g_out.reshape(D_MODEL, D_MODEL)
    w_pg_f = g_pg.reshape(D_MODEL, D_MODEL)
    w_pp_f = jnp.transpose(g_pp, (1, 0, 2)).reshape(PLE_DIM, D_MODEL)

    dx1, dx1b, (gw_out, gw_pg, gw_pp), acc_ple, acc_g2 = _forward_out(
        xs, ps, target, mix, mixt, w_out_f, ple_gate_norm_gain, w_pg_f, b_ple_gate, w_pp_f, ple_norm_gain)

    names = ("w_out", "w_ple_gate", "w_ple_proj")
    gw_pp_t = jnp.transpose(gw_pp.reshape(PLE_DIM, N_DEV, PLE_DIM), (1, 0, 2))
    grads = [_by_owner(gw_out.reshape(N_DEV, D_MODEL // N_DEV, D_MODEL)),
             _by_owner(gw_pg.reshape(N_DEV, D_MODEL // N_DEV, D_MODEL)), _by_owner(gw_pp_t)]
    pairs, paired = _exchange_start("pair_start", grads, [(4,) + g.shape[2:] for g in grads], _to_sibling, 1, dx1b)
    dmix = _mm_nt(dx1b, w_out_f, tm, "out_bwd", paired)
    from_sibling = _exchange_wait("pair_wait", pairs, _to_sibling, (dmix,))
    sums = [_pair_sum(g, r, place, 256, "pair_sum_" + nm) for g, r, nm in zip(pairs["srcs"], from_sibling, names)]
    chips, sent = _exchange_start("chip_start", [pb for pb, _ in sums], [(3,) + pb.shape[1:] for pb, _ in sums],
                                  _to_chips, 3, sums[-1][1])

    dz, gw_in, acc_attn, acc_qk = _backward_attn(
        dmix, h, z, qn, k2, v2, a, rope, gq2, gk2, conv_wp, attn_sinks, sent)

    gw_in_t = [_by_owner(gw_in)]
    pairs_in, paired_in = _exchange_start("pair_start_w_in", gw_in_t, [(4,) + gw_in_t[0].shape[2:]], _to_sibling, 1,
                                          gw_in)
    from_chips = _exchange_wait("chip_wait", chips, _to_chips, (gw_in,))
    big = {}
    for (_, own), oth, w, m, v, nm in zip(sums, from_chips, (w_out, w_ple_gate, w_ple_proj),
                                          (m_w_out, m_w_ple_gate, m_w_ple_proj),
                                          (v_w_out, v_w_ple_gate, v_w_ple_proj), names):
        big[nm] = [t[None] for t in _adamw(own, oth, w[0], m[0], v[0], 256, "adamw_" + nm, paired_in)]

    (from_sibling_in,) = _exchange_wait("pair_wait_w_in", pairs_in, _to_sibling, [big[nm][0] for nm in names])
    pb_in, own_in = _pair_sum(pairs_in["srcs"][0], from_sibling_in, place, SHARD_IN // 2, "pair_sum_w_in")
    chips_in, sent_in = _exchange_start("chip_start_w_in", [pb_in], [(3,) + pb_in.shape[1:]], _to_chips, 3, own_in)
    grad_x, acc_g1 = _in_bwd(dz, w_pairs, xs, dx1, norm_gain, tm, sent_in)
    (from_chips_in,) = _exchange_wait("chip_wait_w_in", chips_in, _to_chips, (grad_x,))
    big["w_in"] = [t.T[None] for t in _adamw(own_in, from_chips_in, w_in[0].T, m_w_in[0].T, v_w_in[0].T, SHARD_IN // 4,
                                             "adamw_w_in", grad_x)]

    red = _reduce_small(acc_g1, acc_g2, acc_ple, acc_qk, acc_attn)
    small = [norm_gain, ple_gate_norm_gain, b_ple_gate, ple_norm_gain, q_norm_gain, k_norm_gain, attn_sinks]
    small_m = [m_norm_gain, m_ple_gate_norm_gain, m_b_ple_gate, m_ple_norm_gain, m_q_norm_gain, m_k_norm_gain,
               m_attn_sinks]
    small_v = [v_norm_gain, v_ple_gate_norm_gain, v_b_ple_gate, v_ple_norm_gain, v_q_norm_gain, v_k_norm_gain,
               v_attn_sinks]
    taps_first = lambda t: jnp.transpose(t, (1, 0, 2))
    loss, kinds = _adamw_small(red, me.reshape(1, 1).astype(jnp.int32), small + [taps_first(conv_w)],
                               small_m + [taps_first(m_conv_w)], small_v + [taps_first(v_conv_w)])

    def order(k):
        sm = kinds[k]
        return [sm[0], big["w_in"][k], sm[4], sm[5], sm[6], taps_first(sm[7]), big["w_out"][k], sm[1],
                big["w_ple_gate"][k], sm[2], big["w_ple_proj"][k], sm[3]]

    return (loss[0, 0], grad_x[None], *order(0), *order(1), *order(2), *order(3))
```

```python
import jax
import jax.numpy as jnp
from jax import lax
from jax.experimental import pallas as pl
from jax.experimental.pallas import tpu as pltpu

F32, BF16 = jnp.float32, jnp.bfloat16

D_MODEL = 2048
PLE_DIM = 256
ATTN_W = 1024
HEAD = 64
N_Q_HEADS = 16
KV_W = 256
QKV_W = ATTN_W + 2 * KV_W
REST_W = 5 * 1024
IN_W = QKV_W + REST_W
GATE_A0, CONV_B0, CONV_C0, CONV_H0, GATE_C0 = (QKV_W + 1024 * t for t in range(5))
K2_W = 4 * 128
ROT = 16
ROPE_THETA = 500000.0
EPS = 1e-6
NEG_INF = -1e30
BLK = 128
LANES = 128
SUBLANES = 8
N_DEV = 8
SHARD_IN = IN_W // N_DEV
PAIR_W = 2 * SHARD_IN
N_PAIRS = IN_W // PAIR_W
SLAB_ROWS = 8
PACKED_ROWS = 16
SUB_ROWS = 128
V7X_VMEM_LIMIT = 52 * 1024 * 1024

ADAM_LR, ADAM_B1, ADAM_B2, ADAM_EPS, ADAM_WD, ADAM_STEP = 0.001, 0.9, 0.999, 1e-08, 0.01, 10
MESH = pl.DeviceIdType.MESH


def _params(*semantics):
    return pltpu.CompilerParams(dimension_semantics=semantics, vmem_limit_bytes=V7X_VMEM_LIMIT)


ANY = pl.BlockSpec(memory_space=pl.ANY)


def _resident(shape):
    return pl.BlockSpec(shape, lambda *_: (0,) * len(shape), pipeline_mode=pl.Buffered(1))


def _dot(a, b):
    return jnp.dot(a, b, preferred_element_type=F32)


def _dot_nt(a, b):
    return lax.dot_general(a, b, (((1,), (1,)), ((), ())), preferred_element_type=F32)


def _rms(xf):
    r = lax.rsqrt(jnp.mean(xf * xf, axis=-1, keepdims=True) + EPS)
    return xf * r, r


def _rms_bwd(dxn, xn, r):
    return r * (dxn - xn * jnp.mean(dxn * xn, axis=-1, keepdims=True))


def _sig(g):
    return jax.nn.sigmoid(g)


def _dsilu(g, sg):
    return sg * (1.0 + g * (1.0 - sg))


def _low_half(shape):
    return lax.broadcasted_iota(jnp.int32, shape, len(shape) - 1) < HEAD


def _half_sums(v):
    lo = _low_half(v.shape)
    s_lo = jnp.sum(jnp.where(lo, v, 0.0), axis=-1, keepdims=True)
    s_hi = jnp.sum(jnp.where(lo, 0.0, v), axis=-1, keepdims=True)
    return jnp.where(lo, s_lo, s_hi)


def _rope(v, a, bm, bp):
    return v * a + pltpu.roll(v, LANES - ROT // 2, 1) * bm + pltpu.roll(v, ROT // 2, 1) * bp


def _rope_t(dy, a, bm, bp):
    return dy * a + pltpu.roll(dy * bm, ROT // 2, 1) + pltpu.roll(dy * bp, LANES - ROT // 2, 1)


def _dup_halves(v):
    lo = _low_half(v.shape)
    a = jnp.where(lo, v, 0.0)
    b = jnp.where(lo, 0.0, v)
    return a + pltpu.roll(a, HEAD, 1), b + pltpu.roll(b, HEAD, 1)


def _rope_tables(s):
    half = ROT // 2
    lane = lax.broadcasted_iota(jnp.int32, (s, LANES), 1) % HEAD
    pos = lax.broadcasted_iota(jnp.int32, (half, s), 1).astype(F32)
    freq = lax.broadcasted_iota(jnp.int32, (half, s), 0).astype(F32)
    ang = pos * jnp.power(jnp.float32(ROPE_THETA), -freq * 2.0 / ROT)
    cos, sin = lax.optimization_barrier((jnp.cos(ang), jnp.sin(ang)))
    cos, sin = (jnp.tile(t.T, (1, LANES // half)) for t in (cos, sin))
    a = jnp.where(lane < ROT, cos, 1.0)
    bm = jnp.where(lane < half, -sin, 0.0)
    bp = jnp.where((lane >= half) & (lane < ROT), sin, 0.0)
    return a, bm, bp


def _prenorm(x, g1, tm, after):
    s = x.shape[0]

    def body(x_ref, g_ref, after_ref, h_ref):
        xn, _ = _rms(x_ref[...])
        h_ref[...] = (xn * g_ref[...]).astype(BF16)

    return pl.pallas_call(
        body, name="prenorm",
        out_shape=jax.ShapeDtypeStruct((s, D_MODEL), BF16),
        grid=(s // tm,),
        in_specs=[pl.BlockSpec((tm, D_MODEL), lambda i: (i, 0)), pl.BlockSpec((1, D_MODEL), lambda i: (0, 0)), ANY],
        out_specs=pl.BlockSpec((tm, D_MODEL), lambda i: (i, 0)),
        compiler_params=_params("parallel"))(x, g1, after)


def _fwd_in_pair(h, shards, z, w_pairs, pair, tm, name, own=None):
    s = h.shape[0]

    def body(pair_ref, h_ref, lo_ref, hi_ref, z_in, wp_in, z_ref, wp_ref):
        @pl.when(pl.program_id(0) == 0)
        def _():
            wp_ref[0, 0:SHARD_IN, :] = lo_ref[0]
            wp_ref[0, SHARD_IN:PAIR_W, :] = hi_ref[0]

        z_ref[...] = _dot_nt(h_ref[...], wp_ref[0])

    def body_own(pair_ref, h_ref, own_ref, other_ref, z_in, wp_in, z_ref, wp_ref):
        @pl.when(pl.program_id(0) == 0)
        def _():
            first = pl.multiple_of(pair_ref[1] * SHARD_IN, SHARD_IN)
            wp_ref[0, pl.ds(first, SHARD_IN), :] = own_ref[...]
            wp_ref[0, pl.ds(SHARD_IN - first, SHARD_IN), :] = other_ref[0]

        z_ref[...] = _dot_nt(h_ref[...], wp_ref[0])

    if own is None:
        blocks = [pl.BlockSpec((1, SHARD_IN, D_MODEL), lambda i, p: (2 * p[0], 0, 0)),
                  pl.BlockSpec((1, SHARD_IN, D_MODEL), lambda i, p: (2 * p[0] + 1, 0, 0))]
        operands = (shards, shards)
    else:
        blocks = [pl.BlockSpec((SHARD_IN, D_MODEL), lambda i, p: (0, 0)),
                  pl.BlockSpec((1, SHARD_IN, D_MODEL), lambda i, p: (2 * p[0] + 1 - p[1], 0, 0))]
        operands = (own, shards)
    grid_spec = pltpu.PrefetchScalarGridSpec(
        num_scalar_prefetch=1, grid=(s // tm,),
        in_specs=[pl.BlockSpec((tm, D_MODEL), lambda i, p: (i, 0)), *blocks, ANY, ANY],
        out_specs=(pl.BlockSpec((tm, PAIR_W), lambda i, p: (i, p[0])),
                   pl.BlockSpec((1, PAIR_W, D_MODEL), lambda i, p: (p[0], 0, 0))))
    return pl.pallas_call(
        body if own is None else body_own, name=name, grid_spec=grid_spec,
        out_shape=(jax.ShapeDtypeStruct(z.shape, z.dtype), jax.ShapeDtypeStruct(w_pairs.shape, w_pairs.dtype)),
        input_output_aliases={4: 0, 5: 1},
        compiler_params=_params("arbitrary"))(pair, h, *operands, z, w_pairs)


def _qk_prep(z, ra, rbm, rbp, gq2, gk2, tm, after):
    s = z.shape[0]

    def body(z_ref, a_ref, bm_ref, bp_ref, gq_ref, gk_ref, after_ref, q_ref, k2_ref, v2_ref):
        a, bm, bp = a_ref[...], bm_ref[...], bp_ref[...]
        for r in range(ATTN_W // LANES):
            x = z_ref[:, LANES * r:LANES * (r + 1)]
            rr = lax.rsqrt(_half_sums(x * x) * (1.0 / HEAD) + EPS)
            q_ref[:, LANES * r:LANES * (r + 1)] = _rope(x * rr * gq_ref[...], a, bm, bp).astype(BF16)
        for m in range(KV_W // LANES):
            x = z_ref[:, ATTN_W + LANES * m:ATTN_W + LANES * (m + 1)]
            rr = lax.rsqrt(_half_sums(x * x) * (1.0 / HEAD) + EPS)
            k_lo, k_hi = _dup_halves(_rope(x * rr * gk_ref[...], a, bm, bp))
            k2_ref[:, 2 * LANES * m:2 * LANES * m + LANES] = k_lo.astype(BF16)
            k2_ref[:, 2 * LANES * m + LANES:2 * LANES * (m + 1)] = k_hi.astype(BF16)
            v_lo, v_hi = _dup_halves(z_ref[:, ATTN_W + KV_W + LANES * m:ATTN_W + KV_W + LANES * (m + 1)])
            v2_ref[:, 2 * LANES * m:2 * LANES * m + LANES] = v_lo.astype(BF16)
            v2_ref[:, 2 * LANES * m + LANES:2 * LANES * (m + 1)] = v_hi.astype(BF16)

    row = lambda w: pl.BlockSpec((tm, w), lambda i: (i, 0))
    one = pl.BlockSpec((1, LANES), lambda i: (0, 0))
    return pl.pallas_call(
        body, name="qk_prep",
        out_shape=(jax.ShapeDtypeStruct((s, ATTN_W), BF16), jax.ShapeDtypeStruct((s, K2_W), BF16),
                   jax.ShapeDtypeStruct((s, K2_W), BF16)),
        grid=(s // tm,),
        in_specs=[row(PAIR_W), row(LANES), row(LANES), row(LANES), one, one, ANY],
        out_specs=(row(ATTN_W), row(K2_W), row(K2_W)),
        compiler_params=_params("parallel"))(z, ra, rbm, rbp, gq2, gk2, after)


GROUP = 4


def _window_mask(n):
    row = lax.broadcasted_iota(jnp.int32, (GROUP * BLK, 2 * BLK), 0) % BLK
    col = lax.broadcasted_iota(jnp.int32, (GROUP * BLK, 2 * BLK), 1)
    return (col > row) & (col <= row + BLK) & ((col >= BLK) | (n > 0))


def _stack_heads(pairs, zero):
    lo = _low_half(pairs[0].shape)
    parts = []
    for v in pairs:
        parts += [jnp.where(lo, v, zero), jnp.where(lo, zero, v)]
    return jnp.concatenate(parts, axis=0)


def _unstack_heads(v4):
    lo = _low_half((BLK, LANES))
    return [jnp.where(lo, v4[2 * i * BLK:(2 * i + 1) * BLK], v4[(2 * i + 1) * BLK:(2 * i + 2) * BLK]) for i in range(2)]


def _group_sinks(sink_ref, kvh):
    slot = lax.broadcasted_iota(jnp.int32, (GROUP * BLK, 1), 0) // BLK
    col = jnp.zeros((GROUP * BLK, 1), F32)
    for i in range(GROUP):
        col = jnp.where(slot == i, sink_ref[0, GROUP * kvh + i], col)
    return col, slot


def _head_probs(qm, kw, valid, sink):
    sc = jnp.where(valid, _dot_nt(qm, kw) * (HEAD ** -0.5), NEG_INF)
    mx = jnp.maximum(jnp.max(sc, axis=-1, keepdims=True), sink)
    ex = jnp.exp(sc - mx)
    den = jnp.sum(ex, axis=-1, keepdims=True) + jnp.exp(sink - mx)
    return ex / den, mx, den


def _cols(start, width=ATTN_W):
    return slice(start, start + width)


def _conv_fwd(z_ref, zp_ref, cw_ref, ext_ref, n):
    u = z_ref[:, _cols(CONV_C0)] * z_ref[:, _cols(CONV_H0)]
    pu = zp_ref[:, _cols(CONV_C0)] * zp_ref[:, _cols(CONV_H0)]
    ext_ref[0:SUBLANES, :] = jnp.where(n > 0, pu, 0.0)
    ext_ref[SUBLANES:SUBLANES + BLK, :] = u
    um1 = ext_ref[SUBLANES - 1:SUBLANES - 1 + BLK, :]
    um2 = ext_ref[SUBLANES - 2:SUBLANES - 2 + BLK, :]
    cv = cw_ref[0:1, :] * um2 + cw_ref[1:2, :] * um1 + cw_ref[2:3, :] * u
    return u, um1, um2, cv


def _prev_rows(n):
    return (jnp.maximum(n * (BLK // SUBLANES) - 1, 0), 0)


def _attn_fwd(qn, k2, v2, z, conv_wp, sinks, after):
    s = qn.shape[0]
    nb = s // BLK

    def body(sink_ref, q_ref, kc_ref, kp_ref, vc_ref, vp_ref, z_ref, zp_ref, cw_ref, after_ref, a_ref, mix_ref,
             mixt_ref, ext_ref):
        n = pl.program_id(0)
        valid = _window_mask(n)
        for kvh in range(K2_W // LANES):
            cols = slice(LANES * kvh, LANES * (kvh + 1))
            kw = jnp.concatenate([kp_ref[:, cols], kc_ref[:, cols]], axis=0)
            vw = jnp.concatenate([vp_ref[:, cols], vc_ref[:, cols]], axis=0)
            blocks = [slice(LANES * r, LANES * (r + 1)) for r in (2 * kvh, 2 * kvh + 1)]
            q4 = _stack_heads([q_ref[:, rc] for rc in blocks], jnp.zeros((BLK, LANES), BF16))
            p, _, _ = _head_probs(q4, kw, valid, _group_sinks(sink_ref, kvh)[0])
            for rc, a in zip(blocks, _unstack_heads(_dot(p.astype(BF16), vw))):
                a_ref[:, rc] = a
                g = z_ref[:, _cols(GATE_A0 + rc.start, LANES)]
                mix_ref[:, rc] = (a * (g * _sig(g))).astype(BF16)
        _, _, _, cv = _conv_fwd(z_ref, zp_ref, cw_ref, ext_ref, n)
        gc = z_ref[:, _cols(GATE_C0)]
        mix_ref[:, ATTN_W:D_MODEL] = (z_ref[:, _cols(CONV_B0)] * cv * (gc * _sig(gc))).astype(BF16)
        mixt_ref[...] = mix_ref[...].T

    cur = lambda w: pl.BlockSpec((BLK, w), lambda n: (n, 0))
    prev = lambda w: pl.BlockSpec((BLK, w), lambda n: (jnp.maximum(n - 1, 0), 0))
    return pl.pallas_call(
        body, name="attn_fwd",
        out_shape=(jax.ShapeDtypeStruct((s, ATTN_W), F32), jax.ShapeDtypeStruct((s, D_MODEL), BF16),
                   jax.ShapeDtypeStruct((D_MODEL, s), BF16)),
        grid=(nb,),
        in_specs=[pl.BlockSpec(memory_space=pltpu.SMEM),
                  cur(ATTN_W), cur(K2_W), prev(K2_W), cur(K2_W), prev(K2_W), cur(IN_W),
                  pl.BlockSpec((SUBLANES, IN_W), _prev_rows),
                  pl.BlockSpec((SUBLANES, ATTN_W), lambda n: (0, 0)), ANY],
        out_specs=(cur(ATTN_W), cur(D_MODEL), pl.BlockSpec((D_MODEL, BLK), lambda n: (0, n))),
        scratch_shapes=[pltpu.VMEM((BLK + 2 * SUBLANES, ATTN_W), F32)],
        compiler_params=_params("parallel"))(sinks, qn, k2, k2, v2, v2, z, z, conv_wp, after)


def _fwd_out(mix, w_out, x, g2, tm):
    s = x.shape[0]

    def body(m_ref, w_ref, x_ref, g_ref, x1_ref, h_ref, ht_ref):
        x1 = x_ref[...] + _dot(m_ref[...], w_ref[...])
        x1_ref[...] = x1
        xn, _ = _rms(x1)
        h = (xn * g_ref[...]).astype(BF16)
        h_ref[...] = h
        ht_ref[...] = h.T

    row = pl.BlockSpec((tm, D_MODEL), lambda i: (i, 0))
    return pl.pallas_call(
        body, name="fwd_out",
        out_shape=(jax.ShapeDtypeStruct((s, D_MODEL), F32), jax.ShapeDtypeStruct((s, D_MODEL), BF16),
                   jax.ShapeDtypeStruct((D_MODEL, s), BF16)),
        grid=(s // tm,),
        in_specs=[row, _resident((D_MODEL, D_MODEL)), row, pl.BlockSpec((1, D_MODEL), lambda i: (0, 0))],
        out_specs=(row, row, pl.BlockSpec((D_MODEL, tm), lambda i: (0, i))),
        compiler_params=_params("parallel"))(mix, w_out, x, g2)


def _ple(hn2, w_pg, b_pg, p, w_pp, g3, x1, target, tm):
    s = x1.shape[0]

    def body(h_ref, wg_ref, b_ref, p_ref, wp_ref, g3_ref, x1_ref, t_ref, dy_ref, dgp_ref, dt_ref, pt_ref, acc_ref):
        gate = _sig(_dot(h_ref[...], wg_ref[...]) + b_ref[...])
        pb = p_ref[...].astype(BF16)
        pt_ref[...] = pb.T
        t = _dot(pb, wp_ref[...])
        tn, r3 = _rms(t)
        e = tn * g3_ref[...]
        diff = x1_ref[...] + gate * e - t_ref[...]
        dy = diff * (1.0 / D_MODEL)
        dy_ref[...] = dy
        dgp = dy * e * (gate * (1.0 - gate))
        dgp_ref[...] = dgp.astype(BF16)
        de = dy * gate
        dt_ref[...] = _rms_bwd(de * g3_ref[...], tn, r3).astype(BF16)

        @pl.when(pl.program_id(0) == 0)
        def _():
            acc_ref[...] = jnp.zeros_like(acc_ref)

        acc_ref[0:1, :] += jnp.sum(dgp, axis=0, keepdims=True)
        acc_ref[1:2, :] += jnp.sum(de * tn, axis=0, keepdims=True)
        acc_ref[2:3, :] += jnp.sum(diff * diff, axis=0, keepdims=True) * (0.5 / D_MODEL)

    row = pl.BlockSpec((tm, D_MODEL), lambda i: (i, 0))
    vec = pl.BlockSpec((1, D_MODEL), lambda i: (0, 0))
    return pl.pallas_call(
        body, name="ple",
        out_shape=(jax.ShapeDtypeStruct((s, D_MODEL), F32), jax.ShapeDtypeStruct((s, D_MODEL), BF16),
                   jax.ShapeDtypeStruct((s, D_MODEL), BF16), jax.ShapeDtypeStruct((PLE_DIM, s), BF16),
                   jax.ShapeDtypeStruct((SUBLANES, D_MODEL), F32)),
        grid=(s // tm,),
        in_specs=[row, _resident((D_MODEL, D_MODEL)), vec, pl.BlockSpec((tm, PLE_DIM), lambda i: (i, 0)),
                  _resident((PLE_DIM, D_MODEL)), vec, row, row],
        out_specs=(row, row, row, pl.BlockSpec((PLE_DIM, tm), lambda i: (0, i)),
                   pl.BlockSpec((SUBLANES, D_MODEL), lambda i: (0, 0))),
        compiler_params=_params("arbitrary"))(hn2, w_pg, b_pg, p, w_pp, g3, x1, target)


def _gate_bwd(dgp, w_pg, x1, dy, g2, tm):
    s = x1.shape[0]

    def body(d_ref, w_ref, x1_ref, dy_ref, g_ref, dx_ref, dxb_ref, acc_ref):
        dh = _dot_nt(d_ref[...], w_ref[...])
        xn, r = _rms(x1_ref[...])
        dx1 = dy_ref[...] + _rms_bwd(dh * g_ref[...], xn, r)
        dx_ref[...] = dx1
        dxb_ref[...] = dx1.astype(BF16)

        @pl.when(pl.program_id(0) == 0)
        def _():
            acc_ref[...] = jnp.zeros_like(acc_ref)

        acc_ref[0:1, :] += jnp.sum(dh * xn, axis=0, keepdims=True)

    row = pl.BlockSpec((tm, D_MODEL), lambda i: (i, 0))
    return pl.pallas_call(
        body, name="gate_bwd",
        out_shape=(jax.ShapeDtypeStruct((s, D_MODEL), F32), jax.ShapeDtypeStruct((s, D_MODEL), BF16),
                   jax.ShapeDtypeStruct((SUBLANES, D_MODEL), F32)),
        grid=(s // tm,),
        in_specs=[row, _resident((D_MODEL, D_MODEL)), row, row, pl.BlockSpec((1, D_MODEL), lambda i: (0, 0))],
        out_specs=(row, row, pl.BlockSpec((SUBLANES, D_MODEL), lambda i: (0, 0))),
        compiler_params=_params("arbitrary"))(dgp, w_pg, x1, dy, g2)


def _mm_nt(a, b, tm, name, after):
    m, k = a.shape
    n = b.shape[0]

    def body(a_ref, b_ref, after_ref, o_ref):
        o_ref[...] = _dot_nt(a_ref[...], b_ref[...])

    return pl.pallas_call(
        body, name=name,
        out_shape=jax.ShapeDtypeStruct((m, n), F32),
        grid=(m // tm,),
        in_specs=[pl.BlockSpec((tm, k), lambda i: (i, 0)), _resident((n, k)), ANY],
        out_specs=pl.BlockSpec((tm, n), lambda i: (i, 0)),
        compiler_params=_params("parallel"))(a, b, after)


def _attn_bwd(qn, k2, v2, a, z, dmix, conv_wp, sinks, after):
    s = qn.shape[0]
    nb = s // BLK

    def body(sink_ref, q_ref, kc_ref, kp_ref, vc_ref, vp_ref, a_ref, z_ref, zp_ref, zn_ref, dm_ref, dmn_ref,
             cw_ref, after_ref, dq_ref, dkc_ref, dkp_ref, dvc_ref, dvp_ref, dz_ref, dzt_ref, acc_ref, ext_ref):
        n = pl.program_id(0)
        valid = _window_mask(n)
        lane = lax.broadcasted_iota(jnp.int32, (1, ATTN_W), 1)

        @pl.when(n == 0)
        def _():
            acc_ref[...] = jnp.zeros_like(acc_ref)

        dz_ref[:, 0:QKV_W] = jnp.zeros((BLK, QKV_W), BF16)
        dsink = jnp.zeros((1, ATTN_W), F32)
        for kvh in range(K2_W // LANES):
            cols = slice(LANES * kvh, LANES * (kvh + 1))
            kw = jnp.concatenate([kp_ref[:, cols], kc_ref[:, cols]], axis=0)
            vw = jnp.concatenate([vp_ref[:, cols], vc_ref[:, cols]], axis=0)
            blocks = [slice(LANES * r, LANES * (r + 1)) for r in (2 * kvh, 2 * kvh + 1)]
            das, avs = [], []
            for rc in blocks:
                g = z_ref[:, _cols(GATE_A0 + rc.start, LANES)]
                sg = _sig(g)
                dm = dm_ref[:, rc]
                av = a_ref[:, rc]
                das.append(dm * (g * sg))
                avs += [av, av]
                dz_ref[:, _cols(GATE_A0 + rc.start, LANES)] = (dm * av * _dsilu(g, sg)).astype(BF16)
            q4 = _stack_heads([q_ref[:, rc] for rc in blocks], jnp.zeros((BLK, LANES), BF16))
            sink, slot = _group_sinks(sink_ref, kvh)
            p, mx, den = _head_probs(q4, kw, valid, sink)
            do4 = _stack_heads(das, 0.0)
            delta = jnp.sum(do4 * jnp.concatenate(avs, axis=0), axis=-1, keepdims=True)
            dob = do4.astype(BF16)
            ds = p * (_dot_nt(dob, vw) - delta) * (HEAD ** -0.5)
            for rc, dq in zip(blocks, _unstack_heads(_dot(ds.astype(BF16), kw))):
                dq_ref[:, rc] = dq
            dk2 = _dot(ds.T.astype(BF16), q4)
            dv2 = _dot(p.T.astype(BF16), dob)
            dkp_ref[:, cols] = dk2[0:BLK]
            dkc_ref[:, cols] = dk2[BLK:2 * BLK]
            dvp_ref[:, cols] = dv2[0:BLK]
            dvc_ref[:, cols] = dv2[BLK:2 * BLK]
            dsk = jnp.exp(sink - mx) / den * delta
            for i in range(GROUP):
                dsink = dsink - jnp.where(lane == GROUP * kvh + i,
                                          jnp.sum(jnp.where(slot == i, dsk, 0.0), axis=0, keepdims=True), 0.0)
        acc_ref[0:1, :] += dsink

        u, um1, um2, cv = _conv_fwd(z_ref, zp_ref, cw_ref, ext_ref, n)
        cb = z_ref[:, _cols(CONV_B0)]
        gc = z_ref[:, _cols(GATE_C0)]
        sgc = _sig(gc)
        dmc = dm_ref[:, ATTN_W:D_MODEL]
        t = dmc * (gc * sgc)
        dcv = t * cb
        dz_ref[:, _cols(CONV_B0)] = (t * cv).astype(BF16)
        dz_ref[:, _cols(GATE_C0)] = (dmc * cb * cv * _dsilu(gc, sgc)).astype(BF16)
        gcn = zn_ref[:, _cols(GATE_C0)]
        dcvn = dmn_ref[:, ATTN_W:D_MODEL] * (gcn * _sig(gcn)) * zn_ref[:, _cols(CONV_B0)]
        ext_ref[0:BLK, :] = dcv
        ext_ref[BLK:BLK + SUBLANES, :] = jnp.where(n < nb - 1, dcvn, 0.0)
        du = (cw_ref[2:3, :] * dcv + cw_ref[1:2, :] * ext_ref[1:1 + BLK, :]
              + cw_ref[0:1, :] * ext_ref[2:2 + BLK, :])
        dz_ref[:, _cols(CONV_C0)] = (du * z_ref[:, _cols(CONV_H0)]).astype(BF16)
        dz_ref[:, _cols(CONV_H0)] = (du * z_ref[:, _cols(CONV_C0)]).astype(BF16)
        acc_ref[1:2, :] += jnp.sum(dcv * um2, axis=0, keepdims=True)
        acc_ref[2:3, :] += jnp.sum(dcv * um1, axis=0, keepdims=True)
        acc_ref[3:4, :] += jnp.sum(dcv * u, axis=0, keepdims=True)
        dzt_ref[...] = dz_ref[...].T

    cur = lambda w: pl.BlockSpec((BLK, w), lambda n: (n, 0))
    prev = lambda w: pl.BlockSpec((BLK, w), lambda n: (jnp.maximum(n - 1, 0), 0))
    nxt = lambda w: pl.BlockSpec(
        (SUBLANES, w), lambda n: (jnp.minimum((n + 1) * (BLK // SUBLANES), nb * (BLK // SUBLANES) - 1), 0))
    f32 = lambda w: jax.ShapeDtypeStruct((s, w), F32)
    return pl.pallas_call(
        body, name="attn_bwd",
        out_shape=(f32(ATTN_W), f32(K2_W), f32(K2_W), f32(K2_W), f32(K2_W),
                   jax.ShapeDtypeStruct((s, IN_W), BF16), jax.ShapeDtypeStruct((IN_W, s), BF16),
                   jax.ShapeDtypeStruct((SUBLANES, ATTN_W), F32)),
        grid=(nb,),
        in_specs=[pl.BlockSpec(memory_space=pltpu.SMEM),
                  cur(ATTN_W), cur(K2_W), prev(K2_W), cur(K2_W), prev(K2_W), cur(ATTN_W), cur(IN_W),
                  pl.BlockSpec((SUBLANES, IN_W), _prev_rows), nxt(IN_W), cur(D_MODEL), nxt(D_MODEL),
                  pl.BlockSpec((SUBLANES, ATTN_W), lambda n: (0, 0)), ANY],
        out_specs=(cur(ATTN_W), cur(K2_W), cur(K2_W), cur(K2_W), cur(K2_W), cur(IN_W),
                   pl.BlockSpec((IN_W, BLK), lambda n: (0, n)), pl.BlockSpec((SUBLANES, ATTN_W), lambda n: (0, 0))),
        scratch_shapes=[pltpu.VMEM((BLK + 2 * SUBLANES, ATTN_W), F32)],
        compiler_params=_params("arbitrary"))(sinks, qn, k2, k2, v2, v2, a, z, z, z, dmix, dmix, conv_wp, after)


def _qkv_bwd(z, dz, dzt, dq, dkc, dkp, dvc, dvp, ra, rbm, rbp, gq2, gk2):
    s = z.shape[0]
    nb = s // BLK

    def body(z_ref, dz_in, dzt_in, dq_ref, dkc_ref, dkp_ref, dvc_ref, dvp_ref, a_ref, bm_ref, bp_ref, gq_ref, gk_ref,
             dz_ref, dzt_ref, acc_ref):
        n = pl.program_id(0)
        a, bm, bp = a_ref[...], bm_ref[...], bp_ref[...]
        lo = _low_half((BLK, LANES))
        last = n == nb - 1

        @pl.when(n == 0)
        def _():
            acc_ref[...] = jnp.zeros_like(acc_ref)

        def norm_bwd(x, dy, gain):
            rr = lax.rsqrt(_half_sums(x * x) * (1.0 / HEAD) + EPS)
            xh = x * rr
            dxg = _rope_t(dy, a, bm, bp)
            dxh = dxg * gain
            dx = rr * (dxh - xh * (_half_sums(dxh * xh) * (1.0 / HEAD)))
            return dx, jnp.sum(dxg * xh, axis=0, keepdims=True)

        def folded(cur_ref, prev_ref, m):
            parts = []
            for h in (2 * m, 2 * m + 1):
                v = cur_ref[:, LANES * h:LANES * (h + 1)] + jnp.where(
                    last, 0.0, prev_ref[:, LANES * h:LANES * (h + 1)])
                parts.append(v + pltpu.roll(v, HEAD, 1))
            return jnp.where(lo, parts[0], parts[1])

        gq_acc = jnp.zeros((1, LANES), F32)
        for r in range(ATTN_W // LANES):
            rc = slice(LANES * r, LANES * (r + 1))
            dx, gg = norm_bwd(z_ref[:, rc], dq_ref[:, rc], gq_ref[...])
            dz_ref[:, rc] = dx.astype(BF16)
            gq_acc = gq_acc + gg
        acc_ref[0:1, :] += gq_acc
        gk_acc = jnp.zeros((1, LANES), F32)
        for m in range(KV_W // LANES):
            kc = slice(ATTN_W + LANES * m, ATTN_W + LANES * (m + 1))
            dx, gg = norm_bwd(z_ref[:, kc], folded(dkc_ref, dkp_ref, m), gk_ref[...])
            dz_ref[:, kc] = dx.astype(BF16)
            gk_acc = gk_acc + gg
            vc = slice(ATTN_W + KV_W + LANES * m, ATTN_W + KV_W + LANES * (m + 1))
            dz_ref[:, vc] = folded(dvc_ref, dvp_ref, m).astype(BF16)
        acc_ref[1:2, :] += gk_acc
        dzt_ref[...] = dz_ref[...].T

    cur = lambda w: pl.BlockSpec((BLK, w), lambda n: (n, 0))
    nxt = lambda w: pl.BlockSpec((BLK, w), lambda n: (jnp.minimum(n + 1, nb - 1), 0))
    one = pl.BlockSpec((1, LANES), lambda n: (0, 0))
    return pl.pallas_call(
        body, name="qkv_bwd",
        out_shape=(jax.ShapeDtypeStruct(dz.shape, dz.dtype), jax.ShapeDtypeStruct(dzt.shape, dzt.dtype),
                   jax.ShapeDtypeStruct((SUBLANES, LANES), F32)),
        grid=(nb,),
        in_specs=[cur(PAIR_W), ANY, ANY, cur(ATTN_W), cur(K2_W), nxt(K2_W), cur(K2_W), nxt(K2_W),
                  cur(LANES), cur(LANES), cur(LANES), one, one],
        out_specs=(cur(QKV_W), pl.BlockSpec((QKV_W, BLK), lambda n: (0, n)),
                   pl.BlockSpec((SUBLANES, LANES), lambda n: (0, 0))),
        input_output_aliases={1: 0, 2: 1},
        compiler_params=_params("arbitrary"))(z, dz, dzt, dq, dkc, dkp, dvc, dvp, ra, rbm, rbp, gq2, gk2)


def _in_bwd(dz, w_pairs, x, dx1, g1, tm, after):
    s = x.shape[0]
    n = s // tm
    sub = tm // N_PAIRS
    stripes = 4

    def body(d_ref, w_ref, x_ref, dx1_ref, g_ref, after_ref, gx_ref, acc_ref, dh_ref):
        i, k = pl.program_id(0), pl.program_id(1)

        def matmul(c):
            cols = slice(c * (D_MODEL // stripes), (c + 1) * (D_MODEL // stripes))
            dh_ref[i % 2, :, cols] += _dot(d_ref[...], w_ref[0, :, cols])

        def norm_bwd(c):
            part = sub // stripes
            mine = slice(c * part, (c + 1) * part)
            rows = pl.ds(pl.multiple_of(k * sub + c * part, part), part)
            dh = dh_ref[(i + 1) % 2, rows, :]
            dh_ref[(i + 1) % 2, rows, :] = jnp.zeros_like(dh)
            xn, r = _rms(x_ref[mine, :])
            gx_ref[rows, :] = dx1_ref[mine, :] + _rms_bwd(dh * g_ref[...], xn, r)
            acc_ref[0:1, :] += jnp.sum(dh * xn, axis=0, keepdims=True)

        @pl.when((i == 0) & (k == 0))
        def _():
            acc_ref[...] = jnp.zeros_like(acc_ref)
            dh_ref[...] = jnp.zeros_like(dh_ref)

        @pl.when(i == 0)
        def _():
            for c in range(stripes):
                matmul(c)

        @pl.when((i > 0) & (i < n))
        def _():
            for c in range(stripes):
                matmul(c)
                norm_bwd(c)

        @pl.when(i == n)
        def _():
            for c in range(stripes):
                norm_bwd(c)

    last = lambda i, k: jnp.where(i == n, N_PAIRS - 1, k)
    rows_before = lambda i, k: (jnp.maximum(i - 1, 0) * N_PAIRS + k, 0)
    return pl.pallas_call(
        body, name="in_bwd",
        out_shape=(jax.ShapeDtypeStruct((s, D_MODEL), F32), jax.ShapeDtypeStruct((SUBLANES, D_MODEL), F32)),
        grid=(n + 1, N_PAIRS),
        in_specs=[pl.BlockSpec((tm, PAIR_W), lambda i, k: (jnp.minimum(i, n - 1), last(i, k))),
                  pl.BlockSpec((1, PAIR_W, D_MODEL), lambda i, k: (last(i, k), 0, 0)),
                  pl.BlockSpec((sub, D_MODEL), rows_before), pl.BlockSpec((sub, D_MODEL), rows_before),
                  pl.BlockSpec((1, D_MODEL), lambda i, k: (0, 0)), ANY],
        out_specs=(pl.BlockSpec((tm, D_MODEL), lambda i, k: (jnp.maximum(i - 1, 0), 0)),
                   pl.BlockSpec((SUBLANES, D_MODEL), lambda i, k: (0, 0))),
        scratch_shapes=[pltpu.VMEM((2, tm, D_MODEL), F32)],
        compiler_params=_params("arbitrary", "arbitrary"))(dz, w_pairs, x, dx1, g1, after)


def _mm_grad(at, bs, tn, name):
    m, kdim = at.shape
    nblk = [b.shape[1] // tn for b in bs]
    starts = [sum(nblk[:t]) for t in range(len(bs))]

    def body(a_ref, *refs):
        b_refs, o_ref = refs[:len(bs)], refs[len(bs)]
        j = pl.program_id(0)
        for t, b_ref in enumerate(b_refs):
            @pl.when((j >= starts[t]) & (j < starts[t] + nblk[t]))
            def _():
                o_ref[...] = _dot(a_ref[...], b_ref[...]).astype(BF16)

    def b_spec(t):
        return pl.BlockSpec((kdim, tn), lambda j: (0, jnp.clip(j - starts[t], 0, nblk[t] - 1)))

    return pl.pallas_call(
        body, name=name,
        out_shape=jax.ShapeDtypeStruct((m, sum(nblk) * tn), BF16),
        grid=(sum(nblk),),
        in_specs=[_resident((m, kdim))] + [b_spec(t) for t in range(len(bs))],
        out_specs=pl.BlockSpec((m, tn), lambda j: (0, j)),
        compiler_params=_params("parallel"))(at, *bs)


def _grad_w_in(dzt, h):
    kdim = h.shape[0]

    def body(d_ref, h_ref, o_ref):
        o_ref[0] = _dot(d_ref[...], h_ref[...]).astype(BF16)

    return pl.pallas_call(
        body, name="grad_w_in",
        out_shape=jax.ShapeDtypeStruct((N_DEV, SHARD_IN, D_MODEL), BF16),
        grid=(N_DEV,),
        in_specs=[pl.BlockSpec((SHARD_IN, kdim), lambda j: (j, 0)), _resident((kdim, D_MODEL))],
        out_specs=pl.BlockSpec((1, SHARD_IN, D_MODEL), lambda j: (j, 0, 0)),
        compiler_params=_params("parallel"))(dzt, h)


def _place():
    return lax.axis_index("x"), lax.axis_index("y"), lax.axis_index("c")


ROW_TAPS, ROW_MISC = 4, 5
Q_AT, K_AT, SINK_AT, LOSS_AT = (ATTN_W + LANES * t for t in range(4))
SMALL_AT = [(0, 0), (1, 0), (2, 0), (3, 0), (ROW_MISC, Q_AT), (ROW_MISC, K_AT), (ROW_MISC, SINK_AT)]


def _tap_at(tap):
    return ROW_TAPS + tap // 2, ATTN_W * (tap % 2)


def _reduce_small(acc_g1, acc_g2, acc_ple, acc_qk, acc_attn):
    def body(g1_ref, g2_ref, ple_ref, qk_ref, attn_ref, out_ref, slab_ref, gath_ref, send_sems, recv_sems):
        x, y, c = _place()
        me = 4 * x + 2 * y + c
        slab_ref[...] = jnp.zeros_like(slab_ref)
        slab_ref[0:1, :] = g1_ref[0:1, :]
        slab_ref[1:2, :] = g2_ref[0:1, :]
        slab_ref[2:4, :] = ple_ref[0:2, :]
        qk = qk_ref[0:2, :]
        qk = jnp.where(_low_half(qk.shape), qk + pltpu.roll(qk, HEAD, 1), 0.0)
        misc = slab_ref.at[ROW_MISC:ROW_MISC + 1]
        misc[:, Q_AT:Q_AT + LANES] = qk[0:1]
        misc[:, K_AT:K_AT + LANES] = qk[1:2]
        lane = lax.broadcasted_iota(jnp.int32, (1, LANES), 1)
        misc[:, SINK_AT:SINK_AT + LANES] = jnp.where(lane < N_Q_HEADS, attn_ref[0:1, 0:LANES], 0.0)
        misc[:, LOSS_AT:LOSS_AT + LANES] = sum(
            ple_ref[2:3, LANES * t:LANES * (t + 1)] for t in range(D_MODEL // LANES))
        for tap in range(3):
            row, at = _tap_at(tap)
            slab_ref[row:row + 1, at:at + ATTN_W] = attn_ref[1 + tap:2 + tap, :]
        gath_ref[me] = slab_ref[...]
        copies = []
        for k in range(1, N_DEV):
            peer = (x ^ (k >> 2), y ^ ((k >> 1) & 1), c ^ (k & 1))
            copies.append(pltpu.make_async_remote_copy(
                src_ref=slab_ref, dst_ref=gath_ref.at[me], send_sem=send_sems.at[k - 1],
                recv_sem=recv_sems.at[k - 1], device_id=peer, device_id_type=MESH))
        for cp in copies:
            cp.start()
        for cp in copies:
            cp.wait_recv()
        for cp in copies:
            cp.wait_send()
        total = gath_ref[0]
        for d in range(1, N_DEV):
            total = total + gath_ref[d]
        out_ref[...] = total

    vmem = pl.BlockSpec(memory_space=pltpu.VMEM)
    return pl.pallas_call(
        body, name="reduce_small",
        out_shape=jax.ShapeDtypeStruct((SLAB_ROWS, D_MODEL), F32),
        in_specs=[vmem] * 5, out_specs=vmem,
        scratch_shapes=[pltpu.VMEM((SLAB_ROWS, D_MODEL), F32), pltpu.VMEM((N_DEV, SLAB_ROWS, D_MODEL), F32),
                        pltpu.SemaphoreType.DMA((N_DEV - 1,)), pltpu.SemaphoreType.DMA((N_DEV - 1,))])(
            acc_g1, acc_g2, acc_ple, acc_qk, acc_attn)


def _pair_sum(g, r, place, tr, name):
    _, _, rows, cols = g.shape

    def body(place_ref, g_ref, r_ref, pb_ref, own_ref):
        tot = g_ref[0, 0].astype(F32) + r_ref[0].astype(F32)
        pb_ref[0] = tot.astype(BF16)

        @pl.when(pl.program_id(1) == place_ref[1])
        def _():
            own_ref[...] = tot

    grid_spec = pltpu.PrefetchScalarGridSpec(
        num_scalar_prefetch=1, grid=(rows // tr, 4),
        in_specs=[pl.BlockSpec((1, 1, tr, cols), lambda i, q, place_ref: (q, place_ref[0], i, 0)),
                  pl.BlockSpec((1, tr, cols), lambda i, q, place_ref: (q, i, 0))],
        out_specs=(pl.BlockSpec((1, tr, cols), lambda i, q, place_ref: (q, i, 0)),
                   pl.BlockSpec((tr, cols), lambda i, q, place_ref: (i, 0))))
    return pl.pallas_call(
        body, name=name, grid_spec=grid_spec,
        out_shape=(jax.ShapeDtypeStruct((4, rows, cols), BF16), jax.ShapeDtypeStruct((rows, cols), F32)),
        compiler_params=_params("arbitrary", "arbitrary"))(place, g, r)


HBM = pl.BlockSpec(memory_space=pltpu.HBM)
SEM = pl.BlockSpec(memory_space=pltpu.SEMAPHORE)
SIDE_EFFECT = pltpu.CompilerParams(has_side_effects=pltpu.SideEffectType.DATAFLOW_SIDE_EFFECTING)
TOKEN = jax.ShapeDtypeStruct((SUBLANES, LANES), F32)


def _hbm(a):
    return pltpu.with_memory_space_constraint(a, pltpu.HBM)


def _hbm_like(arrays):
    return tuple(pltpu.HBM(a.shape, a.dtype) for a in arrays)


def _block_of(px, py, pc):
    return 4 * px + 2 * py + pc


def _relay_parts(rows):
    if rows % (2 * PACKED_ROWS):
        return [pl.ds(0, rows), None]
    return [pl.ds(0, rows // 2), pl.ds(rows // 2, rows // 2)]


def _gather_start(shards, after, relay=False):
    na = len(shards)
    lands = [_hbm(lax.empty((N_DEV,) + a.shape, a.dtype)) for a in shards]

    def body(*refs):
        ins, land = refs[:na], refs[na:2 * na]
        send_sems, recv_ici, recv_d2d = refs[2 * na + 1:2 * na + 4]
        token = refs[-1]
        x, y, c = _place()
        peers = [(x, y, 1 - c), (1 - x, y, c), (x, 1 - y, c), (1 - x, 1 - y, c)]
        for k, peer in enumerate(peers[:3] if relay else peers):
            for t in range(na):
                pltpu.make_async_remote_copy(
                    src_ref=ins[t], dst_ref=land[t].at[_block_of(x, y, c)], send_sem=send_sems.at[4 * t + k],
                    recv_sem=recv_d2d.at[4 * t] if k == 0 else recv_ici.at[3 * t + k - 1],
                    device_id=peer, device_id_type=MESH).start()
        token[...] = jnp.zeros_like(token)

    out = pl.pallas_call(
        body, name="gather_start",
        out_shape=(pltpu.SemaphoreType.DMA((4 * na,)), pltpu.SemaphoreType.DMA((3 * na,)),
                   pltpu.SemaphoreType.DMA((4 * na,)), pltpu.SemaphoreType.DMA((2 * na,)), *_hbm_like(lands), TOKEN),
        in_specs=[ANY] * na + [HBM] * na + [ANY],
        out_specs=(SEM, SEM, SEM, SEM, *[HBM] * na, pl.BlockSpec(memory_space=pltpu.VMEM)),
        input_output_aliases={na + i: 4 + i for i in range(na)},
        compiler_params=SIDE_EFFECT)(*shards, *lands, after)
    send_sems, recv_ici, recv_d2d, recv_relay = out[:4]
    state = dict(send=send_sems, ici=recv_ici, d2d=recv_d2d, relay=recv_relay, relayed=relay, shards=list(shards),
                 lands=out[4:4 + na])
    return state, out[-1]


def _gather_forward(state, after):
    lands = state["lands"]
    na = len(lands)

    def body(*refs):
        land = refs[:na]
        recv_ici, recv_d2d = refs[na], refs[na + 1]
        fwd_sems, token = refs[-2], refs[-1]
        x, y, c = _place()
        for j, chip in enumerate([(1 - x, y), (x, 1 - y), (1 - x, 1 - y)]):
            for t in range(na):
                blk = land[t].at[_block_of(*chip, c)]
                pltpu.make_async_remote_copy(
                    src_ref=blk, dst_ref=blk, send_sem=fwd_sems.at[3 * t + j], recv_sem=recv_ici.at[3 * t + j],
                    device_id=(x, y, c), device_id_type=MESH).wait_recv()
                pltpu.make_async_remote_copy(
                    src_ref=blk, dst_ref=blk, send_sem=fwd_sems.at[3 * t + j], recv_sem=recv_d2d.at[4 * t + 1 + j],
                    device_id=(x, y, 1 - c), device_id_type=MESH).start()
        token[...] = jnp.zeros_like(token)

    out = pl.pallas_call(
        body, name="gather_forward",
        out_shape=(*_hbm_like(lands), pltpu.SemaphoreType.DMA((3 * na,)), TOKEN),
        in_specs=[HBM] * na + [SEM, SEM, ANY],
        out_specs=(*[HBM] * na, SEM, pl.BlockSpec(memory_space=pltpu.VMEM)),
        input_output_aliases={i: i for i in range(na)},
        compiler_params=SIDE_EFFECT)(*lands, state["ici"], state["d2d"], after)
    return dict(state, lands=out[:na], fwd=out[na]), out[-1]


def _gather_wait(state, after):
    shards, lands = state["shards"], state["lands"]
    na = len(lands)

    def body(*refs):
        ins, land = refs[:na], refs[na:2 * na]
        send_sems, fwd_sems, recv_d2d = refs[2 * na:2 * na + 3]
        x, y, c = _place()
        chips = [(1 - x, y), (x, 1 - y), (1 - x, 1 - y)]
        for t in range(na):
            mine = land[t].at[_block_of(x, y, c)]
            for k in range(4):
                pltpu.make_async_remote_copy(
                    src_ref=ins[t], dst_ref=mine, send_sem=send_sems.at[4 * t + k], recv_sem=recv_d2d.at[4 * t],
                    device_id=(x, y, c), device_id_type=MESH).wait_send()
            for j, chip in enumerate(chips):
                blk = land[t].at[_block_of(*chip, c)]
                pltpu.make_async_remote_copy(
                    src_ref=blk, dst_ref=blk, send_sem=fwd_sems.at[3 * t + j], recv_sem=recv_d2d.at[4 * t + 1 + j],
                    device_id=(x, y, c), device_id_type=MESH).wait_send()
            for k, blk_id in enumerate([_block_of(x, y, 1 - c)] + [_block_of(*chip, 1 - c) for chip in chips]):
                blk = land[t].at[blk_id]
                pltpu.make_async_remote_copy(
                    src_ref=blk, dst_ref=blk, send_sem=send_sems.at[4 * t], recv_sem=recv_d2d.at[4 * t + k],
                    device_id=(x, y, c), device_id_type=MESH).wait_recv()

    out = pl.pallas_call(
        body, name="gather_wait",
        out_shape=_hbm_like(lands),
        in_specs=[ANY] * na + [HBM] * na + [SEM, SEM, SEM, ANY],
        out_specs=tuple([HBM] * na),
        input_output_aliases={na + i: i for i in range(na)},
        compiler_params=SIDE_EFFECT)(*shards, *lands, state["send"], state["fwd"], state["d2d"], after)
    return out


def _gather_from_sibling(state, after):
    lands = state["lands"]
    na = len(lands)

    def body(*refs):
        land, recv_d2d = refs[:na], refs[na]
        x, y, c = _place()
        for t in range(na):
            blk = land[t].at[_block_of(x, y, 1 - c)]
            pltpu.make_async_remote_copy(src_ref=blk, dst_ref=blk, send_sem=recv_d2d.at[4 * t],
                                         recv_sem=recv_d2d.at[4 * t], device_id=(x, y, c),
                                         device_id_type=MESH).wait_recv()

    out = pl.pallas_call(
        body, name="gather_from_sibling", out_shape=_hbm_like(lands),
        in_specs=[HBM] * na + [SEM, ANY], out_specs=tuple([HBM] * na),
        input_output_aliases={i: i for i in range(na)},
        compiler_params=SIDE_EFFECT)(*lands, state["d2d"], after)
    return dict(state, lands=list(out))


def _gather_from_chip(state, j, afters, last):
    shards, lands, relayed = state["shards"], state["lands"], state["relayed"]
    na = len(lands)
    parts = [_relay_parts(a.shape[0]) for a in shards]

    def relay_on(land_ref, t, nb, fwd_sems, recv_relay):
        x, y, c = _place()
        blk = chip_blocks(land_ref, nb)[1].at[parts[t][nb]]
        return pltpu.make_async_remote_copy(
            src_ref=blk, dst_ref=blk, send_sem=fwd_sems.at[na + t], recv_sem=recv_relay.at[2 * t + nb],
            device_id=[(x, 1 - y, c), (1 - x, y, c)][nb], device_id_type=MESH)

    def chip_blocks(land_ref, which=j):
        x, y, c = _place()
        chip = [(1 - x, y), (x, 1 - y), (1 - x, 1 - y)][which]
        return (x, y, c), land_ref.at[_block_of(*chip, c)], land_ref.at[_block_of(*chip, 1 - c)]

    def forward(*refs):
        land, recv_ici, recv_d2d, recv_relay, fwd_sems = refs[:na], refs[na], refs[na + 1], refs[na + 2], refs[-1]
        for t in range(na):
            (x, y, c), mine, _ = chip_blocks(land[t])
            if relayed and j == 2:
                for half, rows in enumerate(parts[t]):
                    if rows is not None:
                        pltpu.make_async_remote_copy(
                            src_ref=mine.at[rows], dst_ref=mine.at[rows], send_sem=fwd_sems.at[t],
                            recv_sem=recv_relay.at[2 * t + half], device_id=(x, y, c),
                            device_id_type=MESH).wait_recv()
            else:
                pltpu.make_async_remote_copy(src_ref=mine, dst_ref=mine, send_sem=fwd_sems.at[t],
                                             recv_sem=recv_ici.at[3 * t + j], device_id=(x, y, c),
                                             device_id_type=MESH).wait_recv()
            pltpu.make_async_remote_copy(src_ref=mine, dst_ref=mine, send_sem=fwd_sems.at[t],
                                         recv_sem=recv_d2d.at[4 * t + 1 + j], device_id=(x, y, 1 - c),
                                         device_id_type=MESH).start()
            if relayed and j < 2 and parts[t][j] is not None:
                relay_on(land[t], t, j, fwd_sems, recv_relay).start()

    out = pl.pallas_call(
        forward, name="gather_pass_chip_" + str(j),
        out_shape=(*_hbm_like(lands), pltpu.SemaphoreType.DMA((2 * na,))),
        in_specs=[HBM] * na + [SEM, SEM, SEM] + [ANY] * len(afters), out_specs=(*[HBM] * na, SEM),
        input_output_aliases={i: i for i in range(na)},
        compiler_params=SIDE_EFFECT)(*lands, state["ici"], state["d2d"], state["relay"], *afters)
    passed_sems = out[na]
    relays = state.get("relays", []) + ([passed_sems] if relayed and j < 2 else [])
    waited = relays if last else []

    def arrive(*refs):
        land, fwd_sems, recv_d2d = refs[:na], refs[na], refs[na + 1]
        shard, send_sems, recv_relay = refs[na + 2:2 * na + 2], refs[2 * na + 2], refs[2 * na + 3]
        for nb, relay_sems in enumerate(refs[2 * na + 4:2 * na + 4 + len(waited)]):
            for t in range(na):
                if parts[t][nb] is not None:
                    relay_on(land[t], t, nb, relay_sems, recv_relay).wait_send()
        for t in range(na):
            (x, y, c), mine, theirs = chip_blocks(land[t])
            pltpu.make_async_remote_copy(src_ref=theirs, dst_ref=theirs, send_sem=fwd_sems.at[t],
                                         recv_sem=recv_d2d.at[4 * t + 1 + j], device_id=(x, y, c),
                                         device_id_type=MESH).wait_recv()
            pltpu.make_async_remote_copy(src_ref=mine, dst_ref=mine, send_sem=fwd_sems.at[t],
                                         recv_sem=recv_d2d.at[4 * t + 1 + j], device_id=(x, y, c),
                                         device_id_type=MESH).wait_send()
            for k in range((3 if relayed else 4) if last else 0):
                pltpu.make_async_remote_copy(
                    src_ref=shard[t], dst_ref=land[t].at[_block_of(x, y, c)], send_sem=send_sems.at[4 * t + k],
                    recv_sem=recv_d2d.at[4 * t], device_id=(x, y, c), device_id_type=MESH).wait_send()

    def take(state, afters):
        taken = pl.pallas_call(
            arrive, name="gather_take_chip_" + str(j), out_shape=_hbm_like(lands),
            in_specs=[HBM] * na + [SEM, SEM] + [ANY] * na + [SEM, SEM] + [SEM] * len(waited) + [ANY] * len(afters),
            out_specs=tuple([HBM] * na), input_output_aliases={i: i for i in range(na)},
            compiler_params=SIDE_EFFECT)(*state["lands"], passed_sems, state["d2d"], *shards, state["send"],
                                         state["relay"], *waited, *afters)
        return dict(state, lands=list(taken))

    return dict(state, lands=list(out[:na]), relays=relays), take


def _to_sibling(srcs, lands, send_sems, recv_sems):
    x, y, c = _place()
    return [pltpu.make_async_remote_copy(
        src_ref=srcs[t].at[:, 1 - c], dst_ref=lands[t], send_sem=send_sems.at[t], recv_sem=recv_sems.at[t],
        device_id=(x, y, 1 - c), device_id_type=MESH) for t in range(len(srcs))]


def _to_chips(srcs, lands, send_sems, recv_sems):
    x, y, c = _place()
    copies = []
    for k in (1, 2, 3):
        px, py = x ^ (k >> 1), y ^ (k & 1)
        copies += [pltpu.make_async_remote_copy(
            src_ref=srcs[t].at[2 * px + py], dst_ref=lands[t].at[k - 1], send_sem=send_sems.at[3 * t + k - 1],
            recv_sem=recv_sems.at[3 * t + k - 1], device_id=(px, py, c), device_id_type=MESH) for t in range(len(srcs))]
    return copies


def _exchange_start(name, srcs, land_shapes, copies, per_array, after):
    na = len(srcs)
    lands = [_hbm(lax.empty(shp, a.dtype)) for shp, a in zip(land_shapes, srcs)]

    def body(*refs):
        token = refs[-1]
        for cp in copies(refs[:na], refs[na:2 * na], refs[2 * na + 1], refs[2 * na + 2]):
            cp.start()
        token[...] = jnp.zeros_like(token)

    out = pl.pallas_call(
        body, name=name,
        out_shape=(pltpu.SemaphoreType.DMA((na * per_array,)), pltpu.SemaphoreType.DMA((na * per_array,)),
                   *_hbm_like(lands), TOKEN),
        in_specs=[ANY] * na + [HBM] * na + [ANY],
        out_specs=(SEM, SEM, *[HBM] * na, pl.BlockSpec(memory_space=pltpu.VMEM)),
        input_output_aliases={na + i: 2 + i for i in range(na)},
        compiler_params=SIDE_EFFECT)(*srcs, *lands, after)
    return dict(send=out[0], recv=out[1], srcs=list(srcs), lands=out[2:2 + na]), out[-1]


def _exchange_wait(name, state, copies, afters):
    srcs, lands = state["srcs"], state["lands"]
    na = len(srcs)

    def body(*refs):
        for cp in copies(refs[:na], refs[na:2 * na], refs[2 * na], refs[2 * na + 1]):
            cp.wait_send()
            cp.wait_recv()

    out = pl.pallas_call(
        body, name=name,
        out_shape=_hbm_like(lands),
        in_specs=[ANY] * na + [HBM] * na + [SEM, SEM] + [ANY] * len(afters),
        out_specs=tuple([HBM] * na),
        input_output_aliases={na + i: i for i in range(na)},
        compiler_params=SIDE_EFFECT)(*srcs, *lands, state["send"], state["recv"], *afters)
    return out


def _adamw_math(w, g, m, v):
    m = ADAM_B1 * m + (1.0 - ADAM_B1) * g
    v = ADAM_B2 * v + (1.0 - ADAM_B2) * (g * g)
    m_hat = m / (1.0 - ADAM_B1 ** ADAM_STEP)
    v_hat = v / (1.0 - ADAM_B2 ** ADAM_STEP)
    return -ADAM_LR * (m_hat / (jnp.sqrt(v_hat) + ADAM_EPS) + ADAM_WD * w), m, v


def _adamw(own, others, w, m, v, tr, name, after):
    rows, cols = w.shape
    blk = pl.BlockSpec((tr, cols), lambda i: (i, 0))

    def body(own_ref, oth_ref, w_ref, m_ref, v_ref, after_ref, g_ref, d_ref, nm_ref, nv_ref):
        g = own_ref[...]
        for k in range(3):
            g = g + oth_ref[k].astype(F32)
        g_ref[...] = g
        d_ref[...], nm_ref[...], nv_ref[...] = _adamw_math(w_ref[...], g, m_ref[...], v_ref[...])

    out = jax.ShapeDtypeStruct((rows, cols), F32)
    return pl.pallas_call(
        body, name=name, out_shape=(out, out, out, out), grid=(rows // tr,),
        in_specs=[blk, pl.BlockSpec((3, tr, cols), lambda i: (0, i, 0)), blk, blk, blk, ANY],
        out_specs=(blk, blk, blk, blk),
        compiler_params=_params("parallel"))(own, others, w, m, v, after)


def _adamw_small(red, me, params, moments1, moments2):
    n = len(params)

    def body(me_ref, red_ref, *refs):
        ws, ms, vs = refs[:n], refs[n:2 * n], refs[2 * n:3 * n]
        loss_ref = refs[3 * n]
        outs = refs[3 * n + 1:]
        loss_ref[...] = jnp.sum(red_ref[ROW_MISC:ROW_MISC + 1, LOSS_AT:LOSS_AT + LANES], axis=-1, keepdims=True)
        for t, (row, at) in enumerate(SMALL_AT):
            g = red_ref[row:row + 1, at:at + ws[t].shape[1]]
            d, nm, nv = _adamw_math(ws[t][...], g, ms[t][...], vs[t][...])
            for o, val in zip(outs[4 * t:4 * t + 4], (g, d, nm, nv)):
                o[...] = val
        for tap in range(ws[-1].shape[0]):
            row, at = _tap_at(tap)
            g = red_ref[row:row + 1, pl.ds(pl.multiple_of(at + me_ref[0, 0] * LANES, LANES), LANES)]
            d, nm, nv = _adamw_math(ws[-1][tap], g, ms[-1][tap], vs[-1][tap])
            for o, val in zip(outs[4 * (n - 1):], (g, d, nm, nv)):
                o[tap] = val

    vmem = pl.BlockSpec(memory_space=pltpu.VMEM)
    shapes = [jax.ShapeDtypeStruct(w.shape, F32) for w in params for _ in range(4)]
    out = pl.pallas_call(
        body, name="adamw_small", out_shape=(jax.ShapeDtypeStruct((1, 1), F32), *shapes),
        in_specs=[pl.BlockSpec(memory_space=pltpu.SMEM), vmem] + [vmem] * (3 * n),
        out_specs=tuple([vmem] * (1 + 4 * n)))(me, red, *params, *moments1, *moments2)
    return out[0], [list(out[1 + k::4]) for k in range(4)]


def _tables(gq, gk, conv_w):
    gq2 = jnp.tile(gq.reshape(1, HEAD), (1, 2))
    gk2 = jnp.tile(gk.reshape(1, HEAD), (1, 2))
    conv_wp = jnp.pad(conv_w, ((0, SUBLANES - conv_w.shape[0]), (0, 0)))
    return gq2, gk2, conv_wp


def _pair_id(q):
    return jnp.array([q, 0], jnp.int32)


def _forward_in(x, g1, shards):
    s = x.shape[0]
    h = _prenorm(x, g1, min(512, s), x)
    z, w_pairs = lax.empty((s, IN_W), F32), lax.empty((N_PAIRS, PAIR_W, D_MODEL), BF16)
    for q in range(N_PAIRS):
        z, w_pairs = _fwd_in_pair(h, shards, z, w_pairs, _pair_id(q), min(512, s), "fwd_in_" + str(q),
                                  own=shards[0] if q == 0 else None)
    return h, z, w_pairs


def _forward_attn(z, rope, gq2, gk2, conv_wp, sinks):
    s = z.shape[0]
    qn, k2, v2 = _qk_prep(z, *rope, gq2, gk2, min(256, s), z)
    a, mix, mixt = _attn_fwd(qn, k2, v2, z, conv_wp, sinks, qn)
    return qn, k2, v2, a, mix, mixt


def _forward_out(x, p, target, mix, mixt, w_out, g2, w_pg, b_pg, w_pp, g3):
    s = x.shape[0]
    tm = min(512, s)
    x1, hn2, hn2t = _fwd_out(mix, w_out, x, g2, tm)
    dy, dgp, dt, pt, acc_ple = _ple(hn2, w_pg, b_pg, p, w_pp, g3, x1, target, min(256, s))
    dx1, dx1b, acc_g2 = _gate_bwd(dgp, w_pg, x1, dy, g2, tm)
    gw_out = _mm_grad(mixt, [dx1b], 512, "grad_w_out")
    gw_pg = _mm_grad(hn2t, [dgp], 512, "grad_w_ple_gate")
    gw_pp = _mm_grad(pt, [dt], 512, "grad_w_ple_proj")
    return dx1, dx1b, (gw_out, gw_pg, gw_pp), acc_ple, acc_g2


def _backward_attn(dmix, h, z, qn, k2, v2, a, rope, gq2, gk2, conv_wp, sinks, after):
    dq, dkc, dkp, dvc, dvp, dz, dzt, acc_attn = _attn_bwd(qn, k2, v2, a, z, dmix, conv_wp, sinks, after)
    dz, dzt, acc_qk = _qkv_bwd(z, dz, dzt, dq, dkc, dkp, dvc, dvp, *rope, gq2, gk2)
    return dz, _grad_w_in(dzt, h), acc_attn, acc_qk


def _local_step(x, p, target, g1, shards, gq, gk, sinks, conv_w, w_out, g2, w_pg, b_pg, w_pp, g3):
    rope, (gq2, gk2, conv_wp) = _rope_tables(x.shape[0]), _tables(gq, gk, conv_w)
    h, z, w_pairs = _forward_in(x, g1, shards)
    qn, k2, v2, a, mix, mixt = _forward_attn(z, rope, gq2, gk2, conv_wp, sinks)
    dx1, dx1b, (gw_out, gw_pg, gw_pp), acc_ple, acc_g2 = _forward_out(
        x, p, target, mix, mixt, w_out, g2, w_pg, b_pg, w_pp, g3)
    dmix = _mm_nt(dx1b, w_out, min(512, x.shape[0]), "out_bwd", dx1b)
    dz, gw_in, acc_attn, acc_qk = _backward_attn(dmix, h, z, qn, k2, v2, a, rope, gq2, gk2, conv_wp, sinks, dmix)
    grad_x, acc_g1 = _in_bwd(dz, w_pairs, x, dx1, g1, min(512, x.shape[0]), dx1)
    return grad_x, (gw_in, gw_out, gw_pg, gw_pp), (acc_g1, acc_g2, acc_ple, acc_qk, acc_attn)


def _by_owner(g):
    return g.reshape((4, 2) + g.shape[1:])


def kernel(x, p, norm_gain, w_in, q_norm_gain, k_norm_gain, attn_sinks, conv_w, w_out, ple_gate_norm_gain, w_ple_gate, b_ple_gate, w_ple_proj, ple_norm_gain, loss_target, m_norm_gain, m_w_in, m_q_norm_gain, m_k_norm_gain, m_attn_sinks, m_conv_w, m_w_out, m_ple_gate_norm_gain, m_w_ple_gate, m_b_ple_gate, m_w_ple_proj, m_ple_norm_gain, v_norm_gain, v_w_in, v_q_norm_gain, v_k_norm_gain, v_attn_sinks, v_conv_w, v_w_out, v_ple_gate_norm_gain, v_w_ple_gate, v_b_ple_gate, v_w_ple_proj, v_ple_norm_gain):
    me = 4 * lax.axis_index("x") + 2 * lax.axis_index("y") + lax.axis_index("c")
    place = jnp.stack([lax.axis_index("c"), 2 * lax.axis_index("x") + lax.axis_index("y")]).astype(jnp.int32)
    xs, ps, target = x[0], p[0, 0], loss_target[0]

    shard_in = w_in[0].T.astype(BF16)
    own_late = [w_out[0].astype(BF16), w_ple_gate[0].astype(BF16), w_ple_proj[0].astype(BF16)]
    with_own = lambda gathered, own: lax.dynamic_update_slice(gathered, own[None], (me,) + (0,) * own.ndim)
    early, started = _gather_start([shard_in, conv_w[0]], shard_in, relay=True)
    tm = min(512, xs.shape[0])
    h = _prenorm(xs, norm_gain, tm, started)

    z, w_pairs = lax.empty((xs.shape[0], IN_W), F32), lax.empty((N_PAIRS, PAIR_W, D_MODEL), BF16)
    early = _gather_from_sibling(early, h)
    pair_of = lambda flip: jnp.stack([place[1] ^ flip, place[0]])
    z, w_pairs = _fwd_in_pair(h, early["lands"][0], z, w_pairs, pair_of(0), tm, "fwd_in_own", own=shard_in)
    rope = _rope_tables(xs.shape[0])
    early, take = _gather_from_chip(early, 0, (z, *rope, *own_late), last=False)
    for j, flip in enumerate((2, 1, 3)):
        if j < 2:
            early, take_next = _gather_from_chip(early, j + 1, (z,), last=j == 1)
        if j == 1:
            late, started_late = _gather_start(own_late, z)
        early = take(early, (z, started_late) if j == 1 else (z,))
        z, w_pairs = _fwd_in_pair(h, early["lands"][0], z, w_pairs, pair_of(flip), tm, "fwd_in_chip_" + str(j))
        take = take_next
    conv_full = jnp.transpose(with_own(early["lands"][1], conv_w[0]), (1, 0, 2)).reshape(3, ATTN_W)
    gq2, gk2, conv_wp = _tables(q_norm_gain[0], k_norm_gain[0], conv_full)
    qn, k2, v2 = _qk_prep(z, *rope, gq2, gk2, min(256, xs.shape[0]), z)
    late, forwarded = _gather_forward(late, qn)
    a, mix, mixt = _attn_fwd(qn, k2, v2, z, conv_wp, attn_sinks, forwarded)
    g_out, g_pg, g_pp = (with_own(g, own) for g, own in zip(_gather_wait(late, mix), own_late))
    w_out_f = g_out.reshape(D_MODEL, D_MODEL)
    w_pg_f = g_pg.reshape(D_MODEL, D_MODEL)
    w_pp_f = jnp.transpose(g_pp, (1, 0, 2)).reshape(PLE_DIM, D_MODEL)

    dx1, dx1b, (gw_out, gw_pg, gw_pp), acc_ple, acc_g2 = _forward_out(
        xs, ps, target, mix, mixt, w_out_f, ple_gate_norm_gain, w_pg_f, b_ple_gate, w_pp_f, ple_norm_gain)

    names = ("w_out", "w_ple_gate", "w_ple_proj")
    gw_pp_t = jnp.transpose(gw_pp.reshape(PLE_DIM, N_DEV, PLE_DIM), (1, 0, 2))
    grads = [_by_owner(gw_out.reshape(N_DEV, D_MODEL // N_DEV, D_MODEL)),
             _by_owner(gw_pg.reshape(N_DEV, D_MODEL // N_DEV, D_MODEL)), _by_owner(gw_pp_t)]
    pairs, paired = _exchange_start("pair_start", grads, [(4,) + g.shape[2:] for g in grads], _to_sibling, 1, dx1b)
    dmix = _mm_nt(dx1b, w_out_f, tm, "out_bwd", paired)
    from_sibling = _exchange_wait("pair_wait", pairs, _to_sibling, (dmix,))
    sums = [_pair_sum(g, r, place, 256, "pair_sum_" + nm) for g, r, nm in zip(pairs["srcs"], from_sibling, names)]
    chips, sent = _exchange_start("chip_start", [pb for pb, _ in sums], [(3,) + pb.shape[1:] for pb, _ in sums],
                                  _to_chips, 3, sums[-1][1])

    dz, gw_in, acc_attn, acc_qk = _backward_attn(
        dmix, h, z, qn, k2, v2, a, rope, gq2, gk2, conv_wp, attn_sinks, sent)

    gw_in_t = [_by_owner(gw_in)]
    pairs_in, paired_in = _exchange_start("pair_start_w_in", gw_in_t, [(4,) + gw_in_t[0].shape[2:]], _to_sibling, 1,
                                          gw_in)
    from_chips = _exchange_wait("chip_wait", chips, _to_chips, (gw_in,))
    big = {}
    for (_, own), oth, w, m, v, nm in zip(sums, from_chips, (w_out, w_ple_gate, w_ple_proj),
                                          (m_w_out, m_w_ple_gate, m_w_ple_proj),
                                          (v_w_out, v_w_ple_gate, v_w_ple_proj), names):
        big[nm] = [t[None] for t in _adamw(own, oth, w[0], m[0], v[0], 256, "adamw_" + nm, paired_in)]

    (from_sibling_in,) = _exchange_wait("pair_wait_w_in", pairs_in, _to_sibling, [big[nm][0] for nm in names])
    pb_in, own_in = _pair_sum(pairs_in["srcs"][0], from_sibling_in, place, SHARD_IN // 2, "pair_sum_w_in")
    chips_in, sent_in = _exchange_start("chip_start_w_in", [pb_in], [(3,) + pb_in.shape[1:]], _to_chips, 3, own_in)
    grad_x, acc_g1 = _in_bwd(dz, w_pairs, xs, dx1, norm_gain, tm, sent_in)
    (from_chips_in,) = _exchange_wait("chip_wait_w_in", chips_in, _to_chips, (grad_x,))
    big["w_in"] = [t.T[None] for t in _adamw(own_in, from_chips_in, w_in[0].T, m_w_in[0].T, v_w_in[0].T, SHARD_IN // 4,
                                             "adamw_w_in", grad_x)]

    red = _reduce_small(acc_g1, acc_g2, acc_ple, acc_qk, acc_attn)
    small = [norm_gain, ple_gate_norm_gain, b_ple_gate, ple_norm_gain, q_norm_gain, k_norm_gain, attn_sinks]
    small_m = [m_norm_gain, m_ple_gate_norm_gain, m_b_ple_gate, m_ple_norm_gain, m_q_norm_gain, m_k_norm_gain,
               m_attn_sinks]
    small_v = [v_norm_gain, v_ple_gate_norm_gain, v_b_ple_gate, v_ple_norm_gain, v_q_norm_gain, v_k_norm_gain,
               v_attn_sinks]
    taps_first = lambda t: jnp.transpose(t, (1, 0, 2))
    loss, kinds = _adamw_small(red, me.reshape(1, 1).astype(jnp.int32), small + [taps_first(conv_w)],
                               small_m + [taps_first(m_conv_w)], small_v + [taps_first(v_conv_w)])

    def order(k):
        sm = kinds[k]
        return [sm[0], big["w_in"][k], sm[4], sm[5], sm[6], taps_first(sm[7]), big["w_out"][k], sm[1],
                big["w_ple_gate"][k], sm[2], big["w_ple_proj"][k], sm[3]]

    return (loss[0, 0], grad_x[None], *order(0), *order(1), *order(2), *order(3))
```

```python
import jax
import jax.numpy as jnp
from jax import lax
from jax.experimental import pallas as pl
from jax.experimental.pallas import tpu as pltpu

F32, BF16 = jnp.float32, jnp.bfloat16

D_MODEL = 2048
PLE_DIM = 256
ATTN_W = 1024
HEAD = 64
N_Q_HEADS = 16
KV_W = 256
QKV_W = ATTN_W + 2 * KV_W
REST_W = 5 * 1024
IN_W = QKV_W + REST_W
GATE_A0, CONV_B0, CONV_C0, CONV_H0, GATE_C0 = (QKV_W + 1024 * t for t in range(5))
K2_W = 4 * 128
ROT = 16
ROPE_THETA = 500000.0
EPS = 1e-6
NEG_INF = -1e30
BLK = 128
LANES = 128
SUBLANES = 8
N_DEV = 8
SHARD_IN = IN_W // N_DEV
PAIR_W = 2 * SHARD_IN
N_PAIRS = IN_W // PAIR_W
SLAB_ROWS = 8
PACKED_ROWS = 16
V7X_VMEM_LIMIT = 52 * 1024 * 1024

ADAM_LR, ADAM_B1, ADAM_B2, ADAM_EPS, ADAM_WD, ADAM_STEP = 0.001, 0.9, 0.999, 1e-08, 0.01, 10
MESH = pl.DeviceIdType.MESH


def _params(*semantics):
    return pltpu.CompilerParams(dimension_semantics=semantics, vmem_limit_bytes=V7X_VMEM_LIMIT)


ANY = pl.BlockSpec(memory_space=pl.ANY)


def _resident(shape):
    return pl.BlockSpec(shape, lambda *_: (0,) * len(shape), pipeline_mode=pl.Buffered(1))


def _dot(a, b):
    return jnp.dot(a, b, preferred_element_type=F32)


def _dot_nt(a, b):
    return lax.dot_general(a, b, (((1,), (1,)), ((), ())), preferred_element_type=F32)


def _rms(xf):
    r = lax.rsqrt(jnp.mean(xf * xf, axis=-1, keepdims=True) + EPS)
    return xf * r, r


def _rms_bwd(dxn, xn, r):
    return r * (dxn - xn * jnp.mean(dxn * xn, axis=-1, keepdims=True))


def _sig(g):
    return jax.nn.sigmoid(g)


def _dsilu(g, sg):
    return sg * (1.0 + g * (1.0 - sg))


def _low_half(shape):
    return lax.broadcasted_iota(jnp.int32, shape, len(shape) - 1) < HEAD


def _half_sums(v):
    lo = _low_half(v.shape)
    s_lo = jnp.sum(jnp.where(lo, v, 0.0), axis=-1, keepdims=True)
    s_hi = jnp.sum(jnp.where(lo, 0.0, v), axis=-1, keepdims=True)
    return jnp.where(lo, s_lo, s_hi)


def _rope(v, a, bm, bp):
    return v * a + pltpu.roll(v, LANES - ROT // 2, 1) * bm + pltpu.roll(v, ROT // 2, 1) * bp


def _rope_t(dy, a, bm, bp):
    return dy * a + pltpu.roll(dy * bm, ROT // 2, 1) + pltpu.roll(dy * bp, LANES - ROT // 2, 1)


def _dup_halves(v):
    lo = _low_half(v.shape)
    a = jnp.where(lo, v, 0.0)
    b = jnp.where(lo, 0.0, v)
    return a + pltpu.roll(a, HEAD, 1), b + pltpu.roll(b, HEAD, 1)


def _rope_tables(s):
    half = ROT // 2
    lane = lax.broadcasted_iota(jnp.int32, (s, LANES), 1) % HEAD
    pos = lax.broadcasted_iota(jnp.int32, (half, s), 1).astype(F32)
    freq = lax.broadcasted_iota(jnp.int32, (half, s), 0).astype(F32)
    ang = pos * jnp.power(jnp.float32(ROPE_THETA), -freq * 2.0 / ROT)
    cos, sin = lax.optimization_barrier((jnp.cos(ang), jnp.sin(ang)))
    cos, sin = (jnp.tile(t.T, (1, LANES // half)) for t in (cos, sin))
    a = jnp.where(lane < ROT, cos, 1.0)
    bm = jnp.where(lane < half, -sin, 0.0)
    bp = jnp.where((lane >= half) & (lane < ROT), sin, 0.0)
    return a, bm, bp


def _prenorm(x, g1, tm, after):
    s = x.shape[0]

    def body(x_ref, g_ref, after_ref, h_ref):
        xn, _ = _rms(x_ref[...])
        h_ref[...] = (xn * g_ref[...]).astype(BF16)

    return pl.pallas_call(
        body, name="prenorm",
        out_shape=jax.ShapeDtypeStruct((s, D_MODEL), BF16),
        grid=(s // tm,),
        in_specs=[pl.BlockSpec((tm, D_MODEL), lambda i: (i, 0)), pl.BlockSpec((1, D_MODEL), lambda i: (0, 0)), ANY],
        out_specs=pl.BlockSpec((tm, D_MODEL), lambda i: (i, 0)),
        compiler_params=_params("parallel"))(x, g1, after)


def _fwd_in_pair(h, shards, z, w_pairs, pair, tm, name, own=None):
    s = h.shape[0]

    def body(pair_ref, h_ref, lo_ref, hi_ref, z_in, wp_in, z_ref, wp_ref):
        @pl.when(pl.program_id(0) == 0)
        def _():
            wp_ref[0, 0:SHARD_IN, :] = lo_ref[0]
            wp_ref[0, SHARD_IN:PAIR_W, :] = hi_ref[0]

        z_ref[...] = _dot_nt(h_ref[...], wp_ref[0])

    def body_own(pair_ref, h_ref, own_ref, other_ref, z_in, wp_in, z_ref, wp_ref):
        @pl.when(pl.program_id(0) == 0)
        def _():
            first = pl.multiple_of(pair_ref[1] * SHARD_IN, SHARD_IN)
            wp_ref[0, pl.ds(first, SHARD_IN), :] = own_ref[...]
            wp_ref[0, pl.ds(SHARD_IN - first, SHARD_IN), :] = other_ref[0]

        z_ref[...] = _dot_nt(h_ref[...], wp_ref[0])

    if own is None:
        blocks = [pl.BlockSpec((1, SHARD_IN, D_MODEL), lambda i, p: (2 * p[0], 0, 0)),
                  pl.BlockSpec((1, SHARD_IN, D_MODEL), lambda i, p: (2 * p[0] + 1, 0, 0))]
        operands = (shards, shards)
    else:
        blocks = [pl.BlockSpec((SHARD_IN, D_MODEL), lambda i, p: (0, 0)),
                  pl.BlockSpec((1, SHARD_IN, D_MODEL), lambda i, p: (2 * p[0] + 1 - p[1], 0, 0))]
        operands = (own, shards)
    grid_spec = pltpu.PrefetchScalarGridSpec(
        num_scalar_prefetch=1, grid=(s // tm,),
        in_specs=[pl.BlockSpec((tm, D_MODEL), lambda i, p: (i, 0)), *blocks, ANY, ANY],
        out_specs=(pl.BlockSpec((tm, PAIR_W), lambda i, p: (i, p[0])),
                   pl.BlockSpec((1, PAIR_W, D_MODEL), lambda i, p: (p[0], 0, 0))))
    return pl.pallas_call(
        body if own is None else body_own, name=name, grid_spec=grid_spec,
        out_shape=(jax.ShapeDtypeStruct(z.shape, z.dtype), jax.ShapeDtypeStruct(w_pairs.shape, w_pairs.dtype)),
        input_output_aliases={4: 0, 5: 1},
        compiler_params=_params("arbitrary"))(pair, h, *operands, z, w_pairs)


def _qk_prep(z, ra, rbm, rbp, gq2, gk2, tm, after):
    s = z.shape[0]

    def body(z_ref, a_ref, bm_ref, bp_ref, gq_ref, gk_ref, after_ref, q_ref, k2_ref, v2_ref):
        a, bm, bp = a_ref[...], bm_ref[...], bp_ref[...]
        for r in range(ATTN_W // LANES):
            x = z_ref[:, LANES * r:LANES * (r + 1)]
            rr = lax.rsqrt(_half_sums(x * x) * (1.0 / HEAD) + EPS)
            q_ref[:, LANES * r:LANES * (r + 1)] = _rope(x * rr * gq_ref[...], a, bm, bp).astype(BF16)
        for m in range(KV_W // LANES):
            x = z_ref[:, ATTN_W + LANES * m:ATTN_W + LANES * (m + 1)]
            rr = lax.rsqrt(_half_sums(x * x) * (1.0 / HEAD) + EPS)
            k_lo, k_hi = _dup_halves(_rope(x * rr * gk_ref[...], a, bm, bp))
            k2_ref[:, 2 * LANES * m:2 * LANES * m + LANES] = k_lo.astype(BF16)
            k2_ref[:, 2 * LANES * m + LANES:2 * LANES * (m + 1)] = k_hi.astype(BF16)
            v_lo, v_hi = _dup_halves(z_ref[:, ATTN_W + KV_W + LANES * m:ATTN_W + KV_W + LANES * (m + 1)])
            v2_ref[:, 2 * LANES * m:2 * LANES * m + LANES] = v_lo.astype(BF16)
            v2_ref[:, 2 * LANES * m + LANES:2 * LANES * (m + 1)] = v_hi.astype(BF16)

    row = lambda w: pl.BlockSpec((tm, w), lambda i: (i, 0))
    one = pl.BlockSpec((1, LANES), lambda i: (0, 0))
    return pl.pallas_call(
        body, name="qk_prep",
        out_shape=(jax.ShapeDtypeStruct((s, ATTN_W), BF16), jax.ShapeDtypeStruct((s, K2_W), BF16),
                   jax.ShapeDtypeStruct((s, K2_W), BF16)),
        grid=(s // tm,),
        in_specs=[row(PAIR_W), row(LANES), row(LANES), row(LANES), one, one, ANY],
        out_specs=(row(ATTN_W), row(K2_W), row(K2_W)),
        compiler_params=_params("parallel"))(z, ra, rbm, rbp, gq2, gk2, after)


GROUP = 4


def _window_mask(n):
    row = lax.broadcasted_iota(jnp.int32, (GROUP * BLK, 2 * BLK), 0) % BLK
    col = lax.broadcasted_iota(jnp.int32, (GROUP * BLK, 2 * BLK), 1)
    return (col > row) & (col <= row + BLK) & ((col >= BLK) | (n > 0))


def _stack_heads(pairs, zero):
    lo = _low_half(pairs[0].shape)
    parts = []
    for v in pairs:
        parts += [jnp.where(lo, v, zero), jnp.where(lo, zero, v)]
    return jnp.concatenate(parts, axis=0)


def _unstack_heads(v4):
    lo = _low_half((BLK, LANES))
    return [jnp.where(lo, v4[2 * i * BLK:(2 * i + 1) * BLK], v4[(2 * i + 1) * BLK:(2 * i + 2) * BLK]) for i in range(2)]


def _group_sinks(sink_ref, kvh):
    slot = lax.broadcasted_iota(jnp.int32, (GROUP * BLK, 1), 0) // BLK
    col = jnp.zeros((GROUP * BLK, 1), F32)
    for i in range(GROUP):
        col = jnp.where(slot == i, sink_ref[0, GROUP * kvh + i], col)
    return col, slot


def _head_probs(qm, kw, valid, sink):
    sc = jnp.where(valid, _dot_nt(qm, kw) * (HEAD ** -0.5), NEG_INF)
    mx = jnp.maximum(jnp.max(sc, axis=-1, keepdims=True), sink)
    ex = jnp.exp(sc - mx)
    den = jnp.sum(ex, axis=-1, keepdims=True) + jnp.exp(sink - mx)
    return ex / den, mx, den


def _cols(start, width=ATTN_W):
    return slice(start, start + width)


def _conv_fwd(z_ref, zp_ref, cw_ref, ext_ref, n):
    u = z_ref[:, _cols(CONV_C0)] * z_ref[:, _cols(CONV_H0)]
    pu = zp_ref[:, _cols(CONV_C0)] * zp_ref[:, _cols(CONV_H0)]
    ext_ref[0:SUBLANES, :] = jnp.where(n > 0, pu, 0.0)
    ext_ref[SUBLANES:SUBLANES + BLK, :] = u
    um1 = ext_ref[SUBLANES - 1:SUBLANES - 1 + BLK, :]
    um2 = ext_ref[SUBLANES - 2:SUBLANES - 2 + BLK, :]
    cv = cw_ref[0:1, :] * um2 + cw_ref[1:2, :] * um1 + cw_ref[2:3, :] * u
    return u, um1, um2, cv


def _prev_rows(n):
    return (jnp.maximum(n * (BLK // SUBLANES) - 1, 0), 0)


def _attn_fwd(qn, k2, v2, z, conv_wp, sinks, after):
    s = qn.shape[0]
    nb = s // BLK
    ring = 3

    def body(sink_ref, q_ref, kc_ref, kp_ref, vc_ref, vp_ref, z_hbm, zp_ref, cw_ref, after_ref, a_ref, mix_ref,
             mixt_ref, ext_ref, z_buf, z_sems):
        n = pl.program_id(0)

        def fetch(step):
            used = pl.ds(QKV_W, REST_W)
            return pltpu.make_async_copy(z_hbm.at[pl.ds(pl.multiple_of(step * BLK, BLK), BLK), used],
                                         z_buf.at[step % ring, :, used], z_sems.at[step % ring])

        @pl.when(n == 0)
        def _():
            for step in range(min(ring - 1, nb)):
                fetch(step).start()

        @pl.when(n + ring - 1 < nb)
        def _():
            fetch(n + ring - 1).start()

        fetch(n).wait()
        z_ref = z_buf.at[n % ring]
        valid = _window_mask(n)
        for kvh in range(K2_W // LANES):
            cols = slice(LANES * kvh, LANES * (kvh + 1))
            kw = jnp.concatenate([kp_ref[:, cols], kc_ref[:, cols]], axis=0)
            vw = jnp.concatenate([vp_ref[:, cols], vc_ref[:, cols]], axis=0)
            blocks = [slice(LANES * r, LANES * (r + 1)) for r in (2 * kvh, 2 * kvh + 1)]
            q4 = _stack_heads([q_ref[:, rc] for rc in blocks], jnp.zeros((BLK, LANES), BF16))
            p, _, _ = _head_probs(q4, kw, valid, _group_sinks(sink_ref, kvh)[0])
            for rc, a in zip(blocks, _unstack_heads(_dot(p.astype(BF16), vw))):
                a_ref[:, rc] = a
                g = z_ref[:, _cols(GATE_A0 + rc.start, LANES)]
                mix_ref[:, rc] = (a * (g * _sig(g))).astype(BF16)
        _, _, _, cv = _conv_fwd(z_ref, zp_ref, cw_ref, ext_ref, n)
        gc = z_ref[:, _cols(GATE_C0)]
        mix_ref[:, ATTN_W:D_MODEL] = (z_ref[:, _cols(CONV_B0)] * cv * (gc * _sig(gc))).astype(BF16)
        mixt_ref[...] = mix_ref[...].T

    cur = lambda w: pl.BlockSpec((BLK, w), lambda n: (n, 0))
    prev = lambda w: pl.BlockSpec((BLK, w), lambda n: (jnp.maximum(n - 1, 0), 0))
    return pl.pallas_call(
        body, name="attn_fwd",
        out_shape=(jax.ShapeDtypeStruct((s, ATTN_W), F32), jax.ShapeDtypeStruct((s, D_MODEL), BF16),
                   jax.ShapeDtypeStruct((D_MODEL, s), BF16)),
        grid=(nb,),
        in_specs=[pl.BlockSpec(memory_space=pltpu.SMEM),
                  cur(ATTN_W), cur(K2_W), prev(K2_W), cur(K2_W), prev(K2_W), ANY,
                  pl.BlockSpec((SUBLANES, IN_W), _prev_rows),
                  pl.BlockSpec((SUBLANES, ATTN_W), lambda n: (0, 0)), ANY],
        out_specs=(cur(ATTN_W), cur(D_MODEL), pl.BlockSpec((D_MODEL, BLK), lambda n: (0, n))),
        scratch_shapes=[pltpu.VMEM((BLK + 2 * SUBLANES, ATTN_W), F32), pltpu.VMEM((ring, BLK, IN_W), F32),
                        pltpu.SemaphoreType.DMA((ring,))],
        compiler_params=_params("arbitrary"))(sinks, qn, k2, k2, v2, v2, z, z, conv_wp, after)


def _fwd_out(mix, w_out, x, g2, tm):
    s = x.shape[0]

    def body(m_ref, w_ref, x_ref, g_ref, x1_ref, h_ref, ht_ref):
        x1 = x_ref[...] + _dot(m_ref[...], w_ref[...])
        x1_ref[...] = x1
        xn, _ = _rms(x1)
        h = (xn * g_ref[...]).astype(BF16)
        h_ref[...] = h
        ht_ref[...] = h.T

    row = pl.BlockSpec((tm, D_MODEL), lambda i: (i, 0))
    return pl.pallas_call(
        body, name="fwd_out",
        out_shape=(jax.ShapeDtypeStruct((s, D_MODEL), F32), jax.ShapeDtypeStruct((s, D_MODEL), BF16),
                   jax.ShapeDtypeStruct((D_MODEL, s), BF16)),
        grid=(s // tm,),
        in_specs=[row, _resident((D_MODEL, D_MODEL)), row, pl.BlockSpec((1, D_MODEL), lambda i: (0, 0))],
        out_specs=(row, row, pl.BlockSpec((D_MODEL, tm), lambda i: (0, i))),
        compiler_params=_params("parallel"))(mix, w_out, x, g2)


def _ple(hn2, w_pg, b_pg, p, w_pp, g3, x1, target, tm):
    s = x1.shape[0]

    def body(h_ref, wg_ref, b_ref, p_ref, wp_ref, g3_ref, x1_ref, t_ref, dy_ref, dgp_ref, dt_ref, pt_ref, acc_ref):
        gate = _sig(_dot(h_ref[...], wg_ref[...]) + b_ref[...])
        pb = p_ref[...].astype(BF16)
        pt_ref[...] = pb.T
        t = _dot(pb, wp_ref[...])
        tn, r3 = _rms(t)
        e = tn * g3_ref[...]
        diff = x1_ref[...] + gate * e - t_ref[...]
        dy = diff * (1.0 / D_MODEL)
        dy_ref[...] = dy
        dgp = dy * e * (gate * (1.0 - gate))
        dgp_ref[...] = dgp.astype(BF16)
        de = dy * gate
        dt_ref[...] = _rms_bwd(de * g3_ref[...], tn, r3).astype(BF16)

        @pl.when(pl.program_id(0) == 0)
        def _():
            acc_ref[...] = jnp.zeros_like(acc_ref)

        acc_ref[0:1, :] += jnp.sum(dgp, axis=0, keepdims=True)
        acc_ref[1:2, :] += jnp.sum(de * tn, axis=0, keepdims=True)
        acc_ref[2:3, :] += jnp.sum(diff * diff, axis=0, keepdims=True) * (0.5 / D_MODEL)

    row = pl.BlockSpec((tm, D_MODEL), lambda i: (i, 0))
    vec = pl.BlockSpec((1, D_MODEL), lambda i: (0, 0))
    return pl.pallas_call(
        body, name="ple",
        out_shape=(jax.ShapeDtypeStruct((s, D_MODEL), F32), jax.ShapeDtypeStruct((s, D_MODEL), BF16),
                   jax.ShapeDtypeStruct((s, D_MODEL), BF16), jax.ShapeDtypeStruct((PLE_DIM, s), BF16),
                   jax.ShapeDtypeStruct((SUBLANES, D_MODEL), F32)),
        grid=(s // tm,),
        in_specs=[row, _resident((D_MODEL, D_MODEL)), vec, pl.BlockSpec((tm, PLE_DIM), lambda i: (i, 0)),
                  _resident((PLE_DIM, D_MODEL)), vec, row, row],
        out_specs=(row, row, row, pl.BlockSpec((PLE_DIM, tm), lambda i: (0, i)),
                   pl.BlockSpec((SUBLANES, D_MODEL), lambda i: (0, 0))),
        compiler_params=_params("arbitrary"))(hn2, w_pg, b_pg, p, w_pp, g3, x1, target)


def _gate_bwd(dgp, w_pg, x1, dy, g2, tm):
    s = x1.shape[0]

    def body(d_ref, w_ref, x1_ref, dy_ref, g_ref, dx_ref, dxb_ref, acc_ref):
        dh = _dot_nt(d_ref[...], w_ref[...])
        xn, r = _rms(x1_ref[...])
        dx1 = dy_ref[...] + _rms_bwd(dh * g_ref[...], xn, r)
        dx_ref[...] = dx1
        dxb_ref[...] = dx1.astype(BF16)

        @pl.when(pl.program_id(0) == 0)
        def _():
            acc_ref[...] = jnp.zeros_like(acc_ref)

        acc_ref[0:1, :] += jnp.sum(dh * xn, axis=0, keepdims=True)

    row = pl.BlockSpec((tm, D_MODEL), lambda i: (i, 0))
    return pl.pallas_call(
        body, name="gate_bwd",
        out_shape=(jax.ShapeDtypeStruct((s, D_MODEL), F32), jax.ShapeDtypeStruct((s, D_MODEL), BF16),
                   jax.ShapeDtypeStruct((SUBLANES, D_MODEL), F32)),
        grid=(s // tm,),
        in_specs=[row, _resident((D_MODEL, D_MODEL)), row, row, pl.BlockSpec((1, D_MODEL), lambda i: (0, 0))],
        out_specs=(row, row, pl.BlockSpec((SUBLANES, D_MODEL), lambda i: (0, 0))),
        compiler_params=_params("arbitrary"))(dgp, w_pg, x1, dy, g2)


def _mm_nt(a, b, tm, name, after):
    m, k = a.shape
    n = b.shape[0]

    def body(a_ref, b_ref, after_ref, o_ref):
        o_ref[...] = _dot_nt(a_ref[...], b_ref[...])

    return pl.pallas_call(
        body, name=name,
        out_shape=jax.ShapeDtypeStruct((m, n), F32),
        grid=(m // tm,),
        in_specs=[pl.BlockSpec((tm, k), lambda i: (i, 0)), _resident((n, k)), ANY],
        out_specs=pl.BlockSpec((tm, n), lambda i: (i, 0)),
        compiler_params=_params("parallel"))(a, b, after)


def _attn_bwd(qn, k2, v2, a, z, dmix, conv_wp, sinks, after):
    s = qn.shape[0]
    nb = s // BLK

    def body(sink_ref, q_ref, kc_ref, kp_ref, vc_ref, vp_ref, a_ref, z_ref, zp_ref, zn_ref, dm_ref, dmn_ref,
             cw_ref, after_ref, dq_ref, dkc_ref, dkp_ref, dvc_ref, dvp_ref, dz_ref, dzt_ref, acc_ref, ext_ref):
        n = pl.program_id(0)
        valid = _window_mask(n)
        lane = lax.broadcasted_iota(jnp.int32, (1, ATTN_W), 1)

        @pl.when(n == 0)
        def _():
            acc_ref[...] = jnp.zeros_like(acc_ref)

        dz_ref[:, 0:QKV_W] = jnp.zeros((BLK, QKV_W), BF16)
        dsink = jnp.zeros((1, ATTN_W), F32)
        for kvh in range(K2_W // LANES):
            cols = slice(LANES * kvh, LANES * (kvh + 1))
            kw = jnp.concatenate([kp_ref[:, cols], kc_ref[:, cols]], axis=0)
            vw = jnp.concatenate([vp_ref[:, cols], vc_ref[:, cols]], axis=0)
            blocks = [slice(LANES * r, LANES * (r + 1)) for r in (2 * kvh, 2 * kvh + 1)]
            das, avs = [], []
            for rc in blocks:
                g = z_ref[:, _cols(GATE_A0 + rc.start, LANES)]
                sg = _sig(g)
                dm = dm_ref[:, rc]
                av = a_ref[:, rc]
                das.append(dm * (g * sg))
                avs += [av, av]
                dz_ref[:, _cols(GATE_A0 + rc.start, LANES)] = (dm * av * _dsilu(g, sg)).astype(BF16)
            q4 = _stack_heads([q_ref[:, rc] for rc in blocks], jnp.zeros((BLK, LANES), BF16))
            sink, slot = _group_sinks(sink_ref, kvh)
            p, mx, den = _head_probs(q4, kw, valid, sink)
            do4 = _stack_heads(das, 0.0)
            delta = jnp.sum(do4 * jnp.concatenate(avs, axis=0), axis=-1, keepdims=True)
            dob = do4.astype(BF16)
            ds = p * (_dot_nt(dob, vw) - delta) * (HEAD ** -0.5)
            for rc, dq in zip(blocks, _unstack_heads(_dot(ds.astype(BF16), kw))):
                dq_ref[:, rc] = dq
            dk2 = _dot(ds.T.astype(BF16), q4)
            dv2 = _dot(p.T.astype(BF16), dob)
            dkp_ref[:, cols] = dk2[0:BLK]
            dkc_ref[:, cols] = dk2[BLK:2 * BLK]
            dvp_ref[:, cols] = dv2[0:BLK]
            dvc_ref[:, cols] = dv2[BLK:2 * BLK]
            dsk = jnp.exp(sink - mx) / den * delta
            for i in range(GROUP):
                dsink = dsink - jnp.where(lane == GROUP * kvh + i,
                                          jnp.sum(jnp.where(slot == i, dsk, 0.0), axis=0, keepdims=True), 0.0)
        acc_ref[0:1, :] += dsink

        u, um1, um2, cv = _conv_fwd(z_ref, zp_ref, cw_ref, ext_ref, n)
        cb = z_ref[:, _cols(CONV_B0)]
        gc = z_ref[:, _cols(GATE_C0)]
        sgc = _sig(gc)
        dmc = dm_ref[:, ATTN_W:D_MODEL]
        t = dmc * (gc * sgc)
        dcv = t * cb
        dz_ref[:, _cols(CONV_B0)] = (t * cv).astype(BF16)
        dz_ref[:, _cols(GATE_C0)] = (dmc * cb * cv * _dsilu(gc, sgc)).astype(BF16)
        gcn = zn_ref[:, _cols(GATE_C0)]
        dcvn = dmn_ref[:, ATTN_W:D_MODEL] * (gcn * _sig(gcn)) * zn_ref[:, _cols(CONV_B0)]
        ext_ref[0:BLK, :] = dcv
        ext_ref[BLK:BLK + SUBLANES, :] = jnp.where(n < nb - 1, dcvn, 0.0)
        du = (cw_ref[2:3, :] * dcv + cw_ref[1:2, :] * ext_ref[1:1 + BLK, :]
              + cw_ref[0:1, :] * ext_ref[2:2 + BLK, :])
        dz_ref[:, _cols(CONV_C0)] = (du * z_ref[:, _cols(CONV_H0)]).astype(BF16)
        dz_ref[:, _cols(CONV_H0)] = (du * z_ref[:, _cols(CONV_C0)]).astype(BF16)
        acc_ref[1:2, :] += jnp.sum(dcv * um2, axis=0, keepdims=True)
        acc_ref[2:3, :] += jnp.sum(dcv * um1, axis=0, keepdims=True)
        acc_ref[3:4, :] += jnp.sum(dcv * u, axis=0, keepdims=True)
        dzt_ref[...] = dz_ref[...].T

    cur = lambda w: pl.BlockSpec((BLK, w), lambda n: (n, 0))
    prev = lambda w: pl.BlockSpec((BLK, w), lambda n: (jnp.maximum(n - 1, 0), 0))
    nxt = lambda w: pl.BlockSpec(
        (SUBLANES, w), lambda n: (jnp.minimum((n + 1) * (BLK // SUBLANES), nb * (BLK // SUBLANES) - 1), 0))
    f32 = lambda w: jax.ShapeDtypeStruct((s, w), F32)
    return pl.pallas_call(
        body, name="attn_bwd",
        out_shape=(f32(ATTN_W), f32(K2_W), f32(K2_W), f32(K2_W), f32(K2_W),
                   jax.ShapeDtypeStruct((s, IN_W), BF16), jax.ShapeDtypeStruct((IN_W, s), BF16),
                   jax.ShapeDtypeStruct((SUBLANES, ATTN_W), F32)),
        grid=(nb,),
        in_specs=[pl.BlockSpec(memory_space=pltpu.SMEM),
                  cur(ATTN_W), cur(K2_W), prev(K2_W), cur(K2_W), prev(K2_W), cur(ATTN_W), cur(IN_W),
                  pl.BlockSpec((SUBLANES, IN_W), _prev_rows), nxt(IN_W), cur(D_MODEL), nxt(D_MODEL),
                  pl.BlockSpec((SUBLANES, ATTN_W), lambda n: (0, 0)), ANY],
        out_specs=(cur(ATTN_W), cur(K2_W), cur(K2_W), cur(K2_W), cur(K2_W), cur(IN_W),
                   pl.BlockSpec((IN_W, BLK), lambda n: (0, n)), pl.BlockSpec((SUBLANES, ATTN_W), lambda n: (0, 0))),
        scratch_shapes=[pltpu.VMEM((BLK + 2 * SUBLANES, ATTN_W), F32)],
        compiler_params=_params("arbitrary"))(sinks, qn, k2, k2, v2, v2, a, z, z, z, dmix, dmix, conv_wp, after)


def _qkv_bwd(z, dz, dzt, dq, dkc, dkp, dvc, dvp, ra, rbm, rbp, gq2, gk2):
    s = z.shape[0]
    nb = s // BLK

    def body(z_ref, dz_in, dzt_in, dq_ref, dkc_ref, dkp_ref, dvc_ref, dvp_ref, a_ref, bm_ref, bp_ref, gq_ref, gk_ref,
             dz_ref, dzt_ref, acc_ref):
        n = pl.program_id(0)
        a, bm, bp = a_ref[...], bm_ref[...], bp_ref[...]
        lo = _low_half((BLK, LANES))
        last = n == nb - 1

        @pl.when(n == 0)
        def _():
            acc_ref[...] = jnp.zeros_like(acc_ref)

        def norm_bwd(x, dy, gain):
            rr = lax.rsqrt(_half_sums(x * x) * (1.0 / HEAD) + EPS)
            xh = x * rr
            dxg = _rope_t(dy, a, bm, bp)
            dxh = dxg * gain
            dx = rr * (dxh - xh * (_half_sums(dxh * xh) * (1.0 / HEAD)))
            return dx, jnp.sum(dxg * xh, axis=0, keepdims=True)

        def folded(cur_ref, prev_ref, m):
            parts = []
            for h in (2 * m, 2 * m + 1):
                v = cur_ref[:, LANES * h:LANES * (h + 1)] + jnp.where(
                    last, 0.0, prev_ref[:, LANES * h:LANES * (h + 1)])
                parts.append(v + pltpu.roll(v, HEAD, 1))
            return jnp.where(lo, parts[0], parts[1])

        gq_acc = jnp.zeros((1, LANES), F32)
        for r in range(ATTN_W // LANES):
            rc = slice(LANES * r, LANES * (r + 1))
            dx, gg = norm_bwd(z_ref[:, rc], dq_ref[:, rc], gq_ref[...])
            dz_ref[:, rc] = dx.astype(BF16)
            gq_acc = gq_acc + gg
        acc_ref[0:1, :] += gq_acc
        gk_acc = jnp.zeros((1, LANES), F32)
        for m in range(KV_W // LANES):
            kc = slice(ATTN_W + LANES * m, ATTN_W + LANES * (m + 1))
            dx, gg = norm_bwd(z_ref[:, kc], folded(dkc_ref, dkp_ref, m), gk_ref[...])
            dz_ref[:, kc] = dx.astype(BF16)
            gk_acc = gk_acc + gg
            vc = slice(ATTN_W + KV_W + LANES * m, ATTN_W + KV_W + LANES * (m + 1))
            dz_ref[:, vc] = folded(dvc_ref, dvp_ref, m).astype(BF16)
        acc_ref[1:2, :] += gk_acc
        dzt_ref[...] = dz_ref[...].T

    cur = lambda w: pl.BlockSpec((BLK, w), lambda n: (n, 0))
    nxt = lambda w: pl.BlockSpec((BLK, w), lambda n: (jnp.minimum(n + 1, nb - 1), 0))
    one = pl.BlockSpec((1, LANES), lambda n: (0, 0))
    return pl.pallas_call(
        body, name="qkv_bwd",
        out_shape=(jax.ShapeDtypeStruct(dz.shape, dz.dtype), jax.ShapeDtypeStruct(dzt.shape, dzt.dtype),
                   jax.ShapeDtypeStruct((SUBLANES, LANES), F32)),
        grid=(nb,),
        in_specs=[cur(PAIR_W), ANY, ANY, cur(ATTN_W), cur(K2_W), nxt(K2_W), cur(K2_W), nxt(K2_W),
                  cur(LANES), cur(LANES), cur(LANES), one, one],
        out_specs=(cur(QKV_W), pl.BlockSpec((QKV_W, BLK), lambda n: (0, n)),
                   pl.BlockSpec((SUBLANES, LANES), lambda n: (0, 0))),
        input_output_aliases={1: 0, 2: 1},
        compiler_params=_params("arbitrary"))(z, dz, dzt, dq, dkc, dkp, dvc, dvp, ra, rbm, rbp, gq2, gk2)


def _in_bwd(dz, w_pairs, x, dx1, g1, tm, after):
    s = x.shape[0]
    n = s // tm
    sub = tm // N_PAIRS
    stripes = 4

    def body(d_ref, w_ref, x_ref, dx1_ref, g_ref, after_ref, gx_ref, acc_ref, dh_ref):
        i, k = pl.program_id(0), pl.program_id(1)

        def matmul(c):
            cols = slice(c * (D_MODEL // stripes), (c + 1) * (D_MODEL // stripes))
            dh_ref[i % 2, :, cols] += _dot(d_ref[...], w_ref[0, :, cols])

        def norm_bwd(c):
            part = sub // stripes
            mine = slice(c * part, (c + 1) * part)
            rows = pl.ds(pl.multiple_of(k * sub + c * part, part), part)
            dh = dh_ref[(i + 1) % 2, rows, :]
            dh_ref[(i + 1) % 2, rows, :] = jnp.zeros_like(dh)
            xn, r = _rms(x_ref[mine, :])
            gx_ref[rows, :] = dx1_ref[mine, :] + _rms_bwd(dh * g_ref[...], xn, r)
            acc_ref[0:1, :] += jnp.sum(dh * xn, axis=0, keepdims=True)

        @pl.when((i == 0) & (k == 0))
        def _():
            acc_ref[...] = jnp.zeros_like(acc_ref)
            dh_ref[...] = jnp.zeros_like(dh_ref)

        @pl.when(i == 0)
        def _():
            for c in range(stripes):
                matmul(c)

        @pl.when((i > 0) & (i < n))
        def _():
            for c in range(stripes):
                matmul(c)
                norm_bwd(c)

        @pl.when(i == n)
        def _():
            for c in range(stripes):
                norm_bwd(c)

    last = lambda i, k: jnp.where(i == n, N_PAIRS - 1, k)
    rows_before = lambda i, k: (jnp.maximum(i - 1, 0) * N_PAIRS + k, 0)
    return pl.pallas_call(
        body, name="in_bwd",
        out_shape=(jax.ShapeDtypeStruct((s, D_MODEL), F32), jax.ShapeDtypeStruct((SUBLANES, D_MODEL), F32)),
        grid=(n + 1, N_PAIRS),
        in_specs=[pl.BlockSpec((tm, PAIR_W), lambda i, k: (jnp.minimum(i, n - 1), last(i, k))),
                  pl.BlockSpec((1, PAIR_W, D_MODEL), lambda i, k: (last(i, k), 0, 0)),
                  pl.BlockSpec((sub, D_MODEL), rows_before), pl.BlockSpec((sub, D_MODEL), rows_before),
                  pl.BlockSpec((1, D_MODEL), lambda i, k: (0, 0)), ANY],
        out_specs=(pl.BlockSpec((tm, D_MODEL), lambda i, k: (jnp.maximum(i - 1, 0), 0)),
                   pl.BlockSpec((SUBLANES, D_MODEL), lambda i, k: (0, 0))),
        scratch_shapes=[pltpu.VMEM((2, tm, D_MODEL), F32)],
        compiler_params=_params("arbitrary", "arbitrary"))(dz, w_pairs, x, dx1, g1, after)


def _mm_grad(at, bs, tn, name):
    m, kdim = at.shape
    nblk = [b.shape[1] // tn for b in bs]
    starts = [sum(nblk[:t]) for t in range(len(bs))]

    def body(a_ref, *refs):
        b_refs, o_ref = refs[:len(bs)], refs[len(bs)]
        j = pl.program_id(0)
        for t, b_ref in enumerate(b_refs):
            @pl.when((j >= starts[t]) & (j < starts[t] + nblk[t]))
            def _():
                o_ref[...] = _dot(a_ref[...], b_ref[...]).astype(BF16)

    def b_spec(t):
        return pl.BlockSpec((kdim, tn), lambda j: (0, jnp.clip(j - starts[t], 0, nblk[t] - 1)))

    return pl.pallas_call(
        body, name=name,
        out_shape=jax.ShapeDtypeStruct((m, sum(nblk) * tn), BF16),
        grid=(sum(nblk),),
        in_specs=[_resident((m, kdim))] + [b_spec(t) for t in range(len(bs))],
        out_specs=pl.BlockSpec((m, tn), lambda j: (0, j)),
        compiler_params=_params("parallel"))(at, *bs)


def _grad_w_in(dzt, h):
    kdim = h.shape[0]

    def body(d_ref, h_ref, o_ref):
        o_ref[0] = _dot(d_ref[...], h_ref[...]).astype(BF16)

    return pl.pallas_call(
        body, name="grad_w_in",
        out_shape=jax.ShapeDtypeStruct((N_DEV, SHARD_IN, D_MODEL), BF16),
        grid=(N_DEV,),
        in_specs=[pl.BlockSpec((SHARD_IN, kdim), lambda j: (j, 0)), _resident((kdim, D_MODEL))],
        out_specs=pl.BlockSpec((1, SHARD_IN, D_MODEL), lambda j: (j, 0, 0)),
        compiler_params=_params("parallel"))(dzt, h)


def _place():
    return lax.axis_index("x"), lax.axis_index("y"), lax.axis_index("c")


ROW_TAPS, ROW_MISC = 4, 5
Q_AT, K_AT, SINK_AT, LOSS_AT = (ATTN_W + LANES * t for t in range(4))
SMALL_AT = [(0, 0), (1, 0), (2, 0), (3, 0), (ROW_MISC, Q_AT), (ROW_MISC, K_AT), (ROW_MISC, SINK_AT)]


def _tap_at(tap):
    return ROW_TAPS + tap // 2, ATTN_W * (tap % 2)


def _reduce_small(acc_g1, acc_g2, acc_ple, acc_qk, acc_attn):
    def body(g1_ref, g2_ref, ple_ref, qk_ref, attn_ref, out_ref, slab_ref, gath_ref, send_sems, recv_sems):
        x, y, c = _place()
        me = 4 * x + 2 * y + c
        slab_ref[...] = jnp.zeros_like(slab_ref)
        slab_ref[0:1, :] = g1_ref[0:1, :]
        slab_ref[1:2, :] = g2_ref[0:1, :]
        slab_ref[2:4, :] = ple_ref[0:2, :]
        qk = qk_ref[0:2, :]
        qk = jnp.where(_low_half(qk.shape), qk + pltpu.roll(qk, HEAD, 1), 0.0)
        misc = slab_ref.at[ROW_MISC:ROW_MISC + 1]
        misc[:, Q_AT:Q_AT + LANES] = qk[0:1]
        misc[:, K_AT:K_AT + LANES] = qk[1:2]
        lane = lax.broadcasted_iota(jnp.int32, (1, LANES), 1)
        misc[:, SINK_AT:SINK_AT + LANES] = jnp.where(lane < N_Q_HEADS, attn_ref[0:1, 0:LANES], 0.0)
        misc[:, LOSS_AT:LOSS_AT + LANES] = sum(
            ple_ref[2:3, LANES * t:LANES * (t + 1)] for t in range(D_MODEL // LANES))
        for tap in range(3):
            row, at = _tap_at(tap)
            slab_ref[row:row + 1, at:at + ATTN_W] = attn_ref[1 + tap:2 + tap, :]
        gath_ref[me] = slab_ref[...]
        copies = []
        for k in range(1, N_DEV):
            peer = (x ^ (k >> 2), y ^ ((k >> 1) & 1), c ^ (k & 1))
            copies.append(pltpu.make_async_remote_copy(
                src_ref=slab_ref, dst_ref=gath_ref.at[me], send_sem=send_sems.at[k - 1],
                recv_sem=recv_sems.at[k - 1], device_id=peer, device_id_type=MESH))
        for cp in copies:
            cp.start()
        for cp in copies:
            cp.wait_recv()
        for cp in copies:
            cp.wait_send()
        total = gath_ref[0]
        for d in range(1, N_DEV):
            total = total + gath_ref[d]
        out_ref[...] = total

    vmem = pl.BlockSpec(memory_space=pltpu.VMEM)
    return pl.pallas_call(
        body, name="reduce_small",
        out_shape=jax.ShapeDtypeStruct((SLAB_ROWS, D_MODEL), F32),
        in_specs=[vmem] * 5, out_specs=vmem,
        scratch_shapes=[pltpu.VMEM((SLAB_ROWS, D_MODEL), F32), pltpu.VMEM((N_DEV, SLAB_ROWS, D_MODEL), F32),
                        pltpu.SemaphoreType.DMA((N_DEV - 1,)), pltpu.SemaphoreType.DMA((N_DEV - 1,))])(
            acc_g1, acc_g2, acc_ple, acc_qk, acc_attn)


def _pair_sum(g, r, place, tr, name):
    _, _, rows, cols = g.shape

    def body(place_ref, g_ref, r_ref, pb_ref, own_ref):
        tot = g_ref[0, 0].astype(F32) + r_ref[0].astype(F32)
        pb_ref[0] = tot.astype(BF16)

        @pl.when(pl.program_id(1) == place_ref[1])
        def _():
            own_ref[...] = tot

    grid_spec = pltpu.PrefetchScalarGridSpec(
        num_scalar_prefetch=1, grid=(rows // tr, 4),
        in_specs=[pl.BlockSpec((1, 1, tr, cols), lambda i, q, place_ref: (q, place_ref[0], i, 0)),
                  pl.BlockSpec((1, tr, cols), lambda i, q, place_ref: (q, i, 0))],
        out_specs=(pl.BlockSpec((1, tr, cols), lambda i, q, place_ref: (q, i, 0)),
                   pl.BlockSpec((tr, cols), lambda i, q, place_ref: (i, 0))))
    return pl.pallas_call(
        body, name=name, grid_spec=grid_spec,
        out_shape=(jax.ShapeDtypeStruct((4, rows, cols), BF16), jax.ShapeDtypeStruct((rows, cols), F32)),
        compiler_params=_params("arbitrary", "arbitrary"))(place, g, r)


HBM = pl.BlockSpec(memory_space=pltpu.HBM)
SEM = pl.BlockSpec(memory_space=pltpu.SEMAPHORE)
SIDE_EFFECT = pltpu.CompilerParams(has_side_effects=pltpu.SideEffectType.DATAFLOW_SIDE_EFFECTING)
TOKEN = jax.ShapeDtypeStruct((SUBLANES, LANES), F32)


def _hbm(a):
    return pltpu.with_memory_space_constraint(a, pltpu.HBM)


def _hbm_like(arrays):
    return tuple(pltpu.HBM(a.shape, a.dtype) for a in arrays)


def _block_of(px, py, pc):
    return 4 * px + 2 * py + pc


def _relay_parts(rows):
    if rows % (2 * PACKED_ROWS):
        return [pl.ds(0, rows), None]
    return [pl.ds(0, rows // 2), pl.ds(rows // 2, rows // 2)]


def _gather_start(shards, after, relay=False):
    na = len(shards)
    lands = [_hbm(lax.empty((N_DEV,) + a.shape, a.dtype)) for a in shards]

    def body(*refs):
        ins, land = refs[:na], refs[na:2 * na]
        send_sems, recv_ici, recv_d2d = refs[2 * na + 1:2 * na + 4]
        token = refs[-1]
        x, y, c = _place()
        peers = [(x, y, 1 - c), (1 - x, y, c), (x, 1 - y, c), (1 - x, 1 - y, c)]
        for k, peer in enumerate(peers[:3] if relay else peers):
            for t in range(na):
                pltpu.make_async_remote_copy(
                    src_ref=ins[t], dst_ref=land[t].at[_block_of(x, y, c)], send_sem=send_sems.at[4 * t + k],
                    recv_sem=recv_d2d.at[4 * t] if k == 0 else recv_ici.at[3 * t + k - 1],
                    device_id=peer, device_id_type=MESH).start()
        token[...] = jnp.zeros_like(token)

    out = pl.pallas_call(
        body, name="gather_start",
        out_shape=(pltpu.SemaphoreType.DMA((4 * na,)), pltpu.SemaphoreType.DMA((3 * na,)),
                   pltpu.SemaphoreType.DMA((4 * na,)), pltpu.SemaphoreType.DMA((2 * na,)), *_hbm_like(lands), TOKEN),
        in_specs=[ANY] * na + [HBM] * na + [ANY],
        out_specs=(SEM, SEM, SEM, SEM, *[HBM] * na, pl.BlockSpec(memory_space=pltpu.VMEM)),
        input_output_aliases={na + i: 4 + i for i in range(na)},
        compiler_params=SIDE_EFFECT)(*shards, *lands, after)
    send_sems, recv_ici, recv_d2d, recv_relay = out[:4]
    state = dict(send=send_sems, ici=recv_ici, d2d=recv_d2d, relay=recv_relay, relayed=relay, shards=list(shards),
                 lands=out[4:4 + na])
    return state, out[-1]


def _gather_forward(state, after):
    lands = state["lands"]
    na = len(lands)

    def body(*refs):
        land = refs[:na]
        recv_ici, recv_d2d = refs[na], refs[na + 1]
        fwd_sems, token = refs[-2], refs[-1]
        x, y, c = _place()
        for j, chip in enumerate([(1 - x, y), (x, 1 - y), (1 - x, 1 - y)]):
            for t in range(na):
                blk = land[t].at[_block_of(*chip, c)]
                pltpu.make_async_remote_copy(
                    src_ref=blk, dst_ref=blk, send_sem=fwd_sems.at[3 * t + j], recv_sem=recv_ici.at[3 * t + j],
                    device_id=(x, y, c), device_id_type=MESH).wait_recv()
                pltpu.make_async_remote_copy(
                    src_ref=blk, dst_ref=blk, send_sem=fwd_sems.at[3 * t + j], recv_sem=recv_d2d.at[4 * t + 1 + j],
                    device_id=(x, y, 1 - c), device_id_type=MESH).start()
        token[...] = jnp.zeros_like(token)

    out = pl.pallas_call(
        body, name="gather_forward",
        out_shape=(*_hbm_like(lands), pltpu.SemaphoreType.DMA((3 * na,)), TOKEN),
        in_specs=[HBM] * na + [SEM, SEM, ANY],
        out_specs=(*[HBM] * na, SEM, pl.BlockSpec(memory_space=pltpu.VMEM)),
        input_output_aliases={i: i for i in range(na)},
        compiler_params=SIDE_EFFECT)(*lands, state["ici"], state["d2d"], after)
    return dict(state, lands=out[:na], fwd=out[na]), out[-1]


def _gather_wait(state, after):
    shards, lands = state["shards"], state["lands"]
    na = len(lands)

    def body(*refs):
        ins, land = refs[:na], refs[na:2 * na]
        send_sems, fwd_sems, recv_d2d = refs[2 * na:2 * na + 3]
        x, y, c = _place()
        chips = [(1 - x, y), (x, 1 - y), (1 - x, 1 - y)]
        for t in range(na):
            mine = land[t].at[_block_of(x, y, c)]
            for k in range(4):
                pltpu.make_async_remote_copy(
                    src_ref=ins[t], dst_ref=mine, send_sem=send_sems.at[4 * t + k], recv_sem=recv_d2d.at[4 * t],
                    device_id=(x, y, c), device_id_type=MESH).wait_send()
            for j, chip in enumerate(chips):
                blk = land[t].at[_block_of(*chip, c)]
                pltpu.make_async_remote_copy(
                    src_ref=blk, dst_ref=blk, send_sem=fwd_sems.at[3 * t + j], recv_sem=recv_d2d.at[4 * t + 1 + j],
                    device_id=(x, y, c), device_id_type=MESH).wait_send()
            for k, blk_id in enumerate([_block_of(x, y, 1 - c)] + [_block_of(*chip, 1 - c) for chip in chips]):
                blk = land[t].at[blk_id]
                pltpu.make_async_remote_copy(
                    src_ref=blk, dst_ref=blk, send_sem=send_sems.at[4 * t], recv_sem=recv_d2d.at[4 * t + k],
                    device_id=(x, y, c), device_id_type=MESH).wait_recv()

    out = pl.pallas_call(
        body, name="gather_wait",
        out_shape=_hbm_like(lands),
        in_specs=[ANY] * na + [HBM] * na + [SEM, SEM, SEM, ANY],
        out_specs=tuple([HBM] * na),
        input_output_aliases={na + i: i for i in range(na)},
        compiler_params=SIDE_EFFECT)(*shards, *lands, state["send"], state["fwd"], state["d2d"], after)
    return out


def _gather_from_sibling(state, after):
    lands = state["lands"]
    na = len(lands)

    def body(*refs):
        land, recv_d2d = refs[:na], refs[na]
        x, y, c = _place()
        for t in range(na):
            blk = land[t].at[_block_of(x, y, 1 - c)]
            pltpu.make_async_remote_copy(src_ref=blk, dst_ref=blk, send_sem=recv_d2d.at[4 * t],
                                         recv_sem=recv_d2d.at[4 * t], device_id=(x, y, c),
                                         device_id_type=MESH).wait_recv()

    out = pl.pallas_call(
        body, name="gather_from_sibling", out_shape=_hbm_like(lands),
        in_specs=[HBM] * na + [SEM, ANY], out_specs=tuple([HBM] * na),
        input_output_aliases={i: i for i in range(na)},
        compiler_params=SIDE_EFFECT)(*lands, state["d2d"], after)
    return dict(state, lands=list(out))


def _gather_from_chip(state, j, afters, last):
    shards, lands, relayed = state["shards"], state["lands"], state["relayed"]
    na = len(lands)
    parts = [_relay_parts(a.shape[0]) for a in shards]

    def relay_on(land_ref, t, nb, fwd_sems, recv_relay):
        x, y, c = _place()
        blk = chip_blocks(land_ref, nb)[1].at[parts[t][nb]]
        return pltpu.make_async_remote_copy(
            src_ref=blk, dst_ref=blk, send_sem=fwd_sems.at[na + t], recv_sem=recv_relay.at[2 * t + nb],
            device_id=[(x, 1 - y, c), (1 - x, y, c)][nb], device_id_type=MESH)

    def chip_blocks(land_ref, which=j):
        x, y, c = _place()
        chip = [(1 - x, y), (x, 1 - y), (1 - x, 1 - y)][which]
        return (x, y, c), land_ref.at[_block_of(*chip, c)], land_ref.at[_block_of(*chip, 1 - c)]

    def forward(*refs):
        land, recv_ici, recv_d2d, recv_relay, fwd_sems = refs[:na], refs[na], refs[na + 1], refs[na + 2], refs[-1]
        for t in range(na):
            (x, y, c), mine, _ = chip_blocks(land[t])
            if relayed and j == 2:
                for half, rows in enumerate(parts[t]):
                    if rows is not None:
                        pltpu.make_async_remote_copy(
                            src_ref=mine.at[rows], dst_ref=mine.at[rows], send_sem=fwd_sems.at[t],
                            recv_sem=recv_relay.at[2 * t + half], device_id=(x, y, c),
                            device_id_type=MESH).wait_recv()
            else:
                pltpu.make_async_remote_copy(src_ref=mine, dst_ref=mine, send_sem=fwd_sems.at[t],
                                             recv_sem=recv_ici.at[3 * t + j], device_id=(x, y, c),
                                             device_id_type=MESH).wait_recv()
            pltpu.make_async_remote_copy(src_ref=mine, dst_ref=mine, send_sem=fwd_sems.at[t],
                                         recv_sem=recv_d2d.at[4 * t + 1 + j], device_id=(x, y, 1 - c),
                                         device_id_type=MESH).start()
            if relayed and j < 2 and parts[t][j] is not None:
                relay_on(land[t], t, j, fwd_sems, recv_relay).start()

    out = pl.pallas_call(
        forward, name="gather_pass_chip_" + str(j),
        out_shape=(*_hbm_like(lands), pltpu.SemaphoreType.DMA((2 * na,))),
        in_specs=[HBM] * na + [SEM, SEM, SEM] + [ANY] * len(afters), out_specs=(*[HBM] * na, SEM),
        input_output_aliases={i: i for i in range(na)},
        compiler_params=SIDE_EFFECT)(*lands, state["ici"], state["d2d"], state["relay"], *afters)
    passed_sems = out[na]
    relays = state.get("relays", []) + ([passed_sems] if relayed and j < 2 else [])
    waited = relays if last else []

    def arrive(*refs):
        land, fwd_sems, recv_d2d = refs[:na], refs[na], refs[na + 1]
        shard, send_sems, recv_relay = refs[na + 2:2 * na + 2], refs[2 * na + 2], refs[2 * na + 3]
        for nb, relay_sems in enumerate(refs[2 * na + 4:2 * na + 4 + len(waited)]):
            for t in range(na):
                if parts[t][nb] is not None:
                    relay_on(land[t], t, nb, relay_sems, recv_relay).wait_send()
        for t in range(na):
            (x, y, c), mine, theirs = chip_blocks(land[t])
            pltpu.make_async_remote_copy(src_ref=theirs, dst_ref=theirs, send_sem=fwd_sems.at[t],
                                         recv_sem=recv_d2d.at[4 * t + 1 + j], device_id=(x, y, c),
                                         device_id_type=MESH).wait_recv()
            pltpu.make_async_remote_copy(src_ref=mine, dst_ref=mine, send_sem=fwd_sems.at[t],
                                         recv_sem=recv_d2d.at[4 * t + 1 + j], device_id=(x, y, c),
                                         device_id_type=MESH).wait_send()
            for k in range((3 if relayed else 4) if last else 0):
                pltpu.make_async_remote_copy(
                    src_ref=shard[t], dst_ref=land[t].at[_block_of(x, y, c)], send_sem=send_sems.at[4 * t + k],
                    recv_sem=recv_d2d.at[4 * t], device_id=(x, y, c), device_id_type=MESH).wait_send()

    def take(state, afters):
        taken = pl.pallas_call(
            arrive, name="gather_take_chip_" + str(j), out_shape=_hbm_like(lands),
            in_specs=[HBM] * na + [SEM, SEM] + [ANY] * na + [SEM, SEM] + [SEM] * len(waited) + [ANY] * len(afters),
            out_specs=tuple([HBM] * na), input_output_aliases={i: i for i in range(na)},
            compiler_params=SIDE_EFFECT)(*state["lands"], passed_sems, state["d2d"], *shards, state["send"],
                                         state["relay"], *waited, *afters)
        return dict(state, lands=list(taken))

    return dict(state, lands=list(out[:na]), relays=relays), take


def _to_sibling(srcs, lands, send_sems, recv_sems):
    x, y, c = _place()
    return [pltpu.make_async_remote_copy(
        src_ref=srcs[t].at[:, 1 - c], dst_ref=lands[t], send_sem=send_sems.at[t], recv_sem=recv_sems.at[t],
        device_id=(x, y, 1 - c), device_id_type=MESH) for t in range(len(srcs))]


def _to_chips(srcs, lands, send_sems, recv_sems):
    x, y, c = _place()
    copies = []
    for k in (1, 2, 3):
        px, py = x ^ (k >> 1), y ^ (k & 1)
        copies += [pltpu.make_async_remote_copy(
            src_ref=srcs[t].at[2 * px + py], dst_ref=lands[t].at[k - 1], send_sem=send_sems.at[3 * t + k - 1],
            recv_sem=recv_sems.at[3 * t + k - 1], device_id=(px, py, c), device_id_type=MESH) for t in range(len(srcs))]
    return copies


def _exchange_start(name, srcs, land_shapes, copies, per_array, after):
    na = len(srcs)
    lands = [_hbm(lax.empty(shp, a.dtype)) for shp, a in zip(land_shapes, srcs)]

    def body(*refs):
        token = refs[-1]
        for cp in copies(refs[:na], refs[na:2 * na], refs[2 * na + 1], refs[2 * na + 2]):
            cp.start()
        token[...] = jnp.zeros_like(token)

    out = pl.pallas_call(
        body, name=name,
        out_shape=(pltpu.SemaphoreType.DMA((na * per_array,)), pltpu.SemaphoreType.DMA((na * per_array,)),
                   *_hbm_like(lands), TOKEN),
        in_specs=[ANY] * na + [HBM] * na + [ANY],
        out_specs=(SEM, SEM, *[HBM] * na, pl.BlockSpec(memory_space=pltpu.VMEM)),
        input_output_aliases={na + i: 2 + i for i in range(na)},
        compiler_params=SIDE_EFFECT)(*srcs, *lands, after)
    return dict(send=out[0], recv=out[1], srcs=list(srcs), lands=out[2:2 + na]), out[-1]


def _exchange_wait(name, state, copies, afters):
    srcs, lands = state["srcs"], state["lands"]
    na = len(srcs)

    def body(*refs):
        for cp in copies(refs[:na], refs[na:2 * na], refs[2 * na], refs[2 * na + 1]):
            cp.wait_send()
            cp.wait_recv()

    out = pl.pallas_call(
        body, name=name,
        out_shape=_hbm_like(lands),
        in_specs=[ANY] * na + [HBM] * na + [SEM, SEM] + [ANY] * len(afters),
        out_specs=tuple([HBM] * na),
        input_output_aliases={na + i: i for i in range(na)},
        compiler_params=SIDE_EFFECT)(*srcs, *lands, state["send"], state["recv"], *afters)
    return out


def _adamw_math(w, g, m, v):
    m = ADAM_B1 * m + (1.0 - ADAM_B1) * g
    v = ADAM_B2 * v + (1.0 - ADAM_B2) * (g * g)
    m_hat = m / (1.0 - ADAM_B1 ** ADAM_STEP)
    v_hat = v / (1.0 - ADAM_B2 ** ADAM_STEP)
    return -ADAM_LR * (m_hat / (jnp.sqrt(v_hat) + ADAM_EPS) + ADAM_WD * w), m, v


def _adamw(own, others, w, m, v, tr, name, after):
    rows, cols = w.shape
    blk = pl.BlockSpec((tr, cols), lambda i: (i, 0))

    def body(own_ref, oth_ref, w_ref, m_ref, v_ref, after_ref, g_ref, d_ref, nm_ref, nv_ref):
        g = own_ref[...]
        for k in range(3):
            g = g + oth_ref[k].astype(F32)
        g_ref[...] = g
        d_ref[...], nm_ref[...], nv_ref[...] = _adamw_math(w_ref[...], g, m_ref[...], v_ref[...])

    out = jax.ShapeDtypeStruct((rows, cols), F32)
    return pl.pallas_call(
        body, name=name, out_shape=(out, out, out, out), grid=(rows // tr,),
        in_specs=[blk, pl.BlockSpec((3, tr, cols), lambda i: (0, i, 0)), blk, blk, blk, ANY],
        out_specs=(blk, blk, blk, blk),
        compiler_params=_params("parallel"))(own, others, w, m, v, after)


def _adamw_small(red, me, params, moments1, moments2):
    n = len(params)

    def body(me_ref, red_ref, *refs):
        ws, ms, vs = refs[:n], refs[n:2 * n], refs[2 * n:3 * n]
        loss_ref = refs[3 * n]
        outs = refs[3 * n + 1:]
        loss_ref[...] = jnp.sum(red_ref[ROW_MISC:ROW_MISC + 1, LOSS_AT:LOSS_AT + LANES], axis=-1, keepdims=True)
        for t, (row, at) in enumerate(SMALL_AT):
            g = red_ref[row:row + 1, at:at + ws[t].shape[1]]
            d, nm, nv = _adamw_math(ws[t][...], g, ms[t][...], vs[t][...])
            for o, val in zip(outs[4 * t:4 * t + 4], (g, d, nm, nv)):
                o[...] = val
        for tap in range(ws[-1].shape[0]):
            row, at = _tap_at(tap)
            g = red_ref[row:row + 1, pl.ds(pl.multiple_of(at + me_ref[0, 0] * LANES, LANES), LANES)]
            d, nm, nv = _adamw_math(ws[-1][tap], g, ms[-1][tap], vs[-1][tap])
            for o, val in zip(outs[4 * (n - 1):], (g, d, nm, nv)):
                o[tap] = val

    vmem = pl.BlockSpec(memory_space=pltpu.VMEM)
    shapes = [jax.ShapeDtypeStruct(w.shape, F32) for w in params for _ in range(4)]
    out = pl.pallas_call(
        body, name="adamw_small", out_shape=(jax.ShapeDtypeStruct((1, 1), F32), *shapes),
        in_specs=[pl.BlockSpec(memory_space=pltpu.SMEM), vmem] + [vmem] * (3 * n),
        out_specs=tuple([vmem] * (1 + 4 * n)))(me, red, *params, *moments1, *moments2)
    return out[0], [list(out[1 + k::4]) for k in range(4)]


def _tables(gq, gk, conv_w):
    gq2 = jnp.tile(gq.reshape(1, HEAD), (1, 2))
    gk2 = jnp.tile(gk.reshape(1, HEAD), (1, 2))
    conv_wp = jnp.pad(conv_w, ((0, SUBLANES - conv_w.shape[0]), (0, 0)))
    return gq2, gk2, conv_wp


def _pair_id(q):
    return jnp.array([q, 0], jnp.int32)


def _forward_in(x, g1, shards):
    s = x.shape[0]
    h = _prenorm(x, g1, min(512, s), x)
    z, w_pairs = lax.empty((s, IN_W), F32), lax.empty((N_PAIRS, PAIR_W, D_MODEL), BF16)
    for q in range(N_PAIRS):
        z, w_pairs = _fwd_in_pair(h, shards, z, w_pairs, _pair_id(q), min(512, s), "fwd_in_" + str(q),
                                  own=shards[0] if q == 0 else None)
    return h, z, w_pairs


def _forward_attn(z, rope, gq2, gk2, conv_wp, sinks):
    s = z.shape[0]
    qn, k2, v2 = _qk_prep(z, *rope, gq2, gk2, min(256, s), z)
    a, mix, mixt = _attn_fwd(qn, k2, v2, z, conv_wp, sinks, qn)
    return qn, k2, v2, a, mix, mixt


def _forward_out(x, p, target, mix, mixt, w_out, g2, w_pg, b_pg, w_pp, g3):
    s = x.shape[0]
    tm = min(512, s)
    x1, hn2, hn2t = _fwd_out(mix, w_out, x, g2, tm)
    dy, dgp, dt, pt, acc_ple = _ple(hn2, w_pg, b_pg, p, w_pp, g3, x1, target, min(256, s))
    dx1, dx1b, acc_g2 = _gate_bwd(dgp, w_pg, x1, dy, g2, tm)
    gw_out = _mm_grad(mixt, [dx1b], 512, "grad_w_out")
    gw_pg = _mm_grad(hn2t, [dgp], 512, "grad_w_ple_gate")
    gw_pp = _mm_grad(pt, [dt], 512, "grad_w_ple_proj")
    return dx1, dx1b, (gw_out, gw_pg, gw_pp), acc_ple, acc_g2


def _backward_attn(dmix, h, z, qn, k2, v2, a, rope, gq2, gk2, conv_wp, sinks, after):
    dq, dkc, dkp, dvc, dvp, dz, dzt, acc_attn = _attn_bwd(qn, k2, v2, a, z, dmix, conv_wp, sinks, after)
    dz, dzt, acc_qk = _qkv_bwd(z, dz, dzt, dq, dkc, dkp, dvc, dvp, *rope, gq2, gk2)
    return dz, _grad_w_in(dzt, h), acc_attn, acc_qk


def _local_step(x, p, target, g1, shards, gq, gk, sinks, conv_w, w_out, g2, w_pg, b_pg, w_pp, g3):
    rope, (gq2, gk2, conv_wp) = _rope_tables(x.shape[0]), _tables(gq, gk, conv_w)
    h, z, w_pairs = _forward_in(x, g1, shards)
    qn, k2, v2, a, mix, mixt = _forward_attn(z, rope, gq2, gk2, conv_wp, sinks)
    dx1, dx1b, (gw_out, gw_pg, gw_pp), acc_ple, acc_g2 = _forward_out(
        x, p, target, mix, mixt, w_out, g2, w_pg, b_pg, w_pp, g3)
    dmix = _mm_nt(dx1b, w_out, min(512, x.shape[0]), "out_bwd", dx1b)
    dz, gw_in, acc_attn, acc_qk = _backward_attn(dmix, h, z, qn, k2, v2, a, rope, gq2, gk2, conv_wp, sinks, dmix)
    grad_x, acc_g1 = _in_bwd(dz, w_pairs, x, dx1, g1, min(512, x.shape[0]), dx1)
    return grad_x, (gw_in, gw_out, gw_pg, gw_pp), (acc_g1, acc_g2, acc_ple, acc_qk, acc_attn)


def _by_owner(g):
    return g.reshape((4, 2) + g.shape[1:])


def kernel(x, p, norm_gain, w_in, q_norm_gain, k_norm_gain, attn_sinks, conv_w, w_out, ple_gate_norm_gain, w_ple_gate, b_ple_gate, w_ple_proj, ple_norm_gain, loss_target, m_norm_gain, m_w_in, m_q_norm_gain, m_k_norm_gain, m_attn_sinks, m_conv_w, m_w_out, m_ple_gate_norm_gain, m_w_ple_gate, m_b_ple_gate, m_w_ple_proj, m_ple_norm_gain, v_norm_gain, v_w_in, v_q_norm_gain, v_k_norm_gain, v_attn_sinks, v_conv_w, v_w_out, v_ple_gate_norm_gain, v_w_ple_gate, v_b_ple_gate, v_w_ple_proj, v_ple_norm_gain):
    me = 4 * lax.axis_index("x") + 2 * lax.axis_index("y") + lax.axis_index("c")
    place = jnp.stack([lax.axis_index("c"), 2 * lax.axis_index("x") + lax.axis_index("y")]).astype(jnp.int32)
    xs, ps, target = x[0], p[0, 0], loss_target[0]

    shard_in = w_in[0].T.astype(BF16)
    own_late = [w_out[0].astype(BF16), w_ple_gate[0].astype(BF16), w_ple_proj[0].astype(BF16)]
    with_own = lambda gathered, own: lax.dynamic_update_slice(gathered, own[None], (me,) + (0,) * own.ndim)
    early, started = _gather_start([shard_in, conv_w[0]], shard_in, relay=True)
    tm = min(512, xs.shape[0])
    h = _prenorm(xs, norm_gain, tm, started)

    z, w_pairs = lax.empty((xs.shape[0], IN_W), F32), lax.empty((N_PAIRS, PAIR_W, D_MODEL), BF16)
    early = _gather_from_sibling(early, h)
    pair_of = lambda flip: jnp.stack([place[1] ^ flip, place[0]])
    z, w_pairs = _fwd_in_pair(h, early["lands"][0], z, w_pairs, pair_of(0), tm, "fwd_in_own", own=shard_in)
    rope = _rope_tables(xs.shape[0])
    early, take = _gather_from_chip(early, 0, (z, *rope, *own_late), last=False)
    for j, flip in enumerate((2, 1, 3)):
        if j < 2:
            early, take_next = _gather_from_chip(early, j + 1, (z,), last=j == 1)
        if j == 1:
            late, started_late = _gather_start(own_late, z)
        early = take(early, (z, started_late) if j == 1 else (z,))
        z, w_pairs = _fwd_in_pair(h, early["lands"][0], z, w_pairs, pair_of(flip), tm, "fwd_in_chip_" + str(j))
        take = take_next
    conv_full = jnp.transpose(with_own(early["lands"][1], conv_w[0]), (1, 0, 2)).reshape(3, ATTN_W)
    gq2, gk2, conv_wp = _tables(q_norm_gain[0], k_norm_gain[0], conv_full)
    qn, k2, v2 = _qk_prep(z, *rope, gq2, gk2, min(256, xs.shape[0]), z)
    late, forwarded = _gather_forward(late, qn)
    a, mix, mixt = _attn_fwd(qn, k2, v2, z, conv_wp, attn_sinks, forwarded)
    g_out, g_pg, g_pp = (with_own(g, own) for g, own in zip(_gather_wait(late, mix), own_late))
    w_out_f = g_out.reshape(D_MODEL, D_MODEL)
    w_pg_f = g_pg.reshape(D_MODEL, D_MODEL)
    w_pp_f = jnp.transpose(g_pp, (1, 0, 2)).reshape(PLE_DIM, D_MODEL)

    dx1, dx1b, (gw_out, gw_pg, gw_pp), acc_ple, acc_g2 = _forward_out(
        xs, ps, target, mix, mixt, w_out_f, ple_gate_norm_gain, w_pg_f, b_ple_gate, w_pp_f, ple_norm_gain)

    names = ("w_out", "w_ple_gate", "w_ple_proj")
    gw_pp_t = jnp.transpose(gw_pp.reshape(PLE_DIM, N_DEV, PLE_DIM), (1, 0, 2))
    grads = [_by_owner(gw_out.reshape(N_DEV, D_MODEL // N_DEV, D_MODEL)),
             _by_owner(gw_pg.reshape(N_DEV, D_MODEL // N_DEV, D_MODEL)), _by_owner(gw_pp_t)]
    pairs, paired = _exchange_start("pair_start", grads, [(4,) + g.shape[2:] for g in grads], _to_sibling, 1, dx1b)
    dmix = _mm_nt(dx1b, w_out_f, tm, "out_bwd", paired)
    from_sibling = _exchange_wait("pair_wait", pairs, _to_sibling, (dmix,))
    sums = [_pair_sum(g, r, place, 256, "pair_sum_" + nm) for g, r, nm in zip(pairs["srcs"], from_sibling, names)]
    chips, sent = _exchange_start("chip_start", [pb for pb, _ in sums], [(3,) + pb.shape[1:] for pb, _ in sums],
                                  _to_chips, 3, sums[-1][1])

    dz, gw_in, acc_attn, acc_qk = _backward_attn(
        dmix, h, z, qn, k2, v2, a, rope, gq2, gk2, conv_wp, attn_sinks, sent)

    gw_in_t = [_by_owner(gw_in)]
    pairs_in, paired_in = _exchange_start("pair_start_w_in", gw_in_t, [(4,) + gw_in_t[0].shape[2:]], _to_sibling, 1,
                                          gw_in)
    from_chips = _exchange_wait("chip_wait", chips, _to_chips, (gw_in,))
    big = {}
    for (_, own), oth, w, m, v, nm in zip(sums, from_chips, (w_out, w_ple_gate, w_ple_proj),
                                          (m_w_out, m_w_ple_gate, m_w_ple_proj),
                                          (v_w_out, v_w_ple_gate, v_w_ple_proj), names):
        big[nm] = [t[None] for t in _adamw(own, oth, w[0], m[0], v[0], 256, "adamw_" + nm, paired_in)]

    (from_sibling_in,) = _exchange_wait("pair_wait_w_in", pairs_in, _to_sibling, [big[nm][0] for nm in names])
    pb_in, own_in = _pair_sum(pairs_in["srcs"][0], from_sibling_in, place, SHARD_IN // 2, "pair_sum_w_in")
    chips_in, sent_in = _exchange_start("chip_start_w_in", [pb_in], [(3,) + pb_in.shape[1:]], _to_chips, 3, own_in)
    grad_x, acc_g1 = _in_bwd(dz, w_pairs, xs, dx1, norm_gain, tm, sent_in)
    (from_chips_in,) = _exchange_wait("chip_wait_w_in", chips_in, _to_chips, (grad_x,))
    big["w_in"] = [t.T[None] for t in _adamw(own_in, from_chips_in, w_in[0].T, m_w_in[0].T, v_w_in[0].T, SHARD_IN // 4,
                                             "adamw_w_in", grad_x)]

    red = _reduce_small(acc_g1, acc_g2, acc_ple, acc_qk, acc_attn)
    small = [norm_gain, ple_gate_norm_gain, b_ple_gate, ple_norm_gain, q_norm_gain, k_norm_gain, attn_sinks]
    small_m = [m_norm_gain, m_ple_gate_norm_gain, m_b_ple_gate, m_ple_norm_gain, m_q_norm_gain, m_k_norm_gain,
               m_attn_sinks]
    small_v = [v_norm_gain, v_ple_gate_norm_gain, v_b_ple_gate, v_ple_norm_gain, v_q_norm_gain, v_k_norm_gain,
               v_attn_sinks]
    taps_first = lambda t: jnp.transpose(t, (1, 0, 2))
    loss, kinds = _adamw_small(red, me.reshape(1, 1).astype(jnp.int32), small + [taps_first(conv_w)],
                               small_m + [taps_first(m_conv_w)], small_v + [taps_first(v_conv_w)])

    def order(k):
        sm = kinds[k]
        return [sm[0], big["w_in"][k], sm[4], sm[5], sm[6], taps_first(sm[7]), big["w_out"][k], sm[1],
                big["w_ple_gate"][k], sm[2], big["w_ple_proj"][k], sm[3]]

    return (loss[0, 0], grad_x[None], *order(0), *order(1), *order(2), *order(3))
```

```python
import jax
import jax.numpy as jnp
from jax import lax
from jax.experimental import pallas as pl
from jax.experimental.pallas import tpu as pltpu

F32, BF16 = jnp.float32, jnp.bfloat16

D_MODEL = 2048
PLE_DIM = 256
ATTN_W = 1024
HEAD = 64
N_Q_HEADS = 16
KV_W = 256
QKV_W = ATTN_W + 2 * KV_W
REST_W = 5 * 1024
IN_W = QKV_W + REST_W
GATE_A0, CONV_B0, CONV_C0, CONV_H0, GATE_C0 = (QKV_W + 1024 * t for t in range(5))
K2_W = 4 * 128
ROT = 16
ROPE_THETA = 500000.0
EPS = 1e-6
NEG_INF = -1e30
BLK = 128
LANES = 128
SUBLANES = 8
N_DEV = 8
SHARD_IN = IN_W // N_DEV
PAIR_W = 2 * SHARD_IN
N_PAIRS = IN_W // PAIR_W
SLAB_ROWS = 8
PACKED_ROWS = 16
V7X_VMEM_LIMIT = 52 * 1024 * 1024

ADAM_LR, ADAM_B1, ADAM_B2, ADAM_EPS, ADAM_WD, ADAM_STEP = 0.001, 0.9, 0.999, 1e-08, 0.01, 10
MESH = pl.DeviceIdType.MESH


def _params(*semantics):
    return pltpu.CompilerParams(dimension_semantics=semantics, vmem_limit_bytes=V7X_VMEM_LIMIT)


ANY = pl.BlockSpec(memory_space=pl.ANY)


def _resident(shape):
    return pl.BlockSpec(shape, lambda *_: (0,) * len(shape), pipeline_mode=pl.Buffered(1))


def _dot(a, b):
    return jnp.dot(a, b, preferred_element_type=F32)


def _dot_nt(a, b):
    return lax.dot_general(a, b, (((1,), (1,)), ((), ())), preferred_element_type=F32)


def _rms(xf):
    r = lax.rsqrt(jnp.mean(xf * xf, axis=-1, keepdims=True) + EPS)
    return xf * r, r


def _rms_bwd(dxn, xn, r):
    return r * (dxn - xn * jnp.mean(dxn * xn, axis=-1, keepdims=True))


def _sig(g):
    return jax.nn.sigmoid(g)


def _dsilu(g, sg):
    return sg * (1.0 + g * (1.0 - sg))


def _low_half(shape):
    return lax.broadcasted_iota(jnp.int32, shape, len(shape) - 1) < HEAD


def _half_sums(v):
    lo = _low_half(v.shape)
    s_lo = jnp.sum(jnp.where(lo, v, 0.0), axis=-1, keepdims=True)
    s_hi = jnp.sum(jnp.where(lo, 0.0, v), axis=-1, keepdims=True)
    return jnp.where(lo, s_lo, s_hi)


def _rope(v, a, bm, bp):
    return v * a + pltpu.roll(v, LANES - ROT // 2, 1) * bm + pltpu.roll(v, ROT // 2, 1) * bp


def _rope_t(dy, a, bm, bp):
    return dy * a + pltpu.roll(dy * bm, ROT // 2, 1) + pltpu.roll(dy * bp, LANES - ROT // 2, 1)


def _dup_halves(v):
    lo = _low_half(v.shape)
    a = jnp.where(lo, v, 0.0)
    b = jnp.where(lo, 0.0, v)
    return a + pltpu.roll(a, HEAD, 1), b + pltpu.roll(b, HEAD, 1)


def _rope_tables(s):
    half = ROT // 2
    lane = lax.broadcasted_iota(jnp.int32, (s, LANES), 1) % HEAD
    pos = lax.broadcasted_iota(jnp.int32, (half, s), 1).astype(F32)
    freq = lax.broadcasted_iota(jnp.int32, (half, s), 0).astype(F32)
    ang = pos * jnp.power(jnp.float32(ROPE_THETA), -freq * 2.0 / ROT)
    cos, sin = lax.optimization_barrier((jnp.cos(ang), jnp.sin(ang)))
    cos, sin = (jnp.tile(t.T, (1, LANES // half)) for t in (cos, sin))
    a = jnp.where(lane < ROT, cos, 1.0)
    bm = jnp.where(lane < half, -sin, 0.0)
    bp = jnp.where((lane >= half) & (lane < ROT), sin, 0.0)
    return a, bm, bp


def _prenorm(x, g1, tm, after):
    s = x.shape[0]

    def body(x_ref, g_ref, after_ref, h_ref):
        xn, _ = _rms(x_ref[...])
        h_ref[...] = (xn * g_ref[...]).astype(BF16)

    return pl.pallas_call(
        body, name="prenorm",
        out_shape=jax.ShapeDtypeStruct((s, D_MODEL), BF16),
        grid=(s // tm,),
        in_specs=[pl.BlockSpec((tm, D_MODEL), lambda i: (i, 0)), pl.BlockSpec((1, D_MODEL), lambda i: (0, 0)), ANY],
        out_specs=pl.BlockSpec((tm, D_MODEL), lambda i: (i, 0)),
        compiler_params=_params("parallel"))(x, g1, after)


def _fwd_in_pair(h, shards, z, w_pairs, pair, tm, name, own=None):
    s = h.shape[0]

    def body(pair_ref, h_ref, lo_ref, hi_ref, z_in, wp_in, z_ref, wp_ref):
        @pl.when(pl.program_id(0) == 0)
        def _():
            wp_ref[0, 0:SHARD_IN, :] = lo_ref[0]
            wp_ref[0, SHARD_IN:PAIR_W, :] = hi_ref[0]

        z_ref[...] = _dot_nt(h_ref[...], wp_ref[0])

    def body_own(pair_ref, h_ref, own_ref, other_ref, z_in, wp_in, z_ref, wp_ref):
        @pl.when(pl.program_id(0) == 0)
        def _():
            first = pl.multiple_of(pair_ref[1] * SHARD_IN, SHARD_IN)
            wp_ref[0, pl.ds(first, SHARD_IN), :] = own_ref[...]
            wp_ref[0, pl.ds(SHARD_IN - first, SHARD_IN), :] = other_ref[0]

        z_ref[...] = _dot_nt(h_ref[...], wp_ref[0])

    if own is None:
        blocks = [pl.BlockSpec((1, SHARD_IN, D_MODEL), lambda i, p: (2 * p[0], 0, 0)),
                  pl.BlockSpec((1, SHARD_IN, D_MODEL), lambda i, p: (2 * p[0] + 1, 0, 0))]
        operands = (shards, shards)
    else:
        blocks = [pl.BlockSpec((SHARD_IN, D_MODEL), lambda i, p: (0, 0)),
                  pl.BlockSpec((1, SHARD_IN, D_MODEL), lambda i, p: (2 * p[0] + 1 - p[1], 0, 0))]
        operands = (own, shards)
    grid_spec = pltpu.PrefetchScalarGridSpec(
        num_scalar_prefetch=1, grid=(s // tm,),
        in_specs=[pl.BlockSpec((tm, D_MODEL), lambda i, p: (i, 0)), *blocks, ANY, ANY],
        out_specs=(pl.BlockSpec((tm, PAIR_W), lambda i, p: (i, p[0])),
                   pl.BlockSpec((1, PAIR_W, D_MODEL), lambda i, p: (p[0], 0, 0))))
    return pl.pallas_call(
        body if own is None else body_own, name=name, grid_spec=grid_spec,
        out_shape=(jax.ShapeDtypeStruct(z.shape, z.dtype), jax.ShapeDtypeStruct(w_pairs.shape, w_pairs.dtype)),
        input_output_aliases={4: 0, 5: 1},
        compiler_params=_params("arbitrary"))(pair, h, *operands, z, w_pairs)


def _qk_prep(z, ra, rbm, rbp, gq2, gk2, tm, after):
    s = z.shape[0]

    def body(z_ref, a_ref, bm_ref, bp_ref, gq_ref, gk_ref, after_ref, q_ref, k2_ref, v2_ref):
        a, bm, bp = a_ref[...], bm_ref[...], bp_ref[...]
        for r in range(ATTN_W // LANES):
            x = z_ref[:, LANES * r:LANES * (r + 1)]
            rr = lax.rsqrt(_half_sums(x * x) * (1.0 / HEAD) + EPS)
            q_ref[:, LANES * r:LANES * (r + 1)] = _rope(x * rr * gq_ref[...], a, bm, bp).astype(BF16)
        for m in range(KV_W // LANES):
            x = z_ref[:, ATTN_W + LANES * m:ATTN_W + LANES * (m + 1)]
            rr = lax.rsqrt(_half_sums(x * x) * (1.0 / HEAD) + EPS)
            k_lo, k_hi = _dup_halves(_rope(x * rr * gk_ref[...], a, bm, bp))
            k2_ref[:, 2 * LANES * m:2 * LANES * m + LANES] = k_lo.astype(BF16)
            k2_ref[:, 2 * LANES * m + LANES:2 * LANES * (m + 1)] = k_hi.astype(BF16)
            v_lo, v_hi = _dup_halves(z_ref[:, ATTN_W + KV_W + LANES * m:ATTN_W + KV_W + LANES * (m + 1)])
            v2_ref[:, 2 * LANES * m:2 * LANES * m + LANES] = v_lo.astype(BF16)
            v2_ref[:, 2 * LANES * m + LANES:2 * LANES * (m + 1)] = v_hi.astype(BF16)

    row = lambda w: pl.BlockSpec((tm, w), lambda i: (i, 0))
    one = pl.BlockSpec((1, LANES), lambda i: (0, 0))
    return pl.pallas_call(
        body, name="qk_prep",
        out_shape=(jax.ShapeDtypeStruct((s, ATTN_W), BF16), jax.ShapeDtypeStruct((s, K2_W), BF16),
                   jax.ShapeDtypeStruct((s, K2_W), BF16)),
        grid=(s // tm,),
        in_specs=[row(PAIR_W), row(LANES), row(LANES), row(LANES), one, one, ANY],
        out_specs=(row(ATTN_W), row(K2_W), row(K2_W)),
        compiler_params=_params("parallel"))(z, ra, rbm, rbp, gq2, gk2, after)


GROUP = 4


def _window_mask(n):
    row = lax.broadcasted_iota(jnp.int32, (GROUP * BLK, 2 * BLK), 0) % BLK
    col = lax.broadcasted_iota(jnp.int32, (GROUP * BLK, 2 * BLK), 1)
    return (col > row) & (col <= row + BLK) & ((col >= BLK) | (n > 0))


def _stack_heads(pairs, zero):
    lo = _low_half(pairs[0].shape)
    parts = []
    for v in pairs:
        parts += [jnp.where(lo, v, zero), jnp.where(lo, zero, v)]
    return jnp.concatenate(parts, axis=0)


def _unstack_heads(v4):
    lo = _low_half((BLK, LANES))
    return [jnp.where(lo, v4[2 * i * BLK:(2 * i + 1) * BLK], v4[(2 * i + 1) * BLK:(2 * i + 2) * BLK]) for i in range(2)]


def _group_sinks(sink_ref, kvh):
    slot = lax.broadcasted_iota(jnp.int32, (GROUP * BLK, 1), 0) // BLK
    col = jnp.zeros((GROUP * BLK, 1), F32)
    for i in range(GROUP):
        col = jnp.where(slot == i, sink_ref[0, GROUP * kvh + i], col)
    return col, slot


def _head_probs(qm, kw, valid, sink):
    sc = jnp.where(valid, _dot_nt(qm, kw) * (HEAD ** -0.5), NEG_INF)
    mx = jnp.maximum(jnp.max(sc, axis=-1, keepdims=True), sink)
    ex = jnp.exp(sc - mx)
    den = jnp.sum(ex, axis=-1, keepdims=True) + jnp.exp(sink - mx)
    return ex / den, mx, den


def _cols(start, width=ATTN_W):
    return slice(start, start + width)


def _conv_fwd(z_ref, zp_ref, cw_ref, ext_ref, n):
    u = z_ref[:, _cols(CONV_C0)] * z_ref[:, _cols(CONV_H0)]
    pu = zp_ref[:, _cols(CONV_C0)] * zp_ref[:, _cols(CONV_H0)]
    ext_ref[0:SUBLANES, :] = jnp.where(n > 0, pu, 0.0)
    ext_ref[SUBLANES:SUBLANES + BLK, :] = u
    um1 = ext_ref[SUBLANES - 1:SUBLANES - 1 + BLK, :]
    um2 = ext_ref[SUBLANES - 2:SUBLANES - 2 + BLK, :]
    cv = cw_ref[0:1, :] * um2 + cw_ref[1:2, :] * um1 + cw_ref[2:3, :] * u
    return u, um1, um2, cv


def _prev_rows(n):
    return (jnp.maximum(n * (BLK // SUBLANES) - 1, 0), 0)


def _attn_fwd(qn, k2, v2, z, conv_wp, sinks, after):
    s = qn.shape[0]
    nb = s // BLK
    ring = 3

    def body(sink_ref, q_ref, kc_ref, kp_ref, vc_ref, vp_ref, z_hbm, zp_ref, cw_ref, after_ref, a_ref, mix_ref,
             mixt_ref, ext_ref, z_buf, z_sems):
        n = pl.program_id(0)

        def fetch(step):
            used = pl.ds(QKV_W, REST_W)
            return pltpu.make_async_copy(z_hbm.at[pl.ds(pl.multiple_of(step * BLK, BLK), BLK), used],
                                         z_buf.at[step % ring, :, used], z_sems.at[step % ring])

        @pl.when(n == 0)
        def _():
            for step in range(min(ring - 1, nb)):
                fetch(step).start()

        @pl.when(n + ring - 1 < nb)
        def _():
            fetch(n + ring - 1).start()

        fetch(n).wait()
        z_ref = z_buf.at[n % ring]
        valid = _window_mask(n)
        for kvh in range(K2_W // LANES):
            cols = slice(LANES * kvh, LANES * (kvh + 1))
            kw = jnp.concatenate([kp_ref[:, cols], kc_ref[:, cols]], axis=0)
            vw = jnp.concatenate([vp_ref[:, cols], vc_ref[:, cols]], axis=0)
            blocks = [slice(LANES * r, LANES * (r + 1)) for r in (2 * kvh, 2 * kvh + 1)]
            q4 = _stack_heads([q_ref[:, rc] for rc in blocks], jnp.zeros((BLK, LANES), BF16))
            p, _, _ = _head_probs(q4, kw, valid, _group_sinks(sink_ref, kvh)[0])
            for rc, a in zip(blocks, _unstack_heads(_dot(p.astype(BF16), vw))):
                a_ref[:, rc] = a
                g = z_ref[:, _cols(GATE_A0 + rc.start, LANES)]
                mix_ref[:, rc] = (a * (g * _sig(g))).astype(BF16)
        _, _, _, cv = _conv_fwd(z_ref, zp_ref, cw_ref, ext_ref, n)
        gc = z_ref[:, _cols(GATE_C0)]
        mix_ref[:, ATTN_W:D_MODEL] = (z_ref[:, _cols(CONV_B0)] * cv * (gc * _sig(gc))).astype(BF16)
        mixt_ref[...] = mix_ref[...].T

    cur = lambda w: pl.BlockSpec((BLK, w), lambda n: (n, 0))
    prev = lambda w: pl.BlockSpec((BLK, w), lambda n: (jnp.maximum(n - 1, 0), 0))
    return pl.pallas_call(
        body, name="attn_fwd",
        out_shape=(jax.ShapeDtypeStruct((s, ATTN_W), F32), jax.ShapeDtypeStruct((s, D_MODEL), BF16),
                   jax.ShapeDtypeStruct((D_MODEL, s), BF16)),
        grid=(nb,),
        in_specs=[pl.BlockSpec(memory_space=pltpu.SMEM),
                  cur(ATTN_W), cur(K2_W), prev(K2_W), cur(K2_W), prev(K2_W), ANY,
                  pl.BlockSpec((SUBLANES, IN_W), _prev_rows),
                  pl.BlockSpec((SUBLANES, ATTN_W), lambda n: (0, 0)), ANY],
        out_specs=(cur(ATTN_W), cur(D_MODEL), pl.BlockSpec((D_MODEL, BLK), lambda n: (0, n))),
        scratch_shapes=[pltpu.VMEM((BLK + 2 * SUBLANES, ATTN_W), F32), pltpu.VMEM((ring, BLK, IN_W), F32),
                        pltpu.SemaphoreType.DMA((ring,))],
        compiler_params=_params("arbitrary"))(sinks, qn, k2, k2, v2, v2, z, z, conv_wp, after)


def _fwd_out(mix, w_out, x, g2, tm):
    s = x.shape[0]

    def body(m_ref, w_ref, x_ref, g_ref, x1_ref, h_ref, ht_ref):
        x1 = x_ref[...] + _dot(m_ref[...], w_ref[...])
        x1_ref[...] = x1
        xn, _ = _rms(x1)
        h = (xn * g_ref[...]).astype(BF16)
        h_ref[...] = h
        ht_ref[...] = h.T

    row = pl.BlockSpec((tm, D_MODEL), lambda i: (i, 0))
    return pl.pallas_call(
        body, name="fwd_out",
        out_shape=(jax.ShapeDtypeStruct((s, D_MODEL), F32), jax.ShapeDtypeStruct((s, D_MODEL), BF16),
                   jax.ShapeDtypeStruct((D_MODEL, s), BF16)),
        grid=(s // tm,),
        in_specs=[row, _resident((D_MODEL, D_MODEL)), row, pl.BlockSpec((1, D_MODEL), lambda i: (0, 0))],
        out_specs=(row, row, pl.BlockSpec((D_MODEL, tm), lambda i: (0, i))),
        compiler_params=_params("parallel"))(mix, w_out, x, g2)


def _ple(hn2, w_pg, b_pg, p, w_pp, g3, x1, target, tm):
    s = x1.shape[0]

    def body(h_ref, wg_ref, b_ref, p_ref, wp_ref, g3_ref, x1_ref, t_ref, dy_ref, dgp_ref, dt_ref, pt_ref, acc_ref):
        gate = _sig(_dot(h_ref[...], wg_ref[...]) + b_ref[...])
        pb = p_ref[...].astype(BF16)
        pt_ref[...] = pb.T
        t = _dot(pb, wp_ref[...])
        tn, r3 = _rms(t)
        e = tn * g3_ref[...]
        diff = x1_ref[...] + gate * e - t_ref[...]
        dy = diff * (1.0 / D_MODEL)
        dy_ref[...] = dy
        dgp = dy * e * (gate * (1.0 - gate))
        dgp_ref[...] = dgp.astype(BF16)
        de = dy * gate
        dt_ref[...] = _rms_bwd(de * g3_ref[...], tn, r3).astype(BF16)

        @pl.when(pl.program_id(0) == 0)
        def _():
            acc_ref[...] = jnp.zeros_like(acc_ref)

        acc_ref[0:1, :] += jnp.sum(dgp, axis=0, keepdims=True)
        acc_ref[1:2, :] += jnp.sum(de * tn, axis=0, keepdims=True)
        acc_ref[2:3, :] += jnp.sum(diff * diff, axis=0, keepdims=True) * (0.5 / D_MODEL)

    row = pl.BlockSpec((tm, D_MODEL), lambda i: (i, 0))
    vec = pl.BlockSpec((1, D_MODEL), lambda i: (0, 0))
    return pl.pallas_call(
        body, name="ple",
        out_shape=(jax.ShapeDtypeStruct((s, D_MODEL), F32), jax.ShapeDtypeStruct((s, D_MODEL), BF16),
                   jax.ShapeDtypeStruct((s, D_MODEL), BF16), jax.ShapeDtypeStruct((PLE_DIM, s), BF16),
                   jax.ShapeDtypeStruct((SUBLANES, D_MODEL), F32)),
        grid=(s // tm,),
        in_specs=[row, _resident((D_MODEL, D_MODEL)), vec, pl.BlockSpec((tm, PLE_DIM), lambda i: (i, 0)),
                  _resident((PLE_DIM, D_MODEL)), vec, row, row],
        out_specs=(row, row, row, pl.BlockSpec((PLE_DIM, tm), lambda i: (0, i)),
                   pl.BlockSpec((SUBLANES, D_MODEL), lambda i: (0, 0))),
        compiler_params=_params("arbitrary"))(hn2, w_pg, b_pg, p, w_pp, g3, x1, target)


def _gate_bwd(dgp, w_pg, x1, dy, g2, tm):
    s = x1.shape[0]

    def body(d_ref, w_ref, x1_ref, dy_ref, g_ref, dx_ref, dxb_ref, acc_ref):
        dh = _dot_nt(d_ref[...], w_ref[...])
        xn, r = _rms(x1_ref[...])
        dx1 = dy_ref[...] + _rms_bwd(dh * g_ref[...], xn, r)
        dx_ref[...] = dx1
        dxb_ref[...] = dx1.astype(BF16)

        @pl.when(pl.program_id(0) == 0)
        def _():
            acc_ref[...] = jnp.zeros_like(acc_ref)

        acc_ref[0:1, :] += jnp.sum(dh * xn, axis=0, keepdims=True)

    row = pl.BlockSpec((tm, D_MODEL), lambda i: (i, 0))
    return pl.pallas_call(
        body, name="gate_bwd",
        out_shape=(jax.ShapeDtypeStruct((s, D_MODEL), F32), jax.ShapeDtypeStruct((s, D_MODEL), BF16),
                   jax.ShapeDtypeStruct((SUBLANES, D_MODEL), F32)),
        grid=(s // tm,),
        in_specs=[row, _resident((D_MODEL, D_MODEL)), row, row, pl.BlockSpec((1, D_MODEL), lambda i: (0, 0))],
        out_specs=(row, row, pl.BlockSpec((SUBLANES, D_MODEL), lambda i: (0, 0))),
        compiler_params=_params("arbitrary"))(dgp, w_pg, x1, dy, g2)


def _mm_nt(a, b, tm, name, after):
    m, k = a.shape
    n = b.shape[0]

    def body(a_ref, b_ref, after_ref, o_ref):
        o_ref[...] = _dot_nt(a_ref[...], b_ref[...])

    return pl.pallas_call(
        body, name=name,
        out_shape=jax.ShapeDtypeStruct((m, n), F32),
        grid=(m // tm,),
        in_specs=[pl.BlockSpec((tm, k), lambda i: (i, 0)), _resident((n, k)), ANY],
        out_specs=pl.BlockSpec((tm, n), lambda i: (i, 0)),
        compiler_params=_params("parallel"))(a, b, after)


def _attn_bwd(qn, k2, v2, a, z, dmix, conv_wp, sinks, after):
    s = qn.shape[0]
    nb = s // BLK

    def body(sink_ref, q_ref, kc_ref, kp_ref, vc_ref, vp_ref, a_ref, z_ref, zp_ref, zn_ref, dm_ref, dmn_ref,
             cw_ref, after_ref, dq_ref, dkc_ref, dkp_ref, dvc_ref, dvp_ref, dz_ref, dzt_ref, acc_ref, ext_ref):
        n = pl.program_id(0)
        valid = _window_mask(n)
        lane = lax.broadcasted_iota(jnp.int32, (1, ATTN_W), 1)

        @pl.when(n == 0)
        def _():
            acc_ref[...] = jnp.zeros_like(acc_ref)

        dz_ref[:, 0:QKV_W] = jnp.zeros((BLK, QKV_W), BF16)
        dsink = jnp.zeros((1, ATTN_W), F32)
        for kvh in range(K2_W // LANES):
            cols = slice(LANES * kvh, LANES * (kvh + 1))
            kw = jnp.concatenate([kp_ref[:, cols], kc_ref[:, cols]], axis=0)
            vw = jnp.concatenate([vp_ref[:, cols], vc_ref[:, cols]], axis=0)
            blocks = [slice(LANES * r, LANES * (r + 1)) for r in (2 * kvh, 2 * kvh + 1)]
            das, avs = [], []
            for rc in blocks:
                g = z_ref[:, _cols(GATE_A0 + rc.start, LANES)]
                sg = _sig(g)
                dm = dm_ref[:, rc]
                av = a_ref[:, rc]
                das.append(dm * (g * sg))
                avs += [av, av]
                dz_ref[:, _cols(GATE_A0 + rc.start, LANES)] = (dm * av * _dsilu(g, sg)).astype(BF16)
            q4 = _stack_heads([q_ref[:, rc] for rc in blocks], jnp.zeros((BLK, LANES), BF16))
            sink, slot = _group_sinks(sink_ref, kvh)
            p, mx, den = _head_probs(q4, kw, valid, sink)
            do4 = _stack_heads(das, 0.0)
            delta = jnp.sum(do4 * jnp.concatenate(avs, axis=0), axis=-1, keepdims=True)
            dob = do4.astype(BF16)
            ds = p * (_dot_nt(dob, vw) - delta) * (HEAD ** -0.5)
            for rc, dq in zip(blocks, _unstack_heads(_dot(ds.astype(BF16), kw))):
                dq_ref[:, rc] = dq
            dk2 = _dot(ds.T.astype(BF16), q4)
            dv2 = _dot(p.T.astype(BF16), dob)
            dkp_ref[:, cols] = dk2[0:BLK]
            dkc_ref[:, cols] = dk2[BLK:2 * BLK]
            dvp_ref[:, cols] = dv2[0:BLK]
            dvc_ref[:, cols] = dv2[BLK:2 * BLK]
            dsk = jnp.exp(sink - mx) / den * delta
            for i in range(GROUP):
                dsink = dsink - jnp.where(lane == GROUP * kvh + i,
                                          jnp.sum(jnp.where(slot == i, dsk, 0.0), axis=0, keepdims=True), 0.0)
        acc_ref[0:1, :] += dsink

        u, um1, um2, cv = _conv_fwd(z_ref, zp_ref, cw_ref, ext_ref, n)
        cb = z_ref[:, _cols(CONV_B0)]
        gc = z_ref[:, _cols(GATE_C0)]
        sgc = _sig(gc)
        dmc = dm_ref[:, ATTN_W:D_MODEL]
        t = dmc * (gc * sgc)
        dcv = t * cb
        dz_ref[:, _cols(CONV_B0)] = (t * cv).astype(BF16)
        dz_ref[:, _cols(GATE_C0)] = (dmc * cb * cv * _dsilu(gc, sgc)).astype(BF16)
        gcn = zn_ref[:, _cols(GATE_C0)]
        dcvn = dmn_ref[:, ATTN_W:D_MODEL] * (gcn * _sig(gcn)) * zn_ref[:, _cols(CONV_B0)]
        ext_ref[0:BLK, :] = dcv
        ext_ref[BLK:BLK + SUBLANES, :] = jnp.where(n < nb - 1, dcvn, 0.0)
        du = (cw_ref[2:3, :] * dcv + cw_ref[1:2, :] * ext_ref[1:1 + BLK, :]
              + cw_ref[0:1, :] * ext_ref[2:2 + BLK, :])
        dz_ref[:, _cols(CONV_C0)] = (du * z_ref[:, _cols(CONV_H0)]).astype(BF16)
        dz_ref[:, _cols(CONV_H0)] = (du * z_ref[:, _cols(CONV_C0)]).astype(BF16)
        acc_ref[1:2, :] += jnp.sum(dcv * um2, axis=0, keepdims=True)
        acc_ref[2:3, :] += jnp.sum(dcv * um1, axis=0, keepdims=True)
        acc_ref[3:4, :] += jnp.sum(dcv * u, axis=0, keepdims=True)
        dzt_ref[...] = dz_ref[...].T

    cur = lambda w: pl.BlockSpec((BLK, w), lambda n: (n, 0))
    prev = lambda w: pl.BlockSpec((BLK, w), lambda n: (jnp.maximum(n - 1, 0), 0))
    nxt = lambda w: pl.BlockSpec(
        (SUBLANES, w), lambda n: (jnp.minimum((n + 1) * (BLK // SUBLANES), nb * (BLK // SUBLANES) - 1), 0))
    f32 = lambda w: jax.ShapeDtypeStruct((s, w), F32)
    return pl.pallas_call(
        body, name="attn_bwd",
        out_shape=(f32(ATTN_W), f32(K2_W), f32(K2_W), f32(K2_W), f32(K2_W),
                   jax.ShapeDtypeStruct((s, IN_W), BF16), jax.ShapeDtypeStruct((IN_W, s), BF16),
                   jax.ShapeDtypeStruct((SUBLANES, ATTN_W), F32)),
        grid=(nb,),
        in_specs=[pl.BlockSpec(memory_space=pltpu.SMEM),
                  cur(ATTN_W), cur(K2_W), prev(K2_W), cur(K2_W), prev(K2_W), cur(ATTN_W), cur(IN_W),
                  pl.BlockSpec((SUBLANES, IN_W), _prev_rows), nxt(IN_W), cur(D_MODEL), nxt(D_MODEL),
                  pl.BlockSpec((SUBLANES, ATTN_W), lambda n: (0, 0)), ANY],
        out_specs=(cur(ATTN_W), cur(K2_W), cur(K2_W), cur(K2_W), cur(K2_W), cur(IN_W),
                   pl.BlockSpec((IN_W, BLK), lambda n: (0, n)), pl.BlockSpec((SUBLANES, ATTN_W), lambda n: (0, 0))),
        scratch_shapes=[pltpu.VMEM((BLK + 2 * SUBLANES, ATTN_W), F32)],
        compiler_params=_params("arbitrary"))(sinks, qn, k2, k2, v2, v2, a, z, z, z, dmix, dmix, conv_wp, after)


def _qkv_bwd(z, dz, dzt, dq, dkc, dkp, dvc, dvp, ra, rbm, rbp, gq2, gk2):
    s = z.shape[0]
    nb = s // BLK

    def body(z_ref, dz_in, dzt_in, dq_ref, dkc_ref, dkp_ref, dvc_ref, dvp_ref, a_ref, bm_ref, bp_ref, gq_ref, gk_ref,
             dz_ref, dzt_ref, acc_ref):
        n = pl.program_id(0)
        a, bm, bp = a_ref[...], bm_ref[...], bp_ref[...]
        lo = _low_half((BLK, LANES))
        last = n == nb - 1

        @pl.when(n == 0)
        def _():
            acc_ref[...] = jnp.zeros_like(acc_ref)

        def norm_bwd(x, dy, gain):
            rr = lax.rsqrt(_half_sums(x * x) * (1.0 / HEAD) + EPS)
            xh = x * rr
            dxg = _rope_t(dy, a, bm, bp)
            dxh = dxg * gain
            dx = rr * (dxh - xh * (_half_sums(dxh * xh) * (1.0 / HEAD)))
            return dx, jnp.sum(dxg * xh, axis=0, keepdims=True)

        def folded(cur_ref, prev_ref, m):
            parts = []
            for h in (2 * m, 2 * m + 1):
                v = cur_ref[:, LANES * h:LANES * (h + 1)] + jnp.where(
                    last, 0.0, prev_ref[:, LANES * h:LANES * (h + 1)])
                parts.append(v + pltpu.roll(v, HEAD, 1))
            return jnp.where(lo, parts[0], parts[1])

        gq_acc = jnp.zeros((1, LANES), F32)
        for r in range(ATTN_W // LANES):
            rc = slice(LANES * r, LANES * (r + 1))
            dx, gg = norm_bwd(z_ref[:, rc], dq_ref[:, rc], gq_ref[...])
            dz_ref[:, rc] = dx.astype(BF16)
            gq_acc = gq_acc + gg
        acc_ref[0:1, :] += gq_acc
        gk_acc = jnp.zeros((1, LANES), F32)
        for m in range(KV_W // LANES):
            kc = slice(ATTN_W + LANES * m, ATTN_W + LANES * (m + 1))
            dx, gg = norm_bwd(z_ref[:, kc], folded(dkc_ref, dkp_ref, m), gk_ref[...])
            dz_ref[:, kc] = dx.astype(BF16)
            gk_acc = gk_acc + gg
            vc = slice(ATTN_W + KV_W + LANES * m, ATTN_W + KV_W + LANES * (m + 1))
            dz_ref[:, vc] = folded(dvc_ref, dvp_ref, m).astype(BF16)
        acc_ref[1:2, :] += gk_acc
        dzt_ref[...] = dz_ref[...].T

    cur = lambda w: pl.BlockSpec((BLK, w), lambda n: (n, 0))
    nxt = lambda w: pl.BlockSpec((BLK, w), lambda n: (jnp.minimum(n + 1, nb - 1), 0))
    one = pl.BlockSpec((1, LANES), lambda n: (0, 0))
    return pl.pallas_call(
        body, name="qkv_bwd",
        out_shape=(jax.ShapeDtypeStruct(dz.shape, dz.dtype), jax.ShapeDtypeStruct(dzt.shape, dzt.dtype),
                   jax.ShapeDtypeStruct((SUBLANES, LANES), F32)),
        grid=(nb,),
        in_specs=[cur(PAIR_W), ANY, ANY, cur(ATTN_W), cur(K2_W), nxt(K2_W), cur(K2_W), nxt(K2_W),
                  cur(LANES), cur(LANES), cur(LANES), one, one],
        out_specs=(cur(QKV_W), pl.BlockSpec((QKV_W, BLK), lambda n: (0, n)),
                   pl.BlockSpec((SUBLANES, LANES), lambda n: (0, 0))),
        input_output_aliases={1: 0, 2: 1},
        compiler_params=_params("arbitrary"))(z, dz, dzt, dq, dkc, dkp, dvc, dvp, ra, rbm, rbp, gq2, gk2)


def _in_bwd(dz, w_pairs, x, dx1, g1, tm, after):
    s = x.shape[0]
    n = s // tm
    sub = tm // N_PAIRS
    stripes = 4

    def body(d_ref, w_ref, x_ref, dx1_ref, g_ref, after_ref, gx_ref, acc_ref, dh_ref):
        i, k = pl.program_id(0), pl.program_id(1)

        def matmul(c):
            cols = slice(c * (D_MODEL // stripes), (c + 1) * (D_MODEL // stripes))
            dh_ref[i % 2, :, cols] += _dot(d_ref[...], w_ref[0, :, cols])

        def norm_bwd(c):
            part = sub // stripes
            mine = slice(c * part, (c + 1) * part)
            rows = pl.ds(pl.multiple_of(k * sub + c * part, part), part)
            dh = dh_ref[(i + 1) % 2, rows, :]
            dh_ref[(i + 1) % 2, rows, :] = jnp.zeros_like(dh)
            xn, r = _rms(x_ref[mine, :])
            gx_ref[rows, :] = dx1_ref[mine, :] + _rms_bwd(dh * g_ref[...], xn, r)
            acc_ref[0:1, :] += jnp.sum(dh * xn, axis=0, keepdims=True)

        @pl.when((i == 0) & (k == 0))
        def _():
            acc_ref[...] = jnp.zeros_like(acc_ref)
            dh_ref[...] = jnp.zeros_like(dh_ref)

        @pl.when(i == 0)
        def _():
            for c in range(stripes):
                matmul(c)

        @pl.when((i > 0) & (i < n))
        def _():
            for c in range(stripes):
                matmul(c)
                norm_bwd(c)

        @pl.when(i == n)
        def _():
            for c in range(stripes):
                norm_bwd(c)

    last = lambda i, k: jnp.where(i == n, N_PAIRS - 1, k)
    rows_before = lambda i, k: (jnp.maximum(i - 1, 0) * N_PAIRS + k, 0)
    return pl.pallas_call(
        body, name="in_bwd",
        out_shape=(jax.ShapeDtypeStruct((s, D_MODEL), F32), jax.ShapeDtypeStruct((SUBLANES, D_MODEL), F32)),
        grid=(n + 1, N_PAIRS),
        in_specs=[pl.BlockSpec((tm, PAIR_W), lambda i, k: (jnp.minimum(i, n - 1), last(i, k))),
                  pl.BlockSpec((1, PAIR_W, D_MODEL), lambda i, k: (last(i, k), 0, 0)),
                  pl.BlockSpec((sub, D_MODEL), rows_before), pl.BlockSpec((sub, D_MODEL), rows_before),
                  pl.BlockSpec((1, D_MODEL), lambda i, k: (0, 0)), ANY],
        out_specs=(pl.BlockSpec((tm, D_MODEL), lambda i, k: (jnp.maximum(i - 1, 0), 0)),
                   pl.BlockSpec((SUBLANES, D_MODEL), lambda i, k: (0, 0))),
        scratch_shapes=[pltpu.VMEM((2, tm, D_MODEL), F32)],
        compiler_params=_params("arbitrary", "arbitrary"))(dz, w_pairs, x, dx1, g1, after)


def _mm_grad(at, bs, tn, name):
    m, kdim = at.shape
    nblk = [b.shape[1] // tn for b in bs]
    starts = [sum(nblk[:t]) for t in range(len(bs))]

    def body(a_ref, *refs):
        b_refs, o_ref = refs[:len(bs)], refs[len(bs)]
        j = pl.program_id(0)
        for t, b_ref in enumerate(b_refs):
            @pl.when((j >= starts[t]) & (j < starts[t] + nblk[t]))
            def _():
                o_ref[...] = _dot(a_ref[...], b_ref[...]).astype(BF16)

    def b_spec(t):
        return pl.BlockSpec((kdim, tn), lambda j: (0, jnp.clip(j - starts[t], 0, nblk[t] - 1)))

    return pl.pallas_call(
        body, name=name,
        out_shape=jax.ShapeDtypeStruct((m, sum(nblk) * tn), BF16),
        grid=(sum(nblk),),
        in_specs=[_resident((m, kdim))] + [b_spec(t) for t in range(len(bs))],
        out_specs=pl.BlockSpec((m, tn), lambda j: (0, j)),
        compiler_params=_params("parallel"))(at, *bs)


def _grad_w_in(dzt, h):
    kdim = h.shape[0]

    def body(d_ref, h_ref, o_ref):
        o_ref[0] = _dot(d_ref[...], h_ref[...]).astype(BF16)

    return pl.pallas_call(
        body, name="grad_w_in",
        out_shape=jax.ShapeDtypeStruct((N_DEV, SHARD_IN, D_MODEL), BF16),
        grid=(N_DEV,),
        in_specs=[pl.BlockSpec((SHARD_IN, kdim), lambda j: (j, 0)), _resident((kdim, D_MODEL))],
        out_specs=pl.BlockSpec((1, SHARD_IN, D_MODEL), lambda j: (j, 0, 0)),
        compiler_params=_params("parallel"))(dzt, h)


def _place():
    return lax.axis_index("x"), lax.axis_index("y"), lax.axis_index("c")


ROW_TAPS, ROW_MISC = 4, 5
Q_AT, K_AT, SINK_AT, LOSS_AT = (ATTN_W + LANES * t for t in range(4))
SMALL_AT = [(0, 0), (1, 0), (2, 0), (3, 0), (ROW_MISC, Q_AT), (ROW_MISC, K_AT), (ROW_MISC, SINK_AT)]


def _tap_at(tap):
    return ROW_TAPS + tap // 2, ATTN_W * (tap % 2)


def _pack_small(acc_g1, acc_g2, acc_ple, acc_qk, acc_attn):
    def body(g1_ref, g2_ref, ple_ref, qk_ref, attn_ref, slab_ref):
        slab_ref[...] = jnp.zeros_like(slab_ref)
        slab_ref[0:1, :] = g1_ref[0:1, :]
        slab_ref[1:2, :] = g2_ref[0:1, :]
        slab_ref[2:4, :] = ple_ref[0:2, :]
        qk = qk_ref[0:2, :]
        qk = jnp.where(_low_half(qk.shape), qk + pltpu.roll(qk, HEAD, 1), 0.0)
        misc = slab_ref.at[ROW_MISC:ROW_MISC + 1]
        misc[:, Q_AT:Q_AT + LANES] = qk[0:1]
        misc[:, K_AT:K_AT + LANES] = qk[1:2]
        lane = lax.broadcasted_iota(jnp.int32, (1, LANES), 1)
        misc[:, SINK_AT:SINK_AT + LANES] = jnp.where(lane < N_Q_HEADS, attn_ref[0:1, 0:LANES], 0.0)
        misc[:, LOSS_AT:LOSS_AT + LANES] = sum(
            ple_ref[2:3, LANES * t:LANES * (t + 1)] for t in range(D_MODEL // LANES))
        for tap in range(3):
            row, at = _tap_at(tap)
            slab_ref[row:row + 1, at:at + ATTN_W] = attn_ref[1 + tap:2 + tap, :]

    vmem = pl.BlockSpec(memory_space=pltpu.VMEM)
    return pl.pallas_call(
        body, name="pack_small", out_shape=jax.ShapeDtypeStruct((SLAB_ROWS, D_MODEL), F32),
        in_specs=[vmem] * 5, out_specs=vmem)(acc_g1, acc_g2, acc_ple, acc_qk, acc_attn)


def _pair_sum(g, r, place, tr, name):
    _, _, rows, cols = g.shape

    def body(place_ref, g_ref, r_ref, pb_ref, own_ref):
        tot = g_ref[0, 0].astype(F32) + r_ref[0].astype(F32)
        pb_ref[0] = tot.astype(BF16)

        @pl.when(pl.program_id(1) == place_ref[1])
        def _():
            own_ref[...] = tot

    grid_spec = pltpu.PrefetchScalarGridSpec(
        num_scalar_prefetch=1, grid=(rows // tr, 4),
        in_specs=[pl.BlockSpec((1, 1, tr, cols), lambda i, q, place_ref: (q, place_ref[0], i, 0)),
                  pl.BlockSpec((1, tr, cols), lambda i, q, place_ref: (q, i, 0))],
        out_specs=(pl.BlockSpec((1, tr, cols), lambda i, q, place_ref: (q, i, 0)),
                   pl.BlockSpec((tr, cols), lambda i, q, place_ref: (i, 0))))
    return pl.pallas_call(
        body, name=name, grid_spec=grid_spec,
        out_shape=(jax.ShapeDtypeStruct((4, rows, cols), BF16), jax.ShapeDtypeStruct((rows, cols), F32)),
        compiler_params=_params("arbitrary", "arbitrary"))(place, g, r)


HBM = pl.BlockSpec(memory_space=pltpu.HBM)
SEM = pl.BlockSpec(memory_space=pltpu.SEMAPHORE)
SIDE_EFFECT = pltpu.CompilerParams(has_side_effects=pltpu.SideEffectType.DATAFLOW_SIDE_EFFECTING)
TOKEN = jax.ShapeDtypeStruct((SUBLANES, LANES), F32)


def _hbm(a):
    return pltpu.with_memory_space_constraint(a, pltpu.HBM)


def _hbm_like(arrays):
    return tuple(pltpu.HBM(a.shape, a.dtype) for a in arrays)


def _block_of(px, py, pc):
    return 4 * px + 2 * py + pc


def _relay_parts(rows):
    if rows % (2 * PACKED_ROWS):
        return [pl.ds(0, rows), None]
    return [pl.ds(0, rows // 2), pl.ds(rows // 2, rows // 2)]


def _gather_start(shards, after, relay=False):
    na = len(shards)
    lands = [_hbm(lax.empty((N_DEV,) + a.shape, a.dtype)) for a in shards]

    def body(*refs):
        ins, land = refs[:na], refs[na:2 * na]
        send_sems, recv_ici, recv_d2d = refs[2 * na + 1:2 * na + 4]
        token = refs[-1]
        x, y, c = _place()
        peers = [(x, y, 1 - c), (1 - x, y, c), (x, 1 - y, c), (1 - x, 1 - y, c)]
        for k, peer in enumerate(peers[:3] if relay else peers):
            for t in range(na):
                pltpu.make_async_remote_copy(
                    src_ref=ins[t], dst_ref=land[t].at[_block_of(x, y, c)], send_sem=send_sems.at[4 * t + k],
                    recv_sem=recv_d2d.at[4 * t] if k == 0 else recv_ici.at[3 * t + k - 1],
                    device_id=peer, device_id_type=MESH).start()
        token[...] = jnp.zeros_like(token)

    out = pl.pallas_call(
        body, name="gather_start",
        out_shape=(pltpu.SemaphoreType.DMA((4 * na,)), pltpu.SemaphoreType.DMA((3 * na,)),
                   pltpu.SemaphoreType.DMA((4 * na,)), pltpu.SemaphoreType.DMA((2 * na,)), *_hbm_like(lands), TOKEN),
        in_specs=[ANY] * na + [HBM] * na + [ANY],
        out_specs=(SEM, SEM, SEM, SEM, *[HBM] * na, pl.BlockSpec(memory_space=pltpu.VMEM)),
        input_output_aliases={na + i: 4 + i for i in range(na)},
        compiler_params=SIDE_EFFECT)(*shards, *lands, after)
    send_sems, recv_ici, recv_d2d, recv_relay = out[:4]
    state = dict(send=send_sems, ici=recv_ici, d2d=recv_d2d, relay=recv_relay, relayed=relay, shards=list(shards),
                 lands=out[4:4 + na])
    return state, out[-1]


def _gather_forward(state, after):
    lands = state["lands"]
    na = len(lands)

    def body(*refs):
        land = refs[:na]
        recv_ici, recv_d2d = refs[na], refs[na + 1]
        fwd_sems, token = refs[-2], refs[-1]
        x, y, c = _place()
        for j, chip in enumerate([(1 - x, y), (x, 1 - y), (1 - x, 1 - y)]):
            for t in range(na):
                blk = land[t].at[_block_of(*chip, c)]
                pltpu.make_async_remote_copy(
                    src_ref=blk, dst_ref=blk, send_sem=fwd_sems.at[3 * t + j], recv_sem=recv_ici.at[3 * t + j],
                    device_id=(x, y, c), device_id_type=MESH).wait_recv()
                pltpu.make_async_remote_copy(
                    src_ref=blk, dst_ref=blk, send_sem=fwd_sems.at[3 * t + j], recv_sem=recv_d2d.at[4 * t + 1 + j],
                    device_id=(x, y, 1 - c), device_id_type=MESH).start()
        token[...] = jnp.zeros_like(token)

    out = pl.pallas_call(
        body, name="gather_forward",
        out_shape=(*_hbm_like(lands), pltpu.SemaphoreType.DMA((3 * na,)), TOKEN),
        in_specs=[HBM] * na + [SEM, SEM, ANY],
        out_specs=(*[HBM] * na, SEM, pl.BlockSpec(memory_space=pltpu.VMEM)),
        input_output_aliases={i: i for i in range(na)},
        compiler_params=SIDE_EFFECT)(*lands, state["ici"], state["d2d"], after)
    return dict(state, lands=out[:na], fwd=out[na]), out[-1]


def _gather_wait(state, after):
    shards, lands = state["shards"], state["lands"]
    na = len(lands)

    def body(*refs):
        ins, land = refs[:na], refs[na:2 * na]
        send_sems, fwd_sems, recv_d2d = refs[2 * na:2 * na + 3]
        x, y, c = _place()
        chips = [(1 - x, y), (x, 1 - y), (1 - x, 1 - y)]
        for t in range(na):
            mine = land[t].at[_block_of(x, y, c)]
            for k in range(4):
                pltpu.make_async_remote_copy(
                    src_ref=ins[t], dst_ref=mine, send_sem=send_sems.at[4 * t + k], recv_sem=recv_d2d.at[4 * t],
                    device_id=(x, y, c), device_id_type=MESH).wait_send()
            for j, chip in enumerate(chips):
                blk = land[t].at[_block_of(*chip, c)]
                pltpu.make_async_remote_copy(
                    src_ref=blk, dst_ref=blk, send_sem=fwd_sems.at[3 * t + j], recv_sem=recv_d2d.at[4 * t + 1 + j],
                    device_id=(x, y, c), device_id_type=MESH).wait_send()
            for k, blk_id in enumerate([_block_of(x, y, 1 - c)] + [_block_of(*chip, 1 - c) for chip in chips]):
                blk = land[t].at[blk_id]
                pltpu.make_async_remote_copy(
                    src_ref=blk, dst_ref=blk, send_sem=send_sems.at[4 * t], recv_sem=recv_d2d.at[4 * t + k],
                    device_id=(x, y, c), device_id_type=MESH).wait_recv()

    out = pl.pallas_call(
        body, name="gather_wait",
        out_shape=_hbm_like(lands),
        in_specs=[ANY] * na + [HBM] * na + [SEM, SEM, SEM, ANY],
        out_specs=tuple([HBM] * na),
        input_output_aliases={na + i: i for i in range(na)},
        compiler_params=SIDE_EFFECT)(*shards, *lands, state["send"], state["fwd"], state["d2d"], after)
    return out


def _gather_from_sibling(state, after):
    lands = state["lands"]
    na = len(lands)

    def body(*refs):
        land, recv_d2d = refs[:na], refs[na]
        x, y, c = _place()
        for t in range(na):
            blk = land[t].at[_block_of(x, y, 1 - c)]
            pltpu.make_async_remote_copy(src_ref=blk, dst_ref=blk, send_sem=recv_d2d.at[4 * t],
                                         recv_sem=recv_d2d.at[4 * t], device_id=(x, y, c),
                                         device_id_type=MESH).wait_recv()

    out = pl.pallas_call(
        body, name="gather_from_sibling", out_shape=_hbm_like(lands),
        in_specs=[HBM] * na + [SEM, ANY], out_specs=tuple([HBM] * na),
        input_output_aliases={i: i for i in range(na)},
        compiler_params=SIDE_EFFECT)(*lands, state["d2d"], after)
    return dict(state, lands=list(out))


def _gather_from_chip(state, j, afters, last):
    shards, lands, relayed = state["shards"], state["lands"], state["relayed"]
    na = len(lands)
    parts = [_relay_parts(a.shape[0]) for a in shards]

    def relay_on(land_ref, t, nb, fwd_sems, recv_relay):
        x, y, c = _place()
        blk = chip_blocks(land_ref, nb)[1].at[parts[t][nb]]
        return pltpu.make_async_remote_copy(
            src_ref=blk, dst_ref=blk, send_sem=fwd_sems.at[na + t], recv_sem=recv_relay.at[2 * t + nb],
            device_id=[(x, 1 - y, c), (1 - x, y, c)][nb], device_id_type=MESH)

    def chip_blocks(land_ref, which=j):
        x, y, c = _place()
        chip = [(1 - x, y), (x, 1 - y), (1 - x, 1 - y)][which]
        return (x, y, c), land_ref.at[_block_of(*chip, c)], land_ref.at[_block_of(*chip, 1 - c)]

    def forward(*refs):
        land, recv_ici, recv_d2d, recv_relay, fwd_sems = refs[:na], refs[na], refs[na + 1], refs[na + 2], refs[-1]
        for t in range(na):
            (x, y, c), mine, _ = chip_blocks(land[t])
            if relayed and j == 2:
                for half, rows in enumerate(parts[t]):
                    if rows is not None:
                        pltpu.make_async_remote_copy(
                            src_ref=mine.at[rows], dst_ref=mine.at[rows], send_sem=fwd_sems.at[t],
                            recv_sem=recv_relay.at[2 * t + half], device_id=(x, y, c),
                            device_id_type=MESH).wait_recv()
            else:
                pltpu.make_async_remote_copy(src_ref=mine, dst_ref=mine, send_sem=fwd_sems.at[t],
                                             recv_sem=recv_ici.at[3 * t + j], device_id=(x, y, c),
                                             device_id_type=MESH).wait_recv()
            pltpu.make_async_remote_copy(src_ref=mine, dst_ref=mine, send_sem=fwd_sems.at[t],
                                         recv_sem=recv_d2d.at[4 * t + 1 + j], device_id=(x, y, 1 - c),
                                         device_id_type=MESH).start()
            if relayed and j < 2 and parts[t][j] is not None:
                relay_on(land[t], t, j, fwd_sems, recv_relay).start()

    out = pl.pallas_call(
        forward, name="gather_pass_chip_" + str(j),
        out_shape=(*_hbm_like(lands), pltpu.SemaphoreType.DMA((2 * na,))),
        in_specs=[HBM] * na + [SEM, SEM, SEM] + [ANY] * len(afters), out_specs=(*[HBM] * na, SEM),
        input_output_aliases={i: i for i in range(na)},
        compiler_params=SIDE_EFFECT)(*lands, state["ici"], state["d2d"], state["relay"], *afters)
    passed_sems = out[na]
    relays = state.get("relays", []) + ([passed_sems] if relayed and j < 2 else [])
    waited = relays if last else []

    def arrive(*refs):
        land, fwd_sems, recv_d2d = refs[:na], refs[na], refs[na + 1]
        shard, send_sems, recv_relay = refs[na + 2:2 * na + 2], refs[2 * na + 2], refs[2 * na + 3]
        for nb, relay_sems in enumerate(refs[2 * na + 4:2 * na + 4 + len(waited)]):
            for t in range(na):
                if parts[t][nb] is not None:
                    relay_on(land[t], t, nb, relay_sems, recv_relay).wait_send()
        for t in range(na):
            (x, y, c), mine, theirs = chip_blocks(land[t])
            pltpu.make_async_remote_copy(src_ref=theirs, dst_ref=theirs, send_sem=fwd_sems.at[t],
                                         recv_sem=recv_d2d.at[4 * t + 1 + j], device_id=(x, y, c),
                                         device_id_type=MESH).wait_recv()
            pltpu.make_async_remote_copy(src_ref=mine, dst_ref=mine, send_sem=fwd_sems.at[t],
                                         recv_sem=recv_d2d.at[4 * t + 1 + j], device_id=(x, y, c),
                                         device_id_type=MESH).wait_send()
            for k in range((3 if relayed else 4) if last else 0):
                pltpu.make_async_remote_copy(
                    src_ref=shard[t], dst_ref=land[t].at[_block_of(x, y, c)], send_sem=send_sems.at[4 * t + k],
                    recv_sem=recv_d2d.at[4 * t], device_id=(x, y, c), device_id_type=MESH).wait_send()

    def take(state, afters):
        taken = pl.pallas_call(
            arrive, name="gather_take_chip_" + str(j), out_shape=_hbm_like(lands),
            in_specs=[HBM] * na + [SEM, SEM] + [ANY] * na + [SEM, SEM] + [SEM] * len(waited) + [ANY] * len(afters),
            out_specs=tuple([HBM] * na), input_output_aliases={i: i for i in range(na)},
            compiler_params=SIDE_EFFECT)(*state["lands"], passed_sems, state["d2d"], *shards, state["send"],
                                         state["relay"], *waited, *afters)
        return dict(state, lands=list(taken))

    return dict(state, lands=list(out[:na]), relays=relays), take


def _to_sibling(srcs, lands, send_sems, recv_sems):
    x, y, c = _place()
    return [pltpu.make_async_remote_copy(
        src_ref=srcs[t].at[:, 1 - c], dst_ref=lands[t], send_sem=send_sems.at[t], recv_sem=recv_sems.at[t],
        device_id=(x, y, 1 - c), device_id_type=MESH) for t in range(len(srcs))]


def _to_chips(srcs, lands, send_sems, recv_sems):
    x, y, c = _place()
    copies = []
    for k in (1, 2, 3):
        px, py = x ^ (k >> 1), y ^ (k & 1)
        copies += [pltpu.make_async_remote_copy(
            src_ref=srcs[t].at[2 * px + py], dst_ref=lands[t].at[k - 1], send_sem=send_sems.at[3 * t + k - 1],
            recv_sem=recv_sems.at[3 * t + k - 1], device_id=(px, py, c), device_id_type=MESH) for t in range(len(srcs))]
    return copies


def _to_all(srcs, lands, send_sems, recv_sems):
    x, y, c = _place()
    copies = []
    for k in range(1, N_DEV):
        peer = (x ^ (k >> 2), y ^ ((k >> 1) & 1), c ^ (k & 1))
        copies += [pltpu.make_async_remote_copy(
            src_ref=srcs[t], dst_ref=lands[t].at[_block_of(x, y, c)], send_sem=send_sems.at[(N_DEV - 1) * t + k - 1],
            recv_sem=recv_sems.at[(N_DEV - 1) * t + k - 1], device_id=peer, device_id_type=MESH)
            for t in range(len(srcs))]
    return copies


def _exchange_start(name, srcs, land_shapes, copies, per_array, after):
    na = len(srcs)
    lands = [_hbm(lax.empty(shp, a.dtype)) for shp, a in zip(land_shapes, srcs)]

    def body(*refs):
        token = refs[-1]
        for cp in copies(refs[:na], refs[na:2 * na], refs[2 * na + 1], refs[2 * na + 2]):
            cp.start()
        token[...] = jnp.zeros_like(token)

    out = pl.pallas_call(
        body, name=name,
        out_shape=(pltpu.SemaphoreType.DMA((na * per_array,)), pltpu.SemaphoreType.DMA((na * per_array,)),
                   *_hbm_like(lands), TOKEN),
        in_specs=[ANY] * na + [HBM] * na + [ANY],
        out_specs=(SEM, SEM, *[HBM] * na, pl.BlockSpec(memory_space=pltpu.VMEM)),
        input_output_aliases={na + i: 2 + i for i in range(na)},
        compiler_params=SIDE_EFFECT)(*srcs, *lands, after)
    return dict(send=out[0], recv=out[1], srcs=list(srcs), lands=out[2:2 + na]), out[-1]


def _exchange_wait(name, state, copies, afters):
    srcs, lands = state["srcs"], state["lands"]
    na = len(srcs)

    def body(*refs):
        for cp in copies(refs[:na], refs[na:2 * na], refs[2 * na], refs[2 * na + 1]):
            cp.wait_send()
            cp.wait_recv()

    out = pl.pallas_call(
        body, name=name,
        out_shape=_hbm_like(lands),
        in_specs=[ANY] * na + [HBM] * na + [SEM, SEM] + [ANY] * len(afters),
        out_specs=tuple([HBM] * na),
        input_output_aliases={na + i: i for i in range(na)},
        compiler_params=SIDE_EFFECT)(*srcs, *lands, state["send"], state["recv"], *afters)
    return out


def _adamw_math(w, g, m, v):
    m = ADAM_B1 * m + (1.0 - ADAM_B1) * g
    v = ADAM_B2 * v + (1.0 - ADAM_B2) * (g * g)
    m_hat = m / (1.0 - ADAM_B1 ** ADAM_STEP)
    v_hat = v / (1.0 - ADAM_B2 ** ADAM_STEP)
    return -ADAM_LR * (m_hat / (jnp.sqrt(v_hat) + ADAM_EPS) + ADAM_WD * w), m, v


def _adamw(own, others, w, m, v, tr, name, after):
    rows, cols = w.shape
    blk = pl.BlockSpec((tr, cols), lambda i: (i, 0))

    def body(own_ref, oth_ref, w_ref, m_ref, v_ref, after_ref, g_ref, d_ref, nm_ref, nv_ref):
        g = own_ref[...]
        for k in range(3):
            g = g + oth_ref[k].astype(F32)
        g_ref[...] = g
        d_ref[...], nm_ref[...], nv_ref[...] = _adamw_math(w_ref[...], g, m_ref[...], v_ref[...])

    out = jax.ShapeDtypeStruct((rows, cols), F32)
    return pl.pallas_call(
        body, name=name, out_shape=(out, out, out, out), grid=(rows // tr,),
        in_specs=[blk, pl.BlockSpec((3, tr, cols), lambda i: (0, i, 0)), blk, blk, blk, ANY],
        out_specs=(blk, blk, blk, blk),
        compiler_params=_params("parallel"))(own, others, w, m, v, after)


def _adamw_small(own, others, me, params, moments1, moments2):
    n = len(params)

    def body(me_ref, own_ref, oth_ref, *refs):
        ws, ms, vs = refs[:n], refs[n:2 * n], refs[2 * n:3 * n]
        loss_ref = refs[3 * n]
        outs, red_ref = refs[3 * n + 1:-1], refs[-1]
        total = jnp.where(me_ref[0, 0] == 0, own_ref[...], oth_ref[0])
        for d in range(1, N_DEV):
            total = total + jnp.where(me_ref[0, 0] == d, own_ref[...], oth_ref[d])
        red_ref[...] = total
        loss_ref[...] = jnp.sum(red_ref[ROW_MISC:ROW_MISC + 1, LOSS_AT:LOSS_AT + LANES], axis=-1, keepdims=True)
        for t, (row, at) in enumerate(SMALL_AT):
            g = red_ref[row:row + 1, at:at + ws[t].shape[1]]
            d, nm, nv = _adamw_math(ws[t][...], g, ms[t][...], vs[t][...])
            for o, val in zip(outs[4 * t:4 * t + 4], (g, d, nm, nv)):
                o[...] = val
        for tap in range(ws[-1].shape[0]):
            row, at = _tap_at(tap)
            g = red_ref[row:row + 1, pl.ds(pl.multiple_of(at + me_ref[0, 0] * LANES, LANES), LANES)]
            d, nm, nv = _adamw_math(ws[-1][tap], g, ms[-1][tap], vs[-1][tap])
            for o, val in zip(outs[4 * (n - 1):], (g, d, nm, nv)):
                o[tap] = val

    vmem = pl.BlockSpec(memory_space=pltpu.VMEM)
    shapes = [jax.ShapeDtypeStruct(w.shape, F32) for w in params for _ in range(4)]
    out = pl.pallas_call(
        body, name="adamw_small", out_shape=(jax.ShapeDtypeStruct((1, 1), F32), *shapes),
        in_specs=[pl.BlockSpec(memory_space=pltpu.SMEM), vmem, vmem] + [vmem] * (3 * n),
        out_specs=tuple([vmem] * (1 + 4 * n)),
        scratch_shapes=[pltpu.VMEM((SLAB_ROWS, D_MODEL), F32)])(me, own, others, *params, *moments1, *moments2)
    return out[0], [list(out[1 + k::4]) for k in range(4)]


def _tables(gq, gk, conv_w):
    gq2 = jnp.tile(gq.reshape(1, HEAD), (1, 2))
    gk2 = jnp.tile(gk.reshape(1, HEAD), (1, 2))
    conv_wp = jnp.pad(conv_w, ((0, SUBLANES - conv_w.shape[0]), (0, 0)))
    return gq2, gk2, conv_wp


def _pair_id(q):
    return jnp.array([q, 0], jnp.int32)


def _forward_in(x, g1, shards):
    s = x.shape[0]
    h = _prenorm(x, g1, min(512, s), x)
    z, w_pairs = lax.empty((s, IN_W), F32), lax.empty((N_PAIRS, PAIR_W, D_MODEL), BF16)
    for q in range(N_PAIRS):
        z, w_pairs = _fwd_in_pair(h, shards, z, w_pairs, _pair_id(q), min(512, s), "fwd_in_" + str(q),
                                  own=shards[0] if q == 0 else None)
    return h, z, w_pairs


def _forward_attn(z, rope, gq2, gk2, conv_wp, sinks):
    s = z.shape[0]
    qn, k2, v2 = _qk_prep(z, *rope, gq2, gk2, min(256, s), z)
    a, mix, mixt = _attn_fwd(qn, k2, v2, z, conv_wp, sinks, qn)
    return qn, k2, v2, a, mix, mixt


def _forward_out(x, p, target, mix, mixt, w_out, g2, w_pg, b_pg, w_pp, g3):
    s = x.shape[0]
    tm = min(512, s)
    x1, hn2, hn2t = _fwd_out(mix, w_out, x, g2, tm)
    dy, dgp, dt, pt, acc_ple = _ple(hn2, w_pg, b_pg, p, w_pp, g3, x1, target, min(256, s))
    dx1, dx1b, acc_g2 = _gate_bwd(dgp, w_pg, x1, dy, g2, tm)
    gw_out = _mm_grad(mixt, [dx1b], 512, "grad_w_out")
    gw_pg = _mm_grad(hn2t, [dgp], 512, "grad_w_ple_gate")
    gw_pp = _mm_grad(pt, [dt], 512, "grad_w_ple_proj")
    return dx1, dx1b, (gw_out, gw_pg, gw_pp), acc_ple, acc_g2


def _backward_attn(dmix, h, z, qn, k2, v2, a, rope, gq2, gk2, conv_wp, sinks, after):
    dq, dkc, dkp, dvc, dvp, dz, dzt, acc_attn = _attn_bwd(qn, k2, v2, a, z, dmix, conv_wp, sinks, after)
    dz, dzt, acc_qk = _qkv_bwd(z, dz, dzt, dq, dkc, dkp, dvc, dvp, *rope, gq2, gk2)
    return dz, _grad_w_in(dzt, h), acc_attn, acc_qk


def _local_step(x, p, target, g1, shards, gq, gk, sinks, conv_w, w_out, g2, w_pg, b_pg, w_pp, g3):
    rope, (gq2, gk2, conv_wp) = _rope_tables(x.shape[0]), _tables(gq, gk, conv_w)
    h, z, w_pairs = _forward_in(x, g1, shards)
    qn, k2, v2, a, mix, mixt = _forward_attn(z, rope, gq2, gk2, conv_wp, sinks)
    dx1, dx1b, (gw_out, gw_pg, gw_pp), acc_ple, acc_g2 = _forward_out(
        x, p, target, mix, mixt, w_out, g2, w_pg, b_pg, w_pp, g3)
    dmix = _mm_nt(dx1b, w_out, min(512, x.shape[0]), "out_bwd", dx1b)
    dz, gw_in, acc_attn, acc_qk = _backward_attn(dmix, h, z, qn, k2, v2, a, rope, gq2, gk2, conv_wp, sinks, dmix)
    grad_x, acc_g1 = _in_bwd(dz, w_pairs, x, dx1, g1, min(512, x.shape[0]), dx1)
    return grad_x, (gw_in, gw_out, gw_pg, gw_pp), (acc_g1, acc_g2, acc_ple, acc_qk, acc_attn)


def _by_owner(g):
    return g.reshape((4, 2) + g.shape[1:])


def kernel(x, p, norm_gain, w_in, q_norm_gain, k_norm_gain, attn_sinks, conv_w, w_out, ple_gate_norm_gain, w_ple_gate, b_ple_gate, w_ple_proj, ple_norm_gain, loss_target, m_norm_gain, m_w_in, m_q_norm_gain, m_k_norm_gain, m_attn_sinks, m_conv_w, m_w_out, m_ple_gate_norm_gain, m_w_ple_gate, m_b_ple_gate, m_w_ple_proj, m_ple_norm_gain, v_norm_gain, v_w_in, v_q_norm_gain, v_k_norm_gain, v_attn_sinks, v_conv_w, v_w_out, v_ple_gate_norm_gain, v_w_ple_gate, v_b_ple_gate, v_w_ple_proj, v_ple_norm_gain):
    me = 4 * lax.axis_index("x") + 2 * lax.axis_index("y") + lax.axis_index("c")
    place = jnp.stack([lax.axis_index("c"), 2 * lax.axis_index("x") + lax.axis_index("y")]).astype(jnp.int32)
    xs, ps, target = x[0], p[0, 0], loss_target[0]

    shard_in = w_in[0].T.astype(BF16)
    own_late = [w_out[0].astype(BF16), w_ple_gate[0].astype(BF16), w_ple_proj[0].astype(BF16)]
    with_own = lambda gathered, own: lax.dynamic_update_slice(gathered, own[None], (me,) + (0,) * own.ndim)
    early, started = _gather_start([shard_in, conv_w[0]], shard_in, relay=True)
    tm = min(512, xs.shape[0])
    h = _prenorm(xs, norm_gain, tm, started)

    z, w_pairs = lax.empty((xs.shape[0], IN_W), F32), lax.empty((N_PAIRS, PAIR_W, D_MODEL), BF16)
    early = _gather_from_sibling(early, h)
    pair_of = lambda flip: jnp.stack([place[1] ^ flip, place[0]])
    z, w_pairs = _fwd_in_pair(h, early["lands"][0], z, w_pairs, pair_of(0), tm, "fwd_in_own", own=shard_in)
    rope = _rope_tables(xs.shape[0])
    early, take = _gather_from_chip(early, 0, (z, *rope, *own_late), last=False)
    for j, flip in enumerate((2, 1, 3)):
        if j < 2:
            early, take_next = _gather_from_chip(early, j + 1, (z,), last=j == 1)
        if j == 1:
            late, started_late = _gather_start(own_late, z)
        early = take(early, (z, started_late) if j == 1 else (z,))
        z, w_pairs = _fwd_in_pair(h, early["lands"][0], z, w_pairs, pair_of(flip), tm, "fwd_in_chip_" + str(j))
        take = take_next
    conv_full = jnp.transpose(with_own(early["lands"][1], conv_w[0]), (1, 0, 2)).reshape(3, ATTN_W)
    gq2, gk2, conv_wp = _tables(q_norm_gain[0], k_norm_gain[0], conv_full)
    qn, k2, v2 = _qk_prep(z, *rope, gq2, gk2, min(256, xs.shape[0]), z)
    late, forwarded = _gather_forward(late, qn)
    a, mix, mixt = _attn_fwd(qn, k2, v2, z, conv_wp, attn_sinks, forwarded)
    g_out, g_pg, g_pp = (with_own(g, own) for g, own in zip(_gather_wait(late, mix), own_late))
    w_out_f = g_out.reshape(D_MODEL, D_MODEL)
    w_pg_f = g_pg.reshape(D_MODEL, D_MODEL)
    w_pp_f = jnp.transpose(g_pp, (1, 0, 2)).reshape(PLE_DIM, D_MODEL)

    dx1, dx1b, (gw_out, gw_pg, gw_pp), acc_ple, acc_g2 = _forward_out(
        xs, ps, target, mix, mixt, w_out_f, ple_gate_norm_gain, w_pg_f, b_ple_gate, w_pp_f, ple_norm_gain)

    names = ("w_out", "w_ple_gate", "w_ple_proj")
    gw_pp_t = jnp.transpose(gw_pp.reshape(PLE_DIM, N_DEV, PLE_DIM), (1, 0, 2))
    grads = [_by_owner(gw_out.reshape(N_DEV, D_MODEL // N_DEV, D_MODEL)),
             _by_owner(gw_pg.reshape(N_DEV, D_MODEL // N_DEV, D_MODEL)), _by_owner(gw_pp_t)]
    pairs, paired = _exchange_start("pair_start", grads, [(4,) + g.shape[2:] for g in grads], _to_sibling, 1, dx1b)
    dmix = _mm_nt(dx1b, w_out_f, tm, "out_bwd", paired)
    from_sibling = _exchange_wait("pair_wait", pairs, _to_sibling, (dmix,))
    sums = [_pair_sum(g, r, place, 256, "pair_sum_" + nm) for g, r, nm in zip(pairs["srcs"], from_sibling, names)]
    chips, sent = _exchange_start("chip_start", [pb for pb, _ in sums], [(3,) + pb.shape[1:] for pb, _ in sums],
                                  _to_chips, 3, sums[-1][1])

    dz, gw_in, acc_attn, acc_qk = _backward_attn(
        dmix, h, z, qn, k2, v2, a, rope, gq2, gk2, conv_wp, attn_sinks, sent)

    gw_in_t = [_by_owner(gw_in)]
    pairs_in, paired_in = _exchange_start("pair_start_w_in", gw_in_t, [(4,) + gw_in_t[0].shape[2:]], _to_sibling, 1,
                                          gw_in)
    from_chips = _exchange_wait("chip_wait", chips, _to_chips, (gw_in,))
    big = {}
    for (_, own), oth, w, m, v, nm in zip(sums, from_chips, (w_out, w_ple_gate, w_ple_proj),
                                          (m_w_out, m_w_ple_gate, m_w_ple_proj),
                                          (v_w_out, v_w_ple_gate, v_w_ple_proj), names):
        big[nm] = [t[None] for t in _adamw(own, oth, w[0], m[0], v[0], 256, "adamw_" + nm, paired_in)]

    (from_sibling_in,) = _exchange_wait("pair_wait_w_in", pairs_in, _to_sibling, [big[nm][0] for nm in names])
    pb_in, own_in = _pair_sum(pairs_in["srcs"][0], from_sibling_in, place, SHARD_IN // 2, "pair_sum_w_in")
    chips_in, sent_in = _exchange_start("chip_start_w_in", [pb_in], [(3,) + pb_in.shape[1:]], _to_chips, 3, own_in)
    grad_x, acc_g1 = _in_bwd(dz, w_pairs, xs, dx1, norm_gain, tm, sent_in)
    (from_chips_in,) = _exchange_wait("chip_wait_w_in", chips_in, _to_chips, (grad_x,))
    slab = _pack_small(acc_g1, acc_g2, acc_ple, acc_qk, acc_attn)
    slabs, started_small = _exchange_start("small_start", [slab], [(N_DEV,) + slab.shape], _to_all, N_DEV - 1, grad_x)
    big["w_in"] = [t.T[None] for t in _adamw(own_in, from_chips_in, w_in[0].T, m_w_in[0].T, v_w_in[0].T, SHARD_IN // 4,
                                             "adamw_w_in", started_small)]
    (all_slabs,) = _exchange_wait("small_wait", slabs, _to_all, (big["w_in"][0],))

    small = [norm_gain, ple_gate_norm_gain, b_ple_gate, ple_norm_gain, q_norm_gain, k_norm_gain, attn_sinks]
    small_m = [m_norm_gain, m_ple_gate_norm_gain, m_b_ple_gate, m_ple_norm_gain, m_q_norm_gain, m_k_norm_gain,
               m_attn_sinks]
    small_v = [v_norm_gain, v_ple_gate_norm_gain, v_b_ple_gate, v_ple_norm_gain, v_q_norm_gain, v_k_norm_gain,
               v_attn_sinks]
    taps_first = lambda t: jnp.transpose(t, (1, 0, 2))
    loss, kinds = _adamw_small(slab, all_slabs, me.reshape(1, 1).astype(jnp.int32), small + [taps_first(conv_w)],
                               small_m + [taps_first(m_conv_w)], small_v + [taps_first(v_conv_w)])

    def order(k):
        sm = kinds[k]
        return [sm[0], big["w_in"][k], sm[4], sm[5], sm[6], taps_first(sm[7]), big["w_out"][k], sm[1],
                big["w_ple_gate"][k], sm[2], big["w_ple_proj"][k], sm[3]]

    return (loss[0, 0], grad_x[None], *order(0), *order(1), *order(2), *order(3))
```
